```python
import jax, jax.numpy as jnp
from jax import lax
import numpy as np

D_MODEL = 1024
BATCH = 8
SEQ = 4096
DEPTH = 1

HEAD_DIM = 128
HEADS_PER_GROUP = 4
DILATED_GROUPS = ((128, 1), (512, 4), (2048, 16))
N_GROUPS = 3
N_ATTN_HEADS = N_GROUPS * HEADS_PER_GROUP
ATTN_WIDTH = N_ATTN_HEADS * HEAD_DIM
ATTN_OUT_WIDTH = HEADS_PER_GROUP * HEAD_DIM
BLOCK = 128
CONV_WIDTH = D_MODEL
CONV_K = 3
D_FF = 4 * D_MODEL
N_MOD = 6
IN_COLS = 3 * ATTN_WIDTH + 3 * CONV_WIDTH + 2 * D_MODEL
EPS = 1e-6
NEG_INF = -1e30

kernel_name = "hybrid_dilated_attn_shortconv_gated_block"


def rmsnorm(x, g):
    xf = x.astype(jnp.float32)
    y = xf * lax.rsqrt(jnp.mean(xf * xf, axis=-1, keepdims=True) + EPS)
    return (y * g.astype(jnp.float32)).astype(x.dtype)


def alibi_slopes(n):
    return 2.0 ** (-8.0 * jnp.arange(1, n + 1, dtype=jnp.float32) / n)


def dilated_window_attention(q, k, v, window, dilation, slopes):
    B, S, H, E = q.shape
    n_win = window // dilation
    span = dilation * BLOCK
    s_pad = -(-S // span) * span
    L = s_pad // dilation
    nb = L // BLOCK

    def to_blocks(t):
        t = jnp.pad(t, ((0, 0), (0, s_pad - S), (0, 0), (0, 0)))
        t = t.reshape(B, L, dilation, H, E).transpose(0, 2, 1, 3, 4)
        return t.reshape(B, dilation, nb, BLOCK, H, E)

    def with_prev(t):
        prev = jnp.concatenate([jnp.zeros_like(t[:, :, :1]), t[:, :, :-1]], axis=2)
        return jnp.concatenate([prev, t], axis=3)

    qb = to_blocks(q)
    kw = with_prev(to_blocks(k))
    vw = with_prev(to_blocks(v))

    scores = jnp.einsum('brnqhe,brnkhe->brnhqk', qb, kw,
                        preferred_element_type=jnp.float32) * (E ** -0.5)
    qi = jnp.arange(BLOCK)[:, None]
    kj = jnp.arange(2 * BLOCK)[None, :]
    delta = BLOCK + qi - kj
    in_window = (delta >= 0) & (delta <= n_win)
    has_key = (jnp.arange(nb)[:, None, None] > 0) | (kj[None] >= BLOCK)
    mask = in_window[None] & has_key
    bias = -slopes[:, None, None] * (delta * dilation).astype(jnp.float32)
    scores = jnp.where(mask[:, None], scores + bias, NEG_INF)

    m = jnp.max(scores, axis=-1, keepdims=True)
    p = jnp.exp(scores - m)
    denom = jnp.sum(p, axis=-1, keepdims=True)
    o = jnp.einsum('brnhqk,brnkhe->brnqhe', p, vw.astype(jnp.float32))
    o = o / jnp.swapaxes(denom, 3, 4)
    lse = (m + jnp.log(denom))[..., 0]

    o = o.reshape(B, dilation, L, H, E).transpose(0, 2, 1, 3, 4).reshape(B, s_pad, H, E)[:, :S]
    lse = lse.transpose(0, 1, 2, 4, 3).reshape(B, dilation, L, H)
    lse = lse.transpose(0, 2, 1, 3).reshape(B, s_pad, H)[:, :S]
    return o, lse


def causal_short_conv(u, w):
    return lax.conv_general_dilated(
        u, w[:, None, :].astype(u.dtype), window_strides=(1,), padding=[(CONV_K - 1, 0)],
        dimension_numbers=('NWC', 'WIO', 'NWC'), feature_group_count=u.shape[-1])


def _fwd_setup_inputs(seed: int = 0) -> dict:
    key = jax.random.key(seed)
    ks = jax.random.split(key, 16)
    f32 = jnp.float32
    nrm = lambda k, shape, s: jax.random.normal(k, shape, f32) * s
    return {
        "x": jax.random.normal(ks[0], (BATCH, SEQ, D_MODEL), f32),
        "c": jax.random.normal(ks[1], (BATCH, D_MODEL), f32),
        "w_ada": nrm(ks[2], (DEPTH, D_MODEL, N_MOD * D_MODEL), D_MODEL ** -0.5),
        "b_ada": nrm(ks[3], (DEPTH, N_MOD * D_MODEL), 0.01),
        "g_norm_mix": 1.0 + nrm(ks[4], (DEPTH, D_MODEL), 0.02),
        "w_in": nrm(ks[5], (DEPTH, D_MODEL, IN_COLS), D_MODEL ** -0.5),
        "b_gate": nrm(ks[6], (DEPTH, 2 * D_MODEL), 0.01),
        "conv_w": nrm(ks[7], (DEPTH, CONV_K, CONV_WIDTH), CONV_K ** -0.5),
        "w_branch_attn": nrm(ks[8], (DEPTH, ATTN_OUT_WIDTH, D_MODEL), ATTN_OUT_WIDTH ** -0.5),
        "w_branch_conv": nrm(ks[9], (DEPTH, CONV_WIDTH, D_MODEL), CONV_WIDTH ** -0.5),
        "w_out": nrm(ks[10], (DEPTH, D_MODEL, D_MODEL), D_MODEL ** -0.5),
        "g_norm_mlp": 1.0 + nrm(ks[11], (DEPTH, D_MODEL), 0.02),
        "w_mlp_in": nrm(ks[12], (DEPTH, D_MODEL, D_FF), D_MODEL ** -0.5),
        "w_mlp_out": nrm(ks[13], (DEPTH, D_FF, D_MODEL), D_FF ** -0.5),
        "g_norm_final": 1.0 + nrm(ks[14], (D_MODEL,), 0.02),
    }


def _fwd_reference(x, c, w_ada, b_ada, g_norm_mix, w_in, b_gate, conv_w, w_branch_attn,
              w_branch_conv, w_out, g_norm_mlp, w_mlp_in, w_mlp_out, g_norm_final):
    B, S, D = x.shape
    slopes = alibi_slopes(N_ATTN_HEADS)
    widths = [ATTN_WIDTH] * 3 + [CONV_WIDTH] * 3 + [D_MODEL]
    split_pts = [int(s) for s in np.cumsum(widths)]
    c_act = jax.nn.silu(c)
    for l in range(DEPTH):
        mod = (c_act @ w_ada[l] + b_ada[l])[:, None, :]
        shift1, scale1, gate1, shift2, scale2, gate2 = jnp.split(mod, N_MOD, axis=-1)

        h = rmsnorm(x, g_norm_mix[l]) * (1.0 + scale1) + shift1
        proj = h @ w_in[l]
        q, k, v, cb, cc, cx, g_a, g_b = jnp.split(proj, split_pts, axis=-1)
        q = q.reshape(B, S, N_ATTN_HEADS, HEAD_DIM)
        k = k.reshape(B, S, N_ATTN_HEADS, HEAD_DIM)
        v = v.reshape(B, S, N_ATTN_HEADS, HEAD_DIM)

        outs, lses = [], []
        for gi, (window, dilation) in enumerate(DILATED_GROUPS):
            hs = slice(gi * HEADS_PER_GROUP, (gi + 1) * HEADS_PER_GROUP)
            o_g, lse_g = dilated_window_attention(q[:, :, hs], k[:, :, hs], v[:, :, hs],
                                                  window, dilation, slopes[hs])
            outs.append(o_g)
            lses.append(lse_g)
        w_grp = jax.nn.softmax(jnp.stack(lses), axis=0)
        o_attn = jnp.einsum('gbsh,gbshe->bshe', w_grp, jnp.stack(outs))
        y_attn = o_attn.reshape(B, S, ATTN_OUT_WIDTH).astype(x.dtype) @ w_branch_attn[l]

        u = causal_short_conv(cc * cx, conv_w[l])
        y_conv = (cb * u) @ w_branch_conv[l]

        ba, bb = jnp.split(b_gate[l], 2)
        merged = jax.nn.sigmoid(g_a + ba) * y_attn + jax.nn.sigmoid(g_b + bb) * y_conv
        x = x + gate1 * (merged @ w_out[l])

        h2 = rmsnorm(x, g_norm_mlp[l]) * (1.0 + scale2) + shift2
        x = x + gate2 * (jnp.square(jax.nn.relu(h2 @ w_mlp_in[l])) @ w_mlp_out[l])
    return rmsnorm(x, g_norm_final)


import jax as _jax
import jax.numpy as _jnp

TWIN_FORMAT = 'train_step'
FWD_PARAMS = ['x', 'c', 'w_ada', 'b_ada', 'g_norm_mix', 'w_in', 'b_gate', 'conv_w', 'w_branch_attn', 'w_branch_conv', 'w_out', 'g_norm_mlp', 'w_mlp_in', 'w_mlp_out', 'g_norm_final']
TWIN_WEIGHTS = ['w_ada', 'b_ada', 'g_norm_mix', 'w_in', 'b_gate', 'conv_w', 'w_branch_attn', 'w_branch_conv', 'w_out', 'g_norm_mlp', 'w_mlp_in', 'w_mlp_out', 'g_norm_final']
TWIN_DIFF_INPUT = 'x'
TWIN_INPUTS = ['x', 'c', 'w_ada', 'b_ada', 'g_norm_mix', 'w_in', 'b_gate', 'conv_w', 'w_branch_attn', 'w_branch_conv', 'w_out', 'g_norm_mlp', 'w_mlp_in', 'w_mlp_out', 'g_norm_final', 'loss_target', 'm_w_ada', 'm_b_ada', 'm_g_norm_mix', 'm_w_in', 'm_b_gate', 'm_conv_w', 'm_w_branch_attn', 'm_w_branch_conv', 'm_w_out', 'm_g_norm_mlp', 'm_w_mlp_in', 'm_w_mlp_out', 'm_g_norm_final', 'v_w_ada', 'v_b_ada', 'v_g_norm_mix', 'v_w_in', 'v_b_gate', 'v_conv_w', 'v_w_branch_attn', 'v_w_branch_conv', 'v_w_out', 'v_g_norm_mlp', 'v_w_mlp_in', 'v_w_mlp_out', 'v_g_norm_final']
TWIN_OUTPUTS = ['loss', 'grad_x', 'grad_w_ada', 'grad_b_ada', 'grad_g_norm_mix', 'grad_w_in', 'grad_b_gate', 'grad_conv_w', 'grad_w_branch_attn', 'grad_w_branch_conv', 'grad_w_out', 'grad_g_norm_mlp', 'grad_w_mlp_in', 'grad_w_mlp_out', 'grad_g_norm_final', 'delta_w_ada', 'delta_b_ada', 'delta_g_norm_mix', 'delta_w_in', 'delta_b_gate', 'delta_conv_w', 'delta_w_branch_attn', 'delta_w_branch_conv', 'delta_w_out', 'delta_g_norm_mlp', 'delta_w_mlp_in', 'delta_w_mlp_out', 'delta_g_norm_final', 'new_m_w_ada', 'new_m_b_ada', 'new_m_g_norm_mix', 'new_m_w_in', 'new_m_b_gate', 'new_m_conv_w', 'new_m_w_branch_attn', 'new_m_w_branch_conv', 'new_m_w_out', 'new_m_g_norm_mlp', 'new_m_w_mlp_in', 'new_m_w_mlp_out', 'new_m_g_norm_final', 'new_v_w_ada', 'new_v_b_ada', 'new_v_g_norm_mix', 'new_v_w_in', 'new_v_b_gate', 'new_v_conv_w', 'new_v_w_branch_attn', 'new_v_w_branch_conv', 'new_v_w_out', 'new_v_g_norm_mlp', 'new_v_w_mlp_in', 'new_v_w_mlp_out', 'new_v_g_norm_final']
TWIN_LEAF_KINDS = {'loss': 'loss', 'grad_x': 'grad_x', 'grad_w_ada': 'grad_w', 'grad_b_ada': 'grad_w', 'grad_g_norm_mix': 'grad_w', 'grad_w_in': 'grad_w', 'grad_b_gate': 'grad_w', 'grad_conv_w': 'grad_w', 'grad_w_branch_attn': 'grad_w', 'grad_w_branch_conv': 'grad_w', 'grad_w_out': 'grad_w', 'grad_g_norm_mlp': 'grad_w', 'grad_w_mlp_in': 'grad_w', 'grad_w_mlp_out': 'grad_w', 'grad_g_norm_final': 'grad_w', 'delta_w_ada': 'delta_w', 'delta_b_ada': 'delta_w', 'delta_g_norm_mix': 'delta_w', 'delta_w_in': 'delta_w', 'delta_b_gate': 'delta_w', 'delta_conv_w': 'delta_w', 'delta_w_branch_attn': 'delta_w', 'delta_w_branch_conv': 'delta_w', 'delta_w_out': 'delta_w', 'delta_g_norm_mlp': 'delta_w', 'delta_w_mlp_in': 'delta_w', 'delta_w_mlp_out': 'delta_w', 'delta_g_norm_final': 'delta_w', 'new_m_w_ada': 'new_m', 'new_m_b_ada': 'new_m', 'new_m_g_norm_mix': 'new_m', 'new_m_w_in': 'new_m', 'new_m_b_gate': 'new_m', 'new_m_conv_w': 'new_m', 'new_m_w_branch_attn': 'new_m', 'new_m_w_branch_conv': 'new_m', 'new_m_w_out': 'new_m', 'new_m_g_norm_mlp': 'new_m', 'new_m_w_mlp_in': 'new_m', 'new_m_w_mlp_out': 'new_m', 'new_m_g_norm_final': 'new_m', 'new_v_w_ada': 'new_v', 'new_v_b_ada': 'new_v', 'new_v_g_norm_mix': 'new_v', 'new_v_w_in': 'new_v', 'new_v_b_gate': 'new_v', 'new_v_conv_w': 'new_v', 'new_v_w_branch_attn': 'new_v', 'new_v_w_branch_conv': 'new_v', 'new_v_w_out': 'new_v', 'new_v_g_norm_mlp': 'new_v', 'new_v_w_mlp_in': 'new_v', 'new_v_w_mlp_out': 'new_v', 'new_v_g_norm_final': 'new_v'}


def _forward(args):
    return _fwd_reference(*[args[k] for k in FWD_PARAMS])


def _output_shape():
    out = _jax.eval_shape(lambda: _forward(_fwd_setup_inputs(0)))
    return out.shape, out.dtype

N_MICROBATCH = 1
ADAM_LR = 0.001
ADAM_B1 = 0.9
ADAM_B2 = 0.999
ADAM_EPS = 1e-08
ADAM_WD = 0.01
ADAM_STEP = 10
PER_EXAMPLE_BATCH_AXIS = {'x': 0, 'c': 0, 'loss_target': 0}
SHARED_INPUTS = []
_WEIGHT_DTYPES = {'w_ada': _jnp.float32, 'b_ada': _jnp.float32, 'g_norm_mix': _jnp.float32, 'w_in': _jnp.float32, 'b_gate': _jnp.float32, 'conv_w': _jnp.float32, 'w_branch_attn': _jnp.float32, 'w_branch_conv': _jnp.float32, 'w_out': _jnp.float32, 'g_norm_mlp': _jnp.float32, 'w_mlp_in': _jnp.float32, 'w_mlp_out': _jnp.float32, 'g_norm_final': _jnp.float32}
MOMENT_SCALE = {'w_ada': 1.311639e-01, 'b_ada': 2.229244e-01, 'g_norm_mix': 1.819163e-01, 'w_in': 6.770953e-02, 'b_gate': 2.974131e-02, 'conv_w': 1.208241e-01, 'w_branch_attn': 3.312278e-02, 'w_branch_conv': 1.114845e-01, 'w_out': 1.152477e-01, 'g_norm_mlp': 1.326584e-01, 'w_mlp_in': 7.788531e-02, 'w_mlp_out': 1.618757e-01, 'g_norm_final': 3.537433e+01}


def _to_microbatches(a, axis):
    t = _jnp.moveaxis(a, axis, 0)
    t = t.reshape((N_MICROBATCH, t.shape[0] // N_MICROBATCH) + t.shape[1:])
    return _jnp.moveaxis(t, 1, axis + 1)


def setup_inputs(seed: int = 0) -> dict:
    inp = _fwd_setup_inputs(seed)
    key = _jax.random.fold_in(_jax.random.key(seed), 7919)
    shape, _ = _output_shape()
    out = dict(inp)
    out["loss_target"] = _jax.random.normal(_jax.random.fold_in(key, 0), shape, _jnp.float32)
    for i, name in enumerate(TWIN_WEIGHTS):
        w = inp[name].astype(_jnp.float32)
        if MOMENT_SCALE is None:
            s = _jnp.sqrt(_jnp.mean(_jnp.square(w)) + 1e-30)
        else:
            s = MOMENT_SCALE[name]
        km, kv = _jax.random.split(_jax.random.fold_in(key, i + 1))
        out[name] = w
        out["m_" + name] = s * _jax.random.normal(km, w.shape, _jnp.float32)
        out["v_" + name] = (s * s) * _jax.random.uniform(kv, w.shape, _jnp.float32, 0.5, 1.5)
    if N_MICROBATCH > 1:
        for name, axis in PER_EXAMPLE_BATCH_AXIS.items():
            out[name] = _to_microbatches(out[name], axis)
    return {'x': out['x'], 'c': out['c'], 'w_ada': out['w_ada'], 'b_ada': out['b_ada'], 'g_norm_mix': out['g_norm_mix'], 'w_in': out['w_in'], 'b_gate': out['b_gate'], 'conv_w': out['conv_w'], 'w_branch_attn': out['w_branch_attn'], 'w_branch_conv': out['w_branch_conv'], 'w_out': out['w_out'], 'g_norm_mlp': out['g_norm_mlp'], 'w_mlp_in': out['w_mlp_in'], 'w_mlp_out': out['w_mlp_out'], 'g_norm_final': out['g_norm_final'], 'loss_target': out['loss_target'], 'm_w_ada': out['m_w_ada'], 'm_b_ada': out['m_b_ada'], 'm_g_norm_mix': out['m_g_norm_mix'], 'm_w_in': out['m_w_in'], 'm_b_gate': out['m_b_gate'], 'm_conv_w': out['m_conv_w'], 'm_w_branch_attn': out['m_w_branch_attn'], 'm_w_branch_conv': out['m_w_branch_conv'], 'm_w_out': out['m_w_out'], 'm_g_norm_mlp': out['m_g_norm_mlp'], 'm_w_mlp_in': out['m_w_mlp_in'], 'm_w_mlp_out': out['m_w_mlp_out'], 'm_g_norm_final': out['m_g_norm_final'], 'v_w_ada': out['v_w_ada'], 'v_b_ada': out['v_b_ada'], 'v_g_norm_mix': out['v_g_norm_mix'], 'v_w_in': out['v_w_in'], 'v_b_gate': out['v_b_gate'], 'v_conv_w': out['v_conv_w'], 'v_w_branch_attn': out['v_w_branch_attn'], 'v_w_branch_conv': out['v_w_branch_conv'], 'v_w_out': out['v_w_out'], 'v_g_norm_mlp': out['v_g_norm_mlp'], 'v_w_mlp_in': out['v_w_mlp_in'], 'v_w_mlp_out': out['v_w_mlp_out'], 'v_g_norm_final': out['v_g_norm_final']}


def _loss(weights, diff, rest, loss_target):
    with _jax.named_scope("forward"):
        args = {**rest, TWIN_DIFF_INPUT: diff, **{k: w.astype(_WEIGHT_DTYPES[k]) for k, w in weights.items()}}
        y = _forward(args)
    with _jax.named_scope("loss_head"):
        err = _jnp.square(y.astype(_jnp.float32) - loss_target)
        return 0.5 * _jnp.sum(_jnp.mean(err, axis=-1)) if err.ndim else 0.5 * err


def _adamw(w, g, m, v):
    m = ADAM_B1 * m + (1.0 - ADAM_B1) * g
    v = ADAM_B2 * v + (1.0 - ADAM_B2) * _jnp.square(g)
    m_hat = m / (1.0 - ADAM_B1 ** ADAM_STEP)
    v_hat = v / (1.0 - ADAM_B2 ** ADAM_STEP)
    delta = -ADAM_LR * (m_hat / (_jnp.sqrt(v_hat) + ADAM_EPS) + ADAM_WD * w)
    return delta, m, v


def reference(x, c, w_ada, b_ada, g_norm_mix, w_in, b_gate, conv_w, w_branch_attn, w_branch_conv, w_out, g_norm_mlp, w_mlp_in, w_mlp_out, g_norm_final, loss_target, m_w_ada, m_b_ada, m_g_norm_mix, m_w_in, m_b_gate, m_conv_w, m_w_branch_attn, m_w_branch_conv, m_w_out, m_g_norm_mlp, m_w_mlp_in, m_w_mlp_out, m_g_norm_final, v_w_ada, v_b_ada, v_g_norm_mix, v_w_in, v_b_gate, v_conv_w, v_w_branch_attn, v_w_branch_conv, v_w_out, v_g_norm_mlp, v_w_mlp_in, v_w_mlp_out, v_g_norm_final):
    given = dict(x=x, c=c, w_ada=w_ada, b_ada=b_ada, g_norm_mix=g_norm_mix, w_in=w_in, b_gate=b_gate, conv_w=conv_w, w_branch_attn=w_branch_attn, w_branch_conv=w_branch_conv, w_out=w_out, g_norm_mlp=g_norm_mlp, w_mlp_in=w_mlp_in, w_mlp_out=w_mlp_out, g_norm_final=g_norm_final, loss_target=loss_target, m_w_ada=m_w_ada, m_b_ada=m_b_ada, m_g_norm_mix=m_g_norm_mix, m_w_in=m_w_in, m_b_gate=m_b_gate, m_conv_w=m_conv_w, m_w_branch_attn=m_w_branch_attn, m_w_branch_conv=m_w_branch_conv, m_w_out=m_w_out, m_g_norm_mlp=m_g_norm_mlp, m_w_mlp_in=m_w_mlp_in, m_w_mlp_out=m_w_mlp_out, m_g_norm_final=m_g_norm_final, v_w_ada=v_w_ada, v_b_ada=v_b_ada, v_g_norm_mix=v_g_norm_mix, v_w_in=v_w_in, v_b_gate=v_b_gate, v_conv_w=v_conv_w, v_w_branch_attn=v_w_branch_attn, v_w_branch_conv=v_w_branch_conv, v_w_out=v_w_out, v_g_norm_mlp=v_g_norm_mlp, v_w_mlp_in=v_w_mlp_in, v_w_mlp_out=v_w_mlp_out, v_g_norm_final=v_g_norm_final)
    weights = {n: given[n] for n in TWIN_WEIGHTS}
    shared = {n: given[n] for n in SHARED_INPUTS}
    per_example = {n: given[n] for n in ['x', 'c']}
    grad_fn = _jax.value_and_grad(_loss, argnums=(0, 1))

    def one_microbatch(ex, loss_target):
        ex = dict(ex)
        diff = ex.pop(TWIN_DIFF_INPUT)
        return grad_fn(weights, diff, {**shared, **ex}, loss_target)

    if N_MICROBATCH == 1:
        loss, (grad_w, grad_x) = one_microbatch(per_example, given["loss_target"])
    else:
        def body(carry, xs):
            loss_sum, grad_sum = carry
            l_k, (gw_k, gx_k) = one_microbatch(xs[0], xs[1])
            with _jax.named_scope("update"):
                return (loss_sum + l_k, _jax.tree.map(_jnp.add, grad_sum, gw_k)), gx_k

        init = (_jnp.zeros((), _jnp.float32), _jax.tree.map(_jnp.zeros_like, weights))
        (loss, grad_w), grad_x = _jax.lax.scan(body, init, (per_example, given["loss_target"]))
    with _jax.named_scope("update"):
        delta_w, new_m, new_v = {}, {}, {}
        for n in TWIN_WEIGHTS:
            delta_w[n], new_m[n], new_v[n] = _adamw(weights[n], grad_w[n], given["m_" + n], given["v_" + n])
    return (loss, grad_x, *[grad_w[n] for n in TWIN_WEIGHTS], *[delta_w[n] for n in TWIN_WEIGHTS],
            *[new_m[n] for n in TWIN_WEIGHTS], *[new_v[n] for n in TWIN_WEIGHTS])
```

```python
import functools

import numpy as np
import jax
import jax.numpy as jnp
from jax import lax
from jax.experimental import pallas as pl
from jax.experimental.pallas import tpu as pltpu

F32, BF16 = jnp.float32, jnp.bfloat16
D = 1024
HEAD = 128
DILATIONS = (1, 4, 16)
N_SLOT = 4
AOW = N_SLOT * HEAD
DFF = 4 * D
N_DEV = 8
EPS = 1e-6
NEG = -1e30
SCALE = HEAD ** -0.5
LR, B1, B2, ADAM_EPS, WD, STEP = 0.001, 0.9, 0.999, 1e-08, 0.01, 10
V7X_VMEM_LIMIT = 56 * 1024 * 1024
MESH = pl.DeviceIdType.MESH
AXES = ("x", "y", "c")


def _cparams(*sem):
    if sem:
        return pltpu.CompilerParams(dimension_semantics=sem, vmem_limit_bytes=V7X_VMEM_LIMIT)
    return pltpu.CompilerParams(vmem_limit_bytes=V7X_VMEM_LIMIT)


def _nn(a, b):
    return jnp.dot(a, b, preferred_element_type=F32)


def _nt(a, b):
    return lax.dot_general(a, b, (((1,), (1,)), ((), ())), preferred_element_type=F32)


def _tn(a, b):
    return lax.dot_general(a, b, (((0,), (0,)), ((), ())), preferred_element_type=F32)


def _rms_r(x):
    return lax.rsqrt(jnp.mean(x * x, axis=-1, keepdims=True) + EPS)


def _rms_bwd(x, r, g, dn):
    gy = dn * g
    dx = r * gy - x * (r * r * r) * jnp.mean(x * gy, axis=-1, keepdims=True)
    return dx, dn * (x * r)


def _sigmoid(t):
    return 1.0 / (1.0 + jnp.exp(-t))


def _rowsum(v):
    return jnp.sum(v, axis=0, keepdims=True)


def _vec_spec(n=D):
    return pl.BlockSpec((1, n), lambda *_: (0, 0))


def _const_spec(shape):
    nd = len(shape)
    return pl.BlockSpec(shape, lambda *_: (0,) * nd)


def _win_rowblock(j):
    return jnp.where(j < 9, (j % 3) * 3 + j // 3, j)


def _proj(x, g, sc, sh, w_int):
    S = x.shape[0]
    tm = 512

    def body(x_ref, g_ref, sc_ref, sh_ref, w_ref, h_ref, q_ref, e_ref):
        j = pl.program_id(1)

        @pl.when(j == 0)
        def _():
            xv = x_ref[...]
            h = xv * _rms_r(xv) * g_ref[...] * (1.0 + sc_ref[...]) + sh_ref[...]
            h_ref[...] = h.astype(BF16)

        acc = _nt(h_ref[...], w_ref[...])

        @pl.when(j < 9)
        def _():
            q_ref[0] = acc

        @pl.when(j >= 9)
        def _():
            e_ref[0] = acc.astype(BF16)

    def e_idx(i, j):
        k = jnp.maximum(j - 9, 0)
        return (k // 2, i, k % 2)

    return pl.pallas_call(
        body, name="proj", grid=(S // tm, 19),
        in_specs=[pl.BlockSpec((tm, D), lambda i, j: (i, 0)), _vec_spec(), _vec_spec(), _vec_spec(),
                  pl.BlockSpec((512, D), lambda i, j: (_win_rowblock(j), 0))],
        out_specs=[pl.BlockSpec((tm, D), lambda i, j: (i, 0)),
                   pl.BlockSpec((1, tm, 512), lambda i, j: (jnp.minimum(j, 8), i, 0)),
                   pl.BlockSpec((1, tm, 512), e_idx)],
        out_shape=[jax.ShapeDtypeStruct((S, D), BF16), jax.ShapeDtypeStruct((9, S, 512), F32),
                   jax.ShapeDtypeStruct((5, S, D), BF16)],
        compiler_params=_cparams("parallel", "arbitrary"),
    )(x, g, sc, sh, w_int)


def _bias_table():
    slopes = (2.0 ** (-8.0 * np.arange(1, 13, dtype=np.float32) / 12.0)).astype(np.float32)
    qi = np.arange(HEAD)[:, None]
    kj = np.arange(2 * HEAD)[None, :]
    delta = HEAD + qi - kj
    mask = (delta >= 0) & (delta <= HEAD)
    out = np.zeros((3, N_SLOT, HEAD, 2 * HEAD), np.float32)
    for gi, d in enumerate(DILATIONS):
        for j in range(N_SLOT):
            bias = -slopes[gi * N_SLOT + j] * (delta * d).astype(np.float32)
            out[gi, j] = np.where(mask, bias, NEG)
    return jnp.asarray(out)


def _block_rows(b, d):
    r = b % d
    n = b // d
    st = n * (HEAD * d) + r
    stp = jnp.maximum(n - 1, 0) * (HEAD * d) + r
    return n, st, stp


def _attn_fwd(qkv, bias):
    S = qkv.shape[2]
    nblk = S // HEAD

    def body(qkv_ref, b_ref, o_ref, lse_ref, m_s, l_s, a_s):
        g = pl.program_id(1)

        @pl.when(g == 0)
        def _():
            m_s[...] = jnp.full_like(m_s, NEG)
            l_s[...] = jnp.zeros_like(l_s)
            a_s[...] = jnp.zeros_like(a_s)

        bias = b_ref[0, 0]
        col = lax.broadcasted_iota(jnp.int32, bias.shape, 1)
        bias_first = jnp.where(col < HEAD, NEG, bias)

        for gi, d in enumerate(DILATIONS):
            @pl.when(g == gi)
            def _(d=d):
                def step(b, carry):
                    n, st, stp = _block_rows(b, d)
                    cur = pl.ds(st, HEAD, stride=d)
                    prv = pl.ds(stp, HEAD, stride=d)
                    q = qkv_ref.at[0, 0][cur, :].astype(BF16)
                    kw = jnp.concatenate([qkv_ref.at[0, 1][prv, :], qkv_ref.at[0, 1][cur, :]], axis=0).astype(BF16)
                    vw = jnp.concatenate([qkv_ref.at[0, 2][prv, :], qkv_ref.at[0, 2][cur, :]], axis=0).astype(BF16)
                    s = _nt(q, kw) * SCALE + jnp.where(n > 0, bias, bias_first)
                    m_old = m_s[cur, :]
                    m_new = jnp.maximum(m_old, jnp.max(s, axis=-1, keepdims=True))
                    alpha = jnp.exp(m_old - m_new)
                    p = jnp.exp(s - m_new[:, :1])
                    l_s[cur, :] = alpha * l_s[cur, :] + jnp.sum(p, axis=-1, keepdims=True)
                    a_s[cur, :] = alpha * a_s[cur, :] + _nn(p.astype(BF16), vw)
                    m_s[cur, :] = m_new
                    return carry

                lax.fori_loop(0, nblk, step, 0)

        @pl.when(g == len(DILATIONS) - 1)
        def _():
            l = l_s[...]
            o_ref[...] = a_s[...] / l
            lse_ref[...] = m_s[...] + jnp.log(l)

    return pl.pallas_call(
        body, name="attn_fwd", grid=(N_SLOT, 3),
        in_specs=[pl.BlockSpec((1, 3, S, HEAD), lambda j, g: (g, 0, 0, j)),
                  pl.BlockSpec((1, 1, HEAD, 2 * HEAD), lambda j, g: (g, j, 0, 0))],
        out_specs=[pl.BlockSpec((S, HEAD), lambda j, g: (0, j)), pl.BlockSpec((S, HEAD), lambda j, g: (0, j))],
        out_shape=[jax.ShapeDtypeStruct((S, AOW), F32), jax.ShapeDtypeStruct((S, AOW), F32)],
        scratch_shapes=[pltpu.VMEM((S, HEAD), F32)] * 3,
        compiler_params=_cparams("parallel", "arbitrary"),
    )(qkv, bias)


def _shift_down(z, k, halo_rows):
    out = pltpu.roll(z, k, axis=0)
    rid = lax.broadcasted_iota(jnp.int32, z.shape, 0)
    for t in range(k):
        out = jnp.where(rid == t, halo_rows[t], out)
    return out


def _shift_up(z, k, halo_rows):
    n = z.shape[0]
    out = pltpu.roll(z, n - k, axis=0)
    rid = lax.broadcasted_iota(jnp.int32, z.shape, 0)
    for t in range(k):
        out = jnp.where(rid == n - k + t, halo_rows[t], out)
    return out


def _e_spec(chunk, tm):
    return pl.BlockSpec((1, tm, D), lambda i, c=chunk: (c, i, 0))


def _e_prev_spec(chunk, tm):
    return pl.BlockSpec((1, 16, D), lambda i, c=chunk: (c, jnp.maximum(i * (tm // 16) - 1, 0), 0))


def _e_next_spec(chunk, tm, S):
    return pl.BlockSpec((1, 16, D), lambda i, c=chunk: (c, jnp.minimum((i + 1) * (tm // 16), S // 16 - 1), 0))


def _mix(o_attn, e, cw8, ba, bb, w_bat, w_bc):
    S = o_attn.shape[0]
    tm = 256

    def body(o_ref, cb_ref, cc_ref, cx_ref, ga_ref, gb_ref, ccp_ref, cxp_ref, cw_ref, ba_ref, bb_ref, wba_ref, wbc_ref,
             obf_ref, cbu_ref, ya_ref, yc_ref, mg_ref):
        i = pl.program_id(0)
        o = o_ref[...].astype(BF16)
        obf_ref[...] = o
        ya = _nt(o, wba_ref[...])
        z = cc_ref[0].astype(F32) * cx_ref[0].astype(F32)
        zp = ccp_ref[0].astype(F32) * cxp_ref[0].astype(F32) * (i > 0).astype(F32)
        z1 = _shift_down(z, 1, [zp[15:16]])
        z2 = _shift_down(z, 2, [zp[14:15], zp[15:16]])
        cw = cw_ref[...]
        u = cw[0:1] * z2 + cw[1:2] * z1 + cw[2:3] * z
        cbu = (cb_ref[0].astype(F32) * u).astype(BF16)
        cbu_ref[...] = cbu
        yc = _nn(cbu, wbc_ref[...])
        sa = _sigmoid(ga_ref[0].astype(F32) + ba_ref[...])
        sb = _sigmoid(gb_ref[0].astype(F32) + bb_ref[...])
        ya_ref[...] = ya.astype(BF16)
        yc_ref[...] = yc.astype(BF16)
        mg_ref[...] = (sa * ya + sb * yc).astype(BF16)

    row = lambda w: pl.BlockSpec((tm, w), lambda i: (i, 0))
    return pl.pallas_call(
        body, name="mix", grid=(S // tm,),
        in_specs=[row(AOW)] + [_e_spec(c, tm) for c in range(5)] + [_e_prev_spec(1, tm), _e_prev_spec(2, tm),
                  _const_spec((8, D)), _vec_spec(), _vec_spec(), _const_spec((D, AOW)), _const_spec((D, D))],
        out_specs=[row(AOW), row(D), row(D), row(D), row(D)],
        out_shape=[jax.ShapeDtypeStruct((S, AOW), BF16)] + [jax.ShapeDtypeStruct((S, D), BF16)] * 4,
        compiler_params=_cparams("parallel"),
    )(o_attn, e, e, e, e, e, e, e, cw8, ba, bb, w_bat, w_bc)


def _out_proj(merged, w_out, x, gate1, g_mlp, sc2, sh2):
    S = x.shape[0]
    tm = 512

    def body(mg_ref, w_ref, x_ref, gt_ref, g_ref, sc_ref, sh_ref, x1_ref, mo_ref, h2_ref):
        mo = _nn(mg_ref[...], w_ref[...])
        mo_ref[...] = mo.astype(BF16)
        x1 = x_ref[...] + gt_ref[...] * mo
        x1_ref[...] = x1
        h2 = x1 * _rms_r(x1) * g_ref[...] * (1.0 + sc_ref[...]) + sh_ref[...]
        h2_ref[...] = h2.astype(BF16)

    row = pl.BlockSpec((tm, D), lambda i: (i, 0))
    return pl.pallas_call(
        body, name="out_proj", grid=(S // tm,),
        in_specs=[row, _const_spec((D, D)), row, _vec_spec(), _vec_spec(), _vec_spec(), _vec_spec()],
        out_specs=[row, row, row],
        out_shape=[jax.ShapeDtypeStruct((S, D), F32), jax.ShapeDtypeStruct((S, D), BF16), jax.ShapeDtypeStruct((S, D), BF16)],
        compiler_params=_cparams("parallel"),
    )(merged, w_out, x, gate1, g_mlp, sc2, sh2)


def _mlp_in(h2, w_mit):
    S = h2.shape[0]
    tm, tn = 512, 1024

    def body(h_ref, w_ref, a_ref, f_ref):
        a = _nt(h_ref[...], w_ref[...])
        a_ref[...] = a.astype(BF16)
        f_ref[...] = jnp.square(jnp.maximum(a, 0.0)).astype(BF16)

    blk = pl.BlockSpec((tm, tn), lambda i, j: (i, j))
    return pl.pallas_call(
        body, name="mlp_in", grid=(S // tm, DFF // tn),
        in_specs=[pl.BlockSpec((tm, D), lambda i, j: (i, 0)), pl.BlockSpec((tn, D), lambda i, j: (j, 0))],
        out_specs=[blk, blk],
        out_shape=[jax.ShapeDtypeStruct((S, DFF), BF16)] * 2,
        compiler_params=_cparams("parallel", "parallel"),
    )(h2, w_mit)


def _mlp_out(f, w_mo, x1, gate2, g_fin, tgt):
    S = x1.shape[0]
    tm, tk = 512, 1024
    nk = DFF // tk

    def body(f_ref, w_ref, x1_ref, gt_ref, g_ref, t_ref, mlp_ref, dx2_ref, pv_ref, acc):
        i, k = pl.program_id(0), pl.program_id(1)

        @pl.when((i == 0) & (k == 0))
        def _():
            pv_ref[...] = jnp.zeros_like(pv_ref)

        @pl.when(k == 0)
        def _():
            acc[...] = jnp.zeros_like(acc)

        acc[...] += _nn(f_ref[...], w_ref[...])

        @pl.when(k == nk - 1)
        def _():
            mlp = acc[...]
            mlp_ref[...] = mlp.astype(BF16)
            x2 = x1_ref[...] + gt_ref[...] * mlp
            r = _rms_r(x2)
            g = g_ref[...]
            err = x2 * r * g - t_ref[...]
            dy = err * (1.0 / D)
            dx2, pg = _rms_bwd(x2, r, g, dy)
            dx2_ref[...] = dx2
            pv_ref[0:1, :] += _rowsum(pg)
            pv_ref[1:2, :] += 0.5 * _rowsum(jnp.mean(err * err, axis=-1, keepdims=True))

    row = pl.BlockSpec((tm, D), lambda i, k: (i, 0))
    return pl.pallas_call(
        body, name="mlp_out", grid=(S // tm, nk),
        in_specs=[pl.BlockSpec((tm, tk), lambda i, k: (i, k)), pl.BlockSpec((tk, D), lambda i, k: (k, 0)),
                  row, _vec_spec(), _vec_spec(), row],
        out_specs=[row, row, _const_spec((8, D))],
        out_shape=[jax.ShapeDtypeStruct((S, D), BF16), jax.ShapeDtypeStruct((S, D), F32), jax.ShapeDtypeStruct((8, D), F32)],
        scratch_shapes=[pltpu.VMEM((tm, D), F32)],
        compiler_params=_cparams("arbitrary", "arbitrary"),
    )(f, w_mo, x1, gate2, g_fin, tgt)


def _bwd_mlp_a(dx2, gate2, mlp, w_mo, a):
    S = dx2.shape[0]
    tm, tn = 512, 1024

    def body(dx_ref, gt_ref, mlp_ref, w_ref, a_ref, da_ref, dmo_ref, pv_ref):
        i, j = pl.program_id(0), pl.program_id(1)

        @pl.when((i == 0) & (j == 0))
        def _():
            pv_ref[...] = jnp.zeros_like(pv_ref)

        @pl.when(j == 0)
        def _():
            dx = dx_ref[...]
            dmo_ref[...] = (dx * gt_ref[...]).astype(BF16)
            pv_ref[0:1, :] += _rowsum(dx * mlp_ref[...].astype(F32))

        df = _nt(dmo_ref[...], w_ref[...])
        da_ref[...] = (df * (2.0 * jnp.maximum(a_ref[...].astype(F32), 0.0))).astype(BF16)

    row = pl.BlockSpec((tm, D), lambda i, j: (i, 0))
    blk = pl.BlockSpec((tm, tn), lambda i, j: (i, j))
    return pl.pallas_call(
        body, name="bwd_mlp_a", grid=(S // tm, DFF // tn),
        in_specs=[row, _vec_spec(), row, pl.BlockSpec((tn, D), lambda i, j: (j, 0)), blk],
        out_specs=[blk, row, _const_spec((8, D))],
        out_shape=[jax.ShapeDtypeStruct((S, DFF), BF16), jax.ShapeDtypeStruct((S, D), BF16), jax.ShapeDtypeStruct((8, D), F32)],
        compiler_params=_cparams("arbitrary", "arbitrary"),
    )(dx2, gate2, mlp, w_mo, a)


def _bwd_mlp_b(da, w_mit, x1, dx2, g_mlp, sc2):
    S = x1.shape[0]
    tm, tk = 512, 1024
    nk = DFF // tk

    def body(da_ref, w_ref, x1_ref, dx2_ref, g_ref, sc_ref, dx1_ref, pv_ref, acc):
        i, k = pl.program_id(0), pl.program_id(1)

        @pl.when((i == 0) & (k == 0))
        def _():
            pv_ref[...] = jnp.zeros_like(pv_ref)

        @pl.when(k == 0)
        def _():
            acc[...] = jnp.zeros_like(acc)

        acc[...] += _nn(da_ref[...], w_ref[...])

        @pl.when(k == nk - 1)
        def _():
            dh = acc[...]
            x1 = x1_ref[...]
            r = _rms_r(x1)
            g = g_ref[...]
            dxn, pg = _rms_bwd(x1, r, g, dh * (1.0 + sc_ref[...]))
            dx1_ref[...] = dx2_ref[...] + dxn
            pv_ref[0:1, :] += _rowsum(dh)
            pv_ref[1:2, :] += _rowsum(dh * (x1 * r * g))
            pv_ref[2:3, :] += _rowsum(pg)

    row = pl.BlockSpec((tm, D), lambda i, k: (i, 0))
    return pl.pallas_call(
        body, name="bwd_mlp_b", grid=(S // tm, nk),
        in_specs=[pl.BlockSpec((tm, tk), lambda i, k: (i, k)), pl.BlockSpec((tk, D), lambda i, k: (k, 0)),
                  row, row, _vec_spec(), _vec_spec()],
        out_specs=[row, _const_spec((8, D))],
        out_shape=[jax.ShapeDtypeStruct((S, D), F32), jax.ShapeDtypeStruct((8, D), F32)],
        scratch_shapes=[pltpu.VMEM((tm, D), F32)],
        compiler_params=_cparams("arbitrary", "arbitrary"),
    )(da, w_mit, x1, dx2, g_mlp, sc2)


def _bwd_mix(dx1, gate1, mo, e, cw8, ba, bb, ya, yc, o_attn, w_out, w_bc, w_bat):
    S = dx1.shape[0]
    tm = 256
    n_tiles = S // tm

    def body(dx_ref, dxn_ref, gt_ref, mo_ref, cb_ref, cc_ref, cx_ref, ga_ref, gb_ref, cbn_ref, gbn_ref, ccp_ref, cxp_ref,
             cw_ref, ba_ref, bb_ref, ya_ref, yc_ref, o_ref, wout_ref, wbc_ref, wba_ref,
             dmo_ref, dya_ref, dyc_ref, do_ref, dl_ref, de_ref, pv_ref):
        i = pl.program_id(0)

        @pl.when(i == 0)
        def _():
            pv_ref[...] = jnp.zeros_like(pv_ref)

        gate = gt_ref[...]
        bbv = bb_ref[...]

        def conv_branch_grad(dx_rows, gb_rows):
            dmo = (dx_rows * gate).astype(BF16)
            dmg = _nt(dmo, wout_ref[...])
            sb = _sigmoid(gb_rows + bbv)
            dyc = dmg * sb
            return dmo, dmg, sb, dyc, _nt(dyc.astype(BF16), wbc_ref[...])

        dx = dx_ref[...]
        cb = cb_ref[0].astype(F32)
        cc = cc_ref[0].astype(F32)
        cx = cx_ref[0].astype(F32)
        dmo, dmg, sb, dyc, dcbu = conv_branch_grad(dx, gb_ref[0].astype(F32))
        dmo_ref[...] = dmo
        pv_ref[0:1, :] += _rowsum(dx * mo_ref[...].astype(F32))
        sa = _sigmoid(ga_ref[0].astype(F32) + ba_ref[...])
        dya = (dmg * sa).astype(BF16)
        dya_ref[...] = dya
        dyc_ref[...] = dyc.astype(BF16)
        dga = dmg * ya_ref[...].astype(F32) * sa * (1.0 - sa)
        dgb = dmg * yc_ref[...].astype(F32) * sb * (1.0 - sb)
        pv_ref[1:2, :] += _rowsum(dga)
        pv_ref[2:3, :] += _rowsum(dgb)

        do = _nn(dya, wba_ref[...])
        do_ref[...] = do
        prod = do * o_ref[...]
        dl_ref[...] = jnp.concatenate(
            [jnp.broadcast_to(jnp.sum(prod[:, s * HEAD:(s + 1) * HEAD], axis=-1, keepdims=True), (tm, HEAD))
             for s in range(N_SLOT)], axis=1)

        z = cc * cx
        zp = ccp_ref[0].astype(F32) * cxp_ref[0].astype(F32) * (i > 0).astype(F32)
        z1 = _shift_down(z, 1, [zp[15:16]])
        z2 = _shift_down(z, 2, [zp[14:15], zp[15:16]])
        cw = cw_ref[...]
        u = cw[0:1] * z2 + cw[1:2] * z1 + cw[2:3] * z
        du = dcbu * cb
        dcbu_n = conv_branch_grad(dxn_ref[...], gbn_ref[0].astype(F32))[4]
        du_n = dcbu_n * cbn_ref[0].astype(F32) * (i < n_tiles - 1).astype(F32)
        du1 = _shift_up(du, 1, [du_n[0:1]])
        du2 = _shift_up(du, 2, [du_n[0:1], du_n[1:2]])
        dz = cw[2:3] * du + cw[1:2] * du1 + cw[0:1] * du2
        pv_ref[3:4, :] += _rowsum(du * z2)
        pv_ref[4:5, :] += _rowsum(du * z1)
        pv_ref[5:6, :] += _rowsum(du * z)

        de_ref[0] = (dcbu * u).astype(BF16)
        de_ref[1] = (dz * cx).astype(BF16)
        de_ref[2] = (dz * cc).astype(BF16)
        de_ref[3] = dga.astype(BF16)
        de_ref[4] = dgb.astype(BF16)

    row = lambda w: pl.BlockSpec((tm, w), lambda i: (i, 0))
    nxt = pl.BlockSpec((16, D), lambda i: (jnp.minimum((i + 1) * (tm // 16), S // 16 - 1), 0))
    return pl.pallas_call(
        body, name="bwd_mix", grid=(n_tiles,),
        in_specs=[row(D), nxt, _vec_spec(), row(D)] + [_e_spec(c, tm) for c in range(5)]
                 + [_e_next_spec(0, tm, S), _e_next_spec(4, tm, S), _e_prev_spec(1, tm), _e_prev_spec(2, tm),
                    _const_spec((8, D)), _vec_spec(), _vec_spec(), row(D), row(D), row(AOW),
                    _const_spec((D, D)), _const_spec((D, D)), _const_spec((D, AOW))],
        out_specs=[row(D), row(D), row(D), row(AOW), row(AOW), pl.BlockSpec((5, tm, D), lambda i: (0, i, 0)),
                   _const_spec((8, D))],
        out_shape=[jax.ShapeDtypeStruct((S, D), BF16)] * 3 + [jax.ShapeDtypeStruct((S, AOW), F32)] * 2
                  + [jax.ShapeDtypeStruct((5, S, D), BF16), jax.ShapeDtypeStruct((8, D), F32)],
        compiler_params=_cparams("arbitrary"),
    )(dx1, dx1, gate1, mo, e, e, e, e, e, e, e, e, e, cw8, ba, bb, ya, yc, o_attn, w_out, w_bc, w_bat)


def _attn_bwd(qkv, do, lse, dl, bias):
    S = qkv.shape[2]
    nblk = S // HEAD

    def body(qkv_ref, do_ref, lse_ref, dl_ref, b_ref, d_ref):
        g = pl.program_id(1)
        d_ref[...] = jnp.zeros_like(d_ref)
        bias = b_ref[0, 0]
        col = lax.broadcasted_iota(jnp.int32, bias.shape, 1)
        bias_first = jnp.where(col < HEAD, NEG, bias)

        for gi, d in enumerate(DILATIONS):
            @pl.when(g == gi)
            def _(d=d):
                def step(b, carry):
                    n, st, stp = _block_rows(b, d)
                    cur = pl.ds(st, HEAD, stride=d)
                    prv = pl.ds(stp, HEAD, stride=d)
                    q = qkv_ref.at[0, 0][cur, :].astype(BF16)
                    kw = jnp.concatenate([qkv_ref.at[0, 1][prv, :], qkv_ref.at[0, 1][cur, :]], axis=0).astype(BF16)
                    vw = jnp.concatenate([qkv_ref.at[0, 2][prv, :], qkv_ref.at[0, 2][cur, :]], axis=0).astype(BF16)
                    s = _nt(q, kw) * SCALE + jnp.where(n > 0, bias, bias_first)
                    p = jnp.exp(s - lse_ref[cur, :][:, :1])
                    dob = do_ref[cur, :].astype(BF16)
                    dvw = _tn(p.astype(BF16), dob)
                    dp = _nt(dob, vw)
                    ds = (p * (dp - dl_ref[cur, :][:, :1]) * SCALE).astype(BF16)
                    d_ref.at[0, 0][cur, :] = _nn(ds, kw)
                    dkw = _tn(ds, q)
                    d_ref.at[0, 1][cur, :] += dkw[HEAD:]
                    d_ref.at[0, 1][prv, :] += dkw[:HEAD]
                    d_ref.at[0, 2][cur, :] += dvw[HEAD:]
                    d_ref.at[0, 2][prv, :] += dvw[:HEAD]
                    return carry

                lax.fori_loop(0, nblk, step, 0)

    col_blk = pl.BlockSpec((S, HEAD), lambda j, g: (0, j))
    qkv_blk = pl.BlockSpec((1, 3, S, HEAD), lambda j, g: (g, 0, 0, j))
    return pl.pallas_call(
        body, name="attn_bwd", grid=(N_SLOT, 3),
        in_specs=[qkv_blk, col_blk, col_blk, col_blk, pl.BlockSpec((1, 1, HEAD, 2 * HEAD), lambda j, g: (g, j, 0, 0))],
        out_specs=qkv_blk,
        out_shape=jax.ShapeDtypeStruct((3, 3, S, AOW), F32),
        compiler_params=_cparams("parallel", "arbitrary"),
    )(qkv, do, lse, dl, bias)


def _bwd_in(dqkv, de, w_int, x, dx1, g_mix, sc1):
    S = x.shape[0]
    tm = 512

    def body(dq_ref, de_ref, w_ref, x_ref, dx1_ref, g_ref, sc_ref, gx_ref, pv_ref, acc):
        i, k = pl.program_id(0), pl.program_id(1)

        @pl.when((i == 0) & (k == 0))
        def _():
            pv_ref[...] = jnp.zeros_like(pv_ref)

        @pl.when(k == 0)
        def _():
            acc[...] = jnp.zeros_like(acc)

        @pl.when(k < 9)
        def _():
            acc[...] += _nn(dq_ref[0].astype(BF16), w_ref[...])

        @pl.when(k >= 9)
        def _():
            acc[...] += _nn(de_ref[0], w_ref[...])

        @pl.when(k == 18)
        def _():
            dh = acc[...]
            xv = x_ref[...]
            r = _rms_r(xv)
            g = g_ref[...]
            dxn, pg = _rms_bwd(xv, r, g, dh * (1.0 + sc_ref[...]))
            gx_ref[...] = dx1_ref[...] + dxn
            pv_ref[0:1, :] += _rowsum(dh)
            pv_ref[1:2, :] += _rowsum(dh * (xv * r * g))
            pv_ref[2:3, :] += _rowsum(pg)

    def e_idx(i, k):
        kk = jnp.maximum(k - 9, 0)
        return (kk // 2, i, kk % 2)

    row = pl.BlockSpec((tm, D), lambda i, k: (i, 0))
    return pl.pallas_call(
        body, name="bwd_in", grid=(S // tm, 19),
        in_specs=[pl.BlockSpec((1, tm, 512), lambda i, k: (jnp.minimum(k, 8), i, 0)), pl.BlockSpec((1, tm, 512), e_idx),
                  pl.BlockSpec((512, D), lambda i, k: (_win_rowblock(k), 0)), row, row, _vec_spec(), _vec_spec()],
        out_specs=[row, _const_spec((8, D))],
        out_shape=[jax.ShapeDtypeStruct((S, D), F32), jax.ShapeDtypeStruct((8, D), F32)],
        scratch_shapes=[pltpu.VMEM((tm, D), F32)],
        compiler_params=_cparams("arbitrary", "arbitrary"),
    )(dqkv, de, w_int, x, dx1, g_mix, sc1)


def _grad_w(name, a, b):
    S, ka = a.shape
    nb = b.shape[1]
    tt = 1024
    nt = S // tt

    def body(a_ref, b_ref, o_ref, acc):
        t = pl.program_id(1)

        @pl.when(t == 0)
        def _():
            acc[...] = jnp.zeros_like(acc)

        acc[...] += _tn(a_ref[...], b_ref[...])

        @pl.when(t == nt - 1)
        def _():
            o_ref[...] = acc[...].astype(BF16)

    return pl.pallas_call(
        body, name=name, grid=(ka // 512, nt),
        in_specs=[pl.BlockSpec((tt, 512), lambda n, t: (t, n)), pl.BlockSpec((tt, nb), lambda n, t: (t, 0))],
        out_specs=pl.BlockSpec((512, nb), lambda n, t: (n, 0)),
        out_shape=jax.ShapeDtypeStruct((ka, nb), BF16),
        scratch_shapes=[pltpu.VMEM((512, nb), F32)],
        compiler_params=_cparams("parallel", "arbitrary"),
    )(a, b)


def _grad_w_in(dqkv, de, h):
    S = h.shape[0]
    tt = 1024
    nt = S // tt

    def body(dq_ref, de_ref, h_ref, o_ref, acc):
        n, t = pl.program_id(0), pl.program_id(1)

        @pl.when(t == 0)
        def _():
            acc[...] = jnp.zeros_like(acc)

        @pl.when(n < 9)
        def _():
            acc[...] += _tn(dq_ref[0].astype(BF16), h_ref[...])

        @pl.when(n >= 9)
        def _():
            acc[...] += _tn(de_ref[0], h_ref[...])

        @pl.when(t == nt - 1)
        def _():
            o_ref[...] = acc[...].astype(BF16)

    def e_idx(n, t):
        kk = jnp.maximum(n - 9, 0)
        return (kk // 2, jnp.where(n >= 9, t, 0), kk % 2)

    return pl.pallas_call(
        body, name="grad_w_in", grid=(19, nt),
        in_specs=[pl.BlockSpec((1, tt, 512), lambda n, t: (jnp.minimum(n, 8), jnp.where(n < 9, t, nt - 1), 0)),
                  pl.BlockSpec((1, tt, 512), e_idx), pl.BlockSpec((tt, D), lambda n, t: (t, 0))],
        out_specs=pl.BlockSpec((512, D), lambda n, t: (_win_rowblock(n), 0)),
        out_shape=jax.ShapeDtypeStruct((19 * 512, D), BF16),
        scratch_shapes=[pltpu.VMEM((512, D), F32)],
        compiler_params=_cparams("parallel", "arbitrary"),
    )(dqkv, de, h)


def _local_step(x, tgt, mod, g_mix, g_mlp, g_fin, ba, bb, cw8, w_int, w_bat, w_bc, w_out, w_mit, w_mo):
    S = x.shape[0]
    sh1, sc1, gt1, sh2, sc2, gt2 = [mod[k:k + 1] for k in range(6)]
    bias = _bias_table()

    h, qkv, e = _proj(x, g_mix, sc1, sh1, w_int)
    qkv = qkv.reshape(3, 3, S, AOW)
    o_attn, lse = _attn_fwd(qkv, bias)
    o_bf, cbu, ya, yc, merged = _mix(o_attn, e, cw8, ba, bb, w_bat, w_bc)
    x1, mo, h2 = _out_proj(merged, w_out, x, gt1, g_mlp, sc2, sh2)
    a, f = _mlp_in(h2, w_mit)
    mlp, dx2, pv_f = _mlp_out(f, w_mo, x1, gt2, g_fin, tgt)

    da, dmo2, pv_a = _bwd_mlp_a(dx2, gt2, mlp, w_mo, a)
    dx1, pv_b = _bwd_mlp_b(da, w_mit, x1, dx2, g_mlp, sc2)
    dmo, dya, dyc, do, dl, de, pv_m = _bwd_mix(dx1, gt1, mo, e, cw8, ba, bb, ya, yc, o_attn, w_out, w_bc, w_bat)
    dqkv = _attn_bwd(qkv, do, lse, dl, bias).reshape(9, S, AOW)
    grad_x, pv_i = _bwd_in(dqkv, de, w_int, x, dx1, g_mix, sc1)

    grads = [
        _grad_w_in(dqkv, de, h),
        _grad_w("grad_w_ba", dya, o_bf),
        _grad_w("grad_w_bc", cbu, dyc),
        _grad_w("grad_w_out", merged, dmo),
        _grad_w("grad_w_mi", da, h2),
        _grad_w("grad_w_mo", f, dmo2),
    ]
    vec = jnp.concatenate([pv_i[0:2], pv_m[0:1], pv_b[0:2], pv_a[0:1], pv_i[2:3], pv_b[2:3], pv_f[0:1],
                           pv_m[1:3], pv_m[3:6], jnp.zeros((2, D), F32)], axis=0)
    return pv_f[1, 0], grad_x, vec, grads


def _my_place():
    return lax.axis_index("x"), lax.axis_index("y"), lax.axis_index("c")


def _dev_index(px, py, pc):
    return 4 * px + 2 * py + pc


def _allgather_weights(shards):
    nw = len(shards)
    HBM = pl.BlockSpec(memory_space=pl.ANY)

    def body(*refs):
        sh, full = refs[:nw], refs[nw:2 * nw]
        send_sems, recv_sems, local_sems = refs[2 * nw:]
        x, y, c = _my_place()
        me, sibling = (x, y, c), (x, y, 1 - c)
        chips = [(1 - x, y), (x, 1 - y), (1 - x, 1 - y)]

        def rows(w, px, py, pc):
            r = sh[w].shape[0]
            return full[w].at[pl.ds(pl.multiple_of(_dev_index(px, py, pc) * r, 16), r), :]

        def copy(w, k, block, to, src=None):
            return pltpu.make_async_remote_copy(
                src_ref=rows(w, *block) if src is None else src, dst_ref=rows(w, *block),
                send_sem=send_sems.at[w, k], recv_sem=recv_sems.at[w, k], device_id=to, device_id_type=MESH)

        mine = [pltpu.make_async_copy(sh[w], rows(w, *me), local_sems.at[w]) for w in range(nw)]
        for cp in mine:
            cp.start()
        first = []
        for w in range(nw):
            first.append(copy(w, 0, me, sibling, src=sh[w]))
            first += [copy(w, 1 + j, me, (*chip, c), src=sh[w]) for j, chip in enumerate(chips)]
        for cp in first:
            cp.start()
        passed = []
        for w in range(nw):
            for j, chip in enumerate(chips):
                copy(w, 1 + j, (*chip, c), me).wait_recv()
                fwd = copy(w, 4 + j, (*chip, c), sibling)
                fwd.start()
                passed.append(fwd)
        for w in range(nw):
            copy(w, 0, sibling, me).wait_recv()
            for j, chip in enumerate(chips):
                copy(w, 4 + j, (*chip, 1 - c), me).wait_recv()
        for cp in first + passed:
            cp.wait_send()
        for cp in mine:
            cp.wait()

    return pl.pallas_call(
        body, name="allgather_weights",
        out_shape=[jax.ShapeDtypeStruct((N_DEV * s.shape[0], s.shape[1]), s.dtype) for s in shards],
        in_specs=[HBM] * nw, out_specs=[HBM] * nw,
        scratch_shapes=[pltpu.SemaphoreType.DMA((nw, 7)), pltpu.SemaphoreType.DMA((nw, 7)), pltpu.SemaphoreType.DMA((nw,))],
    )(*shards)


def _peer(x, y, c, m):
    return (x ^ ((m >> 2) & 1), y ^ ((m >> 1) & 1), c ^ (m & 1))


def _exchange_partials(grads):
    nw = len(grads)
    HBM = pl.BlockSpec(memory_space=pl.ANY)

    def body(*refs):
        g, parts = refs[:nw], refs[nw:2 * nw]
        send_sems, recv_sems, local_sems = refs[2 * nw:]
        x, y, c = _my_place()
        my_idx = _dev_index(x, y, c)

        def rows(w, idx):
            r = parts[w].shape[1]
            return g[w].at[pl.ds(pl.multiple_of(idx * r, 16), r), :]

        def copy(w, m):
            peer = _peer(x, y, c, m)
            return pltpu.make_async_remote_copy(
                src_ref=rows(w, _dev_index(*peer)), dst_ref=parts[w].at[my_idx],
                send_sem=send_sems.at[w, m - 1], recv_sem=recv_sems.at[w, m - 1], device_id=peer, device_id_type=MESH)

        def arrival(w, m):
            peer = _peer(x, y, c, m)
            return pltpu.make_async_remote_copy(
                src_ref=rows(w, my_idx), dst_ref=parts[w].at[_dev_index(*peer)],
                send_sem=send_sems.at[w, m - 1], recv_sem=recv_sems.at[w, m - 1], device_id=peer, device_id_type=MESH)

        mine = [pltpu.make_async_copy(rows(w, my_idx), parts[w].at[my_idx], local_sems.at[w]) for w in range(nw)]
        for cp in mine:
            cp.start()
        sends = [copy(w, m) for w in range(nw) for m in range(1, N_DEV)]
        for cp in sends:
            cp.start()
        for w in range(nw):
            for m in range(1, N_DEV):
                arrival(w, m).wait_recv()
        for cp in sends:
            cp.wait_send()
        for cp in mine:
            cp.wait()

    return pl.pallas_call(
        body, name="exchange_partials",
        out_shape=[jax.ShapeDtypeStruct((N_DEV, a.shape[0] // N_DEV, a.shape[1]), a.dtype) for a in grads],
        in_specs=[HBM] * nw, out_specs=[HBM] * nw,
        scratch_shapes=[pltpu.SemaphoreType.DMA((nw, 7)), pltpu.SemaphoreType.DMA((nw, 7)), pltpu.SemaphoreType.DMA((nw,))],
    )(*grads)


def _allgather_small(v, name):
    r, ccols = v.shape

    def body(v_ref, out_ref, send_sems, recv_sems):
        x, y, c = _my_place()
        my_idx = _dev_index(x, y, c)
        out_ref[my_idx] = v_ref[...]

        def copy(m):
            peer = _peer(x, y, c, m)
            return pltpu.make_async_remote_copy(
                src_ref=v_ref, dst_ref=out_ref.at[my_idx],
                send_sem=send_sems.at[m - 1], recv_sem=recv_sems.at[m - 1], device_id=peer, device_id_type=MESH)

        def arrival(m):
            peer = _peer(x, y, c, m)
            return pltpu.make_async_remote_copy(
                src_ref=v_ref, dst_ref=out_ref.at[_dev_index(*peer)],
                send_sem=send_sems.at[m - 1], recv_sem=recv_sems.at[m - 1], device_id=peer, device_id_type=MESH)

        sends = [copy(m) for m in range(1, N_DEV)]
        for cp in sends:
            cp.start()
        for m in range(1, N_DEV):
            arrival(m).wait_recv()
        for cp in sends:
            cp.wait_send()

    return pl.pallas_call(
        body, name=name,
        out_shape=jax.ShapeDtypeStruct((N_DEV, r, ccols), v.dtype),
        in_specs=[pl.BlockSpec(memory_space=pltpu.VMEM)], out_specs=pl.BlockSpec(memory_space=pltpu.VMEM),
        scratch_shapes=[pltpu.SemaphoreType.DMA((7,)), pltpu.SemaphoreType.DMA((7,))],
    )(v)


def _ada_fwd(c_all, w_ada, b_cols):
    def body(c_ref, w_ref, b_ref, mod_ref, act_ref):
        cv = c_ref[...]
        act = cv * _sigmoid(cv)
        act_ref[...] = act
        mod_ref[...] = jnp.dot(act, w_ref[...], preferred_element_type=F32, precision=lax.Precision.HIGHEST) + b_ref[...]

    return pl.pallas_call(
        body, name="ada_fwd",
        out_shape=[jax.ShapeDtypeStruct((N_DEV, w_ada.shape[1]), F32), jax.ShapeDtypeStruct((N_DEV, D), F32)],
        compiler_params=_cparams(),
    )(c_all, w_ada, b_cols)


def _ada_bwd(act_t, gm_cols):
    def body(a_ref, g_ref, o_ref):
        o_ref[...] = jnp.dot(a_ref[...], g_ref[...], preferred_element_type=F32, precision=lax.Precision.HIGHEST)

    return pl.pallas_call(
        body, name="ada_bwd", out_shape=jax.ShapeDtypeStruct((D, gm_cols.shape[1]), F32), compiler_params=_cparams(),
    )(act_t, gm_cols)


def _row_tile(r):
    for t in (256, 304, 128, 64, 16):
        if r % t == 0:
            return t
    return r


def _sum_parts(parts, name):
    _, r, ccols = parts.shape
    tr = _row_tile(r)

    def body(p_ref, o_ref):
        acc = p_ref[0].astype(F32)
        for s in range(1, N_DEV):
            acc = acc + p_ref[s].astype(F32)
        o_ref[...] = acc

    return pl.pallas_call(
        body, name=name, grid=(r // tr,),
        in_specs=[pl.BlockSpec((N_DEV, tr, ccols), lambda i: (0, i, 0))],
        out_specs=pl.BlockSpec((tr, ccols), lambda i: (i, 0)),
        out_shape=jax.ShapeDtypeStruct((r, ccols), F32),
        compiler_params=_cparams("parallel"),
    )(parts)


def _adamw(w, g, m, v, name):
    r, ccols = w.shape
    tr = _row_tile(r)
    c1 = 1.0 / (1.0 - B1 ** STEP)
    c2 = 1.0 / (1.0 - B2 ** STEP)

    def body(w_ref, g_ref, m_ref, v_ref, d_ref, nm_ref, nv_ref):
        gv = g_ref[...]
        nm = B1 * m_ref[...] + (1.0 - B1) * gv
        nv = B2 * v_ref[...] + (1.0 - B2) * jnp.square(gv)
        nm_ref[...] = nm
        nv_ref[...] = nv
        d_ref[...] = -LR * ((nm * c1) / (jnp.sqrt(nv * c2) + ADAM_EPS) + WD * w_ref[...])

    blk = pl.BlockSpec((tr, ccols), lambda i: (i, 0))
    return pl.pallas_call(
        body, name=name, grid=(r // tr,), in_specs=[blk] * 4, out_specs=[blk] * 3,
        out_shape=[jax.ShapeDtypeStruct((r, ccols), F32)] * 3,
        compiler_params=_cparams("parallel"),
    )(w, g, m, v)


def _pack_vectors(b_ada, g_mix, g_mlp, g_fin, b_gate, conv_w):
    conv_rows = jnp.pad(conv_w.reshape(3, HEAD), ((0, 0), (0, D - HEAD)))
    return jnp.concatenate([b_ada.reshape(6, D), g_mix.reshape(1, D), g_mlp.reshape(1, D), g_fin.reshape(1, D),
                            b_gate.reshape(2, D), conv_rows, jnp.zeros((2, D), F32)], axis=0)


def _unpack_vectors(p):
    return (p[0:6].reshape(1, 6 * D), p[6:7], p[9:11].reshape(1, 2 * D), p[11:14, :HEAD].reshape(1, 3, HEAD),
            p[7:8], p[8])


def kernel(x, c, w_ada, b_ada, g_norm_mix, w_in, b_gate, conv_w, w_branch_attn, w_branch_conv, w_out, g_norm_mlp, w_mlp_in, w_mlp_out, g_norm_final, loss_target, m_w_ada, m_b_ada, m_g_norm_mix, m_w_in, m_b_gate, m_conv_w, m_w_branch_attn, m_w_branch_conv, m_w_out, m_g_norm_mlp, m_w_mlp_in, m_w_mlp_out, m_g_norm_final, v_w_ada, v_b_ada, v_g_norm_mix, v_w_in, v_b_gate, v_conv_w, v_w_branch_attn, v_w_branch_conv, v_w_out, v_g_norm_mlp, v_w_mlp_in, v_w_mlp_out, v_g_norm_final):
    S = x.shape[1]
    xi, yi, ci = _my_place()
    me = _dev_index(xi, yi, ci)
    x2 = x.reshape(S, D)
    tgt = loss_target.reshape(S, D)

    shards = [w_in[0].T.astype(BF16), w_branch_attn[0].T.astype(BF16), w_branch_conv[0].astype(BF16),
              w_out[0].astype(BF16), w_mlp_in[0].T.astype(BF16), w_mlp_out[0].astype(BF16)]
    w_int, w_bat, w_bc, w_o, w_mit, w_mo = _allgather_weights(shards)

    pay = jnp.zeros((8, D), F32).at[0].set(c[0]).at[1:4, :HEAD].set(conv_w[0])
    got = _allgather_small(pay, "gather_cond")
    c_all = got[:, 0, :]
    cw8 = jnp.pad(got[:, 1:4, :HEAD].transpose(1, 0, 2).reshape(3, D), ((0, 5), (0, 0)))
    ncol = w_ada.shape[2]
    b_cols = lax.dynamic_slice(b_ada, (0, me * ncol), (1, ncol))
    mod_cols, act = _ada_fwd(c_all, w_ada[0], b_cols)
    mod_all = _allgather_small(mod_cols, "gather_mod")
    mod = lax.dynamic_index_in_dim(mod_all, me, axis=1, keepdims=False).reshape(6, D)

    ba, bb = b_gate[:, :D], b_gate[:, D:]
    loss_part, grad_x, vec, grads = _local_step(
        x2, tgt, mod, g_norm_mix, g_norm_mlp, g_norm_final.reshape(1, D), ba, bb, cw8, w_int, w_bat, w_bc, w_o, w_mit, w_mo)
    loss = lax.psum(loss_part, AXES)

    vec_all = _allgather_small(vec, "gather_vec")
    vec_sum = _sum_parts(vec_all, "sum_vec")
    gm_all = vec_all[:, 0:6, :].reshape(N_DEV, 6 * D)
    gm_cols = lax.dynamic_slice(gm_all, (0, me * ncol), (N_DEV, ncol))
    g_w_ada = _ada_bwd(act.T, gm_cols)
    conv_cols = lax.dynamic_slice(vec_sum[11:14], (0, me * HEAD), (3, HEAD))
    g_pack = jnp.concatenate([vec_sum[0:11], jnp.pad(conv_cols, ((0, 0), (0, D - HEAD))), jnp.zeros((2, D), F32)], axis=0)
    packs = [_pack_vectors(*t) for t in ((b_ada, g_norm_mix, g_norm_mlp, g_norm_final, b_gate, conv_w),
                                         (m_b_ada, m_g_norm_mix, m_g_norm_mlp, m_g_norm_final, m_b_gate, m_conv_w),
                                         (v_b_ada, v_g_norm_mix, v_g_norm_mlp, v_g_norm_final, v_b_gate, v_conv_w))]
    d_pack, m_pack, v_pack = _adamw(packs[0], g_pack, packs[1], packs[2], "adamw_vectors")
    d_ada, nm_ada, nv_ada = _adamw(w_ada[0], g_w_ada, m_w_ada[0], v_w_ada[0], "adamw_w_ada")

    parts = _exchange_partials(grads)
    names = ("w_in", "w_ba", "w_bc", "w_out", "w_mi", "w_mo")
    sums = [_sum_parts(p, "sum_" + n) for p, n in zip(parts, names)]
    g_in, g_ba, g_bc, g_out, g_mi, g_mo = sums[0].T, sums[1].T, sums[2], sums[3], sums[4].T, sums[5]
    big = {}
    for n, w, g, m, v in (("w_in", w_in, g_in, m_w_in, v_w_in), ("w_ba", w_branch_attn, g_ba, m_w_branch_attn, v_w_branch_attn),
                          ("w_bc", w_branch_conv, g_bc, m_w_branch_conv, v_w_branch_conv), ("w_out", w_out, g_out, m_w_out, v_w_out),
                          ("w_mi", w_mlp_in, g_mi, m_w_mlp_in, v_w_mlp_in), ("w_mo", w_mlp_out, g_mo, m_w_mlp_out, v_w_mlp_out)):
        big[n] = (g[None],) + tuple(t[None] for t in _adamw(w[0], g, m[0], v[0], "adamw_" + n))

    gv = _unpack_vectors(g_pack)
    dv = _unpack_vectors(d_pack)
    mv = _unpack_vectors(m_pack)
    vv = _unpack_vectors(v_pack)

    def ordered(k, ada, vecs):
        return (ada[None], vecs[0], vecs[1], big["w_in"][k], vecs[2], vecs[3], big["w_ba"][k], big["w_bc"][k],
                big["w_out"][k], vecs[4], big["w_mi"][k], big["w_mo"][k], vecs[5])

    return (loss, grad_x.reshape(1, S, D), *ordered(0, g_w_ada, gv), *ordered(1, d_ada, dv),
            *ordered(2, nm_ada, mv), *ordered(3, nv_ada, vv))
```

```python
import functools

import numpy as np
import jax
import jax.numpy as jnp
from jax import lax
from jax.experimental import pallas as pl
from jax.experimental.pallas import tpu as pltpu

F32, BF16 = jnp.float32, jnp.bfloat16
D = 1024
HEAD = 128
DILATIONS = (1, 4, 16)
N_SLOT = 4
AOW = N_SLOT * HEAD
DFF = 4 * D
N_DEV = 8
EPS = 1e-6
NEG = -1e30
SCALE = HEAD ** -0.5
LR, B1, B2, ADAM_EPS, WD, STEP = 0.001, 0.9, 0.999, 1e-08, 0.01, 10
V7X_VMEM_LIMIT = 56 * 1024 * 1024
TM = 1024
MESH = pl.DeviceIdType.MESH
AXES = ("x", "y", "c")


def _cparams(*sem):
    if sem:
        return pltpu.CompilerParams(dimension_semantics=sem, vmem_limit_bytes=V7X_VMEM_LIMIT)
    return pltpu.CompilerParams(vmem_limit_bytes=V7X_VMEM_LIMIT)


def _nn(a, b):
    return jnp.dot(a, b, preferred_element_type=F32)


def _nt(a, b):
    return lax.dot_general(a, b, (((1,), (1,)), ((), ())), preferred_element_type=F32)


def _tn(a, b):
    return lax.dot_general(a, b, (((0,), (0,)), ((), ())), preferred_element_type=F32)


def _rms_r(x):
    return lax.rsqrt(jnp.mean(x * x, axis=-1, keepdims=True) + EPS)


def _rms_bwd(x, r, g, dn):
    gy = dn * g
    dx = r * gy - x * (r * r * r) * jnp.mean(x * gy, axis=-1, keepdims=True)
    return dx, dn * (x * r)


def _sigmoid(t):
    return 1.0 / (1.0 + jnp.exp(-t))


def _rowsum(v):
    return jnp.sum(v, axis=0, keepdims=True)


def _vec_spec(n=D):
    return pl.BlockSpec((1, n), lambda *_: (0, 0))


def _const_spec(shape):
    nd = len(shape)
    return pl.BlockSpec(shape, lambda *_: (0,) * nd)


def _win_rowblock(j):
    return jnp.where(j < 9, (j % 3) * 3 + j // 3, j)


def _proj(x, g, sc, sh, w_int):
    S = x.shape[0]
    tm = TM

    def body(x_ref, g_ref, sc_ref, sh_ref, w_ref, h_ref, q_ref, e_ref):
        j = pl.program_id(1)

        @pl.when(j == 0)
        def _():
            xv = x_ref[...]
            h = xv * _rms_r(xv) * g_ref[...] * (1.0 + sc_ref[...]) + sh_ref[...]
            h_ref[...] = h.astype(BF16)

        acc = _nt(h_ref[...], w_ref[...])

        @pl.when(j < 9)
        def _():
            q_ref[0] = acc

        @pl.when(j >= 9)
        def _():
            e_ref[0] = acc.astype(BF16)

    def e_idx(i, j):
        k = jnp.maximum(j - 9, 0)
        return (k // 2, i, k % 2)

    return pl.pallas_call(
        body, name="proj", grid=(S // tm, 19),
        in_specs=[pl.BlockSpec((tm, D), lambda i, j: (i, 0)), _vec_spec(), _vec_spec(), _vec_spec(),
                  pl.BlockSpec((512, D), lambda i, j: (_win_rowblock(j), 0))],
        out_specs=[pl.BlockSpec((tm, D), lambda i, j: (i, 0)),
                   pl.BlockSpec((1, tm, 512), lambda i, j: (jnp.minimum(j, 8), i, 0)),
                   pl.BlockSpec((1, tm, 512), e_idx)],
        out_shape=[jax.ShapeDtypeStruct((S, D), BF16), jax.ShapeDtypeStruct((9, S, 512), F32),
                   jax.ShapeDtypeStruct((5, S, D), BF16)],
        compiler_params=_cparams("parallel", "arbitrary"),
    )(x, g, sc, sh, w_int)


def _bias_table():
    slopes = (2.0 ** (-8.0 * np.arange(1, 13, dtype=np.float32) / 12.0)).astype(np.float32)
    qi = np.arange(HEAD)[:, None]
    kj = np.arange(2 * HEAD)[None, :]
    delta = HEAD + qi - kj
    mask = (delta >= 0) & (delta <= HEAD)
    out = np.zeros((3, N_SLOT, HEAD, 2 * HEAD), np.float32)
    for gi, d in enumerate(DILATIONS):
        for j in range(N_SLOT):
            bias = -slopes[gi * N_SLOT + j] * (delta * d).astype(np.float32)
            out[gi, j] = np.where(mask, bias, NEG)
    return jnp.asarray(out)


def _block_rows(b, d):
    r = b % d
    n = b // d
    st = n * (HEAD * d) + r
    stp = jnp.maximum(n - 1, 0) * (HEAD * d) + r
    return n, st, stp


def _attn_fwd(qkv, bias):
    S = qkv.shape[2]
    nblk = S // HEAD

    def body(qkv_ref, b_ref, o_ref, lse_ref, m_s, l_s, a_s):
        g = pl.program_id(1)

        @pl.when(g == 0)
        def _():
            m_s[...] = jnp.full_like(m_s, NEG)
            l_s[...] = jnp.zeros_like(l_s)
            a_s[...] = jnp.zeros_like(a_s)

        bias = b_ref[0, 0]
        col = lax.broadcasted_iota(jnp.int32, bias.shape, 1)
        bias_first = jnp.where(col < HEAD, NEG, bias)

        for gi, d in enumerate(DILATIONS):
            @pl.when(g == gi)
            def _(d=d):
                def step(b, carry):
                    n, st, stp = _block_rows(b, d)
                    cur = pl.ds(st, HEAD, stride=d)
                    prv = pl.ds(stp, HEAD, stride=d)
                    q = qkv_ref.at[0, 0][cur, :].astype(BF16)
                    kw = jnp.concatenate([qkv_ref.at[0, 1][prv, :], qkv_ref.at[0, 1][cur, :]], axis=0).astype(BF16)
                    vw = jnp.concatenate([qkv_ref.at[0, 2][prv, :], qkv_ref.at[0, 2][cur, :]], axis=0).astype(BF16)
                    s = _nt(q, kw) * SCALE + jnp.where(n > 0, bias, bias_first)
                    m_old = m_s[cur, :]
                    m_new = jnp.maximum(m_old, jnp.max(s, axis=-1, keepdims=True))
                    alpha = jnp.exp(m_old - m_new)
                    p = jnp.exp(s - m_new[:, :1])
                    l_s[cur, :] = alpha * l_s[cur, :] + jnp.sum(p, axis=-1, keepdims=True)
                    a_s[cur, :] = alpha * a_s[cur, :] + _nn(p.astype(BF16), vw)
                    m_s[cur, :] = m_new
                    return carry

                lax.fori_loop(0, nblk, step, 0, unroll=4)

        @pl.when(g == len(DILATIONS) - 1)
        def _():
            l = l_s[...]
            o_ref[...] = a_s[...] / l
            lse_ref[...] = m_s[...] + jnp.log(l)

    return pl.pallas_call(
        body, name="attn_fwd", grid=(N_SLOT, 3),
        in_specs=[pl.BlockSpec((1, 3, S, HEAD), lambda j, g: (g, 0, 0, j)),
                  pl.BlockSpec((1, 1, HEAD, 2 * HEAD), lambda j, g: (g, j, 0, 0))],
        out_specs=[pl.BlockSpec((S, HEAD), lambda j, g: (0, j)), pl.BlockSpec((S, HEAD), lambda j, g: (0, j))],
        out_shape=[jax.ShapeDtypeStruct((S, AOW), F32), jax.ShapeDtypeStruct((S, AOW), F32)],
        scratch_shapes=[pltpu.VMEM((S, HEAD), F32)] * 3,
        compiler_params=_cparams("parallel", "arbitrary"),
    )(qkv, bias)


def _shift_down(z, k, halo_rows):
    out = pltpu.roll(z, k, axis=0)
    rid = lax.broadcasted_iota(jnp.int32, z.shape, 0)
    for t in range(k):
        out = jnp.where(rid == t, halo_rows[t], out)
    return out


def _shift_up(z, k, halo_rows):
    n = z.shape[0]
    out = pltpu.roll(z, n - k, axis=0)
    rid = lax.broadcasted_iota(jnp.int32, z.shape, 0)
    for t in range(k):
        out = jnp.where(rid == n - k + t, halo_rows[t], out)
    return out


def _e_spec(chunk, tm):
    return pl.BlockSpec((1, tm, D), lambda i, c=chunk: (c, i, 0))


def _e_prev_spec(chunk, tm):
    return pl.BlockSpec((1, 16, D), lambda i, c=chunk: (c, jnp.maximum(i * (tm // 16) - 1, 0), 0))


def _e_next_spec(chunk, tm, S):
    return pl.BlockSpec((1, 16, D), lambda i, c=chunk: (c, jnp.minimum((i + 1) * (tm // 16), S // 16 - 1), 0))


def _mix(o_attn, e, cw8, ba, bb, w_bat, w_bc):
    S = o_attn.shape[0]
    tm = 256

    def body(o_ref, cb_ref, cc_ref, cx_ref, ga_ref, gb_ref, ccp_ref, cxp_ref, cw_ref, ba_ref, bb_ref, wba_ref, wbc_ref,
             obf_ref, cbu_ref, ya_ref, yc_ref, mg_ref):
        i = pl.program_id(0)
        o = o_ref[...].astype(BF16)
        obf_ref[...] = o
        ya = _nt(o, wba_ref[...])
        z = cc_ref[0].astype(F32) * cx_ref[0].astype(F32)
        zp = ccp_ref[0].astype(F32) * cxp_ref[0].astype(F32) * (i > 0).astype(F32)
        z1 = _shift_down(z, 1, [zp[15:16]])
        z2 = _shift_down(z, 2, [zp[14:15], zp[15:16]])
        cw = cw_ref[...]
        u = cw[0:1] * z2 + cw[1:2] * z1 + cw[2:3] * z
        cbu = (cb_ref[0].astype(F32) * u).astype(BF16)
        cbu_ref[...] = cbu
        yc = _nn(cbu, wbc_ref[...])
        sa = _sigmoid(ga_ref[0].astype(F32) + ba_ref[...])
        sb = _sigmoid(gb_ref[0].astype(F32) + bb_ref[...])
        ya_ref[...] = ya.astype(BF16)
        yc_ref[...] = yc.astype(BF16)
        mg_ref[...] = (sa * ya + sb * yc).astype(BF16)

    row = lambda w: pl.BlockSpec((tm, w), lambda i: (i, 0))
    return pl.pallas_call(
        body, name="mix", grid=(S // tm,),
        in_specs=[row(AOW)] + [_e_spec(c, tm) for c in range(5)] + [_e_prev_spec(1, tm), _e_prev_spec(2, tm),
                  _const_spec((8, D)), _vec_spec(), _vec_spec(), _const_spec((D, AOW)), _const_spec((D, D))],
        out_specs=[row(AOW), row(D), row(D), row(D), row(D)],
        out_shape=[jax.ShapeDtypeStruct((S, AOW), BF16)] + [jax.ShapeDtypeStruct((S, D), BF16)] * 4,
        compiler_params=_cparams("parallel"),
    )(o_attn, e, e, e, e, e, e, e, cw8, ba, bb, w_bat, w_bc)


def _out_proj(merged, w_out, x, gate1, g_mlp, sc2, sh2):
    S = x.shape[0]
    tm = TM

    def body(mg_ref, w_ref, x_ref, gt_ref, g_ref, sc_ref, sh_ref, x1_ref, mo_ref, h2_ref):
        mo = _nn(mg_ref[...], w_ref[...])
        mo_ref[...] = mo.astype(BF16)
        x1 = x_ref[...] + gt_ref[...] * mo
        x1_ref[...] = x1
        h2 = x1 * _rms_r(x1) * g_ref[...] * (1.0 + sc_ref[...]) + sh_ref[...]
        h2_ref[...] = h2.astype(BF16)

    row = pl.BlockSpec((tm, D), lambda i: (i, 0))
    return pl.pallas_call(
        body, name="out_proj", grid=(S // tm,),
        in_specs=[row, _const_spec((D, D)), row, _vec_spec(), _vec_spec(), _vec_spec(), _vec_spec()],
        out_specs=[row, row, row],
        out_shape=[jax.ShapeDtypeStruct((S, D), F32), jax.ShapeDtypeStruct((S, D), BF16), jax.ShapeDtypeStruct((S, D), BF16)],
        compiler_params=_cparams("parallel"),
    )(merged, w_out, x, gate1, g_mlp, sc2, sh2)


def _mlp_in(h2, w_mit):
    S = h2.shape[0]
    tm, tn = TM, 1024

    def body(h_ref, w_ref, a_ref, f_ref):
        a = _nt(h_ref[...], w_ref[...])
        a_ref[...] = a.astype(BF16)
        f_ref[...] = jnp.square(jnp.maximum(a, 0.0)).astype(BF16)

    blk = pl.BlockSpec((tm, tn), lambda i, j: (i, j))
    return pl.pallas_call(
        body, name="mlp_in", grid=(S // tm, DFF // tn),
        in_specs=[pl.BlockSpec((tm, D), lambda i, j: (i, 0)), pl.BlockSpec((tn, D), lambda i, j: (j, 0))],
        out_specs=[blk, blk],
        out_shape=[jax.ShapeDtypeStruct((S, DFF), BF16)] * 2,
        compiler_params=_cparams("parallel", "parallel"),
    )(h2, w_mit)


def _mlp_out(f, w_mo, x1, gate2, g_fin, tgt):
    S = x1.shape[0]
    tm, tk = TM, 1024
    nk = DFF // tk

    def body(f_ref, w_ref, x1_ref, gt_ref, g_ref, t_ref, mlp_ref, dx2_ref, pv_ref, acc):
        i, k = pl.program_id(0), pl.program_id(1)

        @pl.when((i == 0) & (k == 0))
        def _():
            pv_ref[...] = jnp.zeros_like(pv_ref)

        @pl.when(k == 0)
        def _():
            acc[...] = jnp.zeros_like(acc)

        acc[...] += _nn(f_ref[...], w_ref[...])

        @pl.when(k == nk - 1)
        def _():
            mlp = acc[...]
            mlp_ref[...] = mlp.astype(BF16)
            x2 = x1_ref[...] + gt_ref[...] * mlp
            r = _rms_r(x2)
            g = g_ref[...]
            err = x2 * r * g - t_ref[...]
            dy = err * (1.0 / D)
            dx2, pg = _rms_bwd(x2, r, g, dy)
            dx2_ref[...] = dx2
            pv_ref[0:1, :] += _rowsum(pg)
            pv_ref[1:2, :] += 0.5 * _rowsum(jnp.mean(err * err, axis=-1, keepdims=True))

    row = pl.BlockSpec((tm, D), lambda i, k: (i, 0))
    return pl.pallas_call(
        body, name="mlp_out", grid=(S // tm, nk),
        in_specs=[pl.BlockSpec((tm, tk), lambda i, k: (i, k)), pl.BlockSpec((tk, D), lambda i, k: (k, 0)),
                  row, _vec_spec(), _vec_spec(), row],
        out_specs=[row, row, _const_spec((8, D))],
        out_shape=[jax.ShapeDtypeStruct((S, D), BF16), jax.ShapeDtypeStruct((S, D), F32), jax.ShapeDtypeStruct((8, D), F32)],
        scratch_shapes=[pltpu.VMEM((tm, D), F32)],
        compiler_params=_cparams("arbitrary", "arbitrary"),
    )(f, w_mo, x1, gate2, g_fin, tgt)


def _bwd_mlp_a(dx2, gate2, mlp, w_mo, a):
    S = dx2.shape[0]
    tm, tn = TM, 1024

    def body(dx_ref, gt_ref, mlp_ref, w_ref, a_ref, da_ref, dmo_ref, pv_ref):
        i, j = pl.program_id(0), pl.program_id(1)

        @pl.when((i == 0) & (j == 0))
        def _():
            pv_ref[...] = jnp.zeros_like(pv_ref)

        @pl.when(j == 0)
        def _():
            dx = dx_ref[...]
            dmo_ref[...] = (dx * gt_ref[...]).astype(BF16)
            pv_ref[0:1, :] += _rowsum(dx * mlp_ref[...].astype(F32))

        df = _nt(dmo_ref[...], w_ref[...])
        da_ref[...] = (df * (2.0 * jnp.maximum(a_ref[...].astype(F32), 0.0))).astype(BF16)

    row = pl.BlockSpec((tm, D), lambda i, j: (i, 0))
    blk = pl.BlockSpec((tm, tn), lambda i, j: (i, j))
    return pl.pallas_call(
        body, name="bwd_mlp_a", grid=(S // tm, DFF // tn),
        in_specs=[row, _vec_spec(), row, pl.BlockSpec((tn, D), lambda i, j: (j, 0)), blk],
        out_specs=[blk, row, _const_spec((8, D))],
        out_shape=[jax.ShapeDtypeStruct((S, DFF), BF16), jax.ShapeDtypeStruct((S, D), BF16), jax.ShapeDtypeStruct((8, D), F32)],
        compiler_params=_cparams("arbitrary", "arbitrary"),
    )(dx2, gate2, mlp, w_mo, a)


def _bwd_mlp_b(da, w_mit, x1, dx2, g_mlp, sc2):
    S = x1.shape[0]
    tm, tk = TM, 1024
    nk = DFF // tk

    def body(da_ref, w_ref, x1_ref, dx2_ref, g_ref, sc_ref, dx1_ref, pv_ref, acc):
        i, k = pl.program_id(0), pl.program_id(1)

        @pl.when((i == 0) & (k == 0))
        def _():
            pv_ref[...] = jnp.zeros_like(pv_ref)

        @pl.when(k == 0)
        def _():
            acc[...] = jnp.zeros_like(acc)

        acc[...] += _nn(da_ref[...], w_ref[...])

        @pl.when(k == nk - 1)
        def _():
            dh = acc[...]
            x1 = x1_ref[...]
            r = _rms_r(x1)
            g = g_ref[...]
            dxn, pg = _rms_bwd(x1, r, g, dh * (1.0 + sc_ref[...]))
            dx1_ref[...] = dx2_ref[...] + dxn
            pv_ref[0:1, :] += _rowsum(dh)
            pv_ref[1:2, :] += _rowsum(dh * (x1 * r * g))
            pv_ref[2:3, :] += _rowsum(pg)

    row = pl.BlockSpec((tm, D), lambda i, k: (i, 0))
    return pl.pallas_call(
        body, name="bwd_mlp_b", grid=(S // tm, nk),
        in_specs=[pl.BlockSpec((tm, tk), lambda i, k: (i, k)), pl.BlockSpec((tk, D), lambda i, k: (k, 0)),
                  row, row, _vec_spec(), _vec_spec()],
        out_specs=[row, _const_spec((8, D))],
        out_shape=[jax.ShapeDtypeStruct((S, D), F32), jax.ShapeDtypeStruct((8, D), F32)],
        scratch_shapes=[pltpu.VMEM((tm, D), F32)],
        compiler_params=_cparams("arbitrary", "arbitrary"),
    )(da, w_mit, x1, dx2, g_mlp, sc2)


def _bwd_mix(dx1, gate1, mo, e, cw8, ba, bb, ya, yc, o_attn, w_out, w_bc, w_bat):
    S = dx1.shape[0]
    tm = 256
    n_tiles = S // tm

    def body(dx_ref, dxn_ref, gt_ref, mo_ref, cb_ref, cc_ref, cx_ref, ga_ref, gb_ref, cbn_ref, gbn_ref, ccp_ref, cxp_ref,
             cw_ref, ba_ref, bb_ref, ya_ref, yc_ref, o_ref, wout_ref, wbc_ref, wba_ref,
             dmo_ref, dya_ref, dyc_ref, do_ref, dl_ref, de_ref, pv_ref):
        i = pl.program_id(0)

        @pl.when(i == 0)
        def _():
            pv_ref[...] = jnp.zeros_like(pv_ref)

        gate = gt_ref[...]
        bbv = bb_ref[...]

        def conv_branch_grad(dx_rows, gb_rows):
            dmo = (dx_rows * gate).astype(BF16)
            dmg = _nt(dmo, wout_ref[...])
            sb = _sigmoid(gb_rows + bbv)
            dyc = dmg * sb
            return dmo, dmg, sb, dyc, _nt(dyc.astype(BF16), wbc_ref[...])

        dx = dx_ref[...]
        cb = cb_ref[0].astype(F32)
        cc = cc_ref[0].astype(F32)
        cx = cx_ref[0].astype(F32)
        dmo, dmg, sb, dyc, dcbu = conv_branch_grad(dx, gb_ref[0].astype(F32))
        dmo_ref[...] = dmo
        pv_ref[0:1, :] += _rowsum(dx * mo_ref[...].astype(F32))
        sa = _sigmoid(ga_ref[0].astype(F32) + ba_ref[...])
        dya = (dmg * sa).astype(BF16)
        dya_ref[...] = dya
        dyc_ref[...] = dyc.astype(BF16)
        dga = dmg * ya_ref[...].astype(F32) * sa * (1.0 - sa)
        dgb = dmg * yc_ref[...].astype(F32) * sb * (1.0 - sb)
        pv_ref[1:2, :] += _rowsum(dga)
        pv_ref[2:3, :] += _rowsum(dgb)

        do = _nn(dya, wba_ref[...])
        do_ref[...] = do
        prod = do * o_ref[...]
        dl_ref[...] = jnp.concatenate(
            [jnp.broadcast_to(jnp.sum(prod[:, s * HEAD:(s + 1) * HEAD], axis=-1, keepdims=True), (tm, HEAD))
             for s in range(N_SLOT)], axis=1)

        z = cc * cx
        zp = ccp_ref[0].astype(F32) * cxp_ref[0].astype(F32) * (i > 0).astype(F32)
        z1 = _shift_down(z, 1, [zp[15:16]])
        z2 = _shift_down(z, 2, [zp[14:15], zp[15:16]])
        cw = cw_ref[...]
        u = cw[0:1] * z2 + cw[1:2] * z1 + cw[2:3] * z
        du = dcbu * cb
        dcbu_n = conv_branch_grad(dxn_ref[...], gbn_ref[0].astype(F32))[4]
        du_n = dcbu_n * cbn_ref[0].astype(F32) * (i < n_tiles - 1).astype(F32)
        du1 = _shift_up(du, 1, [du_n[0:1]])
        du2 = _shift_up(du, 2, [du_n[0:1], du_n[1:2]])
        dz = cw[2:3] * du + cw[1:2] * du1 + cw[0:1] * du2
        pv_ref[3:4, :] += _rowsum(du * z2)
        pv_ref[4:5, :] += _rowsum(du * z1)
        pv_ref[5:6, :] += _rowsum(du * z)

        de_ref[0] = (dcbu * u).astype(BF16)
        de_ref[1] = (dz * cx).astype(BF16)
        de_ref[2] = (dz * cc).astype(BF16)
        de_ref[3] = dga.astype(BF16)
        de_ref[4] = dgb.astype(BF16)

    row = lambda w: pl.BlockSpec((tm, w), lambda i: (i, 0))
    nxt = pl.BlockSpec((16, D), lambda i: (jnp.minimum((i + 1) * (tm // 16), S // 16 - 1), 0))
    return pl.pallas_call(
        body, name="bwd_mix", grid=(n_tiles,),
        in_specs=[row(D), nxt, _vec_spec(), row(D)] + [_e_spec(c, tm) for c in range(5)]
                 + [_e_next_spec(0, tm, S), _e_next_spec(4, tm, S), _e_prev_spec(1, tm), _e_prev_spec(2, tm),
                    _const_spec((8, D)), _vec_spec(), _vec_spec(), row(D), row(D), row(AOW),
                    _const_spec((D, D)), _const_spec((D, D)), _const_spec((D, AOW))],
        out_specs=[row(D), row(D), row(D), row(AOW), row(AOW), pl.BlockSpec((5, tm, D), lambda i: (0, i, 0)),
                   _const_spec((8, D))],
        out_shape=[jax.ShapeDtypeStruct((S, D), BF16)] * 3 + [jax.ShapeDtypeStruct((S, AOW), F32)] * 2
                  + [jax.ShapeDtypeStruct((5, S, D), BF16), jax.ShapeDtypeStruct((8, D), F32)],
        compiler_params=_cparams("arbitrary"),
    )(dx1, dx1, gate1, mo, e, e, e, e, e, e, e, e, e, cw8, ba, bb, ya, yc, o_attn, w_out, w_bc, w_bat)


def _attn_bwd(qkv, do, lse, dl, bias):
    S = qkv.shape[2]
    nblk = S // HEAD

    def body(qkv_ref, do_ref, lse_ref, dl_ref, b_ref, d_ref):
        g = pl.program_id(1)
        d_ref[...] = jnp.zeros_like(d_ref)
        bias = b_ref[0, 0]
        col = lax.broadcasted_iota(jnp.int32, bias.shape, 1)
        bias_first = jnp.where(col < HEAD, NEG, bias)

        for gi, d in enumerate(DILATIONS):
            @pl.when(g == gi)
            def _(d=d):
                def step(b, carry):
                    n, st, stp = _block_rows(b, d)
                    cur = pl.ds(st, HEAD, stride=d)
                    prv = pl.ds(stp, HEAD, stride=d)
                    q = qkv_ref.at[0, 0][cur, :].astype(BF16)
                    kw = jnp.concatenate([qkv_ref.at[0, 1][prv, :], qkv_ref.at[0, 1][cur, :]], axis=0).astype(BF16)
                    vw = jnp.concatenate([qkv_ref.at[0, 2][prv, :], qkv_ref.at[0, 2][cur, :]], axis=0).astype(BF16)
                    s = _nt(q, kw) * SCALE + jnp.where(n > 0, bias, bias_first)
                    p = jnp.exp(s - lse_ref[cur, :][:, :1])
                    dob = do_ref[cur, :].astype(BF16)
                    dvw = _tn(p.astype(BF16), dob)
                    dp = _nt(dob, vw)
                    ds = (p * (dp - dl_ref[cur, :][:, :1]) * SCALE).astype(BF16)
                    d_ref.at[0, 0][cur, :] = _nn(ds, kw)
                    dkw = _tn(ds, q)
                    d_ref.at[0, 1][cur, :] += dkw[HEAD:]
                    d_ref.at[0, 1][prv, :] += dkw[:HEAD]
                    d_ref.at[0, 2][cur, :] += dvw[HEAD:]
                    d_ref.at[0, 2][prv, :] += dvw[:HEAD]
                    return carry

                lax.fori_loop(0, nblk, step, 0, unroll=4)

    col_blk = pl.BlockSpec((S, HEAD), lambda j, g: (0, j))
    qkv_blk = pl.BlockSpec((1, 3, S, HEAD), lambda j, g: (g, 0, 0, j))
    return pl.pallas_call(
        body, name="attn_bwd", grid=(N_SLOT, 3),
        in_specs=[qkv_blk, col_blk, col_blk, col_blk, pl.BlockSpec((1, 1, HEAD, 2 * HEAD), lambda j, g: (g, j, 0, 0))],
        out_specs=qkv_blk,
        out_shape=jax.ShapeDtypeStruct((3, 3, S, AOW), F32),
        compiler_params=_cparams("parallel", "arbitrary"),
    )(qkv, do, lse, dl, bias)


def _bwd_in(dqkv, de, w_int, x, dx1, g_mix, sc1):
    S = x.shape[0]
    tm = TM

    def body(dq_ref, de_ref, w_ref, x_ref, dx1_ref, g_ref, sc_ref, gx_ref, pv_ref, acc):
        i, k = pl.program_id(0), pl.program_id(1)

        @pl.when((i == 0) & (k == 0))
        def _():
            pv_ref[...] = jnp.zeros_like(pv_ref)

        @pl.when(k == 0)
        def _():
            acc[...] = jnp.zeros_like(acc)

        @pl.when(k < 9)
        def _():
            acc[...] += _nn(dq_ref[0].astype(BF16), w_ref[...])

        @pl.when(k >= 9)
        def _():
            acc[...] += _nn(de_ref[0], w_ref[...])

        @pl.when(k == 18)
        def _():
            dh = acc[...]
            xv = x_ref[...]
            r = _rms_r(xv)
            g = g_ref[...]
            dxn, pg = _rms_bwd(xv, r, g, dh * (1.0 + sc_ref[...]))
            gx_ref[...] = dx1_ref[...] + dxn
            pv_ref[0:1, :] += _rowsum(dh)
            pv_ref[1:2, :] += _rowsum(dh * (xv * r * g))
            pv_ref[2:3, :] += _rowsum(pg)

    def e_idx(i, k):
        kk = jnp.maximum(k - 9, 0)
        return (kk // 2, i, kk % 2)

    row = pl.BlockSpec((tm, D), lambda i, k: (i, 0))
    return pl.pallas_call(
        body, name="bwd_in", grid=(S // tm, 19),
        in_specs=[pl.BlockSpec((1, tm, 512), lambda i, k: (jnp.minimum(k, 8), i, 0)), pl.BlockSpec((1, tm, 512), e_idx),
                  pl.BlockSpec((512, D), lambda i, k: (_win_rowblock(k), 0)), row, row, _vec_spec(), _vec_spec()],
        out_specs=[row, _const_spec((8, D))],
        out_shape=[jax.ShapeDtypeStruct((S, D), F32), jax.ShapeDtypeStruct((8, D), F32)],
        scratch_shapes=[pltpu.VMEM((tm, D), F32)],
        compiler_params=_cparams("arbitrary", "arbitrary"),
    )(dqkv, de, w_int, x, dx1, g_mix, sc1)


def _grad_w(name, a, b):
    S, ka = a.shape
    nb = b.shape[1]
    tt = 1024
    nt = S // tt

    def body(a_ref, b_ref, o_ref, acc):
        t = pl.program_id(1)

        @pl.when(t == 0)
        def _():
            acc[...] = jnp.zeros_like(acc)

        acc[...] += _tn(a_ref[...], b_ref[...])

        @pl.when(t == nt - 1)
        def _():
            o_ref[...] = acc[...].astype(BF16)

    return pl.pallas_call(
        body, name=name, grid=(ka // 512, nt),
        in_specs=[pl.BlockSpec((tt, 512), lambda n, t: (t, n)), pl.BlockSpec((tt, nb), lambda n, t: (t, 0))],
        out_specs=pl.BlockSpec((512, nb), lambda n, t: (n, 0)),
        out_shape=jax.ShapeDtypeStruct((ka, nb), BF16),
        scratch_shapes=[pltpu.VMEM((512, nb), F32)],
        compiler_params=_cparams("parallel", "arbitrary"),
    )(a, b)


def _grad_w_in(dqkv, de, h):
    S = h.shape[0]
    tt = 1024
    nt = S // tt

    def body(dq_ref, de_ref, h_ref, o_ref, acc):
        n, t = pl.program_id(0), pl.program_id(1)

        @pl.when(t == 0)
        def _():
            acc[...] = jnp.zeros_like(acc)

        @pl.when(n < 9)
        def _():
            acc[...] += _tn(dq_ref[0].astype(BF16), h_ref[...])

        @pl.when(n >= 9)
        def _():
            acc[...] += _tn(de_ref[0], h_ref[...])

        @pl.when(t == nt - 1)
        def _():
            o_ref[...] = acc[...].astype(BF16)

    def e_idx(n, t):
        kk = jnp.maximum(n - 9, 0)
        return (kk // 2, jnp.where(n >= 9, t, 0), kk % 2)

    return pl.pallas_call(
        body, name="grad_w_in", grid=(19, nt),
        in_specs=[pl.BlockSpec((1, tt, 512), lambda n, t: (jnp.minimum(n, 8), jnp.where(n < 9, t, nt - 1), 0)),
                  pl.BlockSpec((1, tt, 512), e_idx), pl.BlockSpec((tt, D), lambda n, t: (t, 0))],
        out_specs=pl.BlockSpec((512, D), lambda n, t: (_win_rowblock(n), 0)),
        out_shape=jax.ShapeDtypeStruct((19 * 512, D), BF16),
        scratch_shapes=[pltpu.VMEM((512, D), F32)],
        compiler_params=_cparams("parallel", "arbitrary"),
    )(dqkv, de, h)


def _local_step(x, tgt, mod, g_mix, g_mlp, g_fin, ba, bb, cw8, w_int, w_bat, w_bc, w_out, w_mit, w_mo):
    S = x.shape[0]
    sh1, sc1, gt1, sh2, sc2, gt2 = [mod[k:k + 1] for k in range(6)]
    bias = _bias_table()

    h, qkv, e = _proj(x, g_mix, sc1, sh1, w_int)
    qkv = qkv.reshape(3, 3, S, AOW)
    o_attn, lse = _attn_fwd(qkv, bias)
    o_bf, cbu, ya, yc, merged = _mix(o_attn, e, cw8, ba, bb, w_bat, w_bc)
    x1, mo, h2 = _out_proj(merged, w_out, x, gt1, g_mlp, sc2, sh2)
    a, f = _mlp_in(h2, w_mit)
    mlp, dx2, pv_f = _mlp_out(f, w_mo, x1, gt2, g_fin, tgt)

    da, dmo2, pv_a = _bwd_mlp_a(dx2, gt2, mlp, w_mo, a)
    dx1, pv_b = _bwd_mlp_b(da, w_mit, x1, dx2, g_mlp, sc2)
    dmo, dya, dyc, do, dl, de, pv_m = _bwd_mix(dx1, gt1, mo, e, cw8, ba, bb, ya, yc, o_attn, w_out, w_bc, w_bat)
    dqkv = _attn_bwd(qkv, do, lse, dl, bias).reshape(9, S, AOW)
    grad_x, pv_i = _bwd_in(dqkv, de, w_int, x, dx1, g_mix, sc1)

    grads = [
        _grad_w_in(dqkv, de, h),
        _grad_w("grad_w_ba", dya, o_bf),
        _grad_w("grad_w_bc", cbu, dyc),
        _grad_w("grad_w_out", merged, dmo),
        _grad_w("grad_w_mi", da, h2),
        _grad_w("grad_w_mo", f, dmo2),
    ]
    vec = jnp.concatenate([pv_i[0:2], pv_m[0:1], pv_b[0:2], pv_a[0:1], pv_i[2:3], pv_b[2:3], pv_f[0:1],
                           pv_m[1:3], pv_m[3:6], jnp.zeros((2, D), F32)], axis=0)
    return pv_f[1, 0], grad_x, vec, grads


def _my_place():
    return lax.axis_index("x"), lax.axis_index("y"), lax.axis_index("c")


def _dev_index(px, py, pc):
    return 4 * px + 2 * py + pc


def _allgather_weights(shards):
    nw = len(shards)
    HBM = pl.BlockSpec(memory_space=pl.ANY)

    def body(*refs):
        sh, full = refs[:nw], refs[nw:2 * nw]
        send_sems, recv_sems, local_sems = refs[2 * nw:]
        x, y, c = _my_place()
        me, sibling = (x, y, c), (x, y, 1 - c)
        chips = [(1 - x, y), (x, 1 - y), (1 - x, 1 - y)]

        def rows(w, px, py, pc):
            r = sh[w].shape[0]
            return full[w].at[pl.ds(pl.multiple_of(_dev_index(px, py, pc) * r, 16), r), :]

        def copy(w, k, block, to, src=None):
            return pltpu.make_async_remote_copy(
                src_ref=rows(w, *block) if src is None else src, dst_ref=rows(w, *block),
                send_sem=send_sems.at[w, k], recv_sem=recv_sems.at[w, k], device_id=to, device_id_type=MESH)

        mine = [pltpu.make_async_copy(sh[w], rows(w, *me), local_sems.at[w]) for w in range(nw)]
        for cp in mine:
            cp.start()
        first = []
        for w in range(nw):
            first.append(copy(w, 0, me, sibling, src=sh[w]))
            first += [copy(w, 1 + j, me, (*chip, c), src=sh[w]) for j, chip in enumerate(chips)]
        for cp in first:
            cp.start()
        passed = []
        for w in range(nw):
            for j, chip in enumerate(chips):
                copy(w, 1 + j, (*chip, c), me).wait_recv()
                fwd = copy(w, 4 + j, (*chip, c), sibling)
                fwd.start()
                passed.append(fwd)
        for w in range(nw):
            copy(w, 0, sibling, me).wait_recv()
            for j, chip in enumerate(chips):
                copy(w, 4 + j, (*chip, 1 - c), me).wait_recv()
        for cp in first + passed:
            cp.wait_send()
        for cp in mine:
            cp.wait()

    return pl.pallas_call(
        body, name="allgather_weights",
        out_shape=[jax.ShapeDtypeStruct((N_DEV * s.shape[0], s.shape[1]), s.dtype) for s in shards],
        in_specs=[HBM] * nw, out_specs=[HBM] * nw,
        scratch_shapes=[pltpu.SemaphoreType.DMA((nw, 7)), pltpu.SemaphoreType.DMA((nw, 7)), pltpu.SemaphoreType.DMA((nw,))],
    )(*shards)


def _peer(x, y, c, m):
    return (x ^ ((m >> 2) & 1), y ^ ((m >> 1) & 1), c ^ (m & 1))


def _exchange_partials(grads):
    nw = len(grads)
    HBM = pl.BlockSpec(memory_space=pl.ANY)

    def body(*refs):
        g, parts = refs[:nw], refs[nw:2 * nw]
        send_sems, recv_sems, local_sems = refs[2 * nw:]
        x, y, c = _my_place()
        my_idx = _dev_index(x, y, c)

        def rows(w, idx):
            r = parts[w].shape[1]
            return g[w].at[pl.ds(pl.multiple_of(idx * r, 16), r), :]

        def copy(w, m):
            peer = _peer(x, y, c, m)
            return pltpu.make_async_remote_copy(
                src_ref=rows(w, _dev_index(*peer)), dst_ref=parts[w].at[my_idx],
                send_sem=send_sems.at[w, m - 1], recv_sem=recv_sems.at[w, m - 1], device_id=peer, device_id_type=MESH)

        def arrival(w, m):
            peer = _peer(x, y, c, m)
            return pltpu.make_async_remote_copy(
                src_ref=rows(w, my_idx), dst_ref=parts[w].at[_dev_index(*peer)],
                send_sem=send_sems.at[w, m - 1], recv_sem=recv_sems.at[w, m - 1], device_id=peer, device_id_type=MESH)

        mine = [pltpu.make_async_copy(rows(w, my_idx), parts[w].at[my_idx], local_sems.at[w]) for w in range(nw)]
        for cp in mine:
            cp.start()
        sends = [copy(w, m) for w in range(nw) for m in range(1, N_DEV)]
        for cp in sends:
            cp.start()
        for w in range(nw):
            for m in range(1, N_DEV):
                arrival(w, m).wait_recv()
        for cp in sends:
            cp.wait_send()
        for cp in mine:
            cp.wait()

    return pl.pallas_call(
        body, name="exchange_partials",
        out_shape=[jax.ShapeDtypeStruct((N_DEV, a.shape[0] // N_DEV, a.shape[1]), a.dtype) for a in grads],
        in_specs=[HBM] * nw, out_specs=[HBM] * nw,
        scratch_shapes=[pltpu.SemaphoreType.DMA((nw, 7)), pltpu.SemaphoreType.DMA((nw, 7)), pltpu.SemaphoreType.DMA((nw,))],
    )(*grads)


def _allgather_small(v, name):
    r, ccols = v.shape

    def body(v_ref, out_ref, send_sems, recv_sems):
        x, y, c = _my_place()
        my_idx = _dev_index(x, y, c)
        out_ref[my_idx] = v_ref[...]

        def copy(m):
            peer = _peer(x, y, c, m)
            return pltpu.make_async_remote_copy(
                src_ref=v_ref, dst_ref=out_ref.at[my_idx],
                send_sem=send_sems.at[m - 1], recv_sem=recv_sems.at[m - 1], device_id=peer, device_id_type=MESH)

        def arrival(m):
            peer = _peer(x, y, c, m)
            return pltpu.make_async_remote_copy(
                src_ref=v_ref, dst_ref=out_ref.at[_dev_index(*peer)],
                send_sem=send_sems.at[m - 1], recv_sem=recv_sems.at[m - 1], device_id=peer, device_id_type=MESH)

        sends = [copy(m) for m in range(1, N_DEV)]
        for cp in sends:
            cp.start()
        for m in range(1, N_DEV):
            arrival(m).wait_recv()
        for cp in sends:
            cp.wait_send()

    return pl.pallas_call(
        body, name=name,
        out_shape=jax.ShapeDtypeStruct((N_DEV, r, ccols), v.dtype),
        in_specs=[pl.BlockSpec(memory_space=pltpu.VMEM)], out_specs=pl.BlockSpec(memory_space=pltpu.VMEM),
        scratch_shapes=[pltpu.SemaphoreType.DMA((7,)), pltpu.SemaphoreType.DMA((7,))],
    )(v)


def _ada_fwd(c_all, w_ada, b_cols):
    def body(c_ref, w_ref, b_ref, mod_ref, act_ref):
        cv = c_ref[...]
        act = cv * _sigmoid(cv)
        act_ref[...] = act
        mod_ref[...] = jnp.dot(act, w_ref[...], preferred_element_type=F32, precision=lax.Precision.HIGHEST) + b_ref[...]

    return pl.pallas_call(
        body, name="ada_fwd",
        out_shape=[jax.ShapeDtypeStruct((N_DEV, w_ada.shape[1]), F32), jax.ShapeDtypeStruct((N_DEV, D), F32)],
        compiler_params=_cparams(),
    )(c_all, w_ada, b_cols)


def _ada_bwd(act_t, gm_cols):
    def body(a_ref, g_ref, o_ref):
        o_ref[...] = jnp.dot(a_ref[...], g_ref[...], preferred_element_type=F32, precision=lax.Precision.HIGHEST)

    return pl.pallas_call(
        body, name="ada_bwd", out_shape=jax.ShapeDtypeStruct((D, gm_cols.shape[1]), F32), compiler_params=_cparams(),
    )(act_t, gm_cols)


def _row_tile(r):
    for t in (256, 304, 128, 64, 16):
        if r % t == 0:
            return t
    return r


def _sum_parts(parts, name):
    _, r, ccols = parts.shape
    tr = _row_tile(r)

    def body(p_ref, o_ref):
        acc = p_ref[0].astype(F32)
        for s in range(1, N_DEV):
            acc = acc + p_ref[s].astype(F32)
        o_ref[...] = acc

    return pl.pallas_call(
        body, name=name, grid=(r // tr,),
        in_specs=[pl.BlockSpec((N_DEV, tr, ccols), lambda i: (0, i, 0))],
        out_specs=pl.BlockSpec((tr, ccols), lambda i: (i, 0)),
        out_shape=jax.ShapeDtypeStruct((r, ccols), F32),
        compiler_params=_cparams("parallel"),
    )(parts)


def _adamw(w, g, m, v, name):
    r, ccols = w.shape
    tr = _row_tile(r)
    c1 = 1.0 / (1.0 - B1 ** STEP)
    c2 = 1.0 / (1.0 - B2 ** STEP)

    def body(w_ref, g_ref, m_ref, v_ref, d_ref, nm_ref, nv_ref):
        gv = g_ref[...]
        nm = B1 * m_ref[...] + (1.0 - B1) * gv
        nv = B2 * v_ref[...] + (1.0 - B2) * jnp.square(gv)
        nm_ref[...] = nm
        nv_ref[...] = nv
        d_ref[...] = -LR * ((nm * c1) / (jnp.sqrt(nv * c2) + ADAM_EPS) + WD * w_ref[...])

    blk = pl.BlockSpec((tr, ccols), lambda i: (i, 0))
    return pl.pallas_call(
        body, name=name, grid=(r // tr,), in_specs=[blk] * 4, out_specs=[blk] * 3,
        out_shape=[jax.ShapeDtypeStruct((r, ccols), F32)] * 3,
        compiler_params=_cparams("parallel"),
    )(w, g, m, v)


def _pack_vectors(b_ada, g_mix, g_mlp, g_fin, b_gate, conv_w):
    conv_rows = jnp.pad(conv_w.reshape(3, HEAD), ((0, 0), (0, D - HEAD)))
    return jnp.concatenate([b_ada.reshape(6, D), g_mix.reshape(1, D), g_mlp.reshape(1, D), g_fin.reshape(1, D),
                            b_gate.reshape(2, D), conv_rows, jnp.zeros((2, D), F32)], axis=0)


def _unpack_vectors(p):
    return (p[0:6].reshape(1, 6 * D), p[6:7], p[9:11].reshape(1, 2 * D), p[11:14, :HEAD].reshape(1, 3, HEAD),
            p[7:8], p[8])


def kernel(x, c, w_ada, b_ada, g_norm_mix, w_in, b_gate, conv_w, w_branch_attn, w_branch_conv, w_out, g_norm_mlp, w_mlp_in, w_mlp_out, g_norm_final, loss_target, m_w_ada, m_b_ada, m_g_norm_mix, m_w_in, m_b_gate, m_conv_w, m_w_branch_attn, m_w_branch_conv, m_w_out, m_g_norm_mlp, m_w_mlp_in, m_w_mlp_out, m_g_norm_final, v_w_ada, v_b_ada, v_g_norm_mix, v_w_in, v_b_gate, v_conv_w, v_w_branch_attn, v_w_branch_conv, v_w_out, v_g_norm_mlp, v_w_mlp_in, v_w_mlp_out, v_g_norm_final):
    S = x.shape[1]
    xi, yi, ci = _my_place()
    me = _dev_index(xi, yi, ci)
    x2 = x.reshape(S, D)
    tgt = loss_target.reshape(S, D)

    shards = [w_in[0].T.astype(BF16), w_branch_attn[0].T.astype(BF16), w_branch_conv[0].astype(BF16),
              w_out[0].astype(BF16), w_mlp_in[0].T.astype(BF16), w_mlp_out[0].astype(BF16)]
    w_int, w_bat, w_bc, w_o, w_mit, w_mo = _allgather_weights(shards)

    pay = jnp.zeros((8, D), F32).at[0].set(c[0]).at[1:4, :HEAD].set(conv_w[0])
    got = _allgather_small(pay, "gather_cond")
    c_all = got[:, 0, :]
    cw8 = jnp.pad(got[:, 1:4, :HEAD].transpose(1, 0, 2).reshape(3, D), ((0, 5), (0, 0)))
    ncol = w_ada.shape[2]
    b_cols = lax.dynamic_slice(b_ada, (0, me * ncol), (1, ncol))
    mod_cols, act = _ada_fwd(c_all, w_ada[0], b_cols)
    mod_all = _allgather_small(mod_cols, "gather_mod")
    mod = lax.dynamic_index_in_dim(mod_all, me, axis=1, keepdims=False).reshape(6, D)

    ba, bb = b_gate[:, :D], b_gate[:, D:]
    loss_part, grad_x, vec, grads = _local_step(
        x2, tgt, mod, g_norm_mix, g_norm_mlp, g_norm_final.reshape(1, D), ba, bb, cw8, w_int, w_bat, w_bc, w_o, w_mit, w_mo)
    loss = lax.psum(loss_part, AXES)

    vec_all = _allgather_small(vec, "gather_vec")
    vec_sum = _sum_parts(vec_all, "sum_vec")
    gm_all = vec_all[:, 0:6, :].reshape(N_DEV, 6 * D)
    gm_cols = lax.dynamic_slice(gm_all, (0, me * ncol), (N_DEV, ncol))
    g_w_ada = _ada_bwd(act.T, gm_cols)
    conv_cols = lax.dynamic_slice(vec_sum[11:14], (0, me * HEAD), (3, HEAD))
    g_pack = jnp.concatenate([vec_sum[0:11], jnp.pad(conv_cols, ((0, 0), (0, D - HEAD))), jnp.zeros((2, D), F32)], axis=0)
    packs = [_pack_vectors(*t) for t in ((b_ada, g_norm_mix, g_norm_mlp, g_norm_final, b_gate, conv_w),
                                         (m_b_ada, m_g_norm_mix, m_g_norm_mlp, m_g_norm_final, m_b_gate, m_conv_w),
                                         (v_b_ada, v_g_norm_mix, v_g_norm_mlp, v_g_norm_final, v_b_gate, v_conv_w))]
    d_pack, m_pack, v_pack = _adamw(packs[0], g_pack, packs[1], packs[2], "adamw_vectors")
    d_ada, nm_ada, nv_ada = _adamw(w_ada[0], g_w_ada, m_w_ada[0], v_w_ada[0], "adamw_w_ada")

    parts = _exchange_partials(grads)
    names = ("w_in", "w_ba", "w_bc", "w_out", "w_mi", "w_mo")
    sums = [_sum_parts(p, "sum_" + n) for p, n in zip(parts, names)]
    g_in, g_ba, g_bc, g_out, g_mi, g_mo = sums[0].T, sums[1].T, sums[2], sums[3], sums[4].T, sums[5]
    big = {}
    for n, w, g, m, v in (("w_in", w_in, g_in, m_w_in, v_w_in), ("w_ba", w_branch_attn, g_ba, m_w_branch_attn, v_w_branch_attn),
                          ("w_bc", w_branch_conv, g_bc, m_w_branch_conv, v_w_branch_conv), ("w_out", w_out, g_out, m_w_out, v_w_out),
                          ("w_mi", w_mlp_in, g_mi, m_w_mlp_in, v_w_mlp_in), ("w_mo", w_mlp_out, g_mo, m_w_mlp_out, v_w_mlp_out)):
        big[n] = (g[None],) + tuple(t[None] for t in _adamw(w[0], g, m[0], v[0], "adamw_" + n))

    gv = _unpack_vectors(g_pack)
    dv = _unpack_vectors(d_pack)
    mv = _unpack_vectors(m_pack)
    vv = _unpack_vectors(v_pack)

    def ordered(k, ada, vecs):
        return (ada[None], vecs[0], vecs[1], big["w_in"][k], vecs[2], vecs[3], big["w_ba"][k], big["w_bc"][k],
                big["w_out"][k], vecs[4], big["w_mi"][k], big["w_mo"][k], vecs[5])

    return (loss, grad_x.reshape(1, S, D), *ordered(0, g_w_ada, gv), *ordered(1, d_ada, dv),
            *ordered(2, nm_ada, mv), *ordered(3, nv_ada, vv))
```

```python
import functools

import numpy as np
import jax
import jax.numpy as jnp
from jax import lax
from jax.experimental import pallas as pl
from jax.experimental.pallas import tpu as pltpu

F32, BF16 = jnp.float32, jnp.bfloat16
D = 1024
HEAD = 128
DILATIONS = (1, 4, 16)
N_SLOT = 4
AOW = N_SLOT * HEAD
DFF = 4 * D
N_DEV = 8
EPS = 1e-6
NEG = -1e30
SCALE = HEAD ** -0.5
LR, B1, B2, ADAM_EPS, WD, STEP = 0.001, 0.9, 0.999, 1e-08, 0.01, 10
V7X_VMEM_LIMIT = 56 * 1024 * 1024
TM = 1024
MESH = pl.DeviceIdType.MESH
AXES = ("x", "y", "c")


def _cparams(*sem):
    if sem:
        return pltpu.CompilerParams(dimension_semantics=sem, vmem_limit_bytes=V7X_VMEM_LIMIT)
    return pltpu.CompilerParams(vmem_limit_bytes=V7X_VMEM_LIMIT)


def _nn(a, b):
    return jnp.dot(a, b, preferred_element_type=F32)


def _nt(a, b):
    return lax.dot_general(a, b, (((1,), (1,)), ((), ())), preferred_element_type=F32)


def _tn(a, b):
    return lax.dot_general(a, b, (((0,), (0,)), ((), ())), preferred_element_type=F32)


def _rms_r(x):
    return lax.rsqrt(jnp.mean(x * x, axis=-1, keepdims=True) + EPS)


def _rms_bwd(x, r, g, dn):
    gy = dn * g
    dx = r * gy - x * (r * r * r) * jnp.mean(x * gy, axis=-1, keepdims=True)
    return dx, dn * (x * r)


def _sigmoid(t):
    return 1.0 / (1.0 + jnp.exp(-t))


def _rowsum(v):
    return jnp.sum(v, axis=0, keepdims=True)


def _vec_spec(n=D):
    return pl.BlockSpec((1, n), lambda *_: (0, 0))


def _const_spec(shape):
    nd = len(shape)
    return pl.BlockSpec(shape, lambda *_: (0,) * nd)


def _win_rowblock(j):
    return jnp.where(j < 9, (j % 3) * 3 + j // 3, j)


def _proj(x, g, sc, sh, w_int):
    S = x.shape[0]
    tm = TM

    def body(x_ref, g_ref, sc_ref, sh_ref, w_ref, h_ref, q_ref, e_ref):
        j = pl.program_id(1)

        @pl.when(j == 0)
        def _():
            xv = x_ref[...]
            h = xv * _rms_r(xv) * g_ref[...] * (1.0 + sc_ref[...]) + sh_ref[...]
            h_ref[...] = h.astype(BF16)

        acc = _nt(h_ref[...], w_ref[...])

        @pl.when(j < 9)
        def _():
            q_ref[0] = acc

        @pl.when(j >= 9)
        def _():
            e_ref[0] = acc.astype(BF16)

    def e_idx(i, j):
        k = jnp.maximum(j - 9, 0)
        return (k // 2, i, k % 2)

    return pl.pallas_call(
        body, name="proj", grid=(S // tm, 19),
        in_specs=[pl.BlockSpec((tm, D), lambda i, j: (i, 0)), _vec_spec(), _vec_spec(), _vec_spec(),
                  pl.BlockSpec((512, D), lambda i, j: (_win_rowblock(j), 0))],
        out_specs=[pl.BlockSpec((tm, D), lambda i, j: (i, 0)),
                   pl.BlockSpec((1, tm, 512), lambda i, j: (jnp.minimum(j, 8), i, 0)),
                   pl.BlockSpec((1, tm, 512), e_idx)],
        out_shape=[jax.ShapeDtypeStruct((S, D), BF16), jax.ShapeDtypeStruct((9, S, 512), F32),
                   jax.ShapeDtypeStruct((5, S, D), BF16)],
        compiler_params=_cparams("parallel", "arbitrary"),
    )(x, g, sc, sh, w_int)


def _bias_table():
    slopes = (2.0 ** (-8.0 * np.arange(1, 13, dtype=np.float32) / 12.0)).astype(np.float32)
    qi = np.arange(HEAD)[:, None]
    kj = np.arange(2 * HEAD)[None, :]
    delta = HEAD + qi - kj
    mask = (delta >= 0) & (delta <= HEAD)
    out = np.zeros((3, N_SLOT, HEAD, 2 * HEAD), np.float32)
    for gi, d in enumerate(DILATIONS):
        for j in range(N_SLOT):
            bias = -slopes[gi * N_SLOT + j] * (delta * d).astype(np.float32)
            out[gi, j] = np.where(mask, bias, NEG)
    return jnp.asarray(out)


def _block_rows(b, d):
    r = b % d
    n = b // d
    st = n * (HEAD * d) + r
    stp = jnp.maximum(n - 1, 0) * (HEAD * d) + r
    return n, st, stp


def _attn_fwd(qkv, bias):
    S = qkv.shape[2]
    nblk = S // HEAD

    def body(qkv_ref, b_ref, o_ref, lse_ref, m_s, l_s, a_s):
        g = pl.program_id(1)

        @pl.when(g == 0)
        def _():
            m_s[...] = jnp.full_like(m_s, NEG)
            l_s[...] = jnp.zeros_like(l_s)
            a_s[...] = jnp.zeros_like(a_s)

        bias = b_ref[0, 0]
        col = lax.broadcasted_iota(jnp.int32, bias.shape, 1)
        bias_first = jnp.where(col < HEAD, NEG, bias)

        for gi, d in enumerate(DILATIONS):
            @pl.when(g == gi)
            def _(d=d):
                def step(b, carry):
                    n, st, stp = _block_rows(b, d)
                    cur = pl.ds(st, HEAD, stride=d)
                    prv = pl.ds(stp, HEAD, stride=d)
                    q = qkv_ref.at[0, 0][cur, :].astype(BF16)
                    kw = jnp.concatenate([qkv_ref.at[0, 1][prv, :], qkv_ref.at[0, 1][cur, :]], axis=0).astype(BF16)
                    vw = jnp.concatenate([qkv_ref.at[0, 2][prv, :], qkv_ref.at[0, 2][cur, :]], axis=0).astype(BF16)
                    s = _nt(q, kw) * SCALE + jnp.where(n > 0, bias, bias_first)
                    m_old = m_s[cur, :]
                    m_new = jnp.maximum(m_old, jnp.max(s, axis=-1, keepdims=True))
                    alpha = jnp.exp(m_old - m_new)
                    p = jnp.exp(s - m_new[:, :1])
                    l_s[cur, :] = alpha * l_s[cur, :] + jnp.sum(p, axis=-1, keepdims=True)
                    a_s[cur, :] = alpha * a_s[cur, :] + _nn(p.astype(BF16), vw)
                    m_s[cur, :] = m_new
                    return carry

                lax.fori_loop(0, nblk, step, 0, unroll=4)

        @pl.when(g == len(DILATIONS) - 1)
        def _():
            l = l_s[...]
            o_ref[...] = a_s[...] / l
            lse_ref[...] = m_s[...] + jnp.log(l)

    return pl.pallas_call(
        body, name="attn_fwd", grid=(N_SLOT, 3),
        in_specs=[pl.BlockSpec((1, 3, S, HEAD), lambda j, g: (g, 0, 0, j)),
                  pl.BlockSpec((1, 1, HEAD, 2 * HEAD), lambda j, g: (g, j, 0, 0))],
        out_specs=[pl.BlockSpec((S, HEAD), lambda j, g: (0, j)), pl.BlockSpec((S, HEAD), lambda j, g: (0, j))],
        out_shape=[jax.ShapeDtypeStruct((S, AOW), F32), jax.ShapeDtypeStruct((S, AOW), F32)],
        scratch_shapes=[pltpu.VMEM((S, HEAD), F32)] * 3,
        compiler_params=_cparams("parallel", "arbitrary"),
    )(qkv, bias)


def _shift_down(z, k, halo_rows):
    out = pltpu.roll(z, k, axis=0)
    rid = lax.broadcasted_iota(jnp.int32, z.shape, 0)
    for t in range(k):
        out = jnp.where(rid == t, halo_rows[t], out)
    return out


def _shift_up(z, k, halo_rows):
    n = z.shape[0]
    out = pltpu.roll(z, n - k, axis=0)
    rid = lax.broadcasted_iota(jnp.int32, z.shape, 0)
    for t in range(k):
        out = jnp.where(rid == n - k + t, halo_rows[t], out)
    return out


def _e_spec(chunk, tm):
    return pl.BlockSpec((1, tm, D), lambda i, c=chunk: (c, i, 0))


def _e_prev_spec(chunk, tm):
    return pl.BlockSpec((1, 16, D), lambda i, c=chunk: (c, jnp.maximum(i * (tm // 16) - 1, 0), 0))


def _e_next_spec(chunk, tm, S):
    return pl.BlockSpec((1, 16, D), lambda i, c=chunk: (c, jnp.minimum((i + 1) * (tm // 16), S // 16 - 1), 0))


def _mix(o_attn, e, cw8, ba, bb, w_bat, w_bc):
    S = o_attn.shape[0]
    tm = 256

    def body(o_ref, cb_ref, cc_ref, cx_ref, ga_ref, gb_ref, ccp_ref, cxp_ref, cw_ref, ba_ref, bb_ref, wba_ref, wbc_ref,
             obf_ref, cbu_ref, ya_ref, yc_ref, mg_ref):
        i = pl.program_id(0)
        o = o_ref[...].astype(BF16)
        obf_ref[...] = o
        ya = _nt(o, wba_ref[...])
        z = cc_ref[0].astype(F32) * cx_ref[0].astype(F32)
        zp = ccp_ref[0].astype(F32) * cxp_ref[0].astype(F32) * (i > 0).astype(F32)
        z1 = _shift_down(z, 1, [zp[15:16]])
        z2 = _shift_down(z, 2, [zp[14:15], zp[15:16]])
        cw = cw_ref[...]
        u = cw[0:1] * z2 + cw[1:2] * z1 + cw[2:3] * z
        cbu = (cb_ref[0].astype(F32) * u).astype(BF16)
        cbu_ref[...] = cbu
        yc = _nn(cbu, wbc_ref[...])
        sa = _sigmoid(ga_ref[0].astype(F32) + ba_ref[...])
        sb = _sigmoid(gb_ref[0].astype(F32) + bb_ref[...])
        ya_ref[...] = ya.astype(BF16)
        yc_ref[...] = yc.astype(BF16)
        mg_ref[...] = (sa * ya + sb * yc).astype(BF16)

    row = lambda w: pl.BlockSpec((tm, w), lambda i: (i, 0))
    return pl.pallas_call(
        body, name="mix", grid=(S // tm,),
        in_specs=[row(AOW)] + [_e_spec(c, tm) for c in range(5)] + [_e_prev_spec(1, tm), _e_prev_spec(2, tm),
                  _const_spec((8, D)), _vec_spec(), _vec_spec(), _const_spec((D, AOW)), _const_spec((D, D))],
        out_specs=[row(AOW), row(D), row(D), row(D), row(D)],
        out_shape=[jax.ShapeDtypeStruct((S, AOW), BF16)] + [jax.ShapeDtypeStruct((S, D), BF16)] * 4,
        compiler_params=_cparams("parallel"),
    )(o_attn, e, e, e, e, e, e, e, cw8, ba, bb, w_bat, w_bc)


def _out_proj(merged, w_out, x, gate1, g_mlp, sc2, sh2):
    S = x.shape[0]
    tm = TM

    def body(mg_ref, w_ref, x_ref, gt_ref, g_ref, sc_ref, sh_ref, x1_ref, mo_ref, h2_ref):
        mo = _nn(mg_ref[...], w_ref[...])
        mo_ref[...] = mo.astype(BF16)
        x1 = x_ref[...] + gt_ref[...] * mo
        x1_ref[...] = x1
        h2 = x1 * _rms_r(x1) * g_ref[...] * (1.0 + sc_ref[...]) + sh_ref[...]
        h2_ref[...] = h2.astype(BF16)

    row = pl.BlockSpec((tm, D), lambda i: (i, 0))
    return pl.pallas_call(
        body, name="out_proj", grid=(S // tm,),
        in_specs=[row, _const_spec((D, D)), row, _vec_spec(), _vec_spec(), _vec_spec(), _vec_spec()],
        out_specs=[row, row, row],
        out_shape=[jax.ShapeDtypeStruct((S, D), F32), jax.ShapeDtypeStruct((S, D), BF16), jax.ShapeDtypeStruct((S, D), BF16)],
        compiler_params=_cparams("parallel"),
    )(merged, w_out, x, gate1, g_mlp, sc2, sh2)


def _mlp_in(h2, w_mit):
    S = h2.shape[0]
    tm, tn = TM, 1024

    def body(h_ref, w_ref, a_ref, f_ref):
        a = _nt(h_ref[...], w_ref[...])
        a_ref[...] = a.astype(BF16)
        f_ref[...] = jnp.square(jnp.maximum(a, 0.0)).astype(BF16)

    blk = pl.BlockSpec((tm, tn), lambda i, j: (i, j))
    return pl.pallas_call(
        body, name="mlp_in", grid=(S // tm, DFF // tn),
        in_specs=[pl.BlockSpec((tm, D), lambda i, j: (i, 0)), pl.BlockSpec((tn, D), lambda i, j: (j, 0))],
        out_specs=[blk, blk],
        out_shape=[jax.ShapeDtypeStruct((S, DFF), BF16)] * 2,
        compiler_params=_cparams("parallel", "parallel"),
    )(h2, w_mit)


def _mlp_out(f, w_mo, x1, gate2, g_fin, tgt):
    S = x1.shape[0]
    tm, tk = TM, 1024
    nk = DFF // tk

    def body(f_ref, w_ref, x1_ref, gt_ref, g_ref, t_ref, mlp_ref, dx2_ref, pv_ref, acc):
        i, k = pl.program_id(0), pl.program_id(1)

        @pl.when((i == 0) & (k == 0))
        def _():
            pv_ref[...] = jnp.zeros_like(pv_ref)

        @pl.when(k == 0)
        def _():
            acc[...] = jnp.zeros_like(acc)

        acc[...] += _nn(f_ref[...], w_ref[...])

        @pl.when(k == nk - 1)
        def _():
            mlp = acc[...]
            mlp_ref[...] = mlp.astype(BF16)
            x2 = x1_ref[...] + gt_ref[...] * mlp
            r = _rms_r(x2)
            g = g_ref[...]
            err = x2 * r * g - t_ref[...]
            dy = err * (1.0 / D)
            dx2, pg = _rms_bwd(x2, r, g, dy)
            dx2_ref[...] = dx2
            pv_ref[0:1, :] += _rowsum(pg)
            pv_ref[1:2, :] += 0.5 * _rowsum(jnp.mean(err * err, axis=-1, keepdims=True))

    row = pl.BlockSpec((tm, D), lambda i, k: (i, 0))
    return pl.pallas_call(
        body, name="mlp_out", grid=(S // tm, nk),
        in_specs=[pl.BlockSpec((tm, tk), lambda i, k: (i, k)), pl.BlockSpec((tk, D), lambda i, k: (k, 0)),
                  row, _vec_spec(), _vec_spec(), row],
        out_specs=[row, row, _const_spec((8, D))],
        out_shape=[jax.ShapeDtypeStruct((S, D), BF16), jax.ShapeDtypeStruct((S, D), F32), jax.ShapeDtypeStruct((8, D), F32)],
        scratch_shapes=[pltpu.VMEM((tm, D), F32)],
        compiler_params=_cparams("arbitrary", "arbitrary"),
    )(f, w_mo, x1, gate2, g_fin, tgt)


def _bwd_mlp_a(dx2, gate2, mlp, w_mo, a):
    S = dx2.shape[0]
    tm, tn = TM, 1024

    def body(dx_ref, gt_ref, mlp_ref, w_ref, a_ref, da_ref, dmo_ref, pv_ref):
        i, j = pl.program_id(0), pl.program_id(1)

        @pl.when((i == 0) & (j == 0))
        def _():
            pv_ref[...] = jnp.zeros_like(pv_ref)

        @pl.when(j == 0)
        def _():
            dx = dx_ref[...]
            dmo_ref[...] = (dx * gt_ref[...]).astype(BF16)
            pv_ref[0:1, :] += _rowsum(dx * mlp_ref[...].astype(F32))

        df = _nt(dmo_ref[...], w_ref[...])
        da_ref[...] = (df * (2.0 * jnp.maximum(a_ref[...].astype(F32), 0.0))).astype(BF16)

    row = pl.BlockSpec((tm, D), lambda i, j: (i, 0))
    blk = pl.BlockSpec((tm, tn), lambda i, j: (i, j))
    return pl.pallas_call(
        body, name="bwd_mlp_a", grid=(S // tm, DFF // tn),
        in_specs=[row, _vec_spec(), row, pl.BlockSpec((tn, D), lambda i, j: (j, 0)), blk],
        out_specs=[blk, row, _const_spec((8, D))],
        out_shape=[jax.ShapeDtypeStruct((S, DFF), BF16), jax.ShapeDtypeStruct((S, D), BF16), jax.ShapeDtypeStruct((8, D), F32)],
        compiler_params=_cparams("arbitrary", "arbitrary"),
    )(dx2, gate2, mlp, w_mo, a)


def _bwd_mlp_b(da, w_mit, x1, dx2, g_mlp, sc2):
    S = x1.shape[0]
    tm, tk = TM, 1024
    nk = DFF // tk

    def body(da_ref, w_ref, x1_ref, dx2_ref, g_ref, sc_ref, dx1_ref, pv_ref, acc):
        i, k = pl.program_id(0), pl.program_id(1)

        @pl.when((i == 0) & (k == 0))
        def _():
            pv_ref[...] = jnp.zeros_like(pv_ref)

        @pl.when(k == 0)
        def _():
            acc[...] = jnp.zeros_like(acc)

        acc[...] += _nn(da_ref[...], w_ref[...])

        @pl.when(k == nk - 1)
        def _():
            dh = acc[...]
            x1 = x1_ref[...]
            r = _rms_r(x1)
            g = g_ref[...]
            dxn, pg = _rms_bwd(x1, r, g, dh * (1.0 + sc_ref[...]))
            dx1_ref[...] = dx2_ref[...] + dxn
            pv_ref[0:1, :] += _rowsum(dh)
            pv_ref[1:2, :] += _rowsum(dh * (x1 * r * g))
            pv_ref[2:3, :] += _rowsum(pg)

    row = pl.BlockSpec((tm, D), lambda i, k: (i, 0))
    return pl.pallas_call(
        body, name="bwd_mlp_b", grid=(S // tm, nk),
        in_specs=[pl.BlockSpec((tm, tk), lambda i, k: (i, k)), pl.BlockSpec((tk, D), lambda i, k: (k, 0)),
                  row, row, _vec_spec(), _vec_spec()],
        out_specs=[row, _const_spec((8, D))],
        out_shape=[jax.ShapeDtypeStruct((S, D), F32), jax.ShapeDtypeStruct((8, D), F32)],
        scratch_shapes=[pltpu.VMEM((tm, D), F32)],
        compiler_params=_cparams("arbitrary", "arbitrary"),
    )(da, w_mit, x1, dx2, g_mlp, sc2)


def _bwd_mix(dx1, gate1, mo, e, cw8, ba, bb, ya, yc, o_attn, w_out, w_bc, w_bat):
    S = dx1.shape[0]
    tm = 256
    n_tiles = S // tm

    def body(dx_ref, dxn_ref, gt_ref, mo_ref, cb_ref, cc_ref, cx_ref, ga_ref, gb_ref, cbn_ref, gbn_ref, ccp_ref, cxp_ref,
             cw_ref, ba_ref, bb_ref, ya_ref, yc_ref, o_ref, wout_ref, wbc_ref, wba_ref,
             dmo_ref, dya_ref, dyc_ref, do_ref, dl_ref, de_ref, pv_ref):
        i = pl.program_id(0)

        @pl.when(i == 0)
        def _():
            pv_ref[...] = jnp.zeros_like(pv_ref)

        gate = gt_ref[...]
        bbv = bb_ref[...]

        def conv_branch_grad(dx_rows, gb_rows):
            dmo = (dx_rows * gate).astype(BF16)
            dmg = _nt(dmo, wout_ref[...])
            sb = _sigmoid(gb_rows + bbv)
            dyc = dmg * sb
            return dmo, dmg, sb, dyc, _nt(dyc.astype(BF16), wbc_ref[...])

        dx = dx_ref[...]
        cb = cb_ref[0].astype(F32)
        cc = cc_ref[0].astype(F32)
        cx = cx_ref[0].astype(F32)
        dmo, dmg, sb, dyc, dcbu = conv_branch_grad(dx, gb_ref[0].astype(F32))
        dmo_ref[...] = dmo
        pv_ref[0:1, :] += _rowsum(dx * mo_ref[...].astype(F32))
        sa = _sigmoid(ga_ref[0].astype(F32) + ba_ref[...])
        dya = (dmg * sa).astype(BF16)
        dya_ref[...] = dya
        dyc_ref[...] = dyc.astype(BF16)
        dga = dmg * ya_ref[...].astype(F32) * sa * (1.0 - sa)
        dgb = dmg * yc_ref[...].astype(F32) * sb * (1.0 - sb)
        pv_ref[1:2, :] += _rowsum(dga)
        pv_ref[2:3, :] += _rowsum(dgb)

        do = _nn(dya, wba_ref[...])
        do_ref[...] = do
        prod = do * o_ref[...]
        dl_ref[...] = jnp.concatenate(
            [jnp.broadcast_to(jnp.sum(prod[:, s * HEAD:(s + 1) * HEAD], axis=-1, keepdims=True), (tm, HEAD))
             for s in range(N_SLOT)], axis=1)

        z = cc * cx
        zp = ccp_ref[0].astype(F32) * cxp_ref[0].astype(F32) * (i > 0).astype(F32)
        z1 = _shift_down(z, 1, [zp[15:16]])
        z2 = _shift_down(z, 2, [zp[14:15], zp[15:16]])
        cw = cw_ref[...]
        u = cw[0:1] * z2 + cw[1:2] * z1 + cw[2:3] * z
        du = dcbu * cb
        dcbu_n = conv_branch_grad(dxn_ref[...], gbn_ref[0].astype(F32))[4]
        du_n = dcbu_n * cbn_ref[0].astype(F32) * (i < n_tiles - 1).astype(F32)
        du1 = _shift_up(du, 1, [du_n[0:1]])
        du2 = _shift_up(du, 2, [du_n[0:1], du_n[1:2]])
        dz = cw[2:3] * du + cw[1:2] * du1 + cw[0:1] * du2
        pv_ref[3:4, :] += _rowsum(du * z2)
        pv_ref[4:5, :] += _rowsum(du * z1)
        pv_ref[5:6, :] += _rowsum(du * z)

        de_ref[0] = (dcbu * u).astype(BF16)
        de_ref[1] = (dz * cx).astype(BF16)
        de_ref[2] = (dz * cc).astype(BF16)
        de_ref[3] = dga.astype(BF16)
        de_ref[4] = dgb.astype(BF16)

    row = lambda w: pl.BlockSpec((tm, w), lambda i: (i, 0))
    nxt = pl.BlockSpec((16, D), lambda i: (jnp.minimum((i + 1) * (tm // 16), S // 16 - 1), 0))
    return pl.pallas_call(
        body, name="bwd_mix", grid=(n_tiles,),
        in_specs=[row(D), nxt, _vec_spec(), row(D)] + [_e_spec(c, tm) for c in range(5)]
                 + [_e_next_spec(0, tm, S), _e_next_spec(4, tm, S), _e_prev_spec(1, tm), _e_prev_spec(2, tm),
                    _const_spec((8, D)), _vec_spec(), _vec_spec(), row(D), row(D), row(AOW),
                    _const_spec((D, D)), _const_spec((D, D)), _const_spec((D, AOW))],
        out_specs=[row(D), row(D), row(D), row(AOW), row(AOW), pl.BlockSpec((5, tm, D), lambda i: (0, i, 0)),
                   _const_spec((8, D))],
        out_shape=[jax.ShapeDtypeStruct((S, D), BF16)] * 3 + [jax.ShapeDtypeStruct((S, AOW), F32)] * 2
                  + [jax.ShapeDtypeStruct((5, S, D), BF16), jax.ShapeDtypeStruct((8, D), F32)],
        compiler_params=_cparams("arbitrary"),
    )(dx1, dx1, gate1, mo, e, e, e, e, e, e, e, e, e, cw8, ba, bb, ya, yc, o_attn, w_out, w_bc, w_bat)


def _attn_bwd(qkv, do, lse, dl, bias):
    S = qkv.shape[2]
    nblk = S // HEAD

    def body(qkv_ref, do_ref, lse_ref, dl_ref, b_ref, d_ref):
        g = pl.program_id(1)
        d_ref[...] = jnp.zeros_like(d_ref)
        bias = b_ref[0, 0]
        col = lax.broadcasted_iota(jnp.int32, bias.shape, 1)
        bias_first = jnp.where(col < HEAD, NEG, bias)

        for gi, d in enumerate(DILATIONS):
            @pl.when(g == gi)
            def _(d=d):
                def step(b, carry):
                    n, st, stp = _block_rows(b, d)
                    cur = pl.ds(st, HEAD, stride=d)
                    prv = pl.ds(stp, HEAD, stride=d)
                    q = qkv_ref.at[0, 0][cur, :].astype(BF16)
                    kw = jnp.concatenate([qkv_ref.at[0, 1][prv, :], qkv_ref.at[0, 1][cur, :]], axis=0).astype(BF16)
                    vw = jnp.concatenate([qkv_ref.at[0, 2][prv, :], qkv_ref.at[0, 2][cur, :]], axis=0).astype(BF16)
                    s = _nt(q, kw) * SCALE + jnp.where(n > 0, bias, bias_first)
                    p = jnp.exp(s - lse_ref[cur, :][:, :1])
                    dob = do_ref[cur, :].astype(BF16)
                    dvw = _tn(p.astype(BF16), dob)
                    dp = _nt(dob, vw)
                    ds = (p * (dp - dl_ref[cur, :][:, :1]) * SCALE).astype(BF16)
                    d_ref.at[0, 0][cur, :] = _nn(ds, kw)
                    dkw = _tn(ds, q)
                    d_ref.at[0, 1][cur, :] += dkw[HEAD:]
                    d_ref.at[0, 1][prv, :] += dkw[:HEAD]
                    d_ref.at[0, 2][cur, :] += dvw[HEAD:]
                    d_ref.at[0, 2][prv, :] += dvw[:HEAD]
                    return carry

                lax.fori_loop(0, nblk, step, 0, unroll=4)

    col_blk = pl.BlockSpec((S, HEAD), lambda j, g: (0, j))
    qkv_blk = pl.BlockSpec((1, 3, S, HEAD), lambda j, g: (g, 0, 0, j))
    return pl.pallas_call(
        body, name="attn_bwd", grid=(N_SLOT, 3),
        in_specs=[qkv_blk, col_blk, col_blk, col_blk, pl.BlockSpec((1, 1, HEAD, 2 * HEAD), lambda j, g: (g, j, 0, 0))],
        out_specs=qkv_blk,
        out_shape=jax.ShapeDtypeStruct((3, 3, S, AOW), F32),
        compiler_params=_cparams("parallel", "arbitrary"),
    )(qkv, do, lse, dl, bias)


def _bwd_in(dqkv, de, w_int, x, dx1, g_mix, sc1):
    S = x.shape[0]
    tm = TM

    def body(dq_ref, de_ref, w_ref, x_ref, dx1_ref, g_ref, sc_ref, gx_ref, pv_ref, acc):
        i, k = pl.program_id(0), pl.program_id(1)

        @pl.when((i == 0) & (k == 0))
        def _():
            pv_ref[...] = jnp.zeros_like(pv_ref)

        @pl.when(k == 0)
        def _():
            acc[...] = jnp.zeros_like(acc)

        @pl.when(k < 9)
        def _():
            acc[...] += _nn(dq_ref[0].astype(BF16), w_ref[...])

        @pl.when(k >= 9)
        def _():
            acc[...] += _nn(de_ref[0], w_ref[...])

        @pl.when(k == 18)
        def _():
            dh = acc[...]
            xv = x_ref[...]
            r = _rms_r(xv)
            g = g_ref[...]
            dxn, pg = _rms_bwd(xv, r, g, dh * (1.0 + sc_ref[...]))
            gx_ref[...] = dx1_ref[...] + dxn
            pv_ref[0:1, :] += _rowsum(dh)
            pv_ref[1:2, :] += _rowsum(dh * (xv * r * g))
            pv_ref[2:3, :] += _rowsum(pg)

    def e_idx(i, k):
        kk = jnp.maximum(k - 9, 0)
        return (kk // 2, i, kk % 2)

    row = pl.BlockSpec((tm, D), lambda i, k: (i, 0))
    return pl.pallas_call(
        body, name="bwd_in", grid=(S // tm, 19),
        in_specs=[pl.BlockSpec((1, tm, 512), lambda i, k: (jnp.minimum(k, 8), i, 0)), pl.BlockSpec((1, tm, 512), e_idx),
                  pl.BlockSpec((512, D), lambda i, k: (_win_rowblock(k), 0)), row, row, _vec_spec(), _vec_spec()],
        out_specs=[row, _const_spec((8, D))],
        out_shape=[jax.ShapeDtypeStruct((S, D), F32), jax.ShapeDtypeStruct((8, D), F32)],
        scratch_shapes=[pltpu.VMEM((tm, D), F32)],
        compiler_params=_cparams("arbitrary", "arbitrary"),
    )(dqkv, de, w_int, x, dx1, g_mix, sc1)


def _grad_w(name, a, b):
    S, ka = a.shape
    nb = b.shape[1]
    tt = 1024
    nt = S // tt

    def body(a_ref, b_ref, o_ref, acc):
        t = pl.program_id(1)

        @pl.when(t == 0)
        def _():
            acc[...] = jnp.zeros_like(acc)

        acc[...] += _tn(a_ref[...], b_ref[...])

        @pl.when(t == nt - 1)
        def _():
            o_ref[...] = acc[...].astype(BF16)

    return pl.pallas_call(
        body, name=name, grid=(ka // 512, nt),
        in_specs=[pl.BlockSpec((tt, 512), lambda n, t: (t, n)), pl.BlockSpec((tt, nb), lambda n, t: (t, 0))],
        out_specs=pl.BlockSpec((512, nb), lambda n, t: (n, 0)),
        out_shape=jax.ShapeDtypeStruct((ka, nb), BF16),
        scratch_shapes=[pltpu.VMEM((512, nb), F32)],
        compiler_params=_cparams("parallel", "arbitrary"),
    )(a, b)


def _grad_w_in(dqkv, de, h):
    S = h.shape[0]
    tt = 1024
    nt = S // tt

    def body(dq_ref, de_ref, h_ref, o_ref, acc):
        n, t = pl.program_id(0), pl.program_id(1)

        @pl.when(t == 0)
        def _():
            acc[...] = jnp.zeros_like(acc)

        @pl.when(n < 9)
        def _():
            acc[...] += _tn(dq_ref[0].astype(BF16), h_ref[...])

        @pl.when(n >= 9)
        def _():
            acc[...] += _tn(de_ref[0], h_ref[...])

        @pl.when(t == nt - 1)
        def _():
            o_ref[...] = acc[...].astype(BF16)

    def e_idx(n, t):
        kk = jnp.maximum(n - 9, 0)
        return (kk // 2, jnp.where(n >= 9, t, 0), kk % 2)

    return pl.pallas_call(
        body, name="grad_w_in", grid=(19, nt),
        in_specs=[pl.BlockSpec((1, tt, 512), lambda n, t: (jnp.minimum(n, 8), jnp.where(n < 9, t, nt - 1), 0)),
                  pl.BlockSpec((1, tt, 512), e_idx), pl.BlockSpec((tt, D), lambda n, t: (t, 0))],
        out_specs=pl.BlockSpec((512, D), lambda n, t: (_win_rowblock(n), 0)),
        out_shape=jax.ShapeDtypeStruct((19 * 512, D), BF16),
        scratch_shapes=[pltpu.VMEM((512, D), F32)],
        compiler_params=_cparams("parallel", "arbitrary"),
    )(dqkv, de, h)


def _local_step(x, tgt, mod, g_mix, g_mlp, g_fin, ba, bb, cw8, w_int, late_weights, mlp_grads_ready, other_grads_ready):
    S = x.shape[0]
    sh1, sc1, gt1, sh2, sc2, gt2 = [mod[k:k + 1] for k in range(6)]
    bias = _bias_table()

    h, qkv, e = _proj(x, g_mix, sc1, sh1, w_int)
    qkv = qkv.reshape(3, 3, S, AOW)
    o_attn, lse = _attn_fwd(qkv, bias)
    w_bat, w_bc, w_out, w_mit, w_mo = late_weights(o_attn)
    o_bf, cbu, ya, yc, merged = _mix(o_attn, e, cw8, ba, bb, w_bat, w_bc)
    x1, mo, h2 = _out_proj(merged, w_out, x, gt1, g_mlp, sc2, sh2)
    a, f = _mlp_in(h2, w_mit)
    mlp, dx2, pv_f = _mlp_out(f, w_mo, x1, gt2, g_fin, tgt)

    da, dmo2, pv_a = _bwd_mlp_a(dx2, gt2, mlp, w_mo, a)
    dx1, pv_b = _bwd_mlp_b(da, w_mit, x1, dx2, g_mlp, sc2)
    zero = mlp_grads_ready(_grad_w("grad_w_mi", da, h2), _grad_w("grad_w_mo", f, dmo2))
    dmo, dya, dyc, do, dl, de, pv_m = _bwd_mix(dx1, gt1 + zero, mo, e, cw8, ba, bb, ya, yc, o_attn, w_out, w_bc, w_bat)
    dqkv = _attn_bwd(qkv, do, lse, dl, bias).reshape(9, S, AOW)
    zero = other_grads_ready(_grad_w_in(dqkv, de, h), _grad_w("grad_w_ba", dya, o_bf), _grad_w("grad_w_bc", cbu, dyc),
                             _grad_w("grad_w_out", merged, dmo))
    grad_x, pv_i = _bwd_in(dqkv, de, w_int, x, dx1, g_mix, sc1 + zero)

    vec = jnp.concatenate([pv_i[0:2], pv_m[0:1], pv_b[0:2], pv_a[0:1], pv_i[2:3], pv_b[2:3], pv_f[0:1],
                           pv_m[1:3], pv_m[3:6], jnp.zeros((2, D), F32)], axis=0)
    return pv_f[1, 0], grad_x, vec


def _my_place():
    return lax.axis_index("x"), lax.axis_index("y"), lax.axis_index("c")


def _dev_index(px, py, pc):
    return 4 * px + 2 * py + pc


def _allgather_weights(shards):
    nw = len(shards)
    HBM = pl.BlockSpec(memory_space=pl.ANY)

    def body(*refs):
        sh, full = refs[:nw], refs[nw:2 * nw]
        send_sems, recv_sems, local_sems = refs[2 * nw:]
        x, y, c = _my_place()
        me, sibling = (x, y, c), (x, y, 1 - c)
        chips = [(1 - x, y), (x, 1 - y), (1 - x, 1 - y)]

        def rows(w, px, py, pc):
            r = sh[w].shape[0]
            return full[w].at[pl.ds(pl.multiple_of(_dev_index(px, py, pc) * r, 16), r), :]

        def copy(w, k, block, to, src=None):
            return pltpu.make_async_remote_copy(
                src_ref=rows(w, *block) if src is None else src, dst_ref=rows(w, *block),
                send_sem=send_sems.at[w, k], recv_sem=recv_sems.at[w, k], device_id=to, device_id_type=MESH)

        mine = [pltpu.make_async_copy(sh[w], rows(w, *me), local_sems.at[w]) for w in range(nw)]
        for cp in mine:
            cp.start()
        first = []
        for w in range(nw):
            first.append(copy(w, 0, me, sibling, src=sh[w]))
            first += [copy(w, 1 + j, me, (*chip, c), src=sh[w]) for j, chip in enumerate(chips)]
        for cp in first:
            cp.start()
        passed = []
        for w in range(nw):
            for j, chip in enumerate(chips):
                copy(w, 1 + j, (*chip, c), me).wait_recv()
                fwd = copy(w, 4 + j, (*chip, c), sibling)
                fwd.start()
                passed.append(fwd)
        for w in range(nw):
            copy(w, 0, sibling, me).wait_recv()
            for j, chip in enumerate(chips):
                copy(w, 4 + j, (*chip, 1 - c), me).wait_recv()
        for cp in first + passed:
            cp.wait_send()
        for cp in mine:
            cp.wait()

    return pl.pallas_call(
        body, name="allgather_weights",
        out_shape=[jax.ShapeDtypeStruct((N_DEV * s.shape[0], s.shape[1]), s.dtype) for s in shards],
        in_specs=[HBM] * nw, out_specs=[HBM] * nw,
        scratch_shapes=[pltpu.SemaphoreType.DMA((nw, 7)), pltpu.SemaphoreType.DMA((nw, 7)), pltpu.SemaphoreType.DMA((nw,))],
    )(*shards)


def _peer(x, y, c, m):
    return (x ^ ((m >> 2) & 1), y ^ ((m >> 1) & 1), c ^ (m & 1))


HBM_SPEC = pl.BlockSpec(memory_space=pltpu.HBM)
SEM_SPEC = pl.BlockSpec(memory_space=pltpu.SEMAPHORE)
N_PEER = N_DEV - 1


def _split_copy(mode, src_ref, land_ref, send_sems, recv_sems, w, m, place, arriving=False):
    x, y, c = place
    peer = _peer(x, y, c, m)
    k = w * N_PEER + m - 1
    if mode == "gather":
        r = src_ref.shape[0]
        at = _dev_index(*peer) if arriving else _dev_index(x, y, c)
        src, dst = src_ref, land_ref.at[pl.ds(pl.multiple_of(at * r, 16), r), :]
    else:
        r = land_ref.shape[1]
        at = _dev_index(x, y, c) if arriving else _dev_index(*peer)
        src, dst = src_ref.at[pl.ds(pl.multiple_of(at * r, 16), r), :], land_ref.at[m - 1]
    return pltpu.make_async_remote_copy(src_ref=src, dst_ref=dst, send_sem=send_sems.at[k], recv_sem=recv_sems.at[k],
                                        device_id=peer, device_id_type=MESH)


def _split_start(name, mode, srcs, lands):
    n = len(srcs)

    def body(*refs):
        src, land = refs[:n], refs[n:2 * n]
        send_sems, recv_sems = refs[2 * n], refs[2 * n + 1]
        token = refs[-1]
        place = _my_place()
        for w in range(n):
            for m in range(1, N_DEV):
                _split_copy(mode, src[w], land[w], send_sems, recv_sems, w, m, place).start()
        token[...] = jnp.zeros_like(token)

    hbm = lambda t: pltpu.HBM(t.shape, t.dtype)
    out = pl.pallas_call(
        body, name=name,
        out_shape=(pltpu.SemaphoreType.DMA((n * N_PEER,)), pltpu.SemaphoreType.DMA((n * N_PEER,)), *[hbm(t) for t in srcs],
                   *[hbm(t) for t in lands], jax.ShapeDtypeStruct((8, 128), F32)),
        in_specs=(HBM_SPEC,) * (2 * n),
        out_specs=(SEM_SPEC, SEM_SPEC) + (HBM_SPEC,) * (2 * n) + (pl.BlockSpec(memory_space=pltpu.VMEM),),
        input_output_aliases={i: 2 + i for i in range(2 * n)},
        compiler_params=pltpu.CompilerParams(has_side_effects=pltpu.SideEffectType.DATAFLOW_SIDE_EFFECTING),
    )(*[pltpu.with_memory_space_constraint(t, pltpu.HBM) for t in (*srcs, *lands)])
    return out[0], out[1], out[2:2 + n], out[2 + n:2 + 2 * n], out[-1][0:1, 0:1]


def _split_wait(name, mode, send_sems, recv_sems, srcs, lands, after):
    n = len(srcs)

    def body(*refs):
        src, land = refs[:n], refs[n:2 * n]
        ssem, rsem = refs[2 * n], refs[2 * n + 1]
        place = _my_place()
        for w in range(n):
            for m in range(1, N_DEV):
                _split_copy(mode, src[w], land[w], ssem, rsem, w, m, place).wait_send()
                _split_copy(mode, src[w], land[w], ssem, rsem, w, m, place, arriving=True).wait_recv()

    hbm = lambda t: pltpu.HBM(t.shape, t.dtype)
    out = pl.pallas_call(
        body, name=name,
        out_shape=tuple(hbm(t) for t in (*srcs, *lands)),
        in_specs=(HBM_SPEC,) * (2 * n) + (SEM_SPEC, SEM_SPEC, pl.BlockSpec(memory_space=pl.ANY)),
        out_specs=(HBM_SPEC,) * (2 * n),
        input_output_aliases={i: i for i in range(2 * n)},
        compiler_params=pltpu.CompilerParams(has_side_effects=pltpu.SideEffectType.DATAFLOW_SIDE_EFFECTING),
    )(*srcs, *lands, send_sems, recv_sems, after)
    return out[:n], out[n:]


def _allgather_small(v, name):
    r, ccols = v.shape

    def body(v_ref, out_ref, send_sems, recv_sems):
        x, y, c = _my_place()
        my_idx = _dev_index(x, y, c)
        out_ref[my_idx] = v_ref[...]

        def copy(m):
            peer = _peer(x, y, c, m)
            return pltpu.make_async_remote_copy(
                src_ref=v_ref, dst_ref=out_ref.at[my_idx],
                send_sem=send_sems.at[m - 1], recv_sem=recv_sems.at[m - 1], device_id=peer, device_id_type=MESH)

        def arrival(m):
            peer = _peer(x, y, c, m)
            return pltpu.make_async_remote_copy(
                src_ref=v_ref, dst_ref=out_ref.at[_dev_index(*peer)],
                send_sem=send_sems.at[m - 1], recv_sem=recv_sems.at[m - 1], device_id=peer, device_id_type=MESH)

        sends = [copy(m) for m in range(1, N_DEV)]
        for cp in sends:
            cp.start()
        for m in range(1, N_DEV):
            arrival(m).wait_recv()
        for cp in sends:
            cp.wait_send()

    return pl.pallas_call(
        body, name=name,
        out_shape=jax.ShapeDtypeStruct((N_DEV, r, ccols), v.dtype),
        in_specs=[pl.BlockSpec(memory_space=pltpu.VMEM)], out_specs=pl.BlockSpec(memory_space=pltpu.VMEM),
        scratch_shapes=[pltpu.SemaphoreType.DMA((7,)), pltpu.SemaphoreType.DMA((7,))],
    )(v)


def _ada_fwd(c_all, w_ada, b_cols):
    def body(c_ref, w_ref, b_ref, mod_ref, act_ref):
        cv = c_ref[...]
        act = cv * _sigmoid(cv)
        act_ref[...] = act
        mod_ref[...] = jnp.dot(act, w_ref[...], preferred_element_type=F32, precision=lax.Precision.HIGHEST) + b_ref[...]

    return pl.pallas_call(
        body, name="ada_fwd",
        out_shape=[jax.ShapeDtypeStruct((N_DEV, w_ada.shape[1]), F32), jax.ShapeDtypeStruct((N_DEV, D), F32)],
        compiler_params=_cparams(),
    )(c_all, w_ada, b_cols)


def _ada_bwd(act_t, gm_cols):
    def body(a_ref, g_ref, o_ref):
        o_ref[...] = jnp.dot(a_ref[...], g_ref[...], preferred_element_type=F32, precision=lax.Precision.HIGHEST)

    return pl.pallas_call(
        body, name="ada_bwd", out_shape=jax.ShapeDtypeStruct((D, gm_cols.shape[1]), F32), compiler_params=_cparams(),
    )(act_t, gm_cols)


def _row_tile(r):
    for t in (256, 304, 128, 64, 16):
        if r % t == 0:
            return t
    return r


def _sum_parts(parts, name, own=None):
    k, r, ccols = parts.shape
    tr = _row_tile(r)

    def body(*refs):
        p_ref, o_ref = refs[0], refs[-1]
        acc = p_ref[0].astype(F32) if own is None else refs[1][...].astype(F32) + p_ref[0].astype(F32)
        for s in range(1, k):
            acc = acc + p_ref[s].astype(F32)
        o_ref[...] = acc

    blk = pl.BlockSpec((tr, ccols), lambda i: (i, 0))
    return pl.pallas_call(
        body, name=name, grid=(r // tr,),
        in_specs=[pl.BlockSpec((k, tr, ccols), lambda i: (0, i, 0))] + ([] if own is None else [blk]),
        out_specs=blk,
        out_shape=jax.ShapeDtypeStruct((r, ccols), F32),
        compiler_params=_cparams("parallel"),
    )(*((parts,) if own is None else (parts, own)))


def _adamw(w, g, m, v, name):
    r, ccols = w.shape
    tr = _row_tile(r)
    c1 = 1.0 / (1.0 - B1 ** STEP)
    c2 = 1.0 / (1.0 - B2 ** STEP)

    def body(w_ref, g_ref, m_ref, v_ref, d_ref, nm_ref, nv_ref):
        gv = g_ref[...]
        nm = B1 * m_ref[...] + (1.0 - B1) * gv
        nv = B2 * v_ref[...] + (1.0 - B2) * jnp.square(gv)
        nm_ref[...] = nm
        nv_ref[...] = nv
        d_ref[...] = -LR * ((nm * c1) / (jnp.sqrt(nv * c2) + ADAM_EPS) + WD * w_ref[...])

    blk = pl.BlockSpec((tr, ccols), lambda i: (i, 0))
    return pl.pallas_call(
        body, name=name, grid=(r // tr,), in_specs=[blk] * 4, out_specs=[blk] * 3,
        out_shape=[jax.ShapeDtypeStruct((r, ccols), F32)] * 3,
        compiler_params=_cparams("parallel"),
    )(w, g, m, v)


def _pack_vectors(b_ada, g_mix, g_mlp, g_fin, b_gate, conv_w):
    conv_rows = jnp.pad(conv_w.reshape(3, HEAD), ((0, 0), (0, D - HEAD)))
    return jnp.concatenate([b_ada.reshape(6, D), g_mix.reshape(1, D), g_mlp.reshape(1, D), g_fin.reshape(1, D),
                            b_gate.reshape(2, D), conv_rows, jnp.zeros((2, D), F32)], axis=0)


def _unpack_vectors(p):
    return (p[0:6].reshape(1, 6 * D), p[6:7], p[9:11].reshape(1, 2 * D), p[11:14, :HEAD].reshape(1, 3, HEAD),
            p[7:8], p[8])


def kernel(x, c, w_ada, b_ada, g_norm_mix, w_in, b_gate, conv_w, w_branch_attn, w_branch_conv, w_out, g_norm_mlp, w_mlp_in, w_mlp_out, g_norm_final, loss_target, m_w_ada, m_b_ada, m_g_norm_mix, m_w_in, m_b_gate, m_conv_w, m_w_branch_attn, m_w_branch_conv, m_w_out, m_g_norm_mlp, m_w_mlp_in, m_w_mlp_out, m_g_norm_final, v_w_ada, v_b_ada, v_g_norm_mix, v_w_in, v_b_gate, v_conv_w, v_w_branch_attn, v_w_branch_conv, v_w_out, v_g_norm_mlp, v_w_mlp_in, v_w_mlp_out, v_g_norm_final):
    S = x.shape[1]
    xi, yi, ci = _my_place()
    me = _dev_index(xi, yi, ci)
    x2 = x.reshape(S, D)
    tgt = loss_target.reshape(S, D)

    pay = jnp.zeros((8, D), F32).at[0].set(c[0]).at[1:4, :HEAD].set(conv_w[0])
    got = _allgather_small(pay, "gather_cond")
    c_all = got[:, 0, :]
    cw8 = jnp.pad(got[:, 1:4, :HEAD].transpose(1, 0, 2).reshape(3, D), ((0, 5), (0, 0)))
    ncol = w_ada.shape[2]
    b_cols = lax.dynamic_slice(b_ada, (0, me * ncol), (1, ncol))
    mod_cols, act = _ada_fwd(c_all, w_ada[0], b_cols)
    mod_all = _allgather_small(mod_cols, "gather_mod")
    mod = lax.dynamic_index_in_dim(mod_all, me, axis=1, keepdims=False).reshape(6, D)

    (w_int,) = _allgather_weights([w_in[0].T.astype(BF16)])
    late = [w_branch_attn[0].T.astype(BF16), w_branch_conv[0].astype(BF16), w_out[0].astype(BF16),
            w_mlp_in[0].T.astype(BF16), w_mlp_out[0].astype(BF16)]
    w_int, late = lax.optimization_barrier((w_int, late))
    zones = [lax.dynamic_update_slice(lax.empty((N_DEV * t.shape[0], t.shape[1]), BF16), t, (me * t.shape[0], 0)) for t in late]
    ag = _split_start("gather_late_start", "gather", late, zones)

    def late_weights(o_attn):
        return _split_wait("gather_late_wait", "gather", ag[0], ag[1], ag[2], ag[3], o_attn)[1]

    rs = {}

    def scatter_start(key, grads):
        lands = [lax.empty((N_PEER, t.shape[0] // N_DEV, t.shape[1]), BF16) for t in grads]
        rs[key] = _split_start("scatter_%s_start" % key, "scatter", grads, lands)
        return rs[key][4]

    ba, bb = b_gate[:, :D], b_gate[:, D:]
    loss_part, grad_x, vec = _local_step(
        x2, tgt, mod + ag[4], g_norm_mix, g_norm_mlp, g_norm_final.reshape(1, D), ba, bb, cw8, w_int, late_weights,
        lambda g_mi, g_mo: scatter_start("mlp", [g_mi, g_mo]),
        lambda g_in, g_ba, g_bc, g_out: scatter_start("rest", [g_in, g_ba, g_bc, g_out]))
    loss = lax.psum(loss_part, AXES)

    vec_all = _allgather_small(vec, "gather_vec")
    vec_sum = _sum_parts(vec_all, "sum_vec")
    gm_all = vec_all[:, 0:6, :].reshape(N_DEV, 6 * D)
    gm_cols = lax.dynamic_slice(gm_all, (0, me * ncol), (N_DEV, ncol))
    g_w_ada = _ada_bwd(act.T, gm_cols)
    conv_cols = lax.dynamic_slice(vec_sum[11:14], (0, me * HEAD), (3, HEAD))
    g_pack = jnp.concatenate([vec_sum[0:11], jnp.pad(conv_cols, ((0, 0), (0, D - HEAD))), jnp.zeros((2, D), F32)], axis=0)
    packs = [_pack_vectors(*t) for t in ((b_ada, g_norm_mix, g_norm_mlp, g_norm_final, b_gate, conv_w),
                                         (m_b_ada, m_g_norm_mix, m_g_norm_mlp, m_g_norm_final, m_b_gate, m_conv_w),
                                         (v_b_ada, v_g_norm_mix, v_g_norm_mlp, v_g_norm_final, v_b_gate, v_conv_w))]
    d_pack, m_pack, v_pack = _adamw(packs[0], g_pack, packs[1], packs[2], "adamw_vectors")
    d_ada, nm_ada, nv_ada = _adamw(w_ada[0], g_w_ada, m_w_ada[0], v_w_ada[0], "adamw_w_ada")

    sums = {}
    for key, names in (("mlp", ("w_mi", "w_mo")), ("rest", ("w_in", "w_ba", "w_bc", "w_out"))):
        srcs, lands = _split_wait("scatter_%s_wait" % key, "scatter", rs[key][0], rs[key][1], rs[key][2], rs[key][3], grad_x)
        for n, g, land in zip(names, srcs, lands):
            r = land.shape[1]
            sums[n] = _sum_parts(land, "sum_" + n, own=lax.dynamic_slice(g, (me * r, 0), (r, g.shape[1])))
    g_in, g_ba, g_bc, g_out, g_mi, g_mo = sums["w_in"].T, sums["w_ba"].T, sums["w_bc"], sums["w_out"], sums["w_mi"].T, sums["w_mo"]
    big = {}
    for n, w, g, m, v in (("w_in", w_in, g_in, m_w_in, v_w_in), ("w_ba", w_branch_attn, g_ba, m_w_branch_attn, v_w_branch_attn),
                          ("w_bc", w_branch_conv, g_bc, m_w_branch_conv, v_w_branch_conv), ("w_out", w_out, g_out, m_w_out, v_w_out),
                          ("w_mi", w_mlp_in, g_mi, m_w_mlp_in, v_w_mlp_in), ("w_mo", w_mlp_out, g_mo, m_w_mlp_out, v_w_mlp_out)):
        big[n] = (g[None],) + tuple(t[None] for t in _adamw(w[0], g, m[0], v[0], "adamw_" + n))

    gv = _unpack_vectors(g_pack)
    dv = _unpack_vectors(d_pack)
    mv = _unpack_vectors(m_pack)
    vv = _unpack_vectors(v_pack)

    def ordered(k, ada, vecs):
        return (ada[None], vecs[0], vecs[1], big["w_in"][k], vecs[2], vecs[3], big["w_ba"][k], big["w_bc"][k],
                big["w_out"][k], vecs[4], big["w_mi"][k], big["w_mo"][k], vecs[5])

    return (loss, grad_x.reshape(1, S, D), *ordered(0, g_w_ada, gv), *ordered(1, d_ada, dv),
            *ordered(2, nm_ada, mv), *ordered(3, nv_ada, vv))
```

```python
import functools

import numpy as np
import jax
import jax.numpy as jnp
from jax import lax
from jax.experimental import pallas as pl
from jax.experimental.pallas import tpu as pltpu

F32, BF16 = jnp.float32, jnp.bfloat16
D = 1024
HEAD = 128
DILATIONS = (1, 4, 16)
N_SLOT = 4
AOW = N_SLOT * HEAD
DFF = 4 * D
N_DEV = 8
EPS = 1e-6
NEG = -1e30
SCALE = HEAD ** -0.5
LR, B1, B2, ADAM_EPS, WD, STEP = 0.001, 0.9, 0.999, 1e-08, 0.01, 10
V7X_VMEM_LIMIT = 56 * 1024 * 1024
TM = 1024
MESH = pl.DeviceIdType.MESH
AXES = ("x", "y", "c")


def _cparams(*sem):
    if sem:
        return pltpu.CompilerParams(dimension_semantics=sem, vmem_limit_bytes=V7X_VMEM_LIMIT)
    return pltpu.CompilerParams(vmem_limit_bytes=V7X_VMEM_LIMIT)


def _nn(a, b):
    return jnp.dot(a, b, preferred_element_type=F32)


def _nt(a, b):
    return lax.dot_general(a, b, (((1,), (1,)), ((), ())), preferred_element_type=F32)


def _tn(a, b):
    return lax.dot_general(a, b, (((0,), (0,)), ((), ())), preferred_element_type=F32)


def _rms_r(x):
    return lax.rsqrt(jnp.mean(x * x, axis=-1, keepdims=True) + EPS)


def _rms_bwd(x, r, g, dn):
    gy = dn * g
    dx = r * gy - x * (r * r * r) * jnp.mean(x * gy, axis=-1, keepdims=True)
    return dx, dn * (x * r)


def _sigmoid(t):
    return 1.0 / (1.0 + jnp.exp(-t))


def _rowsum(v):
    return jnp.sum(v, axis=0, keepdims=True)


def _vec_spec(n=D):
    return pl.BlockSpec((1, n), lambda *_: (0, 0))


def _const_spec(shape):
    nd = len(shape)
    return pl.BlockSpec(shape, lambda *_: (0,) * nd)


def _win_rowblock(j):
    return jnp.where(j < 9, (j % 3) * 3 + j // 3, j)


def _proj(x, g, sc, sh, w_int):
    S = x.shape[0]
    tm = TM

    def body(x_ref, g_ref, sc_ref, sh_ref, w_ref, h_ref, q_ref, e_ref):
        j = pl.program_id(1)

        @pl.when(j == 0)
        def _():
            xv = x_ref[...]
            h = xv * _rms_r(xv) * g_ref[...] * (1.0 + sc_ref[...]) + sh_ref[...]
            h_ref[...] = h.astype(BF16)

        acc = _nt(h_ref[...], w_ref[...])

        @pl.when(j < 9)
        def _():
            q_ref[0] = acc

        @pl.when(j >= 9)
        def _():
            e_ref[0] = acc.astype(BF16)

    def e_idx(i, j):
        k = jnp.maximum(j - 9, 0)
        return (k // 2, i, k % 2)

    return pl.pallas_call(
        body, name="proj", grid=(S // tm, 19),
        in_specs=[pl.BlockSpec((tm, D), lambda i, j: (i, 0)), _vec_spec(), _vec_spec(), _vec_spec(),
                  pl.BlockSpec((512, D), lambda i, j: (_win_rowblock(j), 0))],
        out_specs=[pl.BlockSpec((tm, D), lambda i, j: (i, 0)),
                   pl.BlockSpec((1, tm, 512), lambda i, j: (jnp.minimum(j, 8), i, 0)),
                   pl.BlockSpec((1, tm, 512), e_idx)],
        out_shape=[jax.ShapeDtypeStruct((S, D), BF16), jax.ShapeDtypeStruct((9, S, 512), F32),
                   jax.ShapeDtypeStruct((5, S, D), BF16)],
        compiler_params=_cparams("parallel", "arbitrary"),
    )(x, g, sc, sh, w_int)


def _bias_table():
    slopes = (2.0 ** (-8.0 * np.arange(1, 13, dtype=np.float32) / 12.0)).astype(np.float32)
    qi = np.arange(HEAD)[:, None]
    kj = np.arange(2 * HEAD)[None, :]
    delta = HEAD + qi - kj
    mask = (delta >= 0) & (delta <= HEAD)
    out = np.zeros((3, N_SLOT, HEAD, 2 * HEAD), np.float32)
    for gi, d in enumerate(DILATIONS):
        for j in range(N_SLOT):
            bias = -slopes[gi * N_SLOT + j] * (delta * d).astype(np.float32)
            out[gi, j] = np.where(mask, bias, NEG)
    return jnp.asarray(out)


def _block_rows(b, d):
    r = b % d
    n = b // d
    st = n * (HEAD * d) + r
    stp = jnp.maximum(n - 1, 0) * (HEAD * d) + r
    return n, st, stp


def _attn_fwd(qkv, bias):
    S = qkv.shape[2]
    nblk = S // HEAD

    def body(qkv_ref, b_ref, o_ref, lse_ref, m_s, l_s, a_s):
        g = pl.program_id(1)

        @pl.when(g == 0)
        def _():
            m_s[...] = jnp.full_like(m_s, NEG)
            l_s[...] = jnp.zeros_like(l_s)
            a_s[...] = jnp.zeros_like(a_s)

        bias = b_ref[0, 0]
        col = lax.broadcasted_iota(jnp.int32, bias.shape, 1)
        bias_first = jnp.where(col < HEAD, NEG, bias)

        for gi, d in enumerate(DILATIONS):
            @pl.when(g == gi)
            def _(d=d):
                def step(b, carry):
                    n, st, stp = _block_rows(b, d)
                    cur = pl.ds(st, HEAD, stride=d)
                    prv = pl.ds(stp, HEAD, stride=d)
                    q = qkv_ref.at[0, 0][cur, :].astype(BF16)
                    kw = jnp.concatenate([qkv_ref.at[0, 1][prv, :], qkv_ref.at[0, 1][cur, :]], axis=0).astype(BF16)
                    vw = jnp.concatenate([qkv_ref.at[0, 2][prv, :], qkv_ref.at[0, 2][cur, :]], axis=0).astype(BF16)
                    s = _nt(q, kw) * SCALE + jnp.where(n > 0, bias, bias_first)
                    m_old = m_s[cur, :]
                    m_new = jnp.maximum(m_old, jnp.max(s, axis=-1, keepdims=True))
                    alpha = jnp.exp(m_old - m_new)
                    p = jnp.exp(s - m_new[:, :1])
                    l_s[cur, :] = alpha * l_s[cur, :] + jnp.sum(p, axis=-1, keepdims=True)
                    a_s[cur, :] = alpha * a_s[cur, :] + _nn(p.astype(BF16), vw)
                    m_s[cur, :] = m_new
                    return carry

                lax.fori_loop(0, nblk, step, 0, unroll=4)

        @pl.when(g == len(DILATIONS) - 1)
        def _():
            l = l_s[...]
            o_ref[...] = a_s[...] / l
            lse_ref[...] = m_s[...] + jnp.log(l)

    return pl.pallas_call(
        body, name="attn_fwd", grid=(N_SLOT, 3),
        in_specs=[pl.BlockSpec((1, 3, S, HEAD), lambda j, g: (g, 0, 0, j)),
                  pl.BlockSpec((1, 1, HEAD, 2 * HEAD), lambda j, g: (g, j, 0, 0))],
        out_specs=[pl.BlockSpec((S, HEAD), lambda j, g: (0, j)), pl.BlockSpec((S, HEAD), lambda j, g: (0, j))],
        out_shape=[jax.ShapeDtypeStruct((S, AOW), F32), jax.ShapeDtypeStruct((S, AOW), F32)],
        scratch_shapes=[pltpu.VMEM((S, HEAD), F32)] * 3,
        compiler_params=_cparams("parallel", "arbitrary"),
    )(qkv, bias)


def _shift_down(z, k, halo_rows):
    out = pltpu.roll(z, k, axis=0)
    rid = lax.broadcasted_iota(jnp.int32, z.shape, 0)
    for t in range(k):
        out = jnp.where(rid == t, halo_rows[t], out)
    return out


def _shift_up(z, k, halo_rows):
    n = z.shape[0]
    out = pltpu.roll(z, n - k, axis=0)
    rid = lax.broadcasted_iota(jnp.int32, z.shape, 0)
    for t in range(k):
        out = jnp.where(rid == n - k + t, halo_rows[t], out)
    return out


def _e_spec(chunk, tm):
    return pl.BlockSpec((1, tm, D), lambda i, c=chunk: (c, i, 0))


def _e_prev_spec(chunk, tm):
    return pl.BlockSpec((1, 16, D), lambda i, c=chunk: (c, jnp.maximum(i * (tm // 16) - 1, 0), 0))


def _e_next_spec(chunk, tm, S):
    return pl.BlockSpec((1, 16, D), lambda i, c=chunk: (c, jnp.minimum((i + 1) * (tm // 16), S // 16 - 1), 0))


def _mix(o_attn, e, cw8, ba, bb, w_bat, w_bc):
    S = o_attn.shape[0]
    tm = 256

    def body(o_ref, cb_ref, cc_ref, cx_ref, ga_ref, gb_ref, ccp_ref, cxp_ref, cw_ref, ba_ref, bb_ref, wba_ref, wbc_ref,
             obf_ref, cbu_ref, ya_ref, yc_ref, mg_ref):
        i = pl.program_id(0)
        o = o_ref[...].astype(BF16)
        obf_ref[...] = o
        ya = _nt(o, wba_ref[...])
        z = cc_ref[0].astype(F32) * cx_ref[0].astype(F32)
        zp = ccp_ref[0].astype(F32) * cxp_ref[0].astype(F32) * (i > 0).astype(F32)
        z1 = _shift_down(z, 1, [zp[15:16]])
        z2 = _shift_down(z, 2, [zp[14:15], zp[15:16]])
        cw = cw_ref[...]
        u = cw[0:1] * z2 + cw[1:2] * z1 + cw[2:3] * z
        cbu = (cb_ref[0].astype(F32) * u).astype(BF16)
        cbu_ref[...] = cbu
        yc = _nn(cbu, wbc_ref[...])
        sa = _sigmoid(ga_ref[0].astype(F32) + ba_ref[...])
        sb = _sigmoid(gb_ref[0].astype(F32) + bb_ref[...])
        ya_ref[...] = ya.astype(BF16)
        yc_ref[...] = yc.astype(BF16)
        mg_ref[...] = (sa * ya + sb * yc).astype(BF16)

    row = lambda w: pl.BlockSpec((tm, w), lambda i: (i, 0))
    return pl.pallas_call(
        body, name="mix", grid=(S // tm,),
        in_specs=[row(AOW)] + [_e_spec(c, tm) for c in range(5)] + [_e_prev_spec(1, tm), _e_prev_spec(2, tm),
                  _const_spec((8, D)), _vec_spec(), _vec_spec(), _const_spec((D, AOW)), _const_spec((D, D))],
        out_specs=[row(AOW), row(D), row(D), row(D), row(D)],
        out_shape=[jax.ShapeDtypeStruct((S, AOW), BF16)] + [jax.ShapeDtypeStruct((S, D), BF16)] * 4,
        compiler_params=_cparams("parallel"),
    )(o_attn, e, e, e, e, e, e, e, cw8, ba, bb, w_bat, w_bc)


def _out_proj(merged, w_out, x, gate1, g_mlp, sc2, sh2):
    S = x.shape[0]
    tm = TM

    def body(mg_ref, w_ref, x_ref, gt_ref, g_ref, sc_ref, sh_ref, x1_ref, mo_ref, h2_ref):
        mo = _nn(mg_ref[...], w_ref[...])
        mo_ref[...] = mo.astype(BF16)
        x1 = x_ref[...] + gt_ref[...] * mo
        x1_ref[...] = x1
        h2 = x1 * _rms_r(x1) * g_ref[...] * (1.0 + sc_ref[...]) + sh_ref[...]
        h2_ref[...] = h2.astype(BF16)

    row = pl.BlockSpec((tm, D), lambda i: (i, 0))
    return pl.pallas_call(
        body, name="out_proj", grid=(S // tm,),
        in_specs=[row, _const_spec((D, D)), row, _vec_spec(), _vec_spec(), _vec_spec(), _vec_spec()],
        out_specs=[row, row, row],
        out_shape=[jax.ShapeDtypeStruct((S, D), F32), jax.ShapeDtypeStruct((S, D), BF16), jax.ShapeDtypeStruct((S, D), BF16)],
        compiler_params=_cparams("parallel"),
    )(merged, w_out, x, gate1, g_mlp, sc2, sh2)


def _mlp_in(h2, w_mit):
    S = h2.shape[0]
    tm, tn = TM, 1024

    def body(h_ref, w_ref, a_ref, f_ref):
        a = _nt(h_ref[...], w_ref[...])
        a_ref[...] = a.astype(BF16)
        f_ref[...] = jnp.square(jnp.maximum(a, 0.0)).astype(BF16)

    blk = pl.BlockSpec((tm, tn), lambda i, j: (i, j))
    return pl.pallas_call(
        body, name="mlp_in", grid=(S // tm, DFF // tn),
        in_specs=[pl.BlockSpec((tm, D), lambda i, j: (i, 0)), pl.BlockSpec((tn, D), lambda i, j: (j, 0))],
        out_specs=[blk, blk],
        out_shape=[jax.ShapeDtypeStruct((S, DFF), BF16)] * 2,
        compiler_params=_cparams("parallel", "parallel"),
    )(h2, w_mit)


def _mlp_out(f, w_mo, x1, gate2, g_fin, tgt):
    S = x1.shape[0]
    tm, tk = TM, 1024
    nk = DFF // tk

    def body(f_ref, w_ref, x1_ref, gt_ref, g_ref, t_ref, mlp_ref, dx2_ref, pv_ref, acc):
        i, k = pl.program_id(0), pl.program_id(1)

        @pl.when((i == 0) & (k == 0))
        def _():
            pv_ref[...] = jnp.zeros_like(pv_ref)

        @pl.when(k == 0)
        def _():
            acc[...] = jnp.zeros_like(acc)

        acc[...] += _nn(f_ref[...], w_ref[...])

        @pl.when(k == nk - 1)
        def _():
            mlp = acc[...]
            mlp_ref[...] = mlp.astype(BF16)
            x2 = x1_ref[...] + gt_ref[...] * mlp
            r = _rms_r(x2)
            g = g_ref[...]
            err = x2 * r * g - t_ref[...]
            dy = err * (1.0 / D)
            dx2, pg = _rms_bwd(x2, r, g, dy)
            dx2_ref[...] = dx2
            pv_ref[0:1, :] += _rowsum(pg)
            pv_ref[1:2, :] += 0.5 * _rowsum(jnp.mean(err * err, axis=-1, keepdims=True))

    row = pl.BlockSpec((tm, D), lambda i, k: (i, 0))
    return pl.pallas_call(
        body, name="mlp_out", grid=(S // tm, nk),
        in_specs=[pl.BlockSpec((tm, tk), lambda i, k: (i, k)), pl.BlockSpec((tk, D), lambda i, k: (k, 0)),
                  row, _vec_spec(), _vec_spec(), row],
        out_specs=[row, row, _const_spec((8, D))],
        out_shape=[jax.ShapeDtypeStruct((S, D), BF16), jax.ShapeDtypeStruct((S, D), F32), jax.ShapeDtypeStruct((8, D), F32)],
        scratch_shapes=[pltpu.VMEM((tm, D), F32)],
        compiler_params=_cparams("arbitrary", "arbitrary"),
    )(f, w_mo, x1, gate2, g_fin, tgt)


def _bwd_mlp_a(dx2, gate2, mlp, w_mo, a):
    S = dx2.shape[0]
    tm, tn = TM, 1024

    def body(dx_ref, gt_ref, mlp_ref, w_ref, a_ref, da_ref, dmo_ref, pv_ref):
        i, j = pl.program_id(0), pl.program_id(1)

        @pl.when((i == 0) & (j == 0))
        def _():
            pv_ref[...] = jnp.zeros_like(pv_ref)

        @pl.when(j == 0)
        def _():
            dx = dx_ref[...]
            dmo_ref[...] = (dx * gt_ref[...]).astype(BF16)
            pv_ref[0:1, :] += _rowsum(dx * mlp_ref[...].astype(F32))

        df = _nt(dmo_ref[...], w_ref[...])
        da_ref[...] = (df * (2.0 * jnp.maximum(a_ref[...].astype(F32), 0.0))).astype(BF16)

    row = pl.BlockSpec((tm, D), lambda i, j: (i, 0))
    blk = pl.BlockSpec((tm, tn), lambda i, j: (i, j))
    return pl.pallas_call(
        body, name="bwd_mlp_a", grid=(S // tm, DFF // tn),
        in_specs=[row, _vec_spec(), row, pl.BlockSpec((tn, D), lambda i, j: (j, 0)), blk],
        out_specs=[blk, row, _const_spec((8, D))],
        out_shape=[jax.ShapeDtypeStruct((S, DFF), BF16), jax.ShapeDtypeStruct((S, D), BF16), jax.ShapeDtypeStruct((8, D), F32)],
        compiler_params=_cparams("arbitrary", "arbitrary"),
    )(dx2, gate2, mlp, w_mo, a)


def _bwd_mlp_b(da, w_mit, x1, dx2, g_mlp, sc2):
    S = x1.shape[0]
    tm, tk = TM, 1024
    nk = DFF // tk

    def body(da_ref, w_ref, x1_ref, dx2_ref, g_ref, sc_ref, dx1_ref, pv_ref, acc):
        i, k = pl.program_id(0), pl.program_id(1)

        @pl.when((i == 0) & (k == 0))
        def _():
            pv_ref[...] = jnp.zeros_like(pv_ref)

        @pl.when(k == 0)
        def _():
            acc[...] = jnp.zeros_like(acc)

        acc[...] += _nn(da_ref[...], w_ref[...])

        @pl.when(k == nk - 1)
        def _():
            dh = acc[...]
            x1 = x1_ref[...]
            r = _rms_r(x1)
            g = g_ref[...]
            dxn, pg = _rms_bwd(x1, r, g, dh * (1.0 + sc_ref[...]))
            dx1_ref[...] = dx2_ref[...] + dxn
            pv_ref[0:1, :] += _rowsum(dh)
            pv_ref[1:2, :] += _rowsum(dh * (x1 * r * g))
            pv_ref[2:3, :] += _rowsum(pg)

    row = pl.BlockSpec((tm, D), lambda i, k: (i, 0))
    return pl.pallas_call(
        body, name="bwd_mlp_b", grid=(S // tm, nk),
        in_specs=[pl.BlockSpec((tm, tk), lambda i, k: (i, k)), pl.BlockSpec((tk, D), lambda i, k: (k, 0)),
                  row, row, _vec_spec(), _vec_spec()],
        out_specs=[row, _const_spec((8, D))],
        out_shape=[jax.ShapeDtypeStruct((S, D), F32), jax.ShapeDtypeStruct((8, D), F32)],
        scratch_shapes=[pltpu.VMEM((tm, D), F32)],
        compiler_params=_cparams("arbitrary", "arbitrary"),
    )(da, w_mit, x1, dx2, g_mlp, sc2)


def _bwd_mix(dx1, gate1, mo, e, cw8, ba, bb, ya, yc, o_attn, w_out, w_bc, w_bat):
    S = dx1.shape[0]
    tm = 256
    n_tiles = S // tm

    def body(dx_ref, dxn_ref, gt_ref, mo_ref, cb_ref, cc_ref, cx_ref, ga_ref, gb_ref, cbn_ref, gbn_ref, ccp_ref, cxp_ref,
             cw_ref, ba_ref, bb_ref, ya_ref, yc_ref, o_ref, wout_ref, wbc_ref, wba_ref,
             dmo_ref, dya_ref, dyc_ref, do_ref, dl_ref, de_ref, pv_ref):
        i = pl.program_id(0)

        @pl.when(i == 0)
        def _():
            pv_ref[...] = jnp.zeros_like(pv_ref)

        gate = gt_ref[...]
        bbv = bb_ref[...]

        def conv_branch_grad(dx_rows, gb_rows):
            dmo = (dx_rows * gate).astype(BF16)
            dmg = _nt(dmo, wout_ref[...])
            sb = _sigmoid(gb_rows + bbv)
            dyc = dmg * sb
            return dmo, dmg, sb, dyc, _nt(dyc.astype(BF16), wbc_ref[...])

        dx = dx_ref[...]
        cb = cb_ref[0].astype(F32)
        cc = cc_ref[0].astype(F32)
        cx = cx_ref[0].astype(F32)
        dmo, dmg, sb, dyc, dcbu = conv_branch_grad(dx, gb_ref[0].astype(F32))
        dmo_ref[...] = dmo
        pv_ref[0:1, :] += _rowsum(dx * mo_ref[...].astype(F32))
        sa = _sigmoid(ga_ref[0].astype(F32) + ba_ref[...])
        dya = (dmg * sa).astype(BF16)
        dya_ref[...] = dya
        dyc_ref[...] = dyc.astype(BF16)
        dga = dmg * ya_ref[...].astype(F32) * sa * (1.0 - sa)
        dgb = dmg * yc_ref[...].astype(F32) * sb * (1.0 - sb)
        pv_ref[1:2, :] += _rowsum(dga)
        pv_ref[2:3, :] += _rowsum(dgb)

        do = _nn(dya, wba_ref[...])
        do_ref[...] = do
        prod = do * o_ref[...]
        dl_ref[...] = jnp.concatenate(
            [jnp.broadcast_to(jnp.sum(prod[:, s * HEAD:(s + 1) * HEAD], axis=-1, keepdims=True), (tm, HEAD))
             for s in range(N_SLOT)], axis=1)

        z = cc * cx
        zp = ccp_ref[0].astype(F32) * cxp_ref[0].astype(F32) * (i > 0).astype(F32)
        z1 = _shift_down(z, 1, [zp[15:16]])
        z2 = _shift_down(z, 2, [zp[14:15], zp[15:16]])
        cw = cw_ref[...]
        u = cw[0:1] * z2 + cw[1:2] * z1 + cw[2:3] * z
        du = dcbu * cb
        dcbu_n = conv_branch_grad(dxn_ref[...], gbn_ref[0].astype(F32))[4]
        du_n = dcbu_n * cbn_ref[0].astype(F32) * (i < n_tiles - 1).astype(F32)
        du1 = _shift_up(du, 1, [du_n[0:1]])
        du2 = _shift_up(du, 2, [du_n[0:1], du_n[1:2]])
        dz = cw[2:3] * du + cw[1:2] * du1 + cw[0:1] * du2
        pv_ref[3:4, :] += _rowsum(du * z2)
        pv_ref[4:5, :] += _rowsum(du * z1)
        pv_ref[5:6, :] += _rowsum(du * z)

        de_ref[0] = (dcbu * u).astype(BF16)
        de_ref[1] = (dz * cx).astype(BF16)
        de_ref[2] = (dz * cc).astype(BF16)
        de_ref[3] = dga.astype(BF16)
        de_ref[4] = dgb.astype(BF16)

    row = lambda w: pl.BlockSpec((tm, w), lambda i: (i, 0))
    nxt = pl.BlockSpec((16, D), lambda i: (jnp.minimum((i + 1) * (tm // 16), S // 16 - 1), 0))
    return pl.pallas_call(
        body, name="bwd_mix", grid=(n_tiles,),
        in_specs=[row(D), nxt, _vec_spec(), row(D)] + [_e_spec(c, tm) for c in range(5)]
                 + [_e_next_spec(0, tm, S), _e_next_spec(4, tm, S), _e_prev_spec(1, tm), _e_prev_spec(2, tm),
                    _const_spec((8, D)), _vec_spec(), _vec_spec(), row(D), row(D), row(AOW),
                    _const_spec((D, D)), _const_spec((D, D)), _const_spec((D, AOW))],
        out_specs=[row(D), row(D), row(D), row(AOW), row(AOW), pl.BlockSpec((5, tm, D), lambda i: (0, i, 0)),
                   _const_spec((8, D))],
        out_shape=[jax.ShapeDtypeStruct((S, D), BF16)] * 3 + [jax.ShapeDtypeStruct((S, AOW), F32)] * 2
                  + [jax.ShapeDtypeStruct((5, S, D), BF16), jax.ShapeDtypeStruct((8, D), F32)],
        compiler_params=_cparams("arbitrary"),
    )(dx1, dx1, gate1, mo, e, e, e, e, e, e, e, e, e, cw8, ba, bb, ya, yc, o_attn, w_out, w_bc, w_bat)


def _attn_bwd(qkv, do, lse, dl, bias):
    S = qkv.shape[2]
    nblk = S // HEAD

    def body(qkv_ref, do_ref, lse_ref, dl_ref, b_ref, d_ref):
        g = pl.program_id(1)
        d_ref[...] = jnp.zeros_like(d_ref)
        bias = b_ref[0, 0]
        col = lax.broadcasted_iota(jnp.int32, bias.shape, 1)
        bias_first = jnp.where(col < HEAD, NEG, bias)

        for gi, d in enumerate(DILATIONS):
            @pl.when(g == gi)
            def _(d=d):
                def step(b, carry):
                    n, st, stp = _block_rows(b, d)
                    cur = pl.ds(st, HEAD, stride=d)
                    prv = pl.ds(stp, HEAD, stride=d)
                    q = qkv_ref.at[0, 0][cur, :].astype(BF16)
                    kw = jnp.concatenate([qkv_ref.at[0, 1][prv, :], qkv_ref.at[0, 1][cur, :]], axis=0).astype(BF16)
                    vw = jnp.concatenate([qkv_ref.at[0, 2][prv, :], qkv_ref.at[0, 2][cur, :]], axis=0).astype(BF16)
                    s = _nt(q, kw) * SCALE + jnp.where(n > 0, bias, bias_first)
                    p = jnp.exp(s - lse_ref[cur, :][:, :1])
                    dob = do_ref[cur, :].astype(BF16)
                    dvw = _tn(p.astype(BF16), dob)
                    dp = _nt(dob, vw)
                    ds = (p * (dp - dl_ref[cur, :][:, :1]) * SCALE).astype(BF16)
                    d_ref.at[0, 0][cur, :] = _nn(ds, kw)
                    dkw = _tn(ds, q)
                    d_ref.at[0, 1][cur, :] += dkw[HEAD:]
                    d_ref.at[0, 1][prv, :] += dkw[:HEAD]
                    d_ref.at[0, 2][cur, :] += dvw[HEAD:]
                    d_ref.at[0, 2][prv, :] += dvw[:HEAD]
                    return carry

                lax.fori_loop(0, nblk, step, 0, unroll=4)

    col_blk = pl.BlockSpec((S, HEAD), lambda j, g: (0, j))
    qkv_blk = pl.BlockSpec((1, 3, S, HEAD), lambda j, g: (g, 0, 0, j))
    return pl.pallas_call(
        body, name="attn_bwd", grid=(N_SLOT, 3),
        in_specs=[qkv_blk, col_blk, col_blk, col_blk, pl.BlockSpec((1, 1, HEAD, 2 * HEAD), lambda j, g: (g, j, 0, 0))],
        out_specs=qkv_blk,
        out_shape=jax.ShapeDtypeStruct((3, 3, S, AOW), F32),
        compiler_params=_cparams("parallel", "arbitrary"),
    )(qkv, do, lse, dl, bias)


def _bwd_in(dqkv, de, w_int, x, dx1, g_mix, sc1):
    S = x.shape[0]
    tm = TM

    def body(dq_ref, de_ref, w_ref, x_ref, dx1_ref, g_ref, sc_ref, gx_ref, pv_ref, acc):
        i, k = pl.program_id(0), pl.program_id(1)

        @pl.when((i == 0) & (k == 0))
        def _():
            pv_ref[...] = jnp.zeros_like(pv_ref)

        @pl.when(k == 0)
        def _():
            acc[...] = jnp.zeros_like(acc)

        @pl.when(k < 9)
        def _():
            acc[...] += _nn(dq_ref[0].astype(BF16), w_ref[...])

        @pl.when(k >= 9)
        def _():
            acc[...] += _nn(de_ref[0], w_ref[...])

        @pl.when(k == 18)
        def _():
            dh = acc[...]
            xv = x_ref[...]
            r = _rms_r(xv)
            g = g_ref[...]
            dxn, pg = _rms_bwd(xv, r, g, dh * (1.0 + sc_ref[...]))
            gx_ref[...] = dx1_ref[...] + dxn
            pv_ref[0:1, :] += _rowsum(dh)
            pv_ref[1:2, :] += _rowsum(dh * (xv * r * g))
            pv_ref[2:3, :] += _rowsum(pg)

    def e_idx(i, k):
        kk = jnp.maximum(k - 9, 0)
        return (kk // 2, i, kk % 2)

    row = pl.BlockSpec((tm, D), lambda i, k: (i, 0))
    return pl.pallas_call(
        body, name="bwd_in", grid=(S // tm, 19),
        in_specs=[pl.BlockSpec((1, tm, 512), lambda i, k: (jnp.minimum(k, 8), i, 0)), pl.BlockSpec((1, tm, 512), e_idx),
                  pl.BlockSpec((512, D), lambda i, k: (_win_rowblock(k), 0)), row, row, _vec_spec(), _vec_spec()],
        out_specs=[row, _const_spec((8, D))],
        out_shape=[jax.ShapeDtypeStruct((S, D), F32), jax.ShapeDtypeStruct((8, D), F32)],
        scratch_shapes=[pltpu.VMEM((tm, D), F32)],
        compiler_params=_cparams("arbitrary", "arbitrary"),
    )(dqkv, de, w_int, x, dx1, g_mix, sc1)


def _grad_w(name, a, b):
    S, ka = a.shape
    nb = b.shape[1]
    tt = 1024
    nt = S // tt

    def body(a_ref, b_ref, o_ref, acc):
        t = pl.program_id(1)

        @pl.when(t == 0)
        def _():
            acc[...] = jnp.zeros_like(acc)

        acc[...] += _tn(a_ref[...], b_ref[...])

        @pl.when(t == nt - 1)
        def _():
            o_ref[...] = acc[...].astype(BF16)

    return pl.pallas_call(
        body, name=name, grid=(ka // 512, nt),
        in_specs=[pl.BlockSpec((tt, 512), lambda n, t: (t, n)), pl.BlockSpec((tt, nb), lambda n, t: (t, 0))],
        out_specs=pl.BlockSpec((512, nb), lambda n, t: (n, 0)),
        out_shape=jax.ShapeDtypeStruct((ka, nb), BF16),
        scratch_shapes=[pltpu.VMEM((512, nb), F32)],
        compiler_params=_cparams("parallel", "arbitrary"),
    )(a, b)


def _grad_w_in(dqkv, de, h):
    S = h.shape[0]
    tt = 1024
    nt = S // tt

    def body(dq_ref, de_ref, h_ref, o_ref, acc):
        n, t = pl.program_id(0), pl.program_id(1)

        @pl.when(t == 0)
        def _():
            acc[...] = jnp.zeros_like(acc)

        @pl.when(n < 9)
        def _():
            acc[...] += _tn(dq_ref[0].astype(BF16), h_ref[...])

        @pl.when(n >= 9)
        def _():
            acc[...] += _tn(de_ref[0], h_ref[...])

        @pl.when(t == nt - 1)
        def _():
            o_ref[...] = acc[...].astype(BF16)

    def e_idx(n, t):
        kk = jnp.maximum(n - 9, 0)
        return (kk // 2, jnp.where(n >= 9, t, 0), kk % 2)

    return pl.pallas_call(
        body, name="grad_w_in", grid=(19, nt),
        in_specs=[pl.BlockSpec((1, tt, 512), lambda n, t: (jnp.minimum(n, 8), jnp.where(n < 9, t, nt - 1), 0)),
                  pl.BlockSpec((1, tt, 512), e_idx), pl.BlockSpec((tt, D), lambda n, t: (t, 0))],
        out_specs=pl.BlockSpec((512, D), lambda n, t: (_win_rowblock(n), 0)),
        out_shape=jax.ShapeDtypeStruct((19 * 512, D), BF16),
        scratch_shapes=[pltpu.VMEM((512, D), F32)],
        compiler_params=_cparams("parallel", "arbitrary"),
    )(dqkv, de, h)


def _local_step(x, tgt, mod, g_mix, g_mlp, g_fin, ba, bb, cw8, w_int, late_weights, mlp_grads_ready, other_grads_ready):
    S = x.shape[0]
    sh1, sc1, gt1, sh2, sc2, gt2 = [mod[k:k + 1] for k in range(6)]
    bias = _bias_table()

    h, qkv, e = _proj(x, g_mix, sc1, sh1, w_int)
    qkv = qkv.reshape(3, 3, S, AOW)
    o_attn, lse = _attn_fwd(qkv, bias)
    w_bat, w_bc, w_out, w_mit, w_mo = late_weights(o_attn)
    o_bf, cbu, ya, yc, merged = _mix(o_attn, e, cw8, ba, bb, w_bat, w_bc)
    x1, mo, h2 = _out_proj(merged, w_out, x, gt1, g_mlp, sc2, sh2)
    a, f = _mlp_in(h2, w_mit)
    mlp, dx2, pv_f = _mlp_out(f, w_mo, x1, gt2, g_fin, tgt)

    da, dmo2, pv_a = _bwd_mlp_a(dx2, gt2, mlp, w_mo, a)
    dx1, pv_b = _bwd_mlp_b(da, w_mit, x1, dx2, g_mlp, sc2)
    zero = mlp_grads_ready(_grad_w("grad_w_mi", da, h2), _grad_w("grad_w_mo", f, dmo2))
    dmo, dya, dyc, do, dl, de, pv_m = _bwd_mix(dx1, gt1 + zero, mo, e, cw8, ba, bb, ya, yc, o_attn, w_out, w_bc, w_bat)
    dqkv = _attn_bwd(qkv, do, lse, dl, bias).reshape(9, S, AOW)
    zero = other_grads_ready(_grad_w_in(dqkv, de, h), _grad_w("grad_w_ba", dya, o_bf), _grad_w("grad_w_bc", cbu, dyc),
                             _grad_w("grad_w_out", merged, dmo))
    grad_x, pv_i = _bwd_in(dqkv, de, w_int, x, dx1, g_mix, sc1 + zero)

    vec = jnp.concatenate([pv_i[0:2], pv_m[0:1], pv_b[0:2], pv_a[0:1], pv_i[2:3], pv_b[2:3], pv_f[0:1],
                           pv_m[1:3], pv_m[3:6], jnp.zeros((2, D), F32)], axis=0)
    return pv_f[1, 0], grad_x, vec


def _my_place():
    return lax.axis_index("x"), lax.axis_index("y"), lax.axis_index("c")


def _dev_index(px, py, pc):
    return 4 * px + 2 * py + pc


def _allgather_weights(shards):
    nw = len(shards)
    HBM = pl.BlockSpec(memory_space=pl.ANY)

    def body(*refs):
        sh, full = refs[:nw], refs[nw:2 * nw]
        send_sems, recv_sems, local_sems = refs[2 * nw:]
        x, y, c = _my_place()
        me, sibling = (x, y, c), (x, y, 1 - c)
        chips = [(1 - x, y), (x, 1 - y), (1 - x, 1 - y)]

        def rows(w, px, py, pc):
            r = sh[w].shape[0]
            return full[w].at[pl.ds(pl.multiple_of(_dev_index(px, py, pc) * r, 16), r), :]

        def copy(w, k, block, to, src=None):
            return pltpu.make_async_remote_copy(
                src_ref=rows(w, *block) if src is None else src, dst_ref=rows(w, *block),
                send_sem=send_sems.at[w, k], recv_sem=recv_sems.at[w, k], device_id=to, device_id_type=MESH)

        mine = [pltpu.make_async_copy(sh[w], rows(w, *me), local_sems.at[w]) for w in range(nw)]
        for cp in mine:
            cp.start()
        first = []
        for w in range(nw):
            first.append(copy(w, 0, me, sibling, src=sh[w]))
            first += [copy(w, 1 + j, me, (*chip, c), src=sh[w]) for j, chip in enumerate(chips)]
        for cp in first:
            cp.start()
        passed = []
        for w in range(nw):
            for j, chip in enumerate(chips):
                copy(w, 1 + j, (*chip, c), me).wait_recv()
                fwd = copy(w, 4 + j, (*chip, c), sibling)
                fwd.start()
                passed.append(fwd)
        for w in range(nw):
            copy(w, 0, sibling, me).wait_recv()
            for j, chip in enumerate(chips):
                copy(w, 4 + j, (*chip, 1 - c), me).wait_recv()
        for cp in first + passed:
            cp.wait_send()
        for cp in mine:
            cp.wait()

    return pl.pallas_call(
        body, name="allgather_weights",
        out_shape=[jax.ShapeDtypeStruct((N_DEV * s.shape[0], s.shape[1]), s.dtype) for s in shards],
        in_specs=[HBM] * nw, out_specs=[HBM] * nw,
        scratch_shapes=[pltpu.SemaphoreType.DMA((nw, 7)), pltpu.SemaphoreType.DMA((nw, 7)), pltpu.SemaphoreType.DMA((nw,))],
    )(*shards)


def _peer(x, y, c, m):
    return (x ^ ((m >> 2) & 1), y ^ ((m >> 1) & 1), c ^ (m & 1))


HBM_SPEC = pl.BlockSpec(memory_space=pltpu.HBM)
SEM_SPEC = pl.BlockSpec(memory_space=pltpu.SEMAPHORE)
N_PEER = N_DEV - 1


SPLIT_MASKS = {"gather": tuple(range(1, N_DEV)), "scatter": tuple(range(1, N_DEV)), "chips": (2, 4, 6)}


def _split_copy(mode, src_ref, land_ref, send_sems, recv_sems, w, j, place, arriving=False):
    x, y, c = place
    masks = SPLIT_MASKS[mode]
    peer = _peer(x, y, c, masks[j])
    k = w * len(masks) + j
    sender, receiver = ((peer, (x, y, c)) if arriving else ((x, y, c), peer))
    if mode == "gather":
        r = src_ref.shape[0]
        src, dst = src_ref, land_ref.at[pl.ds(pl.multiple_of(_dev_index(*sender) * r, 16), r), :]
    elif mode == "scatter":
        r = land_ref.shape[1]
        src, dst = src_ref.at[pl.ds(pl.multiple_of(_dev_index(*receiver) * r, 16), r), :], land_ref.at[j]
    else:
        src, dst = src_ref.at[2 * receiver[0] + receiver[1]], land_ref.at[j]
    return pltpu.make_async_remote_copy(src_ref=src, dst_ref=dst, send_sem=send_sems.at[k], recv_sem=recv_sems.at[k],
                                        device_id=peer, device_id_type=MESH)


def _split_start(name, mode, srcs, lands):
    n = len(srcs)
    nm = len(SPLIT_MASKS[mode])

    def body(*refs):
        src, land = refs[:n], refs[n:2 * n]
        send_sems, recv_sems = refs[2 * n], refs[2 * n + 1]
        token = refs[-1]
        place = _my_place()
        for w in range(n):
            for j in range(nm):
                _split_copy(mode, src[w], land[w], send_sems, recv_sems, w, j, place).start()
        token[...] = jnp.zeros_like(token)

    hbm = lambda t: pltpu.HBM(t.shape, t.dtype)
    out = pl.pallas_call(
        body, name=name,
        out_shape=(pltpu.SemaphoreType.DMA((n * nm,)), pltpu.SemaphoreType.DMA((n * nm,)), *[hbm(t) for t in srcs],
                   *[hbm(t) for t in lands], jax.ShapeDtypeStruct((8, 128), F32)),
        in_specs=(HBM_SPEC,) * (2 * n),
        out_specs=(SEM_SPEC, SEM_SPEC) + (HBM_SPEC,) * (2 * n) + (pl.BlockSpec(memory_space=pltpu.VMEM),),
        input_output_aliases={i: 2 + i for i in range(2 * n)},
        compiler_params=pltpu.CompilerParams(has_side_effects=pltpu.SideEffectType.DATAFLOW_SIDE_EFFECTING),
    )(*[pltpu.with_memory_space_constraint(t, pltpu.HBM) for t in (*srcs, *lands)])
    return out[0], out[1], out[2:2 + n], out[2 + n:2 + 2 * n], out[-1][0:1, 0:1]


def _split_wait(name, mode, send_sems, recv_sems, srcs, lands, after):
    n = len(srcs)

    def body(*refs):
        src, land = refs[:n], refs[n:2 * n]
        ssem, rsem = refs[2 * n], refs[2 * n + 1]
        place = _my_place()
        for w in range(n):
            for j in range(len(SPLIT_MASKS[mode])):
                _split_copy(mode, src[w], land[w], ssem, rsem, w, j, place).wait_send()
                _split_copy(mode, src[w], land[w], ssem, rsem, w, j, place, arriving=True).wait_recv()

    hbm = lambda t: pltpu.HBM(t.shape, t.dtype)
    out = pl.pallas_call(
        body, name=name,
        out_shape=tuple(hbm(t) for t in (*srcs, *lands)),
        in_specs=(HBM_SPEC,) * (2 * n) + (SEM_SPEC, SEM_SPEC, pl.BlockSpec(memory_space=pl.ANY)),
        out_specs=(HBM_SPEC,) * (2 * n),
        input_output_aliases={i: i for i in range(2 * n)},
        compiler_params=pltpu.CompilerParams(has_side_effects=pltpu.SideEffectType.DATAFLOW_SIDE_EFFECTING),
    )(*srcs, *lands, send_sems, recv_sems, after)
    return out[:n], out[n:]


def _sibling_exchange(grads):
    nw = len(grads)
    HBM = pl.BlockSpec(memory_space=pl.ANY)

    def body(*refs):
        g, land = refs[:nw], refs[nw:2 * nw]
        send_sems, recv_sems = refs[2 * nw:]
        x, y, c = _my_place()

        def copy(w, q, owner_core):
            r = land[w].shape[1]
            return pltpu.make_async_remote_copy(
                src_ref=g[w].at[pl.ds(pl.multiple_of((2 * q + owner_core) * r, 16), r), :], dst_ref=land[w].at[q],
                send_sem=send_sems.at[w, q], recv_sem=recv_sems.at[w, q], device_id=(x, y, 1 - c), device_id_type=MESH)

        sends = [copy(w, q, 1 - c) for w in range(nw) for q in range(4)]
        for cp in sends:
            cp.start()
        for w in range(nw):
            for q in range(4):
                copy(w, q, c).wait_recv()
        for cp in sends:
            cp.wait_send()

    return pl.pallas_call(
        body, name="sibling_exchange",
        out_shape=[jax.ShapeDtypeStruct((4, a.shape[0] // N_DEV, a.shape[1]), a.dtype) for a in grads],
        in_specs=[HBM] * nw, out_specs=[HBM] * nw,
        scratch_shapes=[pltpu.SemaphoreType.DMA((nw, 4)), pltpu.SemaphoreType.DMA((nw, 4))],
    )(*grads)


def _pair_sum(g, sib, core, name):
    _, r, ccols = sib.shape
    tr = _row_tile(r)

    def body(core_ref, g_ref, s_ref, o_ref):
        o_ref[0] = (g_ref[0, 0].astype(F32) + s_ref[0].astype(F32)).astype(BF16)

    return pl.pallas_call(
        body, name=name,
        grid_spec=pltpu.PrefetchScalarGridSpec(
            num_scalar_prefetch=1, grid=(4, r // tr),
            in_specs=[pl.BlockSpec((1, 1, tr, ccols), lambda q, i, core_ref: (q, core_ref[0], i, 0)),
                      pl.BlockSpec((1, tr, ccols), lambda q, i, core_ref: (q, i, 0))],
            out_specs=pl.BlockSpec((1, tr, ccols), lambda q, i, core_ref: (q, i, 0))),
        out_shape=jax.ShapeDtypeStruct(sib.shape, BF16),
        compiler_params=_cparams("parallel", "parallel"),
    )(core, g.reshape(4, 2, r, ccols), sib)


def _allgather_small(v, name):
    r, ccols = v.shape

    def body(v_ref, out_ref, send_sems, recv_sems):
        x, y, c = _my_place()
        my_idx = _dev_index(x, y, c)
        out_ref[my_idx] = v_ref[...]

        def copy(m):
            peer = _peer(x, y, c, m)
            return pltpu.make_async_remote_copy(
                src_ref=v_ref, dst_ref=out_ref.at[my_idx],
                send_sem=send_sems.at[m - 1], recv_sem=recv_sems.at[m - 1], device_id=peer, device_id_type=MESH)

        def arrival(m):
            peer = _peer(x, y, c, m)
            return pltpu.make_async_remote_copy(
                src_ref=v_ref, dst_ref=out_ref.at[_dev_index(*peer)],
                send_sem=send_sems.at[m - 1], recv_sem=recv_sems.at[m - 1], device_id=peer, device_id_type=MESH)

        sends = [copy(m) for m in range(1, N_DEV)]
        for cp in sends:
            cp.start()
        for m in range(1, N_DEV):
            arrival(m).wait_recv()
        for cp in sends:
            cp.wait_send()

    return pl.pallas_call(
        body, name=name,
        out_shape=jax.ShapeDtypeStruct((N_DEV, r, ccols), v.dtype),
        in_specs=[pl.BlockSpec(memory_space=pltpu.VMEM)], out_specs=pl.BlockSpec(memory_space=pltpu.VMEM),
        scratch_shapes=[pltpu.SemaphoreType.DMA((7,)), pltpu.SemaphoreType.DMA((7,))],
    )(v)


def _ada_fwd(c_all, w_ada, b_cols):
    def body(c_ref, w_ref, b_ref, mod_ref, act_ref):
        cv = c_ref[...]
        act = cv * _sigmoid(cv)
        act_ref[...] = act
        mod_ref[...] = jnp.dot(act, w_ref[...], preferred_element_type=F32, precision=lax.Precision.HIGHEST) + b_ref[...]

    return pl.pallas_call(
        body, name="ada_fwd",
        out_shape=[jax.ShapeDtypeStruct((N_DEV, w_ada.shape[1]), F32), jax.ShapeDtypeStruct((N_DEV, D), F32)],
        compiler_params=_cparams(),
    )(c_all, w_ada, b_cols)


def _ada_bwd(act_t, gm_cols):
    def body(a_ref, g_ref, o_ref):
        o_ref[...] = jnp.dot(a_ref[...], g_ref[...], preferred_element_type=F32, precision=lax.Precision.HIGHEST)

    return pl.pallas_call(
        body, name="ada_bwd", out_shape=jax.ShapeDtypeStruct((D, gm_cols.shape[1]), F32), compiler_params=_cparams(),
    )(act_t, gm_cols)


def _row_tile(r):
    for t in (256, 304, 128, 64, 16):
        if r % t == 0:
            return t
    return r


def _sum_parts(parts, name, own=None):
    k, r, ccols = parts.shape
    tr = _row_tile(r)

    def body(*refs):
        p_ref, o_ref = refs[0], refs[-1]
        acc = p_ref[0].astype(F32) if own is None else refs[1][...].astype(F32) + p_ref[0].astype(F32)
        for s in range(1, k):
            acc = acc + p_ref[s].astype(F32)
        o_ref[...] = acc

    blk = pl.BlockSpec((tr, ccols), lambda i: (i, 0))
    return pl.pallas_call(
        body, name=name, grid=(r // tr,),
        in_specs=[pl.BlockSpec((k, tr, ccols), lambda i: (0, i, 0))] + ([] if own is None else [blk]),
        out_specs=blk,
        out_shape=jax.ShapeDtypeStruct((r, ccols), F32),
        compiler_params=_cparams("parallel"),
    )(*((parts,) if own is None else (parts, own)))


def _adamw(w, g, m, v, name):
    r, ccols = w.shape
    tr = _row_tile(r)
    c1 = 1.0 / (1.0 - B1 ** STEP)
    c2 = 1.0 / (1.0 - B2 ** STEP)

    def body(w_ref, g_ref, m_ref, v_ref, d_ref, nm_ref, nv_ref):
        gv = g_ref[...]
        nm = B1 * m_ref[...] + (1.0 - B1) * gv
        nv = B2 * v_ref[...] + (1.0 - B2) * jnp.square(gv)
        nm_ref[...] = nm
        nv_ref[...] = nv
        d_ref[...] = -LR * ((nm * c1) / (jnp.sqrt(nv * c2) + ADAM_EPS) + WD * w_ref[...])

    blk = pl.BlockSpec((tr, ccols), lambda i: (i, 0))
    return pl.pallas_call(
        body, name=name, grid=(r // tr,), in_specs=[blk] * 4, out_specs=[blk] * 3,
        out_shape=[jax.ShapeDtypeStruct((r, ccols), F32)] * 3,
        compiler_params=_cparams("parallel"),
    )(w, g, m, v)


def _pack_vectors(b_ada, g_mix, g_mlp, g_fin, b_gate, conv_w):
    conv_rows = jnp.pad(conv_w.reshape(3, HEAD), ((0, 0), (0, D - HEAD)))
    return jnp.concatenate([b_ada.reshape(6, D), g_mix.reshape(1, D), g_mlp.reshape(1, D), g_fin.reshape(1, D),
                            b_gate.reshape(2, D), conv_rows, jnp.zeros((2, D), F32)], axis=0)


def _unpack_vectors(p):
    return (p[0:6].reshape(1, 6 * D), p[6:7], p[9:11].reshape(1, 2 * D), p[11:14, :HEAD].reshape(1, 3, HEAD),
            p[7:8], p[8])


def kernel(x, c, w_ada, b_ada, g_norm_mix, w_in, b_gate, conv_w, w_branch_attn, w_branch_conv, w_out, g_norm_mlp, w_mlp_in, w_mlp_out, g_norm_final, loss_target, m_w_ada, m_b_ada, m_g_norm_mix, m_w_in, m_b_gate, m_conv_w, m_w_branch_attn, m_w_branch_conv, m_w_out, m_g_norm_mlp, m_w_mlp_in, m_w_mlp_out, m_g_norm_final, v_w_ada, v_b_ada, v_g_norm_mix, v_w_in, v_b_gate, v_conv_w, v_w_branch_attn, v_w_branch_conv, v_w_out, v_g_norm_mlp, v_w_mlp_in, v_w_mlp_out, v_g_norm_final):
    S = x.shape[1]
    xi, yi, ci = _my_place()
    me = _dev_index(xi, yi, ci)
    x2 = x.reshape(S, D)
    tgt = loss_target.reshape(S, D)

    pay = jnp.zeros((8, D), F32).at[0].set(c[0]).at[1:4, :HEAD].set(conv_w[0])
    got = _allgather_small(pay, "gather_cond")
    c_all = got[:, 0, :]
    cw8 = jnp.pad(got[:, 1:4, :HEAD].transpose(1, 0, 2).reshape(3, D), ((0, 5), (0, 0)))
    ncol = w_ada.shape[2]
    b_cols = lax.dynamic_slice(b_ada, (0, me * ncol), (1, ncol))
    mod_cols, act = _ada_fwd(c_all, w_ada[0], b_cols)
    mod_all = _allgather_small(mod_cols, "gather_mod")

    w_in_shard, mod_all = lax.optimization_barrier((w_in[0].T.astype(BF16), mod_all))
    mod = lax.dynamic_index_in_dim(mod_all, me, axis=1, keepdims=False).reshape(6, D)
    (w_int,) = _allgather_weights([w_in_shard])
    late = [w_branch_attn[0].T.astype(BF16), w_branch_conv[0].astype(BF16), w_out[0].astype(BF16),
            w_mlp_in[0].T.astype(BF16), w_mlp_out[0].astype(BF16)]
    w_int, late = lax.optimization_barrier((w_int, late))
    zones = [lax.dynamic_update_slice(lax.empty((N_DEV * t.shape[0], t.shape[1]), BF16), t, (me * t.shape[0], 0)) for t in late]
    ag = _split_start("gather_late_start", "gather", late, zones)

    def late_weights(o_attn):
        return _split_wait("gather_late_wait", "gather", ag[0], ag[1], ag[2], ag[3], o_attn)[1]

    rs = {}

    def mlp_grads_ready(*grads):
        lands = [lax.empty((N_PEER, t.shape[0] // N_DEV, t.shape[1]), BF16) for t in grads]
        rs["mlp"] = _split_start("scatter_mlp_start", "scatter", grads, lands)
        return rs["mlp"][4]

    def other_grads_ready(*grads):
        core = ci.reshape(1).astype(jnp.int32)
        pair = [_pair_sum(g, sib, core, "pair_sum_%d" % k) for k, (g, sib) in enumerate(zip(grads, _sibling_exchange(grads)))]
        lands = [lax.empty((3,) + t.shape[1:], BF16) for t in pair]
        rs["rest"] = _split_start("scatter_rest_start", "chips", pair, lands)
        return rs["rest"][4]

    ba, bb = b_gate[:, :D], b_gate[:, D:]
    loss_part, grad_x, vec = _local_step(
        x2, tgt, mod + ag[4], g_norm_mix, g_norm_mlp, g_norm_final.reshape(1, D), ba, bb, cw8, w_int, late_weights,
        mlp_grads_ready, other_grads_ready)
    loss = lax.psum(loss_part, AXES)

    vec_all = _allgather_small(vec, "gather_vec")
    vec_sum = _sum_parts(vec_all, "sum_vec")
    gm_all = vec_all[:, 0:6, :].reshape(N_DEV, 6 * D)
    gm_cols = lax.dynamic_slice(gm_all, (0, me * ncol), (N_DEV, ncol))
    g_w_ada = _ada_bwd(act.T, gm_cols)
    conv_cols = lax.dynamic_slice(vec_sum[11:14], (0, me * HEAD), (3, HEAD))
    g_pack = jnp.concatenate([vec_sum[0:11], jnp.pad(conv_cols, ((0, 0), (0, D - HEAD))), jnp.zeros((2, D), F32)], axis=0)
    packs = [_pack_vectors(*t) for t in ((b_ada, g_norm_mix, g_norm_mlp, g_norm_final, b_gate, conv_w),
                                         (m_b_ada, m_g_norm_mix, m_g_norm_mlp, m_g_norm_final, m_b_gate, m_conv_w),
                                         (v_b_ada, v_g_norm_mix, v_g_norm_mlp, v_g_norm_final, v_b_gate, v_conv_w))]
    d_pack, m_pack, v_pack = _adamw(packs[0], g_pack, packs[1], packs[2], "adamw_vectors")
    d_ada, nm_ada, nv_ada = _adamw(w_ada[0], g_w_ada, m_w_ada[0], v_w_ada[0], "adamw_w_ada")

    sums = {}
    srcs, lands = _split_wait("scatter_mlp_wait", "scatter", *rs["mlp"][:4], grad_x)
    for n, g, land in zip(("w_mi", "w_mo"), srcs, lands):
        r = land.shape[1]
        sums[n] = _sum_parts(land, "sum_" + n, own=lax.dynamic_slice(g, (me * r, 0), (r, g.shape[1])))
    srcs, lands = _split_wait("scatter_rest_wait", "chips", *rs["rest"][:4], grad_x)
    for n, pair, land in zip(("w_in", "w_ba", "w_bc", "w_out"), srcs, lands):
        sums[n] = _sum_parts(land, "sum_" + n, own=lax.dynamic_index_in_dim(pair, 2 * xi + yi, axis=0, keepdims=False))
    g_in, g_ba, g_bc, g_out, g_mi, g_mo = sums["w_in"].T, sums["w_ba"].T, sums["w_bc"], sums["w_out"], sums["w_mi"].T, sums["w_mo"]
    big = {}
    for n, w, g, m, v in (("w_in", w_in, g_in, m_w_in, v_w_in), ("w_ba", w_branch_attn, g_ba, m_w_branch_attn, v_w_branch_attn),
                          ("w_bc", w_branch_conv, g_bc, m_w_branch_conv, v_w_branch_conv), ("w_out", w_out, g_out, m_w_out, v_w_out),
                          ("w_mi", w_mlp_in, g_mi, m_w_mlp_in, v_w_mlp_in), ("w_mo", w_mlp_out, g_mo, m_w_mlp_out, v_w_mlp_out)):
        big[n] = (g[None],) + tuple(t[None] for t in _adamw(w[0], g, m[0], v[0], "adamw_" + n))

    gv = _unpack_vectors(g_pack)
    dv = _unpack_vectors(d_pack)
    mv = _unpack_vectors(m_pack)
    vv = _unpack_vectors(v_pack)

    def ordered(k, ada, vecs):
        return (ada[None], vecs[0], vecs[1], big["w_in"][k], vecs[2], vecs[3], big["w_ba"][k], big["w_bc"][k],
                big["w_out"][k], vecs[4], big["w_mi"][k], big["w_mo"][k], vecs[5])

    return (loss, grad_x.reshape(1, S, D), *ordered(0, g_w_ada, gv), *ordered(1, d_ada, dv),
            *ordered(2, nm_ada, mv), *ordered(3, nv_ada, vv))
```

```python
import functools

import numpy as np
import jax
import jax.numpy as jnp
from jax import lax
from jax.experimental import pallas as pl
from jax.experimental.pallas import tpu as pltpu

F32, BF16 = jnp.float32, jnp.bfloat16
D = 1024
HEAD = 128
DILATIONS = (1, 4, 16)
N_SLOT = 4
AOW = N_SLOT * HEAD
DFF = 4 * D
N_DEV = 8
EPS = 1e-6
NEG = -1e30
SCALE = HEAD ** -0.5
LR, B1, B2, ADAM_EPS, WD, STEP = 0.001, 0.9, 0.999, 1e-08, 0.01, 10
V7X_VMEM_LIMIT = 56 * 1024 * 1024
TM = 1024
MESH = pl.DeviceIdType.MESH
AXES = ("x", "y", "c")


def _cparams(*sem):
    if sem:
        return pltpu.CompilerParams(dimension_semantics=sem, vmem_limit_bytes=V7X_VMEM_LIMIT)
    return pltpu.CompilerParams(vmem_limit_bytes=V7X_VMEM_LIMIT)


def _nn(a, b):
    return jnp.dot(a, b, preferred_element_type=F32)


def _nt(a, b):
    return lax.dot_general(a, b, (((1,), (1,)), ((), ())), preferred_element_type=F32)


def _tn(a, b):
    return lax.dot_general(a, b, (((0,), (0,)), ((), ())), preferred_element_type=F32)


def _rms_r(x):
    return lax.rsqrt(jnp.mean(x * x, axis=-1, keepdims=True) + EPS)


def _rms_bwd(x, r, g, dn):
    gy = dn * g
    dx = r * gy - x * (r * r * r) * jnp.mean(x * gy, axis=-1, keepdims=True)
    return dx, dn * (x * r)


def _sigmoid(t):
    return 1.0 / (1.0 + jnp.exp(-t))


def _rowsum(v):
    return jnp.sum(v, axis=0, keepdims=True)


def _vec_spec(n=D):
    return pl.BlockSpec((1, n), lambda *_: (0, 0))


def _const_spec(shape):
    nd = len(shape)
    return pl.BlockSpec(shape, lambda *_: (0,) * nd)


def _win_rowblock(j):
    return jnp.where(j < 9, (j % 3) * 3 + j // 3, j)


def _proj(x, g, sc, sh, w_int):
    S = x.shape[0]
    tm = 2 * TM

    def body(x_ref, g_ref, sc_ref, sh_ref, w_ref, h_ref, q_ref, e_ref):
        j = pl.program_id(1)

        @pl.when(j == 0)
        def _():
            xv = x_ref[...]
            h = xv * _rms_r(xv) * g_ref[...] * (1.0 + sc_ref[...]) + sh_ref[...]
            h_ref[...] = h.astype(BF16)

        acc = _nt(h_ref[...], w_ref[...])

        @pl.when(j < 9)
        def _():
            q_ref[0] = acc

        @pl.when(j >= 9)
        def _():
            e_ref[0] = acc.astype(BF16)

    def e_idx(i, j):
        k = jnp.maximum(j - 9, 0)
        return (k // 2, i, k % 2)

    return pl.pallas_call(
        body, name="proj", grid=(S // tm, 19),
        in_specs=[pl.BlockSpec((tm, D), lambda i, j: (i, 0)), _vec_spec(), _vec_spec(), _vec_spec(),
                  pl.BlockSpec((512, D), lambda i, j: (_win_rowblock(j), 0))],
        out_specs=[pl.BlockSpec((tm, D), lambda i, j: (i, 0)),
                   pl.BlockSpec((1, tm, 512), lambda i, j: (jnp.minimum(j, 8), i, 0)),
                   pl.BlockSpec((1, tm, 512), e_idx)],
        out_shape=[jax.ShapeDtypeStruct((S, D), BF16), jax.ShapeDtypeStruct((9, S, 512), F32),
                   jax.ShapeDtypeStruct((5, S, D), BF16)],
        compiler_params=_cparams("parallel", "arbitrary"),
    )(x, g, sc, sh, w_int)


def _bias_table():
    slopes = (2.0 ** (-8.0 * np.arange(1, 13, dtype=np.float32) / 12.0)).astype(np.float32)
    qi = np.arange(HEAD)[:, None]
    kj = np.arange(2 * HEAD)[None, :]
    delta = HEAD + qi - kj
    mask = (delta >= 0) & (delta <= HEAD)
    out = np.zeros((3, N_SLOT, HEAD, 2 * HEAD), np.float32)
    for gi, d in enumerate(DILATIONS):
        for j in range(N_SLOT):
            bias = -slopes[gi * N_SLOT + j] * (delta * d).astype(np.float32)
            out[gi, j] = np.where(mask, bias, NEG)
    return jnp.asarray(out)


def _block_rows(b, d):
    r = b % d
    n = b // d
    st = n * (HEAD * d) + r
    stp = jnp.maximum(n - 1, 0) * (HEAD * d) + r
    return n, st, stp


def _attn_fwd(qkv, bias):
    S = qkv.shape[2]
    nblk = S // HEAD

    def body(qkv_ref, b_ref, o_ref, lse_ref, m_s, l_s, a_s):
        g = pl.program_id(1)

        @pl.when(g == 0)
        def _():
            m_s[...] = jnp.full_like(m_s, NEG)
            l_s[...] = jnp.zeros_like(l_s)
            a_s[...] = jnp.zeros_like(a_s)

        bias = b_ref[0, 0]
        col = lax.broadcasted_iota(jnp.int32, bias.shape, 1)
        bias_first = jnp.where(col < HEAD, NEG, bias)

        for gi, d in enumerate(DILATIONS):
            @pl.when(g == gi)
            def _(d=d):
                def step(b, carry):
                    n, st, stp = _block_rows(b, d)
                    cur = pl.ds(st, HEAD, stride=d)
                    prv = pl.ds(stp, HEAD, stride=d)
                    q = qkv_ref.at[0, 0][cur, :].astype(BF16)
                    kw = jnp.concatenate([qkv_ref.at[0, 1][prv, :], qkv_ref.at[0, 1][cur, :]], axis=0).astype(BF16)
                    vw = jnp.concatenate([qkv_ref.at[0, 2][prv, :], qkv_ref.at[0, 2][cur, :]], axis=0).astype(BF16)
                    s = _nt(q, kw) * SCALE + jnp.where(n > 0, bias, bias_first)
                    m_old = m_s[cur, :]
                    m_new = jnp.maximum(m_old, jnp.max(s, axis=-1, keepdims=True))
                    alpha = jnp.exp(m_old - m_new)
                    p = jnp.exp(s - m_new[:, :1])
                    l_s[cur, :] = alpha * l_s[cur, :] + jnp.sum(p, axis=-1, keepdims=True)
                    a_s[cur, :] = alpha * a_s[cur, :] + _nn(p.astype(BF16), vw)
                    m_s[cur, :] = m_new
                    return carry

                lax.fori_loop(0, nblk, step, 0, unroll=4)

        @pl.when(g == len(DILATIONS) - 1)
        def _():
            l = l_s[...]
            o_ref[...] = a_s[...] / l
            lse_ref[...] = m_s[...] + jnp.log(l)

    return pl.pallas_call(
        body, name="attn_fwd", grid=(N_SLOT, 3),
        in_specs=[pl.BlockSpec((1, 3, S, HEAD), lambda j, g: (g, 0, 0, j)),
                  pl.BlockSpec((1, 1, HEAD, 2 * HEAD), lambda j, g: (g, j, 0, 0))],
        out_specs=[pl.BlockSpec((S, HEAD), lambda j, g: (0, j)), pl.BlockSpec((S, HEAD), lambda j, g: (0, j))],
        out_shape=[jax.ShapeDtypeStruct((S, AOW), F32), jax.ShapeDtypeStruct((S, AOW), F32)],
        scratch_shapes=[pltpu.VMEM((S, HEAD), F32)] * 3,
        compiler_params=_cparams("parallel", "arbitrary"),
    )(qkv, bias)


def _shift_down(z, k, halo_rows):
    out = pltpu.roll(z, k, axis=0)
    rid = lax.broadcasted_iota(jnp.int32, z.shape, 0)
    for t in range(k):
        out = jnp.where(rid == t, halo_rows[t], out)
    return out


def _shift_up(z, k, halo_rows):
    n = z.shape[0]
    out = pltpu.roll(z, n - k, axis=0)
    rid = lax.broadcasted_iota(jnp.int32, z.shape, 0)
    for t in range(k):
        out = jnp.where(rid == n - k + t, halo_rows[t], out)
    return out


def _e_spec(chunk, tm):
    return pl.BlockSpec((1, tm, D), lambda i, c=chunk: (c, i, 0))


def _e_prev_spec(chunk, tm):
    return pl.BlockSpec((1, 16, D), lambda i, c=chunk: (c, jnp.maximum(i * (tm // 16) - 1, 0), 0))


def _e_next_spec(chunk, tm, S):
    return pl.BlockSpec((1, 16, D), lambda i, c=chunk: (c, jnp.minimum((i + 1) * (tm // 16), S // 16 - 1), 0))


def _mix(o_attn, e, cw8, ba, bb, w_bat, w_bc):
    S = o_attn.shape[0]
    tm = 256

    def body(o_ref, cb_ref, cc_ref, cx_ref, ga_ref, gb_ref, ccp_ref, cxp_ref, cw_ref, ba_ref, bb_ref, wba_ref, wbc_ref,
             obf_ref, cbu_ref, ya_ref, yc_ref, mg_ref):
        i = pl.program_id(0)
        o = o_ref[...].astype(BF16)
        obf_ref[...] = o
        ya = _nt(o, wba_ref[...])
        z = cc_ref[0].astype(F32) * cx_ref[0].astype(F32)
        zp = ccp_ref[0].astype(F32) * cxp_ref[0].astype(F32) * (i > 0).astype(F32)
        z1 = _shift_down(z, 1, [zp[15:16]])
        z2 = _shift_down(z, 2, [zp[14:15], zp[15:16]])
        cw = cw_ref[...]
        u = cw[0:1] * z2 + cw[1:2] * z1 + cw[2:3] * z
        cbu = (cb_ref[0].astype(F32) * u).astype(BF16)
        cbu_ref[...] = cbu
        yc = _nn(cbu, wbc_ref[...])
        sa = _sigmoid(ga_ref[0].astype(F32) + ba_ref[...])
        sb = _sigmoid(gb_ref[0].astype(F32) + bb_ref[...])
        ya_ref[...] = ya.astype(BF16)
        yc_ref[...] = yc.astype(BF16)
        mg_ref[...] = (sa * ya + sb * yc).astype(BF16)

    row = lambda w: pl.BlockSpec((tm, w), lambda i: (i, 0))
    return pl.pallas_call(
        body, name="mix", grid=(S // tm,),
        in_specs=[row(AOW)] + [_e_spec(c, tm) for c in range(5)] + [_e_prev_spec(1, tm), _e_prev_spec(2, tm),
                  _const_spec((8, D)), _vec_spec(), _vec_spec(), _const_spec((D, AOW)), _const_spec((D, D))],
        out_specs=[row(AOW), row(D), row(D), row(D), row(D)],
        out_shape=[jax.ShapeDtypeStruct((S, AOW), BF16)] + [jax.ShapeDtypeStruct((S, D), BF16)] * 4,
        compiler_params=_cparams("parallel"),
    )(o_attn, e, e, e, e, e, e, e, cw8, ba, bb, w_bat, w_bc)


def _out_proj(merged, w_out, x, gate1, g_mlp, sc2, sh2):
    S = x.shape[0]
    tm = TM

    def body(mg_ref, w_ref, x_ref, gt_ref, g_ref, sc_ref, sh_ref, x1_ref, mo_ref, h2_ref):
        mo = _nn(mg_ref[...], w_ref[...])
        mo_ref[...] = mo.astype(BF16)
        x1 = x_ref[...] + gt_ref[...] * mo
        x1_ref[...] = x1
        h2 = x1 * _rms_r(x1) * g_ref[...] * (1.0 + sc_ref[...]) + sh_ref[...]
        h2_ref[...] = h2.astype(BF16)

    row = pl.BlockSpec((tm, D), lambda i: (i, 0))
    return pl.pallas_call(
        body, name="out_proj", grid=(S // tm,),
        in_specs=[row, _const_spec((D, D)), row, _vec_spec(), _vec_spec(), _vec_spec(), _vec_spec()],
        out_specs=[row, row, row],
        out_shape=[jax.ShapeDtypeStruct((S, D), F32), jax.ShapeDtypeStruct((S, D), BF16), jax.ShapeDtypeStruct((S, D), BF16)],
        compiler_params=_cparams("parallel"),
    )(merged, w_out, x, gate1, g_mlp, sc2, sh2)


def _mlp_in(h2, w_mit):
    S = h2.shape[0]
    tm, tn = TM, 2048

    def body(h_ref, w_ref, a_ref, f_ref):
        a = _nt(h_ref[...], w_ref[...])
        a_ref[...] = a.astype(BF16)
        f_ref[...] = jnp.square(jnp.maximum(a, 0.0)).astype(BF16)

    blk = pl.BlockSpec((tm, tn), lambda i, j: (i, j))
    return pl.pallas_call(
        body, name="mlp_in", grid=(S // tm, DFF // tn),
        in_specs=[pl.BlockSpec((tm, D), lambda i, j: (i, 0)), pl.BlockSpec((tn, D), lambda i, j: (j, 0))],
        out_specs=[blk, blk],
        out_shape=[jax.ShapeDtypeStruct((S, DFF), BF16)] * 2,
        compiler_params=_cparams("parallel", "parallel"),
    )(h2, w_mit)


def _mlp_out(f, w_mo, x1, gate2, g_fin, tgt):
    S = x1.shape[0]
    tm, tk = TM, 1024
    nk = DFF // tk

    def body(f_ref, w_ref, x1_ref, gt_ref, g_ref, t_ref, mlp_ref, dx2_ref, pv_ref, acc):
        i, k = pl.program_id(0), pl.program_id(1)

        @pl.when((i == 0) & (k == 0))
        def _():
            pv_ref[...] = jnp.zeros_like(pv_ref)

        @pl.when(k == 0)
        def _():
            acc[...] = jnp.zeros_like(acc)

        acc[...] += _nn(f_ref[...], w_ref[...])

        @pl.when(k == nk - 1)
        def _():
            mlp = acc[...]
            mlp_ref[...] = mlp.astype(BF16)
            x2 = x1_ref[...] + gt_ref[...] * mlp
            r = _rms_r(x2)
            g = g_ref[...]
            err = x2 * r * g - t_ref[...]
            dy = err * (1.0 / D)
            dx2, pg = _rms_bwd(x2, r, g, dy)
            dx2_ref[...] = dx2
            pv_ref[0:1, :] += _rowsum(pg)
            pv_ref[1:2, :] += 0.5 * _rowsum(jnp.mean(err * err, axis=-1, keepdims=True))

    row = pl.BlockSpec((tm, D), lambda i, k: (i, 0))
    return pl.pallas_call(
        body, name="mlp_out", grid=(S // tm, nk),
        in_specs=[pl.BlockSpec((tm, tk), lambda i, k: (i, k)), pl.BlockSpec((tk, D), lambda i, k: (k, 0)),
                  row, _vec_spec(), _vec_spec(), row],
        out_specs=[row, row, _const_spec((8, D))],
        out_shape=[jax.ShapeDtypeStruct((S, D), BF16), jax.ShapeDtypeStruct((S, D), F32), jax.ShapeDtypeStruct((8, D), F32)],
        scratch_shapes=[pltpu.VMEM((tm, D), F32)],
        compiler_params=_cparams("arbitrary", "arbitrary"),
    )(f, w_mo, x1, gate2, g_fin, tgt)


def _bwd_mlp_a(dx2, gate2, mlp, w_mo, a):
    S = dx2.shape[0]
    tm, tn = TM, 2048

    def body(dx_ref, gt_ref, mlp_ref, w_ref, a_ref, da_ref, dmo_ref, pv_ref):
        i, j = pl.program_id(0), pl.program_id(1)

        @pl.when((i == 0) & (j == 0))
        def _():
            pv_ref[...] = jnp.zeros_like(pv_ref)

        @pl.when(j == 0)
        def _():
            dx = dx_ref[...]
            dmo_ref[...] = (dx * gt_ref[...]).astype(BF16)
            pv_ref[0:1, :] += _rowsum(dx * mlp_ref[...].astype(F32))

        df = _nt(dmo_ref[...], w_ref[...])
        da_ref[...] = (df * (2.0 * jnp.maximum(a_ref[...].astype(F32), 0.0))).astype(BF16)

    row = pl.BlockSpec((tm, D), lambda i, j: (i, 0))
    blk = pl.BlockSpec((tm, tn), lambda i, j: (i, j))
    return pl.pallas_call(
        body, name="bwd_mlp_a", grid=(S // tm, DFF // tn),
        in_specs=[row, _vec_spec(), row, pl.BlockSpec((tn, D), lambda i, j: (j, 0)), blk],
        out_specs=[blk, row, _const_spec((8, D))],
        out_shape=[jax.ShapeDtypeStruct((S, DFF), BF16), jax.ShapeDtypeStruct((S, D), BF16), jax.ShapeDtypeStruct((8, D), F32)],
        compiler_params=_cparams("arbitrary", "arbitrary"),
    )(dx2, gate2, mlp, w_mo, a)


def _bwd_mlp_b(da, w_mit, x1, dx2, g_mlp, sc2):
    S = x1.shape[0]
    tm, tk = TM, 1024
    nk = DFF // tk

    def body(da_ref, w_ref, x1_ref, dx2_ref, g_ref, sc_ref, dx1_ref, pv_ref, acc):
        i, k = pl.program_id(0), pl.program_id(1)

        @pl.when((i == 0) & (k == 0))
        def _():
            pv_ref[...] = jnp.zeros_like(pv_ref)

        @pl.when(k == 0)
        def _():
            acc[...] = jnp.zeros_like(acc)

        acc[...] += _nn(da_ref[...], w_ref[...])

        @pl.when(k == nk - 1)
        def _():
            dh = acc[...]
            x1 = x1_ref[...]
            r = _rms_r(x1)
            g = g_ref[...]
            dxn, pg = _rms_bwd(x1, r, g, dh * (1.0 + sc_ref[...]))
            dx1_ref[...] = dx2_ref[...] + dxn
            pv_ref[0:1, :] += _rowsum(dh)
            pv_ref[1:2, :] += _rowsum(dh * (x1 * r * g))
            pv_ref[2:3, :] += _rowsum(pg)

    row = pl.BlockSpec((tm, D), lambda i, k: (i, 0))
    return pl.pallas_call(
        body, name="bwd_mlp_b", grid=(S // tm, nk),
        in_specs=[pl.BlockSpec((tm, tk), lambda i, k: (i, k)), pl.BlockSpec((tk, D), lambda i, k: (k, 0)),
                  row, row, _vec_spec(), _vec_spec()],
        out_specs=[row, _const_spec((8, D))],
        out_shape=[jax.ShapeDtypeStruct((S, D), F32), jax.ShapeDtypeStruct((8, D), F32)],
        scratch_shapes=[pltpu.VMEM((tm, D), F32)],
        compiler_params=_cparams("arbitrary", "arbitrary"),
    )(da, w_mit, x1, dx2, g_mlp, sc2)


def _bwd_mix(dx1, gate1, mo, e, cw8, ba, bb, ya, yc, o_attn, w_out, w_bc, w_bat):
    S = dx1.shape[0]
    tm = 256
    n_tiles = S // tm

    def body(dx_ref, dxn_ref, gt_ref, mo_ref, cb_ref, cc_ref, cx_ref, ga_ref, gb_ref, cbn_ref, gbn_ref, ccp_ref, cxp_ref,
             cw_ref, ba_ref, bb_ref, ya_ref, yc_ref, o_ref, wout_ref, wbc_ref, wba_ref,
             dmo_ref, dya_ref, dyc_ref, do_ref, dl_ref, de_ref, pv_ref):
        i = pl.program_id(0)

        @pl.when(i == 0)
        def _():
            pv_ref[...] = jnp.zeros_like(pv_ref)

        gate = gt_ref[...]
        bbv = bb_ref[...]

        def conv_branch_grad(dx_rows, gb_rows):
            dmo = (dx_rows * gate).astype(BF16)
            dmg = _nt(dmo, wout_ref[...])
            sb = _sigmoid(gb_rows + bbv)
            dyc = dmg * sb
            return dmo, dmg, sb, dyc, _nt(dyc.astype(BF16), wbc_ref[...])

        dx = dx_ref[...]
        cb = cb_ref[0].astype(F32)
        cc = cc_ref[0].astype(F32)
        cx = cx_ref[0].astype(F32)
        dmo, dmg, sb, dyc, dcbu = conv_branch_grad(dx, gb_ref[0].astype(F32))
        dmo_ref[...] = dmo
        pv_ref[0:1, :] += _rowsum(dx * mo_ref[...].astype(F32))
        sa = _sigmoid(ga_ref[0].astype(F32) + ba_ref[...])
        dya = (dmg * sa).astype(BF16)
        dya_ref[...] = dya
        dyc_ref[...] = dyc.astype(BF16)
        dga = dmg * ya_ref[...].astype(F32) * sa * (1.0 - sa)
        dgb = dmg * yc_ref[...].astype(F32) * sb * (1.0 - sb)
        pv_ref[1:2, :] += _rowsum(dga)
        pv_ref[2:3, :] += _rowsum(dgb)

        do = _nn(dya, wba_ref[...])
        do_ref[...] = do
        prod = do * o_ref[...]
        dl_ref[...] = jnp.concatenate(
            [jnp.broadcast_to(jnp.sum(prod[:, s * HEAD:(s + 1) * HEAD], axis=-1, keepdims=True), (tm, HEAD))
             for s in range(N_SLOT)], axis=1)

        z = cc * cx
        zp = ccp_ref[0].astype(F32) * cxp_ref[0].astype(F32) * (i > 0).astype(F32)
        z1 = _shift_down(z, 1, [zp[15:16]])
        z2 = _shift_down(z, 2, [zp[14:15], zp[15:16]])
        cw = cw_ref[...]
        u = cw[0:1] * z2 + cw[1:2] * z1 + cw[2:3] * z
        du = dcbu * cb
        dcbu_n = conv_branch_grad(dxn_ref[...], gbn_ref[0].astype(F32))[4]
        du_n = dcbu_n * cbn_ref[0].astype(F32) * (i < n_tiles - 1).astype(F32)
        du1 = _shift_up(du, 1, [du_n[0:1]])
        du2 = _shift_up(du, 2, [du_n[0:1], du_n[1:2]])
        dz = cw[2:3] * du + cw[1:2] * du1 + cw[0:1] * du2
        pv_ref[3:4, :] += _rowsum(du * z2)
        pv_ref[4:5, :] += _rowsum(du * z1)
        pv_ref[5:6, :] += _rowsum(du * z)

        de_ref[0] = (dcbu * u).astype(BF16)
        de_ref[1] = (dz * cx).astype(BF16)
        de_ref[2] = (dz * cc).astype(BF16)
        de_ref[3] = dga.astype(BF16)
        de_ref[4] = dgb.astype(BF16)

    row = lambda w: pl.BlockSpec((tm, w), lambda i: (i, 0))
    nxt = pl.BlockSpec((16, D), lambda i: (jnp.minimum((i + 1) * (tm // 16), S // 16 - 1), 0))
    return pl.pallas_call(
        body, name="bwd_mix", grid=(n_tiles,),
        in_specs=[row(D), nxt, _vec_spec(), row(D)] + [_e_spec(c, tm) for c in range(5)]
                 + [_e_next_spec(0, tm, S), _e_next_spec(4, tm, S), _e_prev_spec(1, tm), _e_prev_spec(2, tm),
                    _const_spec((8, D)), _vec_spec(), _vec_spec(), row(D), row(D), row(AOW),
                    _const_spec((D, D)), _const_spec((D, D)), _const_spec((D, AOW))],
        out_specs=[row(D), row(D), row(D), row(AOW), row(AOW), pl.BlockSpec((5, tm, D), lambda i: (0, i, 0)),
                   _const_spec((8, D))],
        out_shape=[jax.ShapeDtypeStruct((S, D), BF16)] * 3 + [jax.ShapeDtypeStruct((S, AOW), F32)] * 2
                  + [jax.ShapeDtypeStruct((5, S, D), BF16), jax.ShapeDtypeStruct((8, D), F32)],
        compiler_params=_cparams("arbitrary"),
    )(dx1, dx1, gate1, mo, e, e, e, e, e, e, e, e, e, cw8, ba, bb, ya, yc, o_attn, w_out, w_bc, w_bat)


def _attn_bwd(qkv, do, lse, dl, bias):
    S = qkv.shape[2]
    nblk = S // HEAD

    def body(qkv_ref, do_ref, lse_ref, dl_ref, b_ref, d_ref):
        g = pl.program_id(1)
        d_ref[...] = jnp.zeros_like(d_ref)
        bias = b_ref[0, 0]
        col = lax.broadcasted_iota(jnp.int32, bias.shape, 1)
        bias_first = jnp.where(col < HEAD, NEG, bias)

        for gi, d in enumerate(DILATIONS):
            @pl.when(g == gi)
            def _(d=d):
                def step(b, carry):
                    n, st, stp = _block_rows(b, d)
                    cur = pl.ds(st, HEAD, stride=d)
                    prv = pl.ds(stp, HEAD, stride=d)
                    q = qkv_ref.at[0, 0][cur, :].astype(BF16)
                    kw = jnp.concatenate([qkv_ref.at[0, 1][prv, :], qkv_ref.at[0, 1][cur, :]], axis=0).astype(BF16)
                    vw = jnp.concatenate([qkv_ref.at[0, 2][prv, :], qkv_ref.at[0, 2][cur, :]], axis=0).astype(BF16)
                    s = _nt(q, kw) * SCALE + jnp.where(n > 0, bias, bias_first)
                    p = jnp.exp(s - lse_ref[cur, :][:, :1])
                    dob = do_ref[cur, :].astype(BF16)
                    dvw = _tn(p.astype(BF16), dob)
                    dp = _nt(dob, vw)
                    ds = (p * (dp - dl_ref[cur, :][:, :1]) * SCALE).astype(BF16)
                    d_ref.at[0, 0][cur, :] = _nn(ds, kw)
                    dkw = _tn(ds, q)
                    d_ref.at[0, 1][cur, :] += dkw[HEAD:]
                    d_ref.at[0, 1][prv, :] += dkw[:HEAD]
                    d_ref.at[0, 2][cur, :] += dvw[HEAD:]
                    d_ref.at[0, 2][prv, :] += dvw[:HEAD]
                    return carry

                lax.fori_loop(0, nblk, step, 0, unroll=4)

    col_blk = pl.BlockSpec((S, HEAD), lambda j, g: (0, j))
    qkv_blk = pl.BlockSpec((1, 3, S, HEAD), lambda j, g: (g, 0, 0, j))
    return pl.pallas_call(
        body, name="attn_bwd", grid=(N_SLOT, 3),
        in_specs=[qkv_blk, col_blk, col_blk, col_blk, pl.BlockSpec((1, 1, HEAD, 2 * HEAD), lambda j, g: (g, j, 0, 0))],
        out_specs=qkv_blk,
        out_shape=jax.ShapeDtypeStruct((3, 3, S, AOW), F32),
        compiler_params=_cparams("parallel", "arbitrary"),
    )(qkv, do, lse, dl, bias)


def _bwd_in(dqkv, de, w_int, x, dx1, g_mix, sc1):
    S = x.shape[0]
    tm = TM

    def body(dq_ref, de_ref, w_ref, x_ref, dx1_ref, g_ref, sc_ref, gx_ref, pv_ref, acc):
        i, k = pl.program_id(0), pl.program_id(1)

        @pl.when((i == 0) & (k == 0))
        def _():
            pv_ref[...] = jnp.zeros_like(pv_ref)

        @pl.when(k == 0)
        def _():
            acc[...] = jnp.zeros_like(acc)

        @pl.when(k < 9)
        def _():
            acc[...] += _nn(dq_ref[0].astype(BF16), w_ref[...])

        @pl.when(k >= 9)
        def _():
            acc[...] += _nn(de_ref[0], w_ref[...])

        @pl.when(k == 18)
        def _():
            dh = acc[...]
            xv = x_ref[...]
            r = _rms_r(xv)
            g = g_ref[...]
            dxn, pg = _rms_bwd(xv, r, g, dh * (1.0 + sc_ref[...]))
            gx_ref[...] = dx1_ref[...] + dxn
            pv_ref[0:1, :] += _rowsum(dh)
            pv_ref[1:2, :] += _rowsum(dh * (xv * r * g))
            pv_ref[2:3, :] += _rowsum(pg)

    def e_idx(i, k):
        kk = jnp.maximum(k - 9, 0)
        return (kk // 2, i, kk % 2)

    row = pl.BlockSpec((tm, D), lambda i, k: (i, 0))
    return pl.pallas_call(
        body, name="bwd_in", grid=(S // tm, 19),
        in_specs=[pl.BlockSpec((1, tm, 512), lambda i, k: (jnp.minimum(k, 8), i, 0)), pl.BlockSpec((1, tm, 512), e_idx),
                  pl.BlockSpec((512, D), lambda i, k: (_win_rowblock(k), 0)), row, row, _vec_spec(), _vec_spec()],
        out_specs=[row, _const_spec((8, D))],
        out_shape=[jax.ShapeDtypeStruct((S, D), F32), jax.ShapeDtypeStruct((8, D), F32)],
        scratch_shapes=[pltpu.VMEM((tm, D), F32)],
        compiler_params=_cparams("arbitrary", "arbitrary"),
    )(dqkv, de, w_int, x, dx1, g_mix, sc1)


def _grad_w(name, a, b):
    S, ka = a.shape
    nb = b.shape[1]

    def body(a_ref, b_ref, o_ref):
        o_ref[...] = _tn(a_ref[...], b_ref[...]).astype(BF16)

    return pl.pallas_call(
        body, name=name, grid=(ka // 512,),
        in_specs=[pl.BlockSpec((S, 512), lambda n: (0, n)), pl.BlockSpec((S, nb), lambda n: (0, 0))],
        out_specs=pl.BlockSpec((512, nb), lambda n: (n, 0)),
        out_shape=jax.ShapeDtypeStruct((ka, nb), BF16),
        compiler_params=_cparams("parallel"),
    )(a, b)


def _grad_w_in(dqkv, de, h):
    S = h.shape[0]

    def body(dq_ref, de_ref, h_ref, o_ref):
        n = pl.program_id(0)

        @pl.when(n < 9)
        def _():
            o_ref[...] = _tn(dq_ref[0].astype(BF16), h_ref[...]).astype(BF16)

        @pl.when(n >= 9)
        def _():
            o_ref[...] = _tn(de_ref[0], h_ref[...]).astype(BF16)

    def e_idx(n):
        kk = jnp.maximum(n - 9, 0)
        return (kk // 2, 0, kk % 2)

    return pl.pallas_call(
        body, name="grad_w_in", grid=(19,),
        in_specs=[pl.BlockSpec((1, S, 512), lambda n: (jnp.minimum(n, 8), 0, 0)), pl.BlockSpec((1, S, 512), e_idx),
                  pl.BlockSpec((S, D), lambda n: (0, 0))],
        out_specs=pl.BlockSpec((512, D), lambda n: (_win_rowblock(n), 0)),
        out_shape=jax.ShapeDtypeStruct((19 * 512, D), BF16),
        compiler_params=_cparams("parallel"),
    )(dqkv, de, h)


def _local_step(x, tgt, mod, g_mix, g_mlp, g_fin, ba, bb, cw8, w_int, late_weights, mlp_grads_ready, other_grads_ready):
    S = x.shape[0]
    sh1, sc1, gt1, sh2, sc2, gt2 = [mod[k:k + 1] for k in range(6)]
    bias = _bias_table()

    h, qkv, e = _proj(x, g_mix, sc1, sh1, w_int)
    qkv = qkv.reshape(3, 3, S, AOW)
    o_attn, lse = _attn_fwd(qkv, bias)
    w_bat, w_bc, w_out, w_mit, w_mo = late_weights(o_attn)
    o_bf, cbu, ya, yc, merged = _mix(o_attn, e, cw8, ba, bb, w_bat, w_bc)
    x1, mo, h2 = _out_proj(merged, w_out, x, gt1, g_mlp, sc2, sh2)
    a, f = _mlp_in(h2, w_mit)
    mlp, dx2, pv_f = _mlp_out(f, w_mo, x1, gt2, g_fin, tgt)

    da, dmo2, pv_a = _bwd_mlp_a(dx2, gt2, mlp, w_mo, a)
    dx1, pv_b = _bwd_mlp_b(da, w_mit, x1, dx2, g_mlp, sc2)
    zero = mlp_grads_ready(_grad_w("grad_w_mi", da, h2), _grad_w("grad_w_mo", f, dmo2))
    dmo, dya, dyc, do, dl, de, pv_m = _bwd_mix(dx1, gt1 + zero, mo, e, cw8, ba, bb, ya, yc, o_attn, w_out, w_bc, w_bat)
    dqkv = _attn_bwd(qkv, do, lse, dl, bias).reshape(9, S, AOW)
    zero = other_grads_ready(_grad_w_in(dqkv, de, h), _grad_w("grad_w_ba", dya, o_bf), _grad_w("grad_w_bc", cbu, dyc),
                             _grad_w("grad_w_out", merged, dmo))
    grad_x, pv_i = _bwd_in(dqkv, de, w_int, x, dx1, g_mix, sc1 + zero)

    vec = jnp.concatenate([pv_i[0:2], pv_m[0:1], pv_b[0:2], pv_a[0:1], pv_i[2:3], pv_b[2:3], pv_f[0:1],
                           pv_m[1:3], pv_m[3:6], jnp.zeros((2, D), F32)], axis=0)
    return pv_f[1, 0], grad_x, vec


def _my_place():
    return lax.axis_index("x"), lax.axis_index("y"), lax.axis_index("c")


def _dev_index(px, py, pc):
    return 4 * px + 2 * py + pc


def _allgather_weights(shards):
    nw = len(shards)
    HBM = pl.BlockSpec(memory_space=pl.ANY)

    def body(*refs):
        sh, full = refs[:nw], refs[nw:2 * nw]
        send_sems, recv_sems, local_sems = refs[2 * nw:]
        x, y, c = _my_place()
        me, sibling = (x, y, c), (x, y, 1 - c)
        chips = [(1 - x, y), (x, 1 - y), (1 - x, 1 - y)]

        def rows(w, px, py, pc):
            r = sh[w].shape[0]
            return full[w].at[pl.ds(pl.multiple_of(_dev_index(px, py, pc) * r, 16), r), :]

        def copy(w, k, block, to, src=None):
            return pltpu.make_async_remote_copy(
                src_ref=rows(w, *block) if src is None else src, dst_ref=rows(w, *block),
                send_sem=send_sems.at[w, k], recv_sem=recv_sems.at[w, k], device_id=to, device_id_type=MESH)

        mine = [pltpu.make_async_copy(sh[w], rows(w, *me), local_sems.at[w]) for w in range(nw)]
        for cp in mine:
            cp.start()
        first = []
        for w in range(nw):
            first.append(copy(w, 0, me, sibling, src=sh[w]))
            first += [copy(w, 1 + j, me, (*chip, c), src=sh[w]) for j, chip in enumerate(chips)]
        for cp in first:
            cp.start()
        passed = []
        for w in range(nw):
            for j, chip in enumerate(chips):
                copy(w, 1 + j, (*chip, c), me).wait_recv()
                fwd = copy(w, 4 + j, (*chip, c), sibling)
                fwd.start()
                passed.append(fwd)
        for w in range(nw):
            copy(w, 0, sibling, me).wait_recv()
            for j, chip in enumerate(chips):
                copy(w, 4 + j, (*chip, 1 - c), me).wait_recv()
        for cp in first + passed:
            cp.wait_send()
        for cp in mine:
            cp.wait()

    return pl.pallas_call(
        body, name="allgather_weights",
        out_shape=[jax.ShapeDtypeStruct((N_DEV * s.shape[0], s.shape[1]), s.dtype) for s in shards],
        in_specs=[HBM] * nw, out_specs=[HBM] * nw,
        scratch_shapes=[pltpu.SemaphoreType.DMA((nw, 7)), pltpu.SemaphoreType.DMA((nw, 7)), pltpu.SemaphoreType.DMA((nw,))],
    )(*shards)


def _peer(x, y, c, m):
    return (x ^ ((m >> 2) & 1), y ^ ((m >> 1) & 1), c ^ (m & 1))


HBM_SPEC = pl.BlockSpec(memory_space=pltpu.HBM)
SEM_SPEC = pl.BlockSpec(memory_space=pltpu.SEMAPHORE)
N_PEER = N_DEV - 1


SPLIT_MASKS = {"gather": tuple(range(1, N_DEV)), "scatter": tuple(range(1, N_DEV)), "chips": (2, 4, 6)}


def _split_copy(mode, src_ref, land_ref, send_sems, recv_sems, w, j, place, arriving=False):
    x, y, c = place
    masks = SPLIT_MASKS[mode]
    peer = _peer(x, y, c, masks[j])
    k = w * len(masks) + j
    sender, receiver = ((peer, (x, y, c)) if arriving else ((x, y, c), peer))
    if mode == "gather":
        r = src_ref.shape[0]
        src, dst = src_ref, land_ref.at[pl.ds(pl.multiple_of(_dev_index(*sender) * r, 16), r), :]
    elif mode == "scatter":
        r = land_ref.shape[1]
        src, dst = src_ref.at[pl.ds(pl.multiple_of(_dev_index(*receiver) * r, 16), r), :], land_ref.at[j]
    else:
        src, dst = src_ref.at[2 * receiver[0] + receiver[1]], land_ref.at[j]
    return pltpu.make_async_remote_copy(src_ref=src, dst_ref=dst, send_sem=send_sems.at[k], recv_sem=recv_sems.at[k],
                                        device_id=peer, device_id_type=MESH)


def _split_start(name, mode, srcs, lands):
    n = len(srcs)
    nm = len(SPLIT_MASKS[mode])

    def body(*refs):
        src, land = refs[:n], refs[n:2 * n]
        send_sems, recv_sems = refs[2 * n], refs[2 * n + 1]
        token = refs[-1]
        place = _my_place()
        for w in range(n):
            for j in range(nm):
                _split_copy(mode, src[w], land[w], send_sems, recv_sems, w, j, place).start()
        token[...] = jnp.zeros_like(token)

    hbm = lambda t: pltpu.HBM(t.shape, t.dtype)
    out = pl.pallas_call(
        body, name=name,
        out_shape=(pltpu.SemaphoreType.DMA((n * nm,)), pltpu.SemaphoreType.DMA((n * nm,)), *[hbm(t) for t in srcs],
                   *[hbm(t) for t in lands], jax.ShapeDtypeStruct((8, 128), F32)),
        in_specs=(HBM_SPEC,) * (2 * n),
        out_specs=(SEM_SPEC, SEM_SPEC) + (HBM_SPEC,) * (2 * n) + (pl.BlockSpec(memory_space=pltpu.VMEM),),
        input_output_aliases={i: 2 + i for i in range(2 * n)},
        compiler_params=pltpu.CompilerParams(has_side_effects=pltpu.SideEffectType.DATAFLOW_SIDE_EFFECTING),
    )(*[pltpu.with_memory_space_constraint(t, pltpu.HBM) for t in (*srcs, *lands)])
    return out[0], out[1], out[2:2 + n], out[2 + n:2 + 2 * n], out[-1][0:1, 0:1]


def _split_wait(name, mode, send_sems, recv_sems, srcs, lands, after):
    n = len(srcs)

    def body(*refs):
        src, land = refs[:n], refs[n:2 * n]
        ssem, rsem = refs[2 * n], refs[2 * n + 1]
        place = _my_place()
        for w in range(n):
            for j in range(len(SPLIT_MASKS[mode])):
                _split_copy(mode, src[w], land[w], ssem, rsem, w, j, place).wait_send()
                _split_copy(mode, src[w], land[w], ssem, rsem, w, j, place, arriving=True).wait_recv()

    hbm = lambda t: pltpu.HBM(t.shape, t.dtype)
    out = pl.pallas_call(
        body, name=name,
        out_shape=tuple(hbm(t) for t in (*srcs, *lands)),
        in_specs=(HBM_SPEC,) * (2 * n) + (SEM_SPEC, SEM_SPEC, pl.BlockSpec(memory_space=pl.ANY)),
        out_specs=(HBM_SPEC,) * (2 * n),
        input_output_aliases={i: i for i in range(2 * n)},
        compiler_params=pltpu.CompilerParams(has_side_effects=pltpu.SideEffectType.DATAFLOW_SIDE_EFFECTING),
    )(*srcs, *lands, send_sems, recv_sems, after)
    return out[:n], out[n:]


def _sibling_exchange(grads):
    nw = len(grads)
    HBM = pl.BlockSpec(memory_space=pl.ANY)

    def body(*refs):
        g, land = refs[:nw], refs[nw:2 * nw]
        send_sems, recv_sems = refs[2 * nw:]
        x, y, c = _my_place()

        def copy(w, q, owner_core):
            r = land[w].shape[1]
            return pltpu.make_async_remote_copy(
                src_ref=g[w].at[pl.ds(pl.multiple_of((2 * q + owner_core) * r, 16), r), :], dst_ref=land[w].at[q],
                send_sem=send_sems.at[w, q], recv_sem=recv_sems.at[w, q], device_id=(x, y, 1 - c), device_id_type=MESH)

        sends = [copy(w, q, 1 - c) for w in range(nw) for q in range(4)]
        for cp in sends:
            cp.start()
        for w in range(nw):
            for q in range(4):
                copy(w, q, c).wait_recv()
        for cp in sends:
            cp.wait_send()

    return pl.pallas_call(
        body, name="sibling_exchange",
        out_shape=[jax.ShapeDtypeStruct((4, a.shape[0] // N_DEV, a.shape[1]), a.dtype) for a in grads],
        in_specs=[HBM] * nw, out_specs=[HBM] * nw,
        scratch_shapes=[pltpu.SemaphoreType.DMA((nw, 4)), pltpu.SemaphoreType.DMA((nw, 4))],
    )(*grads)


def _pair_sum(g, sib, core, name):
    _, r, ccols = sib.shape
    tr = _row_tile(r)

    def body(core_ref, g_ref, s_ref, o_ref):
        o_ref[0] = (g_ref[0, 0].astype(F32) + s_ref[0].astype(F32)).astype(BF16)

    return pl.pallas_call(
        body, name=name,
        grid_spec=pltpu.PrefetchScalarGridSpec(
            num_scalar_prefetch=1, grid=(4, r // tr),
            in_specs=[pl.BlockSpec((1, 1, tr, ccols), lambda q, i, core_ref: (q, core_ref[0], i, 0)),
                      pl.BlockSpec((1, tr, ccols), lambda q, i, core_ref: (q, i, 0))],
            out_specs=pl.BlockSpec((1, tr, ccols), lambda q, i, core_ref: (q, i, 0))),
        out_shape=jax.ShapeDtypeStruct(sib.shape, BF16),
        compiler_params=_cparams("parallel", "parallel"),
    )(core, g.reshape(4, 2, r, ccols), sib)


def _allgather_small(v, name):
    r, ccols = v.shape

    def body(v_ref, out_ref, send_sems, recv_sems):
        x, y, c = _my_place()
        my_idx = _dev_index(x, y, c)
        out_ref[my_idx] = v_ref[...]

        def copy(m):
            peer = _peer(x, y, c, m)
            return pltpu.make_async_remote_copy(
                src_ref=v_ref, dst_ref=out_ref.at[my_idx],
                send_sem=send_sems.at[m - 1], recv_sem=recv_sems.at[m - 1], device_id=peer, device_id_type=MESH)

        def arrival(m):
            peer = _peer(x, y, c, m)
            return pltpu.make_async_remote_copy(
                src_ref=v_ref, dst_ref=out_ref.at[_dev_index(*peer)],
                send_sem=send_sems.at[m - 1], recv_sem=recv_sems.at[m - 1], device_id=peer, device_id_type=MESH)

        sends = [copy(m) for m in range(1, N_DEV)]
        for cp in sends:
            cp.start()
        for m in range(1, N_DEV):
            arrival(m).wait_recv()
        for cp in sends:
            cp.wait_send()

    return pl.pallas_call(
        body, name=name,
        out_shape=jax.ShapeDtypeStruct((N_DEV, r, ccols), v.dtype),
        in_specs=[pl.BlockSpec(memory_space=pltpu.VMEM)], out_specs=pl.BlockSpec(memory_space=pltpu.VMEM),
        scratch_shapes=[pltpu.SemaphoreType.DMA((7,)), pltpu.SemaphoreType.DMA((7,))],
    )(v)


def _ada_fwd(c_all, w_ada, b_cols):
    def body(c_ref, w_ref, b_ref, mod_ref, act_ref):
        cv = c_ref[...]
        act = cv * _sigmoid(cv)
        act_ref[...] = act
        mod_ref[...] = jnp.dot(act, w_ref[...], preferred_element_type=F32, precision=lax.Precision.HIGHEST) + b_ref[...]

    return pl.pallas_call(
        body, name="ada_fwd",
        out_shape=[jax.ShapeDtypeStruct((N_DEV, w_ada.shape[1]), F32), jax.ShapeDtypeStruct((N_DEV, D), F32)],
        compiler_params=_cparams(),
    )(c_all, w_ada, b_cols)


def _ada_bwd(act_t, gm_cols):
    def body(a_ref, g_ref, o_ref):
        o_ref[...] = jnp.dot(a_ref[...], g_ref[...], preferred_element_type=F32, precision=lax.Precision.HIGHEST)

    return pl.pallas_call(
        body, name="ada_bwd", out_shape=jax.ShapeDtypeStruct((D, gm_cols.shape[1]), F32), compiler_params=_cparams(),
    )(act_t, gm_cols)


def _row_tile(r):
    for t in (256, 304, 128, 64, 16):
        if r % t == 0:
            return t
    return r


def _sum_parts(parts, name, own=None):
    k, r, ccols = parts.shape
    tr = _row_tile(r)

    def body(*refs):
        p_ref, o_ref = refs[0], refs[-1]
        acc = p_ref[0].astype(F32) if own is None else refs[1][...].astype(F32) + p_ref[0].astype(F32)
        for s in range(1, k):
            acc = acc + p_ref[s].astype(F32)
        o_ref[...] = acc

    blk = pl.BlockSpec((tr, ccols), lambda i: (i, 0))
    return pl.pallas_call(
        body, name=name, grid=(r // tr,),
        in_specs=[pl.BlockSpec((k, tr, ccols), lambda i: (0, i, 0))] + ([] if own is None else [blk]),
        out_specs=blk,
        out_shape=jax.ShapeDtypeStruct((r, ccols), F32),
        compiler_params=_cparams("parallel"),
    )(*((parts,) if own is None else (parts, own)))


def _adamw(w, g, m, v, name):
    r, ccols = w.shape
    tr = _row_tile(r)
    c1 = 1.0 / (1.0 - B1 ** STEP)
    c2 = 1.0 / (1.0 - B2 ** STEP)

    def body(w_ref, g_ref, m_ref, v_ref, d_ref, nm_ref, nv_ref):
        gv = g_ref[...]
        nm = B1 * m_ref[...] + (1.0 - B1) * gv
        nv = B2 * v_ref[...] + (1.0 - B2) * jnp.square(gv)
        nm_ref[...] = nm
        nv_ref[...] = nv
        d_ref[...] = -LR * ((nm * c1) / (jnp.sqrt(nv * c2) + ADAM_EPS) + WD * w_ref[...])

    blk = pl.BlockSpec((tr, ccols), lambda i: (i, 0))
    return pl.pallas_call(
        body, name=name, grid=(r // tr,), in_specs=[blk] * 4, out_specs=[blk] * 3,
        out_shape=[jax.ShapeDtypeStruct((r, ccols), F32)] * 3,
        compiler_params=_cparams("parallel"),
    )(w, g, m, v)


def _pack_vectors(b_ada, g_mix, g_mlp, g_fin, b_gate, conv_w):
    conv_rows = jnp.pad(conv_w.reshape(3, HEAD), ((0, 0), (0, D - HEAD)))
    return jnp.concatenate([b_ada.reshape(6, D), g_mix.reshape(1, D), g_mlp.reshape(1, D), g_fin.reshape(1, D),
                            b_gate.reshape(2, D), conv_rows, jnp.zeros((2, D), F32)], axis=0)


def _unpack_vectors(p):
    return (p[0:6].reshape(1, 6 * D), p[6:7], p[9:11].reshape(1, 2 * D), p[11:14, :HEAD].reshape(1, 3, HEAD),
            p[7:8], p[8])


def kernel(x, c, w_ada, b_ada, g_norm_mix, w_in, b_gate, conv_w, w_branch_attn, w_branch_conv, w_out, g_norm_mlp, w_mlp_in, w_mlp_out, g_norm_final, loss_target, m_w_ada, m_b_ada, m_g_norm_mix, m_w_in, m_b_gate, m_conv_w, m_w_branch_attn, m_w_branch_conv, m_w_out, m_g_norm_mlp, m_w_mlp_in, m_w_mlp_out, m_g_norm_final, v_w_ada, v_b_ada, v_g_norm_mix, v_w_in, v_b_gate, v_conv_w, v_w_branch_attn, v_w_branch_conv, v_w_out, v_g_norm_mlp, v_w_mlp_in, v_w_mlp_out, v_g_norm_final):
    S = x.shape[1]
    xi, yi, ci = _my_place()
    me = _dev_index(xi, yi, ci)
    x2 = x.reshape(S, D)
    tgt = loss_target.reshape(S, D)

    pay = jnp.zeros((8, D), F32).at[0].set(c[0]).at[1:4, :HEAD].set(conv_w[0])
    got = _allgather_small(pay, "gather_cond")
    c_all = got[:, 0, :]
    cw8 = jnp.pad(got[:, 1:4, :HEAD].transpose(1, 0, 2).reshape(3, D), ((0, 5), (0, 0)))
    ncol = w_ada.shape[2]
    b_cols = lax.dynamic_slice(b_ada, (0, me * ncol), (1, ncol))
    mod_cols, act = _ada_fwd(c_all, w_ada[0], b_cols)
    mod_all = _allgather_small(mod_cols, "gather_mod")

    w_in_shard, mod_all = lax.optimization_barrier((w_in[0].T.astype(BF16), mod_all))
    mod = lax.dynamic_index_in_dim(mod_all, me, axis=1, keepdims=False).reshape(6, D)
    (w_int,) = _allgather_weights([w_in_shard])
    late = [w_branch_attn[0].T.astype(BF16), w_branch_conv[0].astype(BF16), w_out[0].astype(BF16),
            w_mlp_in[0].T.astype(BF16), w_mlp_out[0].astype(BF16)]
    w_int, late = lax.optimization_barrier((w_int, late))
    zones = [lax.dynamic_update_slice(lax.empty((N_DEV * t.shape[0], t.shape[1]), BF16), t, (me * t.shape[0], 0)) for t in late]
    ag = _split_start("gather_late_start", "gather", late, zones)

    def late_weights(o_attn):
        return _split_wait("gather_late_wait", "gather", ag[0], ag[1], ag[2], ag[3], o_attn)[1]

    rs = {}

    def mlp_grads_ready(*grads):
        lands = [lax.empty((N_PEER, t.shape[0] // N_DEV, t.shape[1]), BF16) for t in grads]
        rs["mlp"] = _split_start("scatter_mlp_start", "scatter", grads, lands)
        return rs["mlp"][4]

    def other_grads_ready(*grads):
        core = ci.reshape(1).astype(jnp.int32)
        pair = [_pair_sum(g, sib, core, "pair_sum_%d" % k) for k, (g, sib) in enumerate(zip(grads, _sibling_exchange(grads)))]
        lands = [lax.empty((3,) + t.shape[1:], BF16) for t in pair]
        rs["rest"] = _split_start("scatter_rest_start", "chips", pair, lands)
        return rs["rest"][4]

    ba, bb = b_gate[:, :D], b_gate[:, D:]
    loss_part, grad_x, vec = _local_step(
        x2, tgt, mod + ag[4], g_norm_mix, g_norm_mlp, g_norm_final.reshape(1, D), ba, bb, cw8, w_int, late_weights,
        mlp_grads_ready, other_grads_ready)
    loss = lax.psum(loss_part, AXES)

    vec_all = _allgather_small(vec, "gather_vec")
    vec_sum = _sum_parts(vec_all, "sum_vec")
    gm_all = vec_all[:, 0:6, :].reshape(N_DEV, 6 * D)
    gm_cols = lax.dynamic_slice(gm_all, (0, me * ncol), (N_DEV, ncol))
    g_w_ada = _ada_bwd(act.T, gm_cols)
    conv_cols = lax.dynamic_slice(vec_sum[11:14], (0, me * HEAD), (3, HEAD))
    g_pack = jnp.concatenate([vec_sum[0:11], jnp.pad(conv_cols, ((0, 0), (0, D - HEAD))), jnp.zeros((2, D), F32)], axis=0)
    packs = [_pack_vectors(*t) for t in ((b_ada, g_norm_mix, g_norm_mlp, g_norm_final, b_gate, conv_w),
                                         (m_b_ada, m_g_norm_mix, m_g_norm_mlp, m_g_norm_final, m_b_gate, m_conv_w),
                                         (v_b_ada, v_g_norm_mix, v_g_norm_mlp, v_g_norm_final, v_b_gate, v_conv_w))]
    d_pack, m_pack, v_pack = _adamw(packs[0], g_pack, packs[1], packs[2], "adamw_vectors")
    d_ada, nm_ada, nv_ada = _adamw(w_ada[0], g_w_ada, m_w_ada[0], v_w_ada[0], "adamw_w_ada")

    sums = {}
    srcs, lands = _split_wait("scatter_mlp_wait", "scatter", *rs["mlp"][:4], grad_x)
    for n, g, land in zip(("w_mi", "w_mo"), srcs, lands):
        r = land.shape[1]
        sums[n] = _sum_parts(land, "sum_" + n, own=lax.dynamic_slice(g, (me * r, 0), (r, g.shape[1])))
    srcs, lands = _split_wait("scatter_rest_wait", "chips", *rs["rest"][:4], grad_x)
    for n, pair, land in zip(("w_in", "w_ba", "w_bc", "w_out"), srcs, lands):
        sums[n] = _sum_parts(land, "sum_" + n, own=lax.dynamic_index_in_dim(pair, 2 * xi + yi, axis=0, keepdims=False))
    g_in, g_ba, g_bc, g_out, g_mi, g_mo = sums["w_in"].T, sums["w_ba"].T, sums["w_bc"], sums["w_out"], sums["w_mi"].T, sums["w_mo"]
    big = {}
    for n, w, g, m, v in (("w_in", w_in, g_in, m_w_in, v_w_in), ("w_ba", w_branch_attn, g_ba, m_w_branch_attn, v_w_branch_attn),
                          ("w_bc", w_branch_conv, g_bc, m_w_branch_conv, v_w_branch_conv), ("w_out", w_out, g_out, m_w_out, v_w_out),
                          ("w_mi", w_mlp_in, g_mi, m_w_mlp_in, v_w_mlp_in), ("w_mo", w_mlp_out, g_mo, m_w_mlp_out, v_w_mlp_out)):
        big[n] = (g[None],) + tuple(t[None] for t in _adamw(w[0], g, m[0], v[0], "adamw_" + n))

    gv = _unpack_vectors(g_pack)
    dv = _unpack_vectors(d_pack)
    mv = _unpack_vectors(m_pack)
    vv = _unpack_vectors(v_pack)

    def ordered(k, ada, vecs):
        return (ada[None], vecs[0], vecs[1], big["w_in"][k], vecs[2], vecs[3], big["w_ba"][k], big["w_bc"][k],
                big["w_out"][k], vecs[4], big["w_mi"][k], big["w_mo"][k], vecs[5])

    return (loss, grad_x.reshape(1, S, D), *ordered(0, g_w_ada, gv), *ordered(1, d_ada, dv),
            *ordered(2, nm_ada, mv), *ordered(3, nv_ada, vv))
```

```python
import functools

import numpy as np
import jax
import jax.numpy as jnp
from jax import lax
from jax.experimental import pallas as pl
from jax.experimental.pallas import tpu as pltpu

F32, BF16 = jnp.float32, jnp.bfloat16
D = 1024
HEAD = 128
DILATIONS = (1, 4, 16)
N_SLOT = 4
AOW = N_SLOT * HEAD
DFF = 4 * D
N_DEV = 8
UNROLL = 8
EPS = 1e-6
NEG = -1e30
SCALE = HEAD ** -0.5
LR, B1, B2, ADAM_EPS, WD, STEP = 0.001, 0.9, 0.999, 1e-08, 0.01, 10
V7X_VMEM_LIMIT = 56 * 1024 * 1024
TM = 1024
MESH = pl.DeviceIdType.MESH
AXES = ("x", "y", "c")


def _cparams(*sem):
    if sem:
        return pltpu.CompilerParams(dimension_semantics=sem, vmem_limit_bytes=V7X_VMEM_LIMIT)
    return pltpu.CompilerParams(vmem_limit_bytes=V7X_VMEM_LIMIT)


def _nn(a, b):
    return jnp.dot(a, b, preferred_element_type=F32)


def _nt(a, b):
    return lax.dot_general(a, b, (((1,), (1,)), ((), ())), preferred_element_type=F32)


def _tn(a, b):
    return lax.dot_general(a, b, (((0,), (0,)), ((), ())), preferred_element_type=F32)


def _rms_r(x):
    return lax.rsqrt(jnp.mean(x * x, axis=-1, keepdims=True) + EPS)


def _rms_bwd(x, r, g, dn):
    gy = dn * g
    dx = r * gy - x * (r * r * r) * jnp.mean(x * gy, axis=-1, keepdims=True)
    return dx, dn * (x * r)


def _sigmoid(t):
    return 1.0 / (1.0 + jnp.exp(-t))


def _rowsum(v):
    return jnp.sum(v, axis=0, keepdims=True)


def _vec_spec(n=D):
    return pl.BlockSpec((1, n), lambda *_: (0, 0))


def _const_spec(shape):
    nd = len(shape)
    return pl.BlockSpec(shape, lambda *_: (0,) * nd)


def _win_rowblock(j):
    return jnp.where(j < 9, (j % 3) * 3 + j // 3, j)


def _proj(x, g, sc, sh, w_int):
    S = x.shape[0]
    tm = 2 * TM

    def body(x_ref, g_ref, sc_ref, sh_ref, w_ref, h_ref, q_ref, e_ref):
        j = pl.program_id(1)

        @pl.when(j == 0)
        def _():
            xv = x_ref[...]
            h = xv * _rms_r(xv) * g_ref[...] * (1.0 + sc_ref[...]) + sh_ref[...]
            h_ref[...] = h.astype(BF16)

        acc = _nt(h_ref[...], w_ref[...])

        @pl.when(j < 9)
        def _():
            q_ref[0] = acc

        @pl.when(j >= 9)
        def _():
            e_ref[0] = acc.astype(BF16)

    def e_idx(i, j):
        k = jnp.maximum(j - 9, 0)
        return (k // 2, i, k % 2)

    return pl.pallas_call(
        body, name="proj", grid=(S // tm, 19),
        in_specs=[pl.BlockSpec((tm, D), lambda i, j: (i, 0)), _vec_spec(), _vec_spec(), _vec_spec(),
                  pl.BlockSpec((512, D), lambda i, j: (_win_rowblock(j), 0))],
        out_specs=[pl.BlockSpec((tm, D), lambda i, j: (i, 0)),
                   pl.BlockSpec((1, tm, 512), lambda i, j: (jnp.minimum(j, 8), i, 0)),
                   pl.BlockSpec((1, tm, 512), e_idx)],
        out_shape=[jax.ShapeDtypeStruct((S, D), BF16), jax.ShapeDtypeStruct((9, S, 512), F32),
                   jax.ShapeDtypeStruct((5, S, D), BF16)],
        compiler_params=_cparams("parallel", "arbitrary"),
    )(x, g, sc, sh, w_int)


def _bias_table():
    slopes = (2.0 ** (-8.0 * np.arange(1, 13, dtype=np.float32) / 12.0)).astype(np.float32)
    qi = np.arange(HEAD)[:, None]
    kj = np.arange(2 * HEAD)[None, :]
    delta = HEAD + qi - kj
    mask = (delta >= 0) & (delta <= HEAD)
    out = np.zeros((3, N_SLOT, HEAD, 2 * HEAD), np.float32)
    for gi, d in enumerate(DILATIONS):
        for j in range(N_SLOT):
            bias = -slopes[gi * N_SLOT + j] * (delta * d).astype(np.float32)
            out[gi, j] = np.where(mask, bias, NEG)
    out_t = np.concatenate([out[..., HEAD:].swapaxes(-1, -2), out[..., :HEAD].swapaxes(-1, -2)], axis=-1)
    return jnp.asarray(out), jnp.asarray(out_t)


def _block_rows(b, d):
    r = b % d
    n = b // d
    st = n * (HEAD * d) + r
    stp = jnp.maximum(n - 1, 0) * (HEAD * d) + r
    return n, st, stp


def _attn_fwd(qkv, bias):
    S = qkv.shape[2]
    nblk = S // HEAD
    rows = 256

    def body(qkv_ref, b_ref, o_ref, lse_ref, o_s, lse_s):
        g = pl.program_id(1)
        bias = b_ref[0, 0]
        col = lax.broadcasted_iota(jnp.int32, bias.shape, 1)
        bias_first = jnp.where(col < HEAD, NEG, bias)

        for gi, d in enumerate(DILATIONS):
            @pl.when(g == gi)
            def _(gi=gi, d=d):
                def step(b, carry):
                    n, st, stp = _block_rows(b, d)
                    cur = pl.ds(st, HEAD, stride=d)
                    prv = pl.ds(stp, HEAD, stride=d)
                    q = qkv_ref.at[0, 0][cur, :].astype(BF16)
                    kw = jnp.concatenate([qkv_ref.at[0, 1][prv, :], qkv_ref.at[0, 1][cur, :]], axis=0).astype(BF16)
                    vw = jnp.concatenate([qkv_ref.at[0, 2][prv, :], qkv_ref.at[0, 2][cur, :]], axis=0).astype(BF16)
                    s = _nt(q, kw) * SCALE + jnp.where(n > 0, bias, bias_first)
                    m = jnp.max(s, axis=-1, keepdims=True)
                    p = jnp.exp(s - m)
                    l = jnp.sum(p, axis=-1, keepdims=True)
                    o_s.at[gi][cur, :] = _nn(p.astype(BF16), vw) / l
                    lse_s.at[gi][cur, :] = jnp.broadcast_to(m + jnp.log(l), (HEAD, HEAD))
                    return carry

                lax.fori_loop(0, nblk, step, 0, unroll=UNROLL)

        @pl.when(g == len(DILATIONS) - 1)
        def _():
            def merge(i, carry):
                r = pl.ds(pl.multiple_of(i * rows, rows), rows)
                ls = [lse_s[k, r, :] for k in range(3)]
                top = jnp.maximum(jnp.maximum(ls[0], ls[1]), ls[2])
                ws = [jnp.exp(t - top) for t in ls]
                den = ws[0] + ws[1] + ws[2]
                o_ref[r, :] = (ws[0] * o_s[0, r, :] + ws[1] * o_s[1, r, :] + ws[2] * o_s[2, r, :]) / den
                lse_ref[r, :] = top + jnp.log(den)
                return carry

            lax.fori_loop(0, S // rows, merge, 0)

    return pl.pallas_call(
        body, name="attn_fwd", grid=(N_SLOT, 3),
        in_specs=[pl.BlockSpec((1, 3, S, HEAD), lambda j, g: (g, 0, 0, j)),
                  pl.BlockSpec((1, 1, HEAD, 2 * HEAD), lambda j, g: (g, j, 0, 0))],
        out_specs=[pl.BlockSpec((S, HEAD), lambda j, g: (0, j)), pl.BlockSpec((S, HEAD), lambda j, g: (0, j))],
        out_shape=[jax.ShapeDtypeStruct((S, AOW), F32), jax.ShapeDtypeStruct((S, AOW), F32)],
        scratch_shapes=[pltpu.VMEM((3, S, HEAD), F32)] * 2,
        compiler_params=_cparams("parallel", "arbitrary"),
    )(qkv, bias)


def _shift_down(z, k, halo_rows):
    out = pltpu.roll(z, k, axis=0)
    rid = lax.broadcasted_iota(jnp.int32, z.shape, 0)
    for t in range(k):
        out = jnp.where(rid == t, halo_rows[t], out)
    return out


def _shift_up(z, k, halo_rows):
    n = z.shape[0]
    out = pltpu.roll(z, n - k, axis=0)
    rid = lax.broadcasted_iota(jnp.int32, z.shape, 0)
    for t in range(k):
        out = jnp.where(rid == n - k + t, halo_rows[t], out)
    return out


def _e_spec(chunk, tm):
    return pl.BlockSpec((1, tm, D), lambda i, c=chunk: (c, i, 0))


def _e_prev_spec(chunk, tm):
    return pl.BlockSpec((1, 16, D), lambda i, c=chunk: (c, jnp.maximum(i * (tm // 16) - 1, 0), 0))


def _e_next_spec(chunk, tm, S):
    return pl.BlockSpec((1, 16, D), lambda i, c=chunk: (c, jnp.minimum((i + 1) * (tm // 16), S // 16 - 1), 0))


def _mix(o_attn, e, cw8, ba, bb, w_bat, w_bc):
    S = o_attn.shape[0]
    tm = 256

    def body(o_ref, cb_ref, cc_ref, cx_ref, ga_ref, gb_ref, ccp_ref, cxp_ref, cw_ref, ba_ref, bb_ref, wba_ref, wbc_ref,
             obf_ref, cbu_ref, ya_ref, yc_ref, mg_ref):
        i = pl.program_id(0)
        o = o_ref[...].astype(BF16)
        obf_ref[...] = o
        ya = _nt(o, wba_ref[...])
        z = cc_ref[0].astype(F32) * cx_ref[0].astype(F32)
        zp = ccp_ref[0].astype(F32) * cxp_ref[0].astype(F32) * (i > 0).astype(F32)
        z1 = _shift_down(z, 1, [zp[15:16]])
        z2 = _shift_down(z, 2, [zp[14:15], zp[15:16]])
        cw = cw_ref[...]
        u = cw[0:1] * z2 + cw[1:2] * z1 + cw[2:3] * z
        cbu = (cb_ref[0].astype(F32) * u).astype(BF16)
        cbu_ref[...] = cbu
        yc = _nn(cbu, wbc_ref[...])
        sa = _sigmoid(ga_ref[0].astype(F32) + ba_ref[...])
        sb = _sigmoid(gb_ref[0].astype(F32) + bb_ref[...])
        ya_ref[...] = ya.astype(BF16)
        yc_ref[...] = yc.astype(BF16)
        mg_ref[...] = (sa * ya + sb * yc).astype(BF16)

    row = lambda w: pl.BlockSpec((tm, w), lambda i: (i, 0))
    return pl.pallas_call(
        body, name="mix", grid=(S // tm,),
        in_specs=[row(AOW)] + [_e_spec(c, tm) for c in range(5)] + [_e_prev_spec(1, tm), _e_prev_spec(2, tm),
                  _const_spec((8, D)), _vec_spec(), _vec_spec(), _const_spec((D, AOW)), _const_spec((D, D))],
        out_specs=[row(AOW), row(D), row(D), row(D), row(D)],
        out_shape=[jax.ShapeDtypeStruct((S, AOW), BF16)] + [jax.ShapeDtypeStruct((S, D), BF16)] * 4,
        compiler_params=_cparams("parallel"),
    )(o_attn, e, e, e, e, e, e, e, cw8, ba, bb, w_bat, w_bc)


def _out_proj(merged, w_out, x, gate1, g_mlp, sc2, sh2):
    S = x.shape[0]
    tm = TM

    def body(mg_ref, w_ref, x_ref, gt_ref, g_ref, sc_ref, sh_ref, x1_ref, mo_ref, h2_ref):
        mo = _nn(mg_ref[...], w_ref[...])
        mo_ref[...] = mo.astype(BF16)
        x1 = x_ref[...] + gt_ref[...] * mo
        x1_ref[...] = x1
        h2 = x1 * _rms_r(x1) * g_ref[...] * (1.0 + sc_ref[...]) + sh_ref[...]
        h2_ref[...] = h2.astype(BF16)

    row = pl.BlockSpec((tm, D), lambda i: (i, 0))
    return pl.pallas_call(
        body, name="out_proj", grid=(S // tm,),
        in_specs=[row, _const_spec((D, D)), row, _vec_spec(), _vec_spec(), _vec_spec(), _vec_spec()],
        out_specs=[row, row, row],
        out_shape=[jax.ShapeDtypeStruct((S, D), F32), jax.ShapeDtypeStruct((S, D), BF16), jax.ShapeDtypeStruct((S, D), BF16)],
        compiler_params=_cparams("parallel"),
    )(merged, w_out, x, gate1, g_mlp, sc2, sh2)


def _mlp_in(h2, w_mit):
    S = h2.shape[0]
    tm, tn = TM, 2048

    def body(h_ref, w_ref, a_ref, f_ref):
        a = _nt(h_ref[...], w_ref[...])
        a_ref[...] = a.astype(BF16)
        f_ref[...] = jnp.square(jnp.maximum(a, 0.0)).astype(BF16)

    blk = pl.BlockSpec((tm, tn), lambda i, j: (i, j))
    return pl.pallas_call(
        body, name="mlp_in", grid=(S // tm, DFF // tn),
        in_specs=[pl.BlockSpec((tm, D), lambda i, j: (i, 0)), pl.BlockSpec((tn, D), lambda i, j: (j, 0))],
        out_specs=[blk, blk],
        out_shape=[jax.ShapeDtypeStruct((S, DFF), BF16)] * 2,
        compiler_params=_cparams("parallel", "parallel"),
    )(h2, w_mit)


def _mlp_out(f, w_mo, x1, gate2, g_fin, tgt):
    S = x1.shape[0]
    tm, tk = TM, 1024
    nk = DFF // tk

    def body(f_ref, w_ref, x1_ref, gt_ref, g_ref, t_ref, mlp_ref, dx2_ref, pv_ref, acc):
        i, k = pl.program_id(0), pl.program_id(1)

        @pl.when((i == 0) & (k == 0))
        def _():
            pv_ref[...] = jnp.zeros_like(pv_ref)

        @pl.when(k == 0)
        def _():
            acc[...] = jnp.zeros_like(acc)

        acc[...] += _nn(f_ref[...], w_ref[...])

        @pl.when(k == nk - 1)
        def _():
            mlp = acc[...]
            mlp_ref[...] = mlp.astype(BF16)
            x2 = x1_ref[...] + gt_ref[...] * mlp
            r = _rms_r(x2)
            g = g_ref[...]
            err = x2 * r * g - t_ref[...]
            dy = err * (1.0 / D)
            dx2, pg = _rms_bwd(x2, r, g, dy)
            dx2_ref[...] = dx2
            pv_ref[0:1, :] += _rowsum(pg)
            pv_ref[1:2, :] += 0.5 * _rowsum(jnp.mean(err * err, axis=-1, keepdims=True))

    row = pl.BlockSpec((tm, D), lambda i, k: (i, 0))
    return pl.pallas_call(
        body, name="mlp_out", grid=(S // tm, nk),
        in_specs=[pl.BlockSpec((tm, tk), lambda i, k: (i, k)), pl.BlockSpec((tk, D), lambda i, k: (k, 0)),
                  row, _vec_spec(), _vec_spec(), row],
        out_specs=[row, row, _const_spec((8, D))],
        out_shape=[jax.ShapeDtypeStruct((S, D), BF16), jax.ShapeDtypeStruct((S, D), F32), jax.ShapeDtypeStruct((8, D), F32)],
        scratch_shapes=[pltpu.VMEM((tm, D), F32)],
        compiler_params=_cparams("arbitrary", "arbitrary"),
    )(f, w_mo, x1, gate2, g_fin, tgt)


def _bwd_mlp_a(dx2, gate2, mlp, w_mo, a):
    S = dx2.shape[0]
    tm, tn = TM, 2048

    def body(dx_ref, gt_ref, mlp_ref, w_ref, a_ref, da_ref, dmo_ref, pv_ref):
        i, j = pl.program_id(0), pl.program_id(1)

        @pl.when((i == 0) & (j == 0))
        def _():
            pv_ref[...] = jnp.zeros_like(pv_ref)

        @pl.when(j == 0)
        def _():
            dx = dx_ref[...]
            dmo_ref[...] = (dx * gt_ref[...]).astype(BF16)
            pv_ref[0:1, :] += _rowsum(dx * mlp_ref[...].astype(F32))

        df = _nt(dmo_ref[...], w_ref[...])
        da_ref[...] = (df * (2.0 * jnp.maximum(a_ref[...].astype(F32), 0.0))).astype(BF16)

    row = pl.BlockSpec((tm, D), lambda i, j: (i, 0))
    blk = pl.BlockSpec((tm, tn), lambda i, j: (i, j))
    return pl.pallas_call(
        body, name="bwd_mlp_a", grid=(S // tm, DFF // tn),
        in_specs=[row, _vec_spec(), row, pl.BlockSpec((tn, D), lambda i, j: (j, 0)), blk],
        out_specs=[blk, row, _const_spec((8, D))],
        out_shape=[jax.ShapeDtypeStruct((S, DFF), BF16), jax.ShapeDtypeStruct((S, D), BF16), jax.ShapeDtypeStruct((8, D), F32)],
        compiler_params=_cparams("arbitrary", "arbitrary"),
    )(dx2, gate2, mlp, w_mo, a)


def _bwd_mlp_b(da, w_mit, x1, dx2, g_mlp, sc2):
    S = x1.shape[0]
    tm, tk = TM, 1024
    nk = DFF // tk

    def body(da_ref, w_ref, x1_ref, dx2_ref, g_ref, sc_ref, dx1_ref, pv_ref, acc):
        i, k = pl.program_id(0), pl.program_id(1)

        @pl.when((i == 0) & (k == 0))
        def _():
            pv_ref[...] = jnp.zeros_like(pv_ref)

        @pl.when(k == 0)
        def _():
            acc[...] = jnp.zeros_like(acc)

        acc[...] += _nn(da_ref[...], w_ref[...])

        @pl.when(k == nk - 1)
        def _():
            dh = acc[...]
            x1 = x1_ref[...]
            r = _rms_r(x1)
            g = g_ref[...]
            dxn, pg = _rms_bwd(x1, r, g, dh * (1.0 + sc_ref[...]))
            dx1_ref[...] = dx2_ref[...] + dxn
            pv_ref[0:1, :] += _rowsum(dh)
            pv_ref[1:2, :] += _rowsum(dh * (x1 * r * g))
            pv_ref[2:3, :] += _rowsum(pg)

    row = pl.BlockSpec((tm, D), lambda i, k: (i, 0))
    return pl.pallas_call(
        body, name="bwd_mlp_b", grid=(S // tm, nk),
        in_specs=[pl.BlockSpec((tm, tk), lambda i, k: (i, k)), pl.BlockSpec((tk, D), lambda i, k: (k, 0)),
                  row, row, _vec_spec(), _vec_spec()],
        out_specs=[row, _const_spec((8, D))],
        out_shape=[jax.ShapeDtypeStruct((S, D), F32), jax.ShapeDtypeStruct((8, D), F32)],
        scratch_shapes=[pltpu.VMEM((tm, D), F32)],
        compiler_params=_cparams("arbitrary", "arbitrary"),
    )(da, w_mit, x1, dx2, g_mlp, sc2)


def _bwd_mix(dx1, gate1, mo, e, cw8, ba, bb, ya, yc, o_attn, w_out, w_bc, w_bat):
    S = dx1.shape[0]
    tm = 256
    n_tiles = S // tm

    def body(dx_ref, dxn_ref, gt_ref, mo_ref, cb_ref, cc_ref, cx_ref, ga_ref, gb_ref, cbn_ref, gbn_ref, ccp_ref, cxp_ref,
             cw_ref, ba_ref, bb_ref, ya_ref, yc_ref, o_ref, wout_ref, wbc_ref, wba_ref,
             dmo_ref, dya_ref, dyc_ref, do_ref, dl_ref, de_ref, pv_ref):
        i = pl.program_id(0)

        @pl.when(i == 0)
        def _():
            pv_ref[...] = jnp.zeros_like(pv_ref)

        gate = gt_ref[...]
        bbv = bb_ref[...]

        def conv_branch_grad(dx_rows, gb_rows):
            dmo = (dx_rows * gate).astype(BF16)
            dmg = _nt(dmo, wout_ref[...])
            sb = _sigmoid(gb_rows + bbv)
            dyc = dmg * sb
            return dmo, dmg, sb, dyc, _nt(dyc.astype(BF16), wbc_ref[...])

        dx = dx_ref[...]
        cb = cb_ref[0].astype(F32)
        cc = cc_ref[0].astype(F32)
        cx = cx_ref[0].astype(F32)
        dmo, dmg, sb, dyc, dcbu = conv_branch_grad(dx, gb_ref[0].astype(F32))
        dmo_ref[...] = dmo
        pv_ref[0:1, :] += _rowsum(dx * mo_ref[...].astype(F32))
        sa = _sigmoid(ga_ref[0].astype(F32) + ba_ref[...])
        dya = (dmg * sa).astype(BF16)
        dya_ref[...] = dya
        dyc_ref[...] = dyc.astype(BF16)
        dga = dmg * ya_ref[...].astype(F32) * sa * (1.0 - sa)
        dgb = dmg * yc_ref[...].astype(F32) * sb * (1.0 - sb)
        pv_ref[1:2, :] += _rowsum(dga)
        pv_ref[2:3, :] += _rowsum(dgb)

        do = _nn(dya, wba_ref[...])
        do_ref[...] = do
        prod = do * o_ref[...]
        dl_ref[...] = jnp.concatenate(
            [jnp.broadcast_to(jnp.sum(prod[:, s * HEAD:(s + 1) * HEAD], axis=-1, keepdims=True), (tm, HEAD))
             for s in range(N_SLOT)], axis=1)

        z = cc * cx
        zp = ccp_ref[0].astype(F32) * cxp_ref[0].astype(F32) * (i > 0).astype(F32)
        z1 = _shift_down(z, 1, [zp[15:16]])
        z2 = _shift_down(z, 2, [zp[14:15], zp[15:16]])
        cw = cw_ref[...]
        u = cw[0:1] * z2 + cw[1:2] * z1 + cw[2:3] * z
        du = dcbu * cb
        dcbu_n = conv_branch_grad(dxn_ref[...], gbn_ref[0].astype(F32))[4]
        du_n = dcbu_n * cbn_ref[0].astype(F32) * (i < n_tiles - 1).astype(F32)
        du1 = _shift_up(du, 1, [du_n[0:1]])
        du2 = _shift_up(du, 2, [du_n[0:1], du_n[1:2]])
        dz = cw[2:3] * du + cw[1:2] * du1 + cw[0:1] * du2
        pv_ref[3:4, :] += _rowsum(du * z2)
        pv_ref[4:5, :] += _rowsum(du * z1)
        pv_ref[5:6, :] += _rowsum(du * z)

        de_ref[0] = (dcbu * u).astype(BF16)
        de_ref[1] = (dz * cx).astype(BF16)
        de_ref[2] = (dz * cc).astype(BF16)
        de_ref[3] = dga.astype(BF16)
        de_ref[4] = dgb.astype(BF16)

    row = lambda w: pl.BlockSpec((tm, w), lambda i: (i, 0))
    nxt = pl.BlockSpec((16, D), lambda i: (jnp.minimum((i + 1) * (tm // 16), S // 16 - 1), 0))
    return pl.pallas_call(
        body, name="bwd_mix", grid=(n_tiles,),
        in_specs=[row(D), nxt, _vec_spec(), row(D)] + [_e_spec(c, tm) for c in range(5)]
                 + [_e_next_spec(0, tm, S), _e_next_spec(4, tm, S), _e_prev_spec(1, tm), _e_prev_spec(2, tm),
                    _const_spec((8, D)), _vec_spec(), _vec_spec(), row(D), row(D), row(AOW),
                    _const_spec((D, D)), _const_spec((D, D)), _const_spec((D, AOW))],
        out_specs=[row(D), row(D), row(D), row(AOW), row(AOW), pl.BlockSpec((5, tm, D), lambda i: (0, i, 0)),
                   _const_spec((8, D))],
        out_shape=[jax.ShapeDtypeStruct((S, D), BF16)] * 3 + [jax.ShapeDtypeStruct((S, AOW), F32)] * 2
                  + [jax.ShapeDtypeStruct((5, S, D), BF16), jax.ShapeDtypeStruct((8, D), F32)],
        compiler_params=_cparams("arbitrary"),
    )(dx1, dx1, gate1, mo, e, e, e, e, e, e, e, e, e, cw8, ba, bb, ya, yc, o_attn, w_out, w_bc, w_bat)


def _attn_bwd(qkv, do, lse, dl, bias_t):
    S = qkv.shape[2]
    nblk = S // HEAD

    def body(qkv_ref, do_ref, lse_ref, dl_ref, b_ref, d_ref):
        g = pl.program_id(1)
        bias = b_ref[0, 0]
        col = lax.broadcasted_iota(jnp.int32, bias.shape, 1)
        bias_last = jnp.where(col >= HEAD, NEG, bias)
        eye = (lax.broadcasted_iota(jnp.int32, (HEAD, HEAD), 0) == lax.broadcasted_iota(jnp.int32, (HEAD, HEAD), 1)).astype(F32)

        def as_row(t):
            return jnp.sum(t * eye, axis=0, keepdims=True)

        for gi, d in enumerate(DILATIONS):
            @pl.when(g == gi)
            def _(d=d):
                nb = nblk // d

                def step(b, dq_part):
                    r, n = b // nb, b % nb
                    cur = pl.ds(n * (HEAD * d) + r, HEAD, stride=d)
                    nxt = pl.ds(jnp.minimum(n + 1, nb - 1) * (HEAD * d) + r, HEAD, stride=d)
                    two = lambda ref: jnp.concatenate([ref[cur, :], ref[nxt, :]], axis=0)
                    two_rows = lambda ref: jnp.concatenate([as_row(ref[cur, :]), as_row(ref[nxt, :])], axis=1)
                    q2 = two(qkv_ref.at[0, 0]).astype(BF16)
                    do2 = two(do_ref).astype(BF16)
                    k = qkv_ref.at[0, 1][cur, :].astype(BF16)
                    v = qkv_ref.at[0, 2][cur, :].astype(BF16)
                    s = _nt(k, q2) * SCALE + jnp.where(n < nb - 1, bias, bias_last)
                    p = jnp.exp(s - two_rows(lse_ref))
                    d_ref.at[0, 2][cur, :] = _nn(p.astype(BF16), do2)
                    dp = _nt(v, do2)
                    ds = (p * (dp - two_rows(dl_ref)) * SCALE).astype(BF16)
                    d_ref.at[0, 1][cur, :] = _nn(ds, q2)
                    dq2 = _tn(ds, k)
                    d_ref.at[0, 0][cur, :] = dq2[:HEAD] + jnp.where(n > 0, dq_part, 0.0)
                    return dq2[HEAD:]

                def steps(i, dq_part):
                    for u in range(UNROLL):
                        dq_part = step(i * UNROLL + u, dq_part)
                    return dq_part

                lax.fori_loop(0, nblk // UNROLL, steps, jnp.zeros((HEAD, HEAD), F32))

    col_blk = pl.BlockSpec((S, HEAD), lambda j, g: (0, j))
    qkv_blk = pl.BlockSpec((1, 3, S, HEAD), lambda j, g: (g, 0, 0, j))
    return pl.pallas_call(
        body, name="attn_bwd", grid=(N_SLOT, 3),
        in_specs=[qkv_blk, col_blk, col_blk, col_blk, pl.BlockSpec((1, 1, HEAD, 2 * HEAD), lambda j, g: (g, j, 0, 0))],
        out_specs=qkv_blk,
        out_shape=jax.ShapeDtypeStruct((3, 3, S, AOW), F32),
        compiler_params=_cparams("parallel", "arbitrary"),
    )(qkv, do, lse, dl, bias_t)


def _bwd_in(dqkv, de, w_int, x, dx1, g_mix, sc1):
    S = x.shape[0]
    tm = TM

    def body(dq_ref, de_ref, w_ref, x_ref, dx1_ref, g_ref, sc_ref, gx_ref, pv_ref, acc):
        i, k = pl.program_id(0), pl.program_id(1)

        @pl.when((i == 0) & (k == 0))
        def _():
            pv_ref[...] = jnp.zeros_like(pv_ref)

        @pl.when(k == 0)
        def _():
            acc[...] = jnp.zeros_like(acc)

        @pl.when(k < 9)
        def _():
            acc[...] += _nn(dq_ref[0].astype(BF16), w_ref[...])

        @pl.when(k >= 9)
        def _():
            acc[...] += _nn(de_ref[0], w_ref[...])

        @pl.when(k == 18)
        def _():
            dh = acc[...]
            xv = x_ref[...]
            r = _rms_r(xv)
            g = g_ref[...]
            dxn, pg = _rms_bwd(xv, r, g, dh * (1.0 + sc_ref[...]))
            gx_ref[...] = dx1_ref[...] + dxn
            pv_ref[0:1, :] += _rowsum(dh)
            pv_ref[1:2, :] += _rowsum(dh * (xv * r * g))
            pv_ref[2:3, :] += _rowsum(pg)

    def e_idx(i, k):
        kk = jnp.maximum(k - 9, 0)
        return (kk // 2, i, kk % 2)

    row = pl.BlockSpec((tm, D), lambda i, k: (i, 0))
    return pl.pallas_call(
        body, name="bwd_in", grid=(S // tm, 19),
        in_specs=[pl.BlockSpec((1, tm, 512), lambda i, k: (jnp.minimum(k, 8), i, 0)), pl.BlockSpec((1, tm, 512), e_idx),
                  pl.BlockSpec((512, D), lambda i, k: (_win_rowblock(k), 0)), row, row, _vec_spec(), _vec_spec()],
        out_specs=[row, _const_spec((8, D))],
        out_shape=[jax.ShapeDtypeStruct((S, D), F32), jax.ShapeDtypeStruct((8, D), F32)],
        scratch_shapes=[pltpu.VMEM((tm, D), F32)],
        compiler_params=_cparams("arbitrary", "arbitrary"),
    )(dqkv, de, w_int, x, dx1, g_mix, sc1)


def _grad_w(name, a, b):
    S, ka = a.shape
    nb = b.shape[1]

    def body(a_ref, b_ref, o_ref):
        o_ref[...] = _tn(a_ref[...], b_ref[...]).astype(BF16)

    return pl.pallas_call(
        body, name=name, grid=(ka // 512,),
        in_specs=[pl.BlockSpec((S, 512), lambda n: (0, n)), pl.BlockSpec((S, nb), lambda n: (0, 0))],
        out_specs=pl.BlockSpec((512, nb), lambda n: (n, 0)),
        out_shape=jax.ShapeDtypeStruct((ka, nb), BF16),
        compiler_params=_cparams("parallel"),
    )(a, b)


def _grad_w_in(dqkv, de, h):
    S = h.shape[0]

    def body(dq_ref, de_ref, h_ref, o_ref):
        n = pl.program_id(0)

        @pl.when(n < 9)
        def _():
            o_ref[...] = _tn(dq_ref[0].astype(BF16), h_ref[...]).astype(BF16)

        @pl.when(n >= 9)
        def _():
            o_ref[...] = _tn(de_ref[0], h_ref[...]).astype(BF16)

    def e_idx(n):
        kk = jnp.maximum(n - 9, 0)
        return (kk // 2, 0, kk % 2)

    return pl.pallas_call(
        body, name="grad_w_in", grid=(19,),
        in_specs=[pl.BlockSpec((1, S, 512), lambda n: (jnp.minimum(n, 8), 0, 0)), pl.BlockSpec((1, S, 512), e_idx),
                  pl.BlockSpec((S, D), lambda n: (0, 0))],
        out_specs=pl.BlockSpec((512, D), lambda n: (_win_rowblock(n), 0)),
        out_shape=jax.ShapeDtypeStruct((19 * 512, D), BF16),
        compiler_params=_cparams("parallel"),
    )(dqkv, de, h)


def _local_step(x, tgt, mod, g_mix, g_mlp, g_fin, ba, bb, cw8, w_int, late_weights, mlp_grads_ready, other_grads_ready):
    S = x.shape[0]
    sh1, sc1, gt1, sh2, sc2, gt2 = [mod[k:k + 1] for k in range(6)]
    bias, bias_t = _bias_table()

    h, qkv, e = _proj(x, g_mix, sc1, sh1, w_int)
    qkv = qkv.reshape(3, 3, S, AOW)
    o_attn, lse = _attn_fwd(qkv, bias)
    w_bat, w_bc, w_out, w_mit, w_mo = late_weights(o_attn)
    o_bf, cbu, ya, yc, merged = _mix(o_attn, e, cw8, ba, bb, w_bat, w_bc)
    x1, mo, h2 = _out_proj(merged, w_out, x, gt1, g_mlp, sc2, sh2)
    a, f = _mlp_in(h2, w_mit)
    mlp, dx2, pv_f = _mlp_out(f, w_mo, x1, gt2, g_fin, tgt)

    da, dmo2, pv_a = _bwd_mlp_a(dx2, gt2, mlp, w_mo, a)
    dx1, pv_b = _bwd_mlp_b(da, w_mit, x1, dx2, g_mlp, sc2)
    zero = mlp_grads_ready(_grad_w("grad_w_mi", da, h2), _grad_w("grad_w_mo", f, dmo2))
    dmo, dya, dyc, do, dl, de, pv_m = _bwd_mix(dx1, gt1 + zero, mo, e, cw8, ba, bb, ya, yc, o_attn, w_out, w_bc, w_bat)
    dqkv = _attn_bwd(qkv, do, lse, dl, bias_t).reshape(9, S, AOW)
    zero = other_grads_ready(_grad_w_in(dqkv, de, h), _grad_w("grad_w_ba", dya, o_bf), _grad_w("grad_w_bc", cbu, dyc),
                             _grad_w("grad_w_out", merged, dmo))
    grad_x, pv_i = _bwd_in(dqkv, de, w_int, x, dx1, g_mix, sc1 + zero)

    vec = jnp.concatenate([pv_i[0:2], pv_m[0:1], pv_b[0:2], pv_a[0:1], pv_i[2:3], pv_b[2:3], pv_f[0:1],
                           pv_m[1:3], pv_m[3:6], jnp.zeros((2, D), F32)], axis=0)
    return pv_f[1, 0], grad_x, vec


def _my_place():
    return lax.axis_index("x"), lax.axis_index("y"), lax.axis_index("c")


def _dev_index(px, py, pc):
    return 4 * px + 2 * py + pc


def _allgather_weights(shards):
    nw = len(shards)
    HBM = pl.BlockSpec(memory_space=pl.ANY)

    def body(*refs):
        sh, full = refs[:nw], refs[nw:2 * nw]
        send_sems, recv_sems, local_sems = refs[2 * nw:]
        x, y, c = _my_place()
        me, sibling = (x, y, c), (x, y, 1 - c)
        chips = [(1 - x, y), (x, 1 - y), (1 - x, 1 - y)]

        def rows(w, px, py, pc):
            r = sh[w].shape[0]
            return full[w].at[pl.ds(pl.multiple_of(_dev_index(px, py, pc) * r, 16), r), :]

        def copy(w, k, block, to, src=None):
            return pltpu.make_async_remote_copy(
                src_ref=rows(w, *block) if src is None else src, dst_ref=rows(w, *block),
                send_sem=send_sems.at[w, k], recv_sem=recv_sems.at[w, k], device_id=to, device_id_type=MESH)

        mine = [pltpu.make_async_copy(sh[w], rows(w, *me), local_sems.at[w]) for w in range(nw)]
        for cp in mine:
            cp.start()
        first = []
        for w in range(nw):
            first.append(copy(w, 0, me, sibling, src=sh[w]))
            first += [copy(w, 1 + j, me, (*chip, c), src=sh[w]) for j, chip in enumerate(chips)]
        for cp in first:
            cp.start()
        passed = []
        for w in range(nw):
            for j, chip in enumerate(chips):
                copy(w, 1 + j, (*chip, c), me).wait_recv()
                fwd = copy(w, 4 + j, (*chip, c), sibling)
                fwd.start()
                passed.append(fwd)
        for w in range(nw):
            copy(w, 0, sibling, me).wait_recv()
            for j, chip in enumerate(chips):
                copy(w, 4 + j, (*chip, 1 - c), me).wait_recv()
        for cp in first + passed:
            cp.wait_send()
        for cp in mine:
            cp.wait()

    return pl.pallas_call(
        body, name="allgather_weights",
        out_shape=[jax.ShapeDtypeStruct((N_DEV * s.shape[0], s.shape[1]), s.dtype) for s in shards],
        in_specs=[HBM] * nw, out_specs=[HBM] * nw,
        scratch_shapes=[pltpu.SemaphoreType.DMA((nw, 7)), pltpu.SemaphoreType.DMA((nw, 7)), pltpu.SemaphoreType.DMA((nw,))],
    )(*shards)


def _peer(x, y, c, m):
    return (x ^ ((m >> 2) & 1), y ^ ((m >> 1) & 1), c ^ (m & 1))


HBM_SPEC = pl.BlockSpec(memory_space=pltpu.HBM)
SEM_SPEC = pl.BlockSpec(memory_space=pltpu.SEMAPHORE)
N_PEER = N_DEV - 1


SPLIT_MASKS = {"gather": tuple(range(1, N_DEV)), "scatter": tuple(range(1, N_DEV)), "chips": (2, 4, 6)}


def _split_copy(mode, src_ref, land_ref, send_sems, recv_sems, w, j, place, arriving=False):
    x, y, c = place
    masks = SPLIT_MASKS[mode]
    peer = _peer(x, y, c, masks[j])
    k = w * len(masks) + j
    sender, receiver = ((peer, (x, y, c)) if arriving else ((x, y, c), peer))
    if mode == "gather":
        r = src_ref.shape[0]
        src, dst = src_ref, land_ref.at[pl.ds(pl.multiple_of(_dev_index(*sender) * r, 16), r), :]
    elif mode == "scatter":
        r = land_ref.shape[1]
        src, dst = src_ref.at[pl.ds(pl.multiple_of(_dev_index(*receiver) * r, 16), r), :], land_ref.at[j]
    else:
        src, dst = src_ref.at[2 * receiver[0] + receiver[1]], land_ref.at[j]
    return pltpu.make_async_remote_copy(src_ref=src, dst_ref=dst, send_sem=send_sems.at[k], recv_sem=recv_sems.at[k],
                                        device_id=peer, device_id_type=MESH)


def _split_start(name, mode, srcs, lands):
    n = len(srcs)
    nm = len(SPLIT_MASKS[mode])

    def body(*refs):
        src, land = refs[:n], refs[n:2 * n]
        send_sems, recv_sems = refs[2 * n], refs[2 * n + 1]
        token = refs[-1]
        place = _my_place()
        for w in range(n):
            for j in range(nm):
                _split_copy(mode, src[w], land[w], send_sems, recv_sems, w, j, place).start()
        token[...] = jnp.zeros_like(token)

    hbm = lambda t: pltpu.HBM(t.shape, t.dtype)
    out = pl.pallas_call(
        body, name=name,
        out_shape=(pltpu.SemaphoreType.DMA((n * nm,)), pltpu.SemaphoreType.DMA((n * nm,)), *[hbm(t) for t in srcs],
                   *[hbm(t) for t in lands], jax.ShapeDtypeStruct((8, 128), F32)),
        in_specs=(HBM_SPEC,) * (2 * n),
        out_specs=(SEM_SPEC, SEM_SPEC) + (HBM_SPEC,) * (2 * n) + (pl.BlockSpec(memory_space=pltpu.VMEM),),
        input_output_aliases={i: 2 + i for i in range(2 * n)},
        compiler_params=pltpu.CompilerParams(has_side_effects=pltpu.SideEffectType.DATAFLOW_SIDE_EFFECTING),
    )(*[pltpu.with_memory_space_constraint(t, pltpu.HBM) for t in (*srcs, *lands)])
    return out[0], out[1], out[2:2 + n], out[2 + n:2 + 2 * n], out[-1][0:1, 0:1]


def _split_wait(name, mode, send_sems, recv_sems, srcs, lands, after):
    n = len(srcs)

    def body(*refs):
        src, land = refs[:n], refs[n:2 * n]
        ssem, rsem = refs[2 * n], refs[2 * n + 1]
        place = _my_place()
        for w in range(n):
            for j in range(len(SPLIT_MASKS[mode])):
                _split_copy(mode, src[w], land[w], ssem, rsem, w, j, place).wait_send()
                _split_copy(mode, src[w], land[w], ssem, rsem, w, j, place, arriving=True).wait_recv()

    hbm = lambda t: pltpu.HBM(t.shape, t.dtype)
    out = pl.pallas_call(
        body, name=name,
        out_shape=tuple(hbm(t) for t in (*srcs, *lands)),
        in_specs=(HBM_SPEC,) * (2 * n) + (SEM_SPEC, SEM_SPEC, pl.BlockSpec(memory_space=pl.ANY)),
        out_specs=(HBM_SPEC,) * (2 * n),
        input_output_aliases={i: i for i in range(2 * n)},
        compiler_params=pltpu.CompilerParams(has_side_effects=pltpu.SideEffectType.DATAFLOW_SIDE_EFFECTING),
    )(*srcs, *lands, send_sems, recv_sems, after)
    return out[:n], out[n:]


def _sibling_exchange(grads):
    nw = len(grads)
    HBM = pl.BlockSpec(memory_space=pl.ANY)

    def body(*refs):
        g, land = refs[:nw], refs[nw:2 * nw]
        send_sems, recv_sems = refs[2 * nw:]
        x, y, c = _my_place()

        def copy(w, q, owner_core):
            r = land[w].shape[1]
            return pltpu.make_async_remote_copy(
                src_ref=g[w].at[pl.ds(pl.multiple_of((2 * q + owner_core) * r, 16), r), :], dst_ref=land[w].at[q],
                send_sem=send_sems.at[w, q], recv_sem=recv_sems.at[w, q], device_id=(x, y, 1 - c), device_id_type=MESH)

        sends = [copy(w, q, 1 - c) for w in range(nw) for q in range(4)]
        for cp in sends:
            cp.start()
        for w in range(nw):
            for q in range(4):
                copy(w, q, c).wait_recv()
        for cp in sends:
            cp.wait_send()

    return pl.pallas_call(
        body, name="sibling_exchange",
        out_shape=[jax.ShapeDtypeStruct((4, a.shape[0] // N_DEV, a.shape[1]), a.dtype) for a in grads],
        in_specs=[HBM] * nw, out_specs=[HBM] * nw,
        scratch_shapes=[pltpu.SemaphoreType.DMA((nw, 4)), pltpu.SemaphoreType.DMA((nw, 4))],
    )(*grads)


def _pair_sum(g, sib, core, name):
    _, r, ccols = sib.shape
    tr = _row_tile(r)

    def body(core_ref, g_ref, s_ref, o_ref):
        o_ref[0] = (g_ref[0, 0].astype(F32) + s_ref[0].astype(F32)).astype(BF16)

    return pl.pallas_call(
        body, name=name,
        grid_spec=pltpu.PrefetchScalarGridSpec(
            num_scalar_prefetch=1, grid=(4, r // tr),
            in_specs=[pl.BlockSpec((1, 1, tr, ccols), lambda q, i, core_ref: (q, core_ref[0], i, 0)),
                      pl.BlockSpec((1, tr, ccols), lambda q, i, core_ref: (q, i, 0))],
            out_specs=pl.BlockSpec((1, tr, ccols), lambda q, i, core_ref: (q, i, 0))),
        out_shape=jax.ShapeDtypeStruct(sib.shape, BF16),
        compiler_params=_cparams("parallel", "parallel"),
    )(core, g.reshape(4, 2, r, ccols), sib)


def _allgather_small(v, name):
    r, ccols = v.shape

    def body(v_ref, out_ref, send_sems, recv_sems):
        x, y, c = _my_place()
        my_idx = _dev_index(x, y, c)
        out_ref[my_idx] = v_ref[...]

        def copy(m):
            peer = _peer(x, y, c, m)
            return pltpu.make_async_remote_copy(
                src_ref=v_ref, dst_ref=out_ref.at[my_idx],
                send_sem=send_sems.at[m - 1], recv_sem=recv_sems.at[m - 1], device_id=peer, device_id_type=MESH)

        def arrival(m):
            peer = _peer(x, y, c, m)
            return pltpu.make_async_remote_copy(
                src_ref=v_ref, dst_ref=out_ref.at[_dev_index(*peer)],
                send_sem=send_sems.at[m - 1], recv_sem=recv_sems.at[m - 1], device_id=peer, device_id_type=MESH)

        sends = [copy(m) for m in range(1, N_DEV)]
        for cp in sends:
            cp.start()
        for m in range(1, N_DEV):
            arrival(m).wait_recv()
        for cp in sends:
            cp.wait_send()

    return pl.pallas_call(
        body, name=name,
        out_shape=jax.ShapeDtypeStruct((N_DEV, r, ccols), v.dtype),
        in_specs=[pl.BlockSpec(memory_space=pltpu.VMEM)], out_specs=pl.BlockSpec(memory_space=pltpu.VMEM),
        scratch_shapes=[pltpu.SemaphoreType.DMA((7,)), pltpu.SemaphoreType.DMA((7,))],
    )(v)


def _ada_fwd(c_all, w_ada, b_cols):
    def body(c_ref, w_ref, b_ref, mod_ref, act_ref):
        cv = c_ref[...]
        act = cv * _sigmoid(cv)
        act_ref[...] = act
        mod_ref[...] = jnp.dot(act, w_ref[...], preferred_element_type=F32, precision=lax.Precision.HIGHEST) + b_ref[...]

    return pl.pallas_call(
        body, name="ada_fwd",
        out_shape=[jax.ShapeDtypeStruct((N_DEV, w_ada.shape[1]), F32), jax.ShapeDtypeStruct((N_DEV, D), F32)],
        compiler_params=_cparams(),
    )(c_all, w_ada, b_cols)


def _ada_bwd(act_t, gm_cols):
    def body(a_ref, g_ref, o_ref):
        o_ref[...] = jnp.dot(a_ref[...], g_ref[...], preferred_element_type=F32, precision=lax.Precision.HIGHEST)

    return pl.pallas_call(
        body, name="ada_bwd", out_shape=jax.ShapeDtypeStruct((D, gm_cols.shape[1]), F32), compiler_params=_cparams(),
    )(act_t, gm_cols)


def _row_tile(r):
    for t in (256, 304, 128, 64, 16):
        if r % t == 0:
            return t
    return r


def _sum_parts(parts, name, own=None):
    k, r, ccols = parts.shape
    tr = _row_tile(r)

    def body(*refs):
        p_ref, o_ref = refs[0], refs[-1]
        acc = p_ref[0].astype(F32) if own is None else refs[1][...].astype(F32) + p_ref[0].astype(F32)
        for s in range(1, k):
            acc = acc + p_ref[s].astype(F32)
        o_ref[...] = acc

    blk = pl.BlockSpec((tr, ccols), lambda i: (i, 0))
    return pl.pallas_call(
        body, name=name, grid=(r // tr,),
        in_specs=[pl.BlockSpec((k, tr, ccols), lambda i: (0, i, 0))] + ([] if own is None else [blk]),
        out_specs=blk,
        out_shape=jax.ShapeDtypeStruct((r, ccols), F32),
        compiler_params=_cparams("parallel"),
    )(*((parts,) if own is None else (parts, own)))


def _adamw(w, g, m, v, name):
    r, ccols = w.shape
    tr = _row_tile(r)
    c1 = 1.0 / (1.0 - B1 ** STEP)
    c2 = 1.0 / (1.0 - B2 ** STEP)

    def body(w_ref, g_ref, m_ref, v_ref, d_ref, nm_ref, nv_ref):
        gv = g_ref[...]
        nm = B1 * m_ref[...] + (1.0 - B1) * gv
        nv = B2 * v_ref[...] + (1.0 - B2) * jnp.square(gv)
        nm_ref[...] = nm
        nv_ref[...] = nv
        d_ref[...] = -LR * ((nm * c1) / (jnp.sqrt(nv * c2) + ADAM_EPS) + WD * w_ref[...])

    blk = pl.BlockSpec((tr, ccols), lambda i: (i, 0))
    return pl.pallas_call(
        body, name=name, grid=(r // tr,), in_specs=[blk] * 4, out_specs=[blk] * 3,
        out_shape=[jax.ShapeDtypeStruct((r, ccols), F32)] * 3,
        compiler_params=_cparams("parallel"),
    )(w, g, m, v)


def _pack_vectors(b_ada, g_mix, g_mlp, g_fin, b_gate, conv_w):
    conv_rows = jnp.pad(conv_w.reshape(3, HEAD), ((0, 0), (0, D - HEAD)))
    return jnp.concatenate([b_ada.reshape(6, D), g_mix.reshape(1, D), g_mlp.reshape(1, D), g_fin.reshape(1, D),
                            b_gate.reshape(2, D), conv_rows, jnp.zeros((2, D), F32)], axis=0)


def _unpack_vectors(p):
    return (p[0:6].reshape(1, 6 * D), p[6:7], p[9:11].reshape(1, 2 * D), p[11:14, :HEAD].reshape(1, 3, HEAD),
            p[7:8], p[8])


def kernel(x, c, w_ada, b_ada, g_norm_mix, w_in, b_gate, conv_w, w_branch_attn, w_branch_conv, w_out, g_norm_mlp, w_mlp_in, w_mlp_out, g_norm_final, loss_target, m_w_ada, m_b_ada, m_g_norm_mix, m_w_in, m_b_gate, m_conv_w, m_w_branch_attn, m_w_branch_conv, m_w_out, m_g_norm_mlp, m_w_mlp_in, m_w_mlp_out, m_g_norm_final, v_w_ada, v_b_ada, v_g_norm_mix, v_w_in, v_b_gate, v_conv_w, v_w_branch_attn, v_w_branch_conv, v_w_out, v_g_norm_mlp, v_w_mlp_in, v_w_mlp_out, v_g_norm_final):
    S = x.shape[1]
    xi, yi, ci = _my_place()
    me = _dev_index(xi, yi, ci)
    x2 = x.reshape(S, D)
    tgt = loss_target.reshape(S, D)

    pay = jnp.zeros((8, D), F32).at[0].set(c[0]).at[1:4, :HEAD].set(conv_w[0])
    got = _allgather_small(pay, "gather_cond")
    c_all = got[:, 0, :]
    cw8 = jnp.pad(got[:, 1:4, :HEAD].transpose(1, 0, 2).reshape(3, D), ((0, 5), (0, 0)))
    ncol = w_ada.shape[2]
    b_cols = lax.dynamic_slice(b_ada, (0, me * ncol), (1, ncol))
    mod_cols, act = _ada_fwd(c_all, w_ada[0], b_cols)
    mod_all = _allgather_small(mod_cols, "gather_mod")

    w_in_shard, mod_all = lax.optimization_barrier((w_in[0].T.astype(BF16), mod_all))
    mod = lax.dynamic_index_in_dim(mod_all, me, axis=1, keepdims=False).reshape(6, D)
    (w_int,) = _allgather_weights([w_in_shard])
    late = [w_branch_attn[0].T.astype(BF16), w_branch_conv[0].astype(BF16), w_out[0].astype(BF16),
            w_mlp_in[0].T.astype(BF16), w_mlp_out[0].astype(BF16)]
    w_int, late = lax.optimization_barrier((w_int, late))
    zones = [lax.dynamic_update_slice(lax.empty((N_DEV * t.shape[0], t.shape[1]), BF16), t, (me * t.shape[0], 0)) for t in late]
    ag = _split_start("gather_late_start", "gather", late, zones)

    def late_weights(o_attn):
        return _split_wait("gather_late_wait", "gather", ag[0], ag[1], ag[2], ag[3], o_attn)[1]

    rs = {}

    def mlp_grads_ready(*grads):
        lands = [lax.empty((N_PEER, t.shape[0] // N_DEV, t.shape[1]), BF16) for t in grads]
        rs["mlp"] = _split_start("scatter_mlp_start", "scatter", grads, lands)
        return rs["mlp"][4]

    def other_grads_ready(*grads):
        core = ci.reshape(1).astype(jnp.int32)
        pair = [_pair_sum(g, sib, core, "pair_sum_%d" % k) for k, (g, sib) in enumerate(zip(grads, _sibling_exchange(grads)))]
        lands = [lax.empty((3,) + t.shape[1:], BF16) for t in pair]
        rs["rest"] = _split_start("scatter_rest_start", "chips", pair, lands)
        return rs["rest"][4]

    ba, bb = b_gate[:, :D], b_gate[:, D:]
    loss_part, grad_x, vec = _local_step(
        x2, tgt, mod + ag[4], g_norm_mix, g_norm_mlp, g_norm_final.reshape(1, D), ba, bb, cw8, w_int, late_weights,
        mlp_grads_ready, other_grads_ready)
    loss = lax.psum(loss_part, AXES)

    vec_all = _allgather_small(vec, "gather_vec")
    vec_sum = _sum_parts(vec_all, "sum_vec")
    gm_all = vec_all[:, 0:6, :].reshape(N_DEV, 6 * D)
    gm_cols = lax.dynamic_slice(gm_all, (0, me * ncol), (N_DEV, ncol))
    g_w_ada = _ada_bwd(act.T, gm_cols)
    conv_cols = lax.dynamic_slice(vec_sum[11:14], (0, me * HEAD), (3, HEAD))
    g_pack = jnp.concatenate([vec_sum[0:11], jnp.pad(conv_cols, ((0, 0), (0, D - HEAD))), jnp.zeros((2, D), F32)], axis=0)
    packs = [_pack_vectors(*t) for t in ((b_ada, g_norm_mix, g_norm_mlp, g_norm_final, b_gate, conv_w),
                                         (m_b_ada, m_g_norm_mix, m_g_norm_mlp, m_g_norm_final, m_b_gate, m_conv_w),
                                         (v_b_ada, v_g_norm_mix, v_g_norm_mlp, v_g_norm_final, v_b_gate, v_conv_w))]
    d_pack, m_pack, v_pack = _adamw(packs[0], g_pack, packs[1], packs[2], "adamw_vectors")
    d_ada, nm_ada, nv_ada = _adamw(w_ada[0], g_w_ada, m_w_ada[0], v_w_ada[0], "adamw_w_ada")

    sums = {}
    srcs, lands = _split_wait("scatter_mlp_wait", "scatter", *rs["mlp"][:4], grad_x)
    for n, g, land in zip(("w_mi", "w_mo"), srcs, lands):
        r = land.shape[1]
        sums[n] = _sum_parts(land, "sum_" + n, own=lax.dynamic_slice(g, (me * r, 0), (r, g.shape[1])))
    srcs, lands = _split_wait("scatter_rest_wait", "chips", *rs["rest"][:4], grad_x)
    for n, pair, land in zip(("w_in", "w_ba", "w_bc", "w_out"), srcs, lands):
        sums[n] = _sum_parts(land, "sum_" + n, own=lax.dynamic_index_in_dim(pair, 2 * xi + yi, axis=0, keepdims=False))
    g_in, g_ba, g_bc, g_out, g_mi, g_mo = sums["w_in"].T, sums["w_ba"].T, sums["w_bc"], sums["w_out"], sums["w_mi"].T, sums["w_mo"]
    big = {}
    for n, w, g, m, v in (("w_in", w_in, g_in, m_w_in, v_w_in), ("w_ba", w_branch_attn, g_ba, m_w_branch_attn, v_w_branch_attn),
                          ("w_bc", w_branch_conv, g_bc, m_w_branch_conv, v_w_branch_conv), ("w_out", w_out, g_out, m_w_out, v_w_out),
                          ("w_mi", w_mlp_in, g_mi, m_w_mlp_in, v_w_mlp_in), ("w_mo", w_mlp_out, g_mo, m_w_mlp_out, v_w_mlp_out)):
        big[n] = (g[None],) + tuple(t[None] for t in _adamw(w[0], g, m[0], v[0], "adamw_" + n))

    gv = _unpack_vectors(g_pack)
    dv = _unpack_vectors(d_pack)
    mv = _unpack_vectors(m_pack)
    vv = _unpack_vectors(v_pack)

    def ordered(k, ada, vecs):
        return (ada[None], vecs[0], vecs[1], big["w_in"][k], vecs[2], vecs[3], big["w_ba"][k], big["w_bc"][k],
                big["w_out"][k], vecs[4], big["w_mi"][k], big["w_mo"][k], vecs[5])

    return (loss, grad_x.reshape(1, S, D), *ordered(0, g_w_ada, gv), *ordered(1, d_ada, dv),
            *ordered(2, nm_ada, mv), *ordered(3, nv_ada, vv))
```

```python
import functools

import numpy as np
import jax
import jax.numpy as jnp
from jax import lax
from jax.experimental import pallas as pl
from jax.experimental.pallas import tpu as pltpu

F32, BF16 = jnp.float32, jnp.bfloat16
D = 1024
HEAD = 128
DILATIONS = (1, 4, 16)
N_SLOT = 4
AOW = N_SLOT * HEAD
DFF = 4 * D
N_DEV = 8
UNROLL = 8
EPS = 1e-6
NEG = -1e30
SCALE = HEAD ** -0.5
LR, B1, B2, ADAM_EPS, WD, STEP = 0.001, 0.9, 0.999, 1e-08, 0.01, 10
V7X_VMEM_LIMIT = 56 * 1024 * 1024
TM = 1024
MESH = pl.DeviceIdType.MESH
AXES = ("x", "y", "c")


def _cparams(*sem):
    if sem:
        return pltpu.CompilerParams(dimension_semantics=sem, vmem_limit_bytes=V7X_VMEM_LIMIT)
    return pltpu.CompilerParams(vmem_limit_bytes=V7X_VMEM_LIMIT)


def _nn(a, b):
    return jnp.dot(a, b, preferred_element_type=F32)


def _nt(a, b):
    return lax.dot_general(a, b, (((1,), (1,)), ((), ())), preferred_element_type=F32)


def _tn(a, b):
    return lax.dot_general(a, b, (((0,), (0,)), ((), ())), preferred_element_type=F32)


def _rms_r(x):
    return lax.rsqrt(jnp.mean(x * x, axis=-1, keepdims=True) + EPS)


def _rms_bwd(x, r, g, dn):
    gy = dn * g
    dx = r * gy - x * (r * r * r) * jnp.mean(x * gy, axis=-1, keepdims=True)
    return dx, dn * (x * r)


def _sigmoid(t):
    return 1.0 / (1.0 + jnp.exp(-t))


def _rowsum(v):
    return jnp.sum(v, axis=0, keepdims=True)


def _vec_spec(n=D):
    return pl.BlockSpec((1, n), lambda *_: (0, 0))


def _const_spec(shape):
    nd = len(shape)
    return pl.BlockSpec(shape, lambda *_: (0,) * nd)


def _win_rowblock(j):
    return jnp.where(j < 9, (j % 3) * 3 + j // 3, j)


def _proj(x, g, sc, sh, w_int):
    S = x.shape[0]
    tm = 2 * TM

    def body(x_ref, g_ref, sc_ref, sh_ref, w_ref, h_ref, q_ref, e_ref):
        j = pl.program_id(1)

        @pl.when(j == 0)
        def _():
            xv = x_ref[...]
            h = xv * _rms_r(xv) * g_ref[...] * (1.0 + sc_ref[...]) + sh_ref[...]
            h_ref[...] = h.astype(BF16)

        acc = _nt(h_ref[...], w_ref[...])

        @pl.when(j < 9)
        def _():
            q_ref[0] = acc

        @pl.when(j >= 9)
        def _():
            e_ref[0] = acc.astype(BF16)

    def e_idx(i, j):
        k = jnp.maximum(j - 9, 0)
        return (k // 2, i, k % 2)

    return pl.pallas_call(
        body, name="proj", grid=(S // tm, 19),
        in_specs=[pl.BlockSpec((tm, D), lambda i, j: (i, 0)), _vec_spec(), _vec_spec(), _vec_spec(),
                  pl.BlockSpec((512, D), lambda i, j: (_win_rowblock(j), 0))],
        out_specs=[pl.BlockSpec((tm, D), lambda i, j: (i, 0)),
                   pl.BlockSpec((1, tm, 512), lambda i, j: (jnp.minimum(j, 8), i, 0)),
                   pl.BlockSpec((1, tm, 512), e_idx)],
        out_shape=[jax.ShapeDtypeStruct((S, D), BF16), jax.ShapeDtypeStruct((9, S, 512), F32),
                   jax.ShapeDtypeStruct((5, S, D), BF16)],
        compiler_params=_cparams("parallel", "arbitrary"),
    )(x, g, sc, sh, w_int)


def _bias_table():
    slopes = (2.0 ** (-8.0 * np.arange(1, 13, dtype=np.float32) / 12.0)).astype(np.float32)
    qi = np.arange(HEAD)[:, None]
    kj = np.arange(2 * HEAD)[None, :]
    delta = HEAD + qi - kj
    mask = (delta >= 0) & (delta <= HEAD)
    out = np.zeros((3, N_SLOT, HEAD, 2 * HEAD), np.float32)
    for gi, d in enumerate(DILATIONS):
        for j in range(N_SLOT):
            bias = -slopes[gi * N_SLOT + j] * (delta * d).astype(np.float32)
            out[gi, j] = np.where(mask, bias, NEG)
    out_t = np.concatenate([out[..., HEAD:].swapaxes(-1, -2), out[..., :HEAD].swapaxes(-1, -2)], axis=-1)
    return jnp.asarray(out), jnp.asarray(out_t)


def _block_rows(b, d):
    r = b % d
    n = b // d
    st = n * (HEAD * d) + r
    stp = jnp.maximum(n - 1, 0) * (HEAD * d) + r
    return n, st, stp


def _attn_fwd(qkv, bias):
    S = qkv.shape[2]
    nblk = S // HEAD
    rows = 256

    def body(qkv_ref, b_ref, o_ref, lse_ref, o_s, lse_s):
        g = pl.program_id(1)
        bias = b_ref[0, 0]
        col = lax.broadcasted_iota(jnp.int32, bias.shape, 1)
        bias_first = jnp.where(col < HEAD, NEG, bias)

        for gi, d in enumerate(DILATIONS):
            @pl.when(g == gi)
            def _(gi=gi, d=d):
                def step(b, carry):
                    n, st, stp = _block_rows(b, d)
                    cur = pl.ds(st, HEAD, stride=d)
                    prv = pl.ds(stp, HEAD, stride=d)
                    q = qkv_ref.at[0, 0][cur, :].astype(BF16)
                    kw = jnp.concatenate([qkv_ref.at[0, 1][prv, :], qkv_ref.at[0, 1][cur, :]], axis=0).astype(BF16)
                    vw = jnp.concatenate([qkv_ref.at[0, 2][prv, :], qkv_ref.at[0, 2][cur, :]], axis=0).astype(BF16)
                    s = _nt(q, kw) * SCALE + jnp.where(n > 0, bias, bias_first)
                    m = jnp.max(s, axis=-1, keepdims=True)
                    p = jnp.exp(s - m)
                    l = jnp.sum(p, axis=-1, keepdims=True)
                    o_s.at[gi][cur, :] = _nn(p.astype(BF16), vw) / l
                    lse_s.at[gi][cur, :] = jnp.broadcast_to(m + jnp.log(l), (HEAD, HEAD))
                    return carry

                lax.fori_loop(0, nblk, step, 0, unroll=UNROLL)

        @pl.when(g == len(DILATIONS) - 1)
        def _():
            def merge(i, carry):
                r = pl.ds(pl.multiple_of(i * rows, rows), rows)
                ls = [lse_s[k, r, :] for k in range(3)]
                top = jnp.maximum(jnp.maximum(ls[0], ls[1]), ls[2])
                ws = [jnp.exp(t - top) for t in ls]
                den = ws[0] + ws[1] + ws[2]
                o_ref[r, :] = (ws[0] * o_s[0, r, :] + ws[1] * o_s[1, r, :] + ws[2] * o_s[2, r, :]) / den
                lse_ref[r, :] = top + jnp.log(den)
                return carry

            lax.fori_loop(0, S // rows, merge, 0)

    return pl.pallas_call(
        body, name="attn_fwd", grid=(N_SLOT, 3),
        in_specs=[pl.BlockSpec((1, 3, S, HEAD), lambda j, g: (g, 0, 0, j)),
                  pl.BlockSpec((1, 1, HEAD, 2 * HEAD), lambda j, g: (g, j, 0, 0))],
        out_specs=[pl.BlockSpec((S, HEAD), lambda j, g: (0, j)), pl.BlockSpec((S, HEAD), lambda j, g: (0, j))],
        out_shape=[jax.ShapeDtypeStruct((S, AOW), F32), jax.ShapeDtypeStruct((S, AOW), F32)],
        scratch_shapes=[pltpu.VMEM((3, S, HEAD), F32)] * 2,
        compiler_params=_cparams("parallel", "arbitrary"),
    )(qkv, bias)


def _shift_down(z, k, halo_rows):
    out = pltpu.roll(z, k, axis=0)
    rid = lax.broadcasted_iota(jnp.int32, z.shape, 0)
    for t in range(k):
        out = jnp.where(rid == t, halo_rows[t], out)
    return out


def _shift_up(z, k, halo_rows):
    n = z.shape[0]
    out = pltpu.roll(z, n - k, axis=0)
    rid = lax.broadcasted_iota(jnp.int32, z.shape, 0)
    for t in range(k):
        out = jnp.where(rid == n - k + t, halo_rows[t], out)
    return out


def _e_spec(chunk, tm):
    return pl.BlockSpec((1, tm, D), lambda i, c=chunk: (c, i, 0))


def _e_prev_spec(chunk, tm):
    return pl.BlockSpec((1, 16, D), lambda i, c=chunk: (c, jnp.maximum(i * (tm // 16) - 1, 0), 0))


def _e_next_spec(chunk, tm, S):
    return pl.BlockSpec((1, 16, D), lambda i, c=chunk: (c, jnp.minimum((i + 1) * (tm // 16), S // 16 - 1), 0))


def _mix(o_attn, e, cw8, ba, bb, w_bat, w_bc):
    S = o_attn.shape[0]
    tm = 256

    def body(o_ref, cb_ref, cc_ref, cx_ref, ga_ref, gb_ref, ccp_ref, cxp_ref, cw_ref, ba_ref, bb_ref, wba_ref, wbc_ref,
             obf_ref, cbu_ref, ya_ref, yc_ref, mg_ref):
        i = pl.program_id(0)
        o = o_ref[...].astype(BF16)
        obf_ref[...] = o
        ya = _nt(o, wba_ref[...])
        z = cc_ref[0].astype(F32) * cx_ref[0].astype(F32)
        zp = ccp_ref[0].astype(F32) * cxp_ref[0].astype(F32) * (i > 0).astype(F32)
        z1 = _shift_down(z, 1, [zp[15:16]])
        z2 = _shift_down(z, 2, [zp[14:15], zp[15:16]])
        cw = cw_ref[...]
        u = cw[0:1] * z2 + cw[1:2] * z1 + cw[2:3] * z
        cbu = (cb_ref[0].astype(F32) * u).astype(BF16)
        cbu_ref[...] = cbu
        yc = _nn(cbu, wbc_ref[...])
        sa = _sigmoid(ga_ref[0].astype(F32) + ba_ref[...])
        sb = _sigmoid(gb_ref[0].astype(F32) + bb_ref[...])
        ya_ref[...] = ya.astype(BF16)
        yc_ref[...] = yc.astype(BF16)
        mg_ref[...] = (sa * ya + sb * yc).astype(BF16)

    row = lambda w: pl.BlockSpec((tm, w), lambda i: (i, 0))
    return pl.pallas_call(
        body, name="mix", grid=(S // tm,),
        in_specs=[row(AOW)] + [_e_spec(c, tm) for c in range(5)] + [_e_prev_spec(1, tm), _e_prev_spec(2, tm),
                  _const_spec((8, D)), _vec_spec(), _vec_spec(), _const_spec((D, AOW)), _const_spec((D, D))],
        out_specs=[row(AOW), row(D), row(D), row(D), row(D)],
        out_shape=[jax.ShapeDtypeStruct((S, AOW), BF16)] + [jax.ShapeDtypeStruct((S, D), BF16)] * 4,
        compiler_params=_cparams("parallel"),
    )(o_attn, e, e, e, e, e, e, e, cw8, ba, bb, w_bat, w_bc)


def _out_proj(merged, w_out, x, gate1, g_mlp, sc2, sh2):
    S = x.shape[0]
    tm = TM

    def body(mg_ref, w_ref, x_ref, gt_ref, g_ref, sc_ref, sh_ref, x1_ref, mo_ref, h2_ref):
        mo = _nn(mg_ref[...], w_ref[...])
        mo_ref[...] = mo.astype(BF16)
        x1 = x_ref[...] + gt_ref[...] * mo
        x1_ref[...] = x1
        h2 = x1 * _rms_r(x1) * g_ref[...] * (1.0 + sc_ref[...]) + sh_ref[...]
        h2_ref[...] = h2.astype(BF16)

    row = pl.BlockSpec((tm, D), lambda i: (i, 0))
    return pl.pallas_call(
        body, name="out_proj", grid=(S // tm,),
        in_specs=[row, _const_spec((D, D)), row, _vec_spec(), _vec_spec(), _vec_spec(), _vec_spec()],
        out_specs=[row, row, row],
        out_shape=[jax.ShapeDtypeStruct((S, D), F32), jax.ShapeDtypeStruct((S, D), BF16), jax.ShapeDtypeStruct((S, D), BF16)],
        compiler_params=_cparams("parallel"),
    )(merged, w_out, x, gate1, g_mlp, sc2, sh2)


def _mlp_in(h2, w_mit):
    S = h2.shape[0]
    tm, tn = TM, 2048

    def body(h_ref, w_ref, a_ref, f_ref):
        a = _nt(h_ref[...], w_ref[...])
        a_ref[...] = a.astype(BF16)
        f_ref[...] = jnp.square(jnp.maximum(a, 0.0)).astype(BF16)

    blk = pl.BlockSpec((tm, tn), lambda i, j: (i, j))
    return pl.pallas_call(
        body, name="mlp_in", grid=(S // tm, DFF // tn),
        in_specs=[pl.BlockSpec((tm, D), lambda i, j: (i, 0)), pl.BlockSpec((tn, D), lambda i, j: (j, 0))],
        out_specs=[blk, blk],
        out_shape=[jax.ShapeDtypeStruct((S, DFF), BF16)] * 2,
        compiler_params=_cparams("parallel", "parallel"),
    )(h2, w_mit)


def _mlp_out(f, w_mo, x1, gate2, g_fin, tgt):
    S = x1.shape[0]
    tm, tk = TM, 1024
    nk = DFF // tk

    def body(f_ref, w_ref, x1_ref, gt_ref, g_ref, t_ref, mlp_ref, dx2_ref, pv_ref, acc):
        i, k = pl.program_id(0), pl.program_id(1)

        @pl.when((i == 0) & (k == 0))
        def _():
            pv_ref[...] = jnp.zeros_like(pv_ref)

        @pl.when(k == 0)
        def _():
            acc[...] = jnp.zeros_like(acc)

        acc[...] += _nn(f_ref[...], w_ref[...])

        @pl.when(k == nk - 1)
        def _():
            mlp = acc[...]
            mlp_ref[...] = mlp.astype(BF16)
            x2 = x1_ref[...] + gt_ref[...] * mlp
            r = _rms_r(x2)
            g = g_ref[...]
            err = x2 * r * g - t_ref[...]
            dy = err * (1.0 / D)
            dx2, pg = _rms_bwd(x2, r, g, dy)
            dx2_ref[...] = dx2
            pv_ref[0:1, :] += _rowsum(pg)
            pv_ref[1:2, :] += 0.5 * _rowsum(jnp.mean(err * err, axis=-1, keepdims=True))

    row = pl.BlockSpec((tm, D), lambda i, k: (i, 0))
    return pl.pallas_call(
        body, name="mlp_out", grid=(S // tm, nk),
        in_specs=[pl.BlockSpec((tm, tk), lambda i, k: (i, k)), pl.BlockSpec((tk, D), lambda i, k: (k, 0)),
                  row, _vec_spec(), _vec_spec(), row],
        out_specs=[row, row, _const_spec((8, D))],
        out_shape=[jax.ShapeDtypeStruct((S, D), BF16), jax.ShapeDtypeStruct((S, D), F32), jax.ShapeDtypeStruct((8, D), F32)],
        scratch_shapes=[pltpu.VMEM((tm, D), F32)],
        compiler_params=_cparams("arbitrary", "arbitrary"),
    )(f, w_mo, x1, gate2, g_fin, tgt)


def _bwd_mlp_a(dx2, gate2, mlp, w_mo, a):
    S = dx2.shape[0]
    tm, tn = TM, 2048

    def body(dx_ref, gt_ref, mlp_ref, w_ref, a_ref, da_ref, dmo_ref, pv_ref):
        i, j = pl.program_id(0), pl.program_id(1)

        @pl.when((i == 0) & (j == 0))
        def _():
            pv_ref[...] = jnp.zeros_like(pv_ref)

        @pl.when(j == 0)
        def _():
            dx = dx_ref[...]
            dmo_ref[...] = (dx * gt_ref[...]).astype(BF16)
            pv_ref[0:1, :] += _rowsum(dx * mlp_ref[...].astype(F32))

        df = _nt(dmo_ref[...], w_ref[...])
        da_ref[...] = (df * (2.0 * jnp.maximum(a_ref[...].astype(F32), 0.0))).astype(BF16)

    row = pl.BlockSpec((tm, D), lambda i, j: (i, 0))
    blk = pl.BlockSpec((tm, tn), lambda i, j: (i, j))
    return pl.pallas_call(
        body, name="bwd_mlp_a", grid=(S // tm, DFF // tn),
        in_specs=[row, _vec_spec(), row, pl.BlockSpec((tn, D), lambda i, j: (j, 0)), blk],
        out_specs=[blk, row, _const_spec((8, D))],
        out_shape=[jax.ShapeDtypeStruct((S, DFF), BF16), jax.ShapeDtypeStruct((S, D), BF16), jax.ShapeDtypeStruct((8, D), F32)],
        compiler_params=_cparams("arbitrary", "arbitrary"),
    )(dx2, gate2, mlp, w_mo, a)


def _bwd_mlp_b(da, w_mit, x1, dx2, g_mlp, sc2):
    S = x1.shape[0]
    tm, tk = TM, 1024
    nk = DFF // tk

    def body(da_ref, w_ref, x1_ref, dx2_ref, g_ref, sc_ref, dx1_ref, pv_ref, acc):
        i, k = pl.program_id(0), pl.program_id(1)

        @pl.when((i == 0) & (k == 0))
        def _():
            pv_ref[...] = jnp.zeros_like(pv_ref)

        @pl.when(k == 0)
        def _():
            acc[...] = jnp.zeros_like(acc)

        acc[...] += _nn(da_ref[...], w_ref[...])

        @pl.when(k == nk - 1)
        def _():
            dh = acc[...]
            x1 = x1_ref[...]
            r = _rms_r(x1)
            g = g_ref[...]
            dxn, pg = _rms_bwd(x1, r, g, dh * (1.0 + sc_ref[...]))
            dx1_ref[...] = dx2_ref[...] + dxn
            pv_ref[0:1, :] += _rowsum(dh)
            pv_ref[1:2, :] += _rowsum(dh * (x1 * r * g))
            pv_ref[2:3, :] += _rowsum(pg)

    row = pl.BlockSpec((tm, D), lambda i, k: (i, 0))
    return pl.pallas_call(
        body, name="bwd_mlp_b", grid=(S // tm, nk),
        in_specs=[pl.BlockSpec((tm, tk), lambda i, k: (i, k)), pl.BlockSpec((tk, D), lambda i, k: (k, 0)),
                  row, row, _vec_spec(), _vec_spec()],
        out_specs=[row, _const_spec((8, D))],
        out_shape=[jax.ShapeDtypeStruct((S, D), F32), jax.ShapeDtypeStruct((8, D), F32)],
        scratch_shapes=[pltpu.VMEM((tm, D), F32)],
        compiler_params=_cparams("arbitrary", "arbitrary"),
    )(da, w_mit, x1, dx2, g_mlp, sc2)


def _bwd_mix(dx1, gate1, mo, e, cw8, ba, bb, ya, yc, o_attn, w_out, w_bc, w_bat):
    S = dx1.shape[0]
    tm = 256
    n_tiles = S // tm

    def body(dx_ref, dxn_ref, gt_ref, mo_ref, cb_ref, cc_ref, cx_ref, ga_ref, gb_ref, cbn_ref, gbn_ref, ccp_ref, cxp_ref,
             cw_ref, ba_ref, bb_ref, ya_ref, yc_ref, o_ref, wout_ref, wbc_ref, wba_ref,
             dmo_ref, dya_ref, dyc_ref, do_ref, dl_ref, de_ref, pv_ref):
        i = pl.program_id(0)

        @pl.when(i == 0)
        def _():
            pv_ref[...] = jnp.zeros_like(pv_ref)

        gate = gt_ref[...]
        bbv = bb_ref[...]

        def conv_branch_grad(dx_rows, gb_rows):
            dmo = (dx_rows * gate).astype(BF16)
            dmg = _nt(dmo, wout_ref[...])
            sb = _sigmoid(gb_rows + bbv)
            dyc = dmg * sb
            return dmo, dmg, sb, dyc, _nt(dyc.astype(BF16), wbc_ref[...])

        dx = dx_ref[...]
        cb = cb_ref[0].astype(F32)
        cc = cc_ref[0].astype(F32)
        cx = cx_ref[0].astype(F32)
        dmo, dmg, sb, dyc, dcbu = conv_branch_grad(dx, gb_ref[0].astype(F32))
        dmo_ref[...] = dmo
        pv_ref[0:1, :] += _rowsum(dx * mo_ref[...].astype(F32))
        sa = _sigmoid(ga_ref[0].astype(F32) + ba_ref[...])
        dya = (dmg * sa).astype(BF16)
        dya_ref[...] = dya
        dyc_ref[...] = dyc.astype(BF16)
        dga = dmg * ya_ref[...].astype(F32) * sa * (1.0 - sa)
        dgb = dmg * yc_ref[...].astype(F32) * sb * (1.0 - sb)
        pv_ref[1:2, :] += _rowsum(dga)
        pv_ref[2:3, :] += _rowsum(dgb)

        do = _nn(dya, wba_ref[...])
        do_ref[...] = do
        prod = do * o_ref[...]
        dl_ref[...] = jnp.concatenate(
            [jnp.broadcast_to(jnp.sum(prod[:, s * HEAD:(s + 1) * HEAD], axis=-1, keepdims=True), (tm, HEAD))
             for s in range(N_SLOT)], axis=1)

        z = cc * cx
        zp = ccp_ref[0].astype(F32) * cxp_ref[0].astype(F32) * (i > 0).astype(F32)
        z1 = _shift_down(z, 1, [zp[15:16]])
        z2 = _shift_down(z, 2, [zp[14:15], zp[15:16]])
        cw = cw_ref[...]
        u = cw[0:1] * z2 + cw[1:2] * z1 + cw[2:3] * z
        du = dcbu * cb
        dcbu_n = conv_branch_grad(dxn_ref[...], gbn_ref[0].astype(F32))[4]
        du_n = dcbu_n * cbn_ref[0].astype(F32) * (i < n_tiles - 1).astype(F32)
        du1 = _shift_up(du, 1, [du_n[0:1]])
        du2 = _shift_up(du, 2, [du_n[0:1], du_n[1:2]])
        dz = cw[2:3] * du + cw[1:2] * du1 + cw[0:1] * du2
        pv_ref[3:4, :] += _rowsum(du * z2)
        pv_ref[4:5, :] += _rowsum(du * z1)
        pv_ref[5:6, :] += _rowsum(du * z)

        de_ref[0] = (dcbu * u).astype(BF16)
        de_ref[1] = (dz * cx).astype(BF16)
        de_ref[2] = (dz * cc).astype(BF16)
        de_ref[3] = dga.astype(BF16)
        de_ref[4] = dgb.astype(BF16)

    row = lambda w: pl.BlockSpec((tm, w), lambda i: (i, 0))
    nxt = pl.BlockSpec((16, D), lambda i: (jnp.minimum((i + 1) * (tm // 16), S // 16 - 1), 0))
    return pl.pallas_call(
        body, name="bwd_mix", grid=(n_tiles,),
        in_specs=[row(D), nxt, _vec_spec(), row(D)] + [_e_spec(c, tm) for c in range(5)]
                 + [_e_next_spec(0, tm, S), _e_next_spec(4, tm, S), _e_prev_spec(1, tm), _e_prev_spec(2, tm),
                    _const_spec((8, D)), _vec_spec(), _vec_spec(), row(D), row(D), row(AOW),
                    _const_spec((D, D)), _const_spec((D, D)), _const_spec((D, AOW))],
        out_specs=[row(D), row(D), row(D), row(AOW), row(AOW), pl.BlockSpec((5, tm, D), lambda i: (0, i, 0)),
                   _const_spec((8, D))],
        out_shape=[jax.ShapeDtypeStruct((S, D), BF16)] * 3 + [jax.ShapeDtypeStruct((S, AOW), F32)] * 2
                  + [jax.ShapeDtypeStruct((5, S, D), BF16), jax.ShapeDtypeStruct((8, D), F32)],
        compiler_params=_cparams("arbitrary"),
    )(dx1, dx1, gate1, mo, e, e, e, e, e, e, e, e, e, cw8, ba, bb, ya, yc, o_attn, w_out, w_bc, w_bat)


def _attn_bwd(qkv, do, lse, dl, bias_t):
    S = qkv.shape[2]
    nblk = S // HEAD

    def body(qkv_ref, do_ref, lse_ref, dl_ref, b_ref, d_ref):
        g = pl.program_id(1)
        bias = b_ref[0, 0]
        col = lax.broadcasted_iota(jnp.int32, bias.shape, 1)
        bias_last = jnp.where(col >= HEAD, NEG, bias)
        eye = (lax.broadcasted_iota(jnp.int32, (HEAD, HEAD), 0) == lax.broadcasted_iota(jnp.int32, (HEAD, HEAD), 1)).astype(F32)

        def as_row(t):
            return jnp.sum(t * eye, axis=0, keepdims=True)

        for gi, d in enumerate(DILATIONS):
            @pl.when(g == gi)
            def _(d=d):
                nb = nblk // d

                def step(b, dq_part):
                    r, n = b // nb, b % nb
                    cur = pl.ds(n * (HEAD * d) + r, HEAD, stride=d)
                    nxt = pl.ds(jnp.minimum(n + 1, nb - 1) * (HEAD * d) + r, HEAD, stride=d)
                    two = lambda ref: jnp.concatenate([ref[cur, :], ref[nxt, :]], axis=0)
                    two_rows = lambda ref: jnp.concatenate([as_row(ref[cur, :]), as_row(ref[nxt, :])], axis=1)
                    q2 = two(qkv_ref.at[0, 0]).astype(BF16)
                    do2 = two(do_ref).astype(BF16)
                    k = qkv_ref.at[0, 1][cur, :].astype(BF16)
                    v = qkv_ref.at[0, 2][cur, :].astype(BF16)
                    s = _nt(k, q2) * SCALE + jnp.where(n < nb - 1, bias, bias_last)
                    p = jnp.exp(s - two_rows(lse_ref))
                    d_ref.at[0, 2][cur, :] = _nn(p.astype(BF16), do2)
                    dp = _nt(v, do2)
                    ds = (p * (dp - two_rows(dl_ref)) * SCALE).astype(BF16)
                    d_ref.at[0, 1][cur, :] = _nn(ds, q2)
                    dq2 = _tn(ds, k)
                    d_ref.at[0, 0][cur, :] = dq2[:HEAD] + jnp.where(n > 0, dq_part, 0.0)
                    return dq2[HEAD:]

                def steps(i, dq_part):
                    for u in range(UNROLL):
                        dq_part = step(i * UNROLL + u, dq_part)
                    return dq_part

                lax.fori_loop(0, nblk // UNROLL, steps, jnp.zeros((HEAD, HEAD), F32))

    col_blk = pl.BlockSpec((S, HEAD), lambda j, g: (0, j))
    qkv_blk = pl.BlockSpec((1, 3, S, HEAD), lambda j, g: (g, 0, 0, j))
    return pl.pallas_call(
        body, name="attn_bwd", grid=(N_SLOT, 3),
        in_specs=[qkv_blk, col_blk, col_blk, col_blk, pl.BlockSpec((1, 1, HEAD, 2 * HEAD), lambda j, g: (g, j, 0, 0))],
        out_specs=qkv_blk,
        out_shape=jax.ShapeDtypeStruct((3, 3, S, AOW), F32),
        compiler_params=_cparams("parallel", "arbitrary"),
    )(qkv, do, lse, dl, bias_t)


def _bwd_in(dqkv, de, w_int, x, dx1, g_mix, sc1):
    S = x.shape[0]
    tm = TM

    def body(dq_ref, de_ref, w_ref, x_ref, dx1_ref, g_ref, sc_ref, gx_ref, pv_ref, acc):
        i, k = pl.program_id(0), pl.program_id(1)

        @pl.when((i == 0) & (k == 0))
        def _():
            pv_ref[...] = jnp.zeros_like(pv_ref)

        @pl.when(k == 0)
        def _():
            acc[...] = jnp.zeros_like(acc)

        @pl.when(k < 9)
        def _():
            acc[...] += _nn(dq_ref[0].astype(BF16), w_ref[...])

        @pl.when(k >= 9)
        def _():
            acc[...] += _nn(de_ref[0], w_ref[...])

        @pl.when(k == 18)
        def _():
            dh = acc[...]
            xv = x_ref[...]
            r = _rms_r(xv)
            g = g_ref[...]
            dxn, pg = _rms_bwd(xv, r, g, dh * (1.0 + sc_ref[...]))
            gx_ref[...] = dx1_ref[...] + dxn
            pv_ref[0:1, :] += _rowsum(dh)
            pv_ref[1:2, :] += _rowsum(dh * (xv * r * g))
            pv_ref[2:3, :] += _rowsum(pg)

    def e_idx(i, k):
        kk = jnp.maximum(k - 9, 0)
        return (kk // 2, i, kk % 2)

    row = pl.BlockSpec((tm, D), lambda i, k: (i, 0))
    return pl.pallas_call(
        body, name="bwd_in", grid=(S // tm, 19),
        in_specs=[pl.BlockSpec((1, tm, 512), lambda i, k: (jnp.minimum(k, 8), i, 0)), pl.BlockSpec((1, tm, 512), e_idx),
                  pl.BlockSpec((512, D), lambda i, k: (_win_rowblock(k), 0)), row, row, _vec_spec(), _vec_spec()],
        out_specs=[row, _const_spec((8, D))],
        out_shape=[jax.ShapeDtypeStruct((S, D), F32), jax.ShapeDtypeStruct((8, D), F32)],
        scratch_shapes=[pltpu.VMEM((tm, D), F32)],
        compiler_params=_cparams("arbitrary", "arbitrary"),
    )(dqkv, de, w_int, x, dx1, g_mix, sc1)


def _grad_w(name, a, b):
    S, ka = a.shape
    nb = b.shape[1]

    def body(a_ref, b_ref, o_ref):
        o_ref[...] = _tn(a_ref[...], b_ref[...]).astype(BF16)

    return pl.pallas_call(
        body, name=name, grid=(ka // 512,),
        in_specs=[pl.BlockSpec((S, 512), lambda n: (0, n)), pl.BlockSpec((S, nb), lambda n: (0, 0))],
        out_specs=pl.BlockSpec((512, nb), lambda n: (n, 0)),
        out_shape=jax.ShapeDtypeStruct((ka, nb), BF16),
        compiler_params=_cparams("parallel"),
    )(a, b)


def _grad_w_in(dqkv, de, h):
    S = h.shape[0]

    def body(dq_ref, de_ref, h_ref, o_ref):
        n = pl.program_id(0)

        @pl.when(n < 9)
        def _():
            o_ref[...] = _tn(dq_ref[0].astype(BF16), h_ref[...]).astype(BF16)

        @pl.when(n >= 9)
        def _():
            o_ref[...] = _tn(de_ref[0], h_ref[...]).astype(BF16)

    def e_idx(n):
        kk = jnp.maximum(n - 9, 0)
        return (kk // 2, 0, kk % 2)

    return pl.pallas_call(
        body, name="grad_w_in", grid=(19,),
        in_specs=[pl.BlockSpec((1, S, 512), lambda n: (jnp.minimum(n, 8), 0, 0)), pl.BlockSpec((1, S, 512), e_idx),
                  pl.BlockSpec((S, D), lambda n: (0, 0))],
        out_specs=pl.BlockSpec((512, D), lambda n: (_win_rowblock(n), 0)),
        out_shape=jax.ShapeDtypeStruct((19 * 512, D), BF16),
        compiler_params=_cparams("parallel"),
    )(dqkv, de, h)


def _local_step(x, tgt, mod, g_mix, g_mlp, g_fin, ba, bb, cw8, w_int, mix_weights, mlp_weights, mlp_grads_ready, other_grads_ready):
    S = x.shape[0]
    sh1, sc1, gt1, sh2, sc2, gt2 = [mod[k:k + 1] for k in range(6)]
    bias, bias_t = _bias_table()

    h, qkv, e = _proj(x, g_mix, sc1, sh1, w_int)
    qkv = qkv.reshape(3, 3, S, AOW)
    o_attn, lse = _attn_fwd(qkv, bias)
    w_bat, w_bc, w_out = mix_weights(o_attn)
    o_bf, cbu, ya, yc, merged = _mix(o_attn, e, cw8, ba, bb, w_bat, w_bc)
    x1, mo, h2 = _out_proj(merged, w_out, x, gt1, g_mlp, sc2, sh2)
    w_mit, w_mo = mlp_weights(x1)
    a, f = _mlp_in(h2, w_mit)
    mlp, dx2, pv_f = _mlp_out(f, w_mo, x1, gt2, g_fin, tgt)

    da, dmo2, pv_a = _bwd_mlp_a(dx2, gt2, mlp, w_mo, a)
    dx1, pv_b = _bwd_mlp_b(da, w_mit, x1, dx2, g_mlp, sc2)
    zero = mlp_grads_ready(_grad_w("grad_w_mi", da, h2), _grad_w("grad_w_mo", f, dmo2))
    dmo, dya, dyc, do, dl, de, pv_m = _bwd_mix(dx1, gt1 + zero, mo, e, cw8, ba, bb, ya, yc, o_attn, w_out, w_bc, w_bat)
    dqkv = _attn_bwd(qkv, do, lse, dl, bias_t).reshape(9, S, AOW)
    zero = other_grads_ready(_grad_w_in(dqkv, de, h), _grad_w("grad_w_ba", dya, o_bf), _grad_w("grad_w_bc", cbu, dyc),
                             _grad_w("grad_w_out", merged, dmo))
    grad_x, pv_i = _bwd_in(dqkv, de, w_int, x, dx1, g_mix, sc1 + zero)

    vec = jnp.concatenate([pv_i[0:2], pv_m[0:1], pv_b[0:2], pv_a[0:1], pv_i[2:3], pv_b[2:3], pv_f[0:1],
                           pv_m[1:3], pv_m[3:6], jnp.zeros((2, D), F32)], axis=0)
    return pv_f[1, 0], grad_x, vec


def _my_place():
    return lax.axis_index("x"), lax.axis_index("y"), lax.axis_index("c")


def _dev_index(px, py, pc):
    return 4 * px + 2 * py + pc


def _allgather_weights(shards):
    nw = len(shards)
    HBM = pl.BlockSpec(memory_space=pl.ANY)

    def body(*refs):
        sh, full = refs[:nw], refs[nw:2 * nw]
        send_sems, recv_sems, local_sems = refs[2 * nw:]
        x, y, c = _my_place()
        me, sibling = (x, y, c), (x, y, 1 - c)
        chips = [(1 - x, y), (x, 1 - y), (1 - x, 1 - y)]

        def rows(w, px, py, pc):
            r = sh[w].shape[0]
            return full[w].at[pl.ds(pl.multiple_of(_dev_index(px, py, pc) * r, 16), r), :]

        def copy(w, k, block, to, src=None):
            return pltpu.make_async_remote_copy(
                src_ref=rows(w, *block) if src is None else src, dst_ref=rows(w, *block),
                send_sem=send_sems.at[w, k], recv_sem=recv_sems.at[w, k], device_id=to, device_id_type=MESH)

        mine = [pltpu.make_async_copy(sh[w], rows(w, *me), local_sems.at[w]) for w in range(nw)]
        for cp in mine:
            cp.start()
        first = []
        for w in range(nw):
            first.append(copy(w, 0, me, sibling, src=sh[w]))
            first += [copy(w, 1 + j, me, (*chip, c), src=sh[w]) for j, chip in enumerate(chips)]
        for cp in first:
            cp.start()
        passed = []
        for w in range(nw):
            for j, chip in enumerate(chips):
                copy(w, 1 + j, (*chip, c), me).wait_recv()
                fwd = copy(w, 4 + j, (*chip, c), sibling)
                fwd.start()
                passed.append(fwd)
        for w in range(nw):
            copy(w, 0, sibling, me).wait_recv()
            for j, chip in enumerate(chips):
                copy(w, 4 + j, (*chip, 1 - c), me).wait_recv()
        for cp in first + passed:
            cp.wait_send()
        for cp in mine:
            cp.wait()

    return pl.pallas_call(
        body, name="allgather_weights",
        out_shape=[jax.ShapeDtypeStruct((N_DEV * s.shape[0], s.shape[1]), s.dtype) for s in shards],
        in_specs=[HBM] * nw, out_specs=[HBM] * nw,
        scratch_shapes=[pltpu.SemaphoreType.DMA((nw, 7)), pltpu.SemaphoreType.DMA((nw, 7)), pltpu.SemaphoreType.DMA((nw,))],
    )(*shards)


def _peer(x, y, c, m):
    return (x ^ ((m >> 2) & 1), y ^ ((m >> 1) & 1), c ^ (m & 1))


HBM_SPEC = pl.BlockSpec(memory_space=pltpu.HBM)
SEM_SPEC = pl.BlockSpec(memory_space=pltpu.SEMAPHORE)
N_PEER = N_DEV - 1


SPLIT_MASKS = {"gather": tuple(range(1, N_DEV)), "scatter": tuple(range(1, N_DEV)), "chips": (2, 4, 6)}


def _split_copy(mode, src_ref, land_ref, send_sems, recv_sems, w, j, place, arriving=False):
    x, y, c = place
    masks = SPLIT_MASKS[mode]
    peer = _peer(x, y, c, masks[j])
    k = w * len(masks) + j
    sender, receiver = ((peer, (x, y, c)) if arriving else ((x, y, c), peer))
    if mode == "gather":
        r = src_ref.shape[0]
        src, dst = src_ref, land_ref.at[pl.ds(pl.multiple_of(_dev_index(*sender) * r, 16), r), :]
    elif mode == "scatter":
        r = land_ref.shape[1]
        src, dst = src_ref.at[pl.ds(pl.multiple_of(_dev_index(*receiver) * r, 16), r), :], land_ref.at[j]
    else:
        src, dst = src_ref.at[2 * receiver[0] + receiver[1]], land_ref.at[j]
    return pltpu.make_async_remote_copy(src_ref=src, dst_ref=dst, send_sem=send_sems.at[k], recv_sem=recv_sems.at[k],
                                        device_id=peer, device_id_type=MESH)


def _split_start(name, mode, srcs, lands):
    n = len(srcs)
    nm = len(SPLIT_MASKS[mode])

    def body(*refs):
        src, land = refs[:n], refs[n:2 * n]
        send_sems, recv_sems = refs[2 * n], refs[2 * n + 1]
        token = refs[-1]
        place = _my_place()
        for w in range(n):
            for j in range(nm):
                _split_copy(mode, src[w], land[w], send_sems, recv_sems, w, j, place).start()
        token[...] = jnp.zeros_like(token)

    hbm = lambda t: pltpu.HBM(t.shape, t.dtype)
    out = pl.pallas_call(
        body, name=name,
        out_shape=(pltpu.SemaphoreType.DMA((n * nm,)), pltpu.SemaphoreType.DMA((n * nm,)), *[hbm(t) for t in srcs],
                   *[hbm(t) for t in lands], jax.ShapeDtypeStruct((8, 128), F32)),
        in_specs=(HBM_SPEC,) * (2 * n),
        out_specs=(SEM_SPEC, SEM_SPEC) + (HBM_SPEC,) * (2 * n) + (pl.BlockSpec(memory_space=pltpu.VMEM),),
        input_output_aliases={i: 2 + i for i in range(2 * n)},
        compiler_params=pltpu.CompilerParams(has_side_effects=pltpu.SideEffectType.DATAFLOW_SIDE_EFFECTING),
    )(*[pltpu.with_memory_space_constraint(t, pltpu.HBM) for t in (*srcs, *lands)])
    return out[0], out[1], out[2:2 + n], out[2 + n:2 + 2 * n], out[-1][0:1, 0:1]


def _split_wait(name, mode, send_sems, recv_sems, srcs, lands, after):
    n = len(srcs)

    def body(*refs):
        src, land = refs[:n], refs[n:2 * n]
        ssem, rsem = refs[2 * n], refs[2 * n + 1]
        place = _my_place()
        for w in range(n):
            for j in range(len(SPLIT_MASKS[mode])):
                _split_copy(mode, src[w], land[w], ssem, rsem, w, j, place).wait_send()
                _split_copy(mode, src[w], land[w], ssem, rsem, w, j, place, arriving=True).wait_recv()

    hbm = lambda t: pltpu.HBM(t.shape, t.dtype)
    out = pl.pallas_call(
        body, name=name,
        out_shape=tuple(hbm(t) for t in (*srcs, *lands)),
        in_specs=(HBM_SPEC,) * (2 * n) + (SEM_SPEC, SEM_SPEC, pl.BlockSpec(memory_space=pl.ANY)),
        out_specs=(HBM_SPEC,) * (2 * n),
        input_output_aliases={i: i for i in range(2 * n)},
        compiler_params=pltpu.CompilerParams(has_side_effects=pltpu.SideEffectType.DATAFLOW_SIDE_EFFECTING),
    )(*srcs, *lands, send_sems, recv_sems, after)
    return out[:n], out[n:]


def _sibling_exchange(grads):
    nw = len(grads)
    HBM = pl.BlockSpec(memory_space=pl.ANY)

    def body(*refs):
        g, land = refs[:nw], refs[nw:2 * nw]
        send_sems, recv_sems = refs[2 * nw:]
        x, y, c = _my_place()

        def copy(w, q, owner_core):
            r = land[w].shape[1]
            return pltpu.make_async_remote_copy(
                src_ref=g[w].at[pl.ds(pl.multiple_of((2 * q + owner_core) * r, 16), r), :], dst_ref=land[w].at[q],
                send_sem=send_sems.at[w, q], recv_sem=recv_sems.at[w, q], device_id=(x, y, 1 - c), device_id_type=MESH)

        sends = [copy(w, q, 1 - c) for w in range(nw) for q in range(4)]
        for cp in sends:
            cp.start()
        for w in range(nw):
            for q in range(4):
                copy(w, q, c).wait_recv()
        for cp in sends:
            cp.wait_send()

    return pl.pallas_call(
        body, name="sibling_exchange",
        out_shape=[jax.ShapeDtypeStruct((4, a.shape[0] // N_DEV, a.shape[1]), a.dtype) for a in grads],
        in_specs=[HBM] * nw, out_specs=[HBM] * nw,
        scratch_shapes=[pltpu.SemaphoreType.DMA((nw, 4)), pltpu.SemaphoreType.DMA((nw, 4))],
    )(*grads)


def _pair_sum(g, sib, core, name):
    _, r, ccols = sib.shape
    tr = _row_tile(r)

    def body(core_ref, g_ref, s_ref, o_ref):
        o_ref[0] = (g_ref[0, 0].astype(F32) + s_ref[0].astype(F32)).astype(BF16)

    return pl.pallas_call(
        body, name=name,
        grid_spec=pltpu.PrefetchScalarGridSpec(
            num_scalar_prefetch=1, grid=(4, r // tr),
            in_specs=[pl.BlockSpec((1, 1, tr, ccols), lambda q, i, core_ref: (q, core_ref[0], i, 0)),
                      pl.BlockSpec((1, tr, ccols), lambda q, i, core_ref: (q, i, 0))],
            out_specs=pl.BlockSpec((1, tr, ccols), lambda q, i, core_ref: (q, i, 0))),
        out_shape=jax.ShapeDtypeStruct(sib.shape, BF16),
        compiler_params=_cparams("parallel", "parallel"),
    )(core, g.reshape(4, 2, r, ccols), sib)


def _allgather_small(v, name):
    r, ccols = v.shape

    def body(v_ref, out_ref, send_sems, recv_sems):
        x, y, c = _my_place()
        my_idx = _dev_index(x, y, c)
        out_ref[my_idx] = v_ref[...]

        def copy(m):
            peer = _peer(x, y, c, m)
            return pltpu.make_async_remote_copy(
                src_ref=v_ref, dst_ref=out_ref.at[my_idx],
                send_sem=send_sems.at[m - 1], recv_sem=recv_sems.at[m - 1], device_id=peer, device_id_type=MESH)

        def arrival(m):
            peer = _peer(x, y, c, m)
            return pltpu.make_async_remote_copy(
                src_ref=v_ref, dst_ref=out_ref.at[_dev_index(*peer)],
                send_sem=send_sems.at[m - 1], recv_sem=recv_sems.at[m - 1], device_id=peer, device_id_type=MESH)

        sends = [copy(m) for m in range(1, N_DEV)]
        for cp in sends:
            cp.start()
        for m in range(1, N_DEV):
            arrival(m).wait_recv()
        for cp in sends:
            cp.wait_send()

    return pl.pallas_call(
        body, name=name,
        out_shape=jax.ShapeDtypeStruct((N_DEV, r, ccols), v.dtype),
        in_specs=[pl.BlockSpec(memory_space=pltpu.VMEM)], out_specs=pl.BlockSpec(memory_space=pltpu.VMEM),
        scratch_shapes=[pltpu.SemaphoreType.DMA((7,)), pltpu.SemaphoreType.DMA((7,))],
    )(v)


def _ada_fwd(c_all, w_ada, b_cols):
    def body(c_ref, w_ref, b_ref, mod_ref, act_ref):
        cv = c_ref[...]
        act = cv * _sigmoid(cv)
        act_ref[...] = act
        mod_ref[...] = jnp.dot(act, w_ref[...], preferred_element_type=F32, precision=lax.Precision.HIGHEST) + b_ref[...]

    return pl.pallas_call(
        body, name="ada_fwd",
        out_shape=[jax.ShapeDtypeStruct((N_DEV, w_ada.shape[1]), F32), jax.ShapeDtypeStruct((N_DEV, D), F32)],
        compiler_params=_cparams(),
    )(c_all, w_ada, b_cols)


def _ada_bwd(act_t, gm_cols):
    def body(a_ref, g_ref, o_ref):
        o_ref[...] = jnp.dot(a_ref[...], g_ref[...], preferred_element_type=F32, precision=lax.Precision.HIGHEST)

    return pl.pallas_call(
        body, name="ada_bwd", out_shape=jax.ShapeDtypeStruct((D, gm_cols.shape[1]), F32), compiler_params=_cparams(),
    )(act_t, gm_cols)


def _row_tile(r):
    for t in (256, 304, 128, 64, 16):
        if r % t == 0:
            return t
    return r


def _sum_parts(parts, name, own=None):
    k, r, ccols = parts.shape
    tr = _row_tile(r)

    def body(*refs):
        p_ref, o_ref = refs[0], refs[-1]
        acc = p_ref[0].astype(F32) if own is None else refs[1][...].astype(F32) + p_ref[0].astype(F32)
        for s in range(1, k):
            acc = acc + p_ref[s].astype(F32)
        o_ref[...] = acc

    blk = pl.BlockSpec((tr, ccols), lambda i: (i, 0))
    return pl.pallas_call(
        body, name=name, grid=(r // tr,),
        in_specs=[pl.BlockSpec((k, tr, ccols), lambda i: (0, i, 0))] + ([] if own is None else [blk]),
        out_specs=blk,
        out_shape=jax.ShapeDtypeStruct((r, ccols), F32),
        compiler_params=_cparams("parallel"),
    )(*((parts,) if own is None else (parts, own)))


def _adamw(w, g, m, v, name):
    r, ccols = w.shape
    tr = _row_tile(r)
    c1 = 1.0 / (1.0 - B1 ** STEP)
    c2 = 1.0 / (1.0 - B2 ** STEP)

    def body(w_ref, g_ref, m_ref, v_ref, d_ref, nm_ref, nv_ref):
        gv = g_ref[...]
        nm = B1 * m_ref[...] + (1.0 - B1) * gv
        nv = B2 * v_ref[...] + (1.0 - B2) * jnp.square(gv)
        nm_ref[...] = nm
        nv_ref[...] = nv
        d_ref[...] = -LR * ((nm * c1) / (jnp.sqrt(nv * c2) + ADAM_EPS) + WD * w_ref[...])

    blk = pl.BlockSpec((tr, ccols), lambda i: (i, 0))
    return pl.pallas_call(
        body, name=name, grid=(r // tr,), in_specs=[blk] * 4, out_specs=[blk] * 3,
        out_shape=[jax.ShapeDtypeStruct((r, ccols), F32)] * 3,
        compiler_params=_cparams("parallel"),
    )(w, g, m, v)


def _pack_vectors(b_ada, g_mix, g_mlp, g_fin, b_gate, conv_w):
    conv_rows = jnp.pad(conv_w.reshape(3, HEAD), ((0, 0), (0, D - HEAD)))
    return jnp.concatenate([b_ada.reshape(6, D), g_mix.reshape(1, D), g_mlp.reshape(1, D), g_fin.reshape(1, D),
                            b_gate.reshape(2, D), conv_rows, jnp.zeros((2, D), F32)], axis=0)


def _unpack_vectors(p):
    return (p[0:6].reshape(1, 6 * D), p[6:7], p[9:11].reshape(1, 2 * D), p[11:14, :HEAD].reshape(1, 3, HEAD),
            p[7:8], p[8])


def kernel(x, c, w_ada, b_ada, g_norm_mix, w_in, b_gate, conv_w, w_branch_attn, w_branch_conv, w_out, g_norm_mlp, w_mlp_in, w_mlp_out, g_norm_final, loss_target, m_w_ada, m_b_ada, m_g_norm_mix, m_w_in, m_b_gate, m_conv_w, m_w_branch_attn, m_w_branch_conv, m_w_out, m_g_norm_mlp, m_w_mlp_in, m_w_mlp_out, m_g_norm_final, v_w_ada, v_b_ada, v_g_norm_mix, v_w_in, v_b_gate, v_conv_w, v_w_branch_attn, v_w_branch_conv, v_w_out, v_g_norm_mlp, v_w_mlp_in, v_w_mlp_out, v_g_norm_final):
    S = x.shape[1]
    xi, yi, ci = _my_place()
    me = _dev_index(xi, yi, ci)
    x2 = x.reshape(S, D)
    tgt = loss_target.reshape(S, D)

    pay = jnp.zeros((8, D), F32).at[0].set(c[0]).at[1:4, :HEAD].set(conv_w[0])
    got = _allgather_small(pay, "gather_cond")
    c_all = got[:, 0, :]
    cw8 = jnp.pad(got[:, 1:4, :HEAD].transpose(1, 0, 2).reshape(3, D), ((0, 5), (0, 0)))
    ncol = w_ada.shape[2]
    b_cols = lax.dynamic_slice(b_ada, (0, me * ncol), (1, ncol))
    mod_cols, act = _ada_fwd(c_all, w_ada[0], b_cols)
    mod_all = _allgather_small(mod_cols, "gather_mod")

    w_in_shard, mod_all = lax.optimization_barrier((w_in[0].T.astype(BF16), mod_all))
    mod = lax.dynamic_index_in_dim(mod_all, me, axis=1, keepdims=False).reshape(6, D)
    (w_int,) = _allgather_weights([w_in_shard])
    late = [w_branch_attn[0].T.astype(BF16), w_branch_conv[0].astype(BF16), w_out[0].astype(BF16),
            w_mlp_in[0].T.astype(BF16), w_mlp_out[0].astype(BF16)]
    w_int, late = lax.optimization_barrier((w_int, late))
    zones = [lax.dynamic_update_slice(lax.empty((N_DEV * t.shape[0], t.shape[1]), BF16), t, (me * t.shape[0], 0)) for t in late]
    ag_mix = _split_start("gather_mix_start", "gather", late[:3], zones[:3])
    ag_mlp = _split_start("gather_mlp_start", "gather", late[3:], zones[3:])

    def mix_weights(o_attn):
        return _split_wait("gather_mix_wait", "gather", *ag_mix[:4], o_attn)[1]

    def mlp_weights(x1):
        return _split_wait("gather_mlp_wait", "gather", *ag_mlp[:4], x1)[1]

    rs = {}

    def mlp_grads_ready(*grads):
        lands = [lax.empty((N_PEER, t.shape[0] // N_DEV, t.shape[1]), BF16) for t in grads]
        rs["mlp"] = _split_start("scatter_mlp_start", "scatter", grads, lands)
        return rs["mlp"][4]

    def other_grads_ready(*grads):
        core = ci.reshape(1).astype(jnp.int32)
        pair = [_pair_sum(g, sib, core, "pair_sum_%d" % k) for k, (g, sib) in enumerate(zip(grads, _sibling_exchange(grads)))]
        lands = [lax.empty((3,) + t.shape[1:], BF16) for t in pair]
        rs["rest"] = _split_start("scatter_rest_start", "chips", pair, lands)
        return rs["rest"][4]

    ba, bb = b_gate[:, :D], b_gate[:, D:]
    loss_part, grad_x, vec = _local_step(
        x2, tgt, mod + ag_mix[4] + ag_mlp[4], g_norm_mix, g_norm_mlp, g_norm_final.reshape(1, D), ba, bb, cw8, w_int, mix_weights, mlp_weights,
        mlp_grads_ready, other_grads_ready)
    loss = lax.psum(loss_part, AXES)

    vec_all = _allgather_small(vec, "gather_vec")
    vec_sum = _sum_parts(vec_all, "sum_vec")
    gm_all = vec_all[:, 0:6, :].reshape(N_DEV, 6 * D)
    gm_cols = lax.dynamic_slice(gm_all, (0, me * ncol), (N_DEV, ncol))
    g_w_ada = _ada_bwd(act.T, gm_cols)
    conv_cols = lax.dynamic_slice(vec_sum[11:14], (0, me * HEAD), (3, HEAD))
    g_pack = jnp.concatenate([vec_sum[0:11], jnp.pad(conv_cols, ((0, 0), (0, D - HEAD))), jnp.zeros((2, D), F32)], axis=0)
    packs = [_pack_vectors(*t) for t in ((b_ada, g_norm_mix, g_norm_mlp, g_norm_final, b_gate, conv_w),
                                         (m_b_ada, m_g_norm_mix, m_g_norm_mlp, m_g_norm_final, m_b_gate, m_conv_w),
                                         (v_b_ada, v_g_norm_mix, v_g_norm_mlp, v_g_norm_final, v_b_gate, v_conv_w))]
    d_pack, m_pack, v_pack = _adamw(packs[0], g_pack, packs[1], packs[2], "adamw_vectors")
    d_ada, nm_ada, nv_ada = _adamw(w_ada[0], g_w_ada, m_w_ada[0], v_w_ada[0], "adamw_w_ada")

    sums = {}
    srcs, lands = _split_wait("scatter_mlp_wait", "scatter", *rs["mlp"][:4], grad_x)
    for n, g, land in zip(("w_mi", "w_mo"), srcs, lands):
        r = land.shape[1]
        sums[n] = _sum_parts(land, "sum_" + n, own=lax.dynamic_slice(g, (me * r, 0), (r, g.shape[1])))
    srcs, lands = _split_wait("scatter_rest_wait", "chips", *rs["rest"][:4], grad_x)
    for n, pair, land in zip(("w_in", "w_ba", "w_bc", "w_out"), srcs, lands):
        sums[n] = _sum_parts(land, "sum_" + n, own=lax.dynamic_index_in_dim(pair, 2 * xi + yi, axis=0, keepdims=False))
    g_ba, g_bc, g_out, g_mi, g_mo = sums["w_ba"].T, sums["w_bc"], sums["w_out"], sums["w_mi"].T, sums["w_mo"]
    big = {}
    for n, w, g, m, v in (("w_ba", w_branch_attn, g_ba, m_w_branch_attn, v_w_branch_attn),
                          ("w_bc", w_branch_conv, g_bc, m_w_branch_conv, v_w_branch_conv), ("w_out", w_out, g_out, m_w_out, v_w_out),
                          ("w_mi", w_mlp_in, g_mi, m_w_mlp_in, v_w_mlp_in), ("w_mo", w_mlp_out, g_mo, m_w_mlp_out, v_w_mlp_out)):
        big[n] = (g[None],) + tuple(t[None] for t in _adamw(w[0], g, m[0], v[0], "adamw_" + n))
    g_in_t = sums["w_in"]
    big["w_in"] = tuple(t.T[None] for t in (g_in_t, *_adamw(w_in[0].T, g_in_t, m_w_in[0].T, v_w_in[0].T, "adamw_w_in")))

    gv = _unpack_vectors(g_pack)
    dv = _unpack_vectors(d_pack)
    mv = _unpack_vectors(m_pack)
    vv = _unpack_vectors(v_pack)

    def ordered(k, ada, vecs):
        return (ada[None], vecs[0], vecs[1], big["w_in"][k], vecs[2], vecs[3], big["w_ba"][k], big["w_bc"][k],
                big["w_out"][k], vecs[4], big["w_mi"][k], big["w_mo"][k], vecs[5])

    return (loss, grad_x.reshape(1, S, D), *ordered(0, g_w_ada, gv), *ordered(1, d_ada, dv),
            *ordered(2, nm_ada, mv), *ordered(3, nv_ada, vv))
```

```python
import functools

import numpy as np
import jax
import jax.numpy as jnp
from jax import lax
from jax.experimental import pallas as pl
from jax.experimental.pallas import tpu as pltpu

F32, BF16 = jnp.float32, jnp.bfloat16
D = 1024
HEAD = 128
DILATIONS = (1, 4, 16)
N_SLOT = 4
AOW = N_SLOT * HEAD
DFF = 4 * D
N_DEV = 8
UNROLL = 8
EPS = 1e-6
NEG = -1e30
SCALE = HEAD ** -0.5
LR, B1, B2, ADAM_EPS, WD, STEP = 0.001, 0.9, 0.999, 1e-08, 0.01, 10
V7X_VMEM_LIMIT = 56 * 1024 * 1024
TM = 1024
MESH = pl.DeviceIdType.MESH
AXES = ("x", "y", "c")


def _cparams(*sem):
    if sem:
        return pltpu.CompilerParams(dimension_semantics=sem, vmem_limit_bytes=V7X_VMEM_LIMIT)
    return pltpu.CompilerParams(vmem_limit_bytes=V7X_VMEM_LIMIT)


def _nn(a, b):
    return jnp.dot(a, b, preferred_element_type=F32)


def _nt(a, b):
    return lax.dot_general(a, b, (((1,), (1,)), ((), ())), preferred_element_type=F32)


def _tn(a, b):
    return lax.dot_general(a, b, (((0,), (0,)), ((), ())), preferred_element_type=F32)


def _rms_r(x):
    return lax.rsqrt(jnp.mean(x * x, axis=-1, keepdims=True) + EPS)


def _rms_bwd(x, r, g, dn):
    gy = dn * g
    dx = r * gy - x * (r * r * r) * jnp.mean(x * gy, axis=-1, keepdims=True)
    return dx, dn * (x * r)


def _sigmoid(t):
    return 1.0 / (1.0 + jnp.exp(-t))


def _rowsum(v):
    return jnp.sum(v, axis=0, keepdims=True)


def _vec_spec(n=D):
    return pl.BlockSpec((1, n), lambda *_: (0, 0))


def _const_spec(shape):
    nd = len(shape)
    return pl.BlockSpec(shape, lambda *_: (0,) * nd)


def _win_rowblock(j):
    return jnp.where(j < 9, (j % 3) * 3 + j // 3, j)


def _proj(x, g, sc, sh, w_int):
    S = x.shape[0]
    tm = 2 * TM

    def body(x_ref, g_ref, sc_ref, sh_ref, w_ref, h_ref, q_ref, e_ref):
        j = pl.program_id(1)

        @pl.when(j == 0)
        def _():
            xv = x_ref[...]
            h = xv * _rms_r(xv) * g_ref[...] * (1.0 + sc_ref[...]) + sh_ref[...]
            h_ref[...] = h.astype(BF16)

        acc = _nt(h_ref[...], w_ref[...])

        @pl.when(j < 9)
        def _():
            q_ref[0] = acc

        @pl.when(j >= 9)
        def _():
            e_ref[0] = acc.astype(BF16)

    def e_idx(i, j):
        k = jnp.maximum(j - 9, 0)
        return (k // 2, i, k % 2)

    return pl.pallas_call(
        body, name="proj", grid=(S // tm, 19),
        in_specs=[pl.BlockSpec((tm, D), lambda i, j: (i, 0)), _vec_spec(), _vec_spec(), _vec_spec(),
                  pl.BlockSpec((512, D), lambda i, j: (_win_rowblock(j), 0))],
        out_specs=[pl.BlockSpec((tm, D), lambda i, j: (i, 0)),
                   pl.BlockSpec((1, tm, 512), lambda i, j: (jnp.minimum(j, 8), i, 0)),
                   pl.BlockSpec((1, tm, 512), e_idx)],
        out_shape=[jax.ShapeDtypeStruct((S, D), BF16), jax.ShapeDtypeStruct((9, S, 512), F32),
                   jax.ShapeDtypeStruct((5, S, D), BF16)],
        compiler_params=_cparams("parallel", "arbitrary"),
    )(x, g, sc, sh, w_int)


def _bias_table():
    slopes = (2.0 ** (-8.0 * np.arange(1, 13, dtype=np.float32) / 12.0)).astype(np.float32)
    qi = np.arange(HEAD)[:, None]
    kj = np.arange(2 * HEAD)[None, :]
    delta = HEAD + qi - kj
    mask = (delta >= 0) & (delta <= HEAD)
    out = np.zeros((3, N_SLOT, HEAD, 2 * HEAD), np.float32)
    for gi, d in enumerate(DILATIONS):
        for j in range(N_SLOT):
            bias = -slopes[gi * N_SLOT + j] * (delta * d).astype(np.float32)
            out[gi, j] = np.where(mask, bias, NEG)
    out_t = np.concatenate([out[..., HEAD:].swapaxes(-1, -2), out[..., :HEAD].swapaxes(-1, -2)], axis=-1)
    return jnp.asarray(out), jnp.asarray(out_t)


def _block_rows(b, d):
    r = b % d
    n = b // d
    st = n * (HEAD * d) + r
    stp = jnp.maximum(n - 1, 0) * (HEAD * d) + r
    return n, st, stp


def _attn_fwd(qkv, bias):
    S = qkv.shape[2]
    nblk = S // HEAD
    rows = 256

    def body(qkv_ref, b_ref, o_ref, lse_ref, o_s, lse_s):
        g = pl.program_id(1)
        bias = b_ref[0, 0]
        col = lax.broadcasted_iota(jnp.int32, bias.shape, 1)
        bias_first = jnp.where(col < HEAD, NEG, bias)

        for gi, d in enumerate(DILATIONS):
            @pl.when(g == gi)
            def _(gi=gi, d=d):
                def step(b, carry):
                    n, st, stp = _block_rows(b, d)
                    cur = pl.ds(st, HEAD, stride=d)
                    prv = pl.ds(stp, HEAD, stride=d)
                    q = qkv_ref.at[0, 0][cur, :].astype(BF16)
                    kw = jnp.concatenate([qkv_ref.at[0, 1][prv, :], qkv_ref.at[0, 1][cur, :]], axis=0).astype(BF16)
                    vw = jnp.concatenate([qkv_ref.at[0, 2][prv, :], qkv_ref.at[0, 2][cur, :]], axis=0).astype(BF16)
                    s = _nt(q, kw) * SCALE + jnp.where(n > 0, bias, bias_first)
                    m = jnp.max(s, axis=-1, keepdims=True)
                    p = jnp.exp(s - m)
                    l = jnp.sum(p, axis=-1, keepdims=True)
                    o_s.at[gi][cur, :] = _nn(p.astype(BF16), vw) / l
                    lse_s.at[gi][cur, :] = jnp.broadcast_to(m + jnp.log(l), (HEAD, HEAD))
                    return carry

                lax.fori_loop(0, nblk, step, 0, unroll=UNROLL)

        @pl.when(g == len(DILATIONS) - 1)
        def _():
            def merge(i, carry):
                r = pl.ds(pl.multiple_of(i * rows, rows), rows)
                ls = [lse_s[k, r, :] for k in range(3)]
                top = jnp.maximum(jnp.maximum(ls[0], ls[1]), ls[2])
                ws = [jnp.exp(t - top) for t in ls]
                den = ws[0] + ws[1] + ws[2]
                o_ref[r, :] = (ws[0] * o_s[0, r, :] + ws[1] * o_s[1, r, :] + ws[2] * o_s[2, r, :]) / den
                lse_ref[r, :] = top + jnp.log(den)
                return carry

            lax.fori_loop(0, S // rows, merge, 0)

    return pl.pallas_call(
        body, name="attn_fwd", grid=(N_SLOT, 3),
        in_specs=[pl.BlockSpec((1, 3, S, HEAD), lambda j, g: (g, 0, 0, j)),
                  pl.BlockSpec((1, 1, HEAD, 2 * HEAD), lambda j, g: (g, j, 0, 0))],
        out_specs=[pl.BlockSpec((S, HEAD), lambda j, g: (0, j)), pl.BlockSpec((S, HEAD), lambda j, g: (0, j))],
        out_shape=[jax.ShapeDtypeStruct((S, AOW), F32), jax.ShapeDtypeStruct((S, AOW), F32)],
        scratch_shapes=[pltpu.VMEM((3, S, HEAD), F32)] * 2,
        compiler_params=_cparams("parallel", "arbitrary"),
    )(qkv, bias)


def _shift_down(z, k, halo_rows):
    out = pltpu.roll(z, k, axis=0)
    top = out[:8]
    rid = lax.broadcasted_iota(jnp.int32, top.shape, 0)
    for t in range(k):
        top = jnp.where(rid == t, halo_rows[t], top)
    return jnp.concatenate([top, out[8:]], axis=0)


def _shift_up(z, k, halo_rows):
    n = z.shape[0]
    out = pltpu.roll(z, n - k, axis=0)
    bottom = out[n - 8:]
    rid = lax.broadcasted_iota(jnp.int32, bottom.shape, 0)
    for t in range(k):
        bottom = jnp.where(rid == 8 - k + t, halo_rows[t], bottom)
    return jnp.concatenate([out[:n - 8], bottom], axis=0)


def _e_spec(chunk, tm):
    return pl.BlockSpec((1, tm, D), lambda i, c=chunk: (c, i, 0))


def _e_prev_spec(chunk, tm):
    return pl.BlockSpec((1, 16, D), lambda i, c=chunk: (c, jnp.maximum(i * (tm // 16) - 1, 0), 0))


def _e_next_spec(chunk, tm, S):
    return pl.BlockSpec((1, 16, D), lambda i, c=chunk: (c, jnp.minimum((i + 1) * (tm // 16), S // 16 - 1), 0))


def _mix(o_attn, e, cw8, ba, bb, w_bat, w_bc):
    S = o_attn.shape[0]
    tm = 256

    def body(o_ref, cb_ref, cc_ref, cx_ref, ga_ref, gb_ref, ccp_ref, cxp_ref, cw_ref, ba_ref, bb_ref, wba_ref, wbc_ref,
             obf_ref, cbu_ref, ya_ref, yc_ref, mg_ref):
        i = pl.program_id(0)
        o = o_ref[...].astype(BF16)
        obf_ref[...] = o
        ya = _nt(o, wba_ref[...])
        z = cc_ref[0].astype(F32) * cx_ref[0].astype(F32)
        zp = ccp_ref[0].astype(F32) * cxp_ref[0].astype(F32) * (i > 0).astype(F32)
        z1 = _shift_down(z, 1, [zp[15:16]])
        z2 = _shift_down(z, 2, [zp[14:15], zp[15:16]])
        cw = cw_ref[...]
        u = cw[0:1] * z2 + cw[1:2] * z1 + cw[2:3] * z
        cbu = (cb_ref[0].astype(F32) * u).astype(BF16)
        cbu_ref[...] = cbu
        yc = _nn(cbu, wbc_ref[...])
        sa = _sigmoid(ga_ref[0].astype(F32) + ba_ref[...])
        sb = _sigmoid(gb_ref[0].astype(F32) + bb_ref[...])
        ya_ref[...] = ya.astype(BF16)
        yc_ref[...] = yc.astype(BF16)
        mg_ref[...] = (sa * ya + sb * yc).astype(BF16)

    row = lambda w: pl.BlockSpec((tm, w), lambda i: (i, 0))
    return pl.pallas_call(
        body, name="mix", grid=(S // tm,),
        in_specs=[row(AOW)] + [_e_spec(c, tm) for c in range(5)] + [_e_prev_spec(1, tm), _e_prev_spec(2, tm),
                  _const_spec((8, D)), _vec_spec(), _vec_spec(), _const_spec((D, AOW)), _const_spec((D, D))],
        out_specs=[row(AOW), row(D), row(D), row(D), row(D)],
        out_shape=[jax.ShapeDtypeStruct((S, AOW), BF16)] + [jax.ShapeDtypeStruct((S, D), BF16)] * 4,
        compiler_params=_cparams("parallel"),
    )(o_attn, e, e, e, e, e, e, e, cw8, ba, bb, w_bat, w_bc)


def _out_proj(merged, w_out, x, gate1, g_mlp, sc2, sh2):
    S = x.shape[0]
    tm = TM

    def body(mg_ref, w_ref, x_ref, gt_ref, g_ref, sc_ref, sh_ref, x1_ref, mo_ref, h2_ref):
        mo = _nn(mg_ref[...], w_ref[...])
        mo_ref[...] = mo.astype(BF16)
        x1 = x_ref[...] + gt_ref[...] * mo
        x1_ref[...] = x1
        h2 = x1 * _rms_r(x1) * g_ref[...] * (1.0 + sc_ref[...]) + sh_ref[...]
        h2_ref[...] = h2.astype(BF16)

    row = pl.BlockSpec((tm, D), lambda i: (i, 0))
    return pl.pallas_call(
        body, name="out_proj", grid=(S // tm,),
        in_specs=[row, _const_spec((D, D)), row, _vec_spec(), _vec_spec(), _vec_spec(), _vec_spec()],
        out_specs=[row, row, row],
        out_shape=[jax.ShapeDtypeStruct((S, D), F32), jax.ShapeDtypeStruct((S, D), BF16), jax.ShapeDtypeStruct((S, D), BF16)],
        compiler_params=_cparams("parallel"),
    )(merged, w_out, x, gate1, g_mlp, sc2, sh2)


def _mlp_in(h2, w_mit):
    S = h2.shape[0]
    tm, tn = TM, 2048

    def body(h_ref, w_ref, a_ref, f_ref):
        a = _nt(h_ref[...], w_ref[...])
        a_ref[...] = a.astype(BF16)
        f_ref[...] = jnp.square(jnp.maximum(a, 0.0)).astype(BF16)

    blk = pl.BlockSpec((tm, tn), lambda i, j: (i, j))
    return pl.pallas_call(
        body, name="mlp_in", grid=(S // tm, DFF // tn),
        in_specs=[pl.BlockSpec((tm, D), lambda i, j: (i, 0)), pl.BlockSpec((tn, D), lambda i, j: (j, 0))],
        out_specs=[blk, blk],
        out_shape=[jax.ShapeDtypeStruct((S, DFF), BF16)] * 2,
        compiler_params=_cparams("parallel", "parallel"),
    )(h2, w_mit)


def _mlp_out(f, w_mo, x1, gate2, g_fin, tgt):
    S = x1.shape[0]
    tm = 512
    half = tm // 2

    def body(f_ref, w_ref, x1_ref, gt_ref, g_ref, t_ref, mlp_ref, dx2_ref, pv_ref):
        @pl.when(pl.program_id(0) == 0)
        def _():
            pv_ref[...] = jnp.zeros_like(pv_ref)

        g = g_ref[...]
        for hs in (pl.ds(0, half), pl.ds(half, half)):
            mlp = _nn(f_ref[hs, :], w_ref[...])
            mlp_ref[hs, :] = mlp.astype(BF16)
            x2 = x1_ref[hs, :] + gt_ref[...] * mlp
            r = _rms_r(x2)
            err = x2 * r * g - t_ref[hs, :]
            dx2, pg = _rms_bwd(x2, r, g, err * (1.0 / D))
            dx2_ref[hs, :] = dx2
            pv_ref[0:1, :] += _rowsum(pg)
            pv_ref[1:2, :] += 0.5 * _rowsum(jnp.mean(err * err, axis=-1, keepdims=True))

    row = pl.BlockSpec((tm, D), lambda i: (i, 0))
    return pl.pallas_call(
        body, name="mlp_out", grid=(S // tm,),
        in_specs=[pl.BlockSpec((tm, DFF), lambda i: (i, 0)), _const_spec((DFF, D)), row, _vec_spec(), _vec_spec(), row],
        out_specs=[row, row, _const_spec((8, D))],
        out_shape=[jax.ShapeDtypeStruct((S, D), BF16), jax.ShapeDtypeStruct((S, D), F32), jax.ShapeDtypeStruct((8, D), F32)],
        compiler_params=_cparams("arbitrary"),
    )(f, w_mo, x1, gate2, g_fin, tgt)


def _bwd_mlp_a(dx2, gate2, mlp, w_mo, a):
    S = dx2.shape[0]
    tm, tn = TM, 2048

    def body(dx_ref, gt_ref, mlp_ref, w_ref, a_ref, da_ref, dmo_ref, pv_ref):
        i, j = pl.program_id(0), pl.program_id(1)

        @pl.when((i == 0) & (j == 0))
        def _():
            pv_ref[...] = jnp.zeros_like(pv_ref)

        @pl.when(j == 0)
        def _():
            dx = dx_ref[...]
            dmo_ref[...] = (dx * gt_ref[...]).astype(BF16)
            pv_ref[0:1, :] += _rowsum(dx * mlp_ref[...].astype(F32))

        df = _nt(dmo_ref[...], w_ref[...])
        da_ref[...] = (df * (2.0 * jnp.maximum(a_ref[...].astype(F32), 0.0))).astype(BF16)

    row = pl.BlockSpec((tm, D), lambda i, j: (i, 0))
    blk = pl.BlockSpec((tm, tn), lambda i, j: (i, j))
    return pl.pallas_call(
        body, name="bwd_mlp_a", grid=(S // tm, DFF // tn),
        in_specs=[row, _vec_spec(), row, pl.BlockSpec((tn, D), lambda i, j: (j, 0)), blk],
        out_specs=[blk, row, _const_spec((8, D))],
        out_shape=[jax.ShapeDtypeStruct((S, DFF), BF16), jax.ShapeDtypeStruct((S, D), BF16), jax.ShapeDtypeStruct((8, D), F32)],
        compiler_params=_cparams("arbitrary", "arbitrary"),
    )(dx2, gate2, mlp, w_mo, a)


def _bwd_mlp_b(da, w_mit, x1, dx2, g_mlp, sc2):
    S = x1.shape[0]
    tm = 512
    half = tm // 2

    def body(da_ref, w_ref, x1_ref, dx2_ref, g_ref, sc_ref, dx1_ref, pv_ref):
        @pl.when(pl.program_id(0) == 0)
        def _():
            pv_ref[...] = jnp.zeros_like(pv_ref)

        g = g_ref[...]
        for hs in (pl.ds(0, half), pl.ds(half, half)):
            dh = _nn(da_ref[hs, :], w_ref[...])
            x1 = x1_ref[hs, :]
            r = _rms_r(x1)
            dxn, pg = _rms_bwd(x1, r, g, dh * (1.0 + sc_ref[...]))
            dx1_ref[hs, :] = dx2_ref[hs, :] + dxn
            pv_ref[0:1, :] += _rowsum(dh)
            pv_ref[1:2, :] += _rowsum(dh * (x1 * r * g))
            pv_ref[2:3, :] += _rowsum(pg)

    row = pl.BlockSpec((tm, D), lambda i: (i, 0))
    return pl.pallas_call(
        body, name="bwd_mlp_b", grid=(S // tm,),
        in_specs=[pl.BlockSpec((tm, DFF), lambda i: (i, 0)), _const_spec((DFF, D)), row, row, _vec_spec(), _vec_spec()],
        out_specs=[row, _const_spec((8, D))],
        out_shape=[jax.ShapeDtypeStruct((S, D), F32), jax.ShapeDtypeStruct((8, D), F32)],
        compiler_params=_cparams("arbitrary"),
    )(da, w_mit, x1, dx2, g_mlp, sc2)


def _bwd_mix(dx1, gate1, mo, e, cw8, ba, bb, ya, yc, o_attn, w_out, w_bc, w_bat):
    S = dx1.shape[0]
    tm = 256
    n_tiles = S // tm

    def body(dx_ref, dxn_ref, gt_ref, mo_ref, cb_ref, cc_ref, cx_ref, ga_ref, gb_ref, cbn_ref, gbn_ref, ccp_ref, cxp_ref,
             cw_ref, ba_ref, bb_ref, ya_ref, yc_ref, o_ref, wout_ref, wbc_ref, wba_ref,
             dmo_ref, dya_ref, dyc_ref, do_ref, dl_ref, de_ref, pv_ref):
        i = pl.program_id(0)

        @pl.when(i == 0)
        def _():
            pv_ref[...] = jnp.zeros_like(pv_ref)

        gate = gt_ref[...]
        bbv = bb_ref[...]

        def conv_branch_grad(dx_rows, gb_rows):
            dmo = (dx_rows * gate).astype(BF16)
            dmg = _nt(dmo, wout_ref[...])
            sb = _sigmoid(gb_rows + bbv)
            dyc = dmg * sb
            return dmo, dmg, sb, dyc, _nt(dyc.astype(BF16), wbc_ref[...])

        dx = dx_ref[...]
        cb = cb_ref[0].astype(F32)
        cc = cc_ref[0].astype(F32)
        cx = cx_ref[0].astype(F32)
        dmo, dmg, sb, dyc, dcbu = conv_branch_grad(dx, gb_ref[0].astype(F32))
        dmo_ref[...] = dmo
        pv_ref[0:1, :] += _rowsum(dx * mo_ref[...].astype(F32))
        sa = _sigmoid(ga_ref[0].astype(F32) + ba_ref[...])
        dya = (dmg * sa).astype(BF16)
        dya_ref[...] = dya
        dyc_ref[...] = dyc.astype(BF16)
        dga = dmg * ya_ref[...].astype(F32) * sa * (1.0 - sa)
        dgb = dmg * yc_ref[...].astype(F32) * sb * (1.0 - sb)
        pv_ref[1:2, :] += _rowsum(dga)
        pv_ref[2:3, :] += _rowsum(dgb)

        do = _nn(dya, wba_ref[...])
        do_ref[...] = do
        prod = do * o_ref[...]
        dl_ref[...] = jnp.concatenate(
            [jnp.broadcast_to(jnp.sum(prod[:, s * HEAD:(s + 1) * HEAD], axis=-1, keepdims=True), (tm, HEAD))
             for s in range(N_SLOT)], axis=1)

        z = cc * cx
        zp = ccp_ref[0].astype(F32) * cxp_ref[0].astype(F32) * (i > 0).astype(F32)
        z1 = _shift_down(z, 1, [zp[15:16]])
        z2 = _shift_down(z, 2, [zp[14:15], zp[15:16]])
        cw = cw_ref[...]
        u = cw[0:1] * z2 + cw[1:2] * z1 + cw[2:3] * z
        du = dcbu * cb
        dcbu_n = conv_branch_grad(dxn_ref[...], gbn_ref[0].astype(F32))[4]
        du_n = dcbu_n * cbn_ref[0].astype(F32) * (i < n_tiles - 1).astype(F32)
        du1 = _shift_up(du, 1, [du_n[0:1]])
        du2 = _shift_up(du, 2, [du_n[0:1], du_n[1:2]])
        dz = cw[2:3] * du + cw[1:2] * du1 + cw[0:1] * du2
        pv_ref[3:4, :] += _rowsum(du * z2)
        pv_ref[4:5, :] += _rowsum(du * z1)
        pv_ref[5:6, :] += _rowsum(du * z)

        de_ref[0] = (dcbu * u).astype(BF16)
        de_ref[1] = (dz * cx).astype(BF16)
        de_ref[2] = (dz * cc).astype(BF16)
        de_ref[3] = dga.astype(BF16)
        de_ref[4] = dgb.astype(BF16)

    row = lambda w: pl.BlockSpec((tm, w), lambda i: (i, 0))
    nxt = pl.BlockSpec((16, D), lambda i: (jnp.minimum((i + 1) * (tm // 16), S // 16 - 1), 0))
    return pl.pallas_call(
        body, name="bwd_mix", grid=(n_tiles,),
        in_specs=[row(D), nxt, _vec_spec(), row(D)] + [_e_spec(c, tm) for c in range(5)]
                 + [_e_next_spec(0, tm, S), _e_next_spec(4, tm, S), _e_prev_spec(1, tm), _e_prev_spec(2, tm),
                    _const_spec((8, D)), _vec_spec(), _vec_spec(), row(D), row(D), row(AOW),
                    _const_spec((D, D)), _const_spec((D, D)), _const_spec((D, AOW))],
        out_specs=[row(D), row(D), row(D), row(AOW), row(AOW), pl.BlockSpec((5, tm, D), lambda i: (0, i, 0)),
                   _const_spec((8, D))],
        out_shape=[jax.ShapeDtypeStruct((S, D), BF16)] * 3 + [jax.ShapeDtypeStruct((S, AOW), F32)] * 2
                  + [jax.ShapeDtypeStruct((5, S, D), BF16), jax.ShapeDtypeStruct((8, D), F32)],
        compiler_params=_cparams("arbitrary"),
    )(dx1, dx1, gate1, mo, e, e, e, e, e, e, e, e, e, cw8, ba, bb, ya, yc, o_attn, w_out, w_bc, w_bat)


def _attn_bwd(qkv, do, lse, dl, bias_t):
    S = qkv.shape[2]
    nblk = S // HEAD

    def body(qkv_ref, do_ref, lse_ref, dl_ref, b_ref, d_ref):
        g = pl.program_id(1)
        bias = b_ref[0, 0]
        col = lax.broadcasted_iota(jnp.int32, bias.shape, 1)
        bias_last = jnp.where(col >= HEAD, NEG, bias)
        eye = (lax.broadcasted_iota(jnp.int32, (HEAD, HEAD), 0) == lax.broadcasted_iota(jnp.int32, (HEAD, HEAD), 1)).astype(F32)

        def as_row(t):
            return jnp.sum(t * eye, axis=0, keepdims=True)

        for gi, d in enumerate(DILATIONS):
            @pl.when(g == gi)
            def _(d=d):
                nb = nblk // d

                def step(b, dq_part):
                    r, n = b // nb, b % nb
                    cur = pl.ds(n * (HEAD * d) + r, HEAD, stride=d)
                    nxt = pl.ds(jnp.minimum(n + 1, nb - 1) * (HEAD * d) + r, HEAD, stride=d)
                    two = lambda ref: jnp.concatenate([ref[cur, :], ref[nxt, :]], axis=0)
                    two_rows = lambda ref: jnp.concatenate([as_row(ref[cur, :]), as_row(ref[nxt, :])], axis=1)
                    q2 = two(qkv_ref.at[0, 0]).astype(BF16)
                    do2 = two(do_ref).astype(BF16)
                    k = qkv_ref.at[0, 1][cur, :].astype(BF16)
                    v = qkv_ref.at[0, 2][cur, :].astype(BF16)
                    s = _nt(k, q2) * SCALE + jnp.where(n < nb - 1, bias, bias_last)
                    p = jnp.exp(s - two_rows(lse_ref))
                    d_ref.at[0, 2][cur, :] = _nn(p.astype(BF16), do2)
                    dp = _nt(v, do2)
                    ds = (p * (dp - two_rows(dl_ref)) * SCALE).astype(BF16)
                    d_ref.at[0, 1][cur, :] = _nn(ds, q2)
                    dq2 = _tn(ds, k)
                    d_ref.at[0, 0][cur, :] = dq2[:HEAD] + jnp.where(n > 0, dq_part, 0.0)
                    return dq2[HEAD:]

                def steps(i, dq_part):
                    for u in range(UNROLL):
                        dq_part = step(i * UNROLL + u, dq_part)
                    return dq_part

                lax.fori_loop(0, nblk // UNROLL, steps, jnp.zeros((HEAD, HEAD), F32))

    col_blk = pl.BlockSpec((S, HEAD), lambda j, g: (0, j))
    qkv_blk = pl.BlockSpec((1, 3, S, HEAD), lambda j, g: (g, 0, 0, j))
    return pl.pallas_call(
        body, name="attn_bwd", grid=(N_SLOT, 3),
        in_specs=[qkv_blk, col_blk, col_blk, col_blk, pl.BlockSpec((1, 1, HEAD, 2 * HEAD), lambda j, g: (g, j, 0, 0))],
        out_specs=qkv_blk,
        out_shape=jax.ShapeDtypeStruct((3, 3, S, AOW), F32),
        compiler_params=_cparams("parallel", "arbitrary"),
    )(qkv, do, lse, dl, bias_t)


def _bwd_in(dqkv, de, w_int, x, dx1, g_mix, sc1):
    S = x.shape[0]
    tm = TM
    dqkv = dqkv.reshape(3, 3, S, AOW)

    def body(dq_ref, de_ref, wq_ref, wk_ref, wv_ref, wa_ref, wb_ref, x_ref, dx1_ref, g_ref, sc_ref, gx_ref, pv_ref):
        acc = gx_ref
        i, k = pl.program_id(0), pl.program_id(1)

        @pl.when((i == 0) & (k == 0))
        def _():
            pv_ref[...] = jnp.zeros_like(pv_ref)

        @pl.when(k == 0)
        def _():
            acc[...] = jnp.zeros_like(acc)

        @pl.when(k < 3)
        def _():
            lhs = jnp.concatenate([dq_ref[0, t].astype(BF16) for t in range(3)], axis=1)
            acc[...] += _nn(lhs, jnp.concatenate([wq_ref[...], wk_ref[...], wv_ref[...]], axis=0))

        @pl.when(k >= 3)
        def _():
            acc[...] += _nn(de_ref[0], jnp.concatenate([wa_ref[...], wb_ref[...]], axis=0))

        @pl.when(k == 7)
        def _():
            dh = acc[...]
            xv = x_ref[...]
            r = _rms_r(xv)
            g = g_ref[...]
            dxn, pg = _rms_bwd(xv, r, g, dh * (1.0 + sc_ref[...]))
            gx_ref[...] = dx1_ref[...] + dxn
            pv_ref[0:1, :] += _rowsum(dh)
            pv_ref[1:2, :] += _rowsum(dh * (xv * r * g))
            pv_ref[2:3, :] += _rowsum(pg)

    grp = lambda k: jnp.minimum(k, 2)
    chunk = lambda k: jnp.maximum(k - 3, 0)
    wblk = lambda f: pl.BlockSpec((512, D), lambda i, k: (f(k), 0))
    row = pl.BlockSpec((tm, D), lambda i, k: (i, 0))
    once = pl.BlockSpec((tm, D), lambda i, k: (i, 0), pipeline_mode=pl.Buffered(1))
    return pl.pallas_call(
        body, name="bwd_in", grid=(S // tm, 8),
        in_specs=[pl.BlockSpec((1, 3, tm, 512), lambda i, k: (grp(k), 0, i, 0)),
                  pl.BlockSpec((1, tm, D), lambda i, k: (chunk(k), i, 0)),
                  wblk(grp), wblk(lambda k: 3 + grp(k)), wblk(lambda k: 6 + grp(k)),
                  wblk(lambda k: 9 + 2 * chunk(k)), wblk(lambda k: 10 + 2 * chunk(k)),
                  once, once, _vec_spec(), _vec_spec()],
        out_specs=[row, _const_spec((8, D))],
        out_shape=[jax.ShapeDtypeStruct((S, D), F32), jax.ShapeDtypeStruct((8, D), F32)],
        compiler_params=_cparams("arbitrary", "arbitrary"),
    )(dqkv, de, w_int, w_int, w_int, w_int, w_int, x, dx1, g_mix, sc1)


def _grad_w(name, a, b):
    S, ka = a.shape
    nb = b.shape[1]

    def body(a_ref, b_ref, o_ref):
        o_ref[...] = _tn(a_ref[...], b_ref[...]).astype(BF16)

    return pl.pallas_call(
        body, name=name, grid=(ka // 512,),
        in_specs=[pl.BlockSpec((S, 512), lambda n: (0, n)), pl.BlockSpec((S, nb), lambda n: (0, 0))],
        out_specs=pl.BlockSpec((512, nb), lambda n: (n, 0)),
        out_shape=jax.ShapeDtypeStruct((ka, nb), BF16),
        compiler_params=_cparams("parallel"),
    )(a, b)


def _grad_w_in(dqkv, de, h):
    S = h.shape[0]

    def body(dq_ref, de_ref, h_ref, o_ref):
        n = pl.program_id(0)

        @pl.when(n < 9)
        def _():
            o_ref[...] = _tn(dq_ref[0].astype(BF16), h_ref[...]).astype(BF16)

        @pl.when(n >= 9)
        def _():
            o_ref[...] = _tn(de_ref[0], h_ref[...]).astype(BF16)

    def e_idx(n):
        kk = jnp.maximum(n - 9, 0)
        return (kk // 2, 0, kk % 2)

    return pl.pallas_call(
        body, name="grad_w_in", grid=(19,),
        in_specs=[pl.BlockSpec((1, S, 512), lambda n: (jnp.minimum(n, 8), 0, 0)), pl.BlockSpec((1, S, 512), e_idx),
                  pl.BlockSpec((S, D), lambda n: (0, 0))],
        out_specs=pl.BlockSpec((512, D), lambda n: (_win_rowblock(n), 0)),
        out_shape=jax.ShapeDtypeStruct((19 * 512, D), BF16),
        compiler_params=_cparams("parallel"),
    )(dqkv, de, h)


def _local_step(x, tgt, mod, g_mix, g_mlp, g_fin, ba, bb, cw8, w_int, mix_weights, mlp_weights, mlp_grads_ready, other_grads_ready):
    S = x.shape[0]
    sh1, sc1, gt1, sh2, sc2, gt2 = [mod[k:k + 1] for k in range(6)]
    bias, bias_t = _bias_table()

    h, qkv, e = _proj(x, g_mix, sc1, sh1, w_int)
    qkv = qkv.reshape(3, 3, S, AOW)
    o_attn, lse = _attn_fwd(qkv, bias)
    w_bat, w_bc, w_out = mix_weights(o_attn)
    o_bf, cbu, ya, yc, merged = _mix(o_attn, e, cw8, ba, bb, w_bat, w_bc)
    x1, mo, h2 = _out_proj(merged, w_out, x, gt1, g_mlp, sc2, sh2)
    w_mit, w_mo = mlp_weights(x1)
    a, f = _mlp_in(h2, w_mit)
    mlp, dx2, pv_f = _mlp_out(f, w_mo, x1, gt2, g_fin, tgt)

    da, dmo2, pv_a = _bwd_mlp_a(dx2, gt2, mlp, w_mo, a)
    dx1, pv_b = _bwd_mlp_b(da, w_mit, x1, dx2, g_mlp, sc2)
    zero = mlp_grads_ready(_grad_w("grad_w_mi", da, h2), _grad_w("grad_w_mo", f, dmo2))
    dmo, dya, dyc, do, dl, de, pv_m = _bwd_mix(dx1, gt1 + zero, mo, e, cw8, ba, bb, ya, yc, o_attn, w_out, w_bc, w_bat)
    dqkv = _attn_bwd(qkv, do, lse, dl, bias_t).reshape(9, S, AOW)
    zero = other_grads_ready(_grad_w_in(dqkv, de, h), _grad_w("grad_w_ba", dya, o_bf), _grad_w("grad_w_bc", cbu, dyc),
                             _grad_w("grad_w_out", merged, dmo))
    grad_x, pv_i = _bwd_in(dqkv, de, w_int, x, dx1, g_mix, sc1 + zero)

    vec = jnp.concatenate([pv_i[0:2], pv_m[0:1], pv_b[0:2], pv_a[0:1], pv_i[2:3], pv_b[2:3], pv_f[0:1],
                           pv_m[1:3], pv_m[3:6], pv_f[1:2], jnp.zeros((1, D), F32)], axis=0)
    return grad_x, vec


def _my_place():
    return lax.axis_index("x"), lax.axis_index("y"), lax.axis_index("c")


def _dev_index(px, py, pc):
    return 4 * px + 2 * py + pc


def _allgather_weights(shards):
    nw = len(shards)
    HBM = pl.BlockSpec(memory_space=pl.ANY)

    def body(*refs):
        sh, full = refs[:nw], refs[nw:2 * nw]
        send_sems, recv_sems, local_sems = refs[2 * nw:]
        x, y, c = _my_place()
        me, sibling = (x, y, c), (x, y, 1 - c)
        chips = [(1 - x, y), (x, 1 - y), (1 - x, 1 - y)]

        def rows(w, px, py, pc):
            r = sh[w].shape[0]
            return full[w].at[pl.ds(pl.multiple_of(_dev_index(px, py, pc) * r, 16), r), :]

        def copy(w, k, block, to, src=None):
            return pltpu.make_async_remote_copy(
                src_ref=rows(w, *block) if src is None else src, dst_ref=rows(w, *block),
                send_sem=send_sems.at[w, k], recv_sem=recv_sems.at[w, k], device_id=to, device_id_type=MESH)

        mine = [pltpu.make_async_copy(sh[w], rows(w, *me), local_sems.at[w]) for w in range(nw)]
        for cp in mine:
            cp.start()
        first = []
        for w in range(nw):
            first.append(copy(w, 0, me, sibling, src=sh[w]))
            first += [copy(w, 1 + j, me, (*chip, c), src=sh[w]) for j, chip in enumerate(chips)]
        for cp in first:
            cp.start()
        passed = []
        for w in range(nw):
            for j, chip in enumerate(chips):
                copy(w, 1 + j, (*chip, c), me).wait_recv()
                fwd = copy(w, 4 + j, (*chip, c), sibling)
                fwd.start()
                passed.append(fwd)
        for w in range(nw):
            copy(w, 0, sibling, me).wait_recv()
            for j, chip in enumerate(chips):
                copy(w, 4 + j, (*chip, 1 - c), me).wait_recv()
        for cp in first + passed:
            cp.wait_send()
        for cp in mine:
            cp.wait()

    return pl.pallas_call(
        body, name="allgather_weights",
        out_shape=[jax.ShapeDtypeStruct((N_DEV * s.shape[0], s.shape[1]), s.dtype) for s in shards],
        in_specs=[HBM] * nw, out_specs=[HBM] * nw,
        scratch_shapes=[pltpu.SemaphoreType.DMA((nw, 7)), pltpu.SemaphoreType.DMA((nw, 7)), pltpu.SemaphoreType.DMA((nw,))],
    )(*shards)


def _peer(x, y, c, m):
    return (x ^ ((m >> 2) & 1), y ^ ((m >> 1) & 1), c ^ (m & 1))


HBM_SPEC = pl.BlockSpec(memory_space=pltpu.HBM)
SEM_SPEC = pl.BlockSpec(memory_space=pltpu.SEMAPHORE)
N_PEER = N_DEV - 1


SPLIT_MASKS = {"gather": tuple(range(1, N_DEV)), "scatter": tuple(range(1, N_DEV)), "chips": (2, 4, 6)}


def _split_copy(mode, src_ref, land_ref, send_sems, recv_sems, w, j, place, arriving=False):
    x, y, c = place
    masks = SPLIT_MASKS[mode]
    peer = _peer(x, y, c, masks[j])
    k = w * len(masks) + j
    sender, receiver = ((peer, (x, y, c)) if arriving else ((x, y, c), peer))
    if mode == "gather":
        r = src_ref.shape[0]
        src, dst = src_ref, land_ref.at[pl.ds(pl.multiple_of(_dev_index(*sender) * r, 16), r), :]
    elif mode == "scatter":
        r = land_ref.shape[1]
        src, dst = src_ref.at[pl.ds(pl.multiple_of(_dev_index(*receiver) * r, 16), r), :], land_ref.at[j]
    else:
        src, dst = src_ref.at[2 * receiver[0] + receiver[1]], land_ref.at[j]
    return pltpu.make_async_remote_copy(src_ref=src, dst_ref=dst, send_sem=send_sems.at[k], recv_sem=recv_sems.at[k],
                                        device_id=peer, device_id_type=MESH)


def _split_start(name, mode, srcs, lands):
    n = len(srcs)
    nm = len(SPLIT_MASKS[mode])

    def body(*refs):
        src, land = refs[:n], refs[n:2 * n]
        send_sems, recv_sems = refs[2 * n], refs[2 * n + 1]
        token = refs[-1]
        place = _my_place()
        for w in range(n):
            for j in range(nm):
                _split_copy(mode, src[w], land[w], send_sems, recv_sems, w, j, place).start()
        token[...] = jnp.zeros_like(token)

    hbm = lambda t: pltpu.HBM(t.shape, t.dtype)
    out = pl.pallas_call(
        body, name=name,
        out_shape=(pltpu.SemaphoreType.DMA((n * nm,)), pltpu.SemaphoreType.DMA((n * nm,)), *[hbm(t) for t in srcs],
                   *[hbm(t) for t in lands], jax.ShapeDtypeStruct((8, 128), F32)),
        in_specs=(HBM_SPEC,) * (2 * n),
        out_specs=(SEM_SPEC, SEM_SPEC) + (HBM_SPEC,) * (2 * n) + (pl.BlockSpec(memory_space=pltpu.VMEM),),
        input_output_aliases={i: 2 + i for i in range(2 * n)},
        compiler_params=pltpu.CompilerParams(has_side_effects=pltpu.SideEffectType.DATAFLOW_SIDE_EFFECTING),
    )(*[pltpu.with_memory_space_constraint(t, pltpu.HBM) for t in (*srcs, *lands)])
    return out[0], out[1], out[2:2 + n], out[2 + n:2 + 2 * n], out[-1][0:1, 0:1]


def _split_wait(name, mode, send_sems, recv_sems, srcs, lands, after):
    n = len(srcs)

    def body(*refs):
        src, land = refs[:n], refs[n:2 * n]
        ssem, rsem = refs[2 * n], refs[2 * n + 1]
        place = _my_place()
        for w in range(n):
            for j in range(len(SPLIT_MASKS[mode])):
                _split_copy(mode, src[w], land[w], ssem, rsem, w, j, place).wait_send()
                _split_copy(mode, src[w], land[w], ssem, rsem, w, j, place, arriving=True).wait_recv()

    hbm = lambda t: pltpu.HBM(t.shape, t.dtype)
    out = pl.pallas_call(
        body, name=name,
        out_shape=tuple(hbm(t) for t in (*srcs, *lands)),
        in_specs=(HBM_SPEC,) * (2 * n) + (SEM_SPEC, SEM_SPEC, pl.BlockSpec(memory_space=pl.ANY)),
        out_specs=(HBM_SPEC,) * (2 * n),
        input_output_aliases={i: i for i in range(2 * n)},
        compiler_params=pltpu.CompilerParams(has_side_effects=pltpu.SideEffectType.DATAFLOW_SIDE_EFFECTING),
    )(*srcs, *lands, send_sems, recv_sems, after)
    return out[:n], out[n:]


def _sibling_exchange(grads):
    nw = len(grads)
    HBM = pl.BlockSpec(memory_space=pl.ANY)

    def body(*refs):
        g, land = refs[:nw], refs[nw:2 * nw]
        send_sems, recv_sems = refs[2 * nw:]
        x, y, c = _my_place()

        def copy(w, q, owner_core):
            r = land[w].shape[1]
            return pltpu.make_async_remote_copy(
                src_ref=g[w].at[pl.ds(pl.multiple_of((2 * q + owner_core) * r, 16), r), :], dst_ref=land[w].at[q],
                send_sem=send_sems.at[w, q], recv_sem=recv_sems.at[w, q], device_id=(x, y, 1 - c), device_id_type=MESH)

        sends = [copy(w, q, 1 - c) for w in range(nw) for q in range(4)]
        for cp in sends:
            cp.start()
        for w in range(nw):
            for q in range(4):
                copy(w, q, c).wait_recv()
        for cp in sends:
            cp.wait_send()

    return pl.pallas_call(
        body, name="sibling_exchange",
        out_shape=[jax.ShapeDtypeStruct((4, a.shape[0] // N_DEV, a.shape[1]), a.dtype) for a in grads],
        in_specs=[HBM] * nw, out_specs=[HBM] * nw,
        scratch_shapes=[pltpu.SemaphoreType.DMA((nw, 4)), pltpu.SemaphoreType.DMA((nw, 4))],
    )(*grads)


def _pair_sum(g, sib, core, name):
    _, r, ccols = sib.shape
    tr = _row_tile(r)

    def body(core_ref, g_ref, s_ref, o_ref):
        o_ref[0] = (g_ref[0, 0].astype(F32) + s_ref[0].astype(F32)).astype(BF16)

    return pl.pallas_call(
        body, name=name,
        grid_spec=pltpu.PrefetchScalarGridSpec(
            num_scalar_prefetch=1, grid=(4, r // tr),
            in_specs=[pl.BlockSpec((1, 1, tr, ccols), lambda q, i, core_ref: (q, core_ref[0], i, 0)),
                      pl.BlockSpec((1, tr, ccols), lambda q, i, core_ref: (q, i, 0))],
            out_specs=pl.BlockSpec((1, tr, ccols), lambda q, i, core_ref: (q, i, 0))),
        out_shape=jax.ShapeDtypeStruct(sib.shape, BF16),
        compiler_params=_cparams("parallel", "parallel"),
    )(core, g.reshape(4, 2, r, ccols), sib)


def _allgather_small(v, name):
    r, ccols = v.shape

    def body(v_ref, out_ref, send_sems, recv_sems):
        x, y, c = _my_place()
        my_idx = _dev_index(x, y, c)
        out_ref[my_idx] = v_ref[...]

        def copy(m):
            peer = _peer(x, y, c, m)
            return pltpu.make_async_remote_copy(
                src_ref=v_ref, dst_ref=out_ref.at[my_idx],
                send_sem=send_sems.at[m - 1], recv_sem=recv_sems.at[m - 1], device_id=peer, device_id_type=MESH)

        def arrival(m):
            peer = _peer(x, y, c, m)
            return pltpu.make_async_remote_copy(
                src_ref=v_ref, dst_ref=out_ref.at[_dev_index(*peer)],
                send_sem=send_sems.at[m - 1], recv_sem=recv_sems.at[m - 1], device_id=peer, device_id_type=MESH)

        sends = [copy(m) for m in range(1, N_DEV)]
        for cp in sends:
            cp.start()
        for m in range(1, N_DEV):
            arrival(m).wait_recv()
        for cp in sends:
            cp.wait_send()

    return pl.pallas_call(
        body, name=name,
        out_shape=jax.ShapeDtypeStruct((N_DEV, r, ccols), v.dtype),
        in_specs=[pl.BlockSpec(memory_space=pltpu.VMEM)], out_specs=pl.BlockSpec(memory_space=pltpu.VMEM),
        scratch_shapes=[pltpu.SemaphoreType.DMA((7,)), pltpu.SemaphoreType.DMA((7,))],
    )(v)


def _ada_fwd(c_all, w_ada, b_cols):
    def body(c_ref, w_ref, b_ref, mod_ref, act_ref):
        cv = c_ref[...]
        act = cv * _sigmoid(cv)
        act_ref[...] = act
        mod_ref[...] = jnp.dot(act, w_ref[...], preferred_element_type=F32, precision=lax.Precision.HIGHEST) + b_ref[...]

    return pl.pallas_call(
        body, name="ada_fwd",
        out_shape=[jax.ShapeDtypeStruct((N_DEV, w_ada.shape[1]), F32), jax.ShapeDtypeStruct((N_DEV, D), F32)],
        compiler_params=_cparams(),
    )(c_all, w_ada, b_cols)


def _ada_bwd(act_t, gm_cols):
    def body(a_ref, g_ref, o_ref):
        o_ref[...] = jnp.dot(a_ref[...], g_ref[...], preferred_element_type=F32, precision=lax.Precision.HIGHEST)

    return pl.pallas_call(
        body, name="ada_bwd", out_shape=jax.ShapeDtypeStruct((D, gm_cols.shape[1]), F32), compiler_params=_cparams(),
    )(act_t, gm_cols)


def _row_tile(r):
    for t in (256, 304, 128, 64, 16):
        if r % t == 0:
            return t
    return r


def _sum_parts(parts, name, own=None):
    k, r, ccols = parts.shape
    tr = _row_tile(r)

    def body(*refs):
        p_ref, o_ref = refs[0], refs[-1]
        acc = p_ref[0].astype(F32) if own is None else refs[1][...].astype(F32) + p_ref[0].astype(F32)
        for s in range(1, k):
            acc = acc + p_ref[s].astype(F32)
        o_ref[...] = acc

    blk = pl.BlockSpec((tr, ccols), lambda i: (i, 0))
    return pl.pallas_call(
        body, name=name, grid=(r // tr,),
        in_specs=[pl.BlockSpec((k, tr, ccols), lambda i: (0, i, 0))] + ([] if own is None else [blk]),
        out_specs=blk,
        out_shape=jax.ShapeDtypeStruct((r, ccols), F32),
        compiler_params=_cparams("parallel"),
    )(*((parts,) if own is None else (parts, own)))


def _adamw(w, g, m, v, name):
    r, ccols = w.shape
    tr = _row_tile(r)
    c1 = 1.0 / (1.0 - B1 ** STEP)
    c2 = 1.0 / (1.0 - B2 ** STEP)

    def body(w_ref, g_ref, m_ref, v_ref, d_ref, nm_ref, nv_ref):
        gv = g_ref[...]
        nm = B1 * m_ref[...] + (1.0 - B1) * gv
        nv = B2 * v_ref[...] + (1.0 - B2) * jnp.square(gv)
        nm_ref[...] = nm
        nv_ref[...] = nv
        d_ref[...] = -LR * ((nm * c1) / (jnp.sqrt(nv * c2) + ADAM_EPS) + WD * w_ref[...])

    blk = pl.BlockSpec((tr, ccols), lambda i: (i, 0))
    return pl.pallas_call(
        body, name=name, grid=(r // tr,), in_specs=[blk] * 4, out_specs=[blk] * 3,
        out_shape=[jax.ShapeDtypeStruct((r, ccols), F32)] * 3,
        compiler_params=_cparams("parallel"),
    )(w, g, m, v)


def _pack_vectors(b_ada, g_mix, g_mlp, g_fin, b_gate, conv_w):
    conv_rows = jnp.pad(conv_w.reshape(3, HEAD), ((0, 0), (0, D - HEAD)))
    return jnp.concatenate([b_ada.reshape(6, D), g_mix.reshape(1, D), g_mlp.reshape(1, D), g_fin.reshape(1, D),
                            b_gate.reshape(2, D), conv_rows, jnp.zeros((2, D), F32)], axis=0)


def _unpack_vectors(p):
    return (p[0:6].reshape(1, 6 * D), p[6:7], p[9:11].reshape(1, 2 * D), p[11:14, :HEAD].reshape(1, 3, HEAD),
            p[7:8], p[8])


def kernel(x, c, w_ada, b_ada, g_norm_mix, w_in, b_gate, conv_w, w_branch_attn, w_branch_conv, w_out, g_norm_mlp, w_mlp_in, w_mlp_out, g_norm_final, loss_target, m_w_ada, m_b_ada, m_g_norm_mix, m_w_in, m_b_gate, m_conv_w, m_w_branch_attn, m_w_branch_conv, m_w_out, m_g_norm_mlp, m_w_mlp_in, m_w_mlp_out, m_g_norm_final, v_w_ada, v_b_ada, v_g_norm_mix, v_w_in, v_b_gate, v_conv_w, v_w_branch_attn, v_w_branch_conv, v_w_out, v_g_norm_mlp, v_w_mlp_in, v_w_mlp_out, v_g_norm_final):
    S = x.shape[1]
    xi, yi, ci = _my_place()
    me = _dev_index(xi, yi, ci)
    x2 = x.reshape(S, D)
    tgt = loss_target.reshape(S, D)

    pay = jnp.zeros((8, D), F32).at[0].set(c[0]).at[1:4, :HEAD].set(conv_w[0])
    got = _allgather_small(pay, "gather_cond")
    c_all = got[:, 0, :]
    cw8 = jnp.pad(got[:, 1:4, :HEAD].transpose(1, 0, 2).reshape(3, D), ((0, 5), (0, 0)))
    ncol = w_ada.shape[2]
    b_cols = lax.dynamic_slice(b_ada, (0, me * ncol), (1, ncol))
    mod_cols, act = _ada_fwd(c_all, w_ada[0], b_cols)
    mod_all = _allgather_small(mod_cols, "gather_mod")

    w_in_shard, mod_all = lax.optimization_barrier((w_in[0].T.astype(BF16), mod_all))
    mod = lax.dynamic_index_in_dim(mod_all, me, axis=1, keepdims=False).reshape(6, D)
    (w_int,) = _allgather_weights([w_in_shard])
    late = [w_branch_attn[0].T.astype(BF16), w_branch_conv[0].astype(BF16), w_out[0].astype(BF16),
            w_mlp_in[0].T.astype(BF16), w_mlp_out[0].astype(BF16)]
    w_int, late = lax.optimization_barrier((w_int, late))
    zones = [lax.dynamic_update_slice(lax.empty((N_DEV * t.shape[0], t.shape[1]), BF16), t, (me * t.shape[0], 0)) for t in late]
    ag_mix = _split_start("gather_mix_start", "gather", late[:3], zones[:3])
    ag_mlp = _split_start("gather_mlp_start", "gather", late[3:], zones[3:])

    def mix_weights(o_attn):
        return _split_wait("gather_mix_wait", "gather", *ag_mix[:4], o_attn)[1]

    def mlp_weights(x1):
        return _split_wait("gather_mlp_wait", "gather", *ag_mlp[:4], x1)[1]

    rs = {}

    def mlp_grads_ready(*grads):
        lands = [lax.empty((N_PEER, t.shape[0] // N_DEV, t.shape[1]), BF16) for t in grads]
        rs["mlp"] = _split_start("scatter_mlp_start", "scatter", grads, lands)
        return rs["mlp"][4]

    def other_grads_ready(*grads):
        core = ci.reshape(1).astype(jnp.int32)
        pair = [_pair_sum(g, sib, core, "pair_sum_%d" % k) for k, (g, sib) in enumerate(zip(grads, _sibling_exchange(grads)))]
        lands = [lax.empty((3,) + t.shape[1:], BF16) for t in pair]
        rs["rest"] = _split_start("scatter_rest_start", "chips", pair, lands)
        return rs["rest"][4]

    ba, bb = b_gate[:, :D], b_gate[:, D:]
    grad_x, vec = _local_step(
        x2, tgt, mod + ag_mix[4] + ag_mlp[4], g_norm_mix, g_norm_mlp, g_norm_final.reshape(1, D), ba, bb, cw8, w_int, mix_weights, mlp_weights,
        mlp_grads_ready, other_grads_ready)

    vec_all = _allgather_small(vec, "gather_vec")
    vec_sum = _sum_parts(vec_all, "sum_vec")
    loss = vec_sum[14, 0]
    gm_all = vec_all[:, 0:6, :].reshape(N_DEV, 6 * D)
    gm_cols = lax.dynamic_slice(gm_all, (0, me * ncol), (N_DEV, ncol))
    g_w_ada = _ada_bwd(act.T, gm_cols)
    conv_cols = lax.dynamic_slice(vec_sum[11:14], (0, me * HEAD), (3, HEAD))
    g_pack = jnp.concatenate([vec_sum[0:11], jnp.pad(conv_cols, ((0, 0), (0, D - HEAD))), jnp.zeros((2, D), F32)], axis=0)
    packs = [_pack_vectors(*t) for t in ((b_ada, g_norm_mix, g_norm_mlp, g_norm_final, b_gate, conv_w),
                                         (m_b_ada, m_g_norm_mix, m_g_norm_mlp, m_g_norm_final, m_b_gate, m_conv_w),
                                         (v_b_ada, v_g_norm_mix, v_g_norm_mlp, v_g_norm_final, v_b_gate, v_conv_w))]
    d_pack, m_pack, v_pack = _adamw(packs[0], g_pack, packs[1], packs[2], "adamw_vectors")
    d_ada, nm_ada, nv_ada = _adamw(w_ada[0], g_w_ada, m_w_ada[0], v_w_ada[0], "adamw_w_ada")

    sums = {}
    srcs, lands = _split_wait("scatter_mlp_wait", "scatter", *rs["mlp"][:4], grad_x)
    for n, g, land in zip(("w_mi", "w_mo"), srcs, lands):
        r = land.shape[1]
        sums[n] = _sum_parts(land, "sum_" + n, own=lax.dynamic_slice(g, (me * r, 0), (r, g.shape[1])))
    srcs, lands = _split_wait("scatter_rest_wait", "chips", *rs["rest"][:4], grad_x)
    for n, pair, land in zip(("w_in", "w_ba", "w_bc", "w_out"), srcs, lands):
        sums[n] = _sum_parts(land, "sum_" + n, own=lax.dynamic_index_in_dim(pair, 2 * xi + yi, axis=0, keepdims=False))
    g_ba, g_bc, g_out, g_mi, g_mo = sums["w_ba"].T, sums["w_bc"], sums["w_out"], sums["w_mi"].T, sums["w_mo"]
    big = {}
    for n, w, g, m, v in (("w_ba", w_branch_attn, g_ba, m_w_branch_attn, v_w_branch_attn),
                          ("w_bc", w_branch_conv, g_bc, m_w_branch_conv, v_w_branch_conv), ("w_out", w_out, g_out, m_w_out, v_w_out),
                          ("w_mi", w_mlp_in, g_mi, m_w_mlp_in, v_w_mlp_in), ("w_mo", w_mlp_out, g_mo, m_w_mlp_out, v_w_mlp_out)):
        big[n] = (g[None],) + tuple(t[None] for t in _adamw(w[0], g, m[0], v[0], "adamw_" + n))
    g_in_t = sums["w_in"]
    big["w_in"] = tuple(t.T[None] for t in (g_in_t, *_adamw(w_in[0].T, g_in_t, m_w_in[0].T, v_w_in[0].T, "adamw_w_in")))

    gv = _unpack_vectors(g_pack)
    dv = _unpack_vectors(d_pack)
    mv = _unpack_vectors(m_pack)
    vv = _unpack_vectors(v_pack)

    def ordered(k, ada, vecs):
        return (ada[None], vecs[0], vecs[1], big["w_in"][k], vecs[2], vecs[3], big["w_ba"][k], big["w_bc"][k],
                big["w_out"][k], vecs[4], big["w_mi"][k], big["w_mo"][k], vecs[5])

    return (loss, grad_x.reshape(1, S, D), *ordered(0, g_w_ada, gv), *ordered(1, d_ada, dv),
            *ordered(2, nm_ada, mv), *ordered(3, nv_ada, vv))
```

```python
import functools

import numpy as np
import jax
import jax.numpy as jnp
from jax import lax
from jax.experimental import pallas as pl
from jax.experimental.pallas import tpu as pltpu

F32, BF16 = jnp.float32, jnp.bfloat16
D = 1024
HEAD = 128
DILATIONS = (1, 4, 16)
N_SLOT = 4
AOW = N_SLOT * HEAD
DFF = 4 * D
N_DEV = 8
UNROLL = 8
EPS = 1e-6
NEG = -1e30
SCALE = HEAD ** -0.5
LR, B1, B2, ADAM_EPS, WD, STEP = 0.001, 0.9, 0.999, 1e-08, 0.01, 10
V7X_VMEM_LIMIT = 56 * 1024 * 1024
TM = 1024
MESH = pl.DeviceIdType.MESH
AXES = ("x", "y", "c")


def _cparams(*sem):
    if sem:
        return pltpu.CompilerParams(dimension_semantics=sem, vmem_limit_bytes=V7X_VMEM_LIMIT)
    return pltpu.CompilerParams(vmem_limit_bytes=V7X_VMEM_LIMIT)


def _nn(a, b):
    return jnp.dot(a, b, preferred_element_type=F32)


def _nt(a, b):
    return lax.dot_general(a, b, (((1,), (1,)), ((), ())), preferred_element_type=F32)


def _tn(a, b):
    return lax.dot_general(a, b, (((0,), (0,)), ((), ())), preferred_element_type=F32)


def _rms_r(x):
    return lax.rsqrt(jnp.mean(x * x, axis=-1, keepdims=True) + EPS)


def _rms_bwd(x, r, g, dn):
    gy = dn * g
    dx = r * gy - x * (r * r * r) * jnp.mean(x * gy, axis=-1, keepdims=True)
    return dx, dn * (x * r)


def _sigmoid(t):
    return 1.0 / (1.0 + jnp.exp(-t))


def _rowsum(v):
    return jnp.sum(v, axis=0, keepdims=True)


def _vec_spec(n=D):
    return pl.BlockSpec((1, n), lambda *_: (0, 0))


def _const_spec(shape):
    nd = len(shape)
    return pl.BlockSpec(shape, lambda *_: (0,) * nd)


def _win_rowblock(j):
    return jnp.where(j < 9, (j % 3) * 3 + j // 3, j)


def _proj(x, g, sc, sh, w_int):
    S = x.shape[0]
    tm = 2 * TM

    def body(x_ref, g_ref, sc_ref, sh_ref, w_ref, h_ref, q_ref, e_ref):
        j = pl.program_id(1)

        @pl.when(j == 0)
        def _():
            xv = x_ref[...]
            h = xv * _rms_r(xv) * g_ref[...] * (1.0 + sc_ref[...]) + sh_ref[...]
            h_ref[...] = h.astype(BF16)

        acc = _nt(h_ref[...], w_ref[...])

        @pl.when(j < 9)
        def _():
            q_ref[0] = acc

        @pl.when(j >= 9)
        def _():
            e_ref[0] = acc.astype(BF16)

    def e_idx(i, j):
        k = jnp.maximum(j - 9, 0)
        return (k // 2, i, k % 2)

    return pl.pallas_call(
        body, name="proj", grid=(S // tm, 19),
        in_specs=[pl.BlockSpec((tm, D), lambda i, j: (i, 0)), _vec_spec(), _vec_spec(), _vec_spec(),
                  pl.BlockSpec((512, D), lambda i, j: (_win_rowblock(j), 0))],
        out_specs=[pl.BlockSpec((tm, D), lambda i, j: (i, 0)),
                   pl.BlockSpec((1, tm, 512), lambda i, j: (jnp.minimum(j, 8), i, 0)),
                   pl.BlockSpec((1, tm, 512), e_idx)],
        out_shape=[jax.ShapeDtypeStruct((S, D), BF16), jax.ShapeDtypeStruct((9, S, 512), F32),
                   jax.ShapeDtypeStruct((5, S, D), BF16)],
        compiler_params=_cparams("parallel", "arbitrary"),
    )(x, g, sc, sh, w_int)


def _bias_table():
    slopes = (2.0 ** (-8.0 * np.arange(1, 13, dtype=np.float32) / 12.0)).astype(np.float32)
    qi = np.arange(HEAD)[:, None]
    kj = np.arange(2 * HEAD)[None, :]
    delta = HEAD + qi - kj
    mask = (delta >= 0) & (delta <= HEAD)
    out = np.zeros((3, N_SLOT, HEAD, 2 * HEAD), np.float32)
    for gi, d in enumerate(DILATIONS):
        for j in range(N_SLOT):
            bias = -slopes[gi * N_SLOT + j] * (delta * d).astype(np.float32)
            out[gi, j] = np.where(mask, bias, NEG)
    out_t = np.concatenate([out[..., HEAD:].swapaxes(-1, -2), out[..., :HEAD].swapaxes(-1, -2)], axis=-1)
    return jnp.asarray(out), jnp.asarray(out_t)


def _block_rows(b, d):
    r = b % d
    n = b // d
    st = n * (HEAD * d) + r
    stp = jnp.maximum(n - 1, 0) * (HEAD * d) + r
    return n, st, stp


def _attn_fwd(qkv, bias):
    S = qkv.shape[2]
    nblk = S // HEAD
    rows = 256

    def body(qkv_ref, b_ref, o_ref, lse_ref, o_s, lse_s):
        g = pl.program_id(1)
        bias = b_ref[0, 0]
        col = lax.broadcasted_iota(jnp.int32, bias.shape, 1)
        bias_first = jnp.where(col < HEAD, NEG, bias)

        for gi, d in enumerate(DILATIONS):
            @pl.when(g == gi)
            def _(gi=gi, d=d):
                def step(b, carry):
                    n, st, stp = _block_rows(b, d)
                    cur = pl.ds(st, HEAD, stride=d)
                    prv = pl.ds(stp, HEAD, stride=d)
                    q = qkv_ref.at[0, 0][cur, :].astype(BF16)
                    kw = jnp.concatenate([qkv_ref.at[0, 1][prv, :], qkv_ref.at[0, 1][cur, :]], axis=0).astype(BF16)
                    vw = jnp.concatenate([qkv_ref.at[0, 2][prv, :], qkv_ref.at[0, 2][cur, :]], axis=0).astype(BF16)
                    s = _nt(q, kw) * SCALE + jnp.where(n > 0, bias, bias_first)
                    m = jnp.max(s, axis=-1, keepdims=True)
                    p = jnp.exp(s - m)
                    l = jnp.sum(p, axis=-1, keepdims=True)
                    o_s.at[gi][cur, :] = _nn(p.astype(BF16), vw) / l
                    lse_s.at[gi][cur, :] = jnp.broadcast_to(m + jnp.log(l), (HEAD, HEAD))
                    return carry

                lax.fori_loop(0, nblk, step, 0, unroll=UNROLL)

        @pl.when(g == len(DILATIONS) - 1)
        def _():
            def merge(i, carry):
                r = pl.ds(pl.multiple_of(i * rows, rows), rows)
                ls = [lse_s[k, r, :] for k in range(3)]
                top = jnp.maximum(jnp.maximum(ls[0], ls[1]), ls[2])
                ws = [jnp.exp(t - top) for t in ls]
                den = ws[0] + ws[1] + ws[2]
                o_ref[r, :] = (ws[0] * o_s[0, r, :] + ws[1] * o_s[1, r, :] + ws[2] * o_s[2, r, :]) / den
                lse_ref[r, :] = top + jnp.log(den)
                return carry

            lax.fori_loop(0, S // rows, merge, 0)

    return pl.pallas_call(
        body, name="attn_fwd", grid=(N_SLOT, 3),
        in_specs=[pl.BlockSpec((1, 3, S, HEAD), lambda j, g: (g, 0, 0, j)),
                  pl.BlockSpec((1, 1, HEAD, 2 * HEAD), lambda j, g: (g, j, 0, 0))],
        out_specs=[pl.BlockSpec((S, HEAD), lambda j, g: (0, j)), pl.BlockSpec((S, HEAD), lambda j, g: (0, j))],
        out_shape=[jax.ShapeDtypeStruct((S, AOW), F32), jax.ShapeDtypeStruct((S, AOW), F32)],
        scratch_shapes=[pltpu.VMEM((3, S, HEAD), F32)] * 2,
        compiler_params=_cparams("parallel", "arbitrary"),
    )(qkv, bias)


def _shift_down(z, k, halo_rows):
    out = pltpu.roll(z, k, axis=0)
    top = out[:8]
    rid = lax.broadcasted_iota(jnp.int32, top.shape, 0)
    for t in range(k):
        top = jnp.where(rid == t, halo_rows[t], top)
    return jnp.concatenate([top, out[8:]], axis=0)


def _shift_up(z, k, halo_rows):
    n = z.shape[0]
    out = pltpu.roll(z, n - k, axis=0)
    bottom = out[n - 8:]
    rid = lax.broadcasted_iota(jnp.int32, bottom.shape, 0)
    for t in range(k):
        bottom = jnp.where(rid == 8 - k + t, halo_rows[t], bottom)
    return jnp.concatenate([out[:n - 8], bottom], axis=0)


def _e_spec(chunk, tm):
    return pl.BlockSpec((1, tm, D), lambda i, c=chunk: (c, i, 0))


def _e_prev_spec(chunk, tm):
    return pl.BlockSpec((1, 16, D), lambda i, c=chunk: (c, jnp.maximum(i * (tm // 16) - 1, 0), 0))


def _e_next_spec(chunk, tm, S):
    return pl.BlockSpec((1, 16, D), lambda i, c=chunk: (c, jnp.minimum((i + 1) * (tm // 16), S // 16 - 1), 0))


def _mix(o_attn, e, cw8, ba, bb, w_bat, w_bc):
    S = o_attn.shape[0]
    tm = 256

    def body(o_ref, cb_ref, cc_ref, cx_ref, ga_ref, gb_ref, ccp_ref, cxp_ref, cw_ref, ba_ref, bb_ref, wba_ref, wbc_ref,
             obf_ref, cbu_ref, ya_ref, yc_ref, mg_ref):
        i = pl.program_id(0)
        o = o_ref[...].astype(BF16)
        obf_ref[...] = o
        ya = _nt(o, wba_ref[...])
        z = cc_ref[0].astype(F32) * cx_ref[0].astype(F32)
        zp = ccp_ref[0].astype(F32) * cxp_ref[0].astype(F32) * (i > 0).astype(F32)
        z1 = _shift_down(z, 1, [zp[15:16]])
        z2 = _shift_down(z, 2, [zp[14:15], zp[15:16]])
        cw = cw_ref[...]
        u = cw[0:1] * z2 + cw[1:2] * z1 + cw[2:3] * z
        cbu = (cb_ref[0].astype(F32) * u).astype(BF16)
        cbu_ref[...] = cbu
        yc = _nn(cbu, wbc_ref[...])
        sa = _sigmoid(ga_ref[0].astype(F32) + ba_ref[...])
        sb = _sigmoid(gb_ref[0].astype(F32) + bb_ref[...])
        ya_ref[...] = ya.astype(BF16)
        yc_ref[...] = yc.astype(BF16)
        mg_ref[...] = (sa * ya + sb * yc).astype(BF16)

    row = lambda w: pl.BlockSpec((tm, w), lambda i: (i, 0))
    return pl.pallas_call(
        body, name="mix", grid=(S // tm,),
        in_specs=[row(AOW)] + [_e_spec(c, tm) for c in range(5)] + [_e_prev_spec(1, tm), _e_prev_spec(2, tm),
                  _const_spec((8, D)), _vec_spec(), _vec_spec(), _const_spec((D, AOW)), _const_spec((D, D))],
        out_specs=[row(AOW), row(D), row(D), row(D), row(D)],
        out_shape=[jax.ShapeDtypeStruct((S, AOW), BF16)] + [jax.ShapeDtypeStruct((S, D), BF16)] * 4,
        compiler_params=_cparams("parallel"),
    )(o_attn, e, e, e, e, e, e, e, cw8, ba, bb, w_bat, w_bc)


def _out_proj(merged, w_out, x, gate1, g_mlp, sc2, sh2):
    S = x.shape[0]
    tm = TM

    def body(mg_ref, w_ref, x_ref, gt_ref, g_ref, sc_ref, sh_ref, x1_ref, mo_ref, h2_ref):
        mo = _nn(mg_ref[...], w_ref[...])
        mo_ref[...] = mo.astype(BF16)
        x1 = x_ref[...] + gt_ref[...] * mo
        x1_ref[...] = x1
        h2 = x1 * _rms_r(x1) * g_ref[...] * (1.0 + sc_ref[...]) + sh_ref[...]
        h2_ref[...] = h2.astype(BF16)

    row = pl.BlockSpec((tm, D), lambda i: (i, 0))
    return pl.pallas_call(
        body, name="out_proj", grid=(S // tm,),
        in_specs=[row, _const_spec((D, D)), row, _vec_spec(), _vec_spec(), _vec_spec(), _vec_spec()],
        out_specs=[row, row, row],
        out_shape=[jax.ShapeDtypeStruct((S, D), F32), jax.ShapeDtypeStruct((S, D), BF16), jax.ShapeDtypeStruct((S, D), BF16)],
        compiler_params=_cparams("parallel"),
    )(merged, w_out, x, gate1, g_mlp, sc2, sh2)


def _mlp_in(h2, w_mit):
    S = h2.shape[0]
    tm, tn = TM, 2048

    def body(h_ref, w_ref, a_ref, f_ref):
        a = _nt(h_ref[...], w_ref[...])
        a_ref[...] = a.astype(BF16)
        f_ref[...] = jnp.square(jnp.maximum(a, 0.0)).astype(BF16)

    blk = pl.BlockSpec((tm, tn), lambda i, j: (i, j))
    return pl.pallas_call(
        body, name="mlp_in", grid=(S // tm, DFF // tn),
        in_specs=[pl.BlockSpec((tm, D), lambda i, j: (i, 0)), pl.BlockSpec((tn, D), lambda i, j: (j, 0))],
        out_specs=[blk, blk],
        out_shape=[jax.ShapeDtypeStruct((S, DFF), BF16)] * 2,
        compiler_params=_cparams("parallel", "parallel"),
    )(h2, w_mit)


def _mlp_out(f, w_mo, x1, gate2, g_fin, tgt):
    S = x1.shape[0]
    tm = 512
    half = tm // 2

    def body(f_ref, w_ref, x1_ref, gt_ref, g_ref, t_ref, mlp_ref, dx2_ref, pv_ref):
        @pl.when(pl.program_id(0) == 0)
        def _():
            pv_ref[...] = jnp.zeros_like(pv_ref)

        g = g_ref[...]
        for hs in (pl.ds(0, half), pl.ds(half, half)):
            mlp = _nn(f_ref[hs, :], w_ref[...])
            mlp_ref[hs, :] = mlp.astype(BF16)
            x2 = x1_ref[hs, :] + gt_ref[...] * mlp
            r = _rms_r(x2)
            err = x2 * r * g - t_ref[hs, :]
            dx2, pg = _rms_bwd(x2, r, g, err * (1.0 / D))
            dx2_ref[hs, :] = dx2
            pv_ref[0:1, :] += _rowsum(pg)
            pv_ref[1:2, :] += 0.5 * _rowsum(jnp.mean(err * err, axis=-1, keepdims=True))

    row = pl.BlockSpec((tm, D), lambda i: (i, 0))
    return pl.pallas_call(
        body, name="mlp_out", grid=(S // tm,),
        in_specs=[pl.BlockSpec((tm, DFF), lambda i: (i, 0)), _const_spec((DFF, D)), row, _vec_spec(), _vec_spec(), row],
        out_specs=[row, row, _const_spec((8, D))],
        out_shape=[jax.ShapeDtypeStruct((S, D), BF16), jax.ShapeDtypeStruct((S, D), F32), jax.ShapeDtypeStruct((8, D), F32)],
        compiler_params=_cparams("arbitrary"),
    )(f, w_mo, x1, gate2, g_fin, tgt)


def _bwd_mlp_a(dx2, gate2, mlp, w_mo, a):
    S = dx2.shape[0]
    tm, tn = TM, 2048

    def body(dx_ref, gt_ref, mlp_ref, w_ref, a_ref, da_ref, dmo_ref, pv_ref):
        i, j = pl.program_id(0), pl.program_id(1)

        @pl.when((i == 0) & (j == 0))
        def _():
            pv_ref[...] = jnp.zeros_like(pv_ref)

        @pl.when(j == 0)
        def _():
            dx = dx_ref[...]
            dmo_ref[...] = (dx * gt_ref[...]).astype(BF16)
            pv_ref[0:1, :] += _rowsum(dx * mlp_ref[...].astype(F32))

        df = _nt(dmo_ref[...], w_ref[...])
        da_ref[...] = (df * (2.0 * jnp.maximum(a_ref[...].astype(F32), 0.0))).astype(BF16)

    row = pl.BlockSpec((tm, D), lambda i, j: (i, 0))
    blk = pl.BlockSpec((tm, tn), lambda i, j: (i, j))
    return pl.pallas_call(
        body, name="bwd_mlp_a", grid=(S // tm, DFF // tn),
        in_specs=[row, _vec_spec(), row, pl.BlockSpec((tn, D), lambda i, j: (j, 0)), blk],
        out_specs=[blk, row, _const_spec((8, D))],
        out_shape=[jax.ShapeDtypeStruct((S, DFF), BF16), jax.ShapeDtypeStruct((S, D), BF16), jax.ShapeDtypeStruct((8, D), F32)],
        compiler_params=_cparams("arbitrary", "arbitrary"),
    )(dx2, gate2, mlp, w_mo, a)


def _bwd_mlp_b(da, w_mit, x1, dx2, g_mlp, sc2):
    S = x1.shape[0]
    tm = 512
    half = tm // 2

    def body(da_ref, w_ref, x1_ref, dx2_ref, g_ref, sc_ref, dx1_ref, pv_ref):
        @pl.when(pl.program_id(0) == 0)
        def _():
            pv_ref[...] = jnp.zeros_like(pv_ref)

        g = g_ref[...]
        for hs in (pl.ds(0, half), pl.ds(half, half)):
            dh = _nn(da_ref[hs, :], w_ref[...])
            x1 = x1_ref[hs, :]
            r = _rms_r(x1)
            dxn, pg = _rms_bwd(x1, r, g, dh * (1.0 + sc_ref[...]))
            dx1_ref[hs, :] = dx2_ref[hs, :] + dxn
            pv_ref[0:1, :] += _rowsum(dh)
            pv_ref[1:2, :] += _rowsum(dh * (x1 * r * g))
            pv_ref[2:3, :] += _rowsum(pg)

    row = pl.BlockSpec((tm, D), lambda i: (i, 0))
    return pl.pallas_call(
        body, name="bwd_mlp_b", grid=(S // tm,),
        in_specs=[pl.BlockSpec((tm, DFF), lambda i: (i, 0)), _const_spec((DFF, D)), row, row, _vec_spec(), _vec_spec()],
        out_specs=[row, _const_spec((8, D))],
        out_shape=[jax.ShapeDtypeStruct((S, D), F32), jax.ShapeDtypeStruct((8, D), F32)],
        compiler_params=_cparams("arbitrary"),
    )(da, w_mit, x1, dx2, g_mlp, sc2)


def _bwd_mix(dx1, gate1, mo, e, cw8, ba, bb, ya, yc, o_attn, w_out, w_bc, w_bat):
    S = dx1.shape[0]
    tm = 256
    n_tiles = S // tm

    def body(dx_ref, dxn_ref, gt_ref, mo_ref, cb_ref, cc_ref, cx_ref, ga_ref, gb_ref, cbn_ref, gbn_ref, ccp_ref, cxp_ref,
             cw_ref, ba_ref, bb_ref, ya_ref, yc_ref, o_ref, wout_ref, wbc_ref, wba_ref,
             dmo_ref, dya_ref, dyc_ref, do_ref, dl_ref, de_ref, pv_ref):
        i = pl.program_id(0)

        @pl.when(i == 0)
        def _():
            pv_ref[...] = jnp.zeros_like(pv_ref)

        gate = gt_ref[...]
        bbv = bb_ref[...]

        def conv_branch_grad(dx_rows, gb_rows):
            dmo = (dx_rows * gate).astype(BF16)
            dmg = _nt(dmo, wout_ref[...])
            sb = _sigmoid(gb_rows + bbv)
            dyc = dmg * sb
            return dmo, dmg, sb, dyc, _nt(dyc.astype(BF16), wbc_ref[...])

        dx = dx_ref[...]
        cb = cb_ref[0].astype(F32)
        cc = cc_ref[0].astype(F32)
        cx = cx_ref[0].astype(F32)
        dmo, dmg, sb, dyc, dcbu = conv_branch_grad(dx, gb_ref[0].astype(F32))
        dmo_ref[...] = dmo
        pv_ref[0:1, :] += _rowsum(dx * mo_ref[...].astype(F32))
        sa = _sigmoid(ga_ref[0].astype(F32) + ba_ref[...])
        dya = (dmg * sa).astype(BF16)
        dya_ref[...] = dya
        dyc_ref[...] = dyc.astype(BF16)
        dga = dmg * ya_ref[...].astype(F32) * sa * (1.0 - sa)
        dgb = dmg * yc_ref[...].astype(F32) * sb * (1.0 - sb)
        pv_ref[1:2, :] += _rowsum(dga)
        pv_ref[2:3, :] += _rowsum(dgb)

        do = _nn(dya, wba_ref[...])
        do_ref[...] = do
        prod = do * o_ref[...]
        dl_ref[...] = jnp.concatenate(
            [jnp.broadcast_to(jnp.sum(prod[:, s * HEAD:(s + 1) * HEAD], axis=-1, keepdims=True), (tm, HEAD))
             for s in range(N_SLOT)], axis=1)

        z = cc * cx
        zp = ccp_ref[0].astype(F32) * cxp_ref[0].astype(F32) * (i > 0).astype(F32)
        z1 = _shift_down(z, 1, [zp[15:16]])
        z2 = _shift_down(z, 2, [zp[14:15], zp[15:16]])
        cw = cw_ref[...]
        u = cw[0:1] * z2 + cw[1:2] * z1 + cw[2:3] * z
        du = dcbu * cb
        dcbu_n = conv_branch_grad(dxn_ref[...], gbn_ref[0].astype(F32))[4]
        du_n = dcbu_n * cbn_ref[0].astype(F32) * (i < n_tiles - 1).astype(F32)
        du1 = _shift_up(du, 1, [du_n[0:1]])
        du2 = _shift_up(du, 2, [du_n[0:1], du_n[1:2]])
        dz = cw[2:3] * du + cw[1:2] * du1 + cw[0:1] * du2
        pv_ref[3:4, :] += _rowsum(du * z2)
        pv_ref[4:5, :] += _rowsum(du * z1)
        pv_ref[5:6, :] += _rowsum(du * z)

        de_ref[0] = (dcbu * u).astype(BF16)
        de_ref[1] = (dz * cx).astype(BF16)
        de_ref[2] = (dz * cc).astype(BF16)
        de_ref[3] = dga.astype(BF16)
        de_ref[4] = dgb.astype(BF16)

    row = lambda w: pl.BlockSpec((tm, w), lambda i: (i, 0))
    nxt = pl.BlockSpec((16, D), lambda i: (jnp.minimum((i + 1) * (tm // 16), S // 16 - 1), 0))
    return pl.pallas_call(
        body, name="bwd_mix", grid=(n_tiles,),
        in_specs=[row(D), nxt, _vec_spec(), row(D)] + [_e_spec(c, tm) for c in range(5)]
                 + [_e_next_spec(0, tm, S), _e_next_spec(4, tm, S), _e_prev_spec(1, tm), _e_prev_spec(2, tm),
                    _const_spec((8, D)), _vec_spec(), _vec_spec(), row(D), row(D), row(AOW),
                    _const_spec((D, D)), _const_spec((D, D)), _const_spec((D, AOW))],
        out_specs=[row(D), row(D), row(D), row(AOW), row(AOW), pl.BlockSpec((5, tm, D), lambda i: (0, i, 0)),
                   _const_spec((8, D))],
        out_shape=[jax.ShapeDtypeStruct((S, D), BF16)] * 3 + [jax.ShapeDtypeStruct((S, AOW), F32)] * 2
                  + [jax.ShapeDtypeStruct((5, S, D), BF16), jax.ShapeDtypeStruct((8, D), F32)],
        compiler_params=_cparams("arbitrary"),
    )(dx1, dx1, gate1, mo, e, e, e, e, e, e, e, e, e, cw8, ba, bb, ya, yc, o_attn, w_out, w_bc, w_bat)


def _attn_bwd(qkv, do, lse, dl, bias_t):
    S = qkv.shape[2]
    nblk = S // HEAD

    def body(qkv_ref, do_ref, lse_ref, dl_ref, b_ref, d_ref):
        g = pl.program_id(1)
        bias = b_ref[0, 0]
        col = lax.broadcasted_iota(jnp.int32, bias.shape, 1)
        bias_last = jnp.where(col >= HEAD, NEG, bias)
        eye = (lax.broadcasted_iota(jnp.int32, (HEAD, HEAD), 0) == lax.broadcasted_iota(jnp.int32, (HEAD, HEAD), 1)).astype(F32)

        def as_row(t):
            return jnp.sum(t * eye, axis=0, keepdims=True)

        for gi, d in enumerate(DILATIONS):
            @pl.when(g == gi)
            def _(d=d):
                nb = nblk // d

                def step(b, dq_part):
                    r, n = b // nb, b % nb
                    cur = pl.ds(n * (HEAD * d) + r, HEAD, stride=d)
                    nxt = pl.ds(jnp.minimum(n + 1, nb - 1) * (HEAD * d) + r, HEAD, stride=d)
                    two = lambda ref: jnp.concatenate([ref[cur, :], ref[nxt, :]], axis=0)
                    two_rows = lambda ref: jnp.concatenate([as_row(ref[cur, :]), as_row(ref[nxt, :])], axis=1)
                    q2 = two(qkv_ref.at[0, 0]).astype(BF16)
                    do2 = two(do_ref).astype(BF16)
                    k = qkv_ref.at[0, 1][cur, :].astype(BF16)
                    v = qkv_ref.at[0, 2][cur, :].astype(BF16)
                    s = _nt(k, q2) * SCALE + jnp.where(n < nb - 1, bias, bias_last)
                    p = jnp.exp(s - two_rows(lse_ref))
                    d_ref.at[0, 2][cur, :] = _nn(p.astype(BF16), do2)
                    dp = _nt(v, do2)
                    ds = (p * (dp - two_rows(dl_ref)) * SCALE).astype(BF16)
                    d_ref.at[0, 1][cur, :] = _nn(ds, q2)
                    dq2 = _tn(ds, k)
                    d_ref.at[0, 0][cur, :] = dq2[:HEAD] + jnp.where(n > 0, dq_part, 0.0)
                    return dq2[HEAD:]

                def steps(i, dq_part):
                    for u in range(UNROLL):
                        dq_part = step(i * UNROLL + u, dq_part)
                    return dq_part

                lax.fori_loop(0, nblk // UNROLL, steps, jnp.zeros((HEAD, HEAD), F32))

    col_blk = pl.BlockSpec((S, HEAD), lambda j, g: (0, j))
    qkv_blk = pl.BlockSpec((1, 3, S, HEAD), lambda j, g: (g, 0, 0, j))
    return pl.pallas_call(
        body, name="attn_bwd", grid=(N_SLOT, 3),
        in_specs=[qkv_blk, col_blk, col_blk, col_blk, pl.BlockSpec((1, 1, HEAD, 2 * HEAD), lambda j, g: (g, j, 0, 0))],
        out_specs=qkv_blk,
        out_shape=jax.ShapeDtypeStruct((3, 3, S, AOW), F32),
        compiler_params=_cparams("parallel", "arbitrary"),
    )(qkv, do, lse, dl, bias_t)


def _bwd_in(dqkv, de, w_int, x, dx1, g_mix, sc1):
    S = x.shape[0]
    tm = TM
    dqkv = dqkv.reshape(3, 3, S, AOW)

    def body(dq_ref, de_ref, wq_ref, wk_ref, wv_ref, wa_ref, wb_ref, x_ref, dx1_ref, g_ref, sc_ref, gx_ref, pv_ref):
        acc = gx_ref
        i, k = pl.program_id(0), pl.program_id(1)

        @pl.when((i == 0) & (k == 0))
        def _():
            pv_ref[...] = jnp.zeros_like(pv_ref)

        @pl.when(k == 0)
        def _():
            acc[...] = jnp.zeros_like(acc)

        @pl.when(k < 3)
        def _():
            lhs = jnp.concatenate([dq_ref[0, t].astype(BF16) for t in range(3)], axis=1)
            acc[...] += _nn(lhs, jnp.concatenate([wq_ref[...], wk_ref[...], wv_ref[...]], axis=0))

        @pl.when(k >= 3)
        def _():
            acc[...] += _nn(de_ref[0], jnp.concatenate([wa_ref[...], wb_ref[...]], axis=0))

        @pl.when(k == 7)
        def _():
            dh = acc[...]
            xv = x_ref[...]
            r = _rms_r(xv)
            g = g_ref[...]
            dxn, pg = _rms_bwd(xv, r, g, dh * (1.0 + sc_ref[...]))
            gx_ref[...] = dx1_ref[...] + dxn
            pv_ref[0:1, :] += _rowsum(dh)
            pv_ref[1:2, :] += _rowsum(dh * (xv * r * g))
            pv_ref[2:3, :] += _rowsum(pg)

    grp = lambda k: jnp.minimum(k, 2)
    chunk = lambda k: jnp.maximum(k - 3, 0)
    wblk = lambda f: pl.BlockSpec((512, D), lambda i, k: (f(k), 0))
    row = pl.BlockSpec((tm, D), lambda i, k: (i, 0))
    once = pl.BlockSpec((tm, D), lambda i, k: (i, 0), pipeline_mode=pl.Buffered(1))
    return pl.pallas_call(
        body, name="bwd_in", grid=(S // tm, 8),
        in_specs=[pl.BlockSpec((1, 3, tm, 512), lambda i, k: (grp(k), 0, i, 0)),
                  pl.BlockSpec((1, tm, D), lambda i, k: (chunk(k), i, 0)),
                  wblk(grp), wblk(lambda k: 3 + grp(k)), wblk(lambda k: 6 + grp(k)),
                  wblk(lambda k: 9 + 2 * chunk(k)), wblk(lambda k: 10 + 2 * chunk(k)),
                  once, once, _vec_spec(), _vec_spec()],
        out_specs=[row, _const_spec((8, D))],
        out_shape=[jax.ShapeDtypeStruct((S, D), F32), jax.ShapeDtypeStruct((8, D), F32)],
        compiler_params=_cparams("arbitrary", "arbitrary"),
    )(dqkv, de, w_int, w_int, w_int, w_int, w_int, x, dx1, g_mix, sc1)


def _grad_w(name, a, b):
    S, ka = a.shape
    nb = b.shape[1]

    def body(a_ref, b_ref, o_ref):
        o_ref[...] = _tn(a_ref[...], b_ref[...]).astype(BF16)

    return pl.pallas_call(
        body, name=name, grid=(ka // 512,),
        in_specs=[pl.BlockSpec((S, 512), lambda n: (0, n)), pl.BlockSpec((S, nb), lambda n: (0, 0))],
        out_specs=pl.BlockSpec((512, nb), lambda n: (n, 0)),
        out_shape=jax.ShapeDtypeStruct((ka, nb), BF16),
        compiler_params=_cparams("parallel"),
    )(a, b)


def _grad_w_in(dqkv, de, h):
    S = h.shape[0]

    def body(dq_ref, de_ref, h_ref, o_ref):
        n = pl.program_id(0)

        @pl.when(n < 9)
        def _():
            o_ref[...] = _tn(dq_ref[0].astype(BF16), h_ref[...]).astype(BF16)

        @pl.when(n >= 9)
        def _():
            o_ref[...] = _tn(de_ref[0], h_ref[...]).astype(BF16)

    def e_idx(n):
        kk = jnp.maximum(n - 9, 0)
        return (kk // 2, 0, kk % 2)

    return pl.pallas_call(
        body, name="grad_w_in", grid=(19,),
        in_specs=[pl.BlockSpec((1, S, 512), lambda n: (jnp.minimum(n, 8), 0, 0)), pl.BlockSpec((1, S, 512), e_idx),
                  pl.BlockSpec((S, D), lambda n: (0, 0))],
        out_specs=pl.BlockSpec((512, D), lambda n: (_win_rowblock(n), 0)),
        out_shape=jax.ShapeDtypeStruct((19 * 512, D), BF16),
        compiler_params=_cparams("parallel"),
    )(dqkv, de, h)


def _local_step(x, tgt, mod, g_mix, g_mlp, g_fin, ba, bb, cw8, w_int, mix_weights, mlp_weights, mlp_grads_ready, other_grads_ready):
    S = x.shape[0]
    sh1, sc1, gt1, sh2, sc2, gt2 = [mod[k:k + 1] for k in range(6)]
    bias, bias_t = _bias_table()

    h, qkv, e = _proj(x, g_mix, sc1, sh1, w_int)
    qkv = qkv.reshape(3, 3, S, AOW)
    o_attn, lse = _attn_fwd(qkv, bias)
    w_bat, w_bc, w_out = mix_weights(o_attn)
    o_bf, cbu, ya, yc, merged = _mix(o_attn, e, cw8, ba, bb, w_bat, w_bc)
    x1, mo, h2 = _out_proj(merged, w_out, x, gt1, g_mlp, sc2, sh2)
    w_mit, w_mo = mlp_weights(x1)
    a, f = _mlp_in(h2, w_mit)
    mlp, dx2, pv_f = _mlp_out(f, w_mo, x1, gt2, g_fin, tgt)

    da, dmo2, pv_a = _bwd_mlp_a(dx2, gt2, mlp, w_mo, a)
    dx1, pv_b = _bwd_mlp_b(da, w_mit, x1, dx2, g_mlp, sc2)
    zero = mlp_grads_ready(_grad_w("grad_w_mi", da, h2), _grad_w("grad_w_mo", f, dmo2))
    dmo, dya, dyc, do, dl, de, pv_m = _bwd_mix(dx1, gt1 + zero, mo, e, cw8, ba, bb, ya, yc, o_attn, w_out, w_bc, w_bat)
    dqkv = _attn_bwd(qkv, do, lse, dl, bias_t).reshape(9, S, AOW)
    zero = other_grads_ready(_grad_w_in(dqkv, de, h), _grad_w("grad_w_ba", dya, o_bf), _grad_w("grad_w_bc", cbu, dyc),
                             _grad_w("grad_w_out", merged, dmo))
    grad_x, pv_i = _bwd_in(dqkv, de, w_int, x, dx1, g_mix, sc1 + zero)

    vec = jnp.concatenate([pv_i[0:2], pv_m[0:1], pv_b[0:2], pv_a[0:1], pv_i[2:3], pv_b[2:3], pv_f[0:1],
                           pv_m[1:3], pv_m[3:6], pv_f[1:2], jnp.zeros((1, D), F32)], axis=0)
    return grad_x, vec


def _my_place():
    return lax.axis_index("x"), lax.axis_index("y"), lax.axis_index("c")


def _dev_index(px, py, pc):
    return 4 * px + 2 * py + pc


def _allgather_weights(shards):
    nw = len(shards)
    HBM = pl.BlockSpec(memory_space=pl.ANY)

    def body(*refs):
        sh, full = refs[:nw], refs[nw:2 * nw]
        send_sems, recv_sems, local_sems = refs[2 * nw:]
        x, y, c = _my_place()
        me, sibling = (x, y, c), (x, y, 1 - c)
        chips = [(1 - x, y), (x, 1 - y), (1 - x, 1 - y)]

        def rows(w, px, py, pc):
            r = sh[w].shape[0]
            return full[w].at[pl.ds(pl.multiple_of(_dev_index(px, py, pc) * r, 16), r), :]

        def copy(w, k, block, to, src=None):
            return pltpu.make_async_remote_copy(
                src_ref=rows(w, *block) if src is None else src, dst_ref=rows(w, *block),
                send_sem=send_sems.at[w, k], recv_sem=recv_sems.at[w, k], device_id=to, device_id_type=MESH)

        mine = [pltpu.make_async_copy(sh[w], rows(w, *me), local_sems.at[w]) for w in range(nw)]
        for cp in mine:
            cp.start()
        first = []
        for w in range(nw):
            first.append(copy(w, 0, me, sibling, src=sh[w]))
            first += [copy(w, 1 + j, me, (*chip, c), src=sh[w]) for j, chip in enumerate(chips)]
        for cp in first:
            cp.start()
        passed = []
        for w in range(nw):
            for j, chip in enumerate(chips):
                copy(w, 1 + j, (*chip, c), me).wait_recv()
                fwd = copy(w, 4 + j, (*chip, c), sibling)
                fwd.start()
                passed.append(fwd)
        for w in range(nw):
            copy(w, 0, sibling, me).wait_recv()
            for j, chip in enumerate(chips):
                copy(w, 4 + j, (*chip, 1 - c), me).wait_recv()
        for cp in first + passed:
            cp.wait_send()
        for cp in mine:
            cp.wait()

    return pl.pallas_call(
        body, name="allgather_weights",
        out_shape=[jax.ShapeDtypeStruct((N_DEV * s.shape[0], s.shape[1]), s.dtype) for s in shards],
        in_specs=[HBM] * nw, out_specs=[HBM] * nw,
        scratch_shapes=[pltpu.SemaphoreType.DMA((nw, 7)), pltpu.SemaphoreType.DMA((nw, 7)), pltpu.SemaphoreType.DMA((nw,))],
    )(*shards)


def _peer(x, y, c, m):
    return (x ^ ((m >> 2) & 1), y ^ ((m >> 1) & 1), c ^ (m & 1))


HBM_SPEC = pl.BlockSpec(memory_space=pltpu.HBM)
SEM_SPEC = pl.BlockSpec(memory_space=pltpu.SEMAPHORE)
N_PEER = N_DEV - 1


SPLIT_MASKS = {"gather": tuple(range(1, N_DEV)), "scatter": tuple(range(1, N_DEV)), "chips": (2, 4, 6)}


def _split_copy(mode, src_ref, land_ref, send_sems, recv_sems, w, j, place, arriving=False):
    x, y, c = place
    masks = SPLIT_MASKS[mode]
    peer = _peer(x, y, c, masks[j])
    k = w * len(masks) + j
    sender, receiver = ((peer, (x, y, c)) if arriving else ((x, y, c), peer))
    if mode == "gather":
        r = src_ref.shape[0]
        src, dst = src_ref, land_ref.at[pl.ds(pl.multiple_of(_dev_index(*sender) * r, 16), r), :]
    elif mode == "scatter":
        r = land_ref.shape[1]
        src, dst = src_ref.at[pl.ds(pl.multiple_of(_dev_index(*receiver) * r, 16), r), :], land_ref.at[j]
    else:
        src, dst = src_ref.at[2 * receiver[0] + receiver[1]], land_ref.at[j]
    return pltpu.make_async_remote_copy(src_ref=src, dst_ref=dst, send_sem=send_sems.at[k], recv_sem=recv_sems.at[k],
                                        device_id=peer, device_id_type=MESH)


def _split_start(name, mode, srcs, lands):
    n = len(srcs)
    nm = len(SPLIT_MASKS[mode])

    def body(*refs):
        src, land = refs[:n], refs[n:2 * n]
        send_sems, recv_sems = refs[2 * n], refs[2 * n + 1]
        token = refs[-1]
        place = _my_place()
        for w in range(n):
            for j in range(nm):
                _split_copy(mode, src[w], land[w], send_sems, recv_sems, w, j, place).start()
        token[...] = jnp.zeros_like(token)

    hbm = lambda t: pltpu.HBM(t.shape, t.dtype)
    out = pl.pallas_call(
        body, name=name,
        out_shape=(pltpu.SemaphoreType.DMA((n * nm,)), pltpu.SemaphoreType.DMA((n * nm,)), *[hbm(t) for t in srcs],
                   *[hbm(t) for t in lands], jax.ShapeDtypeStruct((8, 128), F32)),
        in_specs=(HBM_SPEC,) * (2 * n),
        out_specs=(SEM_SPEC, SEM_SPEC) + (HBM_SPEC,) * (2 * n) + (pl.BlockSpec(memory_space=pltpu.VMEM),),
        input_output_aliases={i: 2 + i for i in range(2 * n)},
        compiler_params=pltpu.CompilerParams(has_side_effects=pltpu.SideEffectType.DATAFLOW_SIDE_EFFECTING),
    )(*[pltpu.with_memory_space_constraint(t, pltpu.HBM) for t in (*srcs, *lands)])
    return out[0], out[1], out[2:2 + n], out[2 + n:2 + 2 * n], out[-1][0:1, 0:1]


def _split_wait(name, mode, send_sems, recv_sems, srcs, lands, after):
    n = len(srcs)

    def body(*refs):
        src, land = refs[:n], refs[n:2 * n]
        ssem, rsem = refs[2 * n], refs[2 * n + 1]
        place = _my_place()
        for w in range(n):
            for j in range(len(SPLIT_MASKS[mode])):
                _split_copy(mode, src[w], land[w], ssem, rsem, w, j, place).wait_send()
                _split_copy(mode, src[w], land[w], ssem, rsem, w, j, place, arriving=True).wait_recv()

    hbm = lambda t: pltpu.HBM(t.shape, t.dtype)
    out = pl.pallas_call(
        body, name=name,
        out_shape=tuple(hbm(t) for t in (*srcs, *lands)),
        in_specs=(HBM_SPEC,) * (2 * n) + (SEM_SPEC, SEM_SPEC, pl.BlockSpec(memory_space=pl.ANY)),
        out_specs=(HBM_SPEC,) * (2 * n),
        input_output_aliases={i: i for i in range(2 * n)},
        compiler_params=pltpu.CompilerParams(has_side_effects=pltpu.SideEffectType.DATAFLOW_SIDE_EFFECTING),
    )(*srcs, *lands, send_sems, recv_sems, after)
    return out[:n], out[n:]


def _sibling_exchange(grads):
    nw = len(grads)
    HBM = pl.BlockSpec(memory_space=pl.ANY)

    def body(*refs):
        g, land = refs[:nw], refs[nw:2 * nw]
        send_sems, recv_sems = refs[2 * nw:]
        x, y, c = _my_place()

        def copy(w, q, owner_core):
            r = land[w].shape[1]
            return pltpu.make_async_remote_copy(
                src_ref=g[w].at[pl.ds(pl.multiple_of((2 * q + owner_core) * r, 16), r), :], dst_ref=land[w].at[q],
                send_sem=send_sems.at[w, q], recv_sem=recv_sems.at[w, q], device_id=(x, y, 1 - c), device_id_type=MESH)

        sends = [copy(w, q, 1 - c) for w in range(nw) for q in range(4)]
        for cp in sends:
            cp.start()
        for w in range(nw):
            for q in range(4):
                copy(w, q, c).wait_recv()
        for cp in sends:
            cp.wait_send()

    return pl.pallas_call(
        body, name="sibling_exchange",
        out_shape=[jax.ShapeDtypeStruct((4, a.shape[0] // N_DEV, a.shape[1]), a.dtype) for a in grads],
        in_specs=[HBM] * nw, out_specs=[HBM] * nw,
        scratch_shapes=[pltpu.SemaphoreType.DMA((nw, 4)), pltpu.SemaphoreType.DMA((nw, 4))],
    )(*grads)


def _pair_sum(g, sib, core, name):
    _, r, ccols = sib.shape
    tr = _row_tile(r)

    def body(core_ref, g_ref, s_ref, o_ref):
        o_ref[0] = (g_ref[0, 0].astype(F32) + s_ref[0].astype(F32)).astype(BF16)

    return pl.pallas_call(
        body, name=name,
        grid_spec=pltpu.PrefetchScalarGridSpec(
            num_scalar_prefetch=1, grid=(4, r // tr),
            in_specs=[pl.BlockSpec((1, 1, tr, ccols), lambda q, i, core_ref: (q, core_ref[0], i, 0)),
                      pl.BlockSpec((1, tr, ccols), lambda q, i, core_ref: (q, i, 0))],
            out_specs=pl.BlockSpec((1, tr, ccols), lambda q, i, core_ref: (q, i, 0))),
        out_shape=jax.ShapeDtypeStruct(sib.shape, BF16),
        compiler_params=_cparams("parallel", "parallel"),
    )(core, g.reshape(4, 2, r, ccols), sib)


def _allgather_small(v, name):
    r, ccols = v.shape

    def body(v_ref, out_ref, send_sems, recv_sems):
        x, y, c = _my_place()
        my_idx = _dev_index(x, y, c)
        out_ref[my_idx] = v_ref[...]

        def copy(m):
            peer = _peer(x, y, c, m)
            return pltpu.make_async_remote_copy(
                src_ref=v_ref, dst_ref=out_ref.at[my_idx],
                send_sem=send_sems.at[m - 1], recv_sem=recv_sems.at[m - 1], device_id=peer, device_id_type=MESH)

        def arrival(m):
            peer = _peer(x, y, c, m)
            return pltpu.make_async_remote_copy(
                src_ref=v_ref, dst_ref=out_ref.at[_dev_index(*peer)],
                send_sem=send_sems.at[m - 1], recv_sem=recv_sems.at[m - 1], device_id=peer, device_id_type=MESH)

        sends = [copy(m) for m in range(1, N_DEV)]
        for cp in sends:
            cp.start()
        for m in range(1, N_DEV):
            arrival(m).wait_recv()
        for cp in sends:
            cp.wait_send()

    return pl.pallas_call(
        body, name=name,
        out_shape=jax.ShapeDtypeStruct((N_DEV, r, ccols), v.dtype),
        in_specs=[pl.BlockSpec(memory_space=pltpu.VMEM)], out_specs=pl.BlockSpec(memory_space=pltpu.VMEM),
        scratch_shapes=[pltpu.SemaphoreType.DMA((7,)), pltpu.SemaphoreType.DMA((7,))],
    )(v)


def _conditioning(pay, w_ada, b_cols):
    ncol = w_ada.shape[1]

    def body(pay_ref, w_ref, b_ref, got_ref, act_ref, mod_ref, send_sems, recv_sems):
        x, y, c = _my_place()
        my_idx = _dev_index(x, y, c)

        def copy(rnd, buf, m, arriving=False):
            peer = _peer(x, y, c, m)
            slot = _dev_index(*peer) if arriving else my_idx
            return pltpu.make_async_remote_copy(
                src_ref=buf.at[my_idx], dst_ref=buf.at[slot], send_sem=send_sems.at[rnd, m - 1],
                recv_sem=recv_sems.at[rnd, m - 1], device_id=peer, device_id_type=MESH)

        def exchange(rnd, buf):
            sends = [copy(rnd, buf, m) for m in range(1, N_DEV)]
            for cp in sends:
                cp.start()
            for m in range(1, N_DEV):
                copy(rnd, buf, m, arriving=True).wait_recv()
            for cp in sends:
                cp.wait_send()

        got_ref[my_idx] = pay_ref[...]
        exchange(0, got_ref)
        cv = jnp.concatenate([got_ref[s, 0:1, :] for s in range(N_DEV)], axis=0)
        act = cv * _sigmoid(cv)
        act_ref[...] = act
        mod_ref[my_idx] = jnp.dot(act, w_ref[...], preferred_element_type=F32, precision=lax.Precision.HIGHEST) + b_ref[...]
        exchange(1, mod_ref)

    vmem = pl.BlockSpec(memory_space=pltpu.VMEM)
    return pl.pallas_call(
        body, name="conditioning",
        out_shape=[jax.ShapeDtypeStruct((N_DEV, 8, D), F32), jax.ShapeDtypeStruct((N_DEV, D), F32),
                   jax.ShapeDtypeStruct((N_DEV, N_DEV, ncol), F32)],
        in_specs=[vmem] * 3, out_specs=[vmem] * 3,
        scratch_shapes=[pltpu.SemaphoreType.DMA((2, 7)), pltpu.SemaphoreType.DMA((2, 7))],
        compiler_params=_cparams(),
    )(pay, w_ada, b_cols)


def _ada_bwd(act_t, gm_cols):
    def body(a_ref, g_ref, o_ref):
        o_ref[...] = jnp.dot(a_ref[...], g_ref[...], preferred_element_type=F32, precision=lax.Precision.HIGHEST)

    return pl.pallas_call(
        body, name="ada_bwd", out_shape=jax.ShapeDtypeStruct((D, gm_cols.shape[1]), F32), compiler_params=_cparams(),
    )(act_t, gm_cols)


def _row_tile(r):
    for t in (256, 304, 128, 64, 16):
        if r % t == 0:
            return t
    return r


def _sum_parts(parts, name, own=None):
    k, r, ccols = parts.shape
    tr = _row_tile(r)

    def body(*refs):
        p_ref, o_ref = refs[0], refs[-1]
        acc = p_ref[0].astype(F32) if own is None else refs[1][...].astype(F32) + p_ref[0].astype(F32)
        for s in range(1, k):
            acc = acc + p_ref[s].astype(F32)
        o_ref[...] = acc

    blk = pl.BlockSpec((tr, ccols), lambda i: (i, 0))
    return pl.pallas_call(
        body, name=name, grid=(r // tr,),
        in_specs=[pl.BlockSpec((k, tr, ccols), lambda i: (0, i, 0))] + ([] if own is None else [blk]),
        out_specs=blk,
        out_shape=jax.ShapeDtypeStruct((r, ccols), F32),
        compiler_params=_cparams("parallel"),
    )(*((parts,) if own is None else (parts, own)))


def _adamw(w, g, m, v, name):
    r, ccols = w.shape
    tr = _row_tile(r)
    c1 = 1.0 / (1.0 - B1 ** STEP)
    c2 = 1.0 / (1.0 - B2 ** STEP)

    def body(w_ref, g_ref, m_ref, v_ref, d_ref, nm_ref, nv_ref):
        gv = g_ref[...]
        nm = B1 * m_ref[...] + (1.0 - B1) * gv
        nv = B2 * v_ref[...] + (1.0 - B2) * jnp.square(gv)
        nm_ref[...] = nm
        nv_ref[...] = nv
        d_ref[...] = -LR * ((nm * c1) / (jnp.sqrt(nv * c2) + ADAM_EPS) + WD * w_ref[...])

    blk = pl.BlockSpec((tr, ccols), lambda i: (i, 0))
    return pl.pallas_call(
        body, name=name, grid=(r // tr,), in_specs=[blk] * 4, out_specs=[blk] * 3,
        out_shape=[jax.ShapeDtypeStruct((r, ccols), F32)] * 3,
        compiler_params=_cparams("parallel"),
    )(w, g, m, v)


def _sum_adamw(parts, own, w, m, v, name):
    k, r, ccols = parts.shape
    tr = _row_tile(r)
    c1 = 1.0 / (1.0 - B1 ** STEP)
    c2 = 1.0 / (1.0 - B2 ** STEP)

    def body(p_ref, own_ref, w_ref, m_ref, v_ref, g_ref, d_ref, nm_ref, nv_ref):
        gv = own_ref[...].astype(F32)
        for s in range(k):
            gv = gv + p_ref[s].astype(F32)
        g_ref[...] = gv
        nm = B1 * m_ref[...] + (1.0 - B1) * gv
        nv = B2 * v_ref[...] + (1.0 - B2) * jnp.square(gv)
        nm_ref[...] = nm
        nv_ref[...] = nv
        d_ref[...] = -LR * ((nm * c1) / (jnp.sqrt(nv * c2) + ADAM_EPS) + WD * w_ref[...])

    blk = pl.BlockSpec((tr, ccols), lambda i: (i, 0))
    return pl.pallas_call(
        body, name=name, grid=(r // tr,),
        in_specs=[pl.BlockSpec((k, tr, ccols), lambda i: (0, i, 0))] + [blk] * 4, out_specs=[blk] * 4,
        out_shape=[jax.ShapeDtypeStruct((r, ccols), F32)] * 4,
        compiler_params=_cparams("parallel"),
    )(parts, own, w, m, v)


VEC_ROWS = ((0, 6), (6, 7), (9, 11), (11, 14), (7, 8), (8, 9))


def _adamw_vectors(w, g, m, v):
    c1 = 1.0 / (1.0 - B1 ** STEP)
    c2 = 1.0 / (1.0 - B2 ** STEP)

    def put(refs, p):
        for ref, (lo, hi) in zip(refs, VEC_ROWS):
            if ref.shape == (3, HEAD):
                ref[...] = p[lo:hi, :HEAD]
            else:
                ref[...] = jnp.concatenate([p[k:k + 1] for k in range(lo, hi)], axis=1)

    def body(w_ref, g_ref, m_ref, v_ref, *outs):
        gv = g_ref[...]
        nm = B1 * m_ref[...] + (1.0 - B1) * gv
        nv = B2 * v_ref[...] + (1.0 - B2) * jnp.square(gv)
        delta = -LR * ((nm * c1) / (jnp.sqrt(nv * c2) + ADAM_EPS) + WD * w_ref[...])
        for kind, p in enumerate((gv, delta, nm, nv)):
            put(outs[6 * kind:6 * kind + 6], p)

    shapes = [(1, 6 * D), (1, D), (1, 2 * D), (3, HEAD), (1, D), (1, D)]
    out = pl.pallas_call(
        body, name="adamw_vectors", out_shape=[jax.ShapeDtypeStruct(sh, F32) for sh in shapes] * 4, compiler_params=_cparams(),
    )(w, g, m, v)
    fix = lambda t: (t[0], t[1], t[2], t[3][None], t[4], t[5].reshape(D))
    return [fix(out[6 * kind:6 * kind + 6]) for kind in range(4)]


def _pack_vectors(b_ada, g_mix, g_mlp, g_fin, b_gate, conv_w):
    conv_rows = jnp.pad(conv_w.reshape(3, HEAD), ((0, 0), (0, D - HEAD)))
    return jnp.concatenate([b_ada.reshape(6, D), g_mix.reshape(1, D), g_mlp.reshape(1, D), g_fin.reshape(1, D),
                            b_gate.reshape(2, D), conv_rows, jnp.zeros((2, D), F32)], axis=0)


def kernel(x, c, w_ada, b_ada, g_norm_mix, w_in, b_gate, conv_w, w_branch_attn, w_branch_conv, w_out, g_norm_mlp, w_mlp_in, w_mlp_out, g_norm_final, loss_target, m_w_ada, m_b_ada, m_g_norm_mix, m_w_in, m_b_gate, m_conv_w, m_w_branch_attn, m_w_branch_conv, m_w_out, m_g_norm_mlp, m_w_mlp_in, m_w_mlp_out, m_g_norm_final, v_w_ada, v_b_ada, v_g_norm_mix, v_w_in, v_b_gate, v_conv_w, v_w_branch_attn, v_w_branch_conv, v_w_out, v_g_norm_mlp, v_w_mlp_in, v_w_mlp_out, v_g_norm_final):
    S = x.shape[1]
    xi, yi, ci = _my_place()
    me = _dev_index(xi, yi, ci)
    x2 = x.reshape(S, D)
    tgt = loss_target.reshape(S, D)

    pay = jnp.zeros((8, D), F32).at[0].set(c[0]).at[1:4, :HEAD].set(conv_w[0])
    ncol = w_ada.shape[2]
    b_cols = lax.dynamic_slice(b_ada, (0, me * ncol), (1, ncol))
    got, act, mod_all = _conditioning(pay, w_ada[0], b_cols)
    cw8 = jnp.pad(got[:, 1:4, :HEAD].transpose(1, 0, 2).reshape(3, D), ((0, 5), (0, 0)))

    w_in_shard, mod_all = lax.optimization_barrier((w_in[0].T.astype(BF16), mod_all))
    mod = lax.dynamic_index_in_dim(mod_all, me, axis=1, keepdims=False).reshape(6, D)
    (w_int,) = _allgather_weights([w_in_shard])
    late = [w_branch_attn[0].T.astype(BF16), w_branch_conv[0].astype(BF16), w_out[0].astype(BF16),
            w_mlp_in[0].T.astype(BF16), w_mlp_out[0].astype(BF16)]
    w_int, late = lax.optimization_barrier((w_int, late))
    zones = [lax.dynamic_update_slice(lax.empty((N_DEV * t.shape[0], t.shape[1]), BF16), t, (me * t.shape[0], 0)) for t in late]
    ag_mix = _split_start("gather_mix_start", "gather", late[:3], zones[:3])
    ag_mlp = _split_start("gather_mlp_start", "gather", late[3:], zones[3:])

    def mix_weights(o_attn):
        return _split_wait("gather_mix_wait", "gather", *ag_mix[:4], o_attn)[1]

    def mlp_weights(x1):
        return _split_wait("gather_mlp_wait", "gather", *ag_mlp[:4], x1)[1]

    rs = {}

    def mlp_grads_ready(*grads):
        lands = [lax.empty((N_PEER, t.shape[0] // N_DEV, t.shape[1]), BF16) for t in grads]
        rs["mlp"] = _split_start("scatter_mlp_start", "scatter", grads, lands)
        return rs["mlp"][4]

    def other_grads_ready(*grads):
        core = ci.reshape(1).astype(jnp.int32)
        pair = [_pair_sum(g, sib, core, "pair_sum_%d" % k) for k, (g, sib) in enumerate(zip(grads, _sibling_exchange(grads)))]
        lands = [lax.empty((3,) + t.shape[1:], BF16) for t in pair]
        rs["rest"] = _split_start("scatter_rest_start", "chips", pair, lands)
        return rs["rest"][4]

    ba, bb = b_gate[:, :D], b_gate[:, D:]
    grad_x, vec = _local_step(
        x2, tgt, mod + ag_mix[4] + ag_mlp[4], g_norm_mix, g_norm_mlp, g_norm_final.reshape(1, D), ba, bb, cw8, w_int, mix_weights, mlp_weights,
        mlp_grads_ready, other_grads_ready)

    vec_all = _allgather_small(vec, "gather_vec")
    vec_sum = _sum_parts(vec_all, "sum_vec")
    loss = vec_sum[14, 0]
    gm_all = vec_all[:, 0:6, :].reshape(N_DEV, 6 * D)
    gm_cols = lax.dynamic_slice(gm_all, (0, me * ncol), (N_DEV, ncol))
    g_w_ada = _ada_bwd(act.T, gm_cols)
    conv_cols = lax.dynamic_slice(vec_sum[11:14], (0, me * HEAD), (3, HEAD))
    g_pack = jnp.concatenate([vec_sum[0:11], jnp.pad(conv_cols, ((0, 0), (0, D - HEAD))), jnp.zeros((2, D), F32)], axis=0)
    packs = [_pack_vectors(*t) for t in ((b_ada, g_norm_mix, g_norm_mlp, g_norm_final, b_gate, conv_w),
                                         (m_b_ada, m_g_norm_mix, m_g_norm_mlp, m_g_norm_final, m_b_gate, m_conv_w),
                                         (v_b_ada, v_g_norm_mix, v_g_norm_mlp, v_g_norm_final, v_b_gate, v_conv_w))]
    gv, dv, mv, vv = _adamw_vectors(packs[0], g_pack, packs[1], packs[2])
    d_ada, nm_ada, nv_ada = _adamw(w_ada[0], g_w_ada, m_w_ada[0], v_w_ada[0], "adamw_w_ada")

    big = {}
    srcs, lands = _split_wait("scatter_mlp_wait", "scatter", *rs["mlp"][:4], grad_x)
    own = [lax.dynamic_slice(g, (me * land.shape[1], 0), land.shape[1:]) for g, land in zip(srcs, lands)]
    g_mi = _sum_parts(lands[0], "sum_w_mi", own=own[0]).T
    big["w_mi"] = (g_mi[None],) + tuple(t[None] for t in _adamw(w_mlp_in[0], g_mi, m_w_mlp_in[0], v_w_mlp_in[0], "adamw_w_mi"))
    big["w_mo"] = tuple(t[None] for t in _sum_adamw(lands[1], own[1], w_mlp_out[0], m_w_mlp_out[0], v_w_mlp_out[0], "adamw_w_mo"))
    srcs, lands = _split_wait("scatter_rest_wait", "chips", *rs["rest"][:4], grad_x)
    own = [lax.dynamic_index_in_dim(pair, 2 * xi + yi, axis=0, keepdims=False) for pair in srcs]
    big["w_in"] = tuple(t.T[None] for t in _sum_adamw(lands[0], own[0], w_in[0].T, m_w_in[0].T, v_w_in[0].T, "adamw_w_in"))
    g_ba = _sum_parts(lands[1], "sum_w_ba", own=own[1]).T
    big["w_ba"] = (g_ba[None],) + tuple(t[None] for t in _adamw(w_branch_attn[0], g_ba, m_w_branch_attn[0], v_w_branch_attn[0], "adamw_w_ba"))
    big["w_bc"] = tuple(t[None] for t in _sum_adamw(lands[2], own[2], w_branch_conv[0], m_w_branch_conv[0], v_w_branch_conv[0], "adamw_w_bc"))
    big["w_out"] = tuple(t[None] for t in _sum_adamw(lands[3], own[3], w_out[0], m_w_out[0], v_w_out[0], "adamw_w_out"))

    def ordered(k, ada, vecs):
        return (ada[None], vecs[0], vecs[1], big["w_in"][k], vecs[2], vecs[3], big["w_ba"][k], big["w_bc"][k],
                big["w_out"][k], vecs[4], big["w_mi"][k], big["w_mo"][k], vecs[5])

    return (loss, grad_x.reshape(1, S, D), *ordered(0, g_w_ada, gv), *ordered(1, d_ada, dv),
            *ordered(2, nm_ada, mv), *ordered(3, nv_ada, vv))
```

```python
import functools

import numpy as np
import jax
import jax.numpy as jnp
from jax import lax
from jax.experimental import pallas as pl
from jax.experimental.pallas import tpu as pltpu

F32, BF16 = jnp.float32, jnp.bfloat16
D = 1024
HEAD = 128
DILATIONS = (1, 4, 16)
N_SLOT = 4
AOW = N_SLOT * HEAD
DFF = 4 * D
N_DEV = 8
UNROLL = 8
EPS = 1e-6
NEG = -1e30
SCALE = HEAD ** -0.5
LR, B1, B2, ADAM_EPS, WD, STEP = 0.001, 0.9, 0.999, 1e-08, 0.01, 10
V7X_VMEM_LIMIT = 56 * 1024 * 1024
TM = 1024
MESH = pl.DeviceIdType.MESH
AXES = ("x", "y", "c")


def _cparams(*sem):
    if sem:
        return pltpu.CompilerParams(dimension_semantics=sem, vmem_limit_bytes=V7X_VMEM_LIMIT)
    return pltpu.CompilerParams(vmem_limit_bytes=V7X_VMEM_LIMIT)


def _nn(a, b):
    return jnp.dot(a, b, preferred_element_type=F32)


def _nt(a, b):
    return lax.dot_general(a, b, (((1,), (1,)), ((), ())), preferred_element_type=F32)


def _tn(a, b):
    return lax.dot_general(a, b, (((0,), (0,)), ((), ())), preferred_element_type=F32)


def _rms_r(x):
    return lax.rsqrt(jnp.mean(x * x, axis=-1, keepdims=True) + EPS)


def _rms_bwd(x, r, g, dn):
    gy = dn * g
    dx = r * gy - x * (r * r * r) * jnp.mean(x * gy, axis=-1, keepdims=True)
    return dx, dn * (x * r)


def _sigmoid(t):
    return 1.0 / (1.0 + jnp.exp(-t))


def _rowsum(v):
    return jnp.sum(v, axis=0, keepdims=True)


def _vec_spec(n=D):
    return pl.BlockSpec((1, n), lambda *_: (0, 0))


def _const_spec(shape):
    nd = len(shape)
    return pl.BlockSpec(shape, lambda *_: (0,) * nd)


def _win_rowblock(j):
    return jnp.where(j < 9, (j % 3) * 3 + j // 3, j)


def _prenorm(x, g, sc, sh):
    S = x.shape[0]
    tm = TM

    def body(x_ref, g_ref, sc_ref, sh_ref, h_ref):
        xv = x_ref[...]
        h_ref[...] = (xv * _rms_r(xv) * g_ref[...] * (1.0 + sc_ref[...]) + sh_ref[...]).astype(BF16)

    row = pl.BlockSpec((tm, D), lambda i: (i, 0))
    return pl.pallas_call(
        body, name="prenorm", grid=(S // tm,), in_specs=[row, _vec_spec(), _vec_spec(), _vec_spec()], out_specs=row,
        out_shape=jax.ShapeDtypeStruct((S, D), BF16), compiler_params=_cparams("parallel"),
    )(x, g, sc, sh)


def _proj(h, w_int):
    S = h.shape[0]

    def body(h_ref, w_ref, q_ref, e_ref):
        j = pl.program_id(0)
        acc = _nt(h_ref[...], w_ref[...])

        @pl.when(j < 9)
        def _():
            q_ref[0] = acc

        @pl.when(j >= 9)
        def _():
            e_ref[0] = acc.astype(BF16)

    def e_idx(j):
        k = jnp.maximum(j - 9, 0)
        return (k // 2, 0, k % 2)

    return pl.pallas_call(
        body, name="proj", grid=(19,),
        in_specs=[pl.BlockSpec((S, D), lambda j: (0, 0), pipeline_mode=pl.Buffered(1)),
                  pl.BlockSpec((512, D), lambda j: (_win_rowblock(j), 0))],
        out_specs=[pl.BlockSpec((1, S, 512), lambda j: (jnp.minimum(j, 8), 0, 0)), pl.BlockSpec((1, S, 512), e_idx)],
        out_shape=[jax.ShapeDtypeStruct((9, S, 512), F32), jax.ShapeDtypeStruct((5, S, D), BF16)],
        compiler_params=_cparams("arbitrary"),
    )(h, w_int)


def _bias_table():
    slopes = (2.0 ** (-8.0 * np.arange(1, 13, dtype=np.float32) / 12.0)).astype(np.float32)
    qi = np.arange(HEAD)[:, None]
    kj = np.arange(2 * HEAD)[None, :]
    delta = HEAD + qi - kj
    mask = (delta >= 0) & (delta <= HEAD)
    out = np.zeros((3, N_SLOT, HEAD, 2 * HEAD), np.float32)
    for gi, d in enumerate(DILATIONS):
        for j in range(N_SLOT):
            bias = -slopes[gi * N_SLOT + j] * (delta * d).astype(np.float32)
            out[gi, j] = np.where(mask, bias, NEG)
    out_t = np.concatenate([out[..., HEAD:].swapaxes(-1, -2), out[..., :HEAD].swapaxes(-1, -2)], axis=-1)
    return jnp.asarray(out), jnp.asarray(out_t)


def _block_rows(b, d):
    r = b % d
    n = b // d
    st = n * (HEAD * d) + r
    stp = jnp.maximum(n - 1, 0) * (HEAD * d) + r
    return n, st, stp


def _attn_fwd(qkv, bias):
    S = qkv.shape[2]
    nblk = S // HEAD
    rows = 256

    def body(qkv_ref, b_ref, o_ref, lse_ref, o_s, lse_s):
        g = pl.program_id(1)
        bias = b_ref[0, 0]
        col = lax.broadcasted_iota(jnp.int32, bias.shape, 1)
        bias_first = jnp.where(col < HEAD, NEG, bias)

        for gi, d in enumerate(DILATIONS):
            @pl.when(g == gi)
            def _(gi=gi, d=d):
                def step(b, carry):
                    n, st, stp = _block_rows(b, d)
                    cur = pl.ds(st, HEAD, stride=d)
                    prv = pl.ds(stp, HEAD, stride=d)
                    q = qkv_ref.at[0, 0][cur, :].astype(BF16)
                    kw = jnp.concatenate([qkv_ref.at[0, 1][prv, :], qkv_ref.at[0, 1][cur, :]], axis=0).astype(BF16)
                    vw = jnp.concatenate([qkv_ref.at[0, 2][prv, :], qkv_ref.at[0, 2][cur, :]], axis=0).astype(BF16)
                    s = _nt(q, kw) * SCALE + jnp.where(n > 0, bias, bias_first)
                    m = jnp.max(s, axis=-1, keepdims=True)
                    p = jnp.exp(s - m)
                    l = jnp.sum(p, axis=-1, keepdims=True)
                    o_s.at[gi][cur, :] = _nn(p.astype(BF16), vw) / l
                    lse_s.at[gi][cur, :] = jnp.broadcast_to(m + jnp.log(l), (HEAD, HEAD))
                    return carry

                lax.fori_loop(0, nblk, step, 0, unroll=UNROLL)

        @pl.when(g == len(DILATIONS) - 1)
        def _():
            def merge(i, carry):
                r = pl.ds(pl.multiple_of(i * rows, rows), rows)
                ls = [lse_s[k, r, :] for k in range(3)]
                top = jnp.maximum(jnp.maximum(ls[0], ls[1]), ls[2])
                ws = [jnp.exp(t - top) for t in ls]
                den = ws[0] + ws[1] + ws[2]
                o_ref[r, :] = (ws[0] * o_s[0, r, :] + ws[1] * o_s[1, r, :] + ws[2] * o_s[2, r, :]) / den
                lse_ref[r, :] = top + jnp.log(den)
                return carry

            lax.fori_loop(0, S // rows, merge, 0)

    return pl.pallas_call(
        body, name="attn_fwd", grid=(N_SLOT, 3),
        in_specs=[pl.BlockSpec((1, 3, S, HEAD), lambda j, g: (g, 0, 0, j)),
                  pl.BlockSpec((1, 1, HEAD, 2 * HEAD), lambda j, g: (g, j, 0, 0))],
        out_specs=[pl.BlockSpec((S, HEAD), lambda j, g: (0, j)), pl.BlockSpec((S, HEAD), lambda j, g: (0, j))],
        out_shape=[jax.ShapeDtypeStruct((S, AOW), F32), jax.ShapeDtypeStruct((S, AOW), F32)],
        scratch_shapes=[pltpu.VMEM((3, S, HEAD), F32)] * 2,
        compiler_params=_cparams("parallel", "arbitrary"),
    )(qkv, bias)


def _shift_down(z, k, halo_rows):
    out = pltpu.roll(z, k, axis=0)
    top = out[:8]
    rid = lax.broadcasted_iota(jnp.int32, top.shape, 0)
    for t in range(k):
        top = jnp.where(rid == t, halo_rows[t], top)
    return jnp.concatenate([top, out[8:]], axis=0)


def _shift_up(z, k, halo_rows):
    n = z.shape[0]
    out = pltpu.roll(z, n - k, axis=0)
    bottom = out[n - 8:]
    rid = lax.broadcasted_iota(jnp.int32, bottom.shape, 0)
    for t in range(k):
        bottom = jnp.where(rid == 8 - k + t, halo_rows[t], bottom)
    return jnp.concatenate([out[:n - 8], bottom], axis=0)


def _e_spec(chunk, tm):
    return pl.BlockSpec((1, tm, D), lambda i, c=chunk: (c, i, 0))


def _e_prev_spec(chunk, tm):
    return pl.BlockSpec((1, 16, D), lambda i, c=chunk: (c, jnp.maximum(i * (tm // 16) - 1, 0), 0))


def _e_next_spec(chunk, tm, S):
    return pl.BlockSpec((1, 16, D), lambda i, c=chunk: (c, jnp.minimum((i + 1) * (tm // 16), S // 16 - 1), 0))


def _mix(o_attn, e, cw8, ba, bb, w_bat, w_bc):
    S = o_attn.shape[0]
    tm = 256

    def body(o_ref, cb_ref, cc_ref, cx_ref, ga_ref, gb_ref, ccp_ref, cxp_ref, cw_ref, ba_ref, bb_ref, wba_ref, wbc_ref,
             obf_ref, cbu_ref, ya_ref, yc_ref, mg_ref):
        i = pl.program_id(0)
        o = o_ref[...].astype(BF16)
        obf_ref[...] = o
        ya = _nt(o, wba_ref[...])
        z = cc_ref[0].astype(F32) * cx_ref[0].astype(F32)
        zp = ccp_ref[0].astype(F32) * cxp_ref[0].astype(F32) * (i > 0).astype(F32)
        z1 = _shift_down(z, 1, [zp[15:16]])
        z2 = _shift_down(z, 2, [zp[14:15], zp[15:16]])
        cw = cw_ref[...]
        u = cw[0:1] * z2 + cw[1:2] * z1 + cw[2:3] * z
        cbu = (cb_ref[0].astype(F32) * u).astype(BF16)
        cbu_ref[...] = cbu
        yc = _nn(cbu, wbc_ref[...])
        sa = _sigmoid(ga_ref[0].astype(F32) + ba_ref[...])
        sb = _sigmoid(gb_ref[0].astype(F32) + bb_ref[...])
        ya_ref[...] = ya.astype(BF16)
        yc_ref[...] = yc.astype(BF16)
        mg_ref[...] = (sa * ya + sb * yc).astype(BF16)

    row = lambda w: pl.BlockSpec((tm, w), lambda i: (i, 0))
    return pl.pallas_call(
        body, name="mix", grid=(S // tm,),
        in_specs=[row(AOW)] + [_e_spec(c, tm) for c in range(5)] + [_e_prev_spec(1, tm), _e_prev_spec(2, tm),
                  _const_spec((8, D)), _vec_spec(), _vec_spec(), _const_spec((D, AOW)), _const_spec((D, D))],
        out_specs=[row(AOW), row(D), row(D), row(D), row(D)],
        out_shape=[jax.ShapeDtypeStruct((S, AOW), BF16)] + [jax.ShapeDtypeStruct((S, D), BF16)] * 4,
        compiler_params=_cparams("parallel"),
    )(o_attn, e, e, e, e, e, e, e, cw8, ba, bb, w_bat, w_bc)


def _out_proj(merged, w_out, x, gate1, g_mlp, sc2, sh2):
    S = x.shape[0]
    tm = TM

    def body(mg_ref, w_ref, x_ref, gt_ref, g_ref, sc_ref, sh_ref, x1_ref, mo_ref, h2_ref):
        mo = _nn(mg_ref[...], w_ref[...])
        mo_ref[...] = mo.astype(BF16)
        x1 = x_ref[...] + gt_ref[...] * mo
        x1_ref[...] = x1
        h2 = x1 * _rms_r(x1) * g_ref[...] * (1.0 + sc_ref[...]) + sh_ref[...]
        h2_ref[...] = h2.astype(BF16)

    row = pl.BlockSpec((tm, D), lambda i: (i, 0))
    return pl.pallas_call(
        body, name="out_proj", grid=(S // tm,),
        in_specs=[row, _const_spec((D, D)), row, _vec_spec(), _vec_spec(), _vec_spec(), _vec_spec()],
        out_specs=[row, row, row],
        out_shape=[jax.ShapeDtypeStruct((S, D), F32), jax.ShapeDtypeStruct((S, D), BF16), jax.ShapeDtypeStruct((S, D), BF16)],
        compiler_params=_cparams("parallel"),
    )(merged, w_out, x, gate1, g_mlp, sc2, sh2)


def _mlp_in(h2, w_mit):
    S = h2.shape[0]
    tm, tn = TM, 2048

    def body(h_ref, w_ref, a_ref, f_ref):
        a = _nt(h_ref[...], w_ref[...])
        a_ref[...] = a.astype(BF16)
        f_ref[...] = jnp.square(jnp.maximum(a, 0.0)).astype(BF16)

    blk = pl.BlockSpec((tm, tn), lambda i, j: (i, j))
    return pl.pallas_call(
        body, name="mlp_in", grid=(S // tm, DFF // tn),
        in_specs=[pl.BlockSpec((tm, D), lambda i, j: (i, 0)), pl.BlockSpec((tn, D), lambda i, j: (j, 0))],
        out_specs=[blk, blk],
        out_shape=[jax.ShapeDtypeStruct((S, DFF), BF16)] * 2,
        compiler_params=_cparams("parallel", "parallel"),
    )(h2, w_mit)


def _mlp_out(f, w_mo, x1, gate2, g_fin, tgt):
    S = x1.shape[0]
    tm = 512
    half = tm // 2

    def body(f_ref, w_ref, x1_ref, gt_ref, g_ref, t_ref, mlp_ref, dx2_ref, pv_ref):
        @pl.when(pl.program_id(0) == 0)
        def _():
            pv_ref[...] = jnp.zeros_like(pv_ref)

        g = g_ref[...]
        for hs in (pl.ds(0, half), pl.ds(half, half)):
            mlp = _nn(f_ref[hs, :], w_ref[...])
            mlp_ref[hs, :] = mlp.astype(BF16)
            x2 = x1_ref[hs, :] + gt_ref[...] * mlp
            r = _rms_r(x2)
            err = x2 * r * g - t_ref[hs, :]
            dx2, pg = _rms_bwd(x2, r, g, err * (1.0 / D))
            dx2_ref[hs, :] = dx2
            pv_ref[0:1, :] += _rowsum(pg)
            pv_ref[1:2, :] += 0.5 * _rowsum(jnp.mean(err * err, axis=-1, keepdims=True))

    row = pl.BlockSpec((tm, D), lambda i: (i, 0))
    return pl.pallas_call(
        body, name="mlp_out", grid=(S // tm,),
        in_specs=[pl.BlockSpec((tm, DFF), lambda i: (i, 0)), _const_spec((DFF, D)), row, _vec_spec(), _vec_spec(), row],
        out_specs=[row, row, _const_spec((8, D))],
        out_shape=[jax.ShapeDtypeStruct((S, D), BF16), jax.ShapeDtypeStruct((S, D), F32), jax.ShapeDtypeStruct((8, D), F32)],
        compiler_params=_cparams("arbitrary"),
    )(f, w_mo, x1, gate2, g_fin, tgt)


def _bwd_mlp_a(dx2, gate2, mlp, w_mo, a):
    S = dx2.shape[0]
    tm, tn = TM, 2048

    def body(dx_ref, gt_ref, mlp_ref, w_ref, a_ref, da_ref, dmo_ref, pv_ref):
        i, j = pl.program_id(0), pl.program_id(1)

        @pl.when((i == 0) & (j == 0))
        def _():
            pv_ref[...] = jnp.zeros_like(pv_ref)

        @pl.when(j == 0)
        def _():
            dx = dx_ref[...]
            dmo_ref[...] = (dx * gt_ref[...]).astype(BF16)
            pv_ref[0:1, :] += _rowsum(dx * mlp_ref[...].astype(F32))

        df = _nt(dmo_ref[...], w_ref[...])
        da_ref[...] = (df * (2.0 * jnp.maximum(a_ref[...].astype(F32), 0.0))).astype(BF16)

    row = pl.BlockSpec((tm, D), lambda i, j: (i, 0))
    blk = pl.BlockSpec((tm, tn), lambda i, j: (i, j))
    return pl.pallas_call(
        body, name="bwd_mlp_a", grid=(S // tm, DFF // tn),
        in_specs=[row, _vec_spec(), row, pl.BlockSpec((tn, D), lambda i, j: (j, 0)), blk],
        out_specs=[blk, row, _const_spec((8, D))],
        out_shape=[jax.ShapeDtypeStruct((S, DFF), BF16), jax.ShapeDtypeStruct((S, D), BF16), jax.ShapeDtypeStruct((8, D), F32)],
        compiler_params=_cparams("arbitrary", "arbitrary"),
    )(dx2, gate2, mlp, w_mo, a)


def _bwd_mlp_b(da, w_mit, x1, dx2, g_mlp, sc2):
    S = x1.shape[0]
    tm = 512
    half = tm // 2

    def body(da_ref, w_ref, x1_ref, dx2_ref, g_ref, sc_ref, dx1_ref, pv_ref):
        @pl.when(pl.program_id(0) == 0)
        def _():
            pv_ref[...] = jnp.zeros_like(pv_ref)

        g = g_ref[...]
        for hs in (pl.ds(0, half), pl.ds(half, half)):
            dh = _nn(da_ref[hs, :], w_ref[...])
            x1 = x1_ref[hs, :]
            r = _rms_r(x1)
            dxn, pg = _rms_bwd(x1, r, g, dh * (1.0 + sc_ref[...]))
            dx1_ref[hs, :] = dx2_ref[hs, :] + dxn
            pv_ref[0:1, :] += _rowsum(dh)
            pv_ref[1:2, :] += _rowsum(dh * (x1 * r * g))
            pv_ref[2:3, :] += _rowsum(pg)

    row = pl.BlockSpec((tm, D), lambda i: (i, 0))
    return pl.pallas_call(
        body, name="bwd_mlp_b", grid=(S // tm,),
        in_specs=[pl.BlockSpec((tm, DFF), lambda i: (i, 0)), _const_spec((DFF, D)), row, row, _vec_spec(), _vec_spec()],
        out_specs=[row, _const_spec((8, D))],
        out_shape=[jax.ShapeDtypeStruct((S, D), F32), jax.ShapeDtypeStruct((8, D), F32)],
        compiler_params=_cparams("arbitrary"),
    )(da, w_mit, x1, dx2, g_mlp, sc2)


def _bwd_mix(dx1, gate1, mo, e, cw8, ba, bb, ya, yc, o_attn, w_out, w_bc, w_bat):
    S = dx1.shape[0]
    tm = 256
    n_tiles = S // tm

    def body(dx_ref, dxn_ref, gt_ref, mo_ref, cb_ref, cc_ref, cx_ref, ga_ref, gb_ref, cbn_ref, gbn_ref, ccp_ref, cxp_ref,
             cw_ref, ba_ref, bb_ref, ya_ref, yc_ref, o_ref, wout_ref, wbc_ref, wba_ref,
             dmo_ref, dya_ref, dyc_ref, do_ref, dl_ref, de_ref, pv_ref):
        i = pl.program_id(0)

        @pl.when(i == 0)
        def _():
            pv_ref[...] = jnp.zeros_like(pv_ref)

        gate = gt_ref[...]
        bbv = bb_ref[...]

        def conv_branch_grad(dx_rows, gb_rows):
            dmo = (dx_rows * gate).astype(BF16)
            dmg = _nt(dmo, wout_ref[...])
            sb = _sigmoid(gb_rows + bbv)
            dyc = dmg * sb
            return dmo, dmg, sb, dyc, _nt(dyc.astype(BF16), wbc_ref[...])

        dx = dx_ref[...]
        cb = cb_ref[0].astype(F32)
        cc = cc_ref[0].astype(F32)
        cx = cx_ref[0].astype(F32)
        dmo, dmg, sb, dyc, dcbu = conv_branch_grad(dx, gb_ref[0].astype(F32))
        dmo_ref[...] = dmo
        pv_ref[0:1, :] += _rowsum(dx * mo_ref[...].astype(F32))
        sa = _sigmoid(ga_ref[0].astype(F32) + ba_ref[...])
        dya = (dmg * sa).astype(BF16)
        dya_ref[...] = dya
        dyc_ref[...] = dyc.astype(BF16)
        dga = dmg * ya_ref[...].astype(F32) * sa * (1.0 - sa)
        dgb = dmg * yc_ref[...].astype(F32) * sb * (1.0 - sb)
        pv_ref[1:2, :] += _rowsum(dga)
        pv_ref[2:3, :] += _rowsum(dgb)

        do = _nn(dya, wba_ref[...])
        do_ref[...] = do
        prod = do * o_ref[...]
        dl_ref[...] = jnp.concatenate(
            [jnp.broadcast_to(jnp.sum(prod[:, s * HEAD:(s + 1) * HEAD], axis=-1, keepdims=True), (tm, HEAD))
             for s in range(N_SLOT)], axis=1)

        z = cc * cx
        zp = ccp_ref[0].astype(F32) * cxp_ref[0].astype(F32) * (i > 0).astype(F32)
        z1 = _shift_down(z, 1, [zp[15:16]])
        z2 = _shift_down(z, 2, [zp[14:15], zp[15:16]])
        cw = cw_ref[...]
        u = cw[0:1] * z2 + cw[1:2] * z1 + cw[2:3] * z
        du = dcbu * cb
        dcbu_n = conv_branch_grad(dxn_ref[...], gbn_ref[0].astype(F32))[4]
        du_n = dcbu_n * cbn_ref[0].astype(F32) * (i < n_tiles - 1).astype(F32)
        du1 = _shift_up(du, 1, [du_n[0:1]])
        du2 = _shift_up(du, 2, [du_n[0:1], du_n[1:2]])
        dz = cw[2:3] * du + cw[1:2] * du1 + cw[0:1] * du2
        pv_ref[3:4, :] += _rowsum(du * z2)
        pv_ref[4:5, :] += _rowsum(du * z1)
        pv_ref[5:6, :] += _rowsum(du * z)

        de_ref[0] = (dcbu * u).astype(BF16)
        de_ref[1] = (dz * cx).astype(BF16)
        de_ref[2] = (dz * cc).astype(BF16)
        de_ref[3] = dga.astype(BF16)
        de_ref[4] = dgb.astype(BF16)

    row = lambda w: pl.BlockSpec((tm, w), lambda i: (i, 0))
    nxt = pl.BlockSpec((16, D), lambda i: (jnp.minimum((i + 1) * (tm // 16), S // 16 - 1), 0))
    return pl.pallas_call(
        body, name="bwd_mix", grid=(n_tiles,),
        in_specs=[row(D), nxt, _vec_spec(), row(D)] + [_e_spec(c, tm) for c in range(5)]
                 + [_e_next_spec(0, tm, S), _e_next_spec(4, tm, S), _e_prev_spec(1, tm), _e_prev_spec(2, tm),
                    _const_spec((8, D)), _vec_spec(), _vec_spec(), row(D), row(D), row(AOW),
                    _const_spec((D, D)), _const_spec((D, D)), _const_spec((D, AOW))],
        out_specs=[row(D), row(D), row(D), row(AOW), row(AOW), pl.BlockSpec((5, tm, D), lambda i: (0, i, 0)),
                   _const_spec((8, D))],
        out_shape=[jax.ShapeDtypeStruct((S, D), BF16)] * 3 + [jax.ShapeDtypeStruct((S, AOW), F32)] * 2
                  + [jax.ShapeDtypeStruct((5, S, D), BF16), jax.ShapeDtypeStruct((8, D), F32)],
        compiler_params=_cparams("arbitrary"),
    )(dx1, dx1, gate1, mo, e, e, e, e, e, e, e, e, e, cw8, ba, bb, ya, yc, o_attn, w_out, w_bc, w_bat)


def _attn_bwd(qkv, do, lse, dl, bias_t):
    S = qkv.shape[2]
    nblk = S // HEAD

    def body(qkv_ref, do_ref, lse_ref, dl_ref, b_ref, d_ref):
        g = pl.program_id(1)
        bias = b_ref[0, 0]
        col = lax.broadcasted_iota(jnp.int32, bias.shape, 1)
        bias_last = jnp.where(col >= HEAD, NEG, bias)
        eye = (lax.broadcasted_iota(jnp.int32, (HEAD, HEAD), 0) == lax.broadcasted_iota(jnp.int32, (HEAD, HEAD), 1)).astype(F32)

        def as_row(t):
            return jnp.sum(t * eye, axis=0, keepdims=True)

        for gi, d in enumerate(DILATIONS):
            @pl.when(g == gi)
            def _(d=d):
                nb = nblk // d

                def step(b, dq_part):
                    r, n = b // nb, b % nb
                    cur = pl.ds(n * (HEAD * d) + r, HEAD, stride=d)
                    nxt = pl.ds(jnp.minimum(n + 1, nb - 1) * (HEAD * d) + r, HEAD, stride=d)
                    two = lambda ref: jnp.concatenate([ref[cur, :], ref[nxt, :]], axis=0)
                    two_rows = lambda ref: jnp.concatenate([as_row(ref[cur, :]), as_row(ref[nxt, :])], axis=1)
                    q2 = two(qkv_ref.at[0, 0]).astype(BF16)
                    do2 = two(do_ref).astype(BF16)
                    k = qkv_ref.at[0, 1][cur, :].astype(BF16)
                    v = qkv_ref.at[0, 2][cur, :].astype(BF16)
                    s = _nt(k, q2) * SCALE + jnp.where(n < nb - 1, bias, bias_last)
                    p = jnp.exp(s - two_rows(lse_ref))
                    d_ref.at[0, 2][cur, :] = _nn(p.astype(BF16), do2)
                    dp = _nt(v, do2)
                    ds = (p * (dp - two_rows(dl_ref)) * SCALE).astype(BF16)
                    d_ref.at[0, 1][cur, :] = _nn(ds, q2)
                    dq2 = _tn(ds, k)
                    d_ref.at[0, 0][cur, :] = dq2[:HEAD] + jnp.where(n > 0, dq_part, 0.0)
                    return dq2[HEAD:]

                def steps(i, dq_part):
                    for u in range(UNROLL):
                        dq_part = step(i * UNROLL + u, dq_part)
                    return dq_part

                lax.fori_loop(0, nblk // UNROLL, steps, jnp.zeros((HEAD, HEAD), F32))

    col_blk = pl.BlockSpec((S, HEAD), lambda j, g: (0, j))
    qkv_blk = pl.BlockSpec((1, 3, S, HEAD), lambda j, g: (g, 0, 0, j))
    return pl.pallas_call(
        body, name="attn_bwd", grid=(N_SLOT, 3),
        in_specs=[qkv_blk, col_blk, col_blk, col_blk, pl.BlockSpec((1, 1, HEAD, 2 * HEAD), lambda j, g: (g, j, 0, 0))],
        out_specs=qkv_blk,
        out_shape=jax.ShapeDtypeStruct((3, 3, S, AOW), F32),
        compiler_params=_cparams("parallel", "arbitrary"),
    )(qkv, do, lse, dl, bias_t)


def _bwd_in(dqkv, de, w_int, x, dx1, g_mix, sc1):
    S = x.shape[0]
    tm = TM
    dqkv = dqkv.reshape(3, 3, S, AOW)

    def body(dq_ref, de_ref, wq_ref, wk_ref, wv_ref, wa_ref, wb_ref, x_ref, dx1_ref, g_ref, sc_ref, gx_ref, pv_ref):
        acc = gx_ref
        i, k = pl.program_id(0), pl.program_id(1)

        @pl.when((i == 0) & (k == 0))
        def _():
            pv_ref[...] = jnp.zeros_like(pv_ref)

        @pl.when(k == 0)
        def _():
            acc[...] = jnp.zeros_like(acc)

        @pl.when(k < 3)
        def _():
            lhs = jnp.concatenate([dq_ref[0, t].astype(BF16) for t in range(3)], axis=1)
            acc[...] += _nn(lhs, jnp.concatenate([wq_ref[...], wk_ref[...], wv_ref[...]], axis=0))

        @pl.when(k >= 3)
        def _():
            acc[...] += _nn(de_ref[0], jnp.concatenate([wa_ref[...], wb_ref[...]], axis=0))

        @pl.when(k == 7)
        def _():
            dh = acc[...]
            xv = x_ref[...]
            r = _rms_r(xv)
            g = g_ref[...]
            dxn, pg = _rms_bwd(xv, r, g, dh * (1.0 + sc_ref[...]))
            gx_ref[...] = dx1_ref[...] + dxn
            pv_ref[0:1, :] += _rowsum(dh)
            pv_ref[1:2, :] += _rowsum(dh * (xv * r * g))
            pv_ref[2:3, :] += _rowsum(pg)

    grp = lambda k: jnp.minimum(k, 2)
    chunk = lambda k: jnp.maximum(k - 3, 0)
    wblk = lambda f: pl.BlockSpec((512, D), lambda i, k: (f(k), 0))
    row = pl.BlockSpec((tm, D), lambda i, k: (i, 0))
    once = pl.BlockSpec((tm, D), lambda i, k: (i, 0), pipeline_mode=pl.Buffered(1))
    return pl.pallas_call(
        body, name="bwd_in", grid=(S // tm, 8),
        in_specs=[pl.BlockSpec((1, 3, tm, 512), lambda i, k: (grp(k), 0, i, 0)),
                  pl.BlockSpec((1, tm, D), lambda i, k: (chunk(k), i, 0)),
                  wblk(grp), wblk(lambda k: 3 + grp(k)), wblk(lambda k: 6 + grp(k)),
                  wblk(lambda k: 9 + 2 * chunk(k)), wblk(lambda k: 10 + 2 * chunk(k)),
                  once, once, _vec_spec(), _vec_spec()],
        out_specs=[row, _const_spec((8, D))],
        out_shape=[jax.ShapeDtypeStruct((S, D), F32), jax.ShapeDtypeStruct((8, D), F32)],
        compiler_params=_cparams("arbitrary", "arbitrary"),
    )(dqkv, de, w_int, w_int, w_int, w_int, w_int, x, dx1, g_mix, sc1)


def _grad_w(name, a, b):
    S, ka = a.shape
    nb = b.shape[1]

    def body(a_ref, b_ref, o_ref):
        o_ref[...] = _tn(a_ref[...], b_ref[...]).astype(BF16)

    return pl.pallas_call(
        body, name=name, grid=(ka // 512,),
        in_specs=[pl.BlockSpec((S, 512), lambda n: (0, n)), pl.BlockSpec((S, nb), lambda n: (0, 0))],
        out_specs=pl.BlockSpec((512, nb), lambda n: (n, 0)),
        out_shape=jax.ShapeDtypeStruct((ka, nb), BF16),
        compiler_params=_cparams("parallel"),
    )(a, b)


def _grad_w_in(dqkv, de, h):
    S = h.shape[0]

    def body(dq_ref, de_ref, h_ref, o_ref):
        n = pl.program_id(0)

        @pl.when(n < 9)
        def _():
            o_ref[...] = _tn(dq_ref[0].astype(BF16), h_ref[...]).astype(BF16)

        @pl.when(n >= 9)
        def _():
            o_ref[...] = _tn(de_ref[0], h_ref[...]).astype(BF16)

    def e_idx(n):
        kk = jnp.maximum(n - 9, 0)
        return (kk // 2, 0, kk % 2)

    return pl.pallas_call(
        body, name="grad_w_in", grid=(19,),
        in_specs=[pl.BlockSpec((1, S, 512), lambda n: (jnp.minimum(n, 8), 0, 0)), pl.BlockSpec((1, S, 512), e_idx),
                  pl.BlockSpec((S, D), lambda n: (0, 0))],
        out_specs=pl.BlockSpec((512, D), lambda n: (_win_rowblock(n), 0)),
        out_shape=jax.ShapeDtypeStruct((19 * 512, D), BF16),
        compiler_params=_cparams("parallel"),
    )(dqkv, de, h)


def _local_step(x, tgt, mod, g_mix, g_mlp, g_fin, ba, bb, cw8, w_int, mix_weights, mlp_weights, mlp_grads_ready, other_grads_ready):
    S = x.shape[0]
    sh1, sc1, gt1, sh2, sc2, gt2 = [mod[k:k + 1] for k in range(6)]
    bias, bias_t = _bias_table()

    h = _prenorm(x, g_mix, sc1, sh1)
    qkv, e = _proj(h, w_int)
    qkv = qkv.reshape(3, 3, S, AOW)
    o_attn, lse = _attn_fwd(qkv, bias)
    w_bat, w_bc, w_out = mix_weights(o_attn)
    o_bf, cbu, ya, yc, merged = _mix(o_attn, e, cw8, ba, bb, w_bat, w_bc)
    x1, mo, h2 = _out_proj(merged, w_out, x, gt1, g_mlp, sc2, sh2)
    w_mit, w_mo = mlp_weights(x1)
    a, f = _mlp_in(h2, w_mit)
    mlp, dx2, pv_f = _mlp_out(f, w_mo, x1, gt2, g_fin, tgt)

    da, dmo2, pv_a = _bwd_mlp_a(dx2, gt2, mlp, w_mo, a)
    dx1, pv_b = _bwd_mlp_b(da, w_mit, x1, dx2, g_mlp, sc2)
    zero = mlp_grads_ready(_grad_w("grad_w_mi", da, h2), _grad_w("grad_w_mo", f, dmo2))
    dmo, dya, dyc, do, dl, de, pv_m = _bwd_mix(dx1, gt1 + zero, mo, e, cw8, ba, bb, ya, yc, o_attn, w_out, w_bc, w_bat)
    dqkv = _attn_bwd(qkv, do, lse, dl, bias_t).reshape(9, S, AOW)
    zero = other_grads_ready(_grad_w_in(dqkv, de, h), _grad_w("grad_w_ba", dya, o_bf), _grad_w("grad_w_bc", cbu, dyc),
                             _grad_w("grad_w_out", merged, dmo))
    grad_x, pv_i = _bwd_in(dqkv, de, w_int, x, dx1, g_mix, sc1 + zero)

    vec = jnp.concatenate([pv_i[0:2], pv_m[0:1], pv_b[0:2], pv_a[0:1], pv_i[2:3], pv_b[2:3], pv_f[0:1],
                           pv_m[1:3], pv_m[3:6], pv_f[1:2], jnp.zeros((1, D), F32)], axis=0)
    return grad_x, vec


def _my_place():
    return lax.axis_index("x"), lax.axis_index("y"), lax.axis_index("c")


def _dev_index(px, py, pc):
    return 4 * px + 2 * py + pc


def _allgather_weights(shards):
    nw = len(shards)
    HBM = pl.BlockSpec(memory_space=pl.ANY)

    def body(*refs):
        sh, full = refs[:nw], refs[nw:2 * nw]
        send_sems, recv_sems, local_sems = refs[2 * nw:]
        x, y, c = _my_place()
        me, sibling = (x, y, c), (x, y, 1 - c)
        chips = [(1 - x, y), (x, 1 - y), (1 - x, 1 - y)]

        def rows(w, px, py, pc):
            r = sh[w].shape[0]
            return full[w].at[pl.ds(pl.multiple_of(_dev_index(px, py, pc) * r, 16), r), :]

        def copy(w, k, block, to, src=None):
            return pltpu.make_async_remote_copy(
                src_ref=rows(w, *block) if src is None else src, dst_ref=rows(w, *block),
                send_sem=send_sems.at[w, k], recv_sem=recv_sems.at[w, k], device_id=to, device_id_type=MESH)

        mine = [pltpu.make_async_copy(sh[w], rows(w, *me), local_sems.at[w]) for w in range(nw)]
        for cp in mine:
            cp.start()
        first = []
        for w in range(nw):
            first.append(copy(w, 0, me, sibling, src=sh[w]))
            first += [copy(w, 1 + j, me, (*chip, c), src=sh[w]) for j, chip in enumerate(chips)]
        for cp in first:
            cp.start()
        passed = []
        for w in range(nw):
            for j, chip in enumerate(chips):
                copy(w, 1 + j, (*chip, c), me).wait_recv()
                fwd = copy(w, 4 + j, (*chip, c), sibling)
                fwd.start()
                passed.append(fwd)
        for w in range(nw):
            copy(w, 0, sibling, me).wait_recv()
            for j, chip in enumerate(chips):
                copy(w, 4 + j, (*chip, 1 - c), me).wait_recv()
        for cp in first + passed:
            cp.wait_send()
        for cp in mine:
            cp.wait()

    return pl.pallas_call(
        body, name="allgather_weights",
        out_shape=[jax.ShapeDtypeStruct((N_DEV * s.shape[0], s.shape[1]), s.dtype) for s in shards],
        in_specs=[HBM] * nw, out_specs=[HBM] * nw,
        scratch_shapes=[pltpu.SemaphoreType.DMA((nw, 7)), pltpu.SemaphoreType.DMA((nw, 7)), pltpu.SemaphoreType.DMA((nw,))],
    )(*shards)


def _peer(x, y, c, m):
    return (x ^ ((m >> 2) & 1), y ^ ((m >> 1) & 1), c ^ (m & 1))


HBM_SPEC = pl.BlockSpec(memory_space=pltpu.HBM)
SEM_SPEC = pl.BlockSpec(memory_space=pltpu.SEMAPHORE)
N_PEER = N_DEV - 1


SPLIT_MASKS = {"gather": tuple(range(1, N_DEV)), "scatter": tuple(range(1, N_DEV)), "chips": (2, 4, 6)}


def _split_copy(mode, src_ref, land_ref, send_sems, recv_sems, w, j, place, arriving=False):
    x, y, c = place
    masks = SPLIT_MASKS[mode]
    peer = _peer(x, y, c, masks[j])
    k = w * len(masks) + j
    sender, receiver = ((peer, (x, y, c)) if arriving else ((x, y, c), peer))
    if mode == "gather":
        r = src_ref.shape[0]
        src, dst = src_ref, land_ref.at[pl.ds(pl.multiple_of(_dev_index(*sender) * r, 16), r), :]
    elif mode == "scatter":
        r = land_ref.shape[1]
        src, dst = src_ref.at[pl.ds(pl.multiple_of(_dev_index(*receiver) * r, 16), r), :], land_ref.at[j]
    else:
        src, dst = src_ref.at[2 * receiver[0] + receiver[1]], land_ref.at[j]
    return pltpu.make_async_remote_copy(src_ref=src, dst_ref=dst, send_sem=send_sems.at[k], recv_sem=recv_sems.at[k],
                                        device_id=peer, device_id_type=MESH)


def _split_start(name, mode, srcs, lands):
    n = len(srcs)
    nm = len(SPLIT_MASKS[mode])

    def body(*refs):
        src, land = refs[:n], refs[n:2 * n]
        send_sems, recv_sems = refs[2 * n], refs[2 * n + 1]
        token = refs[-1]
        place = _my_place()
        for w in range(n):
            for j in range(nm):
                _split_copy(mode, src[w], land[w], send_sems, recv_sems, w, j, place).start()
        token[...] = jnp.zeros_like(token)

    hbm = lambda t: pltpu.HBM(t.shape, t.dtype)
    out = pl.pallas_call(
        body, name=name,
        out_shape=(pltpu.SemaphoreType.DMA((n * nm,)), pltpu.SemaphoreType.DMA((n * nm,)), *[hbm(t) for t in srcs],
                   *[hbm(t) for t in lands], jax.ShapeDtypeStruct((8, 128), F32)),
        in_specs=(HBM_SPEC,) * (2 * n),
        out_specs=(SEM_SPEC, SEM_SPEC) + (HBM_SPEC,) * (2 * n) + (pl.BlockSpec(memory_space=pltpu.VMEM),),
        input_output_aliases={i: 2 + i for i in range(2 * n)},
        compiler_params=pltpu.CompilerParams(has_side_effects=pltpu.SideEffectType.DATAFLOW_SIDE_EFFECTING),
    )(*[pltpu.with_memory_space_constraint(t, pltpu.HBM) for t in (*srcs, *lands)])
    return out[0], out[1], out[2:2 + n], out[2 + n:2 + 2 * n], out[-1][0:1, 0:1]


def _split_wait(name, mode, send_sems, recv_sems, srcs, lands, after):
    n = len(srcs)

    def body(*refs):
        src, land = refs[:n], refs[n:2 * n]
        ssem, rsem = refs[2 * n], refs[2 * n + 1]
        place = _my_place()
        for w in range(n):
            for j in range(len(SPLIT_MASKS[mode])):
                _split_copy(mode, src[w], land[w], ssem, rsem, w, j, place).wait_send()
                _split_copy(mode, src[w], land[w], ssem, rsem, w, j, place, arriving=True).wait_recv()

    hbm = lambda t: pltpu.HBM(t.shape, t.dtype)
    out = pl.pallas_call(
        body, name=name,
        out_shape=tuple(hbm(t) for t in (*srcs, *lands)),
        in_specs=(HBM_SPEC,) * (2 * n) + (SEM_SPEC, SEM_SPEC, pl.BlockSpec(memory_space=pl.ANY)),
        out_specs=(HBM_SPEC,) * (2 * n),
        input_output_aliases={i: i for i in range(2 * n)},
        compiler_params=pltpu.CompilerParams(has_side_effects=pltpu.SideEffectType.DATAFLOW_SIDE_EFFECTING),
    )(*srcs, *lands, send_sems, recv_sems, after)
    return out[:n], out[n:]


def _sibling_exchange(grads):
    nw = len(grads)
    HBM = pl.BlockSpec(memory_space=pl.ANY)

    def body(*refs):
        g, land = refs[:nw], refs[nw:2 * nw]
        send_sems, recv_sems = refs[2 * nw:]
        x, y, c = _my_place()

        def copy(w, q, owner_core):
            r = land[w].shape[1]
            return pltpu.make_async_remote_copy(
                src_ref=g[w].at[pl.ds(pl.multiple_of((2 * q + owner_core) * r, 16), r), :], dst_ref=land[w].at[q],
                send_sem=send_sems.at[w, q], recv_sem=recv_sems.at[w, q], device_id=(x, y, 1 - c), device_id_type=MESH)

        sends = [copy(w, q, 1 - c) for w in range(nw) for q in range(4)]
        for cp in sends:
            cp.start()
        for w in range(nw):
            for q in range(4):
                copy(w, q, c).wait_recv()
        for cp in sends:
            cp.wait_send()

    return pl.pallas_call(
        body, name="sibling_exchange",
        out_shape=[jax.ShapeDtypeStruct((4, a.shape[0] // N_DEV, a.shape[1]), a.dtype) for a in grads],
        in_specs=[HBM] * nw, out_specs=[HBM] * nw,
        scratch_shapes=[pltpu.SemaphoreType.DMA((nw, 4)), pltpu.SemaphoreType.DMA((nw, 4))],
    )(*grads)


def _pair_sums(gs, sibs, core):
    n = len(gs)

    def body(core_ref, *refs):
        for w in range(n):
            refs[2 * n + w][0] = (refs[w][0, 0].astype(F32) + refs[n + w][0].astype(F32)).astype(BF16)

    in_specs = [pl.BlockSpec((1, 1) + t.shape[1:], lambda q, core_ref: (q, core_ref[0], 0, 0)) for t in sibs]
    in_specs += [pl.BlockSpec((1,) + t.shape[1:], lambda q, core_ref: (q, 0, 0)) for t in sibs]
    return pl.pallas_call(
        body, name="pair_sums",
        grid_spec=pltpu.PrefetchScalarGridSpec(
            num_scalar_prefetch=1, grid=(4,), in_specs=in_specs,
            out_specs=[pl.BlockSpec((1,) + t.shape[1:], lambda q, core_ref: (q, 0, 0)) for t in sibs]),
        out_shape=[jax.ShapeDtypeStruct(t.shape, BF16) for t in sibs],
        compiler_params=_cparams("parallel"),
    )(core, *[g.reshape(4, 2, t.shape[1], t.shape[2]) for g, t in zip(gs, sibs)], *sibs)


def _allgather_small(v, name):
    r, ccols = v.shape

    def body(v_ref, out_ref, send_sems, recv_sems):
        x, y, c = _my_place()
        my_idx = _dev_index(x, y, c)
        out_ref[my_idx] = v_ref[...]

        def copy(m):
            peer = _peer(x, y, c, m)
            return pltpu.make_async_remote_copy(
                src_ref=v_ref, dst_ref=out_ref.at[my_idx],
                send_sem=send_sems.at[m - 1], recv_sem=recv_sems.at[m - 1], device_id=peer, device_id_type=MESH)

        def arrival(m):
            peer = _peer(x, y, c, m)
            return pltpu.make_async_remote_copy(
                src_ref=v_ref, dst_ref=out_ref.at[_dev_index(*peer)],
                send_sem=send_sems.at[m - 1], recv_sem=recv_sems.at[m - 1], device_id=peer, device_id_type=MESH)

        sends = [copy(m) for m in range(1, N_DEV)]
        for cp in sends:
            cp.start()
        for m in range(1, N_DEV):
            arrival(m).wait_recv()
        for cp in sends:
            cp.wait_send()

    return pl.pallas_call(
        body, name=name,
        out_shape=jax.ShapeDtypeStruct((N_DEV, r, ccols), v.dtype),
        in_specs=[pl.BlockSpec(memory_space=pltpu.VMEM)], out_specs=pl.BlockSpec(memory_space=pltpu.VMEM),
        scratch_shapes=[pltpu.SemaphoreType.DMA((7,)), pltpu.SemaphoreType.DMA((7,))],
    )(v)


def _conditioning(pay, w_ada, b_cols):
    ncol = w_ada.shape[1]

    def body(pay_ref, w_ref, b_ref, got_ref, act_ref, mod_ref, send_sems, recv_sems):
        x, y, c = _my_place()
        my_idx = _dev_index(x, y, c)

        def copy(rnd, buf, m, arriving=False):
            peer = _peer(x, y, c, m)
            slot = _dev_index(*peer) if arriving else my_idx
            return pltpu.make_async_remote_copy(
                src_ref=buf.at[my_idx], dst_ref=buf.at[slot], send_sem=send_sems.at[rnd, m - 1],
                recv_sem=recv_sems.at[rnd, m - 1], device_id=peer, device_id_type=MESH)

        def exchange(rnd, buf):
            sends = [copy(rnd, buf, m) for m in range(1, N_DEV)]
            for cp in sends:
                cp.start()
            for m in range(1, N_DEV):
                copy(rnd, buf, m, arriving=True).wait_recv()
            for cp in sends:
                cp.wait_send()

        got_ref[my_idx] = pay_ref[...]
        exchange(0, got_ref)
        cv = jnp.concatenate([got_ref[s, 0:1, :] for s in range(N_DEV)], axis=0)
        act = cv * _sigmoid(cv)
        act_ref[...] = act
        mod_ref[my_idx] = jnp.dot(act, w_ref[...], preferred_element_type=F32, precision=lax.Precision.HIGHEST) + b_ref[...]
        exchange(1, mod_ref)

    vmem = pl.BlockSpec(memory_space=pltpu.VMEM)
    return pl.pallas_call(
        body, name="conditioning",
        out_shape=[jax.ShapeDtypeStruct((N_DEV, 8, D), F32), jax.ShapeDtypeStruct((N_DEV, D), F32),
                   jax.ShapeDtypeStruct((N_DEV, N_DEV, ncol), F32)],
        in_specs=[vmem] * 3, out_specs=[vmem] * 3,
        scratch_shapes=[pltpu.SemaphoreType.DMA((2, 7)), pltpu.SemaphoreType.DMA((2, 7))],
        compiler_params=_cparams(),
    )(pay, w_ada, b_cols)


def _ada_bwd(act_t, gm_cols):
    def body(a_ref, g_ref, o_ref):
        o_ref[...] = jnp.dot(a_ref[...], g_ref[...], preferred_element_type=F32, precision=lax.Precision.HIGHEST)

    return pl.pallas_call(
        body, name="ada_bwd", out_shape=jax.ShapeDtypeStruct((D, gm_cols.shape[1]), F32), compiler_params=_cparams(),
    )(act_t, gm_cols)


def _row_tile(r):
    for t in (256, 304, 128, 64, 16):
        if r % t == 0:
            return t
    return r


def _sum_parts(parts, name, own=None):
    k, r, ccols = parts.shape
    tr = _row_tile(r)

    def body(*refs):
        p_ref, o_ref = refs[0], refs[-1]
        acc = p_ref[0].astype(F32) if own is None else refs[1][...].astype(F32) + p_ref[0].astype(F32)
        for s in range(1, k):
            acc = acc + p_ref[s].astype(F32)
        o_ref[...] = acc

    blk = pl.BlockSpec((tr, ccols), lambda i: (i, 0))
    return pl.pallas_call(
        body, name=name, grid=(r // tr,),
        in_specs=[pl.BlockSpec((k, tr, ccols), lambda i: (0, i, 0))] + ([] if own is None else [blk]),
        out_specs=blk,
        out_shape=jax.ShapeDtypeStruct((r, ccols), F32),
        compiler_params=_cparams("parallel"),
    )(*((parts,) if own is None else (parts, own)))


def _adamw(w, g, m, v, name):
    r, ccols = w.shape
    tr = _row_tile(r)
    c1 = 1.0 / (1.0 - B1 ** STEP)
    c2 = 1.0 / (1.0 - B2 ** STEP)

    def body(w_ref, g_ref, m_ref, v_ref, d_ref, nm_ref, nv_ref):
        gv = g_ref[...]
        nm = B1 * m_ref[...] + (1.0 - B1) * gv
        nv = B2 * v_ref[...] + (1.0 - B2) * jnp.square(gv)
        nm_ref[...] = nm
        nv_ref[...] = nv
        d_ref[...] = -LR * ((nm * c1) / (jnp.sqrt(nv * c2) + ADAM_EPS) + WD * w_ref[...])

    blk = pl.BlockSpec((tr, ccols), lambda i: (i, 0))
    return pl.pallas_call(
        body, name=name, grid=(r // tr,), in_specs=[blk] * 4, out_specs=[blk] * 3,
        out_shape=[jax.ShapeDtypeStruct((r, ccols), F32)] * 3,
        compiler_params=_cparams("parallel"),
    )(w, g, m, v)


def _sum_adamw(parts, own, w, m, v, name):
    k, r, ccols = parts.shape
    tr = _row_tile(r)
    c1 = 1.0 / (1.0 - B1 ** STEP)
    c2 = 1.0 / (1.0 - B2 ** STEP)

    def body(p_ref, own_ref, w_ref, m_ref, v_ref, g_ref, d_ref, nm_ref, nv_ref):
        gv = own_ref[...].astype(F32)
        for s in range(k):
            gv = gv + p_ref[s].astype(F32)
        g_ref[...] = gv
        nm = B1 * m_ref[...] + (1.0 - B1) * gv
        nv = B2 * v_ref[...] + (1.0 - B2) * jnp.square(gv)
        nm_ref[...] = nm
        nv_ref[...] = nv
        d_ref[...] = -LR * ((nm * c1) / (jnp.sqrt(nv * c2) + ADAM_EPS) + WD * w_ref[...])

    blk = pl.BlockSpec((tr, ccols), lambda i: (i, 0))
    return pl.pallas_call(
        body, name=name, grid=(r // tr,),
        in_specs=[pl.BlockSpec((k, tr, ccols), lambda i: (0, i, 0))] + [blk] * 4, out_specs=[blk] * 4,
        out_shape=[jax.ShapeDtypeStruct((r, ccols), F32)] * 4,
        compiler_params=_cparams("parallel"),
    )(parts, own, w, m, v)


VEC_ROWS = ((0, 6), (6, 7), (9, 11), (11, 14), (7, 8), (8, 9))


def _adamw_vectors(w, g, m, v):
    c1 = 1.0 / (1.0 - B1 ** STEP)
    c2 = 1.0 / (1.0 - B2 ** STEP)

    def put(refs, p):
        for ref, (lo, hi) in zip(refs, VEC_ROWS):
            if ref.shape == (3, HEAD):
                ref[...] = p[lo:hi, :HEAD]
            else:
                ref[...] = jnp.concatenate([p[k:k + 1] for k in range(lo, hi)], axis=1)

    def body(w_ref, g_ref, m_ref, v_ref, *outs):
        gv = g_ref[...]
        nm = B1 * m_ref[...] + (1.0 - B1) * gv
        nv = B2 * v_ref[...] + (1.0 - B2) * jnp.square(gv)
        delta = -LR * ((nm * c1) / (jnp.sqrt(nv * c2) + ADAM_EPS) + WD * w_ref[...])
        for kind, p in enumerate((gv, delta, nm, nv)):
            put(outs[6 * kind:6 * kind + 6], p)

    shapes = [(1, 6 * D), (1, D), (1, 2 * D), (3, HEAD), (1, D), (1, D)]
    out = pl.pallas_call(
        body, name="adamw_vectors", out_shape=[jax.ShapeDtypeStruct(sh, F32) for sh in shapes] * 4, compiler_params=_cparams(),
    )(w, g, m, v)
    fix = lambda t: (t[0], t[1], t[2], t[3][None], t[4], t[5].reshape(D))
    return [fix(out[6 * kind:6 * kind + 6]) for kind in range(4)]


def _pack_vectors(b_ada, g_mix, g_mlp, g_fin, b_gate, conv_w):
    conv_rows = jnp.pad(conv_w.reshape(3, HEAD), ((0, 0), (0, D - HEAD)))
    return jnp.concatenate([b_ada.reshape(6, D), g_mix.reshape(1, D), g_mlp.reshape(1, D), g_fin.reshape(1, D),
                            b_gate.reshape(2, D), conv_rows, jnp.zeros((2, D), F32)], axis=0)


def kernel(x, c, w_ada, b_ada, g_norm_mix, w_in, b_gate, conv_w, w_branch_attn, w_branch_conv, w_out, g_norm_mlp, w_mlp_in, w_mlp_out, g_norm_final, loss_target, m_w_ada, m_b_ada, m_g_norm_mix, m_w_in, m_b_gate, m_conv_w, m_w_branch_attn, m_w_branch_conv, m_w_out, m_g_norm_mlp, m_w_mlp_in, m_w_mlp_out, m_g_norm_final, v_w_ada, v_b_ada, v_g_norm_mix, v_w_in, v_b_gate, v_conv_w, v_w_branch_attn, v_w_branch_conv, v_w_out, v_g_norm_mlp, v_w_mlp_in, v_w_mlp_out, v_g_norm_final):
    S = x.shape[1]
    xi, yi, ci = _my_place()
    me = _dev_index(xi, yi, ci)
    x2 = x.reshape(S, D)
    tgt = loss_target.reshape(S, D)

    pay = jnp.zeros((8, D), F32).at[0].set(c[0]).at[1:4, :HEAD].set(conv_w[0])
    ncol = w_ada.shape[2]
    b_cols = lax.dynamic_slice(b_ada, (0, me * ncol), (1, ncol))
    got, act, mod_all = _conditioning(pay, w_ada[0], b_cols)
    cw8 = jnp.pad(got[:, 1:4, :HEAD].transpose(1, 0, 2).reshape(3, D), ((0, 5), (0, 0)))

    w_in_shard, mod_all = lax.optimization_barrier((w_in[0].T.astype(BF16), mod_all))
    mod = lax.dynamic_index_in_dim(mod_all, me, axis=1, keepdims=False).reshape(6, D)
    (w_int,) = _allgather_weights([w_in_shard])
    late = [w_branch_attn[0].T.astype(BF16), w_branch_conv[0].astype(BF16), w_out[0].astype(BF16),
            w_mlp_in[0].T.astype(BF16), w_mlp_out[0].astype(BF16)]
    w_int, late = lax.optimization_barrier((w_int, late))
    zones = [lax.dynamic_update_slice(lax.empty((N_DEV * t.shape[0], t.shape[1]), BF16), t, (me * t.shape[0], 0)) for t in late]
    ag_mix = _split_start("gather_mix_start", "gather", late[:3], zones[:3])
    ag_mlp = _split_start("gather_mlp_start", "gather", late[3:], zones[3:])

    def mix_weights(o_attn):
        return _split_wait("gather_mix_wait", "gather", *ag_mix[:4], o_attn)[1]

    def mlp_weights(x1):
        return _split_wait("gather_mlp_wait", "gather", *ag_mlp[:4], x1)[1]

    rs = {}

    def mlp_grads_ready(*grads):
        lands = [lax.empty((N_PEER, t.shape[0] // N_DEV, t.shape[1]), BF16) for t in grads]
        rs["mlp"] = _split_start("scatter_mlp_start", "scatter", grads, lands)
        return rs["mlp"][4]

    def other_grads_ready(*grads):
        core = ci.reshape(1).astype(jnp.int32)
        pair = _pair_sums(grads, _sibling_exchange(grads), core)
        lands = [lax.empty((3,) + t.shape[1:], BF16) for t in pair]
        rs["rest"] = _split_start("scatter_rest_start", "chips", pair, lands)
        return rs["rest"][4]

    ba, bb = b_gate[:, :D], b_gate[:, D:]
    grad_x, vec = _local_step(
        x2, tgt, mod + ag_mix[4] + ag_mlp[4], g_norm_mix, g_norm_mlp, g_norm_final.reshape(1, D), ba, bb, cw8, w_int, mix_weights, mlp_weights,
        mlp_grads_ready, other_grads_ready)

    vec_all = _allgather_small(vec, "gather_vec")
    vec_sum = _sum_parts(vec_all, "sum_vec")
    loss = vec_sum[14, 0]
    gm_all = vec_all[:, 0:6, :].reshape(N_DEV, 6 * D)
    gm_cols = lax.dynamic_slice(gm_all, (0, me * ncol), (N_DEV, ncol))
    g_w_ada = _ada_bwd(act.T, gm_cols)
    conv_cols = lax.dynamic_slice(vec_sum[11:14], (0, me * HEAD), (3, HEAD))
    g_pack = jnp.concatenate([vec_sum[0:11], jnp.pad(conv_cols, ((0, 0), (0, D - HEAD))), jnp.zeros((2, D), F32)], axis=0)
    packs = [_pack_vectors(*t) for t in ((b_ada, g_norm_mix, g_norm_mlp, g_norm_final, b_gate, conv_w),
                                         (m_b_ada, m_g_norm_mix, m_g_norm_mlp, m_g_norm_final, m_b_gate, m_conv_w),
                                         (v_b_ada, v_g_norm_mix, v_g_norm_mlp, v_g_norm_final, v_b_gate, v_conv_w))]
    gv, dv, mv, vv = _adamw_vectors(packs[0], g_pack, packs[1], packs[2])
    d_ada, nm_ada, nv_ada = _adamw(w_ada[0], g_w_ada, m_w_ada[0], v_w_ada[0], "adamw_w_ada")

    big = {}
    srcs, lands = _split_wait("scatter_mlp_wait", "scatter", *rs["mlp"][:4], grad_x)
    own = [lax.dynamic_slice(g, (me * land.shape[1], 0), land.shape[1:]) for g, land in zip(srcs, lands)]
    g_mi = _sum_parts(lands[0], "sum_w_mi", own=own[0]).T
    big["w_mi"] = (g_mi[None],) + tuple(t[None] for t in _adamw(w_mlp_in[0], g_mi, m_w_mlp_in[0], v_w_mlp_in[0], "adamw_w_mi"))
    big["w_mo"] = tuple(t[None] for t in _sum_adamw(lands[1], own[1], w_mlp_out[0], m_w_mlp_out[0], v_w_mlp_out[0], "adamw_w_mo"))
    srcs, lands = _split_wait("scatter_rest_wait", "chips", *rs["rest"][:4], grad_x)
    own = [lax.dynamic_index_in_dim(pair, 2 * xi + yi, axis=0, keepdims=False) for pair in srcs]
    big["w_in"] = tuple(t.T[None] for t in _sum_adamw(lands[0], own[0], w_in[0].T, m_w_in[0].T, v_w_in[0].T, "adamw_w_in"))
    g_ba = _sum_parts(lands[1], "sum_w_ba", own=own[1]).T
    big["w_ba"] = (g_ba[None],) + tuple(t[None] for t in _adamw(w_branch_attn[0], g_ba, m_w_branch_attn[0], v_w_branch_attn[0], "adamw_w_ba"))
    big["w_bc"] = tuple(t[None] for t in _sum_adamw(lands[2], own[2], w_branch_conv[0], m_w_branch_conv[0], v_w_branch_conv[0], "adamw_w_bc"))
    big["w_out"] = tuple(t[None] for t in _sum_adamw(lands[3], own[3], w_out[0], m_w_out[0], v_w_out[0], "adamw_w_out"))

    def ordered(k, ada, vecs):
        return (ada[None], vecs[0], vecs[1], big["w_in"][k], vecs[2], vecs[3], big["w_ba"][k], big["w_bc"][k],
                big["w_out"][k], vecs[4], big["w_mi"][k], big["w_mo"][k], vecs[5])

    return (loss, grad_x.reshape(1, S, D), *ordered(0, g_w_ada, gv), *ordered(1, d_ada, dv),
            *ordered(2, nm_ada, mv), *ordered(3, nv_ada, vv))
```

```python
import functools

import numpy as np
import jax
import jax.numpy as jnp
from jax import lax
from jax.experimental import pallas as pl
from jax.experimental.pallas import tpu as pltpu

F32, BF16 = jnp.float32, jnp.bfloat16
D = 1024
HEAD = 128
DILATIONS = (1, 4, 16)
N_SLOT = 4
AOW = N_SLOT * HEAD
DFF = 4 * D
N_DEV = 8
UNROLL = 8
EPS = 1e-6
NEG = -1e30
SCALE = HEAD ** -0.5
LR, B1, B2, ADAM_EPS, WD, STEP = 0.001, 0.9, 0.999, 1e-08, 0.01, 10
V7X_VMEM_LIMIT = 56 * 1024 * 1024
TM = 1024
MESH = pl.DeviceIdType.MESH
AXES = ("x", "y", "c")


def _cparams(*sem):
    if sem:
        return pltpu.CompilerParams(dimension_semantics=sem, vmem_limit_bytes=V7X_VMEM_LIMIT)
    return pltpu.CompilerParams(vmem_limit_bytes=V7X_VMEM_LIMIT)


def _nn(a, b):
    return jnp.dot(a, b, preferred_element_type=F32)


def _nt(a, b):
    return lax.dot_general(a, b, (((1,), (1,)), ((), ())), preferred_element_type=F32)


def _tn(a, b):
    return lax.dot_general(a, b, (((0,), (0,)), ((), ())), preferred_element_type=F32)


def _rms_r(x):
    return lax.rsqrt(jnp.mean(x * x, axis=-1, keepdims=True) + EPS)


def _rms_bwd(x, r, g, dn):
    gy = dn * g
    dx = r * gy - x * (r * r * r) * jnp.mean(x * gy, axis=-1, keepdims=True)
    return dx, dn * (x * r)


def _sigmoid(t):
    return 1.0 / (1.0 + jnp.exp(-t))


def _rowsum(v):
    return jnp.sum(v, axis=0, keepdims=True)


def _vec_spec(n=D):
    return pl.BlockSpec((1, n), lambda *_: (0, 0))


def _const_spec(shape):
    nd = len(shape)
    return pl.BlockSpec(shape, lambda *_: (0,) * nd)


def _win_rowblock(j):
    return jnp.where(j < 9, (j % 3) * 3 + j // 3, j)


def _prenorm(x, g, sc, sh):
    S = x.shape[0]
    tm = TM

    def body(x_ref, g_ref, sc_ref, sh_ref, h_ref):
        xv = x_ref[...]
        h_ref[...] = (xv * _rms_r(xv) * g_ref[...] * (1.0 + sc_ref[...]) + sh_ref[...]).astype(BF16)

    row = pl.BlockSpec((tm, D), lambda i: (i, 0))
    return pl.pallas_call(
        body, name="prenorm", grid=(S // tm,), in_specs=[row, _vec_spec(), _vec_spec(), _vec_spec()], out_specs=row,
        out_shape=jax.ShapeDtypeStruct((S, D), BF16), compiler_params=_cparams("parallel"),
    )(x, g, sc, sh)


def _proj(h, w_int):
    S = h.shape[0]

    def body(h_ref, w_ref, q_ref, e_ref):
        j = pl.program_id(0)
        acc = _nt(h_ref[...], w_ref[...])

        @pl.when(j < 9)
        def _():
            q_ref[0] = acc

        @pl.when(j >= 9)
        def _():
            e_ref[0] = acc.astype(BF16)

    def e_idx(j):
        k = jnp.maximum(j - 9, 0)
        return (k // 2, 0, k % 2)

    return pl.pallas_call(
        body, name="proj", grid=(19,),
        in_specs=[pl.BlockSpec((S, D), lambda j: (0, 0), pipeline_mode=pl.Buffered(1)),
                  pl.BlockSpec((512, D), lambda j: (_win_rowblock(j), 0))],
        out_specs=[pl.BlockSpec((1, S, 512), lambda j: (jnp.minimum(j, 8), 0, 0)), pl.BlockSpec((1, S, 512), e_idx)],
        out_shape=[jax.ShapeDtypeStruct((9, S, 512), F32), jax.ShapeDtypeStruct((5, S, D), BF16)],
        compiler_params=_cparams("arbitrary"),
    )(h, w_int)


def _bias_table():
    slopes = (2.0 ** (-8.0 * np.arange(1, 13, dtype=np.float32) / 12.0)).astype(np.float32)
    qi = np.arange(HEAD)[:, None]
    kj = np.arange(2 * HEAD)[None, :]
    delta = HEAD + qi - kj
    mask = (delta >= 0) & (delta <= HEAD)
    out = np.zeros((3, N_SLOT, HEAD, 2 * HEAD), np.float32)
    for gi, d in enumerate(DILATIONS):
        for j in range(N_SLOT):
            bias = -slopes[gi * N_SLOT + j] * (delta * d).astype(np.float32)
            out[gi, j] = np.where(mask, bias, NEG)
    out_t = np.concatenate([out[..., HEAD:].swapaxes(-1, -2), out[..., :HEAD].swapaxes(-1, -2)], axis=-1)
    return jnp.asarray(out), jnp.asarray(out_t)


def _block_rows(b, d):
    r = b % d
    n = b // d
    st = n * (HEAD * d) + r
    stp = jnp.maximum(n - 1, 0) * (HEAD * d) + r
    return n, st, stp


def _attn_fwd(qkv, bias):
    S = qkv.shape[2]
    nblk = S // HEAD
    rows = 256

    def body(qkv_ref, b_ref, o_ref, lse_ref, o_s, lse_s):
        g = pl.program_id(1)
        bias = b_ref[0, 0]
        col = lax.broadcasted_iota(jnp.int32, bias.shape, 1)
        bias_first = jnp.where(col < HEAD, NEG, bias)

        for gi, d in enumerate(DILATIONS):
            @pl.when(g == gi)
            def _(gi=gi, d=d):
                def step(b, carry):
                    n, st, stp = _block_rows(b, d)
                    cur = pl.ds(st, HEAD, stride=d)
                    prv = pl.ds(stp, HEAD, stride=d)
                    q = qkv_ref.at[0, 0][cur, :].astype(BF16)
                    kw = jnp.concatenate([qkv_ref.at[0, 1][prv, :], qkv_ref.at[0, 1][cur, :]], axis=0).astype(BF16)
                    vw = jnp.concatenate([qkv_ref.at[0, 2][prv, :], qkv_ref.at[0, 2][cur, :]], axis=0).astype(BF16)
                    s = _nt(q, kw) * SCALE + jnp.where(n > 0, bias, bias_first)
                    m = jnp.max(s, axis=-1, keepdims=True)
                    p = jnp.exp(s - m)
                    l = jnp.sum(p, axis=-1, keepdims=True)
                    o_s.at[gi][cur, :] = _nn(p.astype(BF16), vw) / l
                    lse_s.at[gi][cur, :] = jnp.broadcast_to(m + jnp.log(l), (HEAD, HEAD))
                    return carry

                lax.fori_loop(0, nblk, step, 0, unroll=UNROLL)

        @pl.when(g == len(DILATIONS) - 1)
        def _():
            def merge(i, carry):
                r = pl.ds(pl.multiple_of(i * rows, rows), rows)
                ls = [lse_s[k, r, :] for k in range(3)]
                top = jnp.maximum(jnp.maximum(ls[0], ls[1]), ls[2])
                ws = [jnp.exp(t - top) for t in ls]
                den = ws[0] + ws[1] + ws[2]
                o_ref[r, :] = (ws[0] * o_s[0, r, :] + ws[1] * o_s[1, r, :] + ws[2] * o_s[2, r, :]) / den
                lse_ref[r, :] = top + jnp.log(den)
                return carry

            lax.fori_loop(0, S // rows, merge, 0)

    return pl.pallas_call(
        body, name="attn_fwd", grid=(N_SLOT, 3),
        in_specs=[pl.BlockSpec((1, 3, S, HEAD), lambda j, g: (g, 0, 0, j)),
                  pl.BlockSpec((1, 1, HEAD, 2 * HEAD), lambda j, g: (g, j, 0, 0))],
        out_specs=[pl.BlockSpec((S, HEAD), lambda j, g: (0, j)), pl.BlockSpec((S, HEAD), lambda j, g: (0, j))],
        out_shape=[jax.ShapeDtypeStruct((S, AOW), F32), jax.ShapeDtypeStruct((S, AOW), F32)],
        scratch_shapes=[pltpu.VMEM((3, S, HEAD), F32)] * 2,
        compiler_params=_cparams("parallel", "arbitrary"),
    )(qkv, bias)


def _shift_down(z, k, halo_rows):
    out = pltpu.roll(z, k, axis=0)
    top = out[:8]
    rid = lax.broadcasted_iota(jnp.int32, top.shape, 0)
    for t in range(k):
        top = jnp.where(rid == t, halo_rows[t], top)
    return jnp.concatenate([top, out[8:]], axis=0)


def _shift_up(z, k, halo_rows):
    n = z.shape[0]
    out = pltpu.roll(z, n - k, axis=0)
    bottom = out[n - 8:]
    rid = lax.broadcasted_iota(jnp.int32, bottom.shape, 0)
    for t in range(k):
        bottom = jnp.where(rid == 8 - k + t, halo_rows[t], bottom)
    return jnp.concatenate([out[:n - 8], bottom], axis=0)


def _e_spec(chunk, tm):
    return pl.BlockSpec((1, tm, D), lambda i, c=chunk: (c, i, 0))


def _e_prev_spec(chunk, tm):
    return pl.BlockSpec((1, 16, D), lambda i, c=chunk: (c, jnp.maximum(i * (tm // 16) - 1, 0), 0))


def _e_next_spec(chunk, tm, S):
    return pl.BlockSpec((1, 16, D), lambda i, c=chunk: (c, jnp.minimum((i + 1) * (tm // 16), S // 16 - 1), 0))


def _mix(o_attn, e, cw8, ba, bb, w_bat, w_bc):
    S = o_attn.shape[0]
    tm = 256

    def body(o_ref, cb_ref, cc_ref, cx_ref, ga_ref, gb_ref, ccp_ref, cxp_ref, cw_ref, ba_ref, bb_ref, wba_ref, wbc_ref,
             obf_ref, cbu_ref, ya_ref, yc_ref, mg_ref):
        i = pl.program_id(0)
        o = o_ref[...].astype(BF16)
        obf_ref[...] = o
        ya = _nt(o, wba_ref[...])
        z = cc_ref[0].astype(F32) * cx_ref[0].astype(F32)
        zp = ccp_ref[0].astype(F32) * cxp_ref[0].astype(F32) * (i > 0).astype(F32)
        z1 = _shift_down(z, 1, [zp[15:16]])
        z2 = _shift_down(z, 2, [zp[14:15], zp[15:16]])
        cw = cw_ref[...]
        u = cw[0:1] * z2 + cw[1:2] * z1 + cw[2:3] * z
        cbu = (cb_ref[0].astype(F32) * u).astype(BF16)
        cbu_ref[...] = cbu
        yc = _nn(cbu, wbc_ref[...])
        sa = _sigmoid(ga_ref[0].astype(F32) + ba_ref[...])
        sb = _sigmoid(gb_ref[0].astype(F32) + bb_ref[...])
        ya_ref[...] = ya.astype(BF16)
        yc_ref[...] = yc.astype(BF16)
        mg_ref[...] = (sa * ya + sb * yc).astype(BF16)

    row = lambda w: pl.BlockSpec((tm, w), lambda i: (i, 0))
    return pl.pallas_call(
        body, name="mix", grid=(S // tm,),
        in_specs=[row(AOW)] + [_e_spec(c, tm) for c in range(5)] + [_e_prev_spec(1, tm), _e_prev_spec(2, tm),
                  _const_spec((8, D)), _vec_spec(), _vec_spec(), _const_spec((D, AOW)), _const_spec((D, D))],
        out_specs=[row(AOW), row(D), row(D), row(D), row(D)],
        out_shape=[jax.ShapeDtypeStruct((S, AOW), BF16)] + [jax.ShapeDtypeStruct((S, D), BF16)] * 4,
        compiler_params=_cparams("parallel"),
    )(o_attn, e, e, e, e, e, e, e, cw8, ba, bb, w_bat, w_bc)


def _out_proj(merged, w_out, x, gate1, g_mlp, sc2, sh2):
    S = x.shape[0]
    tm = TM

    def body(mg_ref, w_ref, x_ref, gt_ref, g_ref, sc_ref, sh_ref, x1_ref, mo_ref, h2_ref):
        mo = _nn(mg_ref[...], w_ref[...])
        mo_ref[...] = mo.astype(BF16)
        x1 = x_ref[...] + gt_ref[...] * mo
        x1_ref[...] = x1
        h2 = x1 * _rms_r(x1) * g_ref[...] * (1.0 + sc_ref[...]) + sh_ref[...]
        h2_ref[...] = h2.astype(BF16)

    row = pl.BlockSpec((tm, D), lambda i: (i, 0))
    return pl.pallas_call(
        body, name="out_proj", grid=(S // tm,),
        in_specs=[row, _const_spec((D, D)), row, _vec_spec(), _vec_spec(), _vec_spec(), _vec_spec()],
        out_specs=[row, row, row],
        out_shape=[jax.ShapeDtypeStruct((S, D), F32), jax.ShapeDtypeStruct((S, D), BF16), jax.ShapeDtypeStruct((S, D), BF16)],
        compiler_params=_cparams("parallel"),
    )(merged, w_out, x, gate1, g_mlp, sc2, sh2)


def _mlp_in(h2, w_mit):
    S = h2.shape[0]
    tm, tn = TM, 2048

    def body(h_ref, w_ref, a_ref, f_ref):
        a = _nt(h_ref[...], w_ref[...])
        a_ref[...] = a.astype(BF16)
        f_ref[...] = jnp.square(jnp.maximum(a, 0.0)).astype(BF16)

    blk = pl.BlockSpec((tm, tn), lambda i, j: (i, j))
    return pl.pallas_call(
        body, name="mlp_in", grid=(S // tm, DFF // tn),
        in_specs=[pl.BlockSpec((tm, D), lambda i, j: (i, 0)), pl.BlockSpec((tn, D), lambda i, j: (j, 0))],
        out_specs=[blk, blk],
        out_shape=[jax.ShapeDtypeStruct((S, DFF), BF16)] * 2,
        compiler_params=_cparams("parallel", "parallel"),
    )(h2, w_mit)


def _mlp_out(f, w_mo, x1, gate2, g_fin, tgt):
    S = x1.shape[0]
    tm = 512
    half = tm // 2

    def body(f_ref, w_ref, x1_ref, gt_ref, g_ref, t_ref, mlp_ref, dx2_ref, pv_ref):
        @pl.when(pl.program_id(0) == 0)
        def _():
            pv_ref[...] = jnp.zeros_like(pv_ref)

        g = g_ref[...]
        for hs in (pl.ds(0, half), pl.ds(half, half)):
            mlp = _nn(f_ref[hs, :], w_ref[...])
            mlp_ref[hs, :] = mlp.astype(BF16)
            x2 = x1_ref[hs, :] + gt_ref[...] * mlp
            r = _rms_r(x2)
            err = x2 * r * g - t_ref[hs, :]
            dx2, pg = _rms_bwd(x2, r, g, err * (1.0 / D))
            dx2_ref[hs, :] = dx2
            pv_ref[0:1, :] += _rowsum(pg)
            pv_ref[1:2, :] += 0.5 * _rowsum(jnp.mean(err * err, axis=-1, keepdims=True))

    row = pl.BlockSpec((tm, D), lambda i: (i, 0))
    return pl.pallas_call(
        body, name="mlp_out", grid=(S // tm,),
        in_specs=[pl.BlockSpec((tm, DFF), lambda i: (i, 0)), _const_spec((DFF, D)), row, _vec_spec(), _vec_spec(), row],
        out_specs=[row, row, _const_spec((8, D))],
        out_shape=[jax.ShapeDtypeStruct((S, D), BF16), jax.ShapeDtypeStruct((S, D), F32), jax.ShapeDtypeStruct((8, D), F32)],
        compiler_params=_cparams("arbitrary"),
    )(f, w_mo, x1, gate2, g_fin, tgt)


def _bwd_mlp_a(dx2, gate2, mlp, w_mo, a):
    S = dx2.shape[0]
    tm = 512
    half = tm // 2

    def body(dx_ref, gt_ref, mlp_ref, w_ref, a_ref, da_ref, dmo_ref, pv_ref):
        @pl.when(pl.program_id(0) == 0)
        def _():
            pv_ref[...] = jnp.zeros_like(pv_ref)

        for hs in (pl.ds(0, half), pl.ds(half, half)):
            dx = dx_ref[hs, :]
            dmo = (dx * gt_ref[...]).astype(BF16)
            dmo_ref[hs, :] = dmo
            pv_ref[0:1, :] += _rowsum(dx * mlp_ref[hs, :].astype(F32))
            df = _nt(dmo, w_ref[...])
            da_ref[hs, :] = (df * (2.0 * jnp.maximum(a_ref[hs, :].astype(F32), 0.0))).astype(BF16)

    row = pl.BlockSpec((tm, D), lambda i: (i, 0))
    wide = pl.BlockSpec((tm, DFF), lambda i: (i, 0))
    return pl.pallas_call(
        body, name="bwd_mlp_a", grid=(S // tm,),
        in_specs=[row, _vec_spec(), row, _const_spec((DFF, D)), wide],
        out_specs=[wide, row, _const_spec((8, D))],
        out_shape=[jax.ShapeDtypeStruct((S, DFF), BF16), jax.ShapeDtypeStruct((S, D), BF16), jax.ShapeDtypeStruct((8, D), F32)],
        compiler_params=_cparams("arbitrary"),
    )(dx2, gate2, mlp, w_mo, a)


def _bwd_mlp_b(da, w_mit, x1, dx2, g_mlp, sc2):
    S = x1.shape[0]
    tm = 512
    half = tm // 2

    def body(da_ref, w_ref, x1_ref, dx2_ref, g_ref, sc_ref, dx1_ref, pv_ref):
        @pl.when(pl.program_id(0) == 0)
        def _():
            pv_ref[...] = jnp.zeros_like(pv_ref)

        g = g_ref[...]
        for hs in (pl.ds(0, half), pl.ds(half, half)):
            dh = _nn(da_ref[hs, :], w_ref[...])
            x1 = x1_ref[hs, :]
            r = _rms_r(x1)
            dxn, pg = _rms_bwd(x1, r, g, dh * (1.0 + sc_ref[...]))
            dx1_ref[hs, :] = dx2_ref[hs, :] + dxn
            pv_ref[0:1, :] += _rowsum(dh)
            pv_ref[1:2, :] += _rowsum(dh * (x1 * r * g))
            pv_ref[2:3, :] += _rowsum(pg)

    row = pl.BlockSpec((tm, D), lambda i: (i, 0))
    return pl.pallas_call(
        body, name="bwd_mlp_b", grid=(S // tm,),
        in_specs=[pl.BlockSpec((tm, DFF), lambda i: (i, 0)), _const_spec((DFF, D)), row, row, _vec_spec(), _vec_spec()],
        out_specs=[row, _const_spec((8, D))],
        out_shape=[jax.ShapeDtypeStruct((S, D), F32), jax.ShapeDtypeStruct((8, D), F32)],
        compiler_params=_cparams("arbitrary"),
    )(da, w_mit, x1, dx2, g_mlp, sc2)


def _bwd_mix(dx1, gate1, mo, e, cw8, ba, bb, ya, yc, o_attn, w_out, w_bc, w_bat):
    S = dx1.shape[0]
    tm = 256
    n_tiles = S // tm

    def body(dx_ref, dxn_ref, gt_ref, mo_ref, cb_ref, cc_ref, cx_ref, ga_ref, gb_ref, cbn_ref, gbn_ref, ccp_ref, cxp_ref,
             cw_ref, ba_ref, bb_ref, ya_ref, yc_ref, o_ref, wout_ref, wbc_ref, wba_ref,
             dmo_ref, dya_ref, dyc_ref, do_ref, dl_ref, de_ref, pv_ref):
        i = pl.program_id(0)

        @pl.when(i == 0)
        def _():
            pv_ref[...] = jnp.zeros_like(pv_ref)

        gate = gt_ref[...]
        bbv = bb_ref[...]

        def conv_branch_grad(dx_rows, gb_rows):
            dmo = (dx_rows * gate).astype(BF16)
            dmg = _nt(dmo, wout_ref[...])
            sb = _sigmoid(gb_rows + bbv)
            dyc = dmg * sb
            return dmo, dmg, sb, dyc, _nt(dyc.astype(BF16), wbc_ref[...])

        dx = dx_ref[...]
        cb = cb_ref[0].astype(F32)
        cc = cc_ref[0].astype(F32)
        cx = cx_ref[0].astype(F32)
        dmo, dmg, sb, dyc, dcbu = conv_branch_grad(dx, gb_ref[0].astype(F32))
        dmo_ref[...] = dmo
        pv_ref[0:1, :] += _rowsum(dx * mo_ref[...].astype(F32))
        sa = _sigmoid(ga_ref[0].astype(F32) + ba_ref[...])
        dya = (dmg * sa).astype(BF16)
        dya_ref[...] = dya
        dyc_ref[...] = dyc.astype(BF16)
        dga = dmg * ya_ref[...].astype(F32) * sa * (1.0 - sa)
        dgb = dmg * yc_ref[...].astype(F32) * sb * (1.0 - sb)
        pv_ref[1:2, :] += _rowsum(dga)
        pv_ref[2:3, :] += _rowsum(dgb)

        do = _nn(dya, wba_ref[...])
        do_ref[...] = do
        prod = do * o_ref[...]
        dl_ref[...] = jnp.concatenate(
            [jnp.broadcast_to(jnp.sum(prod[:, s * HEAD:(s + 1) * HEAD], axis=-1, keepdims=True), (tm, HEAD))
             for s in range(N_SLOT)], axis=1)

        z = cc * cx
        zp = ccp_ref[0].astype(F32) * cxp_ref[0].astype(F32) * (i > 0).astype(F32)
        z1 = _shift_down(z, 1, [zp[15:16]])
        z2 = _shift_down(z, 2, [zp[14:15], zp[15:16]])
        cw = cw_ref[...]
        u = cw[0:1] * z2 + cw[1:2] * z1 + cw[2:3] * z
        du = dcbu * cb
        dcbu_n = conv_branch_grad(dxn_ref[...], gbn_ref[0].astype(F32))[4]
        du_n = dcbu_n * cbn_ref[0].astype(F32) * (i < n_tiles - 1).astype(F32)
        du1 = _shift_up(du, 1, [du_n[0:1]])
        du2 = _shift_up(du, 2, [du_n[0:1], du_n[1:2]])
        dz = cw[2:3] * du + cw[1:2] * du1 + cw[0:1] * du2
        pv_ref[3:4, :] += _rowsum(du * z2)
        pv_ref[4:5, :] += _rowsum(du * z1)
        pv_ref[5:6, :] += _rowsum(du * z)

        de_ref[0] = (dcbu * u).astype(BF16)
        de_ref[1] = (dz * cx).astype(BF16)
        de_ref[2] = (dz * cc).astype(BF16)
        de_ref[3] = dga.astype(BF16)
        de_ref[4] = dgb.astype(BF16)

    row = lambda w: pl.BlockSpec((tm, w), lambda i: (i, 0))
    nxt = pl.BlockSpec((16, D), lambda i: (jnp.minimum((i + 1) * (tm // 16), S // 16 - 1), 0))
    return pl.pallas_call(
        body, name="bwd_mix", grid=(n_tiles,),
        in_specs=[row(D), nxt, _vec_spec(), row(D)] + [_e_spec(c, tm) for c in range(5)]
                 + [_e_next_spec(0, tm, S), _e_next_spec(4, tm, S), _e_prev_spec(1, tm), _e_prev_spec(2, tm),
                    _const_spec((8, D)), _vec_spec(), _vec_spec(), row(D), row(D), row(AOW),
                    _const_spec((D, D)), _const_spec((D, D)), _const_spec((D, AOW))],
        out_specs=[row(D), row(D), row(D), row(AOW), row(AOW), pl.BlockSpec((5, tm, D), lambda i: (0, i, 0)),
                   _const_spec((8, D))],
        out_shape=[jax.ShapeDtypeStruct((S, D), BF16)] * 3 + [jax.ShapeDtypeStruct((S, AOW), F32)] * 2
                  + [jax.ShapeDtypeStruct((5, S, D), BF16), jax.ShapeDtypeStruct((8, D), F32)],
        compiler_params=_cparams("arbitrary"),
    )(dx1, dx1, gate1, mo, e, e, e, e, e, e, e, e, e, cw8, ba, bb, ya, yc, o_attn, w_out, w_bc, w_bat)


def _attn_bwd(qkv, do, lse, dl, bias_t):
    S = qkv.shape[2]
    nblk = S // HEAD

    def body(qkv_ref, do_ref, lse_ref, dl_ref, b_ref, d_ref):
        g = pl.program_id(1)
        bias = b_ref[0, 0]
        col = lax.broadcasted_iota(jnp.int32, bias.shape, 1)
        bias_last = jnp.where(col >= HEAD, NEG, bias)
        eye = (lax.broadcasted_iota(jnp.int32, (HEAD, HEAD), 0) == lax.broadcasted_iota(jnp.int32, (HEAD, HEAD), 1)).astype(F32)

        def as_row(t):
            return jnp.sum(t * eye, axis=0, keepdims=True)

        for gi, d in enumerate(DILATIONS):
            @pl.when(g == gi)
            def _(d=d):
                nb = nblk // d

                def step(b, dq_part):
                    r, n = b // nb, b % nb
                    cur = pl.ds(n * (HEAD * d) + r, HEAD, stride=d)
                    nxt = pl.ds(jnp.minimum(n + 1, nb - 1) * (HEAD * d) + r, HEAD, stride=d)
                    two = lambda ref: jnp.concatenate([ref[cur, :], ref[nxt, :]], axis=0)
                    two_rows = lambda ref: jnp.concatenate([as_row(ref[cur, :]), as_row(ref[nxt, :])], axis=1)
                    q2 = two(qkv_ref.at[0, 0]).astype(BF16)
                    do2 = two(do_ref).astype(BF16)
                    k = qkv_ref.at[0, 1][cur, :].astype(BF16)
                    v = qkv_ref.at[0, 2][cur, :].astype(BF16)
                    s = _nt(k, q2) * SCALE + jnp.where(n < nb - 1, bias, bias_last)
                    p = jnp.exp(s - two_rows(lse_ref))
                    d_ref.at[0, 2][cur, :] = _nn(p.astype(BF16), do2)
                    dp = _nt(v, do2)
                    ds = (p * (dp - two_rows(dl_ref)) * SCALE).astype(BF16)
                    d_ref.at[0, 1][cur, :] = _nn(ds, q2)
                    dq2 = _tn(ds, k)
                    d_ref.at[0, 0][cur, :] = dq2[:HEAD] + jnp.where(n > 0, dq_part, 0.0)
                    return dq2[HEAD:]

                def steps(i, dq_part):
                    for u in range(UNROLL):
                        dq_part = step(i * UNROLL + u, dq_part)
                    return dq_part

                lax.fori_loop(0, nblk // UNROLL, steps, jnp.zeros((HEAD, HEAD), F32))

    col_blk = pl.BlockSpec((S, HEAD), lambda j, g: (0, j))
    qkv_blk = pl.BlockSpec((1, 3, S, HEAD), lambda j, g: (g, 0, 0, j))
    return pl.pallas_call(
        body, name="attn_bwd", grid=(N_SLOT, 3),
        in_specs=[qkv_blk, col_blk, col_blk, col_blk, pl.BlockSpec((1, 1, HEAD, 2 * HEAD), lambda j, g: (g, j, 0, 0))],
        out_specs=qkv_blk,
        out_shape=jax.ShapeDtypeStruct((3, 3, S, AOW), F32),
        compiler_params=_cparams("parallel", "arbitrary"),
    )(qkv, do, lse, dl, bias_t)


def _bwd_in(dqkv, de, w_int, x, dx1, g_mix, sc1):
    S = x.shape[0]
    tm = TM
    dqkv = dqkv.reshape(3, 3, S, AOW)

    def body(dq_ref, de_ref, wq_ref, wk_ref, wv_ref, wa_ref, wb_ref, x_ref, dx1_ref, g_ref, sc_ref, gx_ref, pv_ref):
        acc = gx_ref
        i, k = pl.program_id(0), pl.program_id(1)

        @pl.when((i == 0) & (k == 0))
        def _():
            pv_ref[...] = jnp.zeros_like(pv_ref)

        @pl.when(k == 0)
        def _():
            acc[...] = jnp.zeros_like(acc)

        @pl.when(k < 3)
        def _():
            lhs = jnp.concatenate([dq_ref[0, t].astype(BF16) for t in range(3)], axis=1)
            acc[...] += _nn(lhs, jnp.concatenate([wq_ref[...], wk_ref[...], wv_ref[...]], axis=0))

        @pl.when(k >= 3)
        def _():
            acc[...] += _nn(de_ref[0], jnp.concatenate([wa_ref[...], wb_ref[...]], axis=0))

        @pl.when(k == 7)
        def _():
            dh = acc[...]
            xv = x_ref[...]
            r = _rms_r(xv)
            g = g_ref[...]
            dxn, pg = _rms_bwd(xv, r, g, dh * (1.0 + sc_ref[...]))
            gx_ref[...] = dx1_ref[...] + dxn
            pv_ref[0:1, :] += _rowsum(dh)
            pv_ref[1:2, :] += _rowsum(dh * (xv * r * g))
            pv_ref[2:3, :] += _rowsum(pg)

    grp = lambda k: jnp.minimum(k, 2)
    chunk = lambda k: jnp.maximum(k - 3, 0)
    wblk = lambda f: pl.BlockSpec((512, D), lambda i, k: (f(k), 0))
    row = pl.BlockSpec((tm, D), lambda i, k: (i, 0))
    once = pl.BlockSpec((tm, D), lambda i, k: (i, 0), pipeline_mode=pl.Buffered(1))
    return pl.pallas_call(
        body, name="bwd_in", grid=(S // tm, 8),
        in_specs=[pl.BlockSpec((1, 3, tm, 512), lambda i, k: (grp(k), 0, i, 0)),
                  pl.BlockSpec((1, tm, D), lambda i, k: (chunk(k), i, 0)),
                  wblk(grp), wblk(lambda k: 3 + grp(k)), wblk(lambda k: 6 + grp(k)),
                  wblk(lambda k: 9 + 2 * chunk(k)), wblk(lambda k: 10 + 2 * chunk(k)),
                  once, once, _vec_spec(), _vec_spec()],
        out_specs=[row, _const_spec((8, D))],
        out_shape=[jax.ShapeDtypeStruct((S, D), F32), jax.ShapeDtypeStruct((8, D), F32)],
        compiler_params=_cparams("arbitrary", "arbitrary"),
    )(dqkv, de, w_int, w_int, w_int, w_int, w_int, x, dx1, g_mix, sc1)


def _grad_w(name, a, b):
    S, ka = a.shape
    nb = b.shape[1]

    def body(a_ref, b_ref, o_ref):
        o_ref[...] = _tn(a_ref[...], b_ref[...]).astype(BF16)

    return pl.pallas_call(
        body, name=name, grid=(ka // 512,),
        in_specs=[pl.BlockSpec((S, 512), lambda n: (0, n)), pl.BlockSpec((S, nb), lambda n: (0, 0))],
        out_specs=pl.BlockSpec((512, nb), lambda n: (n, 0)),
        out_shape=jax.ShapeDtypeStruct((ka, nb), BF16),
        compiler_params=_cparams("parallel"),
    )(a, b)


def _grad_w_small(dya, o_bf, cbu, dyc, merged, dmo):
    S = dya.shape[0]

    def body(dya_ref, o_ref, cbu_ref, dyc_ref, mg_ref, dmo_ref, gba_ref, gbc_ref, gout_ref):
        gba_ref[...] = _tn(dya_ref[...], o_ref[...]).astype(BF16)
        gbc_ref[...] = _tn(cbu_ref[...], dyc_ref[...]).astype(BF16)
        gout_ref[...] = _tn(mg_ref[...], dmo_ref[...]).astype(BF16)

    a_blk = pl.BlockSpec((S, 512), lambda n: (0, n))
    whole = lambda w: pl.BlockSpec((S, w), lambda n: (0, 0))
    out = lambda w: pl.BlockSpec((512, w), lambda n: (n, 0))
    return pl.pallas_call(
        body, name="grad_w_small", grid=(D // 512,),
        in_specs=[a_blk, whole(AOW), a_blk, whole(D), a_blk, whole(D)],
        out_specs=[out(AOW), out(D), out(D)],
        out_shape=[jax.ShapeDtypeStruct((D, AOW), BF16), jax.ShapeDtypeStruct((D, D), BF16), jax.ShapeDtypeStruct((D, D), BF16)],
        compiler_params=_cparams("parallel"),
    )(dya, o_bf, cbu, dyc, merged, dmo)


def _grad_w_in(dqkv, de, h):
    S = h.shape[0]

    def body(dq_ref, de_ref, h_ref, o_ref):
        n = pl.program_id(0)

        @pl.when(n < 9)
        def _():
            o_ref[...] = _tn(dq_ref[0].astype(BF16), h_ref[...]).astype(BF16)

        @pl.when(n >= 9)
        def _():
            o_ref[...] = _tn(de_ref[0], h_ref[...]).astype(BF16)

    def e_idx(n):
        kk = jnp.maximum(n - 9, 0)
        return (kk // 2, 0, kk % 2)

    return pl.pallas_call(
        body, name="grad_w_in", grid=(19,),
        in_specs=[pl.BlockSpec((1, S, 512), lambda n: (jnp.minimum(n, 8), 0, 0)), pl.BlockSpec((1, S, 512), e_idx),
                  pl.BlockSpec((S, D), lambda n: (0, 0))],
        out_specs=pl.BlockSpec((512, D), lambda n: (_win_rowblock(n), 0)),
        out_shape=jax.ShapeDtypeStruct((19 * 512, D), BF16),
        compiler_params=_cparams("parallel"),
    )(dqkv, de, h)


def _local_step(x, tgt, mod, g_mix, g_mlp, g_fin, ba, bb, cw8, w_int, mix_weights, mlp_weights, mlp_grads_ready, other_grads_ready):
    S = x.shape[0]
    sh1, sc1, gt1, sh2, sc2, gt2 = [mod[k:k + 1] for k in range(6)]
    bias, bias_t = _bias_table()

    h = _prenorm(x, g_mix, sc1, sh1)
    qkv, e = _proj(h, w_int)
    qkv = qkv.reshape(3, 3, S, AOW)
    o_attn, lse = _attn_fwd(qkv, bias)
    w_bat, w_bc, w_out = mix_weights(o_attn)
    o_bf, cbu, ya, yc, merged = _mix(o_attn, e, cw8, ba, bb, w_bat, w_bc)
    x1, mo, h2 = _out_proj(merged, w_out, x, gt1, g_mlp, sc2, sh2)
    w_mit, w_mo = mlp_weights(x1)
    a, f = _mlp_in(h2, w_mit)
    mlp, dx2, pv_f = _mlp_out(f, w_mo, x1, gt2, g_fin, tgt)

    da, dmo2, pv_a = _bwd_mlp_a(dx2, gt2, mlp, w_mo, a)
    dx1, pv_b = _bwd_mlp_b(da, w_mit, x1, dx2, g_mlp, sc2)
    zero = mlp_grads_ready(_grad_w("grad_w_mi", da, h2), _grad_w("grad_w_mo", f, dmo2))
    dmo, dya, dyc, do, dl, de, pv_m = _bwd_mix(dx1, gt1 + zero, mo, e, cw8, ba, bb, ya, yc, o_attn, w_out, w_bc, w_bat)
    dqkv = _attn_bwd(qkv, do, lse, dl, bias_t).reshape(9, S, AOW)
    zero = other_grads_ready(_grad_w_in(dqkv, de, h), *_grad_w_small(dya, o_bf, cbu, dyc, merged, dmo))
    grad_x, pv_i = _bwd_in(dqkv, de, w_int, x, dx1, g_mix, sc1 + zero)

    vec = jnp.concatenate([pv_i[0:2], pv_m[0:1], pv_b[0:2], pv_a[0:1], pv_i[2:3], pv_b[2:3], pv_f[0:1],
                           pv_m[1:3], pv_m[3:6], pv_f[1:2], jnp.zeros((1, D), F32)], axis=0)
    return grad_x, vec


def _my_place():
    return lax.axis_index("x"), lax.axis_index("y"), lax.axis_index("c")


def _dev_index(px, py, pc):
    return 4 * px + 2 * py + pc


def _allgather_weights(shards):
    nw = len(shards)
    HBM = pl.BlockSpec(memory_space=pl.ANY)

    def body(*refs):
        sh, full = refs[:nw], refs[nw:2 * nw]
        send_sems, recv_sems, local_sems = refs[2 * nw:]
        x, y, c = _my_place()
        me, sibling = (x, y, c), (x, y, 1 - c)
        chips = [(1 - x, y), (x, 1 - y), (1 - x, 1 - y)]

        def rows(w, px, py, pc):
            r = sh[w].shape[0]
            return full[w].at[pl.ds(pl.multiple_of(_dev_index(px, py, pc) * r, 16), r), :]

        def copy(w, k, block, to, src=None):
            return pltpu.make_async_remote_copy(
                src_ref=rows(w, *block) if src is None else src, dst_ref=rows(w, *block),
                send_sem=send_sems.at[w, k], recv_sem=recv_sems.at[w, k], device_id=to, device_id_type=MESH)

        mine = [pltpu.make_async_copy(sh[w], rows(w, *me), local_sems.at[w]) for w in range(nw)]
        for cp in mine:
            cp.start()
        first = []
        for w in range(nw):
            first.append(copy(w, 0, me, sibling, src=sh[w]))
            first += [copy(w, 1 + j, me, (*chip, c), src=sh[w]) for j, chip in enumerate(chips)]
        for cp in first:
            cp.start()
        passed = []
        for w in range(nw):
            for j, chip in enumerate(chips):
                copy(w, 1 + j, (*chip, c), me).wait_recv()
                fwd = copy(w, 4 + j, (*chip, c), sibling)
                fwd.start()
                passed.append(fwd)
        for w in range(nw):
            copy(w, 0, sibling, me).wait_recv()
            for j, chip in enumerate(chips):
                copy(w, 4 + j, (*chip, 1 - c), me).wait_recv()
        for cp in first + passed:
            cp.wait_send()
        for cp in mine:
            cp.wait()

    return pl.pallas_call(
        body, name="allgather_weights",
        out_shape=[jax.ShapeDtypeStruct((N_DEV * s.shape[0], s.shape[1]), s.dtype) for s in shards],
        in_specs=[HBM] * nw, out_specs=[HBM] * nw,
        scratch_shapes=[pltpu.SemaphoreType.DMA((nw, 7)), pltpu.SemaphoreType.DMA((nw, 7)), pltpu.SemaphoreType.DMA((nw,))],
    )(*shards)


def _peer(x, y, c, m):
    return (x ^ ((m >> 2) & 1), y ^ ((m >> 1) & 1), c ^ (m & 1))


HBM_SPEC = pl.BlockSpec(memory_space=pltpu.HBM)
SEM_SPEC = pl.BlockSpec(memory_space=pltpu.SEMAPHORE)
N_PEER = N_DEV - 1


SPLIT_MASKS = {"gather": tuple(range(1, N_DEV)), "scatter": tuple(range(1, N_DEV)), "chips": (2, 4, 6)}


def _split_copy(mode, src_ref, land_ref, send_sems, recv_sems, w, j, place, arriving=False):
    x, y, c = place
    masks = SPLIT_MASKS[mode]
    peer = _peer(x, y, c, masks[j])
    k = w * len(masks) + j
    sender, receiver = ((peer, (x, y, c)) if arriving else ((x, y, c), peer))
    if mode == "gather":
        r = src_ref.shape[0]
        src, dst = src_ref, land_ref.at[pl.ds(pl.multiple_of(_dev_index(*sender) * r, 16), r), :]
    elif mode == "scatter":
        r = land_ref.shape[1]
        src, dst = src_ref.at[pl.ds(pl.multiple_of(_dev_index(*receiver) * r, 16), r), :], land_ref.at[j]
    else:
        src, dst = src_ref.at[2 * receiver[0] + receiver[1]], land_ref.at[j]
    return pltpu.make_async_remote_copy(src_ref=src, dst_ref=dst, send_sem=send_sems.at[k], recv_sem=recv_sems.at[k],
                                        device_id=peer, device_id_type=MESH)


def _split_start(name, mode, srcs, lands):
    n = len(srcs)
    nm = len(SPLIT_MASKS[mode])

    def body(*refs):
        src, land = refs[:n], refs[n:2 * n]
        send_sems, recv_sems = refs[2 * n], refs[2 * n + 1]
        token = refs[-1]
        place = _my_place()
        for w in range(n):
            for j in range(nm):
                _split_copy(mode, src[w], land[w], send_sems, recv_sems, w, j, place).start()
        token[...] = jnp.zeros_like(token)

    hbm = lambda t: pltpu.HBM(t.shape, t.dtype)
    out = pl.pallas_call(
        body, name=name,
        out_shape=(pltpu.SemaphoreType.DMA((n * nm,)), pltpu.SemaphoreType.DMA((n * nm,)), *[hbm(t) for t in srcs],
                   *[hbm(t) for t in lands], jax.ShapeDtypeStruct((8, 128), F32)),
        in_specs=(HBM_SPEC,) * (2 * n),
        out_specs=(SEM_SPEC, SEM_SPEC) + (HBM_SPEC,) * (2 * n) + (pl.BlockSpec(memory_space=pltpu.VMEM),),
        input_output_aliases={i: 2 + i for i in range(2 * n)},
        compiler_params=pltpu.CompilerParams(has_side_effects=pltpu.SideEffectType.DATAFLOW_SIDE_EFFECTING),
    )(*[pltpu.with_memory_space_constraint(t, pltpu.HBM) for t in (*srcs, *lands)])
    return out[0], out[1], out[2:2 + n], out[2 + n:2 + 2 * n], out[-1][0:1, 0:1]


def _split_wait(name, mode, send_sems, recv_sems, srcs, lands, after):
    n = len(srcs)

    def body(*refs):
        src, land = refs[:n], refs[n:2 * n]
        ssem, rsem = refs[2 * n], refs[2 * n + 1]
        place = _my_place()
        for w in range(n):
            for j in range(len(SPLIT_MASKS[mode])):
                _split_copy(mode, src[w], land[w], ssem, rsem, w, j, place).wait_send()
                _split_copy(mode, src[w], land[w], ssem, rsem, w, j, place, arriving=True).wait_recv()

    hbm = lambda t: pltpu.HBM(t.shape, t.dtype)
    out = pl.pallas_call(
        body, name=name,
        out_shape=tuple(hbm(t) for t in (*srcs, *lands)),
        in_specs=(HBM_SPEC,) * (2 * n) + (SEM_SPEC, SEM_SPEC, pl.BlockSpec(memory_space=pl.ANY)),
        out_specs=(HBM_SPEC,) * (2 * n),
        input_output_aliases={i: i for i in range(2 * n)},
        compiler_params=pltpu.CompilerParams(has_side_effects=pltpu.SideEffectType.DATAFLOW_SIDE_EFFECTING),
    )(*srcs, *lands, send_sems, recv_sems, after)
    return out[:n], out[n:]


def _sibling_exchange(grads):
    nw = len(grads)
    HBM = pl.BlockSpec(memory_space=pl.ANY)

    def body(*refs):
        g, land = refs[:nw], refs[nw:2 * nw]
        send_sems, recv_sems = refs[2 * nw:]
        x, y, c = _my_place()

        def copy(w, q, owner_core):
            r = land[w].shape[1]
            return pltpu.make_async_remote_copy(
                src_ref=g[w].at[pl.ds(pl.multiple_of((2 * q + owner_core) * r, 16), r), :], dst_ref=land[w].at[q],
                send_sem=send_sems.at[w, q], recv_sem=recv_sems.at[w, q], device_id=(x, y, 1 - c), device_id_type=MESH)

        sends = [copy(w, q, 1 - c) for w in range(nw) for q in range(4)]
        for cp in sends:
            cp.start()
        for w in range(nw):
            for q in range(4):
                copy(w, q, c).wait_recv()
        for cp in sends:
            cp.wait_send()

    return pl.pallas_call(
        body, name="sibling_exchange",
        out_shape=[jax.ShapeDtypeStruct((4, a.shape[0] // N_DEV, a.shape[1]), a.dtype) for a in grads],
        in_specs=[HBM] * nw, out_specs=[HBM] * nw,
        scratch_shapes=[pltpu.SemaphoreType.DMA((nw, 4)), pltpu.SemaphoreType.DMA((nw, 4))],
    )(*grads)


def _pair_sums(gs, sibs, core):
    n = len(gs)

    def body(core_ref, *refs):
        for w in range(n):
            refs[2 * n + w][0] = (refs[w][0, 0].astype(F32) + refs[n + w][0].astype(F32)).astype(BF16)

    in_specs = [pl.BlockSpec((1, 1) + t.shape[1:], lambda q, core_ref: (q, core_ref[0], 0, 0)) for t in sibs]
    in_specs += [pl.BlockSpec((1,) + t.shape[1:], lambda q, core_ref: (q, 0, 0)) for t in sibs]
    return pl.pallas_call(
        body, name="pair_sums",
        grid_spec=pltpu.PrefetchScalarGridSpec(
            num_scalar_prefetch=1, grid=(4,), in_specs=in_specs,
            out_specs=[pl.BlockSpec((1,) + t.shape[1:], lambda q, core_ref: (q, 0, 0)) for t in sibs]),
        out_shape=[jax.ShapeDtypeStruct(t.shape, BF16) for t in sibs],
        compiler_params=_cparams("parallel"),
    )(core, *[g.reshape(4, 2, t.shape[1], t.shape[2]) for g, t in zip(gs, sibs)], *sibs)


def _allgather_small(v, name):
    r, ccols = v.shape

    def body(v_ref, out_ref, send_sems, recv_sems):
        x, y, c = _my_place()
        my_idx = _dev_index(x, y, c)
        out_ref[my_idx] = v_ref[...]

        def copy(m):
            peer = _peer(x, y, c, m)
            return pltpu.make_async_remote_copy(
                src_ref=v_ref, dst_ref=out_ref.at[my_idx],
                send_sem=send_sems.at[m - 1], recv_sem=recv_sems.at[m - 1], device_id=peer, device_id_type=MESH)

        def arrival(m):
            peer = _peer(x, y, c, m)
            return pltpu.make_async_remote_copy(
                src_ref=v_ref, dst_ref=out_ref.at[_dev_index(*peer)],
                send_sem=send_sems.at[m - 1], recv_sem=recv_sems.at[m - 1], device_id=peer, device_id_type=MESH)

        sends = [copy(m) for m in range(1, N_DEV)]
        for cp in sends:
            cp.start()
        for m in range(1, N_DEV):
            arrival(m).wait_recv()
        for cp in sends:
            cp.wait_send()

    return pl.pallas_call(
        body, name=name,
        out_shape=jax.ShapeDtypeStruct((N_DEV, r, ccols), v.dtype),
        in_specs=[pl.BlockSpec(memory_space=pltpu.VMEM)], out_specs=pl.BlockSpec(memory_space=pltpu.VMEM),
        scratch_shapes=[pltpu.SemaphoreType.DMA((7,)), pltpu.SemaphoreType.DMA((7,))],
    )(v)


def _conditioning(pay, w_ada, b_cols):
    ncol = w_ada.shape[1]

    def body(pay_ref, w_ref, b_ref, got_ref, act_ref, mod_ref, send_sems, recv_sems):
        x, y, c = _my_place()
        my_idx = _dev_index(x, y, c)

        def copy(rnd, buf, m, arriving=False):
            peer = _peer(x, y, c, m)
            slot = _dev_index(*peer) if arriving else my_idx
            return pltpu.make_async_remote_copy(
                src_ref=buf.at[my_idx], dst_ref=buf.at[slot], send_sem=send_sems.at[rnd, m - 1],
                recv_sem=recv_sems.at[rnd, m - 1], device_id=peer, device_id_type=MESH)

        def exchange(rnd, buf):
            sends = [copy(rnd, buf, m) for m in range(1, N_DEV)]
            for cp in sends:
                cp.start()
            for m in range(1, N_DEV):
                copy(rnd, buf, m, arriving=True).wait_recv()
            for cp in sends:
                cp.wait_send()

        got_ref[my_idx] = pay_ref[...]
        exchange(0, got_ref)
        cv = jnp.concatenate([got_ref[s, 0:1, :] for s in range(N_DEV)], axis=0)
        act = cv * _sigmoid(cv)
        act_ref[...] = act
        mod_ref[my_idx] = jnp.dot(act, w_ref[...], preferred_element_type=F32, precision=lax.Precision.HIGHEST) + b_ref[...]
        exchange(1, mod_ref)

    vmem = pl.BlockSpec(memory_space=pltpu.VMEM)
    return pl.pallas_call(
        body, name="conditioning",
        out_shape=[jax.ShapeDtypeStruct((N_DEV, 8, D), F32), jax.ShapeDtypeStruct((N_DEV, D), F32),
                   jax.ShapeDtypeStruct((N_DEV, N_DEV, ncol), F32)],
        in_specs=[vmem] * 3, out_specs=[vmem] * 3,
        scratch_shapes=[pltpu.SemaphoreType.DMA((2, 7)), pltpu.SemaphoreType.DMA((2, 7))],
        compiler_params=_cparams(),
    )(pay, w_ada, b_cols)


def _ada_bwd(act_t, gm_cols):
    def body(a_ref, g_ref, o_ref):
        o_ref[...] = jnp.dot(a_ref[...], g_ref[...], preferred_element_type=F32, precision=lax.Precision.HIGHEST)

    return pl.pallas_call(
        body, name="ada_bwd", out_shape=jax.ShapeDtypeStruct((D, gm_cols.shape[1]), F32), compiler_params=_cparams(),
    )(act_t, gm_cols)


def _row_tile(r):
    for t in (256, 304, 128, 64, 16):
        if r % t == 0:
            return t
    return r


def _sum_parts(parts, name, own=None):
    k, r, ccols = parts.shape
    tr = _row_tile(r)

    def body(*refs):
        p_ref, o_ref = refs[0], refs[-1]
        acc = p_ref[0].astype(F32) if own is None else refs[1][...].astype(F32) + p_ref[0].astype(F32)
        for s in range(1, k):
            acc = acc + p_ref[s].astype(F32)
        o_ref[...] = acc

    blk = pl.BlockSpec((tr, ccols), lambda i: (i, 0))
    return pl.pallas_call(
        body, name=name, grid=(r // tr,),
        in_specs=[pl.BlockSpec((k, tr, ccols), lambda i: (0, i, 0))] + ([] if own is None else [blk]),
        out_specs=blk,
        out_shape=jax.ShapeDtypeStruct((r, ccols), F32),
        compiler_params=_cparams("parallel"),
    )(*((parts,) if own is None else (parts, own)))


def _adamw(w, g, m, v, name):
    r, ccols = w.shape
    tr = _row_tile(r)
    c1 = 1.0 / (1.0 - B1 ** STEP)
    c2 = 1.0 / (1.0 - B2 ** STEP)

    def body(w_ref, g_ref, m_ref, v_ref, d_ref, nm_ref, nv_ref):
        gv = g_ref[...]
        nm = B1 * m_ref[...] + (1.0 - B1) * gv
        nv = B2 * v_ref[...] + (1.0 - B2) * jnp.square(gv)
        nm_ref[...] = nm
        nv_ref[...] = nv
        d_ref[...] = -LR * ((nm * c1) / (jnp.sqrt(nv * c2) + ADAM_EPS) + WD * w_ref[...])

    blk = pl.BlockSpec((tr, ccols), lambda i: (i, 0))
    return pl.pallas_call(
        body, name=name, grid=(r // tr,), in_specs=[blk] * 4, out_specs=[blk] * 3,
        out_shape=[jax.ShapeDtypeStruct((r, ccols), F32)] * 3,
        compiler_params=_cparams("parallel"),
    )(w, g, m, v)


def _sum_adamw(parts, own, w, m, v, name):
    k, r, ccols = parts.shape
    tr = _row_tile(r)
    c1 = 1.0 / (1.0 - B1 ** STEP)
    c2 = 1.0 / (1.0 - B2 ** STEP)

    def body(p_ref, own_ref, w_ref, m_ref, v_ref, g_ref, d_ref, nm_ref, nv_ref):
        gv = own_ref[...].astype(F32)
        for s in range(k):
            gv = gv + p_ref[s].astype(F32)
        g_ref[...] = gv
        nm = B1 * m_ref[...] + (1.0 - B1) * gv
        nv = B2 * v_ref[...] + (1.0 - B2) * jnp.square(gv)
        nm_ref[...] = nm
        nv_ref[...] = nv
        d_ref[...] = -LR * ((nm * c1) / (jnp.sqrt(nv * c2) + ADAM_EPS) + WD * w_ref[...])

    blk = pl.BlockSpec((tr, ccols), lambda i: (i, 0))
    return pl.pallas_call(
        body, name=name, grid=(r // tr,),
        in_specs=[pl.BlockSpec((k, tr, ccols), lambda i: (0, i, 0))] + [blk] * 4, out_specs=[blk] * 4,
        out_shape=[jax.ShapeDtypeStruct((r, ccols), F32)] * 4,
        compiler_params=_cparams("parallel"),
    )(parts, own, w, m, v)


VEC_ROWS = ((0, 6), (6, 7), (9, 11), (11, 14), (7, 8), (8, 9))


def _adamw_vectors(w, g, m, v):
    c1 = 1.0 / (1.0 - B1 ** STEP)
    c2 = 1.0 / (1.0 - B2 ** STEP)

    def put(refs, p):
        for ref, (lo, hi) in zip(refs, VEC_ROWS):
            if ref.shape == (3, HEAD):
                ref[...] = p[lo:hi, :HEAD]
            else:
                ref[...] = jnp.concatenate([p[k:k + 1] for k in range(lo, hi)], axis=1)

    def body(w_ref, g_ref, m_ref, v_ref, *outs):
        gv = g_ref[...]
        nm = B1 * m_ref[...] + (1.0 - B1) * gv
        nv = B2 * v_ref[...] + (1.0 - B2) * jnp.square(gv)
        delta = -LR * ((nm * c1) / (jnp.sqrt(nv * c2) + ADAM_EPS) + WD * w_ref[...])
        for kind, p in enumerate((gv, delta, nm, nv)):
            put(outs[6 * kind:6 * kind + 6], p)

    shapes = [(1, 6 * D), (1, D), (1, 2 * D), (3, HEAD), (1, D), (1, D)]
    out = pl.pallas_call(
        body, name="adamw_vectors", out_shape=[jax.ShapeDtypeStruct(sh, F32) for sh in shapes] * 4, compiler_params=_cparams(),
    )(w, g, m, v)
    fix = lambda t: (t[0], t[1], t[2], t[3][None], t[4], t[5].reshape(D))
    return [fix(out[6 * kind:6 * kind + 6]) for kind in range(4)]


def _pack_vectors(b_ada, g_mix, g_mlp, g_fin, b_gate, conv_w):
    conv_rows = jnp.pad(conv_w.reshape(3, HEAD), ((0, 0), (0, D - HEAD)))
    return jnp.concatenate([b_ada.reshape(6, D), g_mix.reshape(1, D), g_mlp.reshape(1, D), g_fin.reshape(1, D),
                            b_gate.reshape(2, D), conv_rows, jnp.zeros((2, D), F32)], axis=0)


def kernel(x, c, w_ada, b_ada, g_norm_mix, w_in, b_gate, conv_w, w_branch_attn, w_branch_conv, w_out, g_norm_mlp, w_mlp_in, w_mlp_out, g_norm_final, loss_target, m_w_ada, m_b_ada, m_g_norm_mix, m_w_in, m_b_gate, m_conv_w, m_w_branch_attn, m_w_branch_conv, m_w_out, m_g_norm_mlp, m_w_mlp_in, m_w_mlp_out, m_g_norm_final, v_w_ada, v_b_ada, v_g_norm_mix, v_w_in, v_b_gate, v_conv_w, v_w_branch_attn, v_w_branch_conv, v_w_out, v_g_norm_mlp, v_w_mlp_in, v_w_mlp_out, v_g_norm_final):
    S = x.shape[1]
    xi, yi, ci = _my_place()
    me = _dev_index(xi, yi, ci)
    x2 = x.reshape(S, D)
    tgt = loss_target.reshape(S, D)

    pay = jnp.zeros((8, D), F32).at[0].set(c[0]).at[1:4, :HEAD].set(conv_w[0])
    ncol = w_ada.shape[2]
    b_cols = lax.dynamic_slice(b_ada, (0, me * ncol), (1, ncol))
    got, act, mod_all = _conditioning(pay, w_ada[0], b_cols)
    cw8 = jnp.pad(got[:, 1:4, :HEAD].transpose(1, 0, 2).reshape(3, D), ((0, 5), (0, 0)))

    w_in_shard, mod_all = lax.optimization_barrier((w_in[0].T.astype(BF16), mod_all))
    mod = lax.dynamic_index_in_dim(mod_all, me, axis=1, keepdims=False).reshape(6, D)
    (w_int,) = _allgather_weights([w_in_shard])
    late = [w_branch_attn[0].T.astype(BF16), w_branch_conv[0].astype(BF16), w_out[0].astype(BF16),
            w_mlp_in[0].T.astype(BF16), w_mlp_out[0].astype(BF16)]
    w_int, late = lax.optimization_barrier((w_int, late))
    zones = [lax.dynamic_update_slice(lax.empty((N_DEV * t.shape[0], t.shape[1]), BF16), t, (me * t.shape[0], 0)) for t in late]
    ag_mix = _split_start("gather_mix_start", "gather", late[:3], zones[:3])
    ag_mlp = _split_start("gather_mlp_start", "gather", late[3:], zones[3:])

    def mix_weights(o_attn):
        return _split_wait("gather_mix_wait", "gather", *ag_mix[:4], o_attn)[1]

    def mlp_weights(x1):
        return _split_wait("gather_mlp_wait", "gather", *ag_mlp[:4], x1)[1]

    rs = {}

    def mlp_grads_ready(*grads):
        lands = [lax.empty((N_PEER, t.shape[0] // N_DEV, t.shape[1]), BF16) for t in grads]
        rs["mlp"] = _split_start("scatter_mlp_start", "scatter", grads, lands)
        return rs["mlp"][4]

    def other_grads_ready(*grads):
        core = ci.reshape(1).astype(jnp.int32)
        pair = _pair_sums(grads, _sibling_exchange(grads), core)
        lands = [lax.empty((3,) + t.shape[1:], BF16) for t in pair]
        rs["rest"] = _split_start("scatter_rest_start", "chips", pair, lands)
        return rs["rest"][4]

    ba, bb = b_gate[:, :D], b_gate[:, D:]
    grad_x, vec = _local_step(
        x2, tgt, mod + ag_mix[4] + ag_mlp[4], g_norm_mix, g_norm_mlp, g_norm_final.reshape(1, D), ba, bb, cw8, w_int, mix_weights, mlp_weights,
        mlp_grads_ready, other_grads_ready)

    vec_all = _allgather_small(vec, "gather_vec")
    vec_sum = _sum_parts(vec_all, "sum_vec")
    loss = vec_sum[14, 0]
    gm_all = vec_all[:, 0:6, :].reshape(N_DEV, 6 * D)
    gm_cols = lax.dynamic_slice(gm_all, (0, me * ncol), (N_DEV, ncol))
    g_w_ada = _ada_bwd(act.T, gm_cols)
    conv_cols = lax.dynamic_slice(vec_sum[11:14], (0, me * HEAD), (3, HEAD))
    g_pack = jnp.concatenate([vec_sum[0:11], jnp.pad(conv_cols, ((0, 0), (0, D - HEAD))), jnp.zeros((2, D), F32)], axis=0)
    packs = [_pack_vectors(*t) for t in ((b_ada, g_norm_mix, g_norm_mlp, g_norm_final, b_gate, conv_w),
                                         (m_b_ada, m_g_norm_mix, m_g_norm_mlp, m_g_norm_final, m_b_gate, m_conv_w),
                                         (v_b_ada, v_g_norm_mix, v_g_norm_mlp, v_g_norm_final, v_b_gate, v_conv_w))]
    gv, dv, mv, vv = _adamw_vectors(packs[0], g_pack, packs[1], packs[2])
    d_ada, nm_ada, nv_ada = _adamw(w_ada[0], g_w_ada, m_w_ada[0], v_w_ada[0], "adamw_w_ada")

    big = {}
    srcs, lands = _split_wait("scatter_mlp_wait", "scatter", *rs["mlp"][:4], grad_x)
    own = [lax.dynamic_slice(g, (me * land.shape[1], 0), land.shape[1:]) for g, land in zip(srcs, lands)]
    g_mi = _sum_parts(lands[0], "sum_w_mi", own=own[0]).T
    big["w_mi"] = (g_mi[None],) + tuple(t[None] for t in _adamw(w_mlp_in[0], g_mi, m_w_mlp_in[0], v_w_mlp_in[0], "adamw_w_mi"))
    big["w_mo"] = tuple(t[None] for t in _sum_adamw(lands[1], own[1], w_mlp_out[0], m_w_mlp_out[0], v_w_mlp_out[0], "adamw_w_mo"))
    srcs, lands = _split_wait("scatter_rest_wait", "chips", *rs["rest"][:4], grad_x)
    own = [lax.dynamic_index_in_dim(pair, 2 * xi + yi, axis=0, keepdims=False) for pair in srcs]
    big["w_in"] = tuple(t.T[None] for t in _sum_adamw(lands[0], own[0], w_in[0].T, m_w_in[0].T, v_w_in[0].T, "adamw_w_in"))
    g_ba = _sum_parts(lands[1], "sum_w_ba", own=own[1]).T
    big["w_ba"] = (g_ba[None],) + tuple(t[None] for t in _adamw(w_branch_attn[0], g_ba, m_w_branch_attn[0], v_w_branch_attn[0], "adamw_w_ba"))
    big["w_bc"] = tuple(t[None] for t in _sum_adamw(lands[2], own[2], w_branch_conv[0], m_w_branch_conv[0], v_w_branch_conv[0], "adamw_w_bc"))
    big["w_out"] = tuple(t[None] for t in _sum_adamw(lands[3], own[3], w_out[0], m_w_out[0], v_w_out[0], "adamw_w_out"))

    def ordered(k, ada, vecs):
        return (ada[None], vecs[0], vecs[1], big["w_in"][k], vecs[2], vecs[3], big["w_ba"][k], big["w_bc"][k],
                big["w_out"][k], vecs[4], big["w_mi"][k], big["w_mo"][k], vecs[5])

    return (loss, grad_x.reshape(1, S, D), *ordered(0, g_w_ada, gv), *ordered(1, d_ada, dv),
            *ordered(2, nm_ada, mv), *ordered(3, nv_ada, vv))
```

```python
import functools

import numpy as np
import jax
import jax.numpy as jnp
from jax import lax
from jax.experimental import pallas as pl
from jax.experimental.pallas import tpu as pltpu

F32, BF16 = jnp.float32, jnp.bfloat16
D = 1024
HEAD = 128
DILATIONS = (1, 4, 16)
N_SLOT = 4
AOW = N_SLOT * HEAD
DFF = 4 * D
N_DEV = 8
UNROLL = 8
EPS = 1e-6
NEG = -1e30
SCALE = HEAD ** -0.5
LR, B1, B2, ADAM_EPS, WD, STEP = 0.001, 0.9, 0.999, 1e-08, 0.01, 10
V7X_VMEM_LIMIT = 56 * 1024 * 1024
TM = 1024
MESH = pl.DeviceIdType.MESH
AXES = ("x", "y", "c")


def _cparams(*sem):
    if sem:
        return pltpu.CompilerParams(dimension_semantics=sem, vmem_limit_bytes=V7X_VMEM_LIMIT)
    return pltpu.CompilerParams(vmem_limit_bytes=V7X_VMEM_LIMIT)


def _nn(a, b):
    return jnp.dot(a, b, preferred_element_type=F32)


def _nt(a, b):
    return lax.dot_general(a, b, (((1,), (1,)), ((), ())), preferred_element_type=F32)


def _tn(a, b):
    return lax.dot_general(a, b, (((0,), (0,)), ((), ())), preferred_element_type=F32)


def _rms_r(x):
    return lax.rsqrt(jnp.mean(x * x, axis=-1, keepdims=True) + EPS)


def _rms_bwd(x, r, g, dn):
    gy = dn * g
    dx = r * gy - x * (r * r * r) * jnp.mean(x * gy, axis=-1, keepdims=True)
    return dx, dn * (x * r)


def _sigmoid(t):
    return 1.0 / (1.0 + jnp.exp(-t))


def _rowsum(v):
    return jnp.sum(v, axis=0, keepdims=True)


def _vec_spec(n=D):
    return pl.BlockSpec((1, n), lambda *_: (0, 0))


def _const_spec(shape):
    nd = len(shape)
    return pl.BlockSpec(shape, lambda *_: (0,) * nd)


def _win_rowblock(j):
    return jnp.where(j < 9, (j % 3) * 3 + j // 3, j)


def _prenorm(x, g, sc, sh):
    S = x.shape[0]
    tm = TM

    def body(x_ref, g_ref, sc_ref, sh_ref, h_ref):
        xv = x_ref[...]
        h_ref[...] = (xv * _rms_r(xv) * g_ref[...] * (1.0 + sc_ref[...]) + sh_ref[...]).astype(BF16)

    row = pl.BlockSpec((tm, D), lambda i: (i, 0))
    return pl.pallas_call(
        body, name="prenorm", grid=(S // tm,), in_specs=[row, _vec_spec(), _vec_spec(), _vec_spec()], out_specs=row,
        out_shape=jax.ShapeDtypeStruct((S, D), BF16), compiler_params=_cparams("parallel"),
    )(x, g, sc, sh)


def _proj(h, w_int):
    S = h.shape[0]

    def body(h_ref, w_ref, q_ref, e_ref):
        j = pl.program_id(0)
        acc = _nt(h_ref[...], w_ref[...])

        @pl.when(j < 9)
        def _():
            q_ref[0] = acc

        @pl.when(j >= 9)
        def _():
            e_ref[0] = acc.astype(BF16)

    def e_idx(j):
        k = jnp.maximum(j - 9, 0)
        return (k // 2, 0, k % 2)

    return pl.pallas_call(
        body, name="proj", grid=(19,),
        in_specs=[pl.BlockSpec((S, D), lambda j: (0, 0), pipeline_mode=pl.Buffered(1)),
                  pl.BlockSpec((512, D), lambda j: (_win_rowblock(j), 0))],
        out_specs=[pl.BlockSpec((1, S, 512), lambda j: (jnp.minimum(j, 8), 0, 0)), pl.BlockSpec((1, S, 512), e_idx)],
        out_shape=[jax.ShapeDtypeStruct((9, S, 512), F32), jax.ShapeDtypeStruct((5, S, D), BF16)],
        compiler_params=_cparams("arbitrary"),
    )(h, w_int)


def _bias_table():
    slopes = (2.0 ** (-8.0 * np.arange(1, 13, dtype=np.float32) / 12.0)).astype(np.float32)
    qi = np.arange(HEAD)[:, None]
    kj = np.arange(2 * HEAD)[None, :]
    delta = HEAD + qi - kj
    mask = (delta >= 0) & (delta <= HEAD)
    out = np.zeros((3, N_SLOT, HEAD, 2 * HEAD), np.float32)
    for gi, d in enumerate(DILATIONS):
        for j in range(N_SLOT):
            bias = -slopes[gi * N_SLOT + j] * (delta * d).astype(np.float32)
            out[gi, j] = np.where(mask, bias, NEG)
    out_t = np.concatenate([out[..., HEAD:].swapaxes(-1, -2), out[..., :HEAD].swapaxes(-1, -2)], axis=-1)
    return jnp.asarray(out), jnp.asarray(out_t)


def _block_rows(b, d):
    r = b % d
    n = b // d
    st = n * (HEAD * d) + r
    stp = jnp.maximum(n - 1, 0) * (HEAD * d) + r
    return n, st, stp


def _attn_fwd(qkv, bias):
    S = qkv.shape[2]
    nblk = S // HEAD
    rows = 256

    def body(qkv_ref, b_ref, o_ref, lse_ref, o_s, lse_s):
        g = pl.program_id(1)
        bias = b_ref[0, 0]
        col = lax.broadcasted_iota(jnp.int32, bias.shape, 1)
        bias_first = jnp.where(col < HEAD, NEG, bias)

        for gi, d in enumerate(DILATIONS):
            @pl.when(g == gi)
            def _(gi=gi, d=d):
                def step(b, carry):
                    n, st, stp = _block_rows(b, d)
                    cur = pl.ds(st, HEAD, stride=d)
                    prv = pl.ds(stp, HEAD, stride=d)
                    q = qkv_ref.at[0, 0][cur, :].astype(BF16)
                    kw = jnp.concatenate([qkv_ref.at[0, 1][prv, :], qkv_ref.at[0, 1][cur, :]], axis=0).astype(BF16)
                    vw = jnp.concatenate([qkv_ref.at[0, 2][prv, :], qkv_ref.at[0, 2][cur, :]], axis=0).astype(BF16)
                    s = _nt(q, kw) * SCALE + jnp.where(n > 0, bias, bias_first)
                    m = jnp.max(s, axis=-1, keepdims=True)
                    p = jnp.exp(s - m)
                    l = jnp.sum(p, axis=-1, keepdims=True)
                    o_s.at[gi][cur, :] = _nn(p.astype(BF16), vw) / l
                    lse_s.at[gi][cur, :] = jnp.broadcast_to(m + jnp.log(l), (HEAD, HEAD))
                    return carry

                lax.fori_loop(0, nblk, step, 0, unroll=UNROLL)

        @pl.when(g == len(DILATIONS) - 1)
        def _():
            def merge(i, carry):
                r = pl.ds(pl.multiple_of(i * rows, rows), rows)
                ls = [lse_s[k, r, :] for k in range(3)]
                top = jnp.maximum(jnp.maximum(ls[0], ls[1]), ls[2])
                ws = [jnp.exp(t - top) for t in ls]
                den = ws[0] + ws[1] + ws[2]
                o_ref[r, :] = (ws[0] * o_s[0, r, :] + ws[1] * o_s[1, r, :] + ws[2] * o_s[2, r, :]) / den
                lse_ref[r, :] = top + jnp.log(den)
                return carry

            lax.fori_loop(0, S // rows, merge, 0)

    return pl.pallas_call(
        body, name="attn_fwd", grid=(N_SLOT, 3),
        in_specs=[pl.BlockSpec((1, 3, S, HEAD), lambda j, g: (g, 0, 0, j)),
                  pl.BlockSpec((1, 1, HEAD, 2 * HEAD), lambda j, g: (g, j, 0, 0))],
        out_specs=[pl.BlockSpec((S, HEAD), lambda j, g: (0, j)), pl.BlockSpec((S, HEAD), lambda j, g: (0, j))],
        out_shape=[jax.ShapeDtypeStruct((S, AOW), F32), jax.ShapeDtypeStruct((S, AOW), F32)],
        scratch_shapes=[pltpu.VMEM((3, S, HEAD), F32)] * 2,
        compiler_params=_cparams("parallel", "arbitrary"),
    )(qkv, bias)


def _shift_down(z, k, halo_rows):
    out = pltpu.roll(z, k, axis=0)
    top = out[:8]
    rid = lax.broadcasted_iota(jnp.int32, top.shape, 0)
    for t in range(k):
        top = jnp.where(rid == t, halo_rows[t], top)
    return jnp.concatenate([top, out[8:]], axis=0)


def _shift_up(z, k, halo_rows):
    n = z.shape[0]
    out = pltpu.roll(z, n - k, axis=0)
    bottom = out[n - 8:]
    rid = lax.broadcasted_iota(jnp.int32, bottom.shape, 0)
    for t in range(k):
        bottom = jnp.where(rid == 8 - k + t, halo_rows[t], bottom)
    return jnp.concatenate([out[:n - 8], bottom], axis=0)


def _e_spec(chunk, tm):
    return pl.BlockSpec((1, tm, D), lambda i, c=chunk: (c, i, 0))


def _e_prev_spec(chunk, tm):
    return pl.BlockSpec((1, 16, D), lambda i, c=chunk: (c, jnp.maximum(i * (tm // 16) - 1, 0), 0))


def _e_next_spec(chunk, tm, S):
    return pl.BlockSpec((1, 16, D), lambda i, c=chunk: (c, jnp.minimum((i + 1) * (tm // 16), S // 16 - 1), 0))


def _mix(o_attn, e, cw8, ba, bb, w_bat, w_bc):
    S = o_attn.shape[0]
    tm = 256

    def body(o_ref, cb_ref, cc_ref, cx_ref, ga_ref, gb_ref, ccp_ref, cxp_ref, cw_ref, ba_ref, bb_ref, wba_ref, wbc_ref,
             obf_ref, cbu_ref, ya_ref, yc_ref, mg_ref):
        i = pl.program_id(0)
        o = o_ref[...].astype(BF16)
        obf_ref[...] = o
        ya = _nt(o, wba_ref[...])
        z = cc_ref[0].astype(F32) * cx_ref[0].astype(F32)
        zp = ccp_ref[0].astype(F32) * cxp_ref[0].astype(F32) * (i > 0).astype(F32)
        z1 = _shift_down(z, 1, [zp[15:16]])
        z2 = _shift_down(z, 2, [zp[14:15], zp[15:16]])
        cw = cw_ref[...]
        u = cw[0:1] * z2 + cw[1:2] * z1 + cw[2:3] * z
        cbu = (cb_ref[0].astype(F32) * u).astype(BF16)
        cbu_ref[...] = cbu
        yc = _nn(cbu, wbc_ref[...])
        sa = _sigmoid(ga_ref[0].astype(F32) + ba_ref[...])
        sb = _sigmoid(gb_ref[0].astype(F32) + bb_ref[...])
        ya_ref[...] = ya.astype(BF16)
        yc_ref[...] = yc.astype(BF16)
        mg_ref[...] = (sa * ya + sb * yc).astype(BF16)

    row = lambda w: pl.BlockSpec((tm, w), lambda i: (i, 0))
    return pl.pallas_call(
        body, name="mix", grid=(S // tm,),
        in_specs=[row(AOW)] + [_e_spec(c, tm) for c in range(5)] + [_e_prev_spec(1, tm), _e_prev_spec(2, tm),
                  _const_spec((8, D)), _vec_spec(), _vec_spec(), _const_spec((D, AOW)), _const_spec((D, D))],
        out_specs=[row(AOW), row(D), row(D), row(D), row(D)],
        out_shape=[jax.ShapeDtypeStruct((S, AOW), BF16)] + [jax.ShapeDtypeStruct((S, D), BF16)] * 4,
        compiler_params=_cparams("parallel"),
    )(o_attn, e, e, e, e, e, e, e, cw8, ba, bb, w_bat, w_bc)


def _out_proj(merged, w_out, x, gate1, g_mlp, sc2, sh2):
    S = x.shape[0]
    tm = TM

    def body(mg_ref, w_ref, x_ref, gt_ref, g_ref, sc_ref, sh_ref, x1_ref, mo_ref, h2_ref):
        mo = _nn(mg_ref[...], w_ref[...])
        mo_ref[...] = mo.astype(BF16)
        x1 = x_ref[...] + gt_ref[...] * mo
        x1_ref[...] = x1
        h2 = x1 * _rms_r(x1) * g_ref[...] * (1.0 + sc_ref[...]) + sh_ref[...]
        h2_ref[...] = h2.astype(BF16)

    row = pl.BlockSpec((tm, D), lambda i: (i, 0))
    return pl.pallas_call(
        body, name="out_proj", grid=(S // tm,),
        in_specs=[row, _const_spec((D, D)), row, _vec_spec(), _vec_spec(), _vec_spec(), _vec_spec()],
        out_specs=[row, row, row],
        out_shape=[jax.ShapeDtypeStruct((S, D), F32), jax.ShapeDtypeStruct((S, D), BF16), jax.ShapeDtypeStruct((S, D), BF16)],
        compiler_params=_cparams("parallel"),
    )(merged, w_out, x, gate1, g_mlp, sc2, sh2)


def _mlp_in(h2, w_mit):
    S = h2.shape[0]
    tm, tn = TM, 2048

    def body(h_ref, w_ref, a_ref, f_ref):
        a = _nt(h_ref[...], w_ref[...])
        a_ref[...] = a.astype(BF16)
        f_ref[...] = jnp.square(jnp.maximum(a, 0.0)).astype(BF16)

    blk = pl.BlockSpec((tm, tn), lambda i, j: (i, j))
    return pl.pallas_call(
        body, name="mlp_in", grid=(S // tm, DFF // tn),
        in_specs=[pl.BlockSpec((tm, D), lambda i, j: (i, 0)), pl.BlockSpec((tn, D), lambda i, j: (j, 0))],
        out_specs=[blk, blk],
        out_shape=[jax.ShapeDtypeStruct((S, DFF), BF16)] * 2,
        compiler_params=_cparams("parallel", "parallel"),
    )(h2, w_mit)


def _mlp_out(f, w_mo, x1, gate2, g_fin, tgt):
    S = x1.shape[0]
    tm = 512
    half = tm // 2

    def body(f_ref, w_ref, x1_ref, gt_ref, g_ref, t_ref, mlp_ref, dx2_ref, pv_ref):
        @pl.when(pl.program_id(0) == 0)
        def _():
            pv_ref[...] = jnp.zeros_like(pv_ref)

        g = g_ref[...]
        for hs in (pl.ds(0, half), pl.ds(half, half)):
            mlp = _nn(f_ref[hs, :], w_ref[...])
            mlp_ref[hs, :] = mlp.astype(BF16)
            x2 = x1_ref[hs, :] + gt_ref[...] * mlp
            r = _rms_r(x2)
            err = x2 * r * g - t_ref[hs, :]
            dx2, pg = _rms_bwd(x2, r, g, err * (1.0 / D))
            dx2_ref[hs, :] = dx2
            pv_ref[0:1, :] += _rowsum(pg)
            pv_ref[1:2, :] += 0.5 * _rowsum(jnp.mean(err * err, axis=-1, keepdims=True))

    row = pl.BlockSpec((tm, D), lambda i: (i, 0))
    return pl.pallas_call(
        body, name="mlp_out", grid=(S // tm,),
        in_specs=[pl.BlockSpec((tm, DFF), lambda i: (i, 0)), _const_spec((DFF, D)), row, _vec_spec(), _vec_spec(), row],
        out_specs=[row, row, _const_spec((8, D))],
        out_shape=[jax.ShapeDtypeStruct((S, D), BF16), jax.ShapeDtypeStruct((S, D), F32), jax.ShapeDtypeStruct((8, D), F32)],
        compiler_params=_cparams("arbitrary"),
    )(f, w_mo, x1, gate2, g_fin, tgt)


def _bwd_mlp_a(dx2, gate2, mlp, w_mo, a):
    S = dx2.shape[0]
    tm = 512
    half = tm // 2

    def body(dx_ref, gt_ref, mlp_ref, w_ref, a_ref, da_ref, dmo_ref, pv_ref):
        @pl.when(pl.program_id(0) == 0)
        def _():
            pv_ref[...] = jnp.zeros_like(pv_ref)

        for hs in (pl.ds(0, half), pl.ds(half, half)):
            dx = dx_ref[hs, :]
            dmo = (dx * gt_ref[...]).astype(BF16)
            dmo_ref[hs, :] = dmo
            pv_ref[0:1, :] += _rowsum(dx * mlp_ref[hs, :].astype(F32))
            df = _nt(dmo, w_ref[...])
            da_ref[hs, :] = (df * (2.0 * jnp.maximum(a_ref[hs, :].astype(F32), 0.0))).astype(BF16)

    row = pl.BlockSpec((tm, D), lambda i: (i, 0))
    wide = pl.BlockSpec((tm, DFF), lambda i: (i, 0))
    return pl.pallas_call(
        body, name="bwd_mlp_a", grid=(S // tm,),
        in_specs=[row, _vec_spec(), row, _const_spec((DFF, D)), wide],
        out_specs=[wide, row, _const_spec((8, D))],
        out_shape=[jax.ShapeDtypeStruct((S, DFF), BF16), jax.ShapeDtypeStruct((S, D), BF16), jax.ShapeDtypeStruct((8, D), F32)],
        compiler_params=_cparams("arbitrary"),
    )(dx2, gate2, mlp, w_mo, a)


def _bwd_mlp_b(da, w_mit, x1, dx2, g_mlp, sc2):
    S = x1.shape[0]
    tm = 512
    half = tm // 2

    def body(da_ref, w_ref, x1_ref, dx2_ref, g_ref, sc_ref, dx1_ref, pv_ref):
        @pl.when(pl.program_id(0) == 0)
        def _():
            pv_ref[...] = jnp.zeros_like(pv_ref)

        g = g_ref[...]
        for hs in (pl.ds(0, half), pl.ds(half, half)):
            dh = _nn(da_ref[hs, :], w_ref[...])
            x1 = x1_ref[hs, :]
            r = _rms_r(x1)
            dxn, pg = _rms_bwd(x1, r, g, dh * (1.0 + sc_ref[...]))
            dx1_ref[hs, :] = dx2_ref[hs, :] + dxn
            pv_ref[0:1, :] += _rowsum(dh)
            pv_ref[1:2, :] += _rowsum(dh * (x1 * r * g))
            pv_ref[2:3, :] += _rowsum(pg)

    row = pl.BlockSpec((tm, D), lambda i: (i, 0))
    return pl.pallas_call(
        body, name="bwd_mlp_b", grid=(S // tm,),
        in_specs=[pl.BlockSpec((tm, DFF), lambda i: (i, 0)), _const_spec((DFF, D)), row, row, _vec_spec(), _vec_spec()],
        out_specs=[row, _const_spec((8, D))],
        out_shape=[jax.ShapeDtypeStruct((S, D), F32), jax.ShapeDtypeStruct((8, D), F32)],
        compiler_params=_cparams("arbitrary"),
    )(da, w_mit, x1, dx2, g_mlp, sc2)


def _bwd_mix(dx1, gate1, mo, e, cw8, ba, bb, ya, yc, o_attn, w_out, w_bc, w_bat):
    S = dx1.shape[0]
    tm = 256
    n_tiles = S // tm

    def body(dx_ref, dxn_ref, gt_ref, mo_ref, cb_ref, cc_ref, cx_ref, ga_ref, gb_ref, cbn_ref, gbn_ref, ccp_ref, cxp_ref,
             cw_ref, ba_ref, bb_ref, ya_ref, yc_ref, o_ref, wout_ref, wbc_ref, wba_ref,
             dmo_ref, dya_ref, dyc_ref, do_ref, dl_ref, de_ref, pv_ref):
        i = pl.program_id(0)

        @pl.when(i == 0)
        def _():
            pv_ref[...] = jnp.zeros_like(pv_ref)

        gate = gt_ref[...]
        bbv = bb_ref[...]

        def conv_branch_grad(dx_rows, gb_rows):
            dmo = (dx_rows * gate).astype(BF16)
            dmg = _nt(dmo, wout_ref[...])
            sb = _sigmoid(gb_rows + bbv)
            dyc = dmg * sb
            return dmo, dmg, sb, dyc, _nt(dyc.astype(BF16), wbc_ref[...])

        dx = dx_ref[...]
        cb = cb_ref[0].astype(F32)
        cc = cc_ref[0].astype(F32)
        cx = cx_ref[0].astype(F32)
        dmo, dmg, sb, dyc, dcbu = conv_branch_grad(dx, gb_ref[0].astype(F32))
        dmo_ref[...] = dmo
        pv_ref[0:1, :] += _rowsum(dx * mo_ref[...].astype(F32))
        sa = _sigmoid(ga_ref[0].astype(F32) + ba_ref[...])
        dya = (dmg * sa).astype(BF16)
        dya_ref[...] = dya
        dyc_ref[...] = dyc.astype(BF16)
        dga = dmg * ya_ref[...].astype(F32) * sa * (1.0 - sa)
        dgb = dmg * yc_ref[...].astype(F32) * sb * (1.0 - sb)
        pv_ref[1:2, :] += _rowsum(dga)
        pv_ref[2:3, :] += _rowsum(dgb)

        do = _nn(dya, wba_ref[...])
        do_ref[...] = do
        prod = do * o_ref[...]
        dl_ref[...] = jnp.concatenate(
            [jnp.broadcast_to(jnp.sum(prod[:, s * HEAD:(s + 1) * HEAD], axis=-1, keepdims=True), (tm, HEAD))
             for s in range(N_SLOT)], axis=1)

        z = cc * cx
        zp = ccp_ref[0].astype(F32) * cxp_ref[0].astype(F32) * (i > 0).astype(F32)
        z1 = _shift_down(z, 1, [zp[15:16]])
        z2 = _shift_down(z, 2, [zp[14:15], zp[15:16]])
        cw = cw_ref[...]
        u = cw[0:1] * z2 + cw[1:2] * z1 + cw[2:3] * z
        du = dcbu * cb
        dcbu_n = conv_branch_grad(dxn_ref[...], gbn_ref[0].astype(F32))[4]
        du_n = dcbu_n * cbn_ref[0].astype(F32) * (i < n_tiles - 1).astype(F32)
        du1 = _shift_up(du, 1, [du_n[0:1]])
        du2 = _shift_up(du, 2, [du_n[0:1], du_n[1:2]])
        dz = cw[2:3] * du + cw[1:2] * du1 + cw[0:1] * du2
        pv_ref[3:4, :] += _rowsum(du * z2)
        pv_ref[4:5, :] += _rowsum(du * z1)
        pv_ref[5:6, :] += _rowsum(du * z)

        de_ref[0] = (dcbu * u).astype(BF16)
        de_ref[1] = (dz * cx).astype(BF16)
        de_ref[2] = (dz * cc).astype(BF16)
        de_ref[3] = dga.astype(BF16)
        de_ref[4] = dgb.astype(BF16)

    row = lambda w: pl.BlockSpec((tm, w), lambda i: (i, 0))
    nxt = pl.BlockSpec((16, D), lambda i: (jnp.minimum((i + 1) * (tm // 16), S // 16 - 1), 0))
    return pl.pallas_call(
        body, name="bwd_mix", grid=(n_tiles,),
        in_specs=[row(D), nxt, _vec_spec(), row(D)] + [_e_spec(c, tm) for c in range(5)]
                 + [_e_next_spec(0, tm, S), _e_next_spec(4, tm, S), _e_prev_spec(1, tm), _e_prev_spec(2, tm),
                    _const_spec((8, D)), _vec_spec(), _vec_spec(), row(D), row(D), row(AOW),
                    _const_spec((D, D)), _const_spec((D, D)), _const_spec((D, AOW))],
        out_specs=[row(D), row(D), row(D), row(AOW), row(AOW), pl.BlockSpec((5, tm, D), lambda i: (0, i, 0)),
                   _const_spec((8, D))],
        out_shape=[jax.ShapeDtypeStruct((S, D), BF16)] * 3 + [jax.ShapeDtypeStruct((S, AOW), F32)] * 2
                  + [jax.ShapeDtypeStruct((5, S, D), BF16), jax.ShapeDtypeStruct((8, D), F32)],
        compiler_params=_cparams("arbitrary"),
    )(dx1, dx1, gate1, mo, e, e, e, e, e, e, e, e, e, cw8, ba, bb, ya, yc, o_attn, w_out, w_bc, w_bat)


def _attn_bwd(qkv, do, lse, dl, bias_t):
    S = qkv.shape[2]
    nblk = S // HEAD

    def body(qkv_ref, do_ref, lse_ref, dl_ref, b_ref, d_ref):
        g = pl.program_id(1)
        bias = b_ref[0, 0]
        col = lax.broadcasted_iota(jnp.int32, bias.shape, 1)
        bias_last = jnp.where(col >= HEAD, NEG, bias)
        eye = (lax.broadcasted_iota(jnp.int32, (HEAD, HEAD), 0) == lax.broadcasted_iota(jnp.int32, (HEAD, HEAD), 1)).astype(F32)

        def as_row(t):
            return jnp.sum(t * eye, axis=0, keepdims=True)

        for gi, d in enumerate(DILATIONS):
            @pl.when(g == gi)
            def _(d=d):
                nb = nblk // d

                def step(b, dq_part):
                    r, n = b // nb, b % nb
                    cur = pl.ds(n * (HEAD * d) + r, HEAD, stride=d)
                    nxt = pl.ds(jnp.minimum(n + 1, nb - 1) * (HEAD * d) + r, HEAD, stride=d)
                    two = lambda ref: jnp.concatenate([ref[cur, :], ref[nxt, :]], axis=0)
                    two_rows = lambda ref: jnp.concatenate([as_row(ref[cur, :]), as_row(ref[nxt, :])], axis=1)
                    q2 = two(qkv_ref.at[0, 0]).astype(BF16)
                    do2 = two(do_ref).astype(BF16)
                    k = qkv_ref.at[0, 1][cur, :].astype(BF16)
                    v = qkv_ref.at[0, 2][cur, :].astype(BF16)
                    s = _nt(k, q2) * SCALE + jnp.where(n < nb - 1, bias, bias_last)
                    p = jnp.exp(s - two_rows(lse_ref))
                    d_ref.at[0, 2][cur, :] = _nn(p.astype(BF16), do2)
                    dp = _nt(v, do2)
                    ds = (p * (dp - two_rows(dl_ref)) * SCALE).astype(BF16)
                    d_ref.at[0, 1][cur, :] = _nn(ds, q2)
                    dq2 = _tn(ds, k)
                    d_ref.at[0, 0][cur, :] = dq2[:HEAD] + jnp.where(n > 0, dq_part, 0.0)
                    return dq2[HEAD:]

                def steps(i, dq_part):
                    for u in range(UNROLL):
                        dq_part = step(i * UNROLL + u, dq_part)
                    return dq_part

                lax.fori_loop(0, nblk // UNROLL, steps, jnp.zeros((HEAD, HEAD), F32))

    col_blk = pl.BlockSpec((S, HEAD), lambda j, g: (0, j))
    qkv_blk = pl.BlockSpec((1, 3, S, HEAD), lambda j, g: (g, 0, 0, j))
    return pl.pallas_call(
        body, name="attn_bwd", grid=(N_SLOT, 3),
        in_specs=[qkv_blk, col_blk, col_blk, col_blk, pl.BlockSpec((1, 1, HEAD, 2 * HEAD), lambda j, g: (g, j, 0, 0))],
        out_specs=qkv_blk,
        out_shape=jax.ShapeDtypeStruct((3, 3, S, AOW), F32),
        compiler_params=_cparams("parallel", "arbitrary"),
    )(qkv, do, lse, dl, bias_t)


def _bwd_in(dqkv, de, w_int, x, dx1, g_mix, sc1):
    S = x.shape[0]
    tm = TM
    dqkv = dqkv.reshape(3, 3, S, AOW)

    def body(dq_ref, de_ref, wq_ref, wk_ref, wv_ref, wa_ref, wb_ref, x_ref, dx1_ref, g_ref, sc_ref, gx_ref, pv_ref):
        acc = gx_ref
        i, k = pl.program_id(0), pl.program_id(1)

        @pl.when((i == 0) & (k == 0))
        def _():
            pv_ref[...] = jnp.zeros_like(pv_ref)

        @pl.when(k == 0)
        def _():
            acc[...] = jnp.zeros_like(acc)

        @pl.when(k < 3)
        def _():
            lhs = jnp.concatenate([dq_ref[0, t].astype(BF16) for t in range(3)], axis=1)
            acc[...] += _nn(lhs, jnp.concatenate([wq_ref[...], wk_ref[...], wv_ref[...]], axis=0))

        @pl.when(k >= 3)
        def _():
            acc[...] += _nn(de_ref[0], jnp.concatenate([wa_ref[...], wb_ref[...]], axis=0))

        @pl.when(k == 7)
        def _():
            dh = acc[...]
            xv = x_ref[...]
            r = _rms_r(xv)
            g = g_ref[...]
            dxn, pg = _rms_bwd(xv, r, g, dh * (1.0 + sc_ref[...]))
            gx_ref[...] = dx1_ref[...] + dxn
            pv_ref[0:1, :] += _rowsum(dh)
            pv_ref[1:2, :] += _rowsum(dh * (xv * r * g))
            pv_ref[2:3, :] += _rowsum(pg)

    grp = lambda k: jnp.minimum(k, 2)
    chunk = lambda k: jnp.maximum(k - 3, 0)
    wblk = lambda f: pl.BlockSpec((512, D), lambda i, k: (f(k), 0))
    row = pl.BlockSpec((tm, D), lambda i, k: (i, 0))
    once = pl.BlockSpec((tm, D), lambda i, k: (i, 0), pipeline_mode=pl.Buffered(1))
    return pl.pallas_call(
        body, name="bwd_in", grid=(S // tm, 8),
        in_specs=[pl.BlockSpec((1, 3, tm, 512), lambda i, k: (grp(k), 0, i, 0)),
                  pl.BlockSpec((1, tm, D), lambda i, k: (chunk(k), i, 0)),
                  wblk(grp), wblk(lambda k: 3 + grp(k)), wblk(lambda k: 6 + grp(k)),
                  wblk(lambda k: 9 + 2 * chunk(k)), wblk(lambda k: 10 + 2 * chunk(k)),
                  once, once, _vec_spec(), _vec_spec()],
        out_specs=[row, _const_spec((8, D))],
        out_shape=[jax.ShapeDtypeStruct((S, D), F32), jax.ShapeDtypeStruct((8, D), F32)],
        compiler_params=_cparams("arbitrary", "arbitrary"),
    )(dqkv, de, w_int, w_int, w_int, w_int, w_int, x, dx1, g_mix, sc1)


def _grad_w(name, a, b):
    S, ka = a.shape
    nb = b.shape[1]

    def body(a_ref, b_ref, o_ref):
        o_ref[...] = _tn(a_ref[...], b_ref[...]).astype(BF16)

    return pl.pallas_call(
        body, name=name, grid=(ka // 512,),
        in_specs=[pl.BlockSpec((S, 512), lambda n: (0, n)), pl.BlockSpec((S, nb), lambda n: (0, 0))],
        out_specs=pl.BlockSpec((512, nb), lambda n: (n, 0)),
        out_shape=jax.ShapeDtypeStruct((ka, nb), BF16),
        compiler_params=_cparams("parallel"),
    )(a, b)


def _grad_w_small(dya, o_bf, cbu, dyc, merged, dmo):
    S = dya.shape[0]

    def body(dya_ref, o_ref, cbu_ref, dyc_ref, mg_ref, dmo_ref, gba_ref, gbc_ref, gout_ref):
        gba_ref[...] = _tn(dya_ref[...], o_ref[...]).astype(BF16)
        gbc_ref[...] = _tn(cbu_ref[...], dyc_ref[...]).astype(BF16)
        gout_ref[...] = _tn(mg_ref[...], dmo_ref[...]).astype(BF16)

    a_blk = pl.BlockSpec((S, 512), lambda n: (0, n))
    whole = lambda w: pl.BlockSpec((S, w), lambda n: (0, 0))
    out = lambda w: pl.BlockSpec((512, w), lambda n: (n, 0))
    return pl.pallas_call(
        body, name="grad_w_small", grid=(D // 512,),
        in_specs=[a_blk, whole(AOW), a_blk, whole(D), a_blk, whole(D)],
        out_specs=[out(AOW), out(D), out(D)],
        out_shape=[jax.ShapeDtypeStruct((D, AOW), BF16), jax.ShapeDtypeStruct((D, D), BF16), jax.ShapeDtypeStruct((D, D), BF16)],
        compiler_params=_cparams("parallel"),
    )(dya, o_bf, cbu, dyc, merged, dmo)


def _grad_w_in(dqkv, de, h):
    S = h.shape[0]

    def body(dq_ref, de_ref, h_ref, o_ref):
        n = pl.program_id(0)

        @pl.when(n < 9)
        def _():
            o_ref[...] = _tn(dq_ref[0].astype(BF16), h_ref[...]).astype(BF16)

        @pl.when(n >= 9)
        def _():
            o_ref[...] = _tn(de_ref[0], h_ref[...]).astype(BF16)

    def e_idx(n):
        kk = jnp.maximum(n - 9, 0)
        return (kk // 2, 0, kk % 2)

    return pl.pallas_call(
        body, name="grad_w_in", grid=(19,),
        in_specs=[pl.BlockSpec((1, S, 512), lambda n: (jnp.minimum(n, 8), 0, 0)), pl.BlockSpec((1, S, 512), e_idx),
                  pl.BlockSpec((S, D), lambda n: (0, 0))],
        out_specs=pl.BlockSpec((512, D), lambda n: (_win_rowblock(n), 0)),
        out_shape=jax.ShapeDtypeStruct((19 * 512, D), BF16),
        compiler_params=_cparams("parallel"),
    )(dqkv, de, h)


def _local_step(x, h, tgt, mod, g_mix, g_mlp, g_fin, ba, bb, cw8, w_int, mix_weights, mlp_weights, mlp_grads_ready, other_grads_ready):
    S = x.shape[0]
    sh1, sc1, gt1, sh2, sc2, gt2 = [mod[k:k + 1] for k in range(6)]
    bias, bias_t = _bias_table()

    qkv, e = _proj(h, w_int)
    qkv = qkv.reshape(3, 3, S, AOW)
    o_attn, lse = _attn_fwd(qkv, bias)
    w_bat, w_bc, w_out = mix_weights(o_attn)
    o_bf, cbu, ya, yc, merged = _mix(o_attn, e, cw8, ba, bb, w_bat, w_bc)
    x1, mo, h2 = _out_proj(merged, w_out, x, gt1, g_mlp, sc2, sh2)
    w_mit, w_mo = mlp_weights(x1)
    a, f = _mlp_in(h2, w_mit)
    mlp, dx2, pv_f = _mlp_out(f, w_mo, x1, gt2, g_fin, tgt)

    da, dmo2, pv_a = _bwd_mlp_a(dx2, gt2, mlp, w_mo, a)
    dx1, pv_b = _bwd_mlp_b(da, w_mit, x1, dx2, g_mlp, sc2)
    zero = mlp_grads_ready(_grad_w("grad_w_mi", da, h2), _grad_w("grad_w_mo", f, dmo2))
    dmo, dya, dyc, do, dl, de, pv_m = _bwd_mix(dx1, gt1 + zero, mo, e, cw8, ba, bb, ya, yc, o_attn, w_out, w_bc, w_bat)
    dqkv = _attn_bwd(qkv, do, lse, dl, bias_t).reshape(9, S, AOW)
    zero = other_grads_ready(_grad_w_in(dqkv, de, h), *_grad_w_small(dya, o_bf, cbu, dyc, merged, dmo))
    grad_x, pv_i = _bwd_in(dqkv, de, w_int, x, dx1, g_mix, sc1 + zero)

    vec = jnp.concatenate([pv_i[0:2], pv_m[0:1], pv_b[0:2], pv_a[0:1], pv_i[2:3], pv_b[2:3], pv_f[0:1],
                           pv_m[1:3], pv_m[3:6], pv_f[1:2], jnp.zeros((1, D), F32)], axis=0)
    return grad_x, vec


def _my_place():
    return lax.axis_index("x"), lax.axis_index("y"), lax.axis_index("c")


def _dev_index(px, py, pc):
    return 4 * px + 2 * py + pc


def _allgather_weights(shards):
    nw = len(shards)
    HBM = pl.BlockSpec(memory_space=pl.ANY)

    def body(*refs):
        sh, full = refs[:nw], refs[nw:2 * nw]
        send_sems, recv_sems, local_sems = refs[2 * nw:]
        x, y, c = _my_place()
        me, sibling = (x, y, c), (x, y, 1 - c)
        chips = [(1 - x, y), (x, 1 - y), (1 - x, 1 - y)]

        def rows(w, px, py, pc):
            r = sh[w].shape[0]
            return full[w].at[pl.ds(pl.multiple_of(_dev_index(px, py, pc) * r, 16), r), :]

        def copy(w, k, block, to, src=None):
            return pltpu.make_async_remote_copy(
                src_ref=rows(w, *block) if src is None else src, dst_ref=rows(w, *block),
                send_sem=send_sems.at[w, k], recv_sem=recv_sems.at[w, k], device_id=to, device_id_type=MESH)

        mine = [pltpu.make_async_copy(sh[w], rows(w, *me), local_sems.at[w]) for w in range(nw)]
        for cp in mine:
            cp.start()
        first = []
        for w in range(nw):
            first.append(copy(w, 0, me, sibling, src=sh[w]))
            first += [copy(w, 1 + j, me, (*chip, c), src=sh[w]) for j, chip in enumerate(chips)]
        for cp in first:
            cp.start()
        passed = []
        for w in range(nw):
            for j, chip in enumerate(chips):
                copy(w, 1 + j, (*chip, c), me).wait_recv()
                fwd = copy(w, 4 + j, (*chip, c), sibling)
                fwd.start()
                passed.append(fwd)
        for w in range(nw):
            copy(w, 0, sibling, me).wait_recv()
            for j, chip in enumerate(chips):
                copy(w, 4 + j, (*chip, 1 - c), me).wait_recv()
        for cp in first + passed:
            cp.wait_send()
        for cp in mine:
            cp.wait()

    return pl.pallas_call(
        body, name="allgather_weights",
        out_shape=[jax.ShapeDtypeStruct((N_DEV * s.shape[0], s.shape[1]), s.dtype) for s in shards],
        in_specs=[HBM] * nw, out_specs=[HBM] * nw,
        scratch_shapes=[pltpu.SemaphoreType.DMA((nw, 7)), pltpu.SemaphoreType.DMA((nw, 7)), pltpu.SemaphoreType.DMA((nw,))],
    )(*shards)


def _peer(x, y, c, m):
    return (x ^ ((m >> 2) & 1), y ^ ((m >> 1) & 1), c ^ (m & 1))


HBM_SPEC = pl.BlockSpec(memory_space=pltpu.HBM)
SEM_SPEC = pl.BlockSpec(memory_space=pltpu.SEMAPHORE)
N_PEER = N_DEV - 1


SPLIT_MASKS = {"gather": tuple(range(1, N_DEV)), "gather_near": (1, 2, 4, 6), "scatter": tuple(range(1, N_DEV)),
               "chips": (2, 4, 6)}


def _split_copy(mode, src_ref, land_ref, send_sems, recv_sems, w, j, place, arriving=False):
    x, y, c = place
    masks = SPLIT_MASKS[mode]
    peer = _peer(x, y, c, masks[j])
    k = w * len(masks) + j
    sender, receiver = ((peer, (x, y, c)) if arriving else ((x, y, c), peer))
    if mode.startswith("gather"):
        r = src_ref.shape[0]
        src, dst = src_ref, land_ref.at[pl.ds(pl.multiple_of(_dev_index(*sender) * r, 16), r), :]
    elif mode == "scatter":
        r = land_ref.shape[1]
        src, dst = src_ref.at[pl.ds(pl.multiple_of(_dev_index(*receiver) * r, 16), r), :], land_ref.at[j]
    else:
        src, dst = src_ref.at[2 * receiver[0] + receiver[1]], land_ref.at[j]
    return pltpu.make_async_remote_copy(src_ref=src, dst_ref=dst, send_sem=send_sems.at[k], recv_sem=recv_sems.at[k],
                                        device_id=peer, device_id_type=MESH)


def _split_start(name, mode, srcs, lands):
    n = len(srcs)
    nm = len(SPLIT_MASKS[mode])

    def body(*refs):
        src, land = refs[:n], refs[n:2 * n]
        send_sems, recv_sems = refs[2 * n], refs[2 * n + 1]
        token = refs[-1]
        place = _my_place()
        for w in range(n):
            for j in range(nm):
                _split_copy(mode, src[w], land[w], send_sems, recv_sems, w, j, place).start()
        token[...] = jnp.zeros_like(token)

    hbm = lambda t: pltpu.HBM(t.shape, t.dtype)
    out = pl.pallas_call(
        body, name=name,
        out_shape=(pltpu.SemaphoreType.DMA((n * nm,)), pltpu.SemaphoreType.DMA((n * nm,)), *[hbm(t) for t in srcs],
                   *[hbm(t) for t in lands], jax.ShapeDtypeStruct((8, 128), F32)),
        in_specs=(HBM_SPEC,) * (2 * n),
        out_specs=(SEM_SPEC, SEM_SPEC) + (HBM_SPEC,) * (2 * n) + (pl.BlockSpec(memory_space=pltpu.VMEM),),
        input_output_aliases={i: 2 + i for i in range(2 * n)},
        compiler_params=pltpu.CompilerParams(has_side_effects=pltpu.SideEffectType.DATAFLOW_SIDE_EFFECTING),
    )(*[pltpu.with_memory_space_constraint(t, pltpu.HBM) for t in (*srcs, *lands)])
    return out[0], out[1], out[2:2 + n], out[2 + n:2 + 2 * n], out[-1][0:1, 0:1]


def _split_wait(name, mode, send_sems, recv_sems, srcs, lands, after):
    n = len(srcs)

    def body(*refs):
        src, land = refs[:n], refs[n:2 * n]
        ssem, rsem = refs[2 * n], refs[2 * n + 1]
        place = _my_place()
        for w in range(n):
            for j in range(len(SPLIT_MASKS[mode])):
                _split_copy(mode, src[w], land[w], ssem, rsem, w, j, place).wait_send()
                _split_copy(mode, src[w], land[w], ssem, rsem, w, j, place, arriving=True).wait_recv()
        if mode.startswith("gather"):
            local_sems = refs[-1]
            mine = []
            for w in range(n):
                r = src[w].shape[0]
                rows = land[w].at[pl.ds(pl.multiple_of(_dev_index(*place) * r, 16), r), :]
                mine.append(pltpu.make_async_copy(src[w], rows, local_sems.at[w]))
            for cp in mine:
                cp.start()
            for cp in mine:
                cp.wait()

    hbm = lambda t: pltpu.HBM(t.shape, t.dtype)
    out = pl.pallas_call(
        body, name=name,
        out_shape=tuple(hbm(t) for t in (*srcs, *lands)),
        in_specs=(HBM_SPEC,) * (2 * n) + (SEM_SPEC, SEM_SPEC, pl.BlockSpec(memory_space=pl.ANY)),
        out_specs=(HBM_SPEC,) * (2 * n),
        input_output_aliases={i: i for i in range(2 * n)},
        scratch_shapes=[pltpu.SemaphoreType.DMA((n,))] if mode.startswith("gather") else [],
        compiler_params=pltpu.CompilerParams(has_side_effects=pltpu.SideEffectType.DATAFLOW_SIDE_EFFECTING),
    )(*srcs, *lands, send_sems, recv_sems, after)
    return out[:n], out[n:]


def _forward_copy(zone_ref, send_sems, recv_sems, j, place, arriving=False):
    x, y, c = place
    r = zone_ref.shape[0] // N_DEV
    chip = _peer(x, y, c, SPLIT_MASKS["chips"][j])
    owner = _dev_index(chip[0], chip[1], 1 - c if arriving else c)
    rows = zone_ref.at[pl.ds(pl.multiple_of(owner * r, 16), r), :]
    return pltpu.make_async_remote_copy(src_ref=rows, dst_ref=rows, send_sem=send_sems.at[j], recv_sem=recv_sems.at[j],
                                        device_id=(x, y, 1 - c), device_id_type=MESH)


def _forward_start(name, zone):
    def body(zone_ref, send_sems, recv_sems, zone_thru, token):
        place = _my_place()
        for j in range(3):
            _forward_copy(zone_ref, send_sems, recv_sems, j, place).start()
        token[...] = jnp.zeros_like(token)

    out = pl.pallas_call(
        body, name=name,
        out_shape=(pltpu.SemaphoreType.DMA((3,)), pltpu.SemaphoreType.DMA((3,)), pltpu.HBM(zone.shape, zone.dtype),
                   jax.ShapeDtypeStruct((8, 128), F32)),
        in_specs=(HBM_SPEC,), out_specs=(SEM_SPEC, SEM_SPEC, HBM_SPEC, pl.BlockSpec(memory_space=pltpu.VMEM)),
        input_output_aliases={0: 2},
        compiler_params=pltpu.CompilerParams(has_side_effects=pltpu.SideEffectType.DATAFLOW_SIDE_EFFECTING),
    )(pltpu.with_memory_space_constraint(zone, pltpu.HBM))
    return out[0], out[1], out[2], out[3]


def _forward_wait(name, send_sems, recv_sems, zone, after):
    def body(zone_ref, ssem, rsem, after_ref, zone_out):
        place = _my_place()
        for j in range(3):
            _forward_copy(zone_ref, ssem, rsem, j, place).wait_send()
            _forward_copy(zone_ref, ssem, rsem, j, place, arriving=True).wait_recv()

    return pl.pallas_call(
        body, name=name, out_shape=pltpu.HBM(zone.shape, zone.dtype),
        in_specs=(HBM_SPEC, SEM_SPEC, SEM_SPEC, pl.BlockSpec(memory_space=pl.ANY)), out_specs=HBM_SPEC,
        input_output_aliases={0: 0},
        compiler_params=pltpu.CompilerParams(has_side_effects=pltpu.SideEffectType.DATAFLOW_SIDE_EFFECTING),
    )(zone, send_sems, recv_sems, after)


def _sibling_exchange(grads):
    nw = len(grads)
    HBM = pl.BlockSpec(memory_space=pl.ANY)

    def body(*refs):
        g, land = refs[:nw], refs[nw:2 * nw]
        send_sems, recv_sems = refs[2 * nw:]
        x, y, c = _my_place()

        def copy(w, q, owner_core):
            r = land[w].shape[1]
            return pltpu.make_async_remote_copy(
                src_ref=g[w].at[pl.ds(pl.multiple_of((2 * q + owner_core) * r, 16), r), :], dst_ref=land[w].at[q],
                send_sem=send_sems.at[w, q], recv_sem=recv_sems.at[w, q], device_id=(x, y, 1 - c), device_id_type=MESH)

        sends = [copy(w, q, 1 - c) for w in range(nw) for q in range(4)]
        for cp in sends:
            cp.start()
        for w in range(nw):
            for q in range(4):
                copy(w, q, c).wait_recv()
        for cp in sends:
            cp.wait_send()

    return pl.pallas_call(
        body, name="sibling_exchange",
        out_shape=[jax.ShapeDtypeStruct((4, a.shape[0] // N_DEV, a.shape[1]), a.dtype) for a in grads],
        in_specs=[HBM] * nw, out_specs=[HBM] * nw,
        scratch_shapes=[pltpu.SemaphoreType.DMA((nw, 4)), pltpu.SemaphoreType.DMA((nw, 4))],
    )(*grads)


def _pair_sums(gs, sibs, core):
    n = len(gs)

    def body(core_ref, *refs):
        for w in range(n):
            refs[2 * n + w][0] = (refs[w][0, 0].astype(F32) + refs[n + w][0].astype(F32)).astype(BF16)

    in_specs = [pl.BlockSpec((1, 1) + t.shape[1:], lambda q, core_ref: (q, core_ref[0], 0, 0)) for t in sibs]
    in_specs += [pl.BlockSpec((1,) + t.shape[1:], lambda q, core_ref: (q, 0, 0)) for t in sibs]
    return pl.pallas_call(
        body, name="pair_sums",
        grid_spec=pltpu.PrefetchScalarGridSpec(
            num_scalar_prefetch=1, grid=(4,), in_specs=in_specs,
            out_specs=[pl.BlockSpec((1,) + t.shape[1:], lambda q, core_ref: (q, 0, 0)) for t in sibs]),
        out_shape=[jax.ShapeDtypeStruct(t.shape, BF16) for t in sibs],
        compiler_params=_cparams("parallel"),
    )(core, *[g.reshape(4, 2, t.shape[1], t.shape[2]) for g, t in zip(gs, sibs)], *sibs)


def _allgather_small(v, name):
    r, ccols = v.shape

    def body(v_ref, out_ref, send_sems, recv_sems):
        x, y, c = _my_place()
        my_idx = _dev_index(x, y, c)
        out_ref[my_idx] = v_ref[...]

        def copy(m):
            peer = _peer(x, y, c, m)
            return pltpu.make_async_remote_copy(
                src_ref=v_ref, dst_ref=out_ref.at[my_idx],
                send_sem=send_sems.at[m - 1], recv_sem=recv_sems.at[m - 1], device_id=peer, device_id_type=MESH)

        def arrival(m):
            peer = _peer(x, y, c, m)
            return pltpu.make_async_remote_copy(
                src_ref=v_ref, dst_ref=out_ref.at[_dev_index(*peer)],
                send_sem=send_sems.at[m - 1], recv_sem=recv_sems.at[m - 1], device_id=peer, device_id_type=MESH)

        sends = [copy(m) for m in range(1, N_DEV)]
        for cp in sends:
            cp.start()
        for m in range(1, N_DEV):
            arrival(m).wait_recv()
        for cp in sends:
            cp.wait_send()

    return pl.pallas_call(
        body, name=name,
        out_shape=jax.ShapeDtypeStruct((N_DEV, r, ccols), v.dtype),
        in_specs=[pl.BlockSpec(memory_space=pltpu.VMEM)], out_specs=pl.BlockSpec(memory_space=pltpu.VMEM),
        scratch_shapes=[pltpu.SemaphoreType.DMA((7,)), pltpu.SemaphoreType.DMA((7,))],
    )(v)


def _conditioning(pay, w_ada, b_cols):
    ncol = w_ada.shape[1]

    def body(pay_ref, w_ref, b_ref, got_ref, act_ref, mod_ref, send_sems, recv_sems):
        x, y, c = _my_place()
        my_idx = _dev_index(x, y, c)

        def copy(rnd, buf, m, arriving=False):
            peer = _peer(x, y, c, m)
            slot = _dev_index(*peer) if arriving else my_idx
            return pltpu.make_async_remote_copy(
                src_ref=buf.at[my_idx], dst_ref=buf.at[slot], send_sem=send_sems.at[rnd, m - 1],
                recv_sem=recv_sems.at[rnd, m - 1], device_id=peer, device_id_type=MESH)

        def exchange(rnd, buf):
            sends = [copy(rnd, buf, m) for m in range(1, N_DEV)]
            for cp in sends:
                cp.start()
            for m in range(1, N_DEV):
                copy(rnd, buf, m, arriving=True).wait_recv()
            for cp in sends:
                cp.wait_send()

        got_ref[my_idx] = pay_ref[...]
        exchange(0, got_ref)
        cv = jnp.concatenate([got_ref[s, 0:1, :] for s in range(N_DEV)], axis=0)
        act = cv * _sigmoid(cv)
        act_ref[...] = act
        mod_ref[my_idx] = jnp.dot(act, w_ref[...], preferred_element_type=F32, precision=lax.Precision.HIGHEST) + b_ref[...]
        exchange(1, mod_ref)

    vmem = pl.BlockSpec(memory_space=pltpu.VMEM)
    return pl.pallas_call(
        body, name="conditioning",
        out_shape=[jax.ShapeDtypeStruct((N_DEV, 8, D), F32), jax.ShapeDtypeStruct((N_DEV, D), F32),
                   jax.ShapeDtypeStruct((N_DEV, N_DEV, ncol), F32)],
        in_specs=[vmem] * 3, out_specs=[vmem] * 3,
        scratch_shapes=[pltpu.SemaphoreType.DMA((2, 7)), pltpu.SemaphoreType.DMA((2, 7))],
        compiler_params=_cparams(),
    )(pay, w_ada, b_cols)


def _ada_bwd(act_t, gm_cols):
    def body(a_ref, g_ref, o_ref):
        o_ref[...] = jnp.dot(a_ref[...], g_ref[...], preferred_element_type=F32, precision=lax.Precision.HIGHEST)

    return pl.pallas_call(
        body, name="ada_bwd", out_shape=jax.ShapeDtypeStruct((D, gm_cols.shape[1]), F32), compiler_params=_cparams(),
    )(act_t, gm_cols)


def _row_tile(r):
    for t in (256, 304, 128, 64, 16):
        if r % t == 0:
            return t
    return r


def _sum_parts(parts, name, own=None):
    k, r, ccols = parts.shape
    tr = _row_tile(r)

    def body(*refs):
        p_ref, o_ref = refs[0], refs[-1]
        acc = p_ref[0].astype(F32) if own is None else refs[1][...].astype(F32) + p_ref[0].astype(F32)
        for s in range(1, k):
            acc = acc + p_ref[s].astype(F32)
        o_ref[...] = acc

    blk = pl.BlockSpec((tr, ccols), lambda i: (i, 0))
    return pl.pallas_call(
        body, name=name, grid=(r // tr,),
        in_specs=[pl.BlockSpec((k, tr, ccols), lambda i: (0, i, 0))] + ([] if own is None else [blk]),
        out_specs=blk,
        out_shape=jax.ShapeDtypeStruct((r, ccols), F32),
        compiler_params=_cparams("parallel"),
    )(*((parts,) if own is None else (parts, own)))


def _adamw(w, g, m, v, name):
    r, ccols = w.shape
    tr = _row_tile(r)
    c1 = 1.0 / (1.0 - B1 ** STEP)
    c2 = 1.0 / (1.0 - B2 ** STEP)

    def body(w_ref, g_ref, m_ref, v_ref, d_ref, nm_ref, nv_ref):
        gv = g_ref[...]
        nm = B1 * m_ref[...] + (1.0 - B1) * gv
        nv = B2 * v_ref[...] + (1.0 - B2) * jnp.square(gv)
        nm_ref[...] = nm
        nv_ref[...] = nv
        d_ref[...] = -LR * ((nm * c1) / (jnp.sqrt(nv * c2) + ADAM_EPS) + WD * w_ref[...])

    blk = pl.BlockSpec((tr, ccols), lambda i: (i, 0))
    return pl.pallas_call(
        body, name=name, grid=(r // tr,), in_specs=[blk] * 4, out_specs=[blk] * 3,
        out_shape=[jax.ShapeDtypeStruct((r, ccols), F32)] * 3,
        compiler_params=_cparams("parallel"),
    )(w, g, m, v)


def _sum_adamw(parts, own, w, m, v, name):
    k, r, ccols = parts.shape
    tr = _row_tile(r)
    c1 = 1.0 / (1.0 - B1 ** STEP)
    c2 = 1.0 / (1.0 - B2 ** STEP)

    def body(p_ref, own_ref, w_ref, m_ref, v_ref, g_ref, d_ref, nm_ref, nv_ref):
        gv = own_ref[...].astype(F32)
        for s in range(k):
            gv = gv + p_ref[s].astype(F32)
        g_ref[...] = gv
        nm = B1 * m_ref[...] + (1.0 - B1) * gv
        nv = B2 * v_ref[...] + (1.0 - B2) * jnp.square(gv)
        nm_ref[...] = nm
        nv_ref[...] = nv
        d_ref[...] = -LR * ((nm * c1) / (jnp.sqrt(nv * c2) + ADAM_EPS) + WD * w_ref[...])

    blk = pl.BlockSpec((tr, ccols), lambda i: (i, 0))
    return pl.pallas_call(
        body, name=name, grid=(r // tr,),
        in_specs=[pl.BlockSpec((k, tr, ccols), lambda i: (0, i, 0))] + [blk] * 4, out_specs=[blk] * 4,
        out_shape=[jax.ShapeDtypeStruct((r, ccols), F32)] * 4,
        compiler_params=_cparams("parallel"),
    )(parts, own, w, m, v)


VEC_ROWS = ((0, 6), (6, 7), (9, 11), (11, 14), (7, 8), (8, 9))


def _adamw_vectors(w, g, m, v):
    c1 = 1.0 / (1.0 - B1 ** STEP)
    c2 = 1.0 / (1.0 - B2 ** STEP)

    def put(refs, p):
        for ref, (lo, hi) in zip(refs, VEC_ROWS):
            if ref.shape == (3, HEAD):
                ref[...] = p[lo:hi, :HEAD]
            else:
                ref[...] = jnp.concatenate([p[k:k + 1] for k in range(lo, hi)], axis=1)

    def body(w_ref, g_ref, m_ref, v_ref, *outs):
        gv = g_ref[...]
        nm = B1 * m_ref[...] + (1.0 - B1) * gv
        nv = B2 * v_ref[...] + (1.0 - B2) * jnp.square(gv)
        delta = -LR * ((nm * c1) / (jnp.sqrt(nv * c2) + ADAM_EPS) + WD * w_ref[...])
        for kind, p in enumerate((gv, delta, nm, nv)):
            put(outs[6 * kind:6 * kind + 6], p)

    shapes = [(1, 6 * D), (1, D), (1, 2 * D), (3, HEAD), (1, D), (1, D)]
    out = pl.pallas_call(
        body, name="adamw_vectors", out_shape=[jax.ShapeDtypeStruct(sh, F32) for sh in shapes] * 4, compiler_params=_cparams(),
    )(w, g, m, v)
    fix = lambda t: (t[0], t[1], t[2], t[3][None], t[4], t[5].reshape(D))
    return [fix(out[6 * kind:6 * kind + 6]) for kind in range(4)]


def _pack_vectors(b_ada, g_mix, g_mlp, g_fin, b_gate, conv_w):
    conv_rows = jnp.pad(conv_w.reshape(3, HEAD), ((0, 0), (0, D - HEAD)))
    return jnp.concatenate([b_ada.reshape(6, D), g_mix.reshape(1, D), g_mlp.reshape(1, D), g_fin.reshape(1, D),
                            b_gate.reshape(2, D), conv_rows, jnp.zeros((2, D), F32)], axis=0)


def kernel(x, c, w_ada, b_ada, g_norm_mix, w_in, b_gate, conv_w, w_branch_attn, w_branch_conv, w_out, g_norm_mlp, w_mlp_in, w_mlp_out, g_norm_final, loss_target, m_w_ada, m_b_ada, m_g_norm_mix, m_w_in, m_b_gate, m_conv_w, m_w_branch_attn, m_w_branch_conv, m_w_out, m_g_norm_mlp, m_w_mlp_in, m_w_mlp_out, m_g_norm_final, v_w_ada, v_b_ada, v_g_norm_mix, v_w_in, v_b_gate, v_conv_w, v_w_branch_attn, v_w_branch_conv, v_w_out, v_g_norm_mlp, v_w_mlp_in, v_w_mlp_out, v_g_norm_final):
    S = x.shape[1]
    xi, yi, ci = _my_place()
    me = _dev_index(xi, yi, ci)
    x2 = x.reshape(S, D)
    tgt = loss_target.reshape(S, D)

    pay = jnp.zeros((8, D), F32).at[0].set(c[0]).at[1:4, :HEAD].set(conv_w[0])
    ncol = w_ada.shape[2]
    b_cols = lax.dynamic_slice(b_ada, (0, me * ncol), (1, ncol))
    got, act, mod_all = _conditioning(pay, w_ada[0], b_cols)
    cw8 = jnp.pad(got[:, 1:4, :HEAD].transpose(1, 0, 2).reshape(3, D), ((0, 5), (0, 0)))

    w_in_shard, mod_all = lax.optimization_barrier((w_in[0].T.astype(BF16), mod_all))
    mod = lax.dynamic_index_in_dim(mod_all, me, axis=1, keepdims=False).reshape(6, D)
    near = _split_start("gather_w_in_start", "gather_near", [w_in_shard], [lax.empty((N_DEV * w_in_shard.shape[0], D), BF16)])
    h = _prenorm(x2, g_norm_mix, mod[1:2] + near[4], mod[0:1])
    zone = _split_wait("gather_w_in_wait", "gather_near", near[0], near[1], near[2], near[3], h)[1][0]
    fwd = _forward_start("forward_w_in_start", zone)
    w_int = _forward_wait("forward_w_in_wait", fwd[0], fwd[1], fwd[2], fwd[3])
    late = [w_branch_attn[0].T.astype(BF16), w_branch_conv[0].astype(BF16), w_out[0].astype(BF16),
            w_mlp_in[0].T.astype(BF16), w_mlp_out[0].astype(BF16)]
    w_int, late = lax.optimization_barrier((w_int, late))
    zones = [lax.empty((N_DEV * t.shape[0], t.shape[1]), BF16) for t in late]
    ag_mix = _split_start("gather_mix_start", "gather", late[:3], zones[:3])
    ag_mlp = _split_start("gather_mlp_start", "gather", late[3:], zones[3:])

    def mix_weights(o_attn):
        return _split_wait("gather_mix_wait", "gather", *ag_mix[:4], o_attn)[1]

    def mlp_weights(x1):
        return _split_wait("gather_mlp_wait", "gather", *ag_mlp[:4], x1)[1]

    rs = {}

    def mlp_grads_ready(*grads):
        lands = [lax.empty((N_PEER, t.shape[0] // N_DEV, t.shape[1]), BF16) for t in grads]
        rs["mlp"] = _split_start("scatter_mlp_start", "scatter", grads, lands)
        return rs["mlp"][4]

    def other_grads_ready(*grads):
        core = ci.reshape(1).astype(jnp.int32)
        pair = _pair_sums(grads, _sibling_exchange(grads), core)
        lands = [lax.empty((3,) + t.shape[1:], BF16) for t in pair]
        rs["rest"] = _split_start("scatter_rest_start", "chips", pair, lands)
        return rs["rest"][4]

    ba, bb = b_gate[:, :D], b_gate[:, D:]
    grad_x, vec = _local_step(
        x2, h, tgt, mod + ag_mix[4] + ag_mlp[4], g_norm_mix, g_norm_mlp, g_norm_final.reshape(1, D), ba, bb, cw8, w_int, mix_weights, mlp_weights,
        mlp_grads_ready, other_grads_ready)

    vec_all = _allgather_small(vec, "gather_vec")
    vec_sum = _sum_parts(vec_all, "sum_vec")
    loss = vec_sum[14, 0]
    gm_all = vec_all[:, 0:6, :].reshape(N_DEV, 6 * D)
    gm_cols = lax.dynamic_slice(gm_all, (0, me * ncol), (N_DEV, ncol))
    g_w_ada = _ada_bwd(act.T, gm_cols)
    conv_cols = lax.dynamic_slice(vec_sum[11:14], (0, me * HEAD), (3, HEAD))
    g_pack = jnp.concatenate([vec_sum[0:11], jnp.pad(conv_cols, ((0, 0), (0, D - HEAD))), jnp.zeros((2, D), F32)], axis=0)
    packs = [_pack_vectors(*t) for t in ((b_ada, g_norm_mix, g_norm_mlp, g_norm_final, b_gate, conv_w),
                                         (m_b_ada, m_g_norm_mix, m_g_norm_mlp, m_g_norm_final, m_b_gate, m_conv_w),
                                         (v_b_ada, v_g_norm_mix, v_g_norm_mlp, v_g_norm_final, v_b_gate, v_conv_w))]
    gv, dv, mv, vv = _adamw_vectors(packs[0], g_pack, packs[1], packs[2])
    d_ada, nm_ada, nv_ada = _adamw(w_ada[0], g_w_ada, m_w_ada[0], v_w_ada[0], "adamw_w_ada")

    big = {}
    srcs, lands = _split_wait("scatter_mlp_wait", "scatter", *rs["mlp"][:4], grad_x)
    own = [lax.dynamic_slice(g, (me * land.shape[1], 0), land.shape[1:]) for g, land in zip(srcs, lands)]
    g_mi = _sum_parts(lands[0], "sum_w_mi", own=own[0]).T
    big["w_mi"] = (g_mi[None],) + tuple(t[None] for t in _adamw(w_mlp_in[0], g_mi, m_w_mlp_in[0], v_w_mlp_in[0], "adamw_w_mi"))
    big["w_mo"] = tuple(t[None] for t in _sum_adamw(lands[1], own[1], w_mlp_out[0], m_w_mlp_out[0], v_w_mlp_out[0], "adamw_w_mo"))
    srcs, lands = _split_wait("scatter_rest_wait", "chips", *rs["rest"][:4], grad_x)
    own = [lax.dynamic_index_in_dim(pair, 2 * xi + yi, axis=0, keepdims=False) for pair in srcs]
    big["w_in"] = tuple(t.T[None] for t in _sum_adamw(lands[0], own[0], w_in[0].T, m_w_in[0].T, v_w_in[0].T, "adamw_w_in"))
    g_ba = _sum_parts(lands[1], "sum_w_ba", own=own[1]).T
    big["w_ba"] = (g_ba[None],) + tuple(t[None] for t in _adamw(w_branch_attn[0], g_ba, m_w_branch_attn[0], v_w_branch_attn[0], "adamw_w_ba"))
    big["w_bc"] = tuple(t[None] for t in _sum_adamw(lands[2], own[2], w_branch_conv[0], m_w_branch_conv[0], v_w_branch_conv[0], "adamw_w_bc"))
    big["w_out"] = tuple(t[None] for t in _sum_adamw(lands[3], own[3], w_out[0], m_w_out[0], v_w_out[0], "adamw_w_out"))

    def ordered(k, ada, vecs):
        return (ada[None], vecs[0], vecs[1], big["w_in"][k], vecs[2], vecs[3], big["w_ba"][k], big["w_bc"][k],
                big["w_out"][k], vecs[4], big["w_mi"][k], big["w_mo"][k], vecs[5])

    return (loss, grad_x.reshape(1, S, D), *ordered(0, g_w_ada, gv), *ordered(1, d_ada, dv),
            *ordered(2, nm_ada, mv), *ordered(3, nv_ada, vv))
```

```python
import functools

import numpy as np
import jax
import jax.numpy as jnp
from jax import lax
from jax.experimental import pallas as pl
from jax.experimental.pallas import tpu as pltpu

F32, BF16 = jnp.float32, jnp.bfloat16
D = 1024
HEAD = 128
DILATIONS = (1, 4, 16)
N_SLOT = 4
AOW = N_SLOT * HEAD
DFF = 4 * D
N_DEV = 8
UNROLL = 8
EPS = 1e-6
NEG = -1e30
SCALE = HEAD ** -0.5
LR, B1, B2, ADAM_EPS, WD, STEP = 0.001, 0.9, 0.999, 1e-08, 0.01, 10
V7X_VMEM_LIMIT = 56 * 1024 * 1024
TM = 1024
MESH = pl.DeviceIdType.MESH
AXES = ("x", "y", "c")


def _cparams(*sem):
    if sem:
        return pltpu.CompilerParams(dimension_semantics=sem, vmem_limit_bytes=V7X_VMEM_LIMIT)
    return pltpu.CompilerParams(vmem_limit_bytes=V7X_VMEM_LIMIT)


def _nn(a, b):
    return jnp.dot(a, b, preferred_element_type=F32)


def _nt(a, b):
    return lax.dot_general(a, b, (((1,), (1,)), ((), ())), preferred_element_type=F32)


def _tn(a, b):
    return lax.dot_general(a, b, (((0,), (0,)), ((), ())), preferred_element_type=F32)


def _rms_r(x):
    return lax.rsqrt(jnp.mean(x * x, axis=-1, keepdims=True) + EPS)


def _rms_bwd(x, r, g, dn):
    gy = dn * g
    dx = r * gy - x * (r * r * r) * jnp.mean(x * gy, axis=-1, keepdims=True)
    return dx, dn * (x * r)


def _sigmoid(t):
    return 1.0 / (1.0 + jnp.exp(-t))


def _rowsum(v):
    return jnp.sum(v, axis=0, keepdims=True)


def _vec_spec(n=D):
    return pl.BlockSpec((1, n), lambda *_: (0, 0))


def _const_spec(shape):
    nd = len(shape)
    return pl.BlockSpec(shape, lambda *_: (0,) * nd)


def _win_rowblock(j):
    return jnp.where(j < 9, (j % 3) * 3 + j // 3, j)


def _prenorm(x, g, sc, sh):
    S = x.shape[0]
    tm = TM

    def body(x_ref, g_ref, sc_ref, sh_ref, h_ref):
        xv = x_ref[...]
        h_ref[...] = (xv * _rms_r(xv) * g_ref[...] * (1.0 + sc_ref[...]) + sh_ref[...]).astype(BF16)

    row = pl.BlockSpec((tm, D), lambda i: (i, 0))
    return pl.pallas_call(
        body, name="prenorm", grid=(S // tm,), in_specs=[row, _vec_spec(), _vec_spec(), _vec_spec()], out_specs=row,
        out_shape=jax.ShapeDtypeStruct((S, D), BF16), compiler_params=_cparams("parallel"),
    )(x, g, sc, sh)


def _proj(h, w_int):
    S = h.shape[0]

    def body(h_ref, w_ref, q_ref, e_ref):
        j = pl.program_id(0)
        acc = _nt(h_ref[...], w_ref[...])

        @pl.when(j < 9)
        def _():
            q_ref[0] = acc

        @pl.when(j >= 9)
        def _():
            e_ref[0] = acc.astype(BF16)

    def e_idx(j):
        k = jnp.maximum(j - 9, 0)
        return (k // 2, 0, k % 2)

    return pl.pallas_call(
        body, name="proj", grid=(19,),
        in_specs=[pl.BlockSpec((S, D), lambda j: (0, 0), pipeline_mode=pl.Buffered(1)),
                  pl.BlockSpec((512, D), lambda j: (_win_rowblock(j), 0))],
        out_specs=[pl.BlockSpec((1, S, 512), lambda j: (jnp.minimum(j, 8), 0, 0)), pl.BlockSpec((1, S, 512), e_idx)],
        out_shape=[jax.ShapeDtypeStruct((9, S, 512), F32), jax.ShapeDtypeStruct((5, S, D), BF16)],
        compiler_params=_cparams("arbitrary"),
    )(h, w_int)


def _bias_table():
    slopes = (2.0 ** (-8.0 * np.arange(1, 13, dtype=np.float32) / 12.0)).astype(np.float32)
    qi = np.arange(HEAD)[:, None]
    kj = np.arange(2 * HEAD)[None, :]
    delta = HEAD + qi - kj
    mask = (delta >= 0) & (delta <= HEAD)
    out = np.zeros((3, N_SLOT, HEAD, 2 * HEAD), np.float32)
    for gi, d in enumerate(DILATIONS):
        for j in range(N_SLOT):
            bias = -slopes[gi * N_SLOT + j] * (delta * d).astype(np.float32)
            out[gi, j] = np.where(mask, bias, NEG)
    out_t = np.concatenate([out[..., HEAD:].swapaxes(-1, -2), out[..., :HEAD].swapaxes(-1, -2)], axis=-1)
    return jnp.asarray(out), jnp.asarray(out_t)


def _block_rows(b, d):
    r = b % d
    n = b // d
    st = n * (HEAD * d) + r
    stp = jnp.maximum(n - 1, 0) * (HEAD * d) + r
    return n, st, stp


def _attn_fwd(qkv, bias):
    S = qkv.shape[2]
    nblk = S // HEAD
    rows = 256

    def body(qkv_ref, b_ref, o_ref, lse_ref, o_s, lse_s):
        g = pl.program_id(1)
        bias = b_ref[0, 0]
        col = lax.broadcasted_iota(jnp.int32, bias.shape, 1)
        bias_first = jnp.where(col < HEAD, NEG, bias)

        for gi, d in enumerate(DILATIONS):
            @pl.when(g == gi)
            def _(gi=gi, d=d):
                def step(b, carry):
                    n, st, stp = _block_rows(b, d)
                    cur = pl.ds(st, HEAD, stride=d)
                    prv = pl.ds(stp, HEAD, stride=d)
                    q = qkv_ref.at[0, 0][cur, :].astype(BF16)
                    kw = jnp.concatenate([qkv_ref.at[0, 1][prv, :], qkv_ref.at[0, 1][cur, :]], axis=0).astype(BF16)
                    vw = jnp.concatenate([qkv_ref.at[0, 2][prv, :], qkv_ref.at[0, 2][cur, :]], axis=0).astype(BF16)
                    s = _nt(q, kw) * SCALE + jnp.where(n > 0, bias, bias_first)
                    m = jnp.max(s, axis=-1, keepdims=True)
                    p = jnp.exp(s - m)
                    l = jnp.sum(p, axis=-1, keepdims=True)
                    o_s.at[gi][cur, :] = _nn(p.astype(BF16), vw) / l
                    lse_s.at[gi][cur, :] = jnp.broadcast_to(m + jnp.log(l), (HEAD, HEAD))
                    return carry

                lax.fori_loop(0, nblk, step, 0, unroll=UNROLL)

        @pl.when(g == len(DILATIONS) - 1)
        def _():
            def merge(i, carry):
                r = pl.ds(pl.multiple_of(i * rows, rows), rows)
                ls = [lse_s[k, r, :] for k in range(3)]
                top = jnp.maximum(jnp.maximum(ls[0], ls[1]), ls[2])
                ws = [jnp.exp(t - top) for t in ls]
                den = ws[0] + ws[1] + ws[2]
                o_ref[r, :] = (ws[0] * o_s[0, r, :] + ws[1] * o_s[1, r, :] + ws[2] * o_s[2, r, :]) / den
                lse_ref[r, :] = top + jnp.log(den)
                return carry

            lax.fori_loop(0, S // rows, merge, 0)

    return pl.pallas_call(
        body, name="attn_fwd", grid=(N_SLOT, 3),
        in_specs=[pl.BlockSpec((1, 3, S, HEAD), lambda j, g: (g, 0, 0, j)),
                  pl.BlockSpec((1, 1, HEAD, 2 * HEAD), lambda j, g: (g, j, 0, 0))],
        out_specs=[pl.BlockSpec((S, HEAD), lambda j, g: (0, j)), pl.BlockSpec((S, HEAD), lambda j, g: (0, j))],
        out_shape=[jax.ShapeDtypeStruct((S, AOW), F32), jax.ShapeDtypeStruct((S, AOW), F32)],
        scratch_shapes=[pltpu.VMEM((3, S, HEAD), F32)] * 2,
        compiler_params=_cparams("parallel", "arbitrary"),
    )(qkv, bias)


def _shift_down(z, k, halo_rows):
    out = pltpu.roll(z, k, axis=0)
    top = out[:8]
    rid = lax.broadcasted_iota(jnp.int32, top.shape, 0)
    for t in range(k):
        top = jnp.where(rid == t, halo_rows[t], top)
    return jnp.concatenate([top, out[8:]], axis=0)


def _shift_up(z, k, halo_rows):
    n = z.shape[0]
    out = pltpu.roll(z, n - k, axis=0)
    bottom = out[n - 8:]
    rid = lax.broadcasted_iota(jnp.int32, bottom.shape, 0)
    for t in range(k):
        bottom = jnp.where(rid == 8 - k + t, halo_rows[t], bottom)
    return jnp.concatenate([out[:n - 8], bottom], axis=0)


def _e_spec(chunk, tm):
    return pl.BlockSpec((1, tm, D), lambda i, c=chunk: (c, i, 0))


def _e_prev_spec(chunk, tm):
    return pl.BlockSpec((1, 16, D), lambda i, c=chunk: (c, jnp.maximum(i * (tm // 16) - 1, 0), 0))


def _e_next_spec(chunk, tm, S):
    return pl.BlockSpec((1, 16, D), lambda i, c=chunk: (c, jnp.minimum((i + 1) * (tm // 16), S // 16 - 1), 0))


def _mix(o_attn, e, cw8, ba, bb, w_bat, w_bc):
    S = o_attn.shape[0]
    tm = 256

    def body(o_ref, cb_ref, cc_ref, cx_ref, ga_ref, gb_ref, ccp_ref, cxp_ref, cw_ref, ba_ref, bb_ref, wba_ref, wbc_ref,
             obf_ref, cbu_ref, ya_ref, yc_ref, mg_ref):
        i = pl.program_id(0)
        o = o_ref[...].astype(BF16)
        obf_ref[...] = o
        ya = _nt(o, wba_ref[...])
        z = cc_ref[0].astype(F32) * cx_ref[0].astype(F32)
        zp = ccp_ref[0].astype(F32) * cxp_ref[0].astype(F32) * (i > 0).astype(F32)
        z1 = _shift_down(z, 1, [zp[15:16]])
        z2 = _shift_down(z, 2, [zp[14:15], zp[15:16]])
        cw = cw_ref[...]
        u = cw[0:1] * z2 + cw[1:2] * z1 + cw[2:3] * z
        cbu = (cb_ref[0].astype(F32) * u).astype(BF16)
        cbu_ref[...] = cbu
        yc = _nn(cbu, wbc_ref[...])
        sa = _sigmoid(ga_ref[0].astype(F32) + ba_ref[...])
        sb = _sigmoid(gb_ref[0].astype(F32) + bb_ref[...])
        ya_ref[...] = ya.astype(BF16)
        yc_ref[...] = yc.astype(BF16)
        mg_ref[...] = (sa * ya + sb * yc).astype(BF16)

    row = lambda w: pl.BlockSpec((tm, w), lambda i: (i, 0))
    return pl.pallas_call(
        body, name="mix", grid=(S // tm,),
        in_specs=[row(AOW)] + [_e_spec(c, tm) for c in range(5)] + [_e_prev_spec(1, tm), _e_prev_spec(2, tm),
                  _const_spec((8, D)), _vec_spec(), _vec_spec(), _const_spec((D, AOW)), _const_spec((D, D))],
        out_specs=[row(AOW), row(D), row(D), row(D), row(D)],
        out_shape=[jax.ShapeDtypeStruct((S, AOW), BF16)] + [jax.ShapeDtypeStruct((S, D), BF16)] * 4,
        compiler_params=_cparams("parallel"),
    )(o_attn, e, e, e, e, e, e, e, cw8, ba, bb, w_bat, w_bc)


def _out_proj(merged, w_out, x, gate1, g_mlp, sc2, sh2):
    S = x.shape[0]
    tm = TM

    def body(mg_ref, w_ref, x_ref, gt_ref, g_ref, sc_ref, sh_ref, x1_ref, mo_ref, h2_ref):
        mo = _nn(mg_ref[...], w_ref[...])
        mo_ref[...] = mo.astype(BF16)
        x1 = x_ref[...] + gt_ref[...] * mo
        x1_ref[...] = x1
        h2 = x1 * _rms_r(x1) * g_ref[...] * (1.0 + sc_ref[...]) + sh_ref[...]
        h2_ref[...] = h2.astype(BF16)

    row = pl.BlockSpec((tm, D), lambda i: (i, 0))
    return pl.pallas_call(
        body, name="out_proj", grid=(S // tm,),
        in_specs=[row, _const_spec((D, D)), row, _vec_spec(), _vec_spec(), _vec_spec(), _vec_spec()],
        out_specs=[row, row, row],
        out_shape=[jax.ShapeDtypeStruct((S, D), F32), jax.ShapeDtypeStruct((S, D), BF16), jax.ShapeDtypeStruct((S, D), BF16)],
        compiler_params=_cparams("parallel"),
    )(merged, w_out, x, gate1, g_mlp, sc2, sh2)


def _mlp_in(h2, w_mit):
    S = h2.shape[0]
    tm, tn = TM, 2048

    def body(h_ref, w_ref, a_ref, f_ref):
        a = _nt(h_ref[...], w_ref[...])
        a_ref[...] = a.astype(BF16)
        f_ref[...] = jnp.square(jnp.maximum(a, 0.0)).astype(BF16)

    blk = pl.BlockSpec((tm, tn), lambda i, j: (i, j))
    return pl.pallas_call(
        body, name="mlp_in", grid=(S // tm, DFF // tn),
        in_specs=[pl.BlockSpec((tm, D), lambda i, j: (i, 0)), pl.BlockSpec((tn, D), lambda i, j: (j, 0))],
        out_specs=[blk, blk],
        out_shape=[jax.ShapeDtypeStruct((S, DFF), BF16)] * 2,
        compiler_params=_cparams("parallel", "parallel"),
    )(h2, w_mit)


def _mlp_out(f, w_mo, x1, gate2, g_fin, tgt):
    S = x1.shape[0]
    tm = 512
    half = tm // 2

    def body(f_ref, w_ref, x1_ref, gt_ref, g_ref, t_ref, mlp_ref, dx2_ref, pv_ref):
        @pl.when(pl.program_id(0) == 0)
        def _():
            pv_ref[...] = jnp.zeros_like(pv_ref)

        g = g_ref[...]
        for hs in (pl.ds(0, half), pl.ds(half, half)):
            mlp = _nn(f_ref[hs, :], w_ref[...])
            mlp_ref[hs, :] = mlp.astype(BF16)
            x2 = x1_ref[hs, :] + gt_ref[...] * mlp
            r = _rms_r(x2)
            err = x2 * r * g - t_ref[hs, :]
            dx2, pg = _rms_bwd(x2, r, g, err * (1.0 / D))
            dx2_ref[hs, :] = dx2
            pv_ref[0:1, :] += _rowsum(pg)
            pv_ref[1:2, :] += 0.5 * _rowsum(jnp.mean(err * err, axis=-1, keepdims=True))

    row = pl.BlockSpec((tm, D), lambda i: (i, 0))
    return pl.pallas_call(
        body, name="mlp_out", grid=(S // tm,),
        in_specs=[pl.BlockSpec((tm, DFF), lambda i: (i, 0)), _const_spec((DFF, D)), row, _vec_spec(), _vec_spec(), row],
        out_specs=[row, row, _const_spec((8, D))],
        out_shape=[jax.ShapeDtypeStruct((S, D), BF16), jax.ShapeDtypeStruct((S, D), F32), jax.ShapeDtypeStruct((8, D), F32)],
        compiler_params=_cparams("arbitrary"),
    )(f, w_mo, x1, gate2, g_fin, tgt)


def _bwd_mlp_a(dx2, gate2, mlp, w_mo, a):
    S = dx2.shape[0]
    tm = 512
    half = tm // 2

    def body(dx_ref, gt_ref, mlp_ref, w_ref, a_ref, da_ref, dmo_ref, pv_ref):
        @pl.when(pl.program_id(0) == 0)
        def _():
            pv_ref[...] = jnp.zeros_like(pv_ref)

        for hs in (pl.ds(0, half), pl.ds(half, half)):
            dx = dx_ref[hs, :]
            dmo = (dx * gt_ref[...]).astype(BF16)
            dmo_ref[hs, :] = dmo
            pv_ref[0:1, :] += _rowsum(dx * mlp_ref[hs, :].astype(F32))
            df = _nt(dmo, w_ref[...])
            da_ref[hs, :] = (df * (2.0 * jnp.maximum(a_ref[hs, :].astype(F32), 0.0))).astype(BF16)

    row = pl.BlockSpec((tm, D), lambda i: (i, 0))
    wide = pl.BlockSpec((tm, DFF), lambda i: (i, 0))
    return pl.pallas_call(
        body, name="bwd_mlp_a", grid=(S // tm,),
        in_specs=[row, _vec_spec(), row, _const_spec((DFF, D)), wide],
        out_specs=[wide, row, _const_spec((8, D))],
        out_shape=[jax.ShapeDtypeStruct((S, DFF), BF16), jax.ShapeDtypeStruct((S, D), BF16), jax.ShapeDtypeStruct((8, D), F32)],
        compiler_params=_cparams("arbitrary"),
    )(dx2, gate2, mlp, w_mo, a)


def _bwd_mlp_b(da, w_mit, x1, dx2, g_mlp, sc2):
    S = x1.shape[0]
    tm = 512
    half = tm // 2

    def body(da_ref, w_ref, x1_ref, dx2_ref, g_ref, sc_ref, dx1_ref, pv_ref):
        @pl.when(pl.program_id(0) == 0)
        def _():
            pv_ref[...] = jnp.zeros_like(pv_ref)

        g = g_ref[...]
        for hs in (pl.ds(0, half), pl.ds(half, half)):
            dh = _nn(da_ref[hs, :], w_ref[...])
            x1 = x1_ref[hs, :]
            r = _rms_r(x1)
            dxn, pg = _rms_bwd(x1, r, g, dh * (1.0 + sc_ref[...]))
            dx1_ref[hs, :] = dx2_ref[hs, :] + dxn
            pv_ref[0:1, :] += _rowsum(dh)
            pv_ref[1:2, :] += _rowsum(dh * (x1 * r * g))
            pv_ref[2:3, :] += _rowsum(pg)

    row = pl.BlockSpec((tm, D), lambda i: (i, 0))
    return pl.pallas_call(
        body, name="bwd_mlp_b", grid=(S // tm,),
        in_specs=[pl.BlockSpec((tm, DFF), lambda i: (i, 0)), _const_spec((DFF, D)), row, row, _vec_spec(), _vec_spec()],
        out_specs=[row, _const_spec((8, D))],
        out_shape=[jax.ShapeDtypeStruct((S, D), F32), jax.ShapeDtypeStruct((8, D), F32)],
        compiler_params=_cparams("arbitrary"),
    )(da, w_mit, x1, dx2, g_mlp, sc2)


def _bwd_mix(dx1, gate1, mo, e, cw8, ba, bb, ya, yc, o_attn, w_out, w_bc, w_bat):
    S = dx1.shape[0]
    tm = 256
    n_tiles = S // tm

    def body(dx_ref, dxn_ref, gt_ref, mo_ref, cb_ref, cc_ref, cx_ref, ga_ref, gb_ref, cbn_ref, gbn_ref, ccp_ref, cxp_ref,
             cw_ref, ba_ref, bb_ref, ya_ref, yc_ref, o_ref, wout_ref, wbc_ref, wba_ref,
             dmo_ref, dya_ref, dyc_ref, do_ref, dl_ref, de_ref, pv_ref):
        i = pl.program_id(0)

        @pl.when(i == 0)
        def _():
            pv_ref[...] = jnp.zeros_like(pv_ref)

        gate = gt_ref[...]
        bbv = bb_ref[...]

        def conv_branch_grad(dx_rows, gb_rows):
            dmo = (dx_rows * gate).astype(BF16)
            dmg = _nt(dmo, wout_ref[...])
            sb = _sigmoid(gb_rows + bbv)
            dyc = dmg * sb
            return dmo, dmg, sb, dyc, _nt(dyc.astype(BF16), wbc_ref[...])

        dx = dx_ref[...]
        cb = cb_ref[0].astype(F32)
        cc = cc_ref[0].astype(F32)
        cx = cx_ref[0].astype(F32)
        dmo, dmg, sb, dyc, dcbu = conv_branch_grad(dx, gb_ref[0].astype(F32))
        dmo_ref[...] = dmo
        pv_ref[0:1, :] += _rowsum(dx * mo_ref[...].astype(F32))
        sa = _sigmoid(ga_ref[0].astype(F32) + ba_ref[...])
        dya = (dmg * sa).astype(BF16)
        dya_ref[...] = dya
        dyc_ref[...] = dyc.astype(BF16)
        dga = dmg * ya_ref[...].astype(F32) * sa * (1.0 - sa)
        dgb = dmg * yc_ref[...].astype(F32) * sb * (1.0 - sb)
        pv_ref[1:2, :] += _rowsum(dga)
        pv_ref[2:3, :] += _rowsum(dgb)

        do = _nn(dya, wba_ref[...])
        do_ref[...] = do
        prod = do * o_ref[...]
        dl_ref[...] = jnp.concatenate(
            [jnp.broadcast_to(jnp.sum(prod[:, s * HEAD:(s + 1) * HEAD], axis=-1, keepdims=True), (tm, HEAD))
             for s in range(N_SLOT)], axis=1)

        z = cc * cx
        zp = ccp_ref[0].astype(F32) * cxp_ref[0].astype(F32) * (i > 0).astype(F32)
        z1 = _shift_down(z, 1, [zp[15:16]])
        z2 = _shift_down(z, 2, [zp[14:15], zp[15:16]])
        cw = cw_ref[...]
        u = cw[0:1] * z2 + cw[1:2] * z1 + cw[2:3] * z
        du = dcbu * cb
        dcbu_n = conv_branch_grad(dxn_ref[...], gbn_ref[0].astype(F32))[4]
        du_n = dcbu_n * cbn_ref[0].astype(F32) * (i < n_tiles - 1).astype(F32)
        du1 = _shift_up(du, 1, [du_n[0:1]])
        du2 = _shift_up(du, 2, [du_n[0:1], du_n[1:2]])
        dz = cw[2:3] * du + cw[1:2] * du1 + cw[0:1] * du2
        pv_ref[3:4, :] += _rowsum(du * z2)
        pv_ref[4:5, :] += _rowsum(du * z1)
        pv_ref[5:6, :] += _rowsum(du * z)

        de_ref[0] = (dcbu * u).astype(BF16)
        de_ref[1] = (dz * cx).astype(BF16)
        de_ref[2] = (dz * cc).astype(BF16)
        de_ref[3] = dga.astype(BF16)
        de_ref[4] = dgb.astype(BF16)

    row = lambda w: pl.BlockSpec((tm, w), lambda i: (i, 0))
    nxt = pl.BlockSpec((16, D), lambda i: (jnp.minimum((i + 1) * (tm // 16), S // 16 - 1), 0))
    return pl.pallas_call(
        body, name="bwd_mix", grid=(n_tiles,),
        in_specs=[row(D), nxt, _vec_spec(), row(D)] + [_e_spec(c, tm) for c in range(5)]
                 + [_e_next_spec(0, tm, S), _e_next_spec(4, tm, S), _e_prev_spec(1, tm), _e_prev_spec(2, tm),
                    _const_spec((8, D)), _vec_spec(), _vec_spec(), row(D), row(D), row(AOW),
                    _const_spec((D, D)), _const_spec((D, D)), _const_spec((D, AOW))],
        out_specs=[row(D), row(D), row(D), row(AOW), row(AOW), pl.BlockSpec((5, tm, D), lambda i: (0, i, 0)),
                   _const_spec((8, D))],
        out_shape=[jax.ShapeDtypeStruct((S, D), BF16)] * 3 + [jax.ShapeDtypeStruct((S, AOW), F32)] * 2
                  + [jax.ShapeDtypeStruct((5, S, D), BF16), jax.ShapeDtypeStruct((8, D), F32)],
        compiler_params=_cparams("arbitrary"),
    )(dx1, dx1, gate1, mo, e, e, e, e, e, e, e, e, e, cw8, ba, bb, ya, yc, o_attn, w_out, w_bc, w_bat)


def _attn_bwd(qkv, do, lse, dl, bias_t):
    S = qkv.shape[2]
    nblk = S // HEAD

    def body(qkv_ref, do_ref, lse_ref, dl_ref, b_ref, d_ref):
        g = pl.program_id(1)
        bias = b_ref[0, 0]
        col = lax.broadcasted_iota(jnp.int32, bias.shape, 1)
        bias_last = jnp.where(col >= HEAD, NEG, bias)
        eye = (lax.broadcasted_iota(jnp.int32, (HEAD, HEAD), 0) == lax.broadcasted_iota(jnp.int32, (HEAD, HEAD), 1)).astype(F32)

        def as_row(t):
            return jnp.sum(t * eye, axis=0, keepdims=True)

        for gi, d in enumerate(DILATIONS):
            @pl.when(g == gi)
            def _(d=d):
                nb = nblk // d

                def step(b, dq_part):
                    r, n = b // nb, b % nb
                    cur = pl.ds(n * (HEAD * d) + r, HEAD, stride=d)
                    nxt = pl.ds(jnp.minimum(n + 1, nb - 1) * (HEAD * d) + r, HEAD, stride=d)
                    two = lambda ref: jnp.concatenate([ref[cur, :], ref[nxt, :]], axis=0)
                    two_rows = lambda ref: jnp.concatenate([as_row(ref[cur, :]), as_row(ref[nxt, :])], axis=1)
                    q2 = two(qkv_ref.at[0, 0]).astype(BF16)
                    do2 = two(do_ref).astype(BF16)
                    k = qkv_ref.at[0, 1][cur, :].astype(BF16)
                    v = qkv_ref.at[0, 2][cur, :].astype(BF16)
                    s = _nt(k, q2) * SCALE + jnp.where(n < nb - 1, bias, bias_last)
                    p = jnp.exp(s - two_rows(lse_ref))
                    d_ref.at[0, 2][cur, :] = _nn(p.astype(BF16), do2)
                    dp = _nt(v, do2)
                    ds = (p * (dp - two_rows(dl_ref)) * SCALE).astype(BF16)
                    d_ref.at[0, 1][cur, :] = _nn(ds, q2)
                    dq2 = _tn(ds, k)
                    d_ref.at[0, 0][cur, :] = dq2[:HEAD] + jnp.where(n > 0, dq_part, 0.0)
                    return dq2[HEAD:]

                def steps(i, dq_part):
                    for u in range(UNROLL):
                        dq_part = step(i * UNROLL + u, dq_part)
                    return dq_part

                lax.fori_loop(0, nblk // UNROLL, steps, jnp.zeros((HEAD, HEAD), F32))

    col_blk = pl.BlockSpec((S, HEAD), lambda j, g: (0, j))
    qkv_blk = pl.BlockSpec((1, 3, S, HEAD), lambda j, g: (g, 0, 0, j))
    return pl.pallas_call(
        body, name="attn_bwd", grid=(N_SLOT, 3),
        in_specs=[qkv_blk, col_blk, col_blk, col_blk, pl.BlockSpec((1, 1, HEAD, 2 * HEAD), lambda j, g: (g, j, 0, 0))],
        out_specs=qkv_blk,
        out_shape=jax.ShapeDtypeStruct((3, 3, S, AOW), F32),
        compiler_params=_cparams("parallel", "arbitrary"),
    )(qkv, do, lse, dl, bias_t)


def _bwd_in(dqkv, de, w_int, x, dx1, g_mix, sc1):
    S = x.shape[0]
    tm = TM
    dqkv = dqkv.reshape(3, 3, S, AOW)

    def body(dq_ref, de_ref, wq_ref, wk_ref, wv_ref, wa_ref, wb_ref, x_ref, dx1_ref, g_ref, sc_ref, gx_ref, pv_ref):
        acc = gx_ref
        i, k = pl.program_id(0), pl.program_id(1)

        @pl.when((i == 0) & (k == 0))
        def _():
            pv_ref[...] = jnp.zeros_like(pv_ref)

        @pl.when(k == 0)
        def _():
            acc[...] = jnp.zeros_like(acc)

        @pl.when(k < 3)
        def _():
            lhs = jnp.concatenate([dq_ref[0, t].astype(BF16) for t in range(3)], axis=1)
            acc[...] += _nn(lhs, jnp.concatenate([wq_ref[...], wk_ref[...], wv_ref[...]], axis=0))

        @pl.when(k >= 3)
        def _():
            acc[...] += _nn(de_ref[0], jnp.concatenate([wa_ref[...], wb_ref[...]], axis=0))

        @pl.when(k == 7)
        def _():
            dh = acc[...]
            xv = x_ref[...]
            r = _rms_r(xv)
            g = g_ref[...]
            dxn, pg = _rms_bwd(xv, r, g, dh * (1.0 + sc_ref[...]))
            gx_ref[...] = dx1_ref[...] + dxn
            pv_ref[0:1, :] += _rowsum(dh)
            pv_ref[1:2, :] += _rowsum(dh * (xv * r * g))
            pv_ref[2:3, :] += _rowsum(pg)

    grp = lambda k: jnp.minimum(k, 2)
    chunk = lambda k: jnp.maximum(k - 3, 0)
    wblk = lambda f: pl.BlockSpec((512, D), lambda i, k: (f(k), 0))
    row = pl.BlockSpec((tm, D), lambda i, k: (i, 0))
    once = pl.BlockSpec((tm, D), lambda i, k: (i, 0), pipeline_mode=pl.Buffered(1))
    return pl.pallas_call(
        body, name="bwd_in", grid=(S // tm, 8),
        in_specs=[pl.BlockSpec((1, 3, tm, 512), lambda i, k: (grp(k), 0, i, 0)),
                  pl.BlockSpec((1, tm, D), lambda i, k: (chunk(k), i, 0)),
                  wblk(grp), wblk(lambda k: 3 + grp(k)), wblk(lambda k: 6 + grp(k)),
                  wblk(lambda k: 9 + 2 * chunk(k)), wblk(lambda k: 10 + 2 * chunk(k)),
                  once, once, _vec_spec(), _vec_spec()],
        out_specs=[row, _const_spec((8, D))],
        out_shape=[jax.ShapeDtypeStruct((S, D), F32), jax.ShapeDtypeStruct((8, D), F32)],
        compiler_params=_cparams("arbitrary", "arbitrary"),
    )(dqkv, de, w_int, w_int, w_int, w_int, w_int, x, dx1, g_mix, sc1)


def _grad_w(name, a, b):
    S, ka = a.shape
    nb = b.shape[1]

    def body(a_ref, b_ref, o_ref):
        o_ref[...] = _tn(a_ref[...], b_ref[...]).astype(BF16)

    return pl.pallas_call(
        body, name=name, grid=(ka // 512,),
        in_specs=[pl.BlockSpec((S, 512), lambda n: (0, n)), pl.BlockSpec((S, nb), lambda n: (0, 0))],
        out_specs=pl.BlockSpec((512, nb), lambda n: (n, 0)),
        out_shape=jax.ShapeDtypeStruct((ka, nb), BF16),
        compiler_params=_cparams("parallel"),
    )(a, b)


def _grad_w_small(dya, o_bf, cbu, dyc, merged, dmo):
    S = dya.shape[0]

    def body(dya_ref, o_ref, cbu_ref, dyc_ref, mg_ref, dmo_ref, gba_ref, gbc_ref, gout_ref):
        gba_ref[...] = _tn(dya_ref[...], o_ref[...]).astype(BF16)
        gbc_ref[...] = _tn(cbu_ref[...], dyc_ref[...]).astype(BF16)
        gout_ref[...] = _tn(mg_ref[...], dmo_ref[...]).astype(BF16)

    a_blk = pl.BlockSpec((S, 512), lambda n: (0, n))
    whole = lambda w: pl.BlockSpec((S, w), lambda n: (0, 0))
    out = lambda w: pl.BlockSpec((512, w), lambda n: (n, 0))
    return pl.pallas_call(
        body, name="grad_w_small", grid=(D // 512,),
        in_specs=[a_blk, whole(AOW), a_blk, whole(D), a_blk, whole(D)],
        out_specs=[out(AOW), out(D), out(D)],
        out_shape=[jax.ShapeDtypeStruct((D, AOW), BF16), jax.ShapeDtypeStruct((D, D), BF16), jax.ShapeDtypeStruct((D, D), BF16)],
        compiler_params=_cparams("parallel"),
    )(dya, o_bf, cbu, dyc, merged, dmo)


def _grad_w_in(dqkv, de, h):
    S = h.shape[0]

    def body(dq_ref, de_ref, h_ref, o_ref):
        n = pl.program_id(0)

        @pl.when(n < 9)
        def _():
            o_ref[...] = _tn(dq_ref[0].astype(BF16), h_ref[...]).astype(BF16)

        @pl.when(n >= 9)
        def _():
            o_ref[...] = _tn(de_ref[0], h_ref[...]).astype(BF16)

    def e_idx(n):
        kk = jnp.maximum(n - 9, 0)
        return (kk // 2, 0, kk % 2)

    return pl.pallas_call(
        body, name="grad_w_in", grid=(19,),
        in_specs=[pl.BlockSpec((1, S, 512), lambda n: (jnp.minimum(n, 8), 0, 0)), pl.BlockSpec((1, S, 512), e_idx),
                  pl.BlockSpec((S, D), lambda n: (0, 0))],
        out_specs=pl.BlockSpec((512, D), lambda n: (_win_rowblock(n), 0)),
        out_shape=jax.ShapeDtypeStruct((19 * 512, D), BF16),
        compiler_params=_cparams("parallel"),
    )(dqkv, de, h)


def _local_step(x, h, tgt, mod, g_mix, g_mlp, g_fin, ba, bb, cw8, w_int, mix_weights, mlp_weights, mlp_grads_ready, other_grads_ready):
    S = x.shape[0]
    sh1, sc1, gt1, sh2, sc2, gt2 = [mod[k:k + 1] for k in range(6)]
    bias, bias_t = _bias_table()

    qkv, e = _proj(h, w_int)
    qkv = qkv.reshape(3, 3, S, AOW)
    o_attn, lse = _attn_fwd(qkv, bias)
    w_bat, w_bc, w_out = mix_weights(o_attn)
    o_bf, cbu, ya, yc, merged = _mix(o_attn, e, cw8, ba, bb, w_bat, w_bc)
    x1, mo, h2 = _out_proj(merged, w_out, x, gt1, g_mlp, sc2, sh2)
    w_mit, w_mo = mlp_weights(x1)
    a, f = _mlp_in(h2, w_mit)
    mlp, dx2, pv_f = _mlp_out(f, w_mo, x1, gt2, g_fin, tgt)

    da, dmo2, pv_a = _bwd_mlp_a(dx2, gt2, mlp, w_mo, a)
    dx1, pv_b = _bwd_mlp_b(da, w_mit, x1, dx2, g_mlp, sc2)
    zero = mlp_grads_ready(_grad_w("grad_w_mi", da, h2), _grad_w("grad_w_mo", f, dmo2))
    dmo, dya, dyc, do, dl, de, pv_m = _bwd_mix(dx1, gt1 + zero, mo, e, cw8, ba, bb, ya, yc, o_attn, w_out, w_bc, w_bat)
    dqkv = _attn_bwd(qkv, do, lse, dl, bias_t).reshape(9, S, AOW)
    zero = other_grads_ready(_grad_w_in(dqkv, de, h), *_grad_w_small(dya, o_bf, cbu, dyc, merged, dmo))
    grad_x, pv_i = _bwd_in(dqkv, de, w_int, x, dx1, g_mix, sc1 + zero)

    vec = jnp.concatenate([pv_i[0:2], pv_m[0:1], pv_b[0:2], pv_a[0:1], pv_i[2:3], pv_b[2:3], pv_f[0:1],
                           pv_m[1:3], pv_m[3:6], pv_f[1:2], jnp.zeros((1, D), F32)], axis=0)
    return grad_x, vec


def _my_place():
    return lax.axis_index("x"), lax.axis_index("y"), lax.axis_index("c")


def _dev_index(px, py, pc):
    return 4 * px + 2 * py + pc


def _allgather_weights(shards):
    nw = len(shards)
    HBM = pl.BlockSpec(memory_space=pl.ANY)

    def body(*refs):
        sh, full = refs[:nw], refs[nw:2 * nw]
        send_sems, recv_sems, local_sems = refs[2 * nw:]
        x, y, c = _my_place()
        me, sibling = (x, y, c), (x, y, 1 - c)
        chips = [(1 - x, y), (x, 1 - y), (1 - x, 1 - y)]

        def rows(w, px, py, pc):
            r = sh[w].shape[0]
            return full[w].at[pl.ds(pl.multiple_of(_dev_index(px, py, pc) * r, 16), r), :]

        def copy(w, k, block, to, src=None):
            return pltpu.make_async_remote_copy(
                src_ref=rows(w, *block) if src is None else src, dst_ref=rows(w, *block),
                send_sem=send_sems.at[w, k], recv_sem=recv_sems.at[w, k], device_id=to, device_id_type=MESH)

        mine = [pltpu.make_async_copy(sh[w], rows(w, *me), local_sems.at[w]) for w in range(nw)]
        for cp in mine:
            cp.start()
        first = []
        for w in range(nw):
            first.append(copy(w, 0, me, sibling, src=sh[w]))
            first += [copy(w, 1 + j, me, (*chip, c), src=sh[w]) for j, chip in enumerate(chips)]
        for cp in first:
            cp.start()
        passed = []
        for w in range(nw):
            for j, chip in enumerate(chips):
                copy(w, 1 + j, (*chip, c), me).wait_recv()
                fwd = copy(w, 4 + j, (*chip, c), sibling)
                fwd.start()
                passed.append(fwd)
        for w in range(nw):
            copy(w, 0, sibling, me).wait_recv()
            for j, chip in enumerate(chips):
                copy(w, 4 + j, (*chip, 1 - c), me).wait_recv()
        for cp in first + passed:
            cp.wait_send()
        for cp in mine:
            cp.wait()

    return pl.pallas_call(
        body, name="allgather_weights",
        out_shape=[jax.ShapeDtypeStruct((N_DEV * s.shape[0], s.shape[1]), s.dtype) for s in shards],
        in_specs=[HBM] * nw, out_specs=[HBM] * nw,
        scratch_shapes=[pltpu.SemaphoreType.DMA((nw, 7)), pltpu.SemaphoreType.DMA((nw, 7)), pltpu.SemaphoreType.DMA((nw,))],
    )(*shards)


def _peer(x, y, c, m):
    return (x ^ ((m >> 2) & 1), y ^ ((m >> 1) & 1), c ^ (m & 1))


HBM_SPEC = pl.BlockSpec(memory_space=pltpu.HBM)
SEM_SPEC = pl.BlockSpec(memory_space=pltpu.SEMAPHORE)
N_PEER = N_DEV - 1


SPLIT_MASKS = {"gather": tuple(range(1, N_DEV)), "gather_near": (1, 2, 4, 6), "scatter": tuple(range(1, N_DEV)),
               "chips": (2, 4, 6)}


def _split_copy(mode, src_ref, land_ref, send_sems, recv_sems, w, j, place, arriving=False):
    x, y, c = place
    masks = SPLIT_MASKS[mode]
    peer = _peer(x, y, c, masks[j])
    k = w * len(masks) + j
    sender, receiver = ((peer, (x, y, c)) if arriving else ((x, y, c), peer))
    if mode.startswith("gather"):
        r = src_ref.shape[0]
        src, dst = src_ref, land_ref.at[pl.ds(pl.multiple_of(_dev_index(*sender) * r, 16), r), :]
    elif mode == "scatter":
        r = land_ref.shape[1]
        src, dst = src_ref.at[pl.ds(pl.multiple_of(_dev_index(*receiver) * r, 16), r), :], land_ref.at[j]
    else:
        src, dst = src_ref.at[2 * receiver[0] + receiver[1]], land_ref.at[j]
    return pltpu.make_async_remote_copy(src_ref=src, dst_ref=dst, send_sem=send_sems.at[k], recv_sem=recv_sems.at[k],
                                        device_id=peer, device_id_type=MESH)


def _split_start(name, mode, srcs, lands):
    n = len(srcs)
    nm = len(SPLIT_MASKS[mode])

    def body(*refs):
        src, land = refs[:n], refs[n:2 * n]
        send_sems, recv_sems = refs[2 * n], refs[2 * n + 1]
        token = refs[-1]
        place = _my_place()
        for w in range(n):
            for j in range(nm):
                _split_copy(mode, src[w], land[w], send_sems, recv_sems, w, j, place).start()
        token[...] = jnp.zeros_like(token)

    hbm = lambda t: pltpu.HBM(t.shape, t.dtype)
    out = pl.pallas_call(
        body, name=name,
        out_shape=(pltpu.SemaphoreType.DMA((n * nm,)), pltpu.SemaphoreType.DMA((n * nm,)), *[hbm(t) for t in srcs],
                   *[hbm(t) for t in lands], jax.ShapeDtypeStruct((8, 128), F32)),
        in_specs=(HBM_SPEC,) * (2 * n),
        out_specs=(SEM_SPEC, SEM_SPEC) + (HBM_SPEC,) * (2 * n) + (pl.BlockSpec(memory_space=pltpu.VMEM),),
        input_output_aliases={i: 2 + i for i in range(2 * n)},
        compiler_params=pltpu.CompilerParams(has_side_effects=pltpu.SideEffectType.DATAFLOW_SIDE_EFFECTING),
    )(*[pltpu.with_memory_space_constraint(t, pltpu.HBM) for t in (*srcs, *lands)])
    return out[0], out[1], out[2:2 + n], out[2 + n:2 + 2 * n], out[-1][0:1, 0:1]


def _split_wait(name, mode, send_sems, recv_sems, srcs, lands, after):
    n = len(srcs)

    def body(*refs):
        src, land = refs[:n], refs[n:2 * n]
        ssem, rsem = refs[2 * n], refs[2 * n + 1]
        place = _my_place()
        for w in range(n):
            for j in range(len(SPLIT_MASKS[mode])):
                _split_copy(mode, src[w], land[w], ssem, rsem, w, j, place).wait_send()
                _split_copy(mode, src[w], land[w], ssem, rsem, w, j, place, arriving=True).wait_recv()
        if mode.startswith("gather"):
            local_sems = refs[-1]
            mine = []
            for w in range(n):
                r = src[w].shape[0]
                rows = land[w].at[pl.ds(pl.multiple_of(_dev_index(*place) * r, 16), r), :]
                mine.append(pltpu.make_async_copy(src[w], rows, local_sems.at[w]))
            for cp in mine:
                cp.start()
            for cp in mine:
                cp.wait()

    hbm = lambda t: pltpu.HBM(t.shape, t.dtype)
    out = pl.pallas_call(
        body, name=name,
        out_shape=tuple(hbm(t) for t in (*srcs, *lands)),
        in_specs=(HBM_SPEC,) * (2 * n) + (SEM_SPEC, SEM_SPEC, pl.BlockSpec(memory_space=pl.ANY)),
        out_specs=(HBM_SPEC,) * (2 * n),
        input_output_aliases={i: i for i in range(2 * n)},
        scratch_shapes=[pltpu.SemaphoreType.DMA((n,))] if mode.startswith("gather") else [],
        compiler_params=pltpu.CompilerParams(has_side_effects=pltpu.SideEffectType.DATAFLOW_SIDE_EFFECTING),
    )(*srcs, *lands, send_sems, recv_sems, after)
    return out[:n], out[n:]


def _forward_copy(zone_ref, send_sems, recv_sems, j, place, arriving=False):
    x, y, c = place
    r = zone_ref.shape[0] // N_DEV
    chip = _peer(x, y, c, SPLIT_MASKS["chips"][j])
    owner = _dev_index(chip[0], chip[1], 1 - c if arriving else c)
    rows = zone_ref.at[pl.ds(pl.multiple_of(owner * r, 16), r), :]
    return pltpu.make_async_remote_copy(src_ref=rows, dst_ref=rows, send_sem=send_sems.at[j], recv_sem=recv_sems.at[j],
                                        device_id=(x, y, 1 - c), device_id_type=MESH)


def _forward_start(name, zone):
    def body(zone_ref, send_sems, recv_sems, zone_thru, token):
        place = _my_place()
        for j in range(3):
            _forward_copy(zone_ref, send_sems, recv_sems, j, place).start()
        token[...] = jnp.zeros_like(token)

    out = pl.pallas_call(
        body, name=name,
        out_shape=(pltpu.SemaphoreType.DMA((3,)), pltpu.SemaphoreType.DMA((3,)), pltpu.HBM(zone.shape, zone.dtype),
                   jax.ShapeDtypeStruct((8, 128), F32)),
        in_specs=(HBM_SPEC,), out_specs=(SEM_SPEC, SEM_SPEC, HBM_SPEC, pl.BlockSpec(memory_space=pltpu.VMEM)),
        input_output_aliases={0: 2},
        compiler_params=pltpu.CompilerParams(has_side_effects=pltpu.SideEffectType.DATAFLOW_SIDE_EFFECTING),
    )(pltpu.with_memory_space_constraint(zone, pltpu.HBM))
    return out[0], out[1], out[2], out[3]


def _forward_wait(name, send_sems, recv_sems, zone, after):
    def body(zone_ref, ssem, rsem, after_ref, zone_out):
        place = _my_place()
        for j in range(3):
            _forward_copy(zone_ref, ssem, rsem, j, place).wait_send()
            _forward_copy(zone_ref, ssem, rsem, j, place, arriving=True).wait_recv()

    return pl.pallas_call(
        body, name=name, out_shape=pltpu.HBM(zone.shape, zone.dtype),
        in_specs=(HBM_SPEC, SEM_SPEC, SEM_SPEC, pl.BlockSpec(memory_space=pl.ANY)), out_specs=HBM_SPEC,
        input_output_aliases={0: 0},
        compiler_params=pltpu.CompilerParams(has_side_effects=pltpu.SideEffectType.DATAFLOW_SIDE_EFFECTING),
    )(zone, send_sems, recv_sems, after)


def _sibling_exchange(grads):
    nw = len(grads)
    HBM = pl.BlockSpec(memory_space=pl.ANY)

    def body(*refs):
        g, land = refs[:nw], refs[nw:2 * nw]
        send_sems, recv_sems = refs[2 * nw:]
        x, y, c = _my_place()

        def copy(w, q, owner_core):
            r = land[w].shape[1]
            return pltpu.make_async_remote_copy(
                src_ref=g[w].at[pl.ds(pl.multiple_of((2 * q + owner_core) * r, 16), r), :], dst_ref=land[w].at[q],
                send_sem=send_sems.at[w, q], recv_sem=recv_sems.at[w, q], device_id=(x, y, 1 - c), device_id_type=MESH)

        sends = [copy(w, q, 1 - c) for w in range(nw) for q in range(4)]
        for cp in sends:
            cp.start()
        for w in range(nw):
            for q in range(4):
                copy(w, q, c).wait_recv()
        for cp in sends:
            cp.wait_send()

    return pl.pallas_call(
        body, name="sibling_exchange",
        out_shape=[jax.ShapeDtypeStruct((4, a.shape[0] // N_DEV, a.shape[1]), a.dtype) for a in grads],
        in_specs=[HBM] * nw, out_specs=[HBM] * nw,
        scratch_shapes=[pltpu.SemaphoreType.DMA((nw, 4)), pltpu.SemaphoreType.DMA((nw, 4))],
    )(*grads)


def _pair_sums(gs, sibs, core):
    n = len(gs)

    def body(core_ref, *refs):
        for w in range(n):
            refs[2 * n + w][0] = (refs[w][0, 0].astype(F32) + refs[n + w][0].astype(F32)).astype(BF16)

    in_specs = [pl.BlockSpec((1, 1) + t.shape[1:], lambda q, core_ref: (q, core_ref[0], 0, 0)) for t in sibs]
    in_specs += [pl.BlockSpec((1,) + t.shape[1:], lambda q, core_ref: (q, 0, 0)) for t in sibs]
    return pl.pallas_call(
        body, name="pair_sums",
        grid_spec=pltpu.PrefetchScalarGridSpec(
            num_scalar_prefetch=1, grid=(4,), in_specs=in_specs,
            out_specs=[pl.BlockSpec((1,) + t.shape[1:], lambda q, core_ref: (q, 0, 0)) for t in sibs]),
        out_shape=[jax.ShapeDtypeStruct(t.shape, BF16) for t in sibs],
        compiler_params=_cparams("parallel"),
    )(core, *[g.reshape(4, 2, t.shape[1], t.shape[2]) for g, t in zip(gs, sibs)], *sibs)


def _allgather_small(v, name):
    r, ccols = v.shape

    def body(v_ref, out_ref, send_sems, recv_sems):
        x, y, c = _my_place()
        my_idx = _dev_index(x, y, c)
        out_ref[my_idx] = v_ref[...]

        def copy(m):
            peer = _peer(x, y, c, m)
            return pltpu.make_async_remote_copy(
                src_ref=v_ref, dst_ref=out_ref.at[my_idx],
                send_sem=send_sems.at[m - 1], recv_sem=recv_sems.at[m - 1], device_id=peer, device_id_type=MESH)

        def arrival(m):
            peer = _peer(x, y, c, m)
            return pltpu.make_async_remote_copy(
                src_ref=v_ref, dst_ref=out_ref.at[_dev_index(*peer)],
                send_sem=send_sems.at[m - 1], recv_sem=recv_sems.at[m - 1], device_id=peer, device_id_type=MESH)

        sends = [copy(m) for m in range(1, N_DEV)]
        for cp in sends:
            cp.start()
        for m in range(1, N_DEV):
            arrival(m).wait_recv()
        for cp in sends:
            cp.wait_send()

    return pl.pallas_call(
        body, name=name,
        out_shape=jax.ShapeDtypeStruct((N_DEV, r, ccols), v.dtype),
        in_specs=[pl.BlockSpec(memory_space=pltpu.VMEM)], out_specs=pl.BlockSpec(memory_space=pltpu.VMEM),
        scratch_shapes=[pltpu.SemaphoreType.DMA((7,)), pltpu.SemaphoreType.DMA((7,))],
    )(v)


def _conditioning(pay, w_ada, b_cols):
    ncol = w_ada.shape[1]

    def body(pay_ref, w_ref, b_ref, got_ref, act_ref, mod_ref, send_sems, recv_sems):
        x, y, c = _my_place()
        my_idx = _dev_index(x, y, c)

        def copy(rnd, buf, m, arriving=False):
            peer = _peer(x, y, c, m)
            slot = _dev_index(*peer) if arriving else my_idx
            return pltpu.make_async_remote_copy(
                src_ref=buf.at[my_idx], dst_ref=buf.at[slot], send_sem=send_sems.at[rnd, m - 1],
                recv_sem=recv_sems.at[rnd, m - 1], device_id=peer, device_id_type=MESH)

        def exchange(rnd, buf):
            sends = [copy(rnd, buf, m) for m in range(1, N_DEV)]
            for cp in sends:
                cp.start()
            for m in range(1, N_DEV):
                copy(rnd, buf, m, arriving=True).wait_recv()
            for cp in sends:
                cp.wait_send()

        got_ref[my_idx] = pay_ref[...]
        exchange(0, got_ref)
        cv = jnp.concatenate([got_ref[s, 0:1, :] for s in range(N_DEV)], axis=0)
        act = cv * _sigmoid(cv)
        act_ref[...] = act
        mod_ref[my_idx] = jnp.dot(act, w_ref[...], preferred_element_type=F32, precision=lax.Precision.HIGHEST) + b_ref[...]
        exchange(1, mod_ref)

    vmem = pl.BlockSpec(memory_space=pltpu.VMEM)
    return pl.pallas_call(
        body, name="conditioning",
        out_shape=[jax.ShapeDtypeStruct((N_DEV, 8, D), F32), jax.ShapeDtypeStruct((N_DEV, D), F32),
                   jax.ShapeDtypeStruct((N_DEV, N_DEV, ncol), F32)],
        in_specs=[vmem] * 3, out_specs=[vmem] * 3,
        scratch_shapes=[pltpu.SemaphoreType.DMA((2, 7)), pltpu.SemaphoreType.DMA((2, 7))],
        compiler_params=_cparams(),
    )(pay, w_ada, b_cols)


def _ada_bwd(act_t, gm_cols):
    def body(a_ref, g_ref, o_ref):
        o_ref[...] = jnp.dot(a_ref[...], g_ref[...], preferred_element_type=F32, precision=lax.Precision.HIGHEST)

    return pl.pallas_call(
        body, name="ada_bwd", out_shape=jax.ShapeDtypeStruct((D, gm_cols.shape[1]), F32), compiler_params=_cparams(),
    )(act_t, gm_cols)


def _row_tile(r):
    for t in (256, 304, 128, 64, 16):
        if r % t == 0:
            return t
    return r


def _sum_parts(parts, name, own=None):
    k, r, ccols = parts.shape
    tr = _row_tile(r)

    def body(*refs):
        p_ref, o_ref = refs[0], refs[-1]
        acc = p_ref[0].astype(F32) if own is None else refs[1][...].astype(F32) + p_ref[0].astype(F32)
        for s in range(1, k):
            acc = acc + p_ref[s].astype(F32)
        o_ref[...] = acc

    blk = pl.BlockSpec((tr, ccols), lambda i: (i, 0))
    return pl.pallas_call(
        body, name=name, grid=(r // tr,),
        in_specs=[pl.BlockSpec((k, tr, ccols), lambda i: (0, i, 0))] + ([] if own is None else [blk]),
        out_specs=blk,
        out_shape=jax.ShapeDtypeStruct((r, ccols), F32),
        compiler_params=_cparams("parallel"),
    )(*((parts,) if own is None else (parts, own)))


def _adamw(w, g, m, v, name):
    r, ccols = w.shape
    tr = _row_tile(r)
    c1 = 1.0 / (1.0 - B1 ** STEP)
    c2 = 1.0 / (1.0 - B2 ** STEP)

    def body(w_ref, g_ref, m_ref, v_ref, d_ref, nm_ref, nv_ref):
        gv = g_ref[...]
        nm = B1 * m_ref[...] + (1.0 - B1) * gv
        nv = B2 * v_ref[...] + (1.0 - B2) * jnp.square(gv)
        nm_ref[...] = nm
        nv_ref[...] = nv
        d_ref[...] = -LR * ((nm * c1) / (jnp.sqrt(nv * c2) + ADAM_EPS) + WD * w_ref[...])

    blk = pl.BlockSpec((tr, ccols), lambda i: (i, 0))
    return pl.pallas_call(
        body, name=name, grid=(r // tr,), in_specs=[blk] * 4, out_specs=[blk] * 3,
        out_shape=[jax.ShapeDtypeStruct((r, ccols), F32)] * 3,
        compiler_params=_cparams("parallel"),
    )(w, g, m, v)


def _sum_adamw(parts, own, w, m, v, name):
    k, r, ccols = parts.shape
    tr = _row_tile(r)
    c1 = 1.0 / (1.0 - B1 ** STEP)
    c2 = 1.0 / (1.0 - B2 ** STEP)

    def body(p_ref, own_ref, w_ref, m_ref, v_ref, g_ref, d_ref, nm_ref, nv_ref):
        gv = own_ref[...].astype(F32)
        for s in range(k):
            gv = gv + p_ref[s].astype(F32)
        g_ref[...] = gv
        nm = B1 * m_ref[...] + (1.0 - B1) * gv
        nv = B2 * v_ref[...] + (1.0 - B2) * jnp.square(gv)
        nm_ref[...] = nm
        nv_ref[...] = nv
        d_ref[...] = -LR * ((nm * c1) / (jnp.sqrt(nv * c2) + ADAM_EPS) + WD * w_ref[...])

    blk = pl.BlockSpec((tr, ccols), lambda i: (i, 0))
    return pl.pallas_call(
        body, name=name, grid=(r // tr,),
        in_specs=[pl.BlockSpec((k, tr, ccols), lambda i: (0, i, 0))] + [blk] * 4, out_specs=[blk] * 4,
        out_shape=[jax.ShapeDtypeStruct((r, ccols), F32)] * 4,
        compiler_params=_cparams("parallel"),
    )(parts, own, w, m, v)


VEC_ROWS = ((0, 6), (6, 7), (9, 11), (11, 14), (7, 8), (8, 9))


def _adamw_vectors(w, g, m, v):
    c1 = 1.0 / (1.0 - B1 ** STEP)
    c2 = 1.0 / (1.0 - B2 ** STEP)

    def put(refs, p):
        for ref, (lo, hi) in zip(refs, VEC_ROWS):
            if ref.shape == (3, HEAD):
                ref[...] = p[lo:hi, :HEAD]
            else:
                ref[...] = jnp.concatenate([p[k:k + 1] for k in range(lo, hi)], axis=1)

    def body(w_ref, g_ref, m_ref, v_ref, *outs):
        gv = g_ref[...]
        nm = B1 * m_ref[...] + (1.0 - B1) * gv
        nv = B2 * v_ref[...] + (1.0 - B2) * jnp.square(gv)
        delta = -LR * ((nm * c1) / (jnp.sqrt(nv * c2) + ADAM_EPS) + WD * w_ref[...])
        for kind, p in enumerate((gv, delta, nm, nv)):
            put(outs[6 * kind:6 * kind + 6], p)

    shapes = [(1, 6 * D), (1, D), (1, 2 * D), (3, HEAD), (1, D), (1, D)]
    out = pl.pallas_call(
        body, name="adamw_vectors", out_shape=[jax.ShapeDtypeStruct(sh, F32) for sh in shapes] * 4, compiler_params=_cparams(),
    )(w, g, m, v)
    fix = lambda t: (t[0], t[1], t[2], t[3][None], t[4], t[5].reshape(D))
    return [fix(out[6 * kind:6 * kind + 6]) for kind in range(4)]


def _pack_vectors(b_ada, g_mix, g_mlp, g_fin, b_gate, conv_w):
    conv_rows = jnp.pad(conv_w.reshape(3, HEAD), ((0, 0), (0, D - HEAD)))
    return jnp.concatenate([b_ada.reshape(6, D), g_mix.reshape(1, D), g_mlp.reshape(1, D), g_fin.reshape(1, D),
                            b_gate.reshape(2, D), conv_rows, jnp.zeros((2, D), F32)], axis=0)


def kernel(x, c, w_ada, b_ada, g_norm_mix, w_in, b_gate, conv_w, w_branch_attn, w_branch_conv, w_out, g_norm_mlp, w_mlp_in, w_mlp_out, g_norm_final, loss_target, m_w_ada, m_b_ada, m_g_norm_mix, m_w_in, m_b_gate, m_conv_w, m_w_branch_attn, m_w_branch_conv, m_w_out, m_g_norm_mlp, m_w_mlp_in, m_w_mlp_out, m_g_norm_final, v_w_ada, v_b_ada, v_g_norm_mix, v_w_in, v_b_gate, v_conv_w, v_w_branch_attn, v_w_branch_conv, v_w_out, v_g_norm_mlp, v_w_mlp_in, v_w_mlp_out, v_g_norm_final):
    S = x.shape[1]
    xi, yi, ci = _my_place()
    me = _dev_index(xi, yi, ci)
    x2 = x.reshape(S, D)
    tgt = loss_target.reshape(S, D)

    pay = jnp.zeros((8, D), F32).at[0].set(c[0]).at[1:4, :HEAD].set(conv_w[0])
    ncol = w_ada.shape[2]
    b_cols = lax.dynamic_slice(b_ada, (0, me * ncol), (1, ncol))
    got, act, mod_all = _conditioning(pay, w_ada[0], b_cols)
    cw8 = jnp.pad(got[:, 1:4, :HEAD].transpose(1, 0, 2).reshape(3, D), ((0, 5), (0, 0)))

    w_in_shard, mod_all = lax.optimization_barrier((w_in[0].T.astype(BF16), mod_all))
    mod = lax.dynamic_index_in_dim(mod_all, me, axis=1, keepdims=False).reshape(6, D)
    h = _prenorm(x2, g_norm_mix, mod[1:2], mod[0:1])
    (w_int,) = _allgather_weights([w_in_shard])
    late = [w_branch_attn[0].T.astype(BF16), w_branch_conv[0].astype(BF16), w_out[0].astype(BF16),
            w_mlp_in[0].T.astype(BF16), w_mlp_out[0].astype(BF16)]
    w_int, late = lax.optimization_barrier((w_int, late))
    zones = [lax.empty((N_DEV * t.shape[0], t.shape[1]), BF16) for t in late]
    ag_mix = _split_start("gather_mix_start", "gather", late[:3], zones[:3])
    first, mlp_shards = lax.optimization_barrier((ag_mix[4], late[3:]))
    ag_mlp = _split_start("gather_mlp_start", "gather", mlp_shards, zones[3:])

    def mix_weights(o_attn):
        return _split_wait("gather_mix_wait", "gather", *ag_mix[:4], o_attn)[1]

    def mlp_weights(x1):
        return _split_wait("gather_mlp_wait", "gather", *ag_mlp[:4], x1)[1]

    rs = {}

    def mlp_grads_ready(*grads):
        lands = [lax.empty((N_PEER, t.shape[0] // N_DEV, t.shape[1]), BF16) for t in grads]
        rs["mlp"] = _split_start("scatter_mlp_start", "scatter", grads, lands)
        return rs["mlp"][4]

    def other_grads_ready(*grads):
        core = ci.reshape(1).astype(jnp.int32)
        pair = _pair_sums(grads, _sibling_exchange(grads), core)
        lands = [lax.empty((3,) + t.shape[1:], BF16) for t in pair]
        rs["rest"] = _split_start("scatter_rest_start", "chips", pair, lands)
        return rs["rest"][4]

    ba, bb = b_gate[:, :D], b_gate[:, D:]
    grad_x, vec = _local_step(
        x2, h, tgt, mod + first + ag_mlp[4], g_norm_mix, g_norm_mlp, g_norm_final.reshape(1, D), ba, bb, cw8, w_int, mix_weights, mlp_weights,
        mlp_grads_ready, other_grads_ready)

    vec_all = _allgather_small(vec, "gather_vec")
    vec_sum = _sum_parts(vec_all, "sum_vec")
    loss = vec_sum[14, 0]
    gm_all = vec_all[:, 0:6, :].reshape(N_DEV, 6 * D)
    gm_cols = lax.dynamic_slice(gm_all, (0, me * ncol), (N_DEV, ncol))
    g_w_ada = _ada_bwd(act.T, gm_cols)
    conv_cols = lax.dynamic_slice(vec_sum[11:14], (0, me * HEAD), (3, HEAD))
    g_pack = jnp.concatenate([vec_sum[0:11], jnp.pad(conv_cols, ((0, 0), (0, D - HEAD))), jnp.zeros((2, D), F32)], axis=0)
    packs = [_pack_vectors(*t) for t in ((b_ada, g_norm_mix, g_norm_mlp, g_norm_final, b_gate, conv_w),
                                         (m_b_ada, m_g_norm_mix, m_g_norm_mlp, m_g_norm_final, m_b_gate, m_conv_w),
                                         (v_b_ada, v_g_norm_mix, v_g_norm_mlp, v_g_norm_final, v_b_gate, v_conv_w))]
    gv, dv, mv, vv = _adamw_vectors(packs[0], g_pack, packs[1], packs[2])
    d_ada, nm_ada, nv_ada = _adamw(w_ada[0], g_w_ada, m_w_ada[0], v_w_ada[0], "adamw_w_ada")

    big = {}
    srcs, lands = _split_wait("scatter_mlp_wait", "scatter", *rs["mlp"][:4], d_ada)
    own = [lax.dynamic_slice(g, (me * land.shape[1], 0), land.shape[1:]) for g, land in zip(srcs, lands)]
    g_mi = _sum_parts(lands[0], "sum_w_mi", own=own[0]).T
    big["w_mi"] = (g_mi[None],) + tuple(t[None] for t in _adamw(w_mlp_in[0], g_mi, m_w_mlp_in[0], v_w_mlp_in[0], "adamw_w_mi"))
    big["w_mo"] = tuple(t[None] for t in _sum_adamw(lands[1], own[1], w_mlp_out[0], m_w_mlp_out[0], v_w_mlp_out[0], "adamw_w_mo"))
    srcs, lands = _split_wait("scatter_rest_wait", "chips", *rs["rest"][:4], big["w_mo"][1])
    own = [lax.dynamic_index_in_dim(pair, 2 * xi + yi, axis=0, keepdims=False) for pair in srcs]
    big["w_in"] = tuple(t.T[None] for t in _sum_adamw(lands[0], own[0], w_in[0].T, m_w_in[0].T, v_w_in[0].T, "adamw_w_in"))
    g_ba = _sum_parts(lands[1], "sum_w_ba", own=own[1]).T
    big["w_ba"] = (g_ba[None],) + tuple(t[None] for t in _adamw(w_branch_attn[0], g_ba, m_w_branch_attn[0], v_w_branch_attn[0], "adamw_w_ba"))
    big["w_bc"] = tuple(t[None] for t in _sum_adamw(lands[2], own[2], w_branch_conv[0], m_w_branch_conv[0], v_w_branch_conv[0], "adamw_w_bc"))
    big["w_out"] = tuple(t[None] for t in _sum_adamw(lands[3], own[3], w_out[0], m_w_out[0], v_w_out[0], "adamw_w_out"))

    def ordered(k, ada, vecs):
        return (ada[None], vecs[0], vecs[1], big["w_in"][k], vecs[2], vecs[3], big["w_ba"][k], big["w_bc"][k],
                big["w_out"][k], vecs[4], big["w_mi"][k], big["w_mo"][k], vecs[5])

    return (loss, grad_x.reshape(1, S, D), *ordered(0, g_w_ada, gv), *ordered(1, d_ada, dv),
            *ordered(2, nm_ada, mv), *ordered(3, nv_ada, vv))
```

```python
import functools

import numpy as np
import jax
import jax.numpy as jnp
from jax import lax
from jax.experimental import pallas as pl
from jax.experimental.pallas import tpu as pltpu

F32, BF16 = jnp.float32, jnp.bfloat16
D = 1024
HEAD = 128
DILATIONS = (1, 4, 16)
N_SLOT = 4
AOW = N_SLOT * HEAD
DFF = 4 * D
N_DEV = 8
UNROLL = 8
EPS = 1e-6
NEG = -1e30
SCALE = HEAD ** -0.5
LR, B1, B2, ADAM_EPS, WD, STEP = 0.001, 0.9, 0.999, 1e-08, 0.01, 10
V7X_VMEM_LIMIT = 56 * 1024 * 1024
TM = 1024
MESH = pl.DeviceIdType.MESH
AXES = ("x", "y", "c")


def _cparams(*sem):
    if sem:
        return pltpu.CompilerParams(dimension_semantics=sem, vmem_limit_bytes=V7X_VMEM_LIMIT)
    return pltpu.CompilerParams(vmem_limit_bytes=V7X_VMEM_LIMIT)


def _nn(a, b):
    return jnp.dot(a, b, preferred_element_type=F32)


def _nt(a, b):
    return lax.dot_general(a, b, (((1,), (1,)), ((), ())), preferred_element_type=F32)


def _tn(a, b):
    return lax.dot_general(a, b, (((0,), (0,)), ((), ())), preferred_element_type=F32)


def _rms_r(x):
    return lax.rsqrt(jnp.mean(x * x, axis=-1, keepdims=True) + EPS)


def _rms_bwd(x, r, g, dn):
    gy = dn * g
    dx = r * gy - x * (r * r * r) * jnp.mean(x * gy, axis=-1, keepdims=True)
    return dx, dn * (x * r)


def _sigmoid(t):
    return 1.0 / (1.0 + jnp.exp(-t))


def _rowsum(v):
    return jnp.sum(v, axis=0, keepdims=True)


def _vec_spec(n=D):
    return pl.BlockSpec((1, n), lambda *_: (0, 0))


def _const_spec(shape):
    nd = len(shape)
    return pl.BlockSpec(shape, lambda *_: (0,) * nd)


def _win_rowblock(j):
    return jnp.where(j < 9, (j % 3) * 3 + j // 3, j)


def _prenorm(x, g, sc, sh):
    S = x.shape[0]
    tm = TM

    def body(x_ref, g_ref, sc_ref, sh_ref, h_ref):
        xv = x_ref[...]
        h_ref[...] = (xv * _rms_r(xv) * g_ref[...] * (1.0 + sc_ref[...]) + sh_ref[...]).astype(BF16)

    row = pl.BlockSpec((tm, D), lambda i: (i, 0))
    return pl.pallas_call(
        body, name="prenorm", grid=(S // tm,), in_specs=[row, _vec_spec(), _vec_spec(), _vec_spec()], out_specs=row,
        out_shape=jax.ShapeDtypeStruct((S, D), BF16), compiler_params=_cparams("parallel"),
    )(x, g, sc, sh)


def _proj(h, w_int):
    S = h.shape[0]

    def body(h_ref, w_ref, q_ref, e_ref):
        j = pl.program_id(0)
        acc = _nt(h_ref[...], w_ref[...])

        @pl.when(j < 9)
        def _():
            q_ref[0] = acc

        @pl.when(j >= 9)
        def _():
            e_ref[0] = acc.astype(BF16)

    def e_idx(j):
        k = jnp.maximum(j - 9, 0)
        return (k // 2, 0, k % 2)

    return pl.pallas_call(
        body, name="proj", grid=(19,),
        in_specs=[pl.BlockSpec((S, D), lambda j: (0, 0), pipeline_mode=pl.Buffered(1)),
                  pl.BlockSpec((512, D), lambda j: (_win_rowblock(j), 0))],
        out_specs=[pl.BlockSpec((1, S, 512), lambda j: (jnp.minimum(j, 8), 0, 0)), pl.BlockSpec((1, S, 512), e_idx)],
        out_shape=[jax.ShapeDtypeStruct((9, S, 512), F32), jax.ShapeDtypeStruct((5, S, D), BF16)],
        compiler_params=_cparams("arbitrary"),
    )(h, w_int)


def _bias_table():
    slopes = (2.0 ** (-8.0 * np.arange(1, 13, dtype=np.float32) / 12.0)).astype(np.float32)
    qi = np.arange(HEAD)[:, None]
    kj = np.arange(2 * HEAD)[None, :]
    delta = HEAD + qi - kj
    mask = (delta >= 0) & (delta <= HEAD)
    out = np.zeros((3, N_SLOT, HEAD, 2 * HEAD), np.float32)
    for gi, d in enumerate(DILATIONS):
        for j in range(N_SLOT):
            bias = -slopes[gi * N_SLOT + j] * (delta * d).astype(np.float32)
            out[gi, j] = np.where(mask, bias, NEG)
    out_t = np.concatenate([out[..., HEAD:].swapaxes(-1, -2), out[..., :HEAD].swapaxes(-1, -2)], axis=-1)
    return jnp.asarray(out), jnp.asarray(out_t)


def _block_rows(b, d):
    r = b % d
    n = b // d
    st = n * (HEAD * d) + r
    stp = jnp.maximum(n - 1, 0) * (HEAD * d) + r
    return n, st, stp


def _attn_fwd(qkv, bias):
    S = qkv.shape[2]
    nblk = S // HEAD
    rows = 256

    def body(qkv_ref, b_ref, o_ref, lse_ref, o_s, lse_s):
        g = pl.program_id(1)
        bias = b_ref[0, 0]
        col = lax.broadcasted_iota(jnp.int32, bias.shape, 1)
        bias_first = jnp.where(col < HEAD, NEG, bias)

        for gi, d in enumerate(DILATIONS):
            @pl.when(g == gi)
            def _(gi=gi, d=d):
                def step(b, carry):
                    n, st, stp = _block_rows(b, d)
                    cur = pl.ds(st, HEAD, stride=d)
                    prv = pl.ds(stp, HEAD, stride=d)
                    q = qkv_ref.at[0, 0][cur, :].astype(BF16)
                    kw = jnp.concatenate([qkv_ref.at[0, 1][prv, :], qkv_ref.at[0, 1][cur, :]], axis=0).astype(BF16)
                    vw = jnp.concatenate([qkv_ref.at[0, 2][prv, :], qkv_ref.at[0, 2][cur, :]], axis=0).astype(BF16)
                    s = _nt(q, kw) * SCALE + jnp.where(n > 0, bias, bias_first)
                    m = jnp.max(s, axis=-1, keepdims=True)
                    p = jnp.exp(s - m)
                    l = jnp.sum(p, axis=-1, keepdims=True)
                    o_s.at[gi][cur, :] = _nn(p.astype(BF16), vw) / l
                    lse_s.at[gi][cur, :] = jnp.broadcast_to(m + jnp.log(l), (HEAD, HEAD))
                    return carry

                lax.fori_loop(0, nblk, step, 0, unroll=UNROLL)

        @pl.when(g == len(DILATIONS) - 1)
        def _():
            def merge(i, carry):
                r = pl.ds(pl.multiple_of(i * rows, rows), rows)
                ls = [lse_s[k, r, :] for k in range(3)]
                top = jnp.maximum(jnp.maximum(ls[0], ls[1]), ls[2])
                ws = [jnp.exp(t - top) for t in ls]
                den = ws[0] + ws[1] + ws[2]
                o_ref[r, :] = (ws[0] * o_s[0, r, :] + ws[1] * o_s[1, r, :] + ws[2] * o_s[2, r, :]) / den
                lse_ref[r, :] = top + jnp.log(den)
                return carry

            lax.fori_loop(0, S // rows, merge, 0)

    return pl.pallas_call(
        body, name="attn_fwd", grid=(N_SLOT, 3),
        in_specs=[pl.BlockSpec((1, 3, S, HEAD), lambda j, g: (g, 0, 0, j)),
                  pl.BlockSpec((1, 1, HEAD, 2 * HEAD), lambda j, g: (g, j, 0, 0))],
        out_specs=[pl.BlockSpec((S, HEAD), lambda j, g: (0, j)), pl.BlockSpec((S, HEAD), lambda j, g: (0, j))],
        out_shape=[jax.ShapeDtypeStruct((S, AOW), F32), jax.ShapeDtypeStruct((S, AOW), F32)],
        scratch_shapes=[pltpu.VMEM((3, S, HEAD), F32)] * 2,
        compiler_params=_cparams("parallel", "arbitrary"),
    )(qkv, bias)


def _shift_down(z, k, halo_rows):
    out = pltpu.roll(z, k, axis=0)
    top = out[:8]
    rid = lax.broadcasted_iota(jnp.int32, top.shape, 0)
    for t in range(k):
        top = jnp.where(rid == t, halo_rows[t], top)
    return jnp.concatenate([top, out[8:]], axis=0)


def _shift_up(z, k, halo_rows):
    n = z.shape[0]
    out = pltpu.roll(z, n - k, axis=0)
    bottom = out[n - 8:]
    rid = lax.broadcasted_iota(jnp.int32, bottom.shape, 0)
    for t in range(k):
        bottom = jnp.where(rid == 8 - k + t, halo_rows[t], bottom)
    return jnp.concatenate([out[:n - 8], bottom], axis=0)


def _e_spec(chunk, tm):
    return pl.BlockSpec((1, tm, D), lambda i, c=chunk: (c, i, 0))


def _e_prev_spec(chunk, tm):
    return pl.BlockSpec((1, 16, D), lambda i, c=chunk: (c, jnp.maximum(i * (tm // 16) - 1, 0), 0))


def _e_next_spec(chunk, tm, S):
    return pl.BlockSpec((1, 16, D), lambda i, c=chunk: (c, jnp.minimum((i + 1) * (tm // 16), S // 16 - 1), 0))


def _mix(o_attn, e, cw8, ba, bb, w_bat, w_bc):
    S = o_attn.shape[0]
    tm = 256

    def body(o_ref, cb_ref, cc_ref, cx_ref, ga_ref, gb_ref, ccp_ref, cxp_ref, cw_ref, ba_ref, bb_ref, wba_ref, wbc_ref,
             obf_ref, cbu_ref, ya_ref, yc_ref, mg_ref):
        i = pl.program_id(0)
        o = o_ref[...].astype(BF16)
        obf_ref[...] = o
        ya = _nt(o, wba_ref[...])
        z = cc_ref[0].astype(F32) * cx_ref[0].astype(F32)
        zp = ccp_ref[0].astype(F32) * cxp_ref[0].astype(F32) * (i > 0).astype(F32)
        z1 = _shift_down(z, 1, [zp[15:16]])
        z2 = _shift_down(z, 2, [zp[14:15], zp[15:16]])
        cw = cw_ref[...]
        u = cw[0:1] * z2 + cw[1:2] * z1 + cw[2:3] * z
        cbu = (cb_ref[0].astype(F32) * u).astype(BF16)
        cbu_ref[...] = cbu
        yc = _nn(cbu, wbc_ref[...])
        sa = _sigmoid(ga_ref[0].astype(F32) + ba_ref[...])
        sb = _sigmoid(gb_ref[0].astype(F32) + bb_ref[...])
        ya_ref[...] = ya.astype(BF16)
        yc_ref[...] = yc.astype(BF16)
        mg_ref[...] = (sa * ya + sb * yc).astype(BF16)

    row = lambda w: pl.BlockSpec((tm, w), lambda i: (i, 0))
    return pl.pallas_call(
        body, name="mix", grid=(S // tm,),
        in_specs=[row(AOW)] + [_e_spec(c, tm) for c in range(5)] + [_e_prev_spec(1, tm), _e_prev_spec(2, tm),
                  _const_spec((8, D)), _vec_spec(), _vec_spec(), _const_spec((D, AOW)), _const_spec((D, D))],
        out_specs=[row(AOW), row(D), row(D), row(D), row(D)],
        out_shape=[jax.ShapeDtypeStruct((S, AOW), BF16)] + [jax.ShapeDtypeStruct((S, D), BF16)] * 4,
        compiler_params=_cparams("parallel"),
    )(o_attn, e, e, e, e, e, e, e, cw8, ba, bb, w_bat, w_bc)


def _out_proj(merged, w_out, x, gate1, g_mlp, sc2, sh2):
    S = x.shape[0]
    tm = TM

    def body(mg_ref, w_ref, x_ref, gt_ref, g_ref, sc_ref, sh_ref, x1_ref, mo_ref, h2_ref):
        mo = _nn(mg_ref[...], w_ref[...])
        mo_ref[...] = mo.astype(BF16)
        x1 = x_ref[...] + gt_ref[...] * mo
        x1_ref[...] = x1
        h2 = x1 * _rms_r(x1) * g_ref[...] * (1.0 + sc_ref[...]) + sh_ref[...]
        h2_ref[...] = h2.astype(BF16)

    row = pl.BlockSpec((tm, D), lambda i: (i, 0))
    return pl.pallas_call(
        body, name="out_proj", grid=(S // tm,),
        in_specs=[row, _const_spec((D, D)), row, _vec_spec(), _vec_spec(), _vec_spec(), _vec_spec()],
        out_specs=[row, row, row],
        out_shape=[jax.ShapeDtypeStruct((S, D), F32), jax.ShapeDtypeStruct((S, D), BF16), jax.ShapeDtypeStruct((S, D), BF16)],
        compiler_params=_cparams("parallel"),
    )(merged, w_out, x, gate1, g_mlp, sc2, sh2)


def _mlp_in(h2, w_mit):
    S = h2.shape[0]
    tm, tn = TM, 2048

    def body(h_ref, w_ref, a_ref, f_ref):
        a = _nt(h_ref[...], w_ref[...])
        a_ref[...] = a.astype(BF16)
        f_ref[...] = jnp.square(jnp.maximum(a, 0.0)).astype(BF16)

    blk = pl.BlockSpec((tm, tn), lambda i, j: (i, j))
    return pl.pallas_call(
        body, name="mlp_in", grid=(S // tm, DFF // tn),
        in_specs=[pl.BlockSpec((tm, D), lambda i, j: (i, 0)), pl.BlockSpec((tn, D), lambda i, j: (j, 0))],
        out_specs=[blk, blk],
        out_shape=[jax.ShapeDtypeStruct((S, DFF), BF16)] * 2,
        compiler_params=_cparams("parallel", "parallel"),
    )(h2, w_mit)


def _mlp_out(f, w_mo, x1, gate2, g_fin, tgt):
    S = x1.shape[0]
    tm = 512
    half = tm // 2

    def body(f_ref, w_ref, x1_ref, gt_ref, g_ref, t_ref, mlp_ref, dx2_ref, pv_ref):
        @pl.when(pl.program_id(0) == 0)
        def _():
            pv_ref[...] = jnp.zeros_like(pv_ref)

        g = g_ref[...]
        for hs in (pl.ds(0, half), pl.ds(half, half)):
            mlp = _nn(f_ref[hs, :], w_ref[...])
            mlp_ref[hs, :] = mlp.astype(BF16)
            x2 = x1_ref[hs, :] + gt_ref[...] * mlp
            r = _rms_r(x2)
            err = x2 * r * g - t_ref[hs, :]
            dx2, pg = _rms_bwd(x2, r, g, err * (1.0 / D))
            dx2_ref[hs, :] = dx2
            pv_ref[0:1, :] += _rowsum(pg)
            pv_ref[1:2, :] += 0.5 * _rowsum(jnp.mean(err * err, axis=-1, keepdims=True))

    row = pl.BlockSpec((tm, D), lambda i: (i, 0))
    return pl.pallas_call(
        body, name="mlp_out", grid=(S // tm,),
        in_specs=[pl.BlockSpec((tm, DFF), lambda i: (i, 0)), _const_spec((DFF, D)), row, _vec_spec(), _vec_spec(), row],
        out_specs=[row, row, _const_spec((8, D))],
        out_shape=[jax.ShapeDtypeStruct((S, D), BF16), jax.ShapeDtypeStruct((S, D), F32), jax.ShapeDtypeStruct((8, D), F32)],
        compiler_params=_cparams("arbitrary"),
    )(f, w_mo, x1, gate2, g_fin, tgt)


def _bwd_mlp_a(dx2, gate2, mlp, w_mo, a):
    S = dx2.shape[0]
    tm = 512
    half = tm // 2

    def body(dx_ref, gt_ref, mlp_ref, w_ref, a_ref, da_ref, dmo_ref, pv_ref):
        @pl.when(pl.program_id(0) == 0)
        def _():
            pv_ref[...] = jnp.zeros_like(pv_ref)

        for hs in (pl.ds(0, half), pl.ds(half, half)):
            dx = dx_ref[hs, :]
            dmo = (dx * gt_ref[...]).astype(BF16)
            dmo_ref[hs, :] = dmo
            pv_ref[0:1, :] += _rowsum(dx * mlp_ref[hs, :].astype(F32))
            df = _nt(dmo, w_ref[...])
            da_ref[hs, :] = (df * (2.0 * jnp.maximum(a_ref[hs, :].astype(F32), 0.0))).astype(BF16)

    row = pl.BlockSpec((tm, D), lambda i: (i, 0))
    wide = pl.BlockSpec((tm, DFF), lambda i: (i, 0))
    return pl.pallas_call(
        body, name="bwd_mlp_a", grid=(S // tm,),
        in_specs=[row, _vec_spec(), row, _const_spec((DFF, D)), wide],
        out_specs=[wide, row, _const_spec((8, D))],
        out_shape=[jax.ShapeDtypeStruct((S, DFF), BF16), jax.ShapeDtypeStruct((S, D), BF16), jax.ShapeDtypeStruct((8, D), F32)],
        compiler_params=_cparams("arbitrary"),
    )(dx2, gate2, mlp, w_mo, a)


def _bwd_mlp_b(da, w_mit, x1, dx2, g_mlp, sc2):
    S = x1.shape[0]
    tm = 512
    half = tm // 2

    def body(da_ref, w_ref, x1_ref, dx2_ref, g_ref, sc_ref, dx1_ref, pv_ref):
        @pl.when(pl.program_id(0) == 0)
        def _():
            pv_ref[...] = jnp.zeros_like(pv_ref)

        g = g_ref[...]
        for hs in (pl.ds(0, half), pl.ds(half, half)):
            dh = _nn(da_ref[hs, :], w_ref[...])
            x1 = x1_ref[hs, :]
            r = _rms_r(x1)
            dxn, pg = _rms_bwd(x1, r, g, dh * (1.0 + sc_ref[...]))
            dx1_ref[hs, :] = dx2_ref[hs, :] + dxn
            pv_ref[0:1, :] += _rowsum(dh)
            pv_ref[1:2, :] += _rowsum(dh * (x1 * r * g))
            pv_ref[2:3, :] += _rowsum(pg)

    row = pl.BlockSpec((tm, D), lambda i: (i, 0))
    return pl.pallas_call(
        body, name="bwd_mlp_b", grid=(S // tm,),
        in_specs=[pl.BlockSpec((tm, DFF), lambda i: (i, 0)), _const_spec((DFF, D)), row, row, _vec_spec(), _vec_spec()],
        out_specs=[row, _const_spec((8, D))],
        out_shape=[jax.ShapeDtypeStruct((S, D), F32), jax.ShapeDtypeStruct((8, D), F32)],
        compiler_params=_cparams("arbitrary"),
    )(da, w_mit, x1, dx2, g_mlp, sc2)


def _bwd_mix(dx1, gate1, mo, e, cw8, ba, bb, ya, yc, o_attn, w_out, w_bc, w_bat):
    S = dx1.shape[0]
    tm = 256
    n_tiles = S // tm

    def body(dx_ref, dxn_ref, gt_ref, mo_ref, cb_ref, cc_ref, cx_ref, ga_ref, gb_ref, cbn_ref, gbn_ref, ccp_ref, cxp_ref,
             cw_ref, ba_ref, bb_ref, ya_ref, yc_ref, o_ref, wout_ref, wbc_ref, wba_ref,
             dmo_ref, dya_ref, dyc_ref, do_ref, dl_ref, de_ref, pv_ref):
        i = pl.program_id(0)

        @pl.when(i == 0)
        def _():
            pv_ref[...] = jnp.zeros_like(pv_ref)

        gate = gt_ref[...]
        bbv = bb_ref[...]

        def conv_branch_grad(dx_rows, gb_rows):
            dmo = (dx_rows * gate).astype(BF16)
            dmg = _nt(dmo, wout_ref[...])
            sb = _sigmoid(gb_rows + bbv)
            dyc = dmg * sb
            return dmo, dmg, sb, dyc, _nt(dyc.astype(BF16), wbc_ref[...])

        dx = dx_ref[...]
        cb = cb_ref[0].astype(F32)
        cc = cc_ref[0].astype(F32)
        cx = cx_ref[0].astype(F32)
        dmo, dmg, sb, dyc, dcbu = conv_branch_grad(dx, gb_ref[0].astype(F32))
        dmo_ref[...] = dmo
        pv_ref[0:1, :] += _rowsum(dx * mo_ref[...].astype(F32))
        sa = _sigmoid(ga_ref[0].astype(F32) + ba_ref[...])
        dya = (dmg * sa).astype(BF16)
        dya_ref[...] = dya
        dyc_ref[...] = dyc.astype(BF16)
        dga = dmg * ya_ref[...].astype(F32) * sa * (1.0 - sa)
        dgb = dmg * yc_ref[...].astype(F32) * sb * (1.0 - sb)
        pv_ref[1:2, :] += _rowsum(dga)
        pv_ref[2:3, :] += _rowsum(dgb)

        do = _nn(dya, wba_ref[...])
        do_ref[...] = do
        prod = do * o_ref[...]
        dl_ref[...] = jnp.concatenate(
            [jnp.broadcast_to(jnp.sum(prod[:, s * HEAD:(s + 1) * HEAD], axis=-1, keepdims=True), (tm, HEAD))
             for s in range(N_SLOT)], axis=1)

        z = cc * cx
        zp = ccp_ref[0].astype(F32) * cxp_ref[0].astype(F32) * (i > 0).astype(F32)
        z1 = _shift_down(z, 1, [zp[15:16]])
        z2 = _shift_down(z, 2, [zp[14:15], zp[15:16]])
        cw = cw_ref[...]
        u = cw[0:1] * z2 + cw[1:2] * z1 + cw[2:3] * z
        du = dcbu * cb
        dcbu_n = conv_branch_grad(dxn_ref[...], gbn_ref[0].astype(F32))[4]
        du_n = dcbu_n * cbn_ref[0].astype(F32) * (i < n_tiles - 1).astype(F32)
        du1 = _shift_up(du, 1, [du_n[0:1]])
        du2 = _shift_up(du, 2, [du_n[0:1], du_n[1:2]])
        dz = cw[2:3] * du + cw[1:2] * du1 + cw[0:1] * du2
        pv_ref[3:4, :] += _rowsum(du * z2)
        pv_ref[4:5, :] += _rowsum(du * z1)
        pv_ref[5:6, :] += _rowsum(du * z)

        de_ref[0] = (dcbu * u).astype(BF16)
        de_ref[1] = (dz * cx).astype(BF16)
        de_ref[2] = (dz * cc).astype(BF16)
        de_ref[3] = dga.astype(BF16)
        de_ref[4] = dgb.astype(BF16)

    row = lambda w: pl.BlockSpec((tm, w), lambda i: (i, 0))
    nxt = pl.BlockSpec((16, D), lambda i: (jnp.minimum((i + 1) * (tm // 16), S // 16 - 1), 0))
    return pl.pallas_call(
        body, name="bwd_mix", grid=(n_tiles,),
        in_specs=[row(D), nxt, _vec_spec(), row(D)] + [_e_spec(c, tm) for c in range(5)]
                 + [_e_next_spec(0, tm, S), _e_next_spec(4, tm, S), _e_prev_spec(1, tm), _e_prev_spec(2, tm),
                    _const_spec((8, D)), _vec_spec(), _vec_spec(), row(D), row(D), row(AOW),
                    _const_spec((D, D)), _const_spec((D, D)), _const_spec((D, AOW))],
        out_specs=[row(D), row(D), row(D), row(AOW), row(AOW), pl.BlockSpec((5, tm, D), lambda i: (0, i, 0)),
                   _const_spec((8, D))],
        out_shape=[jax.ShapeDtypeStruct((S, D), BF16)] * 3 + [jax.ShapeDtypeStruct((S, AOW), F32)] * 2
                  + [jax.ShapeDtypeStruct((5, S, D), BF16), jax.ShapeDtypeStruct((8, D), F32)],
        compiler_params=_cparams("arbitrary"),
    )(dx1, dx1, gate1, mo, e, e, e, e, e, e, e, e, e, cw8, ba, bb, ya, yc, o_attn, w_out, w_bc, w_bat)


def _attn_bwd(qkv, do, lse, dl, bias_t):
    S = qkv.shape[2]
    nblk = S // HEAD

    def body(qkv_ref, do_ref, lse_ref, dl_ref, b_ref, d_ref):
        g = pl.program_id(1)
        bias = b_ref[0, 0]
        col = lax.broadcasted_iota(jnp.int32, bias.shape, 1)
        bias_last = jnp.where(col >= HEAD, NEG, bias)
        eye = (lax.broadcasted_iota(jnp.int32, (HEAD, HEAD), 0) == lax.broadcasted_iota(jnp.int32, (HEAD, HEAD), 1)).astype(F32)

        def as_row(t):
            return jnp.sum(t * eye, axis=0, keepdims=True)

        for gi, d in enumerate(DILATIONS):
            @pl.when(g == gi)
            def _(d=d):
                nb = nblk // d

                def step(b, dq_part):
                    r, n = b // nb, b % nb
                    cur = pl.ds(n * (HEAD * d) + r, HEAD, stride=d)
                    nxt = pl.ds(jnp.minimum(n + 1, nb - 1) * (HEAD * d) + r, HEAD, stride=d)
                    two = lambda ref: jnp.concatenate([ref[cur, :], ref[nxt, :]], axis=0)
                    two_rows = lambda ref: jnp.concatenate([as_row(ref[cur, :]), as_row(ref[nxt, :])], axis=1)
                    q2 = two(qkv_ref.at[0, 0]).astype(BF16)
                    do2 = two(do_ref).astype(BF16)
                    k = qkv_ref.at[0, 1][cur, :].astype(BF16)
                    v = qkv_ref.at[0, 2][cur, :].astype(BF16)
                    s = _nt(k, q2) * SCALE + jnp.where(n < nb - 1, bias, bias_last)
                    p = jnp.exp(s - two_rows(lse_ref))
                    d_ref.at[0, 2][cur, :] = _nn(p.astype(BF16), do2)
                    dp = _nt(v, do2)
                    ds = (p * (dp - two_rows(dl_ref)) * SCALE).astype(BF16)
                    d_ref.at[0, 1][cur, :] = _nn(ds, q2)
                    dq2 = _tn(ds, k)
                    d_ref.at[0, 0][cur, :] = dq2[:HEAD] + jnp.where(n > 0, dq_part, 0.0)
                    return dq2[HEAD:]

                def steps(i, dq_part):
                    for u in range(UNROLL):
                        dq_part = step(i * UNROLL + u, dq_part)
                    return dq_part

                lax.fori_loop(0, nblk // UNROLL, steps, jnp.zeros((HEAD, HEAD), F32))

    col_blk = pl.BlockSpec((S, HEAD), lambda j, g: (0, j))
    qkv_blk = pl.BlockSpec((1, 3, S, HEAD), lambda j, g: (g, 0, 0, j))
    return pl.pallas_call(
        body, name="attn_bwd", grid=(N_SLOT, 3),
        in_specs=[qkv_blk, col_blk, col_blk, col_blk, pl.BlockSpec((1, 1, HEAD, 2 * HEAD), lambda j, g: (g, j, 0, 0))],
        out_specs=qkv_blk,
        out_shape=jax.ShapeDtypeStruct((3, 3, S, AOW), F32),
        compiler_params=_cparams("parallel", "arbitrary"),
    )(qkv, do, lse, dl, bias_t)


def _bwd_in(dqkv, de, w_int, x, dx1, g_mix, sc1):
    S = x.shape[0]
    tm = TM
    dqkv = dqkv.reshape(3, 3, S, AOW)

    def body(dq_ref, de_ref, wq_ref, wk_ref, wv_ref, wa_ref, wb_ref, x_ref, dx1_ref, g_ref, sc_ref, gx_ref, pv_ref):
        acc = gx_ref
        i, k = pl.program_id(0), pl.program_id(1)

        @pl.when((i == 0) & (k == 0))
        def _():
            pv_ref[...] = jnp.zeros_like(pv_ref)

        @pl.when(k == 0)
        def _():
            acc[...] = jnp.zeros_like(acc)

        @pl.when(k < 3)
        def _():
            lhs = jnp.concatenate([dq_ref[0, t].astype(BF16) for t in range(3)], axis=1)
            acc[...] += _nn(lhs, jnp.concatenate([wq_ref[...], wk_ref[...], wv_ref[...]], axis=0))

        @pl.when(k >= 3)
        def _():
            acc[...] += _nn(de_ref[0], jnp.concatenate([wa_ref[...], wb_ref[...]], axis=0))

        @pl.when(k == 7)
        def _():
            dh = acc[...]
            xv = x_ref[...]
            r = _rms_r(xv)
            g = g_ref[...]
            dxn, pg = _rms_bwd(xv, r, g, dh * (1.0 + sc_ref[...]))
            gx_ref[...] = dx1_ref[...] + dxn
            pv_ref[0:1, :] += _rowsum(dh)
            pv_ref[1:2, :] += _rowsum(dh * (xv * r * g))
            pv_ref[2:3, :] += _rowsum(pg)

    grp = lambda k: jnp.minimum(k, 2)
    chunk = lambda k: jnp.maximum(k - 3, 0)
    wblk = lambda f: pl.BlockSpec((512, D), lambda i, k: (f(k), 0))
    row = pl.BlockSpec((tm, D), lambda i, k: (i, 0))
    once = pl.BlockSpec((tm, D), lambda i, k: (i, 0), pipeline_mode=pl.Buffered(1))
    return pl.pallas_call(
        body, name="bwd_in", grid=(S // tm, 8),
        in_specs=[pl.BlockSpec((1, 3, tm, 512), lambda i, k: (grp(k), 0, i, 0)),
                  pl.BlockSpec((1, tm, D), lambda i, k: (chunk(k), i, 0)),
                  wblk(grp), wblk(lambda k: 3 + grp(k)), wblk(lambda k: 6 + grp(k)),
                  wblk(lambda k: 9 + 2 * chunk(k)), wblk(lambda k: 10 + 2 * chunk(k)),
                  once, once, _vec_spec(), _vec_spec()],
        out_specs=[row, _const_spec((8, D))],
        out_shape=[jax.ShapeDtypeStruct((S, D), F32), jax.ShapeDtypeStruct((8, D), F32)],
        compiler_params=_cparams("arbitrary", "arbitrary"),
    )(dqkv, de, w_int, w_int, w_int, w_int, w_int, x, dx1, g_mix, sc1)


def _grad_w(name, a, b):
    S, ka = a.shape
    nb = b.shape[1]

    def body(a_ref, b_ref, o_ref):
        o_ref[...] = _tn(a_ref[...], b_ref[...]).astype(BF16)

    return pl.pallas_call(
        body, name=name, grid=(ka // 512,),
        in_specs=[pl.BlockSpec((S, 512), lambda n: (0, n)), pl.BlockSpec((S, nb), lambda n: (0, 0))],
        out_specs=pl.BlockSpec((512, nb), lambda n: (n, 0)),
        out_shape=jax.ShapeDtypeStruct((ka, nb), BF16),
        compiler_params=_cparams("parallel"),
    )(a, b)


def _grad_w_small(dya, o_bf, cbu, dyc, merged, dmo):
    S = dya.shape[0]

    def body(dya_ref, o_ref, cbu_ref, dyc_ref, mg_ref, dmo_ref, gba_ref, gbc_ref, gout_ref):
        gba_ref[...] = _tn(dya_ref[...], o_ref[...]).astype(BF16)
        gbc_ref[...] = _tn(cbu_ref[...], dyc_ref[...]).astype(BF16)
        gout_ref[...] = _tn(mg_ref[...], dmo_ref[...]).astype(BF16)

    a_blk = pl.BlockSpec((S, 512), lambda n: (0, n))
    whole = lambda w: pl.BlockSpec((S, w), lambda n: (0, 0))
    out = lambda w: pl.BlockSpec((512, w), lambda n: (n, 0))
    return pl.pallas_call(
        body, name="grad_w_small", grid=(D // 512,),
        in_specs=[a_blk, whole(AOW), a_blk, whole(D), a_blk, whole(D)],
        out_specs=[out(AOW), out(D), out(D)],
        out_shape=[jax.ShapeDtypeStruct((D, AOW), BF16), jax.ShapeDtypeStruct((D, D), BF16), jax.ShapeDtypeStruct((D, D), BF16)],
        compiler_params=_cparams("parallel"),
    )(dya, o_bf, cbu, dyc, merged, dmo)


def _grad_w_in(dqkv, de, h):
    S = h.shape[0]

    def body(dq_ref, de_ref, h_ref, o_ref):
        n = pl.program_id(0)

        @pl.when(n < 9)
        def _():
            o_ref[...] = _tn(dq_ref[0].astype(BF16), h_ref[...]).astype(BF16)

        @pl.when(n >= 9)
        def _():
            o_ref[...] = _tn(de_ref[0], h_ref[...]).astype(BF16)

    def e_idx(n):
        kk = jnp.maximum(n - 9, 0)
        return (kk // 2, 0, kk % 2)

    return pl.pallas_call(
        body, name="grad_w_in", grid=(19,),
        in_specs=[pl.BlockSpec((1, S, 512), lambda n: (jnp.minimum(n, 8), 0, 0)), pl.BlockSpec((1, S, 512), e_idx),
                  pl.BlockSpec((S, D), lambda n: (0, 0))],
        out_specs=pl.BlockSpec((512, D), lambda n: (_win_rowblock(n), 0)),
        out_shape=jax.ShapeDtypeStruct((19 * 512, D), BF16),
        compiler_params=_cparams("parallel"),
    )(dqkv, de, h)


def _local_step(x, h, tgt, mod, g_mix, g_mlp, g_fin, ba, bb, cw8, w_int, mix_weights, mlp_weights, mlp_grads_ready, other_grads_ready):
    S = x.shape[0]
    sh1, sc1, gt1, sh2, sc2, gt2 = [mod[k:k + 1] for k in range(6)]
    bias, bias_t = _bias_table()

    qkv, e = _proj(h, w_int)
    qkv = qkv.reshape(3, 3, S, AOW)
    o_attn, lse = _attn_fwd(qkv, bias)
    w_bat, w_bc, w_out = mix_weights(o_attn)
    o_bf, cbu, ya, yc, merged = _mix(o_attn, e, cw8, ba, bb, w_bat, w_bc)
    x1, mo, h2 = _out_proj(merged, w_out, x, gt1, g_mlp, sc2, sh2)
    w_mit, w_mo = mlp_weights(x1)
    a, f = _mlp_in(h2, w_mit)
    mlp, dx2, pv_f = _mlp_out(f, w_mo, x1, gt2, g_fin, tgt)

    da, dmo2, pv_a = _bwd_mlp_a(dx2, gt2, mlp, w_mo, a)
    dx1, pv_b = _bwd_mlp_b(da, w_mit, x1, dx2, g_mlp, sc2)
    zero = mlp_grads_ready(_grad_w("grad_w_mi", da, h2), _grad_w("grad_w_mo", f, dmo2))
    dmo, dya, dyc, do, dl, de, pv_m = _bwd_mix(dx1, gt1 + zero, mo, e, cw8, ba, bb, ya, yc, o_attn, w_out, w_bc, w_bat)
    dqkv = _attn_bwd(qkv, do, lse, dl, bias_t).reshape(9, S, AOW)
    zero = other_grads_ready(_grad_w_in(dqkv, de, h), *_grad_w_small(dya, o_bf, cbu, dyc, merged, dmo))
    grad_x, pv_i = _bwd_in(dqkv, de, w_int, x, dx1, g_mix, sc1 + zero)

    vec = jnp.concatenate([pv_i[0:2], pv_m[0:1], pv_b[0:2], pv_a[0:1], pv_i[2:3], pv_b[2:3], pv_f[0:1],
                           pv_m[1:3], pv_m[3:6], pv_f[1:2], jnp.zeros((1, D), F32)], axis=0)
    return grad_x, vec


def _my_place():
    return lax.axis_index("x"), lax.axis_index("y"), lax.axis_index("c")


def _dev_index(px, py, pc):
    return 4 * px + 2 * py + pc


def _allgather_weights(shards):
    nw = len(shards)
    HBM = pl.BlockSpec(memory_space=pl.ANY)

    def body(*refs):
        sh, full = refs[:nw], refs[nw:2 * nw]
        send_sems, recv_sems, local_sems = refs[2 * nw:]
        x, y, c = _my_place()
        me, sibling = (x, y, c), (x, y, 1 - c)
        chips = [(1 - x, y), (x, 1 - y), (1 - x, 1 - y)]

        def rows(w, px, py, pc):
            r = sh[w].shape[0]
            return full[w].at[pl.ds(pl.multiple_of(_dev_index(px, py, pc) * r, 16), r), :]

        def copy(w, k, block, to, src=None):
            return pltpu.make_async_remote_copy(
                src_ref=rows(w, *block) if src is None else src, dst_ref=rows(w, *block),
                send_sem=send_sems.at[w, k], recv_sem=recv_sems.at[w, k], device_id=to, device_id_type=MESH)

        mine = [pltpu.make_async_copy(sh[w], rows(w, *me), local_sems.at[w]) for w in range(nw)]
        for cp in mine:
            cp.start()
        first = []
        for w in range(nw):
            first.append(copy(w, 0, me, sibling, src=sh[w]))
            first += [copy(w, 1 + j, me, (*chip, c), src=sh[w]) for j, chip in enumerate(chips)]
        for cp in first:
            cp.start()
        passed = []
        for w in range(nw):
            for j, chip in enumerate(chips):
                copy(w, 1 + j, (*chip, c), me).wait_recv()
                fwd = copy(w, 4 + j, (*chip, c), sibling)
                fwd.start()
                passed.append(fwd)
        for w in range(nw):
            copy(w, 0, sibling, me).wait_recv()
            for j, chip in enumerate(chips):
                copy(w, 4 + j, (*chip, 1 - c), me).wait_recv()
        for cp in first + passed:
            cp.wait_send()
        for cp in mine:
            cp.wait()

    return pl.pallas_call(
        body, name="allgather_weights",
        out_shape=[jax.ShapeDtypeStruct((N_DEV * s.shape[0], s.shape[1]), s.dtype) for s in shards],
        in_specs=[HBM] * nw, out_specs=[HBM] * nw,
        scratch_shapes=[pltpu.SemaphoreType.DMA((nw, 7)), pltpu.SemaphoreType.DMA((nw, 7)), pltpu.SemaphoreType.DMA((nw,))],
    )(*shards)


def _peer(x, y, c, m):
    return (x ^ ((m >> 2) & 1), y ^ ((m >> 1) & 1), c ^ (m & 1))


HBM_SPEC = pl.BlockSpec(memory_space=pltpu.HBM)
SEM_SPEC = pl.BlockSpec(memory_space=pltpu.SEMAPHORE)
N_PEER = N_DEV - 1


SPLIT_MASKS = {"gather": tuple(range(1, N_DEV)), "gather_near": (1, 2, 4, 6), "scatter": tuple(range(1, N_DEV)),
               "chips": (2, 4, 6)}


def _split_copy(mode, src_ref, land_ref, send_sems, recv_sems, w, j, place, arriving=False):
    x, y, c = place
    masks = SPLIT_MASKS[mode]
    peer = _peer(x, y, c, masks[j])
    k = w * len(masks) + j
    sender, receiver = ((peer, (x, y, c)) if arriving else ((x, y, c), peer))
    if mode.startswith("gather"):
        r = src_ref.shape[0]
        src, dst = src_ref, land_ref.at[pl.ds(pl.multiple_of(_dev_index(*sender) * r, 16), r), :]
    elif mode == "scatter":
        r = land_ref.shape[1]
        src, dst = src_ref.at[pl.ds(pl.multiple_of(_dev_index(*receiver) * r, 16), r), :], land_ref.at[j]
    else:
        src, dst = src_ref.at[2 * receiver[0] + receiver[1]], land_ref.at[j]
    return pltpu.make_async_remote_copy(src_ref=src, dst_ref=dst, send_sem=send_sems.at[k], recv_sem=recv_sems.at[k],
                                        device_id=peer, device_id_type=MESH)


def _split_start(name, mode, srcs, lands):
    n = len(srcs)
    nm = len(SPLIT_MASKS[mode])

    def body(*refs):
        src, land = refs[:n], refs[n:2 * n]
        send_sems, recv_sems = refs[2 * n], refs[2 * n + 1]
        token = refs[-1]
        place = _my_place()
        for w in range(n):
            for j in range(nm):
                _split_copy(mode, src[w], land[w], send_sems, recv_sems, w, j, place).start()
        token[...] = jnp.zeros_like(token)

    hbm = lambda t: pltpu.HBM(t.shape, t.dtype)
    out = pl.pallas_call(
        body, name=name,
        out_shape=(pltpu.SemaphoreType.DMA((n * nm,)), pltpu.SemaphoreType.DMA((n * nm,)), *[hbm(t) for t in srcs],
                   *[hbm(t) for t in lands], jax.ShapeDtypeStruct((8, 128), F32)),
        in_specs=(HBM_SPEC,) * (2 * n),
        out_specs=(SEM_SPEC, SEM_SPEC) + (HBM_SPEC,) * (2 * n) + (pl.BlockSpec(memory_space=pltpu.VMEM),),
        input_output_aliases={i: 2 + i for i in range(2 * n)},
        compiler_params=pltpu.CompilerParams(has_side_effects=pltpu.SideEffectType.DATAFLOW_SIDE_EFFECTING),
    )(*[pltpu.with_memory_space_constraint(t, pltpu.HBM) for t in (*srcs, *lands)])
    return out[0], out[1], out[2:2 + n], out[2 + n:2 + 2 * n], out[-1][0:1, 0:1]


def _split_wait(name, mode, send_sems, recv_sems, srcs, lands, after):
    n = len(srcs)

    def body(*refs):
        src, land = refs[:n], refs[n:2 * n]
        ssem, rsem = refs[2 * n], refs[2 * n + 1]
        place = _my_place()
        for w in range(n):
            for j in range(len(SPLIT_MASKS[mode])):
                _split_copy(mode, src[w], land[w], ssem, rsem, w, j, place).wait_send()
                _split_copy(mode, src[w], land[w], ssem, rsem, w, j, place, arriving=True).wait_recv()

    hbm = lambda t: pltpu.HBM(t.shape, t.dtype)
    out = pl.pallas_call(
        body, name=name,
        out_shape=tuple(hbm(t) for t in (*srcs, *lands)),
        in_specs=(HBM_SPEC,) * (2 * n) + (SEM_SPEC, SEM_SPEC, pl.BlockSpec(memory_space=pl.ANY)),
        out_specs=(HBM_SPEC,) * (2 * n),
        input_output_aliases={i: i for i in range(2 * n)},
        compiler_params=pltpu.CompilerParams(has_side_effects=pltpu.SideEffectType.DATAFLOW_SIDE_EFFECTING),
    )(*srcs, *lands, send_sems, recv_sems, after)
    return out[:n], out[n:]


def _forward_copy(zone_ref, send_sems, recv_sems, j, place, arriving=False):
    x, y, c = place
    r = zone_ref.shape[0] // N_DEV
    chip = _peer(x, y, c, SPLIT_MASKS["chips"][j])
    owner = _dev_index(chip[0], chip[1], 1 - c if arriving else c)
    rows = zone_ref.at[pl.ds(pl.multiple_of(owner * r, 16), r), :]
    return pltpu.make_async_remote_copy(src_ref=rows, dst_ref=rows, send_sem=send_sems.at[j], recv_sem=recv_sems.at[j],
                                        device_id=(x, y, 1 - c), device_id_type=MESH)


def _forward_start(name, zone):
    def body(zone_ref, send_sems, recv_sems, zone_thru, token):
        place = _my_place()
        for j in range(3):
            _forward_copy(zone_ref, send_sems, recv_sems, j, place).start()
        token[...] = jnp.zeros_like(token)

    out = pl.pallas_call(
        body, name=name,
        out_shape=(pltpu.SemaphoreType.DMA((3,)), pltpu.SemaphoreType.DMA((3,)), pltpu.HBM(zone.shape, zone.dtype),
                   jax.ShapeDtypeStruct((8, 128), F32)),
        in_specs=(HBM_SPEC,), out_specs=(SEM_SPEC, SEM_SPEC, HBM_SPEC, pl.BlockSpec(memory_space=pltpu.VMEM)),
        input_output_aliases={0: 2},
        compiler_params=pltpu.CompilerParams(has_side_effects=pltpu.SideEffectType.DATAFLOW_SIDE_EFFECTING),
    )(pltpu.with_memory_space_constraint(zone, pltpu.HBM))
    return out[0], out[1], out[2], out[3]


def _forward_wait(name, send_sems, recv_sems, zone, after):
    def body(zone_ref, ssem, rsem, after_ref, zone_out):
        place = _my_place()
        for j in range(3):
            _forward_copy(zone_ref, ssem, rsem, j, place).wait_send()
            _forward_copy(zone_ref, ssem, rsem, j, place, arriving=True).wait_recv()

    return pl.pallas_call(
        body, name=name, out_shape=pltpu.HBM(zone.shape, zone.dtype),
        in_specs=(HBM_SPEC, SEM_SPEC, SEM_SPEC, pl.BlockSpec(memory_space=pl.ANY)), out_specs=HBM_SPEC,
        input_output_aliases={0: 0},
        compiler_params=pltpu.CompilerParams(has_side_effects=pltpu.SideEffectType.DATAFLOW_SIDE_EFFECTING),
    )(zone, send_sems, recv_sems, after)


def _sibling_exchange(grads):
    nw = len(grads)
    HBM = pl.BlockSpec(memory_space=pl.ANY)

    def body(*refs):
        g, land = refs[:nw], refs[nw:2 * nw]
        send_sems, recv_sems = refs[2 * nw:]
        x, y, c = _my_place()

        def copy(w, q, owner_core):
            r = land[w].shape[1]
            return pltpu.make_async_remote_copy(
                src_ref=g[w].at[pl.ds(pl.multiple_of((2 * q + owner_core) * r, 16), r), :], dst_ref=land[w].at[q],
                send_sem=send_sems.at[w, q], recv_sem=recv_sems.at[w, q], device_id=(x, y, 1 - c), device_id_type=MESH)

        sends = [copy(w, q, 1 - c) for w in range(nw) for q in range(4)]
        for cp in sends:
            cp.start()
        for w in range(nw):
            for q in range(4):
                copy(w, q, c).wait_recv()
        for cp in sends:
            cp.wait_send()

    return pl.pallas_call(
        body, name="sibling_exchange",
        out_shape=[jax.ShapeDtypeStruct((4, a.shape[0] // N_DEV, a.shape[1]), a.dtype) for a in grads],
        in_specs=[HBM] * nw, out_specs=[HBM] * nw,
        scratch_shapes=[pltpu.SemaphoreType.DMA((nw, 4)), pltpu.SemaphoreType.DMA((nw, 4))],
    )(*grads)


def _pair_sums(gs, sibs, core):
    n = len(gs)

    def body(core_ref, *refs):
        for w in range(n):
            refs[2 * n + w][0] = (refs[w][0, 0].astype(F32) + refs[n + w][0].astype(F32)).astype(BF16)

    in_specs = [pl.BlockSpec((1, 1) + t.shape[1:], lambda q, core_ref: (q, core_ref[0], 0, 0)) for t in sibs]
    in_specs += [pl.BlockSpec((1,) + t.shape[1:], lambda q, core_ref: (q, 0, 0)) for t in sibs]
    return pl.pallas_call(
        body, name="pair_sums",
        grid_spec=pltpu.PrefetchScalarGridSpec(
            num_scalar_prefetch=1, grid=(4,), in_specs=in_specs,
            out_specs=[pl.BlockSpec((1,) + t.shape[1:], lambda q, core_ref: (q, 0, 0)) for t in sibs]),
        out_shape=[jax.ShapeDtypeStruct(t.shape, BF16) for t in sibs],
        compiler_params=_cparams("parallel"),
    )(core, *[g.reshape(4, 2, t.shape[1], t.shape[2]) for g, t in zip(gs, sibs)], *sibs)


def _allgather_small(v, name):
    r, ccols = v.shape

    def body(v_ref, out_ref, send_sems, recv_sems):
        x, y, c = _my_place()
        my_idx = _dev_index(x, y, c)
        out_ref[my_idx] = v_ref[...]

        def copy(m):
            peer = _peer(x, y, c, m)
            return pltpu.make_async_remote_copy(
                src_ref=v_ref, dst_ref=out_ref.at[my_idx],
                send_sem=send_sems.at[m - 1], recv_sem=recv_sems.at[m - 1], device_id=peer, device_id_type=MESH)

        def arrival(m):
            peer = _peer(x, y, c, m)
            return pltpu.make_async_remote_copy(
                src_ref=v_ref, dst_ref=out_ref.at[_dev_index(*peer)],
                send_sem=send_sems.at[m - 1], recv_sem=recv_sems.at[m - 1], device_id=peer, device_id_type=MESH)

        sends = [copy(m) for m in range(1, N_DEV)]
        for cp in sends:
            cp.start()
        for m in range(1, N_DEV):
            arrival(m).wait_recv()
        for cp in sends:
            cp.wait_send()

    return pl.pallas_call(
        body, name=name,
        out_shape=jax.ShapeDtypeStruct((N_DEV, r, ccols), v.dtype),
        in_specs=[pl.BlockSpec(memory_space=pltpu.VMEM)], out_specs=pl.BlockSpec(memory_space=pltpu.VMEM),
        scratch_shapes=[pltpu.SemaphoreType.DMA((7,)), pltpu.SemaphoreType.DMA((7,))],
    )(v)


def _conditioning(pay, w_ada, b_cols):
    ncol = w_ada.shape[1]

    def body(pay_ref, w_ref, b_ref, got_ref, act_ref, mod_ref, send_sems, recv_sems):
        x, y, c = _my_place()
        my_idx = _dev_index(x, y, c)

        def copy(rnd, buf, m, arriving=False):
            peer = _peer(x, y, c, m)
            slot = _dev_index(*peer) if arriving else my_idx
            return pltpu.make_async_remote_copy(
                src_ref=buf.at[my_idx], dst_ref=buf.at[slot], send_sem=send_sems.at[rnd, m - 1],
                recv_sem=recv_sems.at[rnd, m - 1], device_id=peer, device_id_type=MESH)

        def exchange(rnd, buf):
            sends = [copy(rnd, buf, m) for m in range(1, N_DEV)]
            for cp in sends:
                cp.start()
            for m in range(1, N_DEV):
                copy(rnd, buf, m, arriving=True).wait_recv()
            for cp in sends:
                cp.wait_send()

        got_ref[my_idx] = pay_ref[...]
        exchange(0, got_ref)
        cv = jnp.concatenate([got_ref[s, 0:1, :] for s in range(N_DEV)], axis=0)
        act = cv * _sigmoid(cv)
        act_ref[...] = act
        mod_ref[my_idx] = jnp.dot(act, w_ref[...], preferred_element_type=F32, precision=lax.Precision.HIGHEST) + b_ref[...]
        exchange(1, mod_ref)

    vmem = pl.BlockSpec(memory_space=pltpu.VMEM)
    return pl.pallas_call(
        body, name="conditioning",
        out_shape=[jax.ShapeDtypeStruct((N_DEV, 8, D), F32), jax.ShapeDtypeStruct((N_DEV, D), F32),
                   jax.ShapeDtypeStruct((N_DEV, N_DEV, ncol), F32)],
        in_specs=[vmem] * 3, out_specs=[vmem] * 3,
        scratch_shapes=[pltpu.SemaphoreType.DMA((2, 7)), pltpu.SemaphoreType.DMA((2, 7))],
        compiler_params=_cparams(),
    )(pay, w_ada, b_cols)


def _ada_bwd(act_t, gm_cols):
    def body(a_ref, g_ref, o_ref):
        o_ref[...] = jnp.dot(a_ref[...], g_ref[...], preferred_element_type=F32, precision=lax.Precision.HIGHEST)

    return pl.pallas_call(
        body, name="ada_bwd", out_shape=jax.ShapeDtypeStruct((D, gm_cols.shape[1]), F32), compiler_params=_cparams(),
    )(act_t, gm_cols)


def _row_tile(r):
    for t in (256, 304, 128, 64, 16):
        if r % t == 0:
            return t
    return r


def _sum_parts(parts, name, own=None):
    k, r, ccols = parts.shape
    tr = _row_tile(r)

    def body(*refs):
        p_ref, o_ref = refs[0], refs[-1]
        acc = p_ref[0].astype(F32) if own is None else refs[1][...].astype(F32) + p_ref[0].astype(F32)
        for s in range(1, k):
            acc = acc + p_ref[s].astype(F32)
        o_ref[...] = acc

    blk = pl.BlockSpec((tr, ccols), lambda i: (i, 0))
    return pl.pallas_call(
        body, name=name, grid=(r // tr,),
        in_specs=[pl.BlockSpec((k, tr, ccols), lambda i: (0, i, 0))] + ([] if own is None else [blk]),
        out_specs=blk,
        out_shape=jax.ShapeDtypeStruct((r, ccols), F32),
        compiler_params=_cparams("parallel"),
    )(*((parts,) if own is None else (parts, own)))


def _adamw(w, g, m, v, name):
    r, ccols = w.shape
    tr = _row_tile(r)
    c1 = 1.0 / (1.0 - B1 ** STEP)
    c2 = 1.0 / (1.0 - B2 ** STEP)

    def body(w_ref, g_ref, m_ref, v_ref, d_ref, nm_ref, nv_ref):
        gv = g_ref[...]
        nm = B1 * m_ref[...] + (1.0 - B1) * gv
        nv = B2 * v_ref[...] + (1.0 - B2) * jnp.square(gv)
        nm_ref[...] = nm
        nv_ref[...] = nv
        d_ref[...] = -LR * ((nm * c1) / (jnp.sqrt(nv * c2) + ADAM_EPS) + WD * w_ref[...])

    blk = pl.BlockSpec((tr, ccols), lambda i: (i, 0))
    return pl.pallas_call(
        body, name=name, grid=(r // tr,), in_specs=[blk] * 4, out_specs=[blk] * 3,
        out_shape=[jax.ShapeDtypeStruct((r, ccols), F32)] * 3,
        compiler_params=_cparams("parallel"),
    )(w, g, m, v)


def _sum_adamw(parts, own, w, m, v, name):
    k, r, ccols = parts.shape
    tr = _row_tile(r)
    c1 = 1.0 / (1.0 - B1 ** STEP)
    c2 = 1.0 / (1.0 - B2 ** STEP)

    def body(p_ref, own_ref, w_ref, m_ref, v_ref, g_ref, d_ref, nm_ref, nv_ref):
        gv = own_ref[...].astype(F32)
        for s in range(k):
            gv = gv + p_ref[s].astype(F32)
        g_ref[...] = gv
        nm = B1 * m_ref[...] + (1.0 - B1) * gv
        nv = B2 * v_ref[...] + (1.0 - B2) * jnp.square(gv)
        nm_ref[...] = nm
        nv_ref[...] = nv
        d_ref[...] = -LR * ((nm * c1) / (jnp.sqrt(nv * c2) + ADAM_EPS) + WD * w_ref[...])

    blk = pl.BlockSpec((tr, ccols), lambda i: (i, 0))
    return pl.pallas_call(
        body, name=name, grid=(r // tr,),
        in_specs=[pl.BlockSpec((k, tr, ccols), lambda i: (0, i, 0))] + [blk] * 4, out_specs=[blk] * 4,
        out_shape=[jax.ShapeDtypeStruct((r, ccols), F32)] * 4,
        compiler_params=_cparams("parallel"),
    )(parts, own, w, m, v)


VEC_ROWS = ((0, 6), (6, 7), (9, 11), (11, 14), (7, 8), (8, 9))


def _adamw_vectors(w, g, m, v):
    c1 = 1.0 / (1.0 - B1 ** STEP)
    c2 = 1.0 / (1.0 - B2 ** STEP)

    def put(refs, p):
        for ref, (lo, hi) in zip(refs, VEC_ROWS):
            if ref.shape == (3, HEAD):
                ref[...] = p[lo:hi, :HEAD]
            else:
                ref[...] = jnp.concatenate([p[k:k + 1] for k in range(lo, hi)], axis=1)

    def body(w_ref, g_ref, m_ref, v_ref, *outs):
        gv = g_ref[...]
        nm = B1 * m_ref[...] + (1.0 - B1) * gv
        nv = B2 * v_ref[...] + (1.0 - B2) * jnp.square(gv)
        delta = -LR * ((nm * c1) / (jnp.sqrt(nv * c2) + ADAM_EPS) + WD * w_ref[...])
        for kind, p in enumerate((gv, delta, nm, nv)):
            put(outs[6 * kind:6 * kind + 6], p)

    shapes = [(1, 6 * D), (1, D), (1, 2 * D), (3, HEAD), (1, D), (1, D)]
    out = pl.pallas_call(
        body, name="adamw_vectors", out_shape=[jax.ShapeDtypeStruct(sh, F32) for sh in shapes] * 4, compiler_params=_cparams(),
    )(w, g, m, v)
    fix = lambda t: (t[0], t[1], t[2], t[3][None], t[4], t[5].reshape(D))
    return [fix(out[6 * kind:6 * kind + 6]) for kind in range(4)]


def _pack_vectors(b_ada, g_mix, g_mlp, g_fin, b_gate, conv_w):
    conv_rows = jnp.pad(conv_w.reshape(3, HEAD), ((0, 0), (0, D - HEAD)))
    return jnp.concatenate([b_ada.reshape(6, D), g_mix.reshape(1, D), g_mlp.reshape(1, D), g_fin.reshape(1, D),
                            b_gate.reshape(2, D), conv_rows, jnp.zeros((2, D), F32)], axis=0)


def kernel(x, c, w_ada, b_ada, g_norm_mix, w_in, b_gate, conv_w, w_branch_attn, w_branch_conv, w_out, g_norm_mlp, w_mlp_in, w_mlp_out, g_norm_final, loss_target, m_w_ada, m_b_ada, m_g_norm_mix, m_w_in, m_b_gate, m_conv_w, m_w_branch_attn, m_w_branch_conv, m_w_out, m_g_norm_mlp, m_w_mlp_in, m_w_mlp_out, m_g_norm_final, v_w_ada, v_b_ada, v_g_norm_mix, v_w_in, v_b_gate, v_conv_w, v_w_branch_attn, v_w_branch_conv, v_w_out, v_g_norm_mlp, v_w_mlp_in, v_w_mlp_out, v_g_norm_final):
    S = x.shape[1]
    xi, yi, ci = _my_place()
    me = _dev_index(xi, yi, ci)
    x2 = x.reshape(S, D)
    tgt = loss_target.reshape(S, D)

    pay = jnp.zeros((8, D), F32).at[0].set(c[0]).at[1:4, :HEAD].set(conv_w[0])
    ncol = w_ada.shape[2]
    b_cols = lax.dynamic_slice(b_ada, (0, me * ncol), (1, ncol))
    got, act, mod_all = _conditioning(pay, w_ada[0], b_cols)
    cw8 = jnp.pad(got[:, 1:4, :HEAD].transpose(1, 0, 2).reshape(3, D), ((0, 5), (0, 0)))

    w_in_shard, mod_all = lax.optimization_barrier((w_in[0].T.astype(BF16), mod_all))
    mod = lax.dynamic_index_in_dim(mod_all, me, axis=1, keepdims=False).reshape(6, D)
    (w_int,) = _allgather_weights([w_in_shard])
    late = [w_branch_attn[0].T.astype(BF16), w_branch_conv[0].astype(BF16), w_out[0].astype(BF16),
            w_mlp_in[0].T.astype(BF16), w_mlp_out[0].astype(BF16)]
    w_int, late = lax.optimization_barrier((w_int, late))
    zones = [lax.dynamic_update_slice(lax.empty((N_DEV * t.shape[0], t.shape[1]), BF16), t, (me * t.shape[0], 0)) for t in late]
    ag_mix = _split_start("gather_mix_start", "gather", late[:3], zones[:3])
    ag_mlp = _split_start("gather_mlp_start", "gather", late[3:], zones[3:])
    mod = mod + ag_mix[4] + ag_mlp[4]
    h = _prenorm(x2, g_norm_mix, mod[1:2], mod[0:1])

    def mix_weights(o_attn):
        return _split_wait("gather_mix_wait", "gather", *ag_mix[:4], o_attn)[1]

    def mlp_weights(x1):
        return _split_wait("gather_mlp_wait", "gather", *ag_mlp[:4], x1)[1]

    rs = {}

    def mlp_grads_ready(*grads):
        lands = [lax.empty((N_PEER, t.shape[0] // N_DEV, t.shape[1]), BF16) for t in grads]
        rs["mlp"] = _split_start("scatter_mlp_start", "scatter", grads, lands)
        return rs["mlp"][4]

    def other_grads_ready(*grads):
        core = ci.reshape(1).astype(jnp.int32)
        pair = _pair_sums(grads, _sibling_exchange(grads), core)
        lands = [lax.empty((3,) + t.shape[1:], BF16) for t in pair]
        rs["rest"] = _split_start("scatter_rest_start", "chips", pair, lands)
        return rs["rest"][4]

    ba, bb = b_gate[:, :D], b_gate[:, D:]
    grad_x, vec = _local_step(
        x2, h, tgt, mod, g_norm_mix, g_norm_mlp, g_norm_final.reshape(1, D), ba, bb, cw8, w_int, mix_weights, mlp_weights,
        mlp_grads_ready, other_grads_ready)

    vec_all = _allgather_small(vec, "gather_vec")
    vec_sum = _sum_parts(vec_all, "sum_vec")
    loss = vec_sum[14, 0]
    gm_all = vec_all[:, 0:6, :].reshape(N_DEV, 6 * D)
    gm_cols = lax.dynamic_slice(gm_all, (0, me * ncol), (N_DEV, ncol))
    g_w_ada = _ada_bwd(act.T, gm_cols)
    conv_cols = lax.dynamic_slice(vec_sum[11:14], (0, me * HEAD), (3, HEAD))
    g_pack = jnp.concatenate([vec_sum[0:11], jnp.pad(conv_cols, ((0, 0), (0, D - HEAD))), jnp.zeros((2, D), F32)], axis=0)
    packs = [_pack_vectors(*t) for t in ((b_ada, g_norm_mix, g_norm_mlp, g_norm_final, b_gate, conv_w),
                                         (m_b_ada, m_g_norm_mix, m_g_norm_mlp, m_g_norm_final, m_b_gate, m_conv_w),
                                         (v_b_ada, v_g_norm_mix, v_g_norm_mlp, v_g_norm_final, v_b_gate, v_conv_w))]
    gv, dv, mv, vv = _adamw_vectors(packs[0], g_pack, packs[1], packs[2])
    d_ada, nm_ada, nv_ada = _adamw(w_ada[0], g_w_ada, m_w_ada[0], v_w_ada[0], "adamw_w_ada")

    big = {}
    srcs, lands = _split_wait("scatter_mlp_wait", "scatter", *rs["mlp"][:4], d_ada)
    own = [lax.dynamic_slice(g, (me * land.shape[1], 0), land.shape[1:]) for g, land in zip(srcs, lands)]
    g_mi = _sum_parts(lands[0], "sum_w_mi", own=own[0]).T
    big["w_mi"] = (g_mi[None],) + tuple(t[None] for t in _adamw(w_mlp_in[0], g_mi, m_w_mlp_in[0], v_w_mlp_in[0], "adamw_w_mi"))
    big["w_mo"] = tuple(t[None] for t in _sum_adamw(lands[1], own[1], w_mlp_out[0], m_w_mlp_out[0], v_w_mlp_out[0], "adamw_w_mo"))
    srcs, lands = _split_wait("scatter_rest_wait", "chips", *rs["rest"][:4], big["w_mo"][1])
    own = [lax.dynamic_index_in_dim(pair, 2 * xi + yi, axis=0, keepdims=False) for pair in srcs]
    big["w_in"] = tuple(t.T[None] for t in _sum_adamw(lands[0], own[0], w_in[0].T, m_w_in[0].T, v_w_in[0].T, "adamw_w_in"))
    g_ba = _sum_parts(lands[1], "sum_w_ba", own=own[1]).T
    big["w_ba"] = (g_ba[None],) + tuple(t[None] for t in _adamw(w_branch_attn[0], g_ba, m_w_branch_attn[0], v_w_branch_attn[0], "adamw_w_ba"))
    big["w_bc"] = tuple(t[None] for t in _sum_adamw(lands[2], own[2], w_branch_conv[0], m_w_branch_conv[0], v_w_branch_conv[0], "adamw_w_bc"))
    big["w_out"] = tuple(t[None] for t in _sum_adamw(lands[3], own[3], w_out[0], m_w_out[0], v_w_out[0], "adamw_w_out"))

    def ordered(k, ada, vecs):
        return (ada[None], vecs[0], vecs[1], big["w_in"][k], vecs[2], vecs[3], big["w_ba"][k], big["w_bc"][k],
                big["w_out"][k], vecs[4], big["w_mi"][k], big["w_mo"][k], vecs[5])

    return (loss, grad_x.reshape(1, S, D), *ordered(0, g_w_ada, gv), *ordered(1, d_ada, dv),
            *ordered(2, nm_ada, mv), *ordered(3, nv_ada, vv))
```

```python
import functools

import numpy as np
import jax
import jax.numpy as jnp
from jax import lax
from jax.experimental import pallas as pl
from jax.experimental.pallas import tpu as pltpu

F32, BF16 = jnp.float32, jnp.bfloat16
D = 1024
HEAD = 128
DILATIONS = (1, 4, 16)
N_SLOT = 4
AOW = N_SLOT * HEAD
DFF = 4 * D
N_DEV = 8
UNROLL = 8
EPS = 1e-6
NEG = -1e30
SCALE = HEAD ** -0.5
LR, B1, B2, ADAM_EPS, WD, STEP = 0.001, 0.9, 0.999, 1e-08, 0.01, 10
V7X_VMEM_LIMIT = 56 * 1024 * 1024
TM = 1024
MESH = pl.DeviceIdType.MESH
AXES = ("x", "y", "c")


def _cparams(*sem):
    if sem:
        return pltpu.CompilerParams(dimension_semantics=sem, vmem_limit_bytes=V7X_VMEM_LIMIT)
    return pltpu.CompilerParams(vmem_limit_bytes=V7X_VMEM_LIMIT)


def _nn(a, b):
    return jnp.dot(a, b, preferred_element_type=F32)


def _nt(a, b):
    return lax.dot_general(a, b, (((1,), (1,)), ((), ())), preferred_element_type=F32)


def _tn(a, b):
    return lax.dot_general(a, b, (((0,), (0,)), ((), ())), preferred_element_type=F32)


def _rms_r(x):
    return lax.rsqrt(jnp.mean(x * x, axis=-1, keepdims=True) + EPS)


def _rms_bwd(x, r, g, dn):
    gy = dn * g
    dx = r * gy - x * (r * r * r) * jnp.mean(x * gy, axis=-1, keepdims=True)
    return dx, dn * (x * r)


def _sigmoid(t):
    return 1.0 / (1.0 + jnp.exp(-t))


def _rowsum(v):
    return jnp.sum(v, axis=0, keepdims=True)


def _vec_spec(n=D):
    return pl.BlockSpec((1, n), lambda *_: (0, 0))


def _const_spec(shape):
    nd = len(shape)
    return pl.BlockSpec(shape, lambda *_: (0,) * nd)


def _win_rowblock(j):
    return jnp.where(j < 9, (j % 3) * 3 + j // 3, j)


def _prenorm(x, g, sc, sh):
    S = x.shape[0]
    tm = TM

    def body(x_ref, g_ref, sc_ref, sh_ref, h_ref):
        xv = x_ref[...]
        h_ref[...] = (xv * _rms_r(xv) * g_ref[...] * (1.0 + sc_ref[...]) + sh_ref[...]).astype(BF16)

    row = pl.BlockSpec((tm, D), lambda i: (i, 0))
    return pl.pallas_call(
        body, name="prenorm", grid=(S // tm,), in_specs=[row, _vec_spec(), _vec_spec(), _vec_spec()], out_specs=row,
        out_shape=jax.ShapeDtypeStruct((S, D), BF16), compiler_params=_cparams("parallel"),
    )(x, g, sc, sh)


def _proj(h, w_int):
    S = h.shape[0]

    def body(h_ref, w_ref, q_ref, e_ref):
        j = pl.program_id(0)
        acc = _nt(h_ref[...], w_ref[...])

        @pl.when(j < 9)
        def _():
            q_ref[0] = acc

        @pl.when(j >= 9)
        def _():
            e_ref[0] = acc.astype(BF16)

    def e_idx(j):
        k = jnp.maximum(j - 9, 0)
        return (k // 2, 0, k % 2)

    return pl.pallas_call(
        body, name="proj", grid=(19,),
        in_specs=[pl.BlockSpec((S, D), lambda j: (0, 0), pipeline_mode=pl.Buffered(1)),
                  pl.BlockSpec((512, D), lambda j: (_win_rowblock(j), 0))],
        out_specs=[pl.BlockSpec((1, S, 512), lambda j: (jnp.minimum(j, 8), 0, 0)), pl.BlockSpec((1, S, 512), e_idx)],
        out_shape=[jax.ShapeDtypeStruct((9, S, 512), F32), jax.ShapeDtypeStruct((5, S, D), BF16)],
        compiler_params=_cparams("arbitrary"),
    )(h, w_int)


def _bias_table():
    slopes = (2.0 ** (-8.0 * np.arange(1, 13, dtype=np.float32) / 12.0)).astype(np.float32)
    qi = np.arange(HEAD)[:, None]
    kj = np.arange(2 * HEAD)[None, :]
    delta = HEAD + qi - kj
    mask = (delta >= 0) & (delta <= HEAD)
    out = np.zeros((3, N_SLOT, HEAD, 2 * HEAD), np.float32)
    for gi, d in enumerate(DILATIONS):
        for j in range(N_SLOT):
            bias = -slopes[gi * N_SLOT + j] * (delta * d).astype(np.float32)
            out[gi, j] = np.where(mask, bias, NEG)
    out_t = np.concatenate([out[..., HEAD:].swapaxes(-1, -2), out[..., :HEAD].swapaxes(-1, -2)], axis=-1)
    return jnp.asarray(out), jnp.asarray(out_t)


def _block_rows(b, d):
    r = b % d
    n = b // d
    st = n * (HEAD * d) + r
    stp = jnp.maximum(n - 1, 0) * (HEAD * d) + r
    return n, st, stp


def _attn_fwd(qkv, bias):
    S = qkv.shape[2]
    nblk = S // HEAD
    rows = 256

    def body(qkv_ref, b_ref, o_ref, lse_ref, o_s, lse_s):
        g = pl.program_id(1)
        bias = b_ref[0, 0]
        col = lax.broadcasted_iota(jnp.int32, bias.shape, 1)
        bias_first = jnp.where(col < HEAD, NEG, bias)

        for gi, d in enumerate(DILATIONS):
            @pl.when(g == gi)
            def _(gi=gi, d=d):
                nb = nblk // d

                def keys(start):
                    sl = pl.ds(start, HEAD, stride=d)
                    return qkv_ref.at[0, 1][sl, :].astype(BF16), qkv_ref.at[0, 2][sl, :].astype(BF16)

                def step(b, first_of_residue, before):
                    r, n = b // nb, b % nb
                    cur = pl.ds(n * (HEAD * d) + r, HEAD, stride=d)
                    own = keys(n * (HEAD * d) + r)
                    if first_of_residue:
                        before = own
                    q = qkv_ref.at[0, 0][cur, :].astype(BF16)
                    kw = jnp.concatenate([before[0], own[0]], axis=0)
                    vw = jnp.concatenate([before[1], own[1]], axis=0)
                    s = _nt(q, kw) * SCALE + jnp.where(n > 0, bias, bias_first)
                    m = jnp.max(s, axis=-1, keepdims=True)
                    p = jnp.exp(s - m)
                    l = jnp.sum(p, axis=-1, keepdims=True)
                    o_s.at[gi][cur, :] = _nn(p.astype(BF16), vw) / l
                    lse_s.at[gi][cur, :] = jnp.broadcast_to(m + jnp.log(l), (HEAD, HEAD))
                    return own

                def steps(i, before):
                    for u in range(UNROLL):
                        before = step(i * UNROLL + u, nb <= UNROLL and u % nb == 0, before)
                    return before

                lax.fori_loop(0, nblk // UNROLL, steps, keys(0))

        @pl.when(g == len(DILATIONS) - 1)
        def _():
            def merge(i, carry):
                r = pl.ds(pl.multiple_of(i * rows, rows), rows)
                ls = [lse_s[k, r, :] for k in range(3)]
                top = jnp.maximum(jnp.maximum(ls[0], ls[1]), ls[2])
                ws = [jnp.exp(t - top) for t in ls]
                den = ws[0] + ws[1] + ws[2]
                o_ref[r, :] = (ws[0] * o_s[0, r, :] + ws[1] * o_s[1, r, :] + ws[2] * o_s[2, r, :]) / den
                lse_ref[r, :] = top + jnp.log(den)
                return carry

            lax.fori_loop(0, S // rows, merge, 0)

    return pl.pallas_call(
        body, name="attn_fwd", grid=(N_SLOT, 3),
        in_specs=[pl.BlockSpec((1, 3, S, HEAD), lambda j, g: (g, 0, 0, j)),
                  pl.BlockSpec((1, 1, HEAD, 2 * HEAD), lambda j, g: (g, j, 0, 0))],
        out_specs=[pl.BlockSpec((S, HEAD), lambda j, g: (0, j)), pl.BlockSpec((S, HEAD), lambda j, g: (0, j))],
        out_shape=[jax.ShapeDtypeStruct((S, AOW), F32), jax.ShapeDtypeStruct((S, AOW), F32)],
        scratch_shapes=[pltpu.VMEM((3, S, HEAD), F32)] * 2,
        compiler_params=_cparams("parallel", "arbitrary"),
    )(qkv, bias)


def _shift_down(z, k, halo_rows):
    out = pltpu.roll(z, k, axis=0)
    top = out[:8]
    rid = lax.broadcasted_iota(jnp.int32, top.shape, 0)
    for t in range(k):
        top = jnp.where(rid == t, halo_rows[t], top)
    return jnp.concatenate([top, out[8:]], axis=0)


def _shift_up(z, k, halo_rows):
    n = z.shape[0]
    out = pltpu.roll(z, n - k, axis=0)
    bottom = out[n - 8:]
    rid = lax.broadcasted_iota(jnp.int32, bottom.shape, 0)
    for t in range(k):
        bottom = jnp.where(rid == 8 - k + t, halo_rows[t], bottom)
    return jnp.concatenate([out[:n - 8], bottom], axis=0)


def _e_spec(chunk, tm):
    return pl.BlockSpec((1, tm, D), lambda i, c=chunk: (c, i, 0))


def _e_prev_spec(chunk, tm):
    return pl.BlockSpec((1, 16, D), lambda i, c=chunk: (c, jnp.maximum(i * (tm // 16) - 1, 0), 0))


def _e_next_spec(chunk, tm, S):
    return pl.BlockSpec((1, 16, D), lambda i, c=chunk: (c, jnp.minimum((i + 1) * (tm // 16), S // 16 - 1), 0))


def _mix(o_attn, e, cw8, ba, bb, w_bat, w_bc):
    S = o_attn.shape[0]
    tm = 256

    def body(o_ref, cb_ref, cc_ref, cx_ref, ga_ref, gb_ref, ccp_ref, cxp_ref, cw_ref, ba_ref, bb_ref, wba_ref, wbc_ref,
             obf_ref, cbu_ref, ya_ref, yc_ref, mg_ref):
        i = pl.program_id(0)
        o = o_ref[...].astype(BF16)
        obf_ref[...] = o
        ya = _nt(o, wba_ref[...])
        z = cc_ref[0].astype(F32) * cx_ref[0].astype(F32)
        zp = ccp_ref[0].astype(F32) * cxp_ref[0].astype(F32) * (i > 0).astype(F32)
        z1 = _shift_down(z, 1, [zp[15:16]])
        z2 = _shift_down(z, 2, [zp[14:15], zp[15:16]])
        cw = cw_ref[...]
        u = cw[0:1] * z2 + cw[1:2] * z1 + cw[2:3] * z
        cbu = (cb_ref[0].astype(F32) * u).astype(BF16)
        cbu_ref[...] = cbu
        yc = _nn(cbu, wbc_ref[...])
        sa = _sigmoid(ga_ref[0].astype(F32) + ba_ref[...])
        sb = _sigmoid(gb_ref[0].astype(F32) + bb_ref[...])
        ya_ref[...] = ya.astype(BF16)
        yc_ref[...] = yc.astype(BF16)
        mg_ref[...] = (sa * ya + sb * yc).astype(BF16)

    row = lambda w: pl.BlockSpec((tm, w), lambda i: (i, 0))
    return pl.pallas_call(
        body, name="mix", grid=(S // tm,),
        in_specs=[row(AOW)] + [_e_spec(c, tm) for c in range(5)] + [_e_prev_spec(1, tm), _e_prev_spec(2, tm),
                  _const_spec((8, D)), _vec_spec(), _vec_spec(), _const_spec((D, AOW)), _const_spec((D, D))],
        out_specs=[row(AOW), row(D), row(D), row(D), row(D)],
        out_shape=[jax.ShapeDtypeStruct((S, AOW), BF16)] + [jax.ShapeDtypeStruct((S, D), BF16)] * 4,
        compiler_params=_cparams("parallel"),
    )(o_attn, e, e, e, e, e, e, e, cw8, ba, bb, w_bat, w_bc)


def _out_proj(merged, w_out, x, gate1, g_mlp, sc2, sh2):
    S = x.shape[0]
    tm = TM

    def body(mg_ref, w_ref, x_ref, gt_ref, g_ref, sc_ref, sh_ref, x1_ref, mo_ref, h2_ref):
        mo = _nn(mg_ref[...], w_ref[...])
        mo_ref[...] = mo.astype(BF16)
        x1 = x_ref[...] + gt_ref[...] * mo
        x1_ref[...] = x1
        h2 = x1 * _rms_r(x1) * g_ref[...] * (1.0 + sc_ref[...]) + sh_ref[...]
        h2_ref[...] = h2.astype(BF16)

    row = pl.BlockSpec((tm, D), lambda i: (i, 0))
    return pl.pallas_call(
        body, name="out_proj", grid=(S // tm,),
        in_specs=[row, _const_spec((D, D)), row, _vec_spec(), _vec_spec(), _vec_spec(), _vec_spec()],
        out_specs=[row, row, row],
        out_shape=[jax.ShapeDtypeStruct((S, D), F32), jax.ShapeDtypeStruct((S, D), BF16), jax.ShapeDtypeStruct((S, D), BF16)],
        compiler_params=_cparams("parallel"),
    )(merged, w_out, x, gate1, g_mlp, sc2, sh2)


def _mlp_in(h2, w_mit):
    S = h2.shape[0]
    tm, tn = TM, 2048

    def body(h_ref, w_ref, a_ref, f_ref):
        a = _nt(h_ref[...], w_ref[...])
        a_ref[...] = a.astype(BF16)
        f_ref[...] = jnp.square(jnp.maximum(a, 0.0)).astype(BF16)

    blk = pl.BlockSpec((tm, tn), lambda i, j: (i, j))
    return pl.pallas_call(
        body, name="mlp_in", grid=(S // tm, DFF // tn),
        in_specs=[pl.BlockSpec((tm, D), lambda i, j: (i, 0)), pl.BlockSpec((tn, D), lambda i, j: (j, 0))],
        out_specs=[blk, blk],
        out_shape=[jax.ShapeDtypeStruct((S, DFF), BF16)] * 2,
        compiler_params=_cparams("parallel", "parallel"),
    )(h2, w_mit)


def _mlp_out(f, w_mo, x1, gate2, g_fin, tgt):
    S = x1.shape[0]
    tm = 512
    half = tm // 2

    def body(f_ref, w_ref, x1_ref, gt_ref, g_ref, t_ref, mlp_ref, dx2_ref, pv_ref):
        @pl.when(pl.program_id(0) == 0)
        def _():
            pv_ref[...] = jnp.zeros_like(pv_ref)

        g = g_ref[...]
        for hs in (pl.ds(0, half), pl.ds(half, half)):
            mlp = _nn(f_ref[hs, :], w_ref[...])
            mlp_ref[hs, :] = mlp.astype(BF16)
            x2 = x1_ref[hs, :] + gt_ref[...] * mlp
            r = _rms_r(x2)
            err = x2 * r * g - t_ref[hs, :]
            dx2, pg = _rms_bwd(x2, r, g, err * (1.0 / D))
            dx2_ref[hs, :] = dx2
            pv_ref[0:1, :] += _rowsum(pg)
            pv_ref[1:2, :] += 0.5 * _rowsum(jnp.mean(err * err, axis=-1, keepdims=True))

    row = pl.BlockSpec((tm, D), lambda i: (i, 0))
    return pl.pallas_call(
        body, name="mlp_out", grid=(S // tm,),
        in_specs=[pl.BlockSpec((tm, DFF), lambda i: (i, 0)), _const_spec((DFF, D)), row, _vec_spec(), _vec_spec(), row],
        out_specs=[row, row, _const_spec((8, D))],
        out_shape=[jax.ShapeDtypeStruct((S, D), BF16), jax.ShapeDtypeStruct((S, D), F32), jax.ShapeDtypeStruct((8, D), F32)],
        compiler_params=_cparams("arbitrary"),
    )(f, w_mo, x1, gate2, g_fin, tgt)


def _bwd_mlp_a(dx2, gate2, mlp, w_mo, a):
    S = dx2.shape[0]
    tm = 512
    half = tm // 2

    def body(dx_ref, gt_ref, mlp_ref, w_ref, a_ref, da_ref, dmo_ref, pv_ref):
        @pl.when(pl.program_id(0) == 0)
        def _():
            pv_ref[...] = jnp.zeros_like(pv_ref)

        for hs in (pl.ds(0, half), pl.ds(half, half)):
            dx = dx_ref[hs, :]
            dmo = (dx * gt_ref[...]).astype(BF16)
            dmo_ref[hs, :] = dmo
            pv_ref[0:1, :] += _rowsum(dx * mlp_ref[hs, :].astype(F32))
            df = _nt(dmo, w_ref[...])
            da_ref[hs, :] = (df * (2.0 * jnp.maximum(a_ref[hs, :].astype(F32), 0.0))).astype(BF16)

    row = pl.BlockSpec((tm, D), lambda i: (i, 0))
    wide = pl.BlockSpec((tm, DFF), lambda i: (i, 0))
    return pl.pallas_call(
        body, name="bwd_mlp_a", grid=(S // tm,),
        in_specs=[row, _vec_spec(), row, _const_spec((DFF, D)), wide],
        out_specs=[wide, row, _const_spec((8, D))],
        out_shape=[jax.ShapeDtypeStruct((S, DFF), BF16), jax.ShapeDtypeStruct((S, D), BF16), jax.ShapeDtypeStruct((8, D), F32)],
        compiler_params=_cparams("arbitrary"),
    )(dx2, gate2, mlp, w_mo, a)


def _bwd_mlp_b(da, w_mit, x1, dx2, g_mlp, sc2):
    S = x1.shape[0]
    tm = 512
    half = tm // 2

    def body(da_ref, w_ref, x1_ref, dx2_ref, g_ref, sc_ref, dx1_ref, pv_ref):
        @pl.when(pl.program_id(0) == 0)
        def _():
            pv_ref[...] = jnp.zeros_like(pv_ref)

        g = g_ref[...]
        for hs in (pl.ds(0, half), pl.ds(half, half)):
            dh = _nn(da_ref[hs, :], w_ref[...])
            x1 = x1_ref[hs, :]
            r = _rms_r(x1)
            dxn, pg = _rms_bwd(x1, r, g, dh * (1.0 + sc_ref[...]))
            dx1_ref[hs, :] = dx2_ref[hs, :] + dxn
            pv_ref[0:1, :] += _rowsum(dh)
            pv_ref[1:2, :] += _rowsum(dh * (x1 * r * g))
            pv_ref[2:3, :] += _rowsum(pg)

    row = pl.BlockSpec((tm, D), lambda i: (i, 0))
    return pl.pallas_call(
        body, name="bwd_mlp_b", grid=(S // tm,),
        in_specs=[pl.BlockSpec((tm, DFF), lambda i: (i, 0)), _const_spec((DFF, D)), row, row, _vec_spec(), _vec_spec()],
        out_specs=[row, _const_spec((8, D))],
        out_shape=[jax.ShapeDtypeStruct((S, D), F32), jax.ShapeDtypeStruct((8, D), F32)],
        compiler_params=_cparams("arbitrary"),
    )(da, w_mit, x1, dx2, g_mlp, sc2)


def _bwd_mix(dx1, gate1, mo, e, cw8, ba, bb, ya, yc, o_attn, w_out, w_bc, w_bat):
    S = dx1.shape[0]
    tm = 256
    n_tiles = S // tm

    def body(dx_ref, dxn_ref, gt_ref, mo_ref, cb_ref, cc_ref, cx_ref, ga_ref, gb_ref, cbn_ref, gbn_ref, ccp_ref, cxp_ref,
             cw_ref, ba_ref, bb_ref, ya_ref, yc_ref, o_ref, wout_ref, wbc_ref, wba_ref,
             dmo_ref, dya_ref, dyc_ref, do_ref, dl_ref, de_ref, pv_ref):
        i = pl.program_id(0)

        @pl.when(i == 0)
        def _():
            pv_ref[...] = jnp.zeros_like(pv_ref)

        gate = gt_ref[...]
        bbv = bb_ref[...]

        def conv_branch_grad(dx_rows, gb_rows):
            dmo = (dx_rows * gate).astype(BF16)
            dmg = _nt(dmo, wout_ref[...])
            sb = _sigmoid(gb_rows + bbv)
            dyc = dmg * sb
            return dmo, dmg, sb, dyc, _nt(dyc.astype(BF16), wbc_ref[...])

        dx = dx_ref[...]
        cb = cb_ref[0].astype(F32)
        cc = cc_ref[0].astype(F32)
        cx = cx_ref[0].astype(F32)
        dmo, dmg, sb, dyc, dcbu = conv_branch_grad(dx, gb_ref[0].astype(F32))
        dmo_ref[...] = dmo
        pv_ref[0:1, :] += _rowsum(dx * mo_ref[...].astype(F32))
        sa = _sigmoid(ga_ref[0].astype(F32) + ba_ref[...])
        dya = (dmg * sa).astype(BF16)
        dya_ref[...] = dya
        dyc_ref[...] = dyc.astype(BF16)
        dga = dmg * ya_ref[...].astype(F32) * sa * (1.0 - sa)
        dgb = dmg * yc_ref[...].astype(F32) * sb * (1.0 - sb)
        pv_ref[1:2, :] += _rowsum(dga)
        pv_ref[2:3, :] += _rowsum(dgb)

        do = _nn(dya, wba_ref[...])
        do_ref[...] = do
        prod = do * o_ref[...]
        dl_ref[...] = jnp.concatenate(
            [jnp.broadcast_to(jnp.sum(prod[:, s * HEAD:(s + 1) * HEAD], axis=-1, keepdims=True), (tm, HEAD))
             for s in range(N_SLOT)], axis=1)

        z = cc * cx
        zp = ccp_ref[0].astype(F32) * cxp_ref[0].astype(F32) * (i > 0).astype(F32)
        z1 = _shift_down(z, 1, [zp[15:16]])
        z2 = _shift_down(z, 2, [zp[14:15], zp[15:16]])
        cw = cw_ref[...]
        u = cw[0:1] * z2 + cw[1:2] * z1 + cw[2:3] * z
        du = dcbu * cb
        dcbu_n = conv_branch_grad(dxn_ref[...], gbn_ref[0].astype(F32))[4]
        du_n = dcbu_n * cbn_ref[0].astype(F32) * (i < n_tiles - 1).astype(F32)
        du1 = _shift_up(du, 1, [du_n[0:1]])
        du2 = _shift_up(du, 2, [du_n[0:1], du_n[1:2]])
        dz = cw[2:3] * du + cw[1:2] * du1 + cw[0:1] * du2
        pv_ref[3:4, :] += _rowsum(du * z2)
        pv_ref[4:5, :] += _rowsum(du * z1)
        pv_ref[5:6, :] += _rowsum(du * z)

        de_ref[0] = (dcbu * u).astype(BF16)
        de_ref[1] = (dz * cx).astype(BF16)
        de_ref[2] = (dz * cc).astype(BF16)
        de_ref[3] = dga.astype(BF16)
        de_ref[4] = dgb.astype(BF16)

    row = lambda w: pl.BlockSpec((tm, w), lambda i: (i, 0))
    nxt = pl.BlockSpec((16, D), lambda i: (jnp.minimum((i + 1) * (tm // 16), S // 16 - 1), 0))
    return pl.pallas_call(
        body, name="bwd_mix", grid=(n_tiles,),
        in_specs=[row(D), nxt, _vec_spec(), row(D)] + [_e_spec(c, tm) for c in range(5)]
                 + [_e_next_spec(0, tm, S), _e_next_spec(4, tm, S), _e_prev_spec(1, tm), _e_prev_spec(2, tm),
                    _const_spec((8, D)), _vec_spec(), _vec_spec(), row(D), row(D), row(AOW),
                    _const_spec((D, D)), _const_spec((D, D)), _const_spec((D, AOW))],
        out_specs=[row(D), row(D), row(D), row(AOW), row(AOW), pl.BlockSpec((5, tm, D), lambda i: (0, i, 0)),
                   _const_spec((8, D))],
        out_shape=[jax.ShapeDtypeStruct((S, D), BF16)] * 3 + [jax.ShapeDtypeStruct((S, AOW), F32)] * 2
                  + [jax.ShapeDtypeStruct((5, S, D), BF16), jax.ShapeDtypeStruct((8, D), F32)],
        compiler_params=_cparams("arbitrary"),
    )(dx1, dx1, gate1, mo, e, e, e, e, e, e, e, e, e, cw8, ba, bb, ya, yc, o_attn, w_out, w_bc, w_bat)


def _attn_bwd(qkv, do, lse, dl, bias_t):
    S = qkv.shape[2]
    nblk = S // HEAD

    def body(qkv_ref, do_ref, lse_ref, dl_ref, b_ref, d_ref):
        g = pl.program_id(1)
        bias = b_ref[0, 0]
        col = lax.broadcasted_iota(jnp.int32, bias.shape, 1)
        bias_last = jnp.where(col >= HEAD, NEG, bias)
        eye = (lax.broadcasted_iota(jnp.int32, (HEAD, HEAD), 0) == lax.broadcasted_iota(jnp.int32, (HEAD, HEAD), 1)).astype(F32)

        def as_row(t):
            return jnp.sum(t * eye, axis=0, keepdims=True)

        for gi, d in enumerate(DILATIONS):
            @pl.when(g == gi)
            def _(d=d):
                nb = nblk // d

                def query_side(start):
                    sl = pl.ds(start, HEAD, stride=d)
                    return (qkv_ref.at[0, 0][sl, :].astype(BF16), do_ref[sl, :].astype(BF16),
                            as_row(lse_ref[sl, :]), as_row(dl_ref[sl, :]))

                def step(b, first_of_residue, carry):
                    dq_part, own = carry
                    r, n = b // nb, b % nb
                    cur = pl.ds(n * (HEAD * d) + r, HEAD, stride=d)
                    if first_of_residue:
                        own = query_side(r)
                    nxt = query_side(jnp.minimum(n + 1, nb - 1) * (HEAD * d) + r)
                    q2 = jnp.concatenate([own[0], nxt[0]], axis=0)
                    do2 = jnp.concatenate([own[1], nxt[1]], axis=0)
                    k = qkv_ref.at[0, 1][cur, :].astype(BF16)
                    v = qkv_ref.at[0, 2][cur, :].astype(BF16)
                    s = _nt(k, q2) * SCALE + jnp.where(n < nb - 1, bias, bias_last)
                    p = jnp.exp(s - jnp.concatenate([own[2], nxt[2]], axis=1))
                    d_ref.at[0, 2][cur, :] = _nn(p.astype(BF16), do2)
                    dp = _nt(v, do2)
                    ds = (p * (dp - jnp.concatenate([own[3], nxt[3]], axis=1)) * SCALE).astype(BF16)
                    d_ref.at[0, 1][cur, :] = _nn(ds, q2)
                    dq2 = _tn(ds, k)
                    d_ref.at[0, 0][cur, :] = dq2[:HEAD] + jnp.where(n > 0, dq_part, 0.0)
                    return dq2[HEAD:], nxt

                def steps(i, carry):
                    for u in range(UNROLL):
                        carry = step(i * UNROLL + u, nb <= UNROLL and u % nb == 0, carry)
                    return carry

                lax.fori_loop(0, nblk // UNROLL, steps, (jnp.zeros((HEAD, HEAD), F32), query_side(0)))

    col_blk = pl.BlockSpec((S, HEAD), lambda j, g: (0, j))
    qkv_blk = pl.BlockSpec((1, 3, S, HEAD), lambda j, g: (g, 0, 0, j))
    return pl.pallas_call(
        body, name="attn_bwd", grid=(N_SLOT, 3),
        in_specs=[qkv_blk, col_blk, col_blk, col_blk, pl.BlockSpec((1, 1, HEAD, 2 * HEAD), lambda j, g: (g, j, 0, 0))],
        out_specs=qkv_blk,
        out_shape=jax.ShapeDtypeStruct((3, 3, S, AOW), F32),
        compiler_params=_cparams("parallel", "arbitrary"),
    )(qkv, do, lse, dl, bias_t)


def _bwd_in(dqkv, de, w_int, x, dx1, g_mix, sc1):
    S = x.shape[0]
    tm = TM
    dqkv = dqkv.reshape(3, 3, S, AOW)

    def body(dq_ref, de_ref, wq_ref, wk_ref, wv_ref, wa_ref, wb_ref, x_ref, dx1_ref, g_ref, sc_ref, gx_ref, pv_ref):
        acc = gx_ref
        i, k = pl.program_id(0), pl.program_id(1)

        @pl.when((i == 0) & (k == 0))
        def _():
            pv_ref[...] = jnp.zeros_like(pv_ref)

        @pl.when(k == 0)
        def _():
            acc[...] = jnp.zeros_like(acc)

        @pl.when(k < 3)
        def _():
            lhs = jnp.concatenate([dq_ref[0, t].astype(BF16) for t in range(3)], axis=1)
            acc[...] += _nn(lhs, jnp.concatenate([wq_ref[...], wk_ref[...], wv_ref[...]], axis=0))

        @pl.when(k >= 3)
        def _():
            acc[...] += _nn(de_ref[0], jnp.concatenate([wa_ref[...], wb_ref[...]], axis=0))

        @pl.when(k == 7)
        def _():
            dh = acc[...]
            xv = x_ref[...]
            r = _rms_r(xv)
            g = g_ref[...]
            dxn, pg = _rms_bwd(xv, r, g, dh * (1.0 + sc_ref[...]))
            gx_ref[...] = dx1_ref[...] + dxn
            pv_ref[0:1, :] += _rowsum(dh)
            pv_ref[1:2, :] += _rowsum(dh * (xv * r * g))
            pv_ref[2:3, :] += _rowsum(pg)

    grp = lambda k: jnp.minimum(k, 2)
    chunk = lambda k: jnp.maximum(k - 3, 0)
    wblk = lambda f: pl.BlockSpec((512, D), lambda i, k: (f(k), 0))
    row = pl.BlockSpec((tm, D), lambda i, k: (i, 0))
    once = pl.BlockSpec((tm, D), lambda i, k: (i, 0), pipeline_mode=pl.Buffered(1))
    return pl.pallas_call(
        body, name="bwd_in", grid=(S // tm, 8),
        in_specs=[pl.BlockSpec((1, 3, tm, 512), lambda i, k: (grp(k), 0, i, 0)),
                  pl.BlockSpec((1, tm, D), lambda i, k: (chunk(k), i, 0)),
                  wblk(grp), wblk(lambda k: 3 + grp(k)), wblk(lambda k: 6 + grp(k)),
                  wblk(lambda k: 9 + 2 * chunk(k)), wblk(lambda k: 10 + 2 * chunk(k)),
                  once, once, _vec_spec(), _vec_spec()],
        out_specs=[row, _const_spec((8, D))],
        out_shape=[jax.ShapeDtypeStruct((S, D), F32), jax.ShapeDtypeStruct((8, D), F32)],
        compiler_params=_cparams("arbitrary", "arbitrary"),
    )(dqkv, de, w_int, w_int, w_int, w_int, w_int, x, dx1, g_mix, sc1)


def _grad_w(name, a, b):
    S, ka = a.shape
    nb = b.shape[1]

    def body(a_ref, b_ref, o_ref):
        o_ref[...] = _tn(a_ref[...], b_ref[...]).astype(BF16)

    return pl.pallas_call(
        body, name=name, grid=(ka // 512,),
        in_specs=[pl.BlockSpec((S, 512), lambda n: (0, n)), pl.BlockSpec((S, nb), lambda n: (0, 0))],
        out_specs=pl.BlockSpec((512, nb), lambda n: (n, 0)),
        out_shape=jax.ShapeDtypeStruct((ka, nb), BF16),
        compiler_params=_cparams("parallel"),
    )(a, b)


def _grad_w_small(dya, o_bf, cbu, dyc, merged, dmo):
    S = dya.shape[0]

    def body(dya_ref, o_ref, cbu_ref, dyc_ref, mg_ref, dmo_ref, gba_ref, gbc_ref, gout_ref):
        gba_ref[...] = _tn(dya_ref[...], o_ref[...]).astype(BF16)
        gbc_ref[...] = _tn(cbu_ref[...], dyc_ref[...]).astype(BF16)
        gout_ref[...] = _tn(mg_ref[...], dmo_ref[...]).astype(BF16)

    a_blk = pl.BlockSpec((S, 512), lambda n: (0, n))
    whole = lambda w: pl.BlockSpec((S, w), lambda n: (0, 0))
    out = lambda w: pl.BlockSpec((512, w), lambda n: (n, 0))
    return pl.pallas_call(
        body, name="grad_w_small", grid=(D // 512,),
        in_specs=[a_blk, whole(AOW), a_blk, whole(D), a_blk, whole(D)],
        out_specs=[out(AOW), out(D), out(D)],
        out_shape=[jax.ShapeDtypeStruct((D, AOW), BF16), jax.ShapeDtypeStruct((D, D), BF16), jax.ShapeDtypeStruct((D, D), BF16)],
        compiler_params=_cparams("parallel"),
    )(dya, o_bf, cbu, dyc, merged, dmo)


def _grad_w_in(dqkv, de, h):
    S = h.shape[0]

    def body(dq_ref, de_ref, h_ref, o_ref):
        n = pl.program_id(0)

        @pl.when(n < 9)
        def _():
            o_ref[...] = _tn(dq_ref[0].astype(BF16), h_ref[...]).astype(BF16)

        @pl.when(n >= 9)
        def _():
            o_ref[...] = _tn(de_ref[0], h_ref[...]).astype(BF16)

    def e_idx(n):
        kk = jnp.maximum(n - 9, 0)
        return (kk // 2, 0, kk % 2)

    return pl.pallas_call(
        body, name="grad_w_in", grid=(19,),
        in_specs=[pl.BlockSpec((1, S, 512), lambda n: (jnp.minimum(n, 8), 0, 0)), pl.BlockSpec((1, S, 512), e_idx),
                  pl.BlockSpec((S, D), lambda n: (0, 0))],
        out_specs=pl.BlockSpec((512, D), lambda n: (_win_rowblock(n), 0)),
        out_shape=jax.ShapeDtypeStruct((19 * 512, D), BF16),
        compiler_params=_cparams("parallel"),
    )(dqkv, de, h)


def _local_step(x, h, tgt, mod, g_mix, g_mlp, g_fin, ba, bb, cw8, w_int, mix_weights, mlp_weights, mlp_grads_ready, other_grads_ready):
    S = x.shape[0]
    sh1, sc1, gt1, sh2, sc2, gt2 = [mod[k:k + 1] for k in range(6)]
    bias, bias_t = _bias_table()

    qkv, e = _proj(h, w_int)
    qkv = qkv.reshape(3, 3, S, AOW)
    o_attn, lse = _attn_fwd(qkv, bias)
    w_bat, w_bc, w_out = mix_weights(o_attn)
    o_bf, cbu, ya, yc, merged = _mix(o_attn, e, cw8, ba, bb, w_bat, w_bc)
    x1, mo, h2 = _out_proj(merged, w_out, x, gt1, g_mlp, sc2, sh2)
    w_mit, w_mo = mlp_weights(x1)
    a, f = _mlp_in(h2, w_mit)
    mlp, dx2, pv_f = _mlp_out(f, w_mo, x1, gt2, g_fin, tgt)

    da, dmo2, pv_a = _bwd_mlp_a(dx2, gt2, mlp, w_mo, a)
    dx1, pv_b = _bwd_mlp_b(da, w_mit, x1, dx2, g_mlp, sc2)
    zero = mlp_grads_ready(_grad_w("grad_w_mi", da, h2), _grad_w("grad_w_mo", f, dmo2))
    dmo, dya, dyc, do, dl, de, pv_m = _bwd_mix(dx1, gt1 + zero, mo, e, cw8, ba, bb, ya, yc, o_attn, w_out, w_bc, w_bat)
    dqkv = _attn_bwd(qkv, do, lse, dl, bias_t).reshape(9, S, AOW)
    zero = other_grads_ready(_grad_w_in(dqkv, de, h), *_grad_w_small(dya, o_bf, cbu, dyc, merged, dmo))
    grad_x, pv_i = _bwd_in(dqkv, de, w_int, x, dx1, g_mix, sc1 + zero)

    vec = jnp.concatenate([pv_i[0:2], pv_m[0:1], pv_b[0:2], pv_a[0:1], pv_i[2:3], pv_b[2:3], pv_f[0:1],
                           pv_m[1:3], pv_m[3:6], pv_f[1:2], jnp.zeros((1, D), F32)], axis=0)
    return grad_x, vec


def _my_place():
    return lax.axis_index("x"), lax.axis_index("y"), lax.axis_index("c")


def _dev_index(px, py, pc):
    return 4 * px + 2 * py + pc


def _allgather_weights(shards):
    nw = len(shards)
    HBM = pl.BlockSpec(memory_space=pl.ANY)

    def body(*refs):
        sh, full = refs[:nw], refs[nw:2 * nw]
        send_sems, recv_sems, local_sems = refs[2 * nw:]
        x, y, c = _my_place()
        me, sibling = (x, y, c), (x, y, 1 - c)
        chips = [(1 - x, y), (x, 1 - y), (1 - x, 1 - y)]

        def rows(w, px, py, pc):
            r = sh[w].shape[0]
            return full[w].at[pl.ds(pl.multiple_of(_dev_index(px, py, pc) * r, 16), r), :]

        def copy(w, k, block, to, src=None):
            return pltpu.make_async_remote_copy(
                src_ref=rows(w, *block) if src is None else src, dst_ref=rows(w, *block),
                send_sem=send_sems.at[w, k], recv_sem=recv_sems.at[w, k], device_id=to, device_id_type=MESH)

        mine = [pltpu.make_async_copy(sh[w], rows(w, *me), local_sems.at[w]) for w in range(nw)]
        for cp in mine:
            cp.start()
        first = []
        for w in range(nw):
            first.append(copy(w, 0, me, sibling, src=sh[w]))
            first += [copy(w, 1 + j, me, (*chip, c), src=sh[w]) for j, chip in enumerate(chips)]
        for cp in first:
            cp.start()
        passed = []
        for w in range(nw):
            for j, chip in enumerate(chips):
                copy(w, 1 + j, (*chip, c), me).wait_recv()
                fwd = copy(w, 4 + j, (*chip, c), sibling)
                fwd.start()
                passed.append(fwd)
        for w in range(nw):
            copy(w, 0, sibling, me).wait_recv()
            for j, chip in enumerate(chips):
                copy(w, 4 + j, (*chip, 1 - c), me).wait_recv()
        for cp in first + passed:
            cp.wait_send()
        for cp in mine:
            cp.wait()

    return pl.pallas_call(
        body, name="allgather_weights",
        out_shape=[jax.ShapeDtypeStruct((N_DEV * s.shape[0], s.shape[1]), s.dtype) for s in shards],
        in_specs=[HBM] * nw, out_specs=[HBM] * nw,
        scratch_shapes=[pltpu.SemaphoreType.DMA((nw, 7)), pltpu.SemaphoreType.DMA((nw, 7)), pltpu.SemaphoreType.DMA((nw,))],
    )(*shards)


def _peer(x, y, c, m):
    return (x ^ ((m >> 2) & 1), y ^ ((m >> 1) & 1), c ^ (m & 1))


HBM_SPEC = pl.BlockSpec(memory_space=pltpu.HBM)
SEM_SPEC = pl.BlockSpec(memory_space=pltpu.SEMAPHORE)
N_PEER = N_DEV - 1


SPLIT_MASKS = {"gather": tuple(range(1, N_DEV)), "gather_near": (1, 2, 4, 6), "scatter": tuple(range(1, N_DEV)),
               "chips": (2, 4, 6)}


def _split_copy(mode, src_ref, land_ref, send_sems, recv_sems, w, j, place, arriving=False):
    x, y, c = place
    masks = SPLIT_MASKS[mode]
    peer = _peer(x, y, c, masks[j])
    k = w * len(masks) + j
    sender, receiver = ((peer, (x, y, c)) if arriving else ((x, y, c), peer))
    if mode.startswith("gather"):
        r = src_ref.shape[0]
        src, dst = src_ref, land_ref.at[pl.ds(pl.multiple_of(_dev_index(*sender) * r, 16), r), :]
    elif mode == "scatter":
        r = land_ref.shape[1]
        src, dst = src_ref.at[pl.ds(pl.multiple_of(_dev_index(*receiver) * r, 16), r), :], land_ref.at[j]
    else:
        src, dst = src_ref.at[2 * receiver[0] + receiver[1]], land_ref.at[j]
    return pltpu.make_async_remote_copy(src_ref=src, dst_ref=dst, send_sem=send_sems.at[k], recv_sem=recv_sems.at[k],
                                        device_id=peer, device_id_type=MESH)


def _split_start(name, mode, srcs, lands):
    n = len(srcs)
    nm = len(SPLIT_MASKS[mode])

    def body(*refs):
        src, land = refs[:n], refs[n:2 * n]
        send_sems, recv_sems = refs[2 * n], refs[2 * n + 1]
        token = refs[-1]
        place = _my_place()
        for w in range(n):
            for j in range(nm):
                _split_copy(mode, src[w], land[w], send_sems, recv_sems, w, j, place).start()
        token[...] = jnp.zeros_like(token)

    hbm = lambda t: pltpu.HBM(t.shape, t.dtype)
    out = pl.pallas_call(
        body, name=name,
        out_shape=(pltpu.SemaphoreType.DMA((n * nm,)), pltpu.SemaphoreType.DMA((n * nm,)), *[hbm(t) for t in srcs],
                   *[hbm(t) for t in lands], jax.ShapeDtypeStruct((8, 128), F32)),
        in_specs=(HBM_SPEC,) * (2 * n),
        out_specs=(SEM_SPEC, SEM_SPEC) + (HBM_SPEC,) * (2 * n) + (pl.BlockSpec(memory_space=pltpu.VMEM),),
        input_output_aliases={i: 2 + i for i in range(2 * n)},
        compiler_params=pltpu.CompilerParams(has_side_effects=pltpu.SideEffectType.DATAFLOW_SIDE_EFFECTING),
    )(*[pltpu.with_memory_space_constraint(t, pltpu.HBM) for t in (*srcs, *lands)])
    return out[0], out[1], out[2:2 + n], out[2 + n:2 + 2 * n], out[-1][0:1, 0:1]


def _split_wait(name, mode, send_sems, recv_sems, srcs, lands, after):
    n = len(srcs)

    def body(*refs):
        src, land = refs[:n], refs[n:2 * n]
        ssem, rsem = refs[2 * n], refs[2 * n + 1]
        place = _my_place()
        for w in range(n):
            for j in range(len(SPLIT_MASKS[mode])):
                _split_copy(mode, src[w], land[w], ssem, rsem, w, j, place).wait_send()
                _split_copy(mode, src[w], land[w], ssem, rsem, w, j, place, arriving=True).wait_recv()

    hbm = lambda t: pltpu.HBM(t.shape, t.dtype)
    out = pl.pallas_call(
        body, name=name,
        out_shape=tuple(hbm(t) for t in (*srcs, *lands)),
        in_specs=(HBM_SPEC,) * (2 * n) + (SEM_SPEC, SEM_SPEC, pl.BlockSpec(memory_space=pl.ANY)),
        out_specs=(HBM_SPEC,) * (2 * n),
        input_output_aliases={i: i for i in range(2 * n)},
        compiler_params=pltpu.CompilerParams(has_side_effects=pltpu.SideEffectType.DATAFLOW_SIDE_EFFECTING),
    )(*srcs, *lands, send_sems, recv_sems, after)
    return out[:n], out[n:]


def _forward_copy(zone_ref, send_sems, recv_sems, j, place, arriving=False):
    x, y, c = place
    r = zone_ref.shape[0] // N_DEV
    chip = _peer(x, y, c, SPLIT_MASKS["chips"][j])
    owner = _dev_index(chip[0], chip[1], 1 - c if arriving else c)
    rows = zone_ref.at[pl.ds(pl.multiple_of(owner * r, 16), r), :]
    return pltpu.make_async_remote_copy(src_ref=rows, dst_ref=rows, send_sem=send_sems.at[j], recv_sem=recv_sems.at[j],
                                        device_id=(x, y, 1 - c), device_id_type=MESH)


def _forward_start(name, zone):
    def body(zone_ref, send_sems, recv_sems, zone_thru, token):
        place = _my_place()
        for j in range(3):
            _forward_copy(zone_ref, send_sems, recv_sems, j, place).start()
        token[...] = jnp.zeros_like(token)

    out = pl.pallas_call(
        body, name=name,
        out_shape=(pltpu.SemaphoreType.DMA((3,)), pltpu.SemaphoreType.DMA((3,)), pltpu.HBM(zone.shape, zone.dtype),
                   jax.ShapeDtypeStruct((8, 128), F32)),
        in_specs=(HBM_SPEC,), out_specs=(SEM_SPEC, SEM_SPEC, HBM_SPEC, pl.BlockSpec(memory_space=pltpu.VMEM)),
        input_output_aliases={0: 2},
        compiler_params=pltpu.CompilerParams(has_side_effects=pltpu.SideEffectType.DATAFLOW_SIDE_EFFECTING),
    )(pltpu.with_memory_space_constraint(zone, pltpu.HBM))
    return out[0], out[1], out[2], out[3]


def _forward_wait(name, send_sems, recv_sems, zone, after):
    def body(zone_ref, ssem, rsem, after_ref, zone_out):
        place = _my_place()
        for j in range(3):
            _forward_copy(zone_ref, ssem, rsem, j, place).wait_send()
            _forward_copy(zone_ref, ssem, rsem, j, place, arriving=True).wait_recv()

    return pl.pallas_call(
        body, name=name, out_shape=pltpu.HBM(zone.shape, zone.dtype),
        in_specs=(HBM_SPEC, SEM_SPEC, SEM_SPEC, pl.BlockSpec(memory_space=pl.ANY)), out_specs=HBM_SPEC,
        input_output_aliases={0: 0},
        compiler_params=pltpu.CompilerParams(has_side_effects=pltpu.SideEffectType.DATAFLOW_SIDE_EFFECTING),
    )(zone, send_sems, recv_sems, after)


def _sibling_exchange(grads):
    nw = len(grads)
    HBM = pl.BlockSpec(memory_space=pl.ANY)

    def body(*refs):
        g, land = refs[:nw], refs[nw:2 * nw]
        send_sems, recv_sems = refs[2 * nw:]
        x, y, c = _my_place()

        def copy(w, q, owner_core):
            r = land[w].shape[1]
            return pltpu.make_async_remote_copy(
                src_ref=g[w].at[pl.ds(pl.multiple_of((2 * q + owner_core) * r, 16), r), :], dst_ref=land[w].at[q],
                send_sem=send_sems.at[w, q], recv_sem=recv_sems.at[w, q], device_id=(x, y, 1 - c), device_id_type=MESH)

        sends = [copy(w, q, 1 - c) for w in range(nw) for q in range(4)]
        for cp in sends:
            cp.start()
        for w in range(nw):
            for q in range(4):
                copy(w, q, c).wait_recv()
        for cp in sends:
            cp.wait_send()

    return pl.pallas_call(
        body, name="sibling_exchange",
        out_shape=[jax.ShapeDtypeStruct((4, a.shape[0] // N_DEV, a.shape[1]), a.dtype) for a in grads],
        in_specs=[HBM] * nw, out_specs=[HBM] * nw,
        scratch_shapes=[pltpu.SemaphoreType.DMA((nw, 4)), pltpu.SemaphoreType.DMA((nw, 4))],
    )(*grads)


def _pair_sums(gs, sibs, core):
    n = len(gs)

    def body(core_ref, *refs):
        for w in range(n):
            refs[2 * n + w][0] = (refs[w][0, 0].astype(F32) + refs[n + w][0].astype(F32)).astype(BF16)

    in_specs = [pl.BlockSpec((1, 1) + t.shape[1:], lambda q, core_ref: (q, core_ref[0], 0, 0)) for t in sibs]
    in_specs += [pl.BlockSpec((1,) + t.shape[1:], lambda q, core_ref: (q, 0, 0)) for t in sibs]
    return pl.pallas_call(
        body, name="pair_sums",
        grid_spec=pltpu.PrefetchScalarGridSpec(
            num_scalar_prefetch=1, grid=(4,), in_specs=in_specs,
            out_specs=[pl.BlockSpec((1,) + t.shape[1:], lambda q, core_ref: (q, 0, 0)) for t in sibs]),
        out_shape=[jax.ShapeDtypeStruct(t.shape, BF16) for t in sibs],
        compiler_params=_cparams("parallel"),
    )(core, *[g.reshape(4, 2, t.shape[1], t.shape[2]) for g, t in zip(gs, sibs)], *sibs)


def _allgather_small(v, name):
    r, ccols = v.shape

    def body(v_ref, out_ref, send_sems, recv_sems):
        x, y, c = _my_place()
        my_idx = _dev_index(x, y, c)
        out_ref[my_idx] = v_ref[...]

        def copy(m):
            peer = _peer(x, y, c, m)
            return pltpu.make_async_remote_copy(
                src_ref=v_ref, dst_ref=out_ref.at[my_idx],
                send_sem=send_sems.at[m - 1], recv_sem=recv_sems.at[m - 1], device_id=peer, device_id_type=MESH)

        def arrival(m):
            peer = _peer(x, y, c, m)
            return pltpu.make_async_remote_copy(
                src_ref=v_ref, dst_ref=out_ref.at[_dev_index(*peer)],
                send_sem=send_sems.at[m - 1], recv_sem=recv_sems.at[m - 1], device_id=peer, device_id_type=MESH)

        sends = [copy(m) for m in range(1, N_DEV)]
        for cp in sends:
            cp.start()
        for m in range(1, N_DEV):
            arrival(m).wait_recv()
        for cp in sends:
            cp.wait_send()

    return pl.pallas_call(
        body, name=name,
        out_shape=jax.ShapeDtypeStruct((N_DEV, r, ccols), v.dtype),
        in_specs=[pl.BlockSpec(memory_space=pltpu.VMEM)], out_specs=pl.BlockSpec(memory_space=pltpu.VMEM),
        scratch_shapes=[pltpu.SemaphoreType.DMA((7,)), pltpu.SemaphoreType.DMA((7,))],
    )(v)


def _conditioning(pay, w_ada, b_cols):
    ncol = w_ada.shape[1]

    def body(pay_ref, w_ref, b_ref, got_ref, act_ref, mod_ref, send_sems, recv_sems):
        x, y, c = _my_place()
        my_idx = _dev_index(x, y, c)

        def copy(rnd, buf, m, arriving=False):
            peer = _peer(x, y, c, m)
            slot = _dev_index(*peer) if arriving else my_idx
            return pltpu.make_async_remote_copy(
                src_ref=buf.at[my_idx], dst_ref=buf.at[slot], send_sem=send_sems.at[rnd, m - 1],
                recv_sem=recv_sems.at[rnd, m - 1], device_id=peer, device_id_type=MESH)

        def exchange(rnd, buf):
            sends = [copy(rnd, buf, m) for m in range(1, N_DEV)]
            for cp in sends:
                cp.start()
            for m in range(1, N_DEV):
                copy(rnd, buf, m, arriving=True).wait_recv()
            for cp in sends:
                cp.wait_send()

        got_ref[my_idx] = pay_ref[...]
        exchange(0, got_ref)
        cv = jnp.concatenate([got_ref[s, 0:1, :] for s in range(N_DEV)], axis=0)
        act = cv * _sigmoid(cv)
        act_ref[...] = act
        mod_ref[my_idx] = jnp.dot(act, w_ref[...], preferred_element_type=F32, precision=lax.Precision.HIGHEST) + b_ref[...]
        exchange(1, mod_ref)

    vmem = pl.BlockSpec(memory_space=pltpu.VMEM)
    return pl.pallas_call(
        body, name="conditioning",
        out_shape=[jax.ShapeDtypeStruct((N_DEV, 8, D), F32), jax.ShapeDtypeStruct((N_DEV, D), F32),
                   jax.ShapeDtypeStruct((N_DEV, N_DEV, ncol), F32)],
        in_specs=[vmem] * 3, out_specs=[vmem] * 3,
        scratch_shapes=[pltpu.SemaphoreType.DMA((2, 7)), pltpu.SemaphoreType.DMA((2, 7))],
        compiler_params=_cparams(),
    )(pay, w_ada, b_cols)


def _ada_bwd(act_t, gm_cols):
    def body(a_ref, g_ref, o_ref):
        o_ref[...] = jnp.dot(a_ref[...], g_ref[...], preferred_element_type=F32, precision=lax.Precision.HIGHEST)

    return pl.pallas_call(
        body, name="ada_bwd", out_shape=jax.ShapeDtypeStruct((D, gm_cols.shape[1]), F32), compiler_params=_cparams(),
    )(act_t, gm_cols)


def _row_tile(r):
    for t in (256, 304, 128, 64, 16):
        if r % t == 0:
            return t
    return r


def _sum_parts(parts, name, own=None):
    k, r, ccols = parts.shape
    tr = _row_tile(r)

    def body(*refs):
        p_ref, o_ref = refs[0], refs[-1]
        acc = p_ref[0].astype(F32) if own is None else refs[1][...].astype(F32) + p_ref[0].astype(F32)
        for s in range(1, k):
            acc = acc + p_ref[s].astype(F32)
        o_ref[...] = acc

    blk = pl.BlockSpec((tr, ccols), lambda i: (i, 0))
    return pl.pallas_call(
        body, name=name, grid=(r // tr,),
        in_specs=[pl.BlockSpec((k, tr, ccols), lambda i: (0, i, 0))] + ([] if own is None else [blk]),
        out_specs=blk,
        out_shape=jax.ShapeDtypeStruct((r, ccols), F32),
        compiler_params=_cparams("parallel"),
    )(*((parts,) if own is None else (parts, own)))


def _adamw(w, g, m, v, name):
    r, ccols = w.shape
    tr = _row_tile(r)
    c1 = 1.0 / (1.0 - B1 ** STEP)
    c2 = 1.0 / (1.0 - B2 ** STEP)

    def body(w_ref, g_ref, m_ref, v_ref, d_ref, nm_ref, nv_ref):
        gv = g_ref[...]
        nm = B1 * m_ref[...] + (1.0 - B1) * gv
        nv = B2 * v_ref[...] + (1.0 - B2) * jnp.square(gv)
        nm_ref[...] = nm
        nv_ref[...] = nv
        d_ref[...] = -LR * ((nm * c1) / (jnp.sqrt(nv * c2) + ADAM_EPS) + WD * w_ref[...])

    blk = pl.BlockSpec((tr, ccols), lambda i: (i, 0))
    return pl.pallas_call(
        body, name=name, grid=(r // tr,), in_specs=[blk] * 4, out_specs=[blk] * 3,
        out_shape=[jax.ShapeDtypeStruct((r, ccols), F32)] * 3,
        compiler_params=_cparams("parallel"),
    )(w, g, m, v)


def _sum_adamw(parts, own, w, m, v, name):
    k, r, ccols = parts.shape
    tr = _row_tile(r)
    c1 = 1.0 / (1.0 - B1 ** STEP)
    c2 = 1.0 / (1.0 - B2 ** STEP)

    def body(p_ref, own_ref, w_ref, m_ref, v_ref, g_ref, d_ref, nm_ref, nv_ref):
        gv = own_ref[...].astype(F32)
        for s in range(k):
            gv = gv + p_ref[s].astype(F32)
        g_ref[...] = gv
        nm = B1 * m_ref[...] + (1.0 - B1) * gv
        nv = B2 * v_ref[...] + (1.0 - B2) * jnp.square(gv)
        nm_ref[...] = nm
        nv_ref[...] = nv
        d_ref[...] = -LR * ((nm * c1) / (jnp.sqrt(nv * c2) + ADAM_EPS) + WD * w_ref[...])

    blk = pl.BlockSpec((tr, ccols), lambda i: (i, 0))
    return pl.pallas_call(
        body, name=name, grid=(r // tr,),
        in_specs=[pl.BlockSpec((k, tr, ccols), lambda i: (0, i, 0))] + [blk] * 4, out_specs=[blk] * 4,
        out_shape=[jax.ShapeDtypeStruct((r, ccols), F32)] * 4,
        compiler_params=_cparams("parallel"),
    )(parts, own, w, m, v)


VEC_ROWS = ((0, 6), (6, 7), (9, 11), (11, 14), (7, 8), (8, 9))


def _adamw_vectors(w, g, m, v):
    c1 = 1.0 / (1.0 - B1 ** STEP)
    c2 = 1.0 / (1.0 - B2 ** STEP)

    def put(refs, p):
        for ref, (lo, hi) in zip(refs, VEC_ROWS):
            if ref.shape == (3, HEAD):
                ref[...] = p[lo:hi, :HEAD]
            else:
                ref[...] = jnp.concatenate([p[k:k + 1] for k in range(lo, hi)], axis=1)

    def body(w_ref, g_ref, m_ref, v_ref, *outs):
        gv = g_ref[...]
        nm = B1 * m_ref[...] + (1.0 - B1) * gv
        nv = B2 * v_ref[...] + (1.0 - B2) * jnp.square(gv)
        delta = -LR * ((nm * c1) / (jnp.sqrt(nv * c2) + ADAM_EPS) + WD * w_ref[...])
        for kind, p in enumerate((gv, delta, nm, nv)):
            put(outs[6 * kind:6 * kind + 6], p)

    shapes = [(1, 6 * D), (1, D), (1, 2 * D), (3, HEAD), (1, D), (1, D)]
    out = pl.pallas_call(
        body, name="adamw_vectors", out_shape=[jax.ShapeDtypeStruct(sh, F32) for sh in shapes] * 4, compiler_params=_cparams(),
    )(w, g, m, v)
    fix = lambda t: (t[0], t[1], t[2], t[3][None], t[4], t[5].reshape(D))
    return [fix(out[6 * kind:6 * kind + 6]) for kind in range(4)]


def _pack_vectors(b_ada, g_mix, g_mlp, g_fin, b_gate, conv_w):
    conv_rows = jnp.pad(conv_w.reshape(3, HEAD), ((0, 0), (0, D - HEAD)))
    return jnp.concatenate([b_ada.reshape(6, D), g_mix.reshape(1, D), g_mlp.reshape(1, D), g_fin.reshape(1, D),
                            b_gate.reshape(2, D), conv_rows, jnp.zeros((2, D), F32)], axis=0)


def kernel(x, c, w_ada, b_ada, g_norm_mix, w_in, b_gate, conv_w, w_branch_attn, w_branch_conv, w_out, g_norm_mlp, w_mlp_in, w_mlp_out, g_norm_final, loss_target, m_w_ada, m_b_ada, m_g_norm_mix, m_w_in, m_b_gate, m_conv_w, m_w_branch_attn, m_w_branch_conv, m_w_out, m_g_norm_mlp, m_w_mlp_in, m_w_mlp_out, m_g_norm_final, v_w_ada, v_b_ada, v_g_norm_mix, v_w_in, v_b_gate, v_conv_w, v_w_branch_attn, v_w_branch_conv, v_w_out, v_g_norm_mlp, v_w_mlp_in, v_w_mlp_out, v_g_norm_final):
    S = x.shape[1]
    xi, yi, ci = _my_place()
    me = _dev_index(xi, yi, ci)
    x2 = x.reshape(S, D)
    tgt = loss_target.reshape(S, D)

    pay = jnp.zeros((8, D), F32).at[0].set(c[0]).at[1:4, :HEAD].set(conv_w[0])
    ncol = w_ada.shape[2]
    b_cols = lax.dynamic_slice(b_ada, (0, me * ncol), (1, ncol))
    got, act, mod_all = _conditioning(pay, w_ada[0], b_cols)
    cw8 = jnp.pad(got[:, 1:4, :HEAD].transpose(1, 0, 2).reshape(3, D), ((0, 5), (0, 0)))

    w_in_shard, mod_all = lax.optimization_barrier((w_in[0].T.astype(BF16), mod_all))
    mod = lax.dynamic_index_in_dim(mod_all, me, axis=1, keepdims=False).reshape(6, D)
    (w_int,) = _allgather_weights([w_in_shard])
    late = [w_branch_attn[0].T.astype(BF16), w_branch_conv[0].astype(BF16), w_out[0].astype(BF16),
            w_mlp_in[0].T.astype(BF16), w_mlp_out[0].astype(BF16)]
    w_int, late = lax.optimization_barrier((w_int, late))
    zones = [lax.dynamic_update_slice(lax.empty((N_DEV * t.shape[0], t.shape[1]), BF16), t, (me * t.shape[0], 0)) for t in late]
    ag_mix = _split_start("gather_mix_start", "gather", late[:3], zones[:3])
    ag_mlp = _split_start("gather_mlp_start", "gather", late[3:], zones[3:])
    mod = mod + ag_mix[4] + ag_mlp[4]
    h = _prenorm(x2, g_norm_mix, mod[1:2], mod[0:1])

    def mix_weights(o_attn):
        return _split_wait("gather_mix_wait", "gather", *ag_mix[:4], o_attn)[1]

    def mlp_weights(x1):
        return _split_wait("gather_mlp_wait", "gather", *ag_mlp[:4], x1)[1]

    rs = {}

    def mlp_grads_ready(*grads):
        lands = [lax.empty((N_PEER, t.shape[0] // N_DEV, t.shape[1]), BF16) for t in grads]
        rs["mlp"] = _split_start("scatter_mlp_start", "scatter", grads, lands)
        return rs["mlp"][4]

    def other_grads_ready(*grads):
        core = ci.reshape(1).astype(jnp.int32)
        pair = _pair_sums(grads, _sibling_exchange(grads), core)
        lands = [lax.empty((3,) + t.shape[1:], BF16) for t in pair]
        rs["rest"] = _split_start("scatter_rest_start", "chips", pair, lands)
        return rs["rest"][4]

    ba, bb = b_gate[:, :D], b_gate[:, D:]
    grad_x, vec = _local_step(
        x2, h, tgt, mod, g_norm_mix, g_norm_mlp, g_norm_final.reshape(1, D), ba, bb, cw8, w_int, mix_weights, mlp_weights,
        mlp_grads_ready, other_grads_ready)

    vec_all = _allgather_small(vec, "gather_vec")
    vec_sum = _sum_parts(vec_all, "sum_vec")
    loss = vec_sum[14, 0]
    gm_all = vec_all[:, 0:6, :].reshape(N_DEV, 6 * D)
    gm_cols = lax.dynamic_slice(gm_all, (0, me * ncol), (N_DEV, ncol))
    g_w_ada = _ada_bwd(act.T, gm_cols)
    conv_cols = lax.dynamic_slice(vec_sum[11:14], (0, me * HEAD), (3, HEAD))
    g_pack = jnp.concatenate([vec_sum[0:11], jnp.pad(conv_cols, ((0, 0), (0, D - HEAD))), jnp.zeros((2, D), F32)], axis=0)
    packs = [_pack_vectors(*t) for t in ((b_ada, g_norm_mix, g_norm_mlp, g_norm_final, b_gate, conv_w),
                                         (m_b_ada, m_g_norm_mix, m_g_norm_mlp, m_g_norm_final, m_b_gate, m_conv_w),
                                         (v_b_ada, v_g_norm_mix, v_g_norm_mlp, v_g_norm_final, v_b_gate, v_conv_w))]
    gv, dv, mv, vv = _adamw_vectors(packs[0], g_pack, packs[1], packs[2])
    d_ada, nm_ada, nv_ada = _adamw(w_ada[0], g_w_ada, m_w_ada[0], v_w_ada[0], "adamw_w_ada")

    big = {}
    srcs, lands = _split_wait("scatter_mlp_wait", "scatter", *rs["mlp"][:4], d_ada)
    own = [lax.dynamic_slice(g, (me * land.shape[1], 0), land.shape[1:]) for g, land in zip(srcs, lands)]
    g_mi = _sum_parts(lands[0], "sum_w_mi", own=own[0]).T
    big["w_mi"] = (g_mi[None],) + tuple(t[None] for t in _adamw(w_mlp_in[0], g_mi, m_w_mlp_in[0], v_w_mlp_in[0], "adamw_w_mi"))
    big["w_mo"] = tuple(t[None] for t in _sum_adamw(lands[1], own[1], w_mlp_out[0], m_w_mlp_out[0], v_w_mlp_out[0], "adamw_w_mo"))
    srcs, lands = _split_wait("scatter_rest_wait", "chips", *rs["rest"][:4], big["w_mo"][1])
    own = [lax.dynamic_index_in_dim(pair, 2 * xi + yi, axis=0, keepdims=False) for pair in srcs]
    big["w_in"] = tuple(t.T[None] for t in _sum_adamw(lands[0], own[0], w_in[0].T, m_w_in[0].T, v_w_in[0].T, "adamw_w_in"))
    g_ba = _sum_parts(lands[1], "sum_w_ba", own=own[1]).T
    big["w_ba"] = (g_ba[None],) + tuple(t[None] for t in _adamw(w_branch_attn[0], g_ba, m_w_branch_attn[0], v_w_branch_attn[0], "adamw_w_ba"))
    big["w_bc"] = tuple(t[None] for t in _sum_adamw(lands[2], own[2], w_branch_conv[0], m_w_branch_conv[0], v_w_branch_conv[0], "adamw_w_bc"))
    big["w_out"] = tuple(t[None] for t in _sum_adamw(lands[3], own[3], w_out[0], m_w_out[0], v_w_out[0], "adamw_w_out"))

    def ordered(k, ada, vecs):
        return (ada[None], vecs[0], vecs[1], big["w_in"][k], vecs[2], vecs[3], big["w_ba"][k], big["w_bc"][k],
                big["w_out"][k], vecs[4], big["w_mi"][k], big["w_mo"][k], vecs[5])

    return (loss, grad_x.reshape(1, S, D), *ordered(0, g_w_ada, gv), *ordered(1, d_ada, dv),
            *ordered(2, nm_ada, mv), *ordered(3, nv_ada, vv))
```

```python
import functools

import numpy as np
import jax
import jax.numpy as jnp
from jax import lax
from jax.experimental import pallas as pl
from jax.experimental.pallas import tpu as pltpu

F32, BF16 = jnp.float32, jnp.bfloat16
D = 1024
HEAD = 128
DILATIONS = (1, 4, 16)
N_SLOT = 4
AOW = N_SLOT * HEAD
DFF = 4 * D
N_DEV = 8
UNROLL = 16
EPS = 1e-6
NEG = -1e30
SCALE = HEAD ** -0.5
LR, B1, B2, ADAM_EPS, WD, STEP = 0.001, 0.9, 0.999, 1e-08, 0.01, 10
V7X_VMEM_LIMIT = 56 * 1024 * 1024
TM = 1024
MESH = pl.DeviceIdType.MESH
AXES = ("x", "y", "c")


def _cparams(*sem):
    if sem:
        return pltpu.CompilerParams(dimension_semantics=sem, vmem_limit_bytes=V7X_VMEM_LIMIT)
    return pltpu.CompilerParams(vmem_limit_bytes=V7X_VMEM_LIMIT)


def _nn(a, b):
    return jnp.dot(a, b, preferred_element_type=F32)


def _nt(a, b):
    return lax.dot_general(a, b, (((1,), (1,)), ((), ())), preferred_element_type=F32)


def _tn(a, b):
    return lax.dot_general(a, b, (((0,), (0,)), ((), ())), preferred_element_type=F32)


def _rms_r(x):
    return lax.rsqrt(jnp.mean(x * x, axis=-1, keepdims=True) + EPS)


def _rms_bwd(x, r, g, dn):
    gy = dn * g
    dx = r * gy - x * (r * r * r) * jnp.mean(x * gy, axis=-1, keepdims=True)
    return dx, dn * (x * r)


def _sigmoid(t):
    return 1.0 / (1.0 + jnp.exp(-t))


def _rowsum(v):
    return jnp.sum(v, axis=0, keepdims=True)


def _vec_spec(n=D):
    return pl.BlockSpec((1, n), lambda *_: (0, 0))


def _const_spec(shape):
    nd = len(shape)
    return pl.BlockSpec(shape, lambda *_: (0,) * nd)


def _win_rowblock(j):
    return jnp.where(j < 9, (j % 3) * 3 + j // 3, j)


def _prenorm(x, g, sc, sh):
    S = x.shape[0]
    tm = TM

    def body(x_ref, g_ref, sc_ref, sh_ref, h_ref):
        xv = x_ref[...]
        h_ref[...] = (xv * _rms_r(xv) * g_ref[...] * (1.0 + sc_ref[...]) + sh_ref[...]).astype(BF16)

    row = pl.BlockSpec((tm, D), lambda i: (i, 0))
    return pl.pallas_call(
        body, name="prenorm", grid=(S // tm,), in_specs=[row, _vec_spec(), _vec_spec(), _vec_spec()], out_specs=row,
        out_shape=jax.ShapeDtypeStruct((S, D), BF16), compiler_params=_cparams("parallel"),
    )(x, g, sc, sh)


def _proj(h, w_int):
    S = h.shape[0]

    def body(h_ref, w_ref, q_ref, e_ref):
        j = pl.program_id(0)
        acc = _nt(h_ref[...], w_ref[...])

        @pl.when(j < 9)
        def _():
            q_ref[0] = acc

        @pl.when(j >= 9)
        def _():
            e_ref[0] = acc.astype(BF16)

    def e_idx(j):
        k = jnp.maximum(j - 9, 0)
        return (k // 2, 0, k % 2)

    return pl.pallas_call(
        body, name="proj", grid=(19,),
        in_specs=[pl.BlockSpec((S, D), lambda j: (0, 0), pipeline_mode=pl.Buffered(1)),
                  pl.BlockSpec((512, D), lambda j: (_win_rowblock(j), 0))],
        out_specs=[pl.BlockSpec((1, S, 512), lambda j: (jnp.minimum(j, 8), 0, 0)), pl.BlockSpec((1, S, 512), e_idx)],
        out_shape=[jax.ShapeDtypeStruct((9, S, 512), F32), jax.ShapeDtypeStruct((5, S, D), BF16)],
        compiler_params=_cparams("arbitrary"),
    )(h, w_int)


def _bias_table():
    slopes = (2.0 ** (-8.0 * np.arange(1, 13, dtype=np.float32) / 12.0)).astype(np.float32)
    qi = np.arange(HEAD)[:, None]
    kj = np.arange(2 * HEAD)[None, :]
    delta = HEAD + qi - kj
    mask = (delta >= 0) & (delta <= HEAD)
    out = np.zeros((3, N_SLOT, HEAD, 2 * HEAD), np.float32)
    for gi, d in enumerate(DILATIONS):
        for j in range(N_SLOT):
            bias = -slopes[gi * N_SLOT + j] * (delta * d).astype(np.float32)
            out[gi, j] = np.where(mask, bias, NEG)
    out_t = np.concatenate([out[..., HEAD:].swapaxes(-1, -2), out[..., :HEAD].swapaxes(-1, -2)], axis=-1)
    return jnp.asarray(out), jnp.asarray(out_t)


def _block_rows(b, d):
    r = b % d
    n = b // d
    st = n * (HEAD * d) + r
    stp = jnp.maximum(n - 1, 0) * (HEAD * d) + r
    return n, st, stp


def _attn_fwd(qkv, bias):
    S = qkv.shape[2]
    nblk = S // HEAD
    rows = 256

    def body(qkv_ref, b_ref, o_ref, lse_ref, o_s, lse_s):
        g = pl.program_id(1)
        bias = b_ref[0, 0]
        col = lax.broadcasted_iota(jnp.int32, bias.shape, 1)
        bias_first = jnp.where(col < HEAD, NEG, bias)

        for gi, d in enumerate(DILATIONS):
            @pl.when(g == gi)
            def _(gi=gi, d=d):
                nb = nblk // d

                def keys(start):
                    sl = pl.ds(start, HEAD, stride=d)
                    return qkv_ref.at[0, 1][sl, :].astype(BF16), qkv_ref.at[0, 2][sl, :].astype(BF16)

                def step(b, first_of_residue, before):
                    r, n = b // nb, b % nb
                    cur = pl.ds(n * (HEAD * d) + r, HEAD, stride=d)
                    own = keys(n * (HEAD * d) + r)
                    if first_of_residue:
                        before = own
                    q = qkv_ref.at[0, 0][cur, :].astype(BF16)
                    kw = jnp.concatenate([before[0], own[0]], axis=0)
                    vw = jnp.concatenate([before[1], own[1]], axis=0)
                    s = _nt(q, kw) * SCALE + jnp.where(n > 0, bias, bias_first)
                    m = jnp.max(s, axis=-1, keepdims=True)
                    p = jnp.exp(s - m)
                    l = jnp.sum(p, axis=-1, keepdims=True)
                    o_s.at[gi][cur, :] = _nn(p.astype(BF16), vw) / l
                    lse_s.at[gi][cur, :] = jnp.broadcast_to(m + jnp.log(l), (HEAD, HEAD))
                    return own

                def steps(i, before):
                    for u in range(UNROLL):
                        before = step(i * UNROLL + u, nb <= UNROLL and u % nb == 0, before)
                    return before

                lax.fori_loop(0, nblk // UNROLL, steps, keys(0))

        @pl.when(g == len(DILATIONS) - 1)
        def _():
            def merge(i, carry):
                r = pl.ds(pl.multiple_of(i * rows, rows), rows)
                ls = [lse_s[k, r, :] for k in range(3)]
                top = jnp.maximum(jnp.maximum(ls[0], ls[1]), ls[2])
                ws = [jnp.exp(t - top) for t in ls]
                den = ws[0] + ws[1] + ws[2]
                o_ref[r, :] = (ws[0] * o_s[0, r, :] + ws[1] * o_s[1, r, :] + ws[2] * o_s[2, r, :]) / den
                lse_ref[r, :] = top + jnp.log(den)
                return carry

            lax.fori_loop(0, S // rows, merge, 0)

    return pl.pallas_call(
        body, name="attn_fwd", grid=(N_SLOT, 3),
        in_specs=[pl.BlockSpec((1, 3, S, HEAD), lambda j, g: (g, 0, 0, j)),
                  pl.BlockSpec((1, 1, HEAD, 2 * HEAD), lambda j, g: (g, j, 0, 0))],
        out_specs=[pl.BlockSpec((S, HEAD), lambda j, g: (0, j)), pl.BlockSpec((S, HEAD), lambda j, g: (0, j))],
        out_shape=[jax.ShapeDtypeStruct((S, AOW), F32), jax.ShapeDtypeStruct((S, AOW), F32)],
        scratch_shapes=[pltpu.VMEM((3, S, HEAD), F32)] * 2,
        compiler_params=_cparams("parallel", "arbitrary"),
    )(qkv, bias)


def _shift_down(z, k, halo_rows):
    out = pltpu.roll(z, k, axis=0)
    top = out[:8]
    rid = lax.broadcasted_iota(jnp.int32, top.shape, 0)
    for t in range(k):
        top = jnp.where(rid == t, halo_rows[t], top)
    return jnp.concatenate([top, out[8:]], axis=0)


def _shift_up(z, k, halo_rows):
    n = z.shape[0]
    out = pltpu.roll(z, n - k, axis=0)
    bottom = out[n - 8:]
    rid = lax.broadcasted_iota(jnp.int32, bottom.shape, 0)
    for t in range(k):
        bottom = jnp.where(rid == 8 - k + t, halo_rows[t], bottom)
    return jnp.concatenate([out[:n - 8], bottom], axis=0)


def _e_spec(chunk, tm):
    return pl.BlockSpec((1, tm, D), lambda i, c=chunk: (c, i, 0))


def _e_prev_spec(chunk, tm):
    return pl.BlockSpec((1, 16, D), lambda i, c=chunk: (c, jnp.maximum(i * (tm // 16) - 1, 0), 0))


def _e_next_spec(chunk, tm, S):
    return pl.BlockSpec((1, 16, D), lambda i, c=chunk: (c, jnp.minimum((i + 1) * (tm // 16), S // 16 - 1), 0))


def _mix(o_attn, e, cw8, ba, bb, w_bat, w_bc):
    S = o_attn.shape[0]
    tm = 256

    def body(o_ref, cb_ref, cc_ref, cx_ref, ga_ref, gb_ref, ccp_ref, cxp_ref, cw_ref, ba_ref, bb_ref, wba_ref, wbc_ref,
             obf_ref, cbu_ref, ya_ref, yc_ref, mg_ref):
        i = pl.program_id(0)
        o = o_ref[...].astype(BF16)
        obf_ref[...] = o
        ya = _nt(o, wba_ref[...])
        z = cc_ref[0].astype(F32) * cx_ref[0].astype(F32)
        zp = ccp_ref[0].astype(F32) * cxp_ref[0].astype(F32) * (i > 0).astype(F32)
        z1 = _shift_down(z, 1, [zp[15:16]])
        z2 = _shift_down(z, 2, [zp[14:15], zp[15:16]])
        cw = cw_ref[...]
        u = cw[0:1] * z2 + cw[1:2] * z1 + cw[2:3] * z
        cbu = (cb_ref[0].astype(F32) * u).astype(BF16)
        cbu_ref[...] = cbu
        yc = _nn(cbu, wbc_ref[...])
        sa = _sigmoid(ga_ref[0].astype(F32) + ba_ref[...])
        sb = _sigmoid(gb_ref[0].astype(F32) + bb_ref[...])
        ya_ref[...] = ya.astype(BF16)
        yc_ref[...] = yc.astype(BF16)
        mg_ref[...] = (sa * ya + sb * yc).astype(BF16)

    row = lambda w: pl.BlockSpec((tm, w), lambda i: (i, 0))
    return pl.pallas_call(
        body, name="mix", grid=(S // tm,),
        in_specs=[row(AOW)] + [_e_spec(c, tm) for c in range(5)] + [_e_prev_spec(1, tm), _e_prev_spec(2, tm),
                  _const_spec((8, D)), _vec_spec(), _vec_spec(), _const_spec((D, AOW)), _const_spec((D, D))],
        out_specs=[row(AOW), row(D), row(D), row(D), row(D)],
        out_shape=[jax.ShapeDtypeStruct((S, AOW), BF16)] + [jax.ShapeDtypeStruct((S, D), BF16)] * 4,
        compiler_params=_cparams("parallel"),
    )(o_attn, e, e, e, e, e, e, e, cw8, ba, bb, w_bat, w_bc)


def _out_proj(merged, w_out, x, gate1, g_mlp, sc2, sh2):
    S = x.shape[0]
    tm = TM

    def body(mg_ref, w_ref, x_ref, gt_ref, g_ref, sc_ref, sh_ref, x1_ref, mo_ref, h2_ref):
        mo = _nn(mg_ref[...], w_ref[...])
        mo_ref[...] = mo.astype(BF16)
        x1 = x_ref[...] + gt_ref[...] * mo
        x1_ref[...] = x1
        h2 = x1 * _rms_r(x1) * g_ref[...] * (1.0 + sc_ref[...]) + sh_ref[...]
        h2_ref[...] = h2.astype(BF16)

    row = pl.BlockSpec((tm, D), lambda i: (i, 0))
    return pl.pallas_call(
        body, name="out_proj", grid=(S // tm,),
        in_specs=[row, _const_spec((D, D)), row, _vec_spec(), _vec_spec(), _vec_spec(), _vec_spec()],
        out_specs=[row, row, row],
        out_shape=[jax.ShapeDtypeStruct((S, D), F32), jax.ShapeDtypeStruct((S, D), BF16), jax.ShapeDtypeStruct((S, D), BF16)],
        compiler_params=_cparams("parallel"),
    )(merged, w_out, x, gate1, g_mlp, sc2, sh2)


def _mlp_in(h2, w_mit):
    S = h2.shape[0]
    tm, tn = TM, 2048

    def body(h_ref, w_ref, a_ref, f_ref):
        a = _nt(h_ref[...], w_ref[...])
        a_ref[...] = a.astype(BF16)
        f_ref[...] = jnp.square(jnp.maximum(a, 0.0)).astype(BF16)

    blk = pl.BlockSpec((tm, tn), lambda i, j: (i, j))
    return pl.pallas_call(
        body, name="mlp_in", grid=(S // tm, DFF // tn),
        in_specs=[pl.BlockSpec((tm, D), lambda i, j: (i, 0)), pl.BlockSpec((tn, D), lambda i, j: (j, 0))],
        out_specs=[blk, blk],
        out_shape=[jax.ShapeDtypeStruct((S, DFF), BF16)] * 2,
        compiler_params=_cparams("parallel", "parallel"),
    )(h2, w_mit)


def _mlp_out(f, w_mo, x1, gate2, g_fin, tgt):
    S = x1.shape[0]
    tm = 512
    half = tm // 2

    def body(f_ref, w_ref, x1_ref, gt_ref, g_ref, t_ref, mlp_ref, dx2_ref, pv_ref):
        @pl.when(pl.program_id(0) == 0)
        def _():
            pv_ref[...] = jnp.zeros_like(pv_ref)

        g = g_ref[...]
        for hs in (pl.ds(0, half), pl.ds(half, half)):
            mlp = _nn(f_ref[hs, :], w_ref[...])
            mlp_ref[hs, :] = mlp.astype(BF16)
            x2 = x1_ref[hs, :] + gt_ref[...] * mlp
            r = _rms_r(x2)
            err = x2 * r * g - t_ref[hs, :]
            dx2, pg = _rms_bwd(x2, r, g, err * (1.0 / D))
            dx2_ref[hs, :] = dx2
            pv_ref[0:1, :] += _rowsum(pg)
            pv_ref[1:2, :] += 0.5 * _rowsum(jnp.mean(err * err, axis=-1, keepdims=True))

    row = pl.BlockSpec((tm, D), lambda i: (i, 0))
    return pl.pallas_call(
        body, name="mlp_out", grid=(S // tm,),
        in_specs=[pl.BlockSpec((tm, DFF), lambda i: (i, 0)), _const_spec((DFF, D)), row, _vec_spec(), _vec_spec(), row],
        out_specs=[row, row, _const_spec((8, D))],
        out_shape=[jax.ShapeDtypeStruct((S, D), BF16), jax.ShapeDtypeStruct((S, D), F32), jax.ShapeDtypeStruct((8, D), F32)],
        compiler_params=_cparams("arbitrary"),
    )(f, w_mo, x1, gate2, g_fin, tgt)


def _bwd_mlp_a(dx2, gate2, mlp, w_mo, a):
    S = dx2.shape[0]
    tm = 512
    half = tm // 2

    def body(dx_ref, gt_ref, mlp_ref, w_ref, a_ref, da_ref, dmo_ref, pv_ref):
        @pl.when(pl.program_id(0) == 0)
        def _():
            pv_ref[...] = jnp.zeros_like(pv_ref)

        for hs in (pl.ds(0, half), pl.ds(half, half)):
            dx = dx_ref[hs, :]
            dmo = (dx * gt_ref[...]).astype(BF16)
            dmo_ref[hs, :] = dmo
            pv_ref[0:1, :] += _rowsum(dx * mlp_ref[hs, :].astype(F32))
            df = _nt(dmo, w_ref[...])
            da_ref[hs, :] = (df * (2.0 * jnp.maximum(a_ref[hs, :].astype(F32), 0.0))).astype(BF16)

    row = pl.BlockSpec((tm, D), lambda i: (i, 0))
    wide = pl.BlockSpec((tm, DFF), lambda i: (i, 0))
    return pl.pallas_call(
        body, name="bwd_mlp_a", grid=(S // tm,),
        in_specs=[row, _vec_spec(), row, _const_spec((DFF, D)), wide],
        out_specs=[wide, row, _const_spec((8, D))],
        out_shape=[jax.ShapeDtypeStruct((S, DFF), BF16), jax.ShapeDtypeStruct((S, D), BF16), jax.ShapeDtypeStruct((8, D), F32)],
        compiler_params=_cparams("arbitrary"),
    )(dx2, gate2, mlp, w_mo, a)


def _bwd_mlp_b(da, w_mit, x1, dx2, g_mlp, sc2):
    S = x1.shape[0]
    tm = 512
    half = tm // 2

    def body(da_ref, w_ref, x1_ref, dx2_ref, g_ref, sc_ref, dx1_ref, pv_ref):
        @pl.when(pl.program_id(0) == 0)
        def _():
            pv_ref[...] = jnp.zeros_like(pv_ref)

        g = g_ref[...]
        for hs in (pl.ds(0, half), pl.ds(half, half)):
            dh = _nn(da_ref[hs, :], w_ref[...])
            x1 = x1_ref[hs, :]
            r = _rms_r(x1)
            dxn, pg = _rms_bwd(x1, r, g, dh * (1.0 + sc_ref[...]))
            dx1_ref[hs, :] = dx2_ref[hs, :] + dxn
            pv_ref[0:1, :] += _rowsum(dh)
            pv_ref[1:2, :] += _rowsum(dh * (x1 * r * g))
            pv_ref[2:3, :] += _rowsum(pg)

    row = pl.BlockSpec((tm, D), lambda i: (i, 0))
    return pl.pallas_call(
        body, name="bwd_mlp_b", grid=(S // tm,),
        in_specs=[pl.BlockSpec((tm, DFF), lambda i: (i, 0)), _const_spec((DFF, D)), row, row, _vec_spec(), _vec_spec()],
        out_specs=[row, _const_spec((8, D))],
        out_shape=[jax.ShapeDtypeStruct((S, D), F32), jax.ShapeDtypeStruct((8, D), F32)],
        compiler_params=_cparams("arbitrary"),
    )(da, w_mit, x1, dx2, g_mlp, sc2)


def _bwd_mix(dx1, gate1, mo, e, cw8, ba, bb, ya, yc, o_attn, w_out, w_bc, w_bat):
    S = dx1.shape[0]
    tm = 256
    n_tiles = S // tm

    def body(dx_ref, dxn_ref, gt_ref, mo_ref, cb_ref, cc_ref, cx_ref, ga_ref, gb_ref, cbn_ref, gbn_ref, ccp_ref, cxp_ref,
             cw_ref, ba_ref, bb_ref, ya_ref, yc_ref, o_ref, wout_ref, wbc_ref, wba_ref,
             dmo_ref, dya_ref, dyc_ref, do_ref, dl_ref, de_ref, pv_ref):
        i = pl.program_id(0)

        @pl.when(i == 0)
        def _():
            pv_ref[...] = jnp.zeros_like(pv_ref)

        gate = gt_ref[...]
        bbv = bb_ref[...]

        def conv_branch_grad(dx_rows, gb_rows):
            dmo = (dx_rows * gate).astype(BF16)
            dmg = _nt(dmo, wout_ref[...])
            sb = _sigmoid(gb_rows + bbv)
            dyc = dmg * sb
            return dmo, dmg, sb, dyc, _nt(dyc.astype(BF16), wbc_ref[...])

        dx = dx_ref[...]
        cb = cb_ref[0].astype(F32)
        cc = cc_ref[0].astype(F32)
        cx = cx_ref[0].astype(F32)
        dmo, dmg, sb, dyc, dcbu = conv_branch_grad(dx, gb_ref[0].astype(F32))
        dmo_ref[...] = dmo
        pv_ref[0:1, :] += _rowsum(dx * mo_ref[...].astype(F32))
        sa = _sigmoid(ga_ref[0].astype(F32) + ba_ref[...])
        dya = (dmg * sa).astype(BF16)
        dya_ref[...] = dya
        dyc_ref[...] = dyc.astype(BF16)
        dga = dmg * ya_ref[...].astype(F32) * sa * (1.0 - sa)
        dgb = dmg * yc_ref[...].astype(F32) * sb * (1.0 - sb)
        pv_ref[1:2, :] += _rowsum(dga)
        pv_ref[2:3, :] += _rowsum(dgb)

        do = _nn(dya, wba_ref[...])
        do_ref[...] = do
        prod = do * o_ref[...]
        dl_ref[...] = jnp.concatenate(
            [jnp.broadcast_to(jnp.sum(prod[:, s * HEAD:(s + 1) * HEAD], axis=-1, keepdims=True), (tm, HEAD))
             for s in range(N_SLOT)], axis=1)

        z = cc * cx
        zp = ccp_ref[0].astype(F32) * cxp_ref[0].astype(F32) * (i > 0).astype(F32)
        z1 = _shift_down(z, 1, [zp[15:16]])
        z2 = _shift_down(z, 2, [zp[14:15], zp[15:16]])
        cw = cw_ref[...]
        u = cw[0:1] * z2 + cw[1:2] * z1 + cw[2:3] * z
        du = dcbu * cb
        dcbu_n = conv_branch_grad(dxn_ref[...], gbn_ref[0].astype(F32))[4]
        du_n = dcbu_n * cbn_ref[0].astype(F32) * (i < n_tiles - 1).astype(F32)
        du1 = _shift_up(du, 1, [du_n[0:1]])
        du2 = _shift_up(du, 2, [du_n[0:1], du_n[1:2]])
        dz = cw[2:3] * du + cw[1:2] * du1 + cw[0:1] * du2
        pv_ref[3:4, :] += _rowsum(du * z2)
        pv_ref[4:5, :] += _rowsum(du * z1)
        pv_ref[5:6, :] += _rowsum(du * z)

        de_ref[0] = (dcbu * u).astype(BF16)
        de_ref[1] = (dz * cx).astype(BF16)
        de_ref[2] = (dz * cc).astype(BF16)
        de_ref[3] = dga.astype(BF16)
        de_ref[4] = dgb.astype(BF16)

    row = lambda w: pl.BlockSpec((tm, w), lambda i: (i, 0))
    nxt = pl.BlockSpec((16, D), lambda i: (jnp.minimum((i + 1) * (tm // 16), S // 16 - 1), 0))
    return pl.pallas_call(
        body, name="bwd_mix", grid=(n_tiles,),
        in_specs=[row(D), nxt, _vec_spec(), row(D)] + [_e_spec(c, tm) for c in range(5)]
                 + [_e_next_spec(0, tm, S), _e_next_spec(4, tm, S), _e_prev_spec(1, tm), _e_prev_spec(2, tm),
                    _const_spec((8, D)), _vec_spec(), _vec_spec(), row(D), row(D), row(AOW),
                    _const_spec((D, D)), _const_spec((D, D)), _const_spec((D, AOW))],
        out_specs=[row(D), row(D), row(D), row(AOW), row(AOW), pl.BlockSpec((5, tm, D), lambda i: (0, i, 0)),
                   _const_spec((8, D))],
        out_shape=[jax.ShapeDtypeStruct((S, D), BF16)] * 3 + [jax.ShapeDtypeStruct((S, AOW), F32)] * 2
                  + [jax.ShapeDtypeStruct((5, S, D), BF16), jax.ShapeDtypeStruct((8, D), F32)],
        compiler_params=_cparams("arbitrary"),
    )(dx1, dx1, gate1, mo, e, e, e, e, e, e, e, e, e, cw8, ba, bb, ya, yc, o_attn, w_out, w_bc, w_bat)


def _attn_bwd(qkv, do, lse, dl, bias_t):
    S = qkv.shape[2]
    nblk = S // HEAD

    def body(qkv_ref, do_ref, lse_ref, dl_ref, b_ref, d_ref):
        g = pl.program_id(1)
        bias = b_ref[0, 0]
        col = lax.broadcasted_iota(jnp.int32, bias.shape, 1)
        bias_last = jnp.where(col >= HEAD, NEG, bias)
        eye = (lax.broadcasted_iota(jnp.int32, (HEAD, HEAD), 0) == lax.broadcasted_iota(jnp.int32, (HEAD, HEAD), 1)).astype(F32)

        def as_row(t):
            return jnp.sum(t * eye, axis=0, keepdims=True)

        for gi, d in enumerate(DILATIONS):
            @pl.when(g == gi)
            def _(d=d):
                nb = nblk // d

                def query_side(start):
                    sl = pl.ds(start, HEAD, stride=d)
                    return (qkv_ref.at[0, 0][sl, :].astype(BF16), do_ref[sl, :].astype(BF16),
                            as_row(lse_ref[sl, :]), as_row(dl_ref[sl, :]))

                def step(b, first_of_residue, carry):
                    dq_part, own = carry
                    r, n = b // nb, b % nb
                    cur = pl.ds(n * (HEAD * d) + r, HEAD, stride=d)
                    if first_of_residue:
                        own = query_side(r)
                    nxt = query_side(jnp.minimum(n + 1, nb - 1) * (HEAD * d) + r)
                    q2 = jnp.concatenate([own[0], nxt[0]], axis=0)
                    do2 = jnp.concatenate([own[1], nxt[1]], axis=0)
                    k = qkv_ref.at[0, 1][cur, :].astype(BF16)
                    v = qkv_ref.at[0, 2][cur, :].astype(BF16)
                    s = _nt(k, q2) * SCALE + jnp.where(n < nb - 1, bias, bias_last)
                    p = jnp.exp(s - jnp.concatenate([own[2], nxt[2]], axis=1))
                    d_ref.at[0, 2][cur, :] = _nn(p.astype(BF16), do2)
                    dp = _nt(v, do2)
                    ds = (p * (dp - jnp.concatenate([own[3], nxt[3]], axis=1)) * SCALE).astype(BF16)
                    d_ref.at[0, 1][cur, :] = _nn(ds, q2)
                    dq2 = _tn(ds, k)
                    d_ref.at[0, 0][cur, :] = dq2[:HEAD] + jnp.where(n > 0, dq_part, 0.0)
                    return dq2[HEAD:], nxt

                def steps(i, carry):
                    for u in range(UNROLL):
                        carry = step(i * UNROLL + u, nb <= UNROLL and u % nb == 0, carry)
                    return carry

                lax.fori_loop(0, nblk // UNROLL, steps, (jnp.zeros((HEAD, HEAD), F32), query_side(0)))

    col_blk = pl.BlockSpec((S, HEAD), lambda j, g: (0, j))
    qkv_blk = pl.BlockSpec((1, 3, S, HEAD), lambda j, g: (g, 0, 0, j))
    return pl.pallas_call(
        body, name="attn_bwd", grid=(N_SLOT, 3),
        in_specs=[qkv_blk, col_blk, col_blk, col_blk, pl.BlockSpec((1, 1, HEAD, 2 * HEAD), lambda j, g: (g, j, 0, 0))],
        out_specs=qkv_blk,
        out_shape=jax.ShapeDtypeStruct((3, 3, S, AOW), F32),
        compiler_params=_cparams("parallel", "arbitrary"),
    )(qkv, do, lse, dl, bias_t)


def _bwd_in(dqkv, de, w_int, x, dx1, g_mix, sc1):
    S = x.shape[0]
    tm = TM
    dqkv = dqkv.reshape(3, 3, S, AOW)

    def body(dq_ref, de_ref, wq_ref, wk_ref, wv_ref, wa_ref, wb_ref, x_ref, dx1_ref, g_ref, sc_ref, gx_ref, pv_ref):
        acc = gx_ref
        i, k = pl.program_id(0), pl.program_id(1)

        @pl.when((i == 0) & (k == 0))
        def _():
            pv_ref[...] = jnp.zeros_like(pv_ref)

        @pl.when(k == 0)
        def _():
            acc[...] = jnp.zeros_like(acc)

        @pl.when(k < 3)
        def _():
            lhs = jnp.concatenate([dq_ref[0, t].astype(BF16) for t in range(3)], axis=1)
            acc[...] += _nn(lhs, jnp.concatenate([wq_ref[...], wk_ref[...], wv_ref[...]], axis=0))

        @pl.when(k >= 3)
        def _():
            acc[...] += _nn(de_ref[0], jnp.concatenate([wa_ref[...], wb_ref[...]], axis=0))

        @pl.when(k == 7)
        def _():
            dh = acc[...]
            xv = x_ref[...]
            r = _rms_r(xv)
            g = g_ref[...]
            dxn, pg = _rms_bwd(xv, r, g, dh * (1.0 + sc_ref[...]))
            gx_ref[...] = dx1_ref[...] + dxn
            pv_ref[0:1, :] += _rowsum(dh)
            pv_ref[1:2, :] += _rowsum(dh * (xv * r * g))
            pv_ref[2:3, :] += _rowsum(pg)

    grp = lambda k: jnp.minimum(k, 2)
    chunk = lambda k: jnp.maximum(k - 3, 0)
    wblk = lambda f: pl.BlockSpec((512, D), lambda i, k: (f(k), 0))
    row = pl.BlockSpec((tm, D), lambda i, k: (i, 0))
    once = pl.BlockSpec((tm, D), lambda i, k: (i, 0), pipeline_mode=pl.Buffered(1))
    return pl.pallas_call(
        body, name="bwd_in", grid=(S // tm, 8),
        in_specs=[pl.BlockSpec((1, 3, tm, 512), lambda i, k: (grp(k), 0, i, 0)),
                  pl.BlockSpec((1, tm, D), lambda i, k: (chunk(k), i, 0)),
                  wblk(grp), wblk(lambda k: 3 + grp(k)), wblk(lambda k: 6 + grp(k)),
                  wblk(lambda k: 9 + 2 * chunk(k)), wblk(lambda k: 10 + 2 * chunk(k)),
                  once, once, _vec_spec(), _vec_spec()],
        out_specs=[row, _const_spec((8, D))],
        out_shape=[jax.ShapeDtypeStruct((S, D), F32), jax.ShapeDtypeStruct((8, D), F32)],
        compiler_params=_cparams("arbitrary", "arbitrary"),
    )(dqkv, de, w_int, w_int, w_int, w_int, w_int, x, dx1, g_mix, sc1)


def _grad_w(name, a, b):
    S, ka = a.shape
    nb = b.shape[1]

    def body(a_ref, b_ref, o_ref):
        o_ref[...] = _tn(a_ref[...], b_ref[...]).astype(BF16)

    return pl.pallas_call(
        body, name=name, grid=(ka // 512,),
        in_specs=[pl.BlockSpec((S, 512), lambda n: (0, n)), pl.BlockSpec((S, nb), lambda n: (0, 0))],
        out_specs=pl.BlockSpec((512, nb), lambda n: (n, 0)),
        out_shape=jax.ShapeDtypeStruct((ka, nb), BF16),
        compiler_params=_cparams("parallel"),
    )(a, b)


def _grad_w_small(dya, o_bf, cbu, dyc, merged, dmo):
    S = dya.shape[0]

    def body(dya_ref, o_ref, cbu_ref, dyc_ref, mg_ref, dmo_ref, gba_ref, gbc_ref, gout_ref):
        gba_ref[...] = _tn(dya_ref[...], o_ref[...]).astype(BF16)
        gbc_ref[...] = _tn(cbu_ref[...], dyc_ref[...]).astype(BF16)
        gout_ref[...] = _tn(mg_ref[...], dmo_ref[...]).astype(BF16)

    a_blk = pl.BlockSpec((S, 512), lambda n: (0, n))
    whole = lambda w: pl.BlockSpec((S, w), lambda n: (0, 0))
    out = lambda w: pl.BlockSpec((512, w), lambda n: (n, 0))
    return pl.pallas_call(
        body, name="grad_w_small", grid=(D // 512,),
        in_specs=[a_blk, whole(AOW), a_blk, whole(D), a_blk, whole(D)],
        out_specs=[out(AOW), out(D), out(D)],
        out_shape=[jax.ShapeDtypeStruct((D, AOW), BF16), jax.ShapeDtypeStruct((D, D), BF16), jax.ShapeDtypeStruct((D, D), BF16)],
        compiler_params=_cparams("parallel"),
    )(dya, o_bf, cbu, dyc, merged, dmo)


def _grad_w_in(dqkv, de, h):
    S = h.shape[0]

    def body(dq_ref, de_ref, h_ref, o_ref):
        n = pl.program_id(0)

        @pl.when(n < 9)
        def _():
            o_ref[...] = _tn(dq_ref[0].astype(BF16), h_ref[...]).astype(BF16)

        @pl.when(n >= 9)
        def _():
            o_ref[...] = _tn(de_ref[0], h_ref[...]).astype(BF16)

    def e_idx(n):
        kk = jnp.maximum(n - 9, 0)
        return (kk // 2, 0, kk % 2)

    return pl.pallas_call(
        body, name="grad_w_in", grid=(19,),
        in_specs=[pl.BlockSpec((1, S, 512), lambda n: (jnp.minimum(n, 8), 0, 0)), pl.BlockSpec((1, S, 512), e_idx),
                  pl.BlockSpec((S, D), lambda n: (0, 0))],
        out_specs=pl.BlockSpec((512, D), lambda n: (_win_rowblock(n), 0)),
        out_shape=jax.ShapeDtypeStruct((19 * 512, D), BF16),
        compiler_params=_cparams("parallel"),
    )(dqkv, de, h)


def _local_step(x, h, tgt, mod, g_mix, g_mlp, g_fin, ba, bb, cw8, w_int, mix_weights, mlp_weights, mlp_grads_ready, other_grads_ready):
    S = x.shape[0]
    sh1, sc1, gt1, sh2, sc2, gt2 = [mod[k:k + 1] for k in range(6)]
    bias, bias_t = _bias_table()

    qkv, e = _proj(h, w_int)
    qkv = qkv.reshape(3, 3, S, AOW)
    o_attn, lse = _attn_fwd(qkv, bias)
    w_bat, w_bc, w_out = mix_weights(o_attn)
    o_bf, cbu, ya, yc, merged = _mix(o_attn, e, cw8, ba, bb, w_bat, w_bc)
    x1, mo, h2 = _out_proj(merged, w_out, x, gt1, g_mlp, sc2, sh2)
    w_mit, w_mo = mlp_weights(x1)
    a, f = _mlp_in(h2, w_mit)
    mlp, dx2, pv_f = _mlp_out(f, w_mo, x1, gt2, g_fin, tgt)

    da, dmo2, pv_a = _bwd_mlp_a(dx2, gt2, mlp, w_mo, a)
    dx1, pv_b = _bwd_mlp_b(da, w_mit, x1, dx2, g_mlp, sc2)
    zero = mlp_grads_ready(_grad_w("grad_w_mi", da, h2), _grad_w("grad_w_mo", f, dmo2))
    dmo, dya, dyc, do, dl, de, pv_m = _bwd_mix(dx1, gt1 + zero, mo, e, cw8, ba, bb, ya, yc, o_attn, w_out, w_bc, w_bat)
    dqkv = _attn_bwd(qkv, do, lse, dl, bias_t).reshape(9, S, AOW)
    zero = other_grads_ready(_grad_w_in(dqkv, de, h), *_grad_w_small(dya, o_bf, cbu, dyc, merged, dmo))
    grad_x, pv_i = _bwd_in(dqkv, de, w_int, x, dx1, g_mix, sc1 + zero)

    vec = jnp.concatenate([pv_i[0:2], pv_m[0:1], pv_b[0:2], pv_a[0:1], pv_i[2:3], pv_b[2:3], pv_f[0:1],
                           pv_m[1:3], pv_m[3:6], pv_f[1:2], jnp.zeros((1, D), F32)], axis=0)
    return grad_x, vec


def _my_place():
    return lax.axis_index("x"), lax.axis_index("y"), lax.axis_index("c")


def _dev_index(px, py, pc):
    return 4 * px + 2 * py + pc


def _allgather_weights(shards):
    nw = len(shards)
    HBM = pl.BlockSpec(memory_space=pl.ANY)

    def body(*refs):
        sh, full = refs[:nw], refs[nw:2 * nw]
        send_sems, recv_sems, local_sems = refs[2 * nw:]
        x, y, c = _my_place()
        me, sibling = (x, y, c), (x, y, 1 - c)
        chips = [(1 - x, y), (x, 1 - y), (1 - x, 1 - y)]

        def rows(w, px, py, pc):
            r = sh[w].shape[0]
            return full[w].at[pl.ds(pl.multiple_of(_dev_index(px, py, pc) * r, 16), r), :]

        def copy(w, k, block, to, src=None):
            return pltpu.make_async_remote_copy(
                src_ref=rows(w, *block) if src is None else src, dst_ref=rows(w, *block),
                send_sem=send_sems.at[w, k], recv_sem=recv_sems.at[w, k], device_id=to, device_id_type=MESH)

        mine = [pltpu.make_async_copy(sh[w], rows(w, *me), local_sems.at[w]) for w in range(nw)]
        for cp in mine:
            cp.start()
        first = []
        for w in range(nw):
            first.append(copy(w, 0, me, sibling, src=sh[w]))
            first += [copy(w, 1 + j, me, (*chip, c), src=sh[w]) for j, chip in enumerate(chips)]
        for cp in first:
            cp.start()
        passed = []
        for w in range(nw):
            for j, chip in enumerate(chips):
                copy(w, 1 + j, (*chip, c), me).wait_recv()
                fwd = copy(w, 4 + j, (*chip, c), sibling)
                fwd.start()
                passed.append(fwd)
        for w in range(nw):
            copy(w, 0, sibling, me).wait_recv()
            for j, chip in enumerate(chips):
                copy(w, 4 + j, (*chip, 1 - c), me).wait_recv()
        for cp in first + passed:
            cp.wait_send()
        for cp in mine:
            cp.wait()

    return pl.pallas_call(
        body, name="allgather_weights",
        out_shape=[jax.ShapeDtypeStruct((N_DEV * s.shape[0], s.shape[1]), s.dtype) for s in shards],
        in_specs=[HBM] * nw, out_specs=[HBM] * nw,
        scratch_shapes=[pltpu.SemaphoreType.DMA((nw, 7)), pltpu.SemaphoreType.DMA((nw, 7)), pltpu.SemaphoreType.DMA((nw,))],
    )(*shards)


def _peer(x, y, c, m):
    return (x ^ ((m >> 2) & 1), y ^ ((m >> 1) & 1), c ^ (m & 1))


HBM_SPEC = pl.BlockSpec(memory_space=pltpu.HBM)
SEM_SPEC = pl.BlockSpec(memory_space=pltpu.SEMAPHORE)
N_PEER = N_DEV - 1


SPLIT_MASKS = {"gather": tuple(range(1, N_DEV)), "gather_near": (1, 2, 4, 6), "scatter": tuple(range(1, N_DEV)),
               "chips": (2, 4, 6)}


def _split_copy(mode, src_ref, land_ref, send_sems, recv_sems, w, j, place, arriving=False):
    x, y, c = place
    masks = SPLIT_MASKS[mode]
    peer = _peer(x, y, c, masks[j])
    k = w * len(masks) + j
    sender, receiver = ((peer, (x, y, c)) if arriving else ((x, y, c), peer))
    if mode.startswith("gather"):
        r = src_ref.shape[0]
        src, dst = src_ref, land_ref.at[pl.ds(pl.multiple_of(_dev_index(*sender) * r, 16), r), :]
    elif mode == "scatter":
        r = land_ref.shape[1]
        src, dst = src_ref.at[pl.ds(pl.multiple_of(_dev_index(*receiver) * r, 16), r), :], land_ref.at[j]
    else:
        src, dst = src_ref.at[2 * receiver[0] + receiver[1]], land_ref.at[j]
    return pltpu.make_async_remote_copy(src_ref=src, dst_ref=dst, send_sem=send_sems.at[k], recv_sem=recv_sems.at[k],
                                        device_id=peer, device_id_type=MESH)


def _split_start(name, mode, srcs, lands):
    n = len(srcs)
    nm = len(SPLIT_MASKS[mode])

    def body(*refs):
        src, land = refs[:n], refs[n:2 * n]
        send_sems, recv_sems = refs[2 * n], refs[2 * n + 1]
        token = refs[-1]
        place = _my_place()
        for w in range(n):
            for j in range(nm):
                _split_copy(mode, src[w], land[w], send_sems, recv_sems, w, j, place).start()
        token[...] = jnp.zeros_like(token)

    hbm = lambda t: pltpu.HBM(t.shape, t.dtype)
    out = pl.pallas_call(
        body, name=name,
        out_shape=(pltpu.SemaphoreType.DMA((n * nm,)), pltpu.SemaphoreType.DMA((n * nm,)), *[hbm(t) for t in srcs],
                   *[hbm(t) for t in lands], jax.ShapeDtypeStruct((8, 128), F32)),
        in_specs=(HBM_SPEC,) * (2 * n),
        out_specs=(SEM_SPEC, SEM_SPEC) + (HBM_SPEC,) * (2 * n) + (pl.BlockSpec(memory_space=pltpu.VMEM),),
        input_output_aliases={i: 2 + i for i in range(2 * n)},
        compiler_params=pltpu.CompilerParams(has_side_effects=pltpu.SideEffectType.DATAFLOW_SIDE_EFFECTING),
    )(*[pltpu.with_memory_space_constraint(t, pltpu.HBM) for t in (*srcs, *lands)])
    return out[0], out[1], out[2:2 + n], out[2 + n:2 + 2 * n], out[-1][0:1, 0:1]


def _split_wait(name, mode, send_sems, recv_sems, srcs, lands, after):
    n = len(srcs)

    def body(*refs):
        src, land = refs[:n], refs[n:2 * n]
        ssem, rsem = refs[2 * n], refs[2 * n + 1]
        place = _my_place()
        for w in range(n):
            for j in range(len(SPLIT_MASKS[mode])):
                _split_copy(mode, src[w], land[w], ssem, rsem, w, j, place).wait_send()
                _split_copy(mode, src[w], land[w], ssem, rsem, w, j, place, arriving=True).wait_recv()

    hbm = lambda t: pltpu.HBM(t.shape, t.dtype)
    out = pl.pallas_call(
        body, name=name,
        out_shape=tuple(hbm(t) for t in (*srcs, *lands)),
        in_specs=(HBM_SPEC,) * (2 * n) + (SEM_SPEC, SEM_SPEC, pl.BlockSpec(memory_space=pl.ANY)),
        out_specs=(HBM_SPEC,) * (2 * n),
        input_output_aliases={i: i for i in range(2 * n)},
        compiler_params=pltpu.CompilerParams(has_side_effects=pltpu.SideEffectType.DATAFLOW_SIDE_EFFECTING),
    )(*srcs, *lands, send_sems, recv_sems, after)
    return out[:n], out[n:]


def _forward_copy(zone_ref, send_sems, recv_sems, j, place, arriving=False):
    x, y, c = place
    r = zone_ref.shape[0] // N_DEV
    chip = _peer(x, y, c, SPLIT_MASKS["chips"][j])
    owner = _dev_index(chip[0], chip[1], 1 - c if arriving else c)
    rows = zone_ref.at[pl.ds(pl.multiple_of(owner * r, 16), r), :]
    return pltpu.make_async_remote_copy(src_ref=rows, dst_ref=rows, send_sem=send_sems.at[j], recv_sem=recv_sems.at[j],
                                        device_id=(x, y, 1 - c), device_id_type=MESH)


def _forward_start(name, zone):
    def body(zone_ref, send_sems, recv_sems, zone_thru, token):
        place = _my_place()
        for j in range(3):
            _forward_copy(zone_ref, send_sems, recv_sems, j, place).start()
        token[...] = jnp.zeros_like(token)

    out = pl.pallas_call(
        body, name=name,
        out_shape=(pltpu.SemaphoreType.DMA((3,)), pltpu.SemaphoreType.DMA((3,)), pltpu.HBM(zone.shape, zone.dtype),
                   jax.ShapeDtypeStruct((8, 128), F32)),
        in_specs=(HBM_SPEC,), out_specs=(SEM_SPEC, SEM_SPEC, HBM_SPEC, pl.BlockSpec(memory_space=pltpu.VMEM)),
        input_output_aliases={0: 2},
        compiler_params=pltpu.CompilerParams(has_side_effects=pltpu.SideEffectType.DATAFLOW_SIDE_EFFECTING),
    )(pltpu.with_memory_space_constraint(zone, pltpu.HBM))
    return out[0], out[1], out[2], out[3]


def _forward_wait(name, send_sems, recv_sems, zone, after):
    def body(zone_ref, ssem, rsem, after_ref, zone_out):
        place = _my_place()
        for j in range(3):
            _forward_copy(zone_ref, ssem, rsem, j, place).wait_send()
            _forward_copy(zone_ref, ssem, rsem, j, place, arriving=True).wait_recv()

    return pl.pallas_call(
        body, name=name, out_shape=pltpu.HBM(zone.shape, zone.dtype),
        in_specs=(HBM_SPEC, SEM_SPEC, SEM_SPEC, pl.BlockSpec(memory_space=pl.ANY)), out_specs=HBM_SPEC,
        input_output_aliases={0: 0},
        compiler_params=pltpu.CompilerParams(has_side_effects=pltpu.SideEffectType.DATAFLOW_SIDE_EFFECTING),
    )(zone, send_sems, recv_sems, after)


def _sibling_exchange(grads):
    nw = len(grads)
    HBM = pl.BlockSpec(memory_space=pl.ANY)

    def body(*refs):
        g, land = refs[:nw], refs[nw:2 * nw]
        send_sems, recv_sems = refs[2 * nw:]
        x, y, c = _my_place()

        def copy(w, q, owner_core):
            r = land[w].shape[1]
            return pltpu.make_async_remote_copy(
                src_ref=g[w].at[pl.ds(pl.multiple_of((2 * q + owner_core) * r, 16), r), :], dst_ref=land[w].at[q],
                send_sem=send_sems.at[w, q], recv_sem=recv_sems.at[w, q], device_id=(x, y, 1 - c), device_id_type=MESH)

        sends = [copy(w, q, 1 - c) for w in range(nw) for q in range(4)]
        for cp in sends:
            cp.start()
        for w in range(nw):
            for q in range(4):
                copy(w, q, c).wait_recv()
        for cp in sends:
            cp.wait_send()

    return pl.pallas_call(
        body, name="sibling_exchange",
        out_shape=[jax.ShapeDtypeStruct((4, a.shape[0] // N_DEV, a.shape[1]), a.dtype) for a in grads],
        in_specs=[HBM] * nw, out_specs=[HBM] * nw,
        scratch_shapes=[pltpu.SemaphoreType.DMA((nw, 4)), pltpu.SemaphoreType.DMA((nw, 4))],
    )(*grads)


def _pair_sums(gs, sibs, core):
    n = len(gs)

    def body(core_ref, *refs):
        for w in range(n):
            refs[2 * n + w][0] = (refs[w][0, 0].astype(F32) + refs[n + w][0].astype(F32)).astype(BF16)

    in_specs = [pl.BlockSpec((1, 1) + t.shape[1:], lambda q, core_ref: (q, core_ref[0], 0, 0)) for t in sibs]
    in_specs += [pl.BlockSpec((1,) + t.shape[1:], lambda q, core_ref: (q, 0, 0)) for t in sibs]
    return pl.pallas_call(
        body, name="pair_sums",
        grid_spec=pltpu.PrefetchScalarGridSpec(
            num_scalar_prefetch=1, grid=(4,), in_specs=in_specs,
            out_specs=[pl.BlockSpec((1,) + t.shape[1:], lambda q, core_ref: (q, 0, 0)) for t in sibs]),
        out_shape=[jax.ShapeDtypeStruct(t.shape, BF16) for t in sibs],
        compiler_params=_cparams("parallel"),
    )(core, *[g.reshape(4, 2, t.shape[1], t.shape[2]) for g, t in zip(gs, sibs)], *sibs)


def _allgather_small(v, name):
    r, ccols = v.shape

    def body(v_ref, out_ref, send_sems, recv_sems):
        x, y, c = _my_place()
        my_idx = _dev_index(x, y, c)
        out_ref[my_idx] = v_ref[...]

        def copy(m):
            peer = _peer(x, y, c, m)
            return pltpu.make_async_remote_copy(
                src_ref=v_ref, dst_ref=out_ref.at[my_idx],
                send_sem=send_sems.at[m - 1], recv_sem=recv_sems.at[m - 1], device_id=peer, device_id_type=MESH)

        def arrival(m):
            peer = _peer(x, y, c, m)
            return pltpu.make_async_remote_copy(
                src_ref=v_ref, dst_ref=out_ref.at[_dev_index(*peer)],
                send_sem=send_sems.at[m - 1], recv_sem=recv_sems.at[m - 1], device_id=peer, device_id_type=MESH)

        sends = [copy(m) for m in range(1, N_DEV)]
        for cp in sends:
            cp.start()
        for m in range(1, N_DEV):
            arrival(m).wait_recv()
        for cp in sends:
            cp.wait_send()

    return pl.pallas_call(
        body, name=name,
        out_shape=jax.ShapeDtypeStruct((N_DEV, r, ccols), v.dtype),
        in_specs=[pl.BlockSpec(memory_space=pltpu.VMEM)], out_specs=pl.BlockSpec(memory_space=pltpu.VMEM),
        scratch_shapes=[pltpu.SemaphoreType.DMA((7,)), pltpu.SemaphoreType.DMA((7,))],
    )(v)


def _conditioning(pay, w_ada, b_cols):
    ncol = w_ada.shape[1]

    def body(pay_ref, w_ref, b_ref, got_ref, act_ref, mod_ref, send_sems, recv_sems):
        x, y, c = _my_place()
        my_idx = _dev_index(x, y, c)

        def copy(rnd, buf, m, arriving=False):
            peer = _peer(x, y, c, m)
            slot = _dev_index(*peer) if arriving else my_idx
            return pltpu.make_async_remote_copy(
                src_ref=buf.at[my_idx], dst_ref=buf.at[slot], send_sem=send_sems.at[rnd, m - 1],
                recv_sem=recv_sems.at[rnd, m - 1], device_id=peer, device_id_type=MESH)

        def exchange(rnd, buf):
            sends = [copy(rnd, buf, m) for m in range(1, N_DEV)]
            for cp in sends:
                cp.start()
            for m in range(1, N_DEV):
                copy(rnd, buf, m, arriving=True).wait_recv()
            for cp in sends:
                cp.wait_send()

        got_ref[my_idx] = pay_ref[...]
        exchange(0, got_ref)
        cv = jnp.concatenate([got_ref[s, 0:1, :] for s in range(N_DEV)], axis=0)
        act = cv * _sigmoid(cv)
        act_ref[...] = act
        mod_ref[my_idx] = jnp.dot(act, w_ref[...], preferred_element_type=F32, precision=lax.Precision.HIGHEST) + b_ref[...]
        exchange(1, mod_ref)

    vmem = pl.BlockSpec(memory_space=pltpu.VMEM)
    return pl.pallas_call(
        body, name="conditioning",
        out_shape=[jax.ShapeDtypeStruct((N_DEV, 8, D), F32), jax.ShapeDtypeStruct((N_DEV, D), F32),
                   jax.ShapeDtypeStruct((N_DEV, N_DEV, ncol), F32)],
        in_specs=[vmem] * 3, out_specs=[vmem] * 3,
        scratch_shapes=[pltpu.SemaphoreType.DMA((2, 7)), pltpu.SemaphoreType.DMA((2, 7))],
        compiler_params=_cparams(),
    )(pay, w_ada, b_cols)


def _ada_bwd(act_t, gm_cols):
    def body(a_ref, g_ref, o_ref):
        o_ref[...] = jnp.dot(a_ref[...], g_ref[...], preferred_element_type=F32, precision=lax.Precision.HIGHEST)

    return pl.pallas_call(
        body, name="ada_bwd", out_shape=jax.ShapeDtypeStruct((D, gm_cols.shape[1]), F32), compiler_params=_cparams(),
    )(act_t, gm_cols)


def _row_tile(r):
    for t in (256, 304, 128, 64, 16):
        if r % t == 0:
            return t
    return r


def _sum_parts(parts, name, own=None):
    k, r, ccols = parts.shape
    tr = _row_tile(r)

    def body(*refs):
        p_ref, o_ref = refs[0], refs[-1]
        acc = p_ref[0].astype(F32) if own is None else refs[1][...].astype(F32) + p_ref[0].astype(F32)
        for s in range(1, k):
            acc = acc + p_ref[s].astype(F32)
        o_ref[...] = acc

    blk = pl.BlockSpec((tr, ccols), lambda i: (i, 0))
    return pl.pallas_call(
        body, name=name, grid=(r // tr,),
        in_specs=[pl.BlockSpec((k, tr, ccols), lambda i: (0, i, 0))] + ([] if own is None else [blk]),
        out_specs=blk,
        out_shape=jax.ShapeDtypeStruct((r, ccols), F32),
        compiler_params=_cparams("parallel"),
    )(*((parts,) if own is None else (parts, own)))


def _adamw(w, g, m, v, name):
    r, ccols = w.shape
    tr = _row_tile(r)
    c1 = 1.0 / (1.0 - B1 ** STEP)
    c2 = 1.0 / (1.0 - B2 ** STEP)

    def body(w_ref, g_ref, m_ref, v_ref, d_ref, nm_ref, nv_ref):
        gv = g_ref[...]
        nm = B1 * m_ref[...] + (1.0 - B1) * gv
        nv = B2 * v_ref[...] + (1.0 - B2) * jnp.square(gv)
        nm_ref[...] = nm
        nv_ref[...] = nv
        d_ref[...] = -LR * ((nm * c1) / (jnp.sqrt(nv * c2) + ADAM_EPS) + WD * w_ref[...])

    blk = pl.BlockSpec((tr, ccols), lambda i: (i, 0))
    return pl.pallas_call(
        body, name=name, grid=(r // tr,), in_specs=[blk] * 4, out_specs=[blk] * 3,
        out_shape=[jax.ShapeDtypeStruct((r, ccols), F32)] * 3,
        compiler_params=_cparams("parallel"),
    )(w, g, m, v)


def _sum_adamw(parts, own, w, m, v, name):
    k, r, ccols = parts.shape
    tr = _row_tile(r)
    c1 = 1.0 / (1.0 - B1 ** STEP)
    c2 = 1.0 / (1.0 - B2 ** STEP)

    def body(p_ref, own_ref, w_ref, m_ref, v_ref, g_ref, d_ref, nm_ref, nv_ref):
        gv = own_ref[...].astype(F32)
        for s in range(k):
            gv = gv + p_ref[s].astype(F32)
        g_ref[...] = gv
        nm = B1 * m_ref[...] + (1.0 - B1) * gv
        nv = B2 * v_ref[...] + (1.0 - B2) * jnp.square(gv)
        nm_ref[...] = nm
        nv_ref[...] = nv
        d_ref[...] = -LR * ((nm * c1) / (jnp.sqrt(nv * c2) + ADAM_EPS) + WD * w_ref[...])

    blk = pl.BlockSpec((tr, ccols), lambda i: (i, 0))
    return pl.pallas_call(
        body, name=name, grid=(r // tr,),
        in_specs=[pl.BlockSpec((k, tr, ccols), lambda i: (0, i, 0))] + [blk] * 4, out_specs=[blk] * 4,
        out_shape=[jax.ShapeDtypeStruct((r, ccols), F32)] * 4,
        compiler_params=_cparams("parallel"),
    )(parts, own, w, m, v)


VEC_ROWS = ((0, 6), (6, 7), (9, 11), (11, 14), (7, 8), (8, 9))


def _adamw_vectors(w, g, m, v):
    c1 = 1.0 / (1.0 - B1 ** STEP)
    c2 = 1.0 / (1.0 - B2 ** STEP)

    def put(refs, p):
        for ref, (lo, hi) in zip(refs, VEC_ROWS):
            if ref.shape == (3, HEAD):
                ref[...] = p[lo:hi, :HEAD]
            else:
                ref[...] = jnp.concatenate([p[k:k + 1] for k in range(lo, hi)], axis=1)

    def body(w_ref, g_ref, m_ref, v_ref, *outs):
        gv = g_ref[...]
        nm = B1 * m_ref[...] + (1.0 - B1) * gv
        nv = B2 * v_ref[...] + (1.0 - B2) * jnp.square(gv)
        delta = -LR * ((nm * c1) / (jnp.sqrt(nv * c2) + ADAM_EPS) + WD * w_ref[...])
        for kind, p in enumerate((gv, delta, nm, nv)):
            put(outs[6 * kind:6 * kind + 6], p)

    shapes = [(1, 6 * D), (1, D), (1, 2 * D), (3, HEAD), (1, D), (1, D)]
    out = pl.pallas_call(
        body, name="adamw_vectors", out_shape=[jax.ShapeDtypeStruct(sh, F32) for sh in shapes] * 4, compiler_params=_cparams(),
    )(w, g, m, v)
    fix = lambda t: (t[0], t[1], t[2], t[3][None], t[4], t[5].reshape(D))
    return [fix(out[6 * kind:6 * kind + 6]) for kind in range(4)]


def _pack_vectors(b_ada, g_mix, g_mlp, g_fin, b_gate, conv_w):
    conv_rows = jnp.pad(conv_w.reshape(3, HEAD), ((0, 0), (0, D - HEAD)))
    return jnp.concatenate([b_ada.reshape(6, D), g_mix.reshape(1, D), g_mlp.reshape(1, D), g_fin.reshape(1, D),
                            b_gate.reshape(2, D), conv_rows, jnp.zeros((2, D), F32)], axis=0)


def kernel(x, c, w_ada, b_ada, g_norm_mix, w_in, b_gate, conv_w, w_branch_attn, w_branch_conv, w_out, g_norm_mlp, w_mlp_in, w_mlp_out, g_norm_final, loss_target, m_w_ada, m_b_ada, m_g_norm_mix, m_w_in, m_b_gate, m_conv_w, m_w_branch_attn, m_w_branch_conv, m_w_out, m_g_norm_mlp, m_w_mlp_in, m_w_mlp_out, m_g_norm_final, v_w_ada, v_b_ada, v_g_norm_mix, v_w_in, v_b_gate, v_conv_w, v_w_branch_attn, v_w_branch_conv, v_w_out, v_g_norm_mlp, v_w_mlp_in, v_w_mlp_out, v_g_norm_final):
    S = x.shape[1]
    xi, yi, ci = _my_place()
    me = _dev_index(xi, yi, ci)
    x2 = x.reshape(S, D)
    tgt = loss_target.reshape(S, D)

    pay = jnp.zeros((8, D), F32).at[0].set(c[0]).at[1:4, :HEAD].set(conv_w[0])
    ncol = w_ada.shape[2]
    b_cols = lax.dynamic_slice(b_ada, (0, me * ncol), (1, ncol))
    got, act, mod_all = _conditioning(pay, w_ada[0], b_cols)
    cw8 = jnp.pad(got[:, 1:4, :HEAD].transpose(1, 0, 2).reshape(3, D), ((0, 5), (0, 0)))

    w_in_shard, mod_all = lax.optimization_barrier((w_in[0].T.astype(BF16), mod_all))
    mod = lax.dynamic_index_in_dim(mod_all, me, axis=1, keepdims=False).reshape(6, D)
    (w_int,) = _allgather_weights([w_in_shard])
    late = [w_branch_attn[0].T.astype(BF16), w_branch_conv[0].astype(BF16), w_out[0].astype(BF16),
            w_mlp_in[0].T.astype(BF16), w_mlp_out[0].astype(BF16)]
    w_int, late = lax.optimization_barrier((w_int, late))
    zones = [lax.dynamic_update_slice(lax.empty((N_DEV * t.shape[0], t.shape[1]), BF16), t, (me * t.shape[0], 0)) for t in late]
    ag_mix = _split_start("gather_mix_start", "gather", late[:3], zones[:3])
    ag_mlp = _split_start("gather_mlp_start", "gather", late[3:], zones[3:])
    mod = mod + ag_mix[4] + ag_mlp[4]
    h = _prenorm(x2, g_norm_mix, mod[1:2], mod[0:1])

    def mix_weights(o_attn):
        return _split_wait("gather_mix_wait", "gather", *ag_mix[:4], o_attn)[1]

    def mlp_weights(x1):
        return _split_wait("gather_mlp_wait", "gather", *ag_mlp[:4], x1)[1]

    rs = {}

    def mlp_grads_ready(*grads):
        lands = [lax.empty((N_PEER, t.shape[0] // N_DEV, t.shape[1]), BF16) for t in grads]
        rs["mlp"] = _split_start("scatter_mlp_start", "scatter", grads, lands)
        return rs["mlp"][4]

    def other_grads_ready(*grads):
        core = ci.reshape(1).astype(jnp.int32)
        pair = _pair_sums(grads, _sibling_exchange(grads), core)
        lands = [lax.empty((3,) + t.shape[1:], BF16) for t in pair]
        rs["rest"] = _split_start("scatter_rest_start", "chips", pair, lands)
        return rs["rest"][4]

    ba, bb = b_gate[:, :D], b_gate[:, D:]
    grad_x, vec = _local_step(
        x2, h, tgt, mod, g_norm_mix, g_norm_mlp, g_norm_final.reshape(1, D), ba, bb, cw8, w_int, mix_weights, mlp_weights,
        mlp_grads_ready, other_grads_ready)

    vec_all = _allgather_small(vec, "gather_vec")
    vec_sum = _sum_parts(vec_all, "sum_vec")
    loss = vec_sum[14, 0]
    gm_all = vec_all[:, 0:6, :].reshape(N_DEV, 6 * D)
    gm_cols = lax.dynamic_slice(gm_all, (0, me * ncol), (N_DEV, ncol))
    g_w_ada = _ada_bwd(act.T, gm_cols)
    conv_cols = lax.dynamic_slice(vec_sum[11:14], (0, me * HEAD), (3, HEAD))
    g_pack = jnp.concatenate([vec_sum[0:11], jnp.pad(conv_cols, ((0, 0), (0, D - HEAD))), jnp.zeros((2, D), F32)], axis=0)
    packs = [_pack_vectors(*t) for t in ((b_ada, g_norm_mix, g_norm_mlp, g_norm_final, b_gate, conv_w),
                                         (m_b_ada, m_g_norm_mix, m_g_norm_mlp, m_g_norm_final, m_b_gate, m_conv_w),
                                         (v_b_ada, v_g_norm_mix, v_g_norm_mlp, v_g_norm_final, v_b_gate, v_conv_w))]
    gv, dv, mv, vv = _adamw_vectors(packs[0], g_pack, packs[1], packs[2])
    d_ada, nm_ada, nv_ada = _adamw(w_ada[0], g_w_ada, m_w_ada[0], v_w_ada[0], "adamw_w_ada")

    big = {}
    srcs, lands = _split_wait("scatter_mlp_wait", "scatter", *rs["mlp"][:4], d_ada)
    own = [lax.dynamic_slice(g, (me * land.shape[1], 0), land.shape[1:]) for g, land in zip(srcs, lands)]
    g_mi = _sum_parts(lands[0], "sum_w_mi", own=own[0]).T
    big["w_mi"] = (g_mi[None],) + tuple(t[None] for t in _adamw(w_mlp_in[0], g_mi, m_w_mlp_in[0], v_w_mlp_in[0], "adamw_w_mi"))
    big["w_mo"] = tuple(t[None] for t in _sum_adamw(lands[1], own[1], w_mlp_out[0], m_w_mlp_out[0], v_w_mlp_out[0], "adamw_w_mo"))
    srcs, lands = _split_wait("scatter_rest_wait", "chips", *rs["rest"][:4], big["w_mo"][1])
    own = [lax.dynamic_index_in_dim(pair, 2 * xi + yi, axis=0, keepdims=False) for pair in srcs]
    big["w_in"] = tuple(t.T[None] for t in _sum_adamw(lands[0], own[0], w_in[0].T, m_w_in[0].T, v_w_in[0].T, "adamw_w_in"))
    g_ba = _sum_parts(lands[1], "sum_w_ba", own=own[1]).T
    big["w_ba"] = (g_ba[None],) + tuple(t[None] for t in _adamw(w_branch_attn[0], g_ba, m_w_branch_attn[0], v_w_branch_attn[0], "adamw_w_ba"))
    big["w_bc"] = tuple(t[None] for t in _sum_adamw(lands[2], own[2], w_branch_conv[0], m_w_branch_conv[0], v_w_branch_conv[0], "adamw_w_bc"))
    big["w_out"] = tuple(t[None] for t in _sum_adamw(lands[3], own[3], w_out[0], m_w_out[0], v_w_out[0], "adamw_w_out"))

    def ordered(k, ada, vecs):
        return (ada[None], vecs[0], vecs[1], big["w_in"][k], vecs[2], vecs[3], big["w_ba"][k], big["w_bc"][k],
                big["w_out"][k], vecs[4], big["w_mi"][k], big["w_mo"][k], vecs[5])

    return (loss, grad_x.reshape(1, S, D), *ordered(0, g_w_ada, gv), *ordered(1, d_ada, dv),
            *ordered(2, nm_ada, mv), *ordered(3, nv_ada, vv))
```

```python
import functools

import numpy as np
import jax
import jax.numpy as jnp
from jax import lax
from jax.experimental import pallas as pl
from jax.experimental.pallas import tpu as pltpu

F32, BF16 = jnp.float32, jnp.bfloat16
D = 1024
HEAD = 128
DILATIONS = (1, 4, 16)
N_SLOT = 4
AOW = N_SLOT * HEAD
DFF = 4 * D
N_DEV = 8
UNROLL = 16
EPS = 1e-6
NEG = -1e30
SCALE = HEAD ** -0.5
LR, B1, B2, ADAM_EPS, WD, STEP = 0.001, 0.9, 0.999, 1e-08, 0.01, 10
V7X_VMEM_LIMIT = 56 * 1024 * 1024
TM = 1024
MESH = pl.DeviceIdType.MESH
AXES = ("x", "y", "c")


def _cparams(*sem):
    if sem:
        return pltpu.CompilerParams(dimension_semantics=sem, vmem_limit_bytes=V7X_VMEM_LIMIT)
    return pltpu.CompilerParams(vmem_limit_bytes=V7X_VMEM_LIMIT)


def _nn(a, b):
    return jnp.dot(a, b, preferred_element_type=F32)


def _nt(a, b):
    return lax.dot_general(a, b, (((1,), (1,)), ((), ())), preferred_element_type=F32)


def _tn(a, b):
    return lax.dot_general(a, b, (((0,), (0,)), ((), ())), preferred_element_type=F32)


def _rms_r(x):
    return lax.rsqrt(jnp.mean(x * x, axis=-1, keepdims=True) + EPS)


def _rms_bwd(x, r, g, dn):
    gy = dn * g
    dx = r * gy - x * (r * r * r) * jnp.mean(x * gy, axis=-1, keepdims=True)
    return dx, dn * (x * r)


def _sigmoid(t):
    return 1.0 / (1.0 + jnp.exp(-t))


def _rowsum(v):
    return jnp.sum(v, axis=0, keepdims=True)


def _vec_spec(n=D):
    return pl.BlockSpec((1, n), lambda *_: (0, 0))


def _const_spec(shape):
    nd = len(shape)
    return pl.BlockSpec(shape, lambda *_: (0,) * nd)


def _win_rowblock(j):
    return jnp.where(j < 9, (j % 3) * 3 + j // 3, j)


def _prenorm(x, g, sc, sh):
    S = x.shape[0]
    tm = TM

    def body(x_ref, g_ref, sc_ref, sh_ref, h_ref):
        xv = x_ref[...]
        h_ref[...] = (xv * _rms_r(xv) * g_ref[...] * (1.0 + sc_ref[...]) + sh_ref[...]).astype(BF16)

    row = pl.BlockSpec((tm, D), lambda i: (i, 0))
    return pl.pallas_call(
        body, name="prenorm", grid=(S // tm,), in_specs=[row, _vec_spec(), _vec_spec(), _vec_spec()], out_specs=row,
        out_shape=jax.ShapeDtypeStruct((S, D), BF16), compiler_params=_cparams("parallel"),
    )(x, g, sc, sh)


def _proj(h, w_int):
    S = h.shape[0]

    def body(h_ref, w_ref, q_ref, e_ref):
        j = pl.program_id(0)
        acc = _nt(h_ref[...], w_ref[...])

        @pl.when(j < 9)
        def _():
            q_ref[0] = acc

        @pl.when(j >= 9)
        def _():
            e_ref[0] = acc.astype(BF16)

    def e_idx(j):
        k = jnp.maximum(j - 9, 0)
        return (k // 2, 0, k % 2)

    return pl.pallas_call(
        body, name="proj", grid=(19,),
        in_specs=[pl.BlockSpec((S, D), lambda j: (0, 0), pipeline_mode=pl.Buffered(1)),
                  pl.BlockSpec((512, D), lambda j: (_win_rowblock(j), 0))],
        out_specs=[pl.BlockSpec((1, S, 512), lambda j: (jnp.minimum(j, 8), 0, 0)), pl.BlockSpec((1, S, 512), e_idx)],
        out_shape=[jax.ShapeDtypeStruct((9, S, 512), F32), jax.ShapeDtypeStruct((5, S, D), BF16)],
        compiler_params=_cparams("arbitrary"),
    )(h, w_int)


def _bias_table():
    slopes = (2.0 ** (-8.0 * np.arange(1, 13, dtype=np.float32) / 12.0)).astype(np.float32)
    qi = np.arange(HEAD)[:, None]
    kj = np.arange(2 * HEAD)[None, :]
    delta = HEAD + qi - kj
    mask = (delta >= 0) & (delta <= HEAD)
    out = np.zeros((3, N_SLOT, HEAD, 2 * HEAD), np.float32)
    for gi, d in enumerate(DILATIONS):
        for j in range(N_SLOT):
            bias = -slopes[gi * N_SLOT + j] * (delta * d).astype(np.float32)
            out[gi, j] = np.where(mask, bias, NEG)
    out_t = np.concatenate([out[..., HEAD:].swapaxes(-1, -2), out[..., :HEAD].swapaxes(-1, -2)], axis=-1)
    return jnp.asarray(out), jnp.asarray(out_t)


def _block_rows(b, d):
    r = b % d
    n = b // d
    st = n * (HEAD * d) + r
    stp = jnp.maximum(n - 1, 0) * (HEAD * d) + r
    return n, st, stp


def _attn_fwd(qkv, bias):
    S = qkv.shape[2]
    nblk = S // HEAD
    rows = 256

    def body(qkv_ref, b_ref, o_ref, lse_ref, o_s, lse_s):
        g = pl.program_id(1)
        bias = b_ref[0, 0]
        col = lax.broadcasted_iota(jnp.int32, bias.shape, 1)
        bias_first = jnp.where(col < HEAD, NEG, bias)

        for gi, d in enumerate(DILATIONS):
            @pl.when(g == gi)
            def _(gi=gi, d=d):
                nb = nblk // d

                def keys(start):
                    sl = pl.ds(start, HEAD, stride=d)
                    return qkv_ref.at[0, 1][sl, :].astype(BF16), qkv_ref.at[0, 2][sl, :].astype(BF16)

                def step(b, first_of_residue, before):
                    r, n = b // nb, b % nb
                    cur = pl.ds(n * (HEAD * d) + r, HEAD, stride=d)
                    own = keys(n * (HEAD * d) + r)
                    if first_of_residue:
                        before = own
                    q = qkv_ref.at[0, 0][cur, :].astype(BF16)
                    kw = jnp.concatenate([before[0], own[0]], axis=0)
                    vw = jnp.concatenate([before[1], own[1]], axis=0)
                    s = _nt(q, kw) * SCALE + jnp.where(n > 0, bias, bias_first)
                    m = jnp.max(s, axis=-1, keepdims=True)
                    p = jnp.exp(s - m)
                    l = jnp.sum(p, axis=-1, keepdims=True)
                    o_s.at[gi][cur, :] = _nn(p.astype(BF16), vw) / l
                    lse_s.at[gi][cur, :] = jnp.broadcast_to(m + jnp.log(l), (HEAD, HEAD))
                    return own

                def steps(i, before):
                    for u in range(UNROLL):
                        before = step(i * UNROLL + u, nb <= UNROLL and u % nb == 0, before)
                    return before

                lax.fori_loop(0, nblk // UNROLL, steps, keys(0))

        @pl.when(g == len(DILATIONS) - 1)
        def _():
            def merge(i, carry):
                r = pl.ds(pl.multiple_of(i * rows, rows), rows)
                ls = [lse_s[k, r, :] for k in range(3)]
                top = jnp.maximum(jnp.maximum(ls[0], ls[1]), ls[2])
                ws = [jnp.exp(t - top) for t in ls]
                den = ws[0] + ws[1] + ws[2]
                o_ref[r, :] = (ws[0] * o_s[0, r, :] + ws[1] * o_s[1, r, :] + ws[2] * o_s[2, r, :]) / den
                lse_ref[r, :] = top + jnp.log(den)
                return carry

            lax.fori_loop(0, S // rows, merge, 0)

    return pl.pallas_call(
        body, name="attn_fwd", grid=(N_SLOT, 3),
        in_specs=[pl.BlockSpec((1, 3, S, HEAD), lambda j, g: (g, 0, 0, j)),
                  pl.BlockSpec((1, 1, HEAD, 2 * HEAD), lambda j, g: (g, j, 0, 0))],
        out_specs=[pl.BlockSpec((S, HEAD), lambda j, g: (0, j)), pl.BlockSpec((S, HEAD), lambda j, g: (0, j))],
        out_shape=[jax.ShapeDtypeStruct((S, AOW), F32), jax.ShapeDtypeStruct((S, AOW), F32)],
        scratch_shapes=[pltpu.VMEM((3, S, HEAD), F32)] * 2,
        compiler_params=_cparams("parallel", "arbitrary"),
    )(qkv, bias)


def _shift_down(z, k, halo_rows):
    out = pltpu.roll(z, k, axis=0)
    top = out[:8]
    rid = lax.broadcasted_iota(jnp.int32, top.shape, 0)
    for t in range(k):
        top = jnp.where(rid == t, halo_rows[t], top)
    return jnp.concatenate([top, out[8:]], axis=0)


def _shift_up(z, k, halo_rows):
    n = z.shape[0]
    out = pltpu.roll(z, n - k, axis=0)
    bottom = out[n - 8:]
    rid = lax.broadcasted_iota(jnp.int32, bottom.shape, 0)
    for t in range(k):
        bottom = jnp.where(rid == 8 - k + t, halo_rows[t], bottom)
    return jnp.concatenate([out[:n - 8], bottom], axis=0)


def _e_spec(chunk, tm):
    return pl.BlockSpec((1, tm, D), lambda i, c=chunk: (c, i, 0))


def _e_prev_spec(chunk, tm):
    return pl.BlockSpec((1, 16, D), lambda i, c=chunk: (c, jnp.maximum(i * (tm // 16) - 1, 0), 0))


def _e_next_spec(chunk, tm, S):
    return pl.BlockSpec((1, 16, D), lambda i, c=chunk: (c, jnp.minimum((i + 1) * (tm // 16), S // 16 - 1), 0))


def _mix(o_attn, e, cw8, ba, bb, w_bat, w_bc):
    S = o_attn.shape[0]
    tm = 256

    def body(o_ref, cb_ref, cc_ref, cx_ref, ga_ref, gb_ref, ccp_ref, cxp_ref, cw_ref, ba_ref, bb_ref, wba_ref, wbc_ref,
             obf_ref, cbu_ref, ya_ref, yc_ref, mg_ref):
        i = pl.program_id(0)
        o = o_ref[...].astype(BF16)
        obf_ref[...] = o
        ya = _nt(o, wba_ref[...])
        z = cc_ref[0].astype(F32) * cx_ref[0].astype(F32)
        zp = ccp_ref[0].astype(F32) * cxp_ref[0].astype(F32) * (i > 0).astype(F32)
        z1 = _shift_down(z, 1, [zp[15:16]])
        z2 = _shift_down(z, 2, [zp[14:15], zp[15:16]])
        cw = cw_ref[...]
        u = cw[0:1] * z2 + cw[1:2] * z1 + cw[2:3] * z
        cbu = (cb_ref[0].astype(F32) * u).astype(BF16)
        cbu_ref[...] = cbu
        yc = _nn(cbu, wbc_ref[...])
        sa = _sigmoid(ga_ref[0].astype(F32) + ba_ref[...])
        sb = _sigmoid(gb_ref[0].astype(F32) + bb_ref[...])
        ya_ref[...] = ya.astype(BF16)
        yc_ref[...] = yc.astype(BF16)
        mg_ref[...] = (sa * ya + sb * yc).astype(BF16)

    row = lambda w: pl.BlockSpec((tm, w), lambda i: (i, 0))
    return pl.pallas_call(
        body, name="mix", grid=(S // tm,),
        in_specs=[row(AOW)] + [_e_spec(c, tm) for c in range(5)] + [_e_prev_spec(1, tm), _e_prev_spec(2, tm),
                  _const_spec((8, D)), _vec_spec(), _vec_spec(), _const_spec((D, AOW)), _const_spec((D, D))],
        out_specs=[row(AOW), row(D), row(D), row(D), row(D)],
        out_shape=[jax.ShapeDtypeStruct((S, AOW), BF16)] + [jax.ShapeDtypeStruct((S, D), BF16)] * 4,
        compiler_params=_cparams("parallel"),
    )(o_attn, e, e, e, e, e, e, e, cw8, ba, bb, w_bat, w_bc)


def _out_proj(merged, w_out, x, gate1, g_mlp, sc2, sh2):
    S = x.shape[0]
    tm = TM

    def body(mg_ref, w_ref, x_ref, gt_ref, g_ref, sc_ref, sh_ref, x1_ref, mo_ref, h2_ref):
        mo = _nn(mg_ref[...], w_ref[...])
        mo_ref[...] = mo.astype(BF16)
        x1 = x_ref[...] + gt_ref[...] * mo
        x1_ref[...] = x1
        h2 = x1 * _rms_r(x1) * g_ref[...] * (1.0 + sc_ref[...]) + sh_ref[...]
        h2_ref[...] = h2.astype(BF16)

    row = pl.BlockSpec((tm, D), lambda i: (i, 0))
    return pl.pallas_call(
        body, name="out_proj", grid=(S // tm,),
        in_specs=[row, _const_spec((D, D)), row, _vec_spec(), _vec_spec(), _vec_spec(), _vec_spec()],
        out_specs=[row, row, row],
        out_shape=[jax.ShapeDtypeStruct((S, D), F32), jax.ShapeDtypeStruct((S, D), BF16), jax.ShapeDtypeStruct((S, D), BF16)],
        compiler_params=_cparams("parallel"),
    )(merged, w_out, x, gate1, g_mlp, sc2, sh2)


def _mlp_in(h2, w_mit):
    S = h2.shape[0]
    tm, tn = TM, 2048

    def body(h_ref, w_ref, a_ref, f_ref):
        a = _nt(h_ref[...], w_ref[...])
        a_ref[...] = a.astype(BF16)
        f_ref[...] = jnp.square(jnp.maximum(a, 0.0)).astype(BF16)

    blk = pl.BlockSpec((tm, tn), lambda i, j: (i, j))
    return pl.pallas_call(
        body, name="mlp_in", grid=(S // tm, DFF // tn),
        in_specs=[pl.BlockSpec((tm, D), lambda i, j: (i, 0)), pl.BlockSpec((tn, D), lambda i, j: (j, 0))],
        out_specs=[blk, blk],
        out_shape=[jax.ShapeDtypeStruct((S, DFF), BF16)] * 2,
        compiler_params=_cparams("parallel", "parallel"),
    )(h2, w_mit)


def _mlp_out(f, w_mo, x1, gate2, g_fin, tgt):
    S = x1.shape[0]
    tm = 512
    half = tm // 2

    def body(f_ref, w_ref, x1_ref, gt_ref, g_ref, t_ref, mlp_ref, dx2_ref, pv_ref):
        @pl.when(pl.program_id(0) == 0)
        def _():
            pv_ref[...] = jnp.zeros_like(pv_ref)

        g = g_ref[...]
        for hs in (pl.ds(0, half), pl.ds(half, half)):
            mlp = _nn(f_ref[hs, :], w_ref[...])
            mlp_ref[hs, :] = mlp.astype(BF16)
            x2 = x1_ref[hs, :] + gt_ref[...] * mlp
            r = _rms_r(x2)
            err = x2 * r * g - t_ref[hs, :]
            dx2, pg = _rms_bwd(x2, r, g, err * (1.0 / D))
            dx2_ref[hs, :] = dx2
            pv_ref[0:1, :] += _rowsum(pg)
            pv_ref[1:2, :] += 0.5 * _rowsum(jnp.mean(err * err, axis=-1, keepdims=True))

    row = pl.BlockSpec((tm, D), lambda i: (i, 0))
    return pl.pallas_call(
        body, name="mlp_out", grid=(S // tm,),
        in_specs=[pl.BlockSpec((tm, DFF), lambda i: (i, 0)), _const_spec((DFF, D)), row, _vec_spec(), _vec_spec(), row],
        out_specs=[row, row, _const_spec((8, D))],
        out_shape=[jax.ShapeDtypeStruct((S, D), BF16), jax.ShapeDtypeStruct((S, D), F32), jax.ShapeDtypeStruct((8, D), F32)],
        compiler_params=_cparams("arbitrary"),
    )(f, w_mo, x1, gate2, g_fin, tgt)


def _bwd_mlp_a(dx2, gate2, mlp, w_mo, a):
    S = dx2.shape[0]
    tm = 512
    half = tm // 2

    def body(dx_ref, gt_ref, mlp_ref, w_ref, a_ref, da_ref, dmo_ref, pv_ref):
        @pl.when(pl.program_id(0) == 0)
        def _():
            pv_ref[...] = jnp.zeros_like(pv_ref)

        for hs in (pl.ds(0, half), pl.ds(half, half)):
            dx = dx_ref[hs, :]
            dmo = (dx * gt_ref[...]).astype(BF16)
            dmo_ref[hs, :] = dmo
            pv_ref[0:1, :] += _rowsum(dx * mlp_ref[hs, :].astype(F32))
            df = _nt(dmo, w_ref[...])
            da_ref[hs, :] = (df * (2.0 * jnp.maximum(a_ref[hs, :].astype(F32), 0.0))).astype(BF16)

    row = pl.BlockSpec((tm, D), lambda i: (i, 0))
    wide = pl.BlockSpec((tm, DFF), lambda i: (i, 0))
    return pl.pallas_call(
        body, name="bwd_mlp_a", grid=(S // tm,),
        in_specs=[row, _vec_spec(), row, _const_spec((DFF, D)), wide],
        out_specs=[wide, row, _const_spec((8, D))],
        out_shape=[jax.ShapeDtypeStruct((S, DFF), BF16), jax.ShapeDtypeStruct((S, D), BF16), jax.ShapeDtypeStruct((8, D), F32)],
        compiler_params=_cparams("arbitrary"),
    )(dx2, gate2, mlp, w_mo, a)


def _bwd_mlp_b(da, w_mit, x1, dx2, g_mlp, sc2):
    S = x1.shape[0]
    tm = 512
    half = tm // 2

    def body(da_ref, w_ref, x1_ref, dx2_ref, g_ref, sc_ref, dx1_ref, pv_ref):
        @pl.when(pl.program_id(0) == 0)
        def _():
            pv_ref[...] = jnp.zeros_like(pv_ref)

        g = g_ref[...]
        for hs in (pl.ds(0, half), pl.ds(half, half)):
            dh = _nn(da_ref[hs, :], w_ref[...])
            x1 = x1_ref[hs, :]
            r = _rms_r(x1)
            dxn, pg = _rms_bwd(x1, r, g, dh * (1.0 + sc_ref[...]))
            dx1_ref[hs, :] = dx2_ref[hs, :] + dxn
            pv_ref[0:1, :] += _rowsum(dh)
            pv_ref[1:2, :] += _rowsum(dh * (x1 * r * g))
            pv_ref[2:3, :] += _rowsum(pg)

    row = pl.BlockSpec((tm, D), lambda i: (i, 0))
    return pl.pallas_call(
        body, name="bwd_mlp_b", grid=(S // tm,),
        in_specs=[pl.BlockSpec((tm, DFF), lambda i: (i, 0)), _const_spec((DFF, D)), row, row, _vec_spec(), _vec_spec()],
        out_specs=[row, _const_spec((8, D))],
        out_shape=[jax.ShapeDtypeStruct((S, D), F32), jax.ShapeDtypeStruct((8, D), F32)],
        compiler_params=_cparams("arbitrary"),
    )(da, w_mit, x1, dx2, g_mlp, sc2)


def _bwd_mix(dx1, gate1, mo, e, cw8, ba, bb, ya, yc, o_attn, w_out, w_bc, w_bat):
    S = dx1.shape[0]
    tm = 256
    n_tiles = S // tm

    def body(dx_ref, dxn_ref, gt_ref, mo_ref, cb_ref, cc_ref, cx_ref, ga_ref, gb_ref, cbn_ref, gbn_ref, ccp_ref, cxp_ref,
             cw_ref, ba_ref, bb_ref, ya_ref, yc_ref, o_ref, wout_ref, wbc_ref, wba_ref,
             dmo_ref, dya_ref, dyc_ref, do_ref, dl_ref, de_ref, pv_ref):
        i = pl.program_id(0)

        @pl.when(i == 0)
        def _():
            pv_ref[...] = jnp.zeros_like(pv_ref)

        gate = gt_ref[...]
        bbv = bb_ref[...]

        def conv_branch_grad(dx_rows, gb_rows):
            dmo = (dx_rows * gate).astype(BF16)
            dmg = _nt(dmo, wout_ref[...])
            sb = _sigmoid(gb_rows + bbv)
            dyc = dmg * sb
            return dmo, dmg, sb, dyc, _nt(dyc.astype(BF16), wbc_ref[...])

        dx = dx_ref[...]
        cb = cb_ref[0].astype(F32)
        cc = cc_ref[0].astype(F32)
        cx = cx_ref[0].astype(F32)
        dmo, dmg, sb, dyc, dcbu = conv_branch_grad(dx, gb_ref[0].astype(F32))
        dmo_ref[...] = dmo
        pv_ref[0:1, :] += _rowsum(dx * mo_ref[...].astype(F32))
        sa = _sigmoid(ga_ref[0].astype(F32) + ba_ref[...])
        dya = (dmg * sa).astype(BF16)
        dya_ref[...] = dya
        dyc_ref[...] = dyc.astype(BF16)
        dga = dmg * ya_ref[...].astype(F32) * sa * (1.0 - sa)
        dgb = dmg * yc_ref[...].astype(F32) * sb * (1.0 - sb)
        pv_ref[1:2, :] += _rowsum(dga)
        pv_ref[2:3, :] += _rowsum(dgb)

        do = _nn(dya, wba_ref[...])
        do_ref[...] = do
        prod = do * o_ref[...]
        dl_ref[...] = jnp.concatenate(
            [jnp.broadcast_to(jnp.sum(prod[:, s * HEAD:(s + 1) * HEAD], axis=-1, keepdims=True), (tm, HEAD))
             for s in range(N_SLOT)], axis=1)

        z = cc * cx
        zp = ccp_ref[0].astype(F32) * cxp_ref[0].astype(F32) * (i > 0).astype(F32)
        z1 = _shift_down(z, 1, [zp[15:16]])
        z2 = _shift_down(z, 2, [zp[14:15], zp[15:16]])
        cw = cw_ref[...]
        u = cw[0:1] * z2 + cw[1:2] * z1 + cw[2:3] * z
        du = dcbu * cb
        dcbu_n = conv_branch_grad(dxn_ref[...], gbn_ref[0].astype(F32))[4]
        du_n = dcbu_n * cbn_ref[0].astype(F32) * (i < n_tiles - 1).astype(F32)
        du1 = _shift_up(du, 1, [du_n[0:1]])
        du2 = _shift_up(du, 2, [du_n[0:1], du_n[1:2]])
        dz = cw[2:3] * du + cw[1:2] * du1 + cw[0:1] * du2
        pv_ref[3:4, :] += _rowsum(du * z2)
        pv_ref[4:5, :] += _rowsum(du * z1)
        pv_ref[5:6, :] += _rowsum(du * z)

        de_ref[0] = (dcbu * u).astype(BF16)
        de_ref[1] = (dz * cx).astype(BF16)
        de_ref[2] = (dz * cc).astype(BF16)
        de_ref[3] = dga.astype(BF16)
        de_ref[4] = dgb.astype(BF16)

    row = lambda w: pl.BlockSpec((tm, w), lambda i: (i, 0))
    nxt = pl.BlockSpec((16, D), lambda i: (jnp.minimum((i + 1) * (tm // 16), S // 16 - 1), 0))
    return pl.pallas_call(
        body, name="bwd_mix", grid=(n_tiles,),
        in_specs=[row(D), nxt, _vec_spec(), row(D)] + [_e_spec(c, tm) for c in range(5)]
                 + [_e_next_spec(0, tm, S), _e_next_spec(4, tm, S), _e_prev_spec(1, tm), _e_prev_spec(2, tm),
                    _const_spec((8, D)), _vec_spec(), _vec_spec(), row(D), row(D), row(AOW),
                    _const_spec((D, D)), _const_spec((D, D)), _const_spec((D, AOW))],
        out_specs=[row(D), row(D), row(D), row(AOW), row(AOW), pl.BlockSpec((5, tm, D), lambda i: (0, i, 0)),
                   _const_spec((8, D))],
        out_shape=[jax.ShapeDtypeStruct((S, D), BF16)] * 3 + [jax.ShapeDtypeStruct((S, AOW), F32)] * 2
                  + [jax.ShapeDtypeStruct((5, S, D), BF16), jax.ShapeDtypeStruct((8, D), F32)],
        compiler_params=_cparams("arbitrary"),
    )(dx1, dx1, gate1, mo, e, e, e, e, e, e, e, e, e, cw8, ba, bb, ya, yc, o_attn, w_out, w_bc, w_bat)


def _attn_bwd(qkv, do, lse, dl, bias_t):
    S = qkv.shape[2]
    nblk = S // HEAD

    def body(qkv_ref, do_ref, lse_ref, dl_ref, b_ref, d_ref):
        g = pl.program_id(1)
        bias = b_ref[0, 0]
        col = lax.broadcasted_iota(jnp.int32, bias.shape, 1)
        bias_last = jnp.where(col >= HEAD, NEG, bias)
        eye = (lax.broadcasted_iota(jnp.int32, (HEAD, HEAD), 0) == lax.broadcasted_iota(jnp.int32, (HEAD, HEAD), 1)).astype(F32)

        def as_row(t):
            return jnp.sum(t * eye, axis=0, keepdims=True)

        for gi, d in enumerate(DILATIONS):
            @pl.when(g == gi)
            def _(d=d):
                nb = nblk // d

                def query_side(start):
                    sl = pl.ds(start, HEAD, stride=d)
                    return (qkv_ref.at[0, 0][sl, :].astype(BF16), do_ref[sl, :].astype(BF16),
                            as_row(lse_ref[sl, :]), as_row(dl_ref[sl, :]))

                def step(b, first_of_residue, carry):
                    dq_part, own = carry
                    r, n = b // nb, b % nb
                    cur = pl.ds(n * (HEAD * d) + r, HEAD, stride=d)
                    if first_of_residue:
                        own = query_side(r)
                    nxt = query_side(jnp.minimum(n + 1, nb - 1) * (HEAD * d) + r)
                    q2 = jnp.concatenate([own[0], nxt[0]], axis=0)
                    do2 = jnp.concatenate([own[1], nxt[1]], axis=0)
                    k = qkv_ref.at[0, 1][cur, :].astype(BF16)
                    v = qkv_ref.at[0, 2][cur, :].astype(BF16)
                    s = _nt(k, q2) * SCALE + jnp.where(n < nb - 1, bias, bias_last)
                    p = jnp.exp(s - jnp.concatenate([own[2], nxt[2]], axis=1))
                    d_ref.at[0, 2][cur, :] = _nn(p.astype(BF16), do2)
                    dp = _nt(v, do2)
                    ds = (p * (dp - jnp.concatenate([own[3], nxt[3]], axis=1)) * SCALE).astype(BF16)
                    d_ref.at[0, 1][cur, :] = _nn(ds, q2)
                    dq2 = _tn(ds, k)
                    d_ref.at[0, 0][cur, :] = dq2[:HEAD] + jnp.where(n > 0, dq_part, 0.0)
                    return dq2[HEAD:], nxt

                def steps(i, carry):
                    for u in range(UNROLL):
                        carry = step(i * UNROLL + u, nb <= UNROLL and u % nb == 0, carry)
                    return carry

                lax.fori_loop(0, nblk // UNROLL, steps, (jnp.zeros((HEAD, HEAD), F32), query_side(0)))

    col_blk = pl.BlockSpec((S, HEAD), lambda j, g: (0, j))
    qkv_blk = pl.BlockSpec((1, 3, S, HEAD), lambda j, g: (g, 0, 0, j))
    return pl.pallas_call(
        body, name="attn_bwd", grid=(N_SLOT, 3),
        in_specs=[qkv_blk, col_blk, col_blk, col_blk, pl.BlockSpec((1, 1, HEAD, 2 * HEAD), lambda j, g: (g, j, 0, 0))],
        out_specs=qkv_blk,
        out_shape=jax.ShapeDtypeStruct((3, 3, S, AOW), F32),
        compiler_params=_cparams("parallel", "arbitrary"),
    )(qkv, do, lse, dl, bias_t)


def _bwd_in(dqkv, de, w_int, x, dx1, g_mix, sc1):
    S = x.shape[0]
    tm = TM
    dqkv = dqkv.reshape(3, 3, S, AOW)

    def body(dq_ref, de_ref, wq_ref, wk_ref, wv_ref, wa_ref, wb_ref, x_ref, dx1_ref, g_ref, sc_ref, gx_ref, pv_ref):
        acc = gx_ref
        i, k = pl.program_id(0), pl.program_id(1)

        @pl.when((i == 0) & (k == 0))
        def _():
            pv_ref[...] = jnp.zeros_like(pv_ref)

        @pl.when(k == 0)
        def _():
            acc[...] = jnp.zeros_like(acc)

        @pl.when(k < 3)
        def _():
            lhs = jnp.concatenate([dq_ref[0, t].astype(BF16) for t in range(3)], axis=1)
            acc[...] += _nn(lhs, jnp.concatenate([wq_ref[...], wk_ref[...], wv_ref[...]], axis=0))

        @pl.when(k >= 3)
        def _():
            acc[...] += _nn(de_ref[0], jnp.concatenate([wa_ref[...], wb_ref[...]], axis=0))

        @pl.when(k == 7)
        def _():
            dh = acc[...]
            xv = x_ref[...]
            r = _rms_r(xv)
            g = g_ref[...]
            dxn, pg = _rms_bwd(xv, r, g, dh * (1.0 + sc_ref[...]))
            gx_ref[...] = dx1_ref[...] + dxn
            pv_ref[0:1, :] += _rowsum(dh)
            pv_ref[1:2, :] += _rowsum(dh * (xv * r * g))
            pv_ref[2:3, :] += _rowsum(pg)

    grp = lambda k: jnp.minimum(k, 2)
    chunk = lambda k: jnp.maximum(k - 3, 0)
    wblk = lambda f: pl.BlockSpec((512, D), lambda i, k: (f(k), 0))
    row = pl.BlockSpec((tm, D), lambda i, k: (i, 0))
    once = pl.BlockSpec((tm, D), lambda i, k: (i, 0), pipeline_mode=pl.Buffered(1))
    return pl.pallas_call(
        body, name="bwd_in", grid=(S // tm, 8),
        in_specs=[pl.BlockSpec((1, 3, tm, 512), lambda i, k: (grp(k), 0, i, 0)),
                  pl.BlockSpec((1, tm, D), lambda i, k: (chunk(k), i, 0)),
                  wblk(grp), wblk(lambda k: 3 + grp(k)), wblk(lambda k: 6 + grp(k)),
                  wblk(lambda k: 9 + 2 * chunk(k)), wblk(lambda k: 10 + 2 * chunk(k)),
                  once, once, _vec_spec(), _vec_spec()],
        out_specs=[row, _const_spec((8, D))],
        out_shape=[jax.ShapeDtypeStruct((S, D), F32), jax.ShapeDtypeStruct((8, D), F32)],
        compiler_params=_cparams("arbitrary", "arbitrary"),
    )(dqkv, de, w_int, w_int, w_int, w_int, w_int, x, dx1, g_mix, sc1)


def _grad_w(name, a, b):
    S, ka = a.shape
    nb = b.shape[1]

    def body(a_ref, b_ref, o_ref):
        o_ref[...] = _tn(a_ref[...], b_ref[...]).astype(BF16)

    return pl.pallas_call(
        body, name=name, grid=(ka // 512,),
        in_specs=[pl.BlockSpec((S, 512), lambda n: (0, n)), pl.BlockSpec((S, nb), lambda n: (0, 0))],
        out_specs=pl.BlockSpec((512, nb), lambda n: (n, 0)),
        out_shape=jax.ShapeDtypeStruct((ka, nb), BF16),
        compiler_params=_cparams("parallel"),
    )(a, b)


def _grad_w_small(dya, o_bf, cbu, dyc, merged, dmo, after):
    S = dya.shape[0]

    def body(dya_ref, o_ref, cbu_ref, dyc_ref, mg_ref, dmo_ref, after_ref, gba_ref, gbc_ref, gout_ref):
        gba_ref[...] = _tn(dya_ref[...], o_ref[...]).astype(BF16)
        gbc_ref[...] = _tn(cbu_ref[...], dyc_ref[...]).astype(BF16)
        gout_ref[...] = _tn(mg_ref[...], dmo_ref[...]).astype(BF16)

    a_blk = pl.BlockSpec((S, 512), lambda n: (0, n))
    whole = lambda w: pl.BlockSpec((S, w), lambda n: (0, 0))
    out = lambda w: pl.BlockSpec((512, w), lambda n: (n, 0))
    return pl.pallas_call(
        body, name="grad_w_small", grid=(D // 512,),
        in_specs=[a_blk, whole(AOW), a_blk, whole(D), a_blk, whole(D), pl.BlockSpec(memory_space=pl.ANY)],
        out_specs=[out(AOW), out(D), out(D)],
        out_shape=[jax.ShapeDtypeStruct((D, AOW), BF16), jax.ShapeDtypeStruct((D, D), BF16), jax.ShapeDtypeStruct((D, D), BF16)],
        compiler_params=_cparams("parallel"),
    )(dya, o_bf, cbu, dyc, merged, dmo, after)


def _grad_w_in(dqkv, de, h):
    S = h.shape[0]

    def body(dq_ref, de_ref, h_ref, o_ref):
        n = pl.program_id(0)

        @pl.when(n < 9)
        def _():
            o_ref[...] = _tn(dq_ref[0].astype(BF16), h_ref[...]).astype(BF16)

        @pl.when(n >= 9)
        def _():
            o_ref[...] = _tn(de_ref[0], h_ref[...]).astype(BF16)

    def e_idx(n):
        kk = jnp.maximum(n - 9, 0)
        return (kk // 2, 0, kk % 2)

    return pl.pallas_call(
        body, name="grad_w_in", grid=(19,),
        in_specs=[pl.BlockSpec((1, S, 512), lambda n: (jnp.minimum(n, 8), 0, 0)), pl.BlockSpec((1, S, 512), e_idx),
                  pl.BlockSpec((S, D), lambda n: (0, 0))],
        out_specs=pl.BlockSpec((512, D), lambda n: (_win_rowblock(n), 0)),
        out_shape=jax.ShapeDtypeStruct((19 * 512, D), BF16),
        compiler_params=_cparams("parallel"),
    )(dqkv, de, h)


def _local_step(x, h, tgt, mod, g_mix, g_mlp, g_fin, ba, bb, cw8, w_int, mix_weights, mlp_weights, mlp_grads_ready, w_in_grad_ready,
                other_grads_ready):
    S = x.shape[0]
    sh1, sc1, gt1, sh2, sc2, gt2 = [mod[k:k + 1] for k in range(6)]
    bias, bias_t = _bias_table()

    qkv, e = _proj(h, w_int)
    qkv = qkv.reshape(3, 3, S, AOW)
    o_attn, lse = _attn_fwd(qkv, bias)
    w_bat, w_bc, w_out = mix_weights(o_attn)
    o_bf, cbu, ya, yc, merged = _mix(o_attn, e, cw8, ba, bb, w_bat, w_bc)
    x1, mo, h2 = _out_proj(merged, w_out, x, gt1, g_mlp, sc2, sh2)
    w_mit, w_mo = mlp_weights(x1)
    a, f = _mlp_in(h2, w_mit)
    mlp, dx2, pv_f = _mlp_out(f, w_mo, x1, gt2, g_fin, tgt)

    da, dmo2, pv_a = _bwd_mlp_a(dx2, gt2, mlp, w_mo, a)
    dx1, pv_b = _bwd_mlp_b(da, w_mit, x1, dx2, g_mlp, sc2)
    zero = mlp_grads_ready(_grad_w("grad_w_mi", da, h2), _grad_w("grad_w_mo", f, dmo2))
    dmo, dya, dyc, do, dl, de, pv_m = _bwd_mix(dx1, gt1 + zero, mo, e, cw8, ba, bb, ya, yc, o_attn, w_out, w_bc, w_bat)
    dqkv = _attn_bwd(qkv, do, lse, dl, bias_t).reshape(9, S, AOW)
    after = w_in_grad_ready(_grad_w_in(dqkv, de, h))
    zero = other_grads_ready(*_grad_w_small(dya, o_bf, cbu, dyc, merged, dmo, after))
    grad_x, pv_i = _bwd_in(dqkv, de, w_int, x, dx1, g_mix, sc1 + zero)

    vec = jnp.concatenate([pv_i[0:2], pv_m[0:1], pv_b[0:2], pv_a[0:1], pv_i[2:3], pv_b[2:3], pv_f[0:1],
                           pv_m[1:3], pv_m[3:6], pv_f[1:2], jnp.zeros((1, D), F32)], axis=0)
    return grad_x, vec


def _my_place():
    return lax.axis_index("x"), lax.axis_index("y"), lax.axis_index("c")


def _dev_index(px, py, pc):
    return 4 * px + 2 * py + pc


def _allgather_weights(shards):
    nw = len(shards)
    HBM = pl.BlockSpec(memory_space=pl.ANY)

    def body(*refs):
        sh, full = refs[:nw], refs[nw:2 * nw]
        send_sems, recv_sems, local_sems = refs[2 * nw:]
        x, y, c = _my_place()
        me, sibling = (x, y, c), (x, y, 1 - c)
        chips = [(1 - x, y), (x, 1 - y), (1 - x, 1 - y)]

        def rows(w, px, py, pc):
            r = sh[w].shape[0]
            return full[w].at[pl.ds(pl.multiple_of(_dev_index(px, py, pc) * r, 16), r), :]

        def copy(w, k, block, to, src=None):
            return pltpu.make_async_remote_copy(
                src_ref=rows(w, *block) if src is None else src, dst_ref=rows(w, *block),
                send_sem=send_sems.at[w, k], recv_sem=recv_sems.at[w, k], device_id=to, device_id_type=MESH)

        mine = [pltpu.make_async_copy(sh[w], rows(w, *me), local_sems.at[w]) for w in range(nw)]
        for cp in mine:
            cp.start()
        first = []
        for w in range(nw):
            first.append(copy(w, 0, me, sibling, src=sh[w]))
            first += [copy(w, 1 + j, me, (*chip, c), src=sh[w]) for j, chip in enumerate(chips)]
        for cp in first:
            cp.start()
        passed = []
        for w in range(nw):
            for j, chip in enumerate(chips):
                copy(w, 1 + j, (*chip, c), me).wait_recv()
                fwd = copy(w, 4 + j, (*chip, c), sibling)
                fwd.start()
                passed.append(fwd)
        for w in range(nw):
            copy(w, 0, sibling, me).wait_recv()
            for j, chip in enumerate(chips):
                copy(w, 4 + j, (*chip, 1 - c), me).wait_recv()
        for cp in first + passed:
            cp.wait_send()
        for cp in mine:
            cp.wait()

    return pl.pallas_call(
        body, name="allgather_weights",
        out_shape=[jax.ShapeDtypeStruct((N_DEV * s.shape[0], s.shape[1]), s.dtype) for s in shards],
        in_specs=[HBM] * nw, out_specs=[HBM] * nw,
        scratch_shapes=[pltpu.SemaphoreType.DMA((nw, 7)), pltpu.SemaphoreType.DMA((nw, 7)), pltpu.SemaphoreType.DMA((nw,))],
    )(*shards)


def _peer(x, y, c, m):
    return (x ^ ((m >> 2) & 1), y ^ ((m >> 1) & 1), c ^ (m & 1))


HBM_SPEC = pl.BlockSpec(memory_space=pltpu.HBM)
SEM_SPEC = pl.BlockSpec(memory_space=pltpu.SEMAPHORE)
N_PEER = N_DEV - 1


SPLIT_MASKS = {"gather": tuple(range(1, N_DEV)), "gather_near": (1, 2, 4, 6), "scatter": tuple(range(1, N_DEV)),
               "chips": (2, 4, 6), "sibling": (1, 1, 1, 1)}


def _split_copy(mode, src_ref, land_ref, send_sems, recv_sems, w, j, place, arriving=False):
    x, y, c = place
    masks = SPLIT_MASKS[mode]
    peer = _peer(x, y, c, masks[j])
    k = w * len(masks) + j
    sender, receiver = ((peer, (x, y, c)) if arriving else ((x, y, c), peer))
    if mode.startswith("gather"):
        r = src_ref.shape[0]
        src, dst = src_ref, land_ref.at[pl.ds(pl.multiple_of(_dev_index(*sender) * r, 16), r), :]
    elif mode == "scatter":
        r = land_ref.shape[1]
        src, dst = src_ref.at[pl.ds(pl.multiple_of(_dev_index(*receiver) * r, 16), r), :], land_ref.at[j]
    elif mode == "chips":
        src, dst = src_ref.at[2 * receiver[0] + receiver[1]], land_ref.at[j]
    else:
        r = land_ref.shape[1]
        src, dst = src_ref.at[pl.ds(pl.multiple_of((2 * j + receiver[2]) * r, 16), r), :], land_ref.at[j]
    return pltpu.make_async_remote_copy(src_ref=src, dst_ref=dst, send_sem=send_sems.at[k], recv_sem=recv_sems.at[k],
                                        device_id=peer, device_id_type=MESH)


def _split_start(name, mode, srcs, lands):
    n = len(srcs)
    nm = len(SPLIT_MASKS[mode])

    def body(*refs):
        src, land = refs[:n], refs[n:2 * n]
        send_sems, recv_sems = refs[2 * n], refs[2 * n + 1]
        token = refs[-1]
        place = _my_place()
        for w in range(n):
            for j in range(nm):
                _split_copy(mode, src[w], land[w], send_sems, recv_sems, w, j, place).start()
        token[...] = jnp.zeros_like(token)

    hbm = lambda t: pltpu.HBM(t.shape, t.dtype)
    out = pl.pallas_call(
        body, name=name,
        out_shape=(pltpu.SemaphoreType.DMA((n * nm,)), pltpu.SemaphoreType.DMA((n * nm,)), *[hbm(t) for t in srcs],
                   *[hbm(t) for t in lands], jax.ShapeDtypeStruct((8, 128), F32)),
        in_specs=(HBM_SPEC,) * (2 * n),
        out_specs=(SEM_SPEC, SEM_SPEC) + (HBM_SPEC,) * (2 * n) + (pl.BlockSpec(memory_space=pltpu.VMEM),),
        input_output_aliases={i: 2 + i for i in range(2 * n)},
        compiler_params=pltpu.CompilerParams(has_side_effects=pltpu.SideEffectType.DATAFLOW_SIDE_EFFECTING),
    )(*[pltpu.with_memory_space_constraint(t, pltpu.HBM) for t in (*srcs, *lands)])
    return out[0], out[1], out[2:2 + n], out[2 + n:2 + 2 * n], out[-1][0:1, 0:1], out[-1]


def _split_wait(name, mode, send_sems, recv_sems, srcs, lands, after):
    n = len(srcs)

    def body(*refs):
        src, land = refs[:n], refs[n:2 * n]
        ssem, rsem = refs[2 * n], refs[2 * n + 1]
        place = _my_place()
        for w in range(n):
            for j in range(len(SPLIT_MASKS[mode])):
                _split_copy(mode, src[w], land[w], ssem, rsem, w, j, place).wait_send()
                _split_copy(mode, src[w], land[w], ssem, rsem, w, j, place, arriving=True).wait_recv()

    hbm = lambda t: pltpu.HBM(t.shape, t.dtype)
    out = pl.pallas_call(
        body, name=name,
        out_shape=tuple(hbm(t) for t in (*srcs, *lands)),
        in_specs=(HBM_SPEC,) * (2 * n) + (SEM_SPEC, SEM_SPEC, pl.BlockSpec(memory_space=pl.ANY)),
        out_specs=(HBM_SPEC,) * (2 * n),
        input_output_aliases={i: i for i in range(2 * n)},
        compiler_params=pltpu.CompilerParams(has_side_effects=pltpu.SideEffectType.DATAFLOW_SIDE_EFFECTING),
    )(*srcs, *lands, send_sems, recv_sems, after)
    return out[:n], out[n:]


def _forward_copy(zone_ref, send_sems, recv_sems, j, place, arriving=False):
    x, y, c = place
    r = zone_ref.shape[0] // N_DEV
    chip = _peer(x, y, c, SPLIT_MASKS["chips"][j])
    owner = _dev_index(chip[0], chip[1], 1 - c if arriving else c)
    rows = zone_ref.at[pl.ds(pl.multiple_of(owner * r, 16), r), :]
    return pltpu.make_async_remote_copy(src_ref=rows, dst_ref=rows, send_sem=send_sems.at[j], recv_sem=recv_sems.at[j],
                                        device_id=(x, y, 1 - c), device_id_type=MESH)


def _forward_start(name, zone):
    def body(zone_ref, send_sems, recv_sems, zone_thru, token):
        place = _my_place()
        for j in range(3):
            _forward_copy(zone_ref, send_sems, recv_sems, j, place).start()
        token[...] = jnp.zeros_like(token)

    out = pl.pallas_call(
        body, name=name,
        out_shape=(pltpu.SemaphoreType.DMA((3,)), pltpu.SemaphoreType.DMA((3,)), pltpu.HBM(zone.shape, zone.dtype),
                   jax.ShapeDtypeStruct((8, 128), F32)),
        in_specs=(HBM_SPEC,), out_specs=(SEM_SPEC, SEM_SPEC, HBM_SPEC, pl.BlockSpec(memory_space=pltpu.VMEM)),
        input_output_aliases={0: 2},
        compiler_params=pltpu.CompilerParams(has_side_effects=pltpu.SideEffectType.DATAFLOW_SIDE_EFFECTING),
    )(pltpu.with_memory_space_constraint(zone, pltpu.HBM))
    return out[0], out[1], out[2], out[3]


def _forward_wait(name, send_sems, recv_sems, zone, after):
    def body(zone_ref, ssem, rsem, after_ref, zone_out):
        place = _my_place()
        for j in range(3):
            _forward_copy(zone_ref, ssem, rsem, j, place).wait_send()
            _forward_copy(zone_ref, ssem, rsem, j, place, arriving=True).wait_recv()

    return pl.pallas_call(
        body, name=name, out_shape=pltpu.HBM(zone.shape, zone.dtype),
        in_specs=(HBM_SPEC, SEM_SPEC, SEM_SPEC, pl.BlockSpec(memory_space=pl.ANY)), out_specs=HBM_SPEC,
        input_output_aliases={0: 0},
        compiler_params=pltpu.CompilerParams(has_side_effects=pltpu.SideEffectType.DATAFLOW_SIDE_EFFECTING),
    )(zone, send_sems, recv_sems, after)


def _sibling_exchange(grads):
    nw = len(grads)
    HBM = pl.BlockSpec(memory_space=pl.ANY)

    def body(*refs):
        g, land = refs[:nw], refs[nw:2 * nw]
        send_sems, recv_sems = refs[2 * nw:]
        x, y, c = _my_place()

        def copy(w, q, owner_core):
            r = land[w].shape[1]
            return pltpu.make_async_remote_copy(
                src_ref=g[w].at[pl.ds(pl.multiple_of((2 * q + owner_core) * r, 16), r), :], dst_ref=land[w].at[q],
                send_sem=send_sems.at[w, q], recv_sem=recv_sems.at[w, q], device_id=(x, y, 1 - c), device_id_type=MESH)

        sends = [copy(w, q, 1 - c) for w in range(nw) for q in range(4)]
        for cp in sends:
            cp.start()
        for w in range(nw):
            for q in range(4):
                copy(w, q, c).wait_recv()
        for cp in sends:
            cp.wait_send()

    return pl.pallas_call(
        body, name="sibling_exchange",
        out_shape=[jax.ShapeDtypeStruct((4, a.shape[0] // N_DEV, a.shape[1]), a.dtype) for a in grads],
        in_specs=[HBM] * nw, out_specs=[HBM] * nw,
        scratch_shapes=[pltpu.SemaphoreType.DMA((nw, 4)), pltpu.SemaphoreType.DMA((nw, 4))],
    )(*grads)


def _pair_sums(gs, sibs, core):
    n = len(gs)

    def body(core_ref, *refs):
        for w in range(n):
            refs[2 * n + w][0] = (refs[w][0, 0].astype(F32) + refs[n + w][0].astype(F32)).astype(BF16)

    in_specs = [pl.BlockSpec((1, 1) + t.shape[1:], lambda q, core_ref: (q, core_ref[0], 0, 0)) for t in sibs]
    in_specs += [pl.BlockSpec((1,) + t.shape[1:], lambda q, core_ref: (q, 0, 0)) for t in sibs]
    return pl.pallas_call(
        body, name="pair_sums",
        grid_spec=pltpu.PrefetchScalarGridSpec(
            num_scalar_prefetch=1, grid=(4,), in_specs=in_specs,
            out_specs=[pl.BlockSpec((1,) + t.shape[1:], lambda q, core_ref: (q, 0, 0)) for t in sibs]),
        out_shape=[jax.ShapeDtypeStruct(t.shape, BF16) for t in sibs],
        compiler_params=_cparams("parallel"),
    )(core, *[g.reshape(4, 2, t.shape[1], t.shape[2]) for g, t in zip(gs, sibs)], *sibs)


def _allgather_small(v, name):
    r, ccols = v.shape

    def body(v_ref, out_ref, send_sems, recv_sems):
        x, y, c = _my_place()
        my_idx = _dev_index(x, y, c)
        out_ref[my_idx] = v_ref[...]

        def copy(m):
            peer = _peer(x, y, c, m)
            return pltpu.make_async_remote_copy(
                src_ref=v_ref, dst_ref=out_ref.at[my_idx],
                send_sem=send_sems.at[m - 1], recv_sem=recv_sems.at[m - 1], device_id=peer, device_id_type=MESH)

        def arrival(m):
            peer = _peer(x, y, c, m)
            return pltpu.make_async_remote_copy(
                src_ref=v_ref, dst_ref=out_ref.at[_dev_index(*peer)],
                send_sem=send_sems.at[m - 1], recv_sem=recv_sems.at[m - 1], device_id=peer, device_id_type=MESH)

        sends = [copy(m) for m in range(1, N_DEV)]
        for cp in sends:
            cp.start()
        for m in range(1, N_DEV):
            arrival(m).wait_recv()
        for cp in sends:
            cp.wait_send()

    return pl.pallas_call(
        body, name=name,
        out_shape=jax.ShapeDtypeStruct((N_DEV, r, ccols), v.dtype),
        in_specs=[pl.BlockSpec(memory_space=pltpu.VMEM)], out_specs=pl.BlockSpec(memory_space=pltpu.VMEM),
        scratch_shapes=[pltpu.SemaphoreType.DMA((7,)), pltpu.SemaphoreType.DMA((7,))],
    )(v)


def _gather_w_in_and_condition(shard, pay, w_ada, b_cols):
    r, ccols = shard.shape
    ncol = w_ada.shape[1]

    def body(sh_ref, pay_ref, w_ref, b_ref, full_ref, got_ref, act_ref, mod_ref, send_sems, recv_sems, small_send, small_recv, local_sem):
        x, y, c = _my_place()
        me, sibling = (x, y, c), (x, y, 1 - c)
        my_idx = _dev_index(x, y, c)
        chips = [(1 - x, y), (x, 1 - y), (1 - x, 1 - y)]

        def small(rnd, buf, m, arriving=False):
            peer = _peer(x, y, c, m)
            slot = _dev_index(*peer) if arriving else my_idx
            return pltpu.make_async_remote_copy(
                src_ref=buf.at[my_idx], dst_ref=buf.at[slot], send_sem=small_send.at[rnd, m - 1],
                recv_sem=small_recv.at[rnd, m - 1], device_id=peer, device_id_type=MESH)

        def rows(px, py, pc):
            return full_ref.at[pl.ds(pl.multiple_of(_dev_index(px, py, pc) * r, 16), r), :]

        def copy(k, block, to, src=None):
            return pltpu.make_async_remote_copy(
                src_ref=rows(*block) if src is None else src, dst_ref=rows(*block),
                send_sem=send_sems.at[k], recv_sem=recv_sems.at[k], device_id=to, device_id_type=MESH)

        got_ref[my_idx] = pay_ref[...]
        round1 = [small(0, got_ref, m) for m in range(1, N_DEV)]
        for cp in round1:
            cp.start()
        mine = pltpu.make_async_copy(sh_ref, rows(*me), local_sem)
        mine.start()
        first = [copy(0, me, sibling, src=sh_ref)] + [copy(1 + j, me, (*chip, c), src=sh_ref) for j, chip in enumerate(chips)]
        for cp in first:
            cp.start()

        for m in range(1, N_DEV):
            small(0, got_ref, m, arriving=True).wait_recv()
        cv = jnp.concatenate([got_ref[s, 0:1, :] for s in range(N_DEV)], axis=0)
        act = cv * _sigmoid(cv)
        act_ref[...] = act
        mod_ref[my_idx] = jnp.dot(act, w_ref[...], preferred_element_type=F32, precision=lax.Precision.HIGHEST) + b_ref[...]
        round2 = [small(1, mod_ref, m) for m in range(1, N_DEV)]
        for cp in round2:
            cp.start()

        passed = []
        for j, chip in enumerate(chips):
            copy(1 + j, (*chip, c), me).wait_recv()
            fwd = copy(4 + j, (*chip, c), sibling)
            fwd.start()
            passed.append(fwd)
        copy(0, sibling, me).wait_recv()
        for j, chip in enumerate(chips):
            copy(4 + j, (*chip, 1 - c), me).wait_recv()
        for m in range(1, N_DEV):
            small(1, mod_ref, m, arriving=True).wait_recv()
        for cp in first + passed + round1 + round2:
            cp.wait_send()
        mine.wait()

    anyspec = pl.BlockSpec(memory_space=pl.ANY)
    vmem = pl.BlockSpec(memory_space=pltpu.VMEM)
    return pl.pallas_call(
        body, name="gather_w_in_and_condition",
        out_shape=[jax.ShapeDtypeStruct((N_DEV * r, ccols), shard.dtype), jax.ShapeDtypeStruct((N_DEV, 8, D), F32),
                   jax.ShapeDtypeStruct((N_DEV, D), F32), jax.ShapeDtypeStruct((N_DEV, N_DEV, ncol), F32)],
        in_specs=[anyspec, vmem, vmem, vmem], out_specs=[anyspec, vmem, vmem, vmem],
        scratch_shapes=[pltpu.SemaphoreType.DMA((7,)), pltpu.SemaphoreType.DMA((7,)), pltpu.SemaphoreType.DMA((2, 7)),
                        pltpu.SemaphoreType.DMA((2, 7)), pltpu.SemaphoreType.DMA],
        compiler_params=_cparams(),
    )(shard, pay, w_ada, b_cols)


def _ada_bwd(act_t, gm_cols):
    def body(a_ref, g_ref, o_ref):
        o_ref[...] = jnp.dot(a_ref[...], g_ref[...], preferred_element_type=F32, precision=lax.Precision.HIGHEST)

    return pl.pallas_call(
        body, name="ada_bwd", out_shape=jax.ShapeDtypeStruct((D, gm_cols.shape[1]), F32), compiler_params=_cparams(),
    )(act_t, gm_cols)


def _row_tile(r):
    for t in (256, 304, 128, 64, 16):
        if r % t == 0:
            return t
    return r


def _sum_parts(parts, name, own=None):
    k, r, ccols = parts.shape
    tr = _row_tile(r)

    def body(*refs):
        p_ref, o_ref = refs[0], refs[-1]
        acc = p_ref[0].astype(F32) if own is None else refs[1][...].astype(F32) + p_ref[0].astype(F32)
        for s in range(1, k):
            acc = acc + p_ref[s].astype(F32)
        o_ref[...] = acc

    blk = pl.BlockSpec((tr, ccols), lambda i: (i, 0))
    return pl.pallas_call(
        body, name=name, grid=(r // tr,),
        in_specs=[pl.BlockSpec((k, tr, ccols), lambda i: (0, i, 0))] + ([] if own is None else [blk]),
        out_specs=blk,
        out_shape=jax.ShapeDtypeStruct((r, ccols), F32),
        compiler_params=_cparams("parallel"),
    )(*((parts,) if own is None else (parts, own)))


def _adamw(w, g, m, v, name):
    r, ccols = w.shape
    tr = _row_tile(r)
    c1 = 1.0 / (1.0 - B1 ** STEP)
    c2 = 1.0 / (1.0 - B2 ** STEP)

    def body(w_ref, g_ref, m_ref, v_ref, d_ref, nm_ref, nv_ref):
        gv = g_ref[...]
        nm = B1 * m_ref[...] + (1.0 - B1) * gv
        nv = B2 * v_ref[...] + (1.0 - B2) * jnp.square(gv)
        nm_ref[...] = nm
        nv_ref[...] = nv
        d_ref[...] = -LR * ((nm * c1) / (jnp.sqrt(nv * c2) + ADAM_EPS) + WD * w_ref[...])

    blk = pl.BlockSpec((tr, ccols), lambda i: (i, 0))
    return pl.pallas_call(
        body, name=name, grid=(r // tr,), in_specs=[blk] * 4, out_specs=[blk] * 3,
        out_shape=[jax.ShapeDtypeStruct((r, ccols), F32)] * 3,
        compiler_params=_cparams("parallel"),
    )(w, g, m, v)


def _sum_adamw(parts, own, w, m, v, name):
    k, r, ccols = parts.shape
    tr = _row_tile(r)
    c1 = 1.0 / (1.0 - B1 ** STEP)
    c2 = 1.0 / (1.0 - B2 ** STEP)

    def body(p_ref, own_ref, w_ref, m_ref, v_ref, g_ref, d_ref, nm_ref, nv_ref):
        gv = own_ref[...].astype(F32)
        for s in range(k):
            gv = gv + p_ref[s].astype(F32)
        g_ref[...] = gv
        nm = B1 * m_ref[...] + (1.0 - B1) * gv
        nv = B2 * v_ref[...] + (1.0 - B2) * jnp.square(gv)
        nm_ref[...] = nm
        nv_ref[...] = nv
        d_ref[...] = -LR * ((nm * c1) / (jnp.sqrt(nv * c2) + ADAM_EPS) + WD * w_ref[...])

    blk = pl.BlockSpec((tr, ccols), lambda i: (i, 0))
    return pl.pallas_call(
        body, name=name, grid=(r // tr,),
        in_specs=[pl.BlockSpec((k, tr, ccols), lambda i: (0, i, 0))] + [blk] * 4, out_specs=[blk] * 4,
        out_shape=[jax.ShapeDtypeStruct((r, ccols), F32)] * 4,
        compiler_params=_cparams("parallel"),
    )(parts, own, w, m, v)


VEC_ROWS = ((0, 6), (6, 7), (9, 11), (11, 14), (7, 8), (8, 9))


def _adamw_vectors(w, g, m, v):
    c1 = 1.0 / (1.0 - B1 ** STEP)
    c2 = 1.0 / (1.0 - B2 ** STEP)

    def put(refs, p):
        for ref, (lo, hi) in zip(refs, VEC_ROWS):
            if ref.shape == (3, HEAD):
                ref[...] = p[lo:hi, :HEAD]
            else:
                ref[...] = jnp.concatenate([p[k:k + 1] for k in range(lo, hi)], axis=1)

    def body(w_ref, g_ref, m_ref, v_ref, *outs):
        gv = g_ref[...]
        nm = B1 * m_ref[...] + (1.0 - B1) * gv
        nv = B2 * v_ref[...] + (1.0 - B2) * jnp.square(gv)
        delta = -LR * ((nm * c1) / (jnp.sqrt(nv * c2) + ADAM_EPS) + WD * w_ref[...])
        for kind, p in enumerate((gv, delta, nm, nv)):
            put(outs[6 * kind:6 * kind + 6], p)

    shapes = [(1, 6 * D), (1, D), (1, 2 * D), (3, HEAD), (1, D), (1, D)]
    out = pl.pallas_call(
        body, name="adamw_vectors", out_shape=[jax.ShapeDtypeStruct(sh, F32) for sh in shapes] * 4, compiler_params=_cparams(),
    )(w, g, m, v)
    fix = lambda t: (t[0], t[1], t[2], t[3][None], t[4], t[5].reshape(D))
    return [fix(out[6 * kind:6 * kind + 6]) for kind in range(4)]


def _pack_vectors(b_ada, g_mix, g_mlp, g_fin, b_gate, conv_w):
    conv_rows = jnp.pad(conv_w.reshape(3, HEAD), ((0, 0), (0, D - HEAD)))
    return jnp.concatenate([b_ada.reshape(6, D), g_mix.reshape(1, D), g_mlp.reshape(1, D), g_fin.reshape(1, D),
                            b_gate.reshape(2, D), conv_rows, jnp.zeros((2, D), F32)], axis=0)


def kernel(x, c, w_ada, b_ada, g_norm_mix, w_in, b_gate, conv_w, w_branch_attn, w_branch_conv, w_out, g_norm_mlp, w_mlp_in, w_mlp_out, g_norm_final, loss_target, m_w_ada, m_b_ada, m_g_norm_mix, m_w_in, m_b_gate, m_conv_w, m_w_branch_attn, m_w_branch_conv, m_w_out, m_g_norm_mlp, m_w_mlp_in, m_w_mlp_out, m_g_norm_final, v_w_ada, v_b_ada, v_g_norm_mix, v_w_in, v_b_gate, v_conv_w, v_w_branch_attn, v_w_branch_conv, v_w_out, v_g_norm_mlp, v_w_mlp_in, v_w_mlp_out, v_g_norm_final):
    S = x.shape[1]
    xi, yi, ci = _my_place()
    me = _dev_index(xi, yi, ci)
    x2 = x.reshape(S, D)
    tgt = loss_target.reshape(S, D)

    pay = jnp.zeros((8, D), F32).at[0].set(c[0]).at[1:4, :HEAD].set(conv_w[0])
    ncol = w_ada.shape[2]
    b_cols = lax.dynamic_slice(b_ada, (0, me * ncol), (1, ncol))
    w_int, got, act, mod_all = _gather_w_in_and_condition(w_in[0].T.astype(BF16), pay, w_ada[0], b_cols)
    cw8 = jnp.pad(got[:, 1:4, :HEAD].transpose(1, 0, 2).reshape(3, D), ((0, 5), (0, 0)))
    mod = lax.dynamic_index_in_dim(mod_all, me, axis=1, keepdims=False).reshape(6, D)
    late = [w_branch_attn[0].T.astype(BF16), w_branch_conv[0].astype(BF16), w_out[0].astype(BF16),
            w_mlp_in[0].T.astype(BF16), w_mlp_out[0].astype(BF16)]
    w_int, late = lax.optimization_barrier((w_int, late))
    zones = [lax.dynamic_update_slice(lax.empty((N_DEV * t.shape[0], t.shape[1]), BF16), t, (me * t.shape[0], 0)) for t in late]
    ag_mix = _split_start("gather_mix_start", "gather", late[:3], zones[:3])
    ag_mlp = _split_start("gather_mlp_start", "gather", late[3:], zones[3:])
    mod = mod + ag_mix[4] + ag_mlp[4]
    h = _prenorm(x2, g_norm_mix, mod[1:2], mod[0:1])

    def mix_weights(o_attn):
        return _split_wait("gather_mix_wait", "gather", *ag_mix[:4], o_attn)[1]

    def mlp_weights(x1):
        return _split_wait("gather_mlp_wait", "gather", *ag_mlp[:4], x1)[1]

    rs = {}

    def mlp_grads_ready(*grads):
        lands = [lax.empty((N_PEER, t.shape[0] // N_DEV, t.shape[1]), BF16) for t in grads]
        rs["mlp"] = _split_start("scatter_mlp_start", "scatter", grads, lands)
        return rs["mlp"][4]

    def w_in_grad_ready(g_in):
        r = g_in.shape[0] // N_DEV
        rs["sib"] = _split_start("sibling_w_in_start", "sibling", [g_in], [jnp.zeros((4, r, g_in.shape[1]), BF16)])
        return rs["sib"][5]

    def other_grads_ready(*small):
        core = ci.reshape(1).astype(jnp.int32)
        (g_in,), (sib_in,) = _split_wait("sibling_w_in_wait", "sibling", *rs["sib"][:4], small[0])
        pair = _pair_sums([g_in, *small], [sib_in, *_sibling_exchange(small)], core)
        lands = [lax.empty((3,) + t.shape[1:], BF16) for t in pair]
        rs["rest"] = _split_start("scatter_rest_start", "chips", pair, lands)
        return rs["rest"][4]

    ba, bb = b_gate[:, :D], b_gate[:, D:]
    grad_x, vec = _local_step(
        x2, h, tgt, mod, g_norm_mix, g_norm_mlp, g_norm_final.reshape(1, D), ba, bb, cw8, w_int, mix_weights, mlp_weights,
        mlp_grads_ready, w_in_grad_ready, other_grads_ready)

    vec_all = _allgather_small(vec, "gather_vec")
    vec_sum = _sum_parts(vec_all, "sum_vec")
    loss = vec_sum[14, 0]
    gm_all = vec_all[:, 0:6, :].reshape(N_DEV, 6 * D)
    gm_cols = lax.dynamic_slice(gm_all, (0, me * ncol), (N_DEV, ncol))
    g_w_ada = _ada_bwd(act.T, gm_cols)
    conv_cols = lax.dynamic_slice(vec_sum[11:14], (0, me * HEAD), (3, HEAD))
    g_pack = jnp.concatenate([vec_sum[0:11], jnp.pad(conv_cols, ((0, 0), (0, D - HEAD))), jnp.zeros((2, D), F32)], axis=0)
    packs = [_pack_vectors(*t) for t in ((b_ada, g_norm_mix, g_norm_mlp, g_norm_final, b_gate, conv_w),
                                         (m_b_ada, m_g_norm_mix, m_g_norm_mlp, m_g_norm_final, m_b_gate, m_conv_w),
                                         (v_b_ada, v_g_norm_mix, v_g_norm_mlp, v_g_norm_final, v_b_gate, v_conv_w))]
    gv, dv, mv, vv = _adamw_vectors(packs[0], g_pack, packs[1], packs[2])
    d_ada, nm_ada, nv_ada = _adamw(w_ada[0], g_w_ada, m_w_ada[0], v_w_ada[0], "adamw_w_ada")

    big = {}
    srcs, lands = _split_wait("scatter_mlp_wait", "scatter", *rs["mlp"][:4], d_ada)
    own = [lax.dynamic_slice(g, (me * land.shape[1], 0), land.shape[1:]) for g, land in zip(srcs, lands)]
    g_mi = _sum_parts(lands[0], "sum_w_mi", own=own[0]).T
    big["w_mi"] = (g_mi[None],) + tuple(t[None] for t in _adamw(w_mlp_in[0], g_mi, m_w_mlp_in[0], v_w_mlp_in[0], "adamw_w_mi"))
    big["w_mo"] = tuple(t[None] for t in _sum_adamw(lands[1], own[1], w_mlp_out[0], m_w_mlp_out[0], v_w_mlp_out[0], "adamw_w_mo"))
    srcs, lands = _split_wait("scatter_rest_wait", "chips", *rs["rest"][:4], big["w_mo"][1])
    own = [lax.dynamic_index_in_dim(pair, 2 * xi + yi, axis=0, keepdims=False) for pair in srcs]
    big["w_in"] = tuple(t.T[None] for t in _sum_adamw(lands[0], own[0], w_in[0].T, m_w_in[0].T, v_w_in[0].T, "adamw_w_in"))
    g_ba = _sum_parts(lands[1], "sum_w_ba", own=own[1]).T
    big["w_ba"] = (g_ba[None],) + tuple(t[None] for t in _adamw(w_branch_attn[0], g_ba, m_w_branch_attn[0], v_w_branch_attn[0], "adamw_w_ba"))
    big["w_bc"] = tuple(t[None] for t in _sum_adamw(lands[2], own[2], w_branch_conv[0], m_w_branch_conv[0], v_w_branch_conv[0], "adamw_w_bc"))
    big["w_out"] = tuple(t[None] for t in _sum_adamw(lands[3], own[3], w_out[0], m_w_out[0], v_w_out[0], "adamw_w_out"))

    def ordered(k, ada, vecs):
        return (ada[None], vecs[0], vecs[1], big["w_in"][k], vecs[2], vecs[3], big["w_ba"][k], big["w_bc"][k],
                big["w_out"][k], vecs[4], big["w_mi"][k], big["w_mo"][k], vecs[5])

    return (loss, grad_x.reshape(1, S, D), *ordered(0, g_w_ada, gv), *ordered(1, d_ada, dv),
            *ordered(2, nm_ada, mv), *ordered(3, nv_ada, vv))
```

```python
import numpy as np
import jax
import jax.numpy as jnp
from jax import lax
from jax.experimental import pallas as pl
from jax.experimental.pallas import tpu as pltpu

F32, BF16 = jnp.float32, jnp.bfloat16
D = 1024
HEAD = 128
DILATIONS = (1, 4, 16)
N_SLOT = 4
AOW = N_SLOT * HEAD
DFF = 4 * D
N_DEV = 8
UNROLL = 16
EPS = 1e-6
NEG = -1e30
SCALE = HEAD ** -0.5
LR, B1, B2, ADAM_EPS, WD, STEP = 0.001, 0.9, 0.999, 1e-08, 0.01, 10
V7X_VMEM_LIMIT = 56 * 1024 * 1024
TM = 1024
MESH = pl.DeviceIdType.MESH


def _cparams(*sem):
    if sem:
        return pltpu.CompilerParams(dimension_semantics=sem, vmem_limit_bytes=V7X_VMEM_LIMIT)
    return pltpu.CompilerParams(vmem_limit_bytes=V7X_VMEM_LIMIT)


def _nn(a, b):
    return jnp.dot(a, b, preferred_element_type=F32)


def _nt(a, b):
    return lax.dot_general(a, b, (((1,), (1,)), ((), ())), preferred_element_type=F32)


def _tn(a, b):
    return lax.dot_general(a, b, (((0,), (0,)), ((), ())), preferred_element_type=F32)


def _rms_r(x):
    return lax.rsqrt(jnp.mean(x * x, axis=-1, keepdims=True) + EPS)


def _rms_bwd(x, r, g, dn):
    gy = dn * g
    dx = r * gy - x * (r * r * r) * jnp.mean(x * gy, axis=-1, keepdims=True)
    return dx, dn * (x * r)


def _sigmoid(t):
    return 1.0 / (1.0 + jnp.exp(-t))


def _rowsum(v):
    return jnp.sum(v, axis=0, keepdims=True)


def _vec_spec(n=D):
    return pl.BlockSpec((1, n), lambda *_: (0, 0))


def _const_spec(shape):
    nd = len(shape)
    return pl.BlockSpec(shape, lambda *_: (0,) * nd)


def _win_rowblock(j):
    return jnp.where(j < 9, (j % 3) * 3 + j // 3, j)


def _prenorm(x, g, sc, sh):
    S = x.shape[0]
    tm = TM

    def body(x_ref, g_ref, sc_ref, sh_ref, h_ref):
        xv = x_ref[...]
        h_ref[...] = (xv * _rms_r(xv) * g_ref[...] * (1.0 + sc_ref[...]) + sh_ref[...]).astype(BF16)

    row = pl.BlockSpec((tm, D), lambda i: (i, 0))
    return pl.pallas_call(
        body, name="prenorm", grid=(S // tm,), in_specs=[row, _vec_spec(), _vec_spec(), _vec_spec()], out_specs=row,
        out_shape=jax.ShapeDtypeStruct((S, D), BF16), compiler_params=_cparams("parallel"),
    )(x, g, sc, sh)


def _proj(h, w_int):
    S = h.shape[0]

    def body(h_ref, w_ref, q_ref, e_ref):
        j = pl.program_id(0)
        acc = _nt(h_ref[...], w_ref[...])

        @pl.when(j < 9)
        def _():
            q_ref[0] = acc

        @pl.when(j >= 9)
        def _():
            e_ref[0] = acc.astype(BF16)

    def e_idx(j):
        k = jnp.maximum(j - 9, 0)
        return (k // 2, 0, k % 2)

    return pl.pallas_call(
        body, name="proj", grid=(19,),
        in_specs=[pl.BlockSpec((S, D), lambda j: (0, 0), pipeline_mode=pl.Buffered(1)),
                  pl.BlockSpec((512, D), lambda j: (_win_rowblock(j), 0))],
        out_specs=[pl.BlockSpec((1, S, 512), lambda j: (jnp.minimum(j, 8), 0, 0)), pl.BlockSpec((1, S, 512), e_idx)],
        out_shape=[jax.ShapeDtypeStruct((9, S, 512), F32), jax.ShapeDtypeStruct((5, S, D), BF16)],
        compiler_params=_cparams("arbitrary"),
    )(h, w_int)


def _bias_table():
    slopes = (2.0 ** (-8.0 * np.arange(1, 13, dtype=np.float32) / 12.0)).astype(np.float32)
    qi = np.arange(HEAD)[:, None]
    kj = np.arange(2 * HEAD)[None, :]
    delta = HEAD + qi - kj
    mask = (delta >= 0) & (delta <= HEAD)
    out = np.zeros((3, N_SLOT, HEAD, 2 * HEAD), np.float32)
    for gi, d in enumerate(DILATIONS):
        for j in range(N_SLOT):
            bias = -slopes[gi * N_SLOT + j] * (delta * d).astype(np.float32)
            out[gi, j] = np.where(mask, bias, NEG)
    out_t = np.concatenate([out[..., HEAD:].swapaxes(-1, -2), out[..., :HEAD].swapaxes(-1, -2)], axis=-1)
    return jnp.asarray(out), jnp.asarray(out_t)


def _attn_fwd(qkv, bias):
    S = qkv.shape[2]
    nblk = S // HEAD
    rows = 256

    def body(qkv_ref, b_ref, o_ref, lse_ref, o_s, lse_s):
        g = pl.program_id(1)
        bias = b_ref[0, 0]
        col = lax.broadcasted_iota(jnp.int32, bias.shape, 1)
        bias_first = jnp.where(col < HEAD, NEG, bias)

        for gi, d in enumerate(DILATIONS):
            @pl.when(g == gi)
            def _(gi=gi, d=d):
                nb = nblk // d

                def keys(start):
                    sl = pl.ds(start, HEAD, stride=d)
                    return qkv_ref.at[0, 1][sl, :].astype(BF16), qkv_ref.at[0, 2][sl, :].astype(BF16)

                def step(b, first_of_residue, before):
                    r, n = b // nb, b % nb
                    cur = pl.ds(n * (HEAD * d) + r, HEAD, stride=d)
                    own = keys(n * (HEAD * d) + r)
                    if first_of_residue:
                        before = own
                    q = qkv_ref.at[0, 0][cur, :].astype(BF16)
                    kw = jnp.concatenate([before[0], own[0]], axis=0)
                    vw = jnp.concatenate([before[1], own[1]], axis=0)
                    s = _nt(q, kw) * SCALE + jnp.where(n > 0, bias, bias_first)
                    m = jnp.max(s, axis=-1, keepdims=True)
                    p = jnp.exp(s - m)
                    l = jnp.sum(p, axis=-1, keepdims=True)
                    o_s.at[gi][cur, :] = _nn(p.astype(BF16), vw) / l
                    lse_s.at[gi][cur, :] = jnp.broadcast_to(m + jnp.log(l), (HEAD, HEAD))
                    return own

                def steps(i, before):
                    for u in range(UNROLL):
                        before = step(i * UNROLL + u, nb <= UNROLL and u % nb == 0, before)
                    return before

                lax.fori_loop(0, nblk // UNROLL, steps, keys(0))

        @pl.when(g == len(DILATIONS) - 1)
        def _():
            def merge(i, carry):
                r = pl.ds(pl.multiple_of(i * rows, rows), rows)
                ls = [lse_s[k, r, :] for k in range(3)]
                top = jnp.maximum(jnp.maximum(ls[0], ls[1]), ls[2])
                ws = [jnp.exp(t - top) for t in ls]
                den = ws[0] + ws[1] + ws[2]
                o_ref[r, :] = (ws[0] * o_s[0, r, :] + ws[1] * o_s[1, r, :] + ws[2] * o_s[2, r, :]) / den
                lse_ref[r, :] = top + jnp.log(den)
                return carry

            lax.fori_loop(0, S // rows, merge, 0)

    return pl.pallas_call(
        body, name="attn_fwd", grid=(N_SLOT, 3),
        in_specs=[pl.BlockSpec((1, 3, S, HEAD), lambda j, g: (g, 0, 0, j)),
                  pl.BlockSpec((1, 1, HEAD, 2 * HEAD), lambda j, g: (g, j, 0, 0))],
        out_specs=[pl.BlockSpec((S, HEAD), lambda j, g: (0, j)), pl.BlockSpec((S, HEAD), lambda j, g: (0, j))],
        out_shape=[jax.ShapeDtypeStruct((S, AOW), F32), jax.ShapeDtypeStruct((S, AOW), F32)],
        scratch_shapes=[pltpu.VMEM((3, S, HEAD), F32)] * 2,
        compiler_params=_cparams("parallel", "arbitrary"),
    )(qkv, bias)


def _shift_down(z, k, halo_rows):
    out = pltpu.roll(z, k, axis=0)
    top = out[:8]
    rid = lax.broadcasted_iota(jnp.int32, top.shape, 0)
    for t in range(k):
        top = jnp.where(rid == t, halo_rows[t], top)
    return jnp.concatenate([top, out[8:]], axis=0)


def _shift_up(z, k, halo_rows):
    n = z.shape[0]
    out = pltpu.roll(z, n - k, axis=0)
    bottom = out[n - 8:]
    rid = lax.broadcasted_iota(jnp.int32, bottom.shape, 0)
    for t in range(k):
        bottom = jnp.where(rid == 8 - k + t, halo_rows[t], bottom)
    return jnp.concatenate([out[:n - 8], bottom], axis=0)


def _e_spec(chunk, tm):
    return pl.BlockSpec((1, tm, D), lambda i, c=chunk: (c, i, 0))


def _e_prev_spec(chunk, tm):
    return pl.BlockSpec((1, 16, D), lambda i, c=chunk: (c, jnp.maximum(i * (tm // 16) - 1, 0), 0))


def _e_next_spec(chunk, tm, S):
    return pl.BlockSpec((1, 16, D), lambda i, c=chunk: (c, jnp.minimum((i + 1) * (tm // 16), S // 16 - 1), 0))


def _mix(o_attn, e, cw8, ba, bb, w_bat, w_bc):
    S = o_attn.shape[0]
    tm = 256

    def body(o_ref, cb_ref, cc_ref, cx_ref, ga_ref, gb_ref, ccp_ref, cxp_ref, cw_ref, ba_ref, bb_ref, wba_ref, wbc_ref,
             obf_ref, cbu_ref, ya_ref, yc_ref, mg_ref):
        i = pl.program_id(0)
        o = o_ref[...].astype(BF16)
        obf_ref[...] = o
        ya = _nt(o, wba_ref[...])
        z = cc_ref[0].astype(F32) * cx_ref[0].astype(F32)
        zp = ccp_ref[0].astype(F32) * cxp_ref[0].astype(F32) * (i > 0).astype(F32)
        z1 = _shift_down(z, 1, [zp[15:16]])
        z2 = _shift_down(z, 2, [zp[14:15], zp[15:16]])
        cw = cw_ref[...]
        u = cw[0:1] * z2 + cw[1:2] * z1 + cw[2:3] * z
        cbu = (cb_ref[0].astype(F32) * u).astype(BF16)
        cbu_ref[...] = cbu
        yc = _nn(cbu, wbc_ref[...])
        sa = _sigmoid(ga_ref[0].astype(F32) + ba_ref[...])
        sb = _sigmoid(gb_ref[0].astype(F32) + bb_ref[...])
        ya_ref[...] = ya.astype(BF16)
        yc_ref[...] = yc.astype(BF16)
        mg_ref[...] = (sa * ya + sb * yc).astype(BF16)

    row = lambda w: pl.BlockSpec((tm, w), lambda i: (i, 0))
    return pl.pallas_call(
        body, name="mix", grid=(S // tm,),
        in_specs=[row(AOW)] + [_e_spec(c, tm) for c in range(5)] + [_e_prev_spec(1, tm), _e_prev_spec(2, tm),
                  _const_spec((8, D)), _vec_spec(), _vec_spec(), _const_spec((D, AOW)), _const_spec((D, D))],
        out_specs=[row(AOW), row(D), row(D), row(D), row(D)],
        out_shape=[jax.ShapeDtypeStruct((S, AOW), BF16)] + [jax.ShapeDtypeStruct((S, D), BF16)] * 4,
        compiler_params=_cparams("parallel"),
    )(o_attn, e, e, e, e, e, e, e, cw8, ba, bb, w_bat, w_bc)


def _out_proj(merged, w_out, x, gate1, g_mlp, sc2, sh2):
    S = x.shape[0]
    tm = TM

    def body(mg_ref, w_ref, x_ref, gt_ref, g_ref, sc_ref, sh_ref, x1_ref, mo_ref, h2_ref):
        mo = _nn(mg_ref[...], w_ref[...])
        mo_ref[...] = mo.astype(BF16)
        x1 = x_ref[...] + gt_ref[...] * mo
        x1_ref[...] = x1
        h2 = x1 * _rms_r(x1) * g_ref[...] * (1.0 + sc_ref[...]) + sh_ref[...]
        h2_ref[...] = h2.astype(BF16)

    row = pl.BlockSpec((tm, D), lambda i: (i, 0))
    return pl.pallas_call(
        body, name="out_proj", grid=(S // tm,),
        in_specs=[row, _const_spec((D, D)), row, _vec_spec(), _vec_spec(), _vec_spec(), _vec_spec()],
        out_specs=[row, row, row],
        out_shape=[jax.ShapeDtypeStruct((S, D), F32), jax.ShapeDtypeStruct((S, D), BF16), jax.ShapeDtypeStruct((S, D), BF16)],
        compiler_params=_cparams("parallel"),
    )(merged, w_out, x, gate1, g_mlp, sc2, sh2)


def _mlp_in(h2, w_mit):
    S = h2.shape[0]
    tm = TM

    def body(h_ref, w_ref, f_ref):
        f_ref[...] = jnp.square(jnp.maximum(_nt(h_ref[...], w_ref[...]), 0.0)).astype(BF16)

    return pl.pallas_call(
        body, name="mlp_in", grid=(S // tm,),
        in_specs=[pl.BlockSpec((tm, D), lambda i: (i, 0)), _const_spec((DFF, D))],
        out_specs=pl.BlockSpec((tm, DFF), lambda i: (i, 0)),
        out_shape=jax.ShapeDtypeStruct((S, DFF), BF16),
        compiler_params=_cparams("parallel"),
    )(h2, w_mit)


def _mlp_out(f, w_mo, x1, gate2, g_fin, tgt):
    S = x1.shape[0]
    tm = 512
    half = tm // 2

    def body(f_ref, w_ref, x1_ref, gt_ref, g_ref, t_ref, mlp_ref, dx2_ref, pv_ref):
        @pl.when(pl.program_id(0) == 0)
        def _():
            pv_ref[...] = jnp.zeros_like(pv_ref)

        g = g_ref[...]
        for hs in (pl.ds(0, half), pl.ds(half, half)):
            mlp = _nn(f_ref[hs, :], w_ref[...])
            mlp_ref[hs, :] = mlp.astype(BF16)
            x2 = x1_ref[hs, :] + gt_ref[...] * mlp
            r = _rms_r(x2)
            err = x2 * r * g - t_ref[hs, :]
            dx2, pg = _rms_bwd(x2, r, g, err * (1.0 / D))
            dx2_ref[hs, :] = dx2
            pv_ref[0:1, :] += _rowsum(pg)
            pv_ref[1:2, :] += 0.5 * _rowsum(jnp.mean(err * err, axis=-1, keepdims=True))

    row = pl.BlockSpec((tm, D), lambda i: (i, 0))
    return pl.pallas_call(
        body, name="mlp_out", grid=(S // tm,),
        in_specs=[pl.BlockSpec((tm, DFF), lambda i: (i, 0)), _const_spec((DFF, D)), row, _vec_spec(), _vec_spec(), row],
        out_specs=[row, row, _const_spec((8, D))],
        out_shape=[jax.ShapeDtypeStruct((S, D), BF16), jax.ShapeDtypeStruct((S, D), F32), jax.ShapeDtypeStruct((8, D), F32)],
        compiler_params=_cparams("arbitrary"),
    )(f, w_mo, x1, gate2, g_fin, tgt)


def _bwd_mlp_a(dx2, gate2, mlp, w_mo, f):
    S = dx2.shape[0]
    tm = 512
    half = tm // 2

    def body(dx_ref, gt_ref, mlp_ref, w_ref, f_ref, da_ref, dmo_ref, pv_ref):
        @pl.when(pl.program_id(0) == 0)
        def _():
            pv_ref[...] = jnp.zeros_like(pv_ref)

        for hs in (pl.ds(0, half), pl.ds(half, half)):
            dx = dx_ref[hs, :]
            dmo = (dx * gt_ref[...]).astype(BF16)
            dmo_ref[hs, :] = dmo
            pv_ref[0:1, :] += _rowsum(dx * mlp_ref[hs, :].astype(F32))
            df = _nt(dmo, w_ref[...])
            da_ref[hs, :] = (df * (2.0 * jnp.sqrt(f_ref[hs, :].astype(F32)))).astype(BF16)

    row = pl.BlockSpec((tm, D), lambda i: (i, 0))
    wide = pl.BlockSpec((tm, DFF), lambda i: (i, 0))
    return pl.pallas_call(
        body, name="bwd_mlp_a", grid=(S // tm,),
        in_specs=[row, _vec_spec(), row, _const_spec((DFF, D)), wide],
        out_specs=[wide, row, _const_spec((8, D))],
        out_shape=[jax.ShapeDtypeStruct((S, DFF), BF16), jax.ShapeDtypeStruct((S, D), BF16), jax.ShapeDtypeStruct((8, D), F32)],
        compiler_params=_cparams("arbitrary"),
    )(dx2, gate2, mlp, w_mo, f)


def _bwd_mlp_b(da, w_mit, x1, dx2, g_mlp, sc2):
    S = x1.shape[0]
    tm = 512
    half = tm // 2

    def body(da_ref, w_ref, x1_ref, dx2_ref, g_ref, sc_ref, dx1_ref, pv_ref):
        @pl.when(pl.program_id(0) == 0)
        def _():
            pv_ref[...] = jnp.zeros_like(pv_ref)

        g = g_ref[...]
        for hs in (pl.ds(0, half), pl.ds(half, half)):
            dh = _nn(da_ref[hs, :], w_ref[...])
            x1 = x1_ref[hs, :]
            r = _rms_r(x1)
            dxn, pg = _rms_bwd(x1, r, g, dh * (1.0 + sc_ref[...]))
            dx1_ref[hs, :] = dx2_ref[hs, :] + dxn
            pv_ref[0:1, :] += _rowsum(dh)
            pv_ref[1:2, :] += _rowsum(dh * (x1 * r * g))
            pv_ref[2:3, :] += _rowsum(pg)

    row = pl.BlockSpec((tm, D), lambda i: (i, 0))
    return pl.pallas_call(
        body, name="bwd_mlp_b", grid=(S // tm,),
        in_specs=[pl.BlockSpec((tm, DFF), lambda i: (i, 0)), _const_spec((DFF, D)), row, row, _vec_spec(), _vec_spec()],
        out_specs=[row, _const_spec((8, D))],
        out_shape=[jax.ShapeDtypeStruct((S, D), F32), jax.ShapeDtypeStruct((8, D), F32)],
        compiler_params=_cparams("arbitrary"),
    )(da, w_mit, x1, dx2, g_mlp, sc2)


def _bwd_mix(dx1, gate1, mo, e, cw8, ba, bb, ya, yc, o_attn, w_out, w_bc, w_bat):
    S = dx1.shape[0]
    tm = 256
    n_tiles = S // tm

    def body(dx_ref, dxn_ref, gt_ref, mo_ref, cb_ref, cc_ref, cx_ref, ga_ref, gb_ref, cbn_ref, gbn_ref, ccp_ref, cxp_ref,
             cw_ref, ba_ref, bb_ref, ya_ref, yc_ref, o_ref, wout_ref, wbc_ref, wba_ref,
             dmo_ref, dya_ref, dyc_ref, do_ref, dl_ref, de_ref, pv_ref):
        i = pl.program_id(0)

        @pl.when(i == 0)
        def _():
            pv_ref[...] = jnp.zeros_like(pv_ref)

        gate = gt_ref[...]
        bbv = bb_ref[...]

        def conv_branch_grad(dx_rows, gb_rows):
            dmo = (dx_rows * gate).astype(BF16)
            dmg = _nt(dmo, wout_ref[...])
            sb = _sigmoid(gb_rows + bbv)
            dyc = dmg * sb
            return dmo, dmg, sb, dyc, _nt(dyc.astype(BF16), wbc_ref[...])

        dx = dx_ref[...]
        cb = cb_ref[0].astype(F32)
        cc = cc_ref[0].astype(F32)
        cx = cx_ref[0].astype(F32)
        dmo, dmg, sb, dyc, dcbu = conv_branch_grad(dx, gb_ref[0].astype(F32))
        dmo_ref[...] = dmo
        pv_ref[0:1, :] += _rowsum(dx * mo_ref[...].astype(F32))
        sa = _sigmoid(ga_ref[0].astype(F32) + ba_ref[...])
        dya = (dmg * sa).astype(BF16)
        dya_ref[...] = dya
        dyc_ref[...] = dyc.astype(BF16)
        dga = dmg * ya_ref[...].astype(F32) * sa * (1.0 - sa)
        dgb = dmg * yc_ref[...].astype(F32) * sb * (1.0 - sb)
        pv_ref[1:2, :] += _rowsum(dga)
        pv_ref[2:3, :] += _rowsum(dgb)

        do = _nn(dya, wba_ref[...])
        do_ref[...] = do
        prod = do * o_ref[...]
        dl_ref[...] = jnp.concatenate(
            [jnp.broadcast_to(jnp.sum(prod[:, s * HEAD:(s + 1) * HEAD], axis=-1, keepdims=True), (tm, HEAD))
             for s in range(N_SLOT)], axis=1)

        z = cc * cx
        zp = ccp_ref[0].astype(F32) * cxp_ref[0].astype(F32) * (i > 0).astype(F32)
        z1 = _shift_down(z, 1, [zp[15:16]])
        z2 = _shift_down(z, 2, [zp[14:15], zp[15:16]])
        cw = cw_ref[...]
        u = cw[0:1] * z2 + cw[1:2] * z1 + cw[2:3] * z
        du = dcbu * cb
        dcbu_n = conv_branch_grad(dxn_ref[...], gbn_ref[0].astype(F32))[4]
        du_n = dcbu_n * cbn_ref[0].astype(F32) * (i < n_tiles - 1).astype(F32)
        du1 = _shift_up(du, 1, [du_n[0:1]])
        du2 = _shift_up(du, 2, [du_n[0:1], du_n[1:2]])
        dz = cw[2:3] * du + cw[1:2] * du1 + cw[0:1] * du2
        pv_ref[3:4, :] += _rowsum(du * z2)
        pv_ref[4:5, :] += _rowsum(du * z1)
        pv_ref[5:6, :] += _rowsum(du * z)

        de_ref[0] = (dcbu * u).astype(BF16)
        de_ref[1] = (dz * cx).astype(BF16)
        de_ref[2] = (dz * cc).astype(BF16)
        de_ref[3] = dga.astype(BF16)
        de_ref[4] = dgb.astype(BF16)

    row = lambda w: pl.BlockSpec((tm, w), lambda i: (i, 0))
    nxt = pl.BlockSpec((16, D), lambda i: (jnp.minimum((i + 1) * (tm // 16), S // 16 - 1), 0))
    return pl.pallas_call(
        body, name="bwd_mix", grid=(n_tiles,),
        in_specs=[row(D), nxt, _vec_spec(), row(D)] + [_e_spec(c, tm) for c in range(5)]
                 + [_e_next_spec(0, tm, S), _e_next_spec(4, tm, S), _e_prev_spec(1, tm), _e_prev_spec(2, tm),
                    _const_spec((8, D)), _vec_spec(), _vec_spec(), row(D), row(D), row(AOW),
                    _const_spec((D, D)), _const_spec((D, D)), _const_spec((D, AOW))],
        out_specs=[row(D), row(D), row(D), row(AOW), row(AOW), pl.BlockSpec((5, tm, D), lambda i: (0, i, 0)),
                   _const_spec((8, D))],
        out_shape=[jax.ShapeDtypeStruct((S, D), BF16)] * 3 + [jax.ShapeDtypeStruct((S, AOW), F32)] * 2
                  + [jax.ShapeDtypeStruct((5, S, D), BF16), jax.ShapeDtypeStruct((8, D), F32)],
        compiler_params=_cparams("arbitrary"),
    )(dx1, dx1, gate1, mo, e, e, e, e, e, e, e, e, e, cw8, ba, bb, ya, yc, o_attn, w_out, w_bc, w_bat)


def _attn_bwd(qkv, do, lse, dl, bias_t):
    S = qkv.shape[2]
    nblk = S // HEAD

    def body(qkv_ref, do_ref, lse_ref, dl_ref, b_ref, d_ref):
        g = pl.program_id(1)
        bias = b_ref[0, 0]
        col = lax.broadcasted_iota(jnp.int32, bias.shape, 1)
        bias_last = jnp.where(col >= HEAD, NEG, bias)
        eye = (lax.broadcasted_iota(jnp.int32, (HEAD, HEAD), 0) == lax.broadcasted_iota(jnp.int32, (HEAD, HEAD), 1)).astype(F32)

        def as_row(t):
            return jnp.sum(t * eye, axis=0, keepdims=True)

        for gi, d in enumerate(DILATIONS):
            @pl.when(g == gi)
            def _(d=d):
                nb = nblk // d

                def query_side(start):
                    sl = pl.ds(start, HEAD, stride=d)
                    return (qkv_ref.at[0, 0][sl, :].astype(BF16), do_ref[sl, :].astype(BF16),
                            as_row(lse_ref[sl, :]), as_row(dl_ref[sl, :]))

                def step(b, first_of_residue, carry):
                    dq_part, own = carry
                    r, n = b // nb, b % nb
                    cur = pl.ds(n * (HEAD * d) + r, HEAD, stride=d)
                    if first_of_residue:
                        own = query_side(r)
                    nxt = query_side(jnp.minimum(n + 1, nb - 1) * (HEAD * d) + r)
                    q2 = jnp.concatenate([own[0], nxt[0]], axis=0)
                    do2 = jnp.concatenate([own[1], nxt[1]], axis=0)
                    k = qkv_ref.at[0, 1][cur, :].astype(BF16)
                    v = qkv_ref.at[0, 2][cur, :].astype(BF16)
                    s = _nt(k, q2) * SCALE + jnp.where(n < nb - 1, bias, bias_last)
                    p = jnp.exp(s - jnp.concatenate([own[2], nxt[2]], axis=1))
                    d_ref.at[0, 2][cur, :] = _nn(p.astype(BF16), do2)
                    dp = _nt(v, do2)
                    ds = (p * (dp - jnp.concatenate([own[3], nxt[3]], axis=1)) * SCALE).astype(BF16)
                    d_ref.at[0, 1][cur, :] = _nn(ds, q2)
                    dq2 = _tn(ds, k)
                    d_ref.at[0, 0][cur, :] = dq2[:HEAD] + jnp.where(n > 0, dq_part, 0.0)
                    return dq2[HEAD:], nxt

                def steps(i, carry):
                    for u in range(UNROLL):
                        carry = step(i * UNROLL + u, nb <= UNROLL and u % nb == 0, carry)
                    return carry

                lax.fori_loop(0, nblk // UNROLL, steps, (jnp.zeros((HEAD, HEAD), F32), query_side(0)))

    col_blk = pl.BlockSpec((S, HEAD), lambda j, g: (0, j))
    qkv_blk = pl.BlockSpec((1, 3, S, HEAD), lambda j, g: (g, 0, 0, j))
    return pl.pallas_call(
        body, name="attn_bwd", grid=(N_SLOT, 3),
        in_specs=[qkv_blk, col_blk, col_blk, col_blk, pl.BlockSpec((1, 1, HEAD, 2 * HEAD), lambda j, g: (g, j, 0, 0))],
        out_specs=qkv_blk,
        out_shape=jax.ShapeDtypeStruct((3, 3, S, AOW), F32),
        compiler_params=_cparams("parallel", "arbitrary"),
    )(qkv, do, lse, dl, bias_t)


def _bwd_in(dqkv, de, w_int, x, dx1, g_mix, sc1):
    S = x.shape[0]
    tm = TM
    dqkv = dqkv.reshape(3, 3, S, AOW)

    def body(dq_ref, de_ref, wq_ref, wk_ref, wv_ref, wa_ref, wb_ref, x_ref, dx1_ref, g_ref, sc_ref, gx_ref, pv_ref):
        acc = gx_ref
        i, k = pl.program_id(0), pl.program_id(1)

        @pl.when((i == 0) & (k == 0))
        def _():
            pv_ref[...] = jnp.zeros_like(pv_ref)

        @pl.when(k == 0)
        def _():
            acc[...] = jnp.zeros_like(acc)

        @pl.when(k < 3)
        def _():
            lhs = jnp.concatenate([dq_ref[0, t].astype(BF16) for t in range(3)], axis=1)
            acc[...] += _nn(lhs, jnp.concatenate([wq_ref[...], wk_ref[...], wv_ref[...]], axis=0))

        @pl.when(k >= 3)
        def _():
            acc[...] += _nn(de_ref[0], jnp.concatenate([wa_ref[...], wb_ref[...]], axis=0))

        @pl.when(k == 7)
        def _():
            dh = acc[...]
            xv = x_ref[...]
            r = _rms_r(xv)
            g = g_ref[...]
            dxn, pg = _rms_bwd(xv, r, g, dh * (1.0 + sc_ref[...]))
            gx_ref[...] = dx1_ref[...] + dxn
            pv_ref[0:1, :] += _rowsum(dh)
            pv_ref[1:2, :] += _rowsum(dh * (xv * r * g))
            pv_ref[2:3, :] += _rowsum(pg)

    grp = lambda k: jnp.minimum(k, 2)
    chunk = lambda k: jnp.maximum(k - 3, 0)
    wblk = lambda f: pl.BlockSpec((512, D), lambda i, k: (f(k), 0))
    row = pl.BlockSpec((tm, D), lambda i, k: (i, 0))
    once = pl.BlockSpec((tm, D), lambda i, k: (i, 0), pipeline_mode=pl.Buffered(1))
    return pl.pallas_call(
        body, name="bwd_in", grid=(S // tm, 8),
        in_specs=[pl.BlockSpec((1, 3, tm, 512), lambda i, k: (grp(k), 0, i, 0)),
                  pl.BlockSpec((1, tm, D), lambda i, k: (chunk(k), i, 0)),
                  wblk(grp), wblk(lambda k: 3 + grp(k)), wblk(lambda k: 6 + grp(k)),
                  wblk(lambda k: 9 + 2 * chunk(k)), wblk(lambda k: 10 + 2 * chunk(k)),
                  once, once, _vec_spec(), _vec_spec()],
        out_specs=[row, _const_spec((8, D))],
        out_shape=[jax.ShapeDtypeStruct((S, D), F32), jax.ShapeDtypeStruct((8, D), F32)],
        compiler_params=_cparams("arbitrary", "arbitrary"),
    )(dqkv, de, w_int, w_int, w_int, w_int, w_int, x, dx1, g_mix, sc1)


def _grad_w(name, a, b):
    S, ka = a.shape
    nb = b.shape[1]

    def body(a_ref, b_ref, o_ref):
        o_ref[...] = _tn(a_ref[...], b_ref[...]).astype(BF16)

    return pl.pallas_call(
        body, name=name, grid=(ka // 512,),
        in_specs=[pl.BlockSpec((S, 512), lambda n: (0, n)), pl.BlockSpec((S, nb), lambda n: (0, 0))],
        out_specs=pl.BlockSpec((512, nb), lambda n: (n, 0)),
        out_shape=jax.ShapeDtypeStruct((ka, nb), BF16),
        compiler_params=_cparams("parallel"),
    )(a, b)


def _grad_w_small(dya, o_bf, cbu, dyc, merged, dmo, after):
    S = dya.shape[0]

    def body(dya_ref, o_ref, cbu_ref, dyc_ref, mg_ref, dmo_ref, after_ref, gba_ref, gbc_ref, gout_ref):
        gba_ref[...] = _tn(dya_ref[...], o_ref[...]).astype(BF16)
        gbc_ref[...] = _tn(cbu_ref[...], dyc_ref[...]).astype(BF16)
        gout_ref[...] = _tn(mg_ref[...], dmo_ref[...]).astype(BF16)

    a_blk = pl.BlockSpec((S, 512), lambda n: (0, n))
    whole = lambda w: pl.BlockSpec((S, w), lambda n: (0, 0))
    out = lambda w: pl.BlockSpec((512, w), lambda n: (n, 0))
    return pl.pallas_call(
        body, name="grad_w_small", grid=(D // 512,),
        in_specs=[a_blk, whole(AOW), a_blk, whole(D), a_blk, whole(D), pl.BlockSpec(memory_space=pl.ANY)],
        out_specs=[out(AOW), out(D), out(D)],
        out_shape=[jax.ShapeDtypeStruct((D, AOW), BF16), jax.ShapeDtypeStruct((D, D), BF16), jax.ShapeDtypeStruct((D, D), BF16)],
        compiler_params=_cparams("parallel"),
    )(dya, o_bf, cbu, dyc, merged, dmo, after)


def _grad_w_in(dqkv, de, h):
    S = h.shape[0]

    def body(dq_ref, de_ref, h_ref, o_ref):
        n = pl.program_id(0)

        @pl.when(n < 9)
        def _():
            o_ref[...] = _tn(dq_ref[0].astype(BF16), h_ref[...]).astype(BF16)

        @pl.when(n >= 9)
        def _():
            o_ref[...] = _tn(de_ref[0], h_ref[...]).astype(BF16)

    def e_idx(n):
        kk = jnp.maximum(n - 9, 0)
        return (kk // 2, 0, kk % 2)

    return pl.pallas_call(
        body, name="grad_w_in", grid=(19,),
        in_specs=[pl.BlockSpec((1, S, 512), lambda n: (jnp.minimum(n, 8), 0, 0)), pl.BlockSpec((1, S, 512), e_idx),
                  pl.BlockSpec((S, D), lambda n: (0, 0))],
        out_specs=pl.BlockSpec((512, D), lambda n: (_win_rowblock(n), 0)),
        out_shape=jax.ShapeDtypeStruct((19 * 512, D), BF16),
        compiler_params=_cparams("parallel"),
    )(dqkv, de, h)


def _local_step(x, h, tgt, mod, g_mix, g_mlp, g_fin, ba, bb, cw8, w_int, mix_weights, mlp_weights, mlp_grads_ready, w_in_grad_ready,
                other_grads_ready):
    S = x.shape[0]
    sh1, sc1, gt1, sh2, sc2, gt2 = [mod[k:k + 1] for k in range(6)]
    bias, bias_t = _bias_table()

    qkv, e = _proj(h, w_int)
    qkv = qkv.reshape(3, 3, S, AOW)
    o_attn, lse = _attn_fwd(qkv, bias)
    w_bat, w_bc, w_out = mix_weights(o_attn)
    o_bf, cbu, ya, yc, merged = _mix(o_attn, e, cw8, ba, bb, w_bat, w_bc)
    x1, mo, h2 = _out_proj(merged, w_out, x, gt1, g_mlp, sc2, sh2)
    w_mit, w_mo = mlp_weights(x1)
    f = _mlp_in(h2, w_mit)
    mlp, dx2, pv_f = _mlp_out(f, w_mo, x1, gt2, g_fin, tgt)

    da, dmo2, pv_a = _bwd_mlp_a(dx2, gt2, mlp, w_mo, f)
    dx1, pv_b = _bwd_mlp_b(da, w_mit, x1, dx2, g_mlp, sc2)
    zero = mlp_grads_ready(_grad_w("grad_w_mi", da, h2), _grad_w("grad_w_mo", f, dmo2))
    dmo, dya, dyc, do, dl, de, pv_m = _bwd_mix(dx1, gt1 + zero, mo, e, cw8, ba, bb, ya, yc, o_attn, w_out, w_bc, w_bat)
    dqkv = _attn_bwd(qkv, do, lse, dl, bias_t).reshape(9, S, AOW)
    after = w_in_grad_ready(_grad_w_in(dqkv, de, h))
    zero = other_grads_ready(*_grad_w_small(dya, o_bf, cbu, dyc, merged, dmo, after))
    grad_x, pv_i = _bwd_in(dqkv, de, w_int, x, dx1, g_mix, sc1 + zero)

    vec = jnp.concatenate([pv_i[0:2], pv_m[0:1], pv_b[0:2], pv_a[0:1], pv_i[2:3], pv_b[2:3], pv_f[0:1],
                           pv_m[1:3], pv_m[3:6], pv_f[1:2], jnp.zeros((1, D), F32)], axis=0)
    return grad_x, vec


def _my_place():
    return lax.axis_index("x"), lax.axis_index("y"), lax.axis_index("c")


def _dev_index(px, py, pc):
    return 4 * px + 2 * py + pc


def _peer(x, y, c, m):
    return (x ^ ((m >> 2) & 1), y ^ ((m >> 1) & 1), c ^ (m & 1))


HBM_SPEC = pl.BlockSpec(memory_space=pltpu.HBM)
SEM_SPEC = pl.BlockSpec(memory_space=pltpu.SEMAPHORE)
N_PEER = N_DEV - 1


SPLIT_MASKS = {"gather": tuple(range(1, N_DEV)), "scatter": tuple(range(1, N_DEV)), "chips": (2, 4, 6), "sibling": (1, 1, 1, 1)}


def _split_copy(mode, src_ref, land_ref, send_sems, recv_sems, w, j, place, arriving=False):
    x, y, c = place
    masks = SPLIT_MASKS[mode]
    peer = _peer(x, y, c, masks[j])
    k = w * len(masks) + j
    sender, receiver = ((peer, (x, y, c)) if arriving else ((x, y, c), peer))
    if mode == "gather":
        r = src_ref.shape[0]
        src, dst = src_ref, land_ref.at[pl.ds(pl.multiple_of(_dev_index(*sender) * r, 16), r), :]
    elif mode == "scatter":
        r = land_ref.shape[1]
        src, dst = src_ref.at[pl.ds(pl.multiple_of(_dev_index(*receiver) * r, 16), r), :], land_ref.at[j]
    elif mode == "chips":
        src, dst = src_ref.at[2 * receiver[0] + receiver[1]], land_ref.at[j]
    else:
        r = land_ref.shape[1]
        src, dst = src_ref.at[pl.ds(pl.multiple_of((2 * j + receiver[2]) * r, 16), r), :], land_ref.at[j]
    return pltpu.make_async_remote_copy(src_ref=src, dst_ref=dst, send_sem=send_sems.at[k], recv_sem=recv_sems.at[k],
                                        device_id=peer, device_id_type=MESH)


def _split_start(name, mode, srcs, lands):
    n = len(srcs)
    nm = len(SPLIT_MASKS[mode])

    def body(*refs):
        src, land = refs[:n], refs[n:2 * n]
        send_sems, recv_sems = refs[2 * n], refs[2 * n + 1]
        token = refs[-1]
        place = _my_place()
        for w in range(n):
            for j in range(nm):
                _split_copy(mode, src[w], land[w], send_sems, recv_sems, w, j, place).start()
        token[...] = jnp.zeros_like(token)

    hbm = lambda t: pltpu.HBM(t.shape, t.dtype)
    out = pl.pallas_call(
        body, name=name,
        out_shape=(pltpu.SemaphoreType.DMA((n * nm,)), pltpu.SemaphoreType.DMA((n * nm,)), *[hbm(t) for t in srcs],
                   *[hbm(t) for t in lands], jax.ShapeDtypeStruct((8, 128), F32)),
        in_specs=(HBM_SPEC,) * (2 * n),
        out_specs=(SEM_SPEC, SEM_SPEC) + (HBM_SPEC,) * (2 * n) + (pl.BlockSpec(memory_space=pltpu.VMEM),),
        input_output_aliases={i: 2 + i for i in range(2 * n)},
        compiler_params=pltpu.CompilerParams(has_side_effects=pltpu.SideEffectType.DATAFLOW_SIDE_EFFECTING),
    )(*[pltpu.with_memory_space_constraint(t, pltpu.HBM) for t in (*srcs, *lands)])
    return out[0], out[1], out[2:2 + n], out[2 + n:2 + 2 * n], out[-1][0:1, 0:1], out[-1]


def _split_wait(name, mode, send_sems, recv_sems, srcs, lands, after):
    n = len(srcs)

    def body(*refs):
        src, land = refs[:n], refs[n:2 * n]
        ssem, rsem = refs[2 * n], refs[2 * n + 1]
        place = _my_place()
        for w in range(n):
            for j in range(len(SPLIT_MASKS[mode])):
                _split_copy(mode, src[w], land[w], ssem, rsem, w, j, place).wait_send()
                _split_copy(mode, src[w], land[w], ssem, rsem, w, j, place, arriving=True).wait_recv()

    hbm = lambda t: pltpu.HBM(t.shape, t.dtype)
    out = pl.pallas_call(
        body, name=name,
        out_shape=tuple(hbm(t) for t in (*srcs, *lands)),
        in_specs=(HBM_SPEC,) * (2 * n) + (SEM_SPEC, SEM_SPEC, pl.BlockSpec(memory_space=pl.ANY)),
        out_specs=(HBM_SPEC,) * (2 * n),
        input_output_aliases={i: i for i in range(2 * n)},
        compiler_params=pltpu.CompilerParams(has_side_effects=pltpu.SideEffectType.DATAFLOW_SIDE_EFFECTING),
    )(*srcs, *lands, send_sems, recv_sems, after)
    return out[:n], out[n:]


def _sibling_exchange(grads):
    nw = len(grads)
    HBM = pl.BlockSpec(memory_space=pl.ANY)

    def body(*refs):
        g, land = refs[:nw], refs[nw:2 * nw]
        send_sems, recv_sems = refs[2 * nw:]
        x, y, c = _my_place()

        def copy(w, q, owner_core):
            r = land[w].shape[1]
            return pltpu.make_async_remote_copy(
                src_ref=g[w].at[pl.ds(pl.multiple_of((2 * q + owner_core) * r, 16), r), :], dst_ref=land[w].at[q],
                send_sem=send_sems.at[w, q], recv_sem=recv_sems.at[w, q], device_id=(x, y, 1 - c), device_id_type=MESH)

        sends = [copy(w, q, 1 - c) for w in range(nw) for q in range(4)]
        for cp in sends:
            cp.start()
        for w in range(nw):
            for q in range(4):
                copy(w, q, c).wait_recv()
        for cp in sends:
            cp.wait_send()

    return pl.pallas_call(
        body, name="sibling_exchange",
        out_shape=[jax.ShapeDtypeStruct((4, a.shape[0] // N_DEV, a.shape[1]), a.dtype) for a in grads],
        in_specs=[HBM] * nw, out_specs=[HBM] * nw,
        scratch_shapes=[pltpu.SemaphoreType.DMA((nw, 4)), pltpu.SemaphoreType.DMA((nw, 4))],
    )(*grads)


def _pair_sums(gs, sibs, core):
    n = len(gs)

    def body(core_ref, *refs):
        for w in range(n):
            refs[2 * n + w][0] = (refs[w][0, 0].astype(F32) + refs[n + w][0].astype(F32)).astype(BF16)

    in_specs = [pl.BlockSpec((1, 1) + t.shape[1:], lambda q, core_ref: (q, core_ref[0], 0, 0)) for t in sibs]
    in_specs += [pl.BlockSpec((1,) + t.shape[1:], lambda q, core_ref: (q, 0, 0)) for t in sibs]
    return pl.pallas_call(
        body, name="pair_sums",
        grid_spec=pltpu.PrefetchScalarGridSpec(
            num_scalar_prefetch=1, grid=(4,), in_specs=in_specs,
            out_specs=[pl.BlockSpec((1,) + t.shape[1:], lambda q, core_ref: (q, 0, 0)) for t in sibs]),
        out_shape=[jax.ShapeDtypeStruct(t.shape, BF16) for t in sibs],
        compiler_params=_cparams("parallel"),
    )(core, *[g.reshape(4, 2, t.shape[1], t.shape[2]) for g, t in zip(gs, sibs)], *sibs)


def _allgather_small(v, name):
    r, ccols = v.shape

    def body(v_ref, out_ref, send_sems, recv_sems):
        x, y, c = _my_place()
        my_idx = _dev_index(x, y, c)
        out_ref[my_idx] = v_ref[...]

        def copy(m):
            peer = _peer(x, y, c, m)
            return pltpu.make_async_remote_copy(
                src_ref=v_ref, dst_ref=out_ref.at[my_idx],
                send_sem=send_sems.at[m - 1], recv_sem=recv_sems.at[m - 1], device_id=peer, device_id_type=MESH)

        def arrival(m):
            peer = _peer(x, y, c, m)
            return pltpu.make_async_remote_copy(
                src_ref=v_ref, dst_ref=out_ref.at[_dev_index(*peer)],
                send_sem=send_sems.at[m - 1], recv_sem=recv_sems.at[m - 1], device_id=peer, device_id_type=MESH)

        sends = [copy(m) for m in range(1, N_DEV)]
        for cp in sends:
            cp.start()
        for m in range(1, N_DEV):
            arrival(m).wait_recv()
        for cp in sends:
            cp.wait_send()

    return pl.pallas_call(
        body, name=name,
        out_shape=jax.ShapeDtypeStruct((N_DEV, r, ccols), v.dtype),
        in_specs=[pl.BlockSpec(memory_space=pltpu.VMEM)], out_specs=pl.BlockSpec(memory_space=pltpu.VMEM),
        scratch_shapes=[pltpu.SemaphoreType.DMA((7,)), pltpu.SemaphoreType.DMA((7,))],
    )(v)


def _gather_w_in_and_condition(shard, pay, w_ada, b_cols):
    r, ccols = shard.shape
    ncol = w_ada.shape[1]

    def body(sh_ref, pay_ref, w_ref, b_ref, full_ref, got_ref, act_ref, mod_ref, send_sems, recv_sems, small_send, small_recv, local_sem):
        x, y, c = _my_place()
        me, sibling = (x, y, c), (x, y, 1 - c)
        my_idx = _dev_index(x, y, c)
        chips = [(1 - x, y), (x, 1 - y), (1 - x, 1 - y)]

        def small(rnd, buf, m, arriving=False):
            peer = _peer(x, y, c, m)
            slot = _dev_index(*peer) if arriving else my_idx
            return pltpu.make_async_remote_copy(
                src_ref=buf.at[my_idx], dst_ref=buf.at[slot], send_sem=small_send.at[rnd, m - 1],
                recv_sem=small_recv.at[rnd, m - 1], device_id=peer, device_id_type=MESH)

        def rows(px, py, pc):
            return full_ref.at[pl.ds(pl.multiple_of(_dev_index(px, py, pc) * r, 16), r), :]

        def copy(k, block, to, src=None):
            return pltpu.make_async_remote_copy(
                src_ref=rows(*block) if src is None else src, dst_ref=rows(*block),
                send_sem=send_sems.at[k], recv_sem=recv_sems.at[k], device_id=to, device_id_type=MESH)

        got_ref[my_idx] = pay_ref[...]
        round1 = [small(0, got_ref, m) for m in range(1, N_DEV)]
        for cp in round1:
            cp.start()
        mine = pltpu.make_async_copy(sh_ref, rows(*me), local_sem)
        mine.start()
        first = [copy(0, me, sibling, src=sh_ref)] + [copy(1 + j, me, (*chip, c), src=sh_ref) for j, chip in enumerate(chips)]
        for cp in first:
            cp.start()

        for m in range(1, N_DEV):
            small(0, got_ref, m, arriving=True).wait_recv()
        cv = jnp.concatenate([got_ref[s, 0:1, :] for s in range(N_DEV)], axis=0)
        act = cv * _sigmoid(cv)
        act_ref[...] = act
        mod_ref[my_idx] = jnp.dot(act, w_ref[...], preferred_element_type=F32, precision=lax.Precision.HIGHEST) + b_ref[...]
        round2 = [small(1, mod_ref, m) for m in range(1, N_DEV)]
        for cp in round2:
            cp.start()

        passed = []
        for j, chip in enumerate(chips):
            copy(1 + j, (*chip, c), me).wait_recv()
            fwd = copy(4 + j, (*chip, c), sibling)
            fwd.start()
            passed.append(fwd)
        copy(0, sibling, me).wait_recv()
        for j, chip in enumerate(chips):
            copy(4 + j, (*chip, 1 - c), me).wait_recv()
        for m in range(1, N_DEV):
            small(1, mod_ref, m, arriving=True).wait_recv()
        for cp in first + passed + round1 + round2:
            cp.wait_send()
        mine.wait()

    anyspec = pl.BlockSpec(memory_space=pl.ANY)
    vmem = pl.BlockSpec(memory_space=pltpu.VMEM)
    return pl.pallas_call(
        body, name="gather_w_in_and_condition",
        out_shape=[jax.ShapeDtypeStruct((N_DEV * r, ccols), shard.dtype), jax.ShapeDtypeStruct((N_DEV, 8, D), F32),
                   jax.ShapeDtypeStruct((N_DEV, D), F32), jax.ShapeDtypeStruct((N_DEV, N_DEV, ncol), F32)],
        in_specs=[anyspec, vmem, vmem, vmem], out_specs=[anyspec, vmem, vmem, vmem],
        scratch_shapes=[pltpu.SemaphoreType.DMA((7,)), pltpu.SemaphoreType.DMA((7,)), pltpu.SemaphoreType.DMA((2, 7)),
                        pltpu.SemaphoreType.DMA((2, 7)), pltpu.SemaphoreType.DMA],
        compiler_params=_cparams(),
    )(shard, pay, w_ada, b_cols)


def _ada_bwd(act_t, gm_cols):
    def body(a_ref, g_ref, o_ref):
        o_ref[...] = jnp.dot(a_ref[...], g_ref[...], preferred_element_type=F32, precision=lax.Precision.HIGHEST)

    return pl.pallas_call(
        body, name="ada_bwd", out_shape=jax.ShapeDtypeStruct((D, gm_cols.shape[1]), F32), compiler_params=_cparams(),
    )(act_t, gm_cols)


def _row_tile(r):
    for t in (256, 304, 128, 64, 16):
        if r % t == 0:
            return t
    return r


def _sum_parts(parts, name, own=None):
    k, r, ccols = parts.shape
    tr = _row_tile(r)

    def body(*refs):
        p_ref, o_ref = refs[0], refs[-1]
        acc = p_ref[0].astype(F32) if own is None else refs[1][...].astype(F32) + p_ref[0].astype(F32)
        for s in range(1, k):
            acc = acc + p_ref[s].astype(F32)
        o_ref[...] = acc

    blk = pl.BlockSpec((tr, ccols), lambda i: (i, 0))
    return pl.pallas_call(
        body, name=name, grid=(r // tr,),
        in_specs=[pl.BlockSpec((k, tr, ccols), lambda i: (0, i, 0))] + ([] if own is None else [blk]),
        out_specs=blk,
        out_shape=jax.ShapeDtypeStruct((r, ccols), F32),
        compiler_params=_cparams("parallel"),
    )(*((parts,) if own is None else (parts, own)))


def _adamw(w, g, m, v, name):
    r, ccols = w.shape
    tr = _row_tile(r)
    c1 = 1.0 / (1.0 - B1 ** STEP)
    c2 = 1.0 / (1.0 - B2 ** STEP)

    def body(w_ref, g_ref, m_ref, v_ref, d_ref, nm_ref, nv_ref):
        gv = g_ref[...]
        nm = B1 * m_ref[...] + (1.0 - B1) * gv
        nv = B2 * v_ref[...] + (1.0 - B2) * jnp.square(gv)
        nm_ref[...] = nm
        nv_ref[...] = nv
        d_ref[...] = -LR * ((nm * c1) / (jnp.sqrt(nv * c2) + ADAM_EPS) + WD * w_ref[...])

    blk = pl.BlockSpec((tr, ccols), lambda i: (i, 0))
    return pl.pallas_call(
        body, name=name, grid=(r // tr,), in_specs=[blk] * 4, out_specs=[blk] * 3,
        out_shape=[jax.ShapeDtypeStruct((r, ccols), F32)] * 3,
        compiler_params=_cparams("parallel"),
    )(w, g, m, v)


def _sum_adamw(parts, own, w, m, v, name):
    k, r, ccols = parts.shape
    tr = _row_tile(r)
    c1 = 1.0 / (1.0 - B1 ** STEP)
    c2 = 1.0 / (1.0 - B2 ** STEP)

    def body(p_ref, own_ref, w_ref, m_ref, v_ref, g_ref, d_ref, nm_ref, nv_ref):
        gv = own_ref[...].astype(F32)
        for s in range(k):
            gv = gv + p_ref[s].astype(F32)
        g_ref[...] = gv
        nm = B1 * m_ref[...] + (1.0 - B1) * gv
        nv = B2 * v_ref[...] + (1.0 - B2) * jnp.square(gv)
        nm_ref[...] = nm
        nv_ref[...] = nv
        d_ref[...] = -LR * ((nm * c1) / (jnp.sqrt(nv * c2) + ADAM_EPS) + WD * w_ref[...])

    blk = pl.BlockSpec((tr, ccols), lambda i: (i, 0))
    return pl.pallas_call(
        body, name=name, grid=(r // tr,),
        in_specs=[pl.BlockSpec((k, tr, ccols), lambda i: (0, i, 0))] + [blk] * 4, out_specs=[blk] * 4,
        out_shape=[jax.ShapeDtypeStruct((r, ccols), F32)] * 4,
        compiler_params=_cparams("parallel"),
    )(parts, own, w, m, v)


VEC_ROWS = ((0, 6), (6, 7), (9, 11), (11, 14), (7, 8), (8, 9))


def _adamw_vectors(w, g, m, v):
    c1 = 1.0 / (1.0 - B1 ** STEP)
    c2 = 1.0 / (1.0 - B2 ** STEP)

    def put(refs, p):
        for ref, (lo, hi) in zip(refs, VEC_ROWS):
            if ref.shape == (3, HEAD):
                ref[...] = p[lo:hi, :HEAD]
            else:
                ref[...] = jnp.concatenate([p[k:k + 1] for k in range(lo, hi)], axis=1)

    def body(w_ref, g_ref, m_ref, v_ref, *outs):
        gv = g_ref[...]
        nm = B1 * m_ref[...] + (1.0 - B1) * gv
        nv = B2 * v_ref[...] + (1.0 - B2) * jnp.square(gv)
        delta = -LR * ((nm * c1) / (jnp.sqrt(nv * c2) + ADAM_EPS) + WD * w_ref[...])
        for kind, p in enumerate((gv, delta, nm, nv)):
            put(outs[6 * kind:6 * kind + 6], p)

    shapes = [(1, 6 * D), (1, D), (1, 2 * D), (3, HEAD), (1, D), (1, D)]
    out = pl.pallas_call(
        body, name="adamw_vectors", out_shape=[jax.ShapeDtypeStruct(sh, F32) for sh in shapes] * 4, compiler_params=_cparams(),
    )(w, g, m, v)
    fix = lambda t: (t[0], t[1], t[2], t[3][None], t[4], t[5].reshape(D))
    return [fix(out[6 * kind:6 * kind + 6]) for kind in range(4)]


def _pack_vectors(b_ada, g_mix, g_mlp, g_fin, b_gate, conv_w):
    conv_rows = jnp.pad(conv_w.reshape(3, HEAD), ((0, 0), (0, D - HEAD)))
    return jnp.concatenate([b_ada.reshape(6, D), g_mix.reshape(1, D), g_mlp.reshape(1, D), g_fin.reshape(1, D),
                            b_gate.reshape(2, D), conv_rows, jnp.zeros((2, D), F32)], axis=0)


def kernel(x, c, w_ada, b_ada, g_norm_mix, w_in, b_gate, conv_w, w_branch_attn, w_branch_conv, w_out, g_norm_mlp, w_mlp_in, w_mlp_out, g_norm_final, loss_target, m_w_ada, m_b_ada, m_g_norm_mix, m_w_in, m_b_gate, m_conv_w, m_w_branch_attn, m_w_branch_conv, m_w_out, m_g_norm_mlp, m_w_mlp_in, m_w_mlp_out, m_g_norm_final, v_w_ada, v_b_ada, v_g_norm_mix, v_w_in, v_b_gate, v_conv_w, v_w_branch_attn, v_w_branch_conv, v_w_out, v_g_norm_mlp, v_w_mlp_in, v_w_mlp_out, v_g_norm_final):
    S = x.shape[1]
    xi, yi, ci = _my_place()
    me = _dev_index(xi, yi, ci)
    x2 = x.reshape(S, D)
    tgt = loss_target.reshape(S, D)

    pay = jnp.zeros((8, D), F32).at[0].set(c[0]).at[1:4, :HEAD].set(conv_w[0])
    ncol = w_ada.shape[2]
    b_cols = lax.dynamic_slice(b_ada, (0, me * ncol), (1, ncol))
    w_int, got, act, mod_all = _gather_w_in_and_condition(w_in[0].T.astype(BF16), pay, w_ada[0], b_cols)
    cw8 = jnp.pad(got[:, 1:4, :HEAD].transpose(1, 0, 2).reshape(3, D), ((0, 5), (0, 0)))
    mod = lax.dynamic_index_in_dim(mod_all, me, axis=1, keepdims=False).reshape(6, D)
    late = [w_branch_attn[0].T.astype(BF16), w_branch_conv[0].astype(BF16), w_out[0].astype(BF16),
            w_mlp_in[0].T.astype(BF16), w_mlp_out[0].astype(BF16)]
    w_int, late = lax.optimization_barrier((w_int, late))
    zones = [lax.dynamic_update_slice(lax.empty((N_DEV * t.shape[0], t.shape[1]), BF16), t, (me * t.shape[0], 0)) for t in late]
    ag_mix = _split_start("gather_mix_start", "gather", late[:3], zones[:3])
    ag_mlp = _split_start("gather_mlp_start", "gather", late[3:], zones[3:])
    mod = mod + ag_mix[4] + ag_mlp[4]
    h = _prenorm(x2, g_norm_mix, mod[1:2], mod[0:1])

    def mix_weights(o_attn):
        return _split_wait("gather_mix_wait", "gather", *ag_mix[:4], o_attn)[1]

    def mlp_weights(x1):
        return _split_wait("gather_mlp_wait", "gather", *ag_mlp[:4], x1)[1]

    rs = {}

    def mlp_grads_ready(*grads):
        lands = [lax.empty((N_PEER, t.shape[0] // N_DEV, t.shape[1]), BF16) for t in grads]
        rs["mlp"] = _split_start("scatter_mlp_start", "scatter", grads, lands)
        return rs["mlp"][4]

    def w_in_grad_ready(g_in):
        r = g_in.shape[0] // N_DEV
        rs["sib"] = _split_start("sibling_w_in_start", "sibling", [g_in], [jnp.zeros((4, r, g_in.shape[1]), BF16)])
        return rs["sib"][5]

    def other_grads_ready(*small):
        core = ci.reshape(1).astype(jnp.int32)
        (g_in,), (sib_in,) = _split_wait("sibling_w_in_wait", "sibling", *rs["sib"][:4], small[0])
        pair = _pair_sums([g_in, *small], [sib_in, *_sibling_exchange(small)], core)
        lands = [lax.empty((3,) + t.shape[1:], BF16) for t in pair]
        rs["rest"] = _split_start("scatter_rest_start", "chips", pair, lands)
        return rs["rest"][4]

    ba, bb = b_gate[:, :D], b_gate[:, D:]
    grad_x, vec = _local_step(
        x2, h, tgt, mod, g_norm_mix, g_norm_mlp, g_norm_final.reshape(1, D), ba, bb, cw8, w_int, mix_weights, mlp_weights,
        mlp_grads_ready, w_in_grad_ready, other_grads_ready)

    vec_all = _allgather_small(vec, "gather_vec")
    vec_sum = _sum_parts(vec_all, "sum_vec")
    loss = vec_sum[14, 0]
    gm_all = vec_all[:, 0:6, :].reshape(N_DEV, 6 * D)
    gm_cols = lax.dynamic_slice(gm_all, (0, me * ncol), (N_DEV, ncol))
    g_w_ada = _ada_bwd(act.T, gm_cols)
    conv_cols = lax.dynamic_slice(vec_sum[11:14], (0, me * HEAD), (3, HEAD))
    g_pack = jnp.concatenate([vec_sum[0:11], jnp.pad(conv_cols, ((0, 0), (0, D - HEAD))), jnp.zeros((2, D), F32)], axis=0)
    packs = [_pack_vectors(*t) for t in ((b_ada, g_norm_mix, g_norm_mlp, g_norm_final, b_gate, conv_w),
                                         (m_b_ada, m_g_norm_mix, m_g_norm_mlp, m_g_norm_final, m_b_gate, m_conv_w),
                                         (v_b_ada, v_g_norm_mix, v_g_norm_mlp, v_g_norm_final, v_b_gate, v_conv_w))]
    gv, dv, mv, vv = _adamw_vectors(packs[0], g_pack, packs[1], packs[2])
    d_ada, nm_ada, nv_ada = _adamw(w_ada[0], g_w_ada, m_w_ada[0], v_w_ada[0], "adamw_w_ada")

    big = {}
    srcs, lands = _split_wait("scatter_mlp_wait", "scatter", *rs["mlp"][:4], d_ada)
    own = [lax.dynamic_slice(g, (me * land.shape[1], 0), land.shape[1:]) for g, land in zip(srcs, lands)]
    g_mi = _sum_parts(lands[0], "sum_w_mi", own=own[0]).T
    big["w_mi"] = (g_mi[None],) + tuple(t[None] for t in _adamw(w_mlp_in[0], g_mi, m_w_mlp_in[0], v_w_mlp_in[0], "adamw_w_mi"))
    big["w_mo"] = tuple(t[None] for t in _sum_adamw(lands[1], own[1], w_mlp_out[0], m_w_mlp_out[0], v_w_mlp_out[0], "adamw_w_mo"))
    srcs, lands = _split_wait("scatter_rest_wait", "chips", *rs["rest"][:4], big["w_mo"][1])
    own = [lax.dynamic_index_in_dim(pair, 2 * xi + yi, axis=0, keepdims=False) for pair in srcs]
    big["w_in"] = tuple(t.T[None] for t in _sum_adamw(lands[0], own[0], w_in[0].T, m_w_in[0].T, v_w_in[0].T, "adamw_w_in"))
    g_ba = _sum_parts(lands[1], "sum_w_ba", own=own[1]).T
    big["w_ba"] = (g_ba[None],) + tuple(t[None] for t in _adamw(w_branch_attn[0], g_ba, m_w_branch_attn[0], v_w_branch_attn[0], "adamw_w_ba"))
    big["w_bc"] = tuple(t[None] for t in _sum_adamw(lands[2], own[2], w_branch_conv[0], m_w_branch_conv[0], v_w_branch_conv[0], "adamw_w_bc"))
    big["w_out"] = tuple(t[None] for t in _sum_adamw(lands[3], own[3], w_out[0], m_w_out[0], v_w_out[0], "adamw_w_out"))

    def ordered(k, ada, vecs):
        return (ada[None], vecs[0], vecs[1], big["w_in"][k], vecs[2], vecs[3], big["w_ba"][k], big["w_bc"][k],
                big["w_out"][k], vecs[4], big["w_mi"][k], big["w_mo"][k], vecs[5])

    return (loss, grad_x.reshape(1, S, D), *ordered(0, g_w_ada, gv), *ordered(1, d_ada, dv),
            *ordered(2, nm_ada, mv), *ordered(3, nv_ada, vv))
```

```python
import numpy as np
import jax
import jax.numpy as jnp
from jax import lax
from jax.experimental import pallas as pl
from jax.experimental.pallas import tpu as pltpu

F32, BF16 = jnp.float32, jnp.bfloat16
D = 1024
HEAD = 128
DILATIONS = (1, 4, 16)
N_SLOT = 4
AOW = N_SLOT * HEAD
DFF = 4 * D
N_DEV = 8
UNROLL = 16
EPS = 1e-6
NEG = -1e30
SCALE = HEAD ** -0.5
LR, B1, B2, ADAM_EPS, WD, STEP = 0.001, 0.9, 0.999, 1e-08, 0.01, 10
V7X_VMEM_LIMIT = 56 * 1024 * 1024
TM = 1024
MESH = pl.DeviceIdType.MESH


def _cparams(*sem):
    if sem:
        return pltpu.CompilerParams(dimension_semantics=sem, vmem_limit_bytes=V7X_VMEM_LIMIT)
    return pltpu.CompilerParams(vmem_limit_bytes=V7X_VMEM_LIMIT)


def _nn(a, b):
    return jnp.dot(a, b, preferred_element_type=F32)


def _nt(a, b):
    return lax.dot_general(a, b, (((1,), (1,)), ((), ())), preferred_element_type=F32)


def _tn(a, b):
    return lax.dot_general(a, b, (((0,), (0,)), ((), ())), preferred_element_type=F32)


def _rms_r(x):
    return lax.rsqrt(jnp.mean(x * x, axis=-1, keepdims=True) + EPS)


def _rms_bwd(x, r, g, dn):
    gy = dn * g
    dx = r * gy - x * (r * r * r) * jnp.mean(x * gy, axis=-1, keepdims=True)
    return dx, dn * (x * r)


def _sigmoid(t):
    return 1.0 / (1.0 + jnp.exp(-t))


def _rowsum(v):
    return jnp.sum(v, axis=0, keepdims=True)


def _vec_spec(n=D):
    return pl.BlockSpec((1, n), lambda *_: (0, 0))


def _const_spec(shape):
    nd = len(shape)
    return pl.BlockSpec(shape, lambda *_: (0,) * nd)


def _win_rowblock(j):
    return jnp.where(j < 9, (j % 3) * 3 + j // 3, j)


def _prenorm(x, g, sc, sh):
    S = x.shape[0]
    tm = TM

    def body(x_ref, g_ref, sc_ref, sh_ref, h_ref):
        xv = x_ref[...]
        h_ref[...] = (xv * _rms_r(xv) * g_ref[...] * (1.0 + sc_ref[...]) + sh_ref[...]).astype(BF16)

    row = pl.BlockSpec((tm, D), lambda i: (i, 0))
    return pl.pallas_call(
        body, name="prenorm", grid=(S // tm,), in_specs=[row, _vec_spec(), _vec_spec(), _vec_spec()], out_specs=row,
        out_shape=jax.ShapeDtypeStruct((S, D), BF16), compiler_params=_cparams("parallel"),
    )(x, g, sc, sh)


def _proj(h, w_int):
    S = h.shape[0]

    def body(h_ref, w_ref, q_ref, e_ref):
        j = pl.program_id(0)
        acc = _nt(h_ref[...], w_ref[...])

        @pl.when(j < 9)
        def _():
            q_ref[0] = acc

        @pl.when(j >= 9)
        def _():
            e_ref[0] = acc.astype(BF16)

    def e_idx(j):
        k = jnp.maximum(j - 9, 0)
        return (k // 2, 0, k % 2)

    return pl.pallas_call(
        body, name="proj", grid=(19,),
        in_specs=[pl.BlockSpec((S, D), lambda j: (0, 0), pipeline_mode=pl.Buffered(1)),
                  pl.BlockSpec((512, D), lambda j: (_win_rowblock(j), 0))],
        out_specs=[pl.BlockSpec((1, S, 512), lambda j: (jnp.minimum(j, 8), 0, 0)), pl.BlockSpec((1, S, 512), e_idx)],
        out_shape=[jax.ShapeDtypeStruct((9, S, 512), F32), jax.ShapeDtypeStruct((5, S, D), BF16)],
        compiler_params=_cparams("arbitrary"),
    )(h, w_int)


def _bias_table():
    slopes = (2.0 ** (-8.0 * np.arange(1, 13, dtype=np.float32) / 12.0)).astype(np.float32)
    qi = np.arange(HEAD)[:, None]
    kj = np.arange(2 * HEAD)[None, :]
    delta = HEAD + qi - kj
    mask = (delta >= 0) & (delta <= HEAD)
    out = np.zeros((3, N_SLOT, HEAD, 2 * HEAD), np.float32)
    for gi, d in enumerate(DILATIONS):
        for j in range(N_SLOT):
            bias = -slopes[gi * N_SLOT + j] * (delta * d).astype(np.float32)
            out[gi, j] = np.where(mask, bias, NEG)
    out_t = np.concatenate([out[..., HEAD:].swapaxes(-1, -2), out[..., :HEAD].swapaxes(-1, -2)], axis=-1)
    return jnp.asarray(out), jnp.asarray(out_t)


def _attn_fwd(qkv, bias):
    S = qkv.shape[2]
    nblk = S // HEAD
    rows = 256

    def body(qkv_ref, b_ref, o_ref, lse_ref, o_s, lse_s):
        g = pl.program_id(1)
        bias = b_ref[0, 0]
        col = lax.broadcasted_iota(jnp.int32, bias.shape, 1)
        bias_first = jnp.where(col < HEAD, NEG, bias)

        for gi, d in enumerate(DILATIONS):
            @pl.when(g == gi)
            def _(gi=gi, d=d):
                nb = nblk // d

                def keys(start):
                    sl = pl.ds(start, HEAD, stride=d)
                    return qkv_ref.at[0, 1][sl, :].astype(BF16), qkv_ref.at[0, 2][sl, :].astype(BF16)

                def step(b, first_of_residue, before):
                    r, n = b // nb, b % nb
                    cur = pl.ds(n * (HEAD * d) + r, HEAD, stride=d)
                    own = keys(n * (HEAD * d) + r)
                    if first_of_residue:
                        before = own
                    q = qkv_ref.at[0, 0][cur, :].astype(BF16)
                    kw = jnp.concatenate([before[0], own[0]], axis=0)
                    vw = jnp.concatenate([before[1], own[1]], axis=0)
                    s = _nt(q, kw) * SCALE + jnp.where(n > 0, bias, bias_first)
                    m = jnp.max(s, axis=-1, keepdims=True)
                    p = jnp.exp(s - m)
                    l = jnp.sum(p, axis=-1, keepdims=True)
                    o_s.at[gi][cur, :] = _nn(p.astype(BF16), vw) / l
                    lse_s.at[gi][cur, :] = jnp.broadcast_to(m + jnp.log(l), (HEAD, HEAD))
                    return own

                def steps(i, before):
                    for u in range(UNROLL):
                        before = step(i * UNROLL + u, nb <= UNROLL and u % nb == 0, before)
                    return before

                lax.fori_loop(0, nblk // UNROLL, steps, keys(0))

        @pl.when(g == len(DILATIONS) - 1)
        def _():
            def merge(i, carry):
                r = pl.ds(pl.multiple_of(i * rows, rows), rows)
                ls = [lse_s[k, r, :] for k in range(3)]
                top = jnp.maximum(jnp.maximum(ls[0], ls[1]), ls[2])
                ws = [jnp.exp(t - top) for t in ls]
                den = ws[0] + ws[1] + ws[2]
                o_ref[r, :] = (ws[0] * o_s[0, r, :] + ws[1] * o_s[1, r, :] + ws[2] * o_s[2, r, :]) / den
                lse_ref[r, :] = top + jnp.log(den)
                return carry

            lax.fori_loop(0, S // rows, merge, 0)

    return pl.pallas_call(
        body, name="attn_fwd", grid=(N_SLOT, 3),
        in_specs=[pl.BlockSpec((1, 3, S, HEAD), lambda j, g: (g, 0, 0, j)),
                  pl.BlockSpec((1, 1, HEAD, 2 * HEAD), lambda j, g: (g, j, 0, 0))],
        out_specs=[pl.BlockSpec((S, HEAD), lambda j, g: (0, j)), pl.BlockSpec((S, HEAD), lambda j, g: (0, j))],
        out_shape=[jax.ShapeDtypeStruct((S, AOW), F32), jax.ShapeDtypeStruct((S, AOW), F32)],
        scratch_shapes=[pltpu.VMEM((3, S, HEAD), F32)] * 2,
        compiler_params=_cparams("parallel", "arbitrary"),
    )(qkv, bias)


def _shift_down(z, k, halo_rows):
    out = pltpu.roll(z, k, axis=0)
    top = out[:8]
    rid = lax.broadcasted_iota(jnp.int32, top.shape, 0)
    for t in range(k):
        top = jnp.where(rid == t, halo_rows[t], top)
    return jnp.concatenate([top, out[8:]], axis=0)


def _shift_up(z, k, halo_rows):
    n = z.shape[0]
    out = pltpu.roll(z, n - k, axis=0)
    bottom = out[n - 8:]
    rid = lax.broadcasted_iota(jnp.int32, bottom.shape, 0)
    for t in range(k):
        bottom = jnp.where(rid == 8 - k + t, halo_rows[t], bottom)
    return jnp.concatenate([out[:n - 8], bottom], axis=0)


def _e_spec(chunk, tm):
    return pl.BlockSpec((1, tm, D), lambda i, c=chunk: (c, i, 0))


def _e_prev_spec(chunk, tm):
    return pl.BlockSpec((1, 16, D), lambda i, c=chunk: (c, jnp.maximum(i * (tm // 16) - 1, 0), 0))


def _e_next_spec(chunk, tm, S):
    return pl.BlockSpec((1, 16, D), lambda i, c=chunk: (c, jnp.minimum((i + 1) * (tm // 16), S // 16 - 1), 0))


def _mix(o_attn, e, cw8, ba, bb, w_bat, w_bc):
    S = o_attn.shape[0]
    tm = 512

    def body(o_ref, cb_ref, cc_ref, cx_ref, ga_ref, gb_ref, ccp_ref, cxp_ref, cw_ref, ba_ref, bb_ref, wba_ref, wbc_ref,
             obf_ref, cbu_ref, ya_ref, yc_ref, mg_ref):
        i = pl.program_id(0)
        o = o_ref[...].astype(BF16)
        obf_ref[...] = o
        ya = _nt(o, wba_ref[...])
        z = cc_ref[0].astype(F32) * cx_ref[0].astype(F32)
        zp = ccp_ref[0].astype(F32) * cxp_ref[0].astype(F32) * (i > 0).astype(F32)
        z1 = _shift_down(z, 1, [zp[15:16]])
        z2 = _shift_down(z, 2, [zp[14:15], zp[15:16]])
        cw = cw_ref[...]
        u = cw[0:1] * z2 + cw[1:2] * z1 + cw[2:3] * z
        cbu = (cb_ref[0].astype(F32) * u).astype(BF16)
        cbu_ref[...] = cbu
        yc = _nn(cbu, wbc_ref[...])
        sa = _sigmoid(ga_ref[0].astype(F32) + ba_ref[...])
        sb = _sigmoid(gb_ref[0].astype(F32) + bb_ref[...])
        ya_ref[...] = ya.astype(BF16)
        yc_ref[...] = yc.astype(BF16)
        mg_ref[...] = (sa * ya + sb * yc).astype(BF16)

    row = lambda w: pl.BlockSpec((tm, w), lambda i: (i, 0))
    return pl.pallas_call(
        body, name="mix", grid=(S // tm,),
        in_specs=[row(AOW)] + [_e_spec(c, tm) for c in range(5)] + [_e_prev_spec(1, tm), _e_prev_spec(2, tm),
                  _const_spec((8, D)), _vec_spec(), _vec_spec(), _const_spec((D, AOW)), _const_spec((D, D))],
        out_specs=[row(AOW), row(D), row(D), row(D), row(D)],
        out_shape=[jax.ShapeDtypeStruct((S, AOW), BF16)] + [jax.ShapeDtypeStruct((S, D), BF16)] * 4,
        compiler_params=_cparams("parallel"),
    )(o_attn, e, e, e, e, e, e, e, cw8, ba, bb, w_bat, w_bc)


def _out_proj(merged, w_out, x, gate1, g_mlp, sc2, sh2):
    S = x.shape[0]
    tm = TM

    def body(mg_ref, w_ref, x_ref, gt_ref, g_ref, sc_ref, sh_ref, x1_ref, mo_ref, h2_ref):
        mo = _nn(mg_ref[...], w_ref[...])
        mo_ref[...] = mo.astype(BF16)
        x1 = x_ref[...] + gt_ref[...] * mo
        x1_ref[...] = x1
        h2 = x1 * _rms_r(x1) * g_ref[...] * (1.0 + sc_ref[...]) + sh_ref[...]
        h2_ref[...] = h2.astype(BF16)

    row = pl.BlockSpec((tm, D), lambda i: (i, 0))
    return pl.pallas_call(
        body, name="out_proj", grid=(S // tm,),
        in_specs=[row, _const_spec((D, D)), row, _vec_spec(), _vec_spec(), _vec_spec(), _vec_spec()],
        out_specs=[row, row, row],
        out_shape=[jax.ShapeDtypeStruct((S, D), F32), jax.ShapeDtypeStruct((S, D), BF16), jax.ShapeDtypeStruct((S, D), BF16)],
        compiler_params=_cparams("parallel"),
    )(merged, w_out, x, gate1, g_mlp, sc2, sh2)


def _mlp_in(h2, w_mit):
    S = h2.shape[0]
    tm, tn = TM, 2048

    def body(h_ref, w_ref, a_ref, f_ref):
        a = _nt(h_ref[...], w_ref[...])
        a_ref[...] = a.astype(BF16)
        f_ref[...] = jnp.square(jnp.maximum(a, 0.0)).astype(BF16)

    blk = pl.BlockSpec((tm, tn), lambda i, j: (i, j))
    return pl.pallas_call(
        body, name="mlp_in", grid=(S // tm, DFF // tn),
        in_specs=[pl.BlockSpec((tm, D), lambda i, j: (i, 0)), pl.BlockSpec((tn, D), lambda i, j: (j, 0))],
        out_specs=[blk, blk],
        out_shape=[jax.ShapeDtypeStruct((S, DFF), BF16)] * 2,
        compiler_params=_cparams("parallel", "parallel"),
    )(h2, w_mit)


def _mlp_out(f, w_mo, x1, gate2, g_fin, tgt):
    S = x1.shape[0]
    tm = 512
    half = tm // 2

    def body(f_ref, w_ref, x1_ref, gt_ref, g_ref, t_ref, mlp_ref, dx2_ref, pv_ref):
        @pl.when(pl.program_id(0) == 0)
        def _():
            pv_ref[...] = jnp.zeros_like(pv_ref)

        g = g_ref[...]
        for hs in (pl.ds(0, half), pl.ds(half, half)):
            mlp = _nn(f_ref[hs, :], w_ref[...])
            mlp_ref[hs, :] = mlp.astype(BF16)
            x2 = x1_ref[hs, :] + gt_ref[...] * mlp
            r = _rms_r(x2)
            err = x2 * r * g - t_ref[hs, :]
            dx2, pg = _rms_bwd(x2, r, g, err * (1.0 / D))
            dx2_ref[hs, :] = dx2
            pv_ref[0:1, :] += _rowsum(pg)
            pv_ref[1:2, :] += 0.5 * _rowsum(jnp.mean(err * err, axis=-1, keepdims=True))

    row = pl.BlockSpec((tm, D), lambda i: (i, 0))
    return pl.pallas_call(
        body, name="mlp_out", grid=(S // tm,),
        in_specs=[pl.BlockSpec((tm, DFF), lambda i: (i, 0)), _const_spec((DFF, D)), row, _vec_spec(), _vec_spec(), row],
        out_specs=[row, row, _const_spec((8, D))],
        out_shape=[jax.ShapeDtypeStruct((S, D), BF16), jax.ShapeDtypeStruct((S, D), F32), jax.ShapeDtypeStruct((8, D), F32)],
        compiler_params=_cparams("arbitrary"),
    )(f, w_mo, x1, gate2, g_fin, tgt)


def _bwd_mlp_a(dx2, gate2, mlp, w_mo, a):
    S = dx2.shape[0]
    tm = 512
    half = tm // 2

    def body(dx_ref, gt_ref, mlp_ref, w_ref, a_ref, da_ref, dmo_ref, pv_ref):
        @pl.when(pl.program_id(0) == 0)
        def _():
            pv_ref[...] = jnp.zeros_like(pv_ref)

        for hs in (pl.ds(0, half), pl.ds(half, half)):
            dx = dx_ref[hs, :]
            dmo = (dx * gt_ref[...]).astype(BF16)
            dmo_ref[hs, :] = dmo
            pv_ref[0:1, :] += _rowsum(dx * mlp_ref[hs, :].astype(F32))
            df = _nt(dmo, w_ref[...])
            da_ref[hs, :] = (df * (2.0 * jnp.maximum(a_ref[hs, :].astype(F32), 0.0))).astype(BF16)

    row = pl.BlockSpec((tm, D), lambda i: (i, 0))
    wide = pl.BlockSpec((tm, DFF), lambda i: (i, 0))
    return pl.pallas_call(
        body, name="bwd_mlp_a", grid=(S // tm,),
        in_specs=[row, _vec_spec(), row, _const_spec((DFF, D)), wide],
        out_specs=[wide, row, _const_spec((8, D))],
        out_shape=[jax.ShapeDtypeStruct((S, DFF), BF16), jax.ShapeDtypeStruct((S, D), BF16), jax.ShapeDtypeStruct((8, D), F32)],
        compiler_params=_cparams("arbitrary"),
    )(dx2, gate2, mlp, w_mo, a)


def _bwd_mlp_b(da, w_mit, x1, dx2, g_mlp, sc2):
    S = x1.shape[0]
    tm = 512
    half = tm // 2

    def body(da_ref, w_ref, x1_ref, dx2_ref, g_ref, sc_ref, dx1_ref, pv_ref):
        @pl.when(pl.program_id(0) == 0)
        def _():
            pv_ref[...] = jnp.zeros_like(pv_ref)

        g = g_ref[...]
        for hs in (pl.ds(0, half), pl.ds(half, half)):
            dh = _nn(da_ref[hs, :], w_ref[...])
            x1 = x1_ref[hs, :]
            r = _rms_r(x1)
            dxn, pg = _rms_bwd(x1, r, g, dh * (1.0 + sc_ref[...]))
            dx1_ref[hs, :] = dx2_ref[hs, :] + dxn
            pv_ref[0:1, :] += _rowsum(dh)
            pv_ref[1:2, :] += _rowsum(dh * (x1 * r * g))
            pv_ref[2:3, :] += _rowsum(pg)

    row = pl.BlockSpec((tm, D), lambda i: (i, 0))
    return pl.pallas_call(
        body, name="bwd_mlp_b", grid=(S // tm,),
        in_specs=[pl.BlockSpec((tm, DFF), lambda i: (i, 0)), _const_spec((DFF, D)), row, row, _vec_spec(), _vec_spec()],
        out_specs=[row, _const_spec((8, D))],
        out_shape=[jax.ShapeDtypeStruct((S, D), F32), jax.ShapeDtypeStruct((8, D), F32)],
        compiler_params=_cparams("arbitrary"),
    )(da, w_mit, x1, dx2, g_mlp, sc2)


def _bwd_mix(dx1, gate1, mo, e, cw8, ba, bb, ya, yc, o_attn, w_out, w_bc, w_bat):
    S = dx1.shape[0]
    tm = 256
    n_tiles = S // tm

    def body(dx_ref, dxn_ref, gt_ref, mo_ref, cb_ref, cc_ref, cx_ref, ga_ref, gb_ref, cbn_ref, gbn_ref, ccp_ref, cxp_ref,
             cw_ref, ba_ref, bb_ref, ya_ref, yc_ref, o_ref, wout_ref, wbc_ref, wba_ref,
             dmo_ref, dya_ref, dyc_ref, do_ref, dl_ref, de_ref, pv_ref):
        i = pl.program_id(0)

        @pl.when(i == 0)
        def _():
            pv_ref[...] = jnp.zeros_like(pv_ref)

        dx = dx_ref[...]
        cb = cb_ref[0].astype(F32)
        cc = cc_ref[0].astype(F32)
        cx = cx_ref[0].astype(F32)
        dmo_all = (jnp.concatenate([dx, dxn_ref[...]], axis=0) * gt_ref[...]).astype(BF16)
        dmg_all = _nt(dmo_all, wout_ref[...])
        sb_all = _sigmoid(jnp.concatenate([gb_ref[0], gbn_ref[0]], axis=0).astype(F32) + bb_ref[...])
        dyc_all = dmg_all * sb_all
        dcbu_all = _nt(dyc_all.astype(BF16), wbc_ref[...])
        dmo, dmg, sb, dyc, dcbu = dmo_all[:tm], dmg_all[:tm], sb_all[:tm], dyc_all[:tm], dcbu_all[:tm]
        dmo_ref[...] = dmo
        pv_ref[0:1, :] += _rowsum(dx * mo_ref[...].astype(F32))
        sa = _sigmoid(ga_ref[0].astype(F32) + ba_ref[...])
        dya = (dmg * sa).astype(BF16)
        dya_ref[...] = dya
        dyc_ref[...] = dyc.astype(BF16)
        dga = dmg * ya_ref[...].astype(F32) * sa * (1.0 - sa)
        dgb = dmg * yc_ref[...].astype(F32) * sb * (1.0 - sb)
        pv_ref[1:2, :] += _rowsum(dga)
        pv_ref[2:3, :] += _rowsum(dgb)

        do = _nn(dya, wba_ref[...])
        do_ref[...] = do
        prod = do * o_ref[...]
        dl_ref[...] = jnp.concatenate(
            [jnp.broadcast_to(jnp.sum(prod[:, s * HEAD:(s + 1) * HEAD], axis=-1, keepdims=True), (tm, HEAD))
             for s in range(N_SLOT)], axis=1)

        z = cc * cx
        zp = ccp_ref[0].astype(F32) * cxp_ref[0].astype(F32) * (i > 0).astype(F32)
        z1 = _shift_down(z, 1, [zp[15:16]])
        z2 = _shift_down(z, 2, [zp[14:15], zp[15:16]])
        cw = cw_ref[...]
        u = cw[0:1] * z2 + cw[1:2] * z1 + cw[2:3] * z
        du = dcbu * cb
        du_n = dcbu_all[tm:] * cbn_ref[0].astype(F32) * (i < n_tiles - 1).astype(F32)
        du1 = _shift_up(du, 1, [du_n[0:1]])
        du2 = _shift_up(du, 2, [du_n[0:1], du_n[1:2]])
        dz = cw[2:3] * du + cw[1:2] * du1 + cw[0:1] * du2
        pv_ref[3:4, :] += _rowsum(du * z2)
        pv_ref[4:5, :] += _rowsum(du * z1)
        pv_ref[5:6, :] += _rowsum(du * z)

        de_ref[0] = (dcbu * u).astype(BF16)
        de_ref[1] = (dz * cx).astype(BF16)
        de_ref[2] = (dz * cc).astype(BF16)
        de_ref[3] = dga.astype(BF16)
        de_ref[4] = dgb.astype(BF16)

    row = lambda w: pl.BlockSpec((tm, w), lambda i: (i, 0))
    nxt = pl.BlockSpec((16, D), lambda i: (jnp.minimum((i + 1) * (tm // 16), S // 16 - 1), 0))
    return pl.pallas_call(
        body, name="bwd_mix", grid=(n_tiles,),
        in_specs=[row(D), nxt, _vec_spec(), row(D)] + [_e_spec(c, tm) for c in range(5)]
                 + [_e_next_spec(0, tm, S), _e_next_spec(4, tm, S), _e_prev_spec(1, tm), _e_prev_spec(2, tm),
                    _const_spec((8, D)), _vec_spec(), _vec_spec(), row(D), row(D), row(AOW),
                    _const_spec((D, D)), _const_spec((D, D)), _const_spec((D, AOW))],
        out_specs=[row(D), row(D), row(D), row(AOW), row(AOW), pl.BlockSpec((5, tm, D), lambda i: (0, i, 0)),
                   _const_spec((8, D))],
        out_shape=[jax.ShapeDtypeStruct((S, D), BF16)] * 3 + [jax.ShapeDtypeStruct((S, AOW), F32)] * 2
                  + [jax.ShapeDtypeStruct((5, S, D), BF16), jax.ShapeDtypeStruct((8, D), F32)],
        compiler_params=_cparams("arbitrary"),
    )(dx1, dx1, gate1, mo, e, e, e, e, e, e, e, e, e, cw8, ba, bb, ya, yc, o_attn, w_out, w_bc, w_bat)


def _attn_bwd(qkv, do, lse, dl, bias_t):
    S = qkv.shape[2]
    nblk = S // HEAD

    def body(qkv_ref, do_ref, lse_ref, dl_ref, b_ref, d_ref):
        g = pl.program_id(1)
        bias = b_ref[0, 0]
        col = lax.broadcasted_iota(jnp.int32, bias.shape, 1)
        bias_last = jnp.where(col >= HEAD, NEG, bias)
        eye = (lax.broadcasted_iota(jnp.int32, (HEAD, HEAD), 0) == lax.broadcasted_iota(jnp.int32, (HEAD, HEAD), 1)).astype(F32)

        def as_row(t):
            return jnp.sum(t * eye, axis=0, keepdims=True)

        for gi, d in enumerate(DILATIONS):
            @pl.when(g == gi)
            def _(d=d):
                nb = nblk // d

                def query_side(start):
                    sl = pl.ds(start, HEAD, stride=d)
                    return (qkv_ref.at[0, 0][sl, :].astype(BF16), do_ref[sl, :].astype(BF16),
                            as_row(lse_ref[sl, :]), as_row(dl_ref[sl, :]))

                def step(b, first_of_residue, carry):
                    dq_part, own = carry
                    r, n = b // nb, b % nb
                    cur = pl.ds(n * (HEAD * d) + r, HEAD, stride=d)
                    if first_of_residue:
                        own = query_side(r)
                    nxt = query_side(jnp.minimum(n + 1, nb - 1) * (HEAD * d) + r)
                    q2 = jnp.concatenate([own[0], nxt[0]], axis=0)
                    do2 = jnp.concatenate([own[1], nxt[1]], axis=0)
                    k = qkv_ref.at[0, 1][cur, :].astype(BF16)
                    v = qkv_ref.at[0, 2][cur, :].astype(BF16)
                    s = _nt(k, q2) * SCALE + jnp.where(n < nb - 1, bias, bias_last)
                    p = jnp.exp(s - jnp.concatenate([own[2], nxt[2]], axis=1))
                    d_ref.at[0, 2][cur, :] = _nn(p.astype(BF16), do2)
                    dp = _nt(v, do2)
                    ds = (p * (dp - jnp.concatenate([own[3], nxt[3]], axis=1)) * SCALE).astype(BF16)
                    d_ref.at[0, 1][cur, :] = _nn(ds, q2)
                    dq2 = _tn(ds, k)
                    d_ref.at[0, 0][cur, :] = dq2[:HEAD] + jnp.where(n > 0, dq_part, 0.0)
                    return dq2[HEAD:], nxt

                def steps(i, carry):
                    for u in range(UNROLL):
                        carry = step(i * UNROLL + u, nb <= UNROLL and u % nb == 0, carry)
                    return carry

                lax.fori_loop(0, nblk // UNROLL, steps, (jnp.zeros((HEAD, HEAD), F32), query_side(0)))

    col_blk = pl.BlockSpec((S, HEAD), lambda j, g: (0, j))
    qkv_blk = pl.BlockSpec((1, 3, S, HEAD), lambda j, g: (g, 0, 0, j))
    return pl.pallas_call(
        body, name="attn_bwd", grid=(N_SLOT, 3),
        in_specs=[qkv_blk, col_blk, col_blk, col_blk, pl.BlockSpec((1, 1, HEAD, 2 * HEAD), lambda j, g: (g, j, 0, 0))],
        out_specs=qkv_blk,
        out_shape=jax.ShapeDtypeStruct((3, 3, S, AOW), F32),
        compiler_params=_cparams("parallel", "arbitrary"),
    )(qkv, do, lse, dl, bias_t)


def _bwd_in(dqkv, de, w_int, x, dx1, g_mix, sc1):
    S = x.shape[0]
    tm = TM
    dqkv = dqkv.reshape(3, 3, S, AOW)

    def body(dq_ref, de_ref, wq_ref, wk_ref, wv_ref, wa_ref, wb_ref, x_ref, dx1_ref, g_ref, sc_ref, gx_ref, pv_ref):
        acc = gx_ref
        i, k = pl.program_id(0), pl.program_id(1)

        @pl.when((i == 0) & (k == 0))
        def _():
            pv_ref[...] = jnp.zeros_like(pv_ref)

        @pl.when(k == 0)
        def _():
            acc[...] = jnp.zeros_like(acc)

        @pl.when(k < 3)
        def _():
            lhs = jnp.concatenate([dq_ref[0, t].astype(BF16) for t in range(3)], axis=1)
            acc[...] += _nn(lhs, jnp.concatenate([wq_ref[...], wk_ref[...], wv_ref[...]], axis=0))

        @pl.when(k >= 3)
        def _():
            acc[...] += _nn(de_ref[0], jnp.concatenate([wa_ref[...], wb_ref[...]], axis=0))

        @pl.when(k == 7)
        def _():
            dh = acc[...]
            xv = x_ref[...]
            r = _rms_r(xv)
            g = g_ref[...]
            dxn, pg = _rms_bwd(xv, r, g, dh * (1.0 + sc_ref[...]))
            gx_ref[...] = dx1_ref[...] + dxn
            pv_ref[0:1, :] += _rowsum(dh)
            pv_ref[1:2, :] += _rowsum(dh * (xv * r * g))
            pv_ref[2:3, :] += _rowsum(pg)

    grp = lambda k: jnp.minimum(k, 2)
    chunk = lambda k: jnp.maximum(k - 3, 0)
    wblk = lambda f: pl.BlockSpec((512, D), lambda i, k: (f(k), 0))
    row = pl.BlockSpec((tm, D), lambda i, k: (i, 0))
    once = pl.BlockSpec((tm, D), lambda i, k: (i, 0), pipeline_mode=pl.Buffered(1))
    return pl.pallas_call(
        body, name="bwd_in", grid=(S // tm, 8),
        in_specs=[pl.BlockSpec((1, 3, tm, 512), lambda i, k: (grp(k), 0, i, 0)),
                  pl.BlockSpec((1, tm, D), lambda i, k: (chunk(k), i, 0)),
                  wblk(grp), wblk(lambda k: 3 + grp(k)), wblk(lambda k: 6 + grp(k)),
                  wblk(lambda k: 9 + 2 * chunk(k)), wblk(lambda k: 10 + 2 * chunk(k)),
                  once, once, _vec_spec(), _vec_spec()],
        out_specs=[row, _const_spec((8, D))],
        out_shape=[jax.ShapeDtypeStruct((S, D), F32), jax.ShapeDtypeStruct((8, D), F32)],
        compiler_params=_cparams("arbitrary", "arbitrary"),
    )(dqkv, de, w_int, w_int, w_int, w_int, w_int, x, dx1, g_mix, sc1)


def _grad_w(name, a, b):
    S, ka = a.shape
    nb = b.shape[1]

    def body(a_ref, b_ref, o_ref):
        o_ref[...] = _tn(a_ref[...], b_ref[...]).astype(BF16)

    return pl.pallas_call(
        body, name=name, grid=(ka // 512,),
        in_specs=[pl.BlockSpec((S, 512), lambda n: (0, n)), pl.BlockSpec((S, nb), lambda n: (0, 0))],
        out_specs=pl.BlockSpec((512, nb), lambda n: (n, 0)),
        out_shape=jax.ShapeDtypeStruct((ka, nb), BF16),
        compiler_params=_cparams("parallel"),
    )(a, b)


def _grad_w_small(dya, o_bf, cbu, dyc, merged, dmo, after):
    S = dya.shape[0]

    def body(dya_ref, o_ref, cbu_ref, dyc_ref, mg_ref, dmo_ref, after_ref, gba_ref, gbc_ref, gout_ref):
        gba_ref[...] = _tn(dya_ref[...], o_ref[...]).astype(BF16)
        gbc_ref[...] = _tn(cbu_ref[...], dyc_ref[...]).astype(BF16)
        gout_ref[...] = _tn(mg_ref[...], dmo_ref[...]).astype(BF16)

    a_blk = pl.BlockSpec((S, 512), lambda n: (0, n))
    whole = lambda w: pl.BlockSpec((S, w), lambda n: (0, 0))
    out = lambda w: pl.BlockSpec((512, w), lambda n: (n, 0))
    return pl.pallas_call(
        body, name="grad_w_small", grid=(D // 512,),
        in_specs=[a_blk, whole(AOW), a_blk, whole(D), a_blk, whole(D), pl.BlockSpec(memory_space=pl.ANY)],
        out_specs=[out(AOW), out(D), out(D)],
        out_shape=[jax.ShapeDtypeStruct((D, AOW), BF16), jax.ShapeDtypeStruct((D, D), BF16), jax.ShapeDtypeStruct((D, D), BF16)],
        compiler_params=_cparams("parallel"),
    )(dya, o_bf, cbu, dyc, merged, dmo, after)


def _grad_w_in(dqkv, de, h):
    S = h.shape[0]

    def body(dq_ref, de_ref, h_ref, o_ref):
        n = pl.program_id(0)

        @pl.when(n < 9)
        def _():
            o_ref[...] = _tn(dq_ref[0].astype(BF16), h_ref[...]).astype(BF16)

        @pl.when(n >= 9)
        def _():
            o_ref[...] = _tn(de_ref[0], h_ref[...]).astype(BF16)

    def e_idx(n):
        kk = jnp.maximum(n - 9, 0)
        return (kk // 2, 0, kk % 2)

    return pl.pallas_call(
        body, name="grad_w_in", grid=(19,),
        in_specs=[pl.BlockSpec((1, S, 512), lambda n: (jnp.minimum(n, 8), 0, 0)), pl.BlockSpec((1, S, 512), e_idx),
                  pl.BlockSpec((S, D), lambda n: (0, 0))],
        out_specs=pl.BlockSpec((512, D), lambda n: (_win_rowblock(n), 0)),
        out_shape=jax.ShapeDtypeStruct((19 * 512, D), BF16),
        compiler_params=_cparams("parallel"),
    )(dqkv, de, h)


def _local_step(x, h, tgt, mod, g_mix, g_mlp, g_fin, ba, bb, cw8, w_int, mix_weights, mlp_weights, mlp_grads_ready, w_in_grad_ready,
                other_grads_ready):
    S = x.shape[0]
    sh1, sc1, gt1, sh2, sc2, gt2 = [mod[k:k + 1] for k in range(6)]
    bias, bias_t = _bias_table()

    qkv, e = _proj(h, w_int)
    qkv = qkv.reshape(3, 3, S, AOW)
    o_attn, lse = _attn_fwd(qkv, bias)
    w_bat, w_bc, w_out = mix_weights(o_attn)
    o_bf, cbu, ya, yc, merged = _mix(o_attn, e, cw8, ba, bb, w_bat, w_bc)
    x1, mo, h2 = _out_proj(merged, w_out, x, gt1, g_mlp, sc2, sh2)
    w_mit, w_mo = mlp_weights(x1)
    a, f = _mlp_in(h2, w_mit)
    mlp, dx2, pv_f = _mlp_out(f, w_mo, x1, gt2, g_fin, tgt)

    da, dmo2, pv_a = _bwd_mlp_a(dx2, gt2, mlp, w_mo, a)
    dx1, pv_b = _bwd_mlp_b(da, w_mit, x1, dx2, g_mlp, sc2)
    zero = mlp_grads_ready(_grad_w("grad_w_mi", da, h2), _grad_w("grad_w_mo", f, dmo2))
    dmo, dya, dyc, do, dl, de, pv_m = _bwd_mix(dx1, gt1 + zero, mo, e, cw8, ba, bb, ya, yc, o_attn, w_out, w_bc, w_bat)
    dqkv = _attn_bwd(qkv, do, lse, dl, bias_t).reshape(9, S, AOW)
    after = w_in_grad_ready(_grad_w_in(dqkv, de, h))
    zero = other_grads_ready(*_grad_w_small(dya, o_bf, cbu, dyc, merged, dmo, after))
    grad_x, pv_i = _bwd_in(dqkv, de, w_int, x, dx1, g_mix, sc1 + zero)

    vec = jnp.concatenate([pv_i[0:2], pv_m[0:1], pv_b[0:2], pv_a[0:1], pv_i[2:3], pv_b[2:3], pv_f[0:1],
                           pv_m[1:3], pv_m[3:6], pv_f[1:2], jnp.zeros((1, D), F32)], axis=0)
    return grad_x, vec


def _my_place():
    return lax.axis_index("x"), lax.axis_index("y"), lax.axis_index("c")


def _dev_index(px, py, pc):
    return 4 * px + 2 * py + pc


def _peer(x, y, c, m):
    return (x ^ ((m >> 2) & 1), y ^ ((m >> 1) & 1), c ^ (m & 1))


HBM_SPEC = pl.BlockSpec(memory_space=pltpu.HBM)
SEM_SPEC = pl.BlockSpec(memory_space=pltpu.SEMAPHORE)
N_PEER = N_DEV - 1


SPLIT_MASKS = {"gather": tuple(range(1, N_DEV)), "scatter": tuple(range(1, N_DEV)), "chips": (2, 4, 6), "sibling": (1, 1, 1, 1)}


def _split_copy(mode, src_ref, land_ref, send_sems, recv_sems, w, j, place, arriving=False):
    x, y, c = place
    masks = SPLIT_MASKS[mode]
    peer = _peer(x, y, c, masks[j])
    k = w * len(masks) + j
    sender, receiver = ((peer, (x, y, c)) if arriving else ((x, y, c), peer))
    if mode == "gather":
        r = src_ref.shape[0]
        src, dst = src_ref, land_ref.at[pl.ds(pl.multiple_of(_dev_index(*sender) * r, 16), r), :]
    elif mode == "scatter":
        r = land_ref.shape[1]
        src, dst = src_ref.at[pl.ds(pl.multiple_of(_dev_index(*receiver) * r, 16), r), :], land_ref.at[j]
    elif mode == "chips":
        src, dst = src_ref.at[2 * receiver[0] + receiver[1]], land_ref.at[j]
    else:
        r = land_ref.shape[1]
        src, dst = src_ref.at[pl.ds(pl.multiple_of((2 * j + receiver[2]) * r, 16), r), :], land_ref.at[j]
    return pltpu.make_async_remote_copy(src_ref=src, dst_ref=dst, send_sem=send_sems.at[k], recv_sem=recv_sems.at[k],
                                        device_id=peer, device_id_type=MESH)


def _split_start(name, mode, srcs, lands):
    n = len(srcs)
    nm = len(SPLIT_MASKS[mode])

    def body(*refs):
        src, land = refs[:n], refs[n:2 * n]
        send_sems, recv_sems = refs[2 * n], refs[2 * n + 1]
        token = refs[-1]
        place = _my_place()
        for w in range(n):
            for j in range(nm):
                _split_copy(mode, src[w], land[w], send_sems, recv_sems, w, j, place).start()
        token[...] = jnp.zeros_like(token)

    hbm = lambda t: pltpu.HBM(t.shape, t.dtype)
    out = pl.pallas_call(
        body, name=name,
        out_shape=(pltpu.SemaphoreType.DMA((n * nm,)), pltpu.SemaphoreType.DMA((n * nm,)), *[hbm(t) for t in srcs],
                   *[hbm(t) for t in lands], jax.ShapeDtypeStruct((8, 128), F32)),
        in_specs=(HBM_SPEC,) * (2 * n),
        out_specs=(SEM_SPEC, SEM_SPEC) + (HBM_SPEC,) * (2 * n) + (pl.BlockSpec(memory_space=pltpu.VMEM),),
        input_output_aliases={i: 2 + i for i in range(2 * n)},
        compiler_params=pltpu.CompilerParams(has_side_effects=pltpu.SideEffectType.DATAFLOW_SIDE_EFFECTING),
    )(*[pltpu.with_memory_space_constraint(t, pltpu.HBM) for t in (*srcs, *lands)])
    return out[0], out[1], out[2:2 + n], out[2 + n:2 + 2 * n], out[-1][0:1, 0:1], out[-1]


def _split_wait(name, mode, send_sems, recv_sems, srcs, lands, after):
    n = len(srcs)

    def body(*refs):
        src, land = refs[:n], refs[n:2 * n]
        ssem, rsem = refs[2 * n], refs[2 * n + 1]
        place = _my_place()
        for w in range(n):
            for j in range(len(SPLIT_MASKS[mode])):
                _split_copy(mode, src[w], land[w], ssem, rsem, w, j, place).wait_send()
                _split_copy(mode, src[w], land[w], ssem, rsem, w, j, place, arriving=True).wait_recv()

    hbm = lambda t: pltpu.HBM(t.shape, t.dtype)
    out = pl.pallas_call(
        body, name=name,
        out_shape=tuple(hbm(t) for t in (*srcs, *lands)),
        in_specs=(HBM_SPEC,) * (2 * n) + (SEM_SPEC, SEM_SPEC, pl.BlockSpec(memory_space=pl.ANY)),
        out_specs=(HBM_SPEC,) * (2 * n),
        input_output_aliases={i: i for i in range(2 * n)},
        compiler_params=pltpu.CompilerParams(has_side_effects=pltpu.SideEffectType.DATAFLOW_SIDE_EFFECTING),
    )(*srcs, *lands, send_sems, recv_sems, after)
    return out[:n], out[n:]


def _sibling_exchange(grads):
    nw = len(grads)
    HBM = pl.BlockSpec(memory_space=pl.ANY)

    def body(*refs):
        g, land = refs[:nw], refs[nw:2 * nw]
        send_sems, recv_sems = refs[2 * nw:]
        x, y, c = _my_place()

        def copy(w, q, owner_core):
            r = land[w].shape[1]
            return pltpu.make_async_remote_copy(
                src_ref=g[w].at[pl.ds(pl.multiple_of((2 * q + owner_core) * r, 16), r), :], dst_ref=land[w].at[q],
                send_sem=send_sems.at[w, q], recv_sem=recv_sems.at[w, q], device_id=(x, y, 1 - c), device_id_type=MESH)

        sends = [copy(w, q, 1 - c) for w in range(nw) for q in range(4)]
        for cp in sends:
            cp.start()
        for w in range(nw):
            for q in range(4):
                copy(w, q, c).wait_recv()
        for cp in sends:
            cp.wait_send()

    return pl.pallas_call(
        body, name="sibling_exchange",
        out_shape=[jax.ShapeDtypeStruct((4, a.shape[0] // N_DEV, a.shape[1]), a.dtype) for a in grads],
        in_specs=[HBM] * nw, out_specs=[HBM] * nw,
        scratch_shapes=[pltpu.SemaphoreType.DMA((nw, 4)), pltpu.SemaphoreType.DMA((nw, 4))],
    )(*grads)


def _pair_sums(gs, sibs, core):
    n = len(gs)

    def body(core_ref, *refs):
        for w in range(n):
            refs[2 * n + w][0] = (refs[w][0, 0].astype(F32) + refs[n + w][0].astype(F32)).astype(BF16)

    in_specs = [pl.BlockSpec((1, 1) + t.shape[1:], lambda q, core_ref: (q, core_ref[0], 0, 0)) for t in sibs]
    in_specs += [pl.BlockSpec((1,) + t.shape[1:], lambda q, core_ref: (q, 0, 0)) for t in sibs]
    return pl.pallas_call(
        body, name="pair_sums",
        grid_spec=pltpu.PrefetchScalarGridSpec(
            num_scalar_prefetch=1, grid=(4,), in_specs=in_specs,
            out_specs=[pl.BlockSpec((1,) + t.shape[1:], lambda q, core_ref: (q, 0, 0)) for t in sibs]),
        out_shape=[jax.ShapeDtypeStruct(t.shape, BF16) for t in sibs],
        compiler_params=_cparams("parallel"),
    )(core, *[g.reshape(4, 2, t.shape[1], t.shape[2]) for g, t in zip(gs, sibs)], *sibs)


def _allgather_small(v, name):
    r, ccols = v.shape

    def body(v_ref, out_ref, send_sems, recv_sems):
        x, y, c = _my_place()
        my_idx = _dev_index(x, y, c)
        out_ref[my_idx] = v_ref[...]

        def copy(m):
            peer = _peer(x, y, c, m)
            return pltpu.make_async_remote_copy(
                src_ref=v_ref, dst_ref=out_ref.at[my_idx],
                send_sem=send_sems.at[m - 1], recv_sem=recv_sems.at[m - 1], device_id=peer, device_id_type=MESH)

        def arrival(m):
            peer = _peer(x, y, c, m)
            return pltpu.make_async_remote_copy(
                src_ref=v_ref, dst_ref=out_ref.at[_dev_index(*peer)],
                send_sem=send_sems.at[m - 1], recv_sem=recv_sems.at[m - 1], device_id=peer, device_id_type=MESH)

        sends = [copy(m) for m in range(1, N_DEV)]
        for cp in sends:
            cp.start()
        for m in range(1, N_DEV):
            arrival(m).wait_recv()
        for cp in sends:
            cp.wait_send()

    return pl.pallas_call(
        body, name=name,
        out_shape=jax.ShapeDtypeStruct((N_DEV, r, ccols), v.dtype),
        in_specs=[pl.BlockSpec(memory_space=pltpu.VMEM)], out_specs=pl.BlockSpec(memory_space=pltpu.VMEM),
        scratch_shapes=[pltpu.SemaphoreType.DMA((7,)), pltpu.SemaphoreType.DMA((7,))],
    )(v)


def _gather_w_in_and_condition(shard, pay, w_ada, b_cols):
    r, ccols = shard.shape
    ncol = w_ada.shape[1]

    def body(sh_ref, pay_ref, w_ref, b_ref, full_ref, got_ref, act_ref, mod_ref, send_sems, recv_sems, small_send, small_recv, local_sem):
        x, y, c = _my_place()
        me, sibling = (x, y, c), (x, y, 1 - c)
        my_idx = _dev_index(x, y, c)
        chips = [(1 - x, y), (x, 1 - y), (1 - x, 1 - y)]

        def small(rnd, buf, m, arriving=False):
            peer = _peer(x, y, c, m)
            slot = _dev_index(*peer) if arriving else my_idx
            return pltpu.make_async_remote_copy(
                src_ref=buf.at[my_idx], dst_ref=buf.at[slot], send_sem=small_send.at[rnd, m - 1],
                recv_sem=small_recv.at[rnd, m - 1], device_id=peer, device_id_type=MESH)

        def rows(px, py, pc):
            return full_ref.at[pl.ds(pl.multiple_of(_dev_index(px, py, pc) * r, 16), r), :]

        def copy(k, block, to, src=None):
            return pltpu.make_async_remote_copy(
                src_ref=rows(*block) if src is None else src, dst_ref=rows(*block),
                send_sem=send_sems.at[k], recv_sem=recv_sems.at[k], device_id=to, device_id_type=MESH)

        got_ref[my_idx] = pay_ref[...]
        round1 = [small(0, got_ref, m) for m in range(1, N_DEV)]
        for cp in round1:
            cp.start()
        mine = pltpu.make_async_copy(sh_ref, rows(*me), local_sem)
        mine.start()
        first = [copy(0, me, sibling, src=sh_ref)] + [copy(1 + j, me, (*chip, c), src=sh_ref) for j, chip in enumerate(chips)]
        for cp in first:
            cp.start()

        for m in range(1, N_DEV):
            small(0, got_ref, m, arriving=True).wait_recv()
        cv = jnp.concatenate([got_ref[s, 0:1, :] for s in range(N_DEV)], axis=0)
        act = cv * _sigmoid(cv)
        act_ref[...] = act
        mod_ref[my_idx] = jnp.dot(act, w_ref[...], preferred_element_type=F32, precision=lax.Precision.HIGHEST) + b_ref[...]
        round2 = [small(1, mod_ref, m) for m in range(1, N_DEV)]
        for cp in round2:
            cp.start()

        passed = []
        for j, chip in enumerate(chips):
            copy(1 + j, (*chip, c), me).wait_recv()
            fwd = copy(4 + j, (*chip, c), sibling)
            fwd.start()
            passed.append(fwd)
        copy(0, sibling, me).wait_recv()
        for j, chip in enumerate(chips):
            copy(4 + j, (*chip, 1 - c), me).wait_recv()
        for m in range(1, N_DEV):
            small(1, mod_ref, m, arriving=True).wait_recv()
        for cp in first + passed + round1 + round2:
            cp.wait_send()
        mine.wait()

    anyspec = pl.BlockSpec(memory_space=pl.ANY)
    vmem = pl.BlockSpec(memory_space=pltpu.VMEM)
    return pl.pallas_call(
        body, name="gather_w_in_and_condition",
        out_shape=[jax.ShapeDtypeStruct((N_DEV * r, ccols), shard.dtype), jax.ShapeDtypeStruct((N_DEV, 8, D), F32),
                   jax.ShapeDtypeStruct((N_DEV, D), F32), jax.ShapeDtypeStruct((N_DEV, N_DEV, ncol), F32)],
        in_specs=[anyspec, vmem, vmem, vmem], out_specs=[anyspec, vmem, vmem, vmem],
        scratch_shapes=[pltpu.SemaphoreType.DMA((7,)), pltpu.SemaphoreType.DMA((7,)), pltpu.SemaphoreType.DMA((2, 7)),
                        pltpu.SemaphoreType.DMA((2, 7)), pltpu.SemaphoreType.DMA],
        compiler_params=_cparams(),
    )(shard, pay, w_ada, b_cols)


def _ada_bwd(act_t, gm_cols):
    def body(a_ref, g_ref, o_ref):
        o_ref[...] = jnp.dot(a_ref[...], g_ref[...], preferred_element_type=F32, precision=lax.Precision.HIGHEST)

    return pl.pallas_call(
        body, name="ada_bwd", out_shape=jax.ShapeDtypeStruct((D, gm_cols.shape[1]), F32), compiler_params=_cparams(),
    )(act_t, gm_cols)


def _row_tile(r):
    for t in (256, 304, 128, 64, 16):
        if r % t == 0:
            return t
    return r


def _sum_parts(parts, name, own=None):
    k, r, ccols = parts.shape
    tr = _row_tile(r)

    def body(*refs):
        p_ref, o_ref = refs[0], refs[-1]
        acc = p_ref[0].astype(F32) if own is None else refs[1][...].astype(F32) + p_ref[0].astype(F32)
        for s in range(1, k):
            acc = acc + p_ref[s].astype(F32)
        o_ref[...] = acc

    blk = pl.BlockSpec((tr, ccols), lambda i: (i, 0))
    return pl.pallas_call(
        body, name=name, grid=(r // tr,),
        in_specs=[pl.BlockSpec((k, tr, ccols), lambda i: (0, i, 0))] + ([] if own is None else [blk]),
        out_specs=blk,
        out_shape=jax.ShapeDtypeStruct((r, ccols), F32),
        compiler_params=_cparams("parallel"),
    )(*((parts,) if own is None else (parts, own)))


def _adamw(w, g, m, v, name):
    r, ccols = w.shape
    tr = _row_tile(r)
    c1 = 1.0 / (1.0 - B1 ** STEP)
    c2 = 1.0 / (1.0 - B2 ** STEP)

    def body(w_ref, g_ref, m_ref, v_ref, d_ref, nm_ref, nv_ref):
        gv = g_ref[...]
        nm = B1 * m_ref[...] + (1.0 - B1) * gv
        nv = B2 * v_ref[...] + (1.0 - B2) * jnp.square(gv)
        nm_ref[...] = nm
        nv_ref[...] = nv
        d_ref[...] = -LR * ((nm * c1) / (jnp.sqrt(nv * c2) + ADAM_EPS) + WD * w_ref[...])

    blk = pl.BlockSpec((tr, ccols), lambda i: (i, 0))
    return pl.pallas_call(
        body, name=name, grid=(r // tr,), in_specs=[blk] * 4, out_specs=[blk] * 3,
        out_shape=[jax.ShapeDtypeStruct((r, ccols), F32)] * 3,
        compiler_params=_cparams("parallel"),
    )(w, g, m, v)


def _sum_adamw(parts, own, w, m, v, name):
    k, r, ccols = parts.shape
    tr = _row_tile(r)
    c1 = 1.0 / (1.0 - B1 ** STEP)
    c2 = 1.0 / (1.0 - B2 ** STEP)

    def body(p_ref, own_ref, w_ref, m_ref, v_ref, g_ref, d_ref, nm_ref, nv_ref):
        gv = own_ref[...].astype(F32)
        for s in range(k):
            gv = gv + p_ref[s].astype(F32)
        g_ref[...] = gv
        nm = B1 * m_ref[...] + (1.0 - B1) * gv
        nv = B2 * v_ref[...] + (1.0 - B2) * jnp.square(gv)
        nm_ref[...] = nm
        nv_ref[...] = nv
        d_ref[...] = -LR * ((nm * c1) / (jnp.sqrt(nv * c2) + ADAM_EPS) + WD * w_ref[...])

    blk = pl.BlockSpec((tr, ccols), lambda i: (i, 0))
    return pl.pallas_call(
        body, name=name, grid=(r // tr,),
        in_specs=[pl.BlockSpec((k, tr, ccols), lambda i: (0, i, 0))] + [blk] * 4, out_specs=[blk] * 4,
        out_shape=[jax.ShapeDtypeStruct((r, ccols), F32)] * 4,
        compiler_params=_cparams("parallel"),
    )(parts, own, w, m, v)


VEC_ROWS = ((0, 6), (6, 7), (9, 11), (11, 14), (7, 8), (8, 9))


def _adamw_vectors(w, g, m, v):
    c1 = 1.0 / (1.0 - B1 ** STEP)
    c2 = 1.0 / (1.0 - B2 ** STEP)

    def put(refs, p):
        for ref, (lo, hi) in zip(refs, VEC_ROWS):
            if ref.shape == (3, HEAD):
                ref[...] = p[lo:hi, :HEAD]
            else:
                ref[...] = jnp.concatenate([p[k:k + 1] for k in range(lo, hi)], axis=1)

    def body(w_ref, g_ref, m_ref, v_ref, *outs):
        gv = g_ref[...]
        nm = B1 * m_ref[...] + (1.0 - B1) * gv
        nv = B2 * v_ref[...] + (1.0 - B2) * jnp.square(gv)
        delta = -LR * ((nm * c1) / (jnp.sqrt(nv * c2) + ADAM_EPS) + WD * w_ref[...])
        for kind, p in enumerate((gv, delta, nm, nv)):
            put(outs[6 * kind:6 * kind + 6], p)

    shapes = [(1, 6 * D), (1, D), (1, 2 * D), (3, HEAD), (1, D), (1, D)]
    out = pl.pallas_call(
        body, name="adamw_vectors", out_shape=[jax.ShapeDtypeStruct(sh, F32) for sh in shapes] * 4, compiler_params=_cparams(),
    )(w, g, m, v)
    fix = lambda t: (t[0], t[1], t[2], t[3][None], t[4], t[5].reshape(D))
    return [fix(out[6 * kind:6 * kind + 6]) for kind in range(4)]


def _pack_vectors(b_ada, g_mix, g_mlp, g_fin, b_gate, conv_w):
    conv_rows = jnp.pad(conv_w.reshape(3, HEAD), ((0, 0), (0, D - HEAD)))
    return jnp.concatenate([b_ada.reshape(6, D), g_mix.reshape(1, D), g_mlp.reshape(1, D), g_fin.reshape(1, D),
                            b_gate.reshape(2, D), conv_rows, jnp.zeros((2, D), F32)], axis=0)


def kernel(x, c, w_ada, b_ada, g_norm_mix, w_in, b_gate, conv_w, w_branch_attn, w_branch_conv, w_out, g_norm_mlp, w_mlp_in, w_mlp_out, g_norm_final, loss_target, m_w_ada, m_b_ada, m_g_norm_mix, m_w_in, m_b_gate, m_conv_w, m_w_branch_attn, m_w_branch_conv, m_w_out, m_g_norm_mlp, m_w_mlp_in, m_w_mlp_out, m_g_norm_final, v_w_ada, v_b_ada, v_g_norm_mix, v_w_in, v_b_gate, v_conv_w, v_w_branch_attn, v_w_branch_conv, v_w_out, v_g_norm_mlp, v_w_mlp_in, v_w_mlp_out, v_g_norm_final):
    S = x.shape[1]
    xi, yi, ci = _my_place()
    me = _dev_index(xi, yi, ci)
    x2 = x.reshape(S, D)
    tgt = loss_target.reshape(S, D)

    pay = jnp.zeros((8, D), F32).at[0].set(c[0]).at[1:4, :HEAD].set(conv_w[0])
    ncol = w_ada.shape[2]
    b_cols = lax.dynamic_slice(b_ada, (0, me * ncol), (1, ncol))
    w_int, got, act, mod_all = _gather_w_in_and_condition(w_in[0].T.astype(BF16), pay, w_ada[0], b_cols)
    cw8 = jnp.pad(got[:, 1:4, :HEAD].transpose(1, 0, 2).reshape(3, D), ((0, 5), (0, 0)))
    mod = lax.dynamic_index_in_dim(mod_all, me, axis=1, keepdims=False).reshape(6, D)
    late = [w_branch_attn[0].T.astype(BF16), w_branch_conv[0].astype(BF16), w_out[0].astype(BF16),
            w_mlp_in[0].T.astype(BF16), w_mlp_out[0].astype(BF16)]
    w_int, late = lax.optimization_barrier((w_int, late))
    zones = [lax.dynamic_update_slice(lax.empty((N_DEV * t.shape[0], t.shape[1]), BF16), t, (me * t.shape[0], 0)) for t in late]
    ag_mix = _split_start("gather_mix_start", "gather", late[:3], zones[:3])
    ag_mlp = _split_start("gather_mlp_start", "gather", late[3:], zones[3:])
    mod = mod + ag_mix[4] + ag_mlp[4]
    h = _prenorm(x2, g_norm_mix, mod[1:2], mod[0:1])

    def mix_weights(o_attn):
        return _split_wait("gather_mix_wait", "gather", *ag_mix[:4], o_attn)[1]

    def mlp_weights(x1):
        return _split_wait("gather_mlp_wait", "gather", *ag_mlp[:4], x1)[1]

    rs = {}

    def mlp_grads_ready(*grads):
        lands = [lax.empty((N_PEER, t.shape[0] // N_DEV, t.shape[1]), BF16) for t in grads]
        rs["mlp"] = _split_start("scatter_mlp_start", "scatter", grads, lands)
        return rs["mlp"][4]

    def w_in_grad_ready(g_in):
        r = g_in.shape[0] // N_DEV
        rs["sib"] = _split_start("sibling_w_in_start", "sibling", [g_in], [jnp.zeros((4, r, g_in.shape[1]), BF16)])
        return rs["sib"][5]

    def other_grads_ready(*small):
        core = ci.reshape(1).astype(jnp.int32)
        (g_in,), (sib_in,) = _split_wait("sibling_w_in_wait", "sibling", *rs["sib"][:4], small[0])
        pair = _pair_sums([g_in, *small], [sib_in, *_sibling_exchange(small)], core)
        lands = [lax.empty((3,) + t.shape[1:], BF16) for t in pair]
        rs["rest"] = _split_start("scatter_rest_start", "chips", pair, lands)
        return rs["rest"][4]

    ba, bb = b_gate[:, :D], b_gate[:, D:]
    grad_x, vec = _local_step(
        x2, h, tgt, mod, g_norm_mix, g_norm_mlp, g_norm_final.reshape(1, D), ba, bb, cw8, w_int, mix_weights, mlp_weights,
        mlp_grads_ready, w_in_grad_ready, other_grads_ready)

    vec_all = _allgather_small(vec, "gather_vec")
    vec_sum = _sum_parts(vec_all, "sum_vec")
    loss = vec_sum[14, 0]
    gm_all = vec_all[:, 0:6, :].reshape(N_DEV, 6 * D)
    gm_cols = lax.dynamic_slice(gm_all, (0, me * ncol), (N_DEV, ncol))
    g_w_ada = _ada_bwd(act.T, gm_cols)
    conv_cols = lax.dynamic_slice(vec_sum[11:14], (0, me * HEAD), (3, HEAD))
    g_pack = jnp.concatenate([vec_sum[0:11], jnp.pad(conv_cols, ((0, 0), (0, D - HEAD))), jnp.zeros((2, D), F32)], axis=0)
    packs = [_pack_vectors(*t) for t in ((b_ada, g_norm_mix, g_norm_mlp, g_norm_final, b_gate, conv_w),
                                         (m_b_ada, m_g_norm_mix, m_g_norm_mlp, m_g_norm_final, m_b_gate, m_conv_w),
                                         (v_b_ada, v_g_norm_mix, v_g_norm_mlp, v_g_norm_final, v_b_gate, v_conv_w))]
    gv, dv, mv, vv = _adamw_vectors(packs[0], g_pack, packs[1], packs[2])
    d_ada, nm_ada, nv_ada = _adamw(w_ada[0], g_w_ada, m_w_ada[0], v_w_ada[0], "adamw_w_ada")

    big = {}
    srcs, lands = _split_wait("scatter_mlp_wait", "scatter", *rs["mlp"][:4], d_ada)
    own = [lax.dynamic_slice(g, (me * land.shape[1], 0), land.shape[1:]) for g, land in zip(srcs, lands)]
    g_mi = _sum_parts(lands[0], "sum_w_mi", own=own[0]).T
    big["w_mi"] = (g_mi[None],) + tuple(t[None] for t in _adamw(w_mlp_in[0], g_mi, m_w_mlp_in[0], v_w_mlp_in[0], "adamw_w_mi"))
    big["w_mo"] = tuple(t[None] for t in _sum_adamw(lands[1], own[1], w_mlp_out[0], m_w_mlp_out[0], v_w_mlp_out[0], "adamw_w_mo"))
    srcs, lands = _split_wait("scatter_rest_wait", "chips", *rs["rest"][:4], big["w_mo"][1])
    own = [lax.dynamic_index_in_dim(pair, 2 * xi + yi, axis=0, keepdims=False) for pair in srcs]
    big["w_in"] = tuple(t.T[None] for t in _sum_adamw(lands[0], own[0], w_in[0].T, m_w_in[0].T, v_w_in[0].T, "adamw_w_in"))
    g_ba = _sum_parts(lands[1], "sum_w_ba", own=own[1]).T
    big["w_ba"] = (g_ba[None],) + tuple(t[None] for t in _adamw(w_branch_attn[0], g_ba, m_w_branch_attn[0], v_w_branch_attn[0], "adamw_w_ba"))
    big["w_bc"] = tuple(t[None] for t in _sum_adamw(lands[2], own[2], w_branch_conv[0], m_w_branch_conv[0], v_w_branch_conv[0], "adamw_w_bc"))
    big["w_out"] = tuple(t[None] for t in _sum_adamw(lands[3], own[3], w_out[0], m_w_out[0], v_w_out[0], "adamw_w_out"))

    def ordered(k, ada, vecs):
        return (ada[None], vecs[0], vecs[1], big["w_in"][k], vecs[2], vecs[3], big["w_ba"][k], big["w_bc"][k],
                big["w_out"][k], vecs[4], big["w_mi"][k], big["w_mo"][k], vecs[5])

    return (loss, grad_x.reshape(1, S, D), *ordered(0, g_w_ada, gv), *ordered(1, d_ada, dv),
            *ordered(2, nm_ada, mv), *ordered(3, nv_ada, vv))
```

```python
import numpy as np
import jax
import jax.numpy as jnp
from jax import lax
from jax.experimental import pallas as pl
from jax.experimental.pallas import tpu as pltpu

F32, BF16 = jnp.float32, jnp.bfloat16
D = 1024
HEAD = 128
DILATIONS = (1, 4, 16)
N_SLOT = 4
AOW = N_SLOT * HEAD
DFF = 4 * D
N_DEV = 8
UNROLL = 16
EPS = 1e-6
NEG = -1e30
SCALE = HEAD ** -0.5
LR, B1, B2, ADAM_EPS, WD, STEP = 0.001, 0.9, 0.999, 1e-08, 0.01, 10
V7X_VMEM_LIMIT = 56 * 1024 * 1024
TM = 1024
MESH = pl.DeviceIdType.MESH


def _cparams(*sem):
    if sem:
        return pltpu.CompilerParams(dimension_semantics=sem, vmem_limit_bytes=V7X_VMEM_LIMIT)
    return pltpu.CompilerParams(vmem_limit_bytes=V7X_VMEM_LIMIT)


def _nn(a, b):
    return jnp.dot(a, b, preferred_element_type=F32)


def _nt(a, b):
    return lax.dot_general(a, b, (((1,), (1,)), ((), ())), preferred_element_type=F32)


def _tn(a, b):
    return lax.dot_general(a, b, (((0,), (0,)), ((), ())), preferred_element_type=F32)


def _rms_r(x):
    return lax.rsqrt(jnp.mean(x * x, axis=-1, keepdims=True) + EPS)


def _rms_bwd(x, r, g, dn):
    gy = dn * g
    dx = r * gy - x * (r * r * r) * jnp.mean(x * gy, axis=-1, keepdims=True)
    return dx, dn * (x * r)


def _sigmoid(t):
    return 1.0 / (1.0 + jnp.exp(-t))


def _rowsum(v):
    return jnp.sum(v, axis=0, keepdims=True)


def _vec_spec(n=D):
    return pl.BlockSpec((1, n), lambda *_: (0, 0))


def _const_spec(shape):
    nd = len(shape)
    return pl.BlockSpec(shape, lambda *_: (0,) * nd)


def _win_rowblock(j):
    return jnp.where(j < 9, (j % 3) * 3 + j // 3, j)


def _prenorm(x, g, sc, sh):
    S = x.shape[0]
    tm = TM

    def body(x_ref, g_ref, sc_ref, sh_ref, h_ref):
        xv = x_ref[...]
        h_ref[...] = (xv * _rms_r(xv) * g_ref[...] * (1.0 + sc_ref[...]) + sh_ref[...]).astype(BF16)

    row = pl.BlockSpec((tm, D), lambda i: (i, 0))
    return pl.pallas_call(
        body, name="prenorm", grid=(S // tm,), in_specs=[row, _vec_spec(), _vec_spec(), _vec_spec()], out_specs=row,
        out_shape=jax.ShapeDtypeStruct((S, D), BF16), compiler_params=_cparams("parallel"),
    )(x, g, sc, sh)


def _proj(h, w_int):
    S = h.shape[0]

    def body(h_ref, w_ref, q_ref, e_ref):
        j = pl.program_id(0)
        acc = _nt(h_ref[...], w_ref[...])

        @pl.when(j < 9)
        def _():
            q_ref[0] = acc

        @pl.when(j >= 9)
        def _():
            e_ref[0] = acc.astype(BF16)

    def e_idx(j):
        k = jnp.maximum(j - 9, 0)
        return (k // 2, 0, k % 2)

    return pl.pallas_call(
        body, name="proj", grid=(19,),
        in_specs=[pl.BlockSpec((S, D), lambda j: (0, 0), pipeline_mode=pl.Buffered(1)),
                  pl.BlockSpec((512, D), lambda j: (_win_rowblock(j), 0))],
        out_specs=[pl.BlockSpec((1, S, 512), lambda j: (jnp.minimum(j, 8), 0, 0)), pl.BlockSpec((1, S, 512), e_idx)],
        out_shape=[jax.ShapeDtypeStruct((9, S, 512), F32), jax.ShapeDtypeStruct((5, S, D), BF16)],
        compiler_params=_cparams("arbitrary"),
    )(h, w_int)


def _bias_table():
    slopes = (2.0 ** (-8.0 * np.arange(1, 13, dtype=np.float32) / 12.0)).astype(np.float32)
    qi = np.arange(HEAD)[:, None]
    kj = np.arange(2 * HEAD)[None, :]
    delta = HEAD + qi - kj
    mask = (delta >= 0) & (delta <= HEAD)
    out = np.zeros((3, N_SLOT, HEAD, 2 * HEAD), np.float32)
    for gi, d in enumerate(DILATIONS):
        for j in range(N_SLOT):
            bias = -slopes[gi * N_SLOT + j] * (delta * d).astype(np.float32)
            out[gi, j] = np.where(mask, bias, NEG)
    out_t = np.concatenate([out[..., HEAD:].swapaxes(-1, -2), out[..., :HEAD].swapaxes(-1, -2)], axis=-1)
    return jnp.asarray(out), jnp.asarray(out_t)


def _attn_fwd(qkv, bias):
    S = qkv.shape[2]
    nblk = S // HEAD
    rows = 256

    def body(qkv_ref, b_ref, o_ref, lse_ref, o_s, lse_s):
        g = pl.program_id(1)
        bias = b_ref[0, 0]
        col = lax.broadcasted_iota(jnp.int32, bias.shape, 1)
        bias_first = jnp.where(col < HEAD, NEG, bias)

        for gi, d in enumerate(DILATIONS):
            @pl.when(g == gi)
            def _(gi=gi, d=d):
                nb = nblk // d

                def keys(start):
                    sl = pl.ds(start, HEAD, stride=d)
                    return qkv_ref.at[0, 1][sl, :].astype(BF16), qkv_ref.at[0, 2][sl, :].astype(BF16)

                def step(b, first_of_residue, before):
                    r, n = b // nb, b % nb
                    cur = pl.ds(n * (HEAD * d) + r, HEAD, stride=d)
                    own = keys(n * (HEAD * d) + r)
                    if first_of_residue:
                        before = own
                    q = qkv_ref.at[0, 0][cur, :].astype(BF16)
                    kw = jnp.concatenate([before[0], own[0]], axis=0)
                    vw = jnp.concatenate([before[1], own[1]], axis=0)
                    s = _nt(q, kw) * SCALE + jnp.where(n > 0, bias, bias_first)
                    m = jnp.max(s, axis=-1, keepdims=True)
                    p = jnp.exp(s - m)
                    l = jnp.sum(p, axis=-1, keepdims=True)
                    o_s.at[gi][cur, :] = _nn(p.astype(BF16), vw) / l
                    lse_s.at[gi][cur, :] = jnp.broadcast_to(m + jnp.log(l), (HEAD, HEAD))
                    return own

                def steps(i, before):
                    for u in range(UNROLL):
                        before = step(i * UNROLL + u, nb <= UNROLL and u % nb == 0, before)
                    return before

                lax.fori_loop(0, nblk // UNROLL, steps, keys(0))

        @pl.when(g == len(DILATIONS) - 1)
        def _():
            def merge(i, carry):
                r = pl.ds(pl.multiple_of(i * rows, rows), rows)
                ls = [lse_s[k, r, :] for k in range(3)]
                top = jnp.maximum(jnp.maximum(ls[0], ls[1]), ls[2])
                ws = [jnp.exp(t - top) for t in ls]
                den = ws[0] + ws[1] + ws[2]
                o_ref[r, :] = (ws[0] * o_s[0, r, :] + ws[1] * o_s[1, r, :] + ws[2] * o_s[2, r, :]) / den
                lse_ref[r, :] = top + jnp.log(den)
                return carry

            lax.fori_loop(0, S // rows, merge, 0)

    return pl.pallas_call(
        body, name="attn_fwd", grid=(N_SLOT, 3),
        in_specs=[pl.BlockSpec((1, 3, S, HEAD), lambda j, g: (g, 0, 0, j)),
                  pl.BlockSpec((1, 1, HEAD, 2 * HEAD), lambda j, g: (g, j, 0, 0))],
        out_specs=[pl.BlockSpec((S, HEAD), lambda j, g: (0, j)), pl.BlockSpec((S, HEAD), lambda j, g: (0, j))],
        out_shape=[jax.ShapeDtypeStruct((S, AOW), F32), jax.ShapeDtypeStruct((S, AOW), F32)],
        scratch_shapes=[pltpu.VMEM((3, S, HEAD), F32)] * 2,
        compiler_params=_cparams("parallel", "arbitrary"),
    )(qkv, bias)


def _shift_down(z, k, halo_rows):
    out = pltpu.roll(z, k, axis=0)
    top = out[:8]
    rid = lax.broadcasted_iota(jnp.int32, top.shape, 0)
    for t in range(k):
        top = jnp.where(rid == t, halo_rows[t], top)
    return jnp.concatenate([top, out[8:]], axis=0)


def _shift_up(z, k, halo_rows):
    n = z.shape[0]
    out = pltpu.roll(z, n - k, axis=0)
    bottom = out[n - 8:]
    rid = lax.broadcasted_iota(jnp.int32, bottom.shape, 0)
    for t in range(k):
        bottom = jnp.where(rid == 8 - k + t, halo_rows[t], bottom)
    return jnp.concatenate([out[:n - 8], bottom], axis=0)


def _e_spec(chunk, tm):
    return pl.BlockSpec((1, tm, D), lambda i, c=chunk: (c, i, 0))


def _e_prev_spec(chunk, tm):
    return pl.BlockSpec((1, 16, D), lambda i, c=chunk: (c, jnp.maximum(i * (tm // 16) - 1, 0), 0))


def _e_next_spec(chunk, tm, S):
    return pl.BlockSpec((1, 16, D), lambda i, c=chunk: (c, jnp.minimum((i + 1) * (tm // 16), S // 16 - 1), 0))


def _mix(o_attn, e, cw8, ba, bb, w_bat, w_bc):
    S = o_attn.shape[0]
    tm = 512

    def body(o_ref, cb_ref, cc_ref, cx_ref, ga_ref, gb_ref, ccp_ref, cxp_ref, cw_ref, ba_ref, bb_ref, wba_ref, wbc_ref,
             obf_ref, cbu_ref, ya_ref, yc_ref, mg_ref):
        i = pl.program_id(0)
        o = o_ref[...].astype(BF16)
        obf_ref[...] = o
        ya = _nt(o, wba_ref[...])
        z = cc_ref[0].astype(F32) * cx_ref[0].astype(F32)
        zp = ccp_ref[0].astype(F32) * cxp_ref[0].astype(F32) * (i > 0).astype(F32)
        z1 = _shift_down(z, 1, [zp[15:16]])
        z2 = _shift_down(z, 2, [zp[14:15], zp[15:16]])
        cw = cw_ref[...]
        u = cw[0:1] * z2 + cw[1:2] * z1 + cw[2:3] * z
        cbu = (cb_ref[0].astype(F32) * u).astype(BF16)
        cbu_ref[...] = cbu
        yc = _nn(cbu, wbc_ref[...])
        sa = _sigmoid(ga_ref[0].astype(F32) + ba_ref[...])
        sb = _sigmoid(gb_ref[0].astype(F32) + bb_ref[...])
        ya_ref[...] = ya.astype(BF16)
        yc_ref[...] = yc.astype(BF16)
        mg_ref[...] = (sa * ya + sb * yc).astype(BF16)

    row = lambda w: pl.BlockSpec((tm, w), lambda i: (i, 0))
    return pl.pallas_call(
        body, name="mix", grid=(S // tm,),
        in_specs=[row(AOW)] + [_e_spec(c, tm) for c in range(5)] + [_e_prev_spec(1, tm), _e_prev_spec(2, tm),
                  _const_spec((8, D)), _vec_spec(), _vec_spec(), _const_spec((D, AOW)), _const_spec((D, D))],
        out_specs=[row(AOW), row(D), row(D), row(D), row(D)],
        out_shape=[jax.ShapeDtypeStruct((S, AOW), BF16)] + [jax.ShapeDtypeStruct((S, D), BF16)] * 4,
        compiler_params=_cparams("parallel"),
    )(o_attn, e, e, e, e, e, e, e, cw8, ba, bb, w_bat, w_bc)


def _out_proj(merged, w_out, x, gate1, g_mlp, sc2, sh2):
    S = x.shape[0]
    tm = TM

    def body(mg_ref, w_ref, x_ref, gt_ref, g_ref, sc_ref, sh_ref, x1_ref, mo_ref, h2_ref):
        mo = _nn(mg_ref[...], w_ref[...])
        mo_ref[...] = mo.astype(BF16)
        x1 = x_ref[...] + gt_ref[...] * mo
        x1_ref[...] = x1
        h2 = x1 * _rms_r(x1) * g_ref[...] * (1.0 + sc_ref[...]) + sh_ref[...]
        h2_ref[...] = h2.astype(BF16)

    row = pl.BlockSpec((tm, D), lambda i: (i, 0))
    return pl.pallas_call(
        body, name="out_proj", grid=(S // tm,),
        in_specs=[row, _const_spec((D, D)), row, _vec_spec(), _vec_spec(), _vec_spec(), _vec_spec()],
        out_specs=[row, row, row],
        out_shape=[jax.ShapeDtypeStruct((S, D), F32), jax.ShapeDtypeStruct((S, D), BF16), jax.ShapeDtypeStruct((S, D), BF16)],
        compiler_params=_cparams("parallel"),
    )(merged, w_out, x, gate1, g_mlp, sc2, sh2)


def _mlp_in(h2, w_mit):
    S = h2.shape[0]
    tm, tn = TM, 2048

    def body(h_ref, w_ref, a_ref, f_ref):
        a = _nt(h_ref[...], w_ref[...])
        a_ref[...] = a.astype(BF16)
        f_ref[...] = jnp.square(jnp.maximum(a, 0.0)).astype(BF16)

    blk = pl.BlockSpec((tm, tn), lambda i, j: (i, j))
    return pl.pallas_call(
        body, name="mlp_in", grid=(S // tm, DFF // tn),
        in_specs=[pl.BlockSpec((tm, D), lambda i, j: (i, 0)), pl.BlockSpec((tn, D), lambda i, j: (j, 0))],
        out_specs=[blk, blk],
        out_shape=[jax.ShapeDtypeStruct((S, DFF), BF16)] * 2,
        compiler_params=_cparams("parallel", "parallel"),
    )(h2, w_mit)


def _mlp_out(f, w_mo, x1, gate2, g_fin, tgt):
    S = x1.shape[0]
    tm = 512
    half = tm // 2

    def body(f_ref, w_ref, x1_ref, gt_ref, g_ref, t_ref, mlp_ref, dx2_ref, pv_ref):
        @pl.when(pl.program_id(0) == 0)
        def _():
            pv_ref[...] = jnp.zeros_like(pv_ref)

        g = g_ref[...]
        for hs in (pl.ds(0, half), pl.ds(half, half)):
            mlp = _nn(f_ref[hs, :], w_ref[...])
            mlp_ref[hs, :] = mlp.astype(BF16)
            x2 = x1_ref[hs, :] + gt_ref[...] * mlp
            r = _rms_r(x2)
            err = x2 * r * g - t_ref[hs, :]
            dx2, pg = _rms_bwd(x2, r, g, err * (1.0 / D))
            dx2_ref[hs, :] = dx2
            pv_ref[0:1, :] += _rowsum(pg)
            pv_ref[1:2, :] += 0.5 * _rowsum(jnp.mean(err * err, axis=-1, keepdims=True))

    row = pl.BlockSpec((tm, D), lambda i: (i, 0))
    return pl.pallas_call(
        body, name="mlp_out", grid=(S // tm,),
        in_specs=[pl.BlockSpec((tm, DFF), lambda i: (i, 0)), _const_spec((DFF, D)), row, _vec_spec(), _vec_spec(), row],
        out_specs=[row, row, _const_spec((8, D))],
        out_shape=[jax.ShapeDtypeStruct((S, D), BF16), jax.ShapeDtypeStruct((S, D), F32), jax.ShapeDtypeStruct((8, D), F32)],
        compiler_params=_cparams("arbitrary"),
    )(f, w_mo, x1, gate2, g_fin, tgt)


def _bwd_mlp_a(dx2, gate2, mlp, w_mo, a):
    S = dx2.shape[0]
    tm = 512
    half = tm // 2

    def body(dx_ref, gt_ref, mlp_ref, w_ref, a_ref, da_ref, dmo_ref, pv_ref):
        @pl.when(pl.program_id(0) == 0)
        def _():
            pv_ref[...] = jnp.zeros_like(pv_ref)

        for hs in (pl.ds(0, half), pl.ds(half, half)):
            dx = dx_ref[hs, :]
            dmo = (dx * gt_ref[...]).astype(BF16)
            dmo_ref[hs, :] = dmo
            pv_ref[0:1, :] += _rowsum(dx * mlp_ref[hs, :].astype(F32))
            df = _nt(dmo, w_ref[...])
            da_ref[hs, :] = (df * (2.0 * jnp.maximum(a_ref[hs, :].astype(F32), 0.0))).astype(BF16)

    row = pl.BlockSpec((tm, D), lambda i: (i, 0))
    wide = pl.BlockSpec((tm, DFF), lambda i: (i, 0))
    return pl.pallas_call(
        body, name="bwd_mlp_a", grid=(S // tm,),
        in_specs=[row, _vec_spec(), row, _const_spec((DFF, D)), wide],
        out_specs=[wide, row, _const_spec((8, D))],
        out_shape=[jax.ShapeDtypeStruct((S, DFF), BF16), jax.ShapeDtypeStruct((S, D), BF16), jax.ShapeDtypeStruct((8, D), F32)],
        compiler_params=_cparams("arbitrary"),
    )(dx2, gate2, mlp, w_mo, a)


def _bwd_mlp_b(da, w_mit, x1, dx2, g_mlp, sc2):
    S = x1.shape[0]
    tm = 512
    half = tm // 2

    def body(da_ref, w_ref, x1_ref, dx2_ref, g_ref, sc_ref, dx1_ref, pv_ref):
        @pl.when(pl.program_id(0) == 0)
        def _():
            pv_ref[...] = jnp.zeros_like(pv_ref)

        g = g_ref[...]
        for hs in (pl.ds(0, half), pl.ds(half, half)):
            dh = _nn(da_ref[hs, :], w_ref[...])
            x1 = x1_ref[hs, :]
            r = _rms_r(x1)
            dxn, pg = _rms_bwd(x1, r, g, dh * (1.0 + sc_ref[...]))
            dx1_ref[hs, :] = dx2_ref[hs, :] + dxn
            pv_ref[0:1, :] += _rowsum(dh)
            pv_ref[1:2, :] += _rowsum(dh * (x1 * r * g))
            pv_ref[2:3, :] += _rowsum(pg)

    row = pl.BlockSpec((tm, D), lambda i: (i, 0))
    return pl.pallas_call(
        body, name="bwd_mlp_b", grid=(S // tm,),
        in_specs=[pl.BlockSpec((tm, DFF), lambda i: (i, 0)), _const_spec((DFF, D)), row, row, _vec_spec(), _vec_spec()],
        out_specs=[row, _const_spec((8, D))],
        out_shape=[jax.ShapeDtypeStruct((S, D), F32), jax.ShapeDtypeStruct((8, D), F32)],
        compiler_params=_cparams("arbitrary"),
    )(da, w_mit, x1, dx2, g_mlp, sc2)


def _bwd_mix(dx1, gate1, mo, e, cw8, ba, bb, ya, yc, o_attn, w_out, w_bc, w_bat):
    S = dx1.shape[0]
    tm = 256
    n_tiles = S // tm

    def body(dx_ref, dxn_ref, gt_ref, mo_ref, cb_ref, cc_ref, cx_ref, ga_ref, gb_ref, cbn_ref, gbn_ref, ccp_ref, cxp_ref,
             cw_ref, ba_ref, bb_ref, ya_ref, yc_ref, o_ref, wout_ref, wbc_ref, wba_ref,
             dmo_ref, dya_ref, dyc_ref, do_ref, dl_ref, de_ref, pv_ref):
        i = pl.program_id(0)

        @pl.when(i == 0)
        def _():
            pv_ref[...] = jnp.zeros_like(pv_ref)

        dx = dx_ref[...]
        cb = cb_ref[0].astype(F32)
        cc = cc_ref[0].astype(F32)
        cx = cx_ref[0].astype(F32)
        dmo_all = (jnp.concatenate([dx, dxn_ref[...]], axis=0) * gt_ref[...]).astype(BF16)
        dmg_all = _nt(dmo_all, wout_ref[...])
        sb_all = _sigmoid(jnp.concatenate([gb_ref[0], gbn_ref[0]], axis=0).astype(F32) + bb_ref[...])
        dyc_all = dmg_all * sb_all
        dcbu_all = _nt(dyc_all.astype(BF16), wbc_ref[...])
        dmo, dmg, sb, dyc, dcbu = dmo_all[:tm], dmg_all[:tm], sb_all[:tm], dyc_all[:tm], dcbu_all[:tm]
        dmo_ref[...] = dmo
        pv_ref[0:1, :] += _rowsum(dx * mo_ref[...].astype(F32))
        sa = _sigmoid(ga_ref[0].astype(F32) + ba_ref[...])
        dya = (dmg * sa).astype(BF16)
        dya_ref[...] = dya
        dyc_ref[...] = dyc.astype(BF16)
        dga = dmg * ya_ref[...].astype(F32) * sa * (1.0 - sa)
        dgb = dmg * yc_ref[...].astype(F32) * sb * (1.0 - sb)
        pv_ref[1:2, :] += _rowsum(dga)
        pv_ref[2:3, :] += _rowsum(dgb)

        do = _nn(dya, wba_ref[...])
        do_ref[...] = do
        prod = do * o_ref[...]
        dl_ref[...] = jnp.concatenate(
            [jnp.broadcast_to(jnp.sum(prod[:, s * HEAD:(s + 1) * HEAD], axis=-1, keepdims=True), (tm, HEAD))
             for s in range(N_SLOT)], axis=1)

        z = cc * cx
        zp = ccp_ref[0].astype(F32) * cxp_ref[0].astype(F32) * (i > 0).astype(F32)
        z1 = _shift_down(z, 1, [zp[15:16]])
        z2 = _shift_down(z, 2, [zp[14:15], zp[15:16]])
        cw = cw_ref[...]
        u = cw[0:1] * z2 + cw[1:2] * z1 + cw[2:3] * z
        du = dcbu * cb
        du_n = dcbu_all[tm:] * cbn_ref[0].astype(F32) * (i < n_tiles - 1).astype(F32)
        du1 = _shift_up(du, 1, [du_n[0:1]])
        du2 = _shift_up(du, 2, [du_n[0:1], du_n[1:2]])
        dz = cw[2:3] * du + cw[1:2] * du1 + cw[0:1] * du2
        pv_ref[3:4, :] += _rowsum(du * z2)
        pv_ref[4:5, :] += _rowsum(du * z1)
        pv_ref[5:6, :] += _rowsum(du * z)

        de_ref[0] = (dcbu * u).astype(BF16)
        de_ref[1] = (dz * cx).astype(BF16)
        de_ref[2] = (dz * cc).astype(BF16)
        de_ref[3] = dga.astype(BF16)
        de_ref[4] = dgb.astype(BF16)

    row = lambda w: pl.BlockSpec((tm, w), lambda i: (i, 0))
    nxt = pl.BlockSpec((16, D), lambda i: (jnp.minimum((i + 1) * (tm // 16), S // 16 - 1), 0))
    return pl.pallas_call(
        body, name="bwd_mix", grid=(n_tiles,),
        in_specs=[row(D), nxt, _vec_spec(), row(D)] + [_e_spec(c, tm) for c in range(5)]
                 + [_e_next_spec(0, tm, S), _e_next_spec(4, tm, S), _e_prev_spec(1, tm), _e_prev_spec(2, tm),
                    _const_spec((8, D)), _vec_spec(), _vec_spec(), row(D), row(D), row(AOW),
                    _const_spec((D, D)), _const_spec((D, D)), _const_spec((D, AOW))],
        out_specs=[row(D), row(D), row(D), row(AOW), row(AOW), pl.BlockSpec((5, tm, D), lambda i: (0, i, 0)),
                   _const_spec((8, D))],
        out_shape=[jax.ShapeDtypeStruct((S, D), BF16)] * 3 + [jax.ShapeDtypeStruct((S, AOW), F32)] * 2
                  + [jax.ShapeDtypeStruct((5, S, D), BF16), jax.ShapeDtypeStruct((8, D), F32)],
        compiler_params=_cparams("arbitrary"),
    )(dx1, dx1, gate1, mo, e, e, e, e, e, e, e, e, e, cw8, ba, bb, ya, yc, o_attn, w_out, w_bc, w_bat)


def _attn_bwd(qkv, do, lse, dl, bias_t):
    S = qkv.shape[2]
    nblk = S // HEAD

    def body(qkv_ref, do_ref, lse_ref, dl_ref, b_ref, d_ref):
        g = pl.program_id(1)
        bias = b_ref[0, 0]
        col = lax.broadcasted_iota(jnp.int32, bias.shape, 1)
        bias_last = jnp.where(col >= HEAD, NEG, bias)
        eye = (lax.broadcasted_iota(jnp.int32, (HEAD, HEAD), 0) == lax.broadcasted_iota(jnp.int32, (HEAD, HEAD), 1)).astype(F32)

        def as_row(t):
            return jnp.sum(t * eye, axis=0, keepdims=True)

        for gi, d in enumerate(DILATIONS):
            @pl.when(g == gi)
            def _(d=d):
                nb = nblk // d

                def query_side(start):
                    sl = pl.ds(start, HEAD, stride=d)
                    return (qkv_ref.at[0, 0][sl, :].astype(BF16), do_ref[sl, :].astype(BF16),
                            as_row(lse_ref[sl, :]), as_row(dl_ref[sl, :]))

                def step(b, first_of_residue, carry):
                    dq_part, own = carry
                    r, n = b // nb, b % nb
                    cur = pl.ds(n * (HEAD * d) + r, HEAD, stride=d)
                    if first_of_residue:
                        own = query_side(r)
                    nxt = query_side(jnp.minimum(n + 1, nb - 1) * (HEAD * d) + r)
                    q2 = jnp.concatenate([own[0], nxt[0]], axis=0)
                    do2 = jnp.concatenate([own[1], nxt[1]], axis=0)
                    k = qkv_ref.at[0, 1][cur, :].astype(BF16)
                    v = qkv_ref.at[0, 2][cur, :].astype(BF16)
                    s = _nt(k, q2) * SCALE + jnp.where(n < nb - 1, bias, bias_last)
                    p = jnp.exp(s - jnp.concatenate([own[2], nxt[2]], axis=1))
                    d_ref.at[0, 2][cur, :] = _nn(p.astype(BF16), do2)
                    dp = _nt(v, do2)
                    ds = (p * (dp - jnp.concatenate([own[3], nxt[3]], axis=1)) * SCALE).astype(BF16)
                    d_ref.at[0, 1][cur, :] = _nn(ds, q2)
                    dq2 = _tn(ds, k)
                    d_ref.at[0, 0][cur, :] = dq2[:HEAD] + jnp.where(n > 0, dq_part, 0.0)
                    return dq2[HEAD:], nxt

                def steps(i, carry):
                    for u in range(UNROLL):
                        carry = step(i * UNROLL + u, nb <= UNROLL and u % nb == 0, carry)
                    return carry

                lax.fori_loop(0, nblk // UNROLL, steps, (jnp.zeros((HEAD, HEAD), F32), query_side(0)))

    col_blk = pl.BlockSpec((S, HEAD), lambda j, g: (0, j))
    qkv_blk = pl.BlockSpec((1, 3, S, HEAD), lambda j, g: (g, 0, 0, j))
    return pl.pallas_call(
        body, name="attn_bwd", grid=(N_SLOT, 3),
        in_specs=[qkv_blk, col_blk, col_blk, col_blk, pl.BlockSpec((1, 1, HEAD, 2 * HEAD), lambda j, g: (g, j, 0, 0))],
        out_specs=qkv_blk,
        out_shape=jax.ShapeDtypeStruct((3, 3, S, AOW), F32),
        compiler_params=_cparams("parallel", "arbitrary"),
    )(qkv, do, lse, dl, bias_t)


def _bwd_in(dqkv, de, w_int, x, dx1, g_mix, sc1):
    S = x.shape[0]
    tm = TM
    dqkv = dqkv.reshape(3, 3, S, AOW)

    def body(dq_ref, de_ref, wq_ref, wk_ref, wv_ref, wa_ref, wb_ref, x_ref, dx1_ref, g_ref, sc_ref, gx_ref, pv_ref):
        acc = gx_ref
        i, k = pl.program_id(0), pl.program_id(1)

        @pl.when((i == 0) & (k == 0))
        def _():
            pv_ref[...] = jnp.zeros_like(pv_ref)

        @pl.when(k == 0)
        def _():
            acc[...] = jnp.zeros_like(acc)

        @pl.when(k < 3)
        def _():
            lhs = jnp.concatenate([dq_ref[0, t].astype(BF16) for t in range(3)], axis=1)
            acc[...] += _nn(lhs, jnp.concatenate([wq_ref[...], wk_ref[...], wv_ref[...]], axis=0))

        @pl.when(k >= 3)
        def _():
            acc[...] += _nn(de_ref[0], jnp.concatenate([wa_ref[...], wb_ref[...]], axis=0))

        @pl.when(k == 7)
        def _():
            dh = acc[...]
            xv = x_ref[...]
            r = _rms_r(xv)
            g = g_ref[...]
            dxn, pg = _rms_bwd(xv, r, g, dh * (1.0 + sc_ref[...]))
            gx_ref[...] = dx1_ref[...] + dxn
            pv_ref[0:1, :] += _rowsum(dh)
            pv_ref[1:2, :] += _rowsum(dh * (xv * r * g))
            pv_ref[2:3, :] += _rowsum(pg)

    grp = lambda k: jnp.minimum(k, 2)
    chunk = lambda k: jnp.maximum(k - 3, 0)
    wblk = lambda f: pl.BlockSpec((512, D), lambda i, k: (f(k), 0))
    row = pl.BlockSpec((tm, D), lambda i, k: (i, 0))
    once = pl.BlockSpec((tm, D), lambda i, k: (i, 0), pipeline_mode=pl.Buffered(1))
    return pl.pallas_call(
        body, name="bwd_in", grid=(S // tm, 8),
        in_specs=[pl.BlockSpec((1, 3, tm, 512), lambda i, k: (grp(k), 0, i, 0)),
                  pl.BlockSpec((1, tm, D), lambda i, k: (chunk(k), i, 0)),
                  wblk(grp), wblk(lambda k: 3 + grp(k)), wblk(lambda k: 6 + grp(k)),
                  wblk(lambda k: 9 + 2 * chunk(k)), wblk(lambda k: 10 + 2 * chunk(k)),
                  once, once, _vec_spec(), _vec_spec()],
        out_specs=[row, _const_spec((8, D))],
        out_shape=[jax.ShapeDtypeStruct((S, D), F32), jax.ShapeDtypeStruct((8, D), F32)],
        compiler_params=_cparams("arbitrary", "arbitrary"),
    )(dqkv, de, w_int, w_int, w_int, w_int, w_int, x, dx1, g_mix, sc1)


def _grad_w(name, a, b):
    S, ka = a.shape
    nb = b.shape[1]

    def body(a_ref, b_ref, o_ref):
        o_ref[...] = _tn(a_ref[...], b_ref[...]).astype(BF16)

    return pl.pallas_call(
        body, name=name, grid=(ka // 512,),
        in_specs=[pl.BlockSpec((S, 512), lambda n: (0, n)), pl.BlockSpec((S, nb), lambda n: (0, 0))],
        out_specs=pl.BlockSpec((512, nb), lambda n: (n, 0)),
        out_shape=jax.ShapeDtypeStruct((ka, nb), BF16),
        compiler_params=_cparams("parallel"),
    )(a, b)


def _grad_w_small(dya, o_bf, cbu, dyc, merged, dmo, after):
    S = dya.shape[0]

    def body(dya_ref, o_ref, cbu_ref, dyc_ref, mg_ref, dmo_ref, after_ref, gba_ref, gbc_ref, gout_ref):
        gba_ref[...] = _tn(dya_ref[...], o_ref[...]).astype(BF16)
        gbc_ref[...] = _tn(cbu_ref[...], dyc_ref[...]).astype(BF16)
        gout_ref[...] = _tn(mg_ref[...], dmo_ref[...]).astype(BF16)

    a_blk = pl.BlockSpec((S, 512), lambda n: (0, n))
    whole = lambda w: pl.BlockSpec((S, w), lambda n: (0, 0))
    out = lambda w: pl.BlockSpec((512, w), lambda n: (n, 0))
    return pl.pallas_call(
        body, name="grad_w_small", grid=(D // 512,),
        in_specs=[a_blk, whole(AOW), a_blk, whole(D), a_blk, whole(D), pl.BlockSpec(memory_space=pl.ANY)],
        out_specs=[out(AOW), out(D), out(D)],
        out_shape=[jax.ShapeDtypeStruct((D, AOW), BF16), jax.ShapeDtypeStruct((D, D), BF16), jax.ShapeDtypeStruct((D, D), BF16)],
        compiler_params=_cparams("parallel"),
    )(dya, o_bf, cbu, dyc, merged, dmo, after)


def _grad_w_in(dqkv, de, h):
    S = h.shape[0]

    def body(dq_ref, de_ref, h_ref, o_ref):
        n = pl.program_id(0)

        @pl.when(n < 9)
        def _():
            o_ref[...] = _tn(dq_ref[0].astype(BF16), h_ref[...]).astype(BF16)

        @pl.when(n >= 9)
        def _():
            o_ref[...] = _tn(de_ref[0], h_ref[...]).astype(BF16)

    def e_idx(n):
        kk = jnp.maximum(n - 9, 0)
        return (kk // 2, 0, kk % 2)

    return pl.pallas_call(
        body, name="grad_w_in", grid=(19,),
        in_specs=[pl.BlockSpec((1, S, 512), lambda n: (jnp.minimum(n, 8), 0, 0)), pl.BlockSpec((1, S, 512), e_idx),
                  pl.BlockSpec((S, D), lambda n: (0, 0))],
        out_specs=pl.BlockSpec((512, D), lambda n: (_win_rowblock(n), 0)),
        out_shape=jax.ShapeDtypeStruct((19 * 512, D), BF16),
        compiler_params=_cparams("parallel"),
    )(dqkv, de, h)


def _local_step(x, h, tgt, mod, g_mix, g_mlp, g_fin, ba, bb, cw8, w_int, mix_weights, mlp_weights, mlp_grads_ready, w_in_grad_ready,
                other_grads_ready):
    S = x.shape[0]
    sh1, sc1, gt1, sh2, sc2, gt2 = [mod[k:k + 1] for k in range(6)]
    bias, bias_t = _bias_table()

    qkv, e = _proj(h, w_int)
    qkv = qkv.reshape(3, 3, S, AOW)
    o_attn, lse = _attn_fwd(qkv, bias)
    w_bat, w_bc, w_out = mix_weights(o_attn)
    o_bf, cbu, ya, yc, merged = _mix(o_attn, e, cw8, ba, bb, w_bat, w_bc)
    x1, mo, h2 = _out_proj(merged, w_out, x, gt1, g_mlp, sc2, sh2)
    w_mit, w_mo = mlp_weights(x1)
    a, f = _mlp_in(h2, w_mit)
    mlp, dx2, pv_f = _mlp_out(f, w_mo, x1, gt2, g_fin, tgt)

    da, dmo2, pv_a = _bwd_mlp_a(dx2, gt2, mlp, w_mo, a)
    dx1, pv_b = _bwd_mlp_b(da, w_mit, x1, dx2, g_mlp, sc2)
    zero = mlp_grads_ready(_grad_w("grad_w_mi", da, h2), _grad_w("grad_w_mo", f, dmo2))
    dmo, dya, dyc, do, dl, de, pv_m = _bwd_mix(dx1, gt1 + zero, mo, e, cw8, ba, bb, ya, yc, o_attn, w_out, w_bc, w_bat)
    dqkv = _attn_bwd(qkv, do, lse, dl, bias_t).reshape(9, S, AOW)
    after = w_in_grad_ready(_grad_w_in(dqkv, de, h))
    zero = other_grads_ready(*_grad_w_small(dya, o_bf, cbu, dyc, merged, dmo, after))
    grad_x, pv_i = _bwd_in(dqkv, de, w_int, x, dx1, g_mix, sc1 + zero)

    vec = jnp.concatenate([pv_i[0:2], pv_m[0:1], pv_b[0:2], pv_a[0:1], pv_i[2:3], pv_b[2:3], pv_f[0:1],
                           pv_m[1:3], pv_m[3:6], pv_f[1:2], jnp.zeros((1, D), F32)], axis=0)
    return grad_x, vec


def _my_place():
    return lax.axis_index("x"), lax.axis_index("y"), lax.axis_index("c")


def _dev_index(px, py, pc):
    return 4 * px + 2 * py + pc


def _peer(x, y, c, m):
    return (x ^ ((m >> 2) & 1), y ^ ((m >> 1) & 1), c ^ (m & 1))


HBM_SPEC = pl.BlockSpec(memory_space=pltpu.HBM)
SEM_SPEC = pl.BlockSpec(memory_space=pltpu.SEMAPHORE)
N_PEER = N_DEV - 1


SPLIT_MASKS = {"gather": tuple(range(1, N_DEV)), "scatter": tuple(range(1, N_DEV)), "chips": (2, 4, 6), "sibling": (1, 1, 1, 1)}


def _split_copy(mode, src_ref, land_ref, send_sems, recv_sems, w, j, place, arriving=False):
    x, y, c = place
    masks = SPLIT_MASKS[mode]
    peer = _peer(x, y, c, masks[j])
    k = w * len(masks) + j
    sender, receiver = ((peer, (x, y, c)) if arriving else ((x, y, c), peer))
    if mode == "gather":
        r = src_ref.shape[0]
        src, dst = src_ref, land_ref.at[pl.ds(pl.multiple_of(_dev_index(*sender) * r, 16), r), :]
    elif mode == "scatter":
        r = land_ref.shape[1]
        src, dst = src_ref.at[pl.ds(pl.multiple_of(_dev_index(*receiver) * r, 16), r), :], land_ref.at[j]
    elif mode == "chips":
        src, dst = src_ref.at[2 * receiver[0] + receiver[1]], land_ref.at[j]
    else:
        r = land_ref.shape[1]
        src, dst = src_ref.at[pl.ds(pl.multiple_of((2 * j + receiver[2]) * r, 16), r), :], land_ref.at[j]
    return pltpu.make_async_remote_copy(src_ref=src, dst_ref=dst, send_sem=send_sems.at[k], recv_sem=recv_sems.at[k],
                                        device_id=peer, device_id_type=MESH)


def _split_start(name, mode, srcs, lands):
    n = len(srcs)
    nm = len(SPLIT_MASKS[mode])

    def body(*refs):
        src, land = refs[:n], refs[n:2 * n]
        send_sems, recv_sems = refs[2 * n], refs[2 * n + 1]
        token = refs[-1]
        place = _my_place()
        for w in range(n):
            for j in range(nm):
                _split_copy(mode, src[w], land[w], send_sems, recv_sems, w, j, place).start()
        token[...] = jnp.zeros_like(token)

    hbm = lambda t: pltpu.HBM(t.shape, t.dtype)
    out = pl.pallas_call(
        body, name=name,
        out_shape=(pltpu.SemaphoreType.DMA((n * nm,)), pltpu.SemaphoreType.DMA((n * nm,)), *[hbm(t) for t in srcs],
                   *[hbm(t) for t in lands], jax.ShapeDtypeStruct((8, 128), F32)),
        in_specs=(HBM_SPEC,) * (2 * n),
        out_specs=(SEM_SPEC, SEM_SPEC) + (HBM_SPEC,) * (2 * n) + (pl.BlockSpec(memory_space=pltpu.VMEM),),
        input_output_aliases={i: 2 + i for i in range(2 * n)},
        compiler_params=pltpu.CompilerParams(has_side_effects=pltpu.SideEffectType.DATAFLOW_SIDE_EFFECTING),
    )(*[pltpu.with_memory_space_constraint(t, pltpu.HBM) for t in (*srcs, *lands)])
    return out[0], out[1], out[2:2 + n], out[2 + n:2 + 2 * n], out[-1][0:1, 0:1], out[-1]


def _split_wait(name, mode, send_sems, recv_sems, srcs, lands, after):
    n = len(srcs)

    def body(*refs):
        src, land = refs[:n], refs[n:2 * n]
        ssem, rsem = refs[2 * n], refs[2 * n + 1]
        place = _my_place()
        for w in range(n):
            for j in range(len(SPLIT_MASKS[mode])):
                _split_copy(mode, src[w], land[w], ssem, rsem, w, j, place).wait_send()
                _split_copy(mode, src[w], land[w], ssem, rsem, w, j, place, arriving=True).wait_recv()

    hbm = lambda t: pltpu.HBM(t.shape, t.dtype)
    out = pl.pallas_call(
        body, name=name,
        out_shape=tuple(hbm(t) for t in (*srcs, *lands)),
        in_specs=(HBM_SPEC,) * (2 * n) + (SEM_SPEC, SEM_SPEC, pl.BlockSpec(memory_space=pl.ANY)),
        out_specs=(HBM_SPEC,) * (2 * n),
        input_output_aliases={i: i for i in range(2 * n)},
        compiler_params=pltpu.CompilerParams(has_side_effects=pltpu.SideEffectType.DATAFLOW_SIDE_EFFECTING),
    )(*srcs, *lands, send_sems, recv_sems, after)
    return out[:n], out[n:]


def _sibling_exchange(grads):
    nw = len(grads)
    HBM = pl.BlockSpec(memory_space=pl.ANY)

    def body(*refs):
        g, land = refs[:nw], refs[nw:2 * nw]
        send_sems, recv_sems = refs[2 * nw:]
        x, y, c = _my_place()

        def copy(w, q, owner_core):
            r = land[w].shape[1]
            return pltpu.make_async_remote_copy(
                src_ref=g[w].at[pl.ds(pl.multiple_of((2 * q + owner_core) * r, 16), r), :], dst_ref=land[w].at[q],
                send_sem=send_sems.at[w, q], recv_sem=recv_sems.at[w, q], device_id=(x, y, 1 - c), device_id_type=MESH)

        sends = [copy(w, q, 1 - c) for w in range(nw) for q in range(4)]
        for cp in sends:
            cp.start()
        for w in range(nw):
            for q in range(4):
                copy(w, q, c).wait_recv()
        for cp in sends:
            cp.wait_send()

    return pl.pallas_call(
        body, name="sibling_exchange",
        out_shape=[jax.ShapeDtypeStruct((4, a.shape[0] // N_DEV, a.shape[1]), a.dtype) for a in grads],
        in_specs=[HBM] * nw, out_specs=[HBM] * nw,
        scratch_shapes=[pltpu.SemaphoreType.DMA((nw, 4)), pltpu.SemaphoreType.DMA((nw, 4))],
    )(*grads)


def _pair_sums(gs, sibs, core):
    n = len(gs)

    def body(core_ref, *refs):
        for w in range(n):
            refs[2 * n + w][0] = (refs[w][0, 0].astype(F32) + refs[n + w][0].astype(F32)).astype(BF16)

    in_specs = [pl.BlockSpec((1, 1) + t.shape[1:], lambda q, core_ref: (q, core_ref[0], 0, 0)) for t in sibs]
    in_specs += [pl.BlockSpec((1,) + t.shape[1:], lambda q, core_ref: (q, 0, 0)) for t in sibs]
    return pl.pallas_call(
        body, name="pair_sums",
        grid_spec=pltpu.PrefetchScalarGridSpec(
            num_scalar_prefetch=1, grid=(4,), in_specs=in_specs,
            out_specs=[pl.BlockSpec((1,) + t.shape[1:], lambda q, core_ref: (q, 0, 0)) for t in sibs]),
        out_shape=[jax.ShapeDtypeStruct(t.shape, BF16) for t in sibs],
        compiler_params=_cparams("parallel"),
    )(core, *[g.reshape(4, 2, t.shape[1], t.shape[2]) for g, t in zip(gs, sibs)], *sibs)


def _own_rows_into_zones(shards, me):
    n = len(shards)

    def body(me_ref, *refs):
        for w in range(n):
            refs[2 * n + w][...] = refs[w][...]

    zones = [lax.empty((N_DEV * t.shape[0], t.shape[1]), t.dtype) for t in shards]
    return pl.pallas_call(
        body, name="own_rows_into_zones",
        grid_spec=pltpu.PrefetchScalarGridSpec(
            num_scalar_prefetch=1, grid=(1,),
            in_specs=[pl.BlockSpec(t.shape, lambda i, me_ref: (0, 0)) for t in shards] + [pl.BlockSpec(memory_space=pl.ANY)] * n,
            out_specs=[pl.BlockSpec(t.shape, lambda i, me_ref: (me_ref[0], 0)) for t in shards]),
        out_shape=[jax.ShapeDtypeStruct(z.shape, z.dtype) for z in zones],
        input_output_aliases={1 + n + w: w for w in range(n)},
        compiler_params=_cparams("arbitrary"),
    )(me, *shards, *zones)


def _allgather_small(v, name):
    r, ccols = v.shape

    def body(v_ref, out_ref, send_sems, recv_sems):
        x, y, c = _my_place()
        my_idx = _dev_index(x, y, c)
        out_ref[my_idx] = v_ref[...]

        def copy(m):
            peer = _peer(x, y, c, m)
            return pltpu.make_async_remote_copy(
                src_ref=v_ref, dst_ref=out_ref.at[my_idx],
                send_sem=send_sems.at[m - 1], recv_sem=recv_sems.at[m - 1], device_id=peer, device_id_type=MESH)

        def arrival(m):
            peer = _peer(x, y, c, m)
            return pltpu.make_async_remote_copy(
                src_ref=v_ref, dst_ref=out_ref.at[_dev_index(*peer)],
                send_sem=send_sems.at[m - 1], recv_sem=recv_sems.at[m - 1], device_id=peer, device_id_type=MESH)

        sends = [copy(m) for m in range(1, N_DEV)]
        for cp in sends:
            cp.start()
        for m in range(1, N_DEV):
            arrival(m).wait_recv()
        for cp in sends:
            cp.wait_send()

    return pl.pallas_call(
        body, name=name,
        out_shape=jax.ShapeDtypeStruct((N_DEV, r, ccols), v.dtype),
        in_specs=[pl.BlockSpec(memory_space=pltpu.VMEM)], out_specs=pl.BlockSpec(memory_space=pltpu.VMEM),
        scratch_shapes=[pltpu.SemaphoreType.DMA((7,)), pltpu.SemaphoreType.DMA((7,))],
    )(v)


def _gather_w_in_and_condition(shard, pay, w_ada, b_cols):
    r, ccols = shard.shape
    ncol = w_ada.shape[1]

    def body(sh_ref, pay_ref, w_ref, b_ref, full_ref, got_ref, act_ref, mod_ref, send_sems, recv_sems, small_send, small_recv, local_sem):
        x, y, c = _my_place()
        me, sibling = (x, y, c), (x, y, 1 - c)
        my_idx = _dev_index(x, y, c)
        chips = [(1 - x, y), (x, 1 - y), (1 - x, 1 - y)]

        def small(rnd, buf, m, arriving=False):
            peer = _peer(x, y, c, m)
            slot = _dev_index(*peer) if arriving else my_idx
            return pltpu.make_async_remote_copy(
                src_ref=buf.at[my_idx], dst_ref=buf.at[slot], send_sem=small_send.at[rnd, m - 1],
                recv_sem=small_recv.at[rnd, m - 1], device_id=peer, device_id_type=MESH)

        def rows(px, py, pc):
            return full_ref.at[pl.ds(pl.multiple_of(_dev_index(px, py, pc) * r, 16), r), :]

        def copy(k, block, to, src=None):
            return pltpu.make_async_remote_copy(
                src_ref=rows(*block) if src is None else src, dst_ref=rows(*block),
                send_sem=send_sems.at[k], recv_sem=recv_sems.at[k], device_id=to, device_id_type=MESH)

        got_ref[my_idx] = pay_ref[...]
        round1 = [small(0, got_ref, m) for m in range(1, N_DEV)]
        for cp in round1:
            cp.start()
        mine = pltpu.make_async_copy(sh_ref, rows(*me), local_sem)
        mine.start()
        first = [copy(0, me, sibling, src=sh_ref)] + [copy(1 + j, me, (*chip, c), src=sh_ref) for j, chip in enumerate(chips)]
        for cp in first:
            cp.start()

        for m in range(1, N_DEV):
            small(0, got_ref, m, arriving=True).wait_recv()
        cv = jnp.concatenate([got_ref[s, 0:1, :] for s in range(N_DEV)], axis=0)
        act = cv * _sigmoid(cv)
        act_ref[...] = act
        mod_ref[my_idx] = jnp.dot(act, w_ref[...], preferred_element_type=F32, precision=lax.Precision.HIGHEST) + b_ref[...]
        round2 = [small(1, mod_ref, m) for m in range(1, N_DEV)]
        for cp in round2:
            cp.start()

        passed = []
        for j, chip in enumerate(chips):
            copy(1 + j, (*chip, c), me).wait_recv()
            fwd = copy(4 + j, (*chip, c), sibling)
            fwd.start()
            passed.append(fwd)
        copy(0, sibling, me).wait_recv()
        for j, chip in enumerate(chips):
            copy(4 + j, (*chip, 1 - c), me).wait_recv()
        for m in range(1, N_DEV):
            small(1, mod_ref, m, arriving=True).wait_recv()
        for cp in first + passed + round1 + round2:
            cp.wait_send()
        mine.wait()

    anyspec = pl.BlockSpec(memory_space=pl.ANY)
    vmem = pl.BlockSpec(memory_space=pltpu.VMEM)
    return pl.pallas_call(
        body, name="gather_w_in_and_condition",
        out_shape=[jax.ShapeDtypeStruct((N_DEV * r, ccols), shard.dtype), jax.ShapeDtypeStruct((N_DEV, 8, D), F32),
                   jax.ShapeDtypeStruct((N_DEV, D), F32), jax.ShapeDtypeStruct((N_DEV, N_DEV, ncol), F32)],
        in_specs=[anyspec, vmem, vmem, vmem], out_specs=[anyspec, vmem, vmem, vmem],
        scratch_shapes=[pltpu.SemaphoreType.DMA((7,)), pltpu.SemaphoreType.DMA((7,)), pltpu.SemaphoreType.DMA((2, 7)),
                        pltpu.SemaphoreType.DMA((2, 7)), pltpu.SemaphoreType.DMA],
        compiler_params=_cparams(),
    )(shard, pay, w_ada, b_cols)


def _ada_bwd(act_t, gm_cols):
    def body(a_ref, g_ref, o_ref):
        o_ref[...] = jnp.dot(a_ref[...], g_ref[...], preferred_element_type=F32, precision=lax.Precision.HIGHEST)

    return pl.pallas_call(
        body, name="ada_bwd", out_shape=jax.ShapeDtypeStruct((D, gm_cols.shape[1]), F32), compiler_params=_cparams(),
    )(act_t, gm_cols)


def _row_tile(r):
    for t in (256, 304, 128, 64, 16):
        if r % t == 0:
            return t
    return r


def _sum_parts(parts, name, own=None):
    k, r, ccols = parts.shape
    tr = _row_tile(r)

    def body(*refs):
        p_ref, o_ref = refs[0], refs[-1]
        acc = p_ref[0].astype(F32) if own is None else refs[1][...].astype(F32) + p_ref[0].astype(F32)
        for s in range(1, k):
            acc = acc + p_ref[s].astype(F32)
        o_ref[...] = acc

    blk = pl.BlockSpec((tr, ccols), lambda i: (i, 0))
    return pl.pallas_call(
        body, name=name, grid=(r // tr,),
        in_specs=[pl.BlockSpec((k, tr, ccols), lambda i: (0, i, 0))] + ([] if own is None else [blk]),
        out_specs=blk,
        out_shape=jax.ShapeDtypeStruct((r, ccols), F32),
        compiler_params=_cparams("parallel"),
    )(*((parts,) if own is None else (parts, own)))


def _adamw(w, g, m, v, name):
    r, ccols = w.shape
    tr = _row_tile(r)
    c1 = 1.0 / (1.0 - B1 ** STEP)
    c2 = 1.0 / (1.0 - B2 ** STEP)

    def body(w_ref, g_ref, m_ref, v_ref, d_ref, nm_ref, nv_ref):
        gv = g_ref[...]
        nm = B1 * m_ref[...] + (1.0 - B1) * gv
        nv = B2 * v_ref[...] + (1.0 - B2) * jnp.square(gv)
        nm_ref[...] = nm
        nv_ref[...] = nv
        d_ref[...] = -LR * ((nm * c1) / (jnp.sqrt(nv * c2) + ADAM_EPS) + WD * w_ref[...])

    blk = pl.BlockSpec((tr, ccols), lambda i: (i, 0))
    return pl.pallas_call(
        body, name=name, grid=(r // tr,), in_specs=[blk] * 4, out_specs=[blk] * 3,
        out_shape=[jax.ShapeDtypeStruct((r, ccols), F32)] * 3,
        compiler_params=_cparams("parallel"),
    )(w, g, m, v)


def _sum_adamw(parts, own, w, m, v, name):
    k, r, ccols = parts.shape
    tr = _row_tile(r)
    c1 = 1.0 / (1.0 - B1 ** STEP)
    c2 = 1.0 / (1.0 - B2 ** STEP)

    def body(p_ref, own_ref, w_ref, m_ref, v_ref, g_ref, d_ref, nm_ref, nv_ref):
        gv = own_ref[...].astype(F32)
        for s in range(k):
            gv = gv + p_ref[s].astype(F32)
        g_ref[...] = gv
        nm = B1 * m_ref[...] + (1.0 - B1) * gv
        nv = B2 * v_ref[...] + (1.0 - B2) * jnp.square(gv)
        nm_ref[...] = nm
        nv_ref[...] = nv
        d_ref[...] = -LR * ((nm * c1) / (jnp.sqrt(nv * c2) + ADAM_EPS) + WD * w_ref[...])

    blk = pl.BlockSpec((tr, ccols), lambda i: (i, 0))
    return pl.pallas_call(
        body, name=name, grid=(r // tr,),
        in_specs=[pl.BlockSpec((k, tr, ccols), lambda i: (0, i, 0))] + [blk] * 4, out_specs=[blk] * 4,
        out_shape=[jax.ShapeDtypeStruct((r, ccols), F32)] * 4,
        compiler_params=_cparams("parallel"),
    )(parts, own, w, m, v)


VEC_ROWS = ((0, 6), (6, 7), (9, 11), (11, 14), (7, 8), (8, 9))


def _adamw_vectors(w, g, m, v):
    c1 = 1.0 / (1.0 - B1 ** STEP)
    c2 = 1.0 / (1.0 - B2 ** STEP)

    def put(refs, p):
        for ref, (lo, hi) in zip(refs, VEC_ROWS):
            if ref.shape == (3, HEAD):
                ref[...] = p[lo:hi, :HEAD]
            else:
                ref[...] = jnp.concatenate([p[k:k + 1] for k in range(lo, hi)], axis=1)

    def body(w_ref, g_ref, m_ref, v_ref, *outs):
        gv = g_ref[...]
        nm = B1 * m_ref[...] + (1.0 - B1) * gv
        nv = B2 * v_ref[...] + (1.0 - B2) * jnp.square(gv)
        delta = -LR * ((nm * c1) / (jnp.sqrt(nv * c2) + ADAM_EPS) + WD * w_ref[...])
        for kind, p in enumerate((gv, delta, nm, nv)):
            put(outs[6 * kind:6 * kind + 6], p)

    shapes = [(1, 6 * D), (1, D), (1, 2 * D), (3, HEAD), (1, D), (1, D)]
    out = pl.pallas_call(
        body, name="adamw_vectors", out_shape=[jax.ShapeDtypeStruct(sh, F32) for sh in shapes] * 4, compiler_params=_cparams(),
    )(w, g, m, v)
    fix = lambda t: (t[0], t[1], t[2], t[3][None], t[4], t[5].reshape(D))
    return [fix(out[6 * kind:6 * kind + 6]) for kind in range(4)]


def _pack_vectors(b_ada, g_mix, g_mlp, g_fin, b_gate, conv_w):
    conv_rows = jnp.pad(conv_w.reshape(3, HEAD), ((0, 0), (0, D - HEAD)))
    return jnp.concatenate([b_ada.reshape(6, D), g_mix.reshape(1, D), g_mlp.reshape(1, D), g_fin.reshape(1, D),
                            b_gate.reshape(2, D), conv_rows, jnp.zeros((2, D), F32)], axis=0)


def kernel(x, c, w_ada, b_ada, g_norm_mix, w_in, b_gate, conv_w, w_branch_attn, w_branch_conv, w_out, g_norm_mlp, w_mlp_in, w_mlp_out, g_norm_final, loss_target, m_w_ada, m_b_ada, m_g_norm_mix, m_w_in, m_b_gate, m_conv_w, m_w_branch_attn, m_w_branch_conv, m_w_out, m_g_norm_mlp, m_w_mlp_in, m_w_mlp_out, m_g_norm_final, v_w_ada, v_b_ada, v_g_norm_mix, v_w_in, v_b_gate, v_conv_w, v_w_branch_attn, v_w_branch_conv, v_w_out, v_g_norm_mlp, v_w_mlp_in, v_w_mlp_out, v_g_norm_final):
    S = x.shape[1]
    xi, yi, ci = _my_place()
    me = _dev_index(xi, yi, ci)
    x2 = x.reshape(S, D)
    tgt = loss_target.reshape(S, D)

    pay = jnp.zeros((8, D), F32).at[0].set(c[0]).at[1:4, :HEAD].set(conv_w[0])
    ncol = w_ada.shape[2]
    b_cols = lax.dynamic_slice(b_ada, (0, me * ncol), (1, ncol))
    w_int, got, act, mod_all = _gather_w_in_and_condition(w_in[0].T.astype(BF16), pay, w_ada[0], b_cols)
    cw8 = jnp.pad(got[:, 1:4, :HEAD].transpose(1, 0, 2).reshape(3, D), ((0, 5), (0, 0)))
    mod = lax.dynamic_index_in_dim(mod_all, me, axis=1, keepdims=False).reshape(6, D)
    late = [w_branch_attn[0].T.astype(BF16), w_branch_conv[0].astype(BF16), w_out[0].astype(BF16),
            w_mlp_in[0].T.astype(BF16), w_mlp_out[0].astype(BF16)]
    w_int, late = lax.optimization_barrier((w_int, late))
    zones = _own_rows_into_zones(late, me.reshape(1).astype(jnp.int32))
    ag_mix = _split_start("gather_mix_start", "gather", late[:3], zones[:3])
    ag_mlp = _split_start("gather_mlp_start", "gather", late[3:], zones[3:])
    mod = mod + ag_mix[4] + ag_mlp[4]
    h = _prenorm(x2, g_norm_mix, mod[1:2], mod[0:1])

    def mix_weights(o_attn):
        return _split_wait("gather_mix_wait", "gather", *ag_mix[:4], o_attn)[1]

    def mlp_weights(x1):
        return _split_wait("gather_mlp_wait", "gather", *ag_mlp[:4], x1)[1]

    rs = {}

    def mlp_grads_ready(*grads):
        lands = [lax.empty((N_PEER, t.shape[0] // N_DEV, t.shape[1]), BF16) for t in grads]
        rs["mlp"] = _split_start("scatter_mlp_start", "scatter", grads, lands)
        return rs["mlp"][4]

    def w_in_grad_ready(g_in):
        r = g_in.shape[0] // N_DEV
        rs["sib"] = _split_start("sibling_w_in_start", "sibling", [g_in], [jnp.zeros((4, r, g_in.shape[1]), BF16)])
        return rs["sib"][5]

    def other_grads_ready(*small):
        core = ci.reshape(1).astype(jnp.int32)
        (g_in,), (sib_in,) = _split_wait("sibling_w_in_wait", "sibling", *rs["sib"][:4], small[0])
        pair = _pair_sums([g_in, *small], [sib_in, *_sibling_exchange(small)], core)
        lands = [lax.empty((3,) + t.shape[1:], BF16) for t in pair]
        rs["rest"] = _split_start("scatter_rest_start", "chips", pair, lands)
        return rs["rest"][4]

    ba, bb = b_gate[:, :D], b_gate[:, D:]
    grad_x, vec = _local_step(
        x2, h, tgt, mod, g_norm_mix, g_norm_mlp, g_norm_final.reshape(1, D), ba, bb, cw8, w_int, mix_weights, mlp_weights,
        mlp_grads_ready, w_in_grad_ready, other_grads_ready)

    vec_all = _allgather_small(vec, "gather_vec")
    vec_sum = _sum_parts(vec_all, "sum_vec")
    loss = vec_sum[14, 0]
    gm_all = vec_all[:, 0:6, :].reshape(N_DEV, 6 * D)
    gm_cols = lax.dynamic_slice(gm_all, (0, me * ncol), (N_DEV, ncol))
    g_w_ada = _ada_bwd(act.T, gm_cols)
    conv_cols = lax.dynamic_slice(vec_sum[11:14], (0, me * HEAD), (3, HEAD))
    g_pack = jnp.concatenate([vec_sum[0:11], jnp.pad(conv_cols, ((0, 0), (0, D - HEAD))), jnp.zeros((2, D), F32)], axis=0)
    packs = [_pack_vectors(*t) for t in ((b_ada, g_norm_mix, g_norm_mlp, g_norm_final, b_gate, conv_w),
                                         (m_b_ada, m_g_norm_mix, m_g_norm_mlp, m_g_norm_final, m_b_gate, m_conv_w),
                                         (v_b_ada, v_g_norm_mix, v_g_norm_mlp, v_g_norm_final, v_b_gate, v_conv_w))]
    gv, dv, mv, vv = _adamw_vectors(packs[0], g_pack, packs[1], packs[2])
    d_ada, nm_ada, nv_ada = _adamw(w_ada[0], g_w_ada, m_w_ada[0], v_w_ada[0], "adamw_w_ada")

    big = {}
    srcs, lands = _split_wait("scatter_mlp_wait", "scatter", *rs["mlp"][:4], d_ada)
    own = [lax.dynamic_slice(g, (me * land.shape[1], 0), land.shape[1:]) for g, land in zip(srcs, lands)]
    g_mi = _sum_parts(lands[0], "sum_w_mi", own=own[0]).T
    big["w_mi"] = (g_mi[None],) + tuple(t[None] for t in _adamw(w_mlp_in[0], g_mi, m_w_mlp_in[0], v_w_mlp_in[0], "adamw_w_mi"))
    big["w_mo"] = tuple(t[None] for t in _sum_adamw(lands[1], own[1], w_mlp_out[0], m_w_mlp_out[0], v_w_mlp_out[0], "adamw_w_mo"))
    srcs, lands = _split_wait("scatter_rest_wait", "chips", *rs["rest"][:4], big["w_mo"][1])
    own = [lax.dynamic_index_in_dim(pair, 2 * xi + yi, axis=0, keepdims=False) for pair in srcs]
    big["w_in"] = tuple(t.T[None] for t in _sum_adamw(lands[0], own[0], w_in[0].T, m_w_in[0].T, v_w_in[0].T, "adamw_w_in"))
    g_ba = _sum_parts(lands[1], "sum_w_ba", own=own[1]).T
    big["w_ba"] = (g_ba[None],) + tuple(t[None] for t in _adamw(w_branch_attn[0], g_ba, m_w_branch_attn[0], v_w_branch_attn[0], "adamw_w_ba"))
    big["w_bc"] = tuple(t[None] for t in _sum_adamw(lands[2], own[2], w_branch_conv[0], m_w_branch_conv[0], v_w_branch_conv[0], "adamw_w_bc"))
    big["w_out"] = tuple(t[None] for t in _sum_adamw(lands[3], own[3], w_out[0], m_w_out[0], v_w_out[0], "adamw_w_out"))

    def ordered(k, ada, vecs):
        return (ada[None], vecs[0], vecs[1], big["w_in"][k], vecs[2], vecs[3], big["w_ba"][k], big["w_bc"][k],
                big["w_out"][k], vecs[4], big["w_mi"][k], big["w_mo"][k], vecs[5])

    return (loss, grad_x.reshape(1, S, D), *ordered(0, g_w_ada, gv), *ordered(1, d_ada, dv),
            *ordered(2, nm_ada, mv), *ordered(3, nv_ada, vv))
```

```python
import numpy as np
import jax
import jax.numpy as jnp
from jax import lax
from jax.experimental import pallas as pl
from jax.experimental.pallas import tpu as pltpu

F32, BF16 = jnp.float32, jnp.bfloat16
D = 1024
HEAD = 128
DILATIONS = (1, 4, 16)
N_SLOT = 4
AOW = N_SLOT * HEAD
DFF = 4 * D
N_DEV = 8
UNROLL = 16
EPS = 1e-6
NEG = -1e30
SCALE = HEAD ** -0.5
LR, B1, B2, ADAM_EPS, WD, STEP = 0.001, 0.9, 0.999, 1e-08, 0.01, 10
V7X_VMEM_LIMIT = 56 * 1024 * 1024
TM = 1024
MESH = pl.DeviceIdType.MESH


def _cparams(*sem):
    if sem:
        return pltpu.CompilerParams(dimension_semantics=sem, vmem_limit_bytes=V7X_VMEM_LIMIT)
    return pltpu.CompilerParams(vmem_limit_bytes=V7X_VMEM_LIMIT)


def _nn(a, b):
    return jnp.dot(a, b, preferred_element_type=F32)


def _nt(a, b):
    return lax.dot_general(a, b, (((1,), (1,)), ((), ())), preferred_element_type=F32)


def _tn(a, b):
    return lax.dot_general(a, b, (((0,), (0,)), ((), ())), preferred_element_type=F32)


def _rms_r(x):
    return lax.rsqrt(jnp.mean(x * x, axis=-1, keepdims=True) + EPS)


def _rms_bwd(x, r, g, dn):
    gy = dn * g
    dx = r * gy - x * (r * r * r) * jnp.mean(x * gy, axis=-1, keepdims=True)
    return dx, dn * (x * r)


def _sigmoid(t):
    return 1.0 / (1.0 + jnp.exp(-t))


def _rowsum(v):
    return jnp.sum(v, axis=0, keepdims=True)


def _vec_spec(n=D):
    return pl.BlockSpec((1, n), lambda *_: (0, 0))


def _const_spec(shape):
    nd = len(shape)
    return pl.BlockSpec(shape, lambda *_: (0,) * nd)


def _win_rowblock(j):
    return jnp.where(j < 9, (j % 3) * 3 + j // 3, j)


def _prenorm(x, g, sc, sh):
    S = x.shape[0]
    tm = TM

    def body(x_ref, g_ref, sc_ref, sh_ref, h_ref):
        xv = x_ref[...]
        h_ref[...] = (xv * _rms_r(xv) * g_ref[...] * (1.0 + sc_ref[...]) + sh_ref[...]).astype(BF16)

    row = pl.BlockSpec((tm, D), lambda i: (i, 0))
    return pl.pallas_call(
        body, name="prenorm", grid=(S // tm,), in_specs=[row, _vec_spec(), _vec_spec(), _vec_spec()], out_specs=row,
        out_shape=jax.ShapeDtypeStruct((S, D), BF16), compiler_params=_cparams("parallel"),
    )(x, g, sc, sh)


def _proj(h, w_int):
    S = h.shape[0]

    def body(h_ref, w_ref, q_ref, e_ref):
        j = pl.program_id(0)
        acc = _nt(h_ref[...], w_ref[...])

        @pl.when(j < 9)
        def _():
            q_ref[0] = acc

        @pl.when(j >= 9)
        def _():
            e_ref[0] = acc.astype(BF16)

    def e_idx(j):
        k = jnp.maximum(j - 9, 0)
        return (k // 2, 0, k % 2)

    return pl.pallas_call(
        body, name="proj", grid=(19,),
        in_specs=[pl.BlockSpec((S, D), lambda j: (0, 0), pipeline_mode=pl.Buffered(1)),
                  pl.BlockSpec((512, D), lambda j: (_win_rowblock(j), 0))],
        out_specs=[pl.BlockSpec((1, S, 512), lambda j: (jnp.minimum(j, 8), 0, 0)), pl.BlockSpec((1, S, 512), e_idx)],
        out_shape=[jax.ShapeDtypeStruct((9, S, 512), F32), jax.ShapeDtypeStruct((5, S, D), BF16)],
        compiler_params=_cparams("arbitrary"),
    )(h, w_int)


def _bias_table():
    slopes = (2.0 ** (-8.0 * np.arange(1, 13, dtype=np.float32) / 12.0)).astype(np.float32)
    qi = np.arange(HEAD)[:, None]
    kj = np.arange(2 * HEAD)[None, :]
    delta = HEAD + qi - kj
    mask = (delta >= 0) & (delta <= HEAD)
    out = np.zeros((3, N_SLOT, HEAD, 2 * HEAD), np.float32)
    for gi, d in enumerate(DILATIONS):
        for j in range(N_SLOT):
            bias = -slopes[gi * N_SLOT + j] * (delta * d).astype(np.float32)
            out[gi, j] = np.where(mask, bias, NEG)
    out_t = np.concatenate([out[..., HEAD:].swapaxes(-1, -2), out[..., :HEAD].swapaxes(-1, -2)], axis=-1)
    return jnp.asarray(out), jnp.asarray(out_t)


def _attn_fwd(qkv, bias):
    S = qkv.shape[2]
    nblk = S // HEAD
    rows = 256

    def body(qkv_ref, b_ref, o_ref, lse_ref, o_s, lse_s):
        g = pl.program_id(1)
        bias = b_ref[0, 0]
        col = lax.broadcasted_iota(jnp.int32, bias.shape, 1)
        bias_first = jnp.where(col < HEAD, NEG, bias)

        for gi, d in enumerate(DILATIONS):
            @pl.when(g == gi)
            def _(gi=gi, d=d):
                nb = nblk // d

                def keys(start):
                    sl = pl.ds(start, HEAD, stride=d)
                    return qkv_ref.at[0, 1][sl, :].astype(BF16), qkv_ref.at[0, 2][sl, :].astype(BF16)

                def step(b, first_of_residue, before):
                    r, n = b // nb, b % nb
                    cur = pl.ds(n * (HEAD * d) + r, HEAD, stride=d)
                    own = keys(n * (HEAD * d) + r)
                    if first_of_residue:
                        before = own
                    q = qkv_ref.at[0, 0][cur, :].astype(BF16)
                    kw = jnp.concatenate([before[0], own[0]], axis=0)
                    vw = jnp.concatenate([before[1], own[1]], axis=0)
                    s = _nt(q, kw) * SCALE + jnp.where(n > 0, bias, bias_first)
                    m = jnp.max(s, axis=-1, keepdims=True)
                    p = jnp.exp(s - m)
                    l = jnp.sum(p, axis=-1, keepdims=True)
                    o_s.at[gi][cur, :] = _nn(p.astype(BF16), vw) / l
                    lse_s.at[gi][cur, :] = jnp.broadcast_to(m + jnp.log(l), (HEAD, HEAD))
                    return own

                def steps(i, before):
                    for u in range(UNROLL):
                        before = step(i * UNROLL + u, nb <= UNROLL and u % nb == 0, before)
                    return before

                lax.fori_loop(0, nblk // UNROLL, steps, keys(0))

        @pl.when(g == len(DILATIONS) - 1)
        def _():
            def merge(i, carry):
                r = pl.ds(pl.multiple_of(i * rows, rows), rows)
                ls = [lse_s[k, r, :] for k in range(3)]
                top = jnp.maximum(jnp.maximum(ls[0], ls[1]), ls[2])
                ws = [jnp.exp(t - top) for t in ls]
                den = ws[0] + ws[1] + ws[2]
                o_ref[r, :] = (ws[0] * o_s[0, r, :] + ws[1] * o_s[1, r, :] + ws[2] * o_s[2, r, :]) / den
                lse_ref[r, :] = top + jnp.log(den)
                return carry

            lax.fori_loop(0, S // rows, merge, 0)

    return pl.pallas_call(
        body, name="attn_fwd", grid=(N_SLOT, 3),
        in_specs=[pl.BlockSpec((1, 3, S, HEAD), lambda j, g: (g, 0, 0, j)),
                  pl.BlockSpec((1, 1, HEAD, 2 * HEAD), lambda j, g: (g, j, 0, 0))],
        out_specs=[pl.BlockSpec((S, HEAD), lambda j, g: (0, j)), pl.BlockSpec((S, HEAD), lambda j, g: (0, j))],
        out_shape=[jax.ShapeDtypeStruct((S, AOW), F32), jax.ShapeDtypeStruct((S, AOW), F32)],
        scratch_shapes=[pltpu.VMEM((3, S, HEAD), F32)] * 2,
        compiler_params=_cparams("parallel", "arbitrary"),
    )(qkv, bias)


def _shift_down(z, k, halo_rows):
    out = pltpu.roll(z, k, axis=0)
    top = out[:8]
    rid = lax.broadcasted_iota(jnp.int32, top.shape, 0)
    for t in range(k):
        top = jnp.where(rid == t, halo_rows[t], top)
    return jnp.concatenate([top, out[8:]], axis=0)


def _shift_up(z, k, halo_rows):
    n = z.shape[0]
    out = pltpu.roll(z, n - k, axis=0)
    bottom = out[n - 8:]
    rid = lax.broadcasted_iota(jnp.int32, bottom.shape, 0)
    for t in range(k):
        bottom = jnp.where(rid == 8 - k + t, halo_rows[t], bottom)
    return jnp.concatenate([out[:n - 8], bottom], axis=0)


def _e_spec(chunk, tm):
    return pl.BlockSpec((1, tm, D), lambda i, c=chunk: (c, i, 0))


def _e_prev_spec(chunk, tm):
    return pl.BlockSpec((1, 16, D), lambda i, c=chunk: (c, jnp.maximum(i * (tm // 16) - 1, 0), 0))


def _e_next_spec(chunk, tm, S):
    return pl.BlockSpec((1, 16, D), lambda i, c=chunk: (c, jnp.minimum((i + 1) * (tm // 16), S // 16 - 1), 0))


def _mix(o_attn, e, cw8, ba, bb, w_bat, w_bc):
    S = o_attn.shape[0]
    tm = 512

    def body(o_ref, cb_ref, cc_ref, cx_ref, ga_ref, gb_ref, ccp_ref, cxp_ref, cw_ref, ba_ref, bb_ref, wba_ref, wbc_ref,
             obf_ref, cbu_ref, ya_ref, yc_ref, mg_ref):
        i = pl.program_id(0)
        o = o_ref[...].astype(BF16)
        obf_ref[...] = o
        ya = _nt(o, wba_ref[...])
        z = cc_ref[0].astype(F32) * cx_ref[0].astype(F32)
        zp = ccp_ref[0].astype(F32) * cxp_ref[0].astype(F32) * (i > 0).astype(F32)
        z1 = _shift_down(z, 1, [zp[15:16]])
        z2 = _shift_down(z, 2, [zp[14:15], zp[15:16]])
        cw = cw_ref[...]
        u = cw[0:1] * z2 + cw[1:2] * z1 + cw[2:3] * z
        cbu = (cb_ref[0].astype(F32) * u).astype(BF16)
        cbu_ref[...] = cbu
        yc = _nn(cbu, wbc_ref[...])
        sa = _sigmoid(ga_ref[0].astype(F32) + ba_ref[...])
        sb = _sigmoid(gb_ref[0].astype(F32) + bb_ref[...])
        ya_ref[...] = ya.astype(BF16)
        yc_ref[...] = yc.astype(BF16)
        mg_ref[...] = (sa * ya + sb * yc).astype(BF16)

    row = lambda w: pl.BlockSpec((tm, w), lambda i: (i, 0))
    return pl.pallas_call(
        body, name="mix", grid=(S // tm,),
        in_specs=[row(AOW)] + [_e_spec(c, tm) for c in range(5)] + [_e_prev_spec(1, tm), _e_prev_spec(2, tm),
                  _const_spec((8, D)), _vec_spec(), _vec_spec(), _const_spec((D, AOW)), _const_spec((D, D))],
        out_specs=[row(AOW), row(D), row(D), row(D), row(D)],
        out_shape=[jax.ShapeDtypeStruct((S, AOW), BF16)] + [jax.ShapeDtypeStruct((S, D), BF16)] * 4,
        compiler_params=_cparams("parallel"),
    )(o_attn, e, e, e, e, e, e, e, cw8, ba, bb, w_bat, w_bc)


def _out_proj(merged, w_out, x, gate1, g_mlp, sc2, sh2):
    S = x.shape[0]
    tm = TM

    def body(mg_ref, w_ref, x_ref, gt_ref, g_ref, sc_ref, sh_ref, x1_ref, mo_ref, h2_ref):
        mo = _nn(mg_ref[...], w_ref[...])
        mo_ref[...] = mo.astype(BF16)
        x1 = x_ref[...] + gt_ref[...] * mo
        x1_ref[...] = x1
        h2 = x1 * _rms_r(x1) * g_ref[...] * (1.0 + sc_ref[...]) + sh_ref[...]
        h2_ref[...] = h2.astype(BF16)

    row = pl.BlockSpec((tm, D), lambda i: (i, 0))
    return pl.pallas_call(
        body, name="out_proj", grid=(S // tm,),
        in_specs=[row, _const_spec((D, D)), row, _vec_spec(), _vec_spec(), _vec_spec(), _vec_spec()],
        out_specs=[row, row, row],
        out_shape=[jax.ShapeDtypeStruct((S, D), F32), jax.ShapeDtypeStruct((S, D), BF16), jax.ShapeDtypeStruct((S, D), BF16)],
        compiler_params=_cparams("parallel"),
    )(merged, w_out, x, gate1, g_mlp, sc2, sh2)


def _mlp_in(h2, w_mit):
    S = h2.shape[0]
    tm, tn = TM, 2048

    def body(h_ref, w_ref, a_ref, f_ref):
        a = _nt(h_ref[...], w_ref[...])
        a_ref[...] = a.astype(BF16)
        f_ref[...] = jnp.square(jnp.maximum(a, 0.0)).astype(BF16)

    blk = pl.BlockSpec((tm, tn), lambda i, j: (i, j))
    return pl.pallas_call(
        body, name="mlp_in", grid=(S // tm, DFF // tn),
        in_specs=[pl.BlockSpec((tm, D), lambda i, j: (i, 0)), pl.BlockSpec((tn, D), lambda i, j: (j, 0))],
        out_specs=[blk, blk],
        out_shape=[jax.ShapeDtypeStruct((S, DFF), BF16)] * 2,
        compiler_params=_cparams("parallel", "parallel"),
    )(h2, w_mit)


def _mlp_out(f, w_mo, x1, gate2, g_fin, tgt):
    S = x1.shape[0]
    tm = 512
    half = tm // 2

    def body(f_ref, w_ref, x1_ref, gt_ref, g_ref, t_ref, mlp_ref, dx2_ref, pv_ref):
        @pl.when(pl.program_id(0) == 0)
        def _():
            pv_ref[...] = jnp.zeros_like(pv_ref)

        g = g_ref[...]
        for hs in (pl.ds(0, half), pl.ds(half, half)):
            mlp = _nn(f_ref[hs, :], w_ref[...])
            mlp_ref[hs, :] = mlp.astype(BF16)
            x2 = x1_ref[hs, :] + gt_ref[...] * mlp
            r = _rms_r(x2)
            err = x2 * r * g - t_ref[hs, :]
            dx2, pg = _rms_bwd(x2, r, g, err * (1.0 / D))
            dx2_ref[hs, :] = dx2
            pv_ref[0:1, :] += _rowsum(pg)
            pv_ref[1:2, :] += 0.5 * _rowsum(jnp.mean(err * err, axis=-1, keepdims=True))

    row = pl.BlockSpec((tm, D), lambda i: (i, 0))
    return pl.pallas_call(
        body, name="mlp_out", grid=(S // tm,),
        in_specs=[pl.BlockSpec((tm, DFF), lambda i: (i, 0)), _const_spec((DFF, D)), row, _vec_spec(), _vec_spec(), row],
        out_specs=[row, row, _const_spec((8, D))],
        out_shape=[jax.ShapeDtypeStruct((S, D), BF16), jax.ShapeDtypeStruct((S, D), F32), jax.ShapeDtypeStruct((8, D), F32)],
        compiler_params=_cparams("arbitrary"),
    )(f, w_mo, x1, gate2, g_fin, tgt)


def _bwd_mlp_a(dx2, gate2, mlp, w_mo, a):
    S = dx2.shape[0]
    tm = 512
    half = tm // 2

    def body(dx_ref, gt_ref, mlp_ref, w_ref, a_ref, da_ref, dmo_ref, pv_ref):
        @pl.when(pl.program_id(0) == 0)
        def _():
            pv_ref[...] = jnp.zeros_like(pv_ref)

        for hs in (pl.ds(0, half), pl.ds(half, half)):
            dx = dx_ref[hs, :]
            dmo = (dx * gt_ref[...]).astype(BF16)
            dmo_ref[hs, :] = dmo
            pv_ref[0:1, :] += _rowsum(dx * mlp_ref[hs, :].astype(F32))
            df = _nt(dmo, w_ref[...])
            da_ref[hs, :] = (df * (2.0 * jnp.maximum(a_ref[hs, :].astype(F32), 0.0))).astype(BF16)

    row = pl.BlockSpec((tm, D), lambda i: (i, 0))
    wide = pl.BlockSpec((tm, DFF), lambda i: (i, 0))
    return pl.pallas_call(
        body, name="bwd_mlp_a", grid=(S // tm,),
        in_specs=[row, _vec_spec(), row, _const_spec((DFF, D)), wide],
        out_specs=[wide, row, _const_spec((8, D))],
        out_shape=[jax.ShapeDtypeStruct((S, DFF), BF16), jax.ShapeDtypeStruct((S, D), BF16), jax.ShapeDtypeStruct((8, D), F32)],
        compiler_params=_cparams("arbitrary"),
    )(dx2, gate2, mlp, w_mo, a)


def _bwd_mlp_b(da, w_mit, x1, dx2, g_mlp, sc2):
    S = x1.shape[0]
    tm = 512
    half = tm // 2

    def body(da_ref, w_ref, x1_ref, dx2_ref, g_ref, sc_ref, dx1_ref, pv_ref):
        @pl.when(pl.program_id(0) == 0)
        def _():
            pv_ref[...] = jnp.zeros_like(pv_ref)

        g = g_ref[...]
        for hs in (pl.ds(0, half), pl.ds(half, half)):
            dh = _nn(da_ref[hs, :], w_ref[...])
            x1 = x1_ref[hs, :]
            r = _rms_r(x1)
            dxn, pg = _rms_bwd(x1, r, g, dh * (1.0 + sc_ref[...]))
            dx1_ref[hs, :] = dx2_ref[hs, :] + dxn
            pv_ref[0:1, :] += _rowsum(dh)
            pv_ref[1:2, :] += _rowsum(dh * (x1 * r * g))
            pv_ref[2:3, :] += _rowsum(pg)

    row = pl.BlockSpec((tm, D), lambda i: (i, 0))
    return pl.pallas_call(
        body, name="bwd_mlp_b", grid=(S // tm,),
        in_specs=[pl.BlockSpec((tm, DFF), lambda i: (i, 0)), _const_spec((DFF, D)), row, row, _vec_spec(), _vec_spec()],
        out_specs=[row, _const_spec((8, D))],
        out_shape=[jax.ShapeDtypeStruct((S, D), F32), jax.ShapeDtypeStruct((8, D), F32)],
        compiler_params=_cparams("arbitrary"),
    )(da, w_mit, x1, dx2, g_mlp, sc2)


def _bwd_mix(dx1, gate1, mo, e, cw8, ba, bb, ya, yc, o_attn, w_out, w_bc, w_bat):
    S = dx1.shape[0]
    tm = 256
    n_tiles = S // tm

    def body(dx_ref, dxn_ref, gt_ref, mo_ref, cb_ref, cc_ref, cx_ref, ga_ref, gb_ref, cbn_ref, gbn_ref, ccp_ref, cxp_ref,
             cw_ref, ba_ref, bb_ref, ya_ref, yc_ref, o_ref, wout_ref, wbc_ref, wba_ref,
             dmo_ref, dya_ref, dyc_ref, do_ref, dl_ref, de_ref, pv_ref):
        i = pl.program_id(0)

        @pl.when(i == 0)
        def _():
            pv_ref[...] = jnp.zeros_like(pv_ref)

        dx = dx_ref[...]
        cb = cb_ref[0].astype(F32)
        cc = cc_ref[0].astype(F32)
        cx = cx_ref[0].astype(F32)
        dmo_all = (jnp.concatenate([dx, dxn_ref[...]], axis=0) * gt_ref[...]).astype(BF16)
        dmg_all = _nt(dmo_all, wout_ref[...])
        sb_all = _sigmoid(jnp.concatenate([gb_ref[0], gbn_ref[0]], axis=0).astype(F32) + bb_ref[...])
        dyc_all = dmg_all * sb_all
        dcbu_all = _nt(dyc_all.astype(BF16), wbc_ref[...])
        dmo, dmg, sb, dyc, dcbu = dmo_all[:tm], dmg_all[:tm], sb_all[:tm], dyc_all[:tm], dcbu_all[:tm]
        dmo_ref[...] = dmo
        pv_ref[0:1, :] += _rowsum(dx * mo_ref[...].astype(F32))
        sa = _sigmoid(ga_ref[0].astype(F32) + ba_ref[...])
        dya = (dmg * sa).astype(BF16)
        dya_ref[...] = dya
        dyc_ref[...] = dyc.astype(BF16)
        dga = dmg * ya_ref[...].astype(F32) * sa * (1.0 - sa)
        dgb = dmg * yc_ref[...].astype(F32) * sb * (1.0 - sb)
        pv_ref[1:2, :] += _rowsum(dga)
        pv_ref[2:3, :] += _rowsum(dgb)

        do = _nn(dya, wba_ref[...])
        do_ref[...] = do
        prod = do * o_ref[...]
        dl_ref[...] = jnp.concatenate(
            [jnp.broadcast_to(jnp.sum(prod[:, s * HEAD:(s + 1) * HEAD], axis=-1, keepdims=True), (tm, HEAD))
             for s in range(N_SLOT)], axis=1)

        z = cc * cx
        zp = ccp_ref[0].astype(F32) * cxp_ref[0].astype(F32) * (i > 0).astype(F32)
        z1 = _shift_down(z, 1, [zp[15:16]])
        z2 = _shift_down(z, 2, [zp[14:15], zp[15:16]])
        cw = cw_ref[...]
        u = cw[0:1] * z2 + cw[1:2] * z1 + cw[2:3] * z
        du = dcbu * cb
        du_n = dcbu_all[tm:] * cbn_ref[0].astype(F32) * (i < n_tiles - 1).astype(F32)
        du1 = _shift_up(du, 1, [du_n[0:1]])
        du2 = _shift_up(du, 2, [du_n[0:1], du_n[1:2]])
        dz = cw[2:3] * du + cw[1:2] * du1 + cw[0:1] * du2
        pv_ref[3:4, :] += _rowsum(du * z2)
        pv_ref[4:5, :] += _rowsum(du * z1)
        pv_ref[5:6, :] += _rowsum(du * z)

        de_ref[0] = (dcbu * u).astype(BF16)
        de_ref[1] = (dz * cx).astype(BF16)
        de_ref[2] = (dz * cc).astype(BF16)
        de_ref[3] = dga.astype(BF16)
        de_ref[4] = dgb.astype(BF16)

    row = lambda w: pl.BlockSpec((tm, w), lambda i: (i, 0))
    nxt = pl.BlockSpec((16, D), lambda i: (jnp.minimum((i + 1) * (tm // 16), S // 16 - 1), 0))
    return pl.pallas_call(
        body, name="bwd_mix", grid=(n_tiles,),
        in_specs=[row(D), nxt, _vec_spec(), row(D)] + [_e_spec(c, tm) for c in range(5)]
                 + [_e_next_spec(0, tm, S), _e_next_spec(4, tm, S), _e_prev_spec(1, tm), _e_prev_spec(2, tm),
                    _const_spec((8, D)), _vec_spec(), _vec_spec(), row(D), row(D), row(AOW),
                    _const_spec((D, D)), _const_spec((D, D)), _const_spec((D, AOW))],
        out_specs=[row(D), row(D), row(D), row(AOW), row(AOW), pl.BlockSpec((5, tm, D), lambda i: (0, i, 0)),
                   _const_spec((8, D))],
        out_shape=[jax.ShapeDtypeStruct((S, D), BF16)] * 3 + [jax.ShapeDtypeStruct((S, AOW), F32)] * 2
                  + [jax.ShapeDtypeStruct((5, S, D), BF16), jax.ShapeDtypeStruct((8, D), F32)],
        compiler_params=_cparams("arbitrary"),
    )(dx1, dx1, gate1, mo, e, e, e, e, e, e, e, e, e, cw8, ba, bb, ya, yc, o_attn, w_out, w_bc, w_bat)


def _attn_bwd(qkv, do, lse, dl, bias_t):
    S = qkv.shape[2]
    nblk = S // HEAD

    def body(qkv_ref, do_ref, lse_ref, dl_ref, b_ref, d_ref):
        g = pl.program_id(1)
        bias = b_ref[0, 0]
        col = lax.broadcasted_iota(jnp.int32, bias.shape, 1)
        bias_last = jnp.where(col >= HEAD, NEG, bias)
        eye = (lax.broadcasted_iota(jnp.int32, (HEAD, HEAD), 0) == lax.broadcasted_iota(jnp.int32, (HEAD, HEAD), 1)).astype(F32)

        def as_row(t):
            return jnp.sum(t * eye, axis=0, keepdims=True)

        for gi, d in enumerate(DILATIONS):
            @pl.when(g == gi)
            def _(d=d):
                nb = nblk // d

                def query_side(start):
                    sl = pl.ds(start, HEAD, stride=d)
                    return (qkv_ref.at[0, 0][sl, :].astype(BF16), do_ref[sl, :].astype(BF16),
                            as_row(lse_ref[sl, :]), as_row(dl_ref[sl, :]))

                def step(b, first_of_residue, carry):
                    dq_part, own = carry
                    r, n = b // nb, b % nb
                    cur = pl.ds(n * (HEAD * d) + r, HEAD, stride=d)
                    if first_of_residue:
                        own = query_side(r)
                    nxt = query_side(jnp.minimum(n + 1, nb - 1) * (HEAD * d) + r)
                    q2 = jnp.concatenate([own[0], nxt[0]], axis=0)
                    do2 = jnp.concatenate([own[1], nxt[1]], axis=0)
                    k = qkv_ref.at[0, 1][cur, :].astype(BF16)
                    v = qkv_ref.at[0, 2][cur, :].astype(BF16)
                    s = _nt(k, q2) * SCALE + jnp.where(n < nb - 1, bias, bias_last)
                    p = jnp.exp(s - jnp.concatenate([own[2], nxt[2]], axis=1))
                    d_ref.at[0, 2][cur, :] = _nn(p.astype(BF16), do2)
                    dp = _nt(v, do2)
                    ds = (p * (dp - jnp.concatenate([own[3], nxt[3]], axis=1)) * SCALE).astype(BF16)
                    d_ref.at[0, 1][cur, :] = _nn(ds, q2)
                    dq2 = _tn(ds, k)
                    d_ref.at[0, 0][cur, :] = dq2[:HEAD] + jnp.where(n > 0, dq_part, 0.0)
                    return dq2[HEAD:], nxt

                def steps(i, carry):
                    for u in range(UNROLL):
                        carry = step(i * UNROLL + u, nb <= UNROLL and u % nb == 0, carry)
                    return carry

                lax.fori_loop(0, nblk // UNROLL, steps, (jnp.zeros((HEAD, HEAD), F32), query_side(0)))

    col_blk = pl.BlockSpec((S, HEAD), lambda j, g: (0, j))
    qkv_blk = pl.BlockSpec((1, 3, S, HEAD), lambda j, g: (g, 0, 0, j))
    return pl.pallas_call(
        body, name="attn_bwd", grid=(N_SLOT, 3),
        in_specs=[qkv_blk, col_blk, col_blk, col_blk, pl.BlockSpec((1, 1, HEAD, 2 * HEAD), lambda j, g: (g, j, 0, 0))],
        out_specs=qkv_blk,
        out_shape=jax.ShapeDtypeStruct((3, 3, S, AOW), F32),
        compiler_params=_cparams("parallel", "arbitrary"),
    )(qkv, do, lse, dl, bias_t)


def _bwd_in(dqkv, de, w_int, x, dx1, g_mix, sc1):
    S = x.shape[0]
    tm = TM
    dqkv = dqkv.reshape(3, 3, S, AOW)

    def body(dq_ref, de_ref, wq_ref, wk_ref, wv_ref, wa_ref, wb_ref, x_ref, dx1_ref, g_ref, sc_ref, gx_ref, pv_ref):
        acc = gx_ref
        i, k = pl.program_id(0), pl.program_id(1)

        @pl.when((i == 0) & (k == 0))
        def _():
            pv_ref[...] = jnp.zeros_like(pv_ref)

        @pl.when(k == 0)
        def _():
            acc[...] = jnp.zeros_like(acc)

        @pl.when(k < 3)
        def _():
            lhs = jnp.concatenate([dq_ref[0, t].astype(BF16) for t in range(3)], axis=1)
            acc[...] += _nn(lhs, jnp.concatenate([wq_ref[...], wk_ref[...], wv_ref[...]], axis=0))

        @pl.when(k >= 3)
        def _():
            acc[...] += _nn(de_ref[0], jnp.concatenate([wa_ref[...], wb_ref[...]], axis=0))

        @pl.when(k == 7)
        def _():
            dh = acc[...]
            xv = x_ref[...]
            r = _rms_r(xv)
            g = g_ref[...]
            dxn, pg = _rms_bwd(xv, r, g, dh * (1.0 + sc_ref[...]))
            gx_ref[...] = dx1_ref[...] + dxn
            pv_ref[0:1, :] += _rowsum(dh)
            pv_ref[1:2, :] += _rowsum(dh * (xv * r * g))
            pv_ref[2:3, :] += _rowsum(pg)

    grp = lambda k: jnp.minimum(k, 2)
    chunk = lambda k: jnp.maximum(k - 3, 0)
    wblk = lambda f: pl.BlockSpec((512, D), lambda i, k: (f(k), 0))
    row = pl.BlockSpec((tm, D), lambda i, k: (i, 0))
    once = pl.BlockSpec((tm, D), lambda i, k: (i, 0), pipeline_mode=pl.Buffered(1))
    return pl.pallas_call(
        body, name="bwd_in", grid=(S // tm, 8),
        in_specs=[pl.BlockSpec((1, 3, tm, 512), lambda i, k: (grp(k), 0, i, 0)),
                  pl.BlockSpec((1, tm, D), lambda i, k: (chunk(k), i, 0)),
                  wblk(grp), wblk(lambda k: 3 + grp(k)), wblk(lambda k: 6 + grp(k)),
                  wblk(lambda k: 9 + 2 * chunk(k)), wblk(lambda k: 10 + 2 * chunk(k)),
                  once, once, _vec_spec(), _vec_spec()],
        out_specs=[row, _const_spec((8, D))],
        out_shape=[jax.ShapeDtypeStruct((S, D), F32), jax.ShapeDtypeStruct((8, D), F32)],
        compiler_params=_cparams("arbitrary", "arbitrary"),
    )(dqkv, de, w_int, w_int, w_int, w_int, w_int, x, dx1, g_mix, sc1)


def _grad_w(name, a, b):
    S, ka = a.shape
    nb = b.shape[1]

    def body(a_ref, b_ref, o_ref):
        o_ref[...] = _tn(a_ref[...], b_ref[...]).astype(BF16)

    return pl.pallas_call(
        body, name=name, grid=(ka // 512,),
        in_specs=[pl.BlockSpec((S, 512), lambda n: (0, n)), pl.BlockSpec((S, nb), lambda n: (0, 0))],
        out_specs=pl.BlockSpec((512, nb), lambda n: (n, 0)),
        out_shape=jax.ShapeDtypeStruct((ka, nb), BF16),
        compiler_params=_cparams("parallel"),
    )(a, b)


def _grad_w_small(dya, o_bf, cbu, dyc, merged, dmo, after):
    S = dya.shape[0]

    def body(dya_ref, o_ref, cbu_ref, dyc_ref, mg_ref, dmo_ref, after_ref, gba_ref, gbc_ref, gout_ref):
        gba_ref[...] = _tn(dya_ref[...], o_ref[...]).astype(BF16)
        gbc_ref[...] = _tn(cbu_ref[...], dyc_ref[...]).astype(BF16)
        gout_ref[...] = _tn(mg_ref[...], dmo_ref[...]).astype(BF16)

    a_blk = pl.BlockSpec((S, 512), lambda n: (0, n))
    whole = lambda w: pl.BlockSpec((S, w), lambda n: (0, 0))
    out = lambda w: pl.BlockSpec((512, w), lambda n: (n, 0))
    return pl.pallas_call(
        body, name="grad_w_small", grid=(D // 512,),
        in_specs=[a_blk, whole(AOW), a_blk, whole(D), a_blk, whole(D), pl.BlockSpec(memory_space=pl.ANY)],
        out_specs=[out(AOW), out(D), out(D)],
        out_shape=[jax.ShapeDtypeStruct((D, AOW), BF16), jax.ShapeDtypeStruct((D, D), BF16), jax.ShapeDtypeStruct((D, D), BF16)],
        compiler_params=_cparams("parallel"),
    )(dya, o_bf, cbu, dyc, merged, dmo, after)


def _grad_w_in(dqkv, de, h):
    S = h.shape[0]

    def body(dq_ref, de_ref, h_ref, o_ref):
        n = pl.program_id(0)

        @pl.when(n < 9)
        def _():
            o_ref[...] = _tn(dq_ref[0].astype(BF16), h_ref[...]).astype(BF16)

        @pl.when(n >= 9)
        def _():
            o_ref[...] = _tn(de_ref[0], h_ref[...]).astype(BF16)

    def e_idx(n):
        kk = jnp.maximum(n - 9, 0)
        return (kk // 2, 0, kk % 2)

    return pl.pallas_call(
        body, name="grad_w_in", grid=(19,),
        in_specs=[pl.BlockSpec((1, S, 512), lambda n: (jnp.minimum(n, 8), 0, 0)), pl.BlockSpec((1, S, 512), e_idx),
                  pl.BlockSpec((S, D), lambda n: (0, 0))],
        out_specs=pl.BlockSpec((512, D), lambda n: (_win_rowblock(n), 0)),
        out_shape=jax.ShapeDtypeStruct((19 * 512, D), BF16),
        compiler_params=_cparams("parallel"),
    )(dqkv, de, h)


def _local_step(x, h, tgt, mod, g_mix, g_mlp, g_fin, ba, bb, cw8, w_int, mix_weights, mlp_weights, mlp_grads_ready, w_in_grad_ready,
                other_grads_ready):
    S = x.shape[0]
    sh1, sc1, gt1, sh2, sc2, gt2 = [mod[k:k + 1] for k in range(6)]
    bias, bias_t = _bias_table()

    qkv, e = _proj(h, w_int)
    qkv = qkv.reshape(3, 3, S, AOW)
    o_attn, lse = _attn_fwd(qkv, bias)
    w_bat, w_bc, w_out = mix_weights(o_attn)
    o_bf, cbu, ya, yc, merged = _mix(o_attn, e, cw8, ba, bb, w_bat, w_bc)
    x1, mo, h2 = _out_proj(merged, w_out, x, gt1, g_mlp, sc2, sh2)
    w_mit, w_mo = mlp_weights(x1)
    a, f = _mlp_in(h2, w_mit)
    mlp, dx2, pv_f = _mlp_out(f, w_mo, x1, gt2, g_fin, tgt)

    da, dmo2, pv_a = _bwd_mlp_a(dx2, gt2, mlp, w_mo, a)
    dx1, pv_b = _bwd_mlp_b(da, w_mit, x1, dx2, g_mlp, sc2)
    zero = mlp_grads_ready(_grad_w("grad_w_mi", da, h2), _grad_w("grad_w_mo", f, dmo2))
    dmo, dya, dyc, do, dl, de, pv_m = _bwd_mix(dx1, gt1 + zero, mo, e, cw8, ba, bb, ya, yc, o_attn, w_out, w_bc, w_bat)
    dqkv = _attn_bwd(qkv, do, lse, dl, bias_t).reshape(9, S, AOW)
    after = w_in_grad_ready(_grad_w_in(dqkv, de, h))
    zero = other_grads_ready(*_grad_w_small(dya, o_bf, cbu, dyc, merged, dmo, after))
    grad_x, pv_i = _bwd_in(dqkv, de, w_int, x, dx1, g_mix, sc1 + zero)

    vec = jnp.concatenate([pv_i[0:2], pv_m[0:1], pv_b[0:2], pv_a[0:1], pv_i[2:3], pv_b[2:3], pv_f[0:1],
                           pv_m[1:3], pv_m[3:6], pv_f[1:2], jnp.zeros((1, D), F32)], axis=0)
    return grad_x, vec


def _my_place():
    return lax.axis_index("x"), lax.axis_index("y"), lax.axis_index("c")


def _dev_index(px, py, pc):
    return 4 * px + 2 * py + pc


def _peer(x, y, c, m):
    return (x ^ ((m >> 2) & 1), y ^ ((m >> 1) & 1), c ^ (m & 1))


HBM_SPEC = pl.BlockSpec(memory_space=pltpu.HBM)
SEM_SPEC = pl.BlockSpec(memory_space=pltpu.SEMAPHORE)
N_PEER = N_DEV - 1


SPLIT_MASKS = {"gather": tuple(range(1, N_DEV)), "scatter": tuple(range(1, N_DEV)), "chips": (2, 4, 6), "sibling": (1, 1, 1, 1)}


def _split_copy(mode, src_ref, land_ref, send_sems, recv_sems, w, j, place, arriving=False):
    x, y, c = place
    masks = SPLIT_MASKS[mode]
    peer = _peer(x, y, c, masks[j])
    k = w * len(masks) + j
    sender, receiver = ((peer, (x, y, c)) if arriving else ((x, y, c), peer))
    if mode == "gather":
        r = src_ref.shape[0]
        src, dst = src_ref, land_ref.at[pl.ds(pl.multiple_of(_dev_index(*sender) * r, 16), r), :]
    elif mode == "scatter":
        r = land_ref.shape[1]
        src, dst = src_ref.at[pl.ds(pl.multiple_of(_dev_index(*receiver) * r, 16), r), :], land_ref.at[j]
    elif mode == "chips":
        src, dst = src_ref.at[2 * receiver[0] + receiver[1]], land_ref.at[j]
    else:
        r = land_ref.shape[1]
        src, dst = src_ref.at[pl.ds(pl.multiple_of((2 * j + receiver[2]) * r, 16), r), :], land_ref.at[j]
    return pltpu.make_async_remote_copy(src_ref=src, dst_ref=dst, send_sem=send_sems.at[k], recv_sem=recv_sems.at[k],
                                        device_id=peer, device_id_type=MESH)


def _split_start(name, mode, srcs, lands):
    n = len(srcs)
    nm = len(SPLIT_MASKS[mode])

    def body(*refs):
        src, land = refs[:n], refs[n:2 * n]
        send_sems, recv_sems = refs[2 * n], refs[2 * n + 1]
        token = refs[-1]
        place = _my_place()
        for w in range(n):
            for j in range(nm):
                _split_copy(mode, src[w], land[w], send_sems, recv_sems, w, j, place).start()
        token[...] = jnp.zeros_like(token)

    hbm = lambda t: pltpu.HBM(t.shape, t.dtype)
    out = pl.pallas_call(
        body, name=name,
        out_shape=(pltpu.SemaphoreType.DMA((n * nm,)), pltpu.SemaphoreType.DMA((n * nm,)), *[hbm(t) for t in srcs],
                   *[hbm(t) for t in lands], jax.ShapeDtypeStruct((8, 128), F32)),
        in_specs=(HBM_SPEC,) * (2 * n),
        out_specs=(SEM_SPEC, SEM_SPEC) + (HBM_SPEC,) * (2 * n) + (pl.BlockSpec(memory_space=pltpu.VMEM),),
        input_output_aliases={i: 2 + i for i in range(2 * n)},
        compiler_params=pltpu.CompilerParams(has_side_effects=pltpu.SideEffectType.DATAFLOW_SIDE_EFFECTING),
    )(*[pltpu.with_memory_space_constraint(t, pltpu.HBM) for t in (*srcs, *lands)])
    return out[0], out[1], out[2:2 + n], out[2 + n:2 + 2 * n], out[-1][0:1, 0:1], out[-1]


def _split_wait(name, mode, send_sems, recv_sems, srcs, lands, after):
    n = len(srcs)

    def body(*refs):
        src, land = refs[:n], refs[n:2 * n]
        ssem, rsem = refs[2 * n], refs[2 * n + 1]
        place = _my_place()
        for w in range(n):
            for j in range(len(SPLIT_MASKS[mode])):
                _split_copy(mode, src[w], land[w], ssem, rsem, w, j, place).wait_send()
                _split_copy(mode, src[w], land[w], ssem, rsem, w, j, place, arriving=True).wait_recv()

    hbm = lambda t: pltpu.HBM(t.shape, t.dtype)
    out = pl.pallas_call(
        body, name=name,
        out_shape=tuple(hbm(t) for t in (*srcs, *lands)),
        in_specs=(HBM_SPEC,) * (2 * n) + (SEM_SPEC, SEM_SPEC, pl.BlockSpec(memory_space=pl.ANY)),
        out_specs=(HBM_SPEC,) * (2 * n),
        input_output_aliases={i: i for i in range(2 * n)},
        compiler_params=pltpu.CompilerParams(has_side_effects=pltpu.SideEffectType.DATAFLOW_SIDE_EFFECTING),
    )(*srcs, *lands, send_sems, recv_sems, after)
    return out[:n], out[n:]


def _sibling_exchange(grads):
    nw = len(grads)
    HBM = pl.BlockSpec(memory_space=pl.ANY)

    def body(*refs):
        g, land = refs[:nw], refs[nw:2 * nw]
        send_sems, recv_sems = refs[2 * nw:]
        x, y, c = _my_place()

        def copy(w, q, owner_core):
            r = land[w].shape[1]
            return pltpu.make_async_remote_copy(
                src_ref=g[w].at[pl.ds(pl.multiple_of((2 * q + owner_core) * r, 16), r), :], dst_ref=land[w].at[q],
                send_sem=send_sems.at[w, q], recv_sem=recv_sems.at[w, q], device_id=(x, y, 1 - c), device_id_type=MESH)

        sends = [copy(w, q, 1 - c) for w in range(nw) for q in range(4)]
        for cp in sends:
            cp.start()
        for w in range(nw):
            for q in range(4):
                copy(w, q, c).wait_recv()
        for cp in sends:
            cp.wait_send()

    return pl.pallas_call(
        body, name="sibling_exchange",
        out_shape=[jax.ShapeDtypeStruct((4, a.shape[0] // N_DEV, a.shape[1]), a.dtype) for a in grads],
        in_specs=[HBM] * nw, out_specs=[HBM] * nw,
        scratch_shapes=[pltpu.SemaphoreType.DMA((nw, 4)), pltpu.SemaphoreType.DMA((nw, 4))],
    )(*grads)


def _pair_sums(gs, sibs, core):
    n = len(gs)

    def body(core_ref, *refs):
        for w in range(n):
            refs[2 * n + w][0] = (refs[w][0, 0].astype(F32) + refs[n + w][0].astype(F32)).astype(BF16)

    in_specs = [pl.BlockSpec((1, 1) + t.shape[1:], lambda q, core_ref: (q, core_ref[0], 0, 0)) for t in sibs]
    in_specs += [pl.BlockSpec((1,) + t.shape[1:], lambda q, core_ref: (q, 0, 0)) for t in sibs]
    return pl.pallas_call(
        body, name="pair_sums",
        grid_spec=pltpu.PrefetchScalarGridSpec(
            num_scalar_prefetch=1, grid=(4,), in_specs=in_specs,
            out_specs=[pl.BlockSpec((1,) + t.shape[1:], lambda q, core_ref: (q, 0, 0)) for t in sibs]),
        out_shape=[jax.ShapeDtypeStruct(t.shape, BF16) for t in sibs],
        compiler_params=_cparams("parallel"),
    )(core, *[g.reshape(4, 2, t.shape[1], t.shape[2]) for g, t in zip(gs, sibs)], *sibs)


def _own_rows_into_zones(shards, me):
    n = len(shards)

    def body(me_ref, *refs):
        for w in range(n):
            refs[2 * n + w][...] = refs[w][...]

    zones = [lax.empty((N_DEV * t.shape[0], t.shape[1]), t.dtype) for t in shards]
    return pl.pallas_call(
        body, name="own_rows_into_zones",
        grid_spec=pltpu.PrefetchScalarGridSpec(
            num_scalar_prefetch=1, grid=(1,),
            in_specs=[pl.BlockSpec(t.shape, lambda i, me_ref: (0, 0)) for t in shards] + [pl.BlockSpec(memory_space=pl.ANY)] * n,
            out_specs=[pl.BlockSpec(t.shape, lambda i, me_ref: (me_ref[0], 0)) for t in shards]),
        out_shape=[jax.ShapeDtypeStruct(z.shape, z.dtype) for z in zones],
        input_output_aliases={1 + n + w: w for w in range(n)},
        compiler_params=_cparams("arbitrary"),
    )(me, *shards, *zones)


def _allgather_small(v, name):
    r, ccols = v.shape

    def body(v_ref, out_ref, send_sems, recv_sems):
        x, y, c = _my_place()
        my_idx = _dev_index(x, y, c)
        out_ref[my_idx] = v_ref[...]

        def copy(m):
            peer = _peer(x, y, c, m)
            return pltpu.make_async_remote_copy(
                src_ref=v_ref, dst_ref=out_ref.at[my_idx],
                send_sem=send_sems.at[m - 1], recv_sem=recv_sems.at[m - 1], device_id=peer, device_id_type=MESH)

        def arrival(m):
            peer = _peer(x, y, c, m)
            return pltpu.make_async_remote_copy(
                src_ref=v_ref, dst_ref=out_ref.at[_dev_index(*peer)],
                send_sem=send_sems.at[m - 1], recv_sem=recv_sems.at[m - 1], device_id=peer, device_id_type=MESH)

        sends = [copy(m) for m in range(1, N_DEV)]
        for cp in sends:
            cp.start()
        for m in range(1, N_DEV):
            arrival(m).wait_recv()
        for cp in sends:
            cp.wait_send()

    return pl.pallas_call(
        body, name=name,
        out_shape=jax.ShapeDtypeStruct((N_DEV, r, ccols), v.dtype),
        in_specs=[pl.BlockSpec(memory_space=pltpu.VMEM)], out_specs=pl.BlockSpec(memory_space=pltpu.VMEM),
        scratch_shapes=[pltpu.SemaphoreType.DMA((7,)), pltpu.SemaphoreType.DMA((7,))],
    )(v)


def _gather_w_in_and_condition(shard, pay, w_ada, b_cols):
    r, ccols = shard.shape
    ncol = w_ada.shape[1]

    def body(sh_ref, pay_ref, w_ref, b_ref, full_ref, got_ref, act_ref, mod_ref, send_sems, recv_sems, small_send, small_recv, local_sem):
        x, y, c = _my_place()
        me, sibling = (x, y, c), (x, y, 1 - c)
        my_idx = _dev_index(x, y, c)
        chips = [(1 - x, y), (x, 1 - y), (1 - x, 1 - y)]

        def small(rnd, buf, m, arriving=False):
            peer = _peer(x, y, c, m)
            slot = _dev_index(*peer) if arriving else my_idx
            return pltpu.make_async_remote_copy(
                src_ref=buf.at[my_idx], dst_ref=buf.at[slot], send_sem=small_send.at[rnd, m - 1],
                recv_sem=small_recv.at[rnd, m - 1], device_id=peer, device_id_type=MESH)

        def rows(px, py, pc):
            return full_ref.at[pl.ds(pl.multiple_of(_dev_index(px, py, pc) * r, 16), r), :]

        def copy(k, block, to, src=None):
            return pltpu.make_async_remote_copy(
                src_ref=rows(*block) if src is None else src, dst_ref=rows(*block),
                send_sem=send_sems.at[k], recv_sem=recv_sems.at[k], device_id=to, device_id_type=MESH)

        got_ref[my_idx] = pay_ref[...]
        round1 = [small(0, got_ref, m) for m in range(1, N_DEV)]
        for cp in round1:
            cp.start()
        mine = pltpu.make_async_copy(sh_ref, rows(*me), local_sem)
        mine.start()
        first = [copy(0, me, sibling, src=sh_ref)] + [copy(1 + j, me, (*chip, c), src=sh_ref) for j, chip in enumerate(chips)]
        for cp in first:
            cp.start()

        for m in range(1, N_DEV):
            small(0, got_ref, m, arriving=True).wait_recv()
        cv = jnp.concatenate([got_ref[s, 0:1, :] for s in range(N_DEV)], axis=0)
        act = cv * _sigmoid(cv)
        act_ref[...] = act
        mod_ref[my_idx] = jnp.dot(act, w_ref[...], preferred_element_type=F32, precision=lax.Precision.HIGHEST) + b_ref[...]
        round2 = [small(1, mod_ref, m) for m in range(1, N_DEV)]
        for cp in round2:
            cp.start()

        passed = []
        for j, chip in enumerate(chips):
            copy(1 + j, (*chip, c), me).wait_recv()
            fwd = copy(4 + j, (*chip, c), sibling)
            fwd.start()
            passed.append(fwd)
        copy(0, sibling, me).wait_recv()
        for j, chip in enumerate(chips):
            copy(4 + j, (*chip, 1 - c), me).wait_recv()
        for m in range(1, N_DEV):
            small(1, mod_ref, m, arriving=True).wait_recv()
        for cp in first + passed + round1 + round2:
            cp.wait_send()
        mine.wait()

    anyspec = pl.BlockSpec(memory_space=pl.ANY)
    vmem = pl.BlockSpec(memory_space=pltpu.VMEM)
    return pl.pallas_call(
        body, name="gather_w_in_and_condition",
        out_shape=[jax.ShapeDtypeStruct((N_DEV * r, ccols), shard.dtype), jax.ShapeDtypeStruct((N_DEV, 8, D), F32),
                   jax.ShapeDtypeStruct((N_DEV, D), F32), jax.ShapeDtypeStruct((N_DEV, N_DEV, ncol), F32)],
        in_specs=[anyspec, vmem, vmem, vmem], out_specs=[anyspec, vmem, vmem, vmem],
        scratch_shapes=[pltpu.SemaphoreType.DMA((7,)), pltpu.SemaphoreType.DMA((7,)), pltpu.SemaphoreType.DMA((2, 7)),
                        pltpu.SemaphoreType.DMA((2, 7)), pltpu.SemaphoreType.DMA],
        compiler_params=_cparams(),
    )(shard, pay, w_ada, b_cols)


def _ada_bwd(act_t, gm_cols):
    def body(a_ref, g_ref, o_ref):
        o_ref[...] = jnp.dot(a_ref[...], g_ref[...], preferred_element_type=F32, precision=lax.Precision.HIGHEST)

    return pl.pallas_call(
        body, name="ada_bwd", out_shape=jax.ShapeDtypeStruct((D, gm_cols.shape[1]), F32), compiler_params=_cparams(),
    )(act_t, gm_cols)


def _row_tile(r):
    for t in (256, 304, 128, 64, 16):
        if r % t == 0:
            return t
    return r


def _sum_parts(parts, name, own=None, slot=None):
    k, r, ccols = parts.shape
    tr = _row_tile(r)

    def body(*refs):
        p_ref, o_ref = (refs[0], refs[-1]) if own is None else (refs[1], refs[-1])
        acc = p_ref[0].astype(F32) if own is None else refs[2][0].astype(F32) + p_ref[0].astype(F32)
        for s in range(1, k):
            acc = acc + p_ref[s].astype(F32)
        o_ref[...] = acc

    if own is None:
        blk = pl.BlockSpec((tr, ccols), lambda i: (i, 0))
        return pl.pallas_call(
            body, name=name, grid=(r // tr,), in_specs=[pl.BlockSpec((k, tr, ccols), lambda i: (0, i, 0))], out_specs=blk,
            out_shape=jax.ShapeDtypeStruct((r, ccols), F32), compiler_params=_cparams("parallel"),
        )(parts)
    return pl.pallas_call(
        body, name=name,
        grid_spec=pltpu.PrefetchScalarGridSpec(
            num_scalar_prefetch=1, grid=(r // tr,),
            in_specs=[pl.BlockSpec((k, tr, ccols), lambda i, s_ref: (0, i, 0)),
                      pl.BlockSpec((1, tr, ccols), lambda i, s_ref: (s_ref[0], i, 0))],
            out_specs=pl.BlockSpec((tr, ccols), lambda i, s_ref: (i, 0))),
        out_shape=jax.ShapeDtypeStruct((r, ccols), F32),
        compiler_params=_cparams("parallel"),
    )(slot, parts, own)


def _adamw(w, g, m, v, name):
    r, ccols = w.shape
    tr = _row_tile(r)
    c1 = 1.0 / (1.0 - B1 ** STEP)
    c2 = 1.0 / (1.0 - B2 ** STEP)

    def body(w_ref, g_ref, m_ref, v_ref, d_ref, nm_ref, nv_ref):
        gv = g_ref[...]
        nm = B1 * m_ref[...] + (1.0 - B1) * gv
        nv = B2 * v_ref[...] + (1.0 - B2) * jnp.square(gv)
        nm_ref[...] = nm
        nv_ref[...] = nv
        d_ref[...] = -LR * ((nm * c1) / (jnp.sqrt(nv * c2) + ADAM_EPS) + WD * w_ref[...])

    blk = pl.BlockSpec((tr, ccols), lambda i: (i, 0))
    return pl.pallas_call(
        body, name=name, grid=(r // tr,), in_specs=[blk] * 4, out_specs=[blk] * 3,
        out_shape=[jax.ShapeDtypeStruct((r, ccols), F32)] * 3,
        compiler_params=_cparams("parallel"),
    )(w, g, m, v)


def _sum_adamw(parts, own, slot, w, m, v, name):
    k, r, ccols = parts.shape
    tr = _row_tile(r)
    c1 = 1.0 / (1.0 - B1 ** STEP)
    c2 = 1.0 / (1.0 - B2 ** STEP)

    def body(s_ref, p_ref, own_ref, w_ref, m_ref, v_ref, g_ref, d_ref, nm_ref, nv_ref):
        gv = own_ref[0].astype(F32)
        for s in range(k):
            gv = gv + p_ref[s].astype(F32)
        g_ref[...] = gv
        nm = B1 * m_ref[...] + (1.0 - B1) * gv
        nv = B2 * v_ref[...] + (1.0 - B2) * jnp.square(gv)
        nm_ref[...] = nm
        nv_ref[...] = nv
        d_ref[...] = -LR * ((nm * c1) / (jnp.sqrt(nv * c2) + ADAM_EPS) + WD * w_ref[...])

    blk = pl.BlockSpec((tr, ccols), lambda i, s_ref: (i, 0))
    return pl.pallas_call(
        body, name=name,
        grid_spec=pltpu.PrefetchScalarGridSpec(
            num_scalar_prefetch=1, grid=(r // tr,),
            in_specs=[pl.BlockSpec((k, tr, ccols), lambda i, s_ref: (0, i, 0)),
                      pl.BlockSpec((1, tr, ccols), lambda i, s_ref: (s_ref[0], i, 0))] + [blk] * 3,
            out_specs=[blk] * 4),
        out_shape=[jax.ShapeDtypeStruct((r, ccols), F32)] * 4,
        compiler_params=_cparams("parallel"),
    )(slot, parts, own, w, m, v)


VEC_ROWS = ((0, 6), (6, 7), (9, 11), (11, 14), (7, 8), (8, 9))


def _adamw_vectors(w, g, m, v):
    c1 = 1.0 / (1.0 - B1 ** STEP)
    c2 = 1.0 / (1.0 - B2 ** STEP)

    def put(refs, p):
        for ref, (lo, hi) in zip(refs, VEC_ROWS):
            if ref.shape == (3, HEAD):
                ref[...] = p[lo:hi, :HEAD]
            else:
                ref[...] = jnp.concatenate([p[k:k + 1] for k in range(lo, hi)], axis=1)

    def body(w_ref, g_ref, m_ref, v_ref, *outs):
        gv = g_ref[...]
        nm = B1 * m_ref[...] + (1.0 - B1) * gv
        nv = B2 * v_ref[...] + (1.0 - B2) * jnp.square(gv)
        delta = -LR * ((nm * c1) / (jnp.sqrt(nv * c2) + ADAM_EPS) + WD * w_ref[...])
        for kind, p in enumerate((gv, delta, nm, nv)):
            put(outs[6 * kind:6 * kind + 6], p)

    shapes = [(1, 6 * D), (1, D), (1, 2 * D), (3, HEAD), (1, D), (1, D)]
    out = pl.pallas_call(
        body, name="adamw_vectors", out_shape=[jax.ShapeDtypeStruct(sh, F32) for sh in shapes] * 4, compiler_params=_cparams(),
    )(w, g, m, v)
    fix = lambda t: (t[0], t[1], t[2], t[3][None], t[4], t[5].reshape(D))
    return [fix(out[6 * kind:6 * kind + 6]) for kind in range(4)]


def _pack_vectors(b_ada, g_mix, g_mlp, g_fin, b_gate, conv_w):
    conv_rows = jnp.pad(conv_w.reshape(3, HEAD), ((0, 0), (0, D - HEAD)))
    return jnp.concatenate([b_ada.reshape(6, D), g_mix.reshape(1, D), g_mlp.reshape(1, D), g_fin.reshape(1, D),
                            b_gate.reshape(2, D), conv_rows, jnp.zeros((2, D), F32)], axis=0)


def kernel(x, c, w_ada, b_ada, g_norm_mix, w_in, b_gate, conv_w, w_branch_attn, w_branch_conv, w_out, g_norm_mlp, w_mlp_in, w_mlp_out, g_norm_final, loss_target, m_w_ada, m_b_ada, m_g_norm_mix, m_w_in, m_b_gate, m_conv_w, m_w_branch_attn, m_w_branch_conv, m_w_out, m_g_norm_mlp, m_w_mlp_in, m_w_mlp_out, m_g_norm_final, v_w_ada, v_b_ada, v_g_norm_mix, v_w_in, v_b_gate, v_conv_w, v_w_branch_attn, v_w_branch_conv, v_w_out, v_g_norm_mlp, v_w_mlp_in, v_w_mlp_out, v_g_norm_final):
    S = x.shape[1]
    xi, yi, ci = _my_place()
    me = _dev_index(xi, yi, ci)
    x2 = x.reshape(S, D)
    tgt = loss_target.reshape(S, D)

    pay = jnp.zeros((8, D), F32).at[0].set(c[0]).at[1:4, :HEAD].set(conv_w[0])
    ncol = w_ada.shape[2]
    b_cols = lax.dynamic_slice(b_ada, (0, me * ncol), (1, ncol))
    w_int, got, act, mod_all = _gather_w_in_and_condition(w_in[0].T.astype(BF16), pay, w_ada[0], b_cols)
    cw8 = jnp.pad(got[:, 1:4, :HEAD].transpose(1, 0, 2).reshape(3, D), ((0, 5), (0, 0)))
    mod = lax.dynamic_index_in_dim(mod_all, me, axis=1, keepdims=False).reshape(6, D)
    late = [w_branch_attn[0].T.astype(BF16), w_branch_conv[0].astype(BF16), w_out[0].astype(BF16),
            w_mlp_in[0].T.astype(BF16), w_mlp_out[0].astype(BF16)]
    w_int, late = lax.optimization_barrier((w_int, late))
    zones = _own_rows_into_zones(late, me.reshape(1).astype(jnp.int32))
    ag_mix = _split_start("gather_mix_start", "gather", late[:3], zones[:3])
    ag_mlp = _split_start("gather_mlp_start", "gather", late[3:], zones[3:])
    mod = mod + ag_mix[4] + ag_mlp[4]
    h = _prenorm(x2, g_norm_mix, mod[1:2], mod[0:1])

    def mix_weights(o_attn):
        return _split_wait("gather_mix_wait", "gather", *ag_mix[:4], o_attn)[1]

    def mlp_weights(x1):
        return _split_wait("gather_mlp_wait", "gather", *ag_mlp[:4], x1)[1]

    rs = {}

    def mlp_grads_ready(*grads):
        lands = [lax.empty((N_PEER, t.shape[0] // N_DEV, t.shape[1]), BF16) for t in grads]
        rs["mlp"] = _split_start("scatter_mlp_start", "scatter", grads, lands)
        return rs["mlp"][4]

    def w_in_grad_ready(g_in):
        r = g_in.shape[0] // N_DEV
        rs["sib"] = _split_start("sibling_w_in_start", "sibling", [g_in], [lax.empty((4, r, g_in.shape[1]), BF16)])
        return rs["sib"][5]

    def other_grads_ready(*small):
        core = ci.reshape(1).astype(jnp.int32)
        (g_in,), (sib_in,) = _split_wait("sibling_w_in_wait", "sibling", *rs["sib"][:4], small[0])
        pair = _pair_sums([g_in, *small], [sib_in, *_sibling_exchange(small)], core)
        lands = [lax.empty((3,) + t.shape[1:], BF16) for t in pair]
        rs["rest"] = _split_start("scatter_rest_start", "chips", pair, lands)
        return rs["rest"][4]

    ba, bb = b_gate[:, :D], b_gate[:, D:]
    grad_x, vec = _local_step(
        x2, h, tgt, mod, g_norm_mix, g_norm_mlp, g_norm_final.reshape(1, D), ba, bb, cw8, w_int, mix_weights, mlp_weights,
        mlp_grads_ready, w_in_grad_ready, other_grads_ready)

    vec_all = _allgather_small(vec, "gather_vec")
    vec_sum = _sum_parts(vec_all, "sum_vec")
    loss = vec_sum[14, 0]
    gm_all = vec_all[:, 0:6, :].reshape(N_DEV, 6 * D)
    gm_cols = lax.dynamic_slice(gm_all, (0, me * ncol), (N_DEV, ncol))
    g_w_ada = _ada_bwd(act.T, gm_cols)
    conv_cols = lax.dynamic_slice(vec_sum[11:14], (0, me * HEAD), (3, HEAD))
    g_pack = jnp.concatenate([vec_sum[0:11], jnp.pad(conv_cols, ((0, 0), (0, D - HEAD))), jnp.zeros((2, D), F32)], axis=0)
    packs = [_pack_vectors(*t) for t in ((b_ada, g_norm_mix, g_norm_mlp, g_norm_final, b_gate, conv_w),
                                         (m_b_ada, m_g_norm_mix, m_g_norm_mlp, m_g_norm_final, m_b_gate, m_conv_w),
                                         (v_b_ada, v_g_norm_mix, v_g_norm_mlp, v_g_norm_final, v_b_gate, v_conv_w))]
    gv, dv, mv, vv = _adamw_vectors(packs[0], g_pack, packs[1], packs[2])
    d_ada, nm_ada, nv_ada = _adamw(w_ada[0], g_w_ada, m_w_ada[0], v_w_ada[0], "adamw_w_ada")

    big = {}
    srcs, lands = _split_wait("scatter_mlp_wait", "scatter", *rs["mlp"][:4], d_ada)
    own = [g.reshape((N_DEV,) + land.shape[1:]) for g, land in zip(srcs, lands)]
    slot = me.reshape(1).astype(jnp.int32)
    g_mi = _sum_parts(lands[0], "sum_w_mi", own=own[0], slot=slot).T
    big["w_mi"] = (g_mi[None],) + tuple(t[None] for t in _adamw(w_mlp_in[0], g_mi, m_w_mlp_in[0], v_w_mlp_in[0], "adamw_w_mi"))
    big["w_mo"] = tuple(t[None] for t in _sum_adamw(lands[1], own[1], slot, w_mlp_out[0], m_w_mlp_out[0], v_w_mlp_out[0], "adamw_w_mo"))
    own, lands = _split_wait("scatter_rest_wait", "chips", *rs["rest"][:4], big["w_mo"][1])
    slot = (2 * xi + yi).reshape(1).astype(jnp.int32)
    big["w_in"] = tuple(t.T[None] for t in _sum_adamw(lands[0], own[0], slot, w_in[0].T, m_w_in[0].T, v_w_in[0].T, "adamw_w_in"))
    g_ba = _sum_parts(lands[1], "sum_w_ba", own=own[1], slot=slot).T
    big["w_ba"] = (g_ba[None],) + tuple(t[None] for t in _adamw(w_branch_attn[0], g_ba, m_w_branch_attn[0], v_w_branch_attn[0], "adamw_w_ba"))
    big["w_bc"] = tuple(t[None] for t in _sum_adamw(lands[2], own[2], slot, w_branch_conv[0], m_w_branch_conv[0], v_w_branch_conv[0], "adamw_w_bc"))
    big["w_out"] = tuple(t[None] for t in _sum_adamw(lands[3], own[3], slot, w_out[0], m_w_out[0], v_w_out[0], "adamw_w_out"))

    def ordered(k, ada, vecs):
        return (ada[None], vecs[0], vecs[1], big["w_in"][k], vecs[2], vecs[3], big["w_ba"][k], big["w_bc"][k],
                big["w_out"][k], vecs[4], big["w_mi"][k], big["w_mo"][k], vecs[5])

    return (loss, grad_x.reshape(1, S, D), *ordered(0, g_w_ada, gv), *ordered(1, d_ada, dv),
            *ordered(2, nm_ada, mv), *ordered(3, nv_ada, vv))
```

```python
import numpy as np
import jax
import jax.numpy as jnp
from jax import lax
from jax.experimental import pallas as pl
from jax.experimental.pallas import tpu as pltpu

F32, BF16 = jnp.float32, jnp.bfloat16
D = 1024
HEAD = 128
DILATIONS = (1, 4, 16)
N_SLOT = 4
AOW = N_SLOT * HEAD
DFF = 4 * D
N_DEV = 8
UNROLL = 16
EPS = 1e-6
NEG = -1e30
SCALE = HEAD ** -0.5
LR, B1, B2, ADAM_EPS, WD, STEP = 0.001, 0.9, 0.999, 1e-08, 0.01, 10
V7X_VMEM_LIMIT = 56 * 1024 * 1024
TM = 1024
MESH = pl.DeviceIdType.MESH


def _cparams(*sem):
    if sem:
        return pltpu.CompilerParams(dimension_semantics=sem, vmem_limit_bytes=V7X_VMEM_LIMIT)
    return pltpu.CompilerParams(vmem_limit_bytes=V7X_VMEM_LIMIT)


def _nn(a, b):
    return jnp.dot(a, b, preferred_element_type=F32)


def _nt(a, b):
    return lax.dot_general(a, b, (((1,), (1,)), ((), ())), preferred_element_type=F32)


def _tn(a, b):
    return lax.dot_general(a, b, (((0,), (0,)), ((), ())), preferred_element_type=F32)


def _rms_r(x):
    return lax.rsqrt(jnp.mean(x * x, axis=-1, keepdims=True) + EPS)


def _rms_bwd(x, r, g, dn):
    gy = dn * g
    dx = r * gy - x * (r * r * r) * jnp.mean(x * gy, axis=-1, keepdims=True)
    return dx, dn * (x * r)


def _sigmoid(t):
    return 1.0 / (1.0 + jnp.exp(-t))


def _rowsum(v):
    return jnp.sum(v, axis=0, keepdims=True)


def _vec_spec(n=D):
    return pl.BlockSpec((1, n), lambda *_: (0, 0))


def _const_spec(shape):
    nd = len(shape)
    return pl.BlockSpec(shape, lambda *_: (0,) * nd)


def _win_rowblock(j):
    return jnp.where(j < 9, (j % 3) * 3 + j // 3, j)


def _prenorm(x, g, sc, sh):
    S = x.shape[0]
    tm = TM

    def body(x_ref, g_ref, sc_ref, sh_ref, h_ref):
        xv = x_ref[...]
        h_ref[...] = (xv * _rms_r(xv) * g_ref[...] * (1.0 + sc_ref[...]) + sh_ref[...]).astype(BF16)

    row = pl.BlockSpec((tm, D), lambda i: (i, 0))
    return pl.pallas_call(
        body, name="prenorm", grid=(S // tm,), in_specs=[row, _vec_spec(), _vec_spec(), _vec_spec()], out_specs=row,
        out_shape=jax.ShapeDtypeStruct((S, D), BF16), compiler_params=_cparams("parallel"),
    )(x, g, sc, sh)


def _proj(h, w_int):
    S = h.shape[0]

    def body(h_ref, w_ref, q_ref, e_ref):
        j = pl.program_id(0)
        acc = _nt(h_ref[...], w_ref[...])

        @pl.when(j < 9)
        def _():
            q_ref[0] = acc

        @pl.when(j >= 9)
        def _():
            e_ref[0] = acc.astype(BF16)

    def e_idx(j):
        k = jnp.maximum(j - 9, 0)
        return (k // 2, 0, k % 2)

    return pl.pallas_call(
        body, name="proj", grid=(19,),
        in_specs=[pl.BlockSpec((S, D), lambda j: (0, 0), pipeline_mode=pl.Buffered(1)),
                  pl.BlockSpec((512, D), lambda j: (_win_rowblock(j), 0))],
        out_specs=[pl.BlockSpec((1, S, 512), lambda j: (jnp.minimum(j, 8), 0, 0)), pl.BlockSpec((1, S, 512), e_idx)],
        out_shape=[jax.ShapeDtypeStruct((9, S, 512), F32), jax.ShapeDtypeStruct((5, S, D), BF16)],
        compiler_params=_cparams("arbitrary"),
    )(h, w_int)


def _bias_table():
    slopes = (2.0 ** (-8.0 * np.arange(1, 13, dtype=np.float32) / 12.0)).astype(np.float32)
    qi = np.arange(HEAD)[:, None]
    kj = np.arange(2 * HEAD)[None, :]
    delta = HEAD + qi - kj
    mask = (delta >= 0) & (delta <= HEAD)
    out = np.zeros((3, N_SLOT, HEAD, 2 * HEAD), np.float32)
    for gi, d in enumerate(DILATIONS):
        for j in range(N_SLOT):
            bias = -slopes[gi * N_SLOT + j] * (delta * d).astype(np.float32)
            out[gi, j] = np.where(mask, bias, NEG)
    out_t = np.concatenate([out[..., HEAD:].swapaxes(-1, -2), out[..., :HEAD].swapaxes(-1, -2)], axis=-1)
    return jnp.asarray(out), jnp.asarray(out_t)


def _attn_fwd(qkv, bias):
    S = qkv.shape[2]
    nblk = S // HEAD
    rows = 256

    def body(qkv_ref, b_ref, o_ref, lse_ref, o_s, lse_s):
        g = pl.program_id(1)
        bias = b_ref[0, 0]
        col = lax.broadcasted_iota(jnp.int32, bias.shape, 1)
        bias_first = jnp.where(col < HEAD, NEG, bias)

        for gi, d in enumerate(DILATIONS):
            @pl.when(g == gi)
            def _(gi=gi, d=d):
                nb = nblk // d

                def keys(start):
                    sl = pl.ds(start, HEAD, stride=d)
                    return qkv_ref.at[0, 1][sl, :].astype(BF16), qkv_ref.at[0, 2][sl, :].astype(BF16)

                def step(b, first_of_residue, before):
                    r, n = b // nb, b % nb
                    cur = pl.ds(n * (HEAD * d) + r, HEAD, stride=d)
                    own = keys(n * (HEAD * d) + r)
                    if first_of_residue:
                        before = own
                    q = qkv_ref.at[0, 0][cur, :].astype(BF16)
                    kw = jnp.concatenate([before[0], own[0]], axis=0)
                    vw = jnp.concatenate([before[1], own[1]], axis=0)
                    s = _nt(q, kw) * SCALE + jnp.where(n > 0, bias, bias_first)
                    m = jnp.max(s, axis=-1, keepdims=True)
                    p = jnp.exp(s - m)
                    l = jnp.sum(p, axis=-1, keepdims=True)
                    o_s.at[gi][cur, :] = _nn(p.astype(BF16), vw) / l
                    lse_s.at[gi][cur, :] = jnp.broadcast_to(m + jnp.log(l), (HEAD, HEAD))
                    return own

                def steps(i, before):
                    for u in range(UNROLL):
                        before = step(i * UNROLL + u, nb <= UNROLL and u % nb == 0, before)
                    return before

                lax.fori_loop(0, nblk // UNROLL, steps, keys(0))

        @pl.when(g == len(DILATIONS) - 1)
        def _():
            def merge(i, carry):
                r = pl.ds(pl.multiple_of(i * rows, rows), rows)
                ls = [lse_s[k, r, :] for k in range(3)]
                top = jnp.maximum(jnp.maximum(ls[0], ls[1]), ls[2])
                ws = [jnp.exp(t - top) for t in ls]
                den = ws[0] + ws[1] + ws[2]
                o_ref[r, :] = (ws[0] * o_s[0, r, :] + ws[1] * o_s[1, r, :] + ws[2] * o_s[2, r, :]) / den
                lse_ref[r, :] = top + jnp.log(den)
                return carry

            lax.fori_loop(0, S // rows, merge, 0)

    return pl.pallas_call(
        body, name="attn_fwd", grid=(N_SLOT, 3),
        in_specs=[pl.BlockSpec((1, 3, S, HEAD), lambda j, g: (g, 0, 0, j)),
                  pl.BlockSpec((1, 1, HEAD, 2 * HEAD), lambda j, g: (g, j, 0, 0))],
        out_specs=[pl.BlockSpec((S, HEAD), lambda j, g: (0, j)), pl.BlockSpec((S, HEAD), lambda j, g: (0, j))],
        out_shape=[jax.ShapeDtypeStruct((S, AOW), F32), jax.ShapeDtypeStruct((S, AOW), F32)],
        scratch_shapes=[pltpu.VMEM((3, S, HEAD), F32)] * 2,
        compiler_params=_cparams("parallel", "arbitrary"),
    )(qkv, bias)


def _shift_down(z, k, halo_rows):
    out = pltpu.roll(z, k, axis=0)
    top = out[:8]
    rid = lax.broadcasted_iota(jnp.int32, top.shape, 0)
    for t in range(k):
        top = jnp.where(rid == t, halo_rows[t], top)
    return jnp.concatenate([top, out[8:]], axis=0)


def _shift_up(z, k, halo_rows):
    n = z.shape[0]
    out = pltpu.roll(z, n - k, axis=0)
    bottom = out[n - 8:]
    rid = lax.broadcasted_iota(jnp.int32, bottom.shape, 0)
    for t in range(k):
        bottom = jnp.where(rid == 8 - k + t, halo_rows[t], bottom)
    return jnp.concatenate([out[:n - 8], bottom], axis=0)


def _e_spec(chunk, tm):
    return pl.BlockSpec((1, tm, D), lambda i, c=chunk: (c, i, 0))


def _e_prev_spec(chunk, tm):
    return pl.BlockSpec((1, 16, D), lambda i, c=chunk: (c, jnp.maximum(i * (tm // 16) - 1, 0), 0))


def _e_next_spec(chunk, tm, S):
    return pl.BlockSpec((1, 16, D), lambda i, c=chunk: (c, jnp.minimum((i + 1) * (tm // 16), S // 16 - 1), 0))


def _mix(o_attn, e, cw8, ba, bb, w_bat, w_bc):
    S = o_attn.shape[0]
    tm = 512

    def body(o_ref, cb_ref, cc_ref, cx_ref, ga_ref, gb_ref, ccp_ref, cxp_ref, cw_ref, ba_ref, bb_ref, wba_ref, wbc_ref,
             obf_ref, cbu_ref, ya_ref, yc_ref, mg_ref):
        i = pl.program_id(0)
        o = o_ref[...].astype(BF16)
        obf_ref[...] = o
        ya = _nt(o, wba_ref[...])
        z = cc_ref[0].astype(F32) * cx_ref[0].astype(F32)
        zp = ccp_ref[0].astype(F32) * cxp_ref[0].astype(F32) * (i > 0).astype(F32)
        z1 = _shift_down(z, 1, [zp[15:16]])
        z2 = _shift_down(z, 2, [zp[14:15], zp[15:16]])
        cw = cw_ref[...]
        u = cw[0:1] * z2 + cw[1:2] * z1 + cw[2:3] * z
        cbu = (cb_ref[0].astype(F32) * u).astype(BF16)
        cbu_ref[...] = cbu
        yc = _nn(cbu, wbc_ref[...])
        sa = _sigmoid(ga_ref[0].astype(F32) + ba_ref[...])
        sb = _sigmoid(gb_ref[0].astype(F32) + bb_ref[...])
        ya_ref[...] = ya.astype(BF16)
        yc_ref[...] = yc.astype(BF16)
        mg_ref[...] = (sa * ya + sb * yc).astype(BF16)

    row = lambda w: pl.BlockSpec((tm, w), lambda i: (i, 0))
    return pl.pallas_call(
        body, name="mix", grid=(S // tm,),
        in_specs=[row(AOW)] + [_e_spec(c, tm) for c in range(5)] + [_e_prev_spec(1, tm), _e_prev_spec(2, tm),
                  _const_spec((8, D)), _vec_spec(), _vec_spec(), _const_spec((D, AOW)), _const_spec((D, D))],
        out_specs=[row(AOW), row(D), row(D), row(D), row(D)],
        out_shape=[jax.ShapeDtypeStruct((S, AOW), BF16)] + [jax.ShapeDtypeStruct((S, D), BF16)] * 4,
        compiler_params=_cparams("parallel"),
    )(o_attn, e, e, e, e, e, e, e, cw8, ba, bb, w_bat, w_bc)


def _out_proj(merged, w_out, x, gate1, g_mlp, sc2, sh2):
    S = x.shape[0]
    tm = TM

    def body(mg_ref, w_ref, x_ref, gt_ref, g_ref, sc_ref, sh_ref, x1_ref, mo_ref, h2_ref):
        mo = _nn(mg_ref[...], w_ref[...])
        mo_ref[...] = mo.astype(BF16)
        x1 = x_ref[...] + gt_ref[...] * mo
        x1_ref[...] = x1
        h2 = x1 * _rms_r(x1) * g_ref[...] * (1.0 + sc_ref[...]) + sh_ref[...]
        h2_ref[...] = h2.astype(BF16)

    row = pl.BlockSpec((tm, D), lambda i: (i, 0))
    return pl.pallas_call(
        body, name="out_proj", grid=(S // tm,),
        in_specs=[row, _const_spec((D, D)), row, _vec_spec(), _vec_spec(), _vec_spec(), _vec_spec()],
        out_specs=[row, row, row],
        out_shape=[jax.ShapeDtypeStruct((S, D), F32), jax.ShapeDtypeStruct((S, D), BF16), jax.ShapeDtypeStruct((S, D), BF16)],
        compiler_params=_cparams("parallel"),
    )(merged, w_out, x, gate1, g_mlp, sc2, sh2)


def _mlp_in(h2, w_mit):
    S = h2.shape[0]
    tm, tn = TM, 2048

    def body(h_ref, w_ref, a_ref, f_ref):
        a = _nt(h_ref[...], w_ref[...])
        a_ref[...] = a.astype(BF16)
        f_ref[...] = jnp.square(jnp.maximum(a, 0.0)).astype(BF16)

    blk = pl.BlockSpec((tm, tn), lambda i, j: (i, j))
    return pl.pallas_call(
        body, name="mlp_in", grid=(S // tm, DFF // tn),
        in_specs=[pl.BlockSpec((tm, D), lambda i, j: (i, 0)), pl.BlockSpec((tn, D), lambda i, j: (j, 0))],
        out_specs=[blk, blk],
        out_shape=[jax.ShapeDtypeStruct((S, DFF), BF16)] * 2,
        compiler_params=_cparams("parallel", "parallel"),
    )(h2, w_mit)


def _mlp_out(f, w_mo, x1, gate2, g_fin, tgt):
    S = x1.shape[0]
    tm = 512
    half = tm // 2

    def body(f_ref, w_ref, x1_ref, gt_ref, g_ref, t_ref, mlp_ref, dx2_ref, pv_ref):
        @pl.when(pl.program_id(0) == 0)
        def _():
            pv_ref[...] = jnp.zeros_like(pv_ref)

        g = g_ref[...]
        for hs in (pl.ds(0, half), pl.ds(half, half)):
            mlp = _nn(f_ref[hs, :], w_ref[...])
            mlp_ref[hs, :] = mlp.astype(BF16)
            x2 = x1_ref[hs, :] + gt_ref[...] * mlp
            r = _rms_r(x2)
            err = x2 * r * g - t_ref[hs, :]
            dx2, pg = _rms_bwd(x2, r, g, err * (1.0 / D))
            dx2_ref[hs, :] = dx2
            pv_ref[0:1, :] += _rowsum(pg)
            pv_ref[1:2, :] += 0.5 * _rowsum(jnp.mean(err * err, axis=-1, keepdims=True))

    row = pl.BlockSpec((tm, D), lambda i: (i, 0))
    return pl.pallas_call(
        body, name="mlp_out", grid=(S // tm,),
        in_specs=[pl.BlockSpec((tm, DFF), lambda i: (i, 0)), _const_spec((DFF, D)), row, _vec_spec(), _vec_spec(), row],
        out_specs=[row, row, _const_spec((8, D))],
        out_shape=[jax.ShapeDtypeStruct((S, D), BF16), jax.ShapeDtypeStruct((S, D), F32), jax.ShapeDtypeStruct((8, D), F32)],
        compiler_params=_cparams("arbitrary"),
    )(f, w_mo, x1, gate2, g_fin, tgt)


def _bwd_mlp_a(dx2, gate2, mlp, w_mo, a):
    S = dx2.shape[0]
    tm = 512
    half = tm // 2

    def body(dx_ref, gt_ref, mlp_ref, w_ref, a_ref, da_ref, dmo_ref, pv_ref):
        @pl.when(pl.program_id(0) == 0)
        def _():
            pv_ref[...] = jnp.zeros_like(pv_ref)

        for hs in (pl.ds(0, half), pl.ds(half, half)):
            dx = dx_ref[hs, :]
            dmo = (dx * gt_ref[...]).astype(BF16)
            dmo_ref[hs, :] = dmo
            pv_ref[0:1, :] += _rowsum(dx * mlp_ref[hs, :].astype(F32))
            df = _nt(dmo, w_ref[...])
            da_ref[hs, :] = (df * (2.0 * jnp.maximum(a_ref[hs, :].astype(F32), 0.0))).astype(BF16)

    row = pl.BlockSpec((tm, D), lambda i: (i, 0))
    wide = pl.BlockSpec((tm, DFF), lambda i: (i, 0))
    return pl.pallas_call(
        body, name="bwd_mlp_a", grid=(S // tm,),
        in_specs=[row, _vec_spec(), row, _const_spec((DFF, D)), wide],
        out_specs=[wide, row, _const_spec((8, D))],
        out_shape=[jax.ShapeDtypeStruct((S, DFF), BF16), jax.ShapeDtypeStruct((S, D), BF16), jax.ShapeDtypeStruct((8, D), F32)],
        compiler_params=_cparams("arbitrary"),
    )(dx2, gate2, mlp, w_mo, a)


def _bwd_mlp_b(da, w_mit, x1, dx2, g_mlp, sc2):
    S = x1.shape[0]
    tm = 512
    half = tm // 2

    def body(da_ref, w_ref, x1_ref, dx2_ref, g_ref, sc_ref, dx1_ref, pv_ref):
        @pl.when(pl.program_id(0) == 0)
        def _():
            pv_ref[...] = jnp.zeros_like(pv_ref)

        g = g_ref[...]
        for hs in (pl.ds(0, half), pl.ds(half, half)):
            dh = _nn(da_ref[hs, :], w_ref[...])
            x1 = x1_ref[hs, :]
            r = _rms_r(x1)
            dxn, pg = _rms_bwd(x1, r, g, dh * (1.0 + sc_ref[...]))
            dx1_ref[hs, :] = dx2_ref[hs, :] + dxn
            pv_ref[0:1, :] += _rowsum(dh)
            pv_ref[1:2, :] += _rowsum(dh * (x1 * r * g))
            pv_ref[2:3, :] += _rowsum(pg)

    row = pl.BlockSpec((tm, D), lambda i: (i, 0))
    return pl.pallas_call(
        body, name="bwd_mlp_b", grid=(S // tm,),
        in_specs=[pl.BlockSpec((tm, DFF), lambda i: (i, 0)), _const_spec((DFF, D)), row, row, _vec_spec(), _vec_spec()],
        out_specs=[row, _const_spec((8, D))],
        out_shape=[jax.ShapeDtypeStruct((S, D), F32), jax.ShapeDtypeStruct((8, D), F32)],
        compiler_params=_cparams("arbitrary"),
    )(da, w_mit, x1, dx2, g_mlp, sc2)


def _bwd_mix(dx1, gate1, mo, e, cw8, ba, bb, ya, yc, o_attn, w_out, w_bc, w_bat):
    S = dx1.shape[0]
    tm = 256
    n_tiles = S // tm

    def body(dx_ref, dxn_ref, gt_ref, mo_ref, cb_ref, cc_ref, cx_ref, ga_ref, gb_ref, cbn_ref, gbn_ref, ccp_ref, cxp_ref,
             cw_ref, ba_ref, bb_ref, ya_ref, yc_ref, o_ref, wout_ref, wbc_ref, wba_ref,
             dmo_ref, dya_ref, dyc_ref, do_ref, dl_ref, de_ref, pv_ref):
        i = pl.program_id(0)

        @pl.when(i == 0)
        def _():
            pv_ref[...] = jnp.zeros_like(pv_ref)

        dx = dx_ref[...]
        cb = cb_ref[0].astype(F32)
        cc = cc_ref[0].astype(F32)
        cx = cx_ref[0].astype(F32)
        dmo_all = (jnp.concatenate([dx, dxn_ref[...]], axis=0) * gt_ref[...]).astype(BF16)
        dmg_all = _nt(dmo_all, wout_ref[...])
        sb_all = _sigmoid(jnp.concatenate([gb_ref[0], gbn_ref[0]], axis=0).astype(F32) + bb_ref[...])
        dyc_all = dmg_all * sb_all
        dcbu_all = _nt(dyc_all.astype(BF16), wbc_ref[...])
        dmo, dmg, sb, dyc, dcbu = dmo_all[:tm], dmg_all[:tm], sb_all[:tm], dyc_all[:tm], dcbu_all[:tm]
        dmo_ref[...] = dmo
        pv_ref[0:1, :] += _rowsum(dx * mo_ref[...].astype(F32))
        sa = _sigmoid(ga_ref[0].astype(F32) + ba_ref[...])
        dya = (dmg * sa).astype(BF16)
        dya_ref[...] = dya
        dyc_ref[...] = dyc.astype(BF16)
        dga = dmg * ya_ref[...].astype(F32) * sa * (1.0 - sa)
        dgb = dmg * yc_ref[...].astype(F32) * sb * (1.0 - sb)
        pv_ref[1:2, :] += _rowsum(dga)
        pv_ref[2:3, :] += _rowsum(dgb)

        do = _nn(dya, wba_ref[...])
        do_ref[...] = do
        prod = do * o_ref[...]
        dl_ref[...] = jnp.concatenate(
            [jnp.broadcast_to(jnp.sum(prod[:, s * HEAD:(s + 1) * HEAD], axis=-1, keepdims=True), (tm, HEAD))
             for s in range(N_SLOT)], axis=1)

        z = cc * cx
        zp = ccp_ref[0].astype(F32) * cxp_ref[0].astype(F32) * (i > 0).astype(F32)
        z1 = _shift_down(z, 1, [zp[15:16]])
        z2 = _shift_down(z, 2, [zp[14:15], zp[15:16]])
        cw = cw_ref[...]
        u = cw[0:1] * z2 + cw[1:2] * z1 + cw[2:3] * z
        du = dcbu * cb
        du_n = dcbu_all[tm:] * cbn_ref[0].astype(F32) * (i < n_tiles - 1).astype(F32)
        du1 = _shift_up(du, 1, [du_n[0:1]])
        du2 = _shift_up(du, 2, [du_n[0:1], du_n[1:2]])
        dz = cw[2:3] * du + cw[1:2] * du1 + cw[0:1] * du2
        pv_ref[3:4, :] += _rowsum(du * z2)
        pv_ref[4:5, :] += _rowsum(du * z1)
        pv_ref[5:6, :] += _rowsum(du * z)

        de_ref[0] = (dcbu * u).astype(BF16)
        de_ref[1] = (dz * cx).astype(BF16)
        de_ref[2] = (dz * cc).astype(BF16)
        de_ref[3] = dga.astype(BF16)
        de_ref[4] = dgb.astype(BF16)

    row = lambda w: pl.BlockSpec((tm, w), lambda i: (i, 0))
    nxt = pl.BlockSpec((16, D), lambda i: (jnp.minimum((i + 1) * (tm // 16), S // 16 - 1), 0))
    return pl.pallas_call(
        body, name="bwd_mix", grid=(n_tiles,),
        in_specs=[row(D), nxt, _vec_spec(), row(D)] + [_e_spec(c, tm) for c in range(5)]
                 + [_e_next_spec(0, tm, S), _e_next_spec(4, tm, S), _e_prev_spec(1, tm), _e_prev_spec(2, tm),
                    _const_spec((8, D)), _vec_spec(), _vec_spec(), row(D), row(D), row(AOW),
                    _const_spec((D, D)), _const_spec((D, D)), _const_spec((D, AOW))],
        out_specs=[row(D), row(D), row(D), row(AOW), row(AOW), pl.BlockSpec((5, tm, D), lambda i: (0, i, 0)),
                   _const_spec((8, D))],
        out_shape=[jax.ShapeDtypeStruct((S, D), BF16)] * 3 + [jax.ShapeDtypeStruct((S, AOW), F32)] * 2
                  + [jax.ShapeDtypeStruct((5, S, D), BF16), jax.ShapeDtypeStruct((8, D), F32)],
        compiler_params=_cparams("arbitrary"),
    )(dx1, dx1, gate1, mo, e, e, e, e, e, e, e, e, e, cw8, ba, bb, ya, yc, o_attn, w_out, w_bc, w_bat)


def _attn_bwd(qkv, do, lse, dl, bias_t):
    S = qkv.shape[2]
    nblk = S // HEAD

    def body(qkv_ref, do_ref, lse_ref, dl_ref, b_ref, d_ref):
        g = pl.program_id(1)
        bias = b_ref[0, 0]
        col = lax.broadcasted_iota(jnp.int32, bias.shape, 1)
        bias_last = jnp.where(col >= HEAD, NEG, bias)
        eye = (lax.broadcasted_iota(jnp.int32, (HEAD, HEAD), 0) == lax.broadcasted_iota(jnp.int32, (HEAD, HEAD), 1)).astype(F32)

        def as_row(t):
            return jnp.sum(t * eye, axis=0, keepdims=True)

        for gi, d in enumerate(DILATIONS):
            @pl.when(g == gi)
            def _(d=d):
                nb = nblk // d

                def query_side(start):
                    sl = pl.ds(start, HEAD, stride=d)
                    return (qkv_ref.at[0, 0][sl, :].astype(BF16), do_ref[sl, :].astype(BF16),
                            as_row(lse_ref[sl, :]), as_row(dl_ref[sl, :]))

                def step(b, first_of_residue, carry):
                    dq_part, own = carry
                    r, n = b // nb, b % nb
                    cur = pl.ds(n * (HEAD * d) + r, HEAD, stride=d)
                    if first_of_residue:
                        own = query_side(r)
                    nxt = query_side(jnp.minimum(n + 1, nb - 1) * (HEAD * d) + r)
                    q2 = jnp.concatenate([own[0], nxt[0]], axis=0)
                    do2 = jnp.concatenate([own[1], nxt[1]], axis=0)
                    k = qkv_ref.at[0, 1][cur, :].astype(BF16)
                    v = qkv_ref.at[0, 2][cur, :].astype(BF16)
                    s = _nt(k, q2) * SCALE + jnp.where(n < nb - 1, bias, bias_last)
                    p = jnp.exp(s - jnp.concatenate([own[2], nxt[2]], axis=1))
                    d_ref.at[0, 2][cur, :] = _nn(p.astype(BF16), do2)
                    dp = _nt(v, do2)
                    ds = (p * (dp - jnp.concatenate([own[3], nxt[3]], axis=1)) * SCALE).astype(BF16)
                    d_ref.at[0, 1][cur, :] = _nn(ds, q2)
                    dq2 = _tn(ds, k)
                    d_ref.at[0, 0][cur, :] = dq2[:HEAD] + jnp.where(n > 0, dq_part, 0.0)
                    return dq2[HEAD:], nxt

                def steps(i, carry):
                    for u in range(UNROLL):
                        carry = step(i * UNROLL + u, nb <= UNROLL and u % nb == 0, carry)
                    return carry

                lax.fori_loop(0, nblk // UNROLL, steps, (jnp.zeros((HEAD, HEAD), F32), query_side(0)))

    col_blk = pl.BlockSpec((S, HEAD), lambda j, g: (0, j))
    qkv_blk = pl.BlockSpec((1, 3, S, HEAD), lambda j, g: (g, 0, 0, j))
    return pl.pallas_call(
        body, name="attn_bwd", grid=(N_SLOT, 3),
        in_specs=[qkv_blk, col_blk, col_blk, col_blk, pl.BlockSpec((1, 1, HEAD, 2 * HEAD), lambda j, g: (g, j, 0, 0))],
        out_specs=qkv_blk,
        out_shape=jax.ShapeDtypeStruct((3, 3, S, AOW), F32),
        compiler_params=_cparams("parallel", "arbitrary"),
    )(qkv, do, lse, dl, bias_t)


def _bwd_in(dqkv, de, w_int, x, dx1, g_mix, sc1):
    S = x.shape[0]
    tm = TM
    dqkv = dqkv.reshape(3, 3, S, AOW)

    def body(dq_ref, de_ref, wq_ref, wk_ref, wv_ref, wa_ref, wb_ref, x_ref, dx1_ref, g_ref, sc_ref, gx_ref, pv_ref):
        acc = gx_ref
        i, k = pl.program_id(0), pl.program_id(1)

        @pl.when((i == 0) & (k == 0))
        def _():
            pv_ref[...] = jnp.zeros_like(pv_ref)

        @pl.when(k == 0)
        def _():
            acc[...] = jnp.zeros_like(acc)

        @pl.when(k < 3)
        def _():
            lhs = jnp.concatenate([dq_ref[0, t].astype(BF16) for t in range(3)], axis=1)
            acc[...] += _nn(lhs, jnp.concatenate([wq_ref[...], wk_ref[...], wv_ref[...]], axis=0))

        @pl.when(k >= 3)
        def _():
            acc[...] += _nn(de_ref[0], jnp.concatenate([wa_ref[...], wb_ref[...]], axis=0))

        @pl.when(k == 7)
        def _():
            dh = acc[...]
            xv = x_ref[...]
            r = _rms_r(xv)
            g = g_ref[...]
            dxn, pg = _rms_bwd(xv, r, g, dh * (1.0 + sc_ref[...]))
            gx_ref[...] = dx1_ref[...] + dxn
            pv_ref[0:1, :] += _rowsum(dh)
            pv_ref[1:2, :] += _rowsum(dh * (xv * r * g))
            pv_ref[2:3, :] += _rowsum(pg)

    grp = lambda k: jnp.minimum(k, 2)
    chunk = lambda k: jnp.maximum(k - 3, 0)
    wblk = lambda f: pl.BlockSpec((512, D), lambda i, k: (f(k), 0))
    row = pl.BlockSpec((tm, D), lambda i, k: (i, 0))
    once = pl.BlockSpec((tm, D), lambda i, k: (i, 0), pipeline_mode=pl.Buffered(1))
    return pl.pallas_call(
        body, name="bwd_in", grid=(S // tm, 8),
        in_specs=[pl.BlockSpec((1, 3, tm, 512), lambda i, k: (grp(k), 0, i, 0)),
                  pl.BlockSpec((1, tm, D), lambda i, k: (chunk(k), i, 0)),
                  wblk(grp), wblk(lambda k: 3 + grp(k)), wblk(lambda k: 6 + grp(k)),
                  wblk(lambda k: 9 + 2 * chunk(k)), wblk(lambda k: 10 + 2 * chunk(k)),
                  once, once, _vec_spec(), _vec_spec()],
        out_specs=[row, _const_spec((8, D))],
        out_shape=[jax.ShapeDtypeStruct((S, D), F32), jax.ShapeDtypeStruct((8, D), F32)],
        compiler_params=_cparams("arbitrary", "arbitrary"),
    )(dqkv, de, w_int, w_int, w_int, w_int, w_int, x, dx1, g_mix, sc1)


def _grad_w(name, a, b):
    S, ka = a.shape
    nb = b.shape[1]

    def body(a_ref, b_ref, o_ref):
        o_ref[...] = _tn(a_ref[...], b_ref[...]).astype(BF16)

    return pl.pallas_call(
        body, name=name, grid=(ka // 512,),
        in_specs=[pl.BlockSpec((S, 512), lambda n: (0, n)), pl.BlockSpec((S, nb), lambda n: (0, 0))],
        out_specs=pl.BlockSpec((512, nb), lambda n: (n, 0)),
        out_shape=jax.ShapeDtypeStruct((ka, nb), BF16),
        compiler_params=_cparams("parallel"),
    )(a, b)


def _grad_w_small(dya, o_bf, cbu, dyc, merged, dmo, after):
    S = dya.shape[0]

    def body(dya_ref, o_ref, cbu_ref, dyc_ref, mg_ref, dmo_ref, after_ref, gba_ref, gbc_ref, gout_ref):
        gba_ref[...] = _tn(dya_ref[...], o_ref[...]).astype(BF16)
        gbc_ref[...] = _tn(cbu_ref[...], dyc_ref[...]).astype(BF16)
        gout_ref[...] = _tn(mg_ref[...], dmo_ref[...]).astype(BF16)

    a_blk = pl.BlockSpec((S, 512), lambda n: (0, n))
    whole = lambda w: pl.BlockSpec((S, w), lambda n: (0, 0))
    out = lambda w: pl.BlockSpec((512, w), lambda n: (n, 0))
    return pl.pallas_call(
        body, name="grad_w_small", grid=(D // 512,),
        in_specs=[a_blk, whole(AOW), a_blk, whole(D), a_blk, whole(D), pl.BlockSpec(memory_space=pl.ANY)],
        out_specs=[out(AOW), out(D), out(D)],
        out_shape=[jax.ShapeDtypeStruct((D, AOW), BF16), jax.ShapeDtypeStruct((D, D), BF16), jax.ShapeDtypeStruct((D, D), BF16)],
        compiler_params=_cparams("parallel"),
    )(dya, o_bf, cbu, dyc, merged, dmo, after)


def _grad_w_in(dqkv, de, h):
    S = h.shape[0]

    def body(dq_ref, de_ref, h_ref, o_ref):
        n = pl.program_id(0)

        @pl.when(n < 9)
        def _():
            o_ref[...] = _tn(dq_ref[0].astype(BF16), h_ref[...]).astype(BF16)

        @pl.when(n >= 9)
        def _():
            o_ref[...] = _tn(de_ref[0], h_ref[...]).astype(BF16)

    def e_idx(n):
        kk = jnp.maximum(n - 9, 0)
        return (kk // 2, 0, kk % 2)

    return pl.pallas_call(
        body, name="grad_w_in", grid=(19,),
        in_specs=[pl.BlockSpec((1, S, 512), lambda n: (jnp.minimum(n, 8), 0, 0)), pl.BlockSpec((1, S, 512), e_idx),
                  pl.BlockSpec((S, D), lambda n: (0, 0))],
        out_specs=pl.BlockSpec((512, D), lambda n: (_win_rowblock(n), 0)),
        out_shape=jax.ShapeDtypeStruct((19 * 512, D), BF16),
        compiler_params=_cparams("parallel"),
    )(dqkv, de, h)


def _local_step(x, h, tgt, mod, g_mix, g_mlp, g_fin, ba, bb, cw8, w_int, mix_weights, mlp_weights, mlp_grads_ready, w_in_grad_ready,
                other_grads_ready):
    S = x.shape[0]
    sh1, sc1, gt1, sh2, sc2, gt2 = [mod[k:k + 1] for k in range(6)]
    bias, bias_t = _bias_table()

    qkv, e = _proj(h, w_int)
    qkv = qkv.reshape(3, 3, S, AOW)
    o_attn, lse = _attn_fwd(qkv, bias)
    w_bat, w_bc, w_out = mix_weights(o_attn)
    o_bf, cbu, ya, yc, merged = _mix(o_attn, e, cw8, ba, bb, w_bat, w_bc)
    x1, mo, h2 = _out_proj(merged, w_out, x, gt1, g_mlp, sc2, sh2)
    w_mit, w_mo = mlp_weights(x1)
    a, f = _mlp_in(h2, w_mit)
    mlp, dx2, pv_f = _mlp_out(f, w_mo, x1, gt2, g_fin, tgt)

    da, dmo2, pv_a = _bwd_mlp_a(dx2, gt2, mlp, w_mo, a)
    dx1, pv_b = _bwd_mlp_b(da, w_mit, x1, dx2, g_mlp, sc2)
    zero = mlp_grads_ready(_grad_w("grad_w_mi", da, h2), _grad_w("grad_w_mo", f, dmo2))
    dmo, dya, dyc, do, dl, de, pv_m = _bwd_mix(dx1, gt1 + zero, mo, e, cw8, ba, bb, ya, yc, o_attn, w_out, w_bc, w_bat)
    dqkv = _attn_bwd(qkv, do, lse, dl, bias_t).reshape(9, S, AOW)
    after = w_in_grad_ready(_grad_w_in(dqkv, de, h))
    zero = other_grads_ready(*_grad_w_small(dya, o_bf, cbu, dyc, merged, dmo, after))
    grad_x, pv_i = _bwd_in(dqkv, de, w_int, x, dx1, g_mix, sc1 + zero)

    vec = jnp.concatenate([pv_i[0:2], pv_m[0:1], pv_b[0:2], pv_a[0:1], pv_i[2:3], pv_b[2:3], pv_f[0:1],
                           pv_m[1:3], pv_m[3:6], pv_f[1:2], jnp.zeros((1, D), F32)], axis=0)
    return grad_x, vec


def _my_place():
    return lax.axis_index("x"), lax.axis_index("y"), lax.axis_index("c")


def _dev_index(px, py, pc):
    return 4 * px + 2 * py + pc


def _peer(x, y, c, m):
    return (x ^ ((m >> 2) & 1), y ^ ((m >> 1) & 1), c ^ (m & 1))


HBM_SPEC = pl.BlockSpec(memory_space=pltpu.HBM)
SEM_SPEC = pl.BlockSpec(memory_space=pltpu.SEMAPHORE)
N_PEER = N_DEV - 1


SPLIT_MASKS = {"gather": tuple(range(1, N_DEV)), "scatter": tuple(range(1, N_DEV)), "chips": (2, 4, 6), "sibling": (1, 1, 1, 1)}


def _split_copy(mode, src_ref, land_ref, send_sems, recv_sems, w, j, place, arriving=False):
    x, y, c = place
    masks = SPLIT_MASKS[mode]
    peer = _peer(x, y, c, masks[j])
    k = w * len(masks) + j
    sender, receiver = ((peer, (x, y, c)) if arriving else ((x, y, c), peer))
    if mode == "gather":
        r = src_ref.shape[0]
        src, dst = src_ref, land_ref.at[pl.ds(pl.multiple_of(_dev_index(*sender) * r, 16), r), :]
    elif mode == "scatter":
        r = land_ref.shape[1]
        src, dst = src_ref.at[pl.ds(pl.multiple_of(_dev_index(*receiver) * r, 16), r), :], land_ref.at[j]
    elif mode == "chips":
        src, dst = src_ref.at[2 * receiver[0] + receiver[1]], land_ref.at[j]
    else:
        r = land_ref.shape[1]
        src, dst = src_ref.at[pl.ds(pl.multiple_of((2 * j + receiver[2]) * r, 16), r), :], land_ref.at[j]
    return pltpu.make_async_remote_copy(src_ref=src, dst_ref=dst, send_sem=send_sems.at[k], recv_sem=recv_sems.at[k],
                                        device_id=peer, device_id_type=MESH)


def _split_start(name, mode, srcs, lands):
    n = len(srcs)
    nm = len(SPLIT_MASKS[mode])

    def body(*refs):
        src, land = refs[:n], refs[n:2 * n]
        send_sems, recv_sems = refs[2 * n], refs[2 * n + 1]
        token = refs[-1]
        place = _my_place()
        for w in range(n):
            for j in range(nm):
                _split_copy(mode, src[w], land[w], send_sems, recv_sems, w, j, place).start()
        token[...] = jnp.zeros_like(token)

    hbm = lambda t: pltpu.HBM(t.shape, t.dtype)
    out = pl.pallas_call(
        body, name=name,
        out_shape=(pltpu.SemaphoreType.DMA((n * nm,)), pltpu.SemaphoreType.DMA((n * nm,)), *[hbm(t) for t in srcs],
                   *[hbm(t) for t in lands], jax.ShapeDtypeStruct((8, 128), F32)),
        in_specs=(HBM_SPEC,) * (2 * n),
        out_specs=(SEM_SPEC, SEM_SPEC) + (HBM_SPEC,) * (2 * n) + (pl.BlockSpec(memory_space=pltpu.VMEM),),
        input_output_aliases={i: 2 + i for i in range(2 * n)},
        compiler_params=pltpu.CompilerParams(has_side_effects=pltpu.SideEffectType.DATAFLOW_SIDE_EFFECTING),
    )(*[pltpu.with_memory_space_constraint(t, pltpu.HBM) for t in (*srcs, *lands)])
    return out[0], out[1], out[2:2 + n], out[2 + n:2 + 2 * n], out[-1][0:1, 0:1], out[-1]


def _split_wait(name, mode, send_sems, recv_sems, srcs, lands, after):
    n = len(srcs)

    def body(*refs):
        src, land = refs[:n], refs[n:2 * n]
        ssem, rsem = refs[2 * n], refs[2 * n + 1]
        place = _my_place()
        for w in range(n):
            for j in range(len(SPLIT_MASKS[mode])):
                _split_copy(mode, src[w], land[w], ssem, rsem, w, j, place).wait_send()
                _split_copy(mode, src[w], land[w], ssem, rsem, w, j, place, arriving=True).wait_recv()

    hbm = lambda t: pltpu.HBM(t.shape, t.dtype)
    out = pl.pallas_call(
        body, name=name,
        out_shape=tuple(hbm(t) for t in (*srcs, *lands)),
        in_specs=(HBM_SPEC,) * (2 * n) + (SEM_SPEC, SEM_SPEC, pl.BlockSpec(memory_space=pl.ANY)),
        out_specs=(HBM_SPEC,) * (2 * n),
        input_output_aliases={i: i for i in range(2 * n)},
        compiler_params=pltpu.CompilerParams(has_side_effects=pltpu.SideEffectType.DATAFLOW_SIDE_EFFECTING),
    )(*srcs, *lands, send_sems, recv_sems, after)
    return out[:n], out[n:]


def _sibling_exchange(grads):
    nw = len(grads)
    HBM = pl.BlockSpec(memory_space=pl.ANY)

    def body(*refs):
        g, land = refs[:nw], refs[nw:2 * nw]
        send_sems, recv_sems = refs[2 * nw:]
        x, y, c = _my_place()

        def copy(w, q, owner_core):
            r = land[w].shape[1]
            return pltpu.make_async_remote_copy(
                src_ref=g[w].at[pl.ds(pl.multiple_of((2 * q + owner_core) * r, 16), r), :], dst_ref=land[w].at[q],
                send_sem=send_sems.at[w, q], recv_sem=recv_sems.at[w, q], device_id=(x, y, 1 - c), device_id_type=MESH)

        sends = [copy(w, q, 1 - c) for w in range(nw) for q in range(4)]
        for cp in sends:
            cp.start()
        for w in range(nw):
            for q in range(4):
                copy(w, q, c).wait_recv()
        for cp in sends:
            cp.wait_send()

    return pl.pallas_call(
        body, name="sibling_exchange",
        out_shape=[jax.ShapeDtypeStruct((4, a.shape[0] // N_DEV, a.shape[1]), a.dtype) for a in grads],
        in_specs=[HBM] * nw, out_specs=[HBM] * nw,
        scratch_shapes=[pltpu.SemaphoreType.DMA((nw, 4)), pltpu.SemaphoreType.DMA((nw, 4))],
    )(*grads)


def _pair_sums(gs, sibs, core):
    n = len(gs)

    def body(core_ref, *refs):
        for w in range(n):
            refs[2 * n + w][0] = (refs[w][0, 0].astype(F32) + refs[n + w][0].astype(F32)).astype(BF16)

    in_specs = [pl.BlockSpec((1, 1) + t.shape[1:], lambda q, core_ref: (q, core_ref[0], 0, 0)) for t in sibs]
    in_specs += [pl.BlockSpec((1,) + t.shape[1:], lambda q, core_ref: (q, 0, 0)) for t in sibs]
    return pl.pallas_call(
        body, name="pair_sums",
        grid_spec=pltpu.PrefetchScalarGridSpec(
            num_scalar_prefetch=1, grid=(4,), in_specs=in_specs,
            out_specs=[pl.BlockSpec((1,) + t.shape[1:], lambda q, core_ref: (q, 0, 0)) for t in sibs]),
        out_shape=[jax.ShapeDtypeStruct(t.shape, BF16) for t in sibs],
        compiler_params=_cparams("parallel"),
    )(core, *[g.reshape(4, 2, t.shape[1], t.shape[2]) for g, t in zip(gs, sibs)], *sibs)


def _own_rows_into_zones(shards, me):
    n = len(shards)

    def body(me_ref, *refs):
        for w in range(n):
            refs[2 * n + w][...] = refs[w][...]

    zones = [lax.empty((N_DEV * t.shape[0], t.shape[1]), t.dtype) for t in shards]
    return pl.pallas_call(
        body, name="own_rows_into_zones",
        grid_spec=pltpu.PrefetchScalarGridSpec(
            num_scalar_prefetch=1, grid=(1,),
            in_specs=[pl.BlockSpec(t.shape, lambda i, me_ref: (0, 0)) for t in shards] + [pl.BlockSpec(memory_space=pl.ANY)] * n,
            out_specs=[pl.BlockSpec(t.shape, lambda i, me_ref: (me_ref[0], 0)) for t in shards]),
        out_shape=[jax.ShapeDtypeStruct(z.shape, z.dtype) for z in zones],
        input_output_aliases={1 + n + w: w for w in range(n)},
        compiler_params=_cparams("arbitrary"),
    )(me, *shards, *zones)


def _allgather_small(v, name):
    r, ccols = v.shape

    def body(v_ref, out_ref, send_sems, recv_sems):
        x, y, c = _my_place()
        my_idx = _dev_index(x, y, c)
        out_ref[my_idx] = v_ref[...]

        def copy(m):
            peer = _peer(x, y, c, m)
            return pltpu.make_async_remote_copy(
                src_ref=v_ref, dst_ref=out_ref.at[my_idx],
                send_sem=send_sems.at[m - 1], recv_sem=recv_sems.at[m - 1], device_id=peer, device_id_type=MESH)

        def arrival(m):
            peer = _peer(x, y, c, m)
            return pltpu.make_async_remote_copy(
                src_ref=v_ref, dst_ref=out_ref.at[_dev_index(*peer)],
                send_sem=send_sems.at[m - 1], recv_sem=recv_sems.at[m - 1], device_id=peer, device_id_type=MESH)

        sends = [copy(m) for m in range(1, N_DEV)]
        for cp in sends:
            cp.start()
        for m in range(1, N_DEV):
            arrival(m).wait_recv()
        for cp in sends:
            cp.wait_send()

    return pl.pallas_call(
        body, name=name,
        out_shape=jax.ShapeDtypeStruct((N_DEV, r, ccols), v.dtype),
        in_specs=[pl.BlockSpec(memory_space=pltpu.VMEM)], out_specs=pl.BlockSpec(memory_space=pltpu.VMEM),
        scratch_shapes=[pltpu.SemaphoreType.DMA((7,)), pltpu.SemaphoreType.DMA((7,))],
    )(v)


def _gather_w_in_and_condition(shard, pay, w_ada, b_cols):
    r, ccols = shard.shape
    ncol = w_ada.shape[1]

    def body(sh_ref, pay_ref, w_ref, b_ref, full_ref, got_ref, act_ref, mod_ref, send_sems, recv_sems, small_send, small_recv, local_sem):
        x, y, c = _my_place()
        me, sibling = (x, y, c), (x, y, 1 - c)
        my_idx = _dev_index(x, y, c)
        chips = [(1 - x, y), (x, 1 - y), (1 - x, 1 - y)]

        def small(rnd, buf, m, arriving=False):
            peer = _peer(x, y, c, m)
            slot = _dev_index(*peer) if arriving else my_idx
            return pltpu.make_async_remote_copy(
                src_ref=buf.at[my_idx], dst_ref=buf.at[slot], send_sem=small_send.at[rnd, m - 1],
                recv_sem=small_recv.at[rnd, m - 1], device_id=peer, device_id_type=MESH)

        def rows(px, py, pc):
            return full_ref.at[pl.ds(pl.multiple_of(_dev_index(px, py, pc) * r, 16), r), :]

        def copy(k, block, to, src=None):
            return pltpu.make_async_remote_copy(
                src_ref=rows(*block) if src is None else src, dst_ref=rows(*block),
                send_sem=send_sems.at[k], recv_sem=recv_sems.at[k], device_id=to, device_id_type=MESH)

        got_ref[my_idx] = pay_ref[...]
        round1 = [small(0, got_ref, m) for m in range(1, N_DEV)]
        for cp in round1:
            cp.start()
        mine = pltpu.make_async_copy(sh_ref, rows(*me), local_sem)
        mine.start()
        first = [copy(0, me, sibling, src=sh_ref)] + [copy(1 + j, me, (*chip, c), src=sh_ref) for j, chip in enumerate(chips)]
        for cp in first:
            cp.start()

        for m in range(1, N_DEV):
            small(0, got_ref, m, arriving=True).wait_recv()
        cv = jnp.concatenate([got_ref[s, 0:1, :] for s in range(N_DEV)], axis=0)
        act = cv * _sigmoid(cv)
        act_ref[...] = act
        mod_ref[my_idx] = jnp.dot(act, w_ref[...], preferred_element_type=F32, precision=lax.Precision.HIGHEST) + b_ref[...]
        round2 = [small(1, mod_ref, m) for m in range(1, N_DEV)]
        for cp in round2:
            cp.start()

        passed = []
        for j, chip in enumerate(chips):
            copy(1 + j, (*chip, c), me).wait_recv()
            fwd = copy(4 + j, (*chip, c), sibling)
            fwd.start()
            passed.append(fwd)
        copy(0, sibling, me).wait_recv()
        for j, chip in enumerate(chips):
            copy(4 + j, (*chip, 1 - c), me).wait_recv()
        for m in range(1, N_DEV):
            small(1, mod_ref, m, arriving=True).wait_recv()
        for cp in first + passed + round1 + round2:
            cp.wait_send()
        mine.wait()

    anyspec = pl.BlockSpec(memory_space=pl.ANY)
    vmem = pl.BlockSpec(memory_space=pltpu.VMEM)
    return pl.pallas_call(
        body, name="gather_w_in_and_condition",
        out_shape=[jax.ShapeDtypeStruct((N_DEV * r, ccols), shard.dtype), jax.ShapeDtypeStruct((N_DEV, 8, D), F32),
                   jax.ShapeDtypeStruct((N_DEV, D), F32), jax.ShapeDtypeStruct((N_DEV, N_DEV, ncol), F32)],
        in_specs=[anyspec, vmem, vmem, vmem], out_specs=[anyspec, vmem, vmem, vmem],
        scratch_shapes=[pltpu.SemaphoreType.DMA((7,)), pltpu.SemaphoreType.DMA((7,)), pltpu.SemaphoreType.DMA((2, 7)),
                        pltpu.SemaphoreType.DMA((2, 7)), pltpu.SemaphoreType.DMA],
        compiler_params=_cparams(),
    )(shard, pay, w_ada, b_cols)


def _ada_bwd(act_t, gm_cols):
    def body(a_ref, g_ref, o_ref):
        o_ref[...] = jnp.dot(a_ref[...], g_ref[...], preferred_element_type=F32, precision=lax.Precision.HIGHEST)

    return pl.pallas_call(
        body, name="ada_bwd", out_shape=jax.ShapeDtypeStruct((D, gm_cols.shape[1]), F32), compiler_params=_cparams(),
    )(act_t, gm_cols)


def _row_tile(r):
    for t in (256, 304, 128, 64, 16):
        if r % t == 0:
            return t
    return r


def _sum_parts(parts, name):
    k, r, ccols = parts.shape
    tr = _row_tile(r)

    def body(p_ref, o_ref):
        acc = p_ref[0].astype(F32)
        for s in range(1, k):
            acc = acc + p_ref[s].astype(F32)
        o_ref[...] = acc

    blk = pl.BlockSpec((tr, ccols), lambda i: (i, 0))
    return pl.pallas_call(
        body, name=name, grid=(r // tr,), in_specs=[pl.BlockSpec((k, tr, ccols), lambda i: (0, i, 0))], out_specs=blk,
        out_shape=jax.ShapeDtypeStruct((r, ccols), F32), compiler_params=_cparams("parallel"),
    )(parts)


def _adamw(w, g, m, v, name):
    r, ccols = w.shape
    tr = _row_tile(r)
    c1 = 1.0 / (1.0 - B1 ** STEP)
    c2 = 1.0 / (1.0 - B2 ** STEP)

    def body(w_ref, g_ref, m_ref, v_ref, d_ref, nm_ref, nv_ref):
        gv = g_ref[...]
        nm = B1 * m_ref[...] + (1.0 - B1) * gv
        nv = B2 * v_ref[...] + (1.0 - B2) * jnp.square(gv)
        nm_ref[...] = nm
        nv_ref[...] = nv
        d_ref[...] = -LR * ((nm * c1) / (jnp.sqrt(nv * c2) + ADAM_EPS) + WD * w_ref[...])

    blk = pl.BlockSpec((tr, ccols), lambda i: (i, 0))
    return pl.pallas_call(
        body, name=name, grid=(r // tr,), in_specs=[blk] * 4, out_specs=[blk] * 3,
        out_shape=[jax.ShapeDtypeStruct((r, ccols), F32)] * 3,
        compiler_params=_cparams("parallel"),
    )(w, g, m, v)


def _sum_adamw(parts, own, slot, w, m, v, name, transposed=False):
    k, r, ccols = parts.shape
    tr = _row_tile(r)
    c1 = 1.0 / (1.0 - B1 ** STEP)
    c2 = 1.0 / (1.0 - B2 ** STEP)

    def body(s_ref, p_ref, own_ref, w_ref, m_ref, v_ref, g_ref, d_ref, nm_ref, nv_ref):
        gv = own_ref[0].astype(F32)
        for s in range(k):
            gv = gv + p_ref[s].astype(F32)
        if transposed:
            gv = gv.T
        g_ref[...] = gv
        nm = B1 * m_ref[...] + (1.0 - B1) * gv
        nv = B2 * v_ref[...] + (1.0 - B2) * jnp.square(gv)
        nm_ref[...] = nm
        nv_ref[...] = nv
        d_ref[...] = -LR * ((nm * c1) / (jnp.sqrt(nv * c2) + ADAM_EPS) + WD * w_ref[...])

    if transposed:
        blk = pl.BlockSpec((ccols, tr), lambda i, s_ref: (0, i))
    else:
        blk = pl.BlockSpec((tr, ccols), lambda i, s_ref: (i, 0))
    return pl.pallas_call(
        body, name=name,
        grid_spec=pltpu.PrefetchScalarGridSpec(
            num_scalar_prefetch=1, grid=(r // tr,),
            in_specs=[pl.BlockSpec((k, tr, ccols), lambda i, s_ref: (0, i, 0)),
                      pl.BlockSpec((1, tr, ccols), lambda i, s_ref: (s_ref[0], i, 0))] + [blk] * 3,
            out_specs=[blk] * 4),
        out_shape=[jax.ShapeDtypeStruct(w.shape, F32)] * 4,
        compiler_params=_cparams("parallel"),
    )(slot, parts, own, w, m, v)


VEC_ROWS = ((0, 6), (6, 7), (9, 11), (11, 14), (7, 8), (8, 9))


def _adamw_vectors(w, g, m, v):
    c1 = 1.0 / (1.0 - B1 ** STEP)
    c2 = 1.0 / (1.0 - B2 ** STEP)

    def put(refs, p):
        for ref, (lo, hi) in zip(refs, VEC_ROWS):
            if ref.shape == (3, HEAD):
                ref[...] = p[lo:hi, :HEAD]
            else:
                ref[...] = jnp.concatenate([p[k:k + 1] for k in range(lo, hi)], axis=1)

    def body(w_ref, g_ref, m_ref, v_ref, *outs):
        gv = g_ref[...]
        nm = B1 * m_ref[...] + (1.0 - B1) * gv
        nv = B2 * v_ref[...] + (1.0 - B2) * jnp.square(gv)
        delta = -LR * ((nm * c1) / (jnp.sqrt(nv * c2) + ADAM_EPS) + WD * w_ref[...])
        for kind, p in enumerate((gv, delta, nm, nv)):
            put(outs[6 * kind:6 * kind + 6], p)

    shapes = [(1, 6 * D), (1, D), (1, 2 * D), (3, HEAD), (1, D), (1, D)]
    out = pl.pallas_call(
        body, name="adamw_vectors", out_shape=[jax.ShapeDtypeStruct(sh, F32) for sh in shapes] * 4, compiler_params=_cparams(),
    )(w, g, m, v)
    fix = lambda t: (t[0], t[1], t[2], t[3][None], t[4], t[5].reshape(D))
    return [fix(out[6 * kind:6 * kind + 6]) for kind in range(4)]


def _pack_vectors(b_ada, g_mix, g_mlp, g_fin, b_gate, conv_w):
    conv_rows = jnp.pad(conv_w.reshape(3, HEAD), ((0, 0), (0, D - HEAD)))
    return jnp.concatenate([b_ada.reshape(6, D), g_mix.reshape(1, D), g_mlp.reshape(1, D), g_fin.reshape(1, D),
                            b_gate.reshape(2, D), conv_rows, jnp.zeros((2, D), F32)], axis=0)


def kernel(x, c, w_ada, b_ada, g_norm_mix, w_in, b_gate, conv_w, w_branch_attn, w_branch_conv, w_out, g_norm_mlp, w_mlp_in, w_mlp_out, g_norm_final, loss_target, m_w_ada, m_b_ada, m_g_norm_mix, m_w_in, m_b_gate, m_conv_w, m_w_branch_attn, m_w_branch_conv, m_w_out, m_g_norm_mlp, m_w_mlp_in, m_w_mlp_out, m_g_norm_final, v_w_ada, v_b_ada, v_g_norm_mix, v_w_in, v_b_gate, v_conv_w, v_w_branch_attn, v_w_branch_conv, v_w_out, v_g_norm_mlp, v_w_mlp_in, v_w_mlp_out, v_g_norm_final):
    S = x.shape[1]
    xi, yi, ci = _my_place()
    me = _dev_index(xi, yi, ci)
    x2 = x.reshape(S, D)
    tgt = loss_target.reshape(S, D)

    pay = jnp.zeros((8, D), F32).at[0].set(c[0]).at[1:4, :HEAD].set(conv_w[0])
    ncol = w_ada.shape[2]
    b_cols = lax.dynamic_slice(b_ada, (0, me * ncol), (1, ncol))
    w_int, got, act, mod_all = _gather_w_in_and_condition(w_in[0].T.astype(BF16), pay, w_ada[0], b_cols)
    cw8 = jnp.pad(got[:, 1:4, :HEAD].transpose(1, 0, 2).reshape(3, D), ((0, 5), (0, 0)))
    mod = lax.dynamic_index_in_dim(mod_all, me, axis=1, keepdims=False).reshape(6, D)
    late = [w_branch_attn[0].T.astype(BF16), w_branch_conv[0].astype(BF16), w_out[0].astype(BF16),
            w_mlp_in[0].T.astype(BF16), w_mlp_out[0].astype(BF16)]
    w_int, late = lax.optimization_barrier((w_int, late))
    zones = _own_rows_into_zones(late, me.reshape(1).astype(jnp.int32))
    ag_mix = _split_start("gather_mix_start", "gather", late[:3], zones[:3])
    ag_mlp = _split_start("gather_mlp_start", "gather", late[3:], zones[3:])
    mod = mod + ag_mix[4] + ag_mlp[4]
    h = _prenorm(x2, g_norm_mix, mod[1:2], mod[0:1])

    def mix_weights(o_attn):
        return _split_wait("gather_mix_wait", "gather", *ag_mix[:4], o_attn)[1]

    def mlp_weights(x1):
        return _split_wait("gather_mlp_wait", "gather", *ag_mlp[:4], x1)[1]

    rs = {}

    def mlp_grads_ready(*grads):
        lands = [lax.empty((N_PEER, t.shape[0] // N_DEV, t.shape[1]), BF16) for t in grads]
        rs["mlp"] = _split_start("scatter_mlp_start", "scatter", grads, lands)
        return rs["mlp"][4]

    def w_in_grad_ready(g_in):
        r = g_in.shape[0] // N_DEV
        rs["sib"] = _split_start("sibling_w_in_start", "sibling", [g_in], [lax.empty((4, r, g_in.shape[1]), BF16)])
        return rs["sib"][5]

    def other_grads_ready(*small):
        core = ci.reshape(1).astype(jnp.int32)
        (g_in,), (sib_in,) = _split_wait("sibling_w_in_wait", "sibling", *rs["sib"][:4], small[0])
        pair = _pair_sums([g_in, *small], [sib_in, *_sibling_exchange(small)], core)
        lands = [lax.empty((3,) + t.shape[1:], BF16) for t in pair]
        rs["rest"] = _split_start("scatter_rest_start", "chips", pair, lands)
        return rs["rest"][4]

    ba, bb = b_gate[:, :D], b_gate[:, D:]
    grad_x, vec = _local_step(
        x2, h, tgt, mod, g_norm_mix, g_norm_mlp, g_norm_final.reshape(1, D), ba, bb, cw8, w_int, mix_weights, mlp_weights,
        mlp_grads_ready, w_in_grad_ready, other_grads_ready)

    vec_all = _allgather_small(vec, "gather_vec")
    vec_sum = _sum_parts(vec_all, "sum_vec")
    loss = vec_sum[14, 0]
    gm_all = vec_all[:, 0:6, :].reshape(N_DEV, 6 * D)
    gm_cols = lax.dynamic_slice(gm_all, (0, me * ncol), (N_DEV, ncol))
    g_w_ada = _ada_bwd(act.T, gm_cols)
    conv_cols = lax.dynamic_slice(vec_sum[11:14], (0, me * HEAD), (3, HEAD))
    g_pack = jnp.concatenate([vec_sum[0:11], jnp.pad(conv_cols, ((0, 0), (0, D - HEAD))), jnp.zeros((2, D), F32)], axis=0)
    packs = [_pack_vectors(*t) for t in ((b_ada, g_norm_mix, g_norm_mlp, g_norm_final, b_gate, conv_w),
                                         (m_b_ada, m_g_norm_mix, m_g_norm_mlp, m_g_norm_final, m_b_gate, m_conv_w),
                                         (v_b_ada, v_g_norm_mix, v_g_norm_mlp, v_g_norm_final, v_b_gate, v_conv_w))]
    gv, dv, mv, vv = _adamw_vectors(packs[0], g_pack, packs[1], packs[2])
    d_ada, nm_ada, nv_ada = _adamw(w_ada[0], g_w_ada, m_w_ada[0], v_w_ada[0], "adamw_w_ada")

    big = {}
    srcs, lands = _split_wait("scatter_mlp_wait", "scatter", *rs["mlp"][:4], d_ada)
    own = [g.reshape((N_DEV,) + land.shape[1:]) for g, land in zip(srcs, lands)]
    slot = me.reshape(1).astype(jnp.int32)
    big["w_mi"] = tuple(t[None] for t in _sum_adamw(lands[0], own[0], slot, w_mlp_in[0], m_w_mlp_in[0], v_w_mlp_in[0], "adamw_w_mi",
                                                    transposed=True))
    big["w_mo"] = tuple(t[None] for t in _sum_adamw(lands[1], own[1], slot, w_mlp_out[0], m_w_mlp_out[0], v_w_mlp_out[0], "adamw_w_mo"))
    own, lands = _split_wait("scatter_rest_wait", "chips", *rs["rest"][:4], big["w_mo"][1])
    slot = (2 * xi + yi).reshape(1).astype(jnp.int32)
    big["w_in"] = tuple(t.T[None] for t in _sum_adamw(lands[0], own[0], slot, w_in[0].T, m_w_in[0].T, v_w_in[0].T, "adamw_w_in"))
    big["w_ba"] = tuple(t[None] for t in _sum_adamw(lands[1], own[1], slot, w_branch_attn[0], m_w_branch_attn[0], v_w_branch_attn[0],
                                                    "adamw_w_ba", transposed=True))
    big["w_bc"] = tuple(t[None] for t in _sum_adamw(lands[2], own[2], slot, w_branch_conv[0], m_w_branch_conv[0], v_w_branch_conv[0], "adamw_w_bc"))
    big["w_out"] = tuple(t[None] for t in _sum_adamw(lands[3], own[3], slot, w_out[0], m_w_out[0], v_w_out[0], "adamw_w_out"))

    def ordered(k, ada, vecs):
        return (ada[None], vecs[0], vecs[1], big["w_in"][k], vecs[2], vecs[3], big["w_ba"][k], big["w_bc"][k],
                big["w_out"][k], vecs[4], big["w_mi"][k], big["w_mo"][k], vecs[5])

    return (loss, grad_x.reshape(1, S, D), *ordered(0, g_w_ada, gv), *ordered(1, d_ada, dv),
            *ordered(2, nm_ada, mv), *ordered(3, nv_ada, vv))
```

```python
import numpy as np
import jax
import jax.numpy as jnp
from jax import lax
from jax.experimental import pallas as pl
from jax.experimental.pallas import tpu as pltpu

F32, BF16 = jnp.float32, jnp.bfloat16
D = 1024
HEAD = 128
DILATIONS = (1, 4, 16)
N_SLOT = 4
AOW = N_SLOT * HEAD
DFF = 4 * D
N_DEV = 8
UNROLL = 16
EPS = 1e-6
NEG = -1e30
SCALE = HEAD ** -0.5
LR, B1, B2, ADAM_EPS, WD, STEP = 0.001, 0.9, 0.999, 1e-08, 0.01, 10
V7X_VMEM_LIMIT = 56 * 1024 * 1024
TM = 1024
MESH = pl.DeviceIdType.MESH


def _cparams(*sem):
    if sem:
        return pltpu.CompilerParams(dimension_semantics=sem, vmem_limit_bytes=V7X_VMEM_LIMIT)
    return pltpu.CompilerParams(vmem_limit_bytes=V7X_VMEM_LIMIT)


def _nn(a, b):
    return jnp.dot(a, b, preferred_element_type=F32)


def _nt(a, b):
    return lax.dot_general(a, b, (((1,), (1,)), ((), ())), preferred_element_type=F32)


def _tn(a, b):
    return lax.dot_general(a, b, (((0,), (0,)), ((), ())), preferred_element_type=F32)


def _rms_r(x):
    return lax.rsqrt(jnp.mean(x * x, axis=-1, keepdims=True) + EPS)


def _rms_bwd(x, r, g, dn):
    gy = dn * g
    dx = r * gy - x * (r * r * r) * jnp.mean(x * gy, axis=-1, keepdims=True)
    return dx, dn * (x * r)


def _sigmoid(t):
    return 1.0 / (1.0 + jnp.exp(-t))


def _rowsum(v):
    return jnp.sum(v, axis=0, keepdims=True)


def _vec_spec(n=D):
    return pl.BlockSpec((1, n), lambda *_: (0, 0))


def _const_spec(shape):
    nd = len(shape)
    return pl.BlockSpec(shape, lambda *_: (0,) * nd)


def _win_rowblock(j):
    return jnp.where(j < 9, (j % 3) * 3 + j // 3, j)


def _prenorm(x, g, sc, sh):
    S = x.shape[0]
    tm = TM

    def body(x_ref, g_ref, sc_ref, sh_ref, h_ref):
        xv = x_ref[...]
        h_ref[...] = (xv * _rms_r(xv) * g_ref[...] * (1.0 + sc_ref[...]) + sh_ref[...]).astype(BF16)

    row = pl.BlockSpec((tm, D), lambda i: (i, 0))
    return pl.pallas_call(
        body, name="prenorm", grid=(S // tm,), in_specs=[row, _vec_spec(), _vec_spec(), _vec_spec()], out_specs=row,
        out_shape=jax.ShapeDtypeStruct((S, D), BF16), compiler_params=_cparams("parallel"),
    )(x, g, sc, sh)


def _proj(h, w_int):
    S = h.shape[0]

    def body(h_ref, w_ref, q_ref, e_ref):
        j = pl.program_id(0)
        acc = _nt(h_ref[...], w_ref[...])

        @pl.when(j < 9)
        def _():
            q_ref[0] = acc

        @pl.when(j >= 9)
        def _():
            e_ref[0] = acc.astype(BF16)

    def e_idx(j):
        k = jnp.maximum(j - 9, 0)
        return (k // 2, 0, k % 2)

    return pl.pallas_call(
        body, name="proj", grid=(19,),
        in_specs=[pl.BlockSpec((S, D), lambda j: (0, 0), pipeline_mode=pl.Buffered(1)),
                  pl.BlockSpec((512, D), lambda j: (_win_rowblock(j), 0))],
        out_specs=[pl.BlockSpec((1, S, 512), lambda j: (jnp.minimum(j, 8), 0, 0)), pl.BlockSpec((1, S, 512), e_idx)],
        out_shape=[jax.ShapeDtypeStruct((9, S, 512), F32), jax.ShapeDtypeStruct((5, S, D), BF16)],
        compiler_params=_cparams("arbitrary"),
    )(h, w_int)


def _bias_table():
    slopes = (2.0 ** (-8.0 * np.arange(1, 13, dtype=np.float32) / 12.0)).astype(np.float32)
    qi = np.arange(HEAD)[:, None]
    kj = np.arange(2 * HEAD)[None, :]
    delta = HEAD + qi - kj
    mask = (delta >= 0) & (delta <= HEAD)
    out = np.zeros((3, N_SLOT, HEAD, 2 * HEAD), np.float32)
    for gi, d in enumerate(DILATIONS):
        for j in range(N_SLOT):
            bias = -slopes[gi * N_SLOT + j] * (delta * d).astype(np.float32)
            out[gi, j] = np.where(mask, bias, NEG)
    out_t = np.concatenate([out[..., HEAD:].swapaxes(-1, -2), out[..., :HEAD].swapaxes(-1, -2)], axis=-1)
    return jnp.asarray(out), jnp.asarray(out_t)


def _attn_fwd(qkv, bias):
    S = qkv.shape[2]
    nblk = S // HEAD
    rows = 256

    def body(qkv_ref, b_ref, o_ref, lse_ref, o_s, lse_s):
        g = pl.program_id(1)
        bias = b_ref[0, 0]
        col = lax.broadcasted_iota(jnp.int32, bias.shape, 1)
        bias_first = jnp.where(col < HEAD, NEG, bias)

        for gi, d in enumerate(DILATIONS):
            @pl.when(g == gi)
            def _(gi=gi, d=d):
                nb = nblk // d

                def keys(start):
                    sl = pl.ds(start, HEAD, stride=d)
                    return qkv_ref.at[0, 1][sl, :].astype(BF16), qkv_ref.at[0, 2][sl, :].astype(BF16)

                def step(b, first_of_residue, before):
                    r, n = b // nb, b % nb
                    cur = pl.ds(n * (HEAD * d) + r, HEAD, stride=d)
                    own = keys(n * (HEAD * d) + r)
                    if first_of_residue:
                        before = own
                    q = qkv_ref.at[0, 0][cur, :].astype(BF16)
                    kw = jnp.concatenate([before[0], own[0]], axis=0)
                    vw = jnp.concatenate([before[1], own[1]], axis=0)
                    s = _nt(q, kw) * SCALE + jnp.where(n > 0, bias, bias_first)
                    m = jnp.max(s, axis=-1, keepdims=True)
                    p = jnp.exp(s - m)
                    l = jnp.sum(p, axis=-1, keepdims=True)
                    o_s.at[gi][cur, :] = _nn(p.astype(BF16), vw) / l
                    lse_s.at[gi][cur, :] = jnp.broadcast_to(m + jnp.log(l), (HEAD, HEAD))
                    return own

                def steps(i, before):
                    for u in range(UNROLL):
                        before = step(i * UNROLL + u, nb <= UNROLL and u % nb == 0, before)
                    return before

                lax.fori_loop(0, nblk // UNROLL, steps, keys(0))

        @pl.when(g == len(DILATIONS) - 1)
        def _():
            def merge(i, carry):
                r = pl.ds(pl.multiple_of(i * rows, rows), rows)
                ls = [lse_s[k, r, :] for k in range(3)]
                top = jnp.maximum(jnp.maximum(ls[0], ls[1]), ls[2])
                ws = [jnp.exp(t - top) for t in ls]
                den = ws[0] + ws[1] + ws[2]
                o_ref[r, :] = (ws[0] * o_s[0, r, :] + ws[1] * o_s[1, r, :] + ws[2] * o_s[2, r, :]) / den
                lse_ref[r, :] = top + jnp.log(den)
                return carry

            lax.fori_loop(0, S // rows, merge, 0)

    return pl.pallas_call(
        body, name="attn_fwd", grid=(N_SLOT, 3),
        in_specs=[pl.BlockSpec((1, 3, S, HEAD), lambda j, g: (g, 0, 0, j)),
                  pl.BlockSpec((1, 1, HEAD, 2 * HEAD), lambda j, g: (g, j, 0, 0))],
        out_specs=[pl.BlockSpec((S, HEAD), lambda j, g: (0, j)), pl.BlockSpec((S, HEAD), lambda j, g: (0, j))],
        out_shape=[jax.ShapeDtypeStruct((S, AOW), F32), jax.ShapeDtypeStruct((S, AOW), F32)],
        scratch_shapes=[pltpu.VMEM((3, S, HEAD), F32)] * 2,
        compiler_params=_cparams("parallel", "arbitrary"),
    )(qkv, bias)


def _shift_down(z, k, halo_rows):
    out = pltpu.roll(z, k, axis=0)
    top = out[:8]
    rid = lax.broadcasted_iota(jnp.int32, top.shape, 0)
    for t in range(k):
        top = jnp.where(rid == t, halo_rows[t], top)
    return jnp.concatenate([top, out[8:]], axis=0)


def _shift_up(z, k, halo_rows):
    n = z.shape[0]
    out = pltpu.roll(z, n - k, axis=0)
    bottom = out[n - 8:]
    rid = lax.broadcasted_iota(jnp.int32, bottom.shape, 0)
    for t in range(k):
        bottom = jnp.where(rid == 8 - k + t, halo_rows[t], bottom)
    return jnp.concatenate([out[:n - 8], bottom], axis=0)


def _e_spec(chunk, tm):
    return pl.BlockSpec((1, tm, D), lambda i, c=chunk: (c, i, 0))


def _e_prev_spec(chunk, tm):
    return pl.BlockSpec((1, 16, D), lambda i, c=chunk: (c, jnp.maximum(i * (tm // 16) - 1, 0), 0))


def _e_next_spec(chunk, tm, S):
    return pl.BlockSpec((1, 16, D), lambda i, c=chunk: (c, jnp.minimum((i + 1) * (tm // 16), S // 16 - 1), 0))


def _mix(o_attn, e, cw8, ba, bb, w_bat, w_bc):
    S = o_attn.shape[0]
    tm = 512

    def body(o_ref, cb_ref, cc_ref, cx_ref, ga_ref, gb_ref, ccp_ref, cxp_ref, cw_ref, ba_ref, bb_ref, wba_ref, wbc_ref,
             obf_ref, cbu_ref, ya_ref, yc_ref, mg_ref):
        i = pl.program_id(0)
        o = o_ref[...].astype(BF16)
        obf_ref[...] = o
        ya = _nt(o, wba_ref[...])
        z = cc_ref[0].astype(F32) * cx_ref[0].astype(F32)
        zp = ccp_ref[0].astype(F32) * cxp_ref[0].astype(F32) * (i > 0).astype(F32)
        z1 = _shift_down(z, 1, [zp[15:16]])
        z2 = _shift_down(z, 2, [zp[14:15], zp[15:16]])
        cw = cw_ref[...]
        u = cw[0:1] * z2 + cw[1:2] * z1 + cw[2:3] * z
        cbu = (cb_ref[0].astype(F32) * u).astype(BF16)
        cbu_ref[...] = cbu
        yc = _nn(cbu, wbc_ref[...])
        sa = _sigmoid(ga_ref[0].astype(F32) + ba_ref[...])
        sb = _sigmoid(gb_ref[0].astype(F32) + bb_ref[...])
        ya_ref[...] = ya.astype(BF16)
        yc_ref[...] = yc.astype(BF16)
        mg_ref[...] = (sa * ya + sb * yc).astype(BF16)

    row = lambda w: pl.BlockSpec((tm, w), lambda i: (i, 0))
    return pl.pallas_call(
        body, name="mix", grid=(S // tm,),
        in_specs=[row(AOW)] + [_e_spec(c, tm) for c in range(5)] + [_e_prev_spec(1, tm), _e_prev_spec(2, tm),
                  _const_spec((8, D)), _vec_spec(), _vec_spec(), _const_spec((D, AOW)), _const_spec((D, D))],
        out_specs=[row(AOW), row(D), row(D), row(D), row(D)],
        out_shape=[jax.ShapeDtypeStruct((S, AOW), BF16)] + [jax.ShapeDtypeStruct((S, D), BF16)] * 4,
        compiler_params=_cparams("parallel"),
    )(o_attn, e, e, e, e, e, e, e, cw8, ba, bb, w_bat, w_bc)


def _out_proj(merged, w_out, x, gate1, g_mlp, sc2, sh2):
    S = x.shape[0]
    tm = TM

    def body(mg_ref, w_ref, x_ref, gt_ref, g_ref, sc_ref, sh_ref, x1_ref, mo_ref, h2_ref):
        mo = _nn(mg_ref[...], w_ref[...])
        mo_ref[...] = mo.astype(BF16)
        x1 = x_ref[...] + gt_ref[...] * mo
        x1_ref[...] = x1
        h2 = x1 * _rms_r(x1) * g_ref[...] * (1.0 + sc_ref[...]) + sh_ref[...]
        h2_ref[...] = h2.astype(BF16)

    row = pl.BlockSpec((tm, D), lambda i: (i, 0))
    return pl.pallas_call(
        body, name="out_proj", grid=(S // tm,),
        in_specs=[row, _const_spec((D, D)), row, _vec_spec(), _vec_spec(), _vec_spec(), _vec_spec()],
        out_specs=[row, row, row],
        out_shape=[jax.ShapeDtypeStruct((S, D), F32), jax.ShapeDtypeStruct((S, D), BF16), jax.ShapeDtypeStruct((S, D), BF16)],
        compiler_params=_cparams("parallel"),
    )(merged, w_out, x, gate1, g_mlp, sc2, sh2)


def _mlp_in(h2, w_mit):
    S = h2.shape[0]
    tm, tn = TM, 2048

    def body(h_ref, w_ref, a_ref, f_ref):
        a = _nt(h_ref[...], w_ref[...])
        a_ref[...] = a.astype(BF16)
        f_ref[...] = jnp.square(jnp.maximum(a, 0.0)).astype(BF16)

    blk = pl.BlockSpec((tm, tn), lambda i, j: (i, j))
    return pl.pallas_call(
        body, name="mlp_in", grid=(S // tm, DFF // tn),
        in_specs=[pl.BlockSpec((tm, D), lambda i, j: (i, 0)), pl.BlockSpec((tn, D), lambda i, j: (j, 0))],
        out_specs=[blk, blk],
        out_shape=[jax.ShapeDtypeStruct((S, DFF), BF16)] * 2,
        compiler_params=_cparams("parallel", "parallel"),
    )(h2, w_mit)


def _mlp_out(f, w_mo, x1, gate2, g_fin, tgt):
    S = x1.shape[0]
    tm = 512
    half = tm // 2

    def body(f_ref, w_ref, x1_ref, gt_ref, g_ref, t_ref, mlp_ref, dx2_ref, pv_ref):
        @pl.when(pl.program_id(0) == 0)
        def _():
            pv_ref[...] = jnp.zeros_like(pv_ref)

        g = g_ref[...]
        for hs in (pl.ds(0, half), pl.ds(half, half)):
            mlp = _nn(f_ref[hs, :], w_ref[...])
            mlp_ref[hs, :] = mlp.astype(BF16)
            x2 = x1_ref[hs, :] + gt_ref[...] * mlp
            r = _rms_r(x2)
            err = x2 * r * g - t_ref[hs, :]
            dx2, pg = _rms_bwd(x2, r, g, err * (1.0 / D))
            dx2_ref[hs, :] = dx2
            pv_ref[0:1, :] += _rowsum(pg)
            pv_ref[1:2, :] += 0.5 * _rowsum(jnp.mean(err * err, axis=-1, keepdims=True))

    row = pl.BlockSpec((tm, D), lambda i: (i, 0))
    return pl.pallas_call(
        body, name="mlp_out", grid=(S // tm,),
        in_specs=[pl.BlockSpec((tm, DFF), lambda i: (i, 0)), _const_spec((DFF, D)), row, _vec_spec(), _vec_spec(), row],
        out_specs=[row, row, _const_spec((8, D))],
        out_shape=[jax.ShapeDtypeStruct((S, D), BF16), jax.ShapeDtypeStruct((S, D), F32), jax.ShapeDtypeStruct((8, D), F32)],
        compiler_params=_cparams("arbitrary"),
    )(f, w_mo, x1, gate2, g_fin, tgt)


def _bwd_mlp_a(dx2, gate2, mlp, w_mo, a):
    S = dx2.shape[0]
    tm = 512
    half = tm // 2

    def body(dx_ref, gt_ref, mlp_ref, w_ref, a_ref, da_ref, dmo_ref, pv_ref):
        @pl.when(pl.program_id(0) == 0)
        def _():
            pv_ref[...] = jnp.zeros_like(pv_ref)

        for hs in (pl.ds(0, half), pl.ds(half, half)):
            dx = dx_ref[hs, :]
            dmo = (dx * gt_ref[...]).astype(BF16)
            dmo_ref[hs, :] = dmo
            pv_ref[0:1, :] += _rowsum(dx * mlp_ref[hs, :].astype(F32))
            df = _nt(dmo, w_ref[...])
            da_ref[hs, :] = (df * (2.0 * jnp.maximum(a_ref[hs, :].astype(F32), 0.0))).astype(BF16)

    row = pl.BlockSpec((tm, D), lambda i: (i, 0))
    wide = pl.BlockSpec((tm, DFF), lambda i: (i, 0))
    return pl.pallas_call(
        body, name="bwd_mlp_a", grid=(S // tm,),
        in_specs=[row, _vec_spec(), row, _const_spec((DFF, D)), wide],
        out_specs=[wide, row, _const_spec((8, D))],
        out_shape=[jax.ShapeDtypeStruct((S, DFF), BF16), jax.ShapeDtypeStruct((S, D), BF16), jax.ShapeDtypeStruct((8, D), F32)],
        compiler_params=_cparams("arbitrary"),
    )(dx2, gate2, mlp, w_mo, a)


def _bwd_mlp_b(da, w_mit, x1, dx2, g_mlp, sc2):
    S = x1.shape[0]
    tm = 512
    half = tm // 2

    def body(da_ref, w_ref, x1_ref, dx2_ref, g_ref, sc_ref, dx1_ref, pv_ref):
        @pl.when(pl.program_id(0) == 0)
        def _():
            pv_ref[...] = jnp.zeros_like(pv_ref)

        g = g_ref[...]
        for hs in (pl.ds(0, half), pl.ds(half, half)):
            dh = _nn(da_ref[hs, :], w_ref[...])
            x1 = x1_ref[hs, :]
            r = _rms_r(x1)
            dxn, pg = _rms_bwd(x1, r, g, dh * (1.0 + sc_ref[...]))
            dx1_ref[hs, :] = dx2_ref[hs, :] + dxn
            pv_ref[0:1, :] += _rowsum(dh)
            pv_ref[1:2, :] += _rowsum(dh * (x1 * r * g))
            pv_ref[2:3, :] += _rowsum(pg)

    row = pl.BlockSpec((tm, D), lambda i: (i, 0))
    return pl.pallas_call(
        body, name="bwd_mlp_b", grid=(S // tm,),
        in_specs=[pl.BlockSpec((tm, DFF), lambda i: (i, 0)), _const_spec((DFF, D)), row, row, _vec_spec(), _vec_spec()],
        out_specs=[row, _const_spec((8, D))],
        out_shape=[jax.ShapeDtypeStruct((S, D), F32), jax.ShapeDtypeStruct((8, D), F32)],
        compiler_params=_cparams("arbitrary"),
    )(da, w_mit, x1, dx2, g_mlp, sc2)


def _bwd_mix(dx1, gate1, mo, e, cw8, ba, bb, ya, yc, o_attn, w_out, w_bc, w_bat):
    S = dx1.shape[0]
    tm = 256
    n_tiles = S // tm

    def body(dx_ref, dxn_ref, gt_ref, mo_ref, cb_ref, cc_ref, cx_ref, ga_ref, gb_ref, cbn_ref, gbn_ref, ccp_ref, cxp_ref,
             cw_ref, ba_ref, bb_ref, ya_ref, yc_ref, o_ref, wout_ref, wbc_ref, wba_ref,
             dmo_ref, dya_ref, dyc_ref, do_ref, dl_ref, de_ref, pv_ref):
        i = pl.program_id(0)

        @pl.when(i == 0)
        def _():
            pv_ref[...] = jnp.zeros_like(pv_ref)

        dx = dx_ref[...]
        cb = cb_ref[0].astype(F32)
        cc = cc_ref[0].astype(F32)
        cx = cx_ref[0].astype(F32)
        dmo_all = (jnp.concatenate([dx, dxn_ref[...]], axis=0) * gt_ref[...]).astype(BF16)
        dmg_all = _nt(dmo_all, wout_ref[...])
        sb_all = _sigmoid(jnp.concatenate([gb_ref[0], gbn_ref[0]], axis=0).astype(F32) + bb_ref[...])
        dyc_all = dmg_all * sb_all
        dcbu_all = _nt(dyc_all.astype(BF16), wbc_ref[...])
        dmo, dmg, sb, dyc, dcbu = dmo_all[:tm], dmg_all[:tm], sb_all[:tm], dyc_all[:tm], dcbu_all[:tm]
        dmo_ref[...] = dmo
        pv_ref[0:1, :] += _rowsum(dx * mo_ref[...].astype(F32))
        sa = _sigmoid(ga_ref[0].astype(F32) + ba_ref[...])
        dya = (dmg * sa).astype(BF16)
        dya_ref[...] = dya
        dyc_ref[...] = dyc.astype(BF16)
        dga = dmg * ya_ref[...].astype(F32) * sa * (1.0 - sa)
        dgb = dmg * yc_ref[...].astype(F32) * sb * (1.0 - sb)
        pv_ref[1:2, :] += _rowsum(dga)
        pv_ref[2:3, :] += _rowsum(dgb)

        do = _nn(dya, wba_ref[...])
        do_ref[...] = do
        prod = do * o_ref[...]
        dl_ref[...] = jnp.concatenate(
            [jnp.broadcast_to(jnp.sum(prod[:, s * HEAD:(s + 1) * HEAD], axis=-1, keepdims=True), (tm, HEAD))
             for s in range(N_SLOT)], axis=1)

        z = cc * cx
        zp = ccp_ref[0].astype(F32) * cxp_ref[0].astype(F32) * (i > 0).astype(F32)
        z1 = _shift_down(z, 1, [zp[15:16]])
        z2 = _shift_down(z, 2, [zp[14:15], zp[15:16]])
        cw = cw_ref[...]
        u = cw[0:1] * z2 + cw[1:2] * z1 + cw[2:3] * z
        du = dcbu * cb
        du_n = dcbu_all[tm:] * cbn_ref[0].astype(F32) * (i < n_tiles - 1).astype(F32)
        du1 = _shift_up(du, 1, [du_n[0:1]])
        du2 = _shift_up(du, 2, [du_n[0:1], du_n[1:2]])
        dz = cw[2:3] * du + cw[1:2] * du1 + cw[0:1] * du2
        pv_ref[3:4, :] += _rowsum(du * z2)
        pv_ref[4:5, :] += _rowsum(du * z1)
        pv_ref[5:6, :] += _rowsum(du * z)

        de_ref[0] = (dcbu * u).astype(BF16)
        de_ref[1] = (dz * cx).astype(BF16)
        de_ref[2] = (dz * cc).astype(BF16)
        de_ref[3] = dga.astype(BF16)
        de_ref[4] = dgb.astype(BF16)

    row = lambda w: pl.BlockSpec((tm, w), lambda i: (i, 0))
    nxt = pl.BlockSpec((16, D), lambda i: (jnp.minimum((i + 1) * (tm // 16), S // 16 - 1), 0))
    return pl.pallas_call(
        body, name="bwd_mix", grid=(n_tiles,),
        in_specs=[row(D), nxt, _vec_spec(), row(D)] + [_e_spec(c, tm) for c in range(5)]
                 + [_e_next_spec(0, tm, S), _e_next_spec(4, tm, S), _e_prev_spec(1, tm), _e_prev_spec(2, tm),
                    _const_spec((8, D)), _vec_spec(), _vec_spec(), row(D), row(D), row(AOW),
                    _const_spec((D, D)), _const_spec((D, D)), _const_spec((D, AOW))],
        out_specs=[row(D), row(D), row(D), row(AOW), row(AOW), pl.BlockSpec((5, tm, D), lambda i: (0, i, 0)),
                   _const_spec((8, D))],
        out_shape=[jax.ShapeDtypeStruct((S, D), BF16)] * 3 + [jax.ShapeDtypeStruct((S, AOW), F32)] * 2
                  + [jax.ShapeDtypeStruct((5, S, D), BF16), jax.ShapeDtypeStruct((8, D), F32)],
        compiler_params=_cparams("arbitrary"),
    )(dx1, dx1, gate1, mo, e, e, e, e, e, e, e, e, e, cw8, ba, bb, ya, yc, o_attn, w_out, w_bc, w_bat)


def _attn_bwd(qkv, do, lse, dl, bias_t):
    S = qkv.shape[2]
    nblk = S // HEAD

    def body(qkv_ref, do_ref, lse_ref, dl_ref, b_ref, d_ref):
        g = pl.program_id(1)
        bias = b_ref[0, 0]
        col = lax.broadcasted_iota(jnp.int32, bias.shape, 1)
        bias_last = jnp.where(col >= HEAD, NEG, bias)
        eye = (lax.broadcasted_iota(jnp.int32, (HEAD, HEAD), 0) == lax.broadcasted_iota(jnp.int32, (HEAD, HEAD), 1)).astype(F32)

        def as_row(t):
            return jnp.sum(t * eye, axis=0, keepdims=True)

        for gi, d in enumerate(DILATIONS):
            @pl.when(g == gi)
            def _(d=d):
                nb = nblk // d

                def query_side(start):
                    sl = pl.ds(start, HEAD, stride=d)
                    return (qkv_ref.at[0, 0][sl, :].astype(BF16), do_ref[sl, :].astype(BF16),
                            as_row(lse_ref[sl, :]), as_row(dl_ref[sl, :]))

                def step(b, first_of_residue, carry):
                    dq_part, own = carry
                    r, n = b // nb, b % nb
                    cur = pl.ds(n * (HEAD * d) + r, HEAD, stride=d)
                    if first_of_residue:
                        own = query_side(r)
                    nxt = query_side(jnp.minimum(n + 1, nb - 1) * (HEAD * d) + r)
                    q2 = jnp.concatenate([own[0], nxt[0]], axis=0)
                    do2 = jnp.concatenate([own[1], nxt[1]], axis=0)
                    k = qkv_ref.at[0, 1][cur, :].astype(BF16)
                    v = qkv_ref.at[0, 2][cur, :].astype(BF16)
                    s = _nt(k, q2) * SCALE + jnp.where(n < nb - 1, bias, bias_last)
                    p = jnp.exp(s - jnp.concatenate([own[2], nxt[2]], axis=1))
                    d_ref.at[0, 2][cur, :] = _nn(p.astype(BF16), do2)
                    dp = _nt(v, do2)
                    ds = (p * (dp - jnp.concatenate([own[3], nxt[3]], axis=1)) * SCALE).astype(BF16)
                    d_ref.at[0, 1][cur, :] = _nn(ds, q2)
                    dq2 = _tn(ds, k)
                    d_ref.at[0, 0][cur, :] = dq2[:HEAD] + jnp.where(n > 0, dq_part, 0.0)
                    return dq2[HEAD:], nxt

                def steps(i, carry):
                    for u in range(UNROLL):
                        carry = step(i * UNROLL + u, nb <= UNROLL and u % nb == 0, carry)
                    return carry

                lax.fori_loop(0, nblk // UNROLL, steps, (jnp.zeros((HEAD, HEAD), F32), query_side(0)))

    col_blk = pl.BlockSpec((S, HEAD), lambda j, g: (0, j))
    qkv_blk = pl.BlockSpec((1, 3, S, HEAD), lambda j, g: (g, 0, 0, j))
    return pl.pallas_call(
        body, name="attn_bwd", grid=(N_SLOT, 3),
        in_specs=[qkv_blk, col_blk, col_blk, col_blk, pl.BlockSpec((1, 1, HEAD, 2 * HEAD), lambda j, g: (g, j, 0, 0))],
        out_specs=qkv_blk,
        out_shape=jax.ShapeDtypeStruct((3, 3, S, AOW), F32),
        compiler_params=_cparams("parallel", "arbitrary"),
    )(qkv, do, lse, dl, bias_t)


def _bwd_in(dqkv, de, w_int, x, dx1, g_mix, sc1):
    S = x.shape[0]
    tm = TM
    dqkv = dqkv.reshape(3, 3, S, AOW)

    def body(dq_ref, de_ref, wq_ref, wk_ref, wv_ref, wa_ref, wb_ref, x_ref, dx1_ref, g_ref, sc_ref, gx_ref, pv_ref):
        acc = gx_ref
        i, k = pl.program_id(0), pl.program_id(1)

        @pl.when((i == 0) & (k == 0))
        def _():
            pv_ref[...] = jnp.zeros_like(pv_ref)

        @pl.when(k == 0)
        def _():
            acc[...] = jnp.zeros_like(acc)

        @pl.when(k < 3)
        def _():
            lhs = jnp.concatenate([dq_ref[0, t].astype(BF16) for t in range(3)], axis=1)
            acc[...] += _nn(lhs, jnp.concatenate([wq_ref[...], wk_ref[...], wv_ref[...]], axis=0))

        @pl.when(k >= 3)
        def _():
            acc[...] += _nn(de_ref[0], jnp.concatenate([wa_ref[...], wb_ref[...]], axis=0))

        @pl.when(k == 7)
        def _():
            dh = acc[...]
            xv = x_ref[...]
            r = _rms_r(xv)
            g = g_ref[...]
            dxn, pg = _rms_bwd(xv, r, g, dh * (1.0 + sc_ref[...]))
            gx_ref[...] = dx1_ref[...] + dxn
            pv_ref[0:1, :] += _rowsum(dh)
            pv_ref[1:2, :] += _rowsum(dh * (xv * r * g))
            pv_ref[2:3, :] += _rowsum(pg)

    grp = lambda k: jnp.minimum(k, 2)
    chunk = lambda k: jnp.maximum(k - 3, 0)
    wblk = lambda f: pl.BlockSpec((512, D), lambda i, k: (f(k), 0))
    row = pl.BlockSpec((tm, D), lambda i, k: (i, 0))
    once = pl.BlockSpec((tm, D), lambda i, k: (i, 0), pipeline_mode=pl.Buffered(1))
    return pl.pallas_call(
        body, name="bwd_in", grid=(S // tm, 8),
        in_specs=[pl.BlockSpec((1, 3, tm, 512), lambda i, k: (grp(k), 0, i, 0)),
                  pl.BlockSpec((1, tm, D), lambda i, k: (chunk(k), i, 0)),
                  wblk(grp), wblk(lambda k: 3 + grp(k)), wblk(lambda k: 6 + grp(k)),
                  wblk(lambda k: 9 + 2 * chunk(k)), wblk(lambda k: 10 + 2 * chunk(k)),
                  once, once, _vec_spec(), _vec_spec()],
        out_specs=[row, _const_spec((8, D))],
        out_shape=[jax.ShapeDtypeStruct((S, D), F32), jax.ShapeDtypeStruct((8, D), F32)],
        compiler_params=_cparams("arbitrary", "arbitrary"),
    )(dqkv, de, w_int, w_int, w_int, w_int, w_int, x, dx1, g_mix, sc1)


def _grad_w(name, a, b):
    S, ka = a.shape
    nb = b.shape[1]

    def body(a_ref, b_ref, o_ref):
        o_ref[...] = _tn(a_ref[...], b_ref[...]).astype(BF16)

    return pl.pallas_call(
        body, name=name, grid=(ka // 512,),
        in_specs=[pl.BlockSpec((S, 512), lambda n: (0, n)), pl.BlockSpec((S, nb), lambda n: (0, 0))],
        out_specs=pl.BlockSpec((512, nb), lambda n: (n, 0)),
        out_shape=jax.ShapeDtypeStruct((ka, nb), BF16),
        compiler_params=_cparams("parallel"),
    )(a, b)


def _grad_w_small(dya, o_bf, cbu, dyc, merged, dmo, after):
    S = dya.shape[0]

    def body(dya_ref, o_ref, cbu_ref, dyc_ref, mg_ref, dmo_ref, after_ref, gba_ref, gbc_ref, gout_ref):
        gba_ref[...] = _tn(dya_ref[...], o_ref[...]).astype(BF16)
        gbc_ref[...] = _tn(cbu_ref[...], dyc_ref[...]).astype(BF16)
        gout_ref[...] = _tn(mg_ref[...], dmo_ref[...]).astype(BF16)

    a_blk = pl.BlockSpec((S, 512), lambda n: (0, n))
    whole = lambda w: pl.BlockSpec((S, w), lambda n: (0, 0))
    out = lambda w: pl.BlockSpec((512, w), lambda n: (n, 0))
    return pl.pallas_call(
        body, name="grad_w_small", grid=(D // 512,),
        in_specs=[a_blk, whole(AOW), a_blk, whole(D), a_blk, whole(D), pl.BlockSpec(memory_space=pl.ANY)],
        out_specs=[out(AOW), out(D), out(D)],
        out_shape=[jax.ShapeDtypeStruct((D, AOW), BF16), jax.ShapeDtypeStruct((D, D), BF16), jax.ShapeDtypeStruct((D, D), BF16)],
        compiler_params=_cparams("parallel"),
    )(dya, o_bf, cbu, dyc, merged, dmo, after)


def _grad_w_in(dqkv, de, h):
    S = h.shape[0]

    def body(dq_ref, de_ref, h_ref, o_ref):
        n = pl.program_id(0)

        @pl.when(n < 9)
        def _():
            o_ref[...] = _tn(dq_ref[0].astype(BF16), h_ref[...]).astype(BF16)

        @pl.when(n >= 9)
        def _():
            o_ref[...] = _tn(de_ref[0], h_ref[...]).astype(BF16)

    def e_idx(n):
        kk = jnp.maximum(n - 9, 0)
        return (kk // 2, 0, kk % 2)

    return pl.pallas_call(
        body, name="grad_w_in", grid=(19,),
        in_specs=[pl.BlockSpec((1, S, 512), lambda n: (jnp.minimum(n, 8), 0, 0)), pl.BlockSpec((1, S, 512), e_idx),
                  pl.BlockSpec((S, D), lambda n: (0, 0))],
        out_specs=pl.BlockSpec((512, D), lambda n: (_win_rowblock(n), 0)),
        out_shape=jax.ShapeDtypeStruct((19 * 512, D), BF16),
        compiler_params=_cparams("parallel"),
    )(dqkv, de, h)


def _local_step(x, h, tgt, mod, g_mix, g_mlp, g_fin, ba, bb, cw8, w_int, mix_weights, mlp_weights, mlp_grads_ready, w_in_grad_ready,
                other_grads_ready):
    S = x.shape[0]
    sh1, sc1, gt1, sh2, sc2, gt2 = [mod[k:k + 1] for k in range(6)]
    bias, bias_t = _bias_table()

    qkv, e = _proj(h, w_int)
    qkv = qkv.reshape(3, 3, S, AOW)
    o_attn, lse = _attn_fwd(qkv, bias)
    w_bat, w_bc, w_out = mix_weights(o_attn)
    o_bf, cbu, ya, yc, merged = _mix(o_attn, e, cw8, ba, bb, w_bat, w_bc)
    x1, mo, h2 = _out_proj(merged, w_out, x, gt1, g_mlp, sc2, sh2)
    w_mit, w_mo = mlp_weights(x1)
    a, f = _mlp_in(h2, w_mit)
    mlp, dx2, pv_f = _mlp_out(f, w_mo, x1, gt2, g_fin, tgt)

    da, dmo2, pv_a = _bwd_mlp_a(dx2, gt2, mlp, w_mo, a)
    dx1, pv_b = _bwd_mlp_b(da, w_mit, x1, dx2, g_mlp, sc2)
    zero = mlp_grads_ready(_grad_w("grad_w_mi", da, h2), _grad_w("grad_w_mo", f, dmo2))
    dmo, dya, dyc, do, dl, de, pv_m = _bwd_mix(dx1, gt1 + zero, mo, e, cw8, ba, bb, ya, yc, o_attn, w_out, w_bc, w_bat)
    dqkv = _attn_bwd(qkv, do, lse, dl, bias_t).reshape(9, S, AOW)
    after = w_in_grad_ready(_grad_w_in(dqkv, de, h))
    zero = other_grads_ready(*_grad_w_small(dya, o_bf, cbu, dyc, merged, dmo, after))
    grad_x, pv_i = _bwd_in(dqkv, de, w_int, x, dx1, g_mix, sc1 + zero)

    vec = jnp.concatenate([pv_i[0:2], pv_m[0:1], pv_b[0:2], pv_a[0:1], pv_i[2:3], pv_b[2:3], pv_f[0:1],
                           pv_m[1:3], pv_m[3:6], pv_f[1:2], jnp.zeros((1, D), F32)], axis=0)
    return grad_x, vec


def _my_place():
    return lax.axis_index("x"), lax.axis_index("y"), lax.axis_index("c")


def _dev_index(px, py, pc):
    return 4 * px + 2 * py + pc


def _peer(x, y, c, m):
    return (x ^ ((m >> 2) & 1), y ^ ((m >> 1) & 1), c ^ (m & 1))


HBM_SPEC = pl.BlockSpec(memory_space=pltpu.HBM)
SEM_SPEC = pl.BlockSpec(memory_space=pltpu.SEMAPHORE)
N_PEER = N_DEV - 1


SPLIT_MASKS = {"gather": tuple(range(1, N_DEV)), "scatter": tuple(range(1, N_DEV)), "chips": (2, 4, 6), "sibling": (1, 1, 1, 1)}


def _split_copy(mode, src_ref, land_ref, send_sems, recv_sems, w, j, place, arriving=False):
    x, y, c = place
    masks = SPLIT_MASKS[mode]
    peer = _peer(x, y, c, masks[j])
    k = w * len(masks) + j
    sender, receiver = ((peer, (x, y, c)) if arriving else ((x, y, c), peer))
    if mode == "gather":
        r = src_ref.shape[0]
        src, dst = src_ref, land_ref.at[pl.ds(pl.multiple_of(_dev_index(*sender) * r, 16), r), :]
    elif mode == "scatter":
        r = land_ref.shape[1]
        src, dst = src_ref.at[pl.ds(pl.multiple_of(_dev_index(*receiver) * r, 16), r), :], land_ref.at[j]
    elif mode == "chips":
        src, dst = src_ref.at[2 * receiver[0] + receiver[1]], land_ref.at[j]
    else:
        r = land_ref.shape[1]
        src, dst = src_ref.at[pl.ds(pl.multiple_of((2 * j + receiver[2]) * r, 16), r), :], land_ref.at[j]
    return pltpu.make_async_remote_copy(src_ref=src, dst_ref=dst, send_sem=send_sems.at[k], recv_sem=recv_sems.at[k],
                                        device_id=peer, device_id_type=MESH)


def _split_start(name, mode, srcs, lands):
    n = len(srcs)
    nm = len(SPLIT_MASKS[mode])

    def body(*refs):
        src, land = refs[:n], refs[n:2 * n]
        send_sems, recv_sems = refs[2 * n], refs[2 * n + 1]
        token = refs[-1]
        place = _my_place()
        for w in range(n):
            for j in range(nm):
                _split_copy(mode, src[w], land[w], send_sems, recv_sems, w, j, place).start()
        token[...] = jnp.zeros_like(token)

    hbm = lambda t: pltpu.HBM(t.shape, t.dtype)
    out = pl.pallas_call(
        body, name=name,
        out_shape=(pltpu.SemaphoreType.DMA((n * nm,)), pltpu.SemaphoreType.DMA((n * nm,)), *[hbm(t) for t in srcs],
                   *[hbm(t) for t in lands], jax.ShapeDtypeStruct((8, 128), F32)),
        in_specs=(HBM_SPEC,) * (2 * n),
        out_specs=(SEM_SPEC, SEM_SPEC) + (HBM_SPEC,) * (2 * n) + (pl.BlockSpec(memory_space=pltpu.VMEM),),
        input_output_aliases={i: 2 + i for i in range(2 * n)},
        compiler_params=pltpu.CompilerParams(has_side_effects=pltpu.SideEffectType.DATAFLOW_SIDE_EFFECTING),
    )(*[pltpu.with_memory_space_constraint(t, pltpu.HBM) for t in (*srcs, *lands)])
    return out[0], out[1], out[2:2 + n], out[2 + n:2 + 2 * n], out[-1][0:1, 0:1], out[-1]


def _split_wait(name, mode, send_sems, recv_sems, srcs, lands, after):
    n = len(srcs)

    def body(*refs):
        src, land = refs[:n], refs[n:2 * n]
        ssem, rsem = refs[2 * n], refs[2 * n + 1]
        place = _my_place()
        for w in range(n):
            for j in range(len(SPLIT_MASKS[mode])):
                _split_copy(mode, src[w], land[w], ssem, rsem, w, j, place).wait_send()
                _split_copy(mode, src[w], land[w], ssem, rsem, w, j, place, arriving=True).wait_recv()

    hbm = lambda t: pltpu.HBM(t.shape, t.dtype)
    out = pl.pallas_call(
        body, name=name,
        out_shape=tuple(hbm(t) for t in (*srcs, *lands)),
        in_specs=(HBM_SPEC,) * (2 * n) + (SEM_SPEC, SEM_SPEC, pl.BlockSpec(memory_space=pl.ANY)),
        out_specs=(HBM_SPEC,) * (2 * n),
        input_output_aliases={i: i for i in range(2 * n)},
        compiler_params=pltpu.CompilerParams(has_side_effects=pltpu.SideEffectType.DATAFLOW_SIDE_EFFECTING),
    )(*srcs, *lands, send_sems, recv_sems, after)
    return out[:n], out[n:]


def _sibling_exchange(grads):
    nw = len(grads)
    HBM = pl.BlockSpec(memory_space=pl.ANY)

    def body(*refs):
        g, land = refs[:nw], refs[nw:2 * nw]
        send_sems, recv_sems = refs[2 * nw:]
        x, y, c = _my_place()

        def copy(w, q, owner_core):
            r = land[w].shape[1]
            return pltpu.make_async_remote_copy(
                src_ref=g[w].at[pl.ds(pl.multiple_of((2 * q + owner_core) * r, 16), r), :], dst_ref=land[w].at[q],
                send_sem=send_sems.at[w, q], recv_sem=recv_sems.at[w, q], device_id=(x, y, 1 - c), device_id_type=MESH)

        sends = [copy(w, q, 1 - c) for w in range(nw) for q in range(4)]
        for cp in sends:
            cp.start()
        for w in range(nw):
            for q in range(4):
                copy(w, q, c).wait_recv()
        for cp in sends:
            cp.wait_send()

    return pl.pallas_call(
        body, name="sibling_exchange",
        out_shape=[jax.ShapeDtypeStruct((4, a.shape[0] // N_DEV, a.shape[1]), a.dtype) for a in grads],
        in_specs=[HBM] * nw, out_specs=[HBM] * nw,
        scratch_shapes=[pltpu.SemaphoreType.DMA((nw, 4)), pltpu.SemaphoreType.DMA((nw, 4))],
    )(*grads)


def _pair_sums(gs, sibs, core):
    n = len(gs)

    def body(core_ref, *refs):
        for w in range(n):
            refs[2 * n + w][0] = (refs[w][0, 0].astype(F32) + refs[n + w][0].astype(F32)).astype(BF16)

    in_specs = [pl.BlockSpec((1, 1) + t.shape[1:], lambda q, core_ref: (q, core_ref[0], 0, 0)) for t in sibs]
    in_specs += [pl.BlockSpec((1,) + t.shape[1:], lambda q, core_ref: (q, 0, 0)) for t in sibs]
    return pl.pallas_call(
        body, name="pair_sums",
        grid_spec=pltpu.PrefetchScalarGridSpec(
            num_scalar_prefetch=1, grid=(4,), in_specs=in_specs,
            out_specs=[pl.BlockSpec((1,) + t.shape[1:], lambda q, core_ref: (q, 0, 0)) for t in sibs]),
        out_shape=[jax.ShapeDtypeStruct(t.shape, BF16) for t in sibs],
        compiler_params=_cparams("parallel"),
    )(core, *[g.reshape(4, 2, t.shape[1], t.shape[2]) for g, t in zip(gs, sibs)], *sibs)


def _own_rows_into_zones(shards, me):
    n = len(shards)

    def body(me_ref, *refs):
        for w in range(n):
            refs[2 * n + w][...] = refs[w][...]

    zones = [lax.empty((N_DEV * t.shape[0], t.shape[1]), t.dtype) for t in shards]
    return pl.pallas_call(
        body, name="own_rows_into_zones",
        grid_spec=pltpu.PrefetchScalarGridSpec(
            num_scalar_prefetch=1, grid=(1,),
            in_specs=[pl.BlockSpec(t.shape, lambda i, me_ref: (0, 0)) for t in shards] + [pl.BlockSpec(memory_space=pl.ANY)] * n,
            out_specs=[pl.BlockSpec(t.shape, lambda i, me_ref: (me_ref[0], 0)) for t in shards]),
        out_shape=[jax.ShapeDtypeStruct(z.shape, z.dtype) for z in zones],
        input_output_aliases={1 + n + w: w for w in range(n)},
        compiler_params=_cparams("arbitrary"),
    )(me, *shards, *zones)


def _allgather_small(v, name):
    r, ccols = v.shape

    def body(v_ref, out_ref, sum_ref, send_sems, recv_sems):
        x, y, c = _my_place()
        my_idx = _dev_index(x, y, c)
        out_ref[my_idx] = v_ref[...]

        def copy(m):
            peer = _peer(x, y, c, m)
            return pltpu.make_async_remote_copy(
                src_ref=v_ref, dst_ref=out_ref.at[my_idx],
                send_sem=send_sems.at[m - 1], recv_sem=recv_sems.at[m - 1], device_id=peer, device_id_type=MESH)

        def arrival(m):
            peer = _peer(x, y, c, m)
            return pltpu.make_async_remote_copy(
                src_ref=v_ref, dst_ref=out_ref.at[_dev_index(*peer)],
                send_sem=send_sems.at[m - 1], recv_sem=recv_sems.at[m - 1], device_id=peer, device_id_type=MESH)

        sends = [copy(m) for m in range(1, N_DEV)]
        for cp in sends:
            cp.start()
        for m in range(1, N_DEV):
            arrival(m).wait_recv()
        acc = out_ref[0]
        for s in range(1, N_DEV):
            acc = acc + out_ref[s]
        sum_ref[...] = acc
        for cp in sends:
            cp.wait_send()

    vmem = pl.BlockSpec(memory_space=pltpu.VMEM)
    return pl.pallas_call(
        body, name=name,
        out_shape=[jax.ShapeDtypeStruct((N_DEV, r, ccols), v.dtype), jax.ShapeDtypeStruct((r, ccols), v.dtype)],
        in_specs=[vmem], out_specs=[vmem, vmem],
        scratch_shapes=[pltpu.SemaphoreType.DMA((7,)), pltpu.SemaphoreType.DMA((7,))],
    )(v)


def _gather_w_in_and_condition(shard, pay, w_ada, b_cols):
    r, ccols = shard.shape
    ncol = w_ada.shape[1]

    def body(sh_ref, pay_ref, w_ref, b_ref, full_ref, got_ref, act_ref, mod_ref, send_sems, recv_sems, small_send, small_recv, local_sem):
        x, y, c = _my_place()
        me, sibling = (x, y, c), (x, y, 1 - c)
        my_idx = _dev_index(x, y, c)
        chips = [(1 - x, y), (x, 1 - y), (1 - x, 1 - y)]

        def small(rnd, buf, m, arriving=False):
            peer = _peer(x, y, c, m)
            slot = _dev_index(*peer) if arriving else my_idx
            return pltpu.make_async_remote_copy(
                src_ref=buf.at[my_idx], dst_ref=buf.at[slot], send_sem=small_send.at[rnd, m - 1],
                recv_sem=small_recv.at[rnd, m - 1], device_id=peer, device_id_type=MESH)

        def rows(px, py, pc):
            return full_ref.at[pl.ds(pl.multiple_of(_dev_index(px, py, pc) * r, 16), r), :]

        def copy(k, block, to, src=None):
            return pltpu.make_async_remote_copy(
                src_ref=rows(*block) if src is None else src, dst_ref=rows(*block),
                send_sem=send_sems.at[k], recv_sem=recv_sems.at[k], device_id=to, device_id_type=MESH)

        got_ref[my_idx] = pay_ref[...]
        round1 = [small(0, got_ref, m) for m in range(1, N_DEV)]
        for cp in round1:
            cp.start()
        mine = pltpu.make_async_copy(sh_ref, rows(*me), local_sem)
        mine.start()
        first = [copy(0, me, sibling, src=sh_ref)] + [copy(1 + j, me, (*chip, c), src=sh_ref) for j, chip in enumerate(chips)]
        for cp in first:
            cp.start()

        for m in range(1, N_DEV):
            small(0, got_ref, m, arriving=True).wait_recv()
        cv = jnp.concatenate([got_ref[s, 0:1, :] for s in range(N_DEV)], axis=0)
        act = cv * _sigmoid(cv)
        act_ref[...] = act
        mod_ref[my_idx] = jnp.dot(act, w_ref[...], preferred_element_type=F32, precision=lax.Precision.HIGHEST) + b_ref[...]
        round2 = [small(1, mod_ref, m) for m in range(1, N_DEV)]
        for cp in round2:
            cp.start()

        passed = []
        for j, chip in enumerate(chips):
            copy(1 + j, (*chip, c), me).wait_recv()
            fwd = copy(4 + j, (*chip, c), sibling)
            fwd.start()
            passed.append(fwd)
        copy(0, sibling, me).wait_recv()
        for j, chip in enumerate(chips):
            copy(4 + j, (*chip, 1 - c), me).wait_recv()
        for m in range(1, N_DEV):
            small(1, mod_ref, m, arriving=True).wait_recv()
        for cp in first + passed + round1 + round2:
            cp.wait_send()
        mine.wait()

    anyspec = pl.BlockSpec(memory_space=pl.ANY)
    vmem = pl.BlockSpec(memory_space=pltpu.VMEM)
    return pl.pallas_call(
        body, name="gather_w_in_and_condition",
        out_shape=[jax.ShapeDtypeStruct((N_DEV * r, ccols), shard.dtype), jax.ShapeDtypeStruct((N_DEV, 8, D), F32),
                   jax.ShapeDtypeStruct((N_DEV, D), F32), jax.ShapeDtypeStruct((N_DEV, N_DEV, ncol), F32)],
        in_specs=[anyspec, vmem, vmem, vmem], out_specs=[anyspec, vmem, vmem, vmem],
        scratch_shapes=[pltpu.SemaphoreType.DMA((7,)), pltpu.SemaphoreType.DMA((7,)), pltpu.SemaphoreType.DMA((2, 7)),
                        pltpu.SemaphoreType.DMA((2, 7)), pltpu.SemaphoreType.DMA],
        compiler_params=_cparams(),
    )(shard, pay, w_ada, b_cols)


def _row_tile(r):
    for t in (256, 304, 128, 64, 16):
        if r % t == 0:
            return t
    return r


def _adamw_w_ada(w, act_t, gm_cols, m, v):
    r, ccols = w.shape
    tr = _row_tile(r)
    c1 = 1.0 / (1.0 - B1 ** STEP)
    c2 = 1.0 / (1.0 - B2 ** STEP)

    def body(w_ref, a_ref, gm_ref, m_ref, v_ref, g_ref, d_ref, nm_ref, nv_ref):
        gv = jnp.dot(a_ref[...], gm_ref[...], preferred_element_type=F32, precision=lax.Precision.HIGHEST)
        g_ref[...] = gv
        nm = B1 * m_ref[...] + (1.0 - B1) * gv
        nv = B2 * v_ref[...] + (1.0 - B2) * jnp.square(gv)
        nm_ref[...] = nm
        nv_ref[...] = nv
        d_ref[...] = -LR * ((nm * c1) / (jnp.sqrt(nv * c2) + ADAM_EPS) + WD * w_ref[...])

    blk = pl.BlockSpec((tr, ccols), lambda i: (i, 0))
    return pl.pallas_call(
        body, name="adamw_w_ada", grid=(r // tr,),
        in_specs=[blk, pl.BlockSpec((tr, N_DEV), lambda i: (i, 0)), _const_spec(gm_cols.shape), blk, blk], out_specs=[blk] * 4,
        out_shape=[jax.ShapeDtypeStruct((r, ccols), F32)] * 4,
        compiler_params=_cparams("parallel"),
    )(w, act_t, gm_cols, m, v)


def _sum_adamw(parts, own, slot, w, m, v, name, transposed=False):
    k, r, ccols = parts.shape
    tr = _row_tile(r)
    c1 = 1.0 / (1.0 - B1 ** STEP)
    c2 = 1.0 / (1.0 - B2 ** STEP)

    def body(s_ref, p_ref, own_ref, w_ref, m_ref, v_ref, g_ref, d_ref, nm_ref, nv_ref):
        gv = own_ref[0].astype(F32)
        for s in range(k):
            gv = gv + p_ref[s].astype(F32)
        if transposed:
            gv = gv.T
        g_ref[...] = gv
        nm = B1 * m_ref[...] + (1.0 - B1) * gv
        nv = B2 * v_ref[...] + (1.0 - B2) * jnp.square(gv)
        nm_ref[...] = nm
        nv_ref[...] = nv
        d_ref[...] = -LR * ((nm * c1) / (jnp.sqrt(nv * c2) + ADAM_EPS) + WD * w_ref[...])

    if transposed:
        blk = pl.BlockSpec((ccols, tr), lambda i, s_ref: (0, i))
    else:
        blk = pl.BlockSpec((tr, ccols), lambda i, s_ref: (i, 0))
    return pl.pallas_call(
        body, name=name,
        grid_spec=pltpu.PrefetchScalarGridSpec(
            num_scalar_prefetch=1, grid=(r // tr,),
            in_specs=[pl.BlockSpec((k, tr, ccols), lambda i, s_ref: (0, i, 0)),
                      pl.BlockSpec((1, tr, ccols), lambda i, s_ref: (s_ref[0], i, 0))] + [blk] * 3,
            out_specs=[blk] * 4),
        out_shape=[jax.ShapeDtypeStruct(w.shape, F32)] * 4,
        compiler_params=_cparams("parallel"),
    )(slot, parts, own, w, m, v)


VEC_ROWS = ((0, 6), (6, 7), (9, 11), (11, 14), (7, 8), (8, 9))


def _adamw_vectors(w, g, m, v):
    c1 = 1.0 / (1.0 - B1 ** STEP)
    c2 = 1.0 / (1.0 - B2 ** STEP)

    def put(refs, p):
        for ref, (lo, hi) in zip(refs, VEC_ROWS):
            if ref.shape == (3, HEAD):
                ref[...] = p[lo:hi, :HEAD]
            else:
                ref[...] = jnp.concatenate([p[k:k + 1] for k in range(lo, hi)], axis=1)

    def body(w_ref, g_ref, m_ref, v_ref, *outs):
        gv = g_ref[...]
        nm = B1 * m_ref[...] + (1.0 - B1) * gv
        nv = B2 * v_ref[...] + (1.0 - B2) * jnp.square(gv)
        delta = -LR * ((nm * c1) / (jnp.sqrt(nv * c2) + ADAM_EPS) + WD * w_ref[...])
        for kind, p in enumerate((gv, delta, nm, nv)):
            put(outs[6 * kind:6 * kind + 6], p)

    shapes = [(1, 6 * D), (1, D), (1, 2 * D), (3, HEAD), (1, D), (1, D)]
    out = pl.pallas_call(
        body, name="adamw_vectors", out_shape=[jax.ShapeDtypeStruct(sh, F32) for sh in shapes] * 4, compiler_params=_cparams(),
    )(w, g, m, v)
    fix = lambda t: (t[0], t[1], t[2], t[3][None], t[4], t[5].reshape(D))
    return [fix(out[6 * kind:6 * kind + 6]) for kind in range(4)]


def _pack_vectors(b_ada, g_mix, g_mlp, g_fin, b_gate, conv_w):
    conv_rows = jnp.pad(conv_w.reshape(3, HEAD), ((0, 0), (0, D - HEAD)))
    return jnp.concatenate([b_ada.reshape(6, D), g_mix.reshape(1, D), g_mlp.reshape(1, D), g_fin.reshape(1, D),
                            b_gate.reshape(2, D), conv_rows, jnp.zeros((2, D), F32)], axis=0)


def kernel(x, c, w_ada, b_ada, g_norm_mix, w_in, b_gate, conv_w, w_branch_attn, w_branch_conv, w_out, g_norm_mlp, w_mlp_in, w_mlp_out, g_norm_final, loss_target, m_w_ada, m_b_ada, m_g_norm_mix, m_w_in, m_b_gate, m_conv_w, m_w_branch_attn, m_w_branch_conv, m_w_out, m_g_norm_mlp, m_w_mlp_in, m_w_mlp_out, m_g_norm_final, v_w_ada, v_b_ada, v_g_norm_mix, v_w_in, v_b_gate, v_conv_w, v_w_branch_attn, v_w_branch_conv, v_w_out, v_g_norm_mlp, v_w_mlp_in, v_w_mlp_out, v_g_norm_final):
    S = x.shape[1]
    xi, yi, ci = _my_place()
    me = _dev_index(xi, yi, ci)
    x2 = x.reshape(S, D)
    tgt = loss_target.reshape(S, D)

    pay = jnp.zeros((8, D), F32).at[0].set(c[0]).at[1:4, :HEAD].set(conv_w[0])
    ncol = w_ada.shape[2]
    b_cols = lax.dynamic_slice(b_ada, (0, me * ncol), (1, ncol))
    w_int, got, act, mod_all = _gather_w_in_and_condition(w_in[0].T.astype(BF16), pay, w_ada[0], b_cols)
    cw8 = jnp.pad(got[:, 1:4, :HEAD].transpose(1, 0, 2).reshape(3, D), ((0, 5), (0, 0)))
    mod = lax.dynamic_index_in_dim(mod_all, me, axis=1, keepdims=False).reshape(6, D)
    late = [w_branch_attn[0].T.astype(BF16), w_branch_conv[0].astype(BF16), w_out[0].astype(BF16),
            w_mlp_in[0].T.astype(BF16), w_mlp_out[0].astype(BF16)]
    w_int, late = lax.optimization_barrier((w_int, late))
    zones = _own_rows_into_zones(late, me.reshape(1).astype(jnp.int32))
    ag_mix = _split_start("gather_mix_start", "gather", late[:3], zones[:3])
    ag_mlp = _split_start("gather_mlp_start", "gather", late[3:], zones[3:])
    mod = mod + ag_mix[4] + ag_mlp[4]
    h = _prenorm(x2, g_norm_mix, mod[1:2], mod[0:1])

    def mix_weights(o_attn):
        return _split_wait("gather_mix_wait", "gather", *ag_mix[:4], o_attn)[1]

    def mlp_weights(x1):
        return _split_wait("gather_mlp_wait", "gather", *ag_mlp[:4], x1)[1]

    rs = {}

    def mlp_grads_ready(*grads):
        lands = [lax.empty((N_PEER, t.shape[0] // N_DEV, t.shape[1]), BF16) for t in grads]
        rs["mlp"] = _split_start("scatter_mlp_start", "scatter", grads, lands)
        return rs["mlp"][4]

    def w_in_grad_ready(g_in):
        r = g_in.shape[0] // N_DEV
        rs["sib"] = _split_start("sibling_w_in_start", "sibling", [g_in], [lax.empty((4, r, g_in.shape[1]), BF16)])
        return rs["sib"][5]

    def other_grads_ready(*small):
        core = ci.reshape(1).astype(jnp.int32)
        (g_in,), (sib_in,) = _split_wait("sibling_w_in_wait", "sibling", *rs["sib"][:4], small[0])
        pair = _pair_sums([g_in, *small], [sib_in, *_sibling_exchange(small)], core)
        lands = [lax.empty((3,) + t.shape[1:], BF16) for t in pair]
        rs["rest"] = _split_start("scatter_rest_start", "chips", pair, lands)
        return rs["rest"][4]

    ba, bb = b_gate[:, :D], b_gate[:, D:]
    grad_x, vec = _local_step(
        x2, h, tgt, mod, g_norm_mix, g_norm_mlp, g_norm_final.reshape(1, D), ba, bb, cw8, w_int, mix_weights, mlp_weights,
        mlp_grads_ready, w_in_grad_ready, other_grads_ready)

    vec_all, vec_sum = _allgather_small(vec, "gather_vec")
    loss = vec_sum[14, 0]
    gm_all = vec_all[:, 0:6, :].reshape(N_DEV, 6 * D)
    gm_cols = lax.dynamic_slice(gm_all, (0, me * ncol), (N_DEV, ncol))
    conv_cols = lax.dynamic_slice(vec_sum[11:14], (0, me * HEAD), (3, HEAD))
    g_pack = jnp.concatenate([vec_sum[0:11], jnp.pad(conv_cols, ((0, 0), (0, D - HEAD))), jnp.zeros((2, D), F32)], axis=0)
    packs = [_pack_vectors(*t) for t in ((b_ada, g_norm_mix, g_norm_mlp, g_norm_final, b_gate, conv_w),
                                         (m_b_ada, m_g_norm_mix, m_g_norm_mlp, m_g_norm_final, m_b_gate, m_conv_w),
                                         (v_b_ada, v_g_norm_mix, v_g_norm_mlp, v_g_norm_final, v_b_gate, v_conv_w))]
    gv, dv, mv, vv = _adamw_vectors(packs[0], g_pack, packs[1], packs[2])
    g_w_ada, d_ada, nm_ada, nv_ada = _adamw_w_ada(w_ada[0], act.T, gm_cols, m_w_ada[0], v_w_ada[0])

    big = {}
    srcs, lands = _split_wait("scatter_mlp_wait", "scatter", *rs["mlp"][:4], d_ada)
    own = [g.reshape((N_DEV,) + land.shape[1:]) for g, land in zip(srcs, lands)]
    slot = me.reshape(1).astype(jnp.int32)
    big["w_mi"] = tuple(t[None] for t in _sum_adamw(lands[0], own[0], slot, w_mlp_in[0], m_w_mlp_in[0], v_w_mlp_in[0], "adamw_w_mi",
                                                    transposed=True))
    big["w_mo"] = tuple(t[None] for t in _sum_adamw(lands[1], own[1], slot, w_mlp_out[0], m_w_mlp_out[0], v_w_mlp_out[0], "adamw_w_mo"))
    own, lands = _split_wait("scatter_rest_wait", "chips", *rs["rest"][:4], big["w_mo"][1])
    slot = (2 * xi + yi).reshape(1).astype(jnp.int32)
    big["w_in"] = tuple(t.T[None] for t in _sum_adamw(lands[0], own[0], slot, w_in[0].T, m_w_in[0].T, v_w_in[0].T, "adamw_w_in"))
    big["w_ba"] = tuple(t[None] for t in _sum_adamw(lands[1], own[1], slot, w_branch_attn[0], m_w_branch_attn[0], v_w_branch_attn[0],
                                                    "adamw_w_ba", transposed=True))
    big["w_bc"] = tuple(t[None] for t in _sum_adamw(lands[2], own[2], slot, w_branch_conv[0], m_w_branch_conv[0], v_w_branch_conv[0], "adamw_w_bc"))
    big["w_out"] = tuple(t[None] for t in _sum_adamw(lands[3], own[3], slot, w_out[0], m_w_out[0], v_w_out[0], "adamw_w_out"))

    def ordered(k, ada, vecs):
        return (ada[None], vecs[0], vecs[1], big["w_in"][k], vecs[2], vecs[3], big["w_ba"][k], big["w_bc"][k],
                big["w_out"][k], vecs[4], big["w_mi"][k], big["w_mo"][k], vecs[5])

    return (loss, grad_x.reshape(1, S, D), *ordered(0, g_w_ada, gv), *ordered(1, d_ada, dv),
            *ordered(2, nm_ada, mv), *ordered(3, nv_ada, vv))
```

```python
import numpy as np
import jax
import jax.numpy as jnp
from jax import lax
from jax.experimental import pallas as pl
from jax.experimental.pallas import tpu as pltpu

F32, BF16 = jnp.float32, jnp.bfloat16
D = 1024
HEAD = 128
DILATIONS = (1, 4, 16)
N_SLOT = 4
AOW = N_SLOT * HEAD
DFF = 4 * D
N_DEV = 8
UNROLL = 16
EPS = 1e-6
NEG = -1e30
SCALE = HEAD ** -0.5
LR, B1, B2, ADAM_EPS, WD, STEP = 0.001, 0.9, 0.999, 1e-08, 0.01, 10
V7X_VMEM_LIMIT = 56 * 1024 * 1024
TM = 1024
MESH = pl.DeviceIdType.MESH


def _cparams(*sem):
    if sem:
        return pltpu.CompilerParams(dimension_semantics=sem, vmem_limit_bytes=V7X_VMEM_LIMIT)
    return pltpu.CompilerParams(vmem_limit_bytes=V7X_VMEM_LIMIT)


def _nn(a, b):
    return jnp.dot(a, b, preferred_element_type=F32)


def _nt(a, b):
    return lax.dot_general(a, b, (((1,), (1,)), ((), ())), preferred_element_type=F32)


def _tn(a, b):
    return lax.dot_general(a, b, (((0,), (0,)), ((), ())), preferred_element_type=F32)


def _rms_r(x):
    return lax.rsqrt(jnp.mean(x * x, axis=-1, keepdims=True) + EPS)


def _rms_bwd(x, r, g, dn):
    gy = dn * g
    dx = r * gy - x * (r * r * r) * jnp.mean(x * gy, axis=-1, keepdims=True)
    return dx, dn * (x * r)


def _sigmoid(t):
    return 1.0 / (1.0 + jnp.exp(-t))


def _rowsum(v):
    return jnp.sum(v, axis=0, keepdims=True)


def _vec_spec(n=D):
    return pl.BlockSpec((1, n), lambda *_: (0, 0))


def _const_spec(shape):
    nd = len(shape)
    return pl.BlockSpec(shape, lambda *_: (0,) * nd)


def _win_rowblock(j):
    return jnp.where(j < 9, (j % 3) * 3 + j // 3, j)


def _prenorm(x, g, sc, sh):
    S = x.shape[0]
    tm = TM

    def body(x_ref, g_ref, sc_ref, sh_ref, h_ref):
        xv = x_ref[...]
        h_ref[...] = (xv * _rms_r(xv) * g_ref[...] * (1.0 + sc_ref[...]) + sh_ref[...]).astype(BF16)

    row = pl.BlockSpec((tm, D), lambda i: (i, 0))
    return pl.pallas_call(
        body, name="prenorm", grid=(S // tm,), in_specs=[row, _vec_spec(), _vec_spec(), _vec_spec()], out_specs=row,
        out_shape=jax.ShapeDtypeStruct((S, D), BF16), compiler_params=_cparams("parallel"),
    )(x, g, sc, sh)


def _proj(h, w_int):
    S = h.shape[0]

    def body(h_ref, w_ref, q_ref, e_ref):
        j = pl.program_id(0)
        acc = _nt(h_ref[...], w_ref[...])

        @pl.when(j < 9)
        def _():
            q_ref[0] = acc

        @pl.when(j >= 9)
        def _():
            e_ref[0] = acc.astype(BF16)

    def e_idx(j):
        k = jnp.maximum(j - 9, 0)
        return (k // 2, 0, k % 2)

    return pl.pallas_call(
        body, name="proj", grid=(19,),
        in_specs=[pl.BlockSpec((S, D), lambda j: (0, 0), pipeline_mode=pl.Buffered(1)),
                  pl.BlockSpec((512, D), lambda j: (_win_rowblock(j), 0))],
        out_specs=[pl.BlockSpec((1, S, 512), lambda j: (jnp.minimum(j, 8), 0, 0)), pl.BlockSpec((1, S, 512), e_idx)],
        out_shape=[jax.ShapeDtypeStruct((9, S, 512), F32), jax.ShapeDtypeStruct((5, S, D), BF16)],
        compiler_params=_cparams("arbitrary"),
    )(h, w_int)


def _bias_table():
    slopes = (2.0 ** (-8.0 * np.arange(1, 13, dtype=np.float32) / 12.0)).astype(np.float32)
    qi = np.arange(HEAD)[:, None]
    kj = np.arange(2 * HEAD)[None, :]
    delta = HEAD + qi - kj
    mask = (delta >= 0) & (delta <= HEAD)
    out = np.zeros((3, N_SLOT, HEAD, 2 * HEAD), np.float32)
    for gi, d in enumerate(DILATIONS):
        for j in range(N_SLOT):
            bias = -slopes[gi * N_SLOT + j] * (delta * d).astype(np.float32)
            out[gi, j] = np.where(mask, bias, NEG)
    out_t = np.concatenate([out[..., HEAD:].swapaxes(-1, -2), out[..., :HEAD].swapaxes(-1, -2)], axis=-1)
    return jnp.asarray(out), jnp.asarray(out_t)


def _attn_fwd(qkv, bias):
    S = qkv.shape[2]
    nblk = S // HEAD
    rows = 256

    def body(qkv_ref, b_ref, o_ref, lse_ref, o_s, lse_s):
        g = pl.program_id(1)
        bias = b_ref[0, 0]
        col = lax.broadcasted_iota(jnp.int32, bias.shape, 1)
        bias_first = jnp.where(col < HEAD, NEG, bias)

        for gi, d in enumerate(DILATIONS):
            @pl.when(g == gi)
            def _(gi=gi, d=d):
                nb = nblk // d

                def keys(start):
                    sl = pl.ds(start, HEAD, stride=d)
                    return qkv_ref.at[0, 1][sl, :].astype(BF16), qkv_ref.at[0, 2][sl, :].astype(BF16)

                def step(b, first_of_residue, before):
                    r, n = b // nb, b % nb
                    cur = pl.ds(n * (HEAD * d) + r, HEAD, stride=d)
                    own = keys(n * (HEAD * d) + r)
                    if first_of_residue:
                        before = own
                    q = qkv_ref.at[0, 0][cur, :].astype(BF16)
                    kw = jnp.concatenate([before[0], own[0]], axis=0)
                    vw = jnp.concatenate([before[1], own[1]], axis=0)
                    s = _nt(q, kw) * SCALE + jnp.where(n > 0, bias, bias_first)
                    m = jnp.max(s, axis=-1, keepdims=True)
                    p = jnp.exp(s - m)
                    l = jnp.sum(p, axis=-1, keepdims=True)
                    o_s.at[gi][cur, :] = _nn(p.astype(BF16), vw) / l
                    lse_s.at[gi][cur, :] = jnp.broadcast_to(m + jnp.log(l), (HEAD, HEAD))
                    return own

                def steps(i, before):
                    for u in range(UNROLL):
                        before = step(i * UNROLL + u, nb <= UNROLL and u % nb == 0, before)
                    return before

                lax.fori_loop(0, nblk // UNROLL, steps, keys(0))

        @pl.when(g == len(DILATIONS) - 1)
        def _():
            def merge(i, carry):
                r = pl.ds(pl.multiple_of(i * rows, rows), rows)
                ls = [lse_s[k, r, :] for k in range(3)]
                top = jnp.maximum(jnp.maximum(ls[0], ls[1]), ls[2])
                ws = [jnp.exp(t - top) for t in ls]
                den = ws[0] + ws[1] + ws[2]
                o_ref[r, :] = (ws[0] * o_s[0, r, :] + ws[1] * o_s[1, r, :] + ws[2] * o_s[2, r, :]) / den
                lse_ref[r, :] = top + jnp.log(den)
                return carry

            lax.fori_loop(0, S // rows, merge, 0)

    return pl.pallas_call(
        body, name="attn_fwd", grid=(N_SLOT, 3),
        in_specs=[pl.BlockSpec((1, 3, S, HEAD), lambda j, g: (g, 0, 0, j)),
                  pl.BlockSpec((1, 1, HEAD, 2 * HEAD), lambda j, g: (g, j, 0, 0))],
        out_specs=[pl.BlockSpec((S, HEAD), lambda j, g: (0, j)), pl.BlockSpec((S, HEAD), lambda j, g: (0, j))],
        out_shape=[jax.ShapeDtypeStruct((S, AOW), F32), jax.ShapeDtypeStruct((S, AOW), F32)],
        scratch_shapes=[pltpu.VMEM((3, S, HEAD), F32)] * 2,
        compiler_params=_cparams("parallel", "arbitrary"),
    )(qkv, bias)


def _shift_down(z, k, halo_rows):
    out = pltpu.roll(z, k, axis=0)
    top = out[:8]
    rid = lax.broadcasted_iota(jnp.int32, top.shape, 0)
    for t in range(k):
        top = jnp.where(rid == t, halo_rows[t], top)
    return jnp.concatenate([top, out[8:]], axis=0)


def _shift_up(z, k, halo_rows):
    n = z.shape[0]
    out = pltpu.roll(z, n - k, axis=0)
    bottom = out[n - 8:]
    rid = lax.broadcasted_iota(jnp.int32, bottom.shape, 0)
    for t in range(k):
        bottom = jnp.where(rid == 8 - k + t, halo_rows[t], bottom)
    return jnp.concatenate([out[:n - 8], bottom], axis=0)


def _e_spec(chunk, tm):
    return pl.BlockSpec((1, tm, D), lambda i, c=chunk: (c, i, 0))


def _e_prev_spec(chunk, tm):
    return pl.BlockSpec((1, 16, D), lambda i, c=chunk: (c, jnp.maximum(i * (tm // 16) - 1, 0), 0))


def _e_next_spec(chunk, tm, S):
    return pl.BlockSpec((1, 16, D), lambda i, c=chunk: (c, jnp.minimum((i + 1) * (tm // 16), S // 16 - 1), 0))


def _mix(o_attn, e, cw8, ba, bb, w_bat, w_bc):
    S = o_attn.shape[0]
    tm = 512

    def body(o_ref, cb_ref, cc_ref, cx_ref, ga_ref, gb_ref, ccp_ref, cxp_ref, cw_ref, ba_ref, bb_ref, wba_ref, wbc_ref,
             obf_ref, cbu_ref, ya_ref, yc_ref, mg_ref):
        i = pl.program_id(0)
        o = o_ref[...].astype(BF16)
        obf_ref[...] = o
        ya = _nt(o, wba_ref[...])
        z = cc_ref[0].astype(F32) * cx_ref[0].astype(F32)
        zp = ccp_ref[0].astype(F32) * cxp_ref[0].astype(F32) * (i > 0).astype(F32)
        z1 = _shift_down(z, 1, [zp[15:16]])
        z2 = _shift_down(z, 2, [zp[14:15], zp[15:16]])
        cw = cw_ref[...]
        u = cw[0:1] * z2 + cw[1:2] * z1 + cw[2:3] * z
        cbu = (cb_ref[0].astype(F32) * u).astype(BF16)
        cbu_ref[...] = cbu
        yc = _nn(cbu, wbc_ref[...])
        sa = _sigmoid(ga_ref[0].astype(F32) + ba_ref[...])
        sb = _sigmoid(gb_ref[0].astype(F32) + bb_ref[...])
        ya_ref[...] = ya.astype(BF16)
        yc_ref[...] = yc.astype(BF16)
        mg_ref[...] = (sa * ya + sb * yc).astype(BF16)

    row = lambda w: pl.BlockSpec((tm, w), lambda i: (i, 0))
    return pl.pallas_call(
        body, name="mix", grid=(S // tm,),
        in_specs=[row(AOW)] + [_e_spec(c, tm) for c in range(5)] + [_e_prev_spec(1, tm), _e_prev_spec(2, tm),
                  _const_spec((8, D)), _vec_spec(), _vec_spec(), _const_spec((D, AOW)), _const_spec((D, D))],
        out_specs=[row(AOW), row(D), row(D), row(D), row(D)],
        out_shape=[jax.ShapeDtypeStruct((S, AOW), BF16)] + [jax.ShapeDtypeStruct((S, D), BF16)] * 4,
        compiler_params=_cparams("parallel"),
    )(o_attn, e, e, e, e, e, e, e, cw8, ba, bb, w_bat, w_bc)


def _out_proj(merged, w_out, x, gate1, g_mlp, sc2, sh2):
    S = x.shape[0]
    tm = TM

    def body(mg_ref, w_ref, x_ref, gt_ref, g_ref, sc_ref, sh_ref, x1_ref, mo_ref, h2_ref):
        mo = _nn(mg_ref[...], w_ref[...])
        mo_ref[...] = mo.astype(BF16)
        x1 = x_ref[...] + gt_ref[...] * mo
        x1_ref[...] = x1
        h2 = x1 * _rms_r(x1) * g_ref[...] * (1.0 + sc_ref[...]) + sh_ref[...]
        h2_ref[...] = h2.astype(BF16)

    row = pl.BlockSpec((tm, D), lambda i: (i, 0))
    return pl.pallas_call(
        body, name="out_proj", grid=(S // tm,),
        in_specs=[row, _const_spec((D, D)), row, _vec_spec(), _vec_spec(), _vec_spec(), _vec_spec()],
        out_specs=[row, row, row],
        out_shape=[jax.ShapeDtypeStruct((S, D), F32), jax.ShapeDtypeStruct((S, D), BF16), jax.ShapeDtypeStruct((S, D), BF16)],
        compiler_params=_cparams("parallel"),
    )(merged, w_out, x, gate1, g_mlp, sc2, sh2)


def _mlp_in(h2, w_mit):
    S = h2.shape[0]
    tm, tn = TM, 2048

    def body(h_ref, w_ref, a_ref, f_ref):
        a = _nt(h_ref[...], w_ref[...])
        a_ref[...] = a.astype(BF16)
        f_ref[...] = jnp.square(jnp.maximum(a, 0.0)).astype(BF16)

    blk = pl.BlockSpec((tm, tn), lambda i, j: (i, j))
    return pl.pallas_call(
        body, name="mlp_in", grid=(S // tm, DFF // tn),
        in_specs=[pl.BlockSpec((tm, D), lambda i, j: (i, 0)), pl.BlockSpec((tn, D), lambda i, j: (j, 0))],
        out_specs=[blk, blk],
        out_shape=[jax.ShapeDtypeStruct((S, DFF), BF16)] * 2,
        compiler_params=_cparams("parallel", "parallel"),
    )(h2, w_mit)


def _mlp_out(f, w_mo, x1, gate2, g_fin, tgt):
    S = x1.shape[0]
    tm = 512
    half = tm // 2

    def body(f_ref, w_ref, x1_ref, gt_ref, g_ref, t_ref, mlp_ref, dx2_ref, pv_ref):
        @pl.when(pl.program_id(0) == 0)
        def _():
            pv_ref[...] = jnp.zeros_like(pv_ref)

        g = g_ref[...]
        for hs in (pl.ds(0, half), pl.ds(half, half)):
            mlp = _nn(f_ref[hs, :], w_ref[...])
            mlp_ref[hs, :] = mlp.astype(BF16)
            x2 = x1_ref[hs, :] + gt_ref[...] * mlp
            r = _rms_r(x2)
            err = x2 * r * g - t_ref[hs, :]
            dx2, pg = _rms_bwd(x2, r, g, err * (1.0 / D))
            dx2_ref[hs, :] = dx2
            pv_ref[0:1, :] += _rowsum(pg)
            pv_ref[1:2, :] += 0.5 * _rowsum(jnp.mean(err * err, axis=-1, keepdims=True))

    row = pl.BlockSpec((tm, D), lambda i: (i, 0))
    return pl.pallas_call(
        body, name="mlp_out", grid=(S // tm,),
        in_specs=[pl.BlockSpec((tm, DFF), lambda i: (i, 0)), _const_spec((DFF, D)), row, _vec_spec(), _vec_spec(), row],
        out_specs=[row, row, _const_spec((8, D))],
        out_shape=[jax.ShapeDtypeStruct((S, D), BF16), jax.ShapeDtypeStruct((S, D), F32), jax.ShapeDtypeStruct((8, D), F32)],
        compiler_params=_cparams("arbitrary"),
    )(f, w_mo, x1, gate2, g_fin, tgt)


def _bwd_mlp_a(dx2, gate2, mlp, w_mo, a):
    S = dx2.shape[0]
    tm = 512
    half = tm // 2

    def body(dx_ref, gt_ref, mlp_ref, w_ref, a_ref, da_ref, dmo_ref, pv_ref):
        @pl.when(pl.program_id(0) == 0)
        def _():
            pv_ref[...] = jnp.zeros_like(pv_ref)

        for hs in (pl.ds(0, half), pl.ds(half, half)):
            dx = dx_ref[hs, :]
            dmo = (dx * gt_ref[...]).astype(BF16)
            dmo_ref[hs, :] = dmo
            pv_ref[0:1, :] += _rowsum(dx * mlp_ref[hs, :].astype(F32))
            df = _nt(dmo, w_ref[...])
            da_ref[hs, :] = (df * (2.0 * jnp.maximum(a_ref[hs, :].astype(F32), 0.0))).astype(BF16)

    row = pl.BlockSpec((tm, D), lambda i: (i, 0))
    wide = pl.BlockSpec((tm, DFF), lambda i: (i, 0))
    return pl.pallas_call(
        body, name="bwd_mlp_a", grid=(S // tm,),
        in_specs=[row, _vec_spec(), row, _const_spec((DFF, D)), wide],
        out_specs=[wide, row, _const_spec((8, D))],
        out_shape=[jax.ShapeDtypeStruct((S, DFF), BF16), jax.ShapeDtypeStruct((S, D), BF16), jax.ShapeDtypeStruct((8, D), F32)],
        compiler_params=_cparams("arbitrary"),
    )(dx2, gate2, mlp, w_mo, a)


def _bwd_mlp_b(da, w_mit, x1, dx2, g_mlp, sc2):
    S = x1.shape[0]
    tm = 512
    half = tm // 2

    def body(da_ref, w_ref, x1_ref, dx2_ref, g_ref, sc_ref, dx1_ref, pv_ref):
        @pl.when(pl.program_id(0) == 0)
        def _():
            pv_ref[...] = jnp.zeros_like(pv_ref)

        g = g_ref[...]
        for hs in (pl.ds(0, half), pl.ds(half, half)):
            dh = _nn(da_ref[hs, :], w_ref[...])
            x1 = x1_ref[hs, :]
            r = _rms_r(x1)
            dxn, pg = _rms_bwd(x1, r, g, dh * (1.0 + sc_ref[...]))
            dx1_ref[hs, :] = dx2_ref[hs, :] + dxn
            pv_ref[0:1, :] += _rowsum(dh)
            pv_ref[1:2, :] += _rowsum(dh * (x1 * r * g))
            pv_ref[2:3, :] += _rowsum(pg)

    row = pl.BlockSpec((tm, D), lambda i: (i, 0))
    return pl.pallas_call(
        body, name="bwd_mlp_b", grid=(S // tm,),
        in_specs=[pl.BlockSpec((tm, DFF), lambda i: (i, 0)), _const_spec((DFF, D)), row, row, _vec_spec(), _vec_spec()],
        out_specs=[row, _const_spec((8, D))],
        out_shape=[jax.ShapeDtypeStruct((S, D), F32), jax.ShapeDtypeStruct((8, D), F32)],
        compiler_params=_cparams("arbitrary"),
    )(da, w_mit, x1, dx2, g_mlp, sc2)


def _bwd_mix(dx1, gate1, mo, e, cw8, ba, bb, ya, yc, o_attn, w_out, w_bc, w_bat):
    S = dx1.shape[0]
    tm = 256
    n_tiles = S // tm

    def body(dx_ref, dxn_ref, gt_ref, mo_ref, cb_ref, cc_ref, cx_ref, ga_ref, gb_ref, cbn_ref, gbn_ref, ccp_ref, cxp_ref,
             cw_ref, ba_ref, bb_ref, ya_ref, yc_ref, o_ref, wout_ref, wbc_ref, wba_ref,
             dmo_ref, dya_ref, dyc_ref, do_ref, dl_ref, de_ref, pv_ref):
        i = pl.program_id(0)

        @pl.when(i == 0)
        def _():
            pv_ref[...] = jnp.zeros_like(pv_ref)

        dx = dx_ref[...]
        cb = cb_ref[0].astype(F32)
        cc = cc_ref[0].astype(F32)
        cx = cx_ref[0].astype(F32)
        dmo_all = (jnp.concatenate([dx, dxn_ref[...]], axis=0) * gt_ref[...]).astype(BF16)
        dmg_all = _nt(dmo_all, wout_ref[...])
        sb_all = _sigmoid(jnp.concatenate([gb_ref[0], gbn_ref[0]], axis=0).astype(F32) + bb_ref[...])
        dyc_all = dmg_all * sb_all
        dcbu_all = _nt(dyc_all.astype(BF16), wbc_ref[...])
        dmo, dmg, sb, dyc, dcbu = dmo_all[:tm], dmg_all[:tm], sb_all[:tm], dyc_all[:tm], dcbu_all[:tm]
        dmo_ref[...] = dmo
        pv_ref[0:1, :] += _rowsum(dx * mo_ref[...].astype(F32))
        sa = _sigmoid(ga_ref[0].astype(F32) + ba_ref[...])
        dya = (dmg * sa).astype(BF16)
        dya_ref[...] = dya
        dyc_ref[...] = dyc.astype(BF16)
        dga = dmg * ya_ref[...].astype(F32) * sa * (1.0 - sa)
        dgb = dmg * yc_ref[...].astype(F32) * sb * (1.0 - sb)
        pv_ref[1:2, :] += _rowsum(dga)
        pv_ref[2:3, :] += _rowsum(dgb)

        do = _nn(dya, wba_ref[...])
        do_ref[...] = do
        prod = do * o_ref[...]
        dl_ref[...] = jnp.concatenate(
            [jnp.broadcast_to(jnp.sum(prod[:, s * HEAD:(s + 1) * HEAD], axis=-1, keepdims=True), (tm, HEAD))
             for s in range(N_SLOT)], axis=1)

        z = cc * cx
        zp = ccp_ref[0].astype(F32) * cxp_ref[0].astype(F32) * (i > 0).astype(F32)
        z1 = _shift_down(z, 1, [zp[15:16]])
        z2 = _shift_down(z, 2, [zp[14:15], zp[15:16]])
        cw = cw_ref[...]
        u = cw[0:1] * z2 + cw[1:2] * z1 + cw[2:3] * z
        du = dcbu * cb
        du_n = dcbu_all[tm:] * cbn_ref[0].astype(F32) * (i < n_tiles - 1).astype(F32)
        du1 = _shift_up(du, 1, [du_n[0:1]])
        du2 = _shift_up(du, 2, [du_n[0:1], du_n[1:2]])
        dz = cw[2:3] * du + cw[1:2] * du1 + cw[0:1] * du2
        pv_ref[3:4, :] += _rowsum(du * z2)
        pv_ref[4:5, :] += _rowsum(du * z1)
        pv_ref[5:6, :] += _rowsum(du * z)

        de_ref[0] = (dcbu * u).astype(BF16)
        de_ref[1] = (dz * cx).astype(BF16)
        de_ref[2] = (dz * cc).astype(BF16)
        de_ref[3] = dga.astype(BF16)
        de_ref[4] = dgb.astype(BF16)

    row = lambda w: pl.BlockSpec((tm, w), lambda i: (i, 0))
    nxt = pl.BlockSpec((16, D), lambda i: (jnp.minimum((i + 1) * (tm // 16), S // 16 - 1), 0))
    return pl.pallas_call(
        body, name="bwd_mix", grid=(n_tiles,),
        in_specs=[row(D), nxt, _vec_spec(), row(D)] + [_e_spec(c, tm) for c in range(5)]
                 + [_e_next_spec(0, tm, S), _e_next_spec(4, tm, S), _e_prev_spec(1, tm), _e_prev_spec(2, tm),
                    _const_spec((8, D)), _vec_spec(), _vec_spec(), row(D), row(D), row(AOW),
                    _const_spec((D, D)), _const_spec((D, D)), _const_spec((D, AOW))],
        out_specs=[row(D), row(D), row(D), row(AOW), row(AOW), pl.BlockSpec((5, tm, D), lambda i: (0, i, 0)),
                   _const_spec((8, D))],
        out_shape=[jax.ShapeDtypeStruct((S, D), BF16)] * 3 + [jax.ShapeDtypeStruct((S, AOW), F32)] * 2
                  + [jax.ShapeDtypeStruct((5, S, D), BF16), jax.ShapeDtypeStruct((8, D), F32)],
        compiler_params=_cparams("arbitrary"),
    )(dx1, dx1, gate1, mo, e, e, e, e, e, e, e, e, e, cw8, ba, bb, ya, yc, o_attn, w_out, w_bc, w_bat)


def _attn_bwd(qkv, do, lse, dl, bias_t):
    S = qkv.shape[2]
    nblk = S // HEAD

    def body(qkv_ref, do_ref, lse_ref, dl_ref, b_ref, d_ref):
        g = pl.program_id(1)
        bias = b_ref[0, 0]
        col = lax.broadcasted_iota(jnp.int32, bias.shape, 1)
        bias_last = jnp.where(col >= HEAD, NEG, bias)
        eye = (lax.broadcasted_iota(jnp.int32, (HEAD, HEAD), 0) == lax.broadcasted_iota(jnp.int32, (HEAD, HEAD), 1)).astype(F32)

        def as_row(t):
            return jnp.sum(t * eye, axis=0, keepdims=True)

        for gi, d in enumerate(DILATIONS):
            @pl.when(g == gi)
            def _(d=d):
                nb = nblk // d

                def query_side(start):
                    sl = pl.ds(start, HEAD, stride=d)
                    return (qkv_ref.at[0, 0][sl, :].astype(BF16), do_ref[sl, :].astype(BF16),
                            as_row(lse_ref[sl, :]), as_row(dl_ref[sl, :]))

                def step(b, first_of_residue, carry):
                    dq_part, own = carry
                    r, n = b // nb, b % nb
                    cur = pl.ds(n * (HEAD * d) + r, HEAD, stride=d)
                    if first_of_residue:
                        own = query_side(r)
                    nxt = query_side(jnp.minimum(n + 1, nb - 1) * (HEAD * d) + r)
                    q2 = jnp.concatenate([own[0], nxt[0]], axis=0)
                    do2 = jnp.concatenate([own[1], nxt[1]], axis=0)
                    k = qkv_ref.at[0, 1][cur, :].astype(BF16)
                    v = qkv_ref.at[0, 2][cur, :].astype(BF16)
                    s = _nt(k, q2) * SCALE + jnp.where(n < nb - 1, bias, bias_last)
                    p = jnp.exp(s - jnp.concatenate([own[2], nxt[2]], axis=1))
                    d_ref.at[0, 2][cur, :] = _nn(p.astype(BF16), do2)
                    dp = _nt(v, do2)
                    ds = (p * (dp - jnp.concatenate([own[3], nxt[3]], axis=1)) * SCALE).astype(BF16)
                    d_ref.at[0, 1][cur, :] = _nn(ds, q2)
                    dq2 = _tn(ds, k)
                    d_ref.at[0, 0][cur, :] = dq2[:HEAD] + jnp.where(n > 0, dq_part, 0.0)
                    return dq2[HEAD:], nxt

                def steps(i, carry):
                    for u in range(UNROLL):
                        carry = step(i * UNROLL + u, nb <= UNROLL and u % nb == 0, carry)
                    return carry

                lax.fori_loop(0, nblk // UNROLL, steps, (jnp.zeros((HEAD, HEAD), F32), query_side(0)))

    col_blk = pl.BlockSpec((S, HEAD), lambda j, g: (0, j))
    qkv_blk = pl.BlockSpec((1, 3, S, HEAD), lambda j, g: (g, 0, 0, j))
    return pl.pallas_call(
        body, name="attn_bwd", grid=(N_SLOT, 3),
        in_specs=[qkv_blk, col_blk, col_blk, col_blk, pl.BlockSpec((1, 1, HEAD, 2 * HEAD), lambda j, g: (g, j, 0, 0))],
        out_specs=qkv_blk,
        out_shape=jax.ShapeDtypeStruct((3, 3, S, AOW), F32),
        compiler_params=_cparams("parallel", "arbitrary"),
    )(qkv, do, lse, dl, bias_t)


def _bwd_in(dqkv, de, w_int, x, dx1, g_mix, sc1):
    S = x.shape[0]
    tm = TM
    dqkv = dqkv.reshape(3, 3, S, AOW)

    def body(dq_ref, de_ref, wq_ref, wk_ref, wv_ref, wa_ref, wb_ref, x_ref, dx1_ref, g_ref, sc_ref, gx_ref, pv_ref):
        acc = gx_ref
        i, k = pl.program_id(0), pl.program_id(1)

        @pl.when((i == 0) & (k == 0))
        def _():
            pv_ref[...] = jnp.zeros_like(pv_ref)

        @pl.when(k == 0)
        def _():
            acc[...] = jnp.zeros_like(acc)

        @pl.when(k < 3)
        def _():
            lhs = jnp.concatenate([dq_ref[0, t].astype(BF16) for t in range(3)], axis=1)
            acc[...] += _nn(lhs, jnp.concatenate([wq_ref[...], wk_ref[...], wv_ref[...]], axis=0))

        @pl.when(k >= 3)
        def _():
            acc[...] += _nn(de_ref[0], jnp.concatenate([wa_ref[...], wb_ref[...]], axis=0))

        @pl.when(k == 7)
        def _():
            dh = acc[...]
            xv = x_ref[...]
            r = _rms_r(xv)
            g = g_ref[...]
            dxn, pg = _rms_bwd(xv, r, g, dh * (1.0 + sc_ref[...]))
            gx_ref[...] = dx1_ref[...] + dxn
            pv_ref[0:1, :] += _rowsum(dh)
            pv_ref[1:2, :] += _rowsum(dh * (xv * r * g))
            pv_ref[2:3, :] += _rowsum(pg)

    grp = lambda k: jnp.minimum(k, 2)
    chunk = lambda k: jnp.maximum(k - 3, 0)
    wblk = lambda f: pl.BlockSpec((512, D), lambda i, k: (f(k), 0))
    row = pl.BlockSpec((tm, D), lambda i, k: (i, 0))
    once = pl.BlockSpec((tm, D), lambda i, k: (i, 0), pipeline_mode=pl.Buffered(1))
    return pl.pallas_call(
        body, name="bwd_in", grid=(S // tm, 8),
        in_specs=[pl.BlockSpec((1, 3, tm, 512), lambda i, k: (grp(k), 0, i, 0)),
                  pl.BlockSpec((1, tm, D), lambda i, k: (chunk(k), i, 0)),
                  wblk(grp), wblk(lambda k: 3 + grp(k)), wblk(lambda k: 6 + grp(k)),
                  wblk(lambda k: 9 + 2 * chunk(k)), wblk(lambda k: 10 + 2 * chunk(k)),
                  once, once, _vec_spec(), _vec_spec()],
        out_specs=[row, _const_spec((8, D))],
        out_shape=[jax.ShapeDtypeStruct((S, D), F32), jax.ShapeDtypeStruct((8, D), F32)],
        compiler_params=_cparams("arbitrary", "arbitrary"),
    )(dqkv, de, w_int, w_int, w_int, w_int, w_int, x, dx1, g_mix, sc1)


def _grad_w(name, a, b):
    S, ka = a.shape
    nb = b.shape[1]

    def body(a_ref, b_ref, o_ref):
        o_ref[...] = _tn(a_ref[...], b_ref[...]).astype(BF16)

    return pl.pallas_call(
        body, name=name, grid=(ka // 512,),
        in_specs=[pl.BlockSpec((S, 512), lambda n: (0, n)), pl.BlockSpec((S, nb), lambda n: (0, 0))],
        out_specs=pl.BlockSpec((512, nb), lambda n: (n, 0)),
        out_shape=jax.ShapeDtypeStruct((ka, nb), BF16),
        compiler_params=_cparams("parallel"),
    )(a, b)


def _grad_w_small(dya, o_bf, cbu, dyc, merged, dmo, after):
    S = dya.shape[0]

    def body(dya_h, o_h, cbu_h, dyc_h, mg_h, dmo_h, after_ref, gba_ref, gbc_ref, gout_ref, a0, a1, b_small, b1, b2, sems):
        fetch = [pltpu.make_async_copy(src, buf, sems.at[k])
                 for k, (src, buf) in enumerate(((dya_h, a0), (o_h, b_small), (cbu_h, a1), (dyc_h, b1), (dmo_h, b2)))]
        for cp in fetch:
            cp.start()
        fetch[0].wait()
        fetch[1].wait()
        gba_ref[...] = _tn(a0[...], b_small[...]).astype(BF16)
        last = pltpu.make_async_copy(mg_h, a0, sems.at[5])
        last.start()
        fetch[2].wait()
        fetch[3].wait()
        gbc_ref[...] = _tn(a1[...], b1[...]).astype(BF16)
        fetch[4].wait()
        last.wait()
        gout_ref[...] = _tn(a0[...], b2[...]).astype(BF16)

    anyspec = pl.BlockSpec(memory_space=pl.ANY)
    vmem = pl.BlockSpec(memory_space=pltpu.VMEM)
    wide = pltpu.VMEM((S, D), BF16)
    return pl.pallas_call(
        body, name="grad_w_small",
        in_specs=[anyspec] * 7, out_specs=[vmem] * 3,
        out_shape=[jax.ShapeDtypeStruct((D, AOW), BF16), jax.ShapeDtypeStruct((D, D), BF16), jax.ShapeDtypeStruct((D, D), BF16)],
        scratch_shapes=[wide, wide, pltpu.VMEM((S, AOW), BF16), wide, wide, pltpu.SemaphoreType.DMA((6,))],
        compiler_params=_cparams(),
    )(dya, o_bf, cbu, dyc, merged, dmo, after)


def _grad_w_in(dqkv, de, h):
    S = h.shape[0]

    def body(dq_ref, de_ref, h_ref, o_ref):
        n = pl.program_id(0)

        @pl.when(n < 9)
        def _():
            o_ref[...] = _tn(dq_ref[0].astype(BF16), h_ref[...]).astype(BF16)

        @pl.when(n >= 9)
        def _():
            o_ref[...] = _tn(de_ref[0], h_ref[...]).astype(BF16)

    def e_idx(n):
        kk = jnp.maximum(n - 9, 0)
        return (kk // 2, 0, kk % 2)

    return pl.pallas_call(
        body, name="grad_w_in", grid=(19,),
        in_specs=[pl.BlockSpec((1, S, 512), lambda n: (jnp.minimum(n, 8), 0, 0)), pl.BlockSpec((1, S, 512), e_idx),
                  pl.BlockSpec((S, D), lambda n: (0, 0))],
        out_specs=pl.BlockSpec((512, D), lambda n: (_win_rowblock(n), 0)),
        out_shape=jax.ShapeDtypeStruct((19 * 512, D), BF16),
        compiler_params=_cparams("parallel"),
    )(dqkv, de, h)


def _local_step(x, h, tgt, mod, g_mix, g_mlp, g_fin, ba, bb, cw8, w_int, mix_weights, mlp_weights, mlp_grads_ready, w_in_grad_ready,
                other_grads_ready):
    S = x.shape[0]
    sh1, sc1, gt1, sh2, sc2, gt2 = [mod[k:k + 1] for k in range(6)]
    bias, bias_t = _bias_table()

    qkv, e = _proj(h, w_int)
    qkv = qkv.reshape(3, 3, S, AOW)
    o_attn, lse = _attn_fwd(qkv, bias)
    w_bat, w_bc, w_out = mix_weights(o_attn)
    o_bf, cbu, ya, yc, merged = _mix(o_attn, e, cw8, ba, bb, w_bat, w_bc)
    x1, mo, h2 = _out_proj(merged, w_out, x, gt1, g_mlp, sc2, sh2)
    w_mit, w_mo = mlp_weights(x1)
    a, f = _mlp_in(h2, w_mit)
    mlp, dx2, pv_f = _mlp_out(f, w_mo, x1, gt2, g_fin, tgt)

    da, dmo2, pv_a = _bwd_mlp_a(dx2, gt2, mlp, w_mo, a)
    dx1, pv_b = _bwd_mlp_b(da, w_mit, x1, dx2, g_mlp, sc2)
    zero = mlp_grads_ready(_grad_w("grad_w_mi", da, h2), _grad_w("grad_w_mo", f, dmo2))
    dmo, dya, dyc, do, dl, de, pv_m = _bwd_mix(dx1, gt1 + zero, mo, e, cw8, ba, bb, ya, yc, o_attn, w_out, w_bc, w_bat)
    dqkv = _attn_bwd(qkv, do, lse, dl, bias_t).reshape(9, S, AOW)
    after = w_in_grad_ready(_grad_w_in(dqkv, de, h))
    zero = other_grads_ready(*_grad_w_small(dya, o_bf, cbu, dyc, merged, dmo, after))
    grad_x, pv_i = _bwd_in(dqkv, de, w_int, x, dx1, g_mix, sc1 + zero)

    vec = jnp.concatenate([pv_i[0:2], pv_m[0:1], pv_b[0:2], pv_a[0:1], pv_i[2:3], pv_b[2:3], pv_f[0:1],
                           pv_m[1:3], pv_m[3:6], pv_f[1:2], jnp.zeros((1, D), F32)], axis=0)
    return grad_x, vec


def _my_place():
    return lax.axis_index("x"), lax.axis_index("y"), lax.axis_index("c")


def _dev_index(px, py, pc):
    return 4 * px + 2 * py + pc


def _peer(x, y, c, m):
    return (x ^ ((m >> 2) & 1), y ^ ((m >> 1) & 1), c ^ (m & 1))


HBM_SPEC = pl.BlockSpec(memory_space=pltpu.HBM)
SEM_SPEC = pl.BlockSpec(memory_space=pltpu.SEMAPHORE)
N_PEER = N_DEV - 1


SPLIT_MASKS = {"gather": tuple(range(1, N_DEV)), "scatter": tuple(range(1, N_DEV)), "chips": (2, 4, 6), "sibling": (1, 1, 1, 1)}


def _split_copy(mode, src_ref, land_ref, send_sems, recv_sems, w, j, place, arriving=False):
    x, y, c = place
    masks = SPLIT_MASKS[mode]
    peer = _peer(x, y, c, masks[j])
    k = w * len(masks) + j
    sender, receiver = ((peer, (x, y, c)) if arriving else ((x, y, c), peer))
    if mode == "gather":
        r = src_ref.shape[0]
        src, dst = src_ref, land_ref.at[pl.ds(pl.multiple_of(_dev_index(*sender) * r, 16), r), :]
    elif mode == "scatter":
        r = land_ref.shape[1]
        src, dst = src_ref.at[pl.ds(pl.multiple_of(_dev_index(*receiver) * r, 16), r), :], land_ref.at[j]
    elif mode == "chips":
        src, dst = src_ref.at[2 * receiver[0] + receiver[1]], land_ref.at[j]
    else:
        r = land_ref.shape[1]
        src, dst = src_ref.at[pl.ds(pl.multiple_of((2 * j + receiver[2]) * r, 16), r), :], land_ref.at[j]
    return pltpu.make_async_remote_copy(src_ref=src, dst_ref=dst, send_sem=send_sems.at[k], recv_sem=recv_sems.at[k],
                                        device_id=peer, device_id_type=MESH)


def _split_start(name, mode, srcs, lands):
    n = len(srcs)
    nm = len(SPLIT_MASKS[mode])

    def body(*refs):
        src, land = refs[:n], refs[n:2 * n]
        send_sems, recv_sems = refs[2 * n], refs[2 * n + 1]
        token = refs[-1]
        place = _my_place()
        for w in range(n):
            for j in range(nm):
                _split_copy(mode, src[w], land[w], send_sems, recv_sems, w, j, place).start()
        token[...] = jnp.zeros_like(token)

    hbm = lambda t: pltpu.HBM(t.shape, t.dtype)
    out = pl.pallas_call(
        body, name=name,
        out_shape=(pltpu.SemaphoreType.DMA((n * nm,)), pltpu.SemaphoreType.DMA((n * nm,)), *[hbm(t) for t in srcs],
                   *[hbm(t) for t in lands], jax.ShapeDtypeStruct((8, 128), F32)),
        in_specs=(HBM_SPEC,) * (2 * n),
        out_specs=(SEM_SPEC, SEM_SPEC) + (HBM_SPEC,) * (2 * n) + (pl.BlockSpec(memory_space=pltpu.VMEM),),
        input_output_aliases={i: 2 + i for i in range(2 * n)},
        compiler_params=pltpu.CompilerParams(has_side_effects=pltpu.SideEffectType.DATAFLOW_SIDE_EFFECTING),
    )(*[pltpu.with_memory_space_constraint(t, pltpu.HBM) for t in (*srcs, *lands)])
    return out[0], out[1], out[2:2 + n], out[2 + n:2 + 2 * n], out[-1][0:1, 0:1], out[-1]


def _split_wait(name, mode, send_sems, recv_sems, srcs, lands, after):
    n = len(srcs)

    def body(*refs):
        src, land = refs[:n], refs[n:2 * n]
        ssem, rsem = refs[2 * n], refs[2 * n + 1]
        place = _my_place()
        for w in range(n):
            for j in range(len(SPLIT_MASKS[mode])):
                _split_copy(mode, src[w], land[w], ssem, rsem, w, j, place).wait_send()
                _split_copy(mode, src[w], land[w], ssem, rsem, w, j, place, arriving=True).wait_recv()

    hbm = lambda t: pltpu.HBM(t.shape, t.dtype)
    out = pl.pallas_call(
        body, name=name,
        out_shape=tuple(hbm(t) for t in (*srcs, *lands)),
        in_specs=(HBM_SPEC,) * (2 * n) + (SEM_SPEC, SEM_SPEC, pl.BlockSpec(memory_space=pl.ANY)),
        out_specs=(HBM_SPEC,) * (2 * n),
        input_output_aliases={i: i for i in range(2 * n)},
        compiler_params=pltpu.CompilerParams(has_side_effects=pltpu.SideEffectType.DATAFLOW_SIDE_EFFECTING),
    )(*srcs, *lands, send_sems, recv_sems, after)
    return out[:n], out[n:]


def _sibling_exchange(grads):
    nw = len(grads)
    HBM = pl.BlockSpec(memory_space=pl.ANY)

    def body(*refs):
        g, land = refs[:nw], refs[nw:2 * nw]
        send_sems, recv_sems = refs[2 * nw:]
        x, y, c = _my_place()

        def copy(w, q, owner_core):
            r = land[w].shape[1]
            return pltpu.make_async_remote_copy(
                src_ref=g[w].at[pl.ds(pl.multiple_of((2 * q + owner_core) * r, 16), r), :], dst_ref=land[w].at[q],
                send_sem=send_sems.at[w, q], recv_sem=recv_sems.at[w, q], device_id=(x, y, 1 - c), device_id_type=MESH)

        sends = [copy(w, q, 1 - c) for w in range(nw) for q in range(4)]
        for cp in sends:
            cp.start()
        for w in range(nw):
            for q in range(4):
                copy(w, q, c).wait_recv()
        for cp in sends:
            cp.wait_send()

    return pl.pallas_call(
        body, name="sibling_exchange",
        out_shape=[jax.ShapeDtypeStruct((4, a.shape[0] // N_DEV, a.shape[1]), a.dtype) for a in grads],
        in_specs=[HBM] * nw, out_specs=[HBM] * nw,
        scratch_shapes=[pltpu.SemaphoreType.DMA((nw, 4)), pltpu.SemaphoreType.DMA((nw, 4))],
    )(*grads)


def _pair_sums(gs, sibs, core):
    n = len(gs)

    def body(core_ref, *refs):
        for w in range(n):
            refs[2 * n + w][0] = (refs[w][0, 0].astype(F32) + refs[n + w][0].astype(F32)).astype(BF16)

    in_specs = [pl.BlockSpec((1, 1) + t.shape[1:], lambda q, core_ref: (q, core_ref[0], 0, 0)) for t in sibs]
    in_specs += [pl.BlockSpec((1,) + t.shape[1:], lambda q, core_ref: (q, 0, 0)) for t in sibs]
    return pl.pallas_call(
        body, name="pair_sums",
        grid_spec=pltpu.PrefetchScalarGridSpec(
            num_scalar_prefetch=1, grid=(4,), in_specs=in_specs,
            out_specs=[pl.BlockSpec((1,) + t.shape[1:], lambda q, core_ref: (q, 0, 0)) for t in sibs]),
        out_shape=[jax.ShapeDtypeStruct(t.shape, BF16) for t in sibs],
        compiler_params=_cparams("parallel"),
    )(core, *[g.reshape(4, 2, t.shape[1], t.shape[2]) for g, t in zip(gs, sibs)], *sibs)


def _own_rows_into_zones(shards, me):
    n = len(shards)

    def body(me_ref, *refs):
        for w in range(n):
            refs[2 * n + w][...] = refs[w][...]

    zones = [lax.empty((N_DEV * t.shape[0], t.shape[1]), t.dtype) for t in shards]
    return pl.pallas_call(
        body, name="own_rows_into_zones",
        grid_spec=pltpu.PrefetchScalarGridSpec(
            num_scalar_prefetch=1, grid=(1,),
            in_specs=[pl.BlockSpec(t.shape, lambda i, me_ref: (0, 0)) for t in shards] + [pl.BlockSpec(memory_space=pl.ANY)] * n,
            out_specs=[pl.BlockSpec(t.shape, lambda i, me_ref: (me_ref[0], 0)) for t in shards]),
        out_shape=[jax.ShapeDtypeStruct(z.shape, z.dtype) for z in zones],
        input_output_aliases={1 + n + w: w for w in range(n)},
        compiler_params=_cparams("arbitrary"),
    )(me, *shards, *zones)


def _allgather_small(v, name):
    r, ccols = v.shape

    def body(v_ref, out_ref, sum_ref, send_sems, recv_sems):
        x, y, c = _my_place()
        my_idx = _dev_index(x, y, c)
        out_ref[my_idx] = v_ref[...]

        def copy(m):
            peer = _peer(x, y, c, m)
            return pltpu.make_async_remote_copy(
                src_ref=v_ref, dst_ref=out_ref.at[my_idx],
                send_sem=send_sems.at[m - 1], recv_sem=recv_sems.at[m - 1], device_id=peer, device_id_type=MESH)

        def arrival(m):
            peer = _peer(x, y, c, m)
            return pltpu.make_async_remote_copy(
                src_ref=v_ref, dst_ref=out_ref.at[_dev_index(*peer)],
                send_sem=send_sems.at[m - 1], recv_sem=recv_sems.at[m - 1], device_id=peer, device_id_type=MESH)

        sends = [copy(m) for m in range(1, N_DEV)]
        for cp in sends:
            cp.start()
        for m in range(1, N_DEV):
            arrival(m).wait_recv()
        acc = out_ref[0]
        for s in range(1, N_DEV):
            acc = acc + out_ref[s]
        sum_ref[...] = acc
        for cp in sends:
            cp.wait_send()

    vmem = pl.BlockSpec(memory_space=pltpu.VMEM)
    return pl.pallas_call(
        body, name=name,
        out_shape=[jax.ShapeDtypeStruct((N_DEV, r, ccols), v.dtype), jax.ShapeDtypeStruct((r, ccols), v.dtype)],
        in_specs=[vmem], out_specs=[vmem, vmem],
        scratch_shapes=[pltpu.SemaphoreType.DMA((7,)), pltpu.SemaphoreType.DMA((7,))],
    )(v)


def _gather_w_in_and_condition(shard, pay, w_ada, b_cols):
    r, ccols = shard.shape
    ncol = w_ada.shape[1]

    def body(sh_ref, pay_ref, w_ref, b_ref, full_ref, got_ref, act_ref, mod_ref, send_sems, recv_sems, small_send, small_recv, local_sem):
        x, y, c = _my_place()
        me, sibling = (x, y, c), (x, y, 1 - c)
        my_idx = _dev_index(x, y, c)
        chips = [(1 - x, y), (x, 1 - y), (1 - x, 1 - y)]

        def small(rnd, buf, m, arriving=False):
            peer = _peer(x, y, c, m)
            slot = _dev_index(*peer) if arriving else my_idx
            return pltpu.make_async_remote_copy(
                src_ref=buf.at[my_idx], dst_ref=buf.at[slot], send_sem=small_send.at[rnd, m - 1],
                recv_sem=small_recv.at[rnd, m - 1], device_id=peer, device_id_type=MESH)

        def rows(px, py, pc):
            return full_ref.at[pl.ds(pl.multiple_of(_dev_index(px, py, pc) * r, 16), r), :]

        def copy(k, block, to, src=None):
            return pltpu.make_async_remote_copy(
                src_ref=rows(*block) if src is None else src, dst_ref=rows(*block),
                send_sem=send_sems.at[k], recv_sem=recv_sems.at[k], device_id=to, device_id_type=MESH)

        got_ref[my_idx] = pay_ref[...]
        round1 = [small(0, got_ref, m) for m in range(1, N_DEV)]
        for cp in round1:
            cp.start()
        mine = pltpu.make_async_copy(sh_ref, rows(*me), local_sem)
        mine.start()
        first = [copy(0, me, sibling, src=sh_ref)] + [copy(1 + j, me, (*chip, c), src=sh_ref) for j, chip in enumerate(chips)]
        for cp in first:
            cp.start()

        for m in range(1, N_DEV):
            small(0, got_ref, m, arriving=True).wait_recv()
        cv = jnp.concatenate([got_ref[s, 0:1, :] for s in range(N_DEV)], axis=0)
        act = cv * _sigmoid(cv)
        act_ref[...] = act
        mod_ref[my_idx] = jnp.dot(act, w_ref[...], preferred_element_type=F32, precision=lax.Precision.HIGHEST) + b_ref[...]
        round2 = [small(1, mod_ref, m) for m in range(1, N_DEV)]
        for cp in round2:
            cp.start()

        passed = []
        for j, chip in enumerate(chips):
            copy(1 + j, (*chip, c), me).wait_recv()
            fwd = copy(4 + j, (*chip, c), sibling)
            fwd.start()
            passed.append(fwd)
        copy(0, sibling, me).wait_recv()
        for j, chip in enumerate(chips):
            copy(4 + j, (*chip, 1 - c), me).wait_recv()
        for m in range(1, N_DEV):
            small(1, mod_ref, m, arriving=True).wait_recv()
        for cp in first + passed + round1 + round2:
            cp.wait_send()
        mine.wait()

    anyspec = pl.BlockSpec(memory_space=pl.ANY)
    vmem = pl.BlockSpec(memory_space=pltpu.VMEM)
    return pl.pallas_call(
        body, name="gather_w_in_and_condition",
        out_shape=[jax.ShapeDtypeStruct((N_DEV * r, ccols), shard.dtype), jax.ShapeDtypeStruct((N_DEV, 8, D), F32),
                   jax.ShapeDtypeStruct((N_DEV, D), F32), jax.ShapeDtypeStruct((N_DEV, N_DEV, ncol), F32)],
        in_specs=[anyspec, vmem, vmem, vmem], out_specs=[anyspec, vmem, vmem, vmem],
        scratch_shapes=[pltpu.SemaphoreType.DMA((7,)), pltpu.SemaphoreType.DMA((7,)), pltpu.SemaphoreType.DMA((2, 7)),
                        pltpu.SemaphoreType.DMA((2, 7)), pltpu.SemaphoreType.DMA],
        compiler_params=_cparams(),
    )(shard, pay, w_ada, b_cols)


def _row_tile(r):
    for t in (256, 304, 128, 64, 16):
        if r % t == 0:
            return t
    return r


def _adamw_w_ada(w, act_t, gm_cols, m, v):
    r, ccols = w.shape
    tr = _row_tile(r)
    c1 = 1.0 / (1.0 - B1 ** STEP)
    c2 = 1.0 / (1.0 - B2 ** STEP)

    def body(w_ref, a_ref, gm_ref, m_ref, v_ref, g_ref, d_ref, nm_ref, nv_ref):
        gv = jnp.dot(a_ref[...], gm_ref[...], preferred_element_type=F32, precision=lax.Precision.HIGHEST)
        g_ref[...] = gv
        nm = B1 * m_ref[...] + (1.0 - B1) * gv
        nv = B2 * v_ref[...] + (1.0 - B2) * jnp.square(gv)
        nm_ref[...] = nm
        nv_ref[...] = nv
        d_ref[...] = -LR * ((nm * c1) / (jnp.sqrt(nv * c2) + ADAM_EPS) + WD * w_ref[...])

    blk = pl.BlockSpec((tr, ccols), lambda i: (i, 0))
    return pl.pallas_call(
        body, name="adamw_w_ada", grid=(r // tr,),
        in_specs=[blk, pl.BlockSpec((tr, N_DEV), lambda i: (i, 0)), _const_spec(gm_cols.shape), blk, blk], out_specs=[blk] * 4,
        out_shape=[jax.ShapeDtypeStruct((r, ccols), F32)] * 4,
        compiler_params=_cparams("parallel"),
    )(w, act_t, gm_cols, m, v)


def _sum_adamw(parts, own, slot, w, m, v, name, transposed=False):
    k, r, ccols = parts.shape
    tr = _row_tile(r)
    c1 = 1.0 / (1.0 - B1 ** STEP)
    c2 = 1.0 / (1.0 - B2 ** STEP)

    def body(s_ref, p_ref, own_ref, w_ref, m_ref, v_ref, g_ref, d_ref, nm_ref, nv_ref):
        gv = own_ref[0].astype(F32)
        for s in range(k):
            gv = gv + p_ref[s].astype(F32)
        if transposed:
            gv = gv.T
        g_ref[...] = gv
        nm = B1 * m_ref[...] + (1.0 - B1) * gv
        nv = B2 * v_ref[...] + (1.0 - B2) * jnp.square(gv)
        nm_ref[...] = nm
        nv_ref[...] = nv
        d_ref[...] = -LR * ((nm * c1) / (jnp.sqrt(nv * c2) + ADAM_EPS) + WD * w_ref[...])

    if transposed:
        blk = pl.BlockSpec((ccols, tr), lambda i, s_ref: (0, i))
    else:
        blk = pl.BlockSpec((tr, ccols), lambda i, s_ref: (i, 0))
    return pl.pallas_call(
        body, name=name,
        grid_spec=pltpu.PrefetchScalarGridSpec(
            num_scalar_prefetch=1, grid=(r // tr,),
            in_specs=[pl.BlockSpec((k, tr, ccols), lambda i, s_ref: (0, i, 0)),
                      pl.BlockSpec((1, tr, ccols), lambda i, s_ref: (s_ref[0], i, 0))] + [blk] * 3,
            out_specs=[blk] * 4),
        out_shape=[jax.ShapeDtypeStruct(w.shape, F32)] * 4,
        compiler_params=_cparams("parallel"),
    )(slot, parts, own, w, m, v)


VEC_ROWS = ((0, 6), (6, 7), (9, 11), (11, 14), (7, 8), (8, 9))


def _adamw_vectors(w, g, m, v):
    c1 = 1.0 / (1.0 - B1 ** STEP)
    c2 = 1.0 / (1.0 - B2 ** STEP)

    def put(refs, p):
        for ref, (lo, hi) in zip(refs, VEC_ROWS):
            if ref.shape == (3, HEAD):
                ref[...] = p[lo:hi, :HEAD]
            else:
                ref[...] = jnp.concatenate([p[k:k + 1] for k in range(lo, hi)], axis=1)

    def body(w_ref, g_ref, m_ref, v_ref, *outs):
        gv = g_ref[...]
        nm = B1 * m_ref[...] + (1.0 - B1) * gv
        nv = B2 * v_ref[...] + (1.0 - B2) * jnp.square(gv)
        delta = -LR * ((nm * c1) / (jnp.sqrt(nv * c2) + ADAM_EPS) + WD * w_ref[...])
        for kind, p in enumerate((gv, delta, nm, nv)):
            put(outs[6 * kind:6 * kind + 6], p)

    shapes = [(1, 6 * D), (1, D), (1, 2 * D), (3, HEAD), (1, D), (1, D)]
    out = pl.pallas_call(
        body, name="adamw_vectors", out_shape=[jax.ShapeDtypeStruct(sh, F32) for sh in shapes] * 4, compiler_params=_cparams(),
    )(w, g, m, v)
    fix = lambda t: (t[0], t[1], t[2], t[3][None], t[4], t[5].reshape(D))
    return [fix(out[6 * kind:6 * kind + 6]) for kind in range(4)]


def _pack_vectors(b_ada, g_mix, g_mlp, g_fin, b_gate, conv_w):
    conv_rows = jnp.pad(conv_w.reshape(3, HEAD), ((0, 0), (0, D - HEAD)))
    return jnp.concatenate([b_ada.reshape(6, D), g_mix.reshape(1, D), g_mlp.reshape(1, D), g_fin.reshape(1, D),
                            b_gate.reshape(2, D), conv_rows, jnp.zeros((2, D), F32)], axis=0)


def kernel(x, c, w_ada, b_ada, g_norm_mix, w_in, b_gate, conv_w, w_branch_attn, w_branch_conv, w_out, g_norm_mlp, w_mlp_in, w_mlp_out, g_norm_final, loss_target, m_w_ada, m_b_ada, m_g_norm_mix, m_w_in, m_b_gate, m_conv_w, m_w_branch_attn, m_w_branch_conv, m_w_out, m_g_norm_mlp, m_w_mlp_in, m_w_mlp_out, m_g_norm_final, v_w_ada, v_b_ada, v_g_norm_mix, v_w_in, v_b_gate, v_conv_w, v_w_branch_attn, v_w_branch_conv, v_w_out, v_g_norm_mlp, v_w_mlp_in, v_w_mlp_out, v_g_norm_final):
    S = x.shape[1]
    xi, yi, ci = _my_place()
    me = _dev_index(xi, yi, ci)
    x2 = x.reshape(S, D)
    tgt = loss_target.reshape(S, D)

    pay = jnp.zeros((8, D), F32).at[0].set(c[0]).at[1:4, :HEAD].set(conv_w[0])
    ncol = w_ada.shape[2]
    b_cols = lax.dynamic_slice(b_ada, (0, me * ncol), (1, ncol))
    w_int, got, act, mod_all = _gather_w_in_and_condition(w_in[0].T.astype(BF16), pay, w_ada[0], b_cols)
    cw8 = jnp.pad(got[:, 1:4, :HEAD].transpose(1, 0, 2).reshape(3, D), ((0, 5), (0, 0)))
    mod = lax.dynamic_index_in_dim(mod_all, me, axis=1, keepdims=False).reshape(6, D)
    late = [w_branch_attn[0].T.astype(BF16), w_branch_conv[0].astype(BF16), w_out[0].astype(BF16),
            w_mlp_in[0].T.astype(BF16), w_mlp_out[0].astype(BF16)]
    w_int, late = lax.optimization_barrier((w_int, late))
    zones = _own_rows_into_zones(late, me.reshape(1).astype(jnp.int32))
    ag_mix = _split_start("gather_mix_start", "gather", late[:3], zones[:3])
    ag_mlp = _split_start("gather_mlp_start", "gather", late[3:], zones[3:])
    mod = mod + ag_mix[4] + ag_mlp[4]
    h = _prenorm(x2, g_norm_mix, mod[1:2], mod[0:1])

    def mix_weights(o_attn):
        return _split_wait("gather_mix_wait", "gather", *ag_mix[:4], o_attn)[1]

    def mlp_weights(x1):
        return _split_wait("gather_mlp_wait", "gather", *ag_mlp[:4], x1)[1]

    rs = {}

    def mlp_grads_ready(*grads):
        lands = [lax.empty((N_PEER, t.shape[0] // N_DEV, t.shape[1]), BF16) for t in grads]
        rs["mlp"] = _split_start("scatter_mlp_start", "scatter", grads, lands)
        return rs["mlp"][4]

    def w_in_grad_ready(g_in):
        r = g_in.shape[0] // N_DEV
        rs["sib"] = _split_start("sibling_w_in_start", "sibling", [g_in], [lax.empty((4, r, g_in.shape[1]), BF16)])
        return rs["sib"][5]

    def other_grads_ready(*small):
        core = ci.reshape(1).astype(jnp.int32)
        (g_in,), (sib_in,) = _split_wait("sibling_w_in_wait", "sibling", *rs["sib"][:4], small[0])
        pair = _pair_sums([g_in, *small], [sib_in, *_sibling_exchange(small)], core)
        lands = [lax.empty((3,) + t.shape[1:], BF16) for t in pair]
        rs["rest"] = _split_start("scatter_rest_start", "chips", pair, lands)
        return rs["rest"][4]

    ba, bb = b_gate[:, :D], b_gate[:, D:]
    grad_x, vec = _local_step(
        x2, h, tgt, mod, g_norm_mix, g_norm_mlp, g_norm_final.reshape(1, D), ba, bb, cw8, w_int, mix_weights, mlp_weights,
        mlp_grads_ready, w_in_grad_ready, other_grads_ready)

    vec_all, vec_sum = _allgather_small(vec, "gather_vec")
    loss = vec_sum[14, 0]
    gm_all = vec_all[:, 0:6, :].reshape(N_DEV, 6 * D)
    gm_cols = lax.dynamic_slice(gm_all, (0, me * ncol), (N_DEV, ncol))
    conv_cols = lax.dynamic_slice(vec_sum[11:14], (0, me * HEAD), (3, HEAD))
    g_pack = jnp.concatenate([vec_sum[0:11], jnp.pad(conv_cols, ((0, 0), (0, D - HEAD))), jnp.zeros((2, D), F32)], axis=0)
    packs = [_pack_vectors(*t) for t in ((b_ada, g_norm_mix, g_norm_mlp, g_norm_final, b_gate, conv_w),
                                         (m_b_ada, m_g_norm_mix, m_g_norm_mlp, m_g_norm_final, m_b_gate, m_conv_w),
                                         (v_b_ada, v_g_norm_mix, v_g_norm_mlp, v_g_norm_final, v_b_gate, v_conv_w))]
    gv, dv, mv, vv = _adamw_vectors(packs[0], g_pack, packs[1], packs[2])
    g_w_ada, d_ada, nm_ada, nv_ada = _adamw_w_ada(w_ada[0], act.T, gm_cols, m_w_ada[0], v_w_ada[0])

    big = {}
    srcs, lands = _split_wait("scatter_mlp_wait", "scatter", *rs["mlp"][:4], d_ada)
    own = [g.reshape((N_DEV,) + land.shape[1:]) for g, land in zip(srcs, lands)]
    slot = me.reshape(1).astype(jnp.int32)
    big["w_mi"] = tuple(t[None] for t in _sum_adamw(lands[0], own[0], slot, w_mlp_in[0], m_w_mlp_in[0], v_w_mlp_in[0], "adamw_w_mi",
                                                    transposed=True))
    big["w_mo"] = tuple(t[None] for t in _sum_adamw(lands[1], own[1], slot, w_mlp_out[0], m_w_mlp_out[0], v_w_mlp_out[0], "adamw_w_mo"))
    own, lands = _split_wait("scatter_rest_wait", "chips", *rs["rest"][:4], big["w_mo"][1])
    slot = (2 * xi + yi).reshape(1).astype(jnp.int32)
    big["w_in"] = tuple(t.T[None] for t in _sum_adamw(lands[0], own[0], slot, w_in[0].T, m_w_in[0].T, v_w_in[0].T, "adamw_w_in"))
    big["w_ba"] = tuple(t[None] for t in _sum_adamw(lands[1], own[1], slot, w_branch_attn[0], m_w_branch_attn[0], v_w_branch_attn[0],
                                                    "adamw_w_ba", transposed=True))
    big["w_bc"] = tuple(t[None] for t in _sum_adamw(lands[2], own[2], slot, w_branch_conv[0], m_w_branch_conv[0], v_w_branch_conv[0], "adamw_w_bc"))
    big["w_out"] = tuple(t[None] for t in _sum_adamw(lands[3], own[3], slot, w_out[0], m_w_out[0], v_w_out[0], "adamw_w_out"))

    def ordered(k, ada, vecs):
        return (ada[None], vecs[0], vecs[1], big["w_in"][k], vecs[2], vecs[3], big["w_ba"][k], big["w_bc"][k],
                big["w_out"][k], vecs[4], big["w_mi"][k], big["w_mo"][k], vecs[5])

    return (loss, grad_x.reshape(1, S, D), *ordered(0, g_w_ada, gv), *ordered(1, d_ada, dv),
            *ordered(2, nm_ada, mv), *ordered(3, nv_ada, vv))
```

```python
import numpy as np
import jax
import jax.numpy as jnp
from jax import lax
from jax.experimental import pallas as pl
from jax.experimental.pallas import tpu as pltpu

F32, BF16 = jnp.float32, jnp.bfloat16
D = 1024
HEAD = 128
DILATIONS = (1, 4, 16)
N_SLOT = 4
AOW = N_SLOT * HEAD
DFF = 4 * D
N_DEV = 8
UNROLL = 16
EPS = 1e-6
NEG = -1e30
SCALE = HEAD ** -0.5
LR, B1, B2, ADAM_EPS, WD, STEP = 0.001, 0.9, 0.999, 1e-08, 0.01, 10
V7X_VMEM_LIMIT = 56 * 1024 * 1024
TM = 1024
MESH = pl.DeviceIdType.MESH


def _cparams(*sem):
    if sem:
        return pltpu.CompilerParams(dimension_semantics=sem, vmem_limit_bytes=V7X_VMEM_LIMIT)
    return pltpu.CompilerParams(vmem_limit_bytes=V7X_VMEM_LIMIT)


def _nn(a, b):
    return jnp.dot(a, b, preferred_element_type=F32)


def _nt(a, b):
    return lax.dot_general(a, b, (((1,), (1,)), ((), ())), preferred_element_type=F32)


def _tn(a, b):
    return lax.dot_general(a, b, (((0,), (0,)), ((), ())), preferred_element_type=F32)


def _rms_r(x):
    return lax.rsqrt(jnp.mean(x * x, axis=-1, keepdims=True) + EPS)


def _rms_bwd(x, r, g, dn):
    gy = dn * g
    dx = r * gy - x * (r * r * r) * jnp.mean(x * gy, axis=-1, keepdims=True)
    return dx, dn * (x * r)


def _sigmoid(t):
    return 1.0 / (1.0 + jnp.exp(-t))


def _rowsum(v):
    return jnp.sum(v, axis=0, keepdims=True)


def _vec_spec(n=D):
    return pl.BlockSpec((1, n), lambda *_: (0, 0))


def _const_spec(shape):
    nd = len(shape)
    return pl.BlockSpec(shape, lambda *_: (0,) * nd)


def _win_rowblock(j):
    return jnp.where(j < 9, (j % 3) * 3 + j // 3, j)


def _prenorm(x, g, sc, sh):
    S = x.shape[0]
    tm = TM

    def body(x_ref, g_ref, sc_ref, sh_ref, h_ref):
        xv = x_ref[...]
        h_ref[...] = (xv * _rms_r(xv) * g_ref[...] * (1.0 + sc_ref[...]) + sh_ref[...]).astype(BF16)

    row = pl.BlockSpec((tm, D), lambda i: (i, 0))
    return pl.pallas_call(
        body, name="prenorm", grid=(S // tm,), in_specs=[row, _vec_spec(), _vec_spec(), _vec_spec()], out_specs=row,
        out_shape=jax.ShapeDtypeStruct((S, D), BF16), compiler_params=_cparams("parallel"),
    )(x, g, sc, sh)


def _proj(h, w_int):
    S = h.shape[0]

    def body(h_ref, w_ref, q_ref, e_ref):
        j = pl.program_id(0)
        acc = _nt(h_ref[...], w_ref[...])

        @pl.when(j < 9)
        def _():
            q_ref[0] = acc

        @pl.when(j >= 9)
        def _():
            e_ref[0] = acc.astype(BF16)

    def e_idx(j):
        k = jnp.maximum(j - 9, 0)
        return (k // 2, 0, k % 2)

    return pl.pallas_call(
        body, name="proj", grid=(19,),
        in_specs=[pl.BlockSpec((S, D), lambda j: (0, 0), pipeline_mode=pl.Buffered(1)),
                  pl.BlockSpec((512, D), lambda j: (_win_rowblock(j), 0))],
        out_specs=[pl.BlockSpec((1, S, 512), lambda j: (jnp.minimum(j, 8), 0, 0)), pl.BlockSpec((1, S, 512), e_idx)],
        out_shape=[jax.ShapeDtypeStruct((9, S, 512), F32), jax.ShapeDtypeStruct((5, S, D), BF16)],
        compiler_params=_cparams("arbitrary"),
    )(h, w_int)


def _bias_table():
    slopes = (2.0 ** (-8.0 * np.arange(1, 13, dtype=np.float32) / 12.0)).astype(np.float32)
    qi = np.arange(HEAD)[:, None]
    kj = np.arange(2 * HEAD)[None, :]
    delta = HEAD + qi - kj
    mask = (delta >= 0) & (delta <= HEAD)
    out = np.zeros((3, N_SLOT, HEAD, 2 * HEAD), np.float32)
    for gi, d in enumerate(DILATIONS):
        for j in range(N_SLOT):
            bias = -slopes[gi * N_SLOT + j] * (delta * d).astype(np.float32)
            out[gi, j] = np.where(mask, bias, NEG)
    out_t = np.concatenate([out[..., HEAD:].swapaxes(-1, -2), out[..., :HEAD].swapaxes(-1, -2)], axis=-1)
    return jnp.asarray(out), jnp.asarray(out_t)


def _attn_fwd(qkv, bias):
    S = qkv.shape[2]
    nblk = S // HEAD
    rows = 256

    def body(qkv_ref, b_ref, o_ref, lse_ref, o_s, lse_s):
        g = pl.program_id(1)
        bias = b_ref[0, 0]
        col = lax.broadcasted_iota(jnp.int32, bias.shape, 1)
        bias_first = jnp.where(col < HEAD, NEG, bias)

        for gi, d in enumerate(DILATIONS):
            @pl.when(g == gi)
            def _(gi=gi, d=d):
                nb = nblk // d

                def keys(start):
                    sl = pl.ds(start, HEAD, stride=d)
                    return qkv_ref.at[0, 1][sl, :].astype(BF16), qkv_ref.at[0, 2][sl, :].astype(BF16)

                def step(b, first_of_residue, before):
                    r, n = b // nb, b % nb
                    cur = pl.ds(n * (HEAD * d) + r, HEAD, stride=d)
                    own = keys(n * (HEAD * d) + r)
                    if first_of_residue:
                        before = own
                    q = qkv_ref.at[0, 0][cur, :].astype(BF16)
                    kw = jnp.concatenate([before[0], own[0]], axis=0)
                    vw = jnp.concatenate([before[1], own[1]], axis=0)
                    s = _nt(q, kw) * SCALE + jnp.where(n > 0, bias, bias_first)
                    m = jnp.max(s, axis=-1, keepdims=True)
                    p = jnp.exp(s - m)
                    l = jnp.sum(p, axis=-1, keepdims=True)
                    o_s.at[gi][cur, :] = _nn(p.astype(BF16), vw) / l
                    lse_s.at[gi][cur, :] = jnp.broadcast_to(m + jnp.log(l), (HEAD, HEAD))
                    return own

                def steps(i, before):
                    for u in range(UNROLL):
                        before = step(i * UNROLL + u, nb <= UNROLL and u % nb == 0, before)
                    return before

                lax.fori_loop(0, nblk // UNROLL, steps, keys(0))

        @pl.when(g == len(DILATIONS) - 1)
        def _():
            def merge(i, carry):
                r = pl.ds(pl.multiple_of(i * rows, rows), rows)
                ls = [lse_s[k, r, :] for k in range(3)]
                top = jnp.maximum(jnp.maximum(ls[0], ls[1]), ls[2])
                ws = [jnp.exp(t - top) for t in ls]
                den = ws[0] + ws[1] + ws[2]
                o_ref[r, :] = (ws[0] * o_s[0, r, :] + ws[1] * o_s[1, r, :] + ws[2] * o_s[2, r, :]) / den
                lse_ref[r, :] = top + jnp.log(den)
                return carry

            lax.fori_loop(0, S // rows, merge, 0)

    return pl.pallas_call(
        body, name="attn_fwd", grid=(N_SLOT, 3),
        in_specs=[pl.BlockSpec((1, 3, S, HEAD), lambda j, g: (g, 0, 0, j)),
                  pl.BlockSpec((1, 1, HEAD, 2 * HEAD), lambda j, g: (g, j, 0, 0))],
        out_specs=[pl.BlockSpec((S, HEAD), lambda j, g: (0, j)), pl.BlockSpec((S, HEAD), lambda j, g: (0, j))],
        out_shape=[jax.ShapeDtypeStruct((S, AOW), F32), jax.ShapeDtypeStruct((S, AOW), F32)],
        scratch_shapes=[pltpu.VMEM((3, S, HEAD), F32)] * 2,
        compiler_params=_cparams("parallel", "arbitrary"),
    )(qkv, bias)


def _shift_down(z, k, halo_rows):
    out = pltpu.roll(z, k, axis=0)
    top = out[:8]
    rid = lax.broadcasted_iota(jnp.int32, top.shape, 0)
    for t in range(k):
        top = jnp.where(rid == t, halo_rows[t], top)
    return jnp.concatenate([top, out[8:]], axis=0)


def _shift_up(z, k, halo_rows):
    n = z.shape[0]
    out = pltpu.roll(z, n - k, axis=0)
    bottom = out[n - 8:]
    rid = lax.broadcasted_iota(jnp.int32, bottom.shape, 0)
    for t in range(k):
        bottom = jnp.where(rid == 8 - k + t, halo_rows[t], bottom)
    return jnp.concatenate([out[:n - 8], bottom], axis=0)


def _e_spec(chunk, tm):
    return pl.BlockSpec((1, tm, D), lambda i, c=chunk: (c, i, 0))


def _e_prev_spec(chunk, tm):
    return pl.BlockSpec((1, 16, D), lambda i, c=chunk: (c, jnp.maximum(i * (tm // 16) - 1, 0), 0))


def _e_next_spec(chunk, tm, S):
    return pl.BlockSpec((1, 16, D), lambda i, c=chunk: (c, jnp.minimum((i + 1) * (tm // 16), S // 16 - 1), 0))


def _mix(o_attn, e, cw8, ba, bb, w_bat, w_bc):
    S = o_attn.shape[0]
    tm = 512

    def body(o_ref, cb_ref, cc_ref, cx_ref, ga_ref, gb_ref, ccp_ref, cxp_ref, cw_ref, ba_ref, bb_ref, wba_ref, wbc_ref,
             obf_ref, cbu_ref, ya_ref, yc_ref, mg_ref):
        i = pl.program_id(0)
        o = o_ref[...].astype(BF16)
        obf_ref[...] = o
        ya = _nt(o, wba_ref[...])
        z = cc_ref[0].astype(F32) * cx_ref[0].astype(F32)
        zp = ccp_ref[0].astype(F32) * cxp_ref[0].astype(F32) * (i > 0).astype(F32)
        z1 = _shift_down(z, 1, [zp[15:16]])
        z2 = _shift_down(z, 2, [zp[14:15], zp[15:16]])
        cw = cw_ref[...]
        u = cw[0:1] * z2 + cw[1:2] * z1 + cw[2:3] * z
        cbu = (cb_ref[0].astype(F32) * u).astype(BF16)
        cbu_ref[...] = cbu
        yc = _nn(cbu, wbc_ref[...])
        sa = _sigmoid(ga_ref[0].astype(F32) + ba_ref[...])
        sb = _sigmoid(gb_ref[0].astype(F32) + bb_ref[...])
        ya_ref[...] = ya.astype(BF16)
        yc_ref[...] = yc.astype(BF16)
        mg_ref[...] = (sa * ya + sb * yc).astype(BF16)

    row = lambda w: pl.BlockSpec((tm, w), lambda i: (i, 0))
    return pl.pallas_call(
        body, name="mix", grid=(S // tm,),
        in_specs=[row(AOW)] + [_e_spec(c, tm) for c in range(5)] + [_e_prev_spec(1, tm), _e_prev_spec(2, tm),
                  _const_spec((8, D)), _vec_spec(), _vec_spec(), _const_spec((D, AOW)), _const_spec((D, D))],
        out_specs=[row(AOW), row(D), row(D), row(D), row(D)],
        out_shape=[jax.ShapeDtypeStruct((S, AOW), BF16)] + [jax.ShapeDtypeStruct((S, D), BF16)] * 4,
        compiler_params=_cparams("parallel"),
    )(o_attn, e, e, e, e, e, e, e, cw8, ba, bb, w_bat, w_bc)


def _out_proj(merged, w_out, x, gate1, g_mlp, sc2, sh2):
    S = x.shape[0]
    tm = TM

    def body(mg_ref, w_ref, x_ref, gt_ref, g_ref, sc_ref, sh_ref, x1_ref, mo_ref, h2_ref):
        mo = _nn(mg_ref[...], w_ref[...])
        mo_ref[...] = mo.astype(BF16)
        x1 = x_ref[...] + gt_ref[...] * mo
        x1_ref[...] = x1
        h2 = x1 * _rms_r(x1) * g_ref[...] * (1.0 + sc_ref[...]) + sh_ref[...]
        h2_ref[...] = h2.astype(BF16)

    row = pl.BlockSpec((tm, D), lambda i: (i, 0))
    return pl.pallas_call(
        body, name="out_proj", grid=(S // tm,),
        in_specs=[row, _const_spec((D, D)), row, _vec_spec(), _vec_spec(), _vec_spec(), _vec_spec()],
        out_specs=[row, row, row],
        out_shape=[jax.ShapeDtypeStruct((S, D), F32), jax.ShapeDtypeStruct((S, D), BF16), jax.ShapeDtypeStruct((S, D), BF16)],
        compiler_params=_cparams("parallel"),
    )(merged, w_out, x, gate1, g_mlp, sc2, sh2)


def _mlp_in(h2, w_mit):
    S = h2.shape[0]
    tm, tn = TM, 2048

    def body(h_ref, w_ref, a_ref, f_ref):
        a = _nt(h_ref[...], w_ref[...])
        a_ref[...] = a.astype(BF16)
        f_ref[...] = jnp.square(jnp.maximum(a, 0.0)).astype(BF16)

    blk = pl.BlockSpec((tm, tn), lambda i, j: (i, j))
    return pl.pallas_call(
        body, name="mlp_in", grid=(S // tm, DFF // tn),
        in_specs=[pl.BlockSpec((tm, D), lambda i, j: (i, 0)), pl.BlockSpec((tn, D), lambda i, j: (j, 0))],
        out_specs=[blk, blk],
        out_shape=[jax.ShapeDtypeStruct((S, DFF), BF16)] * 2,
        compiler_params=_cparams("parallel", "parallel"),
    )(h2, w_mit)


def _mlp_out(f, w_mo, x1, gate2, g_fin, tgt):
    S = x1.shape[0]
    tm = 512
    half = tm // 2

    def body(f_ref, w_ref, x1_ref, gt_ref, g_ref, t_ref, mlp_ref, dx2_ref, pv_ref):
        @pl.when(pl.program_id(0) == 0)
        def _():
            pv_ref[...] = jnp.zeros_like(pv_ref)

        g = g_ref[...]
        for hs in (pl.ds(0, half), pl.ds(half, half)):
            mlp = _nn(f_ref[hs, :], w_ref[...])
            mlp_ref[hs, :] = mlp.astype(BF16)
            x2 = x1_ref[hs, :] + gt_ref[...] * mlp
            r = _rms_r(x2)
            err = x2 * r * g - t_ref[hs, :]
            dx2, pg = _rms_bwd(x2, r, g, err * (1.0 / D))
            dx2_ref[hs, :] = dx2
            pv_ref[0:1, :] += _rowsum(pg)
            pv_ref[1:2, :] += 0.5 * _rowsum(jnp.mean(err * err, axis=-1, keepdims=True))

    row = pl.BlockSpec((tm, D), lambda i: (i, 0))
    return pl.pallas_call(
        body, name="mlp_out", grid=(S // tm,),
        in_specs=[pl.BlockSpec((tm, DFF), lambda i: (i, 0)), _const_spec((DFF, D)), row, _vec_spec(), _vec_spec(), row],
        out_specs=[row, row, _const_spec((8, D))],
        out_shape=[jax.ShapeDtypeStruct((S, D), BF16), jax.ShapeDtypeStruct((S, D), F32), jax.ShapeDtypeStruct((8, D), F32)],
        compiler_params=_cparams("arbitrary"),
    )(f, w_mo, x1, gate2, g_fin, tgt)


def _bwd_mlp_a(dx2, gate2, mlp, w_mo, a):
    S = dx2.shape[0]
    tm = 512
    half = tm // 2

    def body(dx_ref, gt_ref, mlp_ref, w_ref, a_ref, da_ref, dmo_ref, pv_ref):
        @pl.when(pl.program_id(0) == 0)
        def _():
            pv_ref[...] = jnp.zeros_like(pv_ref)

        for hs in (pl.ds(0, half), pl.ds(half, half)):
            dx = dx_ref[hs, :]
            dmo = (dx * gt_ref[...]).astype(BF16)
            dmo_ref[hs, :] = dmo
            pv_ref[0:1, :] += _rowsum(dx * mlp_ref[hs, :].astype(F32))
            df = _nt(dmo, w_ref[...])
            da_ref[hs, :] = (df * (2.0 * jnp.maximum(a_ref[hs, :].astype(F32), 0.0))).astype(BF16)

    row = pl.BlockSpec((tm, D), lambda i: (i, 0))
    wide = pl.BlockSpec((tm, DFF), lambda i: (i, 0))
    return pl.pallas_call(
        body, name="bwd_mlp_a", grid=(S // tm,),
        in_specs=[row, _vec_spec(), row, _const_spec((DFF, D)), wide],
        out_specs=[wide, row, _const_spec((8, D))],
        out_shape=[jax.ShapeDtypeStruct((S, DFF), BF16), jax.ShapeDtypeStruct((S, D), BF16), jax.ShapeDtypeStruct((8, D), F32)],
        compiler_params=_cparams("arbitrary"),
    )(dx2, gate2, mlp, w_mo, a)


def _bwd_mlp_b(da, w_mit, x1, dx2, g_mlp, sc2):
    S = x1.shape[0]
    tm = 512
    half = tm // 2

    def body(da_ref, w_ref, x1_ref, dx2_ref, g_ref, sc_ref, dx1_ref, pv_ref):
        @pl.when(pl.program_id(0) == 0)
        def _():
            pv_ref[...] = jnp.zeros_like(pv_ref)

        g = g_ref[...]
        for hs in (pl.ds(0, half), pl.ds(half, half)):
            dh = _nn(da_ref[hs, :], w_ref[...])
            x1 = x1_ref[hs, :]
            r = _rms_r(x1)
            dxn, pg = _rms_bwd(x1, r, g, dh * (1.0 + sc_ref[...]))
            dx1_ref[hs, :] = dx2_ref[hs, :] + dxn
            pv_ref[0:1, :] += _rowsum(dh)
            pv_ref[1:2, :] += _rowsum(dh * (x1 * r * g))
            pv_ref[2:3, :] += _rowsum(pg)

    row = pl.BlockSpec((tm, D), lambda i: (i, 0))
    return pl.pallas_call(
        body, name="bwd_mlp_b", grid=(S // tm,),
        in_specs=[pl.BlockSpec((tm, DFF), lambda i: (i, 0)), _const_spec((DFF, D)), row, row, _vec_spec(), _vec_spec()],
        out_specs=[row, _const_spec((8, D))],
        out_shape=[jax.ShapeDtypeStruct((S, D), F32), jax.ShapeDtypeStruct((8, D), F32)],
        compiler_params=_cparams("arbitrary"),
    )(da, w_mit, x1, dx2, g_mlp, sc2)


def _bwd_mix(dx1, gate1, mo, e, cw8, ba, bb, ya, yc, o_attn, w_out, w_bc, w_bat):
    S = dx1.shape[0]
    tm = 256
    n_tiles = S // tm

    def body(dx_ref, dxn_ref, gt_ref, mo_ref, cb_ref, cc_ref, cx_ref, ga_ref, gb_ref, cbn_ref, gbn_ref, ccp_ref, cxp_ref,
             cw_ref, ba_ref, bb_ref, ya_ref, yc_ref, o_ref, wout_ref, wbc_ref, wba_ref,
             dmo_ref, dya_ref, dyc_ref, do_ref, dl_ref, de_ref, pv_ref):
        i = pl.program_id(0)

        @pl.when(i == 0)
        def _():
            pv_ref[...] = jnp.zeros_like(pv_ref)

        dx = dx_ref[...]
        cb = cb_ref[0].astype(F32)
        cc = cc_ref[0].astype(F32)
        cx = cx_ref[0].astype(F32)
        dmo_all = (jnp.concatenate([dx, dxn_ref[...]], axis=0) * gt_ref[...]).astype(BF16)
        dmg_all = _nt(dmo_all, wout_ref[...])
        sb_all = _sigmoid(jnp.concatenate([gb_ref[0], gbn_ref[0]], axis=0).astype(F32) + bb_ref[...])
        dyc_all = dmg_all * sb_all
        dcbu_all = _nt(dyc_all.astype(BF16), wbc_ref[...])
        dmo, dmg, sb, dyc, dcbu = dmo_all[:tm], dmg_all[:tm], sb_all[:tm], dyc_all[:tm], dcbu_all[:tm]
        dmo_ref[...] = dmo
        pv_ref[0:1, :] += _rowsum(dx * mo_ref[...].astype(F32))
        sa = _sigmoid(ga_ref[0].astype(F32) + ba_ref[...])
        dya = (dmg * sa).astype(BF16)
        dya_ref[...] = dya
        dyc_ref[...] = dyc.astype(BF16)
        dga = dmg * ya_ref[...].astype(F32) * sa * (1.0 - sa)
        dgb = dmg * yc_ref[...].astype(F32) * sb * (1.0 - sb)
        pv_ref[1:2, :] += _rowsum(dga)
        pv_ref[2:3, :] += _rowsum(dgb)

        do = _nn(dya, wba_ref[...])
        do_ref[...] = do
        prod = do * o_ref[...]
        dl_ref[...] = jnp.concatenate(
            [jnp.broadcast_to(jnp.sum(prod[:, s * HEAD:(s + 1) * HEAD], axis=-1, keepdims=True), (tm, HEAD))
             for s in range(N_SLOT)], axis=1)

        z = cc * cx
        zp = ccp_ref[0].astype(F32) * cxp_ref[0].astype(F32) * (i > 0).astype(F32)
        z1 = _shift_down(z, 1, [zp[15:16]])
        z2 = _shift_down(z, 2, [zp[14:15], zp[15:16]])
        cw = cw_ref[...]
        u = cw[0:1] * z2 + cw[1:2] * z1 + cw[2:3] * z
        du = dcbu * cb
        du_n = dcbu_all[tm:] * cbn_ref[0].astype(F32) * (i < n_tiles - 1).astype(F32)
        du1 = _shift_up(du, 1, [du_n[0:1]])
        du2 = _shift_up(du, 2, [du_n[0:1], du_n[1:2]])
        dz = cw[2:3] * du + cw[1:2] * du1 + cw[0:1] * du2
        pv_ref[3:4, :] += _rowsum(du * z2)
        pv_ref[4:5, :] += _rowsum(du * z1)
        pv_ref[5:6, :] += _rowsum(du * z)

        de_ref[0] = (dcbu * u).astype(BF16)
        de_ref[1] = (dz * cx).astype(BF16)
        de_ref[2] = (dz * cc).astype(BF16)
        de_ref[3] = dga.astype(BF16)
        de_ref[4] = dgb.astype(BF16)

    row = lambda w: pl.BlockSpec((tm, w), lambda i: (i, 0))
    nxt = pl.BlockSpec((16, D), lambda i: (jnp.minimum((i + 1) * (tm // 16), S // 16 - 1), 0))
    return pl.pallas_call(
        body, name="bwd_mix", grid=(n_tiles,),
        in_specs=[row(D), nxt, _vec_spec(), row(D)] + [_e_spec(c, tm) for c in range(5)]
                 + [_e_next_spec(0, tm, S), _e_next_spec(4, tm, S), _e_prev_spec(1, tm), _e_prev_spec(2, tm),
                    _const_spec((8, D)), _vec_spec(), _vec_spec(), row(D), row(D), row(AOW),
                    _const_spec((D, D)), _const_spec((D, D)), _const_spec((D, AOW))],
        out_specs=[row(D), row(D), row(D), row(AOW), row(AOW), pl.BlockSpec((5, tm, D), lambda i: (0, i, 0)),
                   _const_spec((8, D))],
        out_shape=[jax.ShapeDtypeStruct((S, D), BF16)] * 3 + [jax.ShapeDtypeStruct((S, AOW), F32)] * 2
                  + [jax.ShapeDtypeStruct((5, S, D), BF16), jax.ShapeDtypeStruct((8, D), F32)],
        compiler_params=_cparams("arbitrary"),
    )(dx1, dx1, gate1, mo, e, e, e, e, e, e, e, e, e, cw8, ba, bb, ya, yc, o_attn, w_out, w_bc, w_bat)


def _attn_bwd(qkv, do, lse, dl, bias_t):
    S = qkv.shape[2]
    nblk = S // HEAD

    def body(qkv_ref, do_ref, lse_ref, dl_ref, b_ref, d_ref):
        g = pl.program_id(1)
        bias = b_ref[0, 0]
        col = lax.broadcasted_iota(jnp.int32, bias.shape, 1)
        bias_last = jnp.where(col >= HEAD, NEG, bias)
        eye = (lax.broadcasted_iota(jnp.int32, (HEAD, HEAD), 0) == lax.broadcasted_iota(jnp.int32, (HEAD, HEAD), 1)).astype(F32)

        def as_row(t):
            return jnp.sum(t * eye, axis=0, keepdims=True)

        for gi, d in enumerate(DILATIONS):
            @pl.when(g == gi)
            def _(d=d):
                nb = nblk // d

                def query_side(start):
                    sl = pl.ds(start, HEAD, stride=d)
                    return (qkv_ref.at[0, 0][sl, :].astype(BF16), do_ref[sl, :].astype(BF16),
                            as_row(lse_ref[sl, :]), as_row(dl_ref[sl, :]))

                def step(b, first_of_residue, carry):
                    dq_part, own = carry
                    r, n = b // nb, b % nb
                    cur = pl.ds(n * (HEAD * d) + r, HEAD, stride=d)
                    if first_of_residue:
                        own = query_side(r)
                    nxt = query_side(jnp.minimum(n + 1, nb - 1) * (HEAD * d) + r)
                    q2 = jnp.concatenate([own[0], nxt[0]], axis=0)
                    do2 = jnp.concatenate([own[1], nxt[1]], axis=0)
                    k = qkv_ref.at[0, 1][cur, :].astype(BF16)
                    v = qkv_ref.at[0, 2][cur, :].astype(BF16)
                    s = _nt(k, q2) * SCALE + jnp.where(n < nb - 1, bias, bias_last)
                    p = jnp.exp(s - jnp.concatenate([own[2], nxt[2]], axis=1))
                    d_ref.at[0, 2][cur, :] = _nn(p.astype(BF16), do2)
                    dp = _nt(v, do2)
                    ds = (p * (dp - jnp.concatenate([own[3], nxt[3]], axis=1)) * SCALE).astype(BF16)
                    d_ref.at[0, 1][cur, :] = _nn(ds, q2)
                    dq2 = _tn(ds, k)
                    d_ref.at[0, 0][cur, :] = dq2[:HEAD] + jnp.where(n > 0, dq_part, 0.0)
                    return dq2[HEAD:], nxt

                def steps(i, carry):
                    for u in range(UNROLL):
                        carry = step(i * UNROLL + u, nb <= UNROLL and u % nb == 0, carry)
                    return carry

                lax.fori_loop(0, nblk // UNROLL, steps, (jnp.zeros((HEAD, HEAD), F32), query_side(0)))

    col_blk = pl.BlockSpec((S, HEAD), lambda j, g: (0, j))
    qkv_blk = pl.BlockSpec((1, 3, S, HEAD), lambda j, g: (g, 0, 0, j))
    return pl.pallas_call(
        body, name="attn_bwd", grid=(N_SLOT, 3),
        in_specs=[qkv_blk, col_blk, col_blk, col_blk, pl.BlockSpec((1, 1, HEAD, 2 * HEAD), lambda j, g: (g, j, 0, 0))],
        out_specs=qkv_blk,
        out_shape=jax.ShapeDtypeStruct((3, 3, S, AOW), F32),
        compiler_params=_cparams("parallel", "arbitrary"),
    )(qkv, do, lse, dl, bias_t)


def _bwd_in(dqkv, de, w_int, x, dx1, g_mix, sc1):
    S = x.shape[0]
    tm = TM
    dqkv = dqkv.reshape(3, 3, S, AOW)

    def body(dq_ref, de_ref, wq_ref, wk_ref, wv_ref, wa_ref, wb_ref, x_hbm, dx1_hbm, g_ref, sc_ref, gx_ref, pv_ref,
             x_ref, dx1_ref, sems):
        acc = gx_ref
        i, k = pl.program_id(0), pl.program_id(1)
        tile = pl.ds(pl.multiple_of(i * tm, tm), tm)
        late = [pltpu.make_async_copy(x_hbm.at[tile, :], x_ref, sems.at[0]),
                pltpu.make_async_copy(dx1_hbm.at[tile, :], dx1_ref, sems.at[1])]

        @pl.when((i == 0) & (k == 0))
        def _():
            pv_ref[...] = jnp.zeros_like(pv_ref)

        @pl.when(k == 0)
        def _():
            acc[...] = jnp.zeros_like(acc)
            for cp in late:
                cp.start()

        @pl.when(k < 3)
        def _():
            lhs = jnp.concatenate([dq_ref[0, t].astype(BF16) for t in range(3)], axis=1)
            acc[...] += _nn(lhs, jnp.concatenate([wq_ref[...], wk_ref[...], wv_ref[...]], axis=0))

        @pl.when(k >= 3)
        def _():
            acc[...] += _nn(de_ref[0], jnp.concatenate([wa_ref[...], wb_ref[...]], axis=0))

        @pl.when(k == 7)
        def _():
            for cp in late:
                cp.wait()
            dh = acc[...]
            xv = x_ref[...]
            r = _rms_r(xv)
            g = g_ref[...]
            dxn, pg = _rms_bwd(xv, r, g, dh * (1.0 + sc_ref[...]))
            gx_ref[...] = dx1_ref[...] + dxn
            pv_ref[0:1, :] += _rowsum(dh)
            pv_ref[1:2, :] += _rowsum(dh * (xv * r * g))
            pv_ref[2:3, :] += _rowsum(pg)

    grp = lambda k: jnp.minimum(k, 2)
    chunk = lambda k: jnp.maximum(k - 3, 0)
    wblk = lambda f: pl.BlockSpec((512, D), lambda i, k: (f(k), 0))
    row = pl.BlockSpec((tm, D), lambda i, k: (i, 0))
    anyspec = pl.BlockSpec(memory_space=pl.ANY)
    return pl.pallas_call(
        body, name="bwd_in", grid=(S // tm, 8),
        in_specs=[pl.BlockSpec((1, 3, tm, 512), lambda i, k: (grp(k), 0, i, 0)),
                  pl.BlockSpec((1, tm, D), lambda i, k: (chunk(k), i, 0)),
                  wblk(grp), wblk(lambda k: 3 + grp(k)), wblk(lambda k: 6 + grp(k)),
                  wblk(lambda k: 9 + 2 * chunk(k)), wblk(lambda k: 10 + 2 * chunk(k)),
                  anyspec, anyspec, _vec_spec(), _vec_spec()],
        out_specs=[row, _const_spec((8, D))],
        out_shape=[jax.ShapeDtypeStruct((S, D), F32), jax.ShapeDtypeStruct((8, D), F32)],
        scratch_shapes=[pltpu.VMEM((tm, D), F32), pltpu.VMEM((tm, D), F32), pltpu.SemaphoreType.DMA((2,))],
        compiler_params=_cparams("arbitrary", "arbitrary"),
    )(dqkv, de, w_int, w_int, w_int, w_int, w_int, x, dx1, g_mix, sc1)


def _grad_w(name, a, b):
    S, ka = a.shape
    nb = b.shape[1]

    def body(a_ref, b_ref, o_ref):
        o_ref[...] = _tn(a_ref[...], b_ref[...]).astype(BF16)

    return pl.pallas_call(
        body, name=name, grid=(ka // 512,),
        in_specs=[pl.BlockSpec((S, 512), lambda n: (0, n)), pl.BlockSpec((S, nb), lambda n: (0, 0))],
        out_specs=pl.BlockSpec((512, nb), lambda n: (n, 0)),
        out_shape=jax.ShapeDtypeStruct((ka, nb), BF16),
        compiler_params=_cparams("parallel"),
    )(a, b)


def _grad_w_small(dya, o_bf, cbu, dyc, merged, dmo, after):
    S = dya.shape[0]

    def body(dya_h, o_h, cbu_h, dyc_h, mg_h, dmo_h, after_ref, gba_ref, gbc_ref, gout_ref, a0, a1, b_small, b1, b2, sems):
        fetch = [pltpu.make_async_copy(src, buf, sems.at[k])
                 for k, (src, buf) in enumerate(((dya_h, a0), (o_h, b_small), (cbu_h, a1), (dyc_h, b1), (dmo_h, b2)))]
        for cp in fetch:
            cp.start()
        fetch[0].wait()
        fetch[1].wait()
        gba_ref[...] = _tn(a0[...], b_small[...]).astype(BF16)
        last = pltpu.make_async_copy(mg_h, a0, sems.at[5])
        last.start()
        fetch[2].wait()
        fetch[3].wait()
        gbc_ref[...] = _tn(a1[...], b1[...]).astype(BF16)
        fetch[4].wait()
        last.wait()
        gout_ref[...] = _tn(a0[...], b2[...]).astype(BF16)

    anyspec = pl.BlockSpec(memory_space=pl.ANY)
    vmem = pl.BlockSpec(memory_space=pltpu.VMEM)
    wide = pltpu.VMEM((S, D), BF16)
    return pl.pallas_call(
        body, name="grad_w_small",
        in_specs=[anyspec] * 7, out_specs=[vmem] * 3,
        out_shape=[jax.ShapeDtypeStruct((D, AOW), BF16), jax.ShapeDtypeStruct((D, D), BF16), jax.ShapeDtypeStruct((D, D), BF16)],
        scratch_shapes=[wide, wide, pltpu.VMEM((S, AOW), BF16), wide, wide, pltpu.SemaphoreType.DMA((6,))],
        compiler_params=_cparams(),
    )(dya, o_bf, cbu, dyc, merged, dmo, after)


def _grad_w_in(dqkv, de, h):
    S = h.shape[0]

    def body(dq_ref, de_ref, h_ref, o_ref):
        n = pl.program_id(0)

        @pl.when(n < 9)
        def _():
            o_ref[...] = _tn(dq_ref[0].astype(BF16), h_ref[...]).astype(BF16)

        @pl.when(n >= 9)
        def _():
            o_ref[...] = _tn(de_ref[0], h_ref[...]).astype(BF16)

    def e_idx(n):
        kk = jnp.maximum(n - 9, 0)
        return (kk // 2, 0, kk % 2)

    return pl.pallas_call(
        body, name="grad_w_in", grid=(19,),
        in_specs=[pl.BlockSpec((1, S, 512), lambda n: (jnp.minimum(n, 8), 0, 0)), pl.BlockSpec((1, S, 512), e_idx),
                  pl.BlockSpec((S, D), lambda n: (0, 0))],
        out_specs=pl.BlockSpec((512, D), lambda n: (_win_rowblock(n), 0)),
        out_shape=jax.ShapeDtypeStruct((19 * 512, D), BF16),
        compiler_params=_cparams("parallel"),
    )(dqkv, de, h)


def _local_step(x, h, tgt, mod, g_mix, g_mlp, g_fin, ba, bb, cw8, w_int, mix_weights, mlp_weights, mlp_grads_ready, w_in_grad_ready,
                other_grads_ready):
    S = x.shape[0]
    sh1, sc1, gt1, sh2, sc2, gt2 = [mod[k:k + 1] for k in range(6)]
    bias, bias_t = _bias_table()

    qkv, e = _proj(h, w_int)
    qkv = qkv.reshape(3, 3, S, AOW)
    o_attn, lse = _attn_fwd(qkv, bias)
    w_bat, w_bc, w_out = mix_weights(o_attn)
    o_bf, cbu, ya, yc, merged = _mix(o_attn, e, cw8, ba, bb, w_bat, w_bc)
    x1, mo, h2 = _out_proj(merged, w_out, x, gt1, g_mlp, sc2, sh2)
    w_mit, w_mo = mlp_weights(x1)
    a, f = _mlp_in(h2, w_mit)
    mlp, dx2, pv_f = _mlp_out(f, w_mo, x1, gt2, g_fin, tgt)

    da, dmo2, pv_a = _bwd_mlp_a(dx2, gt2, mlp, w_mo, a)
    dx1, pv_b = _bwd_mlp_b(da, w_mit, x1, dx2, g_mlp, sc2)
    zero = mlp_grads_ready(_grad_w("grad_w_mi", da, h2), _grad_w("grad_w_mo", f, dmo2))
    dmo, dya, dyc, do, dl, de, pv_m = _bwd_mix(dx1, gt1 + zero, mo, e, cw8, ba, bb, ya, yc, o_attn, w_out, w_bc, w_bat)
    dqkv = _attn_bwd(qkv, do, lse, dl, bias_t).reshape(9, S, AOW)
    after = w_in_grad_ready(_grad_w_in(dqkv, de, h))
    zero = other_grads_ready(*_grad_w_small(dya, o_bf, cbu, dyc, merged, dmo, after))
    grad_x, pv_i = _bwd_in(dqkv, de, w_int, x, dx1, g_mix, sc1 + zero)

    vec = jnp.concatenate([pv_i[0:2], pv_m[0:1], pv_b[0:2], pv_a[0:1], pv_i[2:3], pv_b[2:3], pv_f[0:1],
                           pv_m[1:3], pv_m[3:6], pv_f[1:2], jnp.zeros((1, D), F32)], axis=0)
    return grad_x, vec


def _my_place():
    return lax.axis_index("x"), lax.axis_index("y"), lax.axis_index("c")


def _dev_index(px, py, pc):
    return 4 * px + 2 * py + pc


def _peer(x, y, c, m):
    return (x ^ ((m >> 2) & 1), y ^ ((m >> 1) & 1), c ^ (m & 1))


HBM_SPEC = pl.BlockSpec(memory_space=pltpu.HBM)
SEM_SPEC = pl.BlockSpec(memory_space=pltpu.SEMAPHORE)
N_PEER = N_DEV - 1


SPLIT_MASKS = {"gather": tuple(range(1, N_DEV)), "scatter": tuple(range(1, N_DEV)), "chips": (2, 4, 6), "sibling": (1, 1, 1, 1)}


def _split_copy(mode, src_ref, land_ref, send_sems, recv_sems, w, j, place, arriving=False):
    x, y, c = place
    masks = SPLIT_MASKS[mode]
    peer = _peer(x, y, c, masks[j])
    k = w * len(masks) + j
    sender, receiver = ((peer, (x, y, c)) if arriving else ((x, y, c), peer))
    if mode == "gather":
        r = src_ref.shape[0]
        src, dst = src_ref, land_ref.at[pl.ds(pl.multiple_of(_dev_index(*sender) * r, 16), r), :]
    elif mode == "scatter":
        r = land_ref.shape[1]
        src, dst = src_ref.at[pl.ds(pl.multiple_of(_dev_index(*receiver) * r, 16), r), :], land_ref.at[j]
    elif mode == "chips":
        src, dst = src_ref.at[2 * receiver[0] + receiver[1]], land_ref.at[j]
    else:
        r = land_ref.shape[1]
        src, dst = src_ref.at[pl.ds(pl.multiple_of((2 * j + receiver[2]) * r, 16), r), :], land_ref.at[j]
    return pltpu.make_async_remote_copy(src_ref=src, dst_ref=dst, send_sem=send_sems.at[k], recv_sem=recv_sems.at[k],
                                        device_id=peer, device_id_type=MESH)


def _split_start(name, mode, srcs, lands):
    n = len(srcs)
    nm = len(SPLIT_MASKS[mode])

    def body(*refs):
        src, land = refs[:n], refs[n:2 * n]
        send_sems, recv_sems = refs[2 * n], refs[2 * n + 1]
        token = refs[-1]
        place = _my_place()
        for w in range(n):
            for j in range(nm):
                _split_copy(mode, src[w], land[w], send_sems, recv_sems, w, j, place).start()
        token[...] = jnp.zeros_like(token)

    hbm = lambda t: pltpu.HBM(t.shape, t.dtype)
    out = pl.pallas_call(
        body, name=name,
        out_shape=(pltpu.SemaphoreType.DMA((n * nm,)), pltpu.SemaphoreType.DMA((n * nm,)), *[hbm(t) for t in srcs],
                   *[hbm(t) for t in lands], jax.ShapeDtypeStruct((8, 128), F32)),
        in_specs=(HBM_SPEC,) * (2 * n),
        out_specs=(SEM_SPEC, SEM_SPEC) + (HBM_SPEC,) * (2 * n) + (pl.BlockSpec(memory_space=pltpu.VMEM),),
        input_output_aliases={i: 2 + i for i in range(2 * n)},
        compiler_params=pltpu.CompilerParams(has_side_effects=pltpu.SideEffectType.DATAFLOW_SIDE_EFFECTING),
    )(*[pltpu.with_memory_space_constraint(t, pltpu.HBM) for t in (*srcs, *lands)])
    return out[0], out[1], out[2:2 + n], out[2 + n:2 + 2 * n], out[-1][0:1, 0:1], out[-1]


def _split_wait(name, mode, send_sems, recv_sems, srcs, lands, after):
    n = len(srcs)

    def body(*refs):
        src, land = refs[:n], refs[n:2 * n]
        ssem, rsem = refs[2 * n], refs[2 * n + 1]
        place = _my_place()
        for w in range(n):
            for j in range(len(SPLIT_MASKS[mode])):
                _split_copy(mode, src[w], land[w], ssem, rsem, w, j, place).wait_send()
                _split_copy(mode, src[w], land[w], ssem, rsem, w, j, place, arriving=True).wait_recv()

    hbm = lambda t: pltpu.HBM(t.shape, t.dtype)
    out = pl.pallas_call(
        body, name=name,
        out_shape=tuple(hbm(t) for t in (*srcs, *lands)),
        in_specs=(HBM_SPEC,) * (2 * n) + (SEM_SPEC, SEM_SPEC, pl.BlockSpec(memory_space=pl.ANY)),
        out_specs=(HBM_SPEC,) * (2 * n),
        input_output_aliases={i: i for i in range(2 * n)},
        compiler_params=pltpu.CompilerParams(has_side_effects=pltpu.SideEffectType.DATAFLOW_SIDE_EFFECTING),
    )(*srcs, *lands, send_sems, recv_sems, after)
    return out[:n], out[n:]


def _sibling_exchange(grads):
    nw = len(grads)
    HBM = pl.BlockSpec(memory_space=pl.ANY)

    def body(*refs):
        g, land = refs[:nw], refs[nw:2 * nw]
        send_sems, recv_sems = refs[2 * nw:]
        x, y, c = _my_place()

        def copy(w, q, owner_core):
            r = land[w].shape[1]
            return pltpu.make_async_remote_copy(
                src_ref=g[w].at[pl.ds(pl.multiple_of((2 * q + owner_core) * r, 16), r), :], dst_ref=land[w].at[q],
                send_sem=send_sems.at[w, q], recv_sem=recv_sems.at[w, q], device_id=(x, y, 1 - c), device_id_type=MESH)

        sends = [copy(w, q, 1 - c) for w in range(nw) for q in range(4)]
        for cp in sends:
            cp.start()
        for w in range(nw):
            for q in range(4):
                copy(w, q, c).wait_recv()
        for cp in sends:
            cp.wait_send()

    return pl.pallas_call(
        body, name="sibling_exchange",
        out_shape=[jax.ShapeDtypeStruct((4, a.shape[0] // N_DEV, a.shape[1]), a.dtype) for a in grads],
        in_specs=[HBM] * nw, out_specs=[HBM] * nw,
        scratch_shapes=[pltpu.SemaphoreType.DMA((nw, 4)), pltpu.SemaphoreType.DMA((nw, 4))],
    )(*grads)


def _pair_sums(gs, sibs, core):
    n = len(gs)

    def body(core_ref, *refs):
        for w in range(n):
            refs[2 * n + w][0] = (refs[w][0, 0].astype(F32) + refs[n + w][0].astype(F32)).astype(BF16)

    in_specs = [pl.BlockSpec((1, 1) + t.shape[1:], lambda q, core_ref: (q, core_ref[0], 0, 0)) for t in sibs]
    in_specs += [pl.BlockSpec((1,) + t.shape[1:], lambda q, core_ref: (q, 0, 0)) for t in sibs]
    return pl.pallas_call(
        body, name="pair_sums",
        grid_spec=pltpu.PrefetchScalarGridSpec(
            num_scalar_prefetch=1, grid=(4,), in_specs=in_specs,
            out_specs=[pl.BlockSpec((1,) + t.shape[1:], lambda q, core_ref: (q, 0, 0)) for t in sibs]),
        out_shape=[jax.ShapeDtypeStruct(t.shape, BF16) for t in sibs],
        compiler_params=_cparams("parallel"),
    )(core, *[g.reshape(4, 2, t.shape[1], t.shape[2]) for g, t in zip(gs, sibs)], *sibs)


def _own_rows_into_zones(shards, me):
    n = len(shards)

    def body(me_ref, *refs):
        for w in range(n):
            refs[2 * n + w][...] = refs[w][...]

    zones = [lax.empty((N_DEV * t.shape[0], t.shape[1]), t.dtype) for t in shards]
    return pl.pallas_call(
        body, name="own_rows_into_zones",
        grid_spec=pltpu.PrefetchScalarGridSpec(
            num_scalar_prefetch=1, grid=(1,),
            in_specs=[pl.BlockSpec(t.shape, lambda i, me_ref: (0, 0)) for t in shards] + [pl.BlockSpec(memory_space=pl.ANY)] * n,
            out_specs=[pl.BlockSpec(t.shape, lambda i, me_ref: (me_ref[0], 0)) for t in shards]),
        out_shape=[jax.ShapeDtypeStruct(z.shape, z.dtype) for z in zones],
        input_output_aliases={1 + n + w: w for w in range(n)},
        compiler_params=_cparams("arbitrary"),
    )(me, *shards, *zones)


def _allgather_small(v, name):
    r, ccols = v.shape

    def body(v_ref, out_ref, sum_ref, send_sems, recv_sems):
        x, y, c = _my_place()
        my_idx = _dev_index(x, y, c)
        out_ref[my_idx] = v_ref[...]

        def copy(m):
            peer = _peer(x, y, c, m)
            return pltpu.make_async_remote_copy(
                src_ref=v_ref, dst_ref=out_ref.at[my_idx],
                send_sem=send_sems.at[m - 1], recv_sem=recv_sems.at[m - 1], device_id=peer, device_id_type=MESH)

        def arrival(m):
            peer = _peer(x, y, c, m)
            return pltpu.make_async_remote_copy(
                src_ref=v_ref, dst_ref=out_ref.at[_dev_index(*peer)],
                send_sem=send_sems.at[m - 1], recv_sem=recv_sems.at[m - 1], device_id=peer, device_id_type=MESH)

        sends = [copy(m) for m in range(1, N_DEV)]
        for cp in sends:
            cp.start()
        for m in range(1, N_DEV):
            arrival(m).wait_recv()
        acc = out_ref[0]
        for s in range(1, N_DEV):
            acc = acc + out_ref[s]
        sum_ref[...] = acc
        for cp in sends:
            cp.wait_send()

    vmem = pl.BlockSpec(memory_space=pltpu.VMEM)
    return pl.pallas_call(
        body, name=name,
        out_shape=[jax.ShapeDtypeStruct((N_DEV, r, ccols), v.dtype), jax.ShapeDtypeStruct((r, ccols), v.dtype)],
        in_specs=[vmem], out_specs=[vmem, vmem],
        scratch_shapes=[pltpu.SemaphoreType.DMA((7,)), pltpu.SemaphoreType.DMA((7,))],
    )(v)


def _gather_w_in_and_condition(shard, pay, w_ada, b_cols):
    r, ccols = shard.shape
    ncol = w_ada.shape[1]

    def body(sh_ref, pay_ref, w_ref, b_ref, full_ref, got_ref, act_ref, mod_ref, send_sems, recv_sems, small_send, small_recv, local_sem):
        x, y, c = _my_place()
        me, sibling = (x, y, c), (x, y, 1 - c)
        my_idx = _dev_index(x, y, c)
        chips = [(1 - x, y), (x, 1 - y), (1 - x, 1 - y)]

        def small(rnd, buf, m, arriving=False):
            peer = _peer(x, y, c, m)
            slot = _dev_index(*peer) if arriving else my_idx
            return pltpu.make_async_remote_copy(
                src_ref=buf.at[my_idx], dst_ref=buf.at[slot], send_sem=small_send.at[rnd, m - 1],
                recv_sem=small_recv.at[rnd, m - 1], device_id=peer, device_id_type=MESH)

        def rows(px, py, pc):
            return full_ref.at[pl.ds(pl.multiple_of(_dev_index(px, py, pc) * r, 16), r), :]

        def copy(k, block, to, src=None):
            return pltpu.make_async_remote_copy(
                src_ref=rows(*block) if src is None else src, dst_ref=rows(*block),
                send_sem=send_sems.at[k], recv_sem=recv_sems.at[k], device_id=to, device_id_type=MESH)

        got_ref[my_idx] = pay_ref[...]
        round1 = [small(0, got_ref, m) for m in range(1, N_DEV)]
        for cp in round1:
            cp.start()
        mine = pltpu.make_async_copy(sh_ref, rows(*me), local_sem)
        mine.start()
        first = [copy(0, me, sibling, src=sh_ref)] + [copy(1 + j, me, (*chip, c), src=sh_ref) for j, chip in enumerate(chips)]
        for cp in first:
            cp.start()

        for m in range(1, N_DEV):
            small(0, got_ref, m, arriving=True).wait_recv()
        cv = jnp.concatenate([got_ref[s, 0:1, :] for s in range(N_DEV)], axis=0)
        act = cv * _sigmoid(cv)
        act_ref[...] = act
        mod_ref[my_idx] = jnp.dot(act, w_ref[...], preferred_element_type=F32, precision=lax.Precision.HIGHEST) + b_ref[...]
        round2 = [small(1, mod_ref, m) for m in range(1, N_DEV)]
        for cp in round2:
            cp.start()

        passed = []
        for j, chip in enumerate(chips):
            copy(1 + j, (*chip, c), me).wait_recv()
            fwd = copy(4 + j, (*chip, c), sibling)
            fwd.start()
            passed.append(fwd)
        copy(0, sibling, me).wait_recv()
        for j, chip in enumerate(chips):
            copy(4 + j, (*chip, 1 - c), me).wait_recv()
        for m in range(1, N_DEV):
            small(1, mod_ref, m, arriving=True).wait_recv()
        for cp in first + passed + round1 + round2:
            cp.wait_send()
        mine.wait()

    anyspec = pl.BlockSpec(memory_space=pl.ANY)
    vmem = pl.BlockSpec(memory_space=pltpu.VMEM)
    return pl.pallas_call(
        body, name="gather_w_in_and_condition",
        out_shape=[jax.ShapeDtypeStruct((N_DEV * r, ccols), shard.dtype), jax.ShapeDtypeStruct((N_DEV, 8, D), F32),
                   jax.ShapeDtypeStruct((N_DEV, D), F32), jax.ShapeDtypeStruct((N_DEV, N_DEV, ncol), F32)],
        in_specs=[anyspec, vmem, vmem, vmem], out_specs=[anyspec, vmem, vmem, vmem],
        scratch_shapes=[pltpu.SemaphoreType.DMA((7,)), pltpu.SemaphoreType.DMA((7,)), pltpu.SemaphoreType.DMA((2, 7)),
                        pltpu.SemaphoreType.DMA((2, 7)), pltpu.SemaphoreType.DMA],
        compiler_params=_cparams(),
    )(shard, pay, w_ada, b_cols)


def _row_tile(r):
    for t in (256, 304, 128, 64, 16):
        if r % t == 0:
            return t
    return r


def _adamw_w_ada(w, act_t, gm_cols, m, v):
    r, ccols = w.shape
    tr = _row_tile(r)
    c1 = 1.0 / (1.0 - B1 ** STEP)
    c2 = 1.0 / (1.0 - B2 ** STEP)

    def body(w_ref, a_ref, gm_ref, m_ref, v_ref, g_ref, d_ref, nm_ref, nv_ref):
        gv = jnp.dot(a_ref[...], gm_ref[...], preferred_element_type=F32, precision=lax.Precision.HIGHEST)
        g_ref[...] = gv
        nm = B1 * m_ref[...] + (1.0 - B1) * gv
        nv = B2 * v_ref[...] + (1.0 - B2) * jnp.square(gv)
        nm_ref[...] = nm
        nv_ref[...] = nv
        d_ref[...] = -LR * ((nm * c1) / (jnp.sqrt(nv * c2) + ADAM_EPS) + WD * w_ref[...])

    blk = pl.BlockSpec((tr, ccols), lambda i: (i, 0))
    return pl.pallas_call(
        body, name="adamw_w_ada", grid=(r // tr,),
        in_specs=[blk, pl.BlockSpec((tr, N_DEV), lambda i: (i, 0)), _const_spec(gm_cols.shape), blk, blk], out_specs=[blk] * 4,
        out_shape=[jax.ShapeDtypeStruct((r, ccols), F32)] * 4,
        compiler_params=_cparams("parallel"),
    )(w, act_t, gm_cols, m, v)


def _sum_adamw(parts, own, slot, w, m, v, name, transposed=False):
    k, r, ccols = parts.shape
    tr = _row_tile(r)
    c1 = 1.0 / (1.0 - B1 ** STEP)
    c2 = 1.0 / (1.0 - B2 ** STEP)

    def body(s_ref, p_ref, own_ref, w_ref, m_ref, v_ref, g_ref, d_ref, nm_ref, nv_ref):
        gv = own_ref[0].astype(F32)
        for s in range(k):
            gv = gv + p_ref[s].astype(F32)
        if transposed:
            gv = gv.T
        g_ref[...] = gv
        nm = B1 * m_ref[...] + (1.0 - B1) * gv
        nv = B2 * v_ref[...] + (1.0 - B2) * jnp.square(gv)
        nm_ref[...] = nm
        nv_ref[...] = nv
        d_ref[...] = -LR * ((nm * c1) / (jnp.sqrt(nv * c2) + ADAM_EPS) + WD * w_ref[...])

    if transposed:
        blk = pl.BlockSpec((ccols, tr), lambda i, s_ref: (0, i))
    else:
        blk = pl.BlockSpec((tr, ccols), lambda i, s_ref: (i, 0))
    return pl.pallas_call(
        body, name=name,
        grid_spec=pltpu.PrefetchScalarGridSpec(
            num_scalar_prefetch=1, grid=(r // tr,),
            in_specs=[pl.BlockSpec((k, tr, ccols), lambda i, s_ref: (0, i, 0)),
                      pl.BlockSpec((1, tr, ccols), lambda i, s_ref: (s_ref[0], i, 0))] + [blk] * 3,
            out_specs=[blk] * 4),
        out_shape=[jax.ShapeDtypeStruct(w.shape, F32)] * 4,
        compiler_params=_cparams("parallel"),
    )(slot, parts, own, w, m, v)


VEC_ROWS = ((0, 6), (6, 7), (9, 11), (11, 14), (7, 8), (8, 9))


def _adamw_vectors(w, g, m, v):
    c1 = 1.0 / (1.0 - B1 ** STEP)
    c2 = 1.0 / (1.0 - B2 ** STEP)

    def put(refs, p):
        for ref, (lo, hi) in zip(refs, VEC_ROWS):
            if ref.shape == (3, HEAD):
                ref[...] = p[lo:hi, :HEAD]
            else:
                ref[...] = jnp.concatenate([p[k:k + 1] for k in range(lo, hi)], axis=1)

    def body(w_ref, g_ref, m_ref, v_ref, *outs):
        gv = g_ref[...]
        nm = B1 * m_ref[...] + (1.0 - B1) * gv
        nv = B2 * v_ref[...] + (1.0 - B2) * jnp.square(gv)
        delta = -LR * ((nm * c1) / (jnp.sqrt(nv * c2) + ADAM_EPS) + WD * w_ref[...])
        for kind, p in enumerate((gv, delta, nm, nv)):
            put(outs[6 * kind:6 * kind + 6], p)

    shapes = [(1, 6 * D), (1, D), (1, 2 * D), (3, HEAD), (1, D), (1, D)]
    out = pl.pallas_call(
        body, name="adamw_vectors", out_shape=[jax.ShapeDtypeStruct(sh, F32) for sh in shapes] * 4, compiler_params=_cparams(),
    )(w, g, m, v)
    fix = lambda t: (t[0], t[1], t[2], t[3][None], t[4], t[5].reshape(D))
    return [fix(out[6 * kind:6 * kind + 6]) for kind in range(4)]


def _pack_vectors(b_ada, g_mix, g_mlp, g_fin, b_gate, conv_w):
    conv_rows = jnp.pad(conv_w.reshape(3, HEAD), ((0, 0), (0, D - HEAD)))
    return jnp.concatenate([b_ada.reshape(6, D), g_mix.reshape(1, D), g_mlp.reshape(1, D), g_fin.reshape(1, D),
                            b_gate.reshape(2, D), conv_rows, jnp.zeros((2, D), F32)], axis=0)


def kernel(x, c, w_ada, b_ada, g_norm_mix, w_in, b_gate, conv_w, w_branch_attn, w_branch_conv, w_out, g_norm_mlp, w_mlp_in, w_mlp_out, g_norm_final, loss_target, m_w_ada, m_b_ada, m_g_norm_mix, m_w_in, m_b_gate, m_conv_w, m_w_branch_attn, m_w_branch_conv, m_w_out, m_g_norm_mlp, m_w_mlp_in, m_w_mlp_out, m_g_norm_final, v_w_ada, v_b_ada, v_g_norm_mix, v_w_in, v_b_gate, v_conv_w, v_w_branch_attn, v_w_branch_conv, v_w_out, v_g_norm_mlp, v_w_mlp_in, v_w_mlp_out, v_g_norm_final):
    S = x.shape[1]
    xi, yi, ci = _my_place()
    me = _dev_index(xi, yi, ci)
    x2 = x.reshape(S, D)
    tgt = loss_target.reshape(S, D)

    pay = jnp.zeros((8, D), F32).at[0].set(c[0]).at[1:4, :HEAD].set(conv_w[0])
    ncol = w_ada.shape[2]
    b_cols = lax.dynamic_slice(b_ada, (0, me * ncol), (1, ncol))
    w_int, got, act, mod_all = _gather_w_in_and_condition(w_in[0].T.astype(BF16), pay, w_ada[0], b_cols)
    cw8 = jnp.pad(got[:, 1:4, :HEAD].transpose(1, 0, 2).reshape(3, D), ((0, 5), (0, 0)))
    mod = lax.dynamic_index_in_dim(mod_all, me, axis=1, keepdims=False).reshape(6, D)
    late = [w_branch_attn[0].T.astype(BF16), w_branch_conv[0].astype(BF16), w_out[0].astype(BF16),
            w_mlp_in[0].T.astype(BF16), w_mlp_out[0].astype(BF16)]
    w_int, late = lax.optimization_barrier((w_int, late))
    zones = _own_rows_into_zones(late, me.reshape(1).astype(jnp.int32))
    ag_mix = _split_start("gather_mix_start", "gather", late[:3], zones[:3])
    ag_mlp = _split_start("gather_mlp_start", "gather", late[3:], zones[3:])
    mod = mod + ag_mix[4] + ag_mlp[4]
    h = _prenorm(x2, g_norm_mix, mod[1:2], mod[0:1])

    def mix_weights(o_attn):
        return _split_wait("gather_mix_wait", "gather", *ag_mix[:4], o_attn)[1]

    def mlp_weights(x1):
        return _split_wait("gather_mlp_wait", "gather", *ag_mlp[:4], x1)[1]

    rs = {}

    def mlp_grads_ready(*grads):
        lands = [lax.empty((N_PEER, t.shape[0] // N_DEV, t.shape[1]), BF16) for t in grads]
        rs["mlp"] = _split_start("scatter_mlp_start", "scatter", grads, lands)
        return rs["mlp"][4]

    def w_in_grad_ready(g_in):
        r = g_in.shape[0] // N_DEV
        rs["sib"] = _split_start("sibling_w_in_start", "sibling", [g_in], [lax.empty((4, r, g_in.shape[1]), BF16)])
        return rs["sib"][5]

    def other_grads_ready(*small):
        core = ci.reshape(1).astype(jnp.int32)
        (g_in,), (sib_in,) = _split_wait("sibling_w_in_wait", "sibling", *rs["sib"][:4], small[0])
        pair = _pair_sums([g_in, *small], [sib_in, *_sibling_exchange(small)], core)
        lands = [lax.empty((3,) + t.shape[1:], BF16) for t in pair]
        rs["rest"] = _split_start("scatter_rest_start", "chips", pair, lands)
        return rs["rest"][4]

    ba, bb = b_gate[:, :D], b_gate[:, D:]
    grad_x, vec = _local_step(
        x2, h, tgt, mod, g_norm_mix, g_norm_mlp, g_norm_final.reshape(1, D), ba, bb, cw8, w_int, mix_weights, mlp_weights,
        mlp_grads_ready, w_in_grad_ready, other_grads_ready)

    vec_all, vec_sum = _allgather_small(vec, "gather_vec")
    loss = vec_sum[14, 0]
    gm_all = vec_all[:, 0:6, :].reshape(N_DEV, 6 * D)
    gm_cols = lax.dynamic_slice(gm_all, (0, me * ncol), (N_DEV, ncol))
    conv_cols = lax.dynamic_slice(vec_sum[11:14], (0, me * HEAD), (3, HEAD))
    g_pack = jnp.concatenate([vec_sum[0:11], jnp.pad(conv_cols, ((0, 0), (0, D - HEAD))), jnp.zeros((2, D), F32)], axis=0)
    packs = [_pack_vectors(*t) for t in ((b_ada, g_norm_mix, g_norm_mlp, g_norm_final, b_gate, conv_w),
                                         (m_b_ada, m_g_norm_mix, m_g_norm_mlp, m_g_norm_final, m_b_gate, m_conv_w),
                                         (v_b_ada, v_g_norm_mix, v_g_norm_mlp, v_g_norm_final, v_b_gate, v_conv_w))]
    gv, dv, mv, vv = _adamw_vectors(packs[0], g_pack, packs[1], packs[2])
    g_w_ada, d_ada, nm_ada, nv_ada = _adamw_w_ada(w_ada[0], act.T, gm_cols, m_w_ada[0], v_w_ada[0])

    big = {}
    srcs, lands = _split_wait("scatter_mlp_wait", "scatter", *rs["mlp"][:4], d_ada)
    own = [g.reshape((N_DEV,) + land.shape[1:]) for g, land in zip(srcs, lands)]
    slot = me.reshape(1).astype(jnp.int32)
    big["w_mi"] = tuple(t[None] for t in _sum_adamw(lands[0], own[0], slot, w_mlp_in[0], m_w_mlp_in[0], v_w_mlp_in[0], "adamw_w_mi",
                                                    transposed=True))
    big["w_mo"] = tuple(t[None] for t in _sum_adamw(lands[1], own[1], slot, w_mlp_out[0], m_w_mlp_out[0], v_w_mlp_out[0], "adamw_w_mo"))
    own, lands = _split_wait("scatter_rest_wait", "chips", *rs["rest"][:4], big["w_mo"][1])
    slot = (2 * xi + yi).reshape(1).astype(jnp.int32)
    big["w_in"] = tuple(t.T[None] for t in _sum_adamw(lands[0], own[0], slot, w_in[0].T, m_w_in[0].T, v_w_in[0].T, "adamw_w_in"))
    big["w_ba"] = tuple(t[None] for t in _sum_adamw(lands[1], own[1], slot, w_branch_attn[0], m_w_branch_attn[0], v_w_branch_attn[0],
                                                    "adamw_w_ba", transposed=True))
    big["w_bc"] = tuple(t[None] for t in _sum_adamw(lands[2], own[2], slot, w_branch_conv[0], m_w_branch_conv[0], v_w_branch_conv[0], "adamw_w_bc"))
    big["w_out"] = tuple(t[None] for t in _sum_adamw(lands[3], own[3], slot, w_out[0], m_w_out[0], v_w_out[0], "adamw_w_out"))

    def ordered(k, ada, vecs):
        return (ada[None], vecs[0], vecs[1], big["w_in"][k], vecs[2], vecs[3], big["w_ba"][k], big["w_bc"][k],
                big["w_out"][k], vecs[4], big["w_mi"][k], big["w_mo"][k], vecs[5])

    return (loss, grad_x.reshape(1, S, D), *ordered(0, g_w_ada, gv), *ordered(1, d_ada, dv),
            *ordered(2, nm_ada, mv), *ordered(3, nv_ada, vv))
```

```python
import numpy as np
import jax
import jax.numpy as jnp
from jax import lax
from jax.experimental import pallas as pl
from jax.experimental.pallas import tpu as pltpu

F32, BF16 = jnp.float32, jnp.bfloat16
D = 1024
HEAD = 128
DILATIONS = (1, 4, 16)
N_SLOT = 4
AOW = N_SLOT * HEAD
DFF = 4 * D
N_DEV = 8
UNROLL = 16
UNROLL_FWD = 32
EPS = 1e-6
NEG = -1e30
SCALE = HEAD ** -0.5
LR, B1, B2, ADAM_EPS, WD, STEP = 0.001, 0.9, 0.999, 1e-08, 0.01, 10
V7X_VMEM_LIMIT = 56 * 1024 * 1024
TM = 1024
MESH = pl.DeviceIdType.MESH


def _cparams(*sem):
    if sem:
        return pltpu.CompilerParams(dimension_semantics=sem, vmem_limit_bytes=V7X_VMEM_LIMIT)
    return pltpu.CompilerParams(vmem_limit_bytes=V7X_VMEM_LIMIT)


def _nn(a, b):
    return jnp.dot(a, b, preferred_element_type=F32)


def _nt(a, b):
    return lax.dot_general(a, b, (((1,), (1,)), ((), ())), preferred_element_type=F32)


def _tn(a, b):
    return lax.dot_general(a, b, (((0,), (0,)), ((), ())), preferred_element_type=F32)


def _rms_r(x):
    return lax.rsqrt(jnp.mean(x * x, axis=-1, keepdims=True) + EPS)


def _rms_bwd(x, r, g, dn):
    gy = dn * g
    dx = r * gy - x * (r * r * r) * jnp.mean(x * gy, axis=-1, keepdims=True)
    return dx, dn * (x * r)


def _sigmoid(t):
    return 1.0 / (1.0 + jnp.exp(-t))


def _rowsum(v):
    return jnp.sum(v, axis=0, keepdims=True)


def _vec_spec(n=D):
    return pl.BlockSpec((1, n), lambda *_: (0, 0))


def _const_spec(shape):
    nd = len(shape)
    return pl.BlockSpec(shape, lambda *_: (0,) * nd)


def _win_rowblock(j):
    return jnp.where(j < 9, (j % 3) * 3 + j // 3, j)


def _prenorm(x, g, sc, sh):
    S = x.shape[0]
    tm = TM

    def body(x_ref, g_ref, sc_ref, sh_ref, h_ref):
        xv = x_ref[...]
        h_ref[...] = (xv * _rms_r(xv) * g_ref[...] * (1.0 + sc_ref[...]) + sh_ref[...]).astype(BF16)

    row = pl.BlockSpec((tm, D), lambda i: (i, 0))
    return pl.pallas_call(
        body, name="prenorm", grid=(S // tm,), in_specs=[row, _vec_spec(), _vec_spec(), _vec_spec()], out_specs=row,
        out_shape=jax.ShapeDtypeStruct((S, D), BF16), compiler_params=_cparams("parallel"),
    )(x, g, sc, sh)


def _proj(h, w_int):
    S = h.shape[0]

    def body(h_ref, w_ref, q_ref, e_ref):
        j = pl.program_id(0)
        acc = _nt(h_ref[...], w_ref[...])

        @pl.when(j < 9)
        def _():
            q_ref[0] = acc

        @pl.when(j >= 9)
        def _():
            e_ref[0] = acc.astype(BF16)

    def e_idx(j):
        k = jnp.maximum(j - 9, 0)
        return (k // 2, 0, k % 2)

    return pl.pallas_call(
        body, name="proj", grid=(19,),
        in_specs=[pl.BlockSpec((S, D), lambda j: (0, 0), pipeline_mode=pl.Buffered(1)),
                  pl.BlockSpec((512, D), lambda j: (_win_rowblock(j), 0))],
        out_specs=[pl.BlockSpec((1, S, 512), lambda j: (jnp.minimum(j, 8), 0, 0)), pl.BlockSpec((1, S, 512), e_idx)],
        out_shape=[jax.ShapeDtypeStruct((9, S, 512), F32), jax.ShapeDtypeStruct((5, S, D), BF16)],
        compiler_params=_cparams("arbitrary"),
    )(h, w_int)


def _bias_table():
    slopes = (2.0 ** (-8.0 * np.arange(1, 13, dtype=np.float32) / 12.0)).astype(np.float32)
    qi = np.arange(HEAD)[:, None]
    kj = np.arange(2 * HEAD)[None, :]
    delta = HEAD + qi - kj
    mask = (delta >= 0) & (delta <= HEAD)
    out = np.zeros((3, N_SLOT, HEAD, 2 * HEAD), np.float32)
    for gi, d in enumerate(DILATIONS):
        for j in range(N_SLOT):
            bias = -slopes[gi * N_SLOT + j] * (delta * d).astype(np.float32)
            out[gi, j] = np.where(mask, bias, NEG)
    out_t = np.concatenate([out[..., HEAD:].swapaxes(-1, -2), out[..., :HEAD].swapaxes(-1, -2)], axis=-1)
    return jnp.asarray(out), jnp.asarray(out_t)


def _attn_fwd(qkv, bias):
    S = qkv.shape[2]
    nblk = S // HEAD
    rows = 256

    def body(qkv_ref, b_ref, o_ref, lse_ref, o_s, lse_s):
        g = pl.program_id(1)
        bias = b_ref[0, 0]
        col = lax.broadcasted_iota(jnp.int32, bias.shape, 1)
        bias_first = jnp.where(col < HEAD, NEG, bias)

        for gi, d in enumerate(DILATIONS):
            @pl.when(g == gi)
            def _(gi=gi, d=d):
                nb = nblk // d

                def keys(start):
                    sl = pl.ds(start, HEAD, stride=d)
                    return qkv_ref.at[0, 1][sl, :].astype(BF16), qkv_ref.at[0, 2][sl, :].astype(BF16)

                def step(b, first_of_residue, before):
                    r, n = b // nb, b % nb
                    cur = pl.ds(n * (HEAD * d) + r, HEAD, stride=d)
                    own = keys(n * (HEAD * d) + r)
                    if first_of_residue:
                        before = own
                    q = qkv_ref.at[0, 0][cur, :].astype(BF16)
                    kw = jnp.concatenate([before[0], own[0]], axis=0)
                    vw = jnp.concatenate([before[1], own[1]], axis=0)
                    s = _nt(q, kw) * SCALE + jnp.where(n > 0, bias, bias_first)
                    m = jnp.max(s, axis=-1, keepdims=True)
                    p = jnp.exp(s - m)
                    l = jnp.sum(p, axis=-1, keepdims=True)
                    o_s.at[gi][cur, :] = _nn(p.astype(BF16), vw) / l
                    lse_s.at[gi][cur, :] = jnp.broadcast_to(m + jnp.log(l), (HEAD, HEAD))
                    return own

                def steps(i, before):
                    for u in range(UNROLL_FWD):
                        before = step(i * UNROLL_FWD + u, nb <= UNROLL_FWD and u % nb == 0, before)
                    return before

                lax.fori_loop(0, nblk // UNROLL_FWD, steps, keys(0))

        @pl.when(g == len(DILATIONS) - 1)
        def _():
            def merge(i, carry):
                r = pl.ds(pl.multiple_of(i * rows, rows), rows)
                ls = [lse_s[k, r, :] for k in range(3)]
                top = jnp.maximum(jnp.maximum(ls[0], ls[1]), ls[2])
                ws = [jnp.exp(t - top) for t in ls]
                den = ws[0] + ws[1] + ws[2]
                o_ref[r, :] = (ws[0] * o_s[0, r, :] + ws[1] * o_s[1, r, :] + ws[2] * o_s[2, r, :]) / den
                lse_ref[r, :] = top + jnp.log(den)
                return carry

            lax.fori_loop(0, S // rows, merge, 0)

    return pl.pallas_call(
        body, name="attn_fwd", grid=(N_SLOT, 3),
        in_specs=[pl.BlockSpec((1, 3, S, HEAD), lambda j, g: (g, 0, 0, j)),
                  pl.BlockSpec((1, 1, HEAD, 2 * HEAD), lambda j, g: (g, j, 0, 0))],
        out_specs=[pl.BlockSpec((S, HEAD), lambda j, g: (0, j)), pl.BlockSpec((S, HEAD), lambda j, g: (0, j))],
        out_shape=[jax.ShapeDtypeStruct((S, AOW), F32), jax.ShapeDtypeStruct((S, AOW), F32)],
        scratch_shapes=[pltpu.VMEM((3, S, HEAD), F32)] * 2,
        compiler_params=_cparams("parallel", "arbitrary"),
    )(qkv, bias)


def _shift_down(z, k, halo_rows):
    out = pltpu.roll(z, k, axis=0)
    top = out[:8]
    rid = lax.broadcasted_iota(jnp.int32, top.shape, 0)
    for t in range(k):
        top = jnp.where(rid == t, halo_rows[t], top)
    return jnp.concatenate([top, out[8:]], axis=0)


def _shift_up(z, k, halo_rows):
    n = z.shape[0]
    out = pltpu.roll(z, n - k, axis=0)
    bottom = out[n - 8:]
    rid = lax.broadcasted_iota(jnp.int32, bottom.shape, 0)
    for t in range(k):
        bottom = jnp.where(rid == 8 - k + t, halo_rows[t], bottom)
    return jnp.concatenate([out[:n - 8], bottom], axis=0)


def _e_spec(chunk, tm):
    return pl.BlockSpec((1, tm, D), lambda i, c=chunk: (c, i, 0))


def _e_prev_spec(chunk, tm):
    return pl.BlockSpec((1, 16, D), lambda i, c=chunk: (c, jnp.maximum(i * (tm // 16) - 1, 0), 0))


def _e_next_spec(chunk, tm, S):
    return pl.BlockSpec((1, 16, D), lambda i, c=chunk: (c, jnp.minimum((i + 1) * (tm // 16), S // 16 - 1), 0))


def _mix(o_attn, e, cw8, ba, bb, w_bat, w_bc):
    S = o_attn.shape[0]
    tm = 512

    def body(o_ref, cb_ref, cc_ref, cx_ref, ga_ref, gb_ref, ccp_ref, cxp_ref, cw_ref, ba_ref, bb_ref, wba_ref, wbc_ref,
             obf_ref, cbu_ref, ya_ref, yc_ref, mg_ref):
        i = pl.program_id(0)
        o = o_ref[...].astype(BF16)
        obf_ref[...] = o
        ya = _nt(o, wba_ref[...])
        z = cc_ref[0].astype(F32) * cx_ref[0].astype(F32)
        zp = ccp_ref[0].astype(F32) * cxp_ref[0].astype(F32) * (i > 0).astype(F32)
        z1 = _shift_down(z, 1, [zp[15:16]])
        z2 = _shift_down(z, 2, [zp[14:15], zp[15:16]])
        cw = cw_ref[...]
        u = cw[0:1] * z2 + cw[1:2] * z1 + cw[2:3] * z
        cbu = (cb_ref[0].astype(F32) * u).astype(BF16)
        cbu_ref[...] = cbu
        yc = _nn(cbu, wbc_ref[...])
        sa = _sigmoid(ga_ref[0].astype(F32) + ba_ref[...])
        sb = _sigmoid(gb_ref[0].astype(F32) + bb_ref[...])
        ya_ref[...] = ya.astype(BF16)
        yc_ref[...] = yc.astype(BF16)
        mg_ref[...] = (sa * ya + sb * yc).astype(BF16)

    row = lambda w: pl.BlockSpec((tm, w), lambda i: (i, 0))
    return pl.pallas_call(
        body, name="mix", grid=(S // tm,),
        in_specs=[row(AOW)] + [_e_spec(c, tm) for c in range(5)] + [_e_prev_spec(1, tm), _e_prev_spec(2, tm),
                  _const_spec((8, D)), _vec_spec(), _vec_spec(), _const_spec((D, AOW)), _const_spec((D, D))],
        out_specs=[row(AOW), row(D), row(D), row(D), row(D)],
        out_shape=[jax.ShapeDtypeStruct((S, AOW), BF16)] + [jax.ShapeDtypeStruct((S, D), BF16)] * 4,
        compiler_params=_cparams("parallel"),
    )(o_attn, e, e, e, e, e, e, e, cw8, ba, bb, w_bat, w_bc)


def _out_proj(merged, w_out, x, gate1, g_mlp, sc2, sh2):
    S = x.shape[0]
    tm = TM

    def body(mg_ref, w_ref, x_ref, gt_ref, g_ref, sc_ref, sh_ref, x1_ref, mo_ref, h2_ref):
        mo = _nn(mg_ref[...], w_ref[...])
        mo_ref[...] = mo.astype(BF16)
        x1 = x_ref[...] + gt_ref[...] * mo
        x1_ref[...] = x1
        h2 = x1 * _rms_r(x1) * g_ref[...] * (1.0 + sc_ref[...]) + sh_ref[...]
        h2_ref[...] = h2.astype(BF16)

    row = pl.BlockSpec((tm, D), lambda i: (i, 0))
    return pl.pallas_call(
        body, name="out_proj", grid=(S // tm,),
        in_specs=[row, _const_spec((D, D)), row, _vec_spec(), _vec_spec(), _vec_spec(), _vec_spec()],
        out_specs=[row, row, row],
        out_shape=[jax.ShapeDtypeStruct((S, D), F32), jax.ShapeDtypeStruct((S, D), BF16), jax.ShapeDtypeStruct((S, D), BF16)],
        compiler_params=_cparams("parallel"),
    )(merged, w_out, x, gate1, g_mlp, sc2, sh2)


def _mlp_in(h2, w_mit):
    S = h2.shape[0]
    tm, tn = TM, 2048

    def body(h_ref, w_ref, a_ref, f_ref):
        a = _nt(h_ref[...], w_ref[...])
        a_ref[...] = a.astype(BF16)
        f_ref[...] = jnp.square(jnp.maximum(a, 0.0)).astype(BF16)

    blk = pl.BlockSpec((tm, tn), lambda i, j: (i, j))
    return pl.pallas_call(
        body, name="mlp_in", grid=(S // tm, DFF // tn),
        in_specs=[pl.BlockSpec((tm, D), lambda i, j: (i, 0)), pl.BlockSpec((tn, D), lambda i, j: (j, 0))],
        out_specs=[blk, blk],
        out_shape=[jax.ShapeDtypeStruct((S, DFF), BF16)] * 2,
        compiler_params=_cparams("parallel", "parallel"),
    )(h2, w_mit)


def _mlp_out(f, w_mo, x1, gate2, g_fin, tgt):
    S = x1.shape[0]
    tm = 512
    half = tm // 2

    def body(f_ref, w_ref, x1_ref, gt_ref, g_ref, t_ref, mlp_ref, dx2_ref, pv_ref):
        @pl.when(pl.program_id(0) == 0)
        def _():
            pv_ref[...] = jnp.zeros_like(pv_ref)

        g = g_ref[...]
        for hs in (pl.ds(0, half), pl.ds(half, half)):
            mlp = _nn(f_ref[hs, :], w_ref[...])
            mlp_ref[hs, :] = mlp.astype(BF16)
            x2 = x1_ref[hs, :] + gt_ref[...] * mlp
            r = _rms_r(x2)
            err = x2 * r * g - t_ref[hs, :]
            dx2, pg = _rms_bwd(x2, r, g, err * (1.0 / D))
            dx2_ref[hs, :] = dx2
            pv_ref[0:1, :] += _rowsum(pg)
            pv_ref[1:2, :] += 0.5 * _rowsum(jnp.mean(err * err, axis=-1, keepdims=True))

    row = pl.BlockSpec((tm, D), lambda i: (i, 0))
    return pl.pallas_call(
        body, name="mlp_out", grid=(S // tm,),
        in_specs=[pl.BlockSpec((tm, DFF), lambda i: (i, 0)), _const_spec((DFF, D)), row, _vec_spec(), _vec_spec(), row],
        out_specs=[row, row, _const_spec((8, D))],
        out_shape=[jax.ShapeDtypeStruct((S, D), BF16), jax.ShapeDtypeStruct((S, D), F32), jax.ShapeDtypeStruct((8, D), F32)],
        compiler_params=_cparams("arbitrary"),
    )(f, w_mo, x1, gate2, g_fin, tgt)


def _bwd_mlp_a(dx2, gate2, mlp, w_mo, a):
    S = dx2.shape[0]
    tm = 512
    half = tm // 2

    def body(dx_ref, gt_ref, mlp_ref, w_ref, a_ref, da_ref, dmo_ref, pv_ref):
        @pl.when(pl.program_id(0) == 0)
        def _():
            pv_ref[...] = jnp.zeros_like(pv_ref)

        for hs in (pl.ds(0, half), pl.ds(half, half)):
            dx = dx_ref[hs, :]
            dmo = (dx * gt_ref[...]).astype(BF16)
            dmo_ref[hs, :] = dmo
            pv_ref[0:1, :] += _rowsum(dx * mlp_ref[hs, :].astype(F32))
            df = _nt(dmo, w_ref[...])
            da_ref[hs, :] = (df * (2.0 * jnp.maximum(a_ref[hs, :].astype(F32), 0.0))).astype(BF16)

    row = pl.BlockSpec((tm, D), lambda i: (i, 0))
    wide = pl.BlockSpec((tm, DFF), lambda i: (i, 0))
    return pl.pallas_call(
        body, name="bwd_mlp_a", grid=(S // tm,),
        in_specs=[row, _vec_spec(), row, _const_spec((DFF, D)), wide],
        out_specs=[wide, row, _const_spec((8, D))],
        out_shape=[jax.ShapeDtypeStruct((S, DFF), BF16), jax.ShapeDtypeStruct((S, D), BF16), jax.ShapeDtypeStruct((8, D), F32)],
        compiler_params=_cparams("arbitrary"),
    )(dx2, gate2, mlp, w_mo, a)


def _bwd_mlp_b(da, w_mit, x1, dx2, g_mlp, sc2):
    S = x1.shape[0]
    tm = 512
    half = tm // 2

    def body(da_ref, w_ref, x1_ref, dx2_ref, g_ref, sc_ref, dx1_ref, pv_ref):
        @pl.when(pl.program_id(0) == 0)
        def _():
            pv_ref[...] = jnp.zeros_like(pv_ref)

        g = g_ref[...]
        for hs in (pl.ds(0, half), pl.ds(half, half)):
            dh = _nn(da_ref[hs, :], w_ref[...])
            x1 = x1_ref[hs, :]
            r = _rms_r(x1)
            dxn, pg = _rms_bwd(x1, r, g, dh * (1.0 + sc_ref[...]))
            dx1_ref[hs, :] = dx2_ref[hs, :] + dxn
            pv_ref[0:1, :] += _rowsum(dh)
            pv_ref[1:2, :] += _rowsum(dh * (x1 * r * g))
            pv_ref[2:3, :] += _rowsum(pg)

    row = pl.BlockSpec((tm, D), lambda i: (i, 0))
    return pl.pallas_call(
        body, name="bwd_mlp_b", grid=(S // tm,),
        in_specs=[pl.BlockSpec((tm, DFF), lambda i: (i, 0)), _const_spec((DFF, D)), row, row, _vec_spec(), _vec_spec()],
        out_specs=[row, _const_spec((8, D))],
        out_shape=[jax.ShapeDtypeStruct((S, D), F32), jax.ShapeDtypeStruct((8, D), F32)],
        compiler_params=_cparams("arbitrary"),
    )(da, w_mit, x1, dx2, g_mlp, sc2)


def _bwd_mix(dx1, gate1, mo, e, cw8, ba, bb, ya, yc, o_attn, w_out, w_bc, w_bat):
    S = dx1.shape[0]
    tm = 256
    n_tiles = S // tm

    def body(dx_ref, dxn_ref, gt_ref, mo_ref, cb_ref, cc_ref, cx_ref, ga_ref, gb_ref, cbn_ref, gbn_ref, ccp_ref, cxp_ref,
             cw_ref, ba_ref, bb_ref, ya_ref, yc_ref, o_ref, wout_ref, wbc_ref, wba_ref,
             dmo_ref, dya_ref, dyc_ref, do_ref, dl_ref, de_ref, pv_ref):
        i = pl.program_id(0)

        @pl.when(i == 0)
        def _():
            pv_ref[...] = jnp.zeros_like(pv_ref)

        dx = dx_ref[...]
        cb = cb_ref[0].astype(F32)
        cc = cc_ref[0].astype(F32)
        cx = cx_ref[0].astype(F32)
        dmo_all = (jnp.concatenate([dx, dxn_ref[...]], axis=0) * gt_ref[...]).astype(BF16)
        dmg_all = _nt(dmo_all, wout_ref[...])
        sb_all = _sigmoid(jnp.concatenate([gb_ref[0], gbn_ref[0]], axis=0).astype(F32) + bb_ref[...])
        dyc_all = dmg_all * sb_all
        dcbu_all = _nt(dyc_all.astype(BF16), wbc_ref[...])
        dmo, dmg, sb, dyc, dcbu = dmo_all[:tm], dmg_all[:tm], sb_all[:tm], dyc_all[:tm], dcbu_all[:tm]
        dmo_ref[...] = dmo
        pv_ref[0:1, :] += _rowsum(dx * mo_ref[...].astype(F32))
        sa = _sigmoid(ga_ref[0].astype(F32) + ba_ref[...])
        dya = (dmg * sa).astype(BF16)
        dya_ref[...] = dya
        dyc_ref[...] = dyc.astype(BF16)
        dga = dmg * ya_ref[...].astype(F32) * sa * (1.0 - sa)
        dgb = dmg * yc_ref[...].astype(F32) * sb * (1.0 - sb)
        pv_ref[1:2, :] += _rowsum(dga)
        pv_ref[2:3, :] += _rowsum(dgb)

        do = _nn(dya, wba_ref[...])
        do_ref[...] = do
        prod = do * o_ref[...]
        dl_ref[...] = jnp.concatenate(
            [jnp.broadcast_to(jnp.sum(prod[:, s * HEAD:(s + 1) * HEAD], axis=-1, keepdims=True), (tm, HEAD))
             for s in range(N_SLOT)], axis=1)

        z = cc * cx
        zp = ccp_ref[0].astype(F32) * cxp_ref[0].astype(F32) * (i > 0).astype(F32)
        z1 = _shift_down(z, 1, [zp[15:16]])
        z2 = _shift_down(z, 2, [zp[14:15], zp[15:16]])
        cw = cw_ref[...]
        u = cw[0:1] * z2 + cw[1:2] * z1 + cw[2:3] * z
        du = dcbu * cb
        du_n = dcbu_all[tm:] * cbn_ref[0].astype(F32) * (i < n_tiles - 1).astype(F32)
        du1 = _shift_up(du, 1, [du_n[0:1]])
        du2 = _shift_up(du, 2, [du_n[0:1], du_n[1:2]])
        dz = cw[2:3] * du + cw[1:2] * du1 + cw[0:1] * du2
        pv_ref[3:4, :] += _rowsum(du * z2)
        pv_ref[4:5, :] += _rowsum(du * z1)
        pv_ref[5:6, :] += _rowsum(du * z)

        de_ref[0] = (dcbu * u).astype(BF16)
        de_ref[1] = (dz * cx).astype(BF16)
        de_ref[2] = (dz * cc).astype(BF16)
        de_ref[3] = dga.astype(BF16)
        de_ref[4] = dgb.astype(BF16)

    row = lambda w: pl.BlockSpec((tm, w), lambda i: (i, 0))
    nxt = pl.BlockSpec((16, D), lambda i: (jnp.minimum((i + 1) * (tm // 16), S // 16 - 1), 0))
    return pl.pallas_call(
        body, name="bwd_mix", grid=(n_tiles,),
        in_specs=[row(D), nxt, _vec_spec(), row(D)] + [_e_spec(c, tm) for c in range(5)]
                 + [_e_next_spec(0, tm, S), _e_next_spec(4, tm, S), _e_prev_spec(1, tm), _e_prev_spec(2, tm),
                    _const_spec((8, D)), _vec_spec(), _vec_spec(), row(D), row(D), row(AOW),
                    _const_spec((D, D)), _const_spec((D, D)), _const_spec((D, AOW))],
        out_specs=[row(D), row(D), row(D), row(AOW), row(AOW), pl.BlockSpec((5, tm, D), lambda i: (0, i, 0)),
                   _const_spec((8, D))],
        out_shape=[jax.ShapeDtypeStruct((S, D), BF16)] * 3 + [jax.ShapeDtypeStruct((S, AOW), F32)] * 2
                  + [jax.ShapeDtypeStruct((5, S, D), BF16), jax.ShapeDtypeStruct((8, D), F32)],
        compiler_params=_cparams("arbitrary"),
    )(dx1, dx1, gate1, mo, e, e, e, e, e, e, e, e, e, cw8, ba, bb, ya, yc, o_attn, w_out, w_bc, w_bat)


def _attn_bwd(qkv, do, lse, dl, bias_t):
    S = qkv.shape[2]
    nblk = S // HEAD

    def body(qkv_ref, do_ref, lse_ref, dl_ref, b_ref, d_ref):
        g = pl.program_id(1)
        bias = b_ref[0, 0]
        col = lax.broadcasted_iota(jnp.int32, bias.shape, 1)
        bias_last = jnp.where(col >= HEAD, NEG, bias)
        eye = (lax.broadcasted_iota(jnp.int32, (HEAD, HEAD), 0) == lax.broadcasted_iota(jnp.int32, (HEAD, HEAD), 1)).astype(F32)

        def as_row(t):
            return jnp.sum(t * eye, axis=0, keepdims=True)

        for gi, d in enumerate(DILATIONS):
            @pl.when(g == gi)
            def _(d=d):
                nb = nblk // d

                def query_side(start):
                    sl = pl.ds(start, HEAD, stride=d)
                    return (qkv_ref.at[0, 0][sl, :].astype(BF16), do_ref[sl, :].astype(BF16),
                            as_row(lse_ref[sl, :]), as_row(dl_ref[sl, :]))

                def step(b, first_of_residue, carry):
                    dq_part, own = carry
                    r, n = b // nb, b % nb
                    cur = pl.ds(n * (HEAD * d) + r, HEAD, stride=d)
                    if first_of_residue:
                        own = query_side(r)
                    nxt = query_side(jnp.minimum(n + 1, nb - 1) * (HEAD * d) + r)
                    q2 = jnp.concatenate([own[0], nxt[0]], axis=0)
                    do2 = jnp.concatenate([own[1], nxt[1]], axis=0)
                    k = qkv_ref.at[0, 1][cur, :].astype(BF16)
                    v = qkv_ref.at[0, 2][cur, :].astype(BF16)
                    s = _nt(k, q2) * SCALE + jnp.where(n < nb - 1, bias, bias_last)
                    p = jnp.exp(s - jnp.concatenate([own[2], nxt[2]], axis=1))
                    d_ref.at[0, 2][cur, :] = _nn(p.astype(BF16), do2)
                    dp = _nt(v, do2)
                    ds = (p * (dp - jnp.concatenate([own[3], nxt[3]], axis=1)) * SCALE).astype(BF16)
                    d_ref.at[0, 1][cur, :] = _nn(ds, q2)
                    dq2 = _tn(ds, k)
                    d_ref.at[0, 0][cur, :] = dq2[:HEAD] + jnp.where(n > 0, dq_part, 0.0)
                    return dq2[HEAD:], nxt

                def steps(i, carry):
                    for u in range(UNROLL):
                        carry = step(i * UNROLL + u, nb <= UNROLL and u % nb == 0, carry)
                    return carry

                lax.fori_loop(0, nblk // UNROLL, steps, (jnp.zeros((HEAD, HEAD), F32), query_side(0)))

    col_blk = pl.BlockSpec((S, HEAD), lambda j, g: (0, j))
    qkv_blk = pl.BlockSpec((1, 3, S, HEAD), lambda j, g: (g, 0, 0, j))
    return pl.pallas_call(
        body, name="attn_bwd", grid=(N_SLOT, 3),
        in_specs=[qkv_blk, col_blk, col_blk, col_blk, pl.BlockSpec((1, 1, HEAD, 2 * HEAD), lambda j, g: (g, j, 0, 0))],
        out_specs=qkv_blk,
        out_shape=jax.ShapeDtypeStruct((3, 3, S, AOW), F32),
        compiler_params=_cparams("parallel", "arbitrary"),
    )(qkv, do, lse, dl, bias_t)


def _bwd_in(dqkv, de, w_int, x, dx1, g_mix, sc1):
    S = x.shape[0]
    tm = TM
    dqkv = dqkv.reshape(3, 3, S, AOW)

    def body(dq_ref, de_ref, wq_ref, wk_ref, wv_ref, wa_ref, wb_ref, x_hbm, dx1_hbm, g_ref, sc_ref, gx_ref, pv_ref,
             x_ref, dx1_ref, sems):
        acc = gx_ref
        i, k = pl.program_id(0), pl.program_id(1)
        tile = pl.ds(pl.multiple_of(i * tm, tm), tm)
        late = [pltpu.make_async_copy(x_hbm.at[tile, :], x_ref, sems.at[0]),
                pltpu.make_async_copy(dx1_hbm.at[tile, :], dx1_ref, sems.at[1])]

        @pl.when((i == 0) & (k == 0))
        def _():
            pv_ref[...] = jnp.zeros_like(pv_ref)

        @pl.when(k == 0)
        def _():
            acc[...] = jnp.zeros_like(acc)
            for cp in late:
                cp.start()

        @pl.when(k < 3)
        def _():
            lhs = jnp.concatenate([dq_ref[0, t].astype(BF16) for t in range(3)], axis=1)
            acc[...] += _nn(lhs, jnp.concatenate([wq_ref[...], wk_ref[...], wv_ref[...]], axis=0))

        @pl.when(k >= 3)
        def _():
            acc[...] += _nn(de_ref[0], jnp.concatenate([wa_ref[...], wb_ref[...]], axis=0))

        @pl.when(k == 7)
        def _():
            for cp in late:
                cp.wait()
            dh = acc[...]
            xv = x_ref[...]
            r = _rms_r(xv)
            g = g_ref[...]
            dxn, pg = _rms_bwd(xv, r, g, dh * (1.0 + sc_ref[...]))
            gx_ref[...] = dx1_ref[...] + dxn
            pv_ref[0:1, :] += _rowsum(dh)
            pv_ref[1:2, :] += _rowsum(dh * (xv * r * g))
            pv_ref[2:3, :] += _rowsum(pg)

    grp = lambda k: jnp.minimum(k, 2)
    chunk = lambda k: jnp.maximum(k - 3, 0)
    wblk = lambda f: pl.BlockSpec((512, D), lambda i, k: (f(k), 0))
    row = pl.BlockSpec((tm, D), lambda i, k: (i, 0))
    anyspec = pl.BlockSpec(memory_space=pl.ANY)
    return pl.pallas_call(
        body, name="bwd_in", grid=(S // tm, 8),
        in_specs=[pl.BlockSpec((1, 3, tm, 512), lambda i, k: (grp(k), 0, i, 0)),
                  pl.BlockSpec((1, tm, D), lambda i, k: (chunk(k), i, 0)),
                  wblk(grp), wblk(lambda k: 3 + grp(k)), wblk(lambda k: 6 + grp(k)),
                  wblk(lambda k: 9 + 2 * chunk(k)), wblk(lambda k: 10 + 2 * chunk(k)),
                  anyspec, anyspec, _vec_spec(), _vec_spec()],
        out_specs=[row, _const_spec((8, D))],
        out_shape=[jax.ShapeDtypeStruct((S, D), F32), jax.ShapeDtypeStruct((8, D), F32)],
        scratch_shapes=[pltpu.VMEM((tm, D), F32), pltpu.VMEM((tm, D), F32), pltpu.SemaphoreType.DMA((2,))],
        compiler_params=_cparams("arbitrary", "arbitrary"),
    )(dqkv, de, w_int, w_int, w_int, w_int, w_int, x, dx1, g_mix, sc1)


def _grad_w(name, a, b):
    S, ka = a.shape
    nb = b.shape[1]

    def body(a_ref, b_ref, o_ref):
        o_ref[...] = _tn(a_ref[...], b_ref[...]).astype(BF16)

    return pl.pallas_call(
        body, name=name, grid=(ka // 512,),
        in_specs=[pl.BlockSpec((S, 512), lambda n: (0, n)), pl.BlockSpec((S, nb), lambda n: (0, 0))],
        out_specs=pl.BlockSpec((512, nb), lambda n: (n, 0)),
        out_shape=jax.ShapeDtypeStruct((ka, nb), BF16),
        compiler_params=_cparams("parallel"),
    )(a, b)


def _grad_w_small(dya, o_bf, cbu, dyc, merged, dmo, after):
    S = dya.shape[0]

    def body(dya_h, o_h, cbu_h, dyc_h, mg_h, dmo_h, after_ref, gba_ref, gbc_ref, gout_ref, a0, a1, b_small, b1, b2, sems):
        fetch = [pltpu.make_async_copy(src, buf, sems.at[k])
                 for k, (src, buf) in enumerate(((dya_h, a0), (o_h, b_small), (cbu_h, a1), (dyc_h, b1), (dmo_h, b2)))]
        for cp in fetch:
            cp.start()
        fetch[0].wait()
        fetch[1].wait()
        gba_ref[...] = _tn(a0[...], b_small[...]).astype(BF16)
        last = pltpu.make_async_copy(mg_h, a0, sems.at[5])
        last.start()
        fetch[2].wait()
        fetch[3].wait()
        gbc_ref[...] = _tn(a1[...], b1[...]).astype(BF16)
        fetch[4].wait()
        last.wait()
        gout_ref[...] = _tn(a0[...], b2[...]).astype(BF16)

    anyspec = pl.BlockSpec(memory_space=pl.ANY)
    vmem = pl.BlockSpec(memory_space=pltpu.VMEM)
    wide = pltpu.VMEM((S, D), BF16)
    return pl.pallas_call(
        body, name="grad_w_small",
        in_specs=[anyspec] * 7, out_specs=[vmem] * 3,
        out_shape=[jax.ShapeDtypeStruct((D, AOW), BF16), jax.ShapeDtypeStruct((D, D), BF16), jax.ShapeDtypeStruct((D, D), BF16)],
        scratch_shapes=[wide, wide, pltpu.VMEM((S, AOW), BF16), wide, wide, pltpu.SemaphoreType.DMA((6,))],
        compiler_params=_cparams(),
    )(dya, o_bf, cbu, dyc, merged, dmo, after)


def _grad_w_in(dqkv, de, h):
    S = h.shape[0]

    def body(dq_ref, de_ref, h_ref, o_ref):
        n = pl.program_id(0)

        @pl.when(n < 9)
        def _():
            o_ref[...] = _tn(dq_ref[0].astype(BF16), h_ref[...]).astype(BF16)

        @pl.when(n >= 9)
        def _():
            o_ref[...] = _tn(de_ref[0], h_ref[...]).astype(BF16)

    def e_idx(n):
        kk = jnp.maximum(n - 9, 0)
        return (kk // 2, 0, kk % 2)

    return pl.pallas_call(
        body, name="grad_w_in", grid=(19,),
        in_specs=[pl.BlockSpec((1, S, 512), lambda n: (jnp.minimum(n, 8), 0, 0)), pl.BlockSpec((1, S, 512), e_idx),
                  pl.BlockSpec((S, D), lambda n: (0, 0))],
        out_specs=pl.BlockSpec((512, D), lambda n: (_win_rowblock(n), 0)),
        out_shape=jax.ShapeDtypeStruct((19 * 512, D), BF16),
        compiler_params=_cparams("parallel"),
    )(dqkv, de, h)


def _local_step(x, h, tgt, mod, g_mix, g_mlp, g_fin, ba, bb, cw8, w_int, mix_weights, mlp_weights, mlp_grads_ready, w_in_grad_ready,
                other_grads_ready):
    S = x.shape[0]
    sh1, sc1, gt1, sh2, sc2, gt2 = [mod[k:k + 1] for k in range(6)]
    bias, bias_t = _bias_table()

    qkv, e = _proj(h, w_int)
    qkv = qkv.reshape(3, 3, S, AOW)
    o_attn, lse = _attn_fwd(qkv, bias)
    w_bat, w_bc, w_out = mix_weights(o_attn)
    o_bf, cbu, ya, yc, merged = _mix(o_attn, e, cw8, ba, bb, w_bat, w_bc)
    x1, mo, h2 = _out_proj(merged, w_out, x, gt1, g_mlp, sc2, sh2)
    w_mit, w_mo = mlp_weights(x1)
    a, f = _mlp_in(h2, w_mit)
    mlp, dx2, pv_f = _mlp_out(f, w_mo, x1, gt2, g_fin, tgt)

    da, dmo2, pv_a = _bwd_mlp_a(dx2, gt2, mlp, w_mo, a)
    dx1, pv_b = _bwd_mlp_b(da, w_mit, x1, dx2, g_mlp, sc2)
    zero = mlp_grads_ready(_grad_w("grad_w_mi", da, h2), _grad_w("grad_w_mo", f, dmo2))
    dmo, dya, dyc, do, dl, de, pv_m = _bwd_mix(dx1, gt1 + zero, mo, e, cw8, ba, bb, ya, yc, o_attn, w_out, w_bc, w_bat)
    dqkv = _attn_bwd(qkv, do, lse, dl, bias_t).reshape(9, S, AOW)
    after = w_in_grad_ready(_grad_w_in(dqkv, de, h))
    zero = other_grads_ready(*_grad_w_small(dya, o_bf, cbu, dyc, merged, dmo, after))
    grad_x, pv_i = _bwd_in(dqkv, de, w_int, x, dx1, g_mix, sc1 + zero)

    vec = jnp.concatenate([pv_i[0:2], pv_m[0:1], pv_b[0:2], pv_a[0:1], pv_i[2:3], pv_b[2:3], pv_f[0:1],
                           pv_m[1:3], pv_m[3:6], pv_f[1:2], jnp.zeros((1, D), F32)], axis=0)
    return grad_x, vec


def _my_place():
    return lax.axis_index("x"), lax.axis_index("y"), lax.axis_index("c")


def _dev_index(px, py, pc):
    return 4 * px + 2 * py + pc


def _peer(x, y, c, m):
    return (x ^ ((m >> 2) & 1), y ^ ((m >> 1) & 1), c ^ (m & 1))


HBM_SPEC = pl.BlockSpec(memory_space=pltpu.HBM)
SEM_SPEC = pl.BlockSpec(memory_space=pltpu.SEMAPHORE)
N_PEER = N_DEV - 1


SPLIT_MASKS = {"gather": tuple(range(1, N_DEV)), "scatter": tuple(range(1, N_DEV)), "chips": (2, 4, 6), "sibling": (1, 1, 1, 1)}


def _split_copy(mode, src_ref, land_ref, send_sems, recv_sems, w, j, place, arriving=False):
    x, y, c = place
    masks = SPLIT_MASKS[mode]
    peer = _peer(x, y, c, masks[j])
    k = w * len(masks) + j
    sender, receiver = ((peer, (x, y, c)) if arriving else ((x, y, c), peer))
    if mode == "gather":
        r = src_ref.shape[0]
        src, dst = src_ref, land_ref.at[pl.ds(pl.multiple_of(_dev_index(*sender) * r, 16), r), :]
    elif mode == "scatter":
        r = land_ref.shape[1]
        src, dst = src_ref.at[pl.ds(pl.multiple_of(_dev_index(*receiver) * r, 16), r), :], land_ref.at[j]
    elif mode == "chips":
        src, dst = src_ref.at[2 * receiver[0] + receiver[1]], land_ref.at[j]
    else:
        r = land_ref.shape[1]
        src, dst = src_ref.at[pl.ds(pl.multiple_of((2 * j + receiver[2]) * r, 16), r), :], land_ref.at[j]
    return pltpu.make_async_remote_copy(src_ref=src, dst_ref=dst, send_sem=send_sems.at[k], recv_sem=recv_sems.at[k],
                                        device_id=peer, device_id_type=MESH)


def _split_start(name, mode, srcs, lands):
    n = len(srcs)
    nm = len(SPLIT_MASKS[mode])

    def body(*refs):
        src, land = refs[:n], refs[n:2 * n]
        send_sems, recv_sems = refs[2 * n], refs[2 * n + 1]
        token = refs[-1]
        place = _my_place()
        for w in range(n):
            for j in range(nm):
                _split_copy(mode, src[w], land[w], send_sems, recv_sems, w, j, place).start()
        token[...] = jnp.zeros_like(token)

    hbm = lambda t: pltpu.HBM(t.shape, t.dtype)
    out = pl.pallas_call(
        body, name=name,
        out_shape=(pltpu.SemaphoreType.DMA((n * nm,)), pltpu.SemaphoreType.DMA((n * nm,)), *[hbm(t) for t in srcs],
                   *[hbm(t) for t in lands], jax.ShapeDtypeStruct((8, 128), F32)),
        in_specs=(HBM_SPEC,) * (2 * n),
        out_specs=(SEM_SPEC, SEM_SPEC) + (HBM_SPEC,) * (2 * n) + (pl.BlockSpec(memory_space=pltpu.VMEM),),
        input_output_aliases={i: 2 + i for i in range(2 * n)},
        compiler_params=pltpu.CompilerParams(has_side_effects=pltpu.SideEffectType.DATAFLOW_SIDE_EFFECTING),
    )(*[pltpu.with_memory_space_constraint(t, pltpu.HBM) for t in (*srcs, *lands)])
    return out[0], out[1], out[2:2 + n], out[2 + n:2 + 2 * n], out[-1][0:1, 0:1], out[-1]


def _split_wait(name, mode, send_sems, recv_sems, srcs, lands, after):
    n = len(srcs)

    def body(*refs):
        src, land = refs[:n], refs[n:2 * n]
        ssem, rsem = refs[2 * n], refs[2 * n + 1]
        place = _my_place()
        for w in range(n):
            for j in range(len(SPLIT_MASKS[mode])):
                _split_copy(mode, src[w], land[w], ssem, rsem, w, j, place).wait_send()
                _split_copy(mode, src[w], land[w], ssem, rsem, w, j, place, arriving=True).wait_recv()

    hbm = lambda t: pltpu.HBM(t.shape, t.dtype)
    out = pl.pallas_call(
        body, name=name,
        out_shape=tuple(hbm(t) for t in (*srcs, *lands)),
        in_specs=(HBM_SPEC,) * (2 * n) + (SEM_SPEC, SEM_SPEC, pl.BlockSpec(memory_space=pl.ANY)),
        out_specs=(HBM_SPEC,) * (2 * n),
        input_output_aliases={i: i for i in range(2 * n)},
        compiler_params=pltpu.CompilerParams(has_side_effects=pltpu.SideEffectType.DATAFLOW_SIDE_EFFECTING),
    )(*srcs, *lands, send_sems, recv_sems, after)
    return out[:n], out[n:]


def _sibling_exchange(grads):
    nw = len(grads)
    HBM = pl.BlockSpec(memory_space=pl.ANY)

    def body(*refs):
        g, land = refs[:nw], refs[nw:2 * nw]
        send_sems, recv_sems = refs[2 * nw:]
        x, y, c = _my_place()

        def copy(w, q, owner_core):
            r = land[w].shape[1]
            return pltpu.make_async_remote_copy(
                src_ref=g[w].at[pl.ds(pl.multiple_of((2 * q + owner_core) * r, 16), r), :], dst_ref=land[w].at[q],
                send_sem=send_sems.at[w, q], recv_sem=recv_sems.at[w, q], device_id=(x, y, 1 - c), device_id_type=MESH)

        sends = [copy(w, q, 1 - c) for w in range(nw) for q in range(4)]
        for cp in sends:
            cp.start()
        for w in range(nw):
            for q in range(4):
                copy(w, q, c).wait_recv()
        for cp in sends:
            cp.wait_send()

    return pl.pallas_call(
        body, name="sibling_exchange",
        out_shape=[jax.ShapeDtypeStruct((4, a.shape[0] // N_DEV, a.shape[1]), a.dtype) for a in grads],
        in_specs=[HBM] * nw, out_specs=[HBM] * nw,
        scratch_shapes=[pltpu.SemaphoreType.DMA((nw, 4)), pltpu.SemaphoreType.DMA((nw, 4))],
    )(*grads)


def _pair_sums(gs, sibs, core):
    n = len(gs)

    def body(core_ref, *refs):
        for w in range(n):
            refs[2 * n + w][0] = (refs[w][0, 0].astype(F32) + refs[n + w][0].astype(F32)).astype(BF16)

    in_specs = [pl.BlockSpec((1, 1) + t.shape[1:], lambda q, core_ref: (q, core_ref[0], 0, 0)) for t in sibs]
    in_specs += [pl.BlockSpec((1,) + t.shape[1:], lambda q, core_ref: (q, 0, 0)) for t in sibs]
    return pl.pallas_call(
        body, name="pair_sums",
        grid_spec=pltpu.PrefetchScalarGridSpec(
            num_scalar_prefetch=1, grid=(4,), in_specs=in_specs,
            out_specs=[pl.BlockSpec((1,) + t.shape[1:], lambda q, core_ref: (q, 0, 0)) for t in sibs]),
        out_shape=[jax.ShapeDtypeStruct(t.shape, BF16) for t in sibs],
        compiler_params=_cparams("parallel"),
    )(core, *[g.reshape(4, 2, t.shape[1], t.shape[2]) for g, t in zip(gs, sibs)], *sibs)


def _own_rows_into_zones(shards, me):
    n = len(shards)

    def body(me_ref, *refs):
        for w in range(n):
            refs[2 * n + w][...] = refs[w][...]

    zones = [lax.empty((N_DEV * t.shape[0], t.shape[1]), t.dtype) for t in shards]
    return pl.pallas_call(
        body, name="own_rows_into_zones",
        grid_spec=pltpu.PrefetchScalarGridSpec(
            num_scalar_prefetch=1, grid=(1,),
            in_specs=[pl.BlockSpec(t.shape, lambda i, me_ref: (0, 0)) for t in shards] + [pl.BlockSpec(memory_space=pl.ANY)] * n,
            out_specs=[pl.BlockSpec(t.shape, lambda i, me_ref: (me_ref[0], 0)) for t in shards]),
        out_shape=[jax.ShapeDtypeStruct(z.shape, z.dtype) for z in zones],
        input_output_aliases={1 + n + w: w for w in range(n)},
        compiler_params=_cparams("arbitrary"),
    )(me, *shards, *zones)


def _allgather_small(v, name):
    r, ccols = v.shape

    def body(v_ref, out_ref, sum_ref, send_sems, recv_sems):
        x, y, c = _my_place()
        my_idx = _dev_index(x, y, c)
        out_ref[my_idx] = v_ref[...]

        def copy(m):
            peer = _peer(x, y, c, m)
            return pltpu.make_async_remote_copy(
                src_ref=v_ref, dst_ref=out_ref.at[my_idx],
                send_sem=send_sems.at[m - 1], recv_sem=recv_sems.at[m - 1], device_id=peer, device_id_type=MESH)

        def arrival(m):
            peer = _peer(x, y, c, m)
            return pltpu.make_async_remote_copy(
                src_ref=v_ref, dst_ref=out_ref.at[_dev_index(*peer)],
                send_sem=send_sems.at[m - 1], recv_sem=recv_sems.at[m - 1], device_id=peer, device_id_type=MESH)

        sends = [copy(m) for m in range(1, N_DEV)]
        for cp in sends:
            cp.start()
        for m in range(1, N_DEV):
            arrival(m).wait_recv()
        acc = out_ref[0]
        for s in range(1, N_DEV):
            acc = acc + out_ref[s]
        sum_ref[...] = acc
        for cp in sends:
            cp.wait_send()

    vmem = pl.BlockSpec(memory_space=pltpu.VMEM)
    return pl.pallas_call(
        body, name=name,
        out_shape=[jax.ShapeDtypeStruct((N_DEV, r, ccols), v.dtype), jax.ShapeDtypeStruct((r, ccols), v.dtype)],
        in_specs=[vmem], out_specs=[vmem, vmem],
        scratch_shapes=[pltpu.SemaphoreType.DMA((7,)), pltpu.SemaphoreType.DMA((7,))],
    )(v)


def _gather_w_in_and_condition(shard, pay, w_ada, b_cols):
    r, ccols = shard.shape
    ncol = w_ada.shape[1]

    def body(sh_ref, pay_ref, w_ref, b_ref, full_ref, got_ref, act_ref, mod_ref, send_sems, recv_sems, small_send, small_recv, local_sem):
        x, y, c = _my_place()
        me, sibling = (x, y, c), (x, y, 1 - c)
        my_idx = _dev_index(x, y, c)
        chips = [(1 - x, y), (x, 1 - y), (1 - x, 1 - y)]

        def small(rnd, buf, m, arriving=False):
            peer = _peer(x, y, c, m)
            slot = _dev_index(*peer) if arriving else my_idx
            return pltpu.make_async_remote_copy(
                src_ref=buf.at[my_idx], dst_ref=buf.at[slot], send_sem=small_send.at[rnd, m - 1],
                recv_sem=small_recv.at[rnd, m - 1], device_id=peer, device_id_type=MESH)

        def rows(px, py, pc):
            return full_ref.at[pl.ds(pl.multiple_of(_dev_index(px, py, pc) * r, 16), r), :]

        def copy(k, block, to, src=None):
            return pltpu.make_async_remote_copy(
                src_ref=rows(*block) if src is None else src, dst_ref=rows(*block),
                send_sem=send_sems.at[k], recv_sem=recv_sems.at[k], device_id=to, device_id_type=MESH)

        got_ref[my_idx] = pay_ref[...]
        round1 = [small(0, got_ref, m) for m in range(1, N_DEV)]
        for cp in round1:
            cp.start()
        mine = pltpu.make_async_copy(sh_ref, rows(*me), local_sem)
        mine.start()
        first = [copy(0, me, sibling, src=sh_ref)] + [copy(1 + j, me, (*chip, c), src=sh_ref) for j, chip in enumerate(chips)]
        for cp in first:
            cp.start()

        for m in range(1, N_DEV):
            small(0, got_ref, m, arriving=True).wait_recv()
        cv = jnp.concatenate([got_ref[s, 0:1, :] for s in range(N_DEV)], axis=0)
        act = cv * _sigmoid(cv)
        act_ref[...] = act
        mod_ref[my_idx] = jnp.dot(act, w_ref[...], preferred_element_type=F32, precision=lax.Precision.HIGHEST) + b_ref[...]
        round2 = [small(1, mod_ref, m) for m in range(1, N_DEV)]
        for cp in round2:
            cp.start()

        passed = []
        for j, chip in enumerate(chips):
            copy(1 + j, (*chip, c), me).wait_recv()
            fwd = copy(4 + j, (*chip, c), sibling)
            fwd.start()
            passed.append(fwd)
        copy(0, sibling, me).wait_recv()
        for j, chip in enumerate(chips):
            copy(4 + j, (*chip, 1 - c), me).wait_recv()
        for m in range(1, N_DEV):
            small(1, mod_ref, m, arriving=True).wait_recv()
        for cp in first + passed + round1 + round2:
            cp.wait_send()
        mine.wait()

    anyspec = pl.BlockSpec(memory_space=pl.ANY)
    vmem = pl.BlockSpec(memory_space=pltpu.VMEM)
    return pl.pallas_call(
        body, name="gather_w_in_and_condition",
        out_shape=[jax.ShapeDtypeStruct((N_DEV * r, ccols), shard.dtype), jax.ShapeDtypeStruct((N_DEV, 8, D), F32),
                   jax.ShapeDtypeStruct((N_DEV, D), F32), jax.ShapeDtypeStruct((N_DEV, N_DEV, ncol), F32)],
        in_specs=[anyspec, vmem, vmem, vmem], out_specs=[anyspec, vmem, vmem, vmem],
        scratch_shapes=[pltpu.SemaphoreType.DMA((7,)), pltpu.SemaphoreType.DMA((7,)), pltpu.SemaphoreType.DMA((2, 7)),
                        pltpu.SemaphoreType.DMA((2, 7)), pltpu.SemaphoreType.DMA],
        compiler_params=_cparams(),
    )(shard, pay, w_ada, b_cols)


def _row_tile(r):
    for t in (256, 304, 128, 64, 16):
        if r % t == 0:
            return t
    return r


def _adamw_w_ada(w, act_t, gm_cols, m, v):
    r, ccols = w.shape
    tr = _row_tile(r)
    c1 = 1.0 / (1.0 - B1 ** STEP)
    c2 = 1.0 / (1.0 - B2 ** STEP)

    def body(w_ref, a_ref, gm_ref, m_ref, v_ref, g_ref, d_ref, nm_ref, nv_ref):
        gv = jnp.dot(a_ref[...], gm_ref[...], preferred_element_type=F32, precision=lax.Precision.HIGHEST)
        g_ref[...] = gv
        nm = B1 * m_ref[...] + (1.0 - B1) * gv
        nv = B2 * v_ref[...] + (1.0 - B2) * jnp.square(gv)
        nm_ref[...] = nm
        nv_ref[...] = nv
        d_ref[...] = -LR * ((nm * c1) / (jnp.sqrt(nv * c2) + ADAM_EPS) + WD * w_ref[...])

    blk = pl.BlockSpec((tr, ccols), lambda i: (i, 0))
    return pl.pallas_call(
        body, name="adamw_w_ada", grid=(r // tr,),
        in_specs=[blk, pl.BlockSpec((tr, N_DEV), lambda i: (i, 0)), _const_spec(gm_cols.shape), blk, blk], out_specs=[blk] * 4,
        out_shape=[jax.ShapeDtypeStruct((r, ccols), F32)] * 4,
        compiler_params=_cparams("parallel"),
    )(w, act_t, gm_cols, m, v)


def _sum_adamw(parts, own, slot, w, m, v, name, transposed=False):
    k, r, ccols = parts.shape
    tr = _row_tile(r)
    c1 = 1.0 / (1.0 - B1 ** STEP)
    c2 = 1.0 / (1.0 - B2 ** STEP)

    def body(s_ref, p_ref, own_ref, w_ref, m_ref, v_ref, g_ref, d_ref, nm_ref, nv_ref):
        gv = own_ref[0].astype(F32)
        for s in range(k):
            gv = gv + p_ref[s].astype(F32)
        if transposed:
            gv = gv.T
        g_ref[...] = gv
        nm = B1 * m_ref[...] + (1.0 - B1) * gv
        nv = B2 * v_ref[...] + (1.0 - B2) * jnp.square(gv)
        nm_ref[...] = nm
        nv_ref[...] = nv
        d_ref[...] = -LR * ((nm * c1) / (jnp.sqrt(nv * c2) + ADAM_EPS) + WD * w_ref[...])

    if transposed:
        blk = pl.BlockSpec((ccols, tr), lambda i, s_ref: (0, i))
    else:
        blk = pl.BlockSpec((tr, ccols), lambda i, s_ref: (i, 0))
    return pl.pallas_call(
        body, name=name,
        grid_spec=pltpu.PrefetchScalarGridSpec(
            num_scalar_prefetch=1, grid=(r // tr,),
            in_specs=[pl.BlockSpec((k, tr, ccols), lambda i, s_ref: (0, i, 0)),
                      pl.BlockSpec((1, tr, ccols), lambda i, s_ref: (s_ref[0], i, 0))] + [blk] * 3,
            out_specs=[blk] * 4),
        out_shape=[jax.ShapeDtypeStruct(w.shape, F32)] * 4,
        compiler_params=_cparams("parallel"),
    )(slot, parts, own, w, m, v)


VEC_ROWS = ((0, 6), (6, 7), (9, 11), (11, 14), (7, 8), (8, 9))


def _adamw_vectors(w, g, m, v):
    c1 = 1.0 / (1.0 - B1 ** STEP)
    c2 = 1.0 / (1.0 - B2 ** STEP)

    def put(refs, p):
        for ref, (lo, hi) in zip(refs, VEC_ROWS):
            if ref.shape == (3, HEAD):
                ref[...] = p[lo:hi, :HEAD]
            else:
                ref[...] = jnp.concatenate([p[k:k + 1] for k in range(lo, hi)], axis=1)

    def body(w_ref, g_ref, m_ref, v_ref, *outs):
        gv = g_ref[...]
        nm = B1 * m_ref[...] + (1.0 - B1) * gv
        nv = B2 * v_ref[...] + (1.0 - B2) * jnp.square(gv)
        delta = -LR * ((nm * c1) / (jnp.sqrt(nv * c2) + ADAM_EPS) + WD * w_ref[...])
        for kind, p in enumerate((gv, delta, nm, nv)):
            put(outs[6 * kind:6 * kind + 6], p)

    shapes = [(1, 6 * D), (1, D), (1, 2 * D), (3, HEAD), (1, D), (1, D)]
    out = pl.pallas_call(
        body, name="adamw_vectors", out_shape=[jax.ShapeDtypeStruct(sh, F32) for sh in shapes] * 4, compiler_params=_cparams(),
    )(w, g, m, v)
    fix = lambda t: (t[0], t[1], t[2], t[3][None], t[4], t[5].reshape(D))
    return [fix(out[6 * kind:6 * kind + 6]) for kind in range(4)]


def _pack_vectors(b_ada, g_mix, g_mlp, g_fin, b_gate, conv_w):
    conv_rows = jnp.pad(conv_w.reshape(3, HEAD), ((0, 0), (0, D - HEAD)))
    return jnp.concatenate([b_ada.reshape(6, D), g_mix.reshape(1, D), g_mlp.reshape(1, D), g_fin.reshape(1, D),
                            b_gate.reshape(2, D), conv_rows, jnp.zeros((2, D), F32)], axis=0)


def kernel(x, c, w_ada, b_ada, g_norm_mix, w_in, b_gate, conv_w, w_branch_attn, w_branch_conv, w_out, g_norm_mlp, w_mlp_in, w_mlp_out, g_norm_final, loss_target, m_w_ada, m_b_ada, m_g_norm_mix, m_w_in, m_b_gate, m_conv_w, m_w_branch_attn, m_w_branch_conv, m_w_out, m_g_norm_mlp, m_w_mlp_in, m_w_mlp_out, m_g_norm_final, v_w_ada, v_b_ada, v_g_norm_mix, v_w_in, v_b_gate, v_conv_w, v_w_branch_attn, v_w_branch_conv, v_w_out, v_g_norm_mlp, v_w_mlp_in, v_w_mlp_out, v_g_norm_final):
    S = x.shape[1]
    xi, yi, ci = _my_place()
    me = _dev_index(xi, yi, ci)
    x2 = x.reshape(S, D)
    tgt = loss_target.reshape(S, D)

    pay = jnp.zeros((8, D), F32).at[0].set(c[0]).at[1:4, :HEAD].set(conv_w[0])
    ncol = w_ada.shape[2]
    b_cols = lax.dynamic_slice(b_ada, (0, me * ncol), (1, ncol))
    w_int, got, act, mod_all = _gather_w_in_and_condition(w_in[0].T.astype(BF16), pay, w_ada[0], b_cols)
    cw8 = jnp.pad(got[:, 1:4, :HEAD].transpose(1, 0, 2).reshape(3, D), ((0, 5), (0, 0)))
    mod = lax.dynamic_index_in_dim(mod_all, me, axis=1, keepdims=False).reshape(6, D)
    late = [w_branch_attn[0].T.astype(BF16), w_branch_conv[0].astype(BF16), w_out[0].astype(BF16),
            w_mlp_in[0].T.astype(BF16), w_mlp_out[0].astype(BF16)]
    w_int, late = lax.optimization_barrier((w_int, late))
    zones = _own_rows_into_zones(late, me.reshape(1).astype(jnp.int32))
    ag_mix = _split_start("gather_mix_start", "gather", late[:3], zones[:3])
    ag_mlp = _split_start("gather_mlp_start", "gather", late[3:], zones[3:])
    mod = mod + ag_mix[4] + ag_mlp[4]
    h = _prenorm(x2, g_norm_mix, mod[1:2], mod[0:1])

    def mix_weights(o_attn):
        return _split_wait("gather_mix_wait", "gather", *ag_mix[:4], o_attn)[1]

    def mlp_weights(x1):
        return _split_wait("gather_mlp_wait", "gather", *ag_mlp[:4], x1)[1]

    rs = {}

    def mlp_grads_ready(*grads):
        lands = [lax.empty((N_PEER, t.shape[0] // N_DEV, t.shape[1]), BF16) for t in grads]
        rs["mlp"] = _split_start("scatter_mlp_start", "scatter", grads, lands)
        return rs["mlp"][4]

    def w_in_grad_ready(g_in):
        r = g_in.shape[0] // N_DEV
        rs["sib"] = _split_start("sibling_w_in_start", "sibling", [g_in], [lax.empty((4, r, g_in.shape[1]), BF16)])
        return rs["sib"][5]

    def other_grads_ready(*small):
        core = ci.reshape(1).astype(jnp.int32)
        (g_in,), (sib_in,) = _split_wait("sibling_w_in_wait", "sibling", *rs["sib"][:4], small[0])
        pair = _pair_sums([g_in, *small], [sib_in, *_sibling_exchange(small)], core)
        lands = [lax.empty((3,) + t.shape[1:], BF16) for t in pair]
        rs["rest"] = _split_start("scatter_rest_start", "chips", pair, lands)
        return rs["rest"][4]

    ba, bb = b_gate[:, :D], b_gate[:, D:]
    grad_x, vec = _local_step(
        x2, h, tgt, mod, g_norm_mix, g_norm_mlp, g_norm_final.reshape(1, D), ba, bb, cw8, w_int, mix_weights, mlp_weights,
        mlp_grads_ready, w_in_grad_ready, other_grads_ready)

    vec_all, vec_sum = _allgather_small(vec, "gather_vec")
    loss = vec_sum[14, 0]
    gm_all = vec_all[:, 0:6, :].reshape(N_DEV, 6 * D)
    gm_cols = lax.dynamic_slice(gm_all, (0, me * ncol), (N_DEV, ncol))
    conv_cols = lax.dynamic_slice(vec_sum[11:14], (0, me * HEAD), (3, HEAD))
    g_pack = jnp.concatenate([vec_sum[0:11], jnp.pad(conv_cols, ((0, 0), (0, D - HEAD))), jnp.zeros((2, D), F32)], axis=0)
    packs = [_pack_vectors(*t) for t in ((b_ada, g_norm_mix, g_norm_mlp, g_norm_final, b_gate, conv_w),
                                         (m_b_ada, m_g_norm_mix, m_g_norm_mlp, m_g_norm_final, m_b_gate, m_conv_w),
                                         (v_b_ada, v_g_norm_mix, v_g_norm_mlp, v_g_norm_final, v_b_gate, v_conv_w))]
    gv, dv, mv, vv = _adamw_vectors(packs[0], g_pack, packs[1], packs[2])
    g_w_ada, d_ada, nm_ada, nv_ada = _adamw_w_ada(w_ada[0], act.T, gm_cols, m_w_ada[0], v_w_ada[0])

    big = {}
    srcs, lands = _split_wait("scatter_mlp_wait", "scatter", *rs["mlp"][:4], d_ada)
    own = [g.reshape((N_DEV,) + land.shape[1:]) for g, land in zip(srcs, lands)]
    slot = me.reshape(1).astype(jnp.int32)
    big["w_mi"] = tuple(t[None] for t in _sum_adamw(lands[0], own[0], slot, w_mlp_in[0], m_w_mlp_in[0], v_w_mlp_in[0], "adamw_w_mi",
                                                    transposed=True))
    big["w_mo"] = tuple(t[None] for t in _sum_adamw(lands[1], own[1], slot, w_mlp_out[0], m_w_mlp_out[0], v_w_mlp_out[0], "adamw_w_mo"))
    own, lands = _split_wait("scatter_rest_wait", "chips", *rs["rest"][:4], big["w_mo"][1])
    slot = (2 * xi + yi).reshape(1).astype(jnp.int32)
    big["w_in"] = tuple(t.T[None] for t in _sum_adamw(lands[0], own[0], slot, w_in[0].T, m_w_in[0].T, v_w_in[0].T, "adamw_w_in"))
    big["w_ba"] = tuple(t[None] for t in _sum_adamw(lands[1], own[1], slot, w_branch_attn[0], m_w_branch_attn[0], v_w_branch_attn[0],
                                                    "adamw_w_ba", transposed=True))
    big["w_bc"] = tuple(t[None] for t in _sum_adamw(lands[2], own[2], slot, w_branch_conv[0], m_w_branch_conv[0], v_w_branch_conv[0], "adamw_w_bc"))
    big["w_out"] = tuple(t[None] for t in _sum_adamw(lands[3], own[3], slot, w_out[0], m_w_out[0], v_w_out[0], "adamw_w_out"))

    def ordered(k, ada, vecs):
        return (ada[None], vecs[0], vecs[1], big["w_in"][k], vecs[2], vecs[3], big["w_ba"][k], big["w_bc"][k],
                big["w_out"][k], vecs[4], big["w_mi"][k], big["w_mo"][k], vecs[5])

    return (loss, grad_x.reshape(1, S, D), *ordered(0, g_w_ada, gv), *ordered(1, d_ada, dv),
            *ordered(2, nm_ada, mv), *ordered(3, nv_ada, vv))
```

```python
import numpy as np
import jax
import jax.numpy as jnp
from jax import lax
from jax.experimental import pallas as pl
from jax.experimental.pallas import tpu as pltpu

F32, BF16 = jnp.float32, jnp.bfloat16
D = 1024
HEAD = 128
DILATIONS = (1, 4, 16)
N_SLOT = 4
AOW = N_SLOT * HEAD
DFF = 4 * D
N_DEV = 8
UNROLL = 16
UNROLL_FWD = 32
EPS = 1e-6
NEG = -1e30
SCALE = HEAD ** -0.5
LR, B1, B2, ADAM_EPS, WD, STEP = 0.001, 0.9, 0.999, 1e-08, 0.01, 10
V7X_VMEM_LIMIT = 56 * 1024 * 1024
TM = 1024
MESH = pl.DeviceIdType.MESH


def _cparams(*sem):
    if sem:
        return pltpu.CompilerParams(dimension_semantics=sem, vmem_limit_bytes=V7X_VMEM_LIMIT)
    return pltpu.CompilerParams(vmem_limit_bytes=V7X_VMEM_LIMIT)


def _nn(a, b):
    return jnp.dot(a, b, preferred_element_type=F32)


def _nt(a, b):
    return lax.dot_general(a, b, (((1,), (1,)), ((), ())), preferred_element_type=F32)


def _tn(a, b):
    return lax.dot_general(a, b, (((0,), (0,)), ((), ())), preferred_element_type=F32)


def _rms_r(x):
    return lax.rsqrt(jnp.mean(x * x, axis=-1, keepdims=True) + EPS)


def _rms_bwd(x, r, g, dn):
    gy = dn * g
    dx = r * gy - x * (r * r * r) * jnp.mean(x * gy, axis=-1, keepdims=True)
    return dx, dn * (x * r)


def _sigmoid(t):
    return 1.0 / (1.0 + jnp.exp(-t))


def _rowsum(v):
    return jnp.sum(v, axis=0, keepdims=True)


def _vec_spec(n=D):
    return pl.BlockSpec((1, n), lambda *_: (0, 0))


def _const_spec(shape):
    nd = len(shape)
    return pl.BlockSpec(shape, lambda *_: (0,) * nd)


def _win_rowblock(j):
    return jnp.where(j < 9, (j % 3) * 3 + j // 3, j)


def _prenorm(x, g, sc, sh):
    S = x.shape[0]
    tm = TM

    def body(x_ref, g_ref, sc_ref, sh_ref, h_ref):
        xv = x_ref[...]
        h_ref[...] = (xv * _rms_r(xv) * g_ref[...] * (1.0 + sc_ref[...]) + sh_ref[...]).astype(BF16)

    row = pl.BlockSpec((tm, D), lambda i: (i, 0))
    return pl.pallas_call(
        body, name="prenorm", grid=(S // tm,), in_specs=[row, _vec_spec(), _vec_spec(), _vec_spec()], out_specs=row,
        out_shape=jax.ShapeDtypeStruct((S, D), BF16), compiler_params=_cparams("parallel"),
    )(x, g, sc, sh)


def _proj(h, w_int):
    S = h.shape[0]

    def body(h_ref, w_ref, q_ref, e_ref):
        j = pl.program_id(0)
        acc = _nt(h_ref[...], w_ref[...])

        @pl.when(j < 9)
        def _():
            q_ref[0] = acc

        @pl.when(j >= 9)
        def _():
            e_ref[0] = acc.astype(BF16)

    def e_idx(j):
        k = jnp.maximum(j - 9, 0)
        return (k // 2, 0, k % 2)

    return pl.pallas_call(
        body, name="proj", grid=(19,),
        in_specs=[pl.BlockSpec((S, D), lambda j: (0, 0), pipeline_mode=pl.Buffered(1)),
                  pl.BlockSpec((512, D), lambda j: (_win_rowblock(j), 0))],
        out_specs=[pl.BlockSpec((1, S, 512), lambda j: (jnp.minimum(j, 8), 0, 0)), pl.BlockSpec((1, S, 512), e_idx)],
        out_shape=[jax.ShapeDtypeStruct((9, S, 512), F32), jax.ShapeDtypeStruct((5, S, D), BF16)],
        compiler_params=_cparams("arbitrary"),
    )(h, w_int)


def _bias_table():
    slopes = (2.0 ** (-8.0 * np.arange(1, 13, dtype=np.float32) / 12.0)).astype(np.float32)
    qi = np.arange(HEAD)[:, None]
    kj = np.arange(2 * HEAD)[None, :]
    delta = HEAD + qi - kj
    mask = (delta >= 0) & (delta <= HEAD)
    out = np.zeros((3, N_SLOT, HEAD, 2 * HEAD), np.float32)
    for gi, d in enumerate(DILATIONS):
        for j in range(N_SLOT):
            bias = -slopes[gi * N_SLOT + j] * (delta * d).astype(np.float32)
            out[gi, j] = np.where(mask, bias, NEG)
    out_t = np.concatenate([out[..., HEAD:].swapaxes(-1, -2), out[..., :HEAD].swapaxes(-1, -2)], axis=-1)
    return jnp.asarray(out), jnp.asarray(out_t)


def _attn_fwd(qkv, bias):
    S = qkv.shape[2]
    nblk = S // HEAD
    rows = 256

    def body(qkv_ref, b_ref, o_ref, lse_ref, o_s, lse_s):
        g = pl.program_id(1)
        bias = b_ref[0, 0]
        col = lax.broadcasted_iota(jnp.int32, bias.shape, 1)
        bias_first = jnp.where(col < HEAD, NEG, bias)

        for gi, d in enumerate(DILATIONS):
            @pl.when(g == gi)
            def _(gi=gi, d=d):
                nb = nblk // d

                def keys(start):
                    sl = pl.ds(start, HEAD, stride=d)
                    return qkv_ref.at[0, 1][sl, :].astype(BF16), qkv_ref.at[0, 2][sl, :].astype(BF16)

                def step(b, first_of_residue, before):
                    r, n = b // nb, b % nb
                    cur = pl.ds(n * (HEAD * d) + r, HEAD, stride=d)
                    own = keys(n * (HEAD * d) + r)
                    if first_of_residue:
                        before = own
                    q = qkv_ref.at[0, 0][cur, :].astype(BF16)
                    kw = jnp.concatenate([before[0], own[0]], axis=0)
                    vw = jnp.concatenate([before[1], own[1]], axis=0)
                    s = _nt(q, kw) * SCALE + jnp.where(n > 0, bias, bias_first)
                    m = jnp.max(s, axis=-1, keepdims=True)
                    p = jnp.exp(s - m)
                    l = jnp.sum(p, axis=-1, keepdims=True)
                    o_s.at[gi][cur, :] = _nn(p.astype(BF16), vw) / l
                    lse_s.at[gi][cur, :] = jnp.broadcast_to(m + jnp.log(l), (HEAD, HEAD))
                    return own

                def steps(i, before):
                    for u in range(UNROLL_FWD):
                        before = step(i * UNROLL_FWD + u, nb <= UNROLL_FWD and u % nb == 0, before)
                    return before

                lax.fori_loop(0, nblk // UNROLL_FWD, steps, keys(0))

        @pl.when(g == len(DILATIONS) - 1)
        def _():
            def merge(i, carry):
                r = pl.ds(pl.multiple_of(i * rows, rows), rows)
                ls = [lse_s[k, r, :] for k in range(3)]
                top = jnp.maximum(jnp.maximum(ls[0], ls[1]), ls[2])
                ws = [jnp.exp(t - top) for t in ls]
                den = ws[0] + ws[1] + ws[2]
                o_ref[r, :] = (ws[0] * o_s[0, r, :] + ws[1] * o_s[1, r, :] + ws[2] * o_s[2, r, :]) / den
                lse_ref[r, :] = top + jnp.log(den)
                return carry

            lax.fori_loop(0, S // rows, merge, 0)

    return pl.pallas_call(
        body, name="attn_fwd", grid=(N_SLOT, 3),
        in_specs=[pl.BlockSpec((1, 3, S, HEAD), lambda j, g: (g, 0, 0, j)),
                  pl.BlockSpec((1, 1, HEAD, 2 * HEAD), lambda j, g: (g, j, 0, 0))],
        out_specs=[pl.BlockSpec((S, HEAD), lambda j, g: (0, j)), pl.BlockSpec((S, HEAD), lambda j, g: (0, j))],
        out_shape=[jax.ShapeDtypeStruct((S, AOW), F32), jax.ShapeDtypeStruct((S, AOW), F32)],
        scratch_shapes=[pltpu.VMEM((3, S, HEAD), F32)] * 2,
        compiler_params=_cparams("parallel", "arbitrary"),
    )(qkv, bias)


def _shift_down(z, k, halo_rows):
    out = pltpu.roll(z, k, axis=0)
    top = out[:8]
    rid = lax.broadcasted_iota(jnp.int32, top.shape, 0)
    for t in range(k):
        top = jnp.where(rid == t, halo_rows[t], top)
    return jnp.concatenate([top, out[8:]], axis=0)


def _shift_up(z, k, halo_rows):
    n = z.shape[0]
    out = pltpu.roll(z, n - k, axis=0)
    bottom = out[n - 8:]
    rid = lax.broadcasted_iota(jnp.int32, bottom.shape, 0)
    for t in range(k):
        bottom = jnp.where(rid == 8 - k + t, halo_rows[t], bottom)
    return jnp.concatenate([out[:n - 8], bottom], axis=0)


def _e_spec(chunk, tm):
    return pl.BlockSpec((1, tm, D), lambda i, c=chunk: (c, i, 0))


def _e_prev_spec(chunk, tm):
    return pl.BlockSpec((1, 16, D), lambda i, c=chunk: (c, jnp.maximum(i * (tm // 16) - 1, 0), 0))


def _e_next_spec(chunk, tm, S):
    return pl.BlockSpec((1, 16, D), lambda i, c=chunk: (c, jnp.minimum((i + 1) * (tm // 16), S // 16 - 1), 0))


def _mix(o_attn, e, cw8, ba, bb, w_bat, w_bc):
    S = o_attn.shape[0]
    tm = 512

    def body(o_ref, cb_ref, cc_ref, cx_ref, ga_ref, gb_ref, ccp_ref, cxp_ref, cw_ref, ba_ref, bb_ref, wba_ref, wbc_ref,
             obf_ref, cbu_ref, ya_ref, yc_ref, mg_ref):
        i = pl.program_id(0)
        o = o_ref[...].astype(BF16)
        obf_ref[...] = o
        ya = _nt(o, wba_ref[...])
        z = cc_ref[0].astype(F32) * cx_ref[0].astype(F32)
        zp = ccp_ref[0].astype(F32) * cxp_ref[0].astype(F32) * (i > 0).astype(F32)
        z1 = _shift_down(z, 1, [zp[15:16]])
        z2 = _shift_down(z, 2, [zp[14:15], zp[15:16]])
        cw = cw_ref[...]
        u = cw[0:1] * z2 + cw[1:2] * z1 + cw[2:3] * z
        cbu = (cb_ref[0].astype(F32) * u).astype(BF16)
        cbu_ref[...] = cbu
        yc = _nn(cbu, wbc_ref[...])
        sa = _sigmoid(ga_ref[0].astype(F32) + ba_ref[...])
        sb = _sigmoid(gb_ref[0].astype(F32) + bb_ref[...])
        ya_ref[...] = ya.astype(BF16)
        yc_ref[...] = yc.astype(BF16)
        mg_ref[...] = (sa * ya + sb * yc).astype(BF16)

    row = lambda w: pl.BlockSpec((tm, w), lambda i: (i, 0))
    return pl.pallas_call(
        body, name="mix", grid=(S // tm,),
        in_specs=[row(AOW)] + [_e_spec(c, tm) for c in range(5)] + [_e_prev_spec(1, tm), _e_prev_spec(2, tm),
                  _const_spec((8, D)), _vec_spec(), _vec_spec(), _const_spec((D, AOW)), _const_spec((D, D))],
        out_specs=[row(AOW), row(D), row(D), row(D), row(D)],
        out_shape=[jax.ShapeDtypeStruct((S, AOW), BF16)] + [jax.ShapeDtypeStruct((S, D), BF16)] * 4,
        compiler_params=_cparams("parallel"),
    )(o_attn, e, e, e, e, e, e, e, cw8, ba, bb, w_bat, w_bc)


def _out_proj(merged, w_out, x, gate1, g_mlp, sc2, sh2):
    S = x.shape[0]
    tm = TM

    def body(mg_ref, w_ref, x_ref, gt_ref, g_ref, sc_ref, sh_ref, x1_ref, mo_ref, h2_ref):
        mo = _nn(mg_ref[...], w_ref[...])
        mo_ref[...] = mo.astype(BF16)
        x1 = x_ref[...] + gt_ref[...] * mo
        x1_ref[...] = x1
        h2 = x1 * _rms_r(x1) * g_ref[...] * (1.0 + sc_ref[...]) + sh_ref[...]
        h2_ref[...] = h2.astype(BF16)

    row = pl.BlockSpec((tm, D), lambda i: (i, 0))
    return pl.pallas_call(
        body, name="out_proj", grid=(S // tm,),
        in_specs=[row, _const_spec((D, D)), row, _vec_spec(), _vec_spec(), _vec_spec(), _vec_spec()],
        out_specs=[row, row, row],
        out_shape=[jax.ShapeDtypeStruct((S, D), F32), jax.ShapeDtypeStruct((S, D), BF16), jax.ShapeDtypeStruct((S, D), BF16)],
        compiler_params=_cparams("parallel"),
    )(merged, w_out, x, gate1, g_mlp, sc2, sh2)


def _mlp_in(h2, w_mit):
    S = h2.shape[0]
    tm, tn = TM, 2048

    def body(h_ref, w_ref, a_ref, f_ref):
        a = _nt(h_ref[...], w_ref[...])
        a_ref[...] = a.astype(BF16)
        f_ref[...] = jnp.square(jnp.maximum(a, 0.0)).astype(BF16)

    blk = pl.BlockSpec((tm, tn), lambda i, j: (i, j))
    return pl.pallas_call(
        body, name="mlp_in", grid=(S // tm, DFF // tn),
        in_specs=[pl.BlockSpec((tm, D), lambda i, j: (i, 0)), pl.BlockSpec((tn, D), lambda i, j: (j, 0))],
        out_specs=[blk, blk],
        out_shape=[jax.ShapeDtypeStruct((S, DFF), BF16)] * 2,
        compiler_params=_cparams("parallel", "parallel"),
    )(h2, w_mit)


def _mlp_out(f, w_mo, x1, gate2, g_fin, tgt):
    S = x1.shape[0]
    tm = 512
    half = tm // 2

    def body(f_ref, w_ref, x1_ref, gt_ref, g_ref, t_ref, mlp_ref, dx2_ref, pv_ref):
        @pl.when(pl.program_id(0) == 0)
        def _():
            pv_ref[...] = jnp.zeros_like(pv_ref)

        g = g_ref[...]
        for hs in (pl.ds(0, half), pl.ds(half, half)):
            mlp = _nn(f_ref[hs, :], w_ref[...])
            mlp_ref[hs, :] = mlp.astype(BF16)
            x2 = x1_ref[hs, :] + gt_ref[...] * mlp
            r = _rms_r(x2)
            err = x2 * r * g - t_ref[hs, :]
            dx2, pg = _rms_bwd(x2, r, g, err * (1.0 / D))
            dx2_ref[hs, :] = dx2
            pv_ref[0:1, :] += _rowsum(pg)
            pv_ref[1:2, :] += 0.5 * _rowsum(jnp.mean(err * err, axis=-1, keepdims=True))

    row = pl.BlockSpec((tm, D), lambda i: (i, 0))
    return pl.pallas_call(
        body, name="mlp_out", grid=(S // tm,),
        in_specs=[pl.BlockSpec((tm, DFF), lambda i: (i, 0)), _const_spec((DFF, D)), row, _vec_spec(), _vec_spec(), row],
        out_specs=[row, row, _const_spec((8, D))],
        out_shape=[jax.ShapeDtypeStruct((S, D), BF16), jax.ShapeDtypeStruct((S, D), F32), jax.ShapeDtypeStruct((8, D), F32)],
        compiler_params=_cparams("arbitrary"),
    )(f, w_mo, x1, gate2, g_fin, tgt)


def _bwd_mlp_a(dx2, gate2, mlp, w_mo, a):
    S = dx2.shape[0]
    tm = 512
    half = tm // 2

    def body(dx_ref, gt_ref, mlp_ref, w_ref, a_ref, da_ref, dmo_ref, pv_ref):
        @pl.when(pl.program_id(0) == 0)
        def _():
            pv_ref[...] = jnp.zeros_like(pv_ref)

        for hs in (pl.ds(0, half), pl.ds(half, half)):
            dx = dx_ref[hs, :]
            dmo = (dx * gt_ref[...]).astype(BF16)
            dmo_ref[hs, :] = dmo
            pv_ref[0:1, :] += _rowsum(dx * mlp_ref[hs, :].astype(F32))
            df = _nt(dmo, w_ref[...])
            da_ref[hs, :] = (df * (2.0 * jnp.maximum(a_ref[hs, :].astype(F32), 0.0))).astype(BF16)

    row = pl.BlockSpec((tm, D), lambda i: (i, 0))
    wide = pl.BlockSpec((tm, DFF), lambda i: (i, 0))
    return pl.pallas_call(
        body, name="bwd_mlp_a", grid=(S // tm,),
        in_specs=[row, _vec_spec(), row, _const_spec((DFF, D)), wide],
        out_specs=[wide, row, _const_spec((8, D))],
        out_shape=[jax.ShapeDtypeStruct((S, DFF), BF16), jax.ShapeDtypeStruct((S, D), BF16), jax.ShapeDtypeStruct((8, D), F32)],
        compiler_params=_cparams("arbitrary"),
    )(dx2, gate2, mlp, w_mo, a)


def _bwd_mlp_b(da, w_mit, x1, dx2, g_mlp, sc2):
    S = x1.shape[0]
    tm = 512
    half = tm // 2

    def body(da_ref, w_ref, x1_ref, dx2_ref, g_ref, sc_ref, dx1_ref, pv_ref):
        @pl.when(pl.program_id(0) == 0)
        def _():
            pv_ref[...] = jnp.zeros_like(pv_ref)

        g = g_ref[...]
        for hs in (pl.ds(0, half), pl.ds(half, half)):
            dh = _nn(da_ref[hs, :], w_ref[...])
            x1 = x1_ref[hs, :]
            r = _rms_r(x1)
            dxn, pg = _rms_bwd(x1, r, g, dh * (1.0 + sc_ref[...]))
            dx1_ref[hs, :] = dx2_ref[hs, :] + dxn
            pv_ref[0:1, :] += _rowsum(dh)
            pv_ref[1:2, :] += _rowsum(dh * (x1 * r * g))
            pv_ref[2:3, :] += _rowsum(pg)

    row = pl.BlockSpec((tm, D), lambda i: (i, 0))
    return pl.pallas_call(
        body, name="bwd_mlp_b", grid=(S // tm,),
        in_specs=[pl.BlockSpec((tm, DFF), lambda i: (i, 0)), _const_spec((DFF, D)), row, row, _vec_spec(), _vec_spec()],
        out_specs=[row, _const_spec((8, D))],
        out_shape=[jax.ShapeDtypeStruct((S, D), F32), jax.ShapeDtypeStruct((8, D), F32)],
        compiler_params=_cparams("arbitrary"),
    )(da, w_mit, x1, dx2, g_mlp, sc2)


def _bwd_mix(dx1, gate1, mo, e, cw8, ba, bb, ya, yc, o_attn, w_out, w_bc, w_bat):
    S = dx1.shape[0]
    tm = 256
    n_tiles = S // tm

    def body(dx_ref, dxn_ref, gt_ref, mo_ref, cb_ref, cc_ref, cx_ref, ga_ref, gb_ref, cbn_ref, gbn_ref, ccp_ref, cxp_ref,
             cw_ref, ba_ref, bb_ref, ya_ref, yc_ref, o_ref, wout_ref, wbc_ref, wba_ref,
             dmo_ref, dya_ref, dyc_ref, do_ref, dl_ref, de_ref, pv_ref):
        i = pl.program_id(0)

        @pl.when(i == 0)
        def _():
            pv_ref[...] = jnp.zeros_like(pv_ref)

        dx = dx_ref[...]
        cb = cb_ref[0].astype(F32)
        cc = cc_ref[0].astype(F32)
        cx = cx_ref[0].astype(F32)
        dmo_all = (jnp.concatenate([dx, dxn_ref[...]], axis=0) * gt_ref[...]).astype(BF16)
        dmg_all = _nt(dmo_all, wout_ref[...])
        sb_all = _sigmoid(jnp.concatenate([gb_ref[0], gbn_ref[0]], axis=0).astype(F32) + bb_ref[...])
        dyc_all = dmg_all * sb_all
        dcbu_all = _nt(dyc_all.astype(BF16), wbc_ref[...])
        dmo, dmg, sb, dyc, dcbu = dmo_all[:tm], dmg_all[:tm], sb_all[:tm], dyc_all[:tm], dcbu_all[:tm]
        dmo_ref[...] = dmo
        pv_ref[0:1, :] += _rowsum(dx * mo_ref[...].astype(F32))
        sa = _sigmoid(ga_ref[0].astype(F32) + ba_ref[...])
        dya = (dmg * sa).astype(BF16)
        dya_ref[...] = dya
        dyc_ref[...] = dyc.astype(BF16)
        dga = dmg * ya_ref[...].astype(F32) * sa * (1.0 - sa)
        dgb = dmg * yc_ref[...].astype(F32) * sb * (1.0 - sb)
        pv_ref[1:2, :] += _rowsum(dga)
        pv_ref[2:3, :] += _rowsum(dgb)

        do = _nn(dya, wba_ref[...])
        do_ref[...] = do
        prod = do * o_ref[...]
        dl_ref[...] = jnp.concatenate(
            [jnp.broadcast_to(jnp.sum(prod[:, s * HEAD:(s + 1) * HEAD], axis=-1, keepdims=True), (tm, HEAD))
             for s in range(N_SLOT)], axis=1)

        z = cc * cx
        zp = ccp_ref[0].astype(F32) * cxp_ref[0].astype(F32) * (i > 0).astype(F32)
        z1 = _shift_down(z, 1, [zp[15:16]])
        z2 = _shift_down(z, 2, [zp[14:15], zp[15:16]])
        cw = cw_ref[...]
        u = cw[0:1] * z2 + cw[1:2] * z1 + cw[2:3] * z
        du = dcbu * cb
        du_n = dcbu_all[tm:] * cbn_ref[0].astype(F32) * (i < n_tiles - 1).astype(F32)
        du1 = _shift_up(du, 1, [du_n[0:1]])
        du2 = _shift_up(du, 2, [du_n[0:1], du_n[1:2]])
        dz = cw[2:3] * du + cw[1:2] * du1 + cw[0:1] * du2
        pv_ref[3:4, :] += _rowsum(du * z2)
        pv_ref[4:5, :] += _rowsum(du * z1)
        pv_ref[5:6, :] += _rowsum(du * z)

        de_ref[0] = (dcbu * u).astype(BF16)
        de_ref[1] = (dz * cx).astype(BF16)
        de_ref[2] = (dz * cc).astype(BF16)
        de_ref[3] = dga.astype(BF16)
        de_ref[4] = dgb.astype(BF16)

    row = lambda w: pl.BlockSpec((tm, w), lambda i: (i, 0))
    nxt = pl.BlockSpec((16, D), lambda i: (jnp.minimum((i + 1) * (tm // 16), S // 16 - 1), 0))
    return pl.pallas_call(
        body, name="bwd_mix", grid=(n_tiles,),
        in_specs=[row(D), nxt, _vec_spec(), row(D)] + [_e_spec(c, tm) for c in range(5)]
                 + [_e_next_spec(0, tm, S), _e_next_spec(4, tm, S), _e_prev_spec(1, tm), _e_prev_spec(2, tm),
                    _const_spec((8, D)), _vec_spec(), _vec_spec(), row(D), row(D), row(AOW),
                    _const_spec((D, D)), _const_spec((D, D)), _const_spec((D, AOW))],
        out_specs=[row(D), row(D), row(D), row(AOW), row(AOW), pl.BlockSpec((5, tm, D), lambda i: (0, i, 0)),
                   _const_spec((8, D))],
        out_shape=[jax.ShapeDtypeStruct((S, D), BF16)] * 3 + [jax.ShapeDtypeStruct((S, AOW), F32)] * 2
                  + [jax.ShapeDtypeStruct((5, S, D), BF16), jax.ShapeDtypeStruct((8, D), F32)],
        compiler_params=_cparams("arbitrary"),
    )(dx1, dx1, gate1, mo, e, e, e, e, e, e, e, e, e, cw8, ba, bb, ya, yc, o_attn, w_out, w_bc, w_bat)


def _attn_bwd(qkv, do, lse, dl, bias_t):
    S = qkv.shape[2]
    nblk = S // HEAD

    def body(qkv_ref, do_ref, lse_ref, dl_ref, b_ref, d_ref):
        g = pl.program_id(1)
        bias = b_ref[0, 0]
        col = lax.broadcasted_iota(jnp.int32, bias.shape, 1)
        bias_last = jnp.where(col >= HEAD, NEG, bias)
        eye = (lax.broadcasted_iota(jnp.int32, (HEAD, HEAD), 0) == lax.broadcasted_iota(jnp.int32, (HEAD, HEAD), 1)).astype(F32)

        def as_row(t):
            return jnp.sum(t * eye, axis=0, keepdims=True)

        for gi, d in enumerate(DILATIONS):
            @pl.when(g == gi)
            def _(d=d):
                nb = nblk // d

                def query_side(start):
                    sl = pl.ds(start, HEAD, stride=d)
                    return (qkv_ref.at[0, 0][sl, :].astype(BF16), do_ref[sl, :].astype(BF16),
                            as_row(lse_ref[sl, :]), as_row(dl_ref[sl, :]))

                def step(b, first_of_residue, carry):
                    dq_part, own = carry
                    r, n = b // nb, b % nb
                    cur = pl.ds(n * (HEAD * d) + r, HEAD, stride=d)
                    if first_of_residue:
                        own = query_side(r)
                    nxt = query_side(jnp.minimum(n + 1, nb - 1) * (HEAD * d) + r)
                    q2 = jnp.concatenate([own[0], nxt[0]], axis=0)
                    do2 = jnp.concatenate([own[1], nxt[1]], axis=0)
                    k = qkv_ref.at[0, 1][cur, :].astype(BF16)
                    v = qkv_ref.at[0, 2][cur, :].astype(BF16)
                    s = _nt(k, q2) * SCALE + jnp.where(n < nb - 1, bias, bias_last)
                    p = jnp.exp(s - jnp.concatenate([own[2], nxt[2]], axis=1))
                    d_ref.at[0, 2][cur, :] = _nn(p.astype(BF16), do2)
                    dp = _nt(v, do2)
                    ds = (p * (dp - jnp.concatenate([own[3], nxt[3]], axis=1)) * SCALE).astype(BF16)
                    d_ref.at[0, 1][cur, :] = _nn(ds, q2)
                    dq2 = _tn(ds, k)
                    d_ref.at[0, 0][cur, :] = dq2[:HEAD] + jnp.where(n > 0, dq_part, 0.0)
                    return dq2[HEAD:], nxt

                def steps(i, carry):
                    for u in range(UNROLL):
                        carry = step(i * UNROLL + u, nb <= UNROLL and u % nb == 0, carry)
                    return carry

                lax.fori_loop(0, nblk // UNROLL, steps, (jnp.zeros((HEAD, HEAD), F32), query_side(0)))

    col_blk = pl.BlockSpec((S, HEAD), lambda j, g: (0, j))
    qkv_blk = pl.BlockSpec((1, 3, S, HEAD), lambda j, g: (g, 0, 0, j))
    return pl.pallas_call(
        body, name="attn_bwd", grid=(N_SLOT, 3),
        in_specs=[qkv_blk, col_blk, col_blk, col_blk, pl.BlockSpec((1, 1, HEAD, 2 * HEAD), lambda j, g: (g, j, 0, 0))],
        out_specs=qkv_blk,
        out_shape=jax.ShapeDtypeStruct((3, 3, S, AOW), F32),
        compiler_params=_cparams("parallel", "arbitrary"),
    )(qkv, do, lse, dl, bias_t)


def _bwd_in(dqkv, de, w_int, x, dx1, g_mix, sc1):
    S = x.shape[0]
    tm = TM
    dqkv = dqkv.reshape(3, 3, S, AOW)

    def body(dq_ref, de_ref, wq_ref, wk_ref, wv_ref, wa_ref, wb_ref, x_hbm, dx1_hbm, g_ref, sc_ref, gx_ref, pv_ref,
             x_ref, dx1_ref, sems):
        acc = gx_ref
        i, k = pl.program_id(0), pl.program_id(1)
        tile = pl.ds(pl.multiple_of(i * tm, tm), tm)
        late = [pltpu.make_async_copy(x_hbm.at[tile, :], x_ref, sems.at[0]),
                pltpu.make_async_copy(dx1_hbm.at[tile, :], dx1_ref, sems.at[1])]

        @pl.when((i == 0) & (k == 0))
        def _():
            pv_ref[...] = jnp.zeros_like(pv_ref)

        @pl.when(k == 0)
        def _():
            acc[...] = jnp.zeros_like(acc)
            for cp in late:
                cp.start()

        @pl.when(k < 3)
        def _():
            lhs = jnp.concatenate([dq_ref[0, t].astype(BF16) for t in range(3)], axis=1)
            acc[...] += _nn(lhs, jnp.concatenate([wq_ref[...], wk_ref[...], wv_ref[...]], axis=0))

        @pl.when(k >= 3)
        def _():
            acc[...] += _nn(de_ref[0], jnp.concatenate([wa_ref[...], wb_ref[...]], axis=0))

        @pl.when(k == 7)
        def _():
            for cp in late:
                cp.wait()
            dh = acc[...]
            xv = x_ref[...]
            r = _rms_r(xv)
            g = g_ref[...]
            dxn, pg = _rms_bwd(xv, r, g, dh * (1.0 + sc_ref[...]))
            gx_ref[...] = dx1_ref[...] + dxn
            pv_ref[0:1, :] += _rowsum(dh)
            pv_ref[1:2, :] += _rowsum(dh * (xv * r * g))
            pv_ref[2:3, :] += _rowsum(pg)

    grp = lambda k: jnp.minimum(k, 2)
    chunk = lambda k: jnp.maximum(k - 3, 0)
    wblk = lambda f: pl.BlockSpec((512, D), lambda i, k: (f(k), 0))
    row = pl.BlockSpec((tm, D), lambda i, k: (i, 0))
    anyspec = pl.BlockSpec(memory_space=pl.ANY)
    return pl.pallas_call(
        body, name="bwd_in", grid=(S // tm, 8),
        in_specs=[pl.BlockSpec((1, 3, tm, 512), lambda i, k: (grp(k), 0, i, 0)),
                  pl.BlockSpec((1, tm, D), lambda i, k: (chunk(k), i, 0)),
                  wblk(grp), wblk(lambda k: 3 + grp(k)), wblk(lambda k: 6 + grp(k)),
                  wblk(lambda k: 9 + 2 * chunk(k)), wblk(lambda k: 10 + 2 * chunk(k)),
                  anyspec, anyspec, _vec_spec(), _vec_spec()],
        out_specs=[row, _const_spec((8, D))],
        out_shape=[jax.ShapeDtypeStruct((S, D), F32), jax.ShapeDtypeStruct((8, D), F32)],
        scratch_shapes=[pltpu.VMEM((tm, D), F32), pltpu.VMEM((tm, D), F32), pltpu.SemaphoreType.DMA((2,))],
        compiler_params=_cparams("arbitrary", "arbitrary"),
    )(dqkv, de, w_int, w_int, w_int, w_int, w_int, x, dx1, g_mix, sc1)


def _grad_w(name, a, b):
    S, ka = a.shape
    nb = b.shape[1]

    nsteps = ka // 512
    ring = 3

    def body(a_hbm, b_ref, o_ref, a_buf, sems):
        n = pl.program_id(0)

        def fetch(blk, slot):
            return pltpu.make_async_copy(a_hbm.at[:, pl.ds(pl.multiple_of(blk * 512, 512), 512)], a_buf.at[slot], sems.at[slot])

        @pl.when(n == 0)
        def _():
            for k in range(min(ring, nsteps)):
                fetch(k, k).start()

        slot = n % ring
        fetch(n, slot).wait()
        o_ref[...] = _tn(a_buf[slot], b_ref[...]).astype(BF16)

        @pl.when(n + ring < nsteps)
        def _():
            fetch(n + ring, slot).start()

    return pl.pallas_call(
        body, name=name, grid=(nsteps,),
        in_specs=[pl.BlockSpec(memory_space=pl.ANY), pl.BlockSpec((S, nb), lambda n: (0, 0))],
        out_specs=pl.BlockSpec((512, nb), lambda n: (n, 0)),
        out_shape=jax.ShapeDtypeStruct((ka, nb), BF16),
        scratch_shapes=[pltpu.VMEM((ring, S, 512), BF16), pltpu.SemaphoreType.DMA((ring,))],
        compiler_params=_cparams("arbitrary"),
    )(a, b)


def _grad_w_small(dya, o_bf, cbu, dyc, merged, dmo, after):
    S = dya.shape[0]

    def body(dya_h, o_h, cbu_h, dyc_h, mg_h, dmo_h, after_ref, gba_ref, gbc_ref, gout_ref, a0, a1, b_small, b1, b2, sems):
        fetch = [pltpu.make_async_copy(src, buf, sems.at[k])
                 for k, (src, buf) in enumerate(((dya_h, a0), (o_h, b_small), (cbu_h, a1), (dyc_h, b1), (dmo_h, b2)))]
        for cp in fetch:
            cp.start()
        fetch[0].wait()
        fetch[1].wait()
        gba_ref[...] = _tn(a0[...], b_small[...]).astype(BF16)
        last = pltpu.make_async_copy(mg_h, a0, sems.at[5])
        last.start()
        fetch[2].wait()
        fetch[3].wait()
        gbc_ref[...] = _tn(a1[...], b1[...]).astype(BF16)
        fetch[4].wait()
        last.wait()
        gout_ref[...] = _tn(a0[...], b2[...]).astype(BF16)

    anyspec = pl.BlockSpec(memory_space=pl.ANY)
    vmem = pl.BlockSpec(memory_space=pltpu.VMEM)
    wide = pltpu.VMEM((S, D), BF16)
    return pl.pallas_call(
        body, name="grad_w_small",
        in_specs=[anyspec] * 7, out_specs=[vmem] * 3,
        out_shape=[jax.ShapeDtypeStruct((D, AOW), BF16), jax.ShapeDtypeStruct((D, D), BF16), jax.ShapeDtypeStruct((D, D), BF16)],
        scratch_shapes=[wide, wide, pltpu.VMEM((S, AOW), BF16), wide, wide, pltpu.SemaphoreType.DMA((6,))],
        compiler_params=_cparams(),
    )(dya, o_bf, cbu, dyc, merged, dmo, after)


def _grad_w_in(dqkv, de, h):
    S = h.shape[0]

    def body(dq_ref, de_ref, h_ref, o_ref):
        n = pl.program_id(0)

        @pl.when(n < 9)
        def _():
            o_ref[...] = _tn(dq_ref[0].astype(BF16), h_ref[...]).astype(BF16)

        @pl.when(n >= 9)
        def _():
            o_ref[...] = _tn(de_ref[0], h_ref[...]).astype(BF16)

    def e_idx(n):
        kk = jnp.maximum(n - 9, 0)
        return (kk // 2, 0, kk % 2)

    return pl.pallas_call(
        body, name="grad_w_in", grid=(19,),
        in_specs=[pl.BlockSpec((1, S, 512), lambda n: (jnp.minimum(n, 8), 0, 0)), pl.BlockSpec((1, S, 512), e_idx),
                  pl.BlockSpec((S, D), lambda n: (0, 0))],
        out_specs=pl.BlockSpec((512, D), lambda n: (_win_rowblock(n), 0)),
        out_shape=jax.ShapeDtypeStruct((19 * 512, D), BF16),
        compiler_params=_cparams("parallel"),
    )(dqkv, de, h)


def _local_step(x, h, tgt, mod, g_mix, g_mlp, g_fin, ba, bb, cw8, w_int, mix_weights, mlp_weights, mlp_grads_ready, w_in_grad_ready,
                other_grads_ready):
    S = x.shape[0]
    sh1, sc1, gt1, sh2, sc2, gt2 = [mod[k:k + 1] for k in range(6)]
    bias, bias_t = _bias_table()

    qkv, e = _proj(h, w_int)
    qkv = qkv.reshape(3, 3, S, AOW)
    o_attn, lse = _attn_fwd(qkv, bias)
    w_bat, w_bc, w_out = mix_weights(o_attn)
    o_bf, cbu, ya, yc, merged = _mix(o_attn, e, cw8, ba, bb, w_bat, w_bc)
    x1, mo, h2 = _out_proj(merged, w_out, x, gt1, g_mlp, sc2, sh2)
    w_mit, w_mo = mlp_weights(x1)
    a, f = _mlp_in(h2, w_mit)
    mlp, dx2, pv_f = _mlp_out(f, w_mo, x1, gt2, g_fin, tgt)

    da, dmo2, pv_a = _bwd_mlp_a(dx2, gt2, mlp, w_mo, a)
    dx1, pv_b = _bwd_mlp_b(da, w_mit, x1, dx2, g_mlp, sc2)
    zero = mlp_grads_ready(_grad_w("grad_w_mi", da, h2), _grad_w("grad_w_mo", f, dmo2))
    dmo, dya, dyc, do, dl, de, pv_m = _bwd_mix(dx1, gt1 + zero, mo, e, cw8, ba, bb, ya, yc, o_attn, w_out, w_bc, w_bat)
    dqkv = _attn_bwd(qkv, do, lse, dl, bias_t).reshape(9, S, AOW)
    after = w_in_grad_ready(_grad_w_in(dqkv, de, h))
    zero = other_grads_ready(*_grad_w_small(dya, o_bf, cbu, dyc, merged, dmo, after))
    grad_x, pv_i = _bwd_in(dqkv, de, w_int, x, dx1, g_mix, sc1 + zero)

    vec = jnp.concatenate([pv_i[0:2], pv_m[0:1], pv_b[0:2], pv_a[0:1], pv_i[2:3], pv_b[2:3], pv_f[0:1],
                           pv_m[1:3], pv_m[3:6], pv_f[1:2], jnp.zeros((1, D), F32)], axis=0)
    return grad_x, vec


def _my_place():
    return lax.axis_index("x"), lax.axis_index("y"), lax.axis_index("c")


def _dev_index(px, py, pc):
    return 4 * px + 2 * py + pc


def _peer(x, y, c, m):
    return (x ^ ((m >> 2) & 1), y ^ ((m >> 1) & 1), c ^ (m & 1))


HBM_SPEC = pl.BlockSpec(memory_space=pltpu.HBM)
SEM_SPEC = pl.BlockSpec(memory_space=pltpu.SEMAPHORE)
N_PEER = N_DEV - 1


SPLIT_MASKS = {"gather": tuple(range(1, N_DEV)), "scatter": tuple(range(1, N_DEV)), "chips": (2, 4, 6), "sibling": (1, 1, 1, 1)}


def _split_copy(mode, src_ref, land_ref, send_sems, recv_sems, w, j, place, arriving=False):
    x, y, c = place
    masks = SPLIT_MASKS[mode]
    peer = _peer(x, y, c, masks[j])
    k = w * len(masks) + j
    sender, receiver = ((peer, (x, y, c)) if arriving else ((x, y, c), peer))
    if mode == "gather":
        r = src_ref.shape[0]
        src, dst = src_ref, land_ref.at[pl.ds(pl.multiple_of(_dev_index(*sender) * r, 16), r), :]
    elif mode == "scatter":
        r = land_ref.shape[1]
        src, dst = src_ref.at[pl.ds(pl.multiple_of(_dev_index(*receiver) * r, 16), r), :], land_ref.at[j]
    elif mode == "chips":
        src, dst = src_ref.at[2 * receiver[0] + receiver[1]], land_ref.at[j]
    else:
        r = land_ref.shape[1]
        src, dst = src_ref.at[pl.ds(pl.multiple_of((2 * j + receiver[2]) * r, 16), r), :], land_ref.at[j]
    return pltpu.make_async_remote_copy(src_ref=src, dst_ref=dst, send_sem=send_sems.at[k], recv_sem=recv_sems.at[k],
                                        device_id=peer, device_id_type=MESH)


def _split_start(name, mode, srcs, lands):
    n = len(srcs)
    nm = len(SPLIT_MASKS[mode])

    def body(*refs):
        src, land = refs[:n], refs[n:2 * n]
        send_sems, recv_sems = refs[2 * n], refs[2 * n + 1]
        token = refs[-1]
        place = _my_place()
        for w in range(n):
            for j in range(nm):
                _split_copy(mode, src[w], land[w], send_sems, recv_sems, w, j, place).start()
        token[...] = jnp.zeros_like(token)

    hbm = lambda t: pltpu.HBM(t.shape, t.dtype)
    out = pl.pallas_call(
        body, name=name,
        out_shape=(pltpu.SemaphoreType.DMA((n * nm,)), pltpu.SemaphoreType.DMA((n * nm,)), *[hbm(t) for t in srcs],
                   *[hbm(t) for t in lands], jax.ShapeDtypeStruct((8, 128), F32)),
        in_specs=(HBM_SPEC,) * (2 * n),
        out_specs=(SEM_SPEC, SEM_SPEC) + (HBM_SPEC,) * (2 * n) + (pl.BlockSpec(memory_space=pltpu.VMEM),),
        input_output_aliases={i: 2 + i for i in range(2 * n)},
        compiler_params=pltpu.CompilerParams(has_side_effects=pltpu.SideEffectType.DATAFLOW_SIDE_EFFECTING),
    )(*[pltpu.with_memory_space_constraint(t, pltpu.HBM) for t in (*srcs, *lands)])
    return out[0], out[1], out[2:2 + n], out[2 + n:2 + 2 * n], out[-1][0:1, 0:1], out[-1]


def _split_wait(name, mode, send_sems, recv_sems, srcs, lands, after):
    n = len(srcs)

    def body(*refs):
        src, land = refs[:n], refs[n:2 * n]
        ssem, rsem = refs[2 * n], refs[2 * n + 1]
        place = _my_place()
        for w in range(n):
            for j in range(len(SPLIT_MASKS[mode])):
                _split_copy(mode, src[w], land[w], ssem, rsem, w, j, place).wait_send()
                _split_copy(mode, src[w], land[w], ssem, rsem, w, j, place, arriving=True).wait_recv()

    hbm = lambda t: pltpu.HBM(t.shape, t.dtype)
    out = pl.pallas_call(
        body, name=name,
        out_shape=tuple(hbm(t) for t in (*srcs, *lands)),
        in_specs=(HBM_SPEC,) * (2 * n) + (SEM_SPEC, SEM_SPEC, pl.BlockSpec(memory_space=pl.ANY)),
        out_specs=(HBM_SPEC,) * (2 * n),
        input_output_aliases={i: i for i in range(2 * n)},
        compiler_params=pltpu.CompilerParams(has_side_effects=pltpu.SideEffectType.DATAFLOW_SIDE_EFFECTING),
    )(*srcs, *lands, send_sems, recv_sems, after)
    return out[:n], out[n:]


def _sibling_exchange(grads):
    nw = len(grads)
    HBM = pl.BlockSpec(memory_space=pl.ANY)

    def body(*refs):
        g, land = refs[:nw], refs[nw:2 * nw]
        send_sems, recv_sems = refs[2 * nw:]
        x, y, c = _my_place()

        def copy(w, q, owner_core):
            r = land[w].shape[1]
            return pltpu.make_async_remote_copy(
                src_ref=g[w].at[pl.ds(pl.multiple_of((2 * q + owner_core) * r, 16), r), :], dst_ref=land[w].at[q],
                send_sem=send_sems.at[w, q], recv_sem=recv_sems.at[w, q], device_id=(x, y, 1 - c), device_id_type=MESH)

        sends = [copy(w, q, 1 - c) for w in range(nw) for q in range(4)]
        for cp in sends:
            cp.start()
        for w in range(nw):
            for q in range(4):
                copy(w, q, c).wait_recv()
        for cp in sends:
            cp.wait_send()

    return pl.pallas_call(
        body, name="sibling_exchange",
        out_shape=[jax.ShapeDtypeStruct((4, a.shape[0] // N_DEV, a.shape[1]), a.dtype) for a in grads],
        in_specs=[HBM] * nw, out_specs=[HBM] * nw,
        scratch_shapes=[pltpu.SemaphoreType.DMA((nw, 4)), pltpu.SemaphoreType.DMA((nw, 4))],
    )(*grads)


def _pair_sums(gs, sibs, core):
    n = len(gs)

    def body(core_ref, *refs):
        for w in range(n):
            refs[2 * n + w][0] = (refs[w][0, 0].astype(F32) + refs[n + w][0].astype(F32)).astype(BF16)

    in_specs = [pl.BlockSpec((1, 1) + t.shape[1:], lambda q, core_ref: (q, core_ref[0], 0, 0)) for t in sibs]
    in_specs += [pl.BlockSpec((1,) + t.shape[1:], lambda q, core_ref: (q, 0, 0)) for t in sibs]
    return pl.pallas_call(
        body, name="pair_sums",
        grid_spec=pltpu.PrefetchScalarGridSpec(
            num_scalar_prefetch=1, grid=(4,), in_specs=in_specs,
            out_specs=[pl.BlockSpec((1,) + t.shape[1:], lambda q, core_ref: (q, 0, 0)) for t in sibs]),
        out_shape=[jax.ShapeDtypeStruct(t.shape, BF16) for t in sibs],
        compiler_params=_cparams("parallel"),
    )(core, *[g.reshape(4, 2, t.shape[1], t.shape[2]) for g, t in zip(gs, sibs)], *sibs)


def _own_rows_into_zones(shards, me):
    n = len(shards)

    def body(me_ref, *refs):
        for w in range(n):
            refs[2 * n + w][...] = refs[w][...]

    zones = [lax.empty((N_DEV * t.shape[0], t.shape[1]), t.dtype) for t in shards]
    return pl.pallas_call(
        body, name="own_rows_into_zones",
        grid_spec=pltpu.PrefetchScalarGridSpec(
            num_scalar_prefetch=1, grid=(1,),
            in_specs=[pl.BlockSpec(t.shape, lambda i, me_ref: (0, 0)) for t in shards] + [pl.BlockSpec(memory_space=pl.ANY)] * n,
            out_specs=[pl.BlockSpec(t.shape, lambda i, me_ref: (me_ref[0], 0)) for t in shards]),
        out_shape=[jax.ShapeDtypeStruct(z.shape, z.dtype) for z in zones],
        input_output_aliases={1 + n + w: w for w in range(n)},
        compiler_params=_cparams("arbitrary"),
    )(me, *shards, *zones)


def _allgather_small(v, name):
    r, ccols = v.shape

    def body(v_ref, out_ref, sum_ref, send_sems, recv_sems):
        x, y, c = _my_place()
        my_idx = _dev_index(x, y, c)
        out_ref[my_idx] = v_ref[...]

        def copy(m):
            peer = _peer(x, y, c, m)
            return pltpu.make_async_remote_copy(
                src_ref=v_ref, dst_ref=out_ref.at[my_idx],
                send_sem=send_sems.at[m - 1], recv_sem=recv_sems.at[m - 1], device_id=peer, device_id_type=MESH)

        def arrival(m):
            peer = _peer(x, y, c, m)
            return pltpu.make_async_remote_copy(
                src_ref=v_ref, dst_ref=out_ref.at[_dev_index(*peer)],
                send_sem=send_sems.at[m - 1], recv_sem=recv_sems.at[m - 1], device_id=peer, device_id_type=MESH)

        sends = [copy(m) for m in range(1, N_DEV)]
        for cp in sends:
            cp.start()
        for m in range(1, N_DEV):
            arrival(m).wait_recv()
        acc = out_ref[0]
        for s in range(1, N_DEV):
            acc = acc + out_ref[s]
        sum_ref[...] = acc
        for cp in sends:
            cp.wait_send()

    vmem = pl.BlockSpec(memory_space=pltpu.VMEM)
    return pl.pallas_call(
        body, name=name,
        out_shape=[jax.ShapeDtypeStruct((N_DEV, r, ccols), v.dtype), jax.ShapeDtypeStruct((r, ccols), v.dtype)],
        in_specs=[vmem], out_specs=[vmem, vmem],
        scratch_shapes=[pltpu.SemaphoreType.DMA((7,)), pltpu.SemaphoreType.DMA((7,))],
    )(v)


def _gather_w_in_and_condition(shard, pay, w_ada, b_cols):
    r, ccols = shard.shape
    ncol = w_ada.shape[1]

    def body(sh_ref, pay_ref, w_ref, b_ref, full_ref, got_ref, act_ref, mod_ref, send_sems, recv_sems, small_send, small_recv, local_sem):
        x, y, c = _my_place()
        me, sibling = (x, y, c), (x, y, 1 - c)
        my_idx = _dev_index(x, y, c)
        chips = [(1 - x, y), (x, 1 - y), (1 - x, 1 - y)]

        def small(rnd, buf, m, arriving=False):
            peer = _peer(x, y, c, m)
            slot = _dev_index(*peer) if arriving else my_idx
            return pltpu.make_async_remote_copy(
                src_ref=buf.at[my_idx], dst_ref=buf.at[slot], send_sem=small_send.at[rnd, m - 1],
                recv_sem=small_recv.at[rnd, m - 1], device_id=peer, device_id_type=MESH)

        def rows(px, py, pc):
            return full_ref.at[pl.ds(pl.multiple_of(_dev_index(px, py, pc) * r, 16), r), :]

        def copy(k, block, to, src=None):
            return pltpu.make_async_remote_copy(
                src_ref=rows(*block) if src is None else src, dst_ref=rows(*block),
                send_sem=send_sems.at[k], recv_sem=recv_sems.at[k], device_id=to, device_id_type=MESH)

        got_ref[my_idx] = pay_ref[...]
        round1 = [small(0, got_ref, m) for m in range(1, N_DEV)]
        for cp in round1:
            cp.start()
        mine = pltpu.make_async_copy(sh_ref, rows(*me), local_sem)
        mine.start()
        first = [copy(0, me, sibling, src=sh_ref)] + [copy(1 + j, me, (*chip, c), src=sh_ref) for j, chip in enumerate(chips)]
        for cp in first:
            cp.start()

        for m in range(1, N_DEV):
            small(0, got_ref, m, arriving=True).wait_recv()
        cv = jnp.concatenate([got_ref[s, 0:1, :] for s in range(N_DEV)], axis=0)
        act = cv * _sigmoid(cv)
        act_ref[...] = act
        mod_ref[my_idx] = jnp.dot(act, w_ref[...], preferred_element_type=F32, precision=lax.Precision.HIGHEST) + b_ref[...]
        round2 = [small(1, mod_ref, m) for m in range(1, N_DEV)]
        for cp in round2:
            cp.start()

        passed = []
        for j, chip in enumerate(chips):
            copy(1 + j, (*chip, c), me).wait_recv()
            fwd = copy(4 + j, (*chip, c), sibling)
            fwd.start()
            passed.append(fwd)
        copy(0, sibling, me).wait_recv()
        for j, chip in enumerate(chips):
            copy(4 + j, (*chip, 1 - c), me).wait_recv()
        for m in range(1, N_DEV):
            small(1, mod_ref, m, arriving=True).wait_recv()
        for cp in first + passed + round1 + round2:
            cp.wait_send()
        mine.wait()

    anyspec = pl.BlockSpec(memory_space=pl.ANY)
    vmem = pl.BlockSpec(memory_space=pltpu.VMEM)
    return pl.pallas_call(
        body, name="gather_w_in_and_condition",
        out_shape=[jax.ShapeDtypeStruct((N_DEV * r, ccols), shard.dtype), jax.ShapeDtypeStruct((N_DEV, 8, D), F32),
                   jax.ShapeDtypeStruct((N_DEV, D), F32), jax.ShapeDtypeStruct((N_DEV, N_DEV, ncol), F32)],
        in_specs=[anyspec, vmem, vmem, vmem], out_specs=[anyspec, vmem, vmem, vmem],
        scratch_shapes=[pltpu.SemaphoreType.DMA((7,)), pltpu.SemaphoreType.DMA((7,)), pltpu.SemaphoreType.DMA((2, 7)),
                        pltpu.SemaphoreType.DMA((2, 7)), pltpu.SemaphoreType.DMA],
        compiler_params=_cparams(),
    )(shard, pay, w_ada, b_cols)


def _row_tile(r):
    for t in (256, 304, 128, 64, 16):
        if r % t == 0:
            return t
    return r


def _adamw_w_ada(w, act_t, gm_cols, m, v):
    r, ccols = w.shape
    tr = _row_tile(r)
    c1 = 1.0 / (1.0 - B1 ** STEP)
    c2 = 1.0 / (1.0 - B2 ** STEP)

    def body(w_ref, a_ref, gm_ref, m_ref, v_ref, g_ref, d_ref, nm_ref, nv_ref):
        gv = jnp.dot(a_ref[...], gm_ref[...], preferred_element_type=F32, precision=lax.Precision.HIGHEST)
        g_ref[...] = gv
        nm = B1 * m_ref[...] + (1.0 - B1) * gv
        nv = B2 * v_ref[...] + (1.0 - B2) * jnp.square(gv)
        nm_ref[...] = nm
        nv_ref[...] = nv
        d_ref[...] = -LR * ((nm * c1) / (jnp.sqrt(nv * c2) + ADAM_EPS) + WD * w_ref[...])

    blk = pl.BlockSpec((tr, ccols), lambda i: (i, 0))
    return pl.pallas_call(
        body, name="adamw_w_ada", grid=(r // tr,),
        in_specs=[blk, pl.BlockSpec((tr, N_DEV), lambda i: (i, 0)), _const_spec(gm_cols.shape), blk, blk], out_specs=[blk] * 4,
        out_shape=[jax.ShapeDtypeStruct((r, ccols), F32)] * 4,
        compiler_params=_cparams("parallel"),
    )(w, act_t, gm_cols, m, v)


def _sum_adamw(parts, own, slot, w, m, v, name, transposed=False):
    k, r, ccols = parts.shape
    tr = _row_tile(r)
    c1 = 1.0 / (1.0 - B1 ** STEP)
    c2 = 1.0 / (1.0 - B2 ** STEP)

    def body(s_ref, p_ref, own_ref, w_ref, m_ref, v_ref, g_ref, d_ref, nm_ref, nv_ref):
        gv = own_ref[0].astype(F32)
        for s in range(k):
            gv = gv + p_ref[s].astype(F32)
        if transposed:
            gv = gv.T
        g_ref[...] = gv
        nm = B1 * m_ref[...] + (1.0 - B1) * gv
        nv = B2 * v_ref[...] + (1.0 - B2) * jnp.square(gv)
        nm_ref[...] = nm
        nv_ref[...] = nv
        d_ref[...] = -LR * ((nm * c1) / (jnp.sqrt(nv * c2) + ADAM_EPS) + WD * w_ref[...])

    if transposed:
        blk = pl.BlockSpec((ccols, tr), lambda i, s_ref: (0, i))
    else:
        blk = pl.BlockSpec((tr, ccols), lambda i, s_ref: (i, 0))
    return pl.pallas_call(
        body, name=name,
        grid_spec=pltpu.PrefetchScalarGridSpec(
            num_scalar_prefetch=1, grid=(r // tr,),
            in_specs=[pl.BlockSpec((k, tr, ccols), lambda i, s_ref: (0, i, 0)),
                      pl.BlockSpec((1, tr, ccols), lambda i, s_ref: (s_ref[0], i, 0))] + [blk] * 3,
            out_specs=[blk] * 4),
        out_shape=[jax.ShapeDtypeStruct(w.shape, F32)] * 4,
        compiler_params=_cparams("parallel"),
    )(slot, parts, own, w, m, v)


VEC_ROWS = ((0, 6), (6, 7), (9, 11), (11, 14), (7, 8), (8, 9))


def _adamw_vectors(w, g, m, v):
    c1 = 1.0 / (1.0 - B1 ** STEP)
    c2 = 1.0 / (1.0 - B2 ** STEP)

    def put(refs, p):
        for ref, (lo, hi) in zip(refs, VEC_ROWS):
            if ref.shape == (3, HEAD):
                ref[...] = p[lo:hi, :HEAD]
            else:
                ref[...] = jnp.concatenate([p[k:k + 1] for k in range(lo, hi)], axis=1)

    def body(w_ref, g_ref, m_ref, v_ref, *outs):
        gv = g_ref[...]
        nm = B1 * m_ref[...] + (1.0 - B1) * gv
        nv = B2 * v_ref[...] + (1.0 - B2) * jnp.square(gv)
        delta = -LR * ((nm * c1) / (jnp.sqrt(nv * c2) + ADAM_EPS) + WD * w_ref[...])
        for kind, p in enumerate((gv, delta, nm, nv)):
            put(outs[6 * kind:6 * kind + 6], p)

    shapes = [(1, 6 * D), (1, D), (1, 2 * D), (3, HEAD), (1, D), (1, D)]
    out = pl.pallas_call(
        body, name="adamw_vectors", out_shape=[jax.ShapeDtypeStruct(sh, F32) for sh in shapes] * 4, compiler_params=_cparams(),
    )(w, g, m, v)
    fix = lambda t: (t[0], t[1], t[2], t[3][None], t[4], t[5].reshape(D))
    return [fix(out[6 * kind:6 * kind + 6]) for kind in range(4)]


def _pack_vectors(b_ada, g_mix, g_mlp, g_fin, b_gate, conv_w):
    conv_rows = jnp.pad(conv_w.reshape(3, HEAD), ((0, 0), (0, D - HEAD)))
    return jnp.concatenate([b_ada.reshape(6, D), g_mix.reshape(1, D), g_mlp.reshape(1, D), g_fin.reshape(1, D),
                            b_gate.reshape(2, D), conv_rows, jnp.zeros((2, D), F32)], axis=0)


def kernel(x, c, w_ada, b_ada, g_norm_mix, w_in, b_gate, conv_w, w_branch_attn, w_branch_conv, w_out, g_norm_mlp, w_mlp_in, w_mlp_out, g_norm_final, loss_target, m_w_ada, m_b_ada, m_g_norm_mix, m_w_in, m_b_gate, m_conv_w, m_w_branch_attn, m_w_branch_conv, m_w_out, m_g_norm_mlp, m_w_mlp_in, m_w_mlp_out, m_g_norm_final, v_w_ada, v_b_ada, v_g_norm_mix, v_w_in, v_b_gate, v_conv_w, v_w_branch_attn, v_w_branch_conv, v_w_out, v_g_norm_mlp, v_w_mlp_in, v_w_mlp_out, v_g_norm_final):
    S = x.shape[1]
    xi, yi, ci = _my_place()
    me = _dev_index(xi, yi, ci)
    x2 = x.reshape(S, D)
    tgt = loss_target.reshape(S, D)

    pay = jnp.zeros((8, D), F32).at[0].set(c[0]).at[1:4, :HEAD].set(conv_w[0])
    ncol = w_ada.shape[2]
    b_cols = lax.dynamic_slice(b_ada, (0, me * ncol), (1, ncol))
    w_int, got, act, mod_all = _gather_w_in_and_condition(w_in[0].T.astype(BF16), pay, w_ada[0], b_cols)
    cw8 = jnp.pad(got[:, 1:4, :HEAD].transpose(1, 0, 2).reshape(3, D), ((0, 5), (0, 0)))
    mod = lax.dynamic_index_in_dim(mod_all, me, axis=1, keepdims=False).reshape(6, D)
    late = [w_branch_attn[0].T.astype(BF16), w_branch_conv[0].astype(BF16), w_out[0].astype(BF16),
            w_mlp_in[0].T.astype(BF16), w_mlp_out[0].astype(BF16)]
    w_int, late = lax.optimization_barrier((w_int, late))
    zones = _own_rows_into_zones(late, me.reshape(1).astype(jnp.int32))
    ag_mix = _split_start("gather_mix_start", "gather", late[:3], zones[:3])
    ag_mlp = _split_start("gather_mlp_start", "gather", late[3:], zones[3:])
    mod = mod + ag_mix[4] + ag_mlp[4]
    h = _prenorm(x2, g_norm_mix, mod[1:2], mod[0:1])

    def mix_weights(o_attn):
        return _split_wait("gather_mix_wait", "gather", *ag_mix[:4], o_attn)[1]

    def mlp_weights(x1):
        return _split_wait("gather_mlp_wait", "gather", *ag_mlp[:4], x1)[1]

    rs = {}

    def mlp_grads_ready(*grads):
        lands = [lax.empty((N_PEER, t.shape[0] // N_DEV, t.shape[1]), BF16) for t in grads]
        rs["mlp"] = _split_start("scatter_mlp_start", "scatter", grads, lands)
        return rs["mlp"][4]

    def w_in_grad_ready(g_in):
        r = g_in.shape[0] // N_DEV
        rs["sib"] = _split_start("sibling_w_in_start", "sibling", [g_in], [lax.empty((4, r, g_in.shape[1]), BF16)])
        return rs["sib"][5]

    def other_grads_ready(*small):
        core = ci.reshape(1).astype(jnp.int32)
        (g_in,), (sib_in,) = _split_wait("sibling_w_in_wait", "sibling", *rs["sib"][:4], small[0])
        pair = _pair_sums([g_in, *small], [sib_in, *_sibling_exchange(small)], core)
        lands = [lax.empty((3,) + t.shape[1:], BF16) for t in pair]
        rs["rest"] = _split_start("scatter_rest_start", "chips", pair, lands)
        return rs["rest"][4]

    ba, bb = b_gate[:, :D], b_gate[:, D:]
    grad_x, vec = _local_step(
        x2, h, tgt, mod, g_norm_mix, g_norm_mlp, g_norm_final.reshape(1, D), ba, bb, cw8, w_int, mix_weights, mlp_weights,
        mlp_grads_ready, w_in_grad_ready, other_grads_ready)

    vec_all, vec_sum = _allgather_small(vec, "gather_vec")
    loss = vec_sum[14, 0]
    gm_all = vec_all[:, 0:6, :].reshape(N_DEV, 6 * D)
    gm_cols = lax.dynamic_slice(gm_all, (0, me * ncol), (N_DEV, ncol))
    conv_cols = lax.dynamic_slice(vec_sum[11:14], (0, me * HEAD), (3, HEAD))
    g_pack = jnp.concatenate([vec_sum[0:11], jnp.pad(conv_cols, ((0, 0), (0, D - HEAD))), jnp.zeros((2, D), F32)], axis=0)
    packs = [_pack_vectors(*t) for t in ((b_ada, g_norm_mix, g_norm_mlp, g_norm_final, b_gate, conv_w),
                                         (m_b_ada, m_g_norm_mix, m_g_norm_mlp, m_g_norm_final, m_b_gate, m_conv_w),
                                         (v_b_ada, v_g_norm_mix, v_g_norm_mlp, v_g_norm_final, v_b_gate, v_conv_w))]
    gv, dv, mv, vv = _adamw_vectors(packs[0], g_pack, packs[1], packs[2])
    g_w_ada, d_ada, nm_ada, nv_ada = _adamw_w_ada(w_ada[0], act.T, gm_cols, m_w_ada[0], v_w_ada[0])

    big = {}
    srcs, lands = _split_wait("scatter_mlp_wait", "scatter", *rs["mlp"][:4], d_ada)
    own = [g.reshape((N_DEV,) + land.shape[1:]) for g, land in zip(srcs, lands)]
    slot = me.reshape(1).astype(jnp.int32)
    big["w_mi"] = tuple(t[None] for t in _sum_adamw(lands[0], own[0], slot, w_mlp_in[0], m_w_mlp_in[0], v_w_mlp_in[0], "adamw_w_mi",
                                                    transposed=True))
    big["w_mo"] = tuple(t[None] for t in _sum_adamw(lands[1], own[1], slot, w_mlp_out[0], m_w_mlp_out[0], v_w_mlp_out[0], "adamw_w_mo"))
    own, lands = _split_wait("scatter_rest_wait", "chips", *rs["rest"][:4], big["w_mo"][1])
    slot = (2 * xi + yi).reshape(1).astype(jnp.int32)
    big["w_in"] = tuple(t.T[None] for t in _sum_adamw(lands[0], own[0], slot, w_in[0].T, m_w_in[0].T, v_w_in[0].T, "adamw_w_in"))
    big["w_ba"] = tuple(t[None] for t in _sum_adamw(lands[1], own[1], slot, w_branch_attn[0], m_w_branch_attn[0], v_w_branch_attn[0],
                                                    "adamw_w_ba", transposed=True))
    big["w_bc"] = tuple(t[None] for t in _sum_adamw(lands[2], own[2], slot, w_branch_conv[0], m_w_branch_conv[0], v_w_branch_conv[0], "adamw_w_bc"))
    big["w_out"] = tuple(t[None] for t in _sum_adamw(lands[3], own[3], slot, w_out[0], m_w_out[0], v_w_out[0], "adamw_w_out"))

    def ordered(k, ada, vecs):
        return (ada[None], vecs[0], vecs[1], big["w_in"][k], vecs[2], vecs[3], big["w_ba"][k], big["w_bc"][k],
                big["w_out"][k], vecs[4], big["w_mi"][k], big["w_mo"][k], vecs[5])

    return (loss, grad_x.reshape(1, S, D), *ordered(0, g_w_ada, gv), *ordered(1, d_ada, dv),
            *ordered(2, nm_ada, mv), *ordered(3, nv_ada, vv))
```

```python
import numpy as np
import jax
import jax.numpy as jnp
from jax import lax
from jax.experimental import pallas as pl
from jax.experimental.pallas import tpu as pltpu

F32, BF16 = jnp.float32, jnp.bfloat16
D = 1024
HEAD = 128
DILATIONS = (1, 4, 16)
N_SLOT = 4
AOW = N_SLOT * HEAD
DFF = 4 * D
N_DEV = 8
UNROLL = 16
UNROLL_FWD = 32
EPS = 1e-6
NEG = -1e30
SCALE = HEAD ** -0.5
LR, B1, B2, ADAM_EPS, WD, STEP = 0.001, 0.9, 0.999, 1e-08, 0.01, 10
V7X_VMEM_LIMIT = 56 * 1024 * 1024
TM = 1024
MESH = pl.DeviceIdType.MESH


def _cparams(*sem):
    if sem:
        return pltpu.CompilerParams(dimension_semantics=sem, vmem_limit_bytes=V7X_VMEM_LIMIT)
    return pltpu.CompilerParams(vmem_limit_bytes=V7X_VMEM_LIMIT)


def _nn(a, b):
    return jnp.dot(a, b, preferred_element_type=F32)


def _nt(a, b):
    return lax.dot_general(a, b, (((1,), (1,)), ((), ())), preferred_element_type=F32)


def _tn(a, b):
    return lax.dot_general(a, b, (((0,), (0,)), ((), ())), preferred_element_type=F32)


def _rms_r(x):
    return lax.rsqrt(jnp.mean(x * x, axis=-1, keepdims=True) + EPS)


def _rms_bwd(x, r, g, dn):
    gy = dn * g
    dx = r * gy - x * (r * r * r) * jnp.mean(x * gy, axis=-1, keepdims=True)
    return dx, dn * (x * r)


def _sigmoid(t):
    return 1.0 / (1.0 + jnp.exp(-t))


def _rowsum(v):
    return jnp.sum(v, axis=0, keepdims=True)


def _vec_spec(n=D):
    return pl.BlockSpec((1, n), lambda *_: (0, 0))


def _const_spec(shape):
    nd = len(shape)
    return pl.BlockSpec(shape, lambda *_: (0,) * nd)


def _win_rowblock(j):
    return jnp.where(j < 9, (j % 3) * 3 + j // 3, j)


def _prenorm(x, g, sc, sh):
    S = x.shape[0]
    tm = TM

    def body(x_ref, g_ref, sc_ref, sh_ref, h_ref):
        xv = x_ref[...]
        h_ref[...] = (xv * _rms_r(xv) * g_ref[...] * (1.0 + sc_ref[...]) + sh_ref[...]).astype(BF16)

    row = pl.BlockSpec((tm, D), lambda i: (i, 0))
    return pl.pallas_call(
        body, name="prenorm", grid=(S // tm,), in_specs=[row, _vec_spec(), _vec_spec(), _vec_spec()], out_specs=row,
        out_shape=jax.ShapeDtypeStruct((S, D), BF16), compiler_params=_cparams("parallel"),
    )(x, g, sc, sh)


def _proj(h, w_int):
    S = h.shape[0]

    def body(h_ref, w_ref, q_ref, e_ref):
        j = pl.program_id(0)
        acc = _nt(h_ref[...], w_ref[...])

        @pl.when(j < 9)
        def _():
            q_ref[0] = acc

        @pl.when(j >= 9)
        def _():
            e_ref[0] = acc.astype(BF16)

    def e_idx(j):
        k = jnp.maximum(j - 9, 0)
        return (k // 2, 0, k % 2)

    return pl.pallas_call(
        body, name="proj", grid=(19,),
        in_specs=[pl.BlockSpec((S, D), lambda j: (0, 0), pipeline_mode=pl.Buffered(1)),
                  pl.BlockSpec((512, D), lambda j: (_win_rowblock(j), 0))],
        out_specs=[pl.BlockSpec((1, S, 512), lambda j: (jnp.minimum(j, 8), 0, 0)), pl.BlockSpec((1, S, 512), e_idx)],
        out_shape=[jax.ShapeDtypeStruct((9, S, 512), F32), jax.ShapeDtypeStruct((5, S, D), BF16)],
        compiler_params=_cparams("arbitrary"),
    )(h, w_int)


def _bias_table():
    slopes = (2.0 ** (-8.0 * np.arange(1, 13, dtype=np.float32) / 12.0)).astype(np.float32)
    qi = np.arange(HEAD)[:, None]
    kj = np.arange(2 * HEAD)[None, :]
    delta = HEAD + qi - kj
    mask = (delta >= 0) & (delta <= HEAD)
    out = np.zeros((3, N_SLOT, HEAD, 2 * HEAD), np.float32)
    for gi, d in enumerate(DILATIONS):
        for j in range(N_SLOT):
            bias = -slopes[gi * N_SLOT + j] * (delta * d).astype(np.float32)
            out[gi, j] = np.where(mask, bias, NEG)
    out_t = np.concatenate([out[..., HEAD:].swapaxes(-1, -2), out[..., :HEAD].swapaxes(-1, -2)], axis=-1)
    return jnp.asarray(out), jnp.asarray(out_t)


def _attn_fwd(qkv, bias):
    S = qkv.shape[2]
    nblk = S // HEAD
    rows = 256

    def body(qkv_ref, b_ref, o_ref, lse_ref, o_s, lse_s):
        g = pl.program_id(1)
        bias = b_ref[0, 0]
        col = lax.broadcasted_iota(jnp.int32, bias.shape, 1)
        bias_first = jnp.where(col < HEAD, NEG, bias)

        for gi, d in enumerate(DILATIONS):
            @pl.when(g == gi)
            def _(gi=gi, d=d):
                nb = nblk // d

                def keys(start):
                    sl = pl.ds(start, HEAD, stride=d)
                    return qkv_ref.at[0, 1][sl, :].astype(BF16), qkv_ref.at[0, 2][sl, :].astype(BF16)

                def step(b, first_of_residue, before):
                    r, n = b // nb, b % nb
                    cur = pl.ds(n * (HEAD * d) + r, HEAD, stride=d)
                    own = keys(n * (HEAD * d) + r)
                    if first_of_residue:
                        before = own
                    q = qkv_ref.at[0, 0][cur, :].astype(BF16)
                    kw = jnp.concatenate([before[0], own[0]], axis=0)
                    vw = jnp.concatenate([before[1], own[1]], axis=0)
                    s = _nt(q, kw) * SCALE + jnp.where(n > 0, bias, bias_first)
                    m = jnp.max(s, axis=-1, keepdims=True)
                    p = jnp.exp(s - m)
                    l = jnp.sum(p, axis=-1, keepdims=True)
                    o_s.at[gi][cur, :] = _nn(p.astype(BF16), vw) / l
                    lse_s.at[gi][cur, :] = jnp.broadcast_to(m + jnp.log(l), (HEAD, HEAD))
                    return own

                def steps(i, before):
                    for u in range(UNROLL_FWD):
                        before = step(i * UNROLL_FWD + u, nb <= UNROLL_FWD and u % nb == 0, before)
                    return before

                lax.fori_loop(0, nblk // UNROLL_FWD, steps, keys(0))

        @pl.when(g == len(DILATIONS) - 1)
        def _():
            def merge(i, carry):
                r = pl.ds(pl.multiple_of(i * rows, rows), rows)
                ls = [lse_s[k, r, :] for k in range(3)]
                top = jnp.maximum(jnp.maximum(ls[0], ls[1]), ls[2])
                ws = [jnp.exp(t - top) for t in ls]
                den = ws[0] + ws[1] + ws[2]
                o_ref[r, :] = (ws[0] * o_s[0, r, :] + ws[1] * o_s[1, r, :] + ws[2] * o_s[2, r, :]) / den
                lse_ref[r, :] = top + jnp.log(den)
                return carry

            lax.fori_loop(0, S // rows, merge, 0)

    return pl.pallas_call(
        body, name="attn_fwd", grid=(N_SLOT, 3),
        in_specs=[pl.BlockSpec((1, 3, S, HEAD), lambda j, g: (g, 0, 0, j)),
                  pl.BlockSpec((1, 1, HEAD, 2 * HEAD), lambda j, g: (g, j, 0, 0))],
        out_specs=[pl.BlockSpec((S, HEAD), lambda j, g: (0, j)), pl.BlockSpec((S, HEAD), lambda j, g: (0, j))],
        out_shape=[jax.ShapeDtypeStruct((S, AOW), F32), jax.ShapeDtypeStruct((S, AOW), F32)],
        scratch_shapes=[pltpu.VMEM((3, S, HEAD), F32)] * 2,
        compiler_params=_cparams("parallel", "arbitrary"),
    )(qkv, bias)


def _shift_down(z, k, halo_rows):
    out = pltpu.roll(z, k, axis=0)
    top = out[:8]
    rid = lax.broadcasted_iota(jnp.int32, top.shape, 0)
    for t in range(k):
        top = jnp.where(rid == t, halo_rows[t], top)
    return jnp.concatenate([top, out[8:]], axis=0)


def _shift_up(z, k, halo_rows):
    n = z.shape[0]
    out = pltpu.roll(z, n - k, axis=0)
    bottom = out[n - 8:]
    rid = lax.broadcasted_iota(jnp.int32, bottom.shape, 0)
    for t in range(k):
        bottom = jnp.where(rid == 8 - k + t, halo_rows[t], bottom)
    return jnp.concatenate([out[:n - 8], bottom], axis=0)


def _e_spec(chunk, tm):
    return pl.BlockSpec((1, tm, D), lambda i, c=chunk: (c, i, 0))


def _e_prev_spec(chunk, tm):
    return pl.BlockSpec((1, 16, D), lambda i, c=chunk: (c, jnp.maximum(i * (tm // 16) - 1, 0), 0))


def _e_next_spec(chunk, tm, S):
    return pl.BlockSpec((1, 16, D), lambda i, c=chunk: (c, jnp.minimum((i + 1) * (tm // 16), S // 16 - 1), 0))


def _mix(o_attn, e, cw8, ba, bb, w_bat, w_bc):
    S = o_attn.shape[0]
    tm = 512

    def body(o_ref, cb_ref, cc_ref, cx_ref, ga_ref, gb_ref, ccp_ref, cxp_ref, cw_ref, ba_ref, bb_ref, wba_ref, wbc_ref,
             obf_ref, cbu_ref, ya_ref, yc_ref, mg_ref):
        i = pl.program_id(0)
        o = o_ref[...].astype(BF16)
        obf_ref[...] = o
        ya = _nt(o, wba_ref[...])
        z = cc_ref[0].astype(F32) * cx_ref[0].astype(F32)
        zp = ccp_ref[0].astype(F32) * cxp_ref[0].astype(F32) * (i > 0).astype(F32)
        z1 = _shift_down(z, 1, [zp[15:16]])
        z2 = _shift_down(z, 2, [zp[14:15], zp[15:16]])
        cw = cw_ref[...]
        u = cw[0:1] * z2 + cw[1:2] * z1 + cw[2:3] * z
        cbu = (cb_ref[0].astype(F32) * u).astype(BF16)
        cbu_ref[...] = cbu
        yc = _nn(cbu, wbc_ref[...])
        sa = _sigmoid(ga_ref[0].astype(F32) + ba_ref[...])
        sb = _sigmoid(gb_ref[0].astype(F32) + bb_ref[...])
        ya_ref[...] = ya.astype(BF16)
        yc_ref[...] = yc.astype(BF16)
        mg_ref[...] = (sa * ya + sb * yc).astype(BF16)

    row = lambda w: pl.BlockSpec((tm, w), lambda i: (i, 0))
    return pl.pallas_call(
        body, name="mix", grid=(S // tm,),
        in_specs=[row(AOW)] + [_e_spec(c, tm) for c in range(5)] + [_e_prev_spec(1, tm), _e_prev_spec(2, tm),
                  _const_spec((8, D)), _vec_spec(), _vec_spec(), _const_spec((D, AOW)), _const_spec((D, D))],
        out_specs=[row(AOW), row(D), row(D), row(D), row(D)],
        out_shape=[jax.ShapeDtypeStruct((S, AOW), BF16)] + [jax.ShapeDtypeStruct((S, D), BF16)] * 4,
        compiler_params=_cparams("parallel"),
    )(o_attn, e, e, e, e, e, e, e, cw8, ba, bb, w_bat, w_bc)


def _out_proj(merged, w_out, x, gate1, g_mlp, sc2, sh2):
    S = x.shape[0]
    tm = TM

    def body(mg_ref, w_ref, x_ref, gt_ref, g_ref, sc_ref, sh_ref, x1_ref, mo_ref, h2_ref):
        mo = _nn(mg_ref[...], w_ref[...])
        mo_ref[...] = mo.astype(BF16)
        x1 = x_ref[...] + gt_ref[...] * mo
        x1_ref[...] = x1
        h2 = x1 * _rms_r(x1) * g_ref[...] * (1.0 + sc_ref[...]) + sh_ref[...]
        h2_ref[...] = h2.astype(BF16)

    row = pl.BlockSpec((tm, D), lambda i: (i, 0))
    return pl.pallas_call(
        body, name="out_proj", grid=(S // tm,),
        in_specs=[row, _const_spec((D, D)), row, _vec_spec(), _vec_spec(), _vec_spec(), _vec_spec()],
        out_specs=[row, row, row],
        out_shape=[jax.ShapeDtypeStruct((S, D), F32), jax.ShapeDtypeStruct((S, D), BF16), jax.ShapeDtypeStruct((S, D), BF16)],
        compiler_params=_cparams("parallel"),
    )(merged, w_out, x, gate1, g_mlp, sc2, sh2)


def _mlp_in(h2, w_mit):
    S = h2.shape[0]
    tm = 512

    def body(h_ref, w_ref, a_ref, f_ref):
        a = _nt(h_ref[...], w_ref[...])
        a_ref[...] = a.astype(BF16)
        f_ref[...] = jnp.square(jnp.maximum(a, 0.0)).astype(BF16)

    blk = pl.BlockSpec((tm, DFF), lambda i: (i, 0))
    return pl.pallas_call(
        body, name="mlp_in", grid=(S // tm,),
        in_specs=[pl.BlockSpec((tm, D), lambda i: (i, 0)), _const_spec((DFF, D))],
        out_specs=[blk, blk],
        out_shape=[jax.ShapeDtypeStruct((S, DFF), BF16)] * 2,
        compiler_params=_cparams("parallel"),
    )(h2, w_mit)


def _mlp_out(f, w_mo, x1, gate2, g_fin, tgt):
    S = x1.shape[0]
    tm = 512
    half = tm // 2

    def body(f_ref, w_ref, x1_ref, gt_ref, g_ref, t_ref, mlp_ref, dx2_ref, pv_ref):
        @pl.when(pl.program_id(0) == 0)
        def _():
            pv_ref[...] = jnp.zeros_like(pv_ref)

        g = g_ref[...]
        for hs in (pl.ds(0, half), pl.ds(half, half)):
            mlp = _nn(f_ref[hs, :], w_ref[...])
            mlp_ref[hs, :] = mlp.astype(BF16)
            x2 = x1_ref[hs, :] + gt_ref[...] * mlp
            r = _rms_r(x2)
            err = x2 * r * g - t_ref[hs, :]
            dx2, pg = _rms_bwd(x2, r, g, err * (1.0 / D))
            dx2_ref[hs, :] = dx2
            pv_ref[0:1, :] += _rowsum(pg)
            pv_ref[1:2, :] += 0.5 * _rowsum(jnp.mean(err * err, axis=-1, keepdims=True))

    row = pl.BlockSpec((tm, D), lambda i: (i, 0))
    return pl.pallas_call(
        body, name="mlp_out", grid=(S // tm,),
        in_specs=[pl.BlockSpec((tm, DFF), lambda i: (i, 0)), _const_spec((DFF, D)), row, _vec_spec(), _vec_spec(), row],
        out_specs=[row, row, _const_spec((8, D))],
        out_shape=[jax.ShapeDtypeStruct((S, D), BF16), jax.ShapeDtypeStruct((S, D), F32), jax.ShapeDtypeStruct((8, D), F32)],
        compiler_params=_cparams("arbitrary"),
    )(f, w_mo, x1, gate2, g_fin, tgt)


def _bwd_mlp_a(dx2, gate2, mlp, w_mo, a):
    S = dx2.shape[0]
    tm = 512
    half = tm // 2

    def body(dx_ref, gt_ref, mlp_ref, w_ref, a_ref, da_ref, dmo_ref, pv_ref):
        @pl.when(pl.program_id(0) == 0)
        def _():
            pv_ref[...] = jnp.zeros_like(pv_ref)

        for hs in (pl.ds(0, half), pl.ds(half, half)):
            dx = dx_ref[hs, :]
            dmo = (dx * gt_ref[...]).astype(BF16)
            dmo_ref[hs, :] = dmo
            pv_ref[0:1, :] += _rowsum(dx * mlp_ref[hs, :].astype(F32))
            df = _nt(dmo, w_ref[...])
            da_ref[hs, :] = (df * (2.0 * jnp.maximum(a_ref[hs, :].astype(F32), 0.0))).astype(BF16)

    row = pl.BlockSpec((tm, D), lambda i: (i, 0))
    wide = pl.BlockSpec((tm, DFF), lambda i: (i, 0))
    return pl.pallas_call(
        body, name="bwd_mlp_a", grid=(S // tm,),
        in_specs=[row, _vec_spec(), row, _const_spec((DFF, D)), wide],
        out_specs=[wide, row, _const_spec((8, D))],
        out_shape=[jax.ShapeDtypeStruct((S, DFF), BF16), jax.ShapeDtypeStruct((S, D), BF16), jax.ShapeDtypeStruct((8, D), F32)],
        compiler_params=_cparams("arbitrary"),
    )(dx2, gate2, mlp, w_mo, a)


def _bwd_mlp_b(da, w_mit, x1, dx2, g_mlp, sc2):
    S = x1.shape[0]
    tm = 512
    half = tm // 2

    def body(da_ref, w_ref, x1_ref, dx2_ref, g_ref, sc_ref, dx1_ref, pv_ref):
        @pl.when(pl.program_id(0) == 0)
        def _():
            pv_ref[...] = jnp.zeros_like(pv_ref)

        g = g_ref[...]
        for hs in (pl.ds(0, half), pl.ds(half, half)):
            dh = _nn(da_ref[hs, :], w_ref[...])
            x1 = x1_ref[hs, :]
            r = _rms_r(x1)
            dxn, pg = _rms_bwd(x1, r, g, dh * (1.0 + sc_ref[...]))
            dx1_ref[hs, :] = dx2_ref[hs, :] + dxn
            pv_ref[0:1, :] += _rowsum(dh)
            pv_ref[1:2, :] += _rowsum(dh * (x1 * r * g))
            pv_ref[2:3, :] += _rowsum(pg)

    row = pl.BlockSpec((tm, D), lambda i: (i, 0))
    return pl.pallas_call(
        body, name="bwd_mlp_b", grid=(S // tm,),
        in_specs=[pl.BlockSpec((tm, DFF), lambda i: (i, 0)), _const_spec((DFF, D)), row, row, _vec_spec(), _vec_spec()],
        out_specs=[row, _const_spec((8, D))],
        out_shape=[jax.ShapeDtypeStruct((S, D), F32), jax.ShapeDtypeStruct((8, D), F32)],
        compiler_params=_cparams("arbitrary"),
    )(da, w_mit, x1, dx2, g_mlp, sc2)


def _bwd_mix(dx1, gate1, mo, e, cw8, ba, bb, ya, yc, o_attn, w_out, w_bc, w_bat):
    S = dx1.shape[0]
    tm = 256
    n_tiles = S // tm

    def body(dx_ref, dxn_ref, gt_ref, mo_ref, cb_ref, cc_ref, cx_ref, ga_ref, gb_ref, cbn_ref, gbn_ref, ccp_ref, cxp_ref,
             cw_ref, ba_ref, bb_ref, ya_ref, yc_ref, o_ref, wout_ref, wbc_ref, wba_ref,
             dmo_ref, dya_ref, dyc_ref, do_ref, dl_ref, de_ref, pv_ref):
        i = pl.program_id(0)

        @pl.when(i == 0)
        def _():
            pv_ref[...] = jnp.zeros_like(pv_ref)

        dx = dx_ref[...]
        cb = cb_ref[0].astype(F32)
        cc = cc_ref[0].astype(F32)
        cx = cx_ref[0].astype(F32)
        dmo_all = (jnp.concatenate([dx, dxn_ref[...]], axis=0) * gt_ref[...]).astype(BF16)
        dmg_all = _nt(dmo_all, wout_ref[...])
        sb_all = _sigmoid(jnp.concatenate([gb_ref[0], gbn_ref[0]], axis=0).astype(F32) + bb_ref[...])
        dyc_all = dmg_all * sb_all
        dcbu_all = _nt(dyc_all.astype(BF16), wbc_ref[...])
        dmo, dmg, sb, dyc, dcbu = dmo_all[:tm], dmg_all[:tm], sb_all[:tm], dyc_all[:tm], dcbu_all[:tm]
        dmo_ref[...] = dmo
        pv_ref[0:1, :] += _rowsum(dx * mo_ref[...].astype(F32))
        sa = _sigmoid(ga_ref[0].astype(F32) + ba_ref[...])
        dya = (dmg * sa).astype(BF16)
        dya_ref[...] = dya
        dyc_ref[...] = dyc.astype(BF16)
        dga = dmg * ya_ref[...].astype(F32) * sa * (1.0 - sa)
        dgb = dmg * yc_ref[...].astype(F32) * sb * (1.0 - sb)
        pv_ref[1:2, :] += _rowsum(dga)
        pv_ref[2:3, :] += _rowsum(dgb)

        do = _nn(dya, wba_ref[...])
        do_ref[...] = do
        prod = do * o_ref[...]
        dl_ref[...] = jnp.concatenate(
            [jnp.broadcast_to(jnp.sum(prod[:, s * HEAD:(s + 1) * HEAD], axis=-1, keepdims=True), (tm, HEAD))
             for s in range(N_SLOT)], axis=1)

        z = cc * cx
        zp = ccp_ref[0].astype(F32) * cxp_ref[0].astype(F32) * (i > 0).astype(F32)
        z1 = _shift_down(z, 1, [zp[15:16]])
        z2 = _shift_down(z, 2, [zp[14:15], zp[15:16]])
        cw = cw_ref[...]
        u = cw[0:1] * z2 + cw[1:2] * z1 + cw[2:3] * z
        du = dcbu * cb
        du_n = dcbu_all[tm:] * cbn_ref[0].astype(F32) * (i < n_tiles - 1).astype(F32)
        du1 = _shift_up(du, 1, [du_n[0:1]])
        du2 = _shift_up(du, 2, [du_n[0:1], du_n[1:2]])
        dz = cw[2:3] * du + cw[1:2] * du1 + cw[0:1] * du2
        pv_ref[3:4, :] += _rowsum(du * z2)
        pv_ref[4:5, :] += _rowsum(du * z1)
        pv_ref[5:6, :] += _rowsum(du * z)

        de_ref[0] = (dcbu * u).astype(BF16)
        de_ref[1] = (dz * cx).astype(BF16)
        de_ref[2] = (dz * cc).astype(BF16)
        de_ref[3] = dga.astype(BF16)
        de_ref[4] = dgb.astype(BF16)

    row = lambda w: pl.BlockSpec((tm, w), lambda i: (i, 0))
    nxt = pl.BlockSpec((16, D), lambda i: (jnp.minimum((i + 1) * (tm // 16), S // 16 - 1), 0))
    return pl.pallas_call(
        body, name="bwd_mix", grid=(n_tiles,),
        in_specs=[row(D), nxt, _vec_spec(), row(D)] + [_e_spec(c, tm) for c in range(5)]
                 + [_e_next_spec(0, tm, S), _e_next_spec(4, tm, S), _e_prev_spec(1, tm), _e_prev_spec(2, tm),
                    _const_spec((8, D)), _vec_spec(), _vec_spec(), row(D), row(D), row(AOW),
                    _const_spec((D, D)), _const_spec((D, D)), _const_spec((D, AOW))],
        out_specs=[row(D), row(D), row(D), row(AOW), row(AOW), pl.BlockSpec((5, tm, D), lambda i: (0, i, 0)),
                   _const_spec((8, D))],
        out_shape=[jax.ShapeDtypeStruct((S, D), BF16)] * 3 + [jax.ShapeDtypeStruct((S, AOW), F32)] * 2
                  + [jax.ShapeDtypeStruct((5, S, D), BF16), jax.ShapeDtypeStruct((8, D), F32)],
        compiler_params=_cparams("arbitrary"),
    )(dx1, dx1, gate1, mo, e, e, e, e, e, e, e, e, e, cw8, ba, bb, ya, yc, o_attn, w_out, w_bc, w_bat)


def _attn_bwd(qkv, do, lse, dl, bias_t):
    S = qkv.shape[2]
    nblk = S // HEAD

    def body(qkv_ref, do_ref, lse_ref, dl_ref, b_ref, d_ref):
        g = pl.program_id(1)
        bias = b_ref[0, 0]
        col = lax.broadcasted_iota(jnp.int32, bias.shape, 1)
        bias_last = jnp.where(col >= HEAD, NEG, bias)
        eye = (lax.broadcasted_iota(jnp.int32, (HEAD, HEAD), 0) == lax.broadcasted_iota(jnp.int32, (HEAD, HEAD), 1)).astype(F32)

        def as_row(t):
            return jnp.sum(t * eye, axis=0, keepdims=True)

        for gi, d in enumerate(DILATIONS):
            @pl.when(g == gi)
            def _(d=d):
                nb = nblk // d

                def query_side(start):
                    sl = pl.ds(start, HEAD, stride=d)
                    return (qkv_ref.at[0, 0][sl, :].astype(BF16), do_ref[sl, :].astype(BF16),
                            as_row(lse_ref[sl, :]), as_row(dl_ref[sl, :]))

                def step(b, first_of_residue, carry):
                    dq_part, own = carry
                    r, n = b // nb, b % nb
                    cur = pl.ds(n * (HEAD * d) + r, HEAD, stride=d)
                    if first_of_residue:
                        own = query_side(r)
                    nxt = query_side(jnp.minimum(n + 1, nb - 1) * (HEAD * d) + r)
                    q2 = jnp.concatenate([own[0], nxt[0]], axis=0)
                    do2 = jnp.concatenate([own[1], nxt[1]], axis=0)
                    k = qkv_ref.at[0, 1][cur, :].astype(BF16)
                    v = qkv_ref.at[0, 2][cur, :].astype(BF16)
                    s = _nt(k, q2) * SCALE + jnp.where(n < nb - 1, bias, bias_last)
                    p = jnp.exp(s - jnp.concatenate([own[2], nxt[2]], axis=1))
                    d_ref.at[0, 2][cur, :] = _nn(p.astype(BF16), do2)
                    dp = _nt(v, do2)
                    ds = (p * (dp - jnp.concatenate([own[3], nxt[3]], axis=1)) * SCALE).astype(BF16)
                    d_ref.at[0, 1][cur, :] = _nn(ds, q2)
                    dq2 = _tn(ds, k)
                    d_ref.at[0, 0][cur, :] = dq2[:HEAD] + jnp.where(n > 0, dq_part, 0.0)
                    return dq2[HEAD:], nxt

                def steps(i, carry):
                    for u in range(UNROLL):
                        carry = step(i * UNROLL + u, nb <= UNROLL and u % nb == 0, carry)
                    return carry

                lax.fori_loop(0, nblk // UNROLL, steps, (jnp.zeros((HEAD, HEAD), F32), query_side(0)))

    col_blk = pl.BlockSpec((S, HEAD), lambda j, g: (0, j))
    qkv_blk = pl.BlockSpec((1, 3, S, HEAD), lambda j, g: (g, 0, 0, j))
    return pl.pallas_call(
        body, name="attn_bwd", grid=(N_SLOT, 3),
        in_specs=[qkv_blk, col_blk, col_blk, col_blk, pl.BlockSpec((1, 1, HEAD, 2 * HEAD), lambda j, g: (g, j, 0, 0))],
        out_specs=qkv_blk,
        out_shape=jax.ShapeDtypeStruct((3, 3, S, AOW), F32),
        compiler_params=_cparams("parallel", "arbitrary"),
    )(qkv, do, lse, dl, bias_t)


def _bwd_in(dqkv, de, w_int, x, dx1, g_mix, sc1):
    S = x.shape[0]
    tm = TM
    dqkv = dqkv.reshape(3, 3, S, AOW)

    def body(dq_ref, de_ref, wq_ref, wk_ref, wv_ref, wa_ref, wb_ref, x_hbm, dx1_hbm, g_ref, sc_ref, gx_ref, pv_ref,
             x_ref, dx1_ref, sems):
        acc = gx_ref
        i, k = pl.program_id(0), pl.program_id(1)
        tile = pl.ds(pl.multiple_of(i * tm, tm), tm)
        late = [pltpu.make_async_copy(x_hbm.at[tile, :], x_ref, sems.at[0]),
                pltpu.make_async_copy(dx1_hbm.at[tile, :], dx1_ref, sems.at[1])]

        @pl.when((i == 0) & (k == 0))
        def _():
            pv_ref[...] = jnp.zeros_like(pv_ref)

        @pl.when(k == 0)
        def _():
            acc[...] = jnp.zeros_like(acc)
            for cp in late:
                cp.start()

        @pl.when(k < 3)
        def _():
            lhs = jnp.concatenate([dq_ref[0, t].astype(BF16) for t in range(3)], axis=1)
            acc[...] += _nn(lhs, jnp.concatenate([wq_ref[...], wk_ref[...], wv_ref[...]], axis=0))

        @pl.when(k >= 3)
        def _():
            acc[...] += _nn(de_ref[0], jnp.concatenate([wa_ref[...], wb_ref[...]], axis=0))

        @pl.when(k == 7)
        def _():
            for cp in late:
                cp.wait()
            dh = acc[...]
            xv = x_ref[...]
            r = _rms_r(xv)
            g = g_ref[...]
            dxn, pg = _rms_bwd(xv, r, g, dh * (1.0 + sc_ref[...]))
            gx_ref[...] = dx1_ref[...] + dxn
            pv_ref[0:1, :] += _rowsum(dh)
            pv_ref[1:2, :] += _rowsum(dh * (xv * r * g))
            pv_ref[2:3, :] += _rowsum(pg)

    grp = lambda k: jnp.minimum(k, 2)
    chunk = lambda k: jnp.maximum(k - 3, 0)
    wblk = lambda f: pl.BlockSpec((512, D), lambda i, k: (f(k), 0))
    row = pl.BlockSpec((tm, D), lambda i, k: (i, 0))
    anyspec = pl.BlockSpec(memory_space=pl.ANY)
    return pl.pallas_call(
        body, name="bwd_in", grid=(S // tm, 8),
        in_specs=[pl.BlockSpec((1, 3, tm, 512), lambda i, k: (grp(k), 0, i, 0)),
                  pl.BlockSpec((1, tm, D), lambda i, k: (chunk(k), i, 0)),
                  wblk(grp), wblk(lambda k: 3 + grp(k)), wblk(lambda k: 6 + grp(k)),
                  wblk(lambda k: 9 + 2 * chunk(k)), wblk(lambda k: 10 + 2 * chunk(k)),
                  anyspec, anyspec, _vec_spec(), _vec_spec()],
        out_specs=[row, _const_spec((8, D))],
        out_shape=[jax.ShapeDtypeStruct((S, D), F32), jax.ShapeDtypeStruct((8, D), F32)],
        scratch_shapes=[pltpu.VMEM((tm, D), F32), pltpu.VMEM((tm, D), F32), pltpu.SemaphoreType.DMA((2,))],
        compiler_params=_cparams("arbitrary", "arbitrary"),
    )(dqkv, de, w_int, w_int, w_int, w_int, w_int, x, dx1, g_mix, sc1)


def _grad_w(name, a, b):
    S, ka = a.shape
    nb = b.shape[1]

    def body(a_ref, b_ref, o_ref):
        o_ref[...] = _tn(a_ref[...], b_ref[...]).astype(BF16)

    return pl.pallas_call(
        body, name=name, grid=(ka // 512,),
        in_specs=[pl.BlockSpec((S, 512), lambda n: (0, n)), pl.BlockSpec((S, nb), lambda n: (0, 0))],
        out_specs=pl.BlockSpec((512, nb), lambda n: (n, 0)),
        out_shape=jax.ShapeDtypeStruct((ka, nb), BF16),
        compiler_params=_cparams("parallel"),
    )(a, b)


def _grad_w_small(dya, o_bf, cbu, dyc, merged, dmo, after):
    S = dya.shape[0]

    def body(dya_h, o_h, cbu_h, dyc_h, mg_h, dmo_h, after_ref, gba_ref, gbc_ref, gout_ref, a0, a1, b_small, b1, b2, sems):
        fetch = [pltpu.make_async_copy(src, buf, sems.at[k])
                 for k, (src, buf) in enumerate(((dya_h, a0), (o_h, b_small), (cbu_h, a1), (dyc_h, b1), (dmo_h, b2)))]
        for cp in fetch:
            cp.start()
        fetch[0].wait()
        fetch[1].wait()
        gba_ref[...] = _tn(a0[...], b_small[...]).astype(BF16)
        last = pltpu.make_async_copy(mg_h, a0, sems.at[5])
        last.start()
        fetch[2].wait()
        fetch[3].wait()
        gbc_ref[...] = _tn(a1[...], b1[...]).astype(BF16)
        fetch[4].wait()
        last.wait()
        gout_ref[...] = _tn(a0[...], b2[...]).astype(BF16)

    anyspec = pl.BlockSpec(memory_space=pl.ANY)
    vmem = pl.BlockSpec(memory_space=pltpu.VMEM)
    wide = pltpu.VMEM((S, D), BF16)
    return pl.pallas_call(
        body, name="grad_w_small",
        in_specs=[anyspec] * 7, out_specs=[vmem] * 3,
        out_shape=[jax.ShapeDtypeStruct((D, AOW), BF16), jax.ShapeDtypeStruct((D, D), BF16), jax.ShapeDtypeStruct((D, D), BF16)],
        scratch_shapes=[wide, wide, pltpu.VMEM((S, AOW), BF16), wide, wide, pltpu.SemaphoreType.DMA((6,))],
        compiler_params=_cparams(),
    )(dya, o_bf, cbu, dyc, merged, dmo, after)


def _grad_w_in(dqkv, de, h):
    S = h.shape[0]

    def body(dq_ref, de_ref, h_ref, o_ref):
        n = pl.program_id(0)

        @pl.when(n < 9)
        def _():
            o_ref[...] = _tn(dq_ref[0].astype(BF16), h_ref[...]).astype(BF16)

        @pl.when(n >= 9)
        def _():
            o_ref[...] = _tn(de_ref[0], h_ref[...]).astype(BF16)

    def e_idx(n):
        kk = jnp.maximum(n - 9, 0)
        return (kk // 2, 0, kk % 2)

    return pl.pallas_call(
        body, name="grad_w_in", grid=(19,),
        in_specs=[pl.BlockSpec((1, S, 512), lambda n: (jnp.minimum(n, 8), 0, 0)), pl.BlockSpec((1, S, 512), e_idx),
                  pl.BlockSpec((S, D), lambda n: (0, 0))],
        out_specs=pl.BlockSpec((512, D), lambda n: (_win_rowblock(n), 0)),
        out_shape=jax.ShapeDtypeStruct((19 * 512, D), BF16),
        compiler_params=_cparams("parallel"),
    )(dqkv, de, h)


def _local_step(x, h, tgt, mod, g_mix, g_mlp, g_fin, ba, bb, cw8, w_int, mix_weights, mlp_weights, mlp_grads_ready, w_in_grad_ready,
                other_grads_ready):
    S = x.shape[0]
    sh1, sc1, gt1, sh2, sc2, gt2 = [mod[k:k + 1] for k in range(6)]
    bias, bias_t = _bias_table()

    qkv, e = _proj(h, w_int)
    qkv = qkv.reshape(3, 3, S, AOW)
    o_attn, lse = _attn_fwd(qkv, bias)
    w_bat, w_bc, w_out = mix_weights(o_attn)
    o_bf, cbu, ya, yc, merged = _mix(o_attn, e, cw8, ba, bb, w_bat, w_bc)
    x1, mo, h2 = _out_proj(merged, w_out, x, gt1, g_mlp, sc2, sh2)
    w_mit, w_mo = mlp_weights(x1)
    a, f = _mlp_in(h2, w_mit)
    mlp, dx2, pv_f = _mlp_out(f, w_mo, x1, gt2, g_fin, tgt)

    da, dmo2, pv_a = _bwd_mlp_a(dx2, gt2, mlp, w_mo, a)
    dx1, pv_b = _bwd_mlp_b(da, w_mit, x1, dx2, g_mlp, sc2)
    zero = mlp_grads_ready(_grad_w("grad_w_mi", da, h2), _grad_w("grad_w_mo", f, dmo2))
    dmo, dya, dyc, do, dl, de, pv_m = _bwd_mix(dx1, gt1 + zero, mo, e, cw8, ba, bb, ya, yc, o_attn, w_out, w_bc, w_bat)
    dqkv = _attn_bwd(qkv, do, lse, dl, bias_t).reshape(9, S, AOW)
    after = w_in_grad_ready(_grad_w_in(dqkv, de, h))
    zero = other_grads_ready(*_grad_w_small(dya, o_bf, cbu, dyc, merged, dmo, after))
    grad_x, pv_i = _bwd_in(dqkv, de, w_int, x, dx1, g_mix, sc1 + zero)

    vec = jnp.concatenate([pv_i[0:2], pv_m[0:1], pv_b[0:2], pv_a[0:1], pv_i[2:3], pv_b[2:3], pv_f[0:1],
                           pv_m[1:3], pv_m[3:6], pv_f[1:2], jnp.zeros((1, D), F32)], axis=0)
    return grad_x, vec


def _my_place():
    return lax.axis_index("x"), lax.axis_index("y"), lax.axis_index("c")


def _dev_index(px, py, pc):
    return 4 * px + 2 * py + pc


def _peer(x, y, c, m):
    return (x ^ ((m >> 2) & 1), y ^ ((m >> 1) & 1), c ^ (m & 1))


HBM_SPEC = pl.BlockSpec(memory_space=pltpu.HBM)
SEM_SPEC = pl.BlockSpec(memory_space=pltpu.SEMAPHORE)
N_PEER = N_DEV - 1


SPLIT_MASKS = {"gather": tuple(range(1, N_DEV)), "scatter": tuple(range(1, N_DEV)), "chips": (2, 4, 6), "sibling": (1, 1, 1, 1)}


def _split_copy(mode, src_ref, land_ref, send_sems, recv_sems, w, j, place, arriving=False):
    x, y, c = place
    masks = SPLIT_MASKS[mode]
    peer = _peer(x, y, c, masks[j])
    k = w * len(masks) + j
    sender, receiver = ((peer, (x, y, c)) if arriving else ((x, y, c), peer))
    if mode == "gather":
        r = src_ref.shape[0]
        src, dst = src_ref, land_ref.at[pl.ds(pl.multiple_of(_dev_index(*sender) * r, 16), r), :]
    elif mode == "scatter":
        r = land_ref.shape[1]
        src, dst = src_ref.at[pl.ds(pl.multiple_of(_dev_index(*receiver) * r, 16), r), :], land_ref.at[j]
    elif mode == "chips":
        src, dst = src_ref.at[2 * receiver[0] + receiver[1]], land_ref.at[j]
    else:
        r = land_ref.shape[1]
        src, dst = src_ref.at[pl.ds(pl.multiple_of((2 * j + receiver[2]) * r, 16), r), :], land_ref.at[j]
    return pltpu.make_async_remote_copy(src_ref=src, dst_ref=dst, send_sem=send_sems.at[k], recv_sem=recv_sems.at[k],
                                        device_id=peer, device_id_type=MESH)


def _split_start(name, mode, srcs, lands):
    n = len(srcs)
    nm = len(SPLIT_MASKS[mode])

    def body(*refs):
        src, land = refs[:n], refs[n:2 * n]
        send_sems, recv_sems = refs[2 * n], refs[2 * n + 1]
        token = refs[-1]
        place = _my_place()
        for w in range(n):
            for j in range(nm):
                _split_copy(mode, src[w], land[w], send_sems, recv_sems, w, j, place).start()
        token[...] = jnp.zeros_like(token)

    hbm = lambda t: pltpu.HBM(t.shape, t.dtype)
    out = pl.pallas_call(
        body, name=name,
        out_shape=(pltpu.SemaphoreType.DMA((n * nm,)), pltpu.SemaphoreType.DMA((n * nm,)), *[hbm(t) for t in srcs],
                   *[hbm(t) for t in lands], jax.ShapeDtypeStruct((8, 128), F32)),
        in_specs=(HBM_SPEC,) * (2 * n),
        out_specs=(SEM_SPEC, SEM_SPEC) + (HBM_SPEC,) * (2 * n) + (pl.BlockSpec(memory_space=pltpu.VMEM),),
        input_output_aliases={i: 2 + i for i in range(2 * n)},
        compiler_params=pltpu.CompilerParams(has_side_effects=pltpu.SideEffectType.DATAFLOW_SIDE_EFFECTING),
    )(*[pltpu.with_memory_space_constraint(t, pltpu.HBM) for t in (*srcs, *lands)])
    return out[0], out[1], out[2:2 + n], out[2 + n:2 + 2 * n], out[-1][0:1, 0:1], out[-1]


def _split_wait(name, mode, send_sems, recv_sems, srcs, lands, after):
    n = len(srcs)

    def body(*refs):
        src, land = refs[:n], refs[n:2 * n]
        ssem, rsem = refs[2 * n], refs[2 * n + 1]
        place = _my_place()
        for w in range(n):
            for j in range(len(SPLIT_MASKS[mode])):
                _split_copy(mode, src[w], land[w], ssem, rsem, w, j, place).wait_send()
                _split_copy(mode, src[w], land[w], ssem, rsem, w, j, place, arriving=True).wait_recv()

    hbm = lambda t: pltpu.HBM(t.shape, t.dtype)
    out = pl.pallas_call(
        body, name=name,
        out_shape=tuple(hbm(t) for t in (*srcs, *lands)),
        in_specs=(HBM_SPEC,) * (2 * n) + (SEM_SPEC, SEM_SPEC, pl.BlockSpec(memory_space=pl.ANY)),
        out_specs=(HBM_SPEC,) * (2 * n),
        input_output_aliases={i: i for i in range(2 * n)},
        compiler_params=pltpu.CompilerParams(has_side_effects=pltpu.SideEffectType.DATAFLOW_SIDE_EFFECTING),
    )(*srcs, *lands, send_sems, recv_sems, after)
    return out[:n], out[n:]


def _sibling_exchange(grads):
    nw = len(grads)
    HBM = pl.BlockSpec(memory_space=pl.ANY)

    def body(*refs):
        g, land = refs[:nw], refs[nw:2 * nw]
        send_sems, recv_sems = refs[2 * nw:]
        x, y, c = _my_place()

        def copy(w, q, owner_core):
            r = land[w].shape[1]
            return pltpu.make_async_remote_copy(
                src_ref=g[w].at[pl.ds(pl.multiple_of((2 * q + owner_core) * r, 16), r), :], dst_ref=land[w].at[q],
                send_sem=send_sems.at[w, q], recv_sem=recv_sems.at[w, q], device_id=(x, y, 1 - c), device_id_type=MESH)

        sends = [copy(w, q, 1 - c) for w in range(nw) for q in range(4)]
        for cp in sends:
            cp.start()
        for w in range(nw):
            for q in range(4):
                copy(w, q, c).wait_recv()
        for cp in sends:
            cp.wait_send()

    return pl.pallas_call(
        body, name="sibling_exchange",
        out_shape=[jax.ShapeDtypeStruct((4, a.shape[0] // N_DEV, a.shape[1]), a.dtype) for a in grads],
        in_specs=[HBM] * nw, out_specs=[HBM] * nw,
        scratch_shapes=[pltpu.SemaphoreType.DMA((nw, 4)), pltpu.SemaphoreType.DMA((nw, 4))],
    )(*grads)


def _pair_sums(gs, sibs, core):
    n = len(gs)

    def body(core_ref, *refs):
        for w in range(n):
            refs[2 * n + w][0] = (refs[w][0, 0].astype(F32) + refs[n + w][0].astype(F32)).astype(BF16)

    in_specs = [pl.BlockSpec((1, 1) + t.shape[1:], lambda q, core_ref: (q, core_ref[0], 0, 0)) for t in sibs]
    in_specs += [pl.BlockSpec((1,) + t.shape[1:], lambda q, core_ref: (q, 0, 0)) for t in sibs]
    return pl.pallas_call(
        body, name="pair_sums",
        grid_spec=pltpu.PrefetchScalarGridSpec(
            num_scalar_prefetch=1, grid=(4,), in_specs=in_specs,
            out_specs=[pl.BlockSpec((1,) + t.shape[1:], lambda q, core_ref: (q, 0, 0)) for t in sibs]),
        out_shape=[jax.ShapeDtypeStruct(t.shape, BF16) for t in sibs],
        compiler_params=_cparams("parallel"),
    )(core, *[g.reshape(4, 2, t.shape[1], t.shape[2]) for g, t in zip(gs, sibs)], *sibs)


def _own_rows_into_zones(shards, me):
    n = len(shards)

    def body(me_ref, *refs):
        for w in range(n):
            refs[2 * n + w][...] = refs[w][...]

    zones = [lax.empty((N_DEV * t.shape[0], t.shape[1]), t.dtype) for t in shards]
    return pl.pallas_call(
        body, name="own_rows_into_zones",
        grid_spec=pltpu.PrefetchScalarGridSpec(
            num_scalar_prefetch=1, grid=(1,),
            in_specs=[pl.BlockSpec(t.shape, lambda i, me_ref: (0, 0)) for t in shards] + [pl.BlockSpec(memory_space=pl.ANY)] * n,
            out_specs=[pl.BlockSpec(t.shape, lambda i, me_ref: (me_ref[0], 0)) for t in shards]),
        out_shape=[jax.ShapeDtypeStruct(z.shape, z.dtype) for z in zones],
        input_output_aliases={1 + n + w: w for w in range(n)},
        compiler_params=_cparams("arbitrary"),
    )(me, *shards, *zones)


def _allgather_small(v, name):
    r, ccols = v.shape

    def body(v_ref, out_ref, sum_ref, send_sems, recv_sems):
        x, y, c = _my_place()
        my_idx = _dev_index(x, y, c)
        out_ref[my_idx] = v_ref[...]

        def copy(m):
            peer = _peer(x, y, c, m)
            return pltpu.make_async_remote_copy(
                src_ref=v_ref, dst_ref=out_ref.at[my_idx],
                send_sem=send_sems.at[m - 1], recv_sem=recv_sems.at[m - 1], device_id=peer, device_id_type=MESH)

        def arrival(m):
            peer = _peer(x, y, c, m)
            return pltpu.make_async_remote_copy(
                src_ref=v_ref, dst_ref=out_ref.at[_dev_index(*peer)],
                send_sem=send_sems.at[m - 1], recv_sem=recv_sems.at[m - 1], device_id=peer, device_id_type=MESH)

        sends = [copy(m) for m in range(1, N_DEV)]
        for cp in sends:
            cp.start()
        for m in range(1, N_DEV):
            arrival(m).wait_recv()
        acc = out_ref[0]
        for s in range(1, N_DEV):
            acc = acc + out_ref[s]
        sum_ref[...] = acc
        for cp in sends:
            cp.wait_send()

    vmem = pl.BlockSpec(memory_space=pltpu.VMEM)
    return pl.pallas_call(
        body, name=name,
        out_shape=[jax.ShapeDtypeStruct((N_DEV, r, ccols), v.dtype), jax.ShapeDtypeStruct((r, ccols), v.dtype)],
        in_specs=[vmem], out_specs=[vmem, vmem],
        scratch_shapes=[pltpu.SemaphoreType.DMA((7,)), pltpu.SemaphoreType.DMA((7,))],
    )(v)


def _gather_w_in_and_condition(shard, pay, w_ada, b_cols):
    r, ccols = shard.shape
    ncol = w_ada.shape[1]

    def body(sh_ref, pay_ref, w_ref, b_ref, full_ref, got_ref, act_ref, mod_ref, send_sems, recv_sems, small_send, small_recv, local_sem):
        x, y, c = _my_place()
        me, sibling = (x, y, c), (x, y, 1 - c)
        my_idx = _dev_index(x, y, c)
        chips = [(1 - x, y), (x, 1 - y), (1 - x, 1 - y)]

        def small(rnd, buf, m, arriving=False):
            peer = _peer(x, y, c, m)
            slot = _dev_index(*peer) if arriving else my_idx
            return pltpu.make_async_remote_copy(
                src_ref=buf.at[my_idx], dst_ref=buf.at[slot], send_sem=small_send.at[rnd, m - 1],
                recv_sem=small_recv.at[rnd, m - 1], device_id=peer, device_id_type=MESH)

        def rows(px, py, pc):
            return full_ref.at[pl.ds(pl.multiple_of(_dev_index(px, py, pc) * r, 16), r), :]

        def copy(k, block, to, src=None):
            return pltpu.make_async_remote_copy(
                src_ref=rows(*block) if src is None else src, dst_ref=rows(*block),
                send_sem=send_sems.at[k], recv_sem=recv_sems.at[k], device_id=to, device_id_type=MESH)

        got_ref[my_idx] = pay_ref[...]
        round1 = [small(0, got_ref, m) for m in range(1, N_DEV)]
        for cp in round1:
            cp.start()
        mine = pltpu.make_async_copy(sh_ref, rows(*me), local_sem)
        mine.start()
        first = [copy(0, me, sibling, src=sh_ref)] + [copy(1 + j, me, (*chip, c), src=sh_ref) for j, chip in enumerate(chips)]
        for cp in first:
            cp.start()

        for m in range(1, N_DEV):
            small(0, got_ref, m, arriving=True).wait_recv()
        cv = jnp.concatenate([got_ref[s, 0:1, :] for s in range(N_DEV)], axis=0)
        act = cv * _sigmoid(cv)
        act_ref[...] = act
        mod_ref[my_idx] = jnp.dot(act, w_ref[...], preferred_element_type=F32, precision=lax.Precision.HIGHEST) + b_ref[...]
        round2 = [small(1, mod_ref, m) for m in range(1, N_DEV)]
        for cp in round2:
            cp.start()

        passed = []
        for j, chip in enumerate(chips):
            copy(1 + j, (*chip, c), me).wait_recv()
            fwd = copy(4 + j, (*chip, c), sibling)
            fwd.start()
            passed.append(fwd)
        copy(0, sibling, me).wait_recv()
        for j, chip in enumerate(chips):
            copy(4 + j, (*chip, 1 - c), me).wait_recv()
        for m in range(1, N_DEV):
            small(1, mod_ref, m, arriving=True).wait_recv()
        for cp in first + passed + round1 + round2:
            cp.wait_send()
        mine.wait()

    anyspec = pl.BlockSpec(memory_space=pl.ANY)
    vmem = pl.BlockSpec(memory_space=pltpu.VMEM)
    return pl.pallas_call(
        body, name="gather_w_in_and_condition",
        out_shape=[jax.ShapeDtypeStruct((N_DEV * r, ccols), shard.dtype), jax.ShapeDtypeStruct((N_DEV, 8, D), F32),
                   jax.ShapeDtypeStruct((N_DEV, D), F32), jax.ShapeDtypeStruct((N_DEV, N_DEV, ncol), F32)],
        in_specs=[anyspec, vmem, vmem, vmem], out_specs=[anyspec, vmem, vmem, vmem],
        scratch_shapes=[pltpu.SemaphoreType.DMA((7,)), pltpu.SemaphoreType.DMA((7,)), pltpu.SemaphoreType.DMA((2, 7)),
                        pltpu.SemaphoreType.DMA((2, 7)), pltpu.SemaphoreType.DMA],
        compiler_params=_cparams(),
    )(shard, pay, w_ada, b_cols)


def _row_tile(r):
    for t in (256, 304, 128, 64, 16):
        if r % t == 0:
            return t
    return r


def _adamw_w_ada(w, act_t, gm_cols, m, v):
    r, ccols = w.shape
    tr = _row_tile(r)
    c1 = 1.0 / (1.0 - B1 ** STEP)
    c2 = 1.0 / (1.0 - B2 ** STEP)

    def body(w_ref, a_ref, gm_ref, m_ref, v_ref, g_ref, d_ref, nm_ref, nv_ref):
        gv = jnp.dot(a_ref[...], gm_ref[...], preferred_element_type=F32, precision=lax.Precision.HIGHEST)
        g_ref[...] = gv
        nm = B1 * m_ref[...] + (1.0 - B1) * gv
        nv = B2 * v_ref[...] + (1.0 - B2) * jnp.square(gv)
        nm_ref[...] = nm
        nv_ref[...] = nv
        d_ref[...] = -LR * ((nm * c1) / (jnp.sqrt(nv * c2) + ADAM_EPS) + WD * w_ref[...])

    blk = pl.BlockSpec((tr, ccols), lambda i: (i, 0))
    return pl.pallas_call(
        body, name="adamw_w_ada", grid=(r // tr,),
        in_specs=[blk, pl.BlockSpec((tr, N_DEV), lambda i: (i, 0)), _const_spec(gm_cols.shape), blk, blk], out_specs=[blk] * 4,
        out_shape=[jax.ShapeDtypeStruct((r, ccols), F32)] * 4,
        compiler_params=_cparams("parallel"),
    )(w, act_t, gm_cols, m, v)


def _sum_adamw(parts, own, slot, w, m, v, name, transposed=False):
    k, r, ccols = parts.shape
    tr = _row_tile(r)
    c1 = 1.0 / (1.0 - B1 ** STEP)
    c2 = 1.0 / (1.0 - B2 ** STEP)

    def body(s_ref, p_ref, own_ref, w_ref, m_ref, v_ref, g_ref, d_ref, nm_ref, nv_ref):
        gv = own_ref[0].astype(F32)
        for s in range(k):
            gv = gv + p_ref[s].astype(F32)
        if transposed:
            gv = gv.T
        g_ref[...] = gv
        nm = B1 * m_ref[...] + (1.0 - B1) * gv
        nv = B2 * v_ref[...] + (1.0 - B2) * jnp.square(gv)
        nm_ref[...] = nm
        nv_ref[...] = nv
        d_ref[...] = -LR * ((nm * c1) / (jnp.sqrt(nv * c2) + ADAM_EPS) + WD * w_ref[...])

    if transposed:
        blk = pl.BlockSpec((ccols, tr), lambda i, s_ref: (0, i))
    else:
        blk = pl.BlockSpec((tr, ccols), lambda i, s_ref: (i, 0))
    return pl.pallas_call(
        body, name=name,
        grid_spec=pltpu.PrefetchScalarGridSpec(
            num_scalar_prefetch=1, grid=(r // tr,),
            in_specs=[pl.BlockSpec((k, tr, ccols), lambda i, s_ref: (0, i, 0)),
                      pl.BlockSpec((1, tr, ccols), lambda i, s_ref: (s_ref[0], i, 0))] + [blk] * 3,
            out_specs=[blk] * 4),
        out_shape=[jax.ShapeDtypeStruct(w.shape, F32)] * 4,
        compiler_params=_cparams("parallel"),
    )(slot, parts, own, w, m, v)


VEC_ROWS = ((0, 6), (6, 7), (9, 11), (11, 14), (7, 8), (8, 9))


def _adamw_vectors(w, g, m, v):
    c1 = 1.0 / (1.0 - B1 ** STEP)
    c2 = 1.0 / (1.0 - B2 ** STEP)

    def put(refs, p):
        for ref, (lo, hi) in zip(refs, VEC_ROWS):
            if ref.shape == (3, HEAD):
                ref[...] = p[lo:hi, :HEAD]
            else:
                ref[...] = jnp.concatenate([p[k:k + 1] for k in range(lo, hi)], axis=1)

    def body(w_ref, g_ref, m_ref, v_ref, *outs):
        gv = g_ref[...]
        nm = B1 * m_ref[...] + (1.0 - B1) * gv
        nv = B2 * v_ref[...] + (1.0 - B2) * jnp.square(gv)
        delta = -LR * ((nm * c1) / (jnp.sqrt(nv * c2) + ADAM_EPS) + WD * w_ref[...])
        for kind, p in enumerate((gv, delta, nm, nv)):
            put(outs[6 * kind:6 * kind + 6], p)

    shapes = [(1, 6 * D), (1, D), (1, 2 * D), (3, HEAD), (1, D), (1, D)]
    out = pl.pallas_call(
        body, name="adamw_vectors", out_shape=[jax.ShapeDtypeStruct(sh, F32) for sh in shapes] * 4, compiler_params=_cparams(),
    )(w, g, m, v)
    fix = lambda t: (t[0], t[1], t[2], t[3][None], t[4], t[5].reshape(D))
    return [fix(out[6 * kind:6 * kind + 6]) for kind in range(4)]


def _pack_vectors(b_ada, g_mix, g_mlp, g_fin, b_gate, conv_w):
    conv_rows = jnp.pad(conv_w.reshape(3, HEAD), ((0, 0), (0, D - HEAD)))
    return jnp.concatenate([b_ada.reshape(6, D), g_mix.reshape(1, D), g_mlp.reshape(1, D), g_fin.reshape(1, D),
                            b_gate.reshape(2, D), conv_rows, jnp.zeros((2, D), F32)], axis=0)


def kernel(x, c, w_ada, b_ada, g_norm_mix, w_in, b_gate, conv_w, w_branch_attn, w_branch_conv, w_out, g_norm_mlp, w_mlp_in, w_mlp_out, g_norm_final, loss_target, m_w_ada, m_b_ada, m_g_norm_mix, m_w_in, m_b_gate, m_conv_w, m_w_branch_attn, m_w_branch_conv, m_w_out, m_g_norm_mlp, m_w_mlp_in, m_w_mlp_out, m_g_norm_final, v_w_ada, v_b_ada, v_g_norm_mix, v_w_in, v_b_gate, v_conv_w, v_w_branch_attn, v_w_branch_conv, v_w_out, v_g_norm_mlp, v_w_mlp_in, v_w_mlp_out, v_g_norm_final):
    S = x.shape[1]
    xi, yi, ci = _my_place()
    me = _dev_index(xi, yi, ci)
    x2 = x.reshape(S, D)
    tgt = loss_target.reshape(S, D)

    pay = jnp.zeros((8, D), F32).at[0].set(c[0]).at[1:4, :HEAD].set(conv_w[0])
    ncol = w_ada.shape[2]
    b_cols = lax.dynamic_slice(b_ada, (0, me * ncol), (1, ncol))
    w_int, got, act, mod_all = _gather_w_in_and_condition(w_in[0].T.astype(BF16), pay, w_ada[0], b_cols)
    cw8 = jnp.pad(got[:, 1:4, :HEAD].transpose(1, 0, 2).reshape(3, D), ((0, 5), (0, 0)))
    mod = lax.dynamic_index_in_dim(mod_all, me, axis=1, keepdims=False).reshape(6, D)
    late = [w_branch_attn[0].T.astype(BF16), w_branch_conv[0].astype(BF16), w_out[0].astype(BF16),
            w_mlp_in[0].T.astype(BF16), w_mlp_out[0].astype(BF16)]
    w_int, late = lax.optimization_barrier((w_int, late))
    zones = _own_rows_into_zones(late, me.reshape(1).astype(jnp.int32))
    ag_mix = _split_start("gather_mix_start", "gather", late[:3], zones[:3])
    ag_mlp = _split_start("gather_mlp_start", "gather", late[3:], zones[3:])
    mod = mod + ag_mix[4] + ag_mlp[4]
    h = _prenorm(x2, g_norm_mix, mod[1:2], mod[0:1])

    def mix_weights(o_attn):
        return _split_wait("gather_mix_wait", "gather", *ag_mix[:4], o_attn)[1]

    def mlp_weights(x1):
        return _split_wait("gather_mlp_wait", "gather", *ag_mlp[:4], x1)[1]

    rs = {}

    def mlp_grads_ready(*grads):
        lands = [lax.empty((N_PEER, t.shape[0] // N_DEV, t.shape[1]), BF16) for t in grads]
        rs["mlp"] = _split_start("scatter_mlp_start", "scatter", grads, lands)
        return rs["mlp"][4]

    def w_in_grad_ready(g_in):
        r = g_in.shape[0] // N_DEV
        rs["sib"] = _split_start("sibling_w_in_start", "sibling", [g_in], [lax.empty((4, r, g_in.shape[1]), BF16)])
        return rs["sib"][5]

    def other_grads_ready(*small):
        core = ci.reshape(1).astype(jnp.int32)
        (g_in,), (sib_in,) = _split_wait("sibling_w_in_wait", "sibling", *rs["sib"][:4], small[0])
        pair = _pair_sums([g_in, *small], [sib_in, *_sibling_exchange(small)], core)
        lands = [lax.empty((3,) + t.shape[1:], BF16) for t in pair]
        rs["rest"] = _split_start("scatter_rest_start", "chips", pair, lands)
        return rs["rest"][4]

    ba, bb = b_gate[:, :D], b_gate[:, D:]
    grad_x, vec = _local_step(
        x2, h, tgt, mod, g_norm_mix, g_norm_mlp, g_norm_final.reshape(1, D), ba, bb, cw8, w_int, mix_weights, mlp_weights,
        mlp_grads_ready, w_in_grad_ready, other_grads_ready)

    vec_all, vec_sum = _allgather_small(vec, "gather_vec")
    loss = vec_sum[14, 0]
    gm_all = vec_all[:, 0:6, :].reshape(N_DEV, 6 * D)
    gm_cols = lax.dynamic_slice(gm_all, (0, me * ncol), (N_DEV, ncol))
    conv_cols = lax.dynamic_slice(vec_sum[11:14], (0, me * HEAD), (3, HEAD))
    g_pack = jnp.concatenate([vec_sum[0:11], jnp.pad(conv_cols, ((0, 0), (0, D - HEAD))), jnp.zeros((2, D), F32)], axis=0)
    packs = [_pack_vectors(*t) for t in ((b_ada, g_norm_mix, g_norm_mlp, g_norm_final, b_gate, conv_w),
                                         (m_b_ada, m_g_norm_mix, m_g_norm_mlp, m_g_norm_final, m_b_gate, m_conv_w),
                                         (v_b_ada, v_g_norm_mix, v_g_norm_mlp, v_g_norm_final, v_b_gate, v_conv_w))]
    gv, dv, mv, vv = _adamw_vectors(packs[0], g_pack, packs[1], packs[2])
    g_w_ada, d_ada, nm_ada, nv_ada = _adamw_w_ada(w_ada[0], act.T, gm_cols, m_w_ada[0], v_w_ada[0])

    big = {}
    srcs, lands = _split_wait("scatter_mlp_wait", "scatter", *rs["mlp"][:4], d_ada)
    own = [g.reshape((N_DEV,) + land.shape[1:]) for g, land in zip(srcs, lands)]
    slot = me.reshape(1).astype(jnp.int32)
    big["w_mi"] = tuple(t[None] for t in _sum_adamw(lands[0], own[0], slot, w_mlp_in[0], m_w_mlp_in[0], v_w_mlp_in[0], "adamw_w_mi",
                                                    transposed=True))
    big["w_mo"] = tuple(t[None] for t in _sum_adamw(lands[1], own[1], slot, w_mlp_out[0], m_w_mlp_out[0], v_w_mlp_out[0], "adamw_w_mo"))
    own, lands = _split_wait("scatter_rest_wait", "chips", *rs["rest"][:4], big["w_mo"][1])
    slot = (2 * xi + yi).reshape(1).astype(jnp.int32)
    big["w_in"] = tuple(t.T[None] for t in _sum_adamw(lands[0], own[0], slot, w_in[0].T, m_w_in[0].T, v_w_in[0].T, "adamw_w_in"))
    big["w_ba"] = tuple(t[None] for t in _sum_adamw(lands[1], own[1], slot, w_branch_attn[0], m_w_branch_attn[0], v_w_branch_attn[0],
                                                    "adamw_w_ba", transposed=True))
    big["w_bc"] = tuple(t[None] for t in _sum_adamw(lands[2], own[2], slot, w_branch_conv[0], m_w_branch_conv[0], v_w_branch_conv[0], "adamw_w_bc"))
    big["w_out"] = tuple(t[None] for t in _sum_adamw(lands[3], own[3], slot, w_out[0], m_w_out[0], v_w_out[0], "adamw_w_out"))

    def ordered(k, ada, vecs):
        return (ada[None], vecs[0], vecs[1], big["w_in"][k], vecs[2], vecs[3], big["w_ba"][k], big["w_bc"][k],
                big["w_out"][k], vecs[4], big["w_mi"][k], big["w_mo"][k], vecs[5])

    return (loss, grad_x.reshape(1, S, D), *ordered(0, g_w_ada, gv), *ordered(1, d_ada, dv),
            *ordered(2, nm_ada, mv), *ordered(3, nv_ada, vv))
```
